```python
import jax, jax.numpy as jnp
from jax import lax
import numpy as np

D_MODEL = 2048
BATCH = 8
SEQ = 2048
DEPTH = 1

GMLP_GROUPS = 8
GMLP_GROUP_DIM = 128
GMLP_WIDTH = GMLP_GROUPS * GMLP_GROUP_DIM
GMLP_CHUNK = 128
HG_HEADS = 8
HG_DK = 128
HG_DV = 128
HG_WIDTH = HG_HEADS * HG_DK
HG_CHUNK = 64
FFN_HIDDEN = -(-8 * D_MODEL // (3 * 256)) * 256
IN_WIDTH = 2 * GMLP_WIDTH + 4 * HG_WIDTH + 2 * D_MODEL
EPS = 1e-6

kernel_name = "hybrid_gmlp_hgrn2_adaln_block"


def rmsnorm(x, g):
    xf = x.astype(jnp.float32)
    y = xf * lax.rsqrt(jnp.mean(xf * xf, axis=-1, keepdims=True) + EPS)
    return (y * g.astype(jnp.float32)).astype(x.dtype)


def layernorm(x, g, b):
    xf = x.astype(jnp.float32)
    mu = jnp.mean(xf, axis=-1, keepdims=True)
    var = jnp.mean(jnp.square(xf - mu), axis=-1, keepdims=True)
    y = (xf - mu) * lax.rsqrt(var + EPS)
    return (y * g.astype(jnp.float32) + b.astype(jnp.float32)).astype(x.dtype)


def modulate(h, shift, scale):
    return h * (1 + scale[:, None, :]) + shift[:, None, :]


def gmlp_branch(u, v, ln_g, ln_b, ws, bs):
    B, S, _ = v.shape
    nc = S // GMLP_CHUNK
    v = layernorm(v, ln_g, ln_b)
    vc = v.reshape(B, nc, GMLP_CHUNK, GMLP_GROUPS, GMLP_GROUP_DIM)
    mask = jnp.tril(jnp.ones((GMLP_CHUNK, GMLP_CHUNK), dtype=bool))
    w = jnp.where(mask[None], ws, 0).astype(v.dtype)
    s = jnp.einsum('gts,bcsgd->bctgd', w, vc)
    s = s + jnp.transpose(bs)[None, None, :, :, None].astype(v.dtype)
    return u * s.reshape(B, S, GMLP_WIDTH)


def hgrn2_branch(q, f_pre, i, g, lb, norm_g):
    B, S, _ = q.shape
    dt = q.dtype
    f = lb + (1.0 - lb) * jax.nn.sigmoid(f_pre.astype(jnp.float32))
    log_f = jnp.log(f)
    k = 1.0 - f
    qa = jax.nn.silu(q.astype(jnp.float32))
    nc = S // HG_CHUNK

    def heads(t, d):
        return t.reshape(B, nc, HG_CHUNK, HG_HEADS, d).transpose(1, 0, 3, 2, 4)

    qc, kc, lfc = heads(qa, HG_DK), heads(k, HG_DK), heads(log_f, HG_DK)
    ic = heads(i.astype(jnp.float32), HG_DV)
    mask = jnp.tril(jnp.ones((HG_CHUNK, HG_CHUNK), dtype=bool))

    def step(state, inp):
        qt, kt, it, lft = inp
        b = jnp.cumsum(lft, axis=2)
        o_inter = jnp.einsum('bhtk,bhkv->bhtv', qt * jnp.exp(b), state)
        rel = b[:, :, :, None, :] - b[:, :, None, :, :]
        decay = jnp.exp(jnp.where(mask[:, :, None], rel, -jnp.inf))
        attn = jnp.einsum('bhtk,bhsk,bhtsk->bhts', qt, kt, decay)
        o = o_inter + jnp.einsum('bhts,bhsv->bhtv', attn, it)
        b_last = b[:, :, -1:, :]
        new_state = jnp.exp(b_last[:, :, 0, :])[..., None] * state + jnp.einsum(
            'bhsk,bhsv->bhkv', kt * jnp.exp(b_last - b), it)
        return new_state, o

    s0 = jnp.zeros((B, HG_HEADS, HG_DK, HG_DV), jnp.float32)
    _, o = lax.scan(step, s0, (qc, kc, ic, lfc))
    o = o.transpose(1, 0, 3, 2, 4).reshape(B, S, HG_HEADS, HG_DV)
    o = o * lax.rsqrt(jnp.mean(o * o, axis=-1, keepdims=True) + EPS)
    o = o.reshape(B, S, HG_HEADS * HG_DV) * norm_g.astype(jnp.float32) * jax.nn.silu(g.astype(jnp.float32))
    return o.astype(dt)


def _fwd_setup_inputs(seed: int = 0) -> dict:
    key = jax.random.key(seed)
    ks = jax.random.split(key, 24)
    f32 = jnp.float32
    nrm = lambda k, shape, s: jax.random.normal(k, shape, f32) * s
    L, D = DEPTH, D_MODEL
    return {
        "x": nrm(ks[0], (BATCH, SEQ, D), 1.0),
        "c": nrm(ks[1], (BATCH, D), 1.0),
        "w_ada": nrm(ks[2], (L, D, 6 * D), 0.5 * D ** -0.5),
        "b_ada": nrm(ks[3], (L, 6 * D), 0.02),
        "norm1_g": 1.0 + nrm(ks[4], (L, D), 0.02),
        "w_in": nrm(ks[5], (L, D, IN_WIDTH), D ** -0.5),
        "b_gate": nrm(ks[6], (L, 2 * D), 0.02),
        "gmlp_ln_g": 1.0 + nrm(ks[7], (L, GMLP_WIDTH), 0.02),
        "gmlp_ln_b": nrm(ks[8], (L, GMLP_WIDTH), 0.02),
        "gmlp_ws": nrm(ks[9], (L, GMLP_GROUPS, GMLP_CHUNK, GMLP_CHUNK), GMLP_CHUNK ** -0.5),
        "gmlp_bs": 1.0 + nrm(ks[10], (L, GMLP_GROUPS, GMLP_CHUNK), 0.1),
        "hg_lb": nrm(ks[11], (DEPTH + 1, HG_WIDTH), 1.0),
        "hg_norm_g": 1.0 + nrm(ks[12], (L, HG_WIDTH), 0.02),
        "w_branch_gmlp": nrm(ks[13], (L, GMLP_WIDTH, D), GMLP_WIDTH ** -0.5),
        "w_branch_hg": nrm(ks[14], (L, HG_WIDTH, D), HG_WIDTH ** -0.5),
        "w_out": nrm(ks[15], (L, D, D), D ** -0.5),
        "norm2_g": 1.0 + nrm(ks[16], (L, D), 0.02),
        "w_ffn_in": nrm(ks[17], (L, D, 2 * FFN_HIDDEN), D ** -0.5),
        "w_ffn_out": nrm(ks[18], (L, FFN_HIDDEN, D), FFN_HIDDEN ** -0.5),
        "final_norm_g": 1.0 + nrm(ks[19], (D,), 0.02),
    }


def _fwd_reference(x, c, w_ada, b_ada, norm1_g, w_in, b_gate, gmlp_ln_g, gmlp_ln_b, gmlp_ws, gmlp_bs,
              hg_lb, hg_norm_g, w_branch_gmlp, w_branch_hg, w_out, norm2_g, w_ffn_in, w_ffn_out,
              final_norm_g):
    lb_all = jnp.cumsum(jax.nn.softmax(hg_lb.astype(jnp.float32), axis=0), axis=0)
    splits = np.cumsum([GMLP_WIDTH, GMLP_WIDTH, HG_WIDTH, HG_WIDTH, HG_WIDTH, HG_WIDTH, D_MODEL]).tolist()
    c_act = jax.nn.silu(c)
    for l in range(DEPTH):
        mod = c_act @ w_ada[l] + b_ada[l]
        sh1, sc1, gt1, sh2, sc2, gt2 = jnp.split(mod, 6, axis=-1)
        h = modulate(rmsnorm(x, norm1_g[l]), sh1, sc1)
        z = h @ w_in[l]
        u, v, q, f_pre, iv, og, ga, gb = jnp.split(z, splits, axis=-1)
        ya = gmlp_branch(jax.nn.gelu(u), jax.nn.gelu(v), gmlp_ln_g[l], gmlp_ln_b[l], gmlp_ws[l], gmlp_bs[l])
        yb = hgrn2_branch(q, f_pre, iv, og, lb_all[l], hg_norm_g[l])
        ga = jax.nn.sigmoid(ga + b_gate[l, :D_MODEL])
        gb = jax.nn.sigmoid(gb + b_gate[l, D_MODEL:])
        y = ga * (ya @ w_branch_gmlp[l]) + gb * (yb @ w_branch_hg[l])
        x = x + gt1[:, None, :] * (y @ w_out[l])
        h = modulate(rmsnorm(x, norm2_g[l]), sh2, sc2)
        a, up = jnp.split(h @ w_ffn_in[l], 2, axis=-1)
        x = x + gt2[:, None, :] * ((jax.nn.silu(a) * up) @ w_ffn_out[l])
    return rmsnorm(x, final_norm_g)


import jax as _jax
import jax.numpy as _jnp

TWIN_FORMAT = 'train_step'
FWD_PARAMS = ['x', 'c', 'w_ada', 'b_ada', 'norm1_g', 'w_in', 'b_gate', 'gmlp_ln_g', 'gmlp_ln_b', 'gmlp_ws', 'gmlp_bs', 'hg_lb', 'hg_norm_g', 'w_branch_gmlp', 'w_branch_hg', 'w_out', 'norm2_g', 'w_ffn_in', 'w_ffn_out', 'final_norm_g']
TWIN_WEIGHTS = ['w_ada', 'b_ada', 'norm1_g', 'w_in', 'b_gate', 'gmlp_ln_g', 'gmlp_ln_b', 'gmlp_ws', 'gmlp_bs', 'hg_lb', 'hg_norm_g', 'w_branch_gmlp', 'w_branch_hg', 'w_out', 'norm2_g', 'w_ffn_in', 'w_ffn_out', 'final_norm_g']
TWIN_DIFF_INPUT = 'x'
TWIN_INPUTS = ['x', 'c', 'w_ada', 'b_ada', 'norm1_g', 'w_in', 'b_gate', 'gmlp_ln_g', 'gmlp_ln_b', 'gmlp_ws', 'gmlp_bs', 'hg_lb', 'hg_norm_g', 'w_branch_gmlp', 'w_branch_hg', 'w_out', 'norm2_g', 'w_ffn_in', 'w_ffn_out', 'final_norm_g', 'loss_target', 'm_w_ada', 'm_b_ada', 'm_norm1_g', 'm_w_in', 'm_b_gate', 'm_gmlp_ln_g', 'm_gmlp_ln_b', 'm_gmlp_ws', 'm_gmlp_bs', 'm_hg_lb', 'm_hg_norm_g', 'm_w_branch_gmlp', 'm_w_branch_hg', 'm_w_out', 'm_norm2_g', 'm_w_ffn_in', 'm_w_ffn_out', 'm_final_norm_g', 'v_w_ada', 'v_b_ada', 'v_norm1_g', 'v_w_in', 'v_b_gate', 'v_gmlp_ln_g', 'v_gmlp_ln_b', 'v_gmlp_ws', 'v_gmlp_bs', 'v_hg_lb', 'v_hg_norm_g', 'v_w_branch_gmlp', 'v_w_branch_hg', 'v_w_out', 'v_norm2_g', 'v_w_ffn_in', 'v_w_ffn_out', 'v_final_norm_g']
TWIN_OUTPUTS = ['loss', 'grad_x', 'grad_w_ada', 'grad_b_ada', 'grad_norm1_g', 'grad_w_in', 'grad_b_gate', 'grad_gmlp_ln_g', 'grad_gmlp_ln_b', 'grad_gmlp_ws', 'grad_gmlp_bs', 'grad_hg_lb', 'grad_hg_norm_g', 'grad_w_branch_gmlp', 'grad_w_branch_hg', 'grad_w_out', 'grad_norm2_g', 'grad_w_ffn_in', 'grad_w_ffn_out', 'grad_final_norm_g', 'delta_w_ada', 'delta_b_ada', 'delta_norm1_g', 'delta_w_in', 'delta_b_gate', 'delta_gmlp_ln_g', 'delta_gmlp_ln_b', 'delta_gmlp_ws', 'delta_gmlp_bs', 'delta_hg_lb', 'delta_hg_norm_g', 'delta_w_branch_gmlp', 'delta_w_branch_hg', 'delta_w_out', 'delta_norm2_g', 'delta_w_ffn_in', 'delta_w_ffn_out', 'delta_final_norm_g', 'new_m_w_ada', 'new_m_b_ada', 'new_m_norm1_g', 'new_m_w_in', 'new_m_b_gate', 'new_m_gmlp_ln_g', 'new_m_gmlp_ln_b', 'new_m_gmlp_ws', 'new_m_gmlp_bs', 'new_m_hg_lb', 'new_m_hg_norm_g', 'new_m_w_branch_gmlp', 'new_m_w_branch_hg', 'new_m_w_out', 'new_m_norm2_g', 'new_m_w_ffn_in', 'new_m_w_ffn_out', 'new_m_final_norm_g', 'new_v_w_ada', 'new_v_b_ada', 'new_v_norm1_g', 'new_v_w_in', 'new_v_b_gate', 'new_v_gmlp_ln_g', 'new_v_gmlp_ln_b', 'new_v_gmlp_ws', 'new_v_gmlp_bs', 'new_v_hg_lb', 'new_v_hg_norm_g', 'new_v_w_branch_gmlp', 'new_v_w_branch_hg', 'new_v_w_out', 'new_v_norm2_g', 'new_v_w_ffn_in', 'new_v_w_ffn_out', 'new_v_final_norm_g']
TWIN_LEAF_KINDS = {'loss': 'loss', 'grad_x': 'grad_x', 'grad_w_ada': 'grad_w', 'grad_b_ada': 'grad_w', 'grad_norm1_g': 'grad_w', 'grad_w_in': 'grad_w', 'grad_b_gate': 'grad_w', 'grad_gmlp_ln_g': 'grad_w', 'grad_gmlp_ln_b': 'grad_w', 'grad_gmlp_ws': 'grad_w', 'grad_gmlp_bs': 'grad_w', 'grad_hg_lb': 'grad_w', 'grad_hg_norm_g': 'grad_w', 'grad_w_branch_gmlp': 'grad_w', 'grad_w_branch_hg': 'grad_w', 'grad_w_out': 'grad_w', 'grad_norm2_g': 'grad_w', 'grad_w_ffn_in': 'grad_w', 'grad_w_ffn_out': 'grad_w', 'grad_final_norm_g': 'grad_w', 'delta_w_ada': 'delta_w', 'delta_b_ada': 'delta_w', 'delta_norm1_g': 'delta_w', 'delta_w_in': 'delta_w', 'delta_b_gate': 'delta_w', 'delta_gmlp_ln_g': 'delta_w', 'delta_gmlp_ln_b': 'delta_w', 'delta_gmlp_ws': 'delta_w', 'delta_gmlp_bs': 'delta_w', 'delta_hg_lb': 'delta_w', 'delta_hg_norm_g': 'delta_w', 'delta_w_branch_gmlp': 'delta_w', 'delta_w_branch_hg': 'delta_w', 'delta_w_out': 'delta_w', 'delta_norm2_g': 'delta_w', 'delta_w_ffn_in': 'delta_w', 'delta_w_ffn_out': 'delta_w', 'delta_final_norm_g': 'delta_w', 'new_m_w_ada': 'new_m', 'new_m_b_ada': 'new_m', 'new_m_norm1_g': 'new_m', 'new_m_w_in': 'new_m', 'new_m_b_gate': 'new_m', 'new_m_gmlp_ln_g': 'new_m', 'new_m_gmlp_ln_b': 'new_m', 'new_m_gmlp_ws': 'new_m', 'new_m_gmlp_bs': 'new_m', 'new_m_hg_lb': 'new_m', 'new_m_hg_norm_g': 'new_m', 'new_m_w_branch_gmlp': 'new_m', 'new_m_w_branch_hg': 'new_m', 'new_m_w_out': 'new_m', 'new_m_norm2_g': 'new_m', 'new_m_w_ffn_in': 'new_m', 'new_m_w_ffn_out': 'new_m', 'new_m_final_norm_g': 'new_m', 'new_v_w_ada': 'new_v', 'new_v_b_ada': 'new_v', 'new_v_norm1_g': 'new_v', 'new_v_w_in': 'new_v', 'new_v_b_gate': 'new_v', 'new_v_gmlp_ln_g': 'new_v', 'new_v_gmlp_ln_b': 'new_v', 'new_v_gmlp_ws': 'new_v', 'new_v_gmlp_bs': 'new_v', 'new_v_hg_lb': 'new_v', 'new_v_hg_norm_g': 'new_v', 'new_v_w_branch_gmlp': 'new_v', 'new_v_w_branch_hg': 'new_v', 'new_v_w_out': 'new_v', 'new_v_norm2_g': 'new_v', 'new_v_w_ffn_in': 'new_v', 'new_v_w_ffn_out': 'new_v', 'new_v_final_norm_g': 'new_v'}


def _forward(args):
    return _fwd_reference(*[args[k] for k in FWD_PARAMS])


def _output_shape():
    out = _jax.eval_shape(lambda: _forward(_fwd_setup_inputs(0)))
    return out.shape, out.dtype

N_MICROBATCH = 1
ADAM_LR = 0.001
ADAM_B1 = 0.9
ADAM_B2 = 0.999
ADAM_EPS = 1e-08
ADAM_WD = 0.01
ADAM_STEP = 10
PER_EXAMPLE_BATCH_AXIS = {'x': 0, 'c': 0, 'loss_target': 0}
SHARED_INPUTS = []
_WEIGHT_DTYPES = {'w_ada': _jnp.float32, 'b_ada': _jnp.float32, 'norm1_g': _jnp.float32, 'w_in': _jnp.float32, 'b_gate': _jnp.float32, 'gmlp_ln_g': _jnp.float32, 'gmlp_ln_b': _jnp.float32, 'gmlp_ws': _jnp.float32, 'gmlp_bs': _jnp.float32, 'hg_lb': _jnp.float32, 'hg_norm_g': _jnp.float32, 'w_branch_gmlp': _jnp.float32, 'w_branch_hg': _jnp.float32, 'w_out': _jnp.float32, 'norm2_g': _jnp.float32, 'w_ffn_in': _jnp.float32, 'w_ffn_out': _jnp.float32, 'final_norm_g': _jnp.float32}
MOMENT_SCALE = {'w_ada': 1.695576e-02, 'b_ada': 2.786314e-02, 'norm1_g': 1.462125e-02, 'w_in': 7.033945e-03, 'b_gate': 3.104215e-03, 'gmlp_ln_g': 7.550766e-03, 'gmlp_ln_b': 7.667444e-03, 'gmlp_ws': 7.455407e-03, 'gmlp_bs': 1.070663e-02, 'hg_lb': 6.597453e-04, 'hg_norm_g': 1.020343e-02, 'w_branch_gmlp': 9.242965e-03, 'w_branch_hg': 6.893927e-03, 'w_out': 1.152160e-02, 'norm2_g': 1.855792e-02, 'w_ffn_in': 8.079256e-03, 'w_ffn_out': 1.317495e-02, 'final_norm_g': 8.005565e+00}


def _to_microbatches(a, axis):
    t = _jnp.moveaxis(a, axis, 0)
    t = t.reshape((N_MICROBATCH, t.shape[0] // N_MICROBATCH) + t.shape[1:])
    return _jnp.moveaxis(t, 1, axis + 1)


def setup_inputs(seed: int = 0) -> dict:
    inp = _fwd_setup_inputs(seed)
    key = _jax.random.fold_in(_jax.random.key(seed), 7919)
    shape, _ = _output_shape()
    out = dict(inp)
    out["loss_target"] = _jax.random.normal(_jax.random.fold_in(key, 0), shape, _jnp.float32)
    for i, name in enumerate(TWIN_WEIGHTS):
        w = inp[name].astype(_jnp.float32)
        if MOMENT_SCALE is None:
            s = _jnp.sqrt(_jnp.mean(_jnp.square(w)) + 1e-30)
        else:
            s = MOMENT_SCALE[name]
        km, kv = _jax.random.split(_jax.random.fold_in(key, i + 1))
        out[name] = w
        out["m_" + name] = s * _jax.random.normal(km, w.shape, _jnp.float32)
        out["v_" + name] = (s * s) * _jax.random.uniform(kv, w.shape, _jnp.float32, 0.5, 1.5)
    if N_MICROBATCH > 1:
        for name, axis in PER_EXAMPLE_BATCH_AXIS.items():
            out[name] = _to_microbatches(out[name], axis)
    return {'x': out['x'], 'c': out['c'], 'w_ada': out['w_ada'], 'b_ada': out['b_ada'], 'norm1_g': out['norm1_g'], 'w_in': out['w_in'], 'b_gate': out['b_gate'], 'gmlp_ln_g': out['gmlp_ln_g'], 'gmlp_ln_b': out['gmlp_ln_b'], 'gmlp_ws': out['gmlp_ws'], 'gmlp_bs': out['gmlp_bs'], 'hg_lb': out['hg_lb'], 'hg_norm_g': out['hg_norm_g'], 'w_branch_gmlp': out['w_branch_gmlp'], 'w_branch_hg': out['w_branch_hg'], 'w_out': out['w_out'], 'norm2_g': out['norm2_g'], 'w_ffn_in': out['w_ffn_in'], 'w_ffn_out': out['w_ffn_out'], 'final_norm_g': out['final_norm_g'], 'loss_target': out['loss_target'], 'm_w_ada': out['m_w_ada'], 'm_b_ada': out['m_b_ada'], 'm_norm1_g': out['m_norm1_g'], 'm_w_in': out['m_w_in'], 'm_b_gate': out['m_b_gate'], 'm_gmlp_ln_g': out['m_gmlp_ln_g'], 'm_gmlp_ln_b': out['m_gmlp_ln_b'], 'm_gmlp_ws': out['m_gmlp_ws'], 'm_gmlp_bs': out['m_gmlp_bs'], 'm_hg_lb': out['m_hg_lb'], 'm_hg_norm_g': out['m_hg_norm_g'], 'm_w_branch_gmlp': out['m_w_branch_gmlp'], 'm_w_branch_hg': out['m_w_branch_hg'], 'm_w_out': out['m_w_out'], 'm_norm2_g': out['m_norm2_g'], 'm_w_ffn_in': out['m_w_ffn_in'], 'm_w_ffn_out': out['m_w_ffn_out'], 'm_final_norm_g': out['m_final_norm_g'], 'v_w_ada': out['v_w_ada'], 'v_b_ada': out['v_b_ada'], 'v_norm1_g': out['v_norm1_g'], 'v_w_in': out['v_w_in'], 'v_b_gate': out['v_b_gate'], 'v_gmlp_ln_g': out['v_gmlp_ln_g'], 'v_gmlp_ln_b': out['v_gmlp_ln_b'], 'v_gmlp_ws': out['v_gmlp_ws'], 'v_gmlp_bs': out['v_gmlp_bs'], 'v_hg_lb': out['v_hg_lb'], 'v_hg_norm_g': out['v_hg_norm_g'], 'v_w_branch_gmlp': out['v_w_branch_gmlp'], 'v_w_branch_hg': out['v_w_branch_hg'], 'v_w_out': out['v_w_out'], 'v_norm2_g': out['v_norm2_g'], 'v_w_ffn_in': out['v_w_ffn_in'], 'v_w_ffn_out': out['v_w_ffn_out'], 'v_final_norm_g': out['v_final_norm_g']}


def _loss(weights, diff, rest, loss_target):
    with _jax.named_scope("forward"):
        args = {**rest, TWIN_DIFF_INPUT: diff, **{k: w.astype(_WEIGHT_DTYPES[k]) for k, w in weights.items()}}
        y = _forward(args)
    with _jax.named_scope("loss_head"):
        err = _jnp.square(y.astype(_jnp.float32) - loss_target)
        return 0.5 * _jnp.sum(_jnp.mean(err, axis=-1)) if err.ndim else 0.5 * err


def _adamw(w, g, m, v):
    m = ADAM_B1 * m + (1.0 - ADAM_B1) * g
    v = ADAM_B2 * v + (1.0 - ADAM_B2) * _jnp.square(g)
    m_hat = m / (1.0 - ADAM_B1 ** ADAM_STEP)
    v_hat = v / (1.0 - ADAM_B2 ** ADAM_STEP)
    delta = -ADAM_LR * (m_hat / (_jnp.sqrt(v_hat) + ADAM_EPS) + ADAM_WD * w)
    return delta, m, v


def reference(x, c, w_ada, b_ada, norm1_g, w_in, b_gate, gmlp_ln_g, gmlp_ln_b, gmlp_ws, gmlp_bs, hg_lb, hg_norm_g, w_branch_gmlp, w_branch_hg, w_out, norm2_g, w_ffn_in, w_ffn_out, final_norm_g, loss_target, m_w_ada, m_b_ada, m_norm1_g, m_w_in, m_b_gate, m_gmlp_ln_g, m_gmlp_ln_b, m_gmlp_ws, m_gmlp_bs, m_hg_lb, m_hg_norm_g, m_w_branch_gmlp, m_w_branch_hg, m_w_out, m_norm2_g, m_w_ffn_in, m_w_ffn_out, m_final_norm_g, v_w_ada, v_b_ada, v_norm1_g, v_w_in, v_b_gate, v_gmlp_ln_g, v_gmlp_ln_b, v_gmlp_ws, v_gmlp_bs, v_hg_lb, v_hg_norm_g, v_w_branch_gmlp, v_w_branch_hg, v_w_out, v_norm2_g, v_w_ffn_in, v_w_ffn_out, v_final_norm_g):
    given = dict(x=x, c=c, w_ada=w_ada, b_ada=b_ada, norm1_g=norm1_g, w_in=w_in, b_gate=b_gate, gmlp_ln_g=gmlp_ln_g, gmlp_ln_b=gmlp_ln_b, gmlp_ws=gmlp_ws, gmlp_bs=gmlp_bs, hg_lb=hg_lb, hg_norm_g=hg_norm_g, w_branch_gmlp=w_branch_gmlp, w_branch_hg=w_branch_hg, w_out=w_out, norm2_g=norm2_g, w_ffn_in=w_ffn_in, w_ffn_out=w_ffn_out, final_norm_g=final_norm_g, loss_target=loss_target, m_w_ada=m_w_ada, m_b_ada=m_b_ada, m_norm1_g=m_norm1_g, m_w_in=m_w_in, m_b_gate=m_b_gate, m_gmlp_ln_g=m_gmlp_ln_g, m_gmlp_ln_b=m_gmlp_ln_b, m_gmlp_ws=m_gmlp_ws, m_gmlp_bs=m_gmlp_bs, m_hg_lb=m_hg_lb, m_hg_norm_g=m_hg_norm_g, m_w_branch_gmlp=m_w_branch_gmlp, m_w_branch_hg=m_w_branch_hg, m_w_out=m_w_out, m_norm2_g=m_norm2_g, m_w_ffn_in=m_w_ffn_in, m_w_ffn_out=m_w_ffn_out, m_final_norm_g=m_final_norm_g, v_w_ada=v_w_ada, v_b_ada=v_b_ada, v_norm1_g=v_norm1_g, v_w_in=v_w_in, v_b_gate=v_b_gate, v_gmlp_ln_g=v_gmlp_ln_g, v_gmlp_ln_b=v_gmlp_ln_b, v_gmlp_ws=v_gmlp_ws, v_gmlp_bs=v_gmlp_bs, v_hg_lb=v_hg_lb, v_hg_norm_g=v_hg_norm_g, v_w_branch_gmlp=v_w_branch_gmlp, v_w_branch_hg=v_w_branch_hg, v_w_out=v_w_out, v_norm2_g=v_norm2_g, v_w_ffn_in=v_w_ffn_in, v_w_ffn_out=v_w_ffn_out, v_final_norm_g=v_final_norm_g)
    weights = {n: given[n] for n in TWIN_WEIGHTS}
    shared = {n: given[n] for n in SHARED_INPUTS}
    per_example = {n: given[n] for n in ['x', 'c']}
    grad_fn = _jax.value_and_grad(_loss, argnums=(0, 1))

    def one_microbatch(ex, loss_target):
        ex = dict(ex)
        diff = ex.pop(TWIN_DIFF_INPUT)
        return grad_fn(weights, diff, {**shared, **ex}, loss_target)

    if N_MICROBATCH == 1:
        loss, (grad_w, grad_x) = one_microbatch(per_example, given["loss_target"])
    else:
        def body(carry, xs):
            loss_sum, grad_sum = carry
            l_k, (gw_k, gx_k) = one_microbatch(xs[0], xs[1])
            with _jax.named_scope("update"):
                return (loss_sum + l_k, _jax.tree.map(_jnp.add, grad_sum, gw_k)), gx_k

        init = (_jnp.zeros((), _jnp.float32), _jax.tree.map(_jnp.zeros_like, weights))
        (loss, grad_w), grad_x = _jax.lax.scan(body, init, (per_example, given["loss_target"]))
    with _jax.named_scope("update"):
        delta_w, new_m, new_v = {}, {}, {}
        for n in TWIN_WEIGHTS:
            delta_w[n], new_m[n], new_v[n] = _adamw(weights[n], grad_w[n], given["m_" + n], given["v_" + n])
    return (loss, grad_x, *[grad_w[n] for n in TWIN_WEIGHTS], *[delta_w[n] for n in TWIN_WEIGHTS],
            *[new_m[n] for n in TWIN_WEIGHTS], *[new_v[n] for n in TWIN_WEIGHTS])
```

```python
import functools

import jax
import jax.numpy as jnp
from jax import lax
from jax.experimental import pallas as pl
from jax.experimental.pallas import tpu as pltpu

F32 = jnp.float32
BF16 = jnp.bfloat16
EPS = 1e-6
LANES = 128
N_CHIPS = 4
N_DEV = 8
VMEM_LIMIT_BYTES = 56 * 1024 * 1024
HG_CHUNK = 32
HG_ROWS = 256
EXP_CLAMP = 80.0
ADAM_LR, ADAM_B1, ADAM_B2, ADAM_EPS, ADAM_WD, ADAM_STEP = 0.001, 0.9, 0.999, 1e-08, 0.01, 10
MESH = pl.DeviceIdType.MESH

NN = (((1,), (0,)), ((), ()))
NT = (((1,), (1,)), ((), ()))
TN = (((0,), (0,)), ((), ()))


def _dot(a, b, dims=NN, precision=None):
    return lax.dot_general(a, b, dims, precision=precision, preferred_element_type=F32)


def _params(*semantics):
    return pltpu.CompilerParams(dimension_semantics=semantics, vmem_limit_bytes=VMEM_LIMIT_BYTES)


def _pick(dim, pref):
    if dim <= pref:
        return dim
    best = None
    for cand in range(LANES, pref + 1, LANES):
        if dim % cand == 0:
            best = cand
    assert best is not None, (dim, pref)
    return best


def _sigmoid(x):
    return 1.0 / (1.0 + jnp.exp(-x))


def _gelu(x):
    c = 0.7978845608028654
    return 0.5 * x * (1.0 + jnp.tanh(c * (x + 0.044715 * x * x * x)))


def _gelu_grad(x):
    c = 0.7978845608028654
    t = jnp.tanh(c * (x + 0.044715 * x * x * x))
    return 0.5 * (1.0 + t) + 0.5 * x * (1.0 - t * t) * c * (1.0 + 3.0 * 0.044715 * x * x)


def _rms(x):
    r = lax.rsqrt(jnp.mean(x * x, axis=-1, keepdims=True) + EPS)
    return x * r, r


def _colsum(x):
    return jnp.sum(x, axis=0, keepdims=True)


def _accumulate(first, ref, val):
    @pl.when(first)
    def _():
        ref[...] = val

    @pl.when(jnp.logical_not(first))
    def _():
        ref[...] += val


def _matmul(a, b, *, mode, name, out_dtype, b_slots=False, out_slots=False, bm=1024, bn=1024, bk=512):
    if mode == "nn":
        m, k = a.shape
        n = b.shape[2] * N_CHIPS if b_slots else b.shape[1]
        per = b.shape[2] if b_slots else n
    elif mode == "nt":
        m, k = a.shape
        n = b.shape[1] if b_slots else b.shape[0]
        per = b.shape[2] if b_slots else k
    else:
        k, m = a.shape
        n = b.shape[1]
        per = n // N_CHIPS if out_slots else n
    bm = _pick(m, bm)
    if mode == "nt":
        bn, bk = _pick(n, bn), _pick(per, bk)
    else:
        bn, bk = _pick(per, bn), _pick(k, bk)
    nk = k // bk
    per_blocks = per // (bk if mode == "nt" else bn)
    dims = {"nn": NN, "nt": NT, "tn": TN}[mode]

    def body(a_ref, b_ref, o_ref, acc_ref):
        kk = pl.program_id(2)

        @pl.when(kk == 0)
        def _():
            acc_ref[...] = jnp.zeros_like(acc_ref)

        acc_ref[...] += _dot(a_ref[...], b_ref[...], dims)

        @pl.when(kk == nk - 1)
        def _():
            o_ref[...] = acc_ref[...].astype(o_ref.dtype)

    if mode == "nn":
        a_spec = pl.BlockSpec((bm, bk), lambda i, j, kk: (i, kk))
        if b_slots:
            b_spec = pl.BlockSpec((None, bk, bn), lambda i, j, kk: (j // per_blocks, kk, j % per_blocks))
        else:
            b_spec = pl.BlockSpec((bk, bn), lambda i, j, kk: (kk, j))
    elif mode == "nt":
        a_spec = pl.BlockSpec((bm, bk), lambda i, j, kk: (i, kk))
        if b_slots:
            b_spec = pl.BlockSpec((None, bn, bk), lambda i, j, kk: (kk // per_blocks, j, kk % per_blocks))
        else:
            b_spec = pl.BlockSpec((bn, bk), lambda i, j, kk: (j, kk))
    else:
        a_spec = pl.BlockSpec((bk, bm), lambda i, j, kk: (kk, i))
        b_spec = pl.BlockSpec((bk, bn), lambda i, j, kk: (kk, j))
    if out_slots:
        o_spec = pl.BlockSpec((None, bm, bn), lambda i, j, kk: (j // per_blocks, i, j % per_blocks))
        out_shape = jax.ShapeDtypeStruct((N_CHIPS, m, per), out_dtype)
    else:
        o_spec = pl.BlockSpec((bm, bn), lambda i, j, kk: (i, j))
        out_shape = jax.ShapeDtypeStruct((m, n), out_dtype)
    return pl.pallas_call(
        body, name=name, grid=(m // bm, n // bn, nk), in_specs=[a_spec, b_spec], out_specs=o_spec,
        out_shape=out_shape, scratch_shapes=[pltpu.VMEM((bm, bn), F32)],
        compiler_params=_params("parallel", "parallel", "arbitrary"),
    )(a, b)


def _place():
    x, y, c = lax.axis_index("x"), lax.axis_index("y"), lax.axis_index("c")
    chips = [(1 - x, y), (x, 1 - y), (1 - x, 1 - y)]
    return x, y, c, chips


def _all_gather8(block, name):
    def body(x_ref, out_ref, send_sems, recv_sems, local_sem):
        x, y, c, chips = _place()
        me, sibling = (x, y, c), (x, y, 1 - c)

        def slot(px, py, pc):
            return out_ref.at[4 * px + 2 * py + pc]

        def copy(k, blk, to, src=None):
            return pltpu.make_async_remote_copy(
                src_ref=slot(*blk) if src is None else src, dst_ref=slot(*blk),
                send_sem=send_sems.at[k], recv_sem=recv_sems.at[k], device_id=to, device_id_type=MESH)

        mine = pltpu.make_async_copy(x_ref, slot(*me), local_sem)
        mine.start()
        first = [copy(0, me, sibling, src=x_ref)]
        first += [copy(1 + j, me, (*chip, c), src=x_ref) for j, chip in enumerate(chips)]
        for cp in first:
            cp.start()
        passed = [copy(4 + j, (*chip, c), sibling) for j, chip in enumerate(chips)]
        for j, chip in enumerate(chips):
            copy(1 + j, (*chip, c), me).wait_recv()
            passed[j].start()
        copy(0, sibling, me).wait_recv()
        for j, chip in enumerate(chips):
            copy(4 + j, (*chip, 1 - c), me).wait_recv()
        for cp in first + passed:
            cp.wait_send()
        mine.wait()

    return pl.pallas_call(
        body, name=name, out_shape=jax.ShapeDtypeStruct((N_DEV,) + block.shape, block.dtype),
        in_specs=[pl.BlockSpec(memory_space=pltpu.VMEM)], out_specs=pl.BlockSpec(memory_space=pltpu.VMEM),
        scratch_shapes=[pltpu.SemaphoreType.DMA((7,)), pltpu.SemaphoreType.DMA((7,)), pltpu.SemaphoreType.DMA],
        compiler_params=pltpu.CompilerParams(vmem_limit_bytes=VMEM_LIMIT_BYTES),
    )(block)


def _gather_weights(shards, name):
    n = len(shards)

    def body(*refs):
        in_refs, out_refs = refs[:n], refs[n:2 * n]
        send_sems, recv_sems, local_sems = refs[2 * n:]
        x, y, c, chips = _place()
        sibling = (x, y, 1 - c)
        locals_, sends, passes = [], [], []
        for w in range(n):
            rows = in_refs[w].shape[0] // 2

            def region(px, py, pc, w=w, rows=rows):
                return out_refs[w].at[2 * px + py, pl.ds(pc * rows, rows), :]

            def copy(k, blk, to, src=None, w=w, region=region):
                return pltpu.make_async_remote_copy(
                    src_ref=region(*blk) if src is None else src, dst_ref=region(*blk),
                    send_sem=send_sems.at[6 * w + k], recv_sem=recv_sems.at[6 * w + k],
                    device_id=to, device_id_type=MESH)

            mine = pltpu.make_async_copy(in_refs[w], out_refs[w].at[2 * x + y], local_sems.at[w])
            mine.start()
            locals_.append(mine)
            half = in_refs[w].at[pl.ds(c * rows, rows), :]
            for j, chip in enumerate(chips):
                cp = copy(j, (x, y, c), (*chip, c), src=half)
                cp.start()
                sends.append(cp)
            passes.append((copy, region))
        for w in range(n):
            copy, _ = passes[w]
            for j, chip in enumerate(chips):
                copy(j, (*chip, c), (x, y, c)).wait_recv()
                fwd = copy(3 + j, (*chip, c), sibling)
                fwd.start()
                sends.append(fwd)
        for w in range(n):
            copy, _ = passes[w]
            for j, chip in enumerate(chips):
                copy(3 + j, (*chip, 1 - c), (x, y, c)).wait_recv()
        for cp in sends:
            cp.wait_send()
        for cp in locals_:
            cp.wait()

    any_spec = pl.BlockSpec(memory_space=pl.ANY)
    return pl.pallas_call(
        body, name=name,
        out_shape=[jax.ShapeDtypeStruct((N_CHIPS,) + s.shape, s.dtype) for s in shards],
        in_specs=[any_spec] * n, out_specs=[any_spec] * n,
        scratch_shapes=[pltpu.SemaphoreType.DMA((6 * n,)), pltpu.SemaphoreType.DMA((6 * n,)),
                        pltpu.SemaphoreType.DMA((n,))],
    )(*shards)


def _gather_partials(grads, name):
    n = len(grads)

    def body(*refs):
        in_refs, out_refs = refs[:n], refs[n:2 * n]
        send_sems, recv_sems, local_sems = refs[2 * n:]
        x, y, c, chips = _place()
        me, sibling = (x, y, c), (x, y, 1 - c)
        locals_, sends, makers = [], [], []
        for w in range(n):
            def slot(px, py, pc, w=w):
                return out_refs[w].at[4 * px + 2 * py + pc]

            def copy(k, blk, to, src=None, w=w, slot=slot):
                return pltpu.make_async_remote_copy(
                    src_ref=slot(*blk) if src is None else src, dst_ref=slot(*blk),
                    send_sem=send_sems.at[7 * w + k], recv_sem=recv_sems.at[7 * w + k],
                    device_id=to, device_id_type=MESH)

            own = in_refs[w].at[2 * x + y]
            mine = pltpu.make_async_copy(own, slot(*me), local_sems.at[w])
            mine.start()
            locals_.append(mine)
            first = [copy(0, me, sibling, src=own)]
            first += [copy(1 + j, me, (*chip, c), src=in_refs[w].at[2 * chip[0] + chip[1]])
                      for j, chip in enumerate(chips)]
            for cp in first:
                cp.start()
            sends += first
            makers.append(copy)
        for w in range(n):
            copy = makers[w]
            for j, chip in enumerate(chips):
                copy(1 + j, (*chip, c), me).wait_recv()
                fwd = copy(4 + j, (*chip, c), sibling)
                fwd.start()
                sends.append(fwd)
        for w in range(n):
            copy = makers[w]
            copy(0, sibling, me).wait_recv()
            for j, chip in enumerate(chips):
                copy(4 + j, (*chip, 1 - c), me).wait_recv()
        for cp in sends:
            cp.wait_send()
        for cp in locals_:
            cp.wait()

    any_spec = pl.BlockSpec(memory_space=pl.ANY)
    return pl.pallas_call(
        body, name=name,
        out_shape=[jax.ShapeDtypeStruct((N_DEV,) + g.shape[1:], g.dtype) for g in grads],
        in_specs=[any_spec] * n, out_specs=[any_spec] * n,
        scratch_shapes=[pltpu.SemaphoreType.DMA((7 * n,)), pltpu.SemaphoreType.DMA((7 * n,)),
                        pltpu.SemaphoreType.DMA((n,))],
    )(*grads)


def _ada_fwd(c_all, w_q, b_q):
    d, n = w_q.shape
    bn = _pick(n, 512)

    def body(c_ref, w_ref, b_ref, o_ref):
        cv = c_ref[...]
        act = cv * _sigmoid(cv)
        o_ref[...] = _dot(act, w_ref[...], NN, lax.Precision.HIGHEST) + b_ref[...]

    return pl.pallas_call(
        body, name="ada_fwd", grid=(n // bn,),
        in_specs=[pl.BlockSpec((N_DEV, d), lambda j: (0, 0)), pl.BlockSpec((d, bn), lambda j: (0, j)),
                  pl.BlockSpec((1, bn), lambda j: (0, j))],
        out_specs=pl.BlockSpec((N_DEV, bn), lambda j: (0, j)),
        out_shape=jax.ShapeDtypeStruct((N_DEV, n), F32), compiler_params=_params("parallel"),
    )(c_all, w_q, b_q)


def _ada_bwd(c_all, dmod_q):
    d = c_all.shape[1]
    n = dmod_q.shape[1]
    bn = _pick(n, 512)

    def body(c_ref, g_ref, o_ref):
        cv = c_ref[...]
        act = cv * _sigmoid(cv)
        o_ref[...] = _dot(act, g_ref[...], TN, lax.Precision.HIGHEST)

    return pl.pallas_call(
        body, name="ada_bwd", grid=(n // bn,),
        in_specs=[pl.BlockSpec((N_DEV, d), lambda j: (0, 0)), pl.BlockSpec((N_DEV, bn), lambda j: (0, j))],
        out_specs=pl.BlockSpec((None, d, bn), lambda j: (0, 0, j)),
        out_shape=jax.ShapeDtypeStruct((1, d, n), F32), compiler_params=_params("parallel"),
    )(c_all, dmod_q)


def _row_spec(rb, width, col=0):
    return pl.BlockSpec((rb, width), lambda i, col=col: (i, col))


def _vec_spec(width, col=0):
    return pl.BlockSpec((1, width), lambda i, col=col: (0, col))


def _norm_mod_fwd(x, g, sc, sh, name, res=None, gt=None):
    s, d = x.shape
    rb = _pick(s, 256)
    has_res = res is not None

    def body(*refs):
        if has_res:
            x_ref, res_ref, gt_ref, g_ref, sc_ref, sh_ref, x1_ref, h_ref = refs
            xv = x_ref[...] + gt_ref[...] * res_ref[...]
            x1_ref[...] = xv
        else:
            x_ref, g_ref, sc_ref, sh_ref, h_ref = refs
            xv = x_ref[...]
        xh, _ = _rms(xv)
        h_ref[...] = (xh * g_ref[...] * (1.0 + sc_ref[...]) + sh_ref[...]).astype(BF16)

    row, vec = _row_spec(rb, d), _vec_spec(d)
    if has_res:
        ins, in_specs = (x, res, gt, g, sc, sh), [row, row, vec, vec, vec, vec]
        out_shape = [jax.ShapeDtypeStruct((s, d), F32), jax.ShapeDtypeStruct((s, d), BF16)]
        out_specs = [row, row]
    else:
        ins, in_specs = (x, g, sc, sh), [row, vec, vec, vec]
        out_shape, out_specs = jax.ShapeDtypeStruct((s, d), BF16), row
    return pl.pallas_call(body, name=name, grid=(s // rb,), in_specs=in_specs, out_specs=out_specs,
                          out_shape=out_shape, compiler_params=_params("parallel"))(*ins)


def _final_loss(x1, f, gt2, g_final, target):
    s, d = x1.shape
    rb = _pick(s, 256)

    def body(x1_ref, f_ref, gt_ref, g_ref, t_ref, dx_ref, df_ref, loss_ref, dg_ref, dgt_ref):
        first = pl.program_id(0) == 0
        fv, gt, gv = f_ref[...], gt_ref[...], g_ref[...]
        x2 = x1_ref[...] + gt * fv
        xh, r = _rms(x2)
        err = xh * gv - t_ref[...]
        blk = 0.5 * jnp.sum(jnp.sum(err * err, axis=1, keepdims=True), axis=0, keepdims=True) / d
        dy = err / d
        dxh = dy * gv
        dx = r * (dxh - xh * jnp.mean(dxh * xh, axis=-1, keepdims=True))
        dx_ref[...] = dx
        df_ref[...] = (dx * gt).astype(BF16)
        _accumulate(first, loss_ref, jnp.broadcast_to(blk, (1, LANES)))
        _accumulate(first, dg_ref, _colsum(dy * xh))
        _accumulate(first, dgt_ref, _colsum(dx * fv))

    row, vec = _row_spec(rb, d), _vec_spec(d)
    return pl.pallas_call(
        body, name="final_loss", grid=(s // rb,), in_specs=[row, row, vec, vec, row],
        out_specs=[row, row, _vec_spec(LANES), vec, vec],
        out_shape=[jax.ShapeDtypeStruct((s, d), F32), jax.ShapeDtypeStruct((s, d), BF16),
                   jax.ShapeDtypeStruct((1, LANES), F32), jax.ShapeDtypeStruct((1, d), F32),
                   jax.ShapeDtypeStruct((1, d), F32)],
        compiler_params=_params("arbitrary"),
    )(x1, f, gt2, g_final, target)


def _norm_mod_bwd(dh, xin, dres, g, sc, name, branch=None, gt=None):
    s, d = xin.shape
    rb = _pick(s, 256)
    has_branch = branch is not None

    def body(*refs):
        if has_branch:
            dh_ref, x_ref, dres_ref, g_ref, sc_ref, br_ref, gt_ref, dx_ref, dbr_ref, dsh_ref, dsc_ref, dg_ref, dgt_ref = refs
        else:
            dh_ref, x_ref, dres_ref, g_ref, sc_ref, dx_ref, dsh_ref, dsc_ref, dg_ref = refs
        first = pl.program_id(0) == 0
        gv = g_ref[...]
        xh, r = _rms(x_ref[...])
        dhv = dh_ref[...]
        dn = dhv * (1.0 + sc_ref[...])
        dxh = dn * gv
        dx = dres_ref[...] + r * (dxh - xh * jnp.mean(dxh * xh, axis=-1, keepdims=True))
        dx_ref[...] = dx
        _accumulate(first, dsh_ref, _colsum(dhv))
        _accumulate(first, dsc_ref, _colsum(dhv * xh * gv))
        _accumulate(first, dg_ref, _colsum(dn * xh))
        if has_branch:
            dbr_ref[...] = (dx * gt_ref[...]).astype(BF16)
            _accumulate(first, dgt_ref, _colsum(dx * br_ref[...]))

    row, vec = _row_spec(rb, d), _vec_spec(d)
    vec_shape = jax.ShapeDtypeStruct((1, d), F32)
    if has_branch:
        ins, in_specs = (dh, xin, dres, g, sc, branch, gt), [row, row, row, vec, vec, row, vec]
        out_specs = [row, row, vec, vec, vec, vec]
        out_shape = [jax.ShapeDtypeStruct((s, d), F32), jax.ShapeDtypeStruct((s, d), BF16)] + [vec_shape] * 4
    else:
        ins, in_specs = (dh, xin, dres, g, sc), [row, row, row, vec, vec]
        out_specs = [row, vec, vec, vec]
        out_shape = [jax.ShapeDtypeStruct((s, d), F32)] + [vec_shape] * 3
    return pl.pallas_call(body, name=name, grid=(s // rb,), in_specs=in_specs, out_specs=out_specs,
                          out_shape=out_shape, compiler_params=_params("arbitrary"))(*ins)


def _gate_fwd(pa, pb, z, b_gate, gate_col):
    s, d = pa.shape
    rb = _pick(s, 256)

    def body(pa_ref, pb_ref, za_ref, zb_ref, ba_ref, bb_ref, y_ref):
        ga = _sigmoid(za_ref[...] + ba_ref[...])
        gb = _sigmoid(zb_ref[...] + bb_ref[...])
        y_ref[...] = (ga * pa_ref[...] + gb * pb_ref[...]).astype(BF16)

    row = _row_spec(rb, d)
    return pl.pallas_call(
        body, name="gate_fwd", grid=(s // rb,),
        in_specs=[row, row, _row_spec(rb, d, gate_col), _row_spec(rb, d, gate_col + 1), _vec_spec(d, 0), _vec_spec(d, 1)],
        out_specs=row, out_shape=jax.ShapeDtypeStruct((s, d), BF16), compiler_params=_params("parallel"),
    )(pa, pb, z, z, b_gate, b_gate)


def _gate_bwd(dy, pa, pb, z, b_gate, gate_col):
    s, d = pa.shape
    rb = _pick(s, 256)

    def body(dy_ref, pa_ref, pb_ref, za_ref, zb_ref, ba_ref, bb_ref, dpa_ref, dpb_ref, dz_ref, db_ref):
        first = pl.program_id(0) == 0
        dyv = dy_ref[...]
        ga = _sigmoid(za_ref[...] + ba_ref[...])
        gb = _sigmoid(zb_ref[...] + bb_ref[...])
        dpa_ref[...] = (dyv * ga).astype(BF16)
        dpb_ref[...] = (dyv * gb).astype(BF16)
        dga = dyv * pa_ref[...] * ga * (1.0 - ga)
        dgb = dyv * pb_ref[...] * gb * (1.0 - gb)
        dz_ref[:, :d] = dga.astype(BF16)
        dz_ref[:, d:] = dgb.astype(BF16)

        @pl.when(first)
        def _():
            db_ref[:, :d] = _colsum(dga)
            db_ref[:, d:] = _colsum(dgb)

        @pl.when(jnp.logical_not(first))
        def _():
            db_ref[:, :d] += _colsum(dga)
            db_ref[:, d:] += _colsum(dgb)

    row = _row_spec(rb, d)
    return pl.pallas_call(
        body, name="gate_bwd", grid=(s // rb,),
        in_specs=[row, row, row, _row_spec(rb, d, gate_col), _row_spec(rb, d, gate_col + 1), _vec_spec(d, 0), _vec_spec(d, 1)],
        out_specs=[row, row, _row_spec(rb, 2 * d), _vec_spec(2 * d)],
        out_shape=[jax.ShapeDtypeStruct((s, d), BF16), jax.ShapeDtypeStruct((s, d), BF16),
                   jax.ShapeDtypeStruct((s, 2 * d), BF16), jax.ShapeDtypeStruct((1, 2 * d), F32)],
        compiler_params=_params("arbitrary"),
    )(dy, pa, pb, z, z, b_gate, b_gate)


def _swiglu_fwd(aup):
    s, ff2 = aup.shape
    ff = ff2 // 2
    rb = _pick(s, 128)

    def body(a_ref, u_ref, o_ref):
        a = a_ref[...]
        o_ref[...] = (a * _sigmoid(a) * u_ref[...]).astype(BF16)

    return pl.pallas_call(
        body, name="swiglu_fwd", grid=(s // rb,), in_specs=[_row_spec(rb, ff, 0), _row_spec(rb, ff, 1)],
        out_specs=_row_spec(rb, ff), out_shape=jax.ShapeDtypeStruct((s, ff), BF16),
        compiler_params=_params("parallel"),
    )(aup, aup)


def _swiglu_bwd(dhf, aup):
    s, ff2 = aup.shape
    ff = ff2 // 2
    rb = _pick(s, 128)

    def body(d_ref, a_ref, u_ref, o_ref):
        a, dv = a_ref[...], d_ref[...]
        sa = _sigmoid(a)
        o_ref[:, :ff] = (dv * u_ref[...] * sa * (1.0 + a * (1.0 - sa))).astype(BF16)
        o_ref[:, ff:] = (dv * a * sa).astype(BF16)

    return pl.pallas_call(
        body, name="swiglu_bwd", grid=(s // rb,),
        in_specs=[_row_spec(rb, ff), _row_spec(rb, ff, 0), _row_spec(rb, ff, 1)],
        out_specs=_row_spec(rb, ff2), out_shape=jax.ShapeDtypeStruct((s, ff2), BF16),
        compiler_params=_params("parallel"),
    )(dhf, aup, aup)


def _tril(n):
    return lax.broadcasted_iota(jnp.int32, (n, n), 0) >= lax.broadcasted_iota(jnp.int32, (n, n), 1)


def _gmlp_norm(v, ln_g, ln_b):
    gv = _gelu(v)
    mu = jnp.mean(gv, axis=-1, keepdims=True)
    cen = gv - mu
    rs = lax.rsqrt(jnp.mean(cen * cen, axis=-1, keepdims=True) + EPS)
    xh = cen * rs
    return xh, rs, xh * ln_g + ln_b


def _gmlp_fwd(z, ln_g, ln_b, ws, bs_t):
    s = z.shape[0]
    gw = ln_g.shape[1]
    groups, chunk, _ = ws.shape

    def body(u_ref, v_ref, lg_ref, lb_ref, ws_ref, bs_ref, ya_ref):
        gu = _gelu(u_ref[...])
        _, _, vn = _gmlp_norm(v_ref[...], lg_ref[...], lb_ref[...])
        mask = _tril(chunk)
        for g in range(groups):
            cols = slice(g * LANES, (g + 1) * LANES)
            wm = jnp.where(mask, ws_ref[g], 0.0).astype(BF16)
            sg = _dot(wm, vn[:, cols].astype(BF16)) + bs_ref[:, g:g + 1]
            ya_ref[:, cols] = (gu[:, cols] * sg).astype(BF16)

    return pl.pallas_call(
        body, name="gmlp_fwd", grid=(s // chunk,),
        in_specs=[_row_spec(chunk, gw, 0), _row_spec(chunk, gw, 1), _vec_spec(gw), _vec_spec(gw),
                  pl.BlockSpec((groups, chunk, chunk), lambda i: (0, 0, 0)), pl.BlockSpec((chunk, LANES), lambda i: (0, 0))],
        out_specs=_row_spec(chunk, gw), out_shape=jax.ShapeDtypeStruct((s, gw), BF16),
        compiler_params=_params("parallel"),
    )(z, z, ln_g, ln_b, ws, bs_t)


def _gmlp_bwd(dya, z, ln_g, ln_b, ws, bs_t):
    s = z.shape[0]
    gw = ln_g.shape[1]
    groups, chunk, _ = ws.shape

    def body(dya_ref, u_ref, v_ref, lg_ref, lb_ref, ws_ref, bs_ref, duv_ref, dws_ref, dbs_ref, dlg_ref, dlb_ref, dvn_ref):
        first = pl.program_id(0) == 0
        u, v, lg = u_ref[...], v_ref[...], lg_ref[...]
        gu = _gelu(u)
        xh, rs, vn = _gmlp_norm(v, lg, lb_ref[...])
        dyav = dya_ref[...]
        mask = _tril(chunk)
        lane = lax.broadcasted_iota(jnp.int32, (chunk, LANES), 1)
        dbs = jnp.zeros((chunk, LANES), F32)
        for g in range(groups):
            cols = slice(g * LANES, (g + 1) * LANES)
            wm = jnp.where(mask, ws_ref[g], 0.0).astype(BF16)
            vg = vn[:, cols].astype(BF16)
            sg = _dot(wm, vg) + bs_ref[:, g:g + 1]
            ds = dyav[:, cols] * gu[:, cols]
            duv_ref[:, cols] = (dyav[:, cols] * sg * _gelu_grad(u[:, cols])).astype(BF16)
            dsb = ds.astype(BF16)
            _accumulate(first, dws_ref.at[g], jnp.where(mask, _dot(dsb, vg, NT), 0.0))
            dbs = dbs + jnp.where(lane == g, jnp.sum(ds, axis=-1, keepdims=True), 0.0)
            dvn_ref[:, cols] = _dot(wm, dsb, TN)
        dvn = dvn_ref[...]
        _accumulate(first, dbs_ref, dbs)
        _accumulate(first, dlb_ref, _colsum(dvn))
        _accumulate(first, dlg_ref, _colsum(dvn * xh))
        dxh = dvn * lg
        dgv = rs * (dxh - jnp.mean(dxh, axis=-1, keepdims=True) - xh * jnp.mean(dxh * xh, axis=-1, keepdims=True))
        duv_ref[:, gw:] = (dgv * _gelu_grad(v)).astype(BF16)

    return pl.pallas_call(
        body, name="gmlp_bwd", grid=(s // chunk,),
        in_specs=[_row_spec(chunk, gw), _row_spec(chunk, gw, 0), _row_spec(chunk, gw, 1), _vec_spec(gw), _vec_spec(gw),
                  pl.BlockSpec((groups, chunk, chunk), lambda i: (0, 0, 0)), pl.BlockSpec((chunk, LANES), lambda i: (0, 0))],
        out_specs=[_row_spec(chunk, 2 * gw), pl.BlockSpec((groups, chunk, chunk), lambda i: (0, 0, 0)),
                   pl.BlockSpec((chunk, LANES), lambda i: (0, 0)), _vec_spec(gw), _vec_spec(gw)],
        out_shape=[jax.ShapeDtypeStruct((s, 2 * gw), BF16), jax.ShapeDtypeStruct((groups, chunk, chunk), F32),
                   jax.ShapeDtypeStruct((chunk, LANES), F32), jax.ShapeDtypeStruct((1, gw), F32),
                   jax.ShapeDtypeStruct((1, gw), F32)],
        scratch_shapes=[pltpu.VMEM((chunk, gw), F32)],
        compiler_params=_params("arbitrary"),
    )(dya, z, z, ln_g, ln_b, ws, bs_t)


def _lower_bound(lb_ref):
    a0, a1 = lb_ref[0:1, :], lb_ref[1:2, :]
    mx = jnp.maximum(a0, a1)
    e0, e1 = jnp.exp(a0 - mx), jnp.exp(a1 - mx)
    return e0 / (e0 + e1)


def _hg_chunk(q, fp, lb, tri_incl):
    t = q.shape[0]
    sig = _sigmoid(fp)
    f = lb + (1.0 - lb) * sig
    k = 1.0 - f
    sq = _sigmoid(q)
    qa = q * sq
    b = _dot(tri_incl, jnp.log(f), NN, lax.Precision.HIGHEST)
    row = lax.broadcasted_iota(jnp.int32, b.shape, 0)
    b_last = _colsum(jnp.where(row == t - 1, b, 0.0))
    b_mid = _colsum(jnp.where(row == t // 2, b, 0.0))
    e_q = jnp.exp(jnp.minimum(b - b_mid, EXP_CLAMP))
    e_k = jnp.exp(jnp.minimum(b_mid - b, EXP_CLAMP))
    e_in = jnp.exp(b)
    e_out = jnp.exp(b_last - b)
    return dict(sig=sig, f=f, k=k, sq=sq, qa=qa, b_last=b_last, e_q=e_q, e_k=e_k, e_in=e_in, e_out=e_out,
                q_hat=(qa * e_q).astype(BF16), k_hat=(k * e_k).astype(BF16),
                q_in=(qa * e_in).astype(BF16), k_out=k * e_out)


def _hgrn_fwd(z, hg_lb, norm_g, q_col):
    s = z.shape[0]
    hw = norm_g.shape[1]
    heads = hw // LANES
    t = HG_CHUNK
    rows = min(HG_ROWS, s)
    per_step = rows // t
    per = hw // LANES

    def zspec(which):
        return pl.BlockSpec((rows, LANES), lambda h, r, which=which: (r, q_col + which * per + h))

    def body(q_ref, f_ref, i_ref, g_ref, lb_ref, ng_ref, yb_ref, o_ref, st_out_ref, st_ref):
        @pl.when(pl.program_id(1) == 0)
        def _():
            st_ref[...] = jnp.zeros_like(st_ref)

        lb = _lower_bound(lb_ref)
        ng = ng_ref[...]
        tri = _tril(t)
        tri_f = tri.astype(F32)
        for j in range(per_step):
            rs_ = slice(j * t, (j + 1) * t)
            ch = _hg_chunk(q_ref[rs_, :], f_ref[rs_, :], lb, tri_f)
            iv = i_ref[rs_, :].astype(BF16)
            st = st_ref[...]
            st_out_ref[j] = st
            attn = jnp.where(tri, _dot(ch["q_hat"], ch["k_hat"], NT), 0.0)
            o = _dot(ch["q_in"], st.astype(BF16), NT) + _dot(attn.astype(BF16), iv)
            st_ref[...] = st * jnp.exp(ch["b_last"]) + _dot(iv, ch["k_out"].astype(BF16), TN)
            o_ref[rs_, :] = o
            og = g_ref[rs_, :]
            on, _ = _rms(o)
            yb_ref[rs_, :] = (on * ng * (og * _sigmoid(og))).astype(BF16)

    out_row = pl.BlockSpec((rows, LANES), lambda h, r: (r, h))
    return pl.pallas_call(
        body, name="hgrn_fwd", grid=(heads, s // rows),
        in_specs=[zspec(0), zspec(1), zspec(2), zspec(3),
                  pl.BlockSpec((2, LANES), lambda h, r: (0, h)), pl.BlockSpec((1, LANES), lambda h, r: (0, h))],
        out_specs=[out_row, out_row, pl.BlockSpec((None, per_step, LANES, LANES), lambda h, r: (h, r, 0, 0))],
        out_shape=[jax.ShapeDtypeStruct((s, hw), BF16), jax.ShapeDtypeStruct((s, hw), F32),
                   jax.ShapeDtypeStruct((heads, s // t, LANES, LANES), F32)],
        scratch_shapes=[pltpu.VMEM((LANES, LANES), F32)],
        compiler_params=_params("parallel", "arbitrary"),
    )(z, z, z, z, hg_lb, norm_g)


def _hgrn_bwd(dyb, z, o_raw, states, hg_lb, norm_g, q_col):
    s = z.shape[0]
    hw = norm_g.shape[1]
    heads = hw // LANES
    t = HG_CHUNK
    rows = min(HG_ROWS, s)
    per_step = rows // t
    per = hw // LANES
    n_steps = s // rows

    def zspec(which):
        return pl.BlockSpec((rows, LANES), lambda h, r, which=which: (n_steps - 1 - r, q_col + which * per + h))

    def body(dyb_ref, q_ref, f_ref, i_ref, g_ref, o_ref, st_in_ref, lb_ref, ng_ref,
             dq_ref, df_ref, di_ref, dg_ref, dlb_ref, dng_ref, dst_ref, acc_lb_ref, acc_ng_ref):
        step = pl.program_id(1)

        @pl.when(step == 0)
        def _():
            dst_ref[...] = jnp.zeros_like(dst_ref)
            acc_lb_ref[...] = jnp.zeros_like(acc_lb_ref)
            acc_ng_ref[...] = jnp.zeros_like(acc_ng_ref)

        lb = _lower_bound(lb_ref)
        ng = ng_ref[...]
        tri = _tril(t)
        tri_f = tri.astype(F32)
        tri_rev = (lax.broadcasted_iota(jnp.int32, (t, t), 0) <= lax.broadcasted_iota(jnp.int32, (t, t), 1)).astype(F32)
        last_row = lax.broadcasted_iota(jnp.int32, (t, LANES), 0) == t - 1
        for j in reversed(range(per_step)):
            rs_ = slice(j * t, (j + 1) * t)
            q = q_ref[rs_, :]
            ch = _hg_chunk(q, f_ref[rs_, :], lb, tri_f)
            ivf = i_ref[rs_, :]
            iv = ivf.astype(BF16)
            o, og, dy = o_ref[rs_, :], g_ref[rs_, :], dyb_ref[rs_, :]
            so = _sigmoid(og)
            on, r = _rms(o)
            acc_ng_ref[...] += _colsum(dy * on * (og * so))
            dg_ref[rs_, :] = (dy * on * ng * so * (1.0 + og * (1.0 - so))).astype(BF16)
            don = dy * ng * (og * so)
            do = (r * (don - on * jnp.mean(don * on, axis=-1, keepdims=True))).astype(BF16)
            st_prev = st_in_ref[j]
            dst = dst_ref[...]
            dst_b = dst.astype(BF16)
            attn = jnp.where(tri, _dot(ch["q_hat"], ch["k_hat"], NT), 0.0).astype(BF16)
            d_attn = jnp.where(tri, _dot(do, iv, NT), 0.0).astype(BF16)
            k_out = ch["k_out"]
            dq_in = _dot(do, st_prev.astype(BF16))
            dk_out = _dot(iv, dst_b)
            di_ref[rs_, :] = (_dot(attn, do, TN) + _dot(k_out.astype(BF16), dst_b, NT)).astype(BF16)
            dqa = dq_in * ch["e_in"] + _dot(d_attn, ch["k_hat"]) * ch["e_q"]
            dk = dk_out * ch["e_out"] + _dot(d_attn, ch["q_hat"], TN) * ch["e_k"]
            db = ch["qa"] * dqa - ch["k"] * dk
            e_last = jnp.exp(ch["b_last"])
            extra = _colsum(dk_out * k_out) + e_last * _colsum(st_prev * dst)
            db = jnp.where(last_row, db + extra, db)
            dlf = _dot(tri_rev, db, NN, lax.Precision.HIGHEST)
            dfv = dlf / ch["f"] - dk
            sig = ch["sig"]
            df_ref[rs_, :] = (dfv * (1.0 - lb) * sig * (1.0 - sig)).astype(BF16)
            acc_lb_ref[...] += _colsum(dfv * (1.0 - sig))
            sq = ch["sq"]
            dq_ref[rs_, :] = (dqa * sq * (1.0 + q * (1.0 - sq))).astype(BF16)
            dst_ref[...] = dst * e_last + _dot(do, ch["q_in"], TN)

        @pl.when(step == n_steps - 1)
        def _():
            d0 = acc_lb_ref[...] * lb * (1.0 - lb)
            dlb_ref[0:1, :] = d0
            dlb_ref[1:2, :] = -d0
            dng_ref[...] = acc_ng_ref[...]

    rev_row = pl.BlockSpec((rows, LANES), lambda h, r: (n_steps - 1 - r, h))
    piece = jax.ShapeDtypeStruct((s, hw), BF16)
    return pl.pallas_call(
        body, name="hgrn_bwd", grid=(heads, n_steps),
        in_specs=[rev_row, zspec(0), zspec(1), zspec(2), zspec(3), rev_row,
                  pl.BlockSpec((None, per_step, LANES, LANES), lambda h, r: (h, n_steps - 1 - r, 0, 0)),
                  pl.BlockSpec((2, LANES), lambda h, r: (0, h)), pl.BlockSpec((1, LANES), lambda h, r: (0, h))],
        out_specs=[rev_row, rev_row, rev_row, rev_row,
                   pl.BlockSpec((2, LANES), lambda h, r: (0, h)), pl.BlockSpec((1, LANES), lambda h, r: (0, h))],
        out_shape=[piece, piece, piece, piece, jax.ShapeDtypeStruct((2, hw), F32), jax.ShapeDtypeStruct((1, hw), F32)],
        scratch_shapes=[pltpu.VMEM((LANES, LANES), F32), pltpu.VMEM((1, LANES), F32), pltpu.VMEM((1, LANES), F32)],
        compiler_params=_params("parallel", "arbitrary"),
    )(dyb, z, z, z, z, o_raw, states, hg_lb, norm_g)


def _adamw(w, m, v, parts, name):
    rows, cols = w.shape
    n_parts = parts.shape[0]
    bc = _pick(cols, 1024)
    rb = _pick(rows, 256) if rows % LANES == 0 else rows
    m_corr = 1.0 - ADAM_B1 ** ADAM_STEP
    v_corr = 1.0 - ADAM_B2 ** ADAM_STEP

    def body(w_ref, m_ref, v_ref, p_ref, g_ref, d_ref, mo_ref, vo_ref):
        g = p_ref[0].astype(F32)
        for p in range(1, n_parts):
            g = g + p_ref[p].astype(F32)
        m2 = ADAM_B1 * m_ref[...] + (1.0 - ADAM_B1) * g
        v2 = ADAM_B2 * v_ref[...] + (1.0 - ADAM_B2) * (g * g)
        g_ref[...] = g
        mo_ref[...] = m2
        vo_ref[...] = v2
        d_ref[...] = -ADAM_LR * ((m2 / m_corr) / (jnp.sqrt(v2 / v_corr) + ADAM_EPS) + ADAM_WD * w_ref[...])

    blk = pl.BlockSpec((rb, bc), lambda i, j: (i, j))
    out = jax.ShapeDtypeStruct((rows, cols), F32)
    return pl.pallas_call(
        body, name=name, grid=(rows // rb, cols // bc),
        in_specs=[blk, blk, blk, pl.BlockSpec((n_parts, rb, bc), lambda i, j: (0, i, j))],
        out_specs=[blk] * 4, out_shape=[out] * 4, compiler_params=_params("parallel", "parallel"),
    )(w, m, v, parts)


SMALL = ("b_ada", "norm1_g", "b_gate", "gmlp_ln_g", "gmlp_ln_b", "gmlp_ws", "gmlp_bs", "hg_lb", "hg_norm_g",
         "norm2_g", "final_norm_g")
BIG = ("w_in", "w_branch_gmlp", "w_branch_hg", "w_out", "w_ffn_in", "w_ffn_out")
WEIGHTS = ("w_ada", "b_ada", "norm1_g", "w_in", "b_gate", "gmlp_ln_g", "gmlp_ln_b", "gmlp_ws", "gmlp_bs", "hg_lb",
           "hg_norm_g", "w_branch_gmlp", "w_branch_hg", "w_out", "norm2_g", "w_ffn_in", "w_ffn_out", "final_norm_g")


def _pack(parts):
    return jnp.concatenate([p.reshape(-1, LANES) for p in parts], axis=0)


def _step(x, c, loss_target, w, m, v):
    s, d = x.shape[1], x.shape[2]
    gw = w["gmlp_ln_g"].shape[-1]
    hw = w["hg_norm_g"].shape[-1]
    x2d, tgt = x[0], loss_target[0]
    mx, my, mc = lax.axis_index("x"), lax.axis_index("y"), lax.axis_index("c")
    chip = 2 * mx + my
    dev = 2 * chip + mc
    q_col = 2 * gw // LANES
    gate_col = (2 * gw + 4 * hw) // d

    c_all = _all_gather8(c.reshape(-1, LANES), "gather_c").reshape(N_DEV, d)
    n_ada = w["w_ada"].shape[-1]
    b_ada_q = lax.dynamic_slice(w["b_ada"], (0, chip * n_ada), (1, n_ada))
    mod_q = _ada_fwd(c_all, w["w_ada"][0], b_ada_q)
    mod_all = _all_gather8(mod_q, "gather_mod")
    mod = lax.dynamic_index_in_dim(mod_all, dev, axis=1, keepdims=False)[::2].reshape(1, 6 * d)
    sh1, sc1, gt1, sh2, sc2, gt2 = [mod[:, i * d:(i + 1) * d] for i in range(6)]

    quarters = [w[n][0].astype(BF16) for n in BIG]
    w_in, w_bg, w_bh, w_out, w_fi, w_fo = _gather_weights(quarters, "gather_weights")
    w_out = w_out.reshape(-1, w_out.shape[-1])
    w_fo = w_fo.reshape(-1, w_fo.shape[-1])

    norm1_g, norm2_g, final_g = w["norm1_g"], w["norm2_g"], w["final_norm_g"].reshape(1, d)
    ln_g, ln_b = w["gmlp_ln_g"], w["gmlp_ln_b"]
    ws = w["gmlp_ws"][0]
    groups = ws.shape[0]
    bs_t = jnp.pad(w["gmlp_bs"][0].T, ((0, 0), (0, LANES - groups)))
    hg_lb, hg_ng, b_gate = w["hg_lb"], w["hg_norm_g"], w["b_gate"]

    h1 = _norm_mod_fwd(x2d, norm1_g, sc1, sh1, "norm1_fwd")
    z = _matmul(h1, w_in, mode="nn", name="mm_z", out_dtype=F32, b_slots=True, bn=1280)
    ya = _gmlp_fwd(z, ln_g, ln_b, ws, bs_t)
    yb, o_raw, states = _hgrn_fwd(z, hg_lb, hg_ng, q_col)
    pa = _matmul(ya, w_bg, mode="nn", name="mm_pa", out_dtype=F32, b_slots=True)
    pb = _matmul(yb, w_bh, mode="nn", name="mm_pb", out_dtype=F32, b_slots=True)
    y = _gate_fwd(pa, pb, z, b_gate, gate_col)
    yo = _matmul(y, w_out, mode="nn", name="mm_yo", out_dtype=F32)
    x1, h2 = _norm_mod_fwd(x2d, norm2_g, sc2, sh2, "norm2_fwd", res=yo, gt=gt1)
    aup = _matmul(h2, w_fi, mode="nn", name="mm_aup", out_dtype=F32, b_slots=True, bn=1408)
    hf = _swiglu_fwd(aup)
    ffn = _matmul(hf, w_fo, mode="nn", name="mm_ffn", out_dtype=F32)
    dx2, dffn, loss_row, d_final_g, d_gt2 = _final_loss(x1, ffn, gt2, final_g, tgt)

    g_fo = _matmul(hf, dffn, mode="tn", name="mm_g_fo", out_dtype=BF16, bm=1408)
    dhf = _matmul(dffn, w_fo, mode="nt", name="mm_dhf", out_dtype=F32, bn=1408)
    daup = _swiglu_bwd(dhf, aup)
    g_fi = _matmul(h2, daup, mode="tn", name="mm_g_fi", out_dtype=BF16, out_slots=True, bn=1408)
    dh2 = _matmul(daup, w_fi, mode="nt", name="mm_dh2", out_dtype=F32, b_slots=True, bk=1408)
    dx1, dyo, d_sh2, d_sc2, d_norm2, d_gt1 = _norm_mod_bwd(dh2, x1, dx2, norm2_g, sc2, "norm2_bwd", branch=yo, gt=gt1)
    g_out = _matmul(y, dyo, mode="tn", name="mm_g_out", out_dtype=BF16)
    dy = _matmul(dyo, w_out, mode="nt", name="mm_dy", out_dtype=F32)
    dpa, dpb, dz_gate, d_b_gate = _gate_bwd(dy, pa, pb, z, b_gate, gate_col)
    g_bg = _matmul(ya, dpa, mode="tn", name="mm_g_bg", out_dtype=BF16, out_slots=True)
    g_bh = _matmul(yb, dpb, mode="tn", name="mm_g_bh", out_dtype=BF16, out_slots=True)
    dya = _matmul(dpa, w_bg, mode="nt", name="mm_dya", out_dtype=F32, b_slots=True)
    dyb = _matmul(dpb, w_bh, mode="nt", name="mm_dyb", out_dtype=F32, b_slots=True)
    dz_uv, d_ws, d_bs_t, d_ln_g, d_ln_b = _gmlp_bwd(dya, z, ln_g, ln_b, ws, bs_t)
    dz_q, dz_f, dz_i, dz_g, d_hg_lb, d_hg_ng = _hgrn_bwd(dyb, z, o_raw, states, hg_lb, hg_ng, q_col)
    dz = jnp.concatenate([dz_uv, dz_q, dz_f, dz_i, dz_g, dz_gate], axis=1)
    g_in = _matmul(h1, dz, mode="tn", name="mm_g_in", out_dtype=BF16, out_slots=True, bn=1280)
    dh1 = _matmul(dz, w_in, mode="nt", name="mm_dh1", out_dtype=F32, b_slots=True)
    grad_x, d_sh1, d_sc1, d_norm1 = _norm_mod_bwd(dh1, x2d, dx1, norm1_g, sc1, "norm1_bwd")

    d_mod = jnp.concatenate([d_sh1, d_sc1, d_gt1, d_sh2, d_sc2, d_gt2], axis=1)
    small_part = {"b_ada": d_mod, "norm1_g": d_norm1, "b_gate": d_b_gate, "gmlp_ln_g": d_ln_g, "gmlp_ln_b": d_ln_b,
                  "gmlp_ws": d_ws, "gmlp_bs": d_bs_t[:, :groups].T, "hg_lb": d_hg_lb, "hg_norm_g": d_hg_ng,
                  "norm2_g": d_norm2, "final_norm_g": d_final_g}
    small_all = _all_gather8(_pack([small_part[n] for n in SMALL]), "gather_small")
    d_mod_all = small_all[:, :6 * d // LANES].reshape(N_DEV, 6 * d)
    d_mod_q = lax.dynamic_slice(d_mod_all, (0, chip * n_ada), (N_DEV, n_ada))
    g_ada = _ada_bwd(c_all, d_mod_q)

    g_fo = g_fo.reshape(N_CHIPS, -1, g_fo.shape[-1])
    g_out = g_out.reshape(N_CHIPS, -1, g_out.shape[-1])
    parts = _gather_partials([g_in, g_bg, g_bh, g_out, g_fi, g_fo], "gather_partials")

    grad, delta, new_m, new_v = {}, {}, {}, {}
    for n, p in zip(("w_ada",) + BIG, [g_ada] + list(parts)):
        outs = _adamw(w[n][0], m[n][0], v[n][0], p, "adamw_" + n)
        grad[n], delta[n], new_m[n], new_v[n] = [o[None] for o in outs]
    outs = _adamw(_pack([w[n] for n in SMALL]), _pack([m[n] for n in SMALL]), _pack([v[n] for n in SMALL]),
                  small_all, "adamw_small")
    row = 0
    for n in SMALL:
        cnt = w[n].size // LANES
        for dst, o in zip((grad, delta, new_m, new_v), outs):
            dst[n] = o[row:row + cnt].reshape(w[n].shape)
        row += cnt

    loss = lax.psum(loss_row[0, 0], ("x", "y", "c"))
    return (loss, grad_x[None], *[grad[n] for n in WEIGHTS], *[delta[n] for n in WEIGHTS],
            *[new_m[n] for n in WEIGHTS], *[new_v[n] for n in WEIGHTS])


def kernel(x, c, w_ada, b_ada, norm1_g, w_in, b_gate, gmlp_ln_g, gmlp_ln_b, gmlp_ws, gmlp_bs, hg_lb, hg_norm_g, w_branch_gmlp, w_branch_hg, w_out, norm2_g, w_ffn_in, w_ffn_out, final_norm_g, loss_target, m_w_ada, m_b_ada, m_norm1_g, m_w_in, m_b_gate, m_gmlp_ln_g, m_gmlp_ln_b, m_gmlp_ws, m_gmlp_bs, m_hg_lb, m_hg_norm_g, m_w_branch_gmlp, m_w_branch_hg, m_w_out, m_norm2_g, m_w_ffn_in, m_w_ffn_out, m_final_norm_g, v_w_ada, v_b_ada, v_norm1_g, v_w_in, v_b_gate, v_gmlp_ln_g, v_gmlp_ln_b, v_gmlp_ws, v_gmlp_bs, v_hg_lb, v_hg_norm_g, v_w_branch_gmlp, v_w_branch_hg, v_w_out, v_norm2_g, v_w_ffn_in, v_w_ffn_out, v_final_norm_g):
    w = dict(w_ada=w_ada, b_ada=b_ada, norm1_g=norm1_g, w_in=w_in, b_gate=b_gate, gmlp_ln_g=gmlp_ln_g,
             gmlp_ln_b=gmlp_ln_b, gmlp_ws=gmlp_ws, gmlp_bs=gmlp_bs, hg_lb=hg_lb, hg_norm_g=hg_norm_g,
             w_branch_gmlp=w_branch_gmlp, w_branch_hg=w_branch_hg, w_out=w_out, norm2_g=norm2_g,
             w_ffn_in=w_ffn_in, w_ffn_out=w_ffn_out, final_norm_g=final_norm_g)
    m = dict(w_ada=m_w_ada, b_ada=m_b_ada, norm1_g=m_norm1_g, w_in=m_w_in, b_gate=m_b_gate, gmlp_ln_g=m_gmlp_ln_g,
             gmlp_ln_b=m_gmlp_ln_b, gmlp_ws=m_gmlp_ws, gmlp_bs=m_gmlp_bs, hg_lb=m_hg_lb, hg_norm_g=m_hg_norm_g,
             w_branch_gmlp=m_w_branch_gmlp, w_branch_hg=m_w_branch_hg, w_out=m_w_out, norm2_g=m_norm2_g,
             w_ffn_in=m_w_ffn_in, w_ffn_out=m_w_ffn_out, final_norm_g=m_final_norm_g)
    v = dict(w_ada=v_w_ada, b_ada=v_b_ada, norm1_g=v_norm1_g, w_in=v_w_in, b_gate=v_b_gate, gmlp_ln_g=v_gmlp_ln_g,
             gmlp_ln_b=v_gmlp_ln_b, gmlp_ws=v_gmlp_ws, gmlp_bs=v_gmlp_bs, hg_lb=v_hg_lb, hg_norm_g=v_hg_norm_g,
             w_branch_gmlp=v_w_branch_gmlp, w_branch_hg=v_w_branch_hg, w_out=v_w_out, norm2_g=v_norm2_g,
             w_ffn_in=v_w_ffn_in, w_ffn_out=v_w_ffn_out, final_norm_g=v_final_norm_g)
    return _step(x, c, loss_target, w, m, v)
```

```python
import functools

import jax
import jax.numpy as jnp
from jax import lax
from jax.experimental import pallas as pl
from jax.experimental.pallas import tpu as pltpu

F32 = jnp.float32
BF16 = jnp.bfloat16
EPS = 1e-6
LANES = 128
N_CHIPS = 4
N_DEV = 8
VMEM_LIMIT_BYTES = 56 * 1024 * 1024
HG_CHUNK = 32
HG_ROWS = 256
EXP_CLAMP = 80.0
ADAM_LR, ADAM_B1, ADAM_B2, ADAM_EPS, ADAM_WD, ADAM_STEP = 0.001, 0.9, 0.999, 1e-08, 0.01, 10
MESH = pl.DeviceIdType.MESH

NN = (((1,), (0,)), ((), ()))
NT = (((1,), (1,)), ((), ()))
TN = (((0,), (0,)), ((), ()))


def _dot(a, b, dims=NN, precision=None):
    return lax.dot_general(a, b, dims, precision=precision, preferred_element_type=F32)


def _params(*semantics):
    return pltpu.CompilerParams(dimension_semantics=semantics, vmem_limit_bytes=VMEM_LIMIT_BYTES)


class _Order:
    last = None


def _pcall(body, *, in_specs, out_specs, grid=(), scratch_shapes=(), num_scalar_prefetch=0, **kw):
    def run(*ins):
        deps = () if _Order.last is None else (_Order.last,)
        n_in, n_dep = len(ins), len(deps)

        def wrapped(*refs):
            body(*refs[:n_in], *refs[n_in + n_dep:])

        specs = list(in_specs) + [pl.BlockSpec(memory_space=pl.ANY)] * n_dep
        if num_scalar_prefetch:
            grid_spec = pltpu.PrefetchScalarGridSpec(
                num_scalar_prefetch=num_scalar_prefetch, grid=grid, in_specs=specs, out_specs=out_specs,
                scratch_shapes=scratch_shapes)
            outs = pl.pallas_call(wrapped, grid_spec=grid_spec, **kw)(*ins, *deps)
        else:
            outs = pl.pallas_call(wrapped, grid=grid, in_specs=specs, out_specs=out_specs,
                                  scratch_shapes=scratch_shapes, **kw)(*ins, *deps)
        _Order.last = jax.tree.leaves(outs)[0]
        return outs

    return run


def _pick_rows(dim, pref):
    best = None
    for cand in range(16, min(dim, pref) + 1, 16):
        if dim % cand == 0:
            best = cand
    assert best is not None, (dim, pref)
    return best


def _pick(dim, pref):
    if dim <= pref:
        return dim
    best = None
    for cand in range(LANES, pref + 1, LANES):
        if dim % cand == 0:
            best = cand
    assert best is not None, (dim, pref)
    return best


def _sigmoid(x):
    return 1.0 / (1.0 + jnp.exp(-x))


def _gelu(x):
    c = 0.7978845608028654
    return 0.5 * x * (1.0 + jnp.tanh(c * (x + 0.044715 * x * x * x)))


def _gelu_grad(x):
    c = 0.7978845608028654
    t = jnp.tanh(c * (x + 0.044715 * x * x * x))
    return 0.5 * (1.0 + t) + 0.5 * x * (1.0 - t * t) * c * (1.0 + 3.0 * 0.044715 * x * x)


def _rms(x):
    r = lax.rsqrt(jnp.mean(x * x, axis=-1, keepdims=True) + EPS)
    return x * r, r


def _colsum(x):
    return jnp.sum(x, axis=0, keepdims=True)


def _accumulate(first, ref, val):
    @pl.when(first)
    def _():
        ref[...] = val

    @pl.when(jnp.logical_not(first))
    def _():
        ref[...] += val


def _matmul(a, b, *, mode, name, out_dtype, b_slots=False, out_slots=False, bm=1024, bn=1024, bk=512):
    if mode == "nn":
        m, k = a.shape
        n = b.shape[2] * N_CHIPS if b_slots else b.shape[1]
        per = b.shape[2] if b_slots else n
    elif mode == "nt":
        m, k = a.shape
        n = b.shape[1] if b_slots else b.shape[0]
        per = b.shape[2] if b_slots else k
    else:
        k, m = a.shape
        n = b.shape[1]
        per = n // N_CHIPS if out_slots else n
    bm = _pick(m, bm)
    if mode == "nt":
        bn, bk = _pick(n, bn), _pick(per, bk)
    else:
        bn, bk = _pick(per, bn), _pick(k, bk)
    nk = k // bk
    per_blocks = per // (bk if mode == "nt" else bn)
    dims = {"nn": NN, "nt": NT, "tn": TN}[mode]

    def body(a_ref, b_ref, o_ref, acc_ref):
        kk = pl.program_id(2)

        @pl.when(kk == 0)
        def _():
            acc_ref[...] = jnp.zeros_like(acc_ref)

        acc_ref[...] += _dot(a_ref[...], b_ref[...], dims)

        @pl.when(kk == nk - 1)
        def _():
            o_ref[...] = acc_ref[...].astype(o_ref.dtype)

    if mode == "nn":
        a_spec = pl.BlockSpec((bm, bk), lambda i, j, kk: (i, kk))
        if b_slots:
            b_spec = pl.BlockSpec((None, bk, bn), lambda i, j, kk: (j // per_blocks, kk, j % per_blocks))
        else:
            b_spec = pl.BlockSpec((bk, bn), lambda i, j, kk: (kk, j))
    elif mode == "nt":
        a_spec = pl.BlockSpec((bm, bk), lambda i, j, kk: (i, kk))
        if b_slots:
            b_spec = pl.BlockSpec((None, bn, bk), lambda i, j, kk: (kk // per_blocks, j, kk % per_blocks))
        else:
            b_spec = pl.BlockSpec((bn, bk), lambda i, j, kk: (j, kk))
    else:
        a_spec = pl.BlockSpec((bk, bm), lambda i, j, kk: (kk, i))
        b_spec = pl.BlockSpec((bk, bn), lambda i, j, kk: (kk, j))
    if out_slots:
        o_spec = pl.BlockSpec((None, bm, bn), lambda i, j, kk: (j // per_blocks, i, j % per_blocks))
        out_shape = jax.ShapeDtypeStruct((N_CHIPS, m, per), out_dtype)
    else:
        o_spec = pl.BlockSpec((bm, bn), lambda i, j, kk: (i, j))
        out_shape = jax.ShapeDtypeStruct((m, n), out_dtype)
    return _pcall(
        body, name=name, grid=(m // bm, n // bn, nk), in_specs=[a_spec, b_spec], out_specs=o_spec,
        out_shape=out_shape, scratch_shapes=[pltpu.VMEM((bm, bn), F32)],
        compiler_params=_params("parallel", "parallel", "arbitrary"),
    )(a, b)


def _place():
    x, y, c = lax.axis_index("x"), lax.axis_index("y"), lax.axis_index("c")
    chips = [(1 - x, y), (x, 1 - y), (1 - x, 1 - y)]
    return x, y, c, chips


def _all_gather8(block, name):
    def body(x_ref, out_ref, send_sems, recv_sems, local_sem):
        x, y, c, chips = _place()
        me, sibling = (x, y, c), (x, y, 1 - c)

        def slot(px, py, pc):
            return out_ref.at[4 * px + 2 * py + pc]

        def copy(k, blk, to, src=None):
            return pltpu.make_async_remote_copy(
                src_ref=slot(*blk) if src is None else src, dst_ref=slot(*blk),
                send_sem=send_sems.at[k], recv_sem=recv_sems.at[k], device_id=to, device_id_type=MESH)

        mine = pltpu.make_async_copy(x_ref, slot(*me), local_sem)
        mine.start()
        first = [copy(0, me, sibling, src=x_ref)]
        first += [copy(1 + j, me, (*chip, c), src=x_ref) for j, chip in enumerate(chips)]
        for cp in first:
            cp.start()
        passed = [copy(4 + j, (*chip, c), sibling) for j, chip in enumerate(chips)]
        for j, chip in enumerate(chips):
            copy(1 + j, (*chip, c), me).wait_recv()
            passed[j].start()
        copy(0, sibling, me).wait_recv()
        for j, chip in enumerate(chips):
            copy(4 + j, (*chip, 1 - c), me).wait_recv()
        for cp in first + passed:
            cp.wait_send()
        mine.wait()

    return _pcall(
        body, name=name, out_shape=jax.ShapeDtypeStruct((N_DEV,) + block.shape, block.dtype),
        in_specs=[pl.BlockSpec(memory_space=pltpu.VMEM)], out_specs=pl.BlockSpec(memory_space=pltpu.VMEM),
        scratch_shapes=[pltpu.SemaphoreType.DMA((7,)), pltpu.SemaphoreType.DMA((7,)), pltpu.SemaphoreType.DMA],
        compiler_params=pltpu.CompilerParams(vmem_limit_bytes=VMEM_LIMIT_BYTES),
    )(block)


HBM_SPEC = pl.BlockSpec(memory_space=pltpu.HBM)
SEM_SPEC = pl.BlockSpec(memory_space=pltpu.SEMAPHORE)
ANY_SPEC = pl.BlockSpec(memory_space=pl.ANY)
EFFECT = pltpu.SideEffectType.DATAFLOW_SIDE_EFFECTING


def _xfer_start(name, bufs, plan, n_copies):
    nb = len(bufs)
    deps = () if _Order.last is None else (_Order.last,)
    nd = len(deps)

    def body(*refs):
        send_sems, recv_sems = refs[nb + nd], refs[nb + nd + 1]
        token = refs[nb + nd + 2 + nb]
        for k, (src, dst, dev) in enumerate(plan(refs[:nb], *_place())):
            pltpu.make_async_remote_copy(src_ref=src, dst_ref=dst, send_sem=send_sems.at[k], recv_sem=recv_sems.at[k],
                                         device_id=dev, device_id_type=MESH).start()
        token[...] = jnp.zeros_like(token)

    outs = pl.pallas_call(
        body, name=name,
        out_shape=(pltpu.SemaphoreType.DMA((n_copies,)), pltpu.SemaphoreType.DMA((n_copies,)),
                   *[pltpu.HBM(b.shape, b.dtype) for b in bufs], jax.ShapeDtypeStruct((8, LANES), F32)),
        in_specs=[HBM_SPEC] * nb + [ANY_SPEC] * nd,
        out_specs=(SEM_SPEC, SEM_SPEC, *[HBM_SPEC] * nb, pl.BlockSpec(memory_space=pltpu.VMEM)),
        input_output_aliases={i: 2 + i for i in range(nb)},
        compiler_params=pltpu.CompilerParams(has_side_effects=EFFECT),
    )(*[pltpu.with_memory_space_constraint(b, pltpu.HBM) for b in bufs], *deps)
    _Order.last = outs[-1]
    return (outs[0], outs[1]), list(outs[2:2 + nb])


def _xfer_wait(name, sems, bufs, plan):
    nb = len(bufs)

    def body(*refs):
        send_sems, recv_sems = refs[nb], refs[nb + 1]
        for k, (src, dst, dev) in enumerate(plan(refs[:nb], *_place())):
            copy = pltpu.make_async_remote_copy(src_ref=src, dst_ref=dst, send_sem=send_sems.at[k],
                                                recv_sem=recv_sems.at[k], device_id=dev, device_id_type=MESH)
            copy.wait_send()
            copy.wait_recv()

    outs = pl.pallas_call(
        body, name=name, out_shape=tuple(pltpu.HBM(b.shape, b.dtype) for b in bufs),
        in_specs=[HBM_SPEC] * nb + [SEM_SPEC, SEM_SPEC, ANY_SPEC], out_specs=tuple([HBM_SPEC] * nb),
        input_output_aliases={i: i for i in range(nb)},
        compiler_params=pltpu.CompilerParams(has_side_effects=EFFECT),
    )(*bufs, *sems, _Order.last)
    _Order.last = outs[0]
    return list(outs)


def _half(ref, c, axis):
    rows = ref.shape[axis] // 2
    return pl.ds(c * rows, rows)


def _plan_weights_ici(n):
    def plan(refs, x, y, c, chips):
        out = []
        for w in range(n):
            rows = _half(refs[w], c, 0)
            out += [(refs[w].at[rows, :], refs[n + w].at[2 * x + y, rows, :], (*chip, c)) for chip in chips]
        return out
    return plan


def _plan_weights_d2d(n):
    def plan(refs, x, y, c, chips):
        out = []
        for w in range(n):
            rows = _half(refs[w], c, 1)
            for chip in chips:
                region = refs[w].at[2 * chip[0] + chip[1], rows, :]
                out.append((region, region, (x, y, 1 - c)))
        return out
    return plan


def _plan_grads_d2d(n):
    def plan(refs, x, y, c, chips):
        return [(refs[w].at[:, _half(refs[w], 1 - c, 1), :], refs[n + w], (x, y, 1 - c)) for w in range(n)]
    return plan


def _plan_grads_ici(n):
    def plan(refs, x, y, c, chips):
        out = []
        for w in range(n):
            out += [(refs[w].at[2 * chip[0] + chip[1]], refs[n + w].at[2 * x + y], (*chip, c)) for chip in chips]
        return out
    return plan


def _plan_final_d2d(n):
    def plan(refs, x, y, c, chips):
        out = []
        for w in range(n):
            region = refs[w].at[_half(refs[w], c, 0), :]
            out.append((region, region, (x, y, 1 - c)))
        return out
    return plan


def _place_quarters(quarters, name):
    n = len(quarters)

    def body(*refs):
        chip = 2 * lax.axis_index("x") + lax.axis_index("y")
        copies = [pltpu.make_async_copy(refs[w], refs[n + w].at[chip], refs[2 * n].at[w]) for w in range(n)]
        for cp in copies:
            cp.start()
        for cp in copies:
            cp.wait()

    return _pcall(
        body, name=name, out_shape=[jax.ShapeDtypeStruct((N_CHIPS,) + q.shape, q.dtype) for q in quarters],
        in_specs=[ANY_SPEC] * n, out_specs=[ANY_SPEC] * n, scratch_shapes=[pltpu.SemaphoreType.DMA((n,))],
    )(*quarters)


def _pre_reduce(g, landed, place, name):
    _, rows, cols = g.shape
    hr = rows // 2
    rb, bc = _pick_rows(hr, 256), _pick(cols, 1024)
    nrb = hr // rb

    def body(place_ref, g_ref, l_ref, o_ref):
        o_ref[...] = (g_ref[...].astype(F32) + l_ref[...].astype(F32)).astype(o_ref.dtype)

    return _pcall(
        body, name=name, num_scalar_prefetch=1, grid=(N_CHIPS, nrb, cols // bc),
        in_specs=[pl.BlockSpec((None, rb, bc), lambda j, i, k, p: (j, p[1] * nrb + i, k)),
                  pl.BlockSpec((None, rb, bc), lambda j, i, k, p: (j, i, k))],
        out_specs=pl.BlockSpec((None, rb, bc), lambda j, i, k, p: (j, i, k)),
        out_shape=jax.ShapeDtypeStruct((N_CHIPS, hr, cols), g.dtype),
        compiler_params=_params("parallel", "parallel", "parallel"),
    )(place, g, landed)


def _sum_slots(mine, landed, place, name):
    _, hr, cols = mine.shape
    rb, bc = _pick_rows(hr, 256), _pick(cols, 1024)
    nrb = hr // rb

    def body(place_ref, m_ref, l_ref, o_ref):
        chip = place_ref[0]
        own = m_ref[...].astype(F32)
        total = jnp.where(chip == 0, own, l_ref[0].astype(F32))
        for j in range(1, N_CHIPS):
            total = total + jnp.where(chip == j, own, l_ref[j].astype(F32))
        o_ref[...] = total

    return _pcall(
        body, name=name, num_scalar_prefetch=1, grid=(nrb, cols // bc),
        in_specs=[pl.BlockSpec((None, rb, bc), lambda i, k, p: (p[0], i, k)),
                  pl.BlockSpec((N_CHIPS, rb, bc), lambda i, k, p: (0, i, k))],
        out_specs=pl.BlockSpec((rb, bc), lambda i, k, p: (p[1] * nrb + i, k)),
        out_shape=jax.ShapeDtypeStruct((2 * hr, cols), F32),
        compiler_params=_params("parallel", "parallel"),
    )(place, mine, landed)


class _WeightGather:
    def __init__(self, tag, quarters):
        self.tag, self.n = tag, len(quarters)
        zones = _place_quarters(quarters, "place_" + tag)
        self.plan = _plan_weights_ici(self.n)
        self.sems, self.bufs = _xfer_start("wici_start_" + tag, list(quarters) + list(zones), self.plan, 3 * self.n)

    def pass_on(self):
        bufs = _xfer_wait("wici_wait_" + self.tag, self.sems, self.bufs, self.plan)
        self.plan = _plan_weights_d2d(self.n)
        self.sems, self.bufs = _xfer_start("wd2d_start_" + self.tag, bufs[self.n:], self.plan, 3 * self.n)

    def done(self):
        return _xfer_wait("wd2d_wait_" + self.tag, self.sems, self.bufs, self.plan)


class _GradReduce:
    def __init__(self, tag, grads):
        self.tag, self.n = tag, len(grads)
        zones = [lax.empty((N_CHIPS, g.shape[1] // 2, g.shape[2]), g.dtype) for g in grads]
        self.plan = _plan_grads_d2d(self.n)
        self.sems, self.bufs = _xfer_start("gd2d_start_" + tag, list(grads) + zones, self.plan, self.n)

    def cross(self, place):
        n = self.n
        bufs = _xfer_wait("gd2d_wait_" + self.tag, self.sems, self.bufs, self.plan)
        halves = [_pre_reduce(bufs[w], bufs[n + w], place, f"pre_reduce_{self.tag}{w}") for w in range(n)]
        zones = [lax.empty(h.shape, h.dtype) for h in halves]
        self.plan = _plan_grads_ici(n)
        self.sems, self.bufs = _xfer_start("gici_start_" + self.tag, halves + zones, self.plan, 3 * n)

    def join(self, place):
        n = self.n
        bufs = _xfer_wait("gici_wait_" + self.tag, self.sems, self.bufs, self.plan)
        sums = [_sum_slots(bufs[w], bufs[n + w], place, f"sum_slots_{self.tag}{w}") for w in range(n)]
        self.plan = _plan_final_d2d(n)
        self.sems, self.bufs = _xfer_start("gfin_start_" + self.tag, sums, self.plan, n)

    def done(self):
        return _xfer_wait("gfin_wait_" + self.tag, self.sems, self.bufs, self.plan)


def _ada_fwd(c_all, w_q, b_q):
    d, n = w_q.shape
    bn = _pick(n, 512)

    def body(c_ref, w_ref, b_ref, o_ref):
        cv = c_ref[...]
        act = cv * _sigmoid(cv)
        o_ref[...] = _dot(act, w_ref[...], NN, lax.Precision.HIGHEST) + b_ref[...]

    return _pcall(
        body, name="ada_fwd", grid=(n // bn,),
        in_specs=[pl.BlockSpec((N_DEV, d), lambda j: (0, 0)), pl.BlockSpec((d, bn), lambda j: (0, j)),
                  pl.BlockSpec((1, bn), lambda j: (0, j))],
        out_specs=pl.BlockSpec((N_DEV, bn), lambda j: (0, j)),
        out_shape=jax.ShapeDtypeStruct((N_DEV, n), F32), compiler_params=_params("parallel"),
    )(c_all, w_q, b_q)


def _ada_bwd(c_all, dmod_q):
    d = c_all.shape[1]
    n = dmod_q.shape[1]
    bn = _pick(n, 512)

    def body(c_ref, g_ref, o_ref):
        cv = c_ref[...]
        act = cv * _sigmoid(cv)
        o_ref[...] = _dot(act, g_ref[...], TN, lax.Precision.HIGHEST)

    return _pcall(
        body, name="ada_bwd", grid=(n // bn,),
        in_specs=[pl.BlockSpec((N_DEV, d), lambda j: (0, 0)), pl.BlockSpec((N_DEV, bn), lambda j: (0, j))],
        out_specs=pl.BlockSpec((None, d, bn), lambda j: (0, 0, j)),
        out_shape=jax.ShapeDtypeStruct((1, d, n), F32), compiler_params=_params("parallel"),
    )(c_all, dmod_q)


def _row_spec(rb, width, col=0):
    return pl.BlockSpec((rb, width), lambda i, col=col: (i, col))


def _vec_spec(width, col=0):
    return pl.BlockSpec((1, width), lambda i, col=col: (0, col))


def _norm_mod_fwd(x, g, sc, sh, name, res=None, gt=None):
    s, d = x.shape
    rb = _pick(s, 256)
    has_res = res is not None

    def body(*refs):
        if has_res:
            x_ref, res_ref, gt_ref, g_ref, sc_ref, sh_ref, x1_ref, h_ref = refs
            xv = x_ref[...] + gt_ref[...] * res_ref[...]
            x1_ref[...] = xv
        else:
            x_ref, g_ref, sc_ref, sh_ref, h_ref = refs
            xv = x_ref[...]
        xh, _ = _rms(xv)
        h_ref[...] = (xh * g_ref[...] * (1.0 + sc_ref[...]) + sh_ref[...]).astype(BF16)

    row, vec = _row_spec(rb, d), _vec_spec(d)
    if has_res:
        ins, in_specs = (x, res, gt, g, sc, sh), [row, row, vec, vec, vec, vec]
        out_shape = [jax.ShapeDtypeStruct((s, d), F32), jax.ShapeDtypeStruct((s, d), BF16)]
        out_specs = [row, row]
    else:
        ins, in_specs = (x, g, sc, sh), [row, vec, vec, vec]
        out_shape, out_specs = jax.ShapeDtypeStruct((s, d), BF16), row
    return _pcall(body, name=name, grid=(s // rb,), in_specs=in_specs, out_specs=out_specs,
                          out_shape=out_shape, compiler_params=_params("parallel"))(*ins)


def _final_loss(x1, f, gt2, g_final, target):
    s, d = x1.shape
    rb = _pick(s, 256)

    def body(x1_ref, f_ref, gt_ref, g_ref, t_ref, dx_ref, df_ref, loss_ref, dg_ref, dgt_ref):
        first = pl.program_id(0) == 0
        fv, gt, gv = f_ref[...], gt_ref[...], g_ref[...]
        x2 = x1_ref[...] + gt * fv
        xh, r = _rms(x2)
        err = xh * gv - t_ref[...]
        blk = 0.5 * jnp.sum(jnp.sum(err * err, axis=1, keepdims=True), axis=0, keepdims=True) / d
        dy = err / d
        dxh = dy * gv
        dx = r * (dxh - xh * jnp.mean(dxh * xh, axis=-1, keepdims=True))
        dx_ref[...] = dx
        df_ref[...] = (dx * gt).astype(BF16)
        _accumulate(first, loss_ref, jnp.broadcast_to(blk, (1, LANES)))
        _accumulate(first, dg_ref, _colsum(dy * xh))
        _accumulate(first, dgt_ref, _colsum(dx * fv))

    row, vec = _row_spec(rb, d), _vec_spec(d)
    return _pcall(
        body, name="final_loss", grid=(s // rb,), in_specs=[row, row, vec, vec, row],
        out_specs=[row, row, _vec_spec(LANES), vec, vec],
        out_shape=[jax.ShapeDtypeStruct((s, d), F32), jax.ShapeDtypeStruct((s, d), BF16),
                   jax.ShapeDtypeStruct((1, LANES), F32), jax.ShapeDtypeStruct((1, d), F32),
                   jax.ShapeDtypeStruct((1, d), F32)],
        compiler_params=_params("arbitrary"),
    )(x1, f, gt2, g_final, target)


def _norm_mod_bwd(dh, xin, dres, g, sc, name, branch=None, gt=None):
    s, d = xin.shape
    rb = _pick(s, 256)
    has_branch = branch is not None

    def body(*refs):
        if has_branch:
            dh_ref, x_ref, dres_ref, g_ref, sc_ref, br_ref, gt_ref, dx_ref, dbr_ref, dsh_ref, dsc_ref, dg_ref, dgt_ref = refs
        else:
            dh_ref, x_ref, dres_ref, g_ref, sc_ref, dx_ref, dsh_ref, dsc_ref, dg_ref = refs
        first = pl.program_id(0) == 0
        gv = g_ref[...]
        xh, r = _rms(x_ref[...])
        dhv = dh_ref[...]
        dn = dhv * (1.0 + sc_ref[...])
        dxh = dn * gv
        dx = dres_ref[...] + r * (dxh - xh * jnp.mean(dxh * xh, axis=-1, keepdims=True))
        dx_ref[...] = dx
        _accumulate(first, dsh_ref, _colsum(dhv))
        _accumulate(first, dsc_ref, _colsum(dhv * xh * gv))
        _accumulate(first, dg_ref, _colsum(dn * xh))
        if has_branch:
            dbr_ref[...] = (dx * gt_ref[...]).astype(BF16)
            _accumulate(first, dgt_ref, _colsum(dx * br_ref[...]))

    row, vec = _row_spec(rb, d), _vec_spec(d)
    vec_shape = jax.ShapeDtypeStruct((1, d), F32)
    if has_branch:
        ins, in_specs = (dh, xin, dres, g, sc, branch, gt), [row, row, row, vec, vec, row, vec]
        out_specs = [row, row, vec, vec, vec, vec]
        out_shape = [jax.ShapeDtypeStruct((s, d), F32), jax.ShapeDtypeStruct((s, d), BF16)] + [vec_shape] * 4
    else:
        ins, in_specs = (dh, xin, dres, g, sc), [row, row, row, vec, vec]
        out_specs = [row, vec, vec, vec]
        out_shape = [jax.ShapeDtypeStruct((s, d), F32)] + [vec_shape] * 3
    return _pcall(body, name=name, grid=(s // rb,), in_specs=in_specs, out_specs=out_specs,
                          out_shape=out_shape, compiler_params=_params("arbitrary"))(*ins)


def _gate_fwd(pa, pb, z, b_gate, gate_col):
    s, d = pa.shape
    rb = _pick(s, 256)

    def body(pa_ref, pb_ref, za_ref, zb_ref, ba_ref, bb_ref, y_ref):
        ga = _sigmoid(za_ref[...] + ba_ref[...])
        gb = _sigmoid(zb_ref[...] + bb_ref[...])
        y_ref[...] = (ga * pa_ref[...] + gb * pb_ref[...]).astype(BF16)

    row = _row_spec(rb, d)
    return _pcall(
        body, name="gate_fwd", grid=(s // rb,),
        in_specs=[row, row, _row_spec(rb, d, gate_col), _row_spec(rb, d, gate_col + 1), _vec_spec(d, 0), _vec_spec(d, 1)],
        out_specs=row, out_shape=jax.ShapeDtypeStruct((s, d), BF16), compiler_params=_params("parallel"),
    )(pa, pb, z, z, b_gate, b_gate)


def _gate_bwd(dy, pa, pb, z, b_gate, gate_col):
    s, d = pa.shape
    rb = _pick(s, 256)

    def body(dy_ref, pa_ref, pb_ref, za_ref, zb_ref, ba_ref, bb_ref, dpa_ref, dpb_ref, dz_ref, db_ref):
        first = pl.program_id(0) == 0
        dyv = dy_ref[...]
        ga = _sigmoid(za_ref[...] + ba_ref[...])
        gb = _sigmoid(zb_ref[...] + bb_ref[...])
        dpa_ref[...] = (dyv * ga).astype(BF16)
        dpb_ref[...] = (dyv * gb).astype(BF16)
        dga = dyv * pa_ref[...] * ga * (1.0 - ga)
        dgb = dyv * pb_ref[...] * gb * (1.0 - gb)
        dz_ref[:, :d] = dga.astype(BF16)
        dz_ref[:, d:] = dgb.astype(BF16)

        @pl.when(first)
        def _():
            db_ref[:, :d] = _colsum(dga)
            db_ref[:, d:] = _colsum(dgb)

        @pl.when(jnp.logical_not(first))
        def _():
            db_ref[:, :d] += _colsum(dga)
            db_ref[:, d:] += _colsum(dgb)

    row = _row_spec(rb, d)
    return _pcall(
        body, name="gate_bwd", grid=(s // rb,),
        in_specs=[row, row, row, _row_spec(rb, d, gate_col), _row_spec(rb, d, gate_col + 1), _vec_spec(d, 0), _vec_spec(d, 1)],
        out_specs=[row, row, _row_spec(rb, 2 * d), _vec_spec(2 * d)],
        out_shape=[jax.ShapeDtypeStruct((s, d), BF16), jax.ShapeDtypeStruct((s, d), BF16),
                   jax.ShapeDtypeStruct((s, 2 * d), BF16), jax.ShapeDtypeStruct((1, 2 * d), F32)],
        compiler_params=_params("arbitrary"),
    )(dy, pa, pb, z, z, b_gate, b_gate)


def _swiglu_fwd(aup):
    s, ff2 = aup.shape
    ff = ff2 // 2
    rb = _pick(s, 128)

    def body(a_ref, u_ref, o_ref):
        a = a_ref[...]
        o_ref[...] = (a * _sigmoid(a) * u_ref[...]).astype(BF16)

    return _pcall(
        body, name="swiglu_fwd", grid=(s // rb,), in_specs=[_row_spec(rb, ff, 0), _row_spec(rb, ff, 1)],
        out_specs=_row_spec(rb, ff), out_shape=jax.ShapeDtypeStruct((s, ff), BF16),
        compiler_params=_params("parallel"),
    )(aup, aup)


def _swiglu_bwd(dhf, aup):
    s, ff2 = aup.shape
    ff = ff2 // 2
    rb = _pick(s, 128)

    def body(d_ref, a_ref, u_ref, o_ref):
        a, dv = a_ref[...], d_ref[...]
        sa = _sigmoid(a)
        o_ref[:, :ff] = (dv * u_ref[...] * sa * (1.0 + a * (1.0 - sa))).astype(BF16)
        o_ref[:, ff:] = (dv * a * sa).astype(BF16)

    return _pcall(
        body, name="swiglu_bwd", grid=(s // rb,),
        in_specs=[_row_spec(rb, ff), _row_spec(rb, ff, 0), _row_spec(rb, ff, 1)],
        out_specs=_row_spec(rb, ff2), out_shape=jax.ShapeDtypeStruct((s, ff2), BF16),
        compiler_params=_params("parallel"),
    )(dhf, aup, aup)


def _tril(n):
    return lax.broadcasted_iota(jnp.int32, (n, n), 0) >= lax.broadcasted_iota(jnp.int32, (n, n), 1)


def _gmlp_norm(v, ln_g, ln_b):
    gv = _gelu(v)
    mu = jnp.mean(gv, axis=-1, keepdims=True)
    cen = gv - mu
    rs = lax.rsqrt(jnp.mean(cen * cen, axis=-1, keepdims=True) + EPS)
    xh = cen * rs
    return xh, rs, xh * ln_g + ln_b


def _gmlp_fwd(z, ln_g, ln_b, ws, bs_t):
    s = z.shape[0]
    gw = ln_g.shape[1]
    groups, chunk, _ = ws.shape

    def body(u_ref, v_ref, lg_ref, lb_ref, ws_ref, bs_ref, ya_ref):
        gu = _gelu(u_ref[...])
        _, _, vn = _gmlp_norm(v_ref[...], lg_ref[...], lb_ref[...])
        mask = _tril(chunk)
        for g in range(groups):
            cols = slice(g * LANES, (g + 1) * LANES)
            wm = jnp.where(mask, ws_ref[g], 0.0).astype(BF16)
            sg = _dot(wm, vn[:, cols].astype(BF16)) + bs_ref[:, g:g + 1]
            ya_ref[:, cols] = (gu[:, cols] * sg).astype(BF16)

    return _pcall(
        body, name="gmlp_fwd", grid=(s // chunk,),
        in_specs=[_row_spec(chunk, gw, 0), _row_spec(chunk, gw, 1), _vec_spec(gw), _vec_spec(gw),
                  pl.BlockSpec((groups, chunk, chunk), lambda i: (0, 0, 0)), pl.BlockSpec((chunk, LANES), lambda i: (0, 0))],
        out_specs=_row_spec(chunk, gw), out_shape=jax.ShapeDtypeStruct((s, gw), BF16),
        compiler_params=_params("parallel"),
    )(z, z, ln_g, ln_b, ws, bs_t)


def _gmlp_bwd(dya, z, ln_g, ln_b, ws, bs_t):
    s = z.shape[0]
    gw = ln_g.shape[1]
    groups, chunk, _ = ws.shape

    def body(dya_ref, u_ref, v_ref, lg_ref, lb_ref, ws_ref, bs_ref, duv_ref, dws_ref, dbs_ref, dlg_ref, dlb_ref, dvn_ref):
        first = pl.program_id(0) == 0
        u, v, lg = u_ref[...], v_ref[...], lg_ref[...]
        gu = _gelu(u)
        xh, rs, vn = _gmlp_norm(v, lg, lb_ref[...])
        dyav = dya_ref[...]
        mask = _tril(chunk)
        lane = lax.broadcasted_iota(jnp.int32, (chunk, LANES), 1)
        dbs = jnp.zeros((chunk, LANES), F32)
        for g in range(groups):
            cols = slice(g * LANES, (g + 1) * LANES)
            wm = jnp.where(mask, ws_ref[g], 0.0).astype(BF16)
            vg = vn[:, cols].astype(BF16)
            sg = _dot(wm, vg) + bs_ref[:, g:g + 1]
            ds = dyav[:, cols] * gu[:, cols]
            duv_ref[:, cols] = (dyav[:, cols] * sg * _gelu_grad(u[:, cols])).astype(BF16)
            dsb = ds.astype(BF16)
            _accumulate(first, dws_ref.at[g], jnp.where(mask, _dot(dsb, vg, NT), 0.0))
            dbs = dbs + jnp.where(lane == g, jnp.sum(ds, axis=-1, keepdims=True), 0.0)
            dvn_ref[:, cols] = _dot(wm, dsb, TN)
        dvn = dvn_ref[...]
        _accumulate(first, dbs_ref, dbs)
        _accumulate(first, dlb_ref, _colsum(dvn))
        _accumulate(first, dlg_ref, _colsum(dvn * xh))
        dxh = dvn * lg
        dgv = rs * (dxh - jnp.mean(dxh, axis=-1, keepdims=True) - xh * jnp.mean(dxh * xh, axis=-1, keepdims=True))
        duv_ref[:, gw:] = (dgv * _gelu_grad(v)).astype(BF16)

    return _pcall(
        body, name="gmlp_bwd", grid=(s // chunk,),
        in_specs=[_row_spec(chunk, gw), _row_spec(chunk, gw, 0), _row_spec(chunk, gw, 1), _vec_spec(gw), _vec_spec(gw),
                  pl.BlockSpec((groups, chunk, chunk), lambda i: (0, 0, 0)), pl.BlockSpec((chunk, LANES), lambda i: (0, 0))],
        out_specs=[_row_spec(chunk, 2 * gw), pl.BlockSpec((groups, chunk, chunk), lambda i: (0, 0, 0)),
                   pl.BlockSpec((chunk, LANES), lambda i: (0, 0)), _vec_spec(gw), _vec_spec(gw)],
        out_shape=[jax.ShapeDtypeStruct((s, 2 * gw), BF16), jax.ShapeDtypeStruct((groups, chunk, chunk), F32),
                   jax.ShapeDtypeStruct((chunk, LANES), F32), jax.ShapeDtypeStruct((1, gw), F32),
                   jax.ShapeDtypeStruct((1, gw), F32)],
        scratch_shapes=[pltpu.VMEM((chunk, gw), F32)],
        compiler_params=_params("arbitrary"),
    )(dya, z, z, ln_g, ln_b, ws, bs_t)


def _lower_bound(lb_ref):
    a0, a1 = lb_ref[0:1, :], lb_ref[1:2, :]
    mx = jnp.maximum(a0, a1)
    e0, e1 = jnp.exp(a0 - mx), jnp.exp(a1 - mx)
    return e0 / (e0 + e1)


def _hg_chunk(q, fp, lb, tri_incl):
    t = q.shape[0]
    sig = _sigmoid(fp)
    f = lb + (1.0 - lb) * sig
    k = 1.0 - f
    sq = _sigmoid(q)
    qa = q * sq
    b = _dot(tri_incl, jnp.log(f), NN, lax.Precision.HIGHEST)
    row = lax.broadcasted_iota(jnp.int32, b.shape, 0)
    b_last = _colsum(jnp.where(row == t - 1, b, 0.0))
    b_mid = _colsum(jnp.where(row == t // 2, b, 0.0))
    e_q = jnp.exp(jnp.minimum(b - b_mid, EXP_CLAMP))
    e_k = jnp.exp(jnp.minimum(b_mid - b, EXP_CLAMP))
    e_in = jnp.exp(b)
    e_out = jnp.exp(b_last - b)
    return dict(sig=sig, f=f, k=k, sq=sq, qa=qa, b_last=b_last, e_q=e_q, e_k=e_k, e_in=e_in, e_out=e_out,
                q_hat=(qa * e_q).astype(BF16), k_hat=(k * e_k).astype(BF16),
                q_in=(qa * e_in).astype(BF16), k_out=k * e_out)


def _hgrn_fwd(z, hg_lb, norm_g, q_col):
    s = z.shape[0]
    hw = norm_g.shape[1]
    heads = hw // LANES
    t = HG_CHUNK
    rows = min(HG_ROWS, s)
    per_step = rows // t
    per = hw // LANES

    def zspec(which):
        return pl.BlockSpec((rows, LANES), lambda h, r, which=which: (r, q_col + which * per + h))

    def body(q_ref, f_ref, i_ref, g_ref, lb_ref, ng_ref, yb_ref, o_ref, st_out_ref, st_ref):
        @pl.when(pl.program_id(1) == 0)
        def _():
            st_ref[...] = jnp.zeros_like(st_ref)

        lb = _lower_bound(lb_ref)
        ng = ng_ref[...]
        tri = _tril(t)
        tri_f = tri.astype(F32)
        for j in range(per_step):
            rs_ = slice(j * t, (j + 1) * t)
            ch = _hg_chunk(q_ref[rs_, :], f_ref[rs_, :], lb, tri_f)
            iv = i_ref[rs_, :].astype(BF16)
            st = st_ref[...]
            st_out_ref[j] = st
            attn = jnp.where(tri, _dot(ch["q_hat"], ch["k_hat"], NT), 0.0)
            o = _dot(ch["q_in"], st.astype(BF16), NT) + _dot(attn.astype(BF16), iv)
            st_ref[...] = st * jnp.exp(ch["b_last"]) + _dot(iv, ch["k_out"].astype(BF16), TN)
            o_ref[rs_, :] = o
            og = g_ref[rs_, :]
            on, _ = _rms(o)
            yb_ref[rs_, :] = (on * ng * (og * _sigmoid(og))).astype(BF16)

    out_row = pl.BlockSpec((rows, LANES), lambda h, r: (r, h))
    return _pcall(
        body, name="hgrn_fwd", grid=(heads, s // rows),
        in_specs=[zspec(0), zspec(1), zspec(2), zspec(3),
                  pl.BlockSpec((2, LANES), lambda h, r: (0, h)), pl.BlockSpec((1, LANES), lambda h, r: (0, h))],
        out_specs=[out_row, out_row, pl.BlockSpec((None, per_step, LANES, LANES), lambda h, r: (h, r, 0, 0))],
        out_shape=[jax.ShapeDtypeStruct((s, hw), BF16), jax.ShapeDtypeStruct((s, hw), F32),
                   jax.ShapeDtypeStruct((heads, s // t, LANES, LANES), F32)],
        scratch_shapes=[pltpu.VMEM((LANES, LANES), F32)],
        compiler_params=_params("parallel", "arbitrary"),
    )(z, z, z, z, hg_lb, norm_g)


def _hgrn_bwd(dyb, z, o_raw, states, hg_lb, norm_g, q_col):
    s = z.shape[0]
    hw = norm_g.shape[1]
    heads = hw // LANES
    t = HG_CHUNK
    rows = min(HG_ROWS, s)
    per_step = rows // t
    per = hw // LANES
    n_steps = s // rows

    def zspec(which):
        return pl.BlockSpec((rows, LANES), lambda h, r, which=which: (n_steps - 1 - r, q_col + which * per + h))

    def body(dyb_ref, q_ref, f_ref, i_ref, g_ref, o_ref, st_in_ref, lb_ref, ng_ref,
             dq_ref, df_ref, di_ref, dg_ref, dlb_ref, dng_ref, dst_ref, acc_lb_ref, acc_ng_ref):
        step = pl.program_id(1)

        @pl.when(step == 0)
        def _():
            dst_ref[...] = jnp.zeros_like(dst_ref)
            acc_lb_ref[...] = jnp.zeros_like(acc_lb_ref)
            acc_ng_ref[...] = jnp.zeros_like(acc_ng_ref)

        lb = _lower_bound(lb_ref)
        ng = ng_ref[...]
        tri = _tril(t)
        tri_f = tri.astype(F32)
        tri_rev = (lax.broadcasted_iota(jnp.int32, (t, t), 0) <= lax.broadcasted_iota(jnp.int32, (t, t), 1)).astype(F32)
        last_row = lax.broadcasted_iota(jnp.int32, (t, LANES), 0) == t - 1
        for j in reversed(range(per_step)):
            rs_ = slice(j * t, (j + 1) * t)
            q = q_ref[rs_, :]
            ch = _hg_chunk(q, f_ref[rs_, :], lb, tri_f)
            ivf = i_ref[rs_, :]
            iv = ivf.astype(BF16)
            o, og, dy = o_ref[rs_, :], g_ref[rs_, :], dyb_ref[rs_, :]
            so = _sigmoid(og)
            on, r = _rms(o)
            acc_ng_ref[...] += _colsum(dy * on * (og * so))
            dg_ref[rs_, :] = (dy * on * ng * so * (1.0 + og * (1.0 - so))).astype(BF16)
            don = dy * ng * (og * so)
            do = (r * (don - on * jnp.mean(don * on, axis=-1, keepdims=True))).astype(BF16)
            st_prev = st_in_ref[j]
            dst = dst_ref[...]
            dst_b = dst.astype(BF16)
            attn = jnp.where(tri, _dot(ch["q_hat"], ch["k_hat"], NT), 0.0).astype(BF16)
            d_attn = jnp.where(tri, _dot(do, iv, NT), 0.0).astype(BF16)
            k_out = ch["k_out"]
            dq_in = _dot(do, st_prev.astype(BF16))
            dk_out = _dot(iv, dst_b)
            di_ref[rs_, :] = (_dot(attn, do, TN) + _dot(k_out.astype(BF16), dst_b, NT)).astype(BF16)
            dqa = dq_in * ch["e_in"] + _dot(d_attn, ch["k_hat"]) * ch["e_q"]
            dk = dk_out * ch["e_out"] + _dot(d_attn, ch["q_hat"], TN) * ch["e_k"]
            db = ch["qa"] * dqa - ch["k"] * dk
            e_last = jnp.exp(ch["b_last"])
            extra = _colsum(dk_out * k_out) + e_last * _colsum(st_prev * dst)
            db = jnp.where(last_row, db + extra, db)
            dlf = _dot(tri_rev, db, NN, lax.Precision.HIGHEST)
            dfv = dlf / ch["f"] - dk
            sig = ch["sig"]
            df_ref[rs_, :] = (dfv * (1.0 - lb) * sig * (1.0 - sig)).astype(BF16)
            acc_lb_ref[...] += _colsum(dfv * (1.0 - sig))
            sq = ch["sq"]
            dq_ref[rs_, :] = (dqa * sq * (1.0 + q * (1.0 - sq))).astype(BF16)
            dst_ref[...] = dst * e_last + _dot(do, ch["q_in"], TN)

        @pl.when(step == n_steps - 1)
        def _():
            d0 = acc_lb_ref[...] * lb * (1.0 - lb)
            dlb_ref[0:1, :] = d0
            dlb_ref[1:2, :] = -d0
            dng_ref[...] = acc_ng_ref[...]

    rev_row = pl.BlockSpec((rows, LANES), lambda h, r: (n_steps - 1 - r, h))
    piece = jax.ShapeDtypeStruct((s, hw), BF16)
    return _pcall(
        body, name="hgrn_bwd", grid=(heads, n_steps),
        in_specs=[rev_row, zspec(0), zspec(1), zspec(2), zspec(3), rev_row,
                  pl.BlockSpec((None, per_step, LANES, LANES), lambda h, r: (h, n_steps - 1 - r, 0, 0)),
                  pl.BlockSpec((2, LANES), lambda h, r: (0, h)), pl.BlockSpec((1, LANES), lambda h, r: (0, h))],
        out_specs=[rev_row, rev_row, rev_row, rev_row,
                   pl.BlockSpec((2, LANES), lambda h, r: (0, h)), pl.BlockSpec((1, LANES), lambda h, r: (0, h))],
        out_shape=[piece, piece, piece, piece, jax.ShapeDtypeStruct((2, hw), F32), jax.ShapeDtypeStruct((1, hw), F32)],
        scratch_shapes=[pltpu.VMEM((LANES, LANES), F32), pltpu.VMEM((1, LANES), F32), pltpu.VMEM((1, LANES), F32)],
        compiler_params=_params("parallel", "arbitrary"),
    )(dyb, z, z, z, z, o_raw, states, hg_lb, norm_g)


def _adamw(w, m, v, parts, name):
    rows, cols = w.shape
    n_parts = parts.shape[0]
    bc = _pick(cols, 1024)
    rb = _pick(rows, 256) if rows % LANES == 0 else rows
    m_corr = 1.0 - ADAM_B1 ** ADAM_STEP
    v_corr = 1.0 - ADAM_B2 ** ADAM_STEP

    def body(w_ref, m_ref, v_ref, p_ref, g_ref, d_ref, mo_ref, vo_ref):
        g = p_ref[0].astype(F32)
        for p in range(1, n_parts):
            g = g + p_ref[p].astype(F32)
        m2 = ADAM_B1 * m_ref[...] + (1.0 - ADAM_B1) * g
        v2 = ADAM_B2 * v_ref[...] + (1.0 - ADAM_B2) * (g * g)
        g_ref[...] = g
        mo_ref[...] = m2
        vo_ref[...] = v2
        d_ref[...] = -ADAM_LR * ((m2 / m_corr) / (jnp.sqrt(v2 / v_corr) + ADAM_EPS) + ADAM_WD * w_ref[...])

    blk = pl.BlockSpec((rb, bc), lambda i, j: (i, j))
    out = jax.ShapeDtypeStruct((rows, cols), F32)
    return _pcall(
        body, name=name, grid=(rows // rb, cols // bc),
        in_specs=[blk, blk, blk, pl.BlockSpec((n_parts, rb, bc), lambda i, j: (0, i, j))],
        out_specs=[blk] * 4, out_shape=[out] * 4, compiler_params=_params("parallel", "parallel"),
    )(w, m, v, parts)


SMALL = ("b_ada", "norm1_g", "b_gate", "gmlp_ln_g", "gmlp_ln_b", "gmlp_ws", "gmlp_bs", "hg_lb", "hg_norm_g",
         "norm2_g", "final_norm_g")
BIG = ("w_in", "w_branch_gmlp", "w_branch_hg", "w_out", "w_ffn_in", "w_ffn_out")
WEIGHTS = ("w_ada", "b_ada", "norm1_g", "w_in", "b_gate", "gmlp_ln_g", "gmlp_ln_b", "gmlp_ws", "gmlp_bs", "hg_lb",
           "hg_norm_g", "w_branch_gmlp", "w_branch_hg", "w_out", "norm2_g", "w_ffn_in", "w_ffn_out", "final_norm_g")


def _pack(parts):
    return jnp.concatenate([p.reshape(-1, LANES) for p in parts], axis=0)


def _step(x, c, loss_target, w, m, v):
    s, d = x.shape[1], x.shape[2]
    gw = w["gmlp_ln_g"].shape[-1]
    hw = w["hg_norm_g"].shape[-1]
    x2d, tgt = x[0], loss_target[0]
    mx, my, mc = lax.axis_index("x"), lax.axis_index("y"), lax.axis_index("c")
    chip = 2 * mx + my
    dev = 2 * chip + mc
    q_col = 2 * gw // LANES
    gate_col = (2 * gw + 4 * hw) // d
    place = jnp.stack([chip, mc]).astype(jnp.int32)
    _Order.last = None

    quarters = {n: w[n][0].astype(BF16) for n in BIG}
    gather_in = _WeightGather("in", [quarters["w_in"]])
    gather_mix = _WeightGather("mix", [quarters[n] for n in ("w_branch_gmlp", "w_branch_hg", "w_out")])
    gather_fi = _WeightGather("fi", [quarters["w_ffn_in"]])
    gather_fo = _WeightGather("fo", [quarters["w_ffn_out"]])

    c_all = _all_gather8(c.reshape(-1, LANES), "gather_c").reshape(N_DEV, d)
    n_ada = w["w_ada"].shape[-1]
    b_ada_q = lax.dynamic_slice(w["b_ada"], (0, chip * n_ada), (1, n_ada))
    mod_q = _ada_fwd(c_all, w["w_ada"][0], b_ada_q)
    mod_all = _all_gather8(mod_q, "gather_mod")
    mod = lax.dynamic_index_in_dim(mod_all, dev, axis=1, keepdims=False)[::2].reshape(1, 6 * d)
    sh1, sc1, gt1, sh2, sc2, gt2 = [mod[:, i * d:(i + 1) * d] for i in range(6)]

    norm1_g, norm2_g, final_g = w["norm1_g"], w["norm2_g"], w["final_norm_g"].reshape(1, d)
    ln_g, ln_b = w["gmlp_ln_g"], w["gmlp_ln_b"]
    ws = w["gmlp_ws"][0]
    groups = ws.shape[0]
    bs_t = jnp.pad(w["gmlp_bs"][0].T, ((0, 0), (0, LANES - groups)))
    hg_lb, hg_ng, b_gate = w["hg_lb"], w["hg_norm_g"], w["b_gate"]

    h1 = _norm_mod_fwd(x2d, norm1_g, sc1, sh1, "norm1_fwd")
    gather_in.pass_on()
    w_in, = gather_in.done()
    z = _matmul(h1, w_in, mode="nn", name="mm_z", out_dtype=F32, b_slots=True, bn=1280)
    gather_mix.pass_on()
    ya = _gmlp_fwd(z, ln_g, ln_b, ws, bs_t)
    yb, o_raw, states = _hgrn_fwd(z, hg_lb, hg_ng, q_col)
    w_bg, w_bh, w_out = gather_mix.done()
    w_out = w_out.reshape(-1, w_out.shape[-1])
    pa = _matmul(ya, w_bg, mode="nn", name="mm_pa", out_dtype=F32, b_slots=True)
    pb = _matmul(yb, w_bh, mode="nn", name="mm_pb", out_dtype=F32, b_slots=True)
    y = _gate_fwd(pa, pb, z, b_gate, gate_col)
    gather_fi.pass_on()
    yo = _matmul(y, w_out, mode="nn", name="mm_yo", out_dtype=F32)
    x1, h2 = _norm_mod_fwd(x2d, norm2_g, sc2, sh2, "norm2_fwd", res=yo, gt=gt1)
    w_fi, = gather_fi.done()
    aup = _matmul(h2, w_fi, mode="nn", name="mm_aup", out_dtype=F32, b_slots=True, bn=1408)
    gather_fo.pass_on()
    hf = _swiglu_fwd(aup)
    w_fo, = gather_fo.done()
    w_fo = w_fo.reshape(-1, w_fo.shape[-1])
    ffn = _matmul(hf, w_fo, mode="nn", name="mm_ffn", out_dtype=F32)
    dx2, dffn, loss_row, d_final_g, d_gt2 = _final_loss(x1, ffn, gt2, final_g, tgt)

    g_fo = _matmul(hf, dffn, mode="tn", name="mm_g_fo", out_dtype=BF16, bm=1408)
    red_fo = _GradReduce("fo", [g_fo.reshape(N_CHIPS, -1, g_fo.shape[-1])])
    dhf = _matmul(dffn, w_fo, mode="nt", name="mm_dhf", out_dtype=F32, bn=1408)
    red_fo.cross(place)
    daup = _swiglu_bwd(dhf, aup)
    g_fi = _matmul(h2, daup, mode="tn", name="mm_g_fi", out_dtype=BF16, out_slots=True, bn=1408)
    red_fi = _GradReduce("fi", [g_fi])
    dh2 = _matmul(daup, w_fi, mode="nt", name="mm_dh2", out_dtype=F32, b_slots=True, bk=1408)
    red_fi.cross(place)
    dx1, dyo, d_sh2, d_sc2, d_norm2, d_gt1 = _norm_mod_bwd(dh2, x1, dx2, norm2_g, sc2, "norm2_bwd", branch=yo, gt=gt1)
    g_out = _matmul(y, dyo, mode="tn", name="mm_g_out", out_dtype=BF16)
    red_out = _GradReduce("out", [g_out.reshape(N_CHIPS, -1, g_out.shape[-1])])
    dy = _matmul(dyo, w_out, mode="nt", name="mm_dy", out_dtype=F32)
    red_out.cross(place)
    dpa, dpb, dz_gate, d_b_gate = _gate_bwd(dy, pa, pb, z, b_gate, gate_col)
    g_bg = _matmul(ya, dpa, mode="tn", name="mm_g_bg", out_dtype=BF16, out_slots=True)
    g_bh = _matmul(yb, dpb, mode="tn", name="mm_g_bh", out_dtype=BF16, out_slots=True)
    red_br = _GradReduce("br", [g_bg, g_bh])
    dya = _matmul(dpa, w_bg, mode="nt", name="mm_dya", out_dtype=F32, b_slots=True)
    dyb = _matmul(dpb, w_bh, mode="nt", name="mm_dyb", out_dtype=F32, b_slots=True)
    red_br.cross(place)
    dz_uv, d_ws, d_bs_t, d_ln_g, d_ln_b = _gmlp_bwd(dya, z, ln_g, ln_b, ws, bs_t)
    dz_q, dz_f, dz_i, dz_g, d_hg_lb, d_hg_ng = _hgrn_bwd(dyb, z, o_raw, states, hg_lb, hg_ng, q_col)
    dz = jnp.concatenate([dz_uv, dz_q, dz_f, dz_i, dz_g, dz_gate], axis=1)
    g_in = _matmul(h1, dz, mode="tn", name="mm_g_in", out_dtype=BF16, out_slots=True, bn=1280)
    red_in = _GradReduce("in", [g_in])
    dh1 = _matmul(dz, w_in, mode="nt", name="mm_dh1", out_dtype=F32, b_slots=True)
    red_in.cross(place)
    grad_x, d_sh1, d_sc1, d_norm1 = _norm_mod_bwd(dh1, x2d, dx1, norm1_g, sc1, "norm1_bwd")

    d_mod = jnp.concatenate([d_sh1, d_sc1, d_gt1, d_sh2, d_sc2, d_gt2], axis=1)
    small_part = {"b_ada": d_mod, "norm1_g": d_norm1, "b_gate": d_b_gate, "gmlp_ln_g": d_ln_g, "gmlp_ln_b": d_ln_b,
                  "gmlp_ws": d_ws, "gmlp_bs": d_bs_t[:, :groups].T, "hg_lb": d_hg_lb, "hg_norm_g": d_hg_ng,
                  "norm2_g": d_norm2, "final_norm_g": d_final_g}
    small_all = _all_gather8(_pack([small_part[n] for n in SMALL]), "gather_small")
    d_mod_all = small_all[:, :6 * d // LANES].reshape(N_DEV, 6 * d)
    d_mod_q = lax.dynamic_slice(d_mod_all, (0, chip * n_ada), (N_DEV, n_ada))
    g_ada = _ada_bwd(c_all, d_mod_q)

    grad, delta, new_m, new_v = {}, {}, {}, {}

    def update(n, parts):
        outs = _adamw(w[n][0], m[n][0], v[n][0], parts, "adamw_" + n)
        grad[n], delta[n], new_m[n], new_v[n] = [o[None] for o in outs]

    update("w_ada", g_ada)
    outs = _adamw(_pack([w[n] for n in SMALL]), _pack([m[n] for n in SMALL]), _pack([v[n] for n in SMALL]),
                  small_all, "adamw_small")
    red_fo.join(place)
    red_fi.join(place)
    update("w_ffn_out", red_fo.done()[0][None])
    red_out.join(place)
    update("w_ffn_in", red_fi.done()[0][None])
    red_br.join(place)
    update("w_out", red_out.done()[0][None])
    red_in.join(place)
    g_bg, g_bh = red_br.done()
    update("w_branch_gmlp", g_bg[None])
    update("w_branch_hg", g_bh[None])
    update("w_in", red_in.done()[0][None])
    row = 0
    for n in SMALL:
        cnt = w[n].size // LANES
        for dst, o in zip((grad, delta, new_m, new_v), outs):
            dst[n] = o[row:row + cnt].reshape(w[n].shape)
        row += cnt

    loss = lax.psum(loss_row[0, 0], ("x", "y", "c"))
    return (loss, grad_x[None], *[grad[n] for n in WEIGHTS], *[delta[n] for n in WEIGHTS],
            *[new_m[n] for n in WEIGHTS], *[new_v[n] for n in WEIGHTS])


def kernel(x, c, w_ada, b_ada, norm1_g, w_in, b_gate, gmlp_ln_g, gmlp_ln_b, gmlp_ws, gmlp_bs, hg_lb, hg_norm_g, w_branch_gmlp, w_branch_hg, w_out, norm2_g, w_ffn_in, w_ffn_out, final_norm_g, loss_target, m_w_ada, m_b_ada, m_norm1_g, m_w_in, m_b_gate, m_gmlp_ln_g, m_gmlp_ln_b, m_gmlp_ws, m_gmlp_bs, m_hg_lb, m_hg_norm_g, m_w_branch_gmlp, m_w_branch_hg, m_w_out, m_norm2_g, m_w_ffn_in, m_w_ffn_out, m_final_norm_g, v_w_ada, v_b_ada, v_norm1_g, v_w_in, v_b_gate, v_gmlp_ln_g, v_gmlp_ln_b, v_gmlp_ws, v_gmlp_bs, v_hg_lb, v_hg_norm_g, v_w_branch_gmlp, v_w_branch_hg, v_w_out, v_norm2_g, v_w_ffn_in, v_w_ffn_out, v_final_norm_g):
    w = dict(w_ada=w_ada, b_ada=b_ada, norm1_g=norm1_g, w_in=w_in, b_gate=b_gate, gmlp_ln_g=gmlp_ln_g,
             gmlp_ln_b=gmlp_ln_b, gmlp_ws=gmlp_ws, gmlp_bs=gmlp_bs, hg_lb=hg_lb, hg_norm_g=hg_norm_g,
             w_branch_gmlp=w_branch_gmlp, w_branch_hg=w_branch_hg, w_out=w_out, norm2_g=norm2_g,
             w_ffn_in=w_ffn_in, w_ffn_out=w_ffn_out, final_norm_g=final_norm_g)
    m = dict(w_ada=m_w_ada, b_ada=m_b_ada, norm1_g=m_norm1_g, w_in=m_w_in, b_gate=m_b_gate, gmlp_ln_g=m_gmlp_ln_g,
             gmlp_ln_b=m_gmlp_ln_b, gmlp_ws=m_gmlp_ws, gmlp_bs=m_gmlp_bs, hg_lb=m_hg_lb, hg_norm_g=m_hg_norm_g,
             w_branch_gmlp=m_w_branch_gmlp, w_branch_hg=m_w_branch_hg, w_out=m_w_out, norm2_g=m_norm2_g,
             w_ffn_in=m_w_ffn_in, w_ffn_out=m_w_ffn_out, final_norm_g=m_final_norm_g)
    v = dict(w_ada=v_w_ada, b_ada=v_b_ada, norm1_g=v_norm1_g, w_in=v_w_in, b_gate=v_b_gate, gmlp_ln_g=v_gmlp_ln_g,
             gmlp_ln_b=v_gmlp_ln_b, gmlp_ws=v_gmlp_ws, gmlp_bs=v_gmlp_bs, hg_lb=v_hg_lb, hg_norm_g=v_hg_norm_g,
             w_branch_gmlp=v_w_branch_gmlp, w_branch_hg=v_w_branch_hg, w_out=v_w_out, norm2_g=v_norm2_g,
             w_ffn_in=v_w_ffn_in, w_ffn_out=v_w_ffn_out, final_norm_g=v_final_norm_g)
    return _step(x, c, loss_target, w, m, v)
```

```python
import functools

import jax
import jax.numpy as jnp
from jax import lax
from jax.experimental import pallas as pl
from jax.experimental.pallas import tpu as pltpu

F32 = jnp.float32
BF16 = jnp.bfloat16
EPS = 1e-6
LANES = 128
N_CHIPS = 4
N_DEV = 8
VMEM_LIMIT_BYTES = 56 * 1024 * 1024
HG_CHUNK = 32
HG_ROWS = 256
EXP_CLAMP = 80.0
ADAM_LR, ADAM_B1, ADAM_B2, ADAM_EPS, ADAM_WD, ADAM_STEP = 0.001, 0.9, 0.999, 1e-08, 0.01, 10
MESH = pl.DeviceIdType.MESH

NN = (((1,), (0,)), ((), ()))
NT = (((1,), (1,)), ((), ()))
TN = (((0,), (0,)), ((), ()))


def _dot(a, b, dims=NN, precision=None):
    return lax.dot_general(a, b, dims, precision=precision, preferred_element_type=F32)


def _params(*semantics):
    return pltpu.CompilerParams(dimension_semantics=semantics, vmem_limit_bytes=VMEM_LIMIT_BYTES)


class _Order:
    last = None


def _pcall(body, *, in_specs, out_specs, grid=(), scratch_shapes=(), num_scalar_prefetch=0, **kw):
    def run(*ins):
        deps = () if _Order.last is None else (_Order.last,)
        n_in, n_dep = len(ins), len(deps)

        def wrapped(*refs):
            body(*refs[:n_in], *refs[n_in + n_dep:])

        specs = list(in_specs) + [pl.BlockSpec(memory_space=pl.ANY)] * n_dep
        if num_scalar_prefetch:
            grid_spec = pltpu.PrefetchScalarGridSpec(
                num_scalar_prefetch=num_scalar_prefetch, grid=grid, in_specs=specs, out_specs=out_specs,
                scratch_shapes=scratch_shapes)
            outs = pl.pallas_call(wrapped, grid_spec=grid_spec, **kw)(*ins, *deps)
        else:
            outs = pl.pallas_call(wrapped, grid=grid, in_specs=specs, out_specs=out_specs,
                                  scratch_shapes=scratch_shapes, **kw)(*ins, *deps)
        _Order.last = jax.tree.leaves(outs)[0]
        return outs

    return run


def _pick_rows(dim, pref):
    best = None
    for cand in range(16, min(dim, pref) + 1, 16):
        if dim % cand == 0:
            best = cand
    assert best is not None, (dim, pref)
    return best


def _pick(dim, pref):
    if dim <= pref:
        return dim
    best = None
    for cand in range(LANES, pref + 1, LANES):
        if dim % cand == 0:
            best = cand
    assert best is not None, (dim, pref)
    return best


def _sigmoid(x):
    return 1.0 / (1.0 + jnp.exp(-x))


def _gelu(x):
    c = 0.7978845608028654
    return 0.5 * x * (1.0 + jnp.tanh(c * (x + 0.044715 * x * x * x)))


def _gelu_grad(x):
    c = 0.7978845608028654
    t = jnp.tanh(c * (x + 0.044715 * x * x * x))
    return 0.5 * (1.0 + t) + 0.5 * x * (1.0 - t * t) * c * (1.0 + 3.0 * 0.044715 * x * x)


def _rms(x):
    r = lax.rsqrt(jnp.mean(x * x, axis=-1, keepdims=True) + EPS)
    return x * r, r


def _colsum(x):
    return jnp.sum(x, axis=0, keepdims=True)


def _accumulate(first, ref, val):
    @pl.when(first)
    def _():
        ref[...] = val

    @pl.when(jnp.logical_not(first))
    def _():
        ref[...] += val


def _matmul(a, b, *, mode, name, out_dtype, b_slots=False, out_slots=False, bm=1024, bn=1024, bk=512):
    if mode == "nn":
        m, k = a.shape
        n = b.shape[2] * N_CHIPS if b_slots else b.shape[1]
        per = b.shape[2] if b_slots else n
    elif mode == "nt":
        m, k = a.shape
        n = b.shape[1] if b_slots else b.shape[0]
        per = b.shape[2] if b_slots else k
    else:
        k, m = a.shape
        n = b.shape[1]
        per = n // N_CHIPS if out_slots else n
    bm = _pick(m, bm)
    if mode == "nt":
        bn, bk = _pick(n, bn), _pick(per, bk)
    else:
        bn, bk = _pick(per, bn), _pick(k, bk)
    nk = k // bk
    per_blocks = per // (bk if mode == "nt" else bn)
    dims = {"nn": NN, "nt": NT, "tn": TN}[mode]

    def body(a_ref, b_ref, o_ref, acc_ref):
        kk = pl.program_id(2)

        @pl.when(kk == 0)
        def _():
            acc_ref[...] = jnp.zeros_like(acc_ref)

        acc_ref[...] += _dot(a_ref[...], b_ref[...], dims)

        @pl.when(kk == nk - 1)
        def _():
            o_ref[...] = acc_ref[...].astype(o_ref.dtype)

    if mode == "nn":
        a_spec = pl.BlockSpec((bm, bk), lambda i, j, kk: (i, kk))
        if b_slots:
            b_spec = pl.BlockSpec((None, bk, bn), lambda i, j, kk: (j // per_blocks, kk, j % per_blocks))
        else:
            b_spec = pl.BlockSpec((bk, bn), lambda i, j, kk: (kk, j))
    elif mode == "nt":
        a_spec = pl.BlockSpec((bm, bk), lambda i, j, kk: (i, kk))
        if b_slots:
            b_spec = pl.BlockSpec((None, bn, bk), lambda i, j, kk: (kk // per_blocks, j, kk % per_blocks))
        else:
            b_spec = pl.BlockSpec((bn, bk), lambda i, j, kk: (j, kk))
    else:
        a_spec = pl.BlockSpec((bk, bm), lambda i, j, kk: (kk, i))
        b_spec = pl.BlockSpec((bk, bn), lambda i, j, kk: (kk, j))
    if out_slots:
        o_spec = pl.BlockSpec((None, bm, bn), lambda i, j, kk: (j // per_blocks, i, j % per_blocks))
        out_shape = jax.ShapeDtypeStruct((N_CHIPS, m, per), out_dtype)
    else:
        o_spec = pl.BlockSpec((bm, bn), lambda i, j, kk: (i, j))
        out_shape = jax.ShapeDtypeStruct((m, n), out_dtype)
    return _pcall(
        body, name=name, grid=(m // bm, n // bn, nk), in_specs=[a_spec, b_spec], out_specs=o_spec,
        out_shape=out_shape, scratch_shapes=[pltpu.VMEM((bm, bn), F32)],
        compiler_params=_params("parallel", "parallel", "arbitrary"),
    )(a, b)


def _place():
    x, y, c = lax.axis_index("x"), lax.axis_index("y"), lax.axis_index("c")
    chips = [(1 - x, y), (x, 1 - y), (1 - x, 1 - y)]
    return x, y, c, chips


def _all_gather8(block, name):
    def body(x_ref, out_ref, send_sems, recv_sems, local_sem):
        x, y, c, chips = _place()
        me, sibling = (x, y, c), (x, y, 1 - c)

        def slot(px, py, pc):
            return out_ref.at[4 * px + 2 * py + pc]

        def copy(k, blk, to, src=None):
            return pltpu.make_async_remote_copy(
                src_ref=slot(*blk) if src is None else src, dst_ref=slot(*blk),
                send_sem=send_sems.at[k], recv_sem=recv_sems.at[k], device_id=to, device_id_type=MESH)

        mine = pltpu.make_async_copy(x_ref, slot(*me), local_sem)
        mine.start()
        first = [copy(0, me, sibling, src=x_ref)]
        first += [copy(1 + j, me, (*chip, c), src=x_ref) for j, chip in enumerate(chips)]
        for cp in first:
            cp.start()
        passed = [copy(4 + j, (*chip, c), sibling) for j, chip in enumerate(chips)]
        for j, chip in enumerate(chips):
            copy(1 + j, (*chip, c), me).wait_recv()
            passed[j].start()
        copy(0, sibling, me).wait_recv()
        for j, chip in enumerate(chips):
            copy(4 + j, (*chip, 1 - c), me).wait_recv()
        for cp in first + passed:
            cp.wait_send()
        mine.wait()

    return _pcall(
        body, name=name, out_shape=jax.ShapeDtypeStruct((N_DEV,) + block.shape, block.dtype),
        in_specs=[pl.BlockSpec(memory_space=pltpu.VMEM)], out_specs=pl.BlockSpec(memory_space=pltpu.VMEM),
        scratch_shapes=[pltpu.SemaphoreType.DMA((7,)), pltpu.SemaphoreType.DMA((7,)), pltpu.SemaphoreType.DMA],
        compiler_params=pltpu.CompilerParams(vmem_limit_bytes=VMEM_LIMIT_BYTES),
    )(block)


HBM_SPEC = pl.BlockSpec(memory_space=pltpu.HBM)
SEM_SPEC = pl.BlockSpec(memory_space=pltpu.SEMAPHORE)
ANY_SPEC = pl.BlockSpec(memory_space=pl.ANY)
EFFECT = pltpu.SideEffectType.DATAFLOW_SIDE_EFFECTING


def _xfer_start(name, bufs, plan, n_copies, after_last=False):
    nb = len(bufs)
    deps = (_Order.last,) if after_last and _Order.last is not None else ()
    nd = len(deps)

    def body(*refs):
        send_sems, recv_sems = refs[nb + nd], refs[nb + nd + 1]
        token = refs[nb + nd + 2 + nb]
        for k, (src, dst, dev) in enumerate(plan(refs[:nb], *_place())):
            pltpu.make_async_remote_copy(src_ref=src, dst_ref=dst, send_sem=send_sems.at[k], recv_sem=recv_sems.at[k],
                                         device_id=dev, device_id_type=MESH).start()
        token[...] = jnp.zeros_like(token)

    outs = pl.pallas_call(
        body, name=name,
        out_shape=(pltpu.SemaphoreType.DMA((n_copies,)), pltpu.SemaphoreType.DMA((n_copies,)),
                   *[pltpu.HBM(b.shape, b.dtype) for b in bufs], jax.ShapeDtypeStruct((8, LANES), F32)),
        in_specs=[HBM_SPEC] * nb + [ANY_SPEC] * nd,
        out_specs=(SEM_SPEC, SEM_SPEC, *[HBM_SPEC] * nb, pl.BlockSpec(memory_space=pltpu.VMEM)),
        input_output_aliases={i: 2 + i for i in range(nb)},
        compiler_params=pltpu.CompilerParams(has_side_effects=EFFECT),
    )(*[pltpu.with_memory_space_constraint(b, pltpu.HBM) for b in bufs], *deps)
    _Order.last = outs[-1]
    return (outs[0], outs[1]), list(outs[2:2 + nb])


def _xfer_wait(name, sems, bufs, plan):
    nb = len(bufs)

    def body(*refs):
        send_sems, recv_sems = refs[nb], refs[nb + 1]
        for k, (src, dst, dev) in enumerate(plan(refs[:nb], *_place())):
            copy = pltpu.make_async_remote_copy(src_ref=src, dst_ref=dst, send_sem=send_sems.at[k],
                                                recv_sem=recv_sems.at[k], device_id=dev, device_id_type=MESH)
            copy.wait_send()
            copy.wait_recv()

    outs = pl.pallas_call(
        body, name=name, out_shape=tuple(pltpu.HBM(b.shape, b.dtype) for b in bufs),
        in_specs=[HBM_SPEC] * nb + [SEM_SPEC, SEM_SPEC, ANY_SPEC], out_specs=tuple([HBM_SPEC] * nb),
        input_output_aliases={i: i for i in range(nb)},
        compiler_params=pltpu.CompilerParams(has_side_effects=EFFECT),
    )(*bufs, *sems, _Order.last)
    _Order.last = outs[0]
    return list(outs)


def _half(ref, c, axis):
    rows = ref.shape[axis] // 2
    return pl.ds(c * rows, rows)


def _plan_weights_ici(n):
    def plan(refs, x, y, c, chips):
        out = []
        for w in range(n):
            region = refs[w].at[2 * x + y, _half(refs[w], c, 1), :]
            out += [(region, region, (*chip, c)) for chip in chips]
        return out
    return plan


def _plan_weights_d2d(n):
    def plan(refs, x, y, c, chips):
        out = []
        for w in range(n):
            rows = _half(refs[w], c, 1)
            for chip in chips:
                region = refs[w].at[2 * chip[0] + chip[1], rows, :]
                out.append((region, region, (x, y, 1 - c)))
        return out
    return plan


def _plan_grads_d2d(n):
    def plan(refs, x, y, c, chips):
        return [(refs[w].at[:, _half(refs[w], 1 - c, 1), :], refs[n + w], (x, y, 1 - c)) for w in range(n)]
    return plan


def _plan_grads_ici(n):
    def plan(refs, x, y, c, chips):
        out = []
        for w in range(n):
            out += [(refs[w].at[2 * chip[0] + chip[1]], refs[n + w].at[2 * x + y], (*chip, c)) for chip in chips]
        return out
    return plan


def _plan_final_d2d(n):
    def plan(refs, x, y, c, chips):
        out = []
        for w in range(n):
            region = refs[w].at[_half(refs[w], c, 0), :]
            out.append((region, region, (x, y, 1 - c)))
        return out
    return plan


def _stream_blocks(hr, cols):
    bc = cols if cols <= 4096 else _pick(cols, 4096)
    return _pick_rows(hr, max(16, (768 * 1024) // bc)), bc


def _pre_reduce(g, landed, place, name):
    _, rows, cols = g.shape
    hr = rows // 2
    rb, bc = _stream_blocks(hr, cols)
    nrb = hr // rb

    def body(place_ref, g_ref, l_ref, o_ref):
        o_ref[...] = (g_ref[...].astype(F32) + l_ref[...].astype(F32)).astype(o_ref.dtype)

    return _pcall(
        body, name=name, num_scalar_prefetch=1, grid=(N_CHIPS, nrb, cols // bc),
        in_specs=[pl.BlockSpec((None, rb, bc), lambda j, i, k, p: (j, p[1] * nrb + i, k)),
                  pl.BlockSpec((None, rb, bc), lambda j, i, k, p: (j, i, k))],
        out_specs=pl.BlockSpec((None, rb, bc), lambda j, i, k, p: (j, i, k)),
        out_shape=jax.ShapeDtypeStruct((N_CHIPS, hr, cols), g.dtype),
        compiler_params=_params("parallel", "parallel", "parallel"),
    )(place, g, landed)


def _sum_slots(mine, landed, place, name):
    _, hr, cols = mine.shape
    rb, bc = _stream_blocks(hr, cols)
    rb = _pick_rows(hr, max(16, rb // 2))
    nrb = hr // rb

    def body(place_ref, m_ref, l_ref, o_ref):
        chip = place_ref[0]
        own = m_ref[...].astype(F32)
        total = jnp.where(chip == 0, own, l_ref[0].astype(F32))
        for j in range(1, N_CHIPS):
            total = total + jnp.where(chip == j, own, l_ref[j].astype(F32))
        o_ref[...] = total

    return _pcall(
        body, name=name, num_scalar_prefetch=1, grid=(nrb, cols // bc),
        in_specs=[pl.BlockSpec((None, rb, bc), lambda i, k, p: (p[0], i, k)),
                  pl.BlockSpec((N_CHIPS, rb, bc), lambda i, k, p: (0, i, k))],
        out_specs=pl.BlockSpec((rb, bc), lambda i, k, p: (p[1] * nrb + i, k)),
        out_shape=jax.ShapeDtypeStruct((2 * hr, cols), F32),
        compiler_params=_params("parallel", "parallel"),
    )(place, mine, landed)


class _WeightGather:
    def __init__(self, tag, quarters, chip):
        self.tag, self.n = tag, len(quarters)
        zones = [lax.dynamic_update_slice(lax.empty((N_CHIPS,) + q.shape, BF16), q.astype(BF16)[None], (chip, 0, 0))
                 for q in quarters]
        self.plan = _plan_weights_ici(self.n)
        self.sems, self.bufs = _xfer_start("wici_start_" + tag, zones, self.plan, 3 * self.n, after_last=True)

    def pass_on(self):
        bufs = _xfer_wait("wici_wait_" + self.tag, self.sems, self.bufs, self.plan)
        self.plan = _plan_weights_d2d(self.n)
        self.sems, self.bufs = _xfer_start("wd2d_start_" + self.tag, bufs, self.plan, 3 * self.n)

    def done(self):
        return _xfer_wait("wd2d_wait_" + self.tag, self.sems, self.bufs, self.plan)


class _GradReduce:
    def __init__(self, tag, grads):
        self.tag, self.n = tag, len(grads)
        zones = [lax.empty((N_CHIPS, g.shape[1] // 2, g.shape[2]), g.dtype) for g in grads]
        self.plan = _plan_grads_d2d(self.n)
        self.sems, self.bufs = _xfer_start("gd2d_start_" + tag, list(grads) + zones, self.plan, self.n)

    def pair(self, place):
        n = self.n
        bufs = _xfer_wait("gd2d_wait_" + self.tag, self.sems, self.bufs, self.plan)
        self.halves = [_pre_reduce(bufs[w], bufs[n + w], place, f"pre_reduce_{self.tag}{w}") for w in range(n)]

    def cross(self, after_last=False):
        zones = [lax.empty(h.shape, h.dtype) for h in self.halves]
        self.plan = _plan_grads_ici(self.n)
        self.sems, self.bufs = _xfer_start("gici_start_" + self.tag, self.halves + zones, self.plan, 3 * self.n,
                                           after_last=after_last)

    def step(self, place):
        self.pair(place)
        self.cross()

    def join(self, place):
        n = self.n
        bufs = _xfer_wait("gici_wait_" + self.tag, self.sems, self.bufs, self.plan)
        sums = [_sum_slots(bufs[w], bufs[n + w], place, f"sum_slots_{self.tag}{w}") for w in range(n)]
        self.plan = _plan_final_d2d(n)
        self.sems, self.bufs = _xfer_start("gfin_start_" + self.tag, sums, self.plan, n)

    def done(self):
        return _xfer_wait("gfin_wait_" + self.tag, self.sems, self.bufs, self.plan)


def _ada_fwd(c_all, w_q, b_q):
    d, n = w_q.shape
    bn = _pick(n, 512)

    def body(c_ref, w_ref, b_ref, o_ref):
        cv = c_ref[...]
        act = cv * _sigmoid(cv)
        o_ref[...] = _dot(act, w_ref[...], NN, lax.Precision.HIGHEST) + b_ref[...]

    return _pcall(
        body, name="ada_fwd", grid=(n // bn,),
        in_specs=[pl.BlockSpec((N_DEV, d), lambda j: (0, 0)), pl.BlockSpec((d, bn), lambda j: (0, j)),
                  pl.BlockSpec((1, bn), lambda j: (0, j))],
        out_specs=pl.BlockSpec((N_DEV, bn), lambda j: (0, j)),
        out_shape=jax.ShapeDtypeStruct((N_DEV, n), F32), compiler_params=_params("parallel"),
    )(c_all, w_q, b_q)


def _ada_bwd(c_all, dmod_q):
    d = c_all.shape[1]
    n = dmod_q.shape[1]
    bn = _pick(n, 512)

    def body(c_ref, g_ref, o_ref):
        cv = c_ref[...]
        act = cv * _sigmoid(cv)
        o_ref[...] = _dot(act, g_ref[...], TN, lax.Precision.HIGHEST)

    return _pcall(
        body, name="ada_bwd", grid=(n // bn,),
        in_specs=[pl.BlockSpec((N_DEV, d), lambda j: (0, 0)), pl.BlockSpec((N_DEV, bn), lambda j: (0, j))],
        out_specs=pl.BlockSpec((None, d, bn), lambda j: (0, 0, j)),
        out_shape=jax.ShapeDtypeStruct((1, d, n), F32), compiler_params=_params("parallel"),
    )(c_all, dmod_q)


def _row_spec(rb, width, col=0):
    return pl.BlockSpec((rb, width), lambda i, col=col: (i, col))


def _vec_spec(width, col=0):
    return pl.BlockSpec((1, width), lambda i, col=col: (0, col))


def _norm_mod_fwd(x, g, sc, sh, name, res=None, gt=None):
    s, d = x.shape
    rb = _pick(s, 256)
    has_res = res is not None

    def body(*refs):
        if has_res:
            x_ref, res_ref, gt_ref, g_ref, sc_ref, sh_ref, x1_ref, h_ref = refs
            xv = x_ref[...] + gt_ref[...] * res_ref[...]
            x1_ref[...] = xv
        else:
            x_ref, g_ref, sc_ref, sh_ref, h_ref = refs
            xv = x_ref[...]
        xh, _ = _rms(xv)
        h_ref[...] = (xh * g_ref[...] * (1.0 + sc_ref[...]) + sh_ref[...]).astype(BF16)

    row, vec = _row_spec(rb, d), _vec_spec(d)
    if has_res:
        ins, in_specs = (x, res, gt, g, sc, sh), [row, row, vec, vec, vec, vec]
        out_shape = [jax.ShapeDtypeStruct((s, d), F32), jax.ShapeDtypeStruct((s, d), BF16)]
        out_specs = [row, row]
    else:
        ins, in_specs = (x, g, sc, sh), [row, vec, vec, vec]
        out_shape, out_specs = jax.ShapeDtypeStruct((s, d), BF16), row
    return _pcall(body, name=name, grid=(s // rb,), in_specs=in_specs, out_specs=out_specs,
                          out_shape=out_shape, compiler_params=_params("parallel"))(*ins)


def _final_loss(x1, f, gt2, g_final, target):
    s, d = x1.shape
    rb = _pick(s, 256)

    def body(x1_ref, f_ref, gt_ref, g_ref, t_ref, dx_ref, df_ref, loss_ref, dg_ref, dgt_ref):
        first = pl.program_id(0) == 0
        fv, gt, gv = f_ref[...], gt_ref[...], g_ref[...]
        x2 = x1_ref[...] + gt * fv
        xh, r = _rms(x2)
        err = xh * gv - t_ref[...]
        blk = 0.5 * jnp.sum(jnp.sum(err * err, axis=1, keepdims=True), axis=0, keepdims=True) / d
        dy = err / d
        dxh = dy * gv
        dx = r * (dxh - xh * jnp.mean(dxh * xh, axis=-1, keepdims=True))
        dx_ref[...] = dx
        df_ref[...] = (dx * gt).astype(BF16)
        _accumulate(first, loss_ref, jnp.broadcast_to(blk, (1, LANES)))
        _accumulate(first, dg_ref, _colsum(dy * xh))
        _accumulate(first, dgt_ref, _colsum(dx * fv))

    row, vec = _row_spec(rb, d), _vec_spec(d)
    return _pcall(
        body, name="final_loss", grid=(s // rb,), in_specs=[row, row, vec, vec, row],
        out_specs=[row, row, _vec_spec(LANES), vec, vec],
        out_shape=[jax.ShapeDtypeStruct((s, d), F32), jax.ShapeDtypeStruct((s, d), BF16),
                   jax.ShapeDtypeStruct((1, LANES), F32), jax.ShapeDtypeStruct((1, d), F32),
                   jax.ShapeDtypeStruct((1, d), F32)],
        compiler_params=_params("arbitrary"),
    )(x1, f, gt2, g_final, target)


def _norm_mod_bwd(dh, xin, dres, g, sc, name, branch=None, gt=None):
    s, d = xin.shape
    rb = _pick(s, 256)
    has_branch = branch is not None

    def body(*refs):
        if has_branch:
            dh_ref, x_ref, dres_ref, g_ref, sc_ref, br_ref, gt_ref, dx_ref, dbr_ref, dsh_ref, dsc_ref, dg_ref, dgt_ref = refs
        else:
            dh_ref, x_ref, dres_ref, g_ref, sc_ref, dx_ref, dsh_ref, dsc_ref, dg_ref = refs
        first = pl.program_id(0) == 0
        gv = g_ref[...]
        xh, r = _rms(x_ref[...])
        dhv = dh_ref[...]
        dn = dhv * (1.0 + sc_ref[...])
        dxh = dn * gv
        dx = dres_ref[...] + r * (dxh - xh * jnp.mean(dxh * xh, axis=-1, keepdims=True))
        dx_ref[...] = dx
        _accumulate(first, dsh_ref, _colsum(dhv))
        _accumulate(first, dsc_ref, _colsum(dhv * xh * gv))
        _accumulate(first, dg_ref, _colsum(dn * xh))
        if has_branch:
            dbr_ref[...] = (dx * gt_ref[...]).astype(BF16)
            _accumulate(first, dgt_ref, _colsum(dx * br_ref[...]))

    row, vec = _row_spec(rb, d), _vec_spec(d)
    vec_shape = jax.ShapeDtypeStruct((1, d), F32)
    if has_branch:
        ins, in_specs = (dh, xin, dres, g, sc, branch, gt), [row, row, row, vec, vec, row, vec]
        out_specs = [row, row, vec, vec, vec, vec]
        out_shape = [jax.ShapeDtypeStruct((s, d), F32), jax.ShapeDtypeStruct((s, d), BF16)] + [vec_shape] * 4
    else:
        ins, in_specs = (dh, xin, dres, g, sc), [row, row, row, vec, vec]
        out_specs = [row, vec, vec, vec]
        out_shape = [jax.ShapeDtypeStruct((s, d), F32)] + [vec_shape] * 3
    return _pcall(body, name=name, grid=(s // rb,), in_specs=in_specs, out_specs=out_specs,
                          out_shape=out_shape, compiler_params=_params("arbitrary"))(*ins)


def _gate_fwd(pa, pb, z, b_gate, gate_col):
    s, d = pa.shape
    rb = _pick(s, 256)

    def body(pa_ref, pb_ref, za_ref, zb_ref, ba_ref, bb_ref, y_ref):
        ga = _sigmoid(za_ref[...] + ba_ref[...])
        gb = _sigmoid(zb_ref[...] + bb_ref[...])
        y_ref[...] = (ga * pa_ref[...] + gb * pb_ref[...]).astype(BF16)

    row = _row_spec(rb, d)
    return _pcall(
        body, name="gate_fwd", grid=(s // rb,),
        in_specs=[row, row, _row_spec(rb, d, gate_col), _row_spec(rb, d, gate_col + 1), _vec_spec(d, 0), _vec_spec(d, 1)],
        out_specs=row, out_shape=jax.ShapeDtypeStruct((s, d), BF16), compiler_params=_params("parallel"),
    )(pa, pb, z, z, b_gate, b_gate)


def _gate_bwd(dy, pa, pb, z, b_gate, gate_col):
    s, d = pa.shape
    rb = _pick(s, 256)

    def body(dy_ref, pa_ref, pb_ref, za_ref, zb_ref, ba_ref, bb_ref, dpa_ref, dpb_ref, dz_ref, db_ref):
        first = pl.program_id(0) == 0
        dyv = dy_ref[...]
        ga = _sigmoid(za_ref[...] + ba_ref[...])
        gb = _sigmoid(zb_ref[...] + bb_ref[...])
        dpa_ref[...] = (dyv * ga).astype(BF16)
        dpb_ref[...] = (dyv * gb).astype(BF16)
        dga = dyv * pa_ref[...] * ga * (1.0 - ga)
        dgb = dyv * pb_ref[...] * gb * (1.0 - gb)
        dz_ref[:, :d] = dga.astype(BF16)
        dz_ref[:, d:] = dgb.astype(BF16)

        @pl.when(first)
        def _():
            db_ref[:, :d] = _colsum(dga)
            db_ref[:, d:] = _colsum(dgb)

        @pl.when(jnp.logical_not(first))
        def _():
            db_ref[:, :d] += _colsum(dga)
            db_ref[:, d:] += _colsum(dgb)

    row = _row_spec(rb, d)
    return _pcall(
        body, name="gate_bwd", grid=(s // rb,),
        in_specs=[row, row, row, _row_spec(rb, d, gate_col), _row_spec(rb, d, gate_col + 1), _vec_spec(d, 0), _vec_spec(d, 1)],
        out_specs=[row, row, _row_spec(rb, 2 * d), _vec_spec(2 * d)],
        out_shape=[jax.ShapeDtypeStruct((s, d), BF16), jax.ShapeDtypeStruct((s, d), BF16),
                   jax.ShapeDtypeStruct((s, 2 * d), BF16), jax.ShapeDtypeStruct((1, 2 * d), F32)],
        compiler_params=_params("arbitrary"),
    )(dy, pa, pb, z, z, b_gate, b_gate)


def _swiglu_fwd(aup):
    s, ff2 = aup.shape
    ff = ff2 // 2
    rb = _pick(s, 128)

    def body(a_ref, u_ref, o_ref):
        a = a_ref[...]
        o_ref[...] = (a * _sigmoid(a) * u_ref[...]).astype(BF16)

    return _pcall(
        body, name="swiglu_fwd", grid=(s // rb,), in_specs=[_row_spec(rb, ff, 0), _row_spec(rb, ff, 1)],
        out_specs=_row_spec(rb, ff), out_shape=jax.ShapeDtypeStruct((s, ff), BF16),
        compiler_params=_params("parallel"),
    )(aup, aup)


def _swiglu_bwd(dhf, aup):
    s, ff2 = aup.shape
    ff = ff2 // 2
    rb = _pick(s, 128)

    def body(d_ref, a_ref, u_ref, o_ref):
        a, dv = a_ref[...], d_ref[...]
        sa = _sigmoid(a)
        o_ref[:, :ff] = (dv * u_ref[...] * sa * (1.0 + a * (1.0 - sa))).astype(BF16)
        o_ref[:, ff:] = (dv * a * sa).astype(BF16)

    return _pcall(
        body, name="swiglu_bwd", grid=(s // rb,),
        in_specs=[_row_spec(rb, ff), _row_spec(rb, ff, 0), _row_spec(rb, ff, 1)],
        out_specs=_row_spec(rb, ff2), out_shape=jax.ShapeDtypeStruct((s, ff2), BF16),
        compiler_params=_params("parallel"),
    )(dhf, aup, aup)


def _tril(n):
    return lax.broadcasted_iota(jnp.int32, (n, n), 0) >= lax.broadcasted_iota(jnp.int32, (n, n), 1)


def _gmlp_norm(v, ln_g, ln_b):
    gv = _gelu(v)
    mu = jnp.mean(gv, axis=-1, keepdims=True)
    cen = gv - mu
    rs = lax.rsqrt(jnp.mean(cen * cen, axis=-1, keepdims=True) + EPS)
    xh = cen * rs
    return xh, rs, xh * ln_g + ln_b


def _gmlp_fwd(z, ln_g, ln_b, ws, bs_t):
    s = z.shape[0]
    gw = ln_g.shape[1]
    groups, chunk, _ = ws.shape

    def body(u_ref, v_ref, lg_ref, lb_ref, ws_ref, bs_ref, ya_ref):
        gu = _gelu(u_ref[...])
        _, _, vn = _gmlp_norm(v_ref[...], lg_ref[...], lb_ref[...])
        mask = _tril(chunk)
        for g in range(groups):
            cols = slice(g * LANES, (g + 1) * LANES)
            wm = jnp.where(mask, ws_ref[g], 0.0).astype(BF16)
            sg = _dot(wm, vn[:, cols].astype(BF16)) + bs_ref[:, g:g + 1]
            ya_ref[:, cols] = (gu[:, cols] * sg).astype(BF16)

    return _pcall(
        body, name="gmlp_fwd", grid=(s // chunk,),
        in_specs=[_row_spec(chunk, gw, 0), _row_spec(chunk, gw, 1), _vec_spec(gw), _vec_spec(gw),
                  pl.BlockSpec((groups, chunk, chunk), lambda i: (0, 0, 0)), pl.BlockSpec((chunk, LANES), lambda i: (0, 0))],
        out_specs=_row_spec(chunk, gw), out_shape=jax.ShapeDtypeStruct((s, gw), BF16),
        compiler_params=_params("parallel"),
    )(z, z, ln_g, ln_b, ws, bs_t)


def _gmlp_bwd(dya, z, ln_g, ln_b, ws, bs_t):
    s = z.shape[0]
    gw = ln_g.shape[1]
    groups, chunk, _ = ws.shape

    def body(dya_ref, u_ref, v_ref, lg_ref, lb_ref, ws_ref, bs_ref, duv_ref, dws_ref, dbs_ref, dlg_ref, dlb_ref, dvn_ref):
        first = pl.program_id(0) == 0
        u, v, lg = u_ref[...], v_ref[...], lg_ref[...]
        gu = _gelu(u)
        xh, rs, vn = _gmlp_norm(v, lg, lb_ref[...])
        dyav = dya_ref[...]
        mask = _tril(chunk)
        lane = lax.broadcasted_iota(jnp.int32, (chunk, LANES), 1)
        dbs = jnp.zeros((chunk, LANES), F32)
        for g in range(groups):
            cols = slice(g * LANES, (g + 1) * LANES)
            wm = jnp.where(mask, ws_ref[g], 0.0).astype(BF16)
            vg = vn[:, cols].astype(BF16)
            sg = _dot(wm, vg) + bs_ref[:, g:g + 1]
            ds = dyav[:, cols] * gu[:, cols]
            duv_ref[:, cols] = (dyav[:, cols] * sg * _gelu_grad(u[:, cols])).astype(BF16)
            dsb = ds.astype(BF16)
            _accumulate(first, dws_ref.at[g], jnp.where(mask, _dot(dsb, vg, NT), 0.0))
            dbs = dbs + jnp.where(lane == g, jnp.sum(ds, axis=-1, keepdims=True), 0.0)
            dvn_ref[:, cols] = _dot(wm, dsb, TN)
        dvn = dvn_ref[...]
        _accumulate(first, dbs_ref, dbs)
        _accumulate(first, dlb_ref, _colsum(dvn))
        _accumulate(first, dlg_ref, _colsum(dvn * xh))
        dxh = dvn * lg
        dgv = rs * (dxh - jnp.mean(dxh, axis=-1, keepdims=True) - xh * jnp.mean(dxh * xh, axis=-1, keepdims=True))
        duv_ref[:, gw:] = (dgv * _gelu_grad(v)).astype(BF16)

    return _pcall(
        body, name="gmlp_bwd", grid=(s // chunk,),
        in_specs=[_row_spec(chunk, gw), _row_spec(chunk, gw, 0), _row_spec(chunk, gw, 1), _vec_spec(gw), _vec_spec(gw),
                  pl.BlockSpec((groups, chunk, chunk), lambda i: (0, 0, 0)), pl.BlockSpec((chunk, LANES), lambda i: (0, 0))],
        out_specs=[_row_spec(chunk, 2 * gw), pl.BlockSpec((groups, chunk, chunk), lambda i: (0, 0, 0)),
                   pl.BlockSpec((chunk, LANES), lambda i: (0, 0)), _vec_spec(gw), _vec_spec(gw)],
        out_shape=[jax.ShapeDtypeStruct((s, 2 * gw), BF16), jax.ShapeDtypeStruct((groups, chunk, chunk), F32),
                   jax.ShapeDtypeStruct((chunk, LANES), F32), jax.ShapeDtypeStruct((1, gw), F32),
                   jax.ShapeDtypeStruct((1, gw), F32)],
        scratch_shapes=[pltpu.VMEM((chunk, gw), F32)],
        compiler_params=_params("arbitrary"),
    )(dya, z, z, ln_g, ln_b, ws, bs_t)


def _lower_bound(lb_ref):
    a0, a1 = lb_ref[0:1, :], lb_ref[1:2, :]
    mx = jnp.maximum(a0, a1)
    e0, e1 = jnp.exp(a0 - mx), jnp.exp(a1 - mx)
    return e0 / (e0 + e1)


def _hg_chunk(q, fp, lb, tri_incl):
    t = q.shape[0]
    sig = _sigmoid(fp)
    f = lb + (1.0 - lb) * sig
    k = 1.0 - f
    sq = _sigmoid(q)
    qa = q * sq
    b = _dot(tri_incl, jnp.log(f), NN, lax.Precision.HIGHEST)
    row = lax.broadcasted_iota(jnp.int32, b.shape, 0)
    b_last = _colsum(jnp.where(row == t - 1, b, 0.0))
    b_mid = _colsum(jnp.where(row == t // 2, b, 0.0))
    e_q = jnp.exp(jnp.minimum(b - b_mid, EXP_CLAMP))
    e_k = jnp.exp(jnp.minimum(b_mid - b, EXP_CLAMP))
    e_in = jnp.exp(b)
    e_out = jnp.exp(b_last - b)
    return dict(sig=sig, f=f, k=k, sq=sq, qa=qa, b_last=b_last, e_q=e_q, e_k=e_k, e_in=e_in, e_out=e_out,
                q_hat=(qa * e_q).astype(BF16), k_hat=(k * e_k).astype(BF16),
                q_in=(qa * e_in).astype(BF16), k_out=k * e_out)


def _hgrn_fwd(z, hg_lb, norm_g, q_col):
    s = z.shape[0]
    hw = norm_g.shape[1]
    heads = hw // LANES
    t = HG_CHUNK
    rows = min(HG_ROWS, s)
    per_step = rows // t
    per = hw // LANES

    def zspec(which):
        return pl.BlockSpec((rows, LANES), lambda h, r, which=which: (r, q_col + which * per + h))

    def body(q_ref, f_ref, i_ref, g_ref, lb_ref, ng_ref, yb_ref, o_ref, st_out_ref, st_ref):
        @pl.when(pl.program_id(1) == 0)
        def _():
            st_ref[...] = jnp.zeros_like(st_ref)

        lb = _lower_bound(lb_ref)
        ng = ng_ref[...]
        tri = _tril(t)
        tri_f = tri.astype(F32)
        for j in range(per_step):
            rs_ = slice(j * t, (j + 1) * t)
            ch = _hg_chunk(q_ref[rs_, :], f_ref[rs_, :], lb, tri_f)
            iv = i_ref[rs_, :].astype(BF16)
            st = st_ref[...]
            st_out_ref[j] = st
            attn = jnp.where(tri, _dot(ch["q_hat"], ch["k_hat"], NT), 0.0)
            o = _dot(ch["q_in"], st.astype(BF16), NT) + _dot(attn.astype(BF16), iv)
            st_ref[...] = st * jnp.exp(ch["b_last"]) + _dot(iv, ch["k_out"].astype(BF16), TN)
            o_ref[rs_, :] = o
            og = g_ref[rs_, :]
            on, _ = _rms(o)
            yb_ref[rs_, :] = (on * ng * (og * _sigmoid(og))).astype(BF16)

    out_row = pl.BlockSpec((rows, LANES), lambda h, r: (r, h))
    return _pcall(
        body, name="hgrn_fwd", grid=(heads, s // rows),
        in_specs=[zspec(0), zspec(1), zspec(2), zspec(3),
                  pl.BlockSpec((2, LANES), lambda h, r: (0, h)), pl.BlockSpec((1, LANES), lambda h, r: (0, h))],
        out_specs=[out_row, out_row, pl.BlockSpec((None, per_step, LANES, LANES), lambda h, r: (h, r, 0, 0))],
        out_shape=[jax.ShapeDtypeStruct((s, hw), BF16), jax.ShapeDtypeStruct((s, hw), F32),
                   jax.ShapeDtypeStruct((heads, s // t, LANES, LANES), F32)],
        scratch_shapes=[pltpu.VMEM((LANES, LANES), F32)],
        compiler_params=_params("parallel", "arbitrary"),
    )(z, z, z, z, hg_lb, norm_g)


def _hgrn_bwd(dyb, z, o_raw, states, hg_lb, norm_g, q_col):
    s = z.shape[0]
    hw = norm_g.shape[1]
    heads = hw // LANES
    t = HG_CHUNK
    rows = min(HG_ROWS, s)
    per_step = rows // t
    per = hw // LANES
    n_steps = s // rows

    def zspec(which):
        return pl.BlockSpec((rows, LANES), lambda h, r, which=which: (n_steps - 1 - r, q_col + which * per + h))

    def body(dyb_ref, q_ref, f_ref, i_ref, g_ref, o_ref, st_in_ref, lb_ref, ng_ref,
             dq_ref, df_ref, di_ref, dg_ref, dlb_ref, dng_ref, dst_ref, acc_lb_ref, acc_ng_ref):
        step = pl.program_id(1)

        @pl.when(step == 0)
        def _():
            dst_ref[...] = jnp.zeros_like(dst_ref)
            acc_lb_ref[...] = jnp.zeros_like(acc_lb_ref)
            acc_ng_ref[...] = jnp.zeros_like(acc_ng_ref)

        lb = _lower_bound(lb_ref)
        ng = ng_ref[...]
        tri = _tril(t)
        tri_f = tri.astype(F32)
        tri_rev = (lax.broadcasted_iota(jnp.int32, (t, t), 0) <= lax.broadcasted_iota(jnp.int32, (t, t), 1)).astype(F32)
        last_row = lax.broadcasted_iota(jnp.int32, (t, LANES), 0) == t - 1
        for j in reversed(range(per_step)):
            rs_ = slice(j * t, (j + 1) * t)
            q = q_ref[rs_, :]
            ch = _hg_chunk(q, f_ref[rs_, :], lb, tri_f)
            ivf = i_ref[rs_, :]
            iv = ivf.astype(BF16)
            o, og, dy = o_ref[rs_, :], g_ref[rs_, :], dyb_ref[rs_, :]
            so = _sigmoid(og)
            on, r = _rms(o)
            acc_ng_ref[...] += _colsum(dy * on * (og * so))
            dg_ref[rs_, :] = (dy * on * ng * so * (1.0 + og * (1.0 - so))).astype(BF16)
            don = dy * ng * (og * so)
            do = (r * (don - on * jnp.mean(don * on, axis=-1, keepdims=True))).astype(BF16)
            st_prev = st_in_ref[j]
            dst = dst_ref[...]
            dst_b = dst.astype(BF16)
            attn = jnp.where(tri, _dot(ch["q_hat"], ch["k_hat"], NT), 0.0).astype(BF16)
            d_attn = jnp.where(tri, _dot(do, iv, NT), 0.0).astype(BF16)
            k_out = ch["k_out"]
            dq_in = _dot(do, st_prev.astype(BF16))
            dk_out = _dot(iv, dst_b)
            di_ref[rs_, :] = (_dot(attn, do, TN) + _dot(k_out.astype(BF16), dst_b, NT)).astype(BF16)
            dqa = dq_in * ch["e_in"] + _dot(d_attn, ch["k_hat"]) * ch["e_q"]
            dk = dk_out * ch["e_out"] + _dot(d_attn, ch["q_hat"], TN) * ch["e_k"]
            db = ch["qa"] * dqa - ch["k"] * dk
            e_last = jnp.exp(ch["b_last"])
            extra = _colsum(dk_out * k_out) + e_last * _colsum(st_prev * dst)
            db = jnp.where(last_row, db + extra, db)
            dlf = _dot(tri_rev, db, NN, lax.Precision.HIGHEST)
            dfv = dlf / ch["f"] - dk
            sig = ch["sig"]
            df_ref[rs_, :] = (dfv * (1.0 - lb) * sig * (1.0 - sig)).astype(BF16)
            acc_lb_ref[...] += _colsum(dfv * (1.0 - sig))
            sq = ch["sq"]
            dq_ref[rs_, :] = (dqa * sq * (1.0 + q * (1.0 - sq))).astype(BF16)
            dst_ref[...] = dst * e_last + _dot(do, ch["q_in"], TN)

        @pl.when(step == n_steps - 1)
        def _():
            d0 = acc_lb_ref[...] * lb * (1.0 - lb)
            dlb_ref[0:1, :] = d0
            dlb_ref[1:2, :] = -d0
            dng_ref[...] = acc_ng_ref[...]

    rev_row = pl.BlockSpec((rows, LANES), lambda h, r: (n_steps - 1 - r, h))
    piece = jax.ShapeDtypeStruct((s, hw), BF16)
    return _pcall(
        body, name="hgrn_bwd", grid=(heads, n_steps),
        in_specs=[rev_row, zspec(0), zspec(1), zspec(2), zspec(3), rev_row,
                  pl.BlockSpec((None, per_step, LANES, LANES), lambda h, r: (h, n_steps - 1 - r, 0, 0)),
                  pl.BlockSpec((2, LANES), lambda h, r: (0, h)), pl.BlockSpec((1, LANES), lambda h, r: (0, h))],
        out_specs=[rev_row, rev_row, rev_row, rev_row,
                   pl.BlockSpec((2, LANES), lambda h, r: (0, h)), pl.BlockSpec((1, LANES), lambda h, r: (0, h))],
        out_shape=[piece, piece, piece, piece, jax.ShapeDtypeStruct((2, hw), F32), jax.ShapeDtypeStruct((1, hw), F32)],
        scratch_shapes=[pltpu.VMEM((LANES, LANES), F32), pltpu.VMEM((1, LANES), F32), pltpu.VMEM((1, LANES), F32)],
        compiler_params=_params("parallel", "arbitrary"),
    )(dyb, z, z, z, z, o_raw, states, hg_lb, norm_g)


def _adamw(w, m, v, parts, name):
    rows, cols = w.shape
    n_parts = parts.shape[0]
    bc = _pick(cols, 1024)
    rb = _pick(rows, 256) if rows % LANES == 0 else rows
    m_corr = 1.0 - ADAM_B1 ** ADAM_STEP
    v_corr = 1.0 - ADAM_B2 ** ADAM_STEP

    def body(w_ref, m_ref, v_ref, p_ref, g_ref, d_ref, mo_ref, vo_ref):
        g = p_ref[0].astype(F32)
        for p in range(1, n_parts):
            g = g + p_ref[p].astype(F32)
        m2 = ADAM_B1 * m_ref[...] + (1.0 - ADAM_B1) * g
        v2 = ADAM_B2 * v_ref[...] + (1.0 - ADAM_B2) * (g * g)
        g_ref[...] = g
        mo_ref[...] = m2
        vo_ref[...] = v2
        d_ref[...] = -ADAM_LR * ((m2 / m_corr) / (jnp.sqrt(v2 / v_corr) + ADAM_EPS) + ADAM_WD * w_ref[...])

    blk = pl.BlockSpec((rb, bc), lambda i, j: (i, j))
    out = jax.ShapeDtypeStruct((rows, cols), F32)
    return _pcall(
        body, name=name, grid=(rows // rb, cols // bc),
        in_specs=[blk, blk, blk, pl.BlockSpec((n_parts, rb, bc), lambda i, j: (0, i, j))],
        out_specs=[blk] * 4, out_shape=[out] * 4, compiler_params=_params("parallel", "parallel"),
    )(w, m, v, parts)


SMALL = ("b_ada", "norm1_g", "b_gate", "gmlp_ln_g", "gmlp_ln_b", "gmlp_ws", "gmlp_bs", "hg_lb", "hg_norm_g",
         "norm2_g", "final_norm_g")
BIG = ("w_in", "w_branch_gmlp", "w_branch_hg", "w_out", "w_ffn_in", "w_ffn_out")
WEIGHTS = ("w_ada", "b_ada", "norm1_g", "w_in", "b_gate", "gmlp_ln_g", "gmlp_ln_b", "gmlp_ws", "gmlp_bs", "hg_lb",
           "hg_norm_g", "w_branch_gmlp", "w_branch_hg", "w_out", "norm2_g", "w_ffn_in", "w_ffn_out", "final_norm_g")


def _pack(parts):
    return jnp.concatenate([p.reshape(-1, LANES) for p in parts], axis=0)


def _step(x, c, loss_target, w, m, v):
    s, d = x.shape[1], x.shape[2]
    gw = w["gmlp_ln_g"].shape[-1]
    hw = w["hg_norm_g"].shape[-1]
    x2d, tgt = x[0], loss_target[0]
    mx, my, mc = lax.axis_index("x"), lax.axis_index("y"), lax.axis_index("c")
    chip = 2 * mx + my
    dev = 2 * chip + mc
    q_col = 2 * gw // LANES
    gate_col = (2 * gw + 4 * hw) // d
    place = jnp.stack([chip, mc]).astype(jnp.int32)
    _Order.last = None

    c_all = _all_gather8(c.reshape(-1, LANES), "gather_c").reshape(N_DEV, d)
    n_ada = w["w_ada"].shape[-1]
    b_ada_q = lax.dynamic_slice(w["b_ada"], (0, chip * n_ada), (1, n_ada))
    mod_q = _ada_fwd(c_all, w["w_ada"][0], b_ada_q)
    mod_all = _all_gather8(mod_q, "gather_mod")
    mod = lax.dynamic_index_in_dim(mod_all, dev, axis=1, keepdims=False)[::2].reshape(1, 6 * d)
    sh1, sc1, gt1, sh2, sc2, gt2 = [mod[:, i * d:(i + 1) * d] for i in range(6)]

    gather_in = _WeightGather("in", [w["w_in"][0]], chip)
    gather_mix = _WeightGather("mix", [w[n][0] for n in ("w_branch_gmlp", "w_branch_hg", "w_out")], chip)
    gather_fi = _WeightGather("fi", [w["w_ffn_in"][0]], chip)
    gather_fo = _WeightGather("fo", [w["w_ffn_out"][0]], chip)

    norm1_g, norm2_g, final_g = w["norm1_g"], w["norm2_g"], w["final_norm_g"].reshape(1, d)
    ln_g, ln_b = w["gmlp_ln_g"], w["gmlp_ln_b"]
    ws = w["gmlp_ws"][0]
    groups = ws.shape[0]
    bs_t = jnp.pad(w["gmlp_bs"][0].T, ((0, 0), (0, LANES - groups)))
    hg_lb, hg_ng, b_gate = w["hg_lb"], w["hg_norm_g"], w["b_gate"]

    h1 = _norm_mod_fwd(x2d, norm1_g, sc1, sh1, "norm1_fwd")
    gather_in.pass_on()
    w_in, = gather_in.done()
    z = _matmul(h1, w_in, mode="nn", name="mm_z", out_dtype=F32, b_slots=True, bn=1280)
    gather_mix.pass_on()
    ya = _gmlp_fwd(z, ln_g, ln_b, ws, bs_t)
    yb, o_raw, states = _hgrn_fwd(z, hg_lb, hg_ng, q_col)
    w_bg, w_bh, w_out = gather_mix.done()
    w_out = w_out.reshape(-1, w_out.shape[-1])
    pa = _matmul(ya, w_bg, mode="nn", name="mm_pa", out_dtype=F32, b_slots=True)
    pb = _matmul(yb, w_bh, mode="nn", name="mm_pb", out_dtype=F32, b_slots=True)
    y = _gate_fwd(pa, pb, z, b_gate, gate_col)
    gather_fi.pass_on()
    yo = _matmul(y, w_out, mode="nn", name="mm_yo", out_dtype=F32)
    x1, h2 = _norm_mod_fwd(x2d, norm2_g, sc2, sh2, "norm2_fwd", res=yo, gt=gt1)
    w_fi, = gather_fi.done()
    aup = _matmul(h2, w_fi, mode="nn", name="mm_aup", out_dtype=F32, b_slots=True, bn=1408)
    gather_fo.pass_on()
    hf = _swiglu_fwd(aup)
    w_fo, = gather_fo.done()
    w_fo = w_fo.reshape(-1, w_fo.shape[-1])
    ffn = _matmul(hf, w_fo, mode="nn", name="mm_ffn", out_dtype=F32)
    dx2, dffn, loss_row, d_final_g, d_gt2 = _final_loss(x1, ffn, gt2, final_g, tgt)

    g_fo = _matmul(hf, dffn, mode="tn", name="mm_g_fo", out_dtype=BF16, bm=1408)
    red_fo = _GradReduce("fo", [g_fo.reshape(N_CHIPS, -1, g_fo.shape[-1])])
    dhf = _matmul(dffn, w_fo, mode="nt", name="mm_dhf", out_dtype=F32, bn=1408)
    red_fo.step(place)
    daup = _swiglu_bwd(dhf, aup)
    g_fi = _matmul(h2, daup, mode="tn", name="mm_g_fi", out_dtype=BF16, out_slots=True, bn=1408)
    red_fi = _GradReduce("fi", [g_fi])
    dh2 = _matmul(daup, w_fi, mode="nt", name="mm_dh2", out_dtype=F32, b_slots=True, bk=1408)
    red_fi.step(place)
    dx1, dyo, d_sh2, d_sc2, d_norm2, d_gt1 = _norm_mod_bwd(dh2, x1, dx2, norm2_g, sc2, "norm2_bwd", branch=yo, gt=gt1)
    g_out = _matmul(y, dyo, mode="tn", name="mm_g_out", out_dtype=BF16)
    red_out = _GradReduce("out", [g_out.reshape(N_CHIPS, -1, g_out.shape[-1])])
    dy = _matmul(dyo, w_out, mode="nt", name="mm_dy", out_dtype=F32)
    red_out.step(place)
    dpa, dpb, dz_gate, d_b_gate = _gate_bwd(dy, pa, pb, z, b_gate, gate_col)
    g_bg = _matmul(ya, dpa, mode="tn", name="mm_g_bg", out_dtype=BF16, out_slots=True)
    g_bh = _matmul(yb, dpb, mode="tn", name="mm_g_bh", out_dtype=BF16, out_slots=True)
    red_br = _GradReduce("br", [g_bg, g_bh])
    dya = _matmul(dpa, w_bg, mode="nt", name="mm_dya", out_dtype=F32, b_slots=True)
    dyb = _matmul(dpb, w_bh, mode="nt", name="mm_dyb", out_dtype=F32, b_slots=True)
    red_br.step(place)
    dz_uv, d_ws, d_bs_t, d_ln_g, d_ln_b = _gmlp_bwd(dya, z, ln_g, ln_b, ws, bs_t)
    dz_q, dz_f, dz_i, dz_g, d_hg_lb, d_hg_ng = _hgrn_bwd(dyb, z, o_raw, states, hg_lb, hg_ng, q_col)
    dz = jnp.concatenate([dz_uv, dz_q, dz_f, dz_i, dz_g, dz_gate], axis=1)
    g_in = _matmul(h1, dz, mode="tn", name="mm_g_in", out_dtype=BF16, out_slots=True, bn=1280)
    red_in = _GradReduce("in", [g_in])
    dh1 = _matmul(dz, w_in, mode="nt", name="mm_dh1", out_dtype=F32, b_slots=True)
    red_in.pair(place)
    grad_x, d_sh1, d_sc1, d_norm1 = _norm_mod_bwd(dh1, x2d, dx1, norm1_g, sc1, "norm1_bwd")

    d_mod = jnp.concatenate([d_sh1, d_sc1, d_gt1, d_sh2, d_sc2, d_gt2], axis=1)
    small_part = {"b_ada": d_mod, "norm1_g": d_norm1, "b_gate": d_b_gate, "gmlp_ln_g": d_ln_g, "gmlp_ln_b": d_ln_b,
                  "gmlp_ws": d_ws, "gmlp_bs": d_bs_t[:, :groups].T, "hg_lb": d_hg_lb, "hg_norm_g": d_hg_ng,
                  "norm2_g": d_norm2, "final_norm_g": d_final_g}
    small_all = _all_gather8(_pack([small_part[n] for n in SMALL]), "gather_small")
    red_in.cross(after_last=True)
    d_mod_all = small_all[:, :6 * d // LANES].reshape(N_DEV, 6 * d)
    d_mod_q = lax.dynamic_slice(d_mod_all, (0, chip * n_ada), (N_DEV, n_ada))
    g_ada = _ada_bwd(c_all, d_mod_q)

    grad, delta, new_m, new_v = {}, {}, {}, {}

    def update(n, parts):
        outs = _adamw(w[n][0], m[n][0], v[n][0], parts, "adamw_" + n)
        grad[n], delta[n], new_m[n], new_v[n] = [o[None] for o in outs]

    update("w_ada", g_ada)
    outs = _adamw(_pack([w[n] for n in SMALL]), _pack([m[n] for n in SMALL]), _pack([v[n] for n in SMALL]),
                  small_all, "adamw_small")
    red_fo.join(place)
    red_fi.join(place)
    update("w_ffn_out", red_fo.done()[0][None])
    red_out.join(place)
    update("w_ffn_in", red_fi.done()[0][None])
    red_br.join(place)
    update("w_out", red_out.done()[0][None])
    red_in.join(place)
    g_bg, g_bh = red_br.done()
    update("w_branch_gmlp", g_bg[None])
    update("w_branch_hg", g_bh[None])
    update("w_in", red_in.done()[0][None])
    row = 0
    for n in SMALL:
        cnt = w[n].size // LANES
        for dst, o in zip((grad, delta, new_m, new_v), outs):
            dst[n] = o[row:row + cnt].reshape(w[n].shape)
        row += cnt

    loss = lax.psum(loss_row[0, 0], ("x", "y", "c"))
    return (loss, grad_x[None], *[grad[n] for n in WEIGHTS], *[delta[n] for n in WEIGHTS],
            *[new_m[n] for n in WEIGHTS], *[new_v[n] for n in WEIGHTS])


def kernel(x, c, w_ada, b_ada, norm1_g, w_in, b_gate, gmlp_ln_g, gmlp_ln_b, gmlp_ws, gmlp_bs, hg_lb, hg_norm_g, w_branch_gmlp, w_branch_hg, w_out, norm2_g, w_ffn_in, w_ffn_out, final_norm_g, loss_target, m_w_ada, m_b_ada, m_norm1_g, m_w_in, m_b_gate, m_gmlp_ln_g, m_gmlp_ln_b, m_gmlp_ws, m_gmlp_bs, m_hg_lb, m_hg_norm_g, m_w_branch_gmlp, m_w_branch_hg, m_w_out, m_norm2_g, m_w_ffn_in, m_w_ffn_out, m_final_norm_g, v_w_ada, v_b_ada, v_norm1_g, v_w_in, v_b_gate, v_gmlp_ln_g, v_gmlp_ln_b, v_gmlp_ws, v_gmlp_bs, v_hg_lb, v_hg_norm_g, v_w_branch_gmlp, v_w_branch_hg, v_w_out, v_norm2_g, v_w_ffn_in, v_w_ffn_out, v_final_norm_g):
    w = dict(w_ada=w_ada, b_ada=b_ada, norm1_g=norm1_g, w_in=w_in, b_gate=b_gate, gmlp_ln_g=gmlp_ln_g,
             gmlp_ln_b=gmlp_ln_b, gmlp_ws=gmlp_ws, gmlp_bs=gmlp_bs, hg_lb=hg_lb, hg_norm_g=hg_norm_g,
             w_branch_gmlp=w_branch_gmlp, w_branch_hg=w_branch_hg, w_out=w_out, norm2_g=norm2_g,
             w_ffn_in=w_ffn_in, w_ffn_out=w_ffn_out, final_norm_g=final_norm_g)
    m = dict(w_ada=m_w_ada, b_ada=m_b_ada, norm1_g=m_norm1_g, w_in=m_w_in, b_gate=m_b_gate, gmlp_ln_g=m_gmlp_ln_g,
             gmlp_ln_b=m_gmlp_ln_b, gmlp_ws=m_gmlp_ws, gmlp_bs=m_gmlp_bs, hg_lb=m_hg_lb, hg_norm_g=m_hg_norm_g,
             w_branch_gmlp=m_w_branch_gmlp, w_branch_hg=m_w_branch_hg, w_out=m_w_out, norm2_g=m_norm2_g,
             w_ffn_in=m_w_ffn_in, w_ffn_out=m_w_ffn_out, final_norm_g=m_final_norm_g)
    v = dict(w_ada=v_w_ada, b_ada=v_b_ada, norm1_g=v_norm1_g, w_in=v_w_in, b_gate=v_b_gate, gmlp_ln_g=v_gmlp_ln_g,
             gmlp_ln_b=v_gmlp_ln_b, gmlp_ws=v_gmlp_ws, gmlp_bs=v_gmlp_bs, hg_lb=v_hg_lb, hg_norm_g=v_hg_norm_g,
             w_branch_gmlp=v_w_branch_gmlp, w_branch_hg=v_w_branch_hg, w_out=v_w_out, norm2_g=v_norm2_g,
             w_ffn_in=v_w_ffn_in, w_ffn_out=v_w_ffn_out, final_norm_g=v_final_norm_g)
    return _step(x, c, loss_target, w, m, v)
```

```python
import functools

import jax
import jax.numpy as jnp
from jax import lax
from jax.experimental import pallas as pl
from jax.experimental.pallas import tpu as pltpu

F32 = jnp.float32
BF16 = jnp.bfloat16
EPS = 1e-6
LANES = 128
N_CHIPS = 4
N_DEV = 8
VMEM_LIMIT_BYTES = 56 * 1024 * 1024
HG_CHUNK = 32
HG_ROWS = 256
EXP_CLAMP = 80.0
ADAM_LR, ADAM_B1, ADAM_B2, ADAM_EPS, ADAM_WD, ADAM_STEP = 0.001, 0.9, 0.999, 1e-08, 0.01, 10
MESH = pl.DeviceIdType.MESH

NN = (((1,), (0,)), ((), ()))
NT = (((1,), (1,)), ((), ()))
TN = (((0,), (0,)), ((), ()))


def _dot(a, b, dims=NN, precision=None):
    return lax.dot_general(a, b, dims, precision=precision, preferred_element_type=F32)


def _params(*semantics):
    return pltpu.CompilerParams(dimension_semantics=semantics, vmem_limit_bytes=VMEM_LIMIT_BYTES)


class _Order:
    last = None


def _pcall(body, *, in_specs, out_specs, grid=(), scratch_shapes=(), num_scalar_prefetch=0, **kw):
    def run(*ins):
        deps = () if _Order.last is None else (_Order.last,)
        n_in, n_dep = len(ins), len(deps)

        def wrapped(*refs):
            body(*refs[:n_in], *refs[n_in + n_dep:])

        specs = list(in_specs) + [pl.BlockSpec(memory_space=pl.ANY)] * n_dep
        if num_scalar_prefetch:
            grid_spec = pltpu.PrefetchScalarGridSpec(
                num_scalar_prefetch=num_scalar_prefetch, grid=grid, in_specs=specs, out_specs=out_specs,
                scratch_shapes=scratch_shapes)
            outs = pl.pallas_call(wrapped, grid_spec=grid_spec, **kw)(*ins, *deps)
        else:
            outs = pl.pallas_call(wrapped, grid=grid, in_specs=specs, out_specs=out_specs,
                                  scratch_shapes=scratch_shapes, **kw)(*ins, *deps)
        _Order.last = jax.tree.leaves(outs)[0]
        return outs

    return run


def _pick_rows(dim, pref):
    best = None
    for cand in range(16, min(dim, pref) + 1, 16):
        if dim % cand == 0:
            best = cand
    assert best is not None, (dim, pref)
    return best


def _pick(dim, pref):
    if dim <= pref:
        return dim
    best = None
    for cand in range(LANES, pref + 1, LANES):
        if dim % cand == 0:
            best = cand
    assert best is not None, (dim, pref)
    return best


def _sigmoid(x):
    return 1.0 / (1.0 + jnp.exp(-x))


def _gelu(x):
    c = 0.7978845608028654
    return 0.5 * x * (1.0 + jnp.tanh(c * (x + 0.044715 * x * x * x)))


def _gelu_grad(x):
    c = 0.7978845608028654
    t = jnp.tanh(c * (x + 0.044715 * x * x * x))
    return 0.5 * (1.0 + t) + 0.5 * x * (1.0 - t * t) * c * (1.0 + 3.0 * 0.044715 * x * x)


def _rms(x):
    r = lax.rsqrt(jnp.mean(x * x, axis=-1, keepdims=True) + EPS)
    return x * r, r


def _colsum(x):
    return jnp.sum(x, axis=0, keepdims=True)


def _accumulate(first, ref, val):
    @pl.when(first)
    def _():
        ref[...] = val

    @pl.when(jnp.logical_not(first))
    def _():
        ref[...] += val


def _matmul(a, b, *, mode, name, out_dtype, b_slots=False, out_slots=False, bm=1024, bn=1024, bk=2816):
    if mode == "nn":
        m, k = a.shape
        n = b.shape[2] * N_CHIPS if b_slots else b.shape[1]
        per = b.shape[2] if b_slots else n
    elif mode == "nt":
        m, k = a.shape
        n = b.shape[1] if b_slots else b.shape[0]
        per = b.shape[2] if b_slots else k
    else:
        k, m = a.shape
        n = b.shape[1]
        per = n // N_CHIPS if out_slots else n
    bm = _pick(m, bm)
    if mode == "nt":
        bn, bk = _pick(n, bn), _pick(per, bk)
    else:
        bn, bk = _pick(per, bn), _pick(k, bk)
    nk = k // bk
    per_blocks = per // (bk if mode == "nt" else bn)
    dims = {"nn": NN, "nt": NT, "tn": TN}[mode]

    def body(a_ref, b_ref, o_ref, *acc):
        part = _dot(a_ref[...], b_ref[...], dims)
        if nk == 1:
            o_ref[...] = part.astype(o_ref.dtype)
            return
        acc_ref, kk = acc[0], pl.program_id(2)

        @pl.when(kk == 0)
        def _():
            acc_ref[...] = part

        @pl.when(jnp.logical_and(kk > 0, kk < nk - 1))
        def _():
            acc_ref[...] += part

        @pl.when(kk == nk - 1)
        def _():
            o_ref[...] = (acc_ref[...] + part).astype(o_ref.dtype)

    if mode == "nn":
        a_spec = pl.BlockSpec((bm, bk), lambda i, j, kk: (i, kk))
        if b_slots:
            b_spec = pl.BlockSpec((None, bk, bn), lambda i, j, kk: (j // per_blocks, kk, j % per_blocks))
        else:
            b_spec = pl.BlockSpec((bk, bn), lambda i, j, kk: (kk, j))
    elif mode == "nt":
        a_spec = pl.BlockSpec((bm, bk), lambda i, j, kk: (i, kk))
        if b_slots:
            b_spec = pl.BlockSpec((None, bn, bk), lambda i, j, kk: (kk // per_blocks, j, kk % per_blocks))
        else:
            b_spec = pl.BlockSpec((bn, bk), lambda i, j, kk: (j, kk))
    else:
        a_spec = pl.BlockSpec((bk, bm), lambda i, j, kk: (kk, i))
        b_spec = pl.BlockSpec((bk, bn), lambda i, j, kk: (kk, j))
    if out_slots:
        o_spec = pl.BlockSpec((None, bm, bn), lambda i, j, kk: (j // per_blocks, i, j % per_blocks))
        out_shape = jax.ShapeDtypeStruct((N_CHIPS, m, per), out_dtype)
    else:
        o_spec = pl.BlockSpec((bm, bn), lambda i, j, kk: (i, j))
        out_shape = jax.ShapeDtypeStruct((m, n), out_dtype)
    return _pcall(
        body, name=name, grid=(m // bm, n // bn, nk), in_specs=[a_spec, b_spec], out_specs=o_spec,
        out_shape=out_shape, scratch_shapes=[pltpu.VMEM((bm, bn), F32)] if nk > 1 else [],
        compiler_params=_params("parallel", "parallel", "arbitrary"),
    )(a, b)


def _place():
    x, y, c = lax.axis_index("x"), lax.axis_index("y"), lax.axis_index("c")
    chips = [(1 - x, y), (x, 1 - y), (1 - x, 1 - y)]
    return x, y, c, chips


def _all_gather8(block, name):
    def body(x_ref, out_ref, send_sems, recv_sems, local_sem):
        x, y, c, chips = _place()
        me, sibling = (x, y, c), (x, y, 1 - c)

        def slot(px, py, pc):
            return out_ref.at[4 * px + 2 * py + pc]

        def copy(k, blk, to, src=None):
            return pltpu.make_async_remote_copy(
                src_ref=slot(*blk) if src is None else src, dst_ref=slot(*blk),
                send_sem=send_sems.at[k], recv_sem=recv_sems.at[k], device_id=to, device_id_type=MESH)

        mine = pltpu.make_async_copy(x_ref, slot(*me), local_sem)
        mine.start()
        first = [copy(0, me, sibling, src=x_ref)]
        first += [copy(1 + j, me, (*chip, c), src=x_ref) for j, chip in enumerate(chips)]
        for cp in first:
            cp.start()
        passed = [copy(4 + j, (*chip, c), sibling) for j, chip in enumerate(chips)]
        for j, chip in enumerate(chips):
            copy(1 + j, (*chip, c), me).wait_recv()
            passed[j].start()
        copy(0, sibling, me).wait_recv()
        for j, chip in enumerate(chips):
            copy(4 + j, (*chip, 1 - c), me).wait_recv()
        for cp in first + passed:
            cp.wait_send()
        mine.wait()

    return _pcall(
        body, name=name, out_shape=jax.ShapeDtypeStruct((N_DEV,) + block.shape, block.dtype),
        in_specs=[pl.BlockSpec(memory_space=pltpu.VMEM)], out_specs=pl.BlockSpec(memory_space=pltpu.VMEM),
        scratch_shapes=[pltpu.SemaphoreType.DMA((7,)), pltpu.SemaphoreType.DMA((7,)), pltpu.SemaphoreType.DMA],
        compiler_params=pltpu.CompilerParams(vmem_limit_bytes=VMEM_LIMIT_BYTES),
    )(block)


HBM_SPEC = pl.BlockSpec(memory_space=pltpu.HBM)
SEM_SPEC = pl.BlockSpec(memory_space=pltpu.SEMAPHORE)
ANY_SPEC = pl.BlockSpec(memory_space=pl.ANY)
EFFECT = pltpu.SideEffectType.DATAFLOW_SIDE_EFFECTING


def _xfer_start(name, bufs, plan, n_copies, after_last=False):
    nb = len(bufs)
    deps = (_Order.last,) if after_last and _Order.last is not None else ()
    nd = len(deps)

    def body(*refs):
        send_sems, recv_sems = refs[nb + nd], refs[nb + nd + 1]
        token = refs[nb + nd + 2 + nb]
        for k, (src, dst, dev) in enumerate(plan(refs[:nb], *_place())):
            pltpu.make_async_remote_copy(src_ref=src, dst_ref=dst, send_sem=send_sems.at[k], recv_sem=recv_sems.at[k],
                                         device_id=dev, device_id_type=MESH).start()
        token[...] = jnp.zeros_like(token)

    outs = pl.pallas_call(
        body, name=name,
        out_shape=(pltpu.SemaphoreType.DMA((n_copies,)), pltpu.SemaphoreType.DMA((n_copies,)),
                   *[pltpu.HBM(b.shape, b.dtype) for b in bufs], jax.ShapeDtypeStruct((8, LANES), F32)),
        in_specs=[HBM_SPEC] * nb + [ANY_SPEC] * nd,
        out_specs=(SEM_SPEC, SEM_SPEC, *[HBM_SPEC] * nb, pl.BlockSpec(memory_space=pltpu.VMEM)),
        input_output_aliases={i: 2 + i for i in range(nb)},
        compiler_params=pltpu.CompilerParams(has_side_effects=EFFECT),
    )(*[pltpu.with_memory_space_constraint(b, pltpu.HBM) for b in bufs], *deps)
    _Order.last = outs[-1]
    return (outs[0], outs[1]), list(outs[2:2 + nb])


def _xfer_wait(name, sems, bufs, plan):
    nb = len(bufs)

    def body(*refs):
        send_sems, recv_sems = refs[nb], refs[nb + 1]
        for k, (src, dst, dev) in enumerate(plan(refs[:nb], *_place())):
            copy = pltpu.make_async_remote_copy(src_ref=src, dst_ref=dst, send_sem=send_sems.at[k],
                                                recv_sem=recv_sems.at[k], device_id=dev, device_id_type=MESH)
            copy.wait_send()
            copy.wait_recv()

    outs = pl.pallas_call(
        body, name=name, out_shape=tuple(pltpu.HBM(b.shape, b.dtype) for b in bufs),
        in_specs=[HBM_SPEC] * nb + [SEM_SPEC, SEM_SPEC, ANY_SPEC], out_specs=tuple([HBM_SPEC] * nb),
        input_output_aliases={i: i for i in range(nb)},
        compiler_params=pltpu.CompilerParams(has_side_effects=EFFECT),
    )(*bufs, *sems, _Order.last)
    _Order.last = outs[0]
    return list(outs)


def _half(ref, c, axis):
    rows = ref.shape[axis] // 2
    return pl.ds(c * rows, rows)


def _plan_weights_ici(n):
    def plan(refs, x, y, c, chips):
        out = []
        for w in range(n):
            region = refs[w].at[2 * x + y, _half(refs[w], c, 1), :]
            out += [(region, region, (*chip, c)) for chip in chips]
        return out
    return plan


def _plan_weights_d2d(n):
    def plan(refs, x, y, c, chips):
        out = []
        for w in range(n):
            rows = _half(refs[w], c, 1)
            for chip in chips:
                region = refs[w].at[2 * chip[0] + chip[1], rows, :]
                out.append((region, region, (x, y, 1 - c)))
        return out
    return plan


def _plan_grads_d2d(n):
    def plan(refs, x, y, c, chips):
        return [(refs[w].at[:, _half(refs[w], 1 - c, 1), :], refs[n + w], (x, y, 1 - c)) for w in range(n)]
    return plan


def _plan_grads_ici(n):
    def plan(refs, x, y, c, chips):
        out = []
        for w in range(n):
            out += [(refs[w].at[2 * chip[0] + chip[1]], refs[n + w].at[2 * x + y], (*chip, c)) for chip in chips]
        return out
    return plan


def _plan_final_d2d(n):
    def plan(refs, x, y, c, chips):
        out = []
        for w in range(n):
            region = refs[w].at[_half(refs[w], c, 0), :]
            out.append((region, region, (x, y, 1 - c)))
        return out
    return plan


def _stream_blocks(hr, cols):
    bc = cols if cols <= 4096 else _pick(cols, 4096)
    return _pick_rows(hr, max(16, (768 * 1024) // bc)), bc


def _pre_reduce(g, landed, place, name):
    _, rows, cols = g.shape
    hr = rows // 2
    rb, bc = _stream_blocks(hr, cols)
    nrb = hr // rb

    def body(place_ref, g_ref, l_ref, o_ref):
        o_ref[...] = (g_ref[...].astype(F32) + l_ref[...].astype(F32)).astype(o_ref.dtype)

    return _pcall(
        body, name=name, num_scalar_prefetch=1, grid=(N_CHIPS, nrb, cols // bc),
        in_specs=[pl.BlockSpec((None, rb, bc), lambda j, i, k, p: (j, p[1] * nrb + i, k)),
                  pl.BlockSpec((None, rb, bc), lambda j, i, k, p: (j, i, k))],
        out_specs=pl.BlockSpec((None, rb, bc), lambda j, i, k, p: (j, i, k)),
        out_shape=jax.ShapeDtypeStruct((N_CHIPS, hr, cols), g.dtype),
        compiler_params=_params("parallel", "parallel", "parallel"),
    )(place, g, landed)


def _sum_slots(mine, landed, place, name):
    _, hr, cols = mine.shape
    rb, bc = _stream_blocks(hr, cols)
    rb = _pick_rows(hr, max(16, rb // 2))
    nrb = hr // rb

    def body(place_ref, m_ref, l_ref, o_ref):
        chip = place_ref[0]
        own = m_ref[...].astype(F32)
        total = jnp.where(chip == 0, own, l_ref[0].astype(F32))
        for j in range(1, N_CHIPS):
            total = total + jnp.where(chip == j, own, l_ref[j].astype(F32))
        o_ref[...] = total

    return _pcall(
        body, name=name, num_scalar_prefetch=1, grid=(nrb, cols // bc),
        in_specs=[pl.BlockSpec((None, rb, bc), lambda i, k, p: (p[0], i, k)),
                  pl.BlockSpec((N_CHIPS, rb, bc), lambda i, k, p: (0, i, k))],
        out_specs=pl.BlockSpec((rb, bc), lambda i, k, p: (p[1] * nrb + i, k)),
        out_shape=jax.ShapeDtypeStruct((2 * hr, cols), F32),
        compiler_params=_params("parallel", "parallel"),
    )(place, mine, landed)


class _WeightGather:
    def __init__(self, tag, quarters, chip):
        self.tag, self.n = tag, len(quarters)
        zones = [lax.dynamic_update_slice(lax.empty((N_CHIPS,) + q.shape, BF16), q.astype(BF16)[None], (chip, 0, 0))
                 for q in quarters]
        self.plan = _plan_weights_ici(self.n)
        self.sems, self.bufs = _xfer_start("wici_start_" + tag, zones, self.plan, 3 * self.n, after_last=True)

    def pass_on(self):
        bufs = _xfer_wait("wici_wait_" + self.tag, self.sems, self.bufs, self.plan)
        self.plan = _plan_weights_d2d(self.n)
        self.sems, self.bufs = _xfer_start("wd2d_start_" + self.tag, bufs, self.plan, 3 * self.n)

    def done(self):
        return _xfer_wait("wd2d_wait_" + self.tag, self.sems, self.bufs, self.plan)


class _GradReduce:
    def __init__(self, tag, grads):
        self.tag, self.n = tag, len(grads)
        zones = [lax.empty((N_CHIPS, g.shape[1] // 2, g.shape[2]), g.dtype) for g in grads]
        self.plan = _plan_grads_d2d(self.n)
        self.sems, self.bufs = _xfer_start("gd2d_start_" + tag, list(grads) + zones, self.plan, self.n)

    def pair(self, place):
        n = self.n
        bufs = _xfer_wait("gd2d_wait_" + self.tag, self.sems, self.bufs, self.plan)
        self.halves = [_pre_reduce(bufs[w], bufs[n + w], place, f"pre_reduce_{self.tag}{w}") for w in range(n)]

    def cross(self, after_last=False):
        zones = [lax.empty(h.shape, h.dtype) for h in self.halves]
        self.plan = _plan_grads_ici(self.n)
        self.sems, self.bufs = _xfer_start("gici_start_" + self.tag, self.halves + zones, self.plan, 3 * self.n,
                                           after_last=after_last)

    def step(self, place):
        self.pair(place)
        self.cross()

    def join(self, place):
        n = self.n
        bufs = _xfer_wait("gici_wait_" + self.tag, self.sems, self.bufs, self.plan)
        sums = [_sum_slots(bufs[w], bufs[n + w], place, f"sum_slots_{self.tag}{w}") for w in range(n)]
        self.plan = _plan_final_d2d(n)
        self.sems, self.bufs = _xfer_start("gfin_start_" + self.tag, sums, self.plan, n)

    def done(self):
        return _xfer_wait("gfin_wait_" + self.tag, self.sems, self.bufs, self.plan)


def _ada_fwd(c_all, w_q, b_q):
    d, n = w_q.shape
    bn = _pick(n, 512)

    def body(c_ref, w_ref, b_ref, o_ref):
        cv = c_ref[...]
        act = cv * _sigmoid(cv)
        o_ref[...] = _dot(act, w_ref[...], NN, lax.Precision.HIGHEST) + b_ref[...]

    return _pcall(
        body, name="ada_fwd", grid=(n // bn,),
        in_specs=[pl.BlockSpec((N_DEV, d), lambda j: (0, 0)), pl.BlockSpec((d, bn), lambda j: (0, j)),
                  pl.BlockSpec((1, bn), lambda j: (0, j))],
        out_specs=pl.BlockSpec((N_DEV, bn), lambda j: (0, j)),
        out_shape=jax.ShapeDtypeStruct((N_DEV, n), F32), compiler_params=_params("parallel"),
    )(c_all, w_q, b_q)


def _ada_bwd(c_all, dmod_q):
    d = c_all.shape[1]
    n = dmod_q.shape[1]
    bn = _pick(n, 512)

    def body(c_ref, g_ref, o_ref):
        cv = c_ref[...]
        act = cv * _sigmoid(cv)
        o_ref[...] = _dot(act, g_ref[...], TN, lax.Precision.HIGHEST)

    return _pcall(
        body, name="ada_bwd", grid=(n // bn,),
        in_specs=[pl.BlockSpec((N_DEV, d), lambda j: (0, 0)), pl.BlockSpec((N_DEV, bn), lambda j: (0, j))],
        out_specs=pl.BlockSpec((None, d, bn), lambda j: (0, 0, j)),
        out_shape=jax.ShapeDtypeStruct((1, d, n), F32), compiler_params=_params("parallel"),
    )(c_all, dmod_q)


def _row_spec(rb, width, col=0):
    return pl.BlockSpec((rb, width), lambda i, col=col: (i, col))


def _vec_spec(width, col=0):
    return pl.BlockSpec((1, width), lambda i, col=col: (0, col))


def _norm_mod_fwd(x, g, sc, sh, name, res=None, gt=None):
    s, d = x.shape
    rb = _pick(s, 256)
    has_res = res is not None

    def body(*refs):
        if has_res:
            x_ref, res_ref, gt_ref, g_ref, sc_ref, sh_ref, x1_ref, h_ref = refs
            xv = x_ref[...] + gt_ref[...] * res_ref[...]
            x1_ref[...] = xv
        else:
            x_ref, g_ref, sc_ref, sh_ref, h_ref = refs
            xv = x_ref[...]
        xh, _ = _rms(xv)
        h_ref[...] = (xh * g_ref[...] * (1.0 + sc_ref[...]) + sh_ref[...]).astype(BF16)

    row, vec = _row_spec(rb, d), _vec_spec(d)
    if has_res:
        ins, in_specs = (x, res, gt, g, sc, sh), [row, row, vec, vec, vec, vec]
        out_shape = [jax.ShapeDtypeStruct((s, d), F32), jax.ShapeDtypeStruct((s, d), BF16)]
        out_specs = [row, row]
    else:
        ins, in_specs = (x, g, sc, sh), [row, vec, vec, vec]
        out_shape, out_specs = jax.ShapeDtypeStruct((s, d), BF16), row
    return _pcall(body, name=name, grid=(s // rb,), in_specs=in_specs, out_specs=out_specs,
                          out_shape=out_shape, compiler_params=_params("parallel"))(*ins)


def _final_loss(x1, f, gt2, g_final, target):
    s, d = x1.shape
    rb = _pick(s, 256)

    def body(x1_ref, f_ref, gt_ref, g_ref, t_ref, dx_ref, df_ref, loss_ref, dg_ref, dgt_ref):
        first = pl.program_id(0) == 0
        fv, gt, gv = f_ref[...], gt_ref[...], g_ref[...]
        x2 = x1_ref[...] + gt * fv
        xh, r = _rms(x2)
        err = xh * gv - t_ref[...]
        blk = 0.5 * jnp.sum(jnp.sum(err * err, axis=1, keepdims=True), axis=0, keepdims=True) / d
        dy = err / d
        dxh = dy * gv
        dx = r * (dxh - xh * jnp.mean(dxh * xh, axis=-1, keepdims=True))
        dx_ref[...] = dx
        df_ref[...] = (dx * gt).astype(BF16)
        _accumulate(first, loss_ref, jnp.broadcast_to(blk, (1, LANES)))
        _accumulate(first, dg_ref, _colsum(dy * xh))
        _accumulate(first, dgt_ref, _colsum(dx * fv))

    row, vec = _row_spec(rb, d), _vec_spec(d)
    return _pcall(
        body, name="final_loss", grid=(s // rb,), in_specs=[row, row, vec, vec, row],
        out_specs=[row, row, _vec_spec(LANES), vec, vec],
        out_shape=[jax.ShapeDtypeStruct((s, d), F32), jax.ShapeDtypeStruct((s, d), BF16),
                   jax.ShapeDtypeStruct((1, LANES), F32), jax.ShapeDtypeStruct((1, d), F32),
                   jax.ShapeDtypeStruct((1, d), F32)],
        compiler_params=_params("arbitrary"),
    )(x1, f, gt2, g_final, target)


def _norm_mod_bwd(dh, xin, dres, g, sc, name, branch=None, gt=None):
    s, d = xin.shape
    rb = _pick(s, 256)
    has_branch = branch is not None

    def body(*refs):
        if has_branch:
            dh_ref, x_ref, dres_ref, g_ref, sc_ref, br_ref, gt_ref, dx_ref, dbr_ref, dsh_ref, dsc_ref, dg_ref, dgt_ref = refs
        else:
            dh_ref, x_ref, dres_ref, g_ref, sc_ref, dx_ref, dsh_ref, dsc_ref, dg_ref = refs
        first = pl.program_id(0) == 0
        gv = g_ref[...]
        xh, r = _rms(x_ref[...])
        dhv = dh_ref[...]
        dn = dhv * (1.0 + sc_ref[...])
        dxh = dn * gv
        dx = dres_ref[...] + r * (dxh - xh * jnp.mean(dxh * xh, axis=-1, keepdims=True))
        dx_ref[...] = dx
        _accumulate(first, dsh_ref, _colsum(dhv))
        _accumulate(first, dsc_ref, _colsum(dhv * xh * gv))
        _accumulate(first, dg_ref, _colsum(dn * xh))
        if has_branch:
            dbr_ref[...] = (dx * gt_ref[...]).astype(BF16)
            _accumulate(first, dgt_ref, _colsum(dx * br_ref[...]))

    row, vec = _row_spec(rb, d), _vec_spec(d)
    vec_shape = jax.ShapeDtypeStruct((1, d), F32)
    if has_branch:
        ins, in_specs = (dh, xin, dres, g, sc, branch, gt), [row, row, row, vec, vec, row, vec]
        out_specs = [row, row, vec, vec, vec, vec]
        out_shape = [jax.ShapeDtypeStruct((s, d), F32), jax.ShapeDtypeStruct((s, d), BF16)] + [vec_shape] * 4
    else:
        ins, in_specs = (dh, xin, dres, g, sc), [row, row, row, vec, vec]
        out_specs = [row, vec, vec, vec]
        out_shape = [jax.ShapeDtypeStruct((s, d), F32)] + [vec_shape] * 3
    return _pcall(body, name=name, grid=(s // rb,), in_specs=in_specs, out_specs=out_specs,
                          out_shape=out_shape, compiler_params=_params("arbitrary"))(*ins)


def _gate_fwd(pa, pb, z, b_gate, gate_col):
    s, d = pa.shape
    rb = _pick(s, 256)

    def body(pa_ref, pb_ref, za_ref, zb_ref, ba_ref, bb_ref, y_ref):
        ga = _sigmoid(za_ref[...] + ba_ref[...])
        gb = _sigmoid(zb_ref[...] + bb_ref[...])
        y_ref[...] = (ga * pa_ref[...] + gb * pb_ref[...]).astype(BF16)

    row = _row_spec(rb, d)
    return _pcall(
        body, name="gate_fwd", grid=(s // rb,),
        in_specs=[row, row, _row_spec(rb, d, gate_col), _row_spec(rb, d, gate_col + 1), _vec_spec(d, 0), _vec_spec(d, 1)],
        out_specs=row, out_shape=jax.ShapeDtypeStruct((s, d), BF16), compiler_params=_params("parallel"),
    )(pa, pb, z, z, b_gate, b_gate)


def _gate_bwd(dy, pa, pb, z, b_gate, gate_col):
    s, d = pa.shape
    rb = _pick(s, 256)

    def body(dy_ref, pa_ref, pb_ref, za_ref, zb_ref, ba_ref, bb_ref, dpa_ref, dpb_ref, dz_ref, db_ref):
        first = pl.program_id(0) == 0
        dyv = dy_ref[...]
        ga = _sigmoid(za_ref[...] + ba_ref[...])
        gb = _sigmoid(zb_ref[...] + bb_ref[...])
        dpa_ref[...] = (dyv * ga).astype(BF16)
        dpb_ref[...] = (dyv * gb).astype(BF16)
        dga = dyv * pa_ref[...] * ga * (1.0 - ga)
        dgb = dyv * pb_ref[...] * gb * (1.0 - gb)
        dz_ref[:, :d] = dga.astype(BF16)
        dz_ref[:, d:] = dgb.astype(BF16)

        @pl.when(first)
        def _():
            db_ref[:, :d] = _colsum(dga)
            db_ref[:, d:] = _colsum(dgb)

        @pl.when(jnp.logical_not(first))
        def _():
            db_ref[:, :d] += _colsum(dga)
            db_ref[:, d:] += _colsum(dgb)

    row = _row_spec(rb, d)
    return _pcall(
        body, name="gate_bwd", grid=(s // rb,),
        in_specs=[row, row, row, _row_spec(rb, d, gate_col), _row_spec(rb, d, gate_col + 1), _vec_spec(d, 0), _vec_spec(d, 1)],
        out_specs=[row, row, _row_spec(rb, 2 * d), _vec_spec(2 * d)],
        out_shape=[jax.ShapeDtypeStruct((s, d), BF16), jax.ShapeDtypeStruct((s, d), BF16),
                   jax.ShapeDtypeStruct((s, 2 * d), BF16), jax.ShapeDtypeStruct((1, 2 * d), F32)],
        compiler_params=_params("arbitrary"),
    )(dy, pa, pb, z, z, b_gate, b_gate)


def _swiglu_fwd(aup):
    s, ff2 = aup.shape
    ff = ff2 // 2
    rb = _pick(s, 128)

    def body(a_ref, u_ref, o_ref):
        a = a_ref[...]
        o_ref[...] = (a * _sigmoid(a) * u_ref[...]).astype(BF16)

    return _pcall(
        body, name="swiglu_fwd", grid=(s // rb,), in_specs=[_row_spec(rb, ff, 0), _row_spec(rb, ff, 1)],
        out_specs=_row_spec(rb, ff), out_shape=jax.ShapeDtypeStruct((s, ff), BF16),
        compiler_params=_params("parallel"),
    )(aup, aup)


def _swiglu_bwd(dhf, aup):
    s, ff2 = aup.shape
    ff = ff2 // 2
    rb = _pick(s, 128)

    def body(d_ref, a_ref, u_ref, o_ref):
        a, dv = a_ref[...], d_ref[...]
        sa = _sigmoid(a)
        o_ref[:, :ff] = (dv * u_ref[...] * sa * (1.0 + a * (1.0 - sa))).astype(BF16)
        o_ref[:, ff:] = (dv * a * sa).astype(BF16)

    return _pcall(
        body, name="swiglu_bwd", grid=(s // rb,),
        in_specs=[_row_spec(rb, ff), _row_spec(rb, ff, 0), _row_spec(rb, ff, 1)],
        out_specs=_row_spec(rb, ff2), out_shape=jax.ShapeDtypeStruct((s, ff2), BF16),
        compiler_params=_params("parallel"),
    )(dhf, aup, aup)


def _tril(n):
    return lax.broadcasted_iota(jnp.int32, (n, n), 0) >= lax.broadcasted_iota(jnp.int32, (n, n), 1)


def _gmlp_norm(v, ln_g, ln_b):
    gv = _gelu(v)
    mu = jnp.mean(gv, axis=-1, keepdims=True)
    cen = gv - mu
    rs = lax.rsqrt(jnp.mean(cen * cen, axis=-1, keepdims=True) + EPS)
    xh = cen * rs
    return xh, rs, xh * ln_g + ln_b


def _gmlp_fwd(z, ln_g, ln_b, ws, bs_t):
    s = z.shape[0]
    gw = ln_g.shape[1]
    groups, chunk, _ = ws.shape

    def body(u_ref, v_ref, lg_ref, lb_ref, ws_ref, bs_ref, ya_ref):
        gu = _gelu(u_ref[...])
        _, _, vn = _gmlp_norm(v_ref[...], lg_ref[...], lb_ref[...])
        mask = _tril(chunk)
        for g in range(groups):
            cols = slice(g * LANES, (g + 1) * LANES)
            wm = jnp.where(mask, ws_ref[g], 0.0).astype(BF16)
            sg = _dot(wm, vn[:, cols].astype(BF16)) + bs_ref[:, g:g + 1]
            ya_ref[:, cols] = (gu[:, cols] * sg).astype(BF16)

    return _pcall(
        body, name="gmlp_fwd", grid=(s // chunk,),
        in_specs=[_row_spec(chunk, gw, 0), _row_spec(chunk, gw, 1), _vec_spec(gw), _vec_spec(gw),
                  pl.BlockSpec((groups, chunk, chunk), lambda i: (0, 0, 0)), pl.BlockSpec((chunk, LANES), lambda i: (0, 0))],
        out_specs=_row_spec(chunk, gw), out_shape=jax.ShapeDtypeStruct((s, gw), BF16),
        compiler_params=_params("parallel"),
    )(z, z, ln_g, ln_b, ws, bs_t)


def _gmlp_bwd(dya, z, ln_g, ln_b, ws, bs_t):
    s = z.shape[0]
    gw = ln_g.shape[1]
    groups, chunk, _ = ws.shape

    def body(dya_ref, u_ref, v_ref, lg_ref, lb_ref, ws_ref, bs_ref, duv_ref, dws_ref, dbs_ref, dlg_ref, dlb_ref, dvn_ref):
        first = pl.program_id(0) == 0
        u, v, lg = u_ref[...], v_ref[...], lg_ref[...]
        gu = _gelu(u)
        xh, rs, vn = _gmlp_norm(v, lg, lb_ref[...])
        dyav = dya_ref[...]
        mask = _tril(chunk)
        lane = lax.broadcasted_iota(jnp.int32, (chunk, LANES), 1)
        dbs = jnp.zeros((chunk, LANES), F32)
        for g in range(groups):
            cols = slice(g * LANES, (g + 1) * LANES)
            wm = jnp.where(mask, ws_ref[g], 0.0).astype(BF16)
            vg = vn[:, cols].astype(BF16)
            sg = _dot(wm, vg) + bs_ref[:, g:g + 1]
            ds = dyav[:, cols] * gu[:, cols]
            duv_ref[:, cols] = (dyav[:, cols] * sg * _gelu_grad(u[:, cols])).astype(BF16)
            dsb = ds.astype(BF16)
            _accumulate(first, dws_ref.at[g], jnp.where(mask, _dot(dsb, vg, NT), 0.0))
            dbs = dbs + jnp.where(lane == g, jnp.sum(ds, axis=-1, keepdims=True), 0.0)
            dvn_ref[:, cols] = _dot(wm, dsb, TN)
        dvn = dvn_ref[...]
        _accumulate(first, dbs_ref, dbs)
        _accumulate(first, dlb_ref, _colsum(dvn))
        _accumulate(first, dlg_ref, _colsum(dvn * xh))
        dxh = dvn * lg
        dgv = rs * (dxh - jnp.mean(dxh, axis=-1, keepdims=True) - xh * jnp.mean(dxh * xh, axis=-1, keepdims=True))
        duv_ref[:, gw:] = (dgv * _gelu_grad(v)).astype(BF16)

    return _pcall(
        body, name="gmlp_bwd", grid=(s // chunk,),
        in_specs=[_row_spec(chunk, gw), _row_spec(chunk, gw, 0), _row_spec(chunk, gw, 1), _vec_spec(gw), _vec_spec(gw),
                  pl.BlockSpec((groups, chunk, chunk), lambda i: (0, 0, 0)), pl.BlockSpec((chunk, LANES), lambda i: (0, 0))],
        out_specs=[_row_spec(chunk, 2 * gw), pl.BlockSpec((groups, chunk, chunk), lambda i: (0, 0, 0)),
                   pl.BlockSpec((chunk, LANES), lambda i: (0, 0)), _vec_spec(gw), _vec_spec(gw)],
        out_shape=[jax.ShapeDtypeStruct((s, 2 * gw), BF16), jax.ShapeDtypeStruct((groups, chunk, chunk), F32),
                   jax.ShapeDtypeStruct((chunk, LANES), F32), jax.ShapeDtypeStruct((1, gw), F32),
                   jax.ShapeDtypeStruct((1, gw), F32)],
        scratch_shapes=[pltpu.VMEM((chunk, gw), F32)],
        compiler_params=_params("arbitrary"),
    )(dya, z, z, ln_g, ln_b, ws, bs_t)


def _lower_bound(lb_ref):
    a0, a1 = lb_ref[0:1, :], lb_ref[1:2, :]
    mx = jnp.maximum(a0, a1)
    e0, e1 = jnp.exp(a0 - mx), jnp.exp(a1 - mx)
    return e0 / (e0 + e1)


def _hg_chunk(q, fp, lb, tri_incl):
    t = q.shape[0]
    sig = _sigmoid(fp)
    f = lb + (1.0 - lb) * sig
    k = 1.0 - f
    sq = _sigmoid(q)
    qa = q * sq
    b = _dot(tri_incl, jnp.log(f), NN, lax.Precision.HIGHEST)
    row = lax.broadcasted_iota(jnp.int32, b.shape, 0)
    b_last = _colsum(jnp.where(row == t - 1, b, 0.0))
    b_mid = _colsum(jnp.where(row == t // 2, b, 0.0))
    e_q = jnp.exp(jnp.minimum(b - b_mid, EXP_CLAMP))
    e_k = jnp.exp(jnp.minimum(b_mid - b, EXP_CLAMP))
    e_in = jnp.exp(b)
    e_out = jnp.exp(b_last - b)
    return dict(sig=sig, f=f, k=k, sq=sq, qa=qa, b_last=b_last, e_q=e_q, e_k=e_k, e_in=e_in, e_out=e_out,
                q_hat=(qa * e_q).astype(BF16), k_hat=(k * e_k).astype(BF16),
                q_in=(qa * e_in).astype(BF16), k_out=k * e_out)


def _hgrn_fwd(z, hg_lb, norm_g, q_col):
    s = z.shape[0]
    hw = norm_g.shape[1]
    heads = hw // LANES
    t = HG_CHUNK
    rows = min(HG_ROWS, s)
    per_step = rows // t
    per = hw // LANES

    def zspec(which):
        return pl.BlockSpec((rows, LANES), lambda h, r, which=which: (r, q_col + which * per + h))

    def body(q_ref, f_ref, i_ref, g_ref, lb_ref, ng_ref, yb_ref, o_ref, st_out_ref, st_ref):
        @pl.when(pl.program_id(1) == 0)
        def _():
            st_ref[...] = jnp.zeros_like(st_ref)

        lb = _lower_bound(lb_ref)
        ng = ng_ref[...]
        tri = _tril(t)
        tri_f = tri.astype(F32)
        for j in range(per_step):
            rs_ = slice(j * t, (j + 1) * t)
            ch = _hg_chunk(q_ref[rs_, :], f_ref[rs_, :], lb, tri_f)
            iv = i_ref[rs_, :].astype(BF16)
            st = st_ref[...]
            st_out_ref[j] = st
            attn = jnp.where(tri, _dot(ch["q_hat"], ch["k_hat"], NT), 0.0)
            o = _dot(ch["q_in"], st.astype(BF16), NT) + _dot(attn.astype(BF16), iv)
            st_ref[...] = st * jnp.exp(ch["b_last"]) + _dot(iv, ch["k_out"].astype(BF16), TN)
            o_ref[rs_, :] = o
            og = g_ref[rs_, :]
            on, _ = _rms(o)
            yb_ref[rs_, :] = (on * ng * (og * _sigmoid(og))).astype(BF16)

    out_row = pl.BlockSpec((rows, LANES), lambda h, r: (r, h))
    return _pcall(
        body, name="hgrn_fwd", grid=(heads, s // rows),
        in_specs=[zspec(0), zspec(1), zspec(2), zspec(3),
                  pl.BlockSpec((2, LANES), lambda h, r: (0, h)), pl.BlockSpec((1, LANES), lambda h, r: (0, h))],
        out_specs=[out_row, out_row, pl.BlockSpec((None, per_step, LANES, LANES), lambda h, r: (h, r, 0, 0))],
        out_shape=[jax.ShapeDtypeStruct((s, hw), BF16), jax.ShapeDtypeStruct((s, hw), F32),
                   jax.ShapeDtypeStruct((heads, s // t, LANES, LANES), F32)],
        scratch_shapes=[pltpu.VMEM((LANES, LANES), F32)],
        compiler_params=_params("parallel", "arbitrary"),
    )(z, z, z, z, hg_lb, norm_g)


def _hgrn_bwd(dyb, z, o_raw, states, hg_lb, norm_g, q_col):
    s = z.shape[0]
    hw = norm_g.shape[1]
    heads = hw // LANES
    t = HG_CHUNK
    rows = min(HG_ROWS, s)
    per_step = rows // t
    per = hw // LANES
    n_steps = s // rows

    def zspec(which):
        return pl.BlockSpec((rows, LANES), lambda h, r, which=which: (n_steps - 1 - r, q_col + which * per + h))

    def body(dyb_ref, q_ref, f_ref, i_ref, g_ref, o_ref, st_in_ref, lb_ref, ng_ref,
             dq_ref, df_ref, di_ref, dg_ref, dlb_ref, dng_ref, dst_ref, acc_lb_ref, acc_ng_ref):
        step = pl.program_id(1)

        @pl.when(step == 0)
        def _():
            dst_ref[...] = jnp.zeros_like(dst_ref)
            acc_lb_ref[...] = jnp.zeros_like(acc_lb_ref)
            acc_ng_ref[...] = jnp.zeros_like(acc_ng_ref)

        lb = _lower_bound(lb_ref)
        ng = ng_ref[...]
        tri = _tril(t)
        tri_f = tri.astype(F32)
        tri_rev = (lax.broadcasted_iota(jnp.int32, (t, t), 0) <= lax.broadcasted_iota(jnp.int32, (t, t), 1)).astype(F32)
        last_row = lax.broadcasted_iota(jnp.int32, (t, LANES), 0) == t - 1
        for j in reversed(range(per_step)):
            rs_ = slice(j * t, (j + 1) * t)
            q = q_ref[rs_, :]
            ch = _hg_chunk(q, f_ref[rs_, :], lb, tri_f)
            ivf = i_ref[rs_, :]
            iv = ivf.astype(BF16)
            o, og, dy = o_ref[rs_, :], g_ref[rs_, :], dyb_ref[rs_, :]
            so = _sigmoid(og)
            on, r = _rms(o)
            acc_ng_ref[...] += _colsum(dy * on * (og * so))
            dg_ref[rs_, :] = (dy * on * ng * so * (1.0 + og * (1.0 - so))).astype(BF16)
            don = dy * ng * (og * so)
            do = (r * (don - on * jnp.mean(don * on, axis=-1, keepdims=True))).astype(BF16)
            st_prev = st_in_ref[j]
            dst = dst_ref[...]
            dst_b = dst.astype(BF16)
            attn = jnp.where(tri, _dot(ch["q_hat"], ch["k_hat"], NT), 0.0).astype(BF16)
            d_attn = jnp.where(tri, _dot(do, iv, NT), 0.0).astype(BF16)
            k_out = ch["k_out"]
            dq_in = _dot(do, st_prev.astype(BF16))
            dk_out = _dot(iv, dst_b)
            di_ref[rs_, :] = (_dot(attn, do, TN) + _dot(k_out.astype(BF16), dst_b, NT)).astype(BF16)
            dqa = dq_in * ch["e_in"] + _dot(d_attn, ch["k_hat"]) * ch["e_q"]
            dk = dk_out * ch["e_out"] + _dot(d_attn, ch["q_hat"], TN) * ch["e_k"]
            db = ch["qa"] * dqa - ch["k"] * dk
            e_last = jnp.exp(ch["b_last"])
            extra = _colsum(dk_out * k_out) + e_last * _colsum(st_prev * dst)
            db = jnp.where(last_row, db + extra, db)
            dlf = _dot(tri_rev, db, NN, lax.Precision.HIGHEST)
            dfv = dlf / ch["f"] - dk
            sig = ch["sig"]
            df_ref[rs_, :] = (dfv * (1.0 - lb) * sig * (1.0 - sig)).astype(BF16)
            acc_lb_ref[...] += _colsum(dfv * (1.0 - sig))
            sq = ch["sq"]
            dq_ref[rs_, :] = (dqa * sq * (1.0 + q * (1.0 - sq))).astype(BF16)
            dst_ref[...] = dst * e_last + _dot(do, ch["q_in"], TN)

        @pl.when(step == n_steps - 1)
        def _():
            d0 = acc_lb_ref[...] * lb * (1.0 - lb)
            dlb_ref[0:1, :] = d0
            dlb_ref[1:2, :] = -d0
            dng_ref[...] = acc_ng_ref[...]

    rev_row = pl.BlockSpec((rows, LANES), lambda h, r: (n_steps - 1 - r, h))
    piece = jax.ShapeDtypeStruct((s, hw), BF16)
    return _pcall(
        body, name="hgrn_bwd", grid=(heads, n_steps),
        in_specs=[rev_row, zspec(0), zspec(1), zspec(2), zspec(3), rev_row,
                  pl.BlockSpec((None, per_step, LANES, LANES), lambda h, r: (h, n_steps - 1 - r, 0, 0)),
                  pl.BlockSpec((2, LANES), lambda h, r: (0, h)), pl.BlockSpec((1, LANES), lambda h, r: (0, h))],
        out_specs=[rev_row, rev_row, rev_row, rev_row,
                   pl.BlockSpec((2, LANES), lambda h, r: (0, h)), pl.BlockSpec((1, LANES), lambda h, r: (0, h))],
        out_shape=[piece, piece, piece, piece, jax.ShapeDtypeStruct((2, hw), F32), jax.ShapeDtypeStruct((1, hw), F32)],
        scratch_shapes=[pltpu.VMEM((LANES, LANES), F32), pltpu.VMEM((1, LANES), F32), pltpu.VMEM((1, LANES), F32)],
        compiler_params=_params("parallel", "arbitrary"),
    )(dyb, z, z, z, z, o_raw, states, hg_lb, norm_g)


def _adamw(w, m, v, parts, name):
    rows, cols = w.shape
    n_parts = parts.shape[0]
    bc = _pick(cols, 1024)
    rb = _pick(rows, 256) if rows % LANES == 0 else rows
    m_corr = 1.0 - ADAM_B1 ** ADAM_STEP
    v_corr = 1.0 - ADAM_B2 ** ADAM_STEP

    def body(w_ref, m_ref, v_ref, p_ref, g_ref, d_ref, mo_ref, vo_ref):
        g = p_ref[0].astype(F32)
        for p in range(1, n_parts):
            g = g + p_ref[p].astype(F32)
        m2 = ADAM_B1 * m_ref[...] + (1.0 - ADAM_B1) * g
        v2 = ADAM_B2 * v_ref[...] + (1.0 - ADAM_B2) * (g * g)
        g_ref[...] = g
        mo_ref[...] = m2
        vo_ref[...] = v2
        d_ref[...] = -ADAM_LR * ((m2 / m_corr) / (jnp.sqrt(v2 / v_corr) + ADAM_EPS) + ADAM_WD * w_ref[...])

    blk = pl.BlockSpec((rb, bc), lambda i, j: (i, j))
    out = jax.ShapeDtypeStruct((rows, cols), F32)
    return _pcall(
        body, name=name, grid=(rows // rb, cols // bc),
        in_specs=[blk, blk, blk, pl.BlockSpec((n_parts, rb, bc), lambda i, j: (0, i, j))],
        out_specs=[blk] * 4, out_shape=[out] * 4, compiler_params=_params("parallel", "parallel"),
    )(w, m, v, parts)


SMALL = ("b_ada", "norm1_g", "b_gate", "gmlp_ln_g", "gmlp_ln_b", "gmlp_ws", "gmlp_bs", "hg_lb", "hg_norm_g",
         "norm2_g", "final_norm_g")
BIG = ("w_in", "w_branch_gmlp", "w_branch_hg", "w_out", "w_ffn_in", "w_ffn_out")
WEIGHTS = ("w_ada", "b_ada", "norm1_g", "w_in", "b_gate", "gmlp_ln_g", "gmlp_ln_b", "gmlp_ws", "gmlp_bs", "hg_lb",
           "hg_norm_g", "w_branch_gmlp", "w_branch_hg", "w_out", "norm2_g", "w_ffn_in", "w_ffn_out", "final_norm_g")


def _pack(parts):
    return jnp.concatenate([p.reshape(-1, LANES) for p in parts], axis=0)


def _step(x, c, loss_target, w, m, v):
    s, d = x.shape[1], x.shape[2]
    gw = w["gmlp_ln_g"].shape[-1]
    hw = w["hg_norm_g"].shape[-1]
    x2d, tgt = x[0], loss_target[0]
    mx, my, mc = lax.axis_index("x"), lax.axis_index("y"), lax.axis_index("c")
    chip = 2 * mx + my
    dev = 2 * chip + mc
    q_col = 2 * gw // LANES
    gate_col = (2 * gw + 4 * hw) // d
    place = jnp.stack([chip, mc]).astype(jnp.int32)
    _Order.last = None

    c_all = _all_gather8(c.reshape(-1, LANES), "gather_c").reshape(N_DEV, d)
    n_ada = w["w_ada"].shape[-1]
    b_ada_q = lax.dynamic_slice(w["b_ada"], (0, chip * n_ada), (1, n_ada))
    mod_q = _ada_fwd(c_all, w["w_ada"][0], b_ada_q)
    mod_all = _all_gather8(mod_q, "gather_mod")
    mod = lax.dynamic_index_in_dim(mod_all, dev, axis=1, keepdims=False)[::2].reshape(1, 6 * d)
    sh1, sc1, gt1, sh2, sc2, gt2 = [mod[:, i * d:(i + 1) * d] for i in range(6)]

    gather_in = _WeightGather("in", [w["w_in"][0]], chip)
    gather_mix = _WeightGather("mix", [w[n][0] for n in ("w_branch_gmlp", "w_branch_hg", "w_out")], chip)
    gather_fi = _WeightGather("fi", [w["w_ffn_in"][0]], chip)
    gather_fo = _WeightGather("fo", [w["w_ffn_out"][0]], chip)

    norm1_g, norm2_g, final_g = w["norm1_g"], w["norm2_g"], w["final_norm_g"].reshape(1, d)
    ln_g, ln_b = w["gmlp_ln_g"], w["gmlp_ln_b"]
    ws = w["gmlp_ws"][0]
    groups = ws.shape[0]
    bs_t = jnp.pad(w["gmlp_bs"][0].T, ((0, 0), (0, LANES - groups)))
    hg_lb, hg_ng, b_gate = w["hg_lb"], w["hg_norm_g"], w["b_gate"]

    h1 = _norm_mod_fwd(x2d, norm1_g, sc1, sh1, "norm1_fwd")
    gather_in.pass_on()
    w_in, = gather_in.done()
    z = _matmul(h1, w_in, mode="nn", name="mm_z", out_dtype=F32, b_slots=True, bn=1280)
    gather_mix.pass_on()
    ya = _gmlp_fwd(z, ln_g, ln_b, ws, bs_t)
    yb, o_raw, states = _hgrn_fwd(z, hg_lb, hg_ng, q_col)
    w_bg, w_bh, w_out = gather_mix.done()
    w_out = w_out.reshape(-1, w_out.shape[-1])
    pa = _matmul(ya, w_bg, mode="nn", name="mm_pa", out_dtype=F32, b_slots=True)
    pb = _matmul(yb, w_bh, mode="nn", name="mm_pb", out_dtype=F32, b_slots=True)
    y = _gate_fwd(pa, pb, z, b_gate, gate_col)
    gather_fi.pass_on()
    yo = _matmul(y, w_out, mode="nn", name="mm_yo", out_dtype=F32)
    x1, h2 = _norm_mod_fwd(x2d, norm2_g, sc2, sh2, "norm2_fwd", res=yo, gt=gt1)
    w_fi, = gather_fi.done()
    aup = _matmul(h2, w_fi, mode="nn", name="mm_aup", out_dtype=F32, b_slots=True, bn=1408)
    gather_fo.pass_on()
    hf = _swiglu_fwd(aup)
    w_fo, = gather_fo.done()
    w_fo = w_fo.reshape(-1, w_fo.shape[-1])
    ffn = _matmul(hf, w_fo, mode="nn", name="mm_ffn", out_dtype=F32)
    dx2, dffn, loss_row, d_final_g, d_gt2 = _final_loss(x1, ffn, gt2, final_g, tgt)

    g_fo = _matmul(hf, dffn, mode="tn", name="mm_g_fo", out_dtype=BF16, bm=1408)
    red_fo = _GradReduce("fo", [g_fo.reshape(N_CHIPS, -1, g_fo.shape[-1])])
    dhf = _matmul(dffn, w_fo, mode="nt", name="mm_dhf", out_dtype=F32, bn=1408)
    red_fo.step(place)
    daup = _swiglu_bwd(dhf, aup)
    g_fi = _matmul(h2, daup, mode="tn", name="mm_g_fi", out_dtype=BF16, out_slots=True, bn=1408)
    red_fi = _GradReduce("fi", [g_fi])
    dh2 = _matmul(daup, w_fi, mode="nt", name="mm_dh2", out_dtype=F32, b_slots=True)
    red_fi.step(place)
    dx1, dyo, d_sh2, d_sc2, d_norm2, d_gt1 = _norm_mod_bwd(dh2, x1, dx2, norm2_g, sc2, "norm2_bwd", branch=yo, gt=gt1)
    g_out = _matmul(y, dyo, mode="tn", name="mm_g_out", out_dtype=BF16)
    red_out = _GradReduce("out", [g_out.reshape(N_CHIPS, -1, g_out.shape[-1])])
    dy = _matmul(dyo, w_out, mode="nt", name="mm_dy", out_dtype=F32)
    red_out.step(place)
    dpa, dpb, dz_gate, d_b_gate = _gate_bwd(dy, pa, pb, z, b_gate, gate_col)
    g_bg = _matmul(ya, dpa, mode="tn", name="mm_g_bg", out_dtype=BF16, out_slots=True)
    g_bh = _matmul(yb, dpb, mode="tn", name="mm_g_bh", out_dtype=BF16, out_slots=True)
    red_br = _GradReduce("br", [g_bg, g_bh])
    dya = _matmul(dpa, w_bg, mode="nt", name="mm_dya", out_dtype=F32, b_slots=True)
    dyb = _matmul(dpb, w_bh, mode="nt", name="mm_dyb", out_dtype=F32, b_slots=True)
    red_br.step(place)
    dz_uv, d_ws, d_bs_t, d_ln_g, d_ln_b = _gmlp_bwd(dya, z, ln_g, ln_b, ws, bs_t)
    dz_q, dz_f, dz_i, dz_g, d_hg_lb, d_hg_ng = _hgrn_bwd(dyb, z, o_raw, states, hg_lb, hg_ng, q_col)
    dz = jnp.concatenate([dz_uv, dz_q, dz_f, dz_i, dz_g, dz_gate], axis=1)
    g_in = _matmul(h1, dz, mode="tn", name="mm_g_in", out_dtype=BF16, out_slots=True, bn=1280)
    red_in = _GradReduce("in", [g_in])
    dh1 = _matmul(dz, w_in, mode="nt", name="mm_dh1", out_dtype=F32, b_slots=True)
    red_in.pair(place)
    grad_x, d_sh1, d_sc1, d_norm1 = _norm_mod_bwd(dh1, x2d, dx1, norm1_g, sc1, "norm1_bwd")

    d_mod = jnp.concatenate([d_sh1, d_sc1, d_gt1, d_sh2, d_sc2, d_gt2], axis=1)
    small_part = {"b_ada": d_mod, "norm1_g": d_norm1, "b_gate": d_b_gate, "gmlp_ln_g": d_ln_g, "gmlp_ln_b": d_ln_b,
                  "gmlp_ws": d_ws, "gmlp_bs": d_bs_t[:, :groups].T, "hg_lb": d_hg_lb, "hg_norm_g": d_hg_ng,
                  "norm2_g": d_norm2, "final_norm_g": d_final_g}
    small_all = _all_gather8(_pack([small_part[n] for n in SMALL]), "gather_small")
    red_in.cross(after_last=True)
    d_mod_all = small_all[:, :6 * d // LANES].reshape(N_DEV, 6 * d)
    d_mod_q = lax.dynamic_slice(d_mod_all, (0, chip * n_ada), (N_DEV, n_ada))
    g_ada = _ada_bwd(c_all, d_mod_q)

    grad, delta, new_m, new_v = {}, {}, {}, {}

    def update(n, parts):
        outs = _adamw(w[n][0], m[n][0], v[n][0], parts, "adamw_" + n)
        grad[n], delta[n], new_m[n], new_v[n] = [o[None] for o in outs]

    update("w_ada", g_ada)
    outs = _adamw(_pack([w[n] for n in SMALL]), _pack([m[n] for n in SMALL]), _pack([v[n] for n in SMALL]),
                  small_all, "adamw_small")
    red_fo.join(place)
    red_fi.join(place)
    update("w_ffn_out", red_fo.done()[0][None])
    red_out.join(place)
    update("w_ffn_in", red_fi.done()[0][None])
    red_br.join(place)
    update("w_out", red_out.done()[0][None])
    red_in.join(place)
    g_bg, g_bh = red_br.done()
    update("w_branch_gmlp", g_bg[None])
    update("w_branch_hg", g_bh[None])
    update("w_in", red_in.done()[0][None])
    row = 0
    for n in SMALL:
        cnt = w[n].size // LANES
        for dst, o in zip((grad, delta, new_m, new_v), outs):
            dst[n] = o[row:row + cnt].reshape(w[n].shape)
        row += cnt

    loss = lax.psum(loss_row[0, 0], ("x", "y", "c"))
    return (loss, grad_x[None], *[grad[n] for n in WEIGHTS], *[delta[n] for n in WEIGHTS],
            *[new_m[n] for n in WEIGHTS], *[new_v[n] for n in WEIGHTS])


def kernel(x, c, w_ada, b_ada, norm1_g, w_in, b_gate, gmlp_ln_g, gmlp_ln_b, gmlp_ws, gmlp_bs, hg_lb, hg_norm_g, w_branch_gmlp, w_branch_hg, w_out, norm2_g, w_ffn_in, w_ffn_out, final_norm_g, loss_target, m_w_ada, m_b_ada, m_norm1_g, m_w_in, m_b_gate, m_gmlp_ln_g, m_gmlp_ln_b, m_gmlp_ws, m_gmlp_bs, m_hg_lb, m_hg_norm_g, m_w_branch_gmlp, m_w_branch_hg, m_w_out, m_norm2_g, m_w_ffn_in, m_w_ffn_out, m_final_norm_g, v_w_ada, v_b_ada, v_norm1_g, v_w_in, v_b_gate, v_gmlp_ln_g, v_gmlp_ln_b, v_gmlp_ws, v_gmlp_bs, v_hg_lb, v_hg_norm_g, v_w_branch_gmlp, v_w_branch_hg, v_w_out, v_norm2_g, v_w_ffn_in, v_w_ffn_out, v_final_norm_g):
    w = dict(w_ada=w_ada, b_ada=b_ada, norm1_g=norm1_g, w_in=w_in, b_gate=b_gate, gmlp_ln_g=gmlp_ln_g,
             gmlp_ln_b=gmlp_ln_b, gmlp_ws=gmlp_ws, gmlp_bs=gmlp_bs, hg_lb=hg_lb, hg_norm_g=hg_norm_g,
             w_branch_gmlp=w_branch_gmlp, w_branch_hg=w_branch_hg, w_out=w_out, norm2_g=norm2_g,
             w_ffn_in=w_ffn_in, w_ffn_out=w_ffn_out, final_norm_g=final_norm_g)
    m = dict(w_ada=m_w_ada, b_ada=m_b_ada, norm1_g=m_norm1_g, w_in=m_w_in, b_gate=m_b_gate, gmlp_ln_g=m_gmlp_ln_g,
             gmlp_ln_b=m_gmlp_ln_b, gmlp_ws=m_gmlp_ws, gmlp_bs=m_gmlp_bs, hg_lb=m_hg_lb, hg_norm_g=m_hg_norm_g,
             w_branch_gmlp=m_w_branch_gmlp, w_branch_hg=m_w_branch_hg, w_out=m_w_out, norm2_g=m_norm2_g,
             w_ffn_in=m_w_ffn_in, w_ffn_out=m_w_ffn_out, final_norm_g=m_final_norm_g)
    v = dict(w_ada=v_w_ada, b_ada=v_b_ada, norm1_g=v_norm1_g, w_in=v_w_in, b_gate=v_b_gate, gmlp_ln_g=v_gmlp_ln_g,
             gmlp_ln_b=v_gmlp_ln_b, gmlp_ws=v_gmlp_ws, gmlp_bs=v_gmlp_bs, hg_lb=v_hg_lb, hg_norm_g=v_hg_norm_g,
             w_branch_gmlp=v_w_branch_gmlp, w_branch_hg=v_w_branch_hg, w_out=v_w_out, norm2_g=v_norm2_g,
             w_ffn_in=v_w_ffn_in, w_ffn_out=v_w_ffn_out, final_norm_g=v_final_norm_g)
    return _step(x, c, loss_target, w, m, v)
```

```python
import functools

import jax
import jax.numpy as jnp
from jax import lax
from jax.experimental import pallas as pl
from jax.experimental.pallas import tpu as pltpu

F32 = jnp.float32
BF16 = jnp.bfloat16
EPS = 1e-6
LANES = 128
N_CHIPS = 4
N_DEV = 8
VMEM_LIMIT_BYTES = 56 * 1024 * 1024
HG_CHUNK = 32
HG_ROWS = 256
EXP_CLAMP = 80.0
ADAM_LR, ADAM_B1, ADAM_B2, ADAM_EPS, ADAM_WD, ADAM_STEP = 0.001, 0.9, 0.999, 1e-08, 0.01, 10
MESH = pl.DeviceIdType.MESH

NN = (((1,), (0,)), ((), ()))
NT = (((1,), (1,)), ((), ()))
TN = (((0,), (0,)), ((), ()))


def _dot(a, b, dims=NN, precision=None):
    return lax.dot_general(a, b, dims, precision=precision, preferred_element_type=F32)


def _params(*semantics):
    return pltpu.CompilerParams(dimension_semantics=semantics, vmem_limit_bytes=VMEM_LIMIT_BYTES)


class _Order:
    last = None


def _pcall(body, *, in_specs, out_specs, grid=(), scratch_shapes=(), num_scalar_prefetch=0, **kw):
    def run(*ins):
        deps = () if _Order.last is None else (_Order.last,)
        n_in, n_dep = len(ins), len(deps)

        def wrapped(*refs):
            body(*refs[:n_in], *refs[n_in + n_dep:])

        specs = list(in_specs) + [pl.BlockSpec(memory_space=pl.ANY)] * n_dep
        if num_scalar_prefetch:
            grid_spec = pltpu.PrefetchScalarGridSpec(
                num_scalar_prefetch=num_scalar_prefetch, grid=grid, in_specs=specs, out_specs=out_specs,
                scratch_shapes=scratch_shapes)
            outs = pl.pallas_call(wrapped, grid_spec=grid_spec, **kw)(*ins, *deps)
        else:
            outs = pl.pallas_call(wrapped, grid=grid, in_specs=specs, out_specs=out_specs,
                                  scratch_shapes=scratch_shapes, **kw)(*ins, *deps)
        _Order.last = jax.tree.leaves(outs)[0]
        return outs

    return run


def _pick_rows(dim, pref, mult=16):
    best = None
    for cand in range(mult, min(dim, pref) + 1, mult):
        if dim % cand == 0:
            best = cand
    assert best is not None, (dim, pref)
    return best


def _pick(dim, pref):
    if dim <= pref:
        return dim
    best = None
    for cand in range(LANES, pref + 1, LANES):
        if dim % cand == 0:
            best = cand
    assert best is not None, (dim, pref)
    return best


def _sigmoid(x):
    return 1.0 / (1.0 + jnp.exp(-x))


def _gelu(x):
    c = 0.7978845608028654
    return 0.5 * x * (1.0 + jnp.tanh(c * (x + 0.044715 * x * x * x)))


def _gelu_grad(x):
    c = 0.7978845608028654
    t = jnp.tanh(c * (x + 0.044715 * x * x * x))
    return 0.5 * (1.0 + t) + 0.5 * x * (1.0 - t * t) * c * (1.0 + 3.0 * 0.044715 * x * x)


def _rms(x):
    r = lax.rsqrt(jnp.mean(x * x, axis=-1, keepdims=True) + EPS)
    return x * r, r


def _colsum(x):
    return jnp.sum(x, axis=0, keepdims=True)


def _accumulate(first, ref, val):
    @pl.when(first)
    def _():
        ref[...] = val

    @pl.when(jnp.logical_not(first))
    def _():
        ref[...] += val


def _matmul(a, b, *, mode, name, out_dtype, b_slots=False, out_slots=False, bm=1024, bn=1024, bk=2816):
    if mode == "nn":
        m, k = a.shape
        n = b.shape[2] * N_CHIPS if b_slots else b.shape[1]
        per = b.shape[2] if b_slots else n
    elif mode == "nt":
        m, k = a.shape
        n = b.shape[1] if b_slots else b.shape[0]
        per = b.shape[2] if b_slots else k
    else:
        k, m = a.shape
        n = b.shape[1]
        per = n // N_CHIPS if out_slots else n
    bm = _pick(m, bm)
    if mode == "nt":
        bn, bk = _pick(n, bn), _pick(per, bk)
    else:
        bn, bk = _pick(per, bn), _pick(k, bk)
    nk = k // bk
    per_blocks = per // (bk if mode == "nt" else bn)
    dims = {"nn": NN, "nt": NT, "tn": TN}[mode]

    def body(a_ref, b_ref, o_ref, *acc):
        part = _dot(a_ref[...], b_ref[...], dims)
        if nk == 1:
            o_ref[...] = part.astype(o_ref.dtype)
            return
        acc_ref, kk = acc[0], pl.program_id(2)

        @pl.when(kk == 0)
        def _():
            acc_ref[...] = part

        @pl.when(jnp.logical_and(kk > 0, kk < nk - 1))
        def _():
            acc_ref[...] += part

        @pl.when(kk == nk - 1)
        def _():
            o_ref[...] = (acc_ref[...] + part).astype(o_ref.dtype)

    if mode == "nn":
        a_spec = pl.BlockSpec((bm, bk), lambda i, j, kk: (i, kk))
        if b_slots:
            b_spec = pl.BlockSpec((None, bk, bn), lambda i, j, kk: (j // per_blocks, kk, j % per_blocks))
        else:
            b_spec = pl.BlockSpec((bk, bn), lambda i, j, kk: (kk, j))
    elif mode == "nt":
        a_spec = pl.BlockSpec((bm, bk), lambda i, j, kk: (i, kk))
        if b_slots:
            b_spec = pl.BlockSpec((None, bn, bk), lambda i, j, kk: (kk // per_blocks, j, kk % per_blocks))
        else:
            b_spec = pl.BlockSpec((bn, bk), lambda i, j, kk: (j, kk))
    else:
        a_spec = pl.BlockSpec((bk, bm), lambda i, j, kk: (kk, i))
        b_spec = pl.BlockSpec((bk, bn), lambda i, j, kk: (kk, j))
    if out_slots:
        o_spec = pl.BlockSpec((None, bm, bn), lambda i, j, kk: (j // per_blocks, i, j % per_blocks))
        out_shape = jax.ShapeDtypeStruct((N_CHIPS, m, per), out_dtype)
    else:
        o_spec = pl.BlockSpec((bm, bn), lambda i, j, kk: (i, j))
        out_shape = jax.ShapeDtypeStruct((m, n), out_dtype)
    return _pcall(
        body, name=name, grid=(m // bm, n // bn, nk), in_specs=[a_spec, b_spec], out_specs=o_spec,
        out_shape=out_shape, scratch_shapes=[pltpu.VMEM((bm, bn), F32)] if nk > 1 else [],
        compiler_params=_params("parallel", "parallel", "arbitrary"),
    )(a, b)


def _place():
    x, y, c = lax.axis_index("x"), lax.axis_index("y"), lax.axis_index("c")
    chips = [(1 - x, y), (x, 1 - y), (1 - x, 1 - y)]
    return x, y, c, chips


def _all_gather8(block, name):
    def body(x_ref, out_ref, send_sems, recv_sems, local_sem):
        x, y, c, chips = _place()
        me, sibling = (x, y, c), (x, y, 1 - c)

        def slot(px, py, pc):
            return out_ref.at[4 * px + 2 * py + pc]

        def copy(k, blk, to, src=None):
            return pltpu.make_async_remote_copy(
                src_ref=slot(*blk) if src is None else src, dst_ref=slot(*blk),
                send_sem=send_sems.at[k], recv_sem=recv_sems.at[k], device_id=to, device_id_type=MESH)

        mine = pltpu.make_async_copy(x_ref, slot(*me), local_sem)
        mine.start()
        first = [copy(0, me, sibling, src=x_ref)]
        first += [copy(1 + j, me, (*chip, c), src=x_ref) for j, chip in enumerate(chips)]
        for cp in first:
            cp.start()
        passed = [copy(4 + j, (*chip, c), sibling) for j, chip in enumerate(chips)]
        for j, chip in enumerate(chips):
            copy(1 + j, (*chip, c), me).wait_recv()
            passed[j].start()
        copy(0, sibling, me).wait_recv()
        for j, chip in enumerate(chips):
            copy(4 + j, (*chip, 1 - c), me).wait_recv()
        for cp in first + passed:
            cp.wait_send()
        mine.wait()

    return _pcall(
        body, name=name, out_shape=jax.ShapeDtypeStruct((N_DEV,) + block.shape, block.dtype),
        in_specs=[pl.BlockSpec(memory_space=pltpu.VMEM)], out_specs=pl.BlockSpec(memory_space=pltpu.VMEM),
        scratch_shapes=[pltpu.SemaphoreType.DMA((7,)), pltpu.SemaphoreType.DMA((7,)), pltpu.SemaphoreType.DMA],
        compiler_params=pltpu.CompilerParams(vmem_limit_bytes=VMEM_LIMIT_BYTES),
    )(block)


HBM_SPEC = pl.BlockSpec(memory_space=pltpu.HBM)
SEM_SPEC = pl.BlockSpec(memory_space=pltpu.SEMAPHORE)
ANY_SPEC = pl.BlockSpec(memory_space=pl.ANY)
EFFECT = pltpu.SideEffectType.DATAFLOW_SIDE_EFFECTING


def _xfer_start(name, bufs, plan, n_copies, after_last=False):
    nb = len(bufs)
    deps = (_Order.last,) if after_last and _Order.last is not None else ()
    nd = len(deps)

    def body(*refs):
        send_sems, recv_sems = refs[nb + nd], refs[nb + nd + 1]
        token = refs[nb + nd + 2 + nb]
        for k, (src, dst, dev) in enumerate(plan(refs[:nb], *_place())):
            pltpu.make_async_remote_copy(src_ref=src, dst_ref=dst, send_sem=send_sems.at[k], recv_sem=recv_sems.at[k],
                                         device_id=dev, device_id_type=MESH).start()
        token[...] = jnp.zeros_like(token)

    outs = pl.pallas_call(
        body, name=name,
        out_shape=(pltpu.SemaphoreType.DMA((n_copies,)), pltpu.SemaphoreType.DMA((n_copies,)),
                   *[pltpu.HBM(b.shape, b.dtype) for b in bufs], jax.ShapeDtypeStruct((8, LANES), F32)),
        in_specs=[HBM_SPEC] * nb + [ANY_SPEC] * nd,
        out_specs=(SEM_SPEC, SEM_SPEC, *[HBM_SPEC] * nb, pl.BlockSpec(memory_space=pltpu.VMEM)),
        input_output_aliases={i: 2 + i for i in range(nb)},
        compiler_params=pltpu.CompilerParams(has_side_effects=EFFECT),
    )(*[pltpu.with_memory_space_constraint(b, pltpu.HBM) for b in bufs], *deps)
    _Order.last = outs[-1]
    return (outs[0], outs[1]), list(outs[2:2 + nb])


def _xfer_wait(name, sems, bufs, plan):
    nb = len(bufs)

    def body(*refs):
        send_sems, recv_sems = refs[nb], refs[nb + 1]
        for k, (src, dst, dev) in enumerate(plan(refs[:nb], *_place())):
            copy = pltpu.make_async_remote_copy(src_ref=src, dst_ref=dst, send_sem=send_sems.at[k],
                                                recv_sem=recv_sems.at[k], device_id=dev, device_id_type=MESH)
            copy.wait_send()
            copy.wait_recv()

    outs = pl.pallas_call(
        body, name=name, out_shape=tuple(pltpu.HBM(b.shape, b.dtype) for b in bufs),
        in_specs=[HBM_SPEC] * nb + [SEM_SPEC, SEM_SPEC, ANY_SPEC], out_specs=tuple([HBM_SPEC] * nb),
        input_output_aliases={i: i for i in range(nb)},
        compiler_params=pltpu.CompilerParams(has_side_effects=EFFECT),
    )(*bufs, *sems, _Order.last)
    _Order.last = outs[0]
    return list(outs)


def _half(ref, c, axis):
    rows = ref.shape[axis] // 2
    return pl.ds(c * rows, rows)


def _plan_weights_ici(n):
    def plan(refs, x, y, c, chips):
        out = []
        for w in range(n):
            region = refs[w].at[2 * x + y, _half(refs[w], c, 1), :]
            out += [(region, region, (*chip, c)) for chip in chips]
        return out
    return plan


def _plan_weights_d2d(n):
    def plan(refs, x, y, c, chips):
        out = []
        for w in range(n):
            rows = _half(refs[w], c, 1)
            for chip in chips:
                region = refs[w].at[2 * chip[0] + chip[1], rows, :]
                out.append((region, region, (x, y, 1 - c)))
        return out
    return plan


def _plan_grads_d2d(n):
    def plan(refs, x, y, c, chips):
        return [(refs[w].at[:, _half(refs[w], 1 - c, 1), :], refs[n + w], (x, y, 1 - c)) for w in range(n)]
    return plan


def _plan_grads_ici(n):
    def plan(refs, x, y, c, chips):
        out = []
        for w in range(n):
            out += [(refs[w].at[2 * chip[0] + chip[1]], refs[n + w].at[2 * x + y], (*chip, c)) for chip in chips]
        return out
    return plan


def _plan_final_d2d(n):
    def plan(refs, x, y, c, chips):
        out = []
        for w in range(n):
            region = refs[w].at[_half(refs[w], c, 0), :]
            out.append((region, region, (x, y, 1 - c)))
        return out
    return plan


def _stream_blocks(hr, cols):
    bc = cols if cols <= 4096 else _pick(cols, 4096)
    return _pick_rows(hr, max(16, (768 * 1024) // bc)), bc


def _pre_reduce(g, landed, place, name):
    _, rows, cols = g.shape
    hr = rows // 2
    rb, bc = _stream_blocks(hr, cols)
    nrb = hr // rb

    def body(place_ref, g_ref, l_ref, o_ref):
        o_ref[...] = (g_ref[...].astype(F32) + l_ref[...].astype(F32)).astype(o_ref.dtype)

    return _pcall(
        body, name=name, num_scalar_prefetch=1, grid=(N_CHIPS, nrb, cols // bc),
        in_specs=[pl.BlockSpec((None, rb, bc), lambda j, i, k, p: (j, p[1] * nrb + i, k)),
                  pl.BlockSpec((None, rb, bc), lambda j, i, k, p: (j, i, k))],
        out_specs=pl.BlockSpec((None, rb, bc), lambda j, i, k, p: (j, i, k)),
        out_shape=jax.ShapeDtypeStruct((N_CHIPS, hr, cols), g.dtype),
        compiler_params=_params("parallel", "parallel", "parallel"),
    )(place, g, landed)


def _sum_slots(mine, landed, place, name):
    _, hr, cols = mine.shape
    rb, bc = _stream_blocks(hr, cols)
    rb = _pick_rows(hr, max(16, rb // 2))
    nrb = hr // rb

    def body(place_ref, m_ref, l_ref, o_ref):
        chip = place_ref[0]
        own = m_ref[...].astype(F32)
        total = jnp.where(chip == 0, own, l_ref[0].astype(F32))
        for j in range(1, N_CHIPS):
            total = total + jnp.where(chip == j, own, l_ref[j].astype(F32))
        o_ref[...] = total

    return _pcall(
        body, name=name, num_scalar_prefetch=1, grid=(nrb, cols // bc),
        in_specs=[pl.BlockSpec((None, rb, bc), lambda i, k, p: (p[0], i, k)),
                  pl.BlockSpec((N_CHIPS, rb, bc), lambda i, k, p: (0, i, k))],
        out_specs=pl.BlockSpec((rb, bc), lambda i, k, p: (p[1] * nrb + i, k)),
        out_shape=jax.ShapeDtypeStruct((2 * hr, cols), F32),
        compiler_params=_params("parallel", "parallel"),
    )(place, mine, landed)


class _WeightGather:
    def __init__(self, tag, quarters, chip):
        self.tag, self.n = tag, len(quarters)
        zones = [lax.dynamic_update_slice(lax.empty((N_CHIPS,) + q.shape, BF16), q.astype(BF16)[None], (chip, 0, 0))
                 for q in quarters]
        self.plan = _plan_weights_ici(self.n)
        self.sems, self.bufs = _xfer_start("wici_start_" + tag, zones, self.plan, 3 * self.n, after_last=True)

    def pass_on(self):
        bufs = _xfer_wait("wici_wait_" + self.tag, self.sems, self.bufs, self.plan)
        self.plan = _plan_weights_d2d(self.n)
        self.sems, self.bufs = _xfer_start("wd2d_start_" + self.tag, bufs, self.plan, 3 * self.n)

    def done(self):
        return _xfer_wait("wd2d_wait_" + self.tag, self.sems, self.bufs, self.plan)


class _GradReduce:
    def __init__(self, tag, grads):
        self.tag, self.n = tag, len(grads)
        zones = [lax.empty((N_CHIPS, g.shape[1] // 2, g.shape[2]), g.dtype) for g in grads]
        self.plan = _plan_grads_d2d(self.n)
        self.sems, self.bufs = _xfer_start("gd2d_start_" + tag, list(grads) + zones, self.plan, self.n)

    def pair(self, place):
        n = self.n
        bufs = _xfer_wait("gd2d_wait_" + self.tag, self.sems, self.bufs, self.plan)
        self.halves = [_pre_reduce(bufs[w], bufs[n + w], place, f"pre_reduce_{self.tag}{w}") for w in range(n)]

    def cross(self, after_last=False):
        zones = [lax.empty(h.shape, h.dtype) for h in self.halves]
        self.plan = _plan_grads_ici(self.n)
        self.sems, self.bufs = _xfer_start("gici_start_" + self.tag, self.halves + zones, self.plan, 3 * self.n,
                                           after_last=after_last)

    def step(self, place):
        self.pair(place)
        self.cross()

    def join(self, place):
        n = self.n
        bufs = _xfer_wait("gici_wait_" + self.tag, self.sems, self.bufs, self.plan)
        sums = [_sum_slots(bufs[w], bufs[n + w], place, f"sum_slots_{self.tag}{w}") for w in range(n)]
        self.plan = _plan_final_d2d(n)
        self.sems, self.bufs = _xfer_start("gfin_start_" + self.tag, sums, self.plan, n)

    def done(self):
        return _xfer_wait("gfin_wait_" + self.tag, self.sems, self.bufs, self.plan)


def _ada_fwd(c_all, w_q, b_q):
    d, n = w_q.shape
    bn = _pick(n, 512)

    def body(c_ref, w_ref, b_ref, o_ref):
        cv = c_ref[...]
        act = cv * _sigmoid(cv)
        o_ref[...] = _dot(act, w_ref[...], NN, lax.Precision.HIGHEST) + b_ref[...]

    return _pcall(
        body, name="ada_fwd", grid=(n // bn,),
        in_specs=[pl.BlockSpec((N_DEV, d), lambda j: (0, 0)), pl.BlockSpec((d, bn), lambda j: (0, j)),
                  pl.BlockSpec((1, bn), lambda j: (0, j))],
        out_specs=pl.BlockSpec((N_DEV, bn), lambda j: (0, j)),
        out_shape=jax.ShapeDtypeStruct((N_DEV, n), F32), compiler_params=_params("parallel"),
    )(c_all, w_q, b_q)


def _ada_bwd(c_all, dmod_q):
    d = c_all.shape[1]
    n = dmod_q.shape[1]
    bn = _pick(n, 512)

    def body(c_ref, g_ref, o_ref):
        cv = c_ref[...]
        act = cv * _sigmoid(cv)
        o_ref[...] = _dot(act, g_ref[...], TN, lax.Precision.HIGHEST)

    return _pcall(
        body, name="ada_bwd", grid=(n // bn,),
        in_specs=[pl.BlockSpec((N_DEV, d), lambda j: (0, 0)), pl.BlockSpec((N_DEV, bn), lambda j: (0, j))],
        out_specs=pl.BlockSpec((None, d, bn), lambda j: (0, 0, j)),
        out_shape=jax.ShapeDtypeStruct((1, d, n), F32), compiler_params=_params("parallel"),
    )(c_all, dmod_q)


def _row_spec(rb, width, col=0):
    return pl.BlockSpec((rb, width), lambda i, col=col: (i, col))


def _vec_spec(width, col=0):
    return pl.BlockSpec((1, width), lambda i, col=col: (0, col))


def _norm_mod_fwd(x, g, sc, sh, name, res=None, gt=None):
    s, d = x.shape
    rb = _pick(s, 256)
    has_res = res is not None

    def body(*refs):
        if has_res:
            x_ref, res_ref, gt_ref, g_ref, sc_ref, sh_ref, x1_ref, h_ref = refs
            xv = x_ref[...] + gt_ref[...] * res_ref[...]
            x1_ref[...] = xv
        else:
            x_ref, g_ref, sc_ref, sh_ref, h_ref = refs
            xv = x_ref[...]
        xh, _ = _rms(xv)
        h_ref[...] = (xh * g_ref[...] * (1.0 + sc_ref[...]) + sh_ref[...]).astype(BF16)

    row, vec = _row_spec(rb, d), _vec_spec(d)
    if has_res:
        ins, in_specs = (x, res, gt, g, sc, sh), [row, row, vec, vec, vec, vec]
        out_shape = [jax.ShapeDtypeStruct((s, d), F32), jax.ShapeDtypeStruct((s, d), BF16)]
        out_specs = [row, row]
    else:
        ins, in_specs = (x, g, sc, sh), [row, vec, vec, vec]
        out_shape, out_specs = jax.ShapeDtypeStruct((s, d), BF16), row
    return _pcall(body, name=name, grid=(s // rb,), in_specs=in_specs, out_specs=out_specs,
                          out_shape=out_shape, compiler_params=_params("parallel"))(*ins)


def _final_loss(x1, f, gt2, g_final, target):
    s, d = x1.shape
    rb = _pick(s, 256)

    def body(x1_ref, f_ref, gt_ref, g_ref, t_ref, dx_ref, df_ref, loss_ref, dg_ref, dgt_ref):
        first = pl.program_id(0) == 0
        fv, gt, gv = f_ref[...], gt_ref[...], g_ref[...]
        x2 = x1_ref[...] + gt * fv
        xh, r = _rms(x2)
        err = xh * gv - t_ref[...]
        blk = 0.5 * jnp.sum(jnp.sum(err * err, axis=1, keepdims=True), axis=0, keepdims=True) / d
        dy = err / d
        dxh = dy * gv
        dx = r * (dxh - xh * jnp.mean(dxh * xh, axis=-1, keepdims=True))
        dx_ref[...] = dx
        df_ref[...] = (dx * gt).astype(BF16)
        _accumulate(first, loss_ref, jnp.broadcast_to(blk, (1, LANES)))
        _accumulate(first, dg_ref, _colsum(dy * xh))
        _accumulate(first, dgt_ref, _colsum(dx * fv))

    row, vec = _row_spec(rb, d), _vec_spec(d)
    return _pcall(
        body, name="final_loss", grid=(s // rb,), in_specs=[row, row, vec, vec, row],
        out_specs=[row, row, _vec_spec(LANES), vec, vec],
        out_shape=[jax.ShapeDtypeStruct((s, d), F32), jax.ShapeDtypeStruct((s, d), BF16),
                   jax.ShapeDtypeStruct((1, LANES), F32), jax.ShapeDtypeStruct((1, d), F32),
                   jax.ShapeDtypeStruct((1, d), F32)],
        compiler_params=_params("arbitrary"),
    )(x1, f, gt2, g_final, target)


def _norm_mod_bwd(dh, xin, dres, g, sc, name, branch=None, gt=None):
    s, d = xin.shape
    rb = _pick(s, 256)
    has_branch = branch is not None

    def body(*refs):
        if has_branch:
            dh_ref, x_ref, dres_ref, g_ref, sc_ref, br_ref, gt_ref, dx_ref, dbr_ref, dsh_ref, dsc_ref, dg_ref, dgt_ref = refs
        else:
            dh_ref, x_ref, dres_ref, g_ref, sc_ref, dx_ref, dsh_ref, dsc_ref, dg_ref = refs
        first = pl.program_id(0) == 0
        gv = g_ref[...]
        xh, r = _rms(x_ref[...])
        dhv = dh_ref[...]
        dn = dhv * (1.0 + sc_ref[...])
        dxh = dn * gv
        dx = dres_ref[...] + r * (dxh - xh * jnp.mean(dxh * xh, axis=-1, keepdims=True))
        dx_ref[...] = dx
        _accumulate(first, dsh_ref, _colsum(dhv))
        _accumulate(first, dsc_ref, _colsum(dhv * xh * gv))
        _accumulate(first, dg_ref, _colsum(dn * xh))
        if has_branch:
            dbr_ref[...] = (dx * gt_ref[...]).astype(BF16)
            _accumulate(first, dgt_ref, _colsum(dx * br_ref[...]))

    row, vec = _row_spec(rb, d), _vec_spec(d)
    vec_shape = jax.ShapeDtypeStruct((1, d), F32)
    if has_branch:
        ins, in_specs = (dh, xin, dres, g, sc, branch, gt), [row, row, row, vec, vec, row, vec]
        out_specs = [row, row, vec, vec, vec, vec]
        out_shape = [jax.ShapeDtypeStruct((s, d), F32), jax.ShapeDtypeStruct((s, d), BF16)] + [vec_shape] * 4
    else:
        ins, in_specs = (dh, xin, dres, g, sc), [row, row, row, vec, vec]
        out_specs = [row, vec, vec, vec]
        out_shape = [jax.ShapeDtypeStruct((s, d), F32)] + [vec_shape] * 3
    return _pcall(body, name=name, grid=(s // rb,), in_specs=in_specs, out_specs=out_specs,
                          out_shape=out_shape, compiler_params=_params("arbitrary"))(*ins)


def _gate_fwd(pa, pb, z, b_gate, gate_col):
    s, d = pa.shape
    rb = _pick(s, 256)

    def body(pa_ref, pb_ref, za_ref, zb_ref, ba_ref, bb_ref, y_ref):
        ga = _sigmoid(za_ref[...] + ba_ref[...])
        gb = _sigmoid(zb_ref[...] + bb_ref[...])
        y_ref[...] = (ga * pa_ref[...] + gb * pb_ref[...]).astype(BF16)

    row = _row_spec(rb, d)
    return _pcall(
        body, name="gate_fwd", grid=(s // rb,),
        in_specs=[row, row, _row_spec(rb, d, gate_col), _row_spec(rb, d, gate_col + 1), _vec_spec(d, 0), _vec_spec(d, 1)],
        out_specs=row, out_shape=jax.ShapeDtypeStruct((s, d), BF16), compiler_params=_params("parallel"),
    )(pa, pb, z, z, b_gate, b_gate)


def _gate_bwd(dy, pa, pb, z, b_gate, gate_col):
    s, d = pa.shape
    rb = _pick(s, 256)

    def body(dy_ref, pa_ref, pb_ref, za_ref, zb_ref, ba_ref, bb_ref, dpa_ref, dpb_ref, dz_ref, db_ref):
        first = pl.program_id(0) == 0
        dyv = dy_ref[...]
        ga = _sigmoid(za_ref[...] + ba_ref[...])
        gb = _sigmoid(zb_ref[...] + bb_ref[...])
        dpa_ref[...] = (dyv * ga).astype(BF16)
        dpb_ref[...] = (dyv * gb).astype(BF16)
        dga = dyv * pa_ref[...] * ga * (1.0 - ga)
        dgb = dyv * pb_ref[...] * gb * (1.0 - gb)
        dz_ref[:, :d] = dga.astype(BF16)
        dz_ref[:, d:] = dgb.astype(BF16)

        @pl.when(first)
        def _():
            db_ref[:, :d] = _colsum(dga)
            db_ref[:, d:] = _colsum(dgb)

        @pl.when(jnp.logical_not(first))
        def _():
            db_ref[:, :d] += _colsum(dga)
            db_ref[:, d:] += _colsum(dgb)

    row = _row_spec(rb, d)
    return _pcall(
        body, name="gate_bwd", grid=(s // rb,),
        in_specs=[row, row, row, _row_spec(rb, d, gate_col), _row_spec(rb, d, gate_col + 1), _vec_spec(d, 0), _vec_spec(d, 1)],
        out_specs=[row, row, _row_spec(rb, 2 * d), _vec_spec(2 * d)],
        out_shape=[jax.ShapeDtypeStruct((s, d), BF16), jax.ShapeDtypeStruct((s, d), BF16),
                   jax.ShapeDtypeStruct((s, 2 * d), BF16), jax.ShapeDtypeStruct((1, 2 * d), F32)],
        compiler_params=_params("arbitrary"),
    )(dy, pa, pb, z, z, b_gate, b_gate)


def _swiglu_fwd(aup):
    s, ff2 = aup.shape
    ff = ff2 // 2
    rb = _pick(s, 128)

    def body(a_ref, u_ref, o_ref):
        a = a_ref[...]
        o_ref[...] = (a * _sigmoid(a) * u_ref[...]).astype(BF16)

    return _pcall(
        body, name="swiglu_fwd", grid=(s // rb,), in_specs=[_row_spec(rb, ff, 0), _row_spec(rb, ff, 1)],
        out_specs=_row_spec(rb, ff), out_shape=jax.ShapeDtypeStruct((s, ff), BF16),
        compiler_params=_params("parallel"),
    )(aup, aup)


def _swiglu_bwd(dhf, aup):
    s, ff2 = aup.shape
    ff = ff2 // 2
    rb = _pick(s, 128)

    def body(d_ref, a_ref, u_ref, o_ref):
        a, dv = a_ref[...], d_ref[...]
        sa = _sigmoid(a)
        o_ref[:, :ff] = (dv * u_ref[...] * sa * (1.0 + a * (1.0 - sa))).astype(BF16)
        o_ref[:, ff:] = (dv * a * sa).astype(BF16)

    return _pcall(
        body, name="swiglu_bwd", grid=(s // rb,),
        in_specs=[_row_spec(rb, ff), _row_spec(rb, ff, 0), _row_spec(rb, ff, 1)],
        out_specs=_row_spec(rb, ff2), out_shape=jax.ShapeDtypeStruct((s, ff2), BF16),
        compiler_params=_params("parallel"),
    )(dhf, aup, aup)


def _tril(n):
    return lax.broadcasted_iota(jnp.int32, (n, n), 0) >= lax.broadcasted_iota(jnp.int32, (n, n), 1)


def _gmlp_norm(v, ln_g, ln_b):
    gv = _gelu(v)
    mu = jnp.mean(gv, axis=-1, keepdims=True)
    cen = gv - mu
    rs = lax.rsqrt(jnp.mean(cen * cen, axis=-1, keepdims=True) + EPS)
    xh = cen * rs
    return xh, rs, xh * ln_g + ln_b


def _gmlp_fwd(z, ln_g, ln_b, ws, bs_t):
    s = z.shape[0]
    gw = ln_g.shape[1]
    groups, chunk, _ = ws.shape

    def body(u_ref, v_ref, lg_ref, lb_ref, ws_ref, bs_ref, ya_ref):
        gu = _gelu(u_ref[...])
        _, _, vn = _gmlp_norm(v_ref[...], lg_ref[...], lb_ref[...])
        mask = _tril(chunk)
        for g in range(groups):
            cols = slice(g * LANES, (g + 1) * LANES)
            wm = jnp.where(mask, ws_ref[g], 0.0).astype(BF16)
            sg = _dot(wm, vn[:, cols].astype(BF16)) + bs_ref[:, g:g + 1]
            ya_ref[:, cols] = (gu[:, cols] * sg).astype(BF16)

    return _pcall(
        body, name="gmlp_fwd", grid=(s // chunk,),
        in_specs=[_row_spec(chunk, gw, 0), _row_spec(chunk, gw, 1), _vec_spec(gw), _vec_spec(gw),
                  pl.BlockSpec((groups, chunk, chunk), lambda i: (0, 0, 0)), pl.BlockSpec((chunk, LANES), lambda i: (0, 0))],
        out_specs=_row_spec(chunk, gw), out_shape=jax.ShapeDtypeStruct((s, gw), BF16),
        compiler_params=_params("parallel"),
    )(z, z, ln_g, ln_b, ws, bs_t)


def _gmlp_bwd(dya, z, ln_g, ln_b, ws, bs_t):
    s = z.shape[0]
    gw = ln_g.shape[1]
    groups, chunk, _ = ws.shape

    def body(dya_ref, u_ref, v_ref, lg_ref, lb_ref, ws_ref, bs_ref, duv_ref, dws_ref, dbs_ref, dlg_ref, dlb_ref, dvn_ref):
        first = pl.program_id(0) == 0
        u, v, lg = u_ref[...], v_ref[...], lg_ref[...]
        gu = _gelu(u)
        xh, rs, vn = _gmlp_norm(v, lg, lb_ref[...])
        dyav = dya_ref[...]
        mask = _tril(chunk)
        lane = lax.broadcasted_iota(jnp.int32, (chunk, LANES), 1)
        dbs = jnp.zeros((chunk, LANES), F32)
        for g in range(groups):
            cols = slice(g * LANES, (g + 1) * LANES)
            wm = jnp.where(mask, ws_ref[g], 0.0).astype(BF16)
            vg = vn[:, cols].astype(BF16)
            sg = _dot(wm, vg) + bs_ref[:, g:g + 1]
            ds = dyav[:, cols] * gu[:, cols]
            duv_ref[:, cols] = (dyav[:, cols] * sg * _gelu_grad(u[:, cols])).astype(BF16)
            dsb = ds.astype(BF16)
            _accumulate(first, dws_ref.at[g], jnp.where(mask, _dot(dsb, vg, NT), 0.0))
            dbs = dbs + jnp.where(lane == g, jnp.sum(ds, axis=-1, keepdims=True), 0.0)
            dvn_ref[:, cols] = _dot(wm, dsb, TN)
        dvn = dvn_ref[...]
        _accumulate(first, dbs_ref, dbs)
        _accumulate(first, dlb_ref, _colsum(dvn))
        _accumulate(first, dlg_ref, _colsum(dvn * xh))
        dxh = dvn * lg
        dgv = rs * (dxh - jnp.mean(dxh, axis=-1, keepdims=True) - xh * jnp.mean(dxh * xh, axis=-1, keepdims=True))
        duv_ref[:, gw:] = (dgv * _gelu_grad(v)).astype(BF16)

    return _pcall(
        body, name="gmlp_bwd", grid=(s // chunk,),
        in_specs=[_row_spec(chunk, gw), _row_spec(chunk, gw, 0), _row_spec(chunk, gw, 1), _vec_spec(gw), _vec_spec(gw),
                  pl.BlockSpec((groups, chunk, chunk), lambda i: (0, 0, 0)), pl.BlockSpec((chunk, LANES), lambda i: (0, 0))],
        out_specs=[_row_spec(chunk, 2 * gw), pl.BlockSpec((groups, chunk, chunk), lambda i: (0, 0, 0)),
                   pl.BlockSpec((chunk, LANES), lambda i: (0, 0)), _vec_spec(gw), _vec_spec(gw)],
        out_shape=[jax.ShapeDtypeStruct((s, 2 * gw), BF16), jax.ShapeDtypeStruct((groups, chunk, chunk), F32),
                   jax.ShapeDtypeStruct((chunk, LANES), F32), jax.ShapeDtypeStruct((1, gw), F32),
                   jax.ShapeDtypeStruct((1, gw), F32)],
        scratch_shapes=[pltpu.VMEM((chunk, gw), F32)],
        compiler_params=_params("arbitrary"),
    )(dya, z, z, ln_g, ln_b, ws, bs_t)


def _lower_bound(lb_ref):
    a0, a1 = lb_ref[0:1, :], lb_ref[1:2, :]
    mx = jnp.maximum(a0, a1)
    e0, e1 = jnp.exp(a0 - mx), jnp.exp(a1 - mx)
    return e0 / (e0 + e1)


def _sum_dot(mask, x):
    hi = x.astype(BF16)
    rest = x - hi.astype(F32)
    mid = rest.astype(BF16)
    low = (rest - mid.astype(F32)).astype(BF16)
    return _dot(mask, hi) + _dot(mask, mid) + _dot(mask, low)


def _ones_where(mask):
    return jnp.where(mask, 1.0, 0.0).astype(BF16)


def _hg_masks(rows, t):
    r = lax.broadcasted_iota(jnp.int32, (rows, rows), 0)
    c = lax.broadcasted_iota(jnp.int32, (rows, rows), 1)
    same = (r // t) == (c // t)
    incl = jnp.logical_and(same, c <= r)
    upto_mid = jnp.logical_and(same, (c % t) <= t // 2)
    rev = jnp.logical_and(same, c >= r)
    return same, incl, upto_mid, rev


def _hg_block(q, fp, lb, masks):
    rows = q.shape[0]
    same, incl, upto_mid, _ = masks
    sig = _sigmoid(fp)
    f = lb + (1.0 - lb) * sig
    k = 1.0 - f
    sq = _sigmoid(q)
    qa = q * sq
    stacked = jnp.concatenate([_ones_where(m) for m in (incl, same, upto_mid)], axis=0)
    sums = _sum_dot(stacked, jnp.log(f))
    b, b_last, b_mid = sums[:rows], sums[rows:2 * rows], sums[2 * rows:]
    e_q = jnp.exp(jnp.minimum(b - b_mid, EXP_CLAMP))
    e_k = jnp.exp(jnp.minimum(b_mid - b, EXP_CLAMP))
    e_in = jnp.exp(b)
    e_out = jnp.exp(b_last - b)
    return dict(sig=sig, f=f, k=k, sq=sq, qa=qa, e_last=jnp.exp(b_last), e_q=e_q, e_k=e_k, e_in=e_in, e_out=e_out,
                q_hat=(qa * e_q).astype(BF16), k_hat=(k * e_k).astype(BF16),
                q_in=(qa * e_in).astype(BF16), k_out=k * e_out)


def _hgrn_fwd(z, hg_lb, norm_g, q_col):
    s = z.shape[0]
    hw = norm_g.shape[1]
    heads = hw // LANES
    t = HG_CHUNK
    rows = min(HG_ROWS, s)
    per_step = rows // t
    per = hw // LANES

    def zspec(which):
        return pl.BlockSpec((rows, LANES), lambda h, r, which=which: (r, q_col + which * per + h))

    def body(q_ref, f_ref, i_ref, g_ref, lb_ref, ng_ref, yb_ref, o_ref, st_out_ref, st_ref, e_last_ref, inter_ref):
        @pl.when(pl.program_id(1) == 0)
        def _():
            st_ref[...] = jnp.zeros_like(st_ref)

        masks = _hg_masks(rows, t)
        blk = _hg_block(q_ref[...], f_ref[...], _lower_bound(lb_ref), masks)
        iv = i_ref[...].astype(BF16)
        q_in, k_out = blk["q_in"], blk["k_out"].astype(BF16)
        e_last_ref[...] = blk["e_last"]
        attn = jnp.where(masks[1], _dot(blk["q_hat"], blk["k_hat"], NT), 0.0).astype(BF16)
        o = _dot(attn, iv)
        grown = [_dot(iv[j * t:(j + 1) * t], k_out[j * t:(j + 1) * t], TN) for j in range(per_step)]
        st = st_ref[...]
        for j in range(per_step):
            st_out_ref[j] = st
            inter_ref[j * t:(j + 1) * t, :] = _dot(q_in[j * t:(j + 1) * t], st.astype(BF16), NT)
            st = st * e_last_ref[j * t:j * t + 1, :] + grown[j]
        st_ref[...] = st
        o = o + inter_ref[...]
        o_ref[...] = o
        og = g_ref[...]
        on, _ = _rms(o)
        yb_ref[...] = (on * ng_ref[...] * (og * _sigmoid(og))).astype(BF16)

    out_row = pl.BlockSpec((rows, LANES), lambda h, r: (r, h))
    return _pcall(
        body, name="hgrn_fwd", grid=(heads, s // rows),
        in_specs=[zspec(0), zspec(1), zspec(2), zspec(3),
                  pl.BlockSpec((2, LANES), lambda h, r: (0, h)), pl.BlockSpec((1, LANES), lambda h, r: (0, h))],
        out_specs=[out_row, out_row, pl.BlockSpec((None, per_step, LANES, LANES), lambda h, r: (h, r, 0, 0))],
        out_shape=[jax.ShapeDtypeStruct((s, hw), BF16), jax.ShapeDtypeStruct((s, hw), F32),
                   jax.ShapeDtypeStruct((heads, s // t, LANES, LANES), F32)],
        scratch_shapes=[pltpu.VMEM((LANES, LANES), F32), pltpu.VMEM((rows, LANES), F32), pltpu.VMEM((rows, LANES), F32)],
        compiler_params=_params("parallel", "arbitrary"),
    )(z, z, z, z, hg_lb, norm_g)


def _hgrn_bwd(dyb, z, o_raw, states, hg_lb, norm_g, q_col):
    s = z.shape[0]
    hw = norm_g.shape[1]
    heads = hw // LANES
    t = HG_CHUNK
    rows = min(HG_ROWS, s)
    per_step = rows // t
    per = hw // LANES
    n_steps = s // rows

    def zspec(which):
        return pl.BlockSpec((rows, LANES), lambda h, r, which=which: (n_steps - 1 - r, q_col + which * per + h))

    def body(dyb_ref, q_ref, f_ref, i_ref, g_ref, o_ref, st_in_ref, lb_ref, ng_ref,
             dq_ref, df_ref, di_ref, dg_ref, dlb_ref, dng_ref, dst_ref, acc_lb_ref, acc_ng_ref,
             e_last_ref, dq_in_ref, dk_out_ref, di_inter_ref, carry_ref):
        step = pl.program_id(1)

        @pl.when(step == 0)
        def _():
            dst_ref[...] = jnp.zeros_like(dst_ref)
            acc_lb_ref[...] = jnp.zeros_like(acc_lb_ref)
            acc_ng_ref[...] = jnp.zeros_like(acc_ng_ref)

        lb = _lower_bound(lb_ref)
        ng = ng_ref[...]
        masks = _hg_masks(rows, t)
        same, incl, _, rev = masks
        q = q_ref[...]
        blk = _hg_block(q, f_ref[...], lb, masks)
        iv = i_ref[...].astype(BF16)
        o, og, dy = o_ref[...], g_ref[...], dyb_ref[...]
        so = _sigmoid(og)
        on, r = _rms(o)
        acc_ng_ref[...] += _colsum(dy * on * (og * so))
        dg_ref[...] = (dy * on * ng * so * (1.0 + og * (1.0 - so))).astype(BF16)
        don = dy * ng * (og * so)
        do = (r * (don - on * jnp.mean(don * on, axis=-1, keepdims=True))).astype(BF16)
        q_hat, k_hat, q_in, k_out = blk["q_hat"], blk["k_hat"], blk["q_in"], blk["k_out"]
        k_out_b = k_out.astype(BF16)
        attn = jnp.where(incl, _dot(q_hat, k_hat, NT), 0.0).astype(BF16)
        d_attn = jnp.where(incl, _dot(do, iv, NT), 0.0).astype(BF16)
        di_intra = _dot(attn, do, TN)
        dq_hat = _dot(d_attn, k_hat)
        dk_hat = _dot(d_attn, q_hat, TN)
        e_last_ref[...] = blk["e_last"]
        grown = [_dot(do[j * t:(j + 1) * t], q_in[j * t:(j + 1) * t], TN) for j in range(per_step)]
        dst = dst_ref[...]
        for j in reversed(range(per_step)):
            rs_ = slice(j * t, (j + 1) * t)
            e_last = e_last_ref[j * t:j * t + 1, :]
            st_prev, dst_b = st_in_ref[j], dst.astype(BF16)
            dq_in_ref[rs_, :] = _dot(do[rs_], st_prev.astype(BF16))
            dk_out_ref[rs_, :] = _dot(iv[rs_], dst_b)
            di_inter_ref[rs_, :] = _dot(k_out_b[rs_], dst_b, NT)
            carry_ref[rs_, :] = jnp.broadcast_to(e_last * _colsum(st_prev * dst), (t, LANES))
            dst = dst * e_last + grown[j]
        dst_ref[...] = dst
        di_ref[...] = (di_intra + di_inter_ref[...]).astype(BF16)
        dk_out = dk_out_ref[...]
        dqa = dq_in_ref[...] * blk["e_in"] + dq_hat * blk["e_q"]
        dk = dk_out * blk["e_out"] + dk_hat * blk["e_k"]
        db = blk["qa"] * dqa - blk["k"] * dk
        sum_mask = jnp.concatenate([_ones_where(rev), _ones_where(same)], axis=1)
        dlf = _sum_dot(sum_mask, jnp.concatenate([db, dk_out * k_out], axis=0)) + carry_ref[...]
        dfv = dlf / blk["f"] - dk
        sig, sq = blk["sig"], blk["sq"]
        df_ref[...] = (dfv * (1.0 - lb) * sig * (1.0 - sig)).astype(BF16)
        acc_lb_ref[...] += _colsum(dfv * (1.0 - sig))
        dq_ref[...] = (dqa * sq * (1.0 + q * (1.0 - sq))).astype(BF16)

        @pl.when(step == n_steps - 1)
        def _():
            d0 = acc_lb_ref[...] * lb * (1.0 - lb)
            dlb_ref[0:1, :] = d0
            dlb_ref[1:2, :] = -d0
            dng_ref[...] = acc_ng_ref[...]

    rev_row = pl.BlockSpec((rows, LANES), lambda h, r: (n_steps - 1 - r, h))
    piece = jax.ShapeDtypeStruct((s, hw), BF16)
    return _pcall(
        body, name="hgrn_bwd", grid=(heads, n_steps),
        in_specs=[rev_row, zspec(0), zspec(1), zspec(2), zspec(3), rev_row,
                  pl.BlockSpec((None, per_step, LANES, LANES), lambda h, r: (h, n_steps - 1 - r, 0, 0)),
                  pl.BlockSpec((2, LANES), lambda h, r: (0, h)), pl.BlockSpec((1, LANES), lambda h, r: (0, h))],
        out_specs=[rev_row, rev_row, rev_row, rev_row,
                   pl.BlockSpec((2, LANES), lambda h, r: (0, h)), pl.BlockSpec((1, LANES), lambda h, r: (0, h))],
        out_shape=[piece, piece, piece, piece, jax.ShapeDtypeStruct((2, hw), F32), jax.ShapeDtypeStruct((1, hw), F32)],
        scratch_shapes=[pltpu.VMEM((LANES, LANES), F32), pltpu.VMEM((1, LANES), F32), pltpu.VMEM((1, LANES), F32)]
        + [pltpu.VMEM((rows, LANES), F32)] * 5,
        compiler_params=_params("parallel", "arbitrary"),
    )(dyb, z, z, z, z, o_raw, states, hg_lb, norm_g)


def _adamw(w, m, v, parts, name):
    rows, cols = w.shape
    n_parts = parts.shape[0]
    bc = cols if cols <= 4096 else _pick(cols, 4096)
    rb = _pick_rows(rows, max(8, (384 * 1024) // bc), mult=8)
    m_corr = 1.0 - ADAM_B1 ** ADAM_STEP
    v_corr = 1.0 - ADAM_B2 ** ADAM_STEP

    def body(w_ref, m_ref, v_ref, p_ref, g_ref, d_ref, mo_ref, vo_ref):
        g = p_ref[0].astype(F32)
        for p in range(1, n_parts):
            g = g + p_ref[p].astype(F32)
        m2 = ADAM_B1 * m_ref[...] + (1.0 - ADAM_B1) * g
        v2 = ADAM_B2 * v_ref[...] + (1.0 - ADAM_B2) * (g * g)
        g_ref[...] = g
        mo_ref[...] = m2
        vo_ref[...] = v2
        d_ref[...] = -ADAM_LR * ((m2 / m_corr) / (jnp.sqrt(v2 / v_corr) + ADAM_EPS) + ADAM_WD * w_ref[...])

    blk = pl.BlockSpec((rb, bc), lambda i, j: (i, j))
    out = jax.ShapeDtypeStruct((rows, cols), F32)
    return _pcall(
        body, name=name, grid=(rows // rb, cols // bc),
        in_specs=[blk, blk, blk, pl.BlockSpec((n_parts, rb, bc), lambda i, j: (0, i, j))],
        out_specs=[blk] * 4, out_shape=[out] * 4, compiler_params=_params("parallel", "parallel"),
    )(w, m, v, parts)


SMALL = ("b_ada", "norm1_g", "b_gate", "gmlp_ln_g", "gmlp_ln_b", "gmlp_ws", "gmlp_bs", "hg_lb", "hg_norm_g",
         "norm2_g", "final_norm_g")
BIG = ("w_in", "w_branch_gmlp", "w_branch_hg", "w_out", "w_ffn_in", "w_ffn_out")
WEIGHTS = ("w_ada", "b_ada", "norm1_g", "w_in", "b_gate", "gmlp_ln_g", "gmlp_ln_b", "gmlp_ws", "gmlp_bs", "hg_lb",
           "hg_norm_g", "w_branch_gmlp", "w_branch_hg", "w_out", "norm2_g", "w_ffn_in", "w_ffn_out", "final_norm_g")


def _pack(parts):
    return jnp.concatenate([p.reshape(-1, LANES) for p in parts], axis=0)


def _step(x, c, loss_target, w, m, v):
    s, d = x.shape[1], x.shape[2]
    gw = w["gmlp_ln_g"].shape[-1]
    hw = w["hg_norm_g"].shape[-1]
    x2d, tgt = x[0], loss_target[0]
    mx, my, mc = lax.axis_index("x"), lax.axis_index("y"), lax.axis_index("c")
    chip = 2 * mx + my
    dev = 2 * chip + mc
    q_col = 2 * gw // LANES
    gate_col = (2 * gw + 4 * hw) // d
    place = jnp.stack([chip, mc]).astype(jnp.int32)
    _Order.last = None

    c_all = _all_gather8(c.reshape(-1, LANES), "gather_c").reshape(N_DEV, d)
    n_ada = w["w_ada"].shape[-1]
    b_ada_q = lax.dynamic_slice(w["b_ada"], (0, chip * n_ada), (1, n_ada))
    mod_q = _ada_fwd(c_all, w["w_ada"][0], b_ada_q)
    mod_all = _all_gather8(mod_q, "gather_mod")
    mod = lax.dynamic_index_in_dim(mod_all, dev, axis=1, keepdims=False)[::2].reshape(1, 6 * d)
    sh1, sc1, gt1, sh2, sc2, gt2 = [mod[:, i * d:(i + 1) * d] for i in range(6)]

    gather_in = _WeightGather("in", [w["w_in"][0]], chip)
    gather_mix = _WeightGather("mix", [w[n][0] for n in ("w_branch_gmlp", "w_branch_hg", "w_out")], chip)
    gather_fi = _WeightGather("fi", [w["w_ffn_in"][0]], chip)
    gather_fo = _WeightGather("fo", [w["w_ffn_out"][0]], chip)

    norm1_g, norm2_g, final_g = w["norm1_g"], w["norm2_g"], w["final_norm_g"].reshape(1, d)
    ln_g, ln_b = w["gmlp_ln_g"], w["gmlp_ln_b"]
    ws = w["gmlp_ws"][0]
    groups = ws.shape[0]
    bs_t = jnp.pad(w["gmlp_bs"][0].T, ((0, 0), (0, LANES - groups)))
    hg_lb, hg_ng, b_gate = w["hg_lb"], w["hg_norm_g"], w["b_gate"]

    h1 = _norm_mod_fwd(x2d, norm1_g, sc1, sh1, "norm1_fwd")
    gather_in.pass_on()
    w_in, = gather_in.done()
    z = _matmul(h1, w_in, mode="nn", name="mm_z", out_dtype=F32, b_slots=True, bn=1280)
    gather_mix.pass_on()
    ya = _gmlp_fwd(z, ln_g, ln_b, ws, bs_t)
    yb, o_raw, states = _hgrn_fwd(z, hg_lb, hg_ng, q_col)
    w_bg, w_bh, w_out = gather_mix.done()
    w_out = w_out.reshape(-1, w_out.shape[-1])
    pa = _matmul(ya, w_bg, mode="nn", name="mm_pa", out_dtype=F32, b_slots=True)
    pb = _matmul(yb, w_bh, mode="nn", name="mm_pb", out_dtype=F32, b_slots=True)
    y = _gate_fwd(pa, pb, z, b_gate, gate_col)
    gather_fi.pass_on()
    yo = _matmul(y, w_out, mode="nn", name="mm_yo", out_dtype=F32)
    x1, h2 = _norm_mod_fwd(x2d, norm2_g, sc2, sh2, "norm2_fwd", res=yo, gt=gt1)
    w_fi, = gather_fi.done()
    aup = _matmul(h2, w_fi, mode="nn", name="mm_aup", out_dtype=F32, b_slots=True, bn=1408)
    gather_fo.pass_on()
    hf = _swiglu_fwd(aup)
    w_fo, = gather_fo.done()
    w_fo = w_fo.reshape(-1, w_fo.shape[-1])
    ffn = _matmul(hf, w_fo, mode="nn", name="mm_ffn", out_dtype=F32)
    dx2, dffn, loss_row, d_final_g, d_gt2 = _final_loss(x1, ffn, gt2, final_g, tgt)

    g_fo = _matmul(hf, dffn, mode="tn", name="mm_g_fo", out_dtype=BF16, bm=1408)
    dhf = _matmul(dffn, w_fo, mode="nt", name="mm_dhf", out_dtype=F32, bn=1408)
    daup = _swiglu_bwd(dhf, aup)
    g_fi = _matmul(h2, daup, mode="tn", name="mm_g_fi", out_dtype=BF16, out_slots=True, bn=1408)
    red_ffn = _GradReduce("ffn", [g_fo.reshape(N_CHIPS, -1, g_fo.shape[-1]), g_fi])
    dh2 = _matmul(daup, w_fi, mode="nt", name="mm_dh2", out_dtype=F32, b_slots=True)
    red_ffn.step(place)
    dx1, dyo, d_sh2, d_sc2, d_norm2, d_gt1 = _norm_mod_bwd(dh2, x1, dx2, norm2_g, sc2, "norm2_bwd", branch=yo, gt=gt1)
    g_out = _matmul(y, dyo, mode="tn", name="mm_g_out", out_dtype=BF16)
    dy = _matmul(dyo, w_out, mode="nt", name="mm_dy", out_dtype=F32)
    dpa, dpb, dz_gate, d_b_gate = _gate_bwd(dy, pa, pb, z, b_gate, gate_col)
    g_bg = _matmul(ya, dpa, mode="tn", name="mm_g_bg", out_dtype=BF16, out_slots=True)
    g_bh = _matmul(yb, dpb, mode="tn", name="mm_g_bh", out_dtype=BF16, out_slots=True)
    red_mix = _GradReduce("mix", [g_out.reshape(N_CHIPS, -1, g_out.shape[-1]), g_bg, g_bh])
    dya = _matmul(dpa, w_bg, mode="nt", name="mm_dya", out_dtype=F32, b_slots=True)
    dyb = _matmul(dpb, w_bh, mode="nt", name="mm_dyb", out_dtype=F32, b_slots=True)
    red_mix.step(place)
    dz_uv, d_ws, d_bs_t, d_ln_g, d_ln_b = _gmlp_bwd(dya, z, ln_g, ln_b, ws, bs_t)
    dz_q, dz_f, dz_i, dz_g, d_hg_lb, d_hg_ng = _hgrn_bwd(dyb, z, o_raw, states, hg_lb, hg_ng, q_col)
    dz = jnp.concatenate([dz_uv, dz_q, dz_f, dz_i, dz_g, dz_gate], axis=1)
    g_in = _matmul(h1, dz, mode="tn", name="mm_g_in", out_dtype=BF16, out_slots=True, bn=1280)
    red_in = _GradReduce("in", [g_in])
    dh1 = _matmul(dz, w_in, mode="nt", name="mm_dh1", out_dtype=F32, b_slots=True)
    red_in.pair(place)
    grad_x, d_sh1, d_sc1, d_norm1 = _norm_mod_bwd(dh1, x2d, dx1, norm1_g, sc1, "norm1_bwd")

    d_mod = jnp.concatenate([d_sh1, d_sc1, d_gt1, d_sh2, d_sc2, d_gt2], axis=1)
    small_part = {"b_ada": d_mod, "norm1_g": d_norm1, "b_gate": d_b_gate, "gmlp_ln_g": d_ln_g, "gmlp_ln_b": d_ln_b,
                  "gmlp_ws": d_ws, "gmlp_bs": d_bs_t[:, :groups].T, "hg_lb": d_hg_lb, "hg_norm_g": d_hg_ng,
                  "norm2_g": d_norm2, "final_norm_g": d_final_g}
    small_all = _all_gather8(_pack([small_part[n] for n in SMALL]), "gather_small")
    red_in.cross(after_last=True)
    d_mod_all = small_all[:, :6 * d // LANES].reshape(N_DEV, 6 * d)
    d_mod_q = lax.dynamic_slice(d_mod_all, (0, chip * n_ada), (N_DEV, n_ada))
    g_ada = _ada_bwd(c_all, d_mod_q)

    grad, delta, new_m, new_v = {}, {}, {}, {}

    def update(n, parts):
        outs = _adamw(w[n][0], m[n][0], v[n][0], parts, "adamw_" + n)
        grad[n], delta[n], new_m[n], new_v[n] = [o[None] for o in outs]

    update("w_ada", g_ada)
    outs = _adamw(_pack([w[n] for n in SMALL]), _pack([m[n] for n in SMALL]), _pack([v[n] for n in SMALL]),
                  small_all, "adamw_small")
    red_ffn.join(place)
    red_mix.join(place)
    g_fo, g_fi = red_ffn.done()
    update("w_ffn_out", g_fo[None])
    update("w_ffn_in", g_fi[None])
    red_in.join(place)
    g_out, g_bg, g_bh = red_mix.done()
    update("w_out", g_out[None])
    update("w_branch_gmlp", g_bg[None])
    update("w_branch_hg", g_bh[None])
    update("w_in", red_in.done()[0][None])
    row = 0
    for n in SMALL:
        cnt = w[n].size // LANES
        for dst, o in zip((grad, delta, new_m, new_v), outs):
            dst[n] = o[row:row + cnt].reshape(w[n].shape)
        row += cnt

    loss = lax.psum(loss_row[0, 0], ("x", "y", "c"))
    return (loss, grad_x[None], *[grad[n] for n in WEIGHTS], *[delta[n] for n in WEIGHTS],
            *[new_m[n] for n in WEIGHTS], *[new_v[n] for n in WEIGHTS])


def kernel(x, c, w_ada, b_ada, norm1_g, w_in, b_gate, gmlp_ln_g, gmlp_ln_b, gmlp_ws, gmlp_bs, hg_lb, hg_norm_g, w_branch_gmlp, w_branch_hg, w_out, norm2_g, w_ffn_in, w_ffn_out, final_norm_g, loss_target, m_w_ada, m_b_ada, m_norm1_g, m_w_in, m_b_gate, m_gmlp_ln_g, m_gmlp_ln_b, m_gmlp_ws, m_gmlp_bs, m_hg_lb, m_hg_norm_g, m_w_branch_gmlp, m_w_branch_hg, m_w_out, m_norm2_g, m_w_ffn_in, m_w_ffn_out, m_final_norm_g, v_w_ada, v_b_ada, v_norm1_g, v_w_in, v_b_gate, v_gmlp_ln_g, v_gmlp_ln_b, v_gmlp_ws, v_gmlp_bs, v_hg_lb, v_hg_norm_g, v_w_branch_gmlp, v_w_branch_hg, v_w_out, v_norm2_g, v_w_ffn_in, v_w_ffn_out, v_final_norm_g):
    w = dict(w_ada=w_ada, b_ada=b_ada, norm1_g=norm1_g, w_in=w_in, b_gate=b_gate, gmlp_ln_g=gmlp_ln_g,
             gmlp_ln_b=gmlp_ln_b, gmlp_ws=gmlp_ws, gmlp_bs=gmlp_bs, hg_lb=hg_lb, hg_norm_g=hg_norm_g,
             w_branch_gmlp=w_branch_gmlp, w_branch_hg=w_branch_hg, w_out=w_out, norm2_g=norm2_g,
             w_ffn_in=w_ffn_in, w_ffn_out=w_ffn_out, final_norm_g=final_norm_g)
    m = dict(w_ada=m_w_ada, b_ada=m_b_ada, norm1_g=m_norm1_g, w_in=m_w_in, b_gate=m_b_gate, gmlp_ln_g=m_gmlp_ln_g,
             gmlp_ln_b=m_gmlp_ln_b, gmlp_ws=m_gmlp_ws, gmlp_bs=m_gmlp_bs, hg_lb=m_hg_lb, hg_norm_g=m_hg_norm_g,
             w_branch_gmlp=m_w_branch_gmlp, w_branch_hg=m_w_branch_hg, w_out=m_w_out, norm2_g=m_norm2_g,
             w_ffn_in=m_w_ffn_in, w_ffn_out=m_w_ffn_out, final_norm_g=m_final_norm_g)
    v = dict(w_ada=v_w_ada, b_ada=v_b_ada, norm1_g=v_norm1_g, w_in=v_w_in, b_gate=v_b_gate, gmlp_ln_g=v_gmlp_ln_g,
             gmlp_ln_b=v_gmlp_ln_b, gmlp_ws=v_gmlp_ws, gmlp_bs=v_gmlp_bs, hg_lb=v_hg_lb, hg_norm_g=v_hg_norm_g,
             w_branch_gmlp=v_w_branch_gmlp, w_branch_hg=v_w_branch_hg, w_out=v_w_out, norm2_g=v_norm2_g,
             w_ffn_in=v_w_ffn_in, w_ffn_out=v_w_ffn_out, final_norm_g=v_final_norm_g)
    return _step(x, c, loss_target, w, m, v)
```

```python
import functools

import jax
import jax.numpy as jnp
from jax import lax
from jax.experimental import pallas as pl
from jax.experimental.pallas import tpu as pltpu

F32 = jnp.float32
BF16 = jnp.bfloat16
EPS = 1e-6
LANES = 128
N_CHIPS = 4
N_DEV = 8
VMEM_LIMIT_BYTES = 56 * 1024 * 1024
HG_CHUNK = 32
HG_ROWS = 256
EXP_CLAMP = 80.0
ADAM_LR, ADAM_B1, ADAM_B2, ADAM_EPS, ADAM_WD, ADAM_STEP = 0.001, 0.9, 0.999, 1e-08, 0.01, 10
MESH = pl.DeviceIdType.MESH

NN = (((1,), (0,)), ((), ()))
NT = (((1,), (1,)), ((), ()))
TN = (((0,), (0,)), ((), ()))


def _dot(a, b, dims=NN, precision=None):
    return lax.dot_general(a, b, dims, precision=precision, preferred_element_type=F32)


def _params(*semantics):
    return pltpu.CompilerParams(dimension_semantics=semantics, vmem_limit_bytes=VMEM_LIMIT_BYTES)


class _Order:
    last = None


def _pcall(body, *, in_specs, out_specs, grid=(), scratch_shapes=(), num_scalar_prefetch=0, **kw):
    def run(*ins):
        deps = () if _Order.last is None else (_Order.last,)
        n_in, n_dep = len(ins), len(deps)

        def wrapped(*refs):
            body(*refs[:n_in], *refs[n_in + n_dep:])

        specs = list(in_specs) + [pl.BlockSpec(memory_space=pl.ANY)] * n_dep
        if num_scalar_prefetch:
            grid_spec = pltpu.PrefetchScalarGridSpec(
                num_scalar_prefetch=num_scalar_prefetch, grid=grid, in_specs=specs, out_specs=out_specs,
                scratch_shapes=scratch_shapes)
            outs = pl.pallas_call(wrapped, grid_spec=grid_spec, **kw)(*ins, *deps)
        else:
            outs = pl.pallas_call(wrapped, grid=grid, in_specs=specs, out_specs=out_specs,
                                  scratch_shapes=scratch_shapes, **kw)(*ins, *deps)
        _Order.last = jax.tree.leaves(outs)[0]
        return outs

    return run


def _pick_rows(dim, pref, mult=16):
    best = None
    for cand in range(mult, min(dim, pref) + 1, mult):
        if dim % cand == 0:
            best = cand
    assert best is not None, (dim, pref)
    return best


def _pick(dim, pref):
    if dim <= pref:
        return dim
    best = None
    for cand in range(LANES, pref + 1, LANES):
        if dim % cand == 0:
            best = cand
    assert best is not None, (dim, pref)
    return best


def _sigmoid(x):
    return 1.0 / (1.0 + jnp.exp(-x))


def _gelu(x):
    c = 0.7978845608028654
    return 0.5 * x * (1.0 + jnp.tanh(c * (x + 0.044715 * x * x * x)))


def _gelu_grad(x):
    c = 0.7978845608028654
    t = jnp.tanh(c * (x + 0.044715 * x * x * x))
    return 0.5 * (1.0 + t) + 0.5 * x * (1.0 - t * t) * c * (1.0 + 3.0 * 0.044715 * x * x)


def _rms(x):
    r = lax.rsqrt(jnp.mean(x * x, axis=-1, keepdims=True) + EPS)
    return x * r, r


def _colsum(x):
    return jnp.sum(x, axis=0, keepdims=True)


def _accumulate(first, ref, val):
    @pl.when(first)
    def _():
        ref[...] = val

    @pl.when(jnp.logical_not(first))
    def _():
        ref[...] += val


def _matmul(a, b, *, mode, name, out_dtype, b_slots=False, out_slots=False, bm=1024, bn=1024, bk=2816):
    if mode == "nn":
        m, k = a.shape
        n = b.shape[2] * N_CHIPS if b_slots else b.shape[1]
        per = b.shape[2] if b_slots else n
    elif mode == "nt":
        m, k = a.shape
        n = b.shape[1] if b_slots else b.shape[0]
        per = b.shape[2] if b_slots else k
    else:
        k, m = a.shape
        n = b.shape[1]
        per = n // N_CHIPS if out_slots else n
    bm = _pick(m, bm)
    if mode == "nt":
        bn, bk = _pick(n, bn), _pick(per, bk)
    else:
        bn, bk = _pick(per, bn), _pick(k, bk)
    nk = k // bk
    per_blocks = per // (bk if mode == "nt" else bn)
    dims = {"nn": NN, "nt": NT, "tn": TN}[mode]

    def body(a_ref, b_ref, o_ref, *acc):
        part = _dot(a_ref[...], b_ref[...], dims)
        if nk == 1:
            o_ref[...] = part.astype(o_ref.dtype)
            return
        acc_ref, kk = acc[0], pl.program_id(2)

        @pl.when(kk == 0)
        def _():
            acc_ref[...] = part

        @pl.when(jnp.logical_and(kk > 0, kk < nk - 1))
        def _():
            acc_ref[...] += part

        @pl.when(kk == nk - 1)
        def _():
            o_ref[...] = (acc_ref[...] + part).astype(o_ref.dtype)

    if mode == "nn":
        a_spec = pl.BlockSpec((bm, bk), lambda i, j, kk: (i, kk))
        if b_slots:
            b_spec = pl.BlockSpec((None, bk, bn), lambda i, j, kk: (j // per_blocks, kk, j % per_blocks))
        else:
            b_spec = pl.BlockSpec((bk, bn), lambda i, j, kk: (kk, j))
    elif mode == "nt":
        a_spec = pl.BlockSpec((bm, bk), lambda i, j, kk: (i, kk))
        if b_slots:
            b_spec = pl.BlockSpec((None, bn, bk), lambda i, j, kk: (kk // per_blocks, j, kk % per_blocks))
        else:
            b_spec = pl.BlockSpec((bn, bk), lambda i, j, kk: (j, kk))
    else:
        a_spec = pl.BlockSpec((bk, bm), lambda i, j, kk: (kk, i))
        b_spec = pl.BlockSpec((bk, bn), lambda i, j, kk: (kk, j))
    if out_slots:
        o_spec = pl.BlockSpec((None, bm, bn), lambda i, j, kk: (j // per_blocks, i, j % per_blocks))
        out_shape = jax.ShapeDtypeStruct((N_CHIPS, m, per), out_dtype)
    else:
        o_spec = pl.BlockSpec((bm, bn), lambda i, j, kk: (i, j))
        out_shape = jax.ShapeDtypeStruct((m, n), out_dtype)
    return _pcall(
        body, name=name, grid=(m // bm, n // bn, nk), in_specs=[a_spec, b_spec], out_specs=o_spec,
        out_shape=out_shape, scratch_shapes=[pltpu.VMEM((bm, bn), F32)] if nk > 1 else [],
        compiler_params=_params("parallel", "parallel", "arbitrary"),
    )(a, b)


def _place():
    x, y, c = lax.axis_index("x"), lax.axis_index("y"), lax.axis_index("c")
    chips = [(1 - x, y), (x, 1 - y), (1 - x, 1 - y)]
    return x, y, c, chips


def _all_gather8(block, name):
    def body(x_ref, out_ref, send_sems, recv_sems, local_sem):
        x, y, c, chips = _place()
        me, sibling = (x, y, c), (x, y, 1 - c)

        def slot(px, py, pc):
            return out_ref.at[4 * px + 2 * py + pc]

        def copy(k, blk, to, src=None):
            return pltpu.make_async_remote_copy(
                src_ref=slot(*blk) if src is None else src, dst_ref=slot(*blk),
                send_sem=send_sems.at[k], recv_sem=recv_sems.at[k], device_id=to, device_id_type=MESH)

        mine = pltpu.make_async_copy(x_ref, slot(*me), local_sem)
        mine.start()
        first = [copy(0, me, sibling, src=x_ref)]
        first += [copy(1 + j, me, (*chip, c), src=x_ref) for j, chip in enumerate(chips)]
        for cp in first:
            cp.start()
        passed = [copy(4 + j, (*chip, c), sibling) for j, chip in enumerate(chips)]
        for j, chip in enumerate(chips):
            copy(1 + j, (*chip, c), me).wait_recv()
            passed[j].start()
        copy(0, sibling, me).wait_recv()
        for j, chip in enumerate(chips):
            copy(4 + j, (*chip, 1 - c), me).wait_recv()
        for cp in first + passed:
            cp.wait_send()
        mine.wait()

    return _pcall(
        body, name=name, out_shape=jax.ShapeDtypeStruct((N_DEV,) + block.shape, block.dtype),
        in_specs=[pl.BlockSpec(memory_space=pltpu.VMEM)], out_specs=pl.BlockSpec(memory_space=pltpu.VMEM),
        scratch_shapes=[pltpu.SemaphoreType.DMA((7,)), pltpu.SemaphoreType.DMA((7,)), pltpu.SemaphoreType.DMA],
        compiler_params=pltpu.CompilerParams(vmem_limit_bytes=VMEM_LIMIT_BYTES),
    )(block)


HBM_SPEC = pl.BlockSpec(memory_space=pltpu.HBM)
SEM_SPEC = pl.BlockSpec(memory_space=pltpu.SEMAPHORE)
ANY_SPEC = pl.BlockSpec(memory_space=pl.ANY)
EFFECT = pltpu.SideEffectType.DATAFLOW_SIDE_EFFECTING


def _xfer_start(name, bufs, plan, n_copies, after_last=False):
    nb = len(bufs)
    deps = (_Order.last,) if after_last and _Order.last is not None else ()
    nd = len(deps)

    def body(*refs):
        send_sems, recv_sems = refs[nb + nd], refs[nb + nd + 1]
        token = refs[nb + nd + 2 + nb]
        for k, (src, dst, dev) in enumerate(plan(refs[:nb], *_place())):
            pltpu.make_async_remote_copy(src_ref=src, dst_ref=dst, send_sem=send_sems.at[k], recv_sem=recv_sems.at[k],
                                         device_id=dev, device_id_type=MESH).start()
        token[...] = jnp.zeros_like(token)

    outs = pl.pallas_call(
        body, name=name,
        out_shape=(pltpu.SemaphoreType.DMA((n_copies,)), pltpu.SemaphoreType.DMA((n_copies,)),
                   *[pltpu.HBM(b.shape, b.dtype) for b in bufs], jax.ShapeDtypeStruct((8, LANES), F32)),
        in_specs=[HBM_SPEC] * nb + [ANY_SPEC] * nd,
        out_specs=(SEM_SPEC, SEM_SPEC, *[HBM_SPEC] * nb, pl.BlockSpec(memory_space=pltpu.VMEM)),
        input_output_aliases={i: 2 + i for i in range(nb)},
        compiler_params=pltpu.CompilerParams(has_side_effects=EFFECT),
    )(*[pltpu.with_memory_space_constraint(b, pltpu.HBM) for b in bufs], *deps)
    _Order.last = outs[-1]
    return (outs[0], outs[1]), list(outs[2:2 + nb])


def _xfer_wait(name, sems, bufs, plan):
    nb = len(bufs)

    def body(*refs):
        send_sems, recv_sems = refs[nb], refs[nb + 1]
        for k, (src, dst, dev) in enumerate(plan(refs[:nb], *_place())):
            copy = pltpu.make_async_remote_copy(src_ref=src, dst_ref=dst, send_sem=send_sems.at[k],
                                                recv_sem=recv_sems.at[k], device_id=dev, device_id_type=MESH)
            copy.wait_send()
            copy.wait_recv()

    outs = pl.pallas_call(
        body, name=name, out_shape=tuple(pltpu.HBM(b.shape, b.dtype) for b in bufs),
        in_specs=[HBM_SPEC] * nb + [SEM_SPEC, SEM_SPEC, ANY_SPEC], out_specs=tuple([HBM_SPEC] * nb),
        input_output_aliases={i: i for i in range(nb)},
        compiler_params=pltpu.CompilerParams(has_side_effects=EFFECT),
    )(*bufs, *sems, _Order.last)
    _Order.last = outs[0]
    return list(outs)


def _half(ref, c, axis):
    rows = ref.shape[axis] // 2
    return pl.ds(c * rows, rows)


def _plan_weights_ici(n):
    def plan(refs, x, y, c, chips):
        out = []
        for w in range(n):
            region = refs[w].at[2 * x + y, _half(refs[w], c, 1), :]
            out += [(region, region, (*chip, c)) for chip in chips]
        return out
    return plan


def _plan_weights_d2d(n):
    def plan(refs, x, y, c, chips):
        out = []
        for w in range(n):
            rows = _half(refs[w], c, 1)
            for chip in chips:
                region = refs[w].at[2 * chip[0] + chip[1], rows, :]
                out.append((region, region, (x, y, 1 - c)))
        return out
    return plan


def _plan_grads_d2d(n):
    def plan(refs, x, y, c, chips):
        return [(refs[w].at[:, _half(refs[w], 1 - c, 1), :], refs[n + w], (x, y, 1 - c)) for w in range(n)]
    return plan


def _plan_grads_ici(n):
    def plan(refs, x, y, c, chips):
        out = []
        for w in range(n):
            out += [(refs[w].at[2 * chip[0] + chip[1]], refs[n + w].at[2 * x + y], (*chip, c)) for chip in chips]
        return out
    return plan


def _plan_final_d2d(n):
    def plan(refs, x, y, c, chips):
        out = []
        for w in range(n):
            region = refs[w].at[_half(refs[w], c, 0), :]
            out.append((region, region, (x, y, 1 - c)))
        return out
    return plan


def _stream_blocks(hr, cols):
    bc = cols if cols <= 4096 else _pick(cols, 4096)
    return _pick_rows(hr, max(16, (768 * 1024) // bc)), bc


def _pre_reduce(g, landed, place, name):
    _, rows, cols = g.shape
    hr = rows // 2
    rb, bc = _stream_blocks(hr, cols)
    nrb = hr // rb

    def body(place_ref, g_ref, l_ref, o_ref):
        o_ref[...] = (g_ref[...].astype(F32) + l_ref[...].astype(F32)).astype(o_ref.dtype)

    return _pcall(
        body, name=name, num_scalar_prefetch=1, grid=(N_CHIPS, nrb, cols // bc),
        in_specs=[pl.BlockSpec((None, rb, bc), lambda j, i, k, p: (j, p[1] * nrb + i, k)),
                  pl.BlockSpec((None, rb, bc), lambda j, i, k, p: (j, i, k))],
        out_specs=pl.BlockSpec((None, rb, bc), lambda j, i, k, p: (j, i, k)),
        out_shape=jax.ShapeDtypeStruct((N_CHIPS, hr, cols), g.dtype),
        compiler_params=_params("parallel", "parallel", "parallel"),
    )(place, g, landed)


def _sum_slots(mine, landed, place, name):
    _, hr, cols = mine.shape
    rb, bc = _stream_blocks(hr, cols)
    rb = _pick_rows(hr, max(16, rb // 2))
    nrb = hr // rb

    def body(place_ref, m_ref, l_ref, o_ref):
        chip = place_ref[0]
        own = m_ref[...].astype(F32)
        total = jnp.where(chip == 0, own, l_ref[0].astype(F32))
        for j in range(1, N_CHIPS):
            total = total + jnp.where(chip == j, own, l_ref[j].astype(F32))
        o_ref[...] = total

    return _pcall(
        body, name=name, num_scalar_prefetch=1, grid=(nrb, cols // bc),
        in_specs=[pl.BlockSpec((None, rb, bc), lambda i, k, p: (p[0], i, k)),
                  pl.BlockSpec((N_CHIPS, rb, bc), lambda i, k, p: (0, i, k))],
        out_specs=pl.BlockSpec((rb, bc), lambda i, k, p: (p[1] * nrb + i, k)),
        out_shape=jax.ShapeDtypeStruct((2 * hr, cols), F32),
        compiler_params=_params("parallel", "parallel"),
    )(place, mine, landed)


class _WeightGather:
    def __init__(self, tag, quarters, chip):
        self.tag, self.n = tag, len(quarters)
        zones = [lax.dynamic_update_slice(lax.empty((N_CHIPS,) + q.shape, BF16), q.astype(BF16)[None], (chip, 0, 0))
                 for q in quarters]
        self.plan = _plan_weights_ici(self.n)
        self.sems, self.bufs = _xfer_start("wici_start_" + tag, zones, self.plan, 3 * self.n, after_last=True)

    def pass_on(self):
        bufs = _xfer_wait("wici_wait_" + self.tag, self.sems, self.bufs, self.plan)
        self.plan = _plan_weights_d2d(self.n)
        self.sems, self.bufs = _xfer_start("wd2d_start_" + self.tag, bufs, self.plan, 3 * self.n)

    def done(self):
        return _xfer_wait("wd2d_wait_" + self.tag, self.sems, self.bufs, self.plan)


class _GradReduce:
    def __init__(self, tag, grads):
        self.tag, self.n = tag, len(grads)
        zones = [lax.empty((N_CHIPS, g.shape[1] // 2, g.shape[2]), g.dtype) for g in grads]
        self.plan = _plan_grads_d2d(self.n)
        self.sems, self.bufs = _xfer_start("gd2d_start_" + tag, list(grads) + zones, self.plan, self.n)

    def pair(self, place):
        n = self.n
        bufs = _xfer_wait("gd2d_wait_" + self.tag, self.sems, self.bufs, self.plan)
        self.halves = [_pre_reduce(bufs[w], bufs[n + w], place, f"pre_reduce_{self.tag}{w}") for w in range(n)]

    def cross(self, after_last=False):
        zones = [lax.empty(h.shape, h.dtype) for h in self.halves]
        self.plan = _plan_grads_ici(self.n)
        self.sems, self.bufs = _xfer_start("gici_start_" + self.tag, self.halves + zones, self.plan, 3 * self.n,
                                           after_last=after_last)

    def step(self, place):
        self.pair(place)
        self.cross()

    def join(self, place):
        n = self.n
        bufs = _xfer_wait("gici_wait_" + self.tag, self.sems, self.bufs, self.plan)
        sums = [_sum_slots(bufs[w], bufs[n + w], place, f"sum_slots_{self.tag}{w}") for w in range(n)]
        self.plan = _plan_final_d2d(n)
        self.sems, self.bufs = _xfer_start("gfin_start_" + self.tag, sums, self.plan, n)

    def done(self):
        return _xfer_wait("gfin_wait_" + self.tag, self.sems, self.bufs, self.plan)


def _ada_fwd(c_all, w_q, b_q):
    d, n = w_q.shape
    bn = _pick(n, 512)

    def body(c_ref, w_ref, b_ref, o_ref):
        cv = c_ref[...]
        act = cv * _sigmoid(cv)
        o_ref[...] = _dot(act, w_ref[...], NN, lax.Precision.HIGHEST) + b_ref[...]

    return _pcall(
        body, name="ada_fwd", grid=(n // bn,),
        in_specs=[pl.BlockSpec((N_DEV, d), lambda j: (0, 0)), pl.BlockSpec((d, bn), lambda j: (0, j)),
                  pl.BlockSpec((1, bn), lambda j: (0, j))],
        out_specs=pl.BlockSpec((N_DEV, bn), lambda j: (0, j)),
        out_shape=jax.ShapeDtypeStruct((N_DEV, n), F32), compiler_params=_params("parallel"),
    )(c_all, w_q, b_q)


def _row_spec(rb, width, col=0):
    return pl.BlockSpec((rb, width), lambda i, col=col: (i, col))


def _vec_spec(width, col=0):
    return pl.BlockSpec((1, width), lambda i, col=col: (0, col))


def _norm_mod_fwd(x, g, sc, sh, name, res=None, gt=None):
    s, d = x.shape
    rb = _pick(s, 256)
    has_res = res is not None

    def body(*refs):
        if has_res:
            x_ref, res_ref, gt_ref, g_ref, sc_ref, sh_ref, x1_ref, h_ref = refs
            xv = x_ref[...] + gt_ref[...] * res_ref[...]
            x1_ref[...] = xv
        else:
            x_ref, g_ref, sc_ref, sh_ref, h_ref = refs
            xv = x_ref[...]
        xh, _ = _rms(xv)
        h_ref[...] = (xh * g_ref[...] * (1.0 + sc_ref[...]) + sh_ref[...]).astype(BF16)

    row, vec = _row_spec(rb, d), _vec_spec(d)
    if has_res:
        ins, in_specs = (x, res, gt, g, sc, sh), [row, row, vec, vec, vec, vec]
        out_shape = [jax.ShapeDtypeStruct((s, d), F32), jax.ShapeDtypeStruct((s, d), BF16)]
        out_specs = [row, row]
    else:
        ins, in_specs = (x, g, sc, sh), [row, vec, vec, vec]
        out_shape, out_specs = jax.ShapeDtypeStruct((s, d), BF16), row
    return _pcall(body, name=name, grid=(s // rb,), in_specs=in_specs, out_specs=out_specs,
                          out_shape=out_shape, compiler_params=_params("parallel"))(*ins)


def _final_loss(x1, f, gt2, g_final, target):
    s, d = x1.shape
    rb = _pick(s, 256)

    def body(x1_ref, f_ref, gt_ref, g_ref, t_ref, dx_ref, df_ref, loss_ref, dg_ref, dgt_ref):
        first = pl.program_id(0) == 0
        fv, gt, gv = f_ref[...], gt_ref[...], g_ref[...]
        x2 = x1_ref[...] + gt * fv
        xh, r = _rms(x2)
        err = xh * gv - t_ref[...]
        blk = 0.5 * jnp.sum(jnp.sum(err * err, axis=1, keepdims=True), axis=0, keepdims=True) / d
        dy = err / d
        dxh = dy * gv
        dx = r * (dxh - xh * jnp.mean(dxh * xh, axis=-1, keepdims=True))
        dx_ref[...] = dx
        df_ref[...] = (dx * gt).astype(BF16)
        _accumulate(first, loss_ref, jnp.broadcast_to(blk, (1, LANES)))
        _accumulate(first, dg_ref, _colsum(dy * xh))
        _accumulate(first, dgt_ref, _colsum(dx * fv))

    row, vec = _row_spec(rb, d), _vec_spec(d)
    return _pcall(
        body, name="final_loss", grid=(s // rb,), in_specs=[row, row, vec, vec, row],
        out_specs=[row, row, _vec_spec(LANES), vec, vec],
        out_shape=[jax.ShapeDtypeStruct((s, d), F32), jax.ShapeDtypeStruct((s, d), BF16),
                   jax.ShapeDtypeStruct((1, LANES), F32), jax.ShapeDtypeStruct((1, d), F32),
                   jax.ShapeDtypeStruct((1, d), F32)],
        compiler_params=_params("arbitrary"),
    )(x1, f, gt2, g_final, target)


def _norm_mod_bwd(dh, xin, dres, g, sc, name, branch=None, gt=None):
    s, d = xin.shape
    rb = _pick(s, 256)
    has_branch = branch is not None

    def body(*refs):
        if has_branch:
            dh_ref, x_ref, dres_ref, g_ref, sc_ref, br_ref, gt_ref, dx_ref, dbr_ref, dsh_ref, dsc_ref, dg_ref, dgt_ref = refs
        else:
            dh_ref, x_ref, dres_ref, g_ref, sc_ref, dx_ref, dsh_ref, dsc_ref, dg_ref = refs
        first = pl.program_id(0) == 0
        gv = g_ref[...]
        xh, r = _rms(x_ref[...])
        dhv = dh_ref[...]
        dn = dhv * (1.0 + sc_ref[...])
        dxh = dn * gv
        dx = dres_ref[...] + r * (dxh - xh * jnp.mean(dxh * xh, axis=-1, keepdims=True))
        dx_ref[...] = dx
        _accumulate(first, dsh_ref, _colsum(dhv))
        _accumulate(first, dsc_ref, _colsum(dhv * xh * gv))
        _accumulate(first, dg_ref, _colsum(dn * xh))
        if has_branch:
            dbr_ref[...] = (dx * gt_ref[...]).astype(BF16)
            _accumulate(first, dgt_ref, _colsum(dx * br_ref[...]))

    row, vec = _row_spec(rb, d), _vec_spec(d)
    vec_shape = jax.ShapeDtypeStruct((1, d), F32)
    if has_branch:
        ins, in_specs = (dh, xin, dres, g, sc, branch, gt), [row, row, row, vec, vec, row, vec]
        out_specs = [row, row, vec, vec, vec, vec]
        out_shape = [jax.ShapeDtypeStruct((s, d), F32), jax.ShapeDtypeStruct((s, d), BF16)] + [vec_shape] * 4
    else:
        ins, in_specs = (dh, xin, dres, g, sc), [row, row, row, vec, vec]
        out_specs = [row, vec, vec, vec]
        out_shape = [jax.ShapeDtypeStruct((s, d), F32)] + [vec_shape] * 3
    return _pcall(body, name=name, grid=(s // rb,), in_specs=in_specs, out_specs=out_specs,
                          out_shape=out_shape, compiler_params=_params("arbitrary"))(*ins)


def _gate_fwd(pa, pb, z, b_gate, gate_col):
    s, d = pa.shape
    rb = _pick(s, 256)

    def body(pa_ref, pb_ref, za_ref, zb_ref, ba_ref, bb_ref, y_ref):
        ga = _sigmoid(za_ref[...] + ba_ref[...])
        gb = _sigmoid(zb_ref[...] + bb_ref[...])
        y_ref[...] = (ga * pa_ref[...].astype(F32) + gb * pb_ref[...].astype(F32)).astype(BF16)

    row = _row_spec(rb, d)
    return _pcall(
        body, name="gate_fwd", grid=(s // rb,),
        in_specs=[row, row, _row_spec(rb, d, gate_col), _row_spec(rb, d, gate_col + 1), _vec_spec(d, 0), _vec_spec(d, 1)],
        out_specs=row, out_shape=jax.ShapeDtypeStruct((s, d), BF16), compiler_params=_params("parallel"),
    )(pa, pb, z, z, b_gate, b_gate)


def _gate_bwd(dy, pa, pb, z, b_gate, gate_col):
    s, d = pa.shape
    rb = _pick(s, 256)

    def body(dy_ref, pa_ref, pb_ref, za_ref, zb_ref, ba_ref, bb_ref, dpa_ref, dpb_ref, dz_ref, db_ref):
        first = pl.program_id(0) == 0
        dyv = dy_ref[...]
        ga = _sigmoid(za_ref[...] + ba_ref[...])
        gb = _sigmoid(zb_ref[...] + bb_ref[...])
        dpa_ref[...] = (dyv * ga).astype(BF16)
        dpb_ref[...] = (dyv * gb).astype(BF16)
        dga = dyv * pa_ref[...].astype(F32) * ga * (1.0 - ga)
        dgb = dyv * pb_ref[...].astype(F32) * gb * (1.0 - gb)
        dz_ref[:, :d] = dga.astype(BF16)
        dz_ref[:, d:] = dgb.astype(BF16)

        @pl.when(first)
        def _():
            db_ref[:, :d] = _colsum(dga)
            db_ref[:, d:] = _colsum(dgb)

        @pl.when(jnp.logical_not(first))
        def _():
            db_ref[:, :d] += _colsum(dga)
            db_ref[:, d:] += _colsum(dgb)

    row = _row_spec(rb, d)
    return _pcall(
        body, name="gate_bwd", grid=(s // rb,),
        in_specs=[row, row, row, _row_spec(rb, d, gate_col), _row_spec(rb, d, gate_col + 1), _vec_spec(d, 0), _vec_spec(d, 1)],
        out_specs=[row, row, _row_spec(rb, 2 * d), _vec_spec(2 * d)],
        out_shape=[jax.ShapeDtypeStruct((s, d), BF16), jax.ShapeDtypeStruct((s, d), BF16),
                   jax.ShapeDtypeStruct((s, 2 * d), BF16), jax.ShapeDtypeStruct((1, 2 * d), F32)],
        compiler_params=_params("arbitrary"),
    )(dy, pa, pb, z, z, b_gate, b_gate)


def _ffn_in(h, w_fi, bm=512, bn=1408):
    s, d = h.shape
    per = w_fi.shape[2]
    ff = 2 * per
    bm, bn = _pick(s, bm), _pick(per, bn)
    per_blocks = per // bn

    def body(h_ref, wa_ref, wu_ref, a_ref, u_ref, hf_ref):
        hv = h_ref[...]
        a = _dot(hv, wa_ref[...])
        up = _dot(hv, wu_ref[...])
        a_ref[...] = a.astype(BF16)
        u_ref[...] = up.astype(BF16)
        hf_ref[...] = (a * _sigmoid(a) * up).astype(BF16)

    out = pl.BlockSpec((bm, bn), lambda j, i: (i, j))
    shape = jax.ShapeDtypeStruct((s, ff), BF16)
    return _pcall(
        body, name="ffn_in", grid=(ff // bn, s // bm),
        in_specs=[pl.BlockSpec((bm, d), lambda j, i: (i, 0)),
                  pl.BlockSpec((None, d, bn), lambda j, i: (j // per_blocks, 0, j % per_blocks)),
                  pl.BlockSpec((None, d, bn), lambda j, i: (2 + j // per_blocks, 0, j % per_blocks))],
        out_specs=[out, out, out], out_shape=[shape, shape, shape],
        compiler_params=_params("parallel", "parallel"),
    )(h, w_fi, w_fi)


def _swiglu_bwd(dhf, a_act, up_act):
    s, ff = a_act.shape
    rb = _pick(s, 128)

    def body(d_ref, a_ref, u_ref, o_ref):
        a, dv = a_ref[...].astype(F32), d_ref[...].astype(F32)
        sa = _sigmoid(a)
        o_ref[:, :ff] = (dv * u_ref[...].astype(F32) * sa * (1.0 + a * (1.0 - sa))).astype(BF16)
        o_ref[:, ff:] = (dv * a * sa).astype(BF16)

    return _pcall(
        body, name="swiglu_bwd", grid=(s // rb,),
        in_specs=[_row_spec(rb, ff), _row_spec(rb, ff), _row_spec(rb, ff)],
        out_specs=_row_spec(rb, 2 * ff), out_shape=jax.ShapeDtypeStruct((s, 2 * ff), BF16),
        compiler_params=_params("parallel"),
    )(dhf, a_act, up_act)


def _tril(n):
    return lax.broadcasted_iota(jnp.int32, (n, n), 0) >= lax.broadcasted_iota(jnp.int32, (n, n), 1)


def _gmlp_norm(v, ln_g, ln_b):
    gv = _gelu(v)
    mu = jnp.mean(gv, axis=-1, keepdims=True)
    cen = gv - mu
    rs = lax.rsqrt(jnp.mean(cen * cen, axis=-1, keepdims=True) + EPS)
    xh = cen * rs
    return xh, rs, xh * ln_g + ln_b


def _gmlp_fwd(z, ln_g, ln_b, ws, bs_t):
    s = z.shape[0]
    gw = ln_g.shape[1]
    groups, chunk, _ = ws.shape

    def body(u_ref, v_ref, lg_ref, lb_ref, ws_ref, bs_ref, ya_ref):
        gu = _gelu(u_ref[...])
        _, _, vn = _gmlp_norm(v_ref[...], lg_ref[...], lb_ref[...])
        mask = _tril(chunk)
        for g in range(groups):
            cols = slice(g * LANES, (g + 1) * LANES)
            wm = jnp.where(mask, ws_ref[g], 0.0).astype(BF16)
            sg = _dot(wm, vn[:, cols].astype(BF16)) + bs_ref[:, g:g + 1]
            ya_ref[:, cols] = (gu[:, cols] * sg).astype(BF16)

    return _pcall(
        body, name="gmlp_fwd", grid=(s // chunk,),
        in_specs=[_row_spec(chunk, gw, 0), _row_spec(chunk, gw, 1), _vec_spec(gw), _vec_spec(gw),
                  pl.BlockSpec((groups, chunk, chunk), lambda i: (0, 0, 0)), pl.BlockSpec((chunk, LANES), lambda i: (0, 0))],
        out_specs=_row_spec(chunk, gw), out_shape=jax.ShapeDtypeStruct((s, gw), BF16),
        compiler_params=_params("parallel"),
    )(z, z, ln_g, ln_b, ws, bs_t)


def _gmlp_bwd(dya, z, ln_g, ln_b, ws, bs_t):
    s = z.shape[0]
    gw = ln_g.shape[1]
    groups, chunk, _ = ws.shape

    def body(dya_ref, u_ref, v_ref, lg_ref, lb_ref, ws_ref, bs_ref, duv_ref, dws_ref, dbs_ref, dlg_ref, dlb_ref, dvn_ref):
        first = pl.program_id(0) == 0
        u, v, lg = u_ref[...], v_ref[...], lg_ref[...]
        gu = _gelu(u)
        xh, rs, vn = _gmlp_norm(v, lg, lb_ref[...])
        dyav = dya_ref[...]
        mask = _tril(chunk)
        lane = lax.broadcasted_iota(jnp.int32, (chunk, LANES), 1)
        dbs = jnp.zeros((chunk, LANES), F32)
        for g in range(groups):
            cols = slice(g * LANES, (g + 1) * LANES)
            wm = jnp.where(mask, ws_ref[g], 0.0).astype(BF16)
            vg = vn[:, cols].astype(BF16)
            sg = _dot(wm, vg) + bs_ref[:, g:g + 1]
            ds = dyav[:, cols] * gu[:, cols]
            duv_ref[:, cols] = (dyav[:, cols] * sg * _gelu_grad(u[:, cols])).astype(BF16)
            dsb = ds.astype(BF16)
            _accumulate(first, dws_ref.at[g], jnp.where(mask, _dot(dsb, vg, NT), 0.0))
            dbs = dbs + jnp.where(lane == g, jnp.sum(ds, axis=-1, keepdims=True), 0.0)
            dvn_ref[:, cols] = _dot(wm, dsb, TN)
        dvn = dvn_ref[...]
        _accumulate(first, dbs_ref, dbs)
        _accumulate(first, dlb_ref, _colsum(dvn))
        _accumulate(first, dlg_ref, _colsum(dvn * xh))
        dxh = dvn * lg
        dgv = rs * (dxh - jnp.mean(dxh, axis=-1, keepdims=True) - xh * jnp.mean(dxh * xh, axis=-1, keepdims=True))
        duv_ref[:, gw:] = (dgv * _gelu_grad(v)).astype(BF16)

    return _pcall(
        body, name="gmlp_bwd", grid=(s // chunk,),
        in_specs=[_row_spec(chunk, gw), _row_spec(chunk, gw, 0), _row_spec(chunk, gw, 1), _vec_spec(gw), _vec_spec(gw),
                  pl.BlockSpec((groups, chunk, chunk), lambda i: (0, 0, 0)), pl.BlockSpec((chunk, LANES), lambda i: (0, 0))],
        out_specs=[_row_spec(chunk, 2 * gw), pl.BlockSpec((groups, chunk, chunk), lambda i: (0, 0, 0)),
                   pl.BlockSpec((chunk, LANES), lambda i: (0, 0)), _vec_spec(gw), _vec_spec(gw)],
        out_shape=[jax.ShapeDtypeStruct((s, 2 * gw), BF16), jax.ShapeDtypeStruct((groups, chunk, chunk), F32),
                   jax.ShapeDtypeStruct((chunk, LANES), F32), jax.ShapeDtypeStruct((1, gw), F32),
                   jax.ShapeDtypeStruct((1, gw), F32)],
        scratch_shapes=[pltpu.VMEM((chunk, gw), F32)],
        compiler_params=_params("arbitrary"),
    )(dya, z, z, ln_g, ln_b, ws, bs_t)


def _lower_bound(lb_ref):
    a0, a1 = lb_ref[0:1, :], lb_ref[1:2, :]
    mx = jnp.maximum(a0, a1)
    e0, e1 = jnp.exp(a0 - mx), jnp.exp(a1 - mx)
    return e0 / (e0 + e1)


def _sum_dot(mask, x):
    hi = x.astype(BF16)
    rest = x - hi.astype(F32)
    mid = rest.astype(BF16)
    low = (rest - mid.astype(F32)).astype(BF16)
    return _dot(mask, hi) + _dot(mask, mid) + _dot(mask, low)


def _ones_where(mask):
    return jnp.where(mask, 1.0, 0.0).astype(BF16)


def _hg_masks(rows, t):
    r = lax.broadcasted_iota(jnp.int32, (rows, rows), 0)
    c = lax.broadcasted_iota(jnp.int32, (rows, rows), 1)
    same = (r // t) == (c // t)
    incl = jnp.logical_and(same, c <= r)
    upto_mid = jnp.logical_and(same, (c % t) <= t // 2)
    rev = jnp.logical_and(same, c >= r)
    return same, incl, upto_mid, rev


def _hg_block(q, fp, lb, masks):
    rows = q.shape[0]
    same, incl, upto_mid, _ = masks
    sig = _sigmoid(fp)
    f = lb + (1.0 - lb) * sig
    k = 1.0 - f
    sq = _sigmoid(q)
    qa = q * sq
    stacked = jnp.concatenate([_ones_where(m) for m in (incl, same, upto_mid)], axis=0)
    sums = _sum_dot(stacked, jnp.log(f))
    b, b_last, b_mid = sums[:rows], sums[rows:2 * rows], sums[2 * rows:]
    e_q = jnp.exp(jnp.minimum(b - b_mid, EXP_CLAMP))
    e_k = jnp.exp(jnp.minimum(b_mid - b, EXP_CLAMP))
    e_in = jnp.exp(b)
    e_out = jnp.exp(b_last - b)
    return dict(sig=sig, f=f, k=k, sq=sq, qa=qa, e_last=jnp.exp(b_last), e_q=e_q, e_k=e_k, e_in=e_in, e_out=e_out,
                q_hat=(qa * e_q).astype(BF16), k_hat=(k * e_k).astype(BF16),
                q_in=(qa * e_in).astype(BF16), k_out=k * e_out)


def _hgrn_fwd(z, hg_lb, norm_g, q_col):
    s = z.shape[0]
    hw = norm_g.shape[1]
    heads = hw // LANES
    t = HG_CHUNK
    rows = min(HG_ROWS, s)
    per_step = rows // t
    per = hw // LANES

    def zspec(which):
        return pl.BlockSpec((rows, LANES), lambda h, r, which=which: (r, q_col + which * per + h))

    def body(q_ref, f_ref, i_ref, g_ref, lb_ref, ng_ref, yb_ref, o_ref, st_out_ref, st_ref, e_last_ref, inter_ref):
        @pl.when(pl.program_id(1) == 0)
        def _():
            st_ref[...] = jnp.zeros_like(st_ref)

        masks = _hg_masks(rows, t)
        blk = _hg_block(q_ref[...], f_ref[...], _lower_bound(lb_ref), masks)
        iv = i_ref[...].astype(BF16)
        q_in, k_out = blk["q_in"], blk["k_out"].astype(BF16)
        e_last_ref[...] = blk["e_last"]
        attn = jnp.where(masks[1], _dot(blk["q_hat"], blk["k_hat"], NT), 0.0).astype(BF16)
        o = _dot(attn, iv)
        grown = [_dot(iv[j * t:(j + 1) * t], k_out[j * t:(j + 1) * t], TN) for j in range(per_step)]
        st = st_ref[...]
        for j in range(per_step):
            st_out_ref[j] = st
            inter_ref[j * t:(j + 1) * t, :] = _dot(q_in[j * t:(j + 1) * t], st.astype(BF16), NT)
            st = st * e_last_ref[j * t:j * t + 1, :] + grown[j]
        st_ref[...] = st
        o = o + inter_ref[...]
        o_ref[...] = o
        og = g_ref[...]
        on, _ = _rms(o)
        yb_ref[...] = (on * ng_ref[...] * (og * _sigmoid(og))).astype(BF16)

    out_row = pl.BlockSpec((rows, LANES), lambda h, r: (r, h))
    return _pcall(
        body, name="hgrn_fwd", grid=(heads, s // rows),
        in_specs=[zspec(0), zspec(1), zspec(2), zspec(3),
                  pl.BlockSpec((2, LANES), lambda h, r: (0, h)), pl.BlockSpec((1, LANES), lambda h, r: (0, h))],
        out_specs=[out_row, out_row, pl.BlockSpec((None, per_step, LANES, LANES), lambda h, r: (h, r, 0, 0))],
        out_shape=[jax.ShapeDtypeStruct((s, hw), BF16), jax.ShapeDtypeStruct((s, hw), F32),
                   jax.ShapeDtypeStruct((heads, s // t, LANES, LANES), F32)],
        scratch_shapes=[pltpu.VMEM((LANES, LANES), F32), pltpu.VMEM((rows, LANES), F32), pltpu.VMEM((rows, LANES), F32)],
        compiler_params=_params("parallel", "arbitrary"),
    )(z, z, z, z, hg_lb, norm_g)


def _hgrn_bwd(dyb, z, o_raw, states, hg_lb, norm_g, q_col):
    s = z.shape[0]
    hw = norm_g.shape[1]
    heads = hw // LANES
    t = HG_CHUNK
    rows = min(HG_ROWS, s)
    per_step = rows // t
    per = hw // LANES
    n_steps = s // rows

    def zspec(which):
        return pl.BlockSpec((rows, LANES), lambda h, r, which=which: (n_steps - 1 - r, q_col + which * per + h))

    def body(dyb_ref, q_ref, f_ref, i_ref, g_ref, o_ref, st_in_ref, lb_ref, ng_ref,
             dq_ref, df_ref, di_ref, dg_ref, dlb_ref, dng_ref, dst_ref, acc_lb_ref, acc_ng_ref,
             e_last_ref, dq_in_ref, dk_out_ref, di_inter_ref, carry_ref):
        step = pl.program_id(1)

        @pl.when(step == 0)
        def _():
            dst_ref[...] = jnp.zeros_like(dst_ref)
            acc_lb_ref[...] = jnp.zeros_like(acc_lb_ref)
            acc_ng_ref[...] = jnp.zeros_like(acc_ng_ref)

        lb = _lower_bound(lb_ref)
        ng = ng_ref[...]
        masks = _hg_masks(rows, t)
        same, incl, _, rev = masks
        q = q_ref[...]
        blk = _hg_block(q, f_ref[...], lb, masks)
        iv = i_ref[...].astype(BF16)
        o, og, dy = o_ref[...], g_ref[...], dyb_ref[...]
        so = _sigmoid(og)
        on, r = _rms(o)
        acc_ng_ref[...] += _colsum(dy * on * (og * so))
        dg_ref[...] = (dy * on * ng * so * (1.0 + og * (1.0 - so))).astype(BF16)
        don = dy * ng * (og * so)
        do = (r * (don - on * jnp.mean(don * on, axis=-1, keepdims=True))).astype(BF16)
        q_hat, k_hat, q_in, k_out = blk["q_hat"], blk["k_hat"], blk["q_in"], blk["k_out"]
        k_out_b = k_out.astype(BF16)
        attn = jnp.where(incl, _dot(q_hat, k_hat, NT), 0.0).astype(BF16)
        d_attn = jnp.where(incl, _dot(do, iv, NT), 0.0).astype(BF16)
        di_intra = _dot(attn, do, TN)
        dq_hat = _dot(d_attn, k_hat)
        dk_hat = _dot(d_attn, q_hat, TN)
        e_last_ref[...] = blk["e_last"]
        grown = [_dot(do[j * t:(j + 1) * t], q_in[j * t:(j + 1) * t], TN) for j in range(per_step)]
        dst = dst_ref[...]
        for j in reversed(range(per_step)):
            rs_ = slice(j * t, (j + 1) * t)
            e_last = e_last_ref[j * t:j * t + 1, :]
            st_prev, dst_b = st_in_ref[j], dst.astype(BF16)
            dq_in_ref[rs_, :] = _dot(do[rs_], st_prev.astype(BF16))
            dk_out_ref[rs_, :] = _dot(iv[rs_], dst_b)
            di_inter_ref[rs_, :] = _dot(k_out_b[rs_], dst_b, NT)
            carry_ref[rs_, :] = jnp.broadcast_to(e_last * _colsum(st_prev * dst), (t, LANES))
            dst = dst * e_last + grown[j]
        dst_ref[...] = dst
        di_ref[...] = (di_intra + di_inter_ref[...]).astype(BF16)
        dk_out = dk_out_ref[...]
        dqa = dq_in_ref[...] * blk["e_in"] + dq_hat * blk["e_q"]
        dk = dk_out * blk["e_out"] + dk_hat * blk["e_k"]
        db = blk["qa"] * dqa - blk["k"] * dk
        sum_mask = jnp.concatenate([_ones_where(rev), _ones_where(same)], axis=1)
        dlf = _sum_dot(sum_mask, jnp.concatenate([db, dk_out * k_out], axis=0)) + carry_ref[...]
        dfv = dlf / blk["f"] - dk
        sig, sq = blk["sig"], blk["sq"]
        df_ref[...] = (dfv * (1.0 - lb) * sig * (1.0 - sig)).astype(BF16)
        acc_lb_ref[...] += _colsum(dfv * (1.0 - sig))
        dq_ref[...] = (dqa * sq * (1.0 + q * (1.0 - sq))).astype(BF16)

        @pl.when(step == n_steps - 1)
        def _():
            d0 = acc_lb_ref[...] * lb * (1.0 - lb)
            dlb_ref[0:1, :] = d0
            dlb_ref[1:2, :] = -d0
            dng_ref[...] = acc_ng_ref[...]

    rev_row = pl.BlockSpec((rows, LANES), lambda h, r: (n_steps - 1 - r, h))
    piece = jax.ShapeDtypeStruct((s, hw), BF16)
    return _pcall(
        body, name="hgrn_bwd", grid=(heads, n_steps),
        in_specs=[rev_row, zspec(0), zspec(1), zspec(2), zspec(3), rev_row,
                  pl.BlockSpec((None, per_step, LANES, LANES), lambda h, r: (h, n_steps - 1 - r, 0, 0)),
                  pl.BlockSpec((2, LANES), lambda h, r: (0, h)), pl.BlockSpec((1, LANES), lambda h, r: (0, h))],
        out_specs=[rev_row, rev_row, rev_row, rev_row,
                   pl.BlockSpec((2, LANES), lambda h, r: (0, h)), pl.BlockSpec((1, LANES), lambda h, r: (0, h))],
        out_shape=[piece, piece, piece, piece, jax.ShapeDtypeStruct((2, hw), F32), jax.ShapeDtypeStruct((1, hw), F32)],
        scratch_shapes=[pltpu.VMEM((LANES, LANES), F32), pltpu.VMEM((1, LANES), F32), pltpu.VMEM((1, LANES), F32)]
        + [pltpu.VMEM((rows, LANES), F32)] * 5,
        compiler_params=_params("parallel", "arbitrary"),
    )(dyb, z, z, z, z, o_raw, states, hg_lb, norm_g)


def _adamw(w, m, v, parts, name, outer=False):
    rows, cols = w.shape
    bc = cols if cols <= 4096 else _pick(cols, 4096)
    rb = _pick_rows(rows, max(8, (384 * 1024) // bc), mult=8)
    if outer and rb % LANES:
        rb = rows
    m_corr = 1.0 - ADAM_B1 ** ADAM_STEP
    v_corr = 1.0 - ADAM_B2 ** ADAM_STEP

    def body(w_ref, m_ref, v_ref, *refs):
        g_ref, d_ref, mo_ref, vo_ref = refs[-4:]
        if outer:
            cv = refs[0][...]
            g = _dot(cv * _sigmoid(cv), refs[1][...], TN, lax.Precision.HIGHEST)
        else:
            p_ref = refs[0]
            g = p_ref[0].astype(F32)
            for p in range(1, p_ref.shape[0]):
                g = g + p_ref[p].astype(F32)
        m2 = ADAM_B1 * m_ref[...] + (1.0 - ADAM_B1) * g
        v2 = ADAM_B2 * v_ref[...] + (1.0 - ADAM_B2) * (g * g)
        g_ref[...] = g
        mo_ref[...] = m2
        vo_ref[...] = v2
        d_ref[...] = -ADAM_LR * ((m2 / m_corr) / (jnp.sqrt(v2 / v_corr) + ADAM_EPS) + ADAM_WD * w_ref[...])

    blk = pl.BlockSpec((rb, bc), lambda i, j: (i, j))
    out = jax.ShapeDtypeStruct((rows, cols), F32)
    if outer:
        grad_specs = [pl.BlockSpec((N_DEV, rb), lambda i, j: (0, i)), pl.BlockSpec((N_DEV, bc), lambda i, j: (0, j))]
        grad_ins = tuple(parts)
    else:
        grad_specs = [pl.BlockSpec((parts.shape[0], rb, bc), lambda i, j: (0, i, j))]
        grad_ins = (parts,)
    return _pcall(
        body, name=name, grid=(rows // rb, cols // bc), in_specs=[blk, blk, blk] + grad_specs,
        out_specs=[blk] * 4, out_shape=[out] * 4, compiler_params=_params("parallel", "parallel"),
    )(w, m, v, *grad_ins)


SMALL = ("b_ada", "norm1_g", "b_gate", "gmlp_ln_g", "gmlp_ln_b", "gmlp_ws", "gmlp_bs", "hg_lb", "hg_norm_g",
         "norm2_g", "final_norm_g")
BIG = ("w_in", "w_branch_gmlp", "w_branch_hg", "w_out", "w_ffn_in", "w_ffn_out")
WEIGHTS = ("w_ada", "b_ada", "norm1_g", "w_in", "b_gate", "gmlp_ln_g", "gmlp_ln_b", "gmlp_ws", "gmlp_bs", "hg_lb",
           "hg_norm_g", "w_branch_gmlp", "w_branch_hg", "w_out", "norm2_g", "w_ffn_in", "w_ffn_out", "final_norm_g")


def _pack(parts):
    return jnp.concatenate([p.reshape(-1, LANES) for p in parts], axis=0)


def _step(x, c, loss_target, w, m, v):
    s, d = x.shape[1], x.shape[2]
    gw = w["gmlp_ln_g"].shape[-1]
    hw = w["hg_norm_g"].shape[-1]
    x2d, tgt = x[0], loss_target[0]
    mx, my, mc = lax.axis_index("x"), lax.axis_index("y"), lax.axis_index("c")
    chip = 2 * mx + my
    dev = 2 * chip + mc
    q_col = 2 * gw // LANES
    gate_col = (2 * gw + 4 * hw) // d
    place = jnp.stack([chip, mc]).astype(jnp.int32)
    _Order.last = None

    c_all = _all_gather8(c.reshape(-1, LANES), "gather_c").reshape(N_DEV, d)
    n_ada = w["w_ada"].shape[-1]
    b_ada_q = lax.dynamic_slice(w["b_ada"], (0, chip * n_ada), (1, n_ada))
    mod_q = _ada_fwd(c_all, w["w_ada"][0], b_ada_q)
    mod_all = _all_gather8(mod_q, "gather_mod")
    mod = lax.dynamic_index_in_dim(mod_all, dev, axis=1, keepdims=False)[::2].reshape(1, 6 * d)
    sh1, sc1, gt1, sh2, sc2, gt2 = [mod[:, i * d:(i + 1) * d] for i in range(6)]

    gather_in = _WeightGather("in", [w["w_in"][0]], chip)
    gather_mix = _WeightGather("mix", [w[n][0] for n in ("w_branch_gmlp", "w_branch_hg", "w_out")], chip)
    gather_fi = _WeightGather("fi", [w["w_ffn_in"][0]], chip)
    gather_fo = _WeightGather("fo", [w["w_ffn_out"][0]], chip)

    norm1_g, norm2_g, final_g = w["norm1_g"], w["norm2_g"], w["final_norm_g"].reshape(1, d)
    ln_g, ln_b = w["gmlp_ln_g"], w["gmlp_ln_b"]
    ws = w["gmlp_ws"][0]
    groups = ws.shape[0]
    bs_t = jnp.pad(w["gmlp_bs"][0].T, ((0, 0), (0, LANES - groups)))
    hg_lb, hg_ng, b_gate = w["hg_lb"], w["hg_norm_g"], w["b_gate"]

    h1 = _norm_mod_fwd(x2d, norm1_g, sc1, sh1, "norm1_fwd")
    gather_in.pass_on()
    w_in, = gather_in.done()
    z = _matmul(h1, w_in, mode="nn", name="mm_z", out_dtype=F32, b_slots=True, bn=1280)
    gather_mix.pass_on()
    ya = _gmlp_fwd(z, ln_g, ln_b, ws, bs_t)
    yb, o_raw, states = _hgrn_fwd(z, hg_lb, hg_ng, q_col)
    w_bg, w_bh, w_out = gather_mix.done()
    w_out = w_out.reshape(-1, w_out.shape[-1])
    pa = _matmul(ya, w_bg, mode="nn", name="mm_pa", out_dtype=BF16, b_slots=True)
    pb = _matmul(yb, w_bh, mode="nn", name="mm_pb", out_dtype=BF16, b_slots=True)
    y = _gate_fwd(pa, pb, z, b_gate, gate_col)
    gather_fi.pass_on()
    yo = _matmul(y, w_out, mode="nn", name="mm_yo", out_dtype=F32)
    x1, h2 = _norm_mod_fwd(x2d, norm2_g, sc2, sh2, "norm2_fwd", res=yo, gt=gt1)
    w_fi, = gather_fi.done()
    a_act, up_act, hf = _ffn_in(h2, w_fi)
    gather_fo.pass_on()
    w_fo, = gather_fo.done()
    w_fo = w_fo.reshape(-1, w_fo.shape[-1])
    ffn = _matmul(hf, w_fo, mode="nn", name="mm_ffn", out_dtype=F32)
    dx2, dffn, loss_row, d_final_g, d_gt2 = _final_loss(x1, ffn, gt2, final_g, tgt)

    g_fo = _matmul(hf, dffn, mode="tn", name="mm_g_fo", out_dtype=BF16, bm=1408)
    dhf = _matmul(dffn, w_fo, mode="nt", name="mm_dhf", out_dtype=BF16, bn=1408)
    daup = _swiglu_bwd(dhf, a_act, up_act)
    g_fi = _matmul(h2, daup, mode="tn", name="mm_g_fi", out_dtype=BF16, out_slots=True, bn=1408)
    red_ffn = _GradReduce("ffn", [g_fo.reshape(N_CHIPS, -1, g_fo.shape[-1]), g_fi])
    dh2 = _matmul(daup, w_fi, mode="nt", name="mm_dh2", out_dtype=F32, b_slots=True)
    red_ffn.step(place)
    dx1, dyo, d_sh2, d_sc2, d_norm2, d_gt1 = _norm_mod_bwd(dh2, x1, dx2, norm2_g, sc2, "norm2_bwd", branch=yo, gt=gt1)
    g_out = _matmul(y, dyo, mode="tn", name="mm_g_out", out_dtype=BF16)
    dy = _matmul(dyo, w_out, mode="nt", name="mm_dy", out_dtype=F32)
    dpa, dpb, dz_gate, d_b_gate = _gate_bwd(dy, pa, pb, z, b_gate, gate_col)
    g_bg = _matmul(ya, dpa, mode="tn", name="mm_g_bg", out_dtype=BF16, out_slots=True)
    g_bh = _matmul(yb, dpb, mode="tn", name="mm_g_bh", out_dtype=BF16, out_slots=True)
    red_mix = _GradReduce("mix", [g_out.reshape(N_CHIPS, -1, g_out.shape[-1]), g_bg, g_bh])
    dya = _matmul(dpa, w_bg, mode="nt", name="mm_dya", out_dtype=F32, b_slots=True)
    dyb = _matmul(dpb, w_bh, mode="nt", name="mm_dyb", out_dtype=F32, b_slots=True)
    red_mix.step(place)
    dz_uv, d_ws, d_bs_t, d_ln_g, d_ln_b = _gmlp_bwd(dya, z, ln_g, ln_b, ws, bs_t)
    dz_q, dz_f, dz_i, dz_g, d_hg_lb, d_hg_ng = _hgrn_bwd(dyb, z, o_raw, states, hg_lb, hg_ng, q_col)
    dz = jnp.concatenate([dz_uv, dz_q, dz_f, dz_i, dz_g, dz_gate], axis=1)
    g_in = _matmul(h1, dz, mode="tn", name="mm_g_in", out_dtype=BF16, out_slots=True, bn=1280)
    red_in = _GradReduce("in", [g_in])
    dh1 = _matmul(dz, w_in, mode="nt", name="mm_dh1", out_dtype=F32, b_slots=True)
    red_in.pair(place)
    grad_x, d_sh1, d_sc1, d_norm1 = _norm_mod_bwd(dh1, x2d, dx1, norm1_g, sc1, "norm1_bwd")

    d_mod = jnp.concatenate([d_sh1, d_sc1, d_gt1, d_sh2, d_sc2, d_gt2], axis=1)
    small_part = {"b_ada": d_mod, "norm1_g": d_norm1, "b_gate": d_b_gate, "gmlp_ln_g": d_ln_g, "gmlp_ln_b": d_ln_b,
                  "gmlp_ws": d_ws, "gmlp_bs": d_bs_t[:, :groups].T, "hg_lb": d_hg_lb, "hg_norm_g": d_hg_ng,
                  "norm2_g": d_norm2, "final_norm_g": d_final_g}
    small_all = _all_gather8(_pack([small_part[n] for n in SMALL]), "gather_small")
    red_in.cross(after_last=True)
    d_mod_all = small_all[:, :6 * d // LANES].reshape(N_DEV, 6 * d)
    d_mod_q = lax.dynamic_slice(d_mod_all, (0, chip * n_ada), (N_DEV, n_ada))

    grad, delta, new_m, new_v = {}, {}, {}, {}

    def update(n, parts, outer=False):
        outs = _adamw(w[n][0], m[n][0], v[n][0], parts, "adamw_" + n, outer=outer)
        grad[n], delta[n], new_m[n], new_v[n] = [o[None] for o in outs]

    update("w_ada", (c_all, d_mod_q), outer=True)
    outs = _adamw(_pack([w[n] for n in SMALL]), _pack([m[n] for n in SMALL]), _pack([v[n] for n in SMALL]),
                  small_all, "adamw_small")
    red_ffn.join(place)
    red_mix.join(place)
    g_fo, g_fi = red_ffn.done()
    update("w_ffn_out", g_fo[None])
    update("w_ffn_in", g_fi[None])
    red_in.join(place)
    g_out, g_bg, g_bh = red_mix.done()
    update("w_out", g_out[None])
    update("w_branch_gmlp", g_bg[None])
    update("w_branch_hg", g_bh[None])
    update("w_in", red_in.done()[0][None])
    row = 0
    for n in SMALL:
        cnt = w[n].size // LANES
        for dst, o in zip((grad, delta, new_m, new_v), outs):
            dst[n] = o[row:row + cnt].reshape(w[n].shape)
        row += cnt

    loss = lax.psum(loss_row[0, 0], ("x", "y", "c"))
    return (loss, grad_x[None], *[grad[n] for n in WEIGHTS], *[delta[n] for n in WEIGHTS],
            *[new_m[n] for n in WEIGHTS], *[new_v[n] for n in WEIGHTS])


def kernel(x, c, w_ada, b_ada, norm1_g, w_in, b_gate, gmlp_ln_g, gmlp_ln_b, gmlp_ws, gmlp_bs, hg_lb, hg_norm_g, w_branch_gmlp, w_branch_hg, w_out, norm2_g, w_ffn_in, w_ffn_out, final_norm_g, loss_target, m_w_ada, m_b_ada, m_norm1_g, m_w_in, m_b_gate, m_gmlp_ln_g, m_gmlp_ln_b, m_gmlp_ws, m_gmlp_bs, m_hg_lb, m_hg_norm_g, m_w_branch_gmlp, m_w_branch_hg, m_w_out, m_norm2_g, m_w_ffn_in, m_w_ffn_out, m_final_norm_g, v_w_ada, v_b_ada, v_norm1_g, v_w_in, v_b_gate, v_gmlp_ln_g, v_gmlp_ln_b, v_gmlp_ws, v_gmlp_bs, v_hg_lb, v_hg_norm_g, v_w_branch_gmlp, v_w_branch_hg, v_w_out, v_norm2_g, v_w_ffn_in, v_w_ffn_out, v_final_norm_g):
    w = dict(w_ada=w_ada, b_ada=b_ada, norm1_g=norm1_g, w_in=w_in, b_gate=b_gate, gmlp_ln_g=gmlp_ln_g,
             gmlp_ln_b=gmlp_ln_b, gmlp_ws=gmlp_ws, gmlp_bs=gmlp_bs, hg_lb=hg_lb, hg_norm_g=hg_norm_g,
             w_branch_gmlp=w_branch_gmlp, w_branch_hg=w_branch_hg, w_out=w_out, norm2_g=norm2_g,
             w_ffn_in=w_ffn_in, w_ffn_out=w_ffn_out, final_norm_g=final_norm_g)
    m = dict(w_ada=m_w_ada, b_ada=m_b_ada, norm1_g=m_norm1_g, w_in=m_w_in, b_gate=m_b_gate, gmlp_ln_g=m_gmlp_ln_g,
             gmlp_ln_b=m_gmlp_ln_b, gmlp_ws=m_gmlp_ws, gmlp_bs=m_gmlp_bs, hg_lb=m_hg_lb, hg_norm_g=m_hg_norm_g,
             w_branch_gmlp=m_w_branch_gmlp, w_branch_hg=m_w_branch_hg, w_out=m_w_out, norm2_g=m_norm2_g,
             w_ffn_in=m_w_ffn_in, w_ffn_out=m_w_ffn_out, final_norm_g=m_final_norm_g)
    v = dict(w_ada=v_w_ada, b_ada=v_b_ada, norm1_g=v_norm1_g, w_in=v_w_in, b_gate=v_b_gate, gmlp_ln_g=v_gmlp_ln_g,
             gmlp_ln_b=v_gmlp_ln_b, gmlp_ws=v_gmlp_ws, gmlp_bs=v_gmlp_bs, hg_lb=v_hg_lb, hg_norm_g=v_hg_norm_g,
             w_branch_gmlp=v_w_branch_gmlp, w_branch_hg=v_w_branch_hg, w_out=v_w_out, norm2_g=v_norm2_g,
             w_ffn_in=v_w_ffn_in, w_ffn_out=v_w_ffn_out, final_norm_g=v_final_norm_g)
    return _step(x, c, loss_target, w, m, v)
```

```python
import functools

import jax
import jax.numpy as jnp
from jax import lax
from jax.experimental import pallas as pl
from jax.experimental.pallas import tpu as pltpu

F32 = jnp.float32
BF16 = jnp.bfloat16
EPS = 1e-6
LANES = 128
N_CHIPS = 4
N_DEV = 8
VMEM_LIMIT_BYTES = 56 * 1024 * 1024
HG_CHUNK = 32
HG_ROWS = 256
EXP_CLAMP = 80.0
ADAM_LR, ADAM_B1, ADAM_B2, ADAM_EPS, ADAM_WD, ADAM_STEP = 0.001, 0.9, 0.999, 1e-08, 0.01, 10
MESH = pl.DeviceIdType.MESH

NN = (((1,), (0,)), ((), ()))
NT = (((1,), (1,)), ((), ()))
TN = (((0,), (0,)), ((), ()))


def _dot(a, b, dims=NN, precision=None):
    return lax.dot_general(a, b, dims, precision=precision, preferred_element_type=F32)


def _params(*semantics):
    return pltpu.CompilerParams(dimension_semantics=semantics, vmem_limit_bytes=VMEM_LIMIT_BYTES)


class _Order:
    last = None


def _pcall(body, *, in_specs, out_specs, grid=(), scratch_shapes=(), num_scalar_prefetch=0, **kw):
    def run(*ins):
        deps = () if _Order.last is None else (_Order.last,)
        n_in, n_dep = len(ins), len(deps)

        def wrapped(*refs):
            body(*refs[:n_in], *refs[n_in + n_dep:])

        specs = list(in_specs) + [pl.BlockSpec(memory_space=pl.ANY)] * n_dep
        if num_scalar_prefetch:
            grid_spec = pltpu.PrefetchScalarGridSpec(
                num_scalar_prefetch=num_scalar_prefetch, grid=grid, in_specs=specs, out_specs=out_specs,
                scratch_shapes=scratch_shapes)
            outs = pl.pallas_call(wrapped, grid_spec=grid_spec, **kw)(*ins, *deps)
        else:
            outs = pl.pallas_call(wrapped, grid=grid, in_specs=specs, out_specs=out_specs,
                                  scratch_shapes=scratch_shapes, **kw)(*ins, *deps)
        _Order.last = jax.tree.leaves(outs)[0]
        return outs

    return run


def _pick_rows(dim, pref, mult=16):
    best = None
    for cand in range(mult, min(dim, pref) + 1, mult):
        if dim % cand == 0:
            best = cand
    assert best is not None, (dim, pref)
    return best


def _pick(dim, pref):
    if dim <= pref:
        return dim
    best = None
    for cand in range(LANES, pref + 1, LANES):
        if dim % cand == 0:
            best = cand
    assert best is not None, (dim, pref)
    return best


def _sigmoid(x):
    return 1.0 / (1.0 + jnp.exp(-x))


def _gelu(x):
    c = 0.7978845608028654
    return 0.5 * x * (1.0 + jnp.tanh(c * (x + 0.044715 * x * x * x)))


def _gelu_grad(x):
    c = 0.7978845608028654
    t = jnp.tanh(c * (x + 0.044715 * x * x * x))
    return 0.5 * (1.0 + t) + 0.5 * x * (1.0 - t * t) * c * (1.0 + 3.0 * 0.044715 * x * x)


def _rms(x):
    r = lax.rsqrt(jnp.mean(x * x, axis=-1, keepdims=True) + EPS)
    return x * r, r


def _colsum(x):
    return jnp.sum(x, axis=0, keepdims=True)


def _accumulate(first, ref, val):
    @pl.when(first)
    def _():
        ref[...] = val

    @pl.when(jnp.logical_not(first))
    def _():
        ref[...] += val


def _matmul(a, b, *, mode, name, out_dtype, b_slots=False, out_slots=False, bm=1024, bn=1024, bk=2816):
    if mode == "nn":
        m, k = a.shape
        n = b.shape[2] * N_CHIPS if b_slots else b.shape[1]
        per = b.shape[2] if b_slots else n
    elif mode == "nt":
        m, k = a.shape
        n = b.shape[1] if b_slots else b.shape[0]
        per = b.shape[2] if b_slots else k
    else:
        k, m = a.shape
        n = b.shape[1]
        per = n // N_CHIPS if out_slots else n
    bm = _pick(m, bm)
    if mode == "nt":
        bn, bk = _pick(n, bn), _pick(per, bk)
    else:
        bn, bk = _pick(per, bn), _pick(k, bk)
    nk = k // bk
    per_blocks = per // (bk if mode == "nt" else bn)
    dims = {"nn": NN, "nt": NT, "tn": TN}[mode]

    def body(a_ref, b_ref, o_ref, *acc):
        part = _dot(a_ref[...], b_ref[...], dims)
        if nk == 1:
            o_ref[...] = part.astype(o_ref.dtype)
            return
        acc_ref, kk = acc[0], pl.program_id(2)

        @pl.when(kk == 0)
        def _():
            acc_ref[...] = part

        @pl.when(jnp.logical_and(kk > 0, kk < nk - 1))
        def _():
            acc_ref[...] += part

        @pl.when(kk == nk - 1)
        def _():
            o_ref[...] = (acc_ref[...] + part).astype(o_ref.dtype)

    if mode == "nn":
        a_spec = pl.BlockSpec((bm, bk), lambda i, j, kk: (i, kk))
        if b_slots:
            b_spec = pl.BlockSpec((None, bk, bn), lambda i, j, kk: (j // per_blocks, kk, j % per_blocks))
        else:
            b_spec = pl.BlockSpec((bk, bn), lambda i, j, kk: (kk, j))
    elif mode == "nt":
        a_spec = pl.BlockSpec((bm, bk), lambda i, j, kk: (i, kk))
        if b_slots:
            b_spec = pl.BlockSpec((None, bn, bk), lambda i, j, kk: (kk // per_blocks, j, kk % per_blocks))
        else:
            b_spec = pl.BlockSpec((bn, bk), lambda i, j, kk: (j, kk))
    else:
        a_spec = pl.BlockSpec((bk, bm), lambda i, j, kk: (kk, i))
        b_spec = pl.BlockSpec((bk, bn), lambda i, j, kk: (kk, j))
    if out_slots:
        o_spec = pl.BlockSpec((None, bm, bn), lambda i, j, kk: (j // per_blocks, i, j % per_blocks))
        out_shape = jax.ShapeDtypeStruct((N_CHIPS, m, per), out_dtype)
    else:
        o_spec = pl.BlockSpec((bm, bn), lambda i, j, kk: (i, j))
        out_shape = jax.ShapeDtypeStruct((m, n), out_dtype)
    return _pcall(
        body, name=name, grid=(m // bm, n // bn, nk), in_specs=[a_spec, b_spec], out_specs=o_spec,
        out_shape=out_shape, scratch_shapes=[pltpu.VMEM((bm, bn), F32)] if nk > 1 else [],
        compiler_params=_params("parallel", "parallel", "arbitrary"),
    )(a, b)


def _place():
    x, y, c = lax.axis_index("x"), lax.axis_index("y"), lax.axis_index("c")
    chips = [(1 - x, y), (x, 1 - y), (1 - x, 1 - y)]
    return x, y, c, chips


def _all_gather8(block, name):
    def body(x_ref, out_ref, send_sems, recv_sems, local_sem):
        x, y, c, chips = _place()
        me, sibling = (x, y, c), (x, y, 1 - c)

        def slot(px, py, pc):
            return out_ref.at[4 * px + 2 * py + pc]

        def copy(k, blk, to, src=None):
            return pltpu.make_async_remote_copy(
                src_ref=slot(*blk) if src is None else src, dst_ref=slot(*blk),
                send_sem=send_sems.at[k], recv_sem=recv_sems.at[k], device_id=to, device_id_type=MESH)

        mine = pltpu.make_async_copy(x_ref, slot(*me), local_sem)
        mine.start()
        first = [copy(0, me, sibling, src=x_ref)]
        first += [copy(1 + j, me, (*chip, c), src=x_ref) for j, chip in enumerate(chips)]
        for cp in first:
            cp.start()
        passed = [copy(4 + j, (*chip, c), sibling) for j, chip in enumerate(chips)]
        for j, chip in enumerate(chips):
            copy(1 + j, (*chip, c), me).wait_recv()
            passed[j].start()
        copy(0, sibling, me).wait_recv()
        for j, chip in enumerate(chips):
            copy(4 + j, (*chip, 1 - c), me).wait_recv()
        for cp in first + passed:
            cp.wait_send()
        mine.wait()

    return _pcall(
        body, name=name, out_shape=jax.ShapeDtypeStruct((N_DEV,) + block.shape, block.dtype),
        in_specs=[pl.BlockSpec(memory_space=pltpu.VMEM)], out_specs=pl.BlockSpec(memory_space=pltpu.VMEM),
        scratch_shapes=[pltpu.SemaphoreType.DMA((7,)), pltpu.SemaphoreType.DMA((7,)), pltpu.SemaphoreType.DMA],
        compiler_params=pltpu.CompilerParams(vmem_limit_bytes=VMEM_LIMIT_BYTES),
    )(block)


HBM_SPEC = pl.BlockSpec(memory_space=pltpu.HBM)
SEM_SPEC = pl.BlockSpec(memory_space=pltpu.SEMAPHORE)
ANY_SPEC = pl.BlockSpec(memory_space=pl.ANY)
EFFECT = pltpu.SideEffectType.DATAFLOW_SIDE_EFFECTING


def _xfer_start(name, bufs, plan, n_copies, after_last=False):
    nb = len(bufs)
    deps = (_Order.last,) if after_last and _Order.last is not None else ()
    nd = len(deps)

    def body(*refs):
        send_sems, recv_sems = refs[nb + nd], refs[nb + nd + 1]
        token = refs[nb + nd + 2 + nb]
        for k, (src, dst, dev) in enumerate(plan(refs[:nb], *_place())):
            pltpu.make_async_remote_copy(src_ref=src, dst_ref=dst, send_sem=send_sems.at[k], recv_sem=recv_sems.at[k],
                                         device_id=dev, device_id_type=MESH).start()
        token[...] = jnp.zeros_like(token)

    outs = pl.pallas_call(
        body, name=name,
        out_shape=(pltpu.SemaphoreType.DMA((n_copies,)), pltpu.SemaphoreType.DMA((n_copies,)),
                   *[pltpu.HBM(b.shape, b.dtype) for b in bufs], jax.ShapeDtypeStruct((8, LANES), F32)),
        in_specs=[HBM_SPEC] * nb + [ANY_SPEC] * nd,
        out_specs=(SEM_SPEC, SEM_SPEC, *[HBM_SPEC] * nb, pl.BlockSpec(memory_space=pltpu.VMEM)),
        input_output_aliases={i: 2 + i for i in range(nb)},
        compiler_params=pltpu.CompilerParams(has_side_effects=EFFECT),
    )(*[pltpu.with_memory_space_constraint(b, pltpu.HBM) for b in bufs], *deps)
    _Order.last = outs[-1]
    return (outs[0], outs[1]), list(outs[2:2 + nb])


def _xfer_wait(name, sems, bufs, plan):
    nb = len(bufs)

    def body(*refs):
        send_sems, recv_sems = refs[nb], refs[nb + 1]
        for k, (src, dst, dev) in enumerate(plan(refs[:nb], *_place())):
            copy = pltpu.make_async_remote_copy(src_ref=src, dst_ref=dst, send_sem=send_sems.at[k],
                                                recv_sem=recv_sems.at[k], device_id=dev, device_id_type=MESH)
            copy.wait_send()
            copy.wait_recv()

    outs = pl.pallas_call(
        body, name=name, out_shape=tuple(pltpu.HBM(b.shape, b.dtype) for b in bufs),
        in_specs=[HBM_SPEC] * nb + [SEM_SPEC, SEM_SPEC, ANY_SPEC], out_specs=tuple([HBM_SPEC] * nb),
        input_output_aliases={i: i for i in range(nb)},
        compiler_params=pltpu.CompilerParams(has_side_effects=EFFECT),
    )(*bufs, *sems, _Order.last)
    _Order.last = outs[0]
    return list(outs)


def _half(ref, c, axis):
    rows = ref.shape[axis] // 2
    return pl.ds(c * rows, rows)


def _plan_weights_ici(n):
    def plan(refs, x, y, c, chips):
        out = []
        for w in range(n):
            region = refs[w].at[2 * x + y, _half(refs[w], c, 1), :]
            out += [(region, region, (*chip, c)) for chip in chips]
        return out
    return plan


def _plan_weights_ring(n):
    def plan(refs, x, y, c, chips):
        out = []
        for w in range(n):
            region = refs[w].at[2 * x + y, _half(refs[w], c, 1), :]
            out += [(region, region, (*chip, c)) for chip in chips[:2]]
        return out
    return plan


def _plan_weights_relay(n):
    def plan(refs, x, y, c, chips):
        out = []
        for w in range(n):
            quarter_rows = refs[w].shape[1] // 4
            upper = refs[w].at[2 * x + (1 - y), pl.ds(2 * c * quarter_rows, quarter_rows), :]
            lower = refs[w].at[2 * (1 - x) + y, pl.ds((2 * c + 1) * quarter_rows, quarter_rows), :]
            out += [(upper, upper, (1 - x, y, c)), (lower, lower, (x, 1 - y, c))]
        return out
    return plan


def _plan_weights_d2d(n):
    def plan(refs, x, y, c, chips):
        out = []
        for w in range(n):
            rows = _half(refs[w], c, 1)
            for chip in chips:
                region = refs[w].at[2 * chip[0] + chip[1], rows, :]
                out.append((region, region, (x, y, 1 - c)))
        return out
    return plan


def _plan_grads_d2d(n):
    def plan(refs, x, y, c, chips):
        return [(refs[w].at[:, _half(refs[w], 1 - c, 1), :], refs[n + w], (x, y, 1 - c)) for w in range(n)]
    return plan


def _plan_grads_ici(n):
    def plan(refs, x, y, c, chips):
        out = []
        for w in range(n):
            out += [(refs[w].at[2 * chip[0] + chip[1]], refs[n + w].at[2 * x + y], (*chip, c)) for chip in chips]
        return out
    return plan


def _plan_final_d2d(n):
    def plan(refs, x, y, c, chips):
        out = []
        for w in range(n):
            region = refs[w].at[_half(refs[w], c, 0), :]
            out.append((region, region, (x, y, 1 - c)))
        return out
    return plan


def _stream_blocks(hr, cols):
    bc = cols if cols <= 4096 else _pick(cols, 4096)
    return _pick_rows(hr, max(16, (768 * 1024) // bc)), bc


def _pre_reduce(g, landed, place, name):
    _, rows, cols = g.shape
    hr = rows // 2
    rb, bc = _stream_blocks(hr, cols)
    nrb = hr // rb

    def body(place_ref, g_ref, l_ref, o_ref):
        o_ref[...] = (g_ref[...].astype(F32) + l_ref[...].astype(F32)).astype(o_ref.dtype)

    return _pcall(
        body, name=name, num_scalar_prefetch=1, grid=(N_CHIPS, nrb, cols // bc),
        in_specs=[pl.BlockSpec((None, rb, bc), lambda j, i, k, p: (j, p[1] * nrb + i, k)),
                  pl.BlockSpec((None, rb, bc), lambda j, i, k, p: (j, i, k))],
        out_specs=pl.BlockSpec((None, rb, bc), lambda j, i, k, p: (j, i, k)),
        out_shape=jax.ShapeDtypeStruct((N_CHIPS, hr, cols), g.dtype),
        compiler_params=_params("parallel", "parallel", "parallel"),
    )(place, g, landed)


def _sum_slots(mine, landed, place, name):
    _, hr, cols = mine.shape
    rb, bc = _stream_blocks(hr, cols)
    rb = _pick_rows(hr, max(16, rb // 2))
    nrb = hr // rb

    def body(place_ref, m_ref, l_ref, o_ref):
        chip = place_ref[0]
        own = m_ref[...].astype(F32)
        total = jnp.where(chip == 0, own, l_ref[0].astype(F32))
        for j in range(1, N_CHIPS):
            total = total + jnp.where(chip == j, own, l_ref[j].astype(F32))
        o_ref[...] = total

    return _pcall(
        body, name=name, num_scalar_prefetch=1, grid=(nrb, cols // bc),
        in_specs=[pl.BlockSpec((None, rb, bc), lambda i, k, p: (p[0], i, k)),
                  pl.BlockSpec((N_CHIPS, rb, bc), lambda i, k, p: (0, i, k))],
        out_specs=pl.BlockSpec((rb, bc), lambda i, k, p: (p[1] * nrb + i, k)),
        out_shape=jax.ShapeDtypeStruct((2 * hr, cols), F32),
        compiler_params=_params("parallel", "parallel"),
    )(place, mine, landed)


class _WeightGather:
    def __init__(self, tag, quarters, chip, ring=False):
        self.tag, self.n, self.ring = tag, len(quarters), ring
        zones = [lax.dynamic_update_slice(lax.empty((N_CHIPS,) + q.shape, BF16), q.astype(BF16)[None], (chip, 0, 0))
                 for q in quarters]
        self.plan = _plan_weights_ring(self.n) if ring else _plan_weights_ici(self.n)
        self.sems, self.bufs = _xfer_start("wici_start_" + tag, zones, self.plan, (2 if ring else 3) * self.n,
                                           after_last=True)

    def relay(self):
        bufs = _xfer_wait("wici_wait_" + self.tag, self.sems, self.bufs, self.plan)
        self.plan = _plan_weights_relay(self.n)
        self.sems, self.bufs = _xfer_start("wrel_start_" + self.tag, bufs, self.plan, 2 * self.n)

    def pass_on(self):
        bufs = _xfer_wait(("wrel_wait_" if self.ring else "wici_wait_") + self.tag, self.sems, self.bufs, self.plan)
        self.plan = _plan_weights_d2d(self.n)
        self.sems, self.bufs = _xfer_start("wd2d_start_" + self.tag, bufs, self.plan, 3 * self.n)

    def done(self):
        return _xfer_wait("wd2d_wait_" + self.tag, self.sems, self.bufs, self.plan)


class _GradReduce:
    def __init__(self, tag, grads):
        self.tag, self.n = tag, len(grads)
        zones = [lax.empty((N_CHIPS, g.shape[1] // 2, g.shape[2]), g.dtype) for g in grads]
        self.plan = _plan_grads_d2d(self.n)
        self.sems, self.bufs = _xfer_start("gd2d_start_" + tag, list(grads) + zones, self.plan, self.n)

    def pair(self, place):
        n = self.n
        bufs = _xfer_wait("gd2d_wait_" + self.tag, self.sems, self.bufs, self.plan)
        self.halves = [_pre_reduce(bufs[w], bufs[n + w], place, f"pre_reduce_{self.tag}{w}") for w in range(n)]

    def cross(self, after_last=False):
        zones = [lax.empty(h.shape, h.dtype) for h in self.halves]
        self.plan = _plan_grads_ici(self.n)
        self.sems, self.bufs = _xfer_start("gici_start_" + self.tag, self.halves + zones, self.plan, 3 * self.n,
                                           after_last=after_last)

    def step(self, place):
        self.pair(place)
        self.cross()

    def join(self, place):
        n = self.n
        bufs = _xfer_wait("gici_wait_" + self.tag, self.sems, self.bufs, self.plan)
        sums = [_sum_slots(bufs[w], bufs[n + w], place, f"sum_slots_{self.tag}{w}") for w in range(n)]
        self.plan = _plan_final_d2d(n)
        self.sems, self.bufs = _xfer_start("gfin_start_" + self.tag, sums, self.plan, n)

    def done(self):
        return _xfer_wait("gfin_wait_" + self.tag, self.sems, self.bufs, self.plan)


def _ada_fwd(c_all, w_q, b_q):
    d, n = w_q.shape
    bn = _pick(n, 512)

    def body(c_ref, w_ref, b_ref, o_ref):
        cv = c_ref[...]
        act = cv * _sigmoid(cv)
        o_ref[...] = _dot(act, w_ref[...], NN, lax.Precision.HIGHEST) + b_ref[...]

    return _pcall(
        body, name="ada_fwd", grid=(n // bn,),
        in_specs=[pl.BlockSpec((N_DEV, d), lambda j: (0, 0)), pl.BlockSpec((d, bn), lambda j: (0, j)),
                  pl.BlockSpec((1, bn), lambda j: (0, j))],
        out_specs=pl.BlockSpec((N_DEV, bn), lambda j: (0, j)),
        out_shape=jax.ShapeDtypeStruct((N_DEV, n), F32), compiler_params=_params("parallel"),
    )(c_all, w_q, b_q)


def _row_spec(rb, width, col=0):
    return pl.BlockSpec((rb, width), lambda i, col=col: (i, col))


def _vec_spec(width, col=0):
    return pl.BlockSpec((1, width), lambda i, col=col: (0, col))


def _norm_mod_fwd(x, g, sc, sh, name, res=None, gt=None):
    s, d = x.shape
    rb = _pick(s, 256)
    has_res = res is not None

    def body(*refs):
        if has_res:
            x_ref, res_ref, gt_ref, g_ref, sc_ref, sh_ref, x1_ref, h_ref = refs
            xv = x_ref[...] + gt_ref[...] * res_ref[...]
            x1_ref[...] = xv
        else:
            x_ref, g_ref, sc_ref, sh_ref, h_ref = refs
            xv = x_ref[...]
        xh, _ = _rms(xv)
        h_ref[...] = (xh * g_ref[...] * (1.0 + sc_ref[...]) + sh_ref[...]).astype(BF16)

    row, vec = _row_spec(rb, d), _vec_spec(d)
    if has_res:
        ins, in_specs = (x, res, gt, g, sc, sh), [row, row, vec, vec, vec, vec]
        out_shape = [jax.ShapeDtypeStruct((s, d), F32), jax.ShapeDtypeStruct((s, d), BF16)]
        out_specs = [row, row]
    else:
        ins, in_specs = (x, g, sc, sh), [row, vec, vec, vec]
        out_shape, out_specs = jax.ShapeDtypeStruct((s, d), BF16), row
    return _pcall(body, name=name, grid=(s // rb,), in_specs=in_specs, out_specs=out_specs,
                          out_shape=out_shape, compiler_params=_params("parallel"))(*ins)


def _final_loss(x1, f, gt2, g_final, target):
    s, d = x1.shape
    rb = _pick(s, 256)

    def body(x1_ref, f_ref, gt_ref, g_ref, t_ref, dx_ref, df_ref, loss_ref, dg_ref, dgt_ref):
        first = pl.program_id(0) == 0
        fv, gt, gv = f_ref[...], gt_ref[...], g_ref[...]
        x2 = x1_ref[...] + gt * fv
        xh, r = _rms(x2)
        err = xh * gv - t_ref[...]
        blk = 0.5 * jnp.sum(jnp.sum(err * err, axis=1, keepdims=True), axis=0, keepdims=True) / d
        dy = err / d
        dxh = dy * gv
        dx = r * (dxh - xh * jnp.mean(dxh * xh, axis=-1, keepdims=True))
        dx_ref[...] = dx
        df_ref[...] = (dx * gt).astype(BF16)
        _accumulate(first, loss_ref, jnp.broadcast_to(blk, (1, LANES)))
        _accumulate(first, dg_ref, _colsum(dy * xh))
        _accumulate(first, dgt_ref, _colsum(dx * fv))

    row, vec = _row_spec(rb, d), _vec_spec(d)
    return _pcall(
        body, name="final_loss", grid=(s // rb,), in_specs=[row, row, vec, vec, row],
        out_specs=[row, row, _vec_spec(LANES), vec, vec],
        out_shape=[jax.ShapeDtypeStruct((s, d), F32), jax.ShapeDtypeStruct((s, d), BF16),
                   jax.ShapeDtypeStruct((1, LANES), F32), jax.ShapeDtypeStruct((1, d), F32),
                   jax.ShapeDtypeStruct((1, d), F32)],
        compiler_params=_params("arbitrary"),
    )(x1, f, gt2, g_final, target)


def _norm_mod_bwd(dh, xin, dres, g, sc, name, branch=None, gt=None):
    s, d = xin.shape
    rb = _pick(s, 256)
    has_branch = branch is not None

    def body(*refs):
        if has_branch:
            dh_ref, x_ref, dres_ref, g_ref, sc_ref, br_ref, gt_ref, dx_ref, dbr_ref, dsh_ref, dsc_ref, dg_ref, dgt_ref = refs
        else:
            dh_ref, x_ref, dres_ref, g_ref, sc_ref, dx_ref, dsh_ref, dsc_ref, dg_ref = refs
        first = pl.program_id(0) == 0
        gv = g_ref[...]
        xh, r = _rms(x_ref[...])
        dhv = dh_ref[...]
        dn = dhv * (1.0 + sc_ref[...])
        dxh = dn * gv
        dx = dres_ref[...] + r * (dxh - xh * jnp.mean(dxh * xh, axis=-1, keepdims=True))
        dx_ref[...] = dx
        _accumulate(first, dsh_ref, _colsum(dhv))
        _accumulate(first, dsc_ref, _colsum(dhv * xh * gv))
        _accumulate(first, dg_ref, _colsum(dn * xh))
        if has_branch:
            dbr_ref[...] = (dx * gt_ref[...]).astype(BF16)
            _accumulate(first, dgt_ref, _colsum(dx * br_ref[...]))

    row, vec = _row_spec(rb, d), _vec_spec(d)
    vec_shape = jax.ShapeDtypeStruct((1, d), F32)
    if has_branch:
        ins, in_specs = (dh, xin, dres, g, sc, branch, gt), [row, row, row, vec, vec, row, vec]
        out_specs = [row, row, vec, vec, vec, vec]
        out_shape = [jax.ShapeDtypeStruct((s, d), F32), jax.ShapeDtypeStruct((s, d), BF16)] + [vec_shape] * 4
    else:
        ins, in_specs = (dh, xin, dres, g, sc), [row, row, row, vec, vec]
        out_specs = [row, vec, vec, vec]
        out_shape = [jax.ShapeDtypeStruct((s, d), F32)] + [vec_shape] * 3
    return _pcall(body, name=name, grid=(s // rb,), in_specs=in_specs, out_specs=out_specs,
                          out_shape=out_shape, compiler_params=_params("arbitrary"))(*ins)


def _gate_fwd(pa, pb, z, b_gate, gate_col):
    s, d = pa.shape
    rb = _pick(s, 256)

    def body(pa_ref, pb_ref, za_ref, zb_ref, ba_ref, bb_ref, y_ref):
        ga = _sigmoid(za_ref[...] + ba_ref[...])
        gb = _sigmoid(zb_ref[...] + bb_ref[...])
        y_ref[...] = (ga * pa_ref[...].astype(F32) + gb * pb_ref[...].astype(F32)).astype(BF16)

    row = _row_spec(rb, d)
    return _pcall(
        body, name="gate_fwd", grid=(s // rb,),
        in_specs=[row, row, _row_spec(rb, d, gate_col), _row_spec(rb, d, gate_col + 1), _vec_spec(d, 0), _vec_spec(d, 1)],
        out_specs=row, out_shape=jax.ShapeDtypeStruct((s, d), BF16), compiler_params=_params("parallel"),
    )(pa, pb, z, z, b_gate, b_gate)


def _gate_bwd(dy, pa, pb, z, b_gate, gate_col):
    s, d = pa.shape
    rb = _pick(s, 256)

    def body(dy_ref, pa_ref, pb_ref, za_ref, zb_ref, ba_ref, bb_ref, dpa_ref, dpb_ref, dz_ref, db_ref):
        first = pl.program_id(0) == 0
        dyv = dy_ref[...]
        ga = _sigmoid(za_ref[...] + ba_ref[...])
        gb = _sigmoid(zb_ref[...] + bb_ref[...])
        dpa_ref[...] = (dyv * ga).astype(BF16)
        dpb_ref[...] = (dyv * gb).astype(BF16)
        dga = dyv * pa_ref[...].astype(F32) * ga * (1.0 - ga)
        dgb = dyv * pb_ref[...].astype(F32) * gb * (1.0 - gb)
        dz_ref[:, :d] = dga.astype(BF16)
        dz_ref[:, d:] = dgb.astype(BF16)

        @pl.when(first)
        def _():
            db_ref[:, :d] = _colsum(dga)
            db_ref[:, d:] = _colsum(dgb)

        @pl.when(jnp.logical_not(first))
        def _():
            db_ref[:, :d] += _colsum(dga)
            db_ref[:, d:] += _colsum(dgb)

    row = _row_spec(rb, d)
    return _pcall(
        body, name="gate_bwd", grid=(s // rb,),
        in_specs=[row, row, row, _row_spec(rb, d, gate_col), _row_spec(rb, d, gate_col + 1), _vec_spec(d, 0), _vec_spec(d, 1)],
        out_specs=[row, row, _row_spec(rb, 2 * d), _vec_spec(2 * d)],
        out_shape=[jax.ShapeDtypeStruct((s, d), BF16), jax.ShapeDtypeStruct((s, d), BF16),
                   jax.ShapeDtypeStruct((s, 2 * d), BF16), jax.ShapeDtypeStruct((1, 2 * d), F32)],
        compiler_params=_params("arbitrary"),
    )(dy, pa, pb, z, z, b_gate, b_gate)


def _ffn_in(h, w_fi, bm=512, bn=1408):
    s, d = h.shape
    per = w_fi.shape[2]
    ff = 2 * per
    bm, bn = _pick(s, bm), _pick(per, bn)
    per_blocks = per // bn

    def body(h_ref, wa_ref, wu_ref, a_ref, u_ref, hf_ref):
        hv = h_ref[...]
        a = _dot(hv, wa_ref[...])
        up = _dot(hv, wu_ref[...])
        a_ref[...] = a.astype(BF16)
        u_ref[...] = up.astype(BF16)
        hf_ref[...] = (a * _sigmoid(a) * up).astype(BF16)

    out = pl.BlockSpec((bm, bn), lambda j, i: (i, j))
    shape = jax.ShapeDtypeStruct((s, ff), BF16)
    return _pcall(
        body, name="ffn_in", grid=(ff // bn, s // bm),
        in_specs=[pl.BlockSpec((bm, d), lambda j, i: (i, 0)),
                  pl.BlockSpec((None, d, bn), lambda j, i: (j // per_blocks, 0, j % per_blocks)),
                  pl.BlockSpec((None, d, bn), lambda j, i: (2 + j // per_blocks, 0, j % per_blocks))],
        out_specs=[out, out, out], out_shape=[shape, shape, shape],
        compiler_params=_params("parallel", "parallel"),
    )(h, w_fi, w_fi)


def _swiglu_bwd(dhf, a_act, up_act):
    s, ff = a_act.shape
    rb = _pick(s, 128)

    def body(d_ref, a_ref, u_ref, o_ref):
        a, dv = a_ref[...].astype(F32), d_ref[...].astype(F32)
        sa = _sigmoid(a)
        o_ref[:, :ff] = (dv * u_ref[...].astype(F32) * sa * (1.0 + a * (1.0 - sa))).astype(BF16)
        o_ref[:, ff:] = (dv * a * sa).astype(BF16)

    return _pcall(
        body, name="swiglu_bwd", grid=(s // rb,),
        in_specs=[_row_spec(rb, ff), _row_spec(rb, ff), _row_spec(rb, ff)],
        out_specs=_row_spec(rb, 2 * ff), out_shape=jax.ShapeDtypeStruct((s, 2 * ff), BF16),
        compiler_params=_params("parallel"),
    )(dhf, a_act, up_act)


def _tril(n):
    return lax.broadcasted_iota(jnp.int32, (n, n), 0) >= lax.broadcasted_iota(jnp.int32, (n, n), 1)


def _gmlp_norm(v, ln_g, ln_b):
    gv = _gelu(v)
    mu = jnp.mean(gv, axis=-1, keepdims=True)
    cen = gv - mu
    rs = lax.rsqrt(jnp.mean(cen * cen, axis=-1, keepdims=True) + EPS)
    xh = cen * rs
    return xh, rs, xh * ln_g + ln_b


def _gmlp_fwd(z, ln_g, ln_b, ws, bs_t):
    s = z.shape[0]
    gw = ln_g.shape[1]
    groups, chunk, _ = ws.shape

    def body(u_ref, v_ref, lg_ref, lb_ref, ws_ref, bs_ref, ya_ref):
        gu = _gelu(u_ref[...])
        _, _, vn = _gmlp_norm(v_ref[...], lg_ref[...], lb_ref[...])
        mask = _tril(chunk)
        for g in range(groups):
            cols = slice(g * LANES, (g + 1) * LANES)
            wm = jnp.where(mask, ws_ref[g], 0.0).astype(BF16)
            sg = _dot(wm, vn[:, cols].astype(BF16)) + bs_ref[:, g:g + 1]
            ya_ref[:, cols] = (gu[:, cols] * sg).astype(BF16)

    return _pcall(
        body, name="gmlp_fwd", grid=(s // chunk,),
        in_specs=[_row_spec(chunk, gw, 0), _row_spec(chunk, gw, 1), _vec_spec(gw), _vec_spec(gw),
                  pl.BlockSpec((groups, chunk, chunk), lambda i: (0, 0, 0)), pl.BlockSpec((chunk, LANES), lambda i: (0, 0))],
        out_specs=_row_spec(chunk, gw), out_shape=jax.ShapeDtypeStruct((s, gw), BF16),
        compiler_params=_params("parallel"),
    )(z, z, ln_g, ln_b, ws, bs_t)


def _gmlp_bwd(dya, z, ln_g, ln_b, ws, bs_t):
    s = z.shape[0]
    gw = ln_g.shape[1]
    groups, chunk, _ = ws.shape

    def body(dya_ref, u_ref, v_ref, lg_ref, lb_ref, ws_ref, bs_ref, duv_ref, dws_ref, dbs_ref, dlg_ref, dlb_ref, dvn_ref):
        first = pl.program_id(0) == 0
        u, v, lg = u_ref[...], v_ref[...], lg_ref[...]
        gu = _gelu(u)
        xh, rs, vn = _gmlp_norm(v, lg, lb_ref[...])
        dyav = dya_ref[...]
        mask = _tril(chunk)
        lane = lax.broadcasted_iota(jnp.int32, (chunk, LANES), 1)
        dbs = jnp.zeros((chunk, LANES), F32)
        for g in range(groups):
            cols = slice(g * LANES, (g + 1) * LANES)
            wm = jnp.where(mask, ws_ref[g], 0.0).astype(BF16)
            vg = vn[:, cols].astype(BF16)
            sg = _dot(wm, vg) + bs_ref[:, g:g + 1]
            ds = dyav[:, cols] * gu[:, cols]
            duv_ref[:, cols] = (dyav[:, cols] * sg * _gelu_grad(u[:, cols])).astype(BF16)
            dsb = ds.astype(BF16)
            _accumulate(first, dws_ref.at[g], jnp.where(mask, _dot(dsb, vg, NT), 0.0))
            dbs = dbs + jnp.where(lane == g, jnp.sum(ds, axis=-1, keepdims=True), 0.0)
            dvn_ref[:, cols] = _dot(wm, dsb, TN)
        dvn = dvn_ref[...]
        _accumulate(first, dbs_ref, dbs)
        _accumulate(first, dlb_ref, _colsum(dvn))
        _accumulate(first, dlg_ref, _colsum(dvn * xh))
        dxh = dvn * lg
        dgv = rs * (dxh - jnp.mean(dxh, axis=-1, keepdims=True) - xh * jnp.mean(dxh * xh, axis=-1, keepdims=True))
        duv_ref[:, gw:] = (dgv * _gelu_grad(v)).astype(BF16)

    return _pcall(
        body, name="gmlp_bwd", grid=(s // chunk,),
        in_specs=[_row_spec(chunk, gw), _row_spec(chunk, gw, 0), _row_spec(chunk, gw, 1), _vec_spec(gw), _vec_spec(gw),
                  pl.BlockSpec((groups, chunk, chunk), lambda i: (0, 0, 0)), pl.BlockSpec((chunk, LANES), lambda i: (0, 0))],
        out_specs=[_row_spec(chunk, 2 * gw), pl.BlockSpec((groups, chunk, chunk), lambda i: (0, 0, 0)),
                   pl.BlockSpec((chunk, LANES), lambda i: (0, 0)), _vec_spec(gw), _vec_spec(gw)],
        out_shape=[jax.ShapeDtypeStruct((s, 2 * gw), BF16), jax.ShapeDtypeStruct((groups, chunk, chunk), F32),
                   jax.ShapeDtypeStruct((chunk, LANES), F32), jax.ShapeDtypeStruct((1, gw), F32),
                   jax.ShapeDtypeStruct((1, gw), F32)],
        scratch_shapes=[pltpu.VMEM((chunk, gw), F32)],
        compiler_params=_params("arbitrary"),
    )(dya, z, z, ln_g, ln_b, ws, bs_t)


def _lower_bound(lb_ref):
    a0, a1 = lb_ref[0:1, :], lb_ref[1:2, :]
    mx = jnp.maximum(a0, a1)
    e0, e1 = jnp.exp(a0 - mx), jnp.exp(a1 - mx)
    return e0 / (e0 + e1)


def _sum_dot(mask, x):
    hi = x.astype(BF16)
    rest = x - hi.astype(F32)
    mid = rest.astype(BF16)
    low = (rest - mid.astype(F32)).astype(BF16)
    return _dot(mask, hi) + _dot(mask, mid) + _dot(mask, low)


def _ones_where(mask):
    return jnp.where(mask, 1.0, 0.0).astype(BF16)


def _hg_masks(rows, t):
    r = lax.broadcasted_iota(jnp.int32, (rows, rows), 0)
    c = lax.broadcasted_iota(jnp.int32, (rows, rows), 1)
    same = (r // t) == (c // t)
    incl = jnp.logical_and(same, c <= r)
    upto_mid = jnp.logical_and(same, (c % t) <= t // 2)
    rev = jnp.logical_and(same, c >= r)
    return same, incl, upto_mid, rev


def _hg_block(q, fp, lb, masks):
    rows = q.shape[0]
    same, incl, upto_mid, _ = masks
    sig = _sigmoid(fp)
    f = lb + (1.0 - lb) * sig
    k = 1.0 - f
    sq = _sigmoid(q)
    qa = q * sq
    stacked = jnp.concatenate([_ones_where(m) for m in (incl, same, upto_mid)], axis=0)
    sums = _sum_dot(stacked, jnp.log(f))
    b, b_last, b_mid = sums[:rows], sums[rows:2 * rows], sums[2 * rows:]
    e_q = jnp.exp(jnp.minimum(b - b_mid, EXP_CLAMP))
    e_k = jnp.exp(jnp.minimum(b_mid - b, EXP_CLAMP))
    e_in = jnp.exp(b)
    e_out = jnp.exp(b_last - b)
    return dict(sig=sig, f=f, k=k, sq=sq, qa=qa, e_last=jnp.exp(b_last), e_q=e_q, e_k=e_k, e_in=e_in, e_out=e_out,
                q_hat=(qa * e_q).astype(BF16), k_hat=(k * e_k).astype(BF16),
                q_in=(qa * e_in).astype(BF16), k_out=k * e_out)


def _hgrn_fwd(z, hg_lb, norm_g, q_col):
    s = z.shape[0]
    hw = norm_g.shape[1]
    heads = hw // LANES
    t = HG_CHUNK
    rows = min(HG_ROWS, s)
    per_step = rows // t
    per = hw // LANES

    def zspec(which):
        return pl.BlockSpec((rows, LANES), lambda h, r, which=which: (r, q_col + which * per + h))

    def body(q_ref, f_ref, i_ref, g_ref, lb_ref, ng_ref, yb_ref, o_ref, st_out_ref, st_ref, e_last_ref, inter_ref):
        @pl.when(pl.program_id(1) == 0)
        def _():
            st_ref[...] = jnp.zeros_like(st_ref)

        masks = _hg_masks(rows, t)
        blk = _hg_block(q_ref[...], f_ref[...], _lower_bound(lb_ref), masks)
        iv = i_ref[...].astype(BF16)
        q_in, k_out = blk["q_in"], blk["k_out"].astype(BF16)
        e_last_ref[...] = blk["e_last"]
        attn = jnp.where(masks[1], _dot(blk["q_hat"], blk["k_hat"], NT), 0.0).astype(BF16)
        o = _dot(attn, iv)
        grown = [_dot(iv[j * t:(j + 1) * t], k_out[j * t:(j + 1) * t], TN) for j in range(per_step)]
        st = st_ref[...]
        for j in range(per_step):
            st_out_ref[j] = st
            inter_ref[j * t:(j + 1) * t, :] = _dot(q_in[j * t:(j + 1) * t], st.astype(BF16), NT)
            st = st * e_last_ref[j * t:j * t + 1, :] + grown[j]
        st_ref[...] = st
        o = o + inter_ref[...]
        o_ref[...] = o
        og = g_ref[...]
        on, _ = _rms(o)
        yb_ref[...] = (on * ng_ref[...] * (og * _sigmoid(og))).astype(BF16)

    out_row = pl.BlockSpec((rows, LANES), lambda h, r: (r, h))
    return _pcall(
        body, name="hgrn_fwd", grid=(heads, s // rows),
        in_specs=[zspec(0), zspec(1), zspec(2), zspec(3),
                  pl.BlockSpec((2, LANES), lambda h, r: (0, h)), pl.BlockSpec((1, LANES), lambda h, r: (0, h))],
        out_specs=[out_row, out_row, pl.BlockSpec((None, per_step, LANES, LANES), lambda h, r: (h, r, 0, 0))],
        out_shape=[jax.ShapeDtypeStruct((s, hw), BF16), jax.ShapeDtypeStruct((s, hw), F32),
                   jax.ShapeDtypeStruct((heads, s // t, LANES, LANES), F32)],
        scratch_shapes=[pltpu.VMEM((LANES, LANES), F32), pltpu.VMEM((rows, LANES), F32), pltpu.VMEM((rows, LANES), F32)],
        compiler_params=_params("parallel", "arbitrary"),
    )(z, z, z, z, hg_lb, norm_g)


def _hgrn_bwd(dyb, z, o_raw, states, hg_lb, norm_g, q_col):
    s = z.shape[0]
    hw = norm_g.shape[1]
    heads = hw // LANES
    t = HG_CHUNK
    rows = min(HG_ROWS, s)
    per_step = rows // t
    per = hw // LANES
    n_steps = s // rows

    def zspec(which):
        return pl.BlockSpec((rows, LANES), lambda h, r, which=which: (n_steps - 1 - r, q_col + which * per + h))

    def body(dyb_ref, q_ref, f_ref, i_ref, g_ref, o_ref, st_in_ref, lb_ref, ng_ref,
             dq_ref, df_ref, di_ref, dg_ref, dlb_ref, dng_ref, dst_ref, acc_lb_ref, acc_ng_ref,
             e_last_ref, dq_in_ref, dk_out_ref, di_inter_ref, carry_ref):
        step = pl.program_id(1)

        @pl.when(step == 0)
        def _():
            dst_ref[...] = jnp.zeros_like(dst_ref)
            acc_lb_ref[...] = jnp.zeros_like(acc_lb_ref)
            acc_ng_ref[...] = jnp.zeros_like(acc_ng_ref)

        lb = _lower_bound(lb_ref)
        ng = ng_ref[...]
        masks = _hg_masks(rows, t)
        same, incl, _, rev = masks
        q = q_ref[...]
        blk = _hg_block(q, f_ref[...], lb, masks)
        iv = i_ref[...].astype(BF16)
        o, og, dy = o_ref[...], g_ref[...], dyb_ref[...]
        so = _sigmoid(og)
        on, r = _rms(o)
        acc_ng_ref[...] += _colsum(dy * on * (og * so))
        dg_ref[...] = (dy * on * ng * so * (1.0 + og * (1.0 - so))).astype(BF16)
        don = dy * ng * (og * so)
        do = (r * (don - on * jnp.mean(don * on, axis=-1, keepdims=True))).astype(BF16)
        q_hat, k_hat, q_in, k_out = blk["q_hat"], blk["k_hat"], blk["q_in"], blk["k_out"]
        k_out_b = k_out.astype(BF16)
        attn = jnp.where(incl, _dot(q_hat, k_hat, NT), 0.0).astype(BF16)
        d_attn = jnp.where(incl, _dot(do, iv, NT), 0.0).astype(BF16)
        di_intra = _dot(attn, do, TN)
        dq_hat = _dot(d_attn, k_hat)
        dk_hat = _dot(d_attn, q_hat, TN)
        e_last_ref[...] = blk["e_last"]
        grown = [_dot(do[j * t:(j + 1) * t], q_in[j * t:(j + 1) * t], TN) for j in range(per_step)]
        dst = dst_ref[...]
        for j in reversed(range(per_step)):
            rs_ = slice(j * t, (j + 1) * t)
            e_last = e_last_ref[j * t:j * t + 1, :]
            st_prev, dst_b = st_in_ref[j], dst.astype(BF16)
            dq_in_ref[rs_, :] = _dot(do[rs_], st_prev.astype(BF16))
            dk_out_ref[rs_, :] = _dot(iv[rs_], dst_b)
            di_inter_ref[rs_, :] = _dot(k_out_b[rs_], dst_b, NT)
            carry_ref[rs_, :] = jnp.broadcast_to(e_last * _colsum(st_prev * dst), (t, LANES))
            dst = dst * e_last + grown[j]
        dst_ref[...] = dst
        di_ref[...] = (di_intra + di_inter_ref[...]).astype(BF16)
        dk_out = dk_out_ref[...]
        dqa = dq_in_ref[...] * blk["e_in"] + dq_hat * blk["e_q"]
        dk = dk_out * blk["e_out"] + dk_hat * blk["e_k"]
        db = blk["qa"] * dqa - blk["k"] * dk
        sum_mask = jnp.concatenate([_ones_where(rev), _ones_where(same)], axis=1)
        dlf = _sum_dot(sum_mask, jnp.concatenate([db, dk_out * k_out], axis=0)) + carry_ref[...]
        dfv = dlf / blk["f"] - dk
        sig, sq = blk["sig"], blk["sq"]
        df_ref[...] = (dfv * (1.0 - lb) * sig * (1.0 - sig)).astype(BF16)
        acc_lb_ref[...] += _colsum(dfv * (1.0 - sig))
        dq_ref[...] = (dqa * sq * (1.0 + q * (1.0 - sq))).astype(BF16)

        @pl.when(step == n_steps - 1)
        def _():
            d0 = acc_lb_ref[...] * lb * (1.0 - lb)
            dlb_ref[0:1, :] = d0
            dlb_ref[1:2, :] = -d0
            dng_ref[...] = acc_ng_ref[...]

    rev_row = pl.BlockSpec((rows, LANES), lambda h, r: (n_steps - 1 - r, h))
    piece = jax.ShapeDtypeStruct((s, hw), BF16)
    return _pcall(
        body, name="hgrn_bwd", grid=(heads, n_steps),
        in_specs=[rev_row, zspec(0), zspec(1), zspec(2), zspec(3), rev_row,
                  pl.BlockSpec((None, per_step, LANES, LANES), lambda h, r: (h, n_steps - 1 - r, 0, 0)),
                  pl.BlockSpec((2, LANES), lambda h, r: (0, h)), pl.BlockSpec((1, LANES), lambda h, r: (0, h))],
        out_specs=[rev_row, rev_row, rev_row, rev_row,
                   pl.BlockSpec((2, LANES), lambda h, r: (0, h)), pl.BlockSpec((1, LANES), lambda h, r: (0, h))],
        out_shape=[piece, piece, piece, piece, jax.ShapeDtypeStruct((2, hw), F32), jax.ShapeDtypeStruct((1, hw), F32)],
        scratch_shapes=[pltpu.VMEM((LANES, LANES), F32), pltpu.VMEM((1, LANES), F32), pltpu.VMEM((1, LANES), F32)]
        + [pltpu.VMEM((rows, LANES), F32)] * 5,
        compiler_params=_params("parallel", "arbitrary"),
    )(dyb, z, z, z, z, o_raw, states, hg_lb, norm_g)


def _adamw(w, m, v, parts, name, outer=False):
    rows, cols = w.shape
    bc = cols if cols <= 4096 else _pick(cols, 4096)
    rb = _pick_rows(rows, max(8, (384 * 1024) // bc), mult=8)
    if outer and rb % LANES:
        rb = rows
    m_corr = 1.0 - ADAM_B1 ** ADAM_STEP
    v_corr = 1.0 - ADAM_B2 ** ADAM_STEP

    def body(w_ref, m_ref, v_ref, *refs):
        g_ref, d_ref, mo_ref, vo_ref = refs[-4:]
        if outer:
            cv = refs[0][...]
            g = _dot(cv * _sigmoid(cv), refs[1][...], TN, lax.Precision.HIGHEST)
        else:
            p_ref = refs[0]
            g = p_ref[0].astype(F32)
            for p in range(1, p_ref.shape[0]):
                g = g + p_ref[p].astype(F32)
        m2 = ADAM_B1 * m_ref[...] + (1.0 - ADAM_B1) * g
        v2 = ADAM_B2 * v_ref[...] + (1.0 - ADAM_B2) * (g * g)
        g_ref[...] = g
        mo_ref[...] = m2
        vo_ref[...] = v2
        d_ref[...] = -ADAM_LR * ((m2 / m_corr) / (jnp.sqrt(v2 / v_corr) + ADAM_EPS) + ADAM_WD * w_ref[...])

    blk = pl.BlockSpec((rb, bc), lambda i, j: (i, j))
    out = jax.ShapeDtypeStruct((rows, cols), F32)
    if outer:
        grad_specs = [pl.BlockSpec((N_DEV, rb), lambda i, j: (0, i)), pl.BlockSpec((N_DEV, bc), lambda i, j: (0, j))]
        grad_ins = tuple(parts)
    else:
        grad_specs = [pl.BlockSpec((parts.shape[0], rb, bc), lambda i, j: (0, i, j))]
        grad_ins = (parts,)
    return _pcall(
        body, name=name, grid=(rows // rb, cols // bc), in_specs=[blk, blk, blk] + grad_specs,
        out_specs=[blk] * 4, out_shape=[out] * 4, compiler_params=_params("parallel", "parallel"),
    )(w, m, v, *grad_ins)


SMALL = ("b_ada", "norm1_g", "b_gate", "gmlp_ln_g", "gmlp_ln_b", "gmlp_ws", "gmlp_bs", "hg_lb", "hg_norm_g",
         "norm2_g", "final_norm_g")
BIG = ("w_in", "w_branch_gmlp", "w_branch_hg", "w_out", "w_ffn_in", "w_ffn_out")
WEIGHTS = ("w_ada", "b_ada", "norm1_g", "w_in", "b_gate", "gmlp_ln_g", "gmlp_ln_b", "gmlp_ws", "gmlp_bs", "hg_lb",
           "hg_norm_g", "w_branch_gmlp", "w_branch_hg", "w_out", "norm2_g", "w_ffn_in", "w_ffn_out", "final_norm_g")


def _pack(parts):
    return jnp.concatenate([p.reshape(-1, LANES) for p in parts], axis=0)


def _step(x, c, loss_target, w, m, v):
    s, d = x.shape[1], x.shape[2]
    gw = w["gmlp_ln_g"].shape[-1]
    hw = w["hg_norm_g"].shape[-1]
    x2d, tgt = x[0], loss_target[0]
    mx, my, mc = lax.axis_index("x"), lax.axis_index("y"), lax.axis_index("c")
    chip = 2 * mx + my
    dev = 2 * chip + mc
    q_col = 2 * gw // LANES
    gate_col = (2 * gw + 4 * hw) // d
    place = jnp.stack([chip, mc]).astype(jnp.int32)
    _Order.last = None

    c_all = _all_gather8(c.reshape(-1, LANES), "gather_c").reshape(N_DEV, d)
    gather_in = _WeightGather("in", [w["w_in"][0]], chip, ring=True)
    n_ada = w["w_ada"].shape[-1]
    b_ada_q = lax.dynamic_slice(w["b_ada"], (0, chip * n_ada), (1, n_ada))
    mod_q = _ada_fwd(c_all, w["w_ada"][0], b_ada_q)
    mod_all = _all_gather8(mod_q, "gather_mod")
    mod = lax.dynamic_index_in_dim(mod_all, dev, axis=1, keepdims=False)[::2].reshape(1, 6 * d)
    sh1, sc1, gt1, sh2, sc2, gt2 = [mod[:, i * d:(i + 1) * d] for i in range(6)]

    gather_in.relay()
    gather_mix = _WeightGather("mix", [w[n][0] for n in ("w_branch_gmlp", "w_branch_hg", "w_out")], chip)
    gather_fi = _WeightGather("fi", [w["w_ffn_in"][0]], chip)
    gather_fo = _WeightGather("fo", [w["w_ffn_out"][0]], chip)

    norm1_g, norm2_g, final_g = w["norm1_g"], w["norm2_g"], w["final_norm_g"].reshape(1, d)
    ln_g, ln_b = w["gmlp_ln_g"], w["gmlp_ln_b"]
    ws = w["gmlp_ws"][0]
    groups = ws.shape[0]
    bs_t = jnp.pad(w["gmlp_bs"][0].T, ((0, 0), (0, LANES - groups)))
    hg_lb, hg_ng, b_gate = w["hg_lb"], w["hg_norm_g"], w["b_gate"]

    h1 = _norm_mod_fwd(x2d, norm1_g, sc1, sh1, "norm1_fwd")
    gather_in.pass_on()
    w_in, = gather_in.done()
    z = _matmul(h1, w_in, mode="nn", name="mm_z", out_dtype=F32, b_slots=True, bn=1280)
    gather_mix.pass_on()
    ya = _gmlp_fwd(z, ln_g, ln_b, ws, bs_t)
    yb, o_raw, states = _hgrn_fwd(z, hg_lb, hg_ng, q_col)
    w_bg, w_bh, w_out = gather_mix.done()
    w_out = w_out.reshape(-1, w_out.shape[-1])
    pa = _matmul(ya, w_bg, mode="nn", name="mm_pa", out_dtype=BF16, b_slots=True)
    pb = _matmul(yb, w_bh, mode="nn", name="mm_pb", out_dtype=BF16, b_slots=True)
    y = _gate_fwd(pa, pb, z, b_gate, gate_col)
    gather_fi.pass_on()
    yo = _matmul(y, w_out, mode="nn", name="mm_yo", out_dtype=F32)
    x1, h2 = _norm_mod_fwd(x2d, norm2_g, sc2, sh2, "norm2_fwd", res=yo, gt=gt1)
    w_fi, = gather_fi.done()
    a_act, up_act, hf = _ffn_in(h2, w_fi)
    gather_fo.pass_on()
    w_fo, = gather_fo.done()
    w_fo = w_fo.reshape(-1, w_fo.shape[-1])
    ffn = _matmul(hf, w_fo, mode="nn", name="mm_ffn", out_dtype=F32)
    dx2, dffn, loss_row, d_final_g, d_gt2 = _final_loss(x1, ffn, gt2, final_g, tgt)

    g_fo = _matmul(hf, dffn, mode="tn", name="mm_g_fo", out_dtype=BF16, bm=1408)
    dhf = _matmul(dffn, w_fo, mode="nt", name="mm_dhf", out_dtype=BF16, bn=1408)
    daup = _swiglu_bwd(dhf, a_act, up_act)
    g_fi = _matmul(h2, daup, mode="tn", name="mm_g_fi", out_dtype=BF16, out_slots=True, bn=1408)
    red_ffn = _GradReduce("ffn", [g_fo.reshape(N_CHIPS, -1, g_fo.shape[-1]), g_fi])
    dh2 = _matmul(daup, w_fi, mode="nt", name="mm_dh2", out_dtype=F32, b_slots=True)
    red_ffn.step(place)
    dx1, dyo, d_sh2, d_sc2, d_norm2, d_gt1 = _norm_mod_bwd(dh2, x1, dx2, norm2_g, sc2, "norm2_bwd", branch=yo, gt=gt1)
    g_out = _matmul(y, dyo, mode="tn", name="mm_g_out", out_dtype=BF16)
    dy = _matmul(dyo, w_out, mode="nt", name="mm_dy", out_dtype=F32)
    dpa, dpb, dz_gate, d_b_gate = _gate_bwd(dy, pa, pb, z, b_gate, gate_col)
    g_bg = _matmul(ya, dpa, mode="tn", name="mm_g_bg", out_dtype=BF16, out_slots=True)
    g_bh = _matmul(yb, dpb, mode="tn", name="mm_g_bh", out_dtype=BF16, out_slots=True)
    red_mix = _GradReduce("mix", [g_out.reshape(N_CHIPS, -1, g_out.shape[-1]), g_bg, g_bh])
    dya = _matmul(dpa, w_bg, mode="nt", name="mm_dya", out_dtype=F32, b_slots=True)
    dyb = _matmul(dpb, w_bh, mode="nt", name="mm_dyb", out_dtype=F32, b_slots=True)
    red_mix.step(place)
    dz_uv, d_ws, d_bs_t, d_ln_g, d_ln_b = _gmlp_bwd(dya, z, ln_g, ln_b, ws, bs_t)
    dz_q, dz_f, dz_i, dz_g, d_hg_lb, d_hg_ng = _hgrn_bwd(dyb, z, o_raw, states, hg_lb, hg_ng, q_col)
    dz = jnp.concatenate([dz_uv, dz_q, dz_f, dz_i, dz_g, dz_gate], axis=1)
    g_in = _matmul(h1, dz, mode="tn", name="mm_g_in", out_dtype=BF16, out_slots=True, bn=1280)
    red_in = _GradReduce("in", [g_in])
    dh1 = _matmul(dz, w_in, mode="nt", name="mm_dh1", out_dtype=F32, b_slots=True)
    red_in.pair(place)
    grad_x, d_sh1, d_sc1, d_norm1 = _norm_mod_bwd(dh1, x2d, dx1, norm1_g, sc1, "norm1_bwd")

    d_mod = jnp.concatenate([d_sh1, d_sc1, d_gt1, d_sh2, d_sc2, d_gt2], axis=1)
    small_part = {"b_ada": d_mod, "norm1_g": d_norm1, "b_gate": d_b_gate, "gmlp_ln_g": d_ln_g, "gmlp_ln_b": d_ln_b,
                  "gmlp_ws": d_ws, "gmlp_bs": d_bs_t[:, :groups].T, "hg_lb": d_hg_lb, "hg_norm_g": d_hg_ng,
                  "norm2_g": d_norm2, "final_norm_g": d_final_g}
    small_all = _all_gather8(_pack([small_part[n] for n in SMALL]), "gather_small")
    red_in.cross(after_last=True)
    d_mod_all = small_all[:, :6 * d // LANES].reshape(N_DEV, 6 * d)
    d_mod_q = lax.dynamic_slice(d_mod_all, (0, chip * n_ada), (N_DEV, n_ada))

    grad, delta, new_m, new_v = {}, {}, {}, {}

    def update(n, parts, outer=False):
        outs = _adamw(w[n][0], m[n][0], v[n][0], parts, "adamw_" + n, outer=outer)
        grad[n], delta[n], new_m[n], new_v[n] = [o[None] for o in outs]

    update("w_ada", (c_all, d_mod_q), outer=True)
    outs = _adamw(_pack([w[n] for n in SMALL]), _pack([m[n] for n in SMALL]), _pack([v[n] for n in SMALL]),
                  small_all, "adamw_small")
    red_ffn.join(place)
    red_mix.join(place)
    g_fo, g_fi = red_ffn.done()
    update("w_ffn_out", g_fo[None])
    update("w_ffn_in", g_fi[None])
    red_in.join(place)
    g_out, g_bg, g_bh = red_mix.done()
    update("w_out", g_out[None])
    update("w_branch_gmlp", g_bg[None])
    update("w_branch_hg", g_bh[None])
    update("w_in", red_in.done()[0][None])
    row = 0
    for n in SMALL:
        cnt = w[n].size // LANES
        for dst, o in zip((grad, delta, new_m, new_v), outs):
            dst[n] = o[row:row + cnt].reshape(w[n].shape)
        row += cnt

    loss = lax.psum(loss_row[0, 0], ("x", "y", "c"))
    return (loss, grad_x[None], *[grad[n] for n in WEIGHTS], *[delta[n] for n in WEIGHTS],
            *[new_m[n] for n in WEIGHTS], *[new_v[n] for n in WEIGHTS])


def kernel(x, c, w_ada, b_ada, norm1_g, w_in, b_gate, gmlp_ln_g, gmlp_ln_b, gmlp_ws, gmlp_bs, hg_lb, hg_norm_g, w_branch_gmlp, w_branch_hg, w_out, norm2_g, w_ffn_in, w_ffn_out, final_norm_g, loss_target, m_w_ada, m_b_ada, m_norm1_g, m_w_in, m_b_gate, m_gmlp_ln_g, m_gmlp_ln_b, m_gmlp_ws, m_gmlp_bs, m_hg_lb, m_hg_norm_g, m_w_branch_gmlp, m_w_branch_hg, m_w_out, m_norm2_g, m_w_ffn_in, m_w_ffn_out, m_final_norm_g, v_w_ada, v_b_ada, v_norm1_g, v_w_in, v_b_gate, v_gmlp_ln_g, v_gmlp_ln_b, v_gmlp_ws, v_gmlp_bs, v_hg_lb, v_hg_norm_g, v_w_branch_gmlp, v_w_branch_hg, v_w_out, v_norm2_g, v_w_ffn_in, v_w_ffn_out, v_final_norm_g):
    w = dict(w_ada=w_ada, b_ada=b_ada, norm1_g=norm1_g, w_in=w_in, b_gate=b_gate, gmlp_ln_g=gmlp_ln_g,
             gmlp_ln_b=gmlp_ln_b, gmlp_ws=gmlp_ws, gmlp_bs=gmlp_bs, hg_lb=hg_lb, hg_norm_g=hg_norm_g,
             w_branch_gmlp=w_branch_gmlp, w_branch_hg=w_branch_hg, w_out=w_out, norm2_g=norm2_g,
             w_ffn_in=w_ffn_in, w_ffn_out=w_ffn_out, final_norm_g=final_norm_g)
    m = dict(w_ada=m_w_ada, b_ada=m_b_ada, norm1_g=m_norm1_g, w_in=m_w_in, b_gate=m_b_gate, gmlp_ln_g=m_gmlp_ln_g,
             gmlp_ln_b=m_gmlp_ln_b, gmlp_ws=m_gmlp_ws, gmlp_bs=m_gmlp_bs, hg_lb=m_hg_lb, hg_norm_g=m_hg_norm_g,
             w_branch_gmlp=m_w_branch_gmlp, w_branch_hg=m_w_branch_hg, w_out=m_w_out, norm2_g=m_norm2_g,
             w_ffn_in=m_w_ffn_in, w_ffn_out=m_w_ffn_out, final_norm_g=m_final_norm_g)
    v = dict(w_ada=v_w_ada, b_ada=v_b_ada, norm1_g=v_norm1_g, w_in=v_w_in, b_gate=v_b_gate, gmlp_ln_g=v_gmlp_ln_g,
             gmlp_ln_b=v_gmlp_ln_b, gmlp_ws=v_gmlp_ws, gmlp_bs=v_gmlp_bs, hg_lb=v_hg_lb, hg_norm_g=v_hg_norm_g,
             w_branch_gmlp=v_w_branch_gmlp, w_branch_hg=v_w_branch_hg, w_out=v_w_out, norm2_g=v_norm2_g,
             w_ffn_in=v_w_ffn_in, w_ffn_out=v_w_ffn_out, final_norm_g=v_final_norm_g)
    return _step(x, c, loss_target, w, m, v)
```

```python
import functools

import jax
import jax.numpy as jnp
from jax import lax
from jax.experimental import pallas as pl
from jax.experimental.pallas import tpu as pltpu

F32 = jnp.float32
BF16 = jnp.bfloat16
EPS = 1e-6
LANES = 128
N_CHIPS = 4
N_DEV = 8
VMEM_LIMIT_BYTES = 56 * 1024 * 1024
HG_CHUNK = 32
HG_ROWS = 256
EXP_CLAMP = 80.0
ADAM_LR, ADAM_B1, ADAM_B2, ADAM_EPS, ADAM_WD, ADAM_STEP = 0.001, 0.9, 0.999, 1e-08, 0.01, 10
MESH = pl.DeviceIdType.MESH

NN = (((1,), (0,)), ((), ()))
NT = (((1,), (1,)), ((), ()))
TN = (((0,), (0,)), ((), ()))


def _dot(a, b, dims=NN, precision=None):
    return lax.dot_general(a, b, dims, precision=precision, preferred_element_type=F32)


def _params(*semantics):
    return pltpu.CompilerParams(dimension_semantics=semantics, vmem_limit_bytes=VMEM_LIMIT_BYTES)


class _Order:
    last = None


def _pcall(body, *, in_specs, out_specs, grid=(), scratch_shapes=(), num_scalar_prefetch=0, **kw):
    def run(*ins):
        deps = () if _Order.last is None else (_Order.last,)
        n_in, n_dep = len(ins), len(deps)

        def wrapped(*refs):
            body(*refs[:n_in], *refs[n_in + n_dep:])

        specs = list(in_specs) + [pl.BlockSpec(memory_space=pl.ANY)] * n_dep
        if num_scalar_prefetch:
            grid_spec = pltpu.PrefetchScalarGridSpec(
                num_scalar_prefetch=num_scalar_prefetch, grid=grid, in_specs=specs, out_specs=out_specs,
                scratch_shapes=scratch_shapes)
            outs = pl.pallas_call(wrapped, grid_spec=grid_spec, **kw)(*ins, *deps)
        else:
            outs = pl.pallas_call(wrapped, grid=grid, in_specs=specs, out_specs=out_specs,
                                  scratch_shapes=scratch_shapes, **kw)(*ins, *deps)
        _Order.last = jax.tree.leaves(outs)[0]
        return outs

    return run


def _pick_rows(dim, pref, mult=16):
    best = None
    for cand in range(mult, min(dim, pref) + 1, mult):
        if dim % cand == 0:
            best = cand
    assert best is not None, (dim, pref)
    return best


def _pick(dim, pref):
    if dim <= pref:
        return dim
    best = None
    for cand in range(LANES, pref + 1, LANES):
        if dim % cand == 0:
            best = cand
    assert best is not None, (dim, pref)
    return best


def _sigmoid(x):
    return 1.0 / (1.0 + jnp.exp(-x))


def _gelu(x):
    c = 0.7978845608028654
    return 0.5 * x * (1.0 + jnp.tanh(c * (x + 0.044715 * x * x * x)))


def _gelu_grad(x):
    c = 0.7978845608028654
    t = jnp.tanh(c * (x + 0.044715 * x * x * x))
    return 0.5 * (1.0 + t) + 0.5 * x * (1.0 - t * t) * c * (1.0 + 3.0 * 0.044715 * x * x)


def _rms(x):
    r = lax.rsqrt(jnp.mean(x * x, axis=-1, keepdims=True) + EPS)
    return x * r, r


def _colsum(x):
    return jnp.sum(x, axis=0, keepdims=True)


def _accumulate(first, ref, val):
    @pl.when(first)
    def _():
        ref[...] = val

    @pl.when(jnp.logical_not(first))
    def _():
        ref[...] += val


def _matmul(a, b, *, mode, name, out_dtype, b_slots=False, out_slots=False, bm=1024, bn=1024, bk=2816, rider=None):
    if mode == "nn":
        m, k = a.shape
        n = b.shape[2] * N_CHIPS if b_slots else b.shape[1]
        per = b.shape[2] if b_slots else n
    elif mode == "nt":
        m, k = a.shape
        n = b.shape[1] if b_slots else b.shape[0]
        per = b.shape[2] if b_slots else k
    else:
        k, m = a.shape
        n = b.shape[1]
        per = n // N_CHIPS if out_slots else n
    bm = _pick(m, bm)
    if mode == "nt":
        bn, bk = _pick(n, bn), _pick(per, bk)
    else:
        bn, bk = _pick(per, bn), _pick(k, bk)
    nk = k // bk
    per_blocks = per // (bk if mode == "nt" else bn)
    dims = {"nn": NN, "nt": NT, "tn": TN}[mode]

    steps = (m // bm) * (n // bn) * nk
    n_ride = len(rider[0]) if rider else 0

    def body(a_ref, b_ref, *rest):
        o_ref, acc = rest[n_ride], rest[n_ride + 1 + (rider[2] if rider else 0):]
        if rider:
            def ride(t, carry):
                rows = pl.ds(pl.multiple_of(t * 8, 8), 8)
                outs = rider[1](*[r[rows, :] for r in rest[:n_ride]])
                for ref, val in zip(rest[n_ride + 1:n_ride + 1 + rider[2]], outs):
                    ref[rows, :] = val
                return carry

            lax.fori_loop(0, rest[0].shape[0] // 8, ride, 0)
        part = _dot(a_ref[...], b_ref[...], dims)
        if nk == 1:
            o_ref[...] = part.astype(o_ref.dtype)
            return
        acc_ref, kk = acc[0], pl.program_id(2)

        @pl.when(kk == 0)
        def _():
            acc_ref[...] = part

        @pl.when(jnp.logical_and(kk > 0, kk < nk - 1))
        def _():
            acc_ref[...] += part

        @pl.when(kk == nk - 1)
        def _():
            o_ref[...] = (acc_ref[...] + part).astype(o_ref.dtype)

    if mode == "nn":
        a_spec = pl.BlockSpec((bm, bk), lambda i, j, kk: (i, kk))
        if b_slots:
            b_spec = pl.BlockSpec((None, bk, bn), lambda i, j, kk: (j // per_blocks, kk, j % per_blocks))
        else:
            b_spec = pl.BlockSpec((bk, bn), lambda i, j, kk: (kk, j))
    elif mode == "nt":
        a_spec = pl.BlockSpec((bm, bk), lambda i, j, kk: (i, kk))
        if b_slots:
            b_spec = pl.BlockSpec((None, bn, bk), lambda i, j, kk: (kk // per_blocks, j, kk % per_blocks))
        else:
            b_spec = pl.BlockSpec((bn, bk), lambda i, j, kk: (j, kk))
    else:
        a_spec = pl.BlockSpec((bk, bm), lambda i, j, kk: (kk, i))
        b_spec = pl.BlockSpec((bk, bn), lambda i, j, kk: (kk, j))
    if out_slots:
        o_spec = pl.BlockSpec((None, bm, bn), lambda i, j, kk: (j // per_blocks, i, j % per_blocks))
        out_shape = jax.ShapeDtypeStruct((N_CHIPS, m, per), out_dtype)
    else:
        o_spec = pl.BlockSpec((bm, bn), lambda i, j, kk: (i, j))
        out_shape = jax.ShapeDtypeStruct((m, n), out_dtype)
    scratch = [pltpu.VMEM((bm, bn), F32)] if nk > 1 else []
    if not rider:
        return _pcall(
            body, name=name, grid=(m // bm, n // bn, nk), in_specs=[a_spec, b_spec], out_specs=o_spec,
            out_shape=out_shape, scratch_shapes=scratch, compiler_params=_params("parallel", "parallel", "arbitrary"),
        )(a, b)
    rows, cols = rider[0][0].shape
    rb = rows // steps
    assert rows == rb * steps and rb % 8 == 0, (rows, steps)
    nj = n // bn
    ride_spec = pl.BlockSpec((rb, cols), lambda i, j, kk: ((i * nj + j) * nk + kk, 0))
    outs = _pcall(
        body, name=name, grid=(m // bm, nj, nk), in_specs=[a_spec, b_spec] + [ride_spec] * n_ride,
        out_specs=[o_spec] + [ride_spec] * rider[2],
        out_shape=[out_shape] + [jax.ShapeDtypeStruct((rows, cols), F32)] * rider[2],
        scratch_shapes=scratch, compiler_params=_params("arbitrary", "arbitrary", "arbitrary"),
    )(a, b, *rider[0])
    return outs[0], outs[1:]


def _place():
    x, y, c = lax.axis_index("x"), lax.axis_index("y"), lax.axis_index("c")
    chips = [(1 - x, y), (x, 1 - y), (1 - x, 1 - y)]
    return x, y, c, chips


def _all_gather8(block, name):
    def body(x_ref, out_ref, send_sems, recv_sems, local_sem):
        x, y, c, chips = _place()
        me, sibling = (x, y, c), (x, y, 1 - c)

        def slot(px, py, pc):
            return out_ref.at[4 * px + 2 * py + pc]

        def copy(k, blk, to, src=None):
            return pltpu.make_async_remote_copy(
                src_ref=slot(*blk) if src is None else src, dst_ref=slot(*blk),
                send_sem=send_sems.at[k], recv_sem=recv_sems.at[k], device_id=to, device_id_type=MESH)

        mine = pltpu.make_async_copy(x_ref, slot(*me), local_sem)
        mine.start()
        first = [copy(0, me, sibling, src=x_ref)]
        first += [copy(1 + j, me, (*chip, c), src=x_ref) for j, chip in enumerate(chips)]
        for cp in first:
            cp.start()
        passed = [copy(4 + j, (*chip, c), sibling) for j, chip in enumerate(chips)]
        for j, chip in enumerate(chips):
            copy(1 + j, (*chip, c), me).wait_recv()
            passed[j].start()
        copy(0, sibling, me).wait_recv()
        for j, chip in enumerate(chips):
            copy(4 + j, (*chip, 1 - c), me).wait_recv()
        for cp in first + passed:
            cp.wait_send()
        mine.wait()

    return _pcall(
        body, name=name, out_shape=jax.ShapeDtypeStruct((N_DEV,) + block.shape, block.dtype),
        in_specs=[pl.BlockSpec(memory_space=pltpu.VMEM)], out_specs=pl.BlockSpec(memory_space=pltpu.VMEM),
        scratch_shapes=[pltpu.SemaphoreType.DMA((7,)), pltpu.SemaphoreType.DMA((7,)), pltpu.SemaphoreType.DMA],
        compiler_params=pltpu.CompilerParams(vmem_limit_bytes=VMEM_LIMIT_BYTES),
    )(block)


HBM_SPEC = pl.BlockSpec(memory_space=pltpu.HBM)
SEM_SPEC = pl.BlockSpec(memory_space=pltpu.SEMAPHORE)
ANY_SPEC = pl.BlockSpec(memory_space=pl.ANY)
EFFECT = pltpu.SideEffectType.DATAFLOW_SIDE_EFFECTING


def _xfer_start(name, bufs, plan, n_copies, after_last=False):
    nb = len(bufs)
    deps = (_Order.last,) if after_last and _Order.last is not None else ()
    nd = len(deps)

    def body(*refs):
        send_sems, recv_sems = refs[nb + nd], refs[nb + nd + 1]
        token = refs[nb + nd + 2 + nb]
        for k, (src, dst, dev) in enumerate(plan(refs[:nb], *_place())):
            pltpu.make_async_remote_copy(src_ref=src, dst_ref=dst, send_sem=send_sems.at[k], recv_sem=recv_sems.at[k],
                                         device_id=dev, device_id_type=MESH).start()
        token[...] = jnp.zeros_like(token)

    outs = pl.pallas_call(
        body, name=name,
        out_shape=(pltpu.SemaphoreType.DMA((n_copies,)), pltpu.SemaphoreType.DMA((n_copies,)),
                   *[pltpu.HBM(b.shape, b.dtype) for b in bufs], jax.ShapeDtypeStruct((8, LANES), F32)),
        in_specs=[HBM_SPEC] * nb + [ANY_SPEC] * nd,
        out_specs=(SEM_SPEC, SEM_SPEC, *[HBM_SPEC] * nb, pl.BlockSpec(memory_space=pltpu.VMEM)),
        input_output_aliases={i: 2 + i for i in range(nb)},
        compiler_params=pltpu.CompilerParams(has_side_effects=EFFECT),
    )(*[pltpu.with_memory_space_constraint(b, pltpu.HBM) for b in bufs], *deps)
    _Order.last = outs[-1]
    return (outs[0], outs[1]), list(outs[2:2 + nb])


def _xfer_wait(name, sems, bufs, plan):
    nb = len(bufs)

    def body(*refs):
        send_sems, recv_sems = refs[nb], refs[nb + 1]
        for k, (src, dst, dev) in enumerate(plan(refs[:nb], *_place())):
            copy = pltpu.make_async_remote_copy(src_ref=src, dst_ref=dst, send_sem=send_sems.at[k],
                                                recv_sem=recv_sems.at[k], device_id=dev, device_id_type=MESH)
            copy.wait_send()
            copy.wait_recv()

    outs = pl.pallas_call(
        body, name=name, out_shape=tuple(pltpu.HBM(b.shape, b.dtype) for b in bufs),
        in_specs=[HBM_SPEC] * nb + [SEM_SPEC, SEM_SPEC, ANY_SPEC], out_specs=tuple([HBM_SPEC] * nb),
        input_output_aliases={i: i for i in range(nb)},
        compiler_params=pltpu.CompilerParams(has_side_effects=EFFECT),
    )(*bufs, *sems, _Order.last)
    _Order.last = outs[0]
    return list(outs)


def _half(ref, c, axis):
    rows = ref.shape[axis] // 2
    return pl.ds(c * rows, rows)


def _plan_weights_ici(n):
    def plan(refs, x, y, c, chips):
        out = []
        for w in range(n):
            region = refs[w].at[2 * x + y, _half(refs[w], c, 1), :]
            out += [(region, region, (*chip, c)) for chip in chips]
        return out
    return plan


def _plan_weights_ring(n):
    def plan(refs, x, y, c, chips):
        out = []
        for w in range(n):
            region = refs[w].at[2 * x + y, _half(refs[w], c, 1), :]
            out += [(region, region, (*chip, c)) for chip in chips[:2]]
        return out
    return plan


def _plan_weights_relay(n):
    def plan(refs, x, y, c, chips):
        out = []
        for w in range(n):
            quarter_rows = refs[w].shape[1] // 4
            upper = refs[w].at[2 * x + (1 - y), pl.ds(2 * c * quarter_rows, quarter_rows), :]
            lower = refs[w].at[2 * (1 - x) + y, pl.ds((2 * c + 1) * quarter_rows, quarter_rows), :]
            out += [(upper, upper, (1 - x, y, c)), (lower, lower, (x, 1 - y, c))]
        return out
    return plan


def _plan_weights_d2d(n):
    def plan(refs, x, y, c, chips):
        out = []
        for w in range(n):
            rows = _half(refs[w], c, 1)
            for chip in chips:
                region = refs[w].at[2 * chip[0] + chip[1], rows, :]
                out.append((region, region, (x, y, 1 - c)))
        return out
    return plan


def _plan_grads_d2d(n):
    def plan(refs, x, y, c, chips):
        return [(refs[w].at[:, _half(refs[w], 1 - c, 1), :], refs[n + w], (x, y, 1 - c)) for w in range(n)]
    return plan


def _plan_grads_ici(n):
    def plan(refs, x, y, c, chips):
        out = []
        for w in range(n):
            out += [(refs[w].at[2 * chip[0] + chip[1]], refs[n + w].at[2 * x + y], (*chip, c)) for chip in chips]
        return out
    return plan


def _plan_final_d2d(n):
    def plan(refs, x, y, c, chips):
        out = []
        for w in range(n):
            region = refs[w].at[_half(refs[w], c, 0), :]
            out.append((region, region, (x, y, 1 - c)))
        return out
    return plan


def _stream_blocks(hr, cols):
    bc = cols if cols <= 4096 else _pick(cols, 4096)
    return _pick_rows(hr, max(16, (768 * 1024) // bc)), bc


def _pre_reduce(g, landed, place, name):
    _, rows, cols = g.shape
    hr = rows // 2
    rb, bc = _stream_blocks(hr, cols)
    nrb = hr // rb

    def body(place_ref, g_ref, l_ref, o_ref):
        o_ref[...] = (g_ref[...].astype(F32) + l_ref[...].astype(F32)).astype(o_ref.dtype)

    return _pcall(
        body, name=name, num_scalar_prefetch=1, grid=(N_CHIPS, nrb, cols // bc),
        in_specs=[pl.BlockSpec((None, rb, bc), lambda j, i, k, p: (j, p[1] * nrb + i, k)),
                  pl.BlockSpec((None, rb, bc), lambda j, i, k, p: (j, i, k))],
        out_specs=pl.BlockSpec((None, rb, bc), lambda j, i, k, p: (j, i, k)),
        out_shape=jax.ShapeDtypeStruct((N_CHIPS, hr, cols), g.dtype),
        compiler_params=_params("parallel", "parallel", "parallel"),
    )(place, g, landed)


def _sum_slots(mine, landed, place, name):
    _, hr, cols = mine.shape
    rb, bc = _stream_blocks(hr, cols)
    rb = _pick_rows(hr, max(16, rb // 2))
    nrb = hr // rb

    def body(place_ref, m_ref, l_ref, o_ref):
        chip = place_ref[0]
        own = m_ref[...].astype(F32)
        total = jnp.where(chip == 0, own, l_ref[0].astype(F32))
        for j in range(1, N_CHIPS):
            total = total + jnp.where(chip == j, own, l_ref[j].astype(F32))
        o_ref[...] = total

    return _pcall(
        body, name=name, num_scalar_prefetch=1, grid=(nrb, cols // bc),
        in_specs=[pl.BlockSpec((None, rb, bc), lambda i, k, p: (p[0], i, k)),
                  pl.BlockSpec((N_CHIPS, rb, bc), lambda i, k, p: (0, i, k))],
        out_specs=pl.BlockSpec((rb, bc), lambda i, k, p: (p[1] * nrb + i, k)),
        out_shape=jax.ShapeDtypeStruct((2 * hr, cols), F32),
        compiler_params=_params("parallel", "parallel"),
    )(place, mine, landed)


class _WeightGather:
    def __init__(self, tag, quarters, chip, ring=False):
        self.tag, self.n, self.ring = tag, len(quarters), ring
        zones = [lax.dynamic_update_slice(lax.empty((N_CHIPS,) + q.shape, BF16), q.astype(BF16)[None], (chip, 0, 0))
                 for q in quarters]
        self.plan = _plan_weights_ring(self.n) if ring else _plan_weights_ici(self.n)
        self.sems, self.bufs = _xfer_start("wici_start_" + tag, zones, self.plan, (2 if ring else 3) * self.n,
                                           after_last=True)

    def relay(self):
        bufs = _xfer_wait("wici_wait_" + self.tag, self.sems, self.bufs, self.plan)
        self.plan = _plan_weights_relay(self.n)
        self.sems, self.bufs = _xfer_start("wrel_start_" + self.tag, bufs, self.plan, 2 * self.n)

    def pass_on(self):
        bufs = _xfer_wait(("wrel_wait_" if self.ring else "wici_wait_") + self.tag, self.sems, self.bufs, self.plan)
        self.plan = _plan_weights_d2d(self.n)
        self.sems, self.bufs = _xfer_start("wd2d_start_" + self.tag, bufs, self.plan, 3 * self.n)

    def done(self):
        return _xfer_wait("wd2d_wait_" + self.tag, self.sems, self.bufs, self.plan)


class _GradReduce:
    def __init__(self, tag, grads):
        self.tag, self.n = tag, len(grads)
        zones = [lax.empty((N_CHIPS, g.shape[1] // 2, g.shape[2]), g.dtype) for g in grads]
        self.plan = _plan_grads_d2d(self.n)
        self.sems, self.bufs = _xfer_start("gd2d_start_" + tag, list(grads) + zones, self.plan, self.n)

    def pair(self, place):
        n = self.n
        bufs = _xfer_wait("gd2d_wait_" + self.tag, self.sems, self.bufs, self.plan)
        self.halves = [_pre_reduce(bufs[w], bufs[n + w], place, f"pre_reduce_{self.tag}{w}") for w in range(n)]

    def cross(self, after_last=False):
        zones = [lax.empty(h.shape, h.dtype) for h in self.halves]
        self.plan = _plan_grads_ici(self.n)
        self.sems, self.bufs = _xfer_start("gici_start_" + self.tag, self.halves + zones, self.plan, 3 * self.n,
                                           after_last=after_last)

    def step(self, place):
        self.pair(place)
        self.cross()

    def join(self, place):
        n = self.n
        bufs = _xfer_wait("gici_wait_" + self.tag, self.sems, self.bufs, self.plan)
        sums = [_sum_slots(bufs[w], bufs[n + w], place, f"sum_slots_{self.tag}{w}") for w in range(n)]
        self.plan = _plan_final_d2d(n)
        self.sems, self.bufs = _xfer_start("gfin_start_" + self.tag, sums, self.plan, n)

    def done(self):
        return _xfer_wait("gfin_wait_" + self.tag, self.sems, self.bufs, self.plan)


def _ada_fwd(c_all, w_q, b_q):
    d, n = w_q.shape
    bn = _pick(n, 512)

    def body(c_ref, w_ref, b_ref, o_ref):
        cv = c_ref[...]
        act = cv * _sigmoid(cv)
        o_ref[...] = _dot(act, w_ref[...], NN, lax.Precision.HIGHEST) + b_ref[...]

    return _pcall(
        body, name="ada_fwd", grid=(n // bn,),
        in_specs=[pl.BlockSpec((N_DEV, d), lambda j: (0, 0)), pl.BlockSpec((d, bn), lambda j: (0, j)),
                  pl.BlockSpec((1, bn), lambda j: (0, j))],
        out_specs=pl.BlockSpec((N_DEV, bn), lambda j: (0, j)),
        out_shape=jax.ShapeDtypeStruct((N_DEV, n), F32), compiler_params=_params("parallel"),
    )(c_all, w_q, b_q)


def _row_spec(rb, width, col=0):
    return pl.BlockSpec((rb, width), lambda i, col=col: (i, col))


def _vec_spec(width, col=0):
    return pl.BlockSpec((1, width), lambda i, col=col: (0, col))


def _norm_mod_fwd(x, g, sc, sh, name, res=None, gt=None):
    s, d = x.shape
    rb = _pick(s, 256)
    has_res = res is not None

    def body(*refs):
        if has_res:
            x_ref, res_ref, gt_ref, g_ref, sc_ref, sh_ref, x1_ref, h_ref = refs
            xv = x_ref[...] + gt_ref[...] * res_ref[...]
            x1_ref[...] = xv
        else:
            x_ref, g_ref, sc_ref, sh_ref, h_ref = refs
            xv = x_ref[...]
        xh, _ = _rms(xv)
        h_ref[...] = (xh * g_ref[...] * (1.0 + sc_ref[...]) + sh_ref[...]).astype(BF16)

    row, vec = _row_spec(rb, d), _vec_spec(d)
    if has_res:
        ins, in_specs = (x, res, gt, g, sc, sh), [row, row, vec, vec, vec, vec]
        out_shape = [jax.ShapeDtypeStruct((s, d), F32), jax.ShapeDtypeStruct((s, d), BF16)]
        out_specs = [row, row]
    else:
        ins, in_specs = (x, g, sc, sh), [row, vec, vec, vec]
        out_shape, out_specs = jax.ShapeDtypeStruct((s, d), BF16), row
    return _pcall(body, name=name, grid=(s // rb,), in_specs=in_specs, out_specs=out_specs,
                          out_shape=out_shape, compiler_params=_params("parallel"))(*ins)


def _final_loss(x1, f, gt2, g_final, target):
    s, d = x1.shape
    rb = _pick(s, 256)

    def body(x1_ref, f_ref, gt_ref, g_ref, t_ref, dx_ref, df_ref, loss_ref, dg_ref, dgt_ref):
        first = pl.program_id(0) == 0
        fv, gt, gv = f_ref[...], gt_ref[...], g_ref[...]
        x2 = x1_ref[...] + gt * fv
        xh, r = _rms(x2)
        err = xh * gv - t_ref[...]
        blk = 0.5 * jnp.sum(jnp.sum(err * err, axis=1, keepdims=True), axis=0, keepdims=True) / d
        dy = err / d
        dxh = dy * gv
        dx = r * (dxh - xh * jnp.mean(dxh * xh, axis=-1, keepdims=True))
        dx_ref[...] = dx
        df_ref[...] = (dx * gt).astype(BF16)
        _accumulate(first, loss_ref, jnp.broadcast_to(blk, (1, LANES)))
        _accumulate(first, dg_ref, _colsum(dy * xh))
        _accumulate(first, dgt_ref, _colsum(dx * fv))

    row, vec = _row_spec(rb, d), _vec_spec(d)
    return _pcall(
        body, name="final_loss", grid=(s // rb,), in_specs=[row, row, vec, vec, row],
        out_specs=[row, row, _vec_spec(LANES), vec, vec],
        out_shape=[jax.ShapeDtypeStruct((s, d), F32), jax.ShapeDtypeStruct((s, d), BF16),
                   jax.ShapeDtypeStruct((1, LANES), F32), jax.ShapeDtypeStruct((1, d), F32),
                   jax.ShapeDtypeStruct((1, d), F32)],
        compiler_params=_params("arbitrary"),
    )(x1, f, gt2, g_final, target)


def _norm_mod_bwd(dh, xin, dres, g, sc, name, branch=None, gt=None):
    s, d = xin.shape
    rb = _pick(s, 256)
    has_branch = branch is not None

    def body(*refs):
        if has_branch:
            dh_ref, x_ref, dres_ref, g_ref, sc_ref, br_ref, gt_ref, dx_ref, dbr_ref, dsh_ref, dsc_ref, dg_ref, dgt_ref = refs
        else:
            dh_ref, x_ref, dres_ref, g_ref, sc_ref, dx_ref, dsh_ref, dsc_ref, dg_ref = refs
        first = pl.program_id(0) == 0
        gv = g_ref[...]
        xh, r = _rms(x_ref[...])
        dhv = dh_ref[...]
        dn = dhv * (1.0 + sc_ref[...])
        dxh = dn * gv
        dx = dres_ref[...] + r * (dxh - xh * jnp.mean(dxh * xh, axis=-1, keepdims=True))
        dx_ref[...] = dx
        _accumulate(first, dsh_ref, _colsum(dhv))
        _accumulate(first, dsc_ref, _colsum(dhv * xh * gv))
        _accumulate(first, dg_ref, _colsum(dn * xh))
        if has_branch:
            dbr_ref[...] = (dx * gt_ref[...]).astype(BF16)
            _accumulate(first, dgt_ref, _colsum(dx * br_ref[...]))

    row, vec = _row_spec(rb, d), _vec_spec(d)
    vec_shape = jax.ShapeDtypeStruct((1, d), F32)
    if has_branch:
        ins, in_specs = (dh, xin, dres, g, sc, branch, gt), [row, row, row, vec, vec, row, vec]
        out_specs = [row, row, vec, vec, vec, vec]
        out_shape = [jax.ShapeDtypeStruct((s, d), F32), jax.ShapeDtypeStruct((s, d), BF16)] + [vec_shape] * 4
    else:
        ins, in_specs = (dh, xin, dres, g, sc), [row, row, row, vec, vec]
        out_specs = [row, vec, vec, vec]
        out_shape = [jax.ShapeDtypeStruct((s, d), F32)] + [vec_shape] * 3
    return _pcall(body, name=name, grid=(s // rb,), in_specs=in_specs, out_specs=out_specs,
                          out_shape=out_shape, compiler_params=_params("arbitrary"))(*ins)


def _gate_fwd(pa, pb, z, b_gate, gate_col):
    s, d = pa.shape
    rb = _pick(s, 256)

    def body(pa_ref, pb_ref, za_ref, zb_ref, ba_ref, bb_ref, y_ref):
        ga = _sigmoid(za_ref[...] + ba_ref[...])
        gb = _sigmoid(zb_ref[...] + bb_ref[...])
        y_ref[...] = (ga * pa_ref[...].astype(F32) + gb * pb_ref[...].astype(F32)).astype(BF16)

    row = _row_spec(rb, d)
    return _pcall(
        body, name="gate_fwd", grid=(s // rb,),
        in_specs=[row, row, _row_spec(rb, d, gate_col), _row_spec(rb, d, gate_col + 1), _vec_spec(d, 0), _vec_spec(d, 1)],
        out_specs=row, out_shape=jax.ShapeDtypeStruct((s, d), BF16), compiler_params=_params("parallel"),
    )(pa, pb, z, z, b_gate, b_gate)


def _gate_bwd(dy, pa, pb, z, b_gate, gate_col):
    s, d = pa.shape
    rb = _pick(s, 256)

    def body(dy_ref, pa_ref, pb_ref, za_ref, zb_ref, ba_ref, bb_ref, dpa_ref, dpb_ref, dz_ref, db_ref):
        first = pl.program_id(0) == 0
        dyv = dy_ref[...]
        ga = _sigmoid(za_ref[...] + ba_ref[...])
        gb = _sigmoid(zb_ref[...] + bb_ref[...])
        dpa_ref[...] = (dyv * ga).astype(BF16)
        dpb_ref[...] = (dyv * gb).astype(BF16)
        dga = dyv * pa_ref[...].astype(F32) * ga * (1.0 - ga)
        dgb = dyv * pb_ref[...].astype(F32) * gb * (1.0 - gb)
        dz_ref[:, :d] = dga.astype(BF16)
        dz_ref[:, d:] = dgb.astype(BF16)

        @pl.when(first)
        def _():
            db_ref[:, :d] = _colsum(dga)
            db_ref[:, d:] = _colsum(dgb)

        @pl.when(jnp.logical_not(first))
        def _():
            db_ref[:, :d] += _colsum(dga)
            db_ref[:, d:] += _colsum(dgb)

    row = _row_spec(rb, d)
    return _pcall(
        body, name="gate_bwd", grid=(s // rb,),
        in_specs=[row, row, row, _row_spec(rb, d, gate_col), _row_spec(rb, d, gate_col + 1), _vec_spec(d, 0), _vec_spec(d, 1)],
        out_specs=[row, row, _row_spec(rb, 2 * d), _vec_spec(2 * d)],
        out_shape=[jax.ShapeDtypeStruct((s, d), BF16), jax.ShapeDtypeStruct((s, d), BF16),
                   jax.ShapeDtypeStruct((s, 2 * d), BF16), jax.ShapeDtypeStruct((1, 2 * d), F32)],
        compiler_params=_params("arbitrary"),
    )(dy, pa, pb, z, z, b_gate, b_gate)


def _ffn_in(h, w_fi, bm=512, bn=1408):
    s, d = h.shape
    per = w_fi.shape[2]
    ff = 2 * per
    bm, bn = _pick(s, bm), _pick(per, bn)
    per_blocks = per // bn

    def body(h_ref, wa_ref, wu_ref, a_ref, u_ref, hf_ref):
        hv = h_ref[...]
        a = _dot(hv, wa_ref[...])
        up = _dot(hv, wu_ref[...])
        a_ref[...] = a.astype(BF16)
        u_ref[...] = up.astype(BF16)
        hf_ref[...] = (a * _sigmoid(a) * up).astype(BF16)

    out = pl.BlockSpec((bm, bn), lambda j, i: (i, j))
    shape = jax.ShapeDtypeStruct((s, ff), BF16)
    return _pcall(
        body, name="ffn_in", grid=(ff // bn, s // bm),
        in_specs=[pl.BlockSpec((bm, d), lambda j, i: (i, 0)),
                  pl.BlockSpec((None, d, bn), lambda j, i: (j // per_blocks, 0, j % per_blocks)),
                  pl.BlockSpec((None, d, bn), lambda j, i: (2 + j // per_blocks, 0, j % per_blocks))],
        out_specs=[out, out, out], out_shape=[shape, shape, shape],
        compiler_params=_params("parallel", "parallel"),
    )(h, w_fi, w_fi)


def _swiglu_bwd(dhf, a_act, up_act):
    s, ff = a_act.shape
    rb = _pick(s, 128)

    def body(d_ref, a_ref, u_ref, o_ref):
        a, dv = a_ref[...].astype(F32), d_ref[...].astype(F32)
        sa = _sigmoid(a)
        o_ref[:, :ff] = (dv * u_ref[...].astype(F32) * sa * (1.0 + a * (1.0 - sa))).astype(BF16)
        o_ref[:, ff:] = (dv * a * sa).astype(BF16)

    return _pcall(
        body, name="swiglu_bwd", grid=(s // rb,),
        in_specs=[_row_spec(rb, ff), _row_spec(rb, ff), _row_spec(rb, ff)],
        out_specs=_row_spec(rb, 2 * ff), out_shape=jax.ShapeDtypeStruct((s, 2 * ff), BF16),
        compiler_params=_params("parallel"),
    )(dhf, a_act, up_act)


def _tril(n):
    return lax.broadcasted_iota(jnp.int32, (n, n), 0) >= lax.broadcasted_iota(jnp.int32, (n, n), 1)


def _gmlp_norm(v, ln_g, ln_b):
    gv = _gelu(v)
    mu = jnp.mean(gv, axis=-1, keepdims=True)
    cen = gv - mu
    rs = lax.rsqrt(jnp.mean(cen * cen, axis=-1, keepdims=True) + EPS)
    xh = cen * rs
    return xh, rs, xh * ln_g + ln_b


def _gmlp_fwd(z, ln_g, ln_b, ws, bs_t):
    s = z.shape[0]
    gw = ln_g.shape[1]
    groups, chunk, _ = ws.shape

    def body(u_ref, v_ref, lg_ref, lb_ref, ws_ref, bs_ref, ya_ref):
        gu = _gelu(u_ref[...])
        _, _, vn = _gmlp_norm(v_ref[...], lg_ref[...], lb_ref[...])
        mask = _tril(chunk)
        for g in range(groups):
            cols = slice(g * LANES, (g + 1) * LANES)
            wm = jnp.where(mask, ws_ref[g], 0.0).astype(BF16)
            sg = _dot(wm, vn[:, cols].astype(BF16)) + bs_ref[:, g:g + 1]
            ya_ref[:, cols] = (gu[:, cols] * sg).astype(BF16)

    return _pcall(
        body, name="gmlp_fwd", grid=(s // chunk,),
        in_specs=[_row_spec(chunk, gw, 0), _row_spec(chunk, gw, 1), _vec_spec(gw), _vec_spec(gw),
                  pl.BlockSpec((groups, chunk, chunk), lambda i: (0, 0, 0)), pl.BlockSpec((chunk, LANES), lambda i: (0, 0))],
        out_specs=_row_spec(chunk, gw), out_shape=jax.ShapeDtypeStruct((s, gw), BF16),
        compiler_params=_params("parallel"),
    )(z, z, ln_g, ln_b, ws, bs_t)


def _gmlp_bwd(dya, z, ln_g, ln_b, ws, bs_t):
    s = z.shape[0]
    gw = ln_g.shape[1]
    groups, chunk, _ = ws.shape

    def body(dya_ref, u_ref, v_ref, lg_ref, lb_ref, ws_ref, bs_ref, duv_ref, dws_ref, dbs_ref, dlg_ref, dlb_ref, dvn_ref):
        first = pl.program_id(0) == 0
        u, v, lg = u_ref[...], v_ref[...], lg_ref[...]
        gu = _gelu(u)
        xh, rs, vn = _gmlp_norm(v, lg, lb_ref[...])
        dyav = dya_ref[...]
        mask = _tril(chunk)
        lane = lax.broadcasted_iota(jnp.int32, (chunk, LANES), 1)
        dbs = jnp.zeros((chunk, LANES), F32)
        for g in range(groups):
            cols = slice(g * LANES, (g + 1) * LANES)
            wm = jnp.where(mask, ws_ref[g], 0.0).astype(BF16)
            vg = vn[:, cols].astype(BF16)
            sg = _dot(wm, vg) + bs_ref[:, g:g + 1]
            ds = dyav[:, cols] * gu[:, cols]
            duv_ref[:, cols] = (dyav[:, cols] * sg * _gelu_grad(u[:, cols])).astype(BF16)
            dsb = ds.astype(BF16)
            _accumulate(first, dws_ref.at[g], jnp.where(mask, _dot(dsb, vg, NT), 0.0))
            dbs = dbs + jnp.where(lane == g, jnp.sum(ds, axis=-1, keepdims=True), 0.0)
            dvn_ref[:, cols] = _dot(wm, dsb, TN)
        dvn = dvn_ref[...]
        _accumulate(first, dbs_ref, dbs)
        _accumulate(first, dlb_ref, _colsum(dvn))
        _accumulate(first, dlg_ref, _colsum(dvn * xh))
        dxh = dvn * lg
        dgv = rs * (dxh - jnp.mean(dxh, axis=-1, keepdims=True) - xh * jnp.mean(dxh * xh, axis=-1, keepdims=True))
        duv_ref[:, gw:] = (dgv * _gelu_grad(v)).astype(BF16)

    return _pcall(
        body, name="gmlp_bwd", grid=(s // chunk,),
        in_specs=[_row_spec(chunk, gw), _row_spec(chunk, gw, 0), _row_spec(chunk, gw, 1), _vec_spec(gw), _vec_spec(gw),
                  pl.BlockSpec((groups, chunk, chunk), lambda i: (0, 0, 0)), pl.BlockSpec((chunk, LANES), lambda i: (0, 0))],
        out_specs=[_row_spec(chunk, 2 * gw), pl.BlockSpec((groups, chunk, chunk), lambda i: (0, 0, 0)),
                   pl.BlockSpec((chunk, LANES), lambda i: (0, 0)), _vec_spec(gw), _vec_spec(gw)],
        out_shape=[jax.ShapeDtypeStruct((s, 2 * gw), BF16), jax.ShapeDtypeStruct((groups, chunk, chunk), F32),
                   jax.ShapeDtypeStruct((chunk, LANES), F32), jax.ShapeDtypeStruct((1, gw), F32),
                   jax.ShapeDtypeStruct((1, gw), F32)],
        scratch_shapes=[pltpu.VMEM((chunk, gw), F32)],
        compiler_params=_params("arbitrary"),
    )(dya, z, z, ln_g, ln_b, ws, bs_t)


def _lower_bound(lb_ref):
    a0, a1 = lb_ref[0:1, :], lb_ref[1:2, :]
    mx = jnp.maximum(a0, a1)
    e0, e1 = jnp.exp(a0 - mx), jnp.exp(a1 - mx)
    return e0 / (e0 + e1)


def _sum_dot(mask, x):
    hi = x.astype(BF16)
    rest = x - hi.astype(F32)
    mid = rest.astype(BF16)
    low = (rest - mid.astype(F32)).astype(BF16)
    return _dot(mask, hi) + _dot(mask, mid) + _dot(mask, low)


def _ones_where(mask):
    return jnp.where(mask, 1.0, 0.0).astype(BF16)


def _hg_masks(rows, t):
    r = lax.broadcasted_iota(jnp.int32, (rows, rows), 0)
    c = lax.broadcasted_iota(jnp.int32, (rows, rows), 1)
    same = (r // t) == (c // t)
    incl = jnp.logical_and(same, c <= r)
    upto_mid = jnp.logical_and(same, (c % t) <= t // 2)
    rev = jnp.logical_and(same, c >= r)
    return same, incl, upto_mid, rev


def _hg_block(q, fp, lb, masks):
    rows = q.shape[0]
    same, incl, upto_mid, _ = masks
    sig = _sigmoid(fp)
    f = lb + (1.0 - lb) * sig
    k = 1.0 - f
    sq = _sigmoid(q)
    qa = q * sq
    stacked = jnp.concatenate([_ones_where(m) for m in (incl, same, upto_mid)], axis=0)
    sums = _sum_dot(stacked, jnp.log(f))
    b, b_last, b_mid = sums[:rows], sums[rows:2 * rows], sums[2 * rows:]
    e_q = jnp.exp(jnp.minimum(b - b_mid, EXP_CLAMP))
    e_k = jnp.exp(jnp.minimum(b_mid - b, EXP_CLAMP))
    e_in = jnp.exp(b)
    e_out = jnp.exp(b_last - b)
    return dict(sig=sig, f=f, k=k, sq=sq, qa=qa, e_last=jnp.exp(b_last), e_q=e_q, e_k=e_k, e_in=e_in, e_out=e_out,
                q_hat=(qa * e_q).astype(BF16), k_hat=(k * e_k).astype(BF16),
                q_in=(qa * e_in).astype(BF16), k_out=k * e_out)


def _hgrn_fwd(z, hg_lb, norm_g, q_col):
    s = z.shape[0]
    hw = norm_g.shape[1]
    heads = hw // LANES
    t = HG_CHUNK
    rows = min(HG_ROWS, s)
    per_step = rows // t
    per = hw // LANES

    def zspec(which):
        return pl.BlockSpec((rows, LANES), lambda h, r, which=which: (r, q_col + which * per + h))

    def body(q_ref, f_ref, i_ref, g_ref, lb_ref, ng_ref, yb_ref, o_ref, st_out_ref, st_ref, e_last_ref, inter_ref):
        @pl.when(pl.program_id(1) == 0)
        def _():
            st_ref[...] = jnp.zeros_like(st_ref)

        masks = _hg_masks(rows, t)
        blk = _hg_block(q_ref[...], f_ref[...], _lower_bound(lb_ref), masks)
        iv = i_ref[...].astype(BF16)
        q_in, k_out = blk["q_in"], blk["k_out"].astype(BF16)
        e_last_ref[...] = blk["e_last"]
        attn = jnp.where(masks[1], _dot(blk["q_hat"], blk["k_hat"], NT), 0.0).astype(BF16)
        o = _dot(attn, iv)
        grown = [_dot(iv[j * t:(j + 1) * t], k_out[j * t:(j + 1) * t], TN) for j in range(per_step)]
        st = st_ref[...]
        for j in range(per_step):
            st_out_ref[j] = st
            inter_ref[j * t:(j + 1) * t, :] = _dot(q_in[j * t:(j + 1) * t], st.astype(BF16), NT)
            st = st * e_last_ref[j * t:j * t + 1, :] + grown[j]
        st_ref[...] = st
        o = o + inter_ref[...]
        o_ref[...] = o
        og = g_ref[...]
        on, _ = _rms(o)
        yb_ref[...] = (on * ng_ref[...] * (og * _sigmoid(og))).astype(BF16)

    out_row = pl.BlockSpec((rows, LANES), lambda h, r: (r, h))
    return _pcall(
        body, name="hgrn_fwd", grid=(heads, s // rows),
        in_specs=[zspec(0), zspec(1), zspec(2), zspec(3),
                  pl.BlockSpec((2, LANES), lambda h, r: (0, h)), pl.BlockSpec((1, LANES), lambda h, r: (0, h))],
        out_specs=[out_row, out_row, pl.BlockSpec((None, per_step, LANES, LANES), lambda h, r: (h, r, 0, 0))],
        out_shape=[jax.ShapeDtypeStruct((s, hw), BF16), jax.ShapeDtypeStruct((s, hw), F32),
                   jax.ShapeDtypeStruct((heads, s // t, LANES, LANES), F32)],
        scratch_shapes=[pltpu.VMEM((LANES, LANES), F32), pltpu.VMEM((rows, LANES), F32), pltpu.VMEM((rows, LANES), F32)],
        compiler_params=_params("parallel", "arbitrary"),
    )(z, z, z, z, hg_lb, norm_g)


def _hgrn_bwd(dyb, z, o_raw, states, hg_lb, norm_g, q_col):
    s = z.shape[0]
    hw = norm_g.shape[1]
    heads = hw // LANES
    t = HG_CHUNK
    rows = min(HG_ROWS, s)
    per_step = rows // t
    per = hw // LANES
    n_steps = s // rows

    def zspec(which):
        return pl.BlockSpec((rows, LANES), lambda h, r, which=which: (n_steps - 1 - r, q_col + which * per + h))

    def body(dyb_ref, q_ref, f_ref, i_ref, g_ref, o_ref, st_in_ref, lb_ref, ng_ref,
             dq_ref, df_ref, di_ref, dg_ref, dlb_ref, dng_ref, dst_ref, acc_lb_ref, acc_ng_ref,
             e_last_ref, dq_in_ref, dk_out_ref, di_inter_ref, carry_ref):
        step = pl.program_id(1)

        @pl.when(step == 0)
        def _():
            dst_ref[...] = jnp.zeros_like(dst_ref)
            acc_lb_ref[...] = jnp.zeros_like(acc_lb_ref)
            acc_ng_ref[...] = jnp.zeros_like(acc_ng_ref)

        lb = _lower_bound(lb_ref)
        ng = ng_ref[...]
        masks = _hg_masks(rows, t)
        same, incl, _, rev = masks
        q = q_ref[...]
        blk = _hg_block(q, f_ref[...], lb, masks)
        iv = i_ref[...].astype(BF16)
        o, og, dy = o_ref[...], g_ref[...], dyb_ref[...]
        so = _sigmoid(og)
        on, r = _rms(o)
        acc_ng_ref[...] += _colsum(dy * on * (og * so))
        dg_ref[...] = (dy * on * ng * so * (1.0 + og * (1.0 - so))).astype(BF16)
        don = dy * ng * (og * so)
        do = (r * (don - on * jnp.mean(don * on, axis=-1, keepdims=True))).astype(BF16)
        q_hat, k_hat, q_in, k_out = blk["q_hat"], blk["k_hat"], blk["q_in"], blk["k_out"]
        k_out_b = k_out.astype(BF16)
        attn = jnp.where(incl, _dot(q_hat, k_hat, NT), 0.0).astype(BF16)
        d_attn = jnp.where(incl, _dot(do, iv, NT), 0.0).astype(BF16)
        di_intra = _dot(attn, do, TN)
        dq_hat = _dot(d_attn, k_hat)
        dk_hat = _dot(d_attn, q_hat, TN)
        e_last_ref[...] = blk["e_last"]
        grown = [_dot(do[j * t:(j + 1) * t], q_in[j * t:(j + 1) * t], TN) for j in range(per_step)]
        dst = dst_ref[...]
        for j in reversed(range(per_step)):
            rs_ = slice(j * t, (j + 1) * t)
            e_last = e_last_ref[j * t:j * t + 1, :]
            st_prev, dst_b = st_in_ref[j], dst.astype(BF16)
            dq_in_ref[rs_, :] = _dot(do[rs_], st_prev.astype(BF16))
            dk_out_ref[rs_, :] = _dot(iv[rs_], dst_b)
            di_inter_ref[rs_, :] = _dot(k_out_b[rs_], dst_b, NT)
            carry_ref[rs_, :] = jnp.broadcast_to(e_last * _colsum(st_prev * dst), (t, LANES))
            dst = dst * e_last + grown[j]
        dst_ref[...] = dst
        di_ref[...] = (di_intra + di_inter_ref[...]).astype(BF16)
        dk_out = dk_out_ref[...]
        dqa = dq_in_ref[...] * blk["e_in"] + dq_hat * blk["e_q"]
        dk = dk_out * blk["e_out"] + dk_hat * blk["e_k"]
        db = blk["qa"] * dqa - blk["k"] * dk
        sum_mask = jnp.concatenate([_ones_where(rev), _ones_where(same)], axis=1)
        dlf = _sum_dot(sum_mask, jnp.concatenate([db, dk_out * k_out], axis=0)) + carry_ref[...]
        dfv = dlf / blk["f"] - dk
        sig, sq = blk["sig"], blk["sq"]
        df_ref[...] = (dfv * (1.0 - lb) * sig * (1.0 - sig)).astype(BF16)
        acc_lb_ref[...] += _colsum(dfv * (1.0 - sig))
        dq_ref[...] = (dqa * sq * (1.0 + q * (1.0 - sq))).astype(BF16)

        @pl.when(step == n_steps - 1)
        def _():
            d0 = acc_lb_ref[...] * lb * (1.0 - lb)
            dlb_ref[0:1, :] = d0
            dlb_ref[1:2, :] = -d0
            dng_ref[...] = acc_ng_ref[...]

    rev_row = pl.BlockSpec((rows, LANES), lambda h, r: (n_steps - 1 - r, h))
    piece = jax.ShapeDtypeStruct((s, hw), BF16)
    return _pcall(
        body, name="hgrn_bwd", grid=(heads, n_steps),
        in_specs=[rev_row, zspec(0), zspec(1), zspec(2), zspec(3), rev_row,
                  pl.BlockSpec((None, per_step, LANES, LANES), lambda h, r: (h, n_steps - 1 - r, 0, 0)),
                  pl.BlockSpec((2, LANES), lambda h, r: (0, h)), pl.BlockSpec((1, LANES), lambda h, r: (0, h))],
        out_specs=[rev_row, rev_row, rev_row, rev_row,
                   pl.BlockSpec((2, LANES), lambda h, r: (0, h)), pl.BlockSpec((1, LANES), lambda h, r: (0, h))],
        out_shape=[piece, piece, piece, piece, jax.ShapeDtypeStruct((2, hw), F32), jax.ShapeDtypeStruct((1, hw), F32)],
        scratch_shapes=[pltpu.VMEM((LANES, LANES), F32), pltpu.VMEM((1, LANES), F32), pltpu.VMEM((1, LANES), F32)]
        + [pltpu.VMEM((rows, LANES), F32)] * 5,
        compiler_params=_params("parallel", "arbitrary"),
    )(dyb, z, z, z, z, o_raw, states, hg_lb, norm_g)


def _adam_update(w, m, v, g):
    m2 = ADAM_B1 * m + (1.0 - ADAM_B1) * g
    v2 = ADAM_B2 * v + (1.0 - ADAM_B2) * (g * g)
    m_hat = m2 * (1.0 / (1.0 - ADAM_B1 ** ADAM_STEP))
    v_hat = v2 * (1.0 / (1.0 - ADAM_B2 ** ADAM_STEP))
    return -ADAM_LR * (m_hat / (jnp.sqrt(v_hat) + ADAM_EPS) + ADAM_WD * w), m2, v2


def _adamw(w, m, v, parts, name, outer=False):
    rows, cols = w.shape
    bc = cols if cols <= 4096 else _pick(cols, 4096)
    rb = _pick_rows(rows, max(8, (384 * 1024) // bc), mult=8)
    if outer and rb % LANES:
        rb = rows

    def body(w_ref, m_ref, v_ref, *refs):
        g_ref, d_ref, mo_ref, vo_ref = refs[-4:]
        if outer:
            cv = refs[0][...]
            g = _dot(cv * _sigmoid(cv), refs[1][...], TN, lax.Precision.HIGHEST)
        else:
            p_ref = refs[0]
            g = p_ref[0].astype(F32)
            for p in range(1, p_ref.shape[0]):
                g = g + p_ref[p].astype(F32)
        g_ref[...] = g
        d_ref[...], mo_ref[...], vo_ref[...] = _adam_update(w_ref[...], m_ref[...], v_ref[...], g)

    blk = pl.BlockSpec((rb, bc), lambda i, j: (i, j))
    out = jax.ShapeDtypeStruct((rows, cols), F32)
    if outer:
        grad_specs = [pl.BlockSpec((N_DEV, rb), lambda i, j: (0, i)), pl.BlockSpec((N_DEV, bc), lambda i, j: (0, j))]
        grad_ins = tuple(parts)
    else:
        grad_specs = [pl.BlockSpec((parts.shape[0], rb, bc), lambda i, j: (0, i, j))]
        grad_ins = (parts,)
    return _pcall(
        body, name=name, grid=(rows // rb, cols // bc), in_specs=[blk, blk, blk] + grad_specs,
        out_specs=[blk] * 4, out_shape=[out] * 4, compiler_params=_params("parallel", "parallel"),
    )(w, m, v, *grad_ins)


SMALL = ("b_ada", "norm1_g", "b_gate", "gmlp_ln_g", "gmlp_ln_b", "gmlp_ws", "gmlp_bs", "hg_lb", "hg_norm_g",
         "norm2_g", "final_norm_g")
BIG = ("w_in", "w_branch_gmlp", "w_branch_hg", "w_out", "w_ffn_in", "w_ffn_out")
WEIGHTS = ("w_ada", "b_ada", "norm1_g", "w_in", "b_gate", "gmlp_ln_g", "gmlp_ln_b", "gmlp_ws", "gmlp_bs", "hg_lb",
           "hg_norm_g", "w_branch_gmlp", "w_branch_hg", "w_out", "norm2_g", "w_ffn_in", "w_ffn_out", "final_norm_g")


def _pack(parts):
    return jnp.concatenate([p.reshape(-1, LANES) for p in parts], axis=0)


def _step(x, c, loss_target, w, m, v):
    s, d = x.shape[1], x.shape[2]
    gw = w["gmlp_ln_g"].shape[-1]
    hw = w["hg_norm_g"].shape[-1]
    x2d, tgt = x[0], loss_target[0]
    mx, my, mc = lax.axis_index("x"), lax.axis_index("y"), lax.axis_index("c")
    chip = 2 * mx + my
    dev = 2 * chip + mc
    q_col = 2 * gw // LANES
    gate_col = (2 * gw + 4 * hw) // d
    place = jnp.stack([chip, mc]).astype(jnp.int32)
    _Order.last = None

    c_all = _all_gather8(c.reshape(-1, LANES), "gather_c").reshape(N_DEV, d)
    gather_in = _WeightGather("in", [w["w_in"][0]], chip, ring=True)
    n_ada = w["w_ada"].shape[-1]
    b_ada_q = lax.dynamic_slice(w["b_ada"], (0, chip * n_ada), (1, n_ada))
    mod_q = _ada_fwd(c_all, w["w_ada"][0], b_ada_q)
    mod_all = _all_gather8(mod_q, "gather_mod")
    mod = lax.dynamic_index_in_dim(mod_all, dev, axis=1, keepdims=False)[::2].reshape(1, 6 * d)
    sh1, sc1, gt1, sh2, sc2, gt2 = [mod[:, i * d:(i + 1) * d] for i in range(6)]

    gather_in.relay()
    gather_mix = _WeightGather("mix", [w[n][0] for n in ("w_branch_gmlp", "w_branch_hg", "w_out")], chip)
    gather_fi = _WeightGather("fi", [w["w_ffn_in"][0]], chip)
    gather_fo = _WeightGather("fo", [w["w_ffn_out"][0]], chip)

    norm1_g, norm2_g, final_g = w["norm1_g"], w["norm2_g"], w["final_norm_g"].reshape(1, d)
    ln_g, ln_b = w["gmlp_ln_g"], w["gmlp_ln_b"]
    ws = w["gmlp_ws"][0]
    groups = ws.shape[0]
    bs_t = jnp.pad(w["gmlp_bs"][0].T, ((0, 0), (0, LANES - groups)))
    hg_lb, hg_ng, b_gate = w["hg_lb"], w["hg_norm_g"], w["b_gate"]

    h1 = _norm_mod_fwd(x2d, norm1_g, sc1, sh1, "norm1_fwd")
    gather_in.pass_on()
    w_in, = gather_in.done()
    z = _matmul(h1, w_in, mode="nn", name="mm_z", out_dtype=F32, b_slots=True, bn=1280)
    gather_mix.pass_on()
    ya = _gmlp_fwd(z, ln_g, ln_b, ws, bs_t)
    yb, o_raw, states = _hgrn_fwd(z, hg_lb, hg_ng, q_col)
    w_bg, w_bh, w_out = gather_mix.done()
    w_out = w_out.reshape(-1, w_out.shape[-1])
    pa = _matmul(ya, w_bg, mode="nn", name="mm_pa", out_dtype=BF16, b_slots=True)
    pb = _matmul(yb, w_bh, mode="nn", name="mm_pb", out_dtype=BF16, b_slots=True)
    y = _gate_fwd(pa, pb, z, b_gate, gate_col)
    gather_fi.pass_on()
    yo = _matmul(y, w_out, mode="nn", name="mm_yo", out_dtype=F32)
    x1, h2 = _norm_mod_fwd(x2d, norm2_g, sc2, sh2, "norm2_fwd", res=yo, gt=gt1)
    w_fi, = gather_fi.done()
    a_act, up_act, hf = _ffn_in(h2, w_fi)
    gather_fo.pass_on()
    w_fo, = gather_fo.done()
    w_fo = w_fo.reshape(-1, w_fo.shape[-1])
    ffn = _matmul(hf, w_fo, mode="nn", name="mm_ffn", out_dtype=F32)
    dx2, dffn, loss_row, d_final_g, d_gt2 = _final_loss(x1, ffn, gt2, final_g, tgt)

    g_fo = _matmul(hf, dffn, mode="tn", name="mm_g_fo", out_dtype=BF16, bm=1408)
    dhf = _matmul(dffn, w_fo, mode="nt", name="mm_dhf", out_dtype=BF16, bn=1408)
    daup = _swiglu_bwd(dhf, a_act, up_act)
    g_fi = _matmul(h2, daup, mode="tn", name="mm_g_fi", out_dtype=BF16, out_slots=True, bn=1408)
    red_ffn = _GradReduce("ffn", [g_fo.reshape(N_CHIPS, -1, g_fo.shape[-1]), g_fi])
    dh2 = _matmul(daup, w_fi, mode="nt", name="mm_dh2", out_dtype=F32, b_slots=True)
    red_ffn.step(place)
    dx1, dyo, d_sh2, d_sc2, d_norm2, d_gt1 = _norm_mod_bwd(dh2, x1, dx2, norm2_g, sc2, "norm2_bwd", branch=yo, gt=gt1)
    g_out = _matmul(y, dyo, mode="tn", name="mm_g_out", out_dtype=BF16)
    dy = _matmul(dyo, w_out, mode="nt", name="mm_dy", out_dtype=F32)
    dpa, dpb, dz_gate, d_b_gate = _gate_bwd(dy, pa, pb, z, b_gate, gate_col)
    g_bg = _matmul(ya, dpa, mode="tn", name="mm_g_bg", out_dtype=BF16, out_slots=True)
    g_bh = _matmul(yb, dpb, mode="tn", name="mm_g_bh", out_dtype=BF16, out_slots=True)
    red_mix = _GradReduce("mix", [g_out.reshape(N_CHIPS, -1, g_out.shape[-1]), g_bg, g_bh])
    dya = _matmul(dpa, w_bg, mode="nt", name="mm_dya", out_dtype=F32, b_slots=True)
    dyb = _matmul(dpb, w_bh, mode="nt", name="mm_dyb", out_dtype=F32, b_slots=True)
    red_mix.step(place)
    dz_uv, d_ws, d_bs_t, d_ln_g, d_ln_b = _gmlp_bwd(dya, z, ln_g, ln_b, ws, bs_t)
    dz_q, dz_f, dz_i, dz_g, d_hg_lb, d_hg_ng = _hgrn_bwd(dyb, z, o_raw, states, hg_lb, hg_ng, q_col)
    dz = jnp.concatenate([dz_uv, dz_q, dz_f, dz_i, dz_g, dz_gate], axis=1)
    red_ffn.join(place)
    g_fo, g_fi = red_ffn.done()
    g_in, ride_fi = _matmul(h1, dz, mode="tn", name="mm_g_in", out_dtype=BF16, out_slots=True, bn=1280,
                            rider=([w["w_ffn_in"][0], m["w_ffn_in"][0], v["w_ffn_in"][0], g_fi], _adam_update, 3))
    red_in = _GradReduce("in", [g_in])
    dh1, ride_fo = _matmul(dz, w_in, mode="nt", name="mm_dh1", out_dtype=F32, b_slots=True,
                           rider=([w["w_ffn_out"][0], m["w_ffn_out"][0], v["w_ffn_out"][0], g_fo], _adam_update, 3))
    red_in.pair(place)
    grad_x, d_sh1, d_sc1, d_norm1 = _norm_mod_bwd(dh1, x2d, dx1, norm1_g, sc1, "norm1_bwd")

    d_mod = jnp.concatenate([d_sh1, d_sc1, d_gt1, d_sh2, d_sc2, d_gt2], axis=1)
    small_part = {"b_ada": d_mod, "norm1_g": d_norm1, "b_gate": d_b_gate, "gmlp_ln_g": d_ln_g, "gmlp_ln_b": d_ln_b,
                  "gmlp_ws": d_ws, "gmlp_bs": d_bs_t[:, :groups].T, "hg_lb": d_hg_lb, "hg_norm_g": d_hg_ng,
                  "norm2_g": d_norm2, "final_norm_g": d_final_g}
    small_all = _all_gather8(_pack([small_part[n] for n in SMALL]), "gather_small")
    red_in.cross(after_last=True)
    d_mod_all = small_all[:, :6 * d // LANES].reshape(N_DEV, 6 * d)
    d_mod_q = lax.dynamic_slice(d_mod_all, (0, chip * n_ada), (N_DEV, n_ada))

    grad, delta, new_m, new_v = {}, {}, {}, {}

    def update(n, parts, outer=False):
        outs = _adamw(w[n][0], m[n][0], v[n][0], parts, "adamw_" + n, outer=outer)
        grad[n], delta[n], new_m[n], new_v[n] = [o[None] for o in outs]

    update("w_ada", (c_all, d_mod_q), outer=True)
    outs = _adamw(_pack([w[n] for n in SMALL]), _pack([m[n] for n in SMALL]), _pack([v[n] for n in SMALL]),
                  small_all, "adamw_small")
    for n, g, ride in (("w_ffn_out", g_fo, ride_fo), ("w_ffn_in", g_fi, ride_fi)):
        grad[n], delta[n], new_m[n], new_v[n] = [o[None] for o in (g, *ride)]
    red_mix.join(place)
    red_in.join(place)
    g_out, g_bg, g_bh = red_mix.done()
    update("w_out", g_out[None])
    update("w_branch_gmlp", g_bg[None])
    update("w_branch_hg", g_bh[None])
    update("w_in", red_in.done()[0][None])
    row = 0
    for n in SMALL:
        cnt = w[n].size // LANES
        for dst, o in zip((grad, delta, new_m, new_v), outs):
            dst[n] = o[row:row + cnt].reshape(w[n].shape)
        row += cnt

    loss = lax.psum(loss_row[0, 0], ("x", "y", "c"))
    return (loss, grad_x[None], *[grad[n] for n in WEIGHTS], *[delta[n] for n in WEIGHTS],
            *[new_m[n] for n in WEIGHTS], *[new_v[n] for n in WEIGHTS])


def kernel(x, c, w_ada, b_ada, norm1_g, w_in, b_gate, gmlp_ln_g, gmlp_ln_b, gmlp_ws, gmlp_bs, hg_lb, hg_norm_g, w_branch_gmlp, w_branch_hg, w_out, norm2_g, w_ffn_in, w_ffn_out, final_norm_g, loss_target, m_w_ada, m_b_ada, m_norm1_g, m_w_in, m_b_gate, m_gmlp_ln_g, m_gmlp_ln_b, m_gmlp_ws, m_gmlp_bs, m_hg_lb, m_hg_norm_g, m_w_branch_gmlp, m_w_branch_hg, m_w_out, m_norm2_g, m_w_ffn_in, m_w_ffn_out, m_final_norm_g, v_w_ada, v_b_ada, v_norm1_g, v_w_in, v_b_gate, v_gmlp_ln_g, v_gmlp_ln_b, v_gmlp_ws, v_gmlp_bs, v_hg_lb, v_hg_norm_g, v_w_branch_gmlp, v_w_branch_hg, v_w_out, v_norm2_g, v_w_ffn_in, v_w_ffn_out, v_final_norm_g):
    w = dict(w_ada=w_ada, b_ada=b_ada, norm1_g=norm1_g, w_in=w_in, b_gate=b_gate, gmlp_ln_g=gmlp_ln_g,
             gmlp_ln_b=gmlp_ln_b, gmlp_ws=gmlp_ws, gmlp_bs=gmlp_bs, hg_lb=hg_lb, hg_norm_g=hg_norm_g,
             w_branch_gmlp=w_branch_gmlp, w_branch_hg=w_branch_hg, w_out=w_out, norm2_g=norm2_g,
             w_ffn_in=w_ffn_in, w_ffn_out=w_ffn_out, final_norm_g=final_norm_g)
    m = dict(w_ada=m_w_ada, b_ada=m_b_ada, norm1_g=m_norm1_g, w_in=m_w_in, b_gate=m_b_gate, gmlp_ln_g=m_gmlp_ln_g,
             gmlp_ln_b=m_gmlp_ln_b, gmlp_ws=m_gmlp_ws, gmlp_bs=m_gmlp_bs, hg_lb=m_hg_lb, hg_norm_g=m_hg_norm_g,
             w_branch_gmlp=m_w_branch_gmlp, w_branch_hg=m_w_branch_hg, w_out=m_w_out, norm2_g=m_norm2_g,
             w_ffn_in=m_w_ffn_in, w_ffn_out=m_w_ffn_out, final_norm_g=m_final_norm_g)
    v = dict(w_ada=v_w_ada, b_ada=v_b_ada, norm1_g=v_norm1_g, w_in=v_w_in, b_gate=v_b_gate, gmlp_ln_g=v_gmlp_ln_g,
             gmlp_ln_b=v_gmlp_ln_b, gmlp_ws=v_gmlp_ws, gmlp_bs=v_gmlp_bs, hg_lb=v_hg_lb, hg_norm_g=v_hg_norm_g,
             w_branch_gmlp=v_w_branch_gmlp, w_branch_hg=v_w_branch_hg, w_out=v_w_out, norm2_g=v_norm2_g,
             w_ffn_in=v_w_ffn_in, w_ffn_out=v_w_ffn_out, final_norm_g=v_final_norm_g)
    return _step(x, c, loss_target, w, m, v)
```

```python
import functools

import jax
import jax.numpy as jnp
from jax import lax
from jax.experimental import pallas as pl
from jax.experimental.pallas import tpu as pltpu

F32 = jnp.float32
BF16 = jnp.bfloat16
EPS = 1e-6
LANES = 128
N_CHIPS = 4
N_DEV = 8
VMEM_LIMIT_BYTES = 56 * 1024 * 1024
HG_CHUNK = 32
HG_ROWS = 256
EXP_CLAMP = 80.0
ADAM_LR, ADAM_B1, ADAM_B2, ADAM_EPS, ADAM_WD, ADAM_STEP = 0.001, 0.9, 0.999, 1e-08, 0.01, 10
MESH = pl.DeviceIdType.MESH

NN = (((1,), (0,)), ((), ()))
NT = (((1,), (1,)), ((), ()))
TN = (((0,), (0,)), ((), ()))


def _dot(a, b, dims=NN, precision=None):
    return lax.dot_general(a, b, dims, precision=precision, preferred_element_type=F32)


def _params(*semantics):
    return pltpu.CompilerParams(dimension_semantics=semantics, vmem_limit_bytes=VMEM_LIMIT_BYTES)


class _Order:
    last = None


def _pcall(body, *, in_specs, out_specs, grid=(), scratch_shapes=(), num_scalar_prefetch=0, **kw):
    def run(*ins):
        deps = () if _Order.last is None else (_Order.last,)
        n_in, n_dep = len(ins), len(deps)

        def wrapped(*refs):
            body(*refs[:n_in], *refs[n_in + n_dep:])

        specs = list(in_specs) + [pl.BlockSpec(memory_space=pl.ANY)] * n_dep
        if num_scalar_prefetch:
            grid_spec = pltpu.PrefetchScalarGridSpec(
                num_scalar_prefetch=num_scalar_prefetch, grid=grid, in_specs=specs, out_specs=out_specs,
                scratch_shapes=scratch_shapes)
            outs = pl.pallas_call(wrapped, grid_spec=grid_spec, **kw)(*ins, *deps)
        else:
            outs = pl.pallas_call(wrapped, grid=grid, in_specs=specs, out_specs=out_specs,
                                  scratch_shapes=scratch_shapes, **kw)(*ins, *deps)
        _Order.last = jax.tree.leaves(outs)[0]
        return outs

    return run


def _pick_rows(dim, pref, mult=16):
    best = None
    for cand in range(mult, min(dim, pref) + 1, mult):
        if dim % cand == 0:
            best = cand
    assert best is not None, (dim, pref)
    return best


def _pick(dim, pref):
    if dim <= pref:
        return dim
    best = None
    for cand in range(LANES, pref + 1, LANES):
        if dim % cand == 0:
            best = cand
    assert best is not None, (dim, pref)
    return best


def _sigmoid(x):
    return 1.0 / (1.0 + jnp.exp(-x))


def _gelu(x):
    c = 0.7978845608028654
    return 0.5 * x * (1.0 + jnp.tanh(c * (x + 0.044715 * x * x * x)))


def _gelu_grad(x):
    c = 0.7978845608028654
    t = jnp.tanh(c * (x + 0.044715 * x * x * x))
    return 0.5 * (1.0 + t) + 0.5 * x * (1.0 - t * t) * c * (1.0 + 3.0 * 0.044715 * x * x)


def _rms(x):
    r = lax.rsqrt(jnp.mean(x * x, axis=-1, keepdims=True) + EPS)
    return x * r, r


def _colsum(x):
    return jnp.sum(x, axis=0, keepdims=True)


def _accumulate(first, ref, val):
    @pl.when(first)
    def _():
        ref[...] = val

    @pl.when(jnp.logical_not(first))
    def _():
        ref[...] += val


def _matmul(a, b, *, mode, name, out_dtype, b_slots=False, out_slots=False, bm=1024, bn=1024, bk=2816):
    pair = a if isinstance(a, tuple) else b if isinstance(b, tuple) else None
    if mode == "nn":
        m, k = a.shape
        n = b.shape[2] * N_CHIPS if b_slots else b.shape[1]
        per = b.shape[2] if b_slots else n
    elif mode == "nt":
        m, k = (a[0].shape[0], 2 * a[0].shape[1]) if pair else a.shape
        n = b.shape[1] if b_slots else b.shape[0]
        per = b.shape[2] if b_slots else k
    else:
        k, m = a.shape
        n = 2 * b[0].shape[1] if pair else b.shape[1]
        per = n // N_CHIPS if out_slots else n
    bm = _pick(m, bm)
    if mode == "nt":
        bn, bk = _pick(n, bn), _pick(per, bk)
    else:
        bn, bk = _pick(per, bn), _pick(k, bk)
    nk = k // bk
    per_blocks = per // (bk if mode == "nt" else bn)
    dims = {"nn": NN, "nt": NT, "tn": TN}[mode]
    half = (nk if mode == "nt" else n // bn) // 2

    def product(a_ref, b_ref, o_ref, acc):
        part = _dot(a_ref[...], b_ref[...], dims)
        if nk == 1:
            o_ref[...] = part.astype(o_ref.dtype)
            return
        acc_ref, kk = acc[0], pl.program_id(2)

        @pl.when(kk == 0)
        def _():
            acc_ref[...] = part

        @pl.when(jnp.logical_and(kk > 0, kk < nk - 1))
        def _():
            acc_ref[...] += part

        @pl.when(kk == nk - 1)
        def _():
            o_ref[...] = (acc_ref[...] + part).astype(o_ref.dtype)

    def body(*refs):
        if not pair:
            return product(refs[0], refs[1], refs[2], refs[3:])
        first = pl.program_id(2 if mode == "nt" else 1) < half
        x1, x2, y, o_ref, acc = refs[0], refs[1], refs[2], refs[3], refs[4:]

        @pl.when(first)
        def _():
            product(*((x1, y) if mode == "nt" else (y, x1)), o_ref, acc)

        @pl.when(jnp.logical_not(first))
        def _():
            product(*((x2, y) if mode == "nt" else (y, x2)), o_ref, acc)

    if mode == "nn":
        a_spec = pl.BlockSpec((bm, bk), lambda i, j, kk: (i, kk))
        if b_slots:
            b_spec = pl.BlockSpec((None, bk, bn), lambda i, j, kk: (j // per_blocks, kk, j % per_blocks))
        else:
            b_spec = pl.BlockSpec((bk, bn), lambda i, j, kk: (kk, j))
    elif mode == "nt":
        a_spec = pl.BlockSpec((bm, bk), lambda i, j, kk: (i, kk))
        if b_slots:
            b_spec = pl.BlockSpec((None, bn, bk), lambda i, j, kk: (kk // per_blocks, j, kk % per_blocks))
        else:
            b_spec = pl.BlockSpec((bn, bk), lambda i, j, kk: (j, kk))
    else:
        a_spec = pl.BlockSpec((bk, bm), lambda i, j, kk: (kk, i))
        b_spec = pl.BlockSpec((bk, bn), lambda i, j, kk: (kk, j))
    if out_slots:
        o_spec = pl.BlockSpec((None, bm, bn), lambda i, j, kk: (j // per_blocks, i, j % per_blocks))
        out_shape = jax.ShapeDtypeStruct((N_CHIPS, m, per), out_dtype)
    else:
        o_spec = pl.BlockSpec((bm, bn), lambda i, j, kk: (i, j))
        out_shape = jax.ShapeDtypeStruct((m, n), out_dtype)
    if not pair:
        ins, in_specs = (a, b), [a_spec, b_spec]
    elif mode == "nt":
        ins = (*a, b)
        in_specs = [pl.BlockSpec((bm, bk), lambda i, j, kk: (i, jnp.minimum(kk, half - 1))),
                    pl.BlockSpec((bm, bk), lambda i, j, kk: (i, jnp.maximum(kk - half, 0))), b_spec]
    else:
        ins = (*b, a)
        in_specs = [pl.BlockSpec((bk, bn), lambda i, j, kk: (kk, jnp.minimum(j, half - 1))),
                    pl.BlockSpec((bk, bn), lambda i, j, kk: (kk, jnp.maximum(j - half, 0))), a_spec]
    return _pcall(
        body, name=name, grid=(m // bm, n // bn, nk), in_specs=in_specs, out_specs=o_spec, out_shape=out_shape,
        scratch_shapes=[pltpu.VMEM((bm, bn), F32)] if nk > 1 else [],
        compiler_params=_params("parallel", "parallel", "arbitrary"),
    )(*ins)


def _place():
    x, y, c = lax.axis_index("x"), lax.axis_index("y"), lax.axis_index("c")
    chips = [(1 - x, y), (x, 1 - y), (1 - x, 1 - y)]
    return x, y, c, chips


def _all_gather8(block, name):
    def body(x_ref, out_ref, send_sems, recv_sems, local_sem):
        x, y, c, chips = _place()
        me, sibling = (x, y, c), (x, y, 1 - c)

        def slot(px, py, pc):
            return out_ref.at[4 * px + 2 * py + pc]

        def copy(k, blk, to, src=None):
            return pltpu.make_async_remote_copy(
                src_ref=slot(*blk) if src is None else src, dst_ref=slot(*blk),
                send_sem=send_sems.at[k], recv_sem=recv_sems.at[k], device_id=to, device_id_type=MESH)

        mine = pltpu.make_async_copy(x_ref, slot(*me), local_sem)
        mine.start()
        first = [copy(0, me, sibling, src=x_ref)]
        first += [copy(1 + j, me, (*chip, c), src=x_ref) for j, chip in enumerate(chips)]
        for cp in first:
            cp.start()
        passed = [copy(4 + j, (*chip, c), sibling) for j, chip in enumerate(chips)]
        for j, chip in enumerate(chips):
            copy(1 + j, (*chip, c), me).wait_recv()
            passed[j].start()
        copy(0, sibling, me).wait_recv()
        for j, chip in enumerate(chips):
            copy(4 + j, (*chip, 1 - c), me).wait_recv()
        for cp in first + passed:
            cp.wait_send()
        mine.wait()

    return _pcall(
        body, name=name, out_shape=jax.ShapeDtypeStruct((N_DEV,) + block.shape, block.dtype),
        in_specs=[pl.BlockSpec(memory_space=pltpu.VMEM)], out_specs=pl.BlockSpec(memory_space=pltpu.VMEM),
        scratch_shapes=[pltpu.SemaphoreType.DMA((7,)), pltpu.SemaphoreType.DMA((7,)), pltpu.SemaphoreType.DMA],
        compiler_params=pltpu.CompilerParams(vmem_limit_bytes=VMEM_LIMIT_BYTES),
    )(block)


HBM_SPEC = pl.BlockSpec(memory_space=pltpu.HBM)
SEM_SPEC = pl.BlockSpec(memory_space=pltpu.SEMAPHORE)
ANY_SPEC = pl.BlockSpec(memory_space=pl.ANY)
EFFECT = pltpu.SideEffectType.DATAFLOW_SIDE_EFFECTING


def _xfer_start(name, bufs, plan, n_copies, after_last=False):
    nb = len(bufs)
    deps = (_Order.last,) if after_last and _Order.last is not None else ()
    nd = len(deps)

    def body(*refs):
        send_sems, recv_sems = refs[nb + nd], refs[nb + nd + 1]
        token = refs[nb + nd + 2 + nb]
        for k, (src, dst, dev) in enumerate(plan(refs[:nb], *_place())):
            pltpu.make_async_remote_copy(src_ref=src, dst_ref=dst, send_sem=send_sems.at[k], recv_sem=recv_sems.at[k],
                                         device_id=dev, device_id_type=MESH).start()
        token[...] = jnp.zeros_like(token)

    outs = pl.pallas_call(
        body, name=name,
        out_shape=(pltpu.SemaphoreType.DMA((n_copies,)), pltpu.SemaphoreType.DMA((n_copies,)),
                   *[pltpu.HBM(b.shape, b.dtype) for b in bufs], jax.ShapeDtypeStruct((8, LANES), F32)),
        in_specs=[HBM_SPEC] * nb + [ANY_SPEC] * nd,
        out_specs=(SEM_SPEC, SEM_SPEC, *[HBM_SPEC] * nb, pl.BlockSpec(memory_space=pltpu.VMEM)),
        input_output_aliases={i: 2 + i for i in range(nb)},
        compiler_params=pltpu.CompilerParams(has_side_effects=EFFECT),
    )(*[pltpu.with_memory_space_constraint(b, pltpu.HBM) for b in bufs], *deps)
    _Order.last = outs[-1]
    return (outs[0], outs[1]), list(outs[2:2 + nb])


def _xfer_wait(name, sems, bufs, plan):
    nb = len(bufs)

    def body(*refs):
        send_sems, recv_sems = refs[nb], refs[nb + 1]
        for k, (src, dst, dev) in enumerate(plan(refs[:nb], *_place())):
            copy = pltpu.make_async_remote_copy(src_ref=src, dst_ref=dst, send_sem=send_sems.at[k],
                                                recv_sem=recv_sems.at[k], device_id=dev, device_id_type=MESH)
            copy.wait_send()
            copy.wait_recv()

    outs = pl.pallas_call(
        body, name=name, out_shape=tuple(pltpu.HBM(b.shape, b.dtype) for b in bufs),
        in_specs=[HBM_SPEC] * nb + [SEM_SPEC, SEM_SPEC, ANY_SPEC], out_specs=tuple([HBM_SPEC] * nb),
        input_output_aliases={i: i for i in range(nb)},
        compiler_params=pltpu.CompilerParams(has_side_effects=EFFECT),
    )(*bufs, *sems, _Order.last)
    _Order.last = outs[0]
    return list(outs)


def _half(ref, c, axis):
    rows = ref.shape[axis] // 2
    return pl.ds(c * rows, rows)


def _plan_weights_ici(n):
    def plan(refs, x, y, c, chips):
        out = []
        for w in range(n):
            region = refs[w].at[2 * x + y, _half(refs[w], c, 1), :]
            out += [(region, region, (*chip, c)) for chip in chips]
        return out
    return plan


def _plan_weights_ring(n):
    def plan(refs, x, y, c, chips):
        out = []
        for w in range(n):
            region = refs[w].at[2 * x + y, _half(refs[w], c, 1), :]
            out += [(region, region, (*chip, c)) for chip in chips[:2]]
        return out
    return plan


def _plan_weights_relay(n):
    def plan(refs, x, y, c, chips):
        out = []
        for w in range(n):
            quarter_rows = refs[w].shape[1] // 4
            upper = refs[w].at[2 * x + (1 - y), pl.ds(2 * c * quarter_rows, quarter_rows), :]
            lower = refs[w].at[2 * (1 - x) + y, pl.ds((2 * c + 1) * quarter_rows, quarter_rows), :]
            out += [(upper, upper, (1 - x, y, c)), (lower, lower, (x, 1 - y, c))]
        return out
    return plan


def _plan_weights_d2d(n):
    def plan(refs, x, y, c, chips):
        out = []
        for w in range(n):
            rows = _half(refs[w], c, 1)
            for chip in chips:
                region = refs[w].at[2 * chip[0] + chip[1], rows, :]
                out.append((region, region, (x, y, 1 - c)))
        return out
    return plan


def _plan_grads_d2d(n):
    def plan(refs, x, y, c, chips):
        return [(refs[w].at[:, _half(refs[w], 1 - c, 1), :], refs[n + w], (x, y, 1 - c)) for w in range(n)]
    return plan


def _plan_grads_ici(n):
    def plan(refs, x, y, c, chips):
        out = []
        for w in range(n):
            out += [(refs[w].at[2 * chip[0] + chip[1]], refs[n + w].at[2 * x + y], (*chip, c)) for chip in chips]
        return out
    return plan


def _plan_final_d2d(n):
    def plan(refs, x, y, c, chips):
        out = []
        for w in range(n):
            region = refs[w].at[_half(refs[w], c, 0), :]
            out.append((region, region, (x, y, 1 - c)))
        return out
    return plan


def _stream_blocks(hr, cols):
    bc = cols if cols <= 4096 else _pick(cols, 4096)
    return _pick_rows(hr, max(16, (768 * 1024) // bc)), bc


def _pre_reduce(g, landed, place, name):
    _, rows, cols = g.shape
    hr = rows // 2
    rb, bc = _stream_blocks(hr, cols)
    nrb = hr // rb

    def body(place_ref, g_ref, l_ref, o_ref):
        o_ref[...] = (g_ref[...].astype(F32) + l_ref[...].astype(F32)).astype(o_ref.dtype)

    return _pcall(
        body, name=name, num_scalar_prefetch=1, grid=(N_CHIPS, nrb, cols // bc),
        in_specs=[pl.BlockSpec((None, rb, bc), lambda j, i, k, p: (j, p[1] * nrb + i, k)),
                  pl.BlockSpec((None, rb, bc), lambda j, i, k, p: (j, i, k))],
        out_specs=pl.BlockSpec((None, rb, bc), lambda j, i, k, p: (j, i, k)),
        out_shape=jax.ShapeDtypeStruct((N_CHIPS, hr, cols), g.dtype),
        compiler_params=_params("parallel", "parallel", "parallel"),
    )(place, g, landed)


def _sum_slots(mine, landed, place, name):
    _, hr, cols = mine.shape
    rb, bc = _stream_blocks(hr, cols)
    rb = _pick_rows(hr, max(16, rb // 2))
    nrb = hr // rb

    def body(place_ref, m_ref, l_ref, o_ref):
        chip = place_ref[0]
        own = m_ref[...].astype(F32)
        total = jnp.where(chip == 0, own, l_ref[0].astype(F32))
        for j in range(1, N_CHIPS):
            total = total + jnp.where(chip == j, own, l_ref[j].astype(F32))
        o_ref[...] = total

    return _pcall(
        body, name=name, num_scalar_prefetch=1, grid=(nrb, cols // bc),
        in_specs=[pl.BlockSpec((None, rb, bc), lambda i, k, p: (p[0], i, k)),
                  pl.BlockSpec((N_CHIPS, rb, bc), lambda i, k, p: (0, i, k))],
        out_specs=pl.BlockSpec((rb, bc), lambda i, k, p: (p[1] * nrb + i, k)),
        out_shape=jax.ShapeDtypeStruct((2 * hr, cols), F32),
        compiler_params=_params("parallel", "parallel"),
    )(place, mine, landed)


class _WeightGather:
    def __init__(self, tag, quarters, chip, ring=False):
        self.tag, self.n, self.ring = tag, len(quarters), ring
        zones = [lax.dynamic_update_slice(lax.empty((N_CHIPS,) + q.shape, BF16), q.astype(BF16)[None], (chip, 0, 0))
                 for q in quarters]
        self.plan = _plan_weights_ring(self.n) if ring else _plan_weights_ici(self.n)
        self.sems, self.bufs = _xfer_start("wici_start_" + tag, zones, self.plan, (2 if ring else 3) * self.n,
                                           after_last=True)

    def relay(self):
        bufs = _xfer_wait("wici_wait_" + self.tag, self.sems, self.bufs, self.plan)
        self.plan = _plan_weights_relay(self.n)
        self.sems, self.bufs = _xfer_start("wrel_start_" + self.tag, bufs, self.plan, 2 * self.n)

    def pass_on(self):
        bufs = _xfer_wait(("wrel_wait_" if self.ring else "wici_wait_") + self.tag, self.sems, self.bufs, self.plan)
        self.plan = _plan_weights_d2d(self.n)
        self.sems, self.bufs = _xfer_start("wd2d_start_" + self.tag, bufs, self.plan, 3 * self.n)

    def done(self):
        return _xfer_wait("wd2d_wait_" + self.tag, self.sems, self.bufs, self.plan)


class _GradReduce:
    def __init__(self, tag, grads):
        self.tag, self.n = tag, len(grads)
        zones = [lax.empty((N_CHIPS, g.shape[1] // 2, g.shape[2]), g.dtype) for g in grads]
        self.plan = _plan_grads_d2d(self.n)
        self.sems, self.bufs = _xfer_start("gd2d_start_" + tag, list(grads) + zones, self.plan, self.n)

    def pair(self, place):
        n = self.n
        bufs = _xfer_wait("gd2d_wait_" + self.tag, self.sems, self.bufs, self.plan)
        self.halves = [_pre_reduce(bufs[w], bufs[n + w], place, f"pre_reduce_{self.tag}{w}") for w in range(n)]

    def cross(self, after_last=False):
        zones = [lax.empty(h.shape, h.dtype) for h in self.halves]
        self.plan = _plan_grads_ici(self.n)
        self.sems, self.bufs = _xfer_start("gici_start_" + self.tag, self.halves + zones, self.plan, 3 * self.n,
                                           after_last=after_last)

    def step(self, place):
        self.pair(place)
        self.cross()

    def join(self, place):
        n = self.n
        bufs = _xfer_wait("gici_wait_" + self.tag, self.sems, self.bufs, self.plan)
        sums = [_sum_slots(bufs[w], bufs[n + w], place, f"sum_slots_{self.tag}{w}") for w in range(n)]
        self.plan = _plan_final_d2d(n)
        self.sems, self.bufs = _xfer_start("gfin_start_" + self.tag, sums, self.plan, n)

    def done(self):
        return _xfer_wait("gfin_wait_" + self.tag, self.sems, self.bufs, self.plan)


def _ada_fwd(c_all, w_q, b_q):
    d, n = w_q.shape
    bn = _pick(n, 512)

    def body(c_ref, w_ref, b_ref, o_ref):
        cv = c_ref[...]
        act = cv * _sigmoid(cv)
        o_ref[...] = _dot(act, w_ref[...], NN, lax.Precision.HIGHEST) + b_ref[...]

    return _pcall(
        body, name="ada_fwd", grid=(n // bn,),
        in_specs=[pl.BlockSpec((N_DEV, d), lambda j: (0, 0)), pl.BlockSpec((d, bn), lambda j: (0, j)),
                  pl.BlockSpec((1, bn), lambda j: (0, j))],
        out_specs=pl.BlockSpec((N_DEV, bn), lambda j: (0, j)),
        out_shape=jax.ShapeDtypeStruct((N_DEV, n), F32), compiler_params=_params("parallel"),
    )(c_all, w_q, b_q)


def _row_spec(rb, width, col=0):
    return pl.BlockSpec((rb, width), lambda i, col=col: (i, col))


def _vec_spec(width, col=0):
    return pl.BlockSpec((1, width), lambda i, col=col: (0, col))


def _norm_mod_fwd(x, g, sc, sh, name, res=None, gt=None):
    s, d = x.shape
    rb = _pick(s, 256)
    has_res = res is not None

    def body(*refs):
        if has_res:
            x_ref, res_ref, gt_ref, g_ref, sc_ref, sh_ref, x1_ref, h_ref = refs
            xv = x_ref[...] + gt_ref[...] * res_ref[...]
            x1_ref[...] = xv
        else:
            x_ref, g_ref, sc_ref, sh_ref, h_ref = refs
            xv = x_ref[...]
        xh, _ = _rms(xv)
        h_ref[...] = (xh * g_ref[...] * (1.0 + sc_ref[...]) + sh_ref[...]).astype(BF16)

    row, vec = _row_spec(rb, d), _vec_spec(d)
    if has_res:
        ins, in_specs = (x, res, gt, g, sc, sh), [row, row, vec, vec, vec, vec]
        out_shape = [jax.ShapeDtypeStruct((s, d), F32), jax.ShapeDtypeStruct((s, d), BF16)]
        out_specs = [row, row]
    else:
        ins, in_specs = (x, g, sc, sh), [row, vec, vec, vec]
        out_shape, out_specs = jax.ShapeDtypeStruct((s, d), BF16), row
    return _pcall(body, name=name, grid=(s // rb,), in_specs=in_specs, out_specs=out_specs,
                          out_shape=out_shape, compiler_params=_params("parallel"))(*ins)


def _final_loss(x1, f, gt2, g_final, target):
    s, d = x1.shape
    rb = _pick(s, 256)

    def body(x1_ref, f_ref, gt_ref, g_ref, t_ref, dx_ref, df_ref, loss_ref, dg_ref, dgt_ref):
        first = pl.program_id(0) == 0
        fv, gt, gv = f_ref[...], gt_ref[...], g_ref[...]
        x2 = x1_ref[...] + gt * fv
        xh, r = _rms(x2)
        err = xh * gv - t_ref[...]
        blk = 0.5 * jnp.sum(jnp.sum(err * err, axis=1, keepdims=True), axis=0, keepdims=True) / d
        dy = err / d
        dxh = dy * gv
        dx = r * (dxh - xh * jnp.mean(dxh * xh, axis=-1, keepdims=True))
        dx_ref[...] = dx
        df_ref[...] = (dx * gt).astype(BF16)
        _accumulate(first, loss_ref, jnp.broadcast_to(blk, (1, LANES)))
        _accumulate(first, dg_ref, _colsum(dy * xh))
        _accumulate(first, dgt_ref, _colsum(dx * fv))

    row, vec = _row_spec(rb, d), _vec_spec(d)
    return _pcall(
        body, name="final_loss", grid=(s // rb,), in_specs=[row, row, vec, vec, row],
        out_specs=[row, row, _vec_spec(LANES), vec, vec],
        out_shape=[jax.ShapeDtypeStruct((s, d), F32), jax.ShapeDtypeStruct((s, d), BF16),
                   jax.ShapeDtypeStruct((1, LANES), F32), jax.ShapeDtypeStruct((1, d), F32),
                   jax.ShapeDtypeStruct((1, d), F32)],
        compiler_params=_params("arbitrary"),
    )(x1, f, gt2, g_final, target)


def _norm_mod_bwd(dh, xin, dres, g, sc, name, branch=None, gt=None):
    s, d = xin.shape
    rb = _pick(s, 256)
    has_branch = branch is not None

    def body(*refs):
        if has_branch:
            dh_ref, x_ref, dres_ref, g_ref, sc_ref, br_ref, gt_ref, dx_ref, dbr_ref, dsh_ref, dsc_ref, dg_ref, dgt_ref = refs
        else:
            dh_ref, x_ref, dres_ref, g_ref, sc_ref, dx_ref, dsh_ref, dsc_ref, dg_ref = refs
        first = pl.program_id(0) == 0
        gv = g_ref[...]
        xh, r = _rms(x_ref[...])
        dhv = dh_ref[...]
        dn = dhv * (1.0 + sc_ref[...])
        dxh = dn * gv
        dx = dres_ref[...] + r * (dxh - xh * jnp.mean(dxh * xh, axis=-1, keepdims=True))
        dx_ref[...] = dx
        _accumulate(first, dsh_ref, _colsum(dhv))
        _accumulate(first, dsc_ref, _colsum(dhv * xh * gv))
        _accumulate(first, dg_ref, _colsum(dn * xh))
        if has_branch:
            dbr_ref[...] = (dx * gt_ref[...]).astype(BF16)
            _accumulate(first, dgt_ref, _colsum(dx * br_ref[...]))

    row, vec = _row_spec(rb, d), _vec_spec(d)
    vec_shape = jax.ShapeDtypeStruct((1, d), F32)
    if has_branch:
        ins, in_specs = (dh, xin, dres, g, sc, branch, gt), [row, row, row, vec, vec, row, vec]
        out_specs = [row, row, vec, vec, vec, vec]
        out_shape = [jax.ShapeDtypeStruct((s, d), F32), jax.ShapeDtypeStruct((s, d), BF16)] + [vec_shape] * 4
    else:
        ins, in_specs = (dh, xin, dres, g, sc), [row, row, row, vec, vec]
        out_specs = [row, vec, vec, vec]
        out_shape = [jax.ShapeDtypeStruct((s, d), F32)] + [vec_shape] * 3
    return _pcall(body, name=name, grid=(s // rb,), in_specs=in_specs, out_specs=out_specs,
                          out_shape=out_shape, compiler_params=_params("arbitrary"))(*ins)


def _gate_fwd(pa, pb, z, b_gate, gate_col):
    s, d = pa.shape
    rb = _pick(s, 256)

    def body(pa_ref, pb_ref, za_ref, zb_ref, ba_ref, bb_ref, y_ref):
        ga = _sigmoid(za_ref[...] + ba_ref[...])
        gb = _sigmoid(zb_ref[...] + bb_ref[...])
        y_ref[...] = (ga * pa_ref[...].astype(F32) + gb * pb_ref[...].astype(F32)).astype(BF16)

    row = _row_spec(rb, d)
    return _pcall(
        body, name="gate_fwd", grid=(s // rb,),
        in_specs=[row, row, _row_spec(rb, d, gate_col), _row_spec(rb, d, gate_col + 1), _vec_spec(d, 0), _vec_spec(d, 1)],
        out_specs=row, out_shape=jax.ShapeDtypeStruct((s, d), BF16), compiler_params=_params("parallel"),
    )(pa, pb, z, z, b_gate, b_gate)


def _out_bwd(dyo, w_out, pa, pb, z, b_gate, gate_col, bm=512, bn=1024):
    s, d = dyo.shape
    bm, bn = _pick(s, bm), _pick(d, bn)
    nj = d // bn

    def body(dyo_ref, w_ref, pa_ref, pb_ref, za_ref, zb_ref, ba_ref, bb_ref,
             dpa_ref, dpb_ref, dza_ref, dzb_ref, dba_ref, dbb_ref):
        first = pl.program_id(1) == 0
        dyv = _dot(dyo_ref[...], w_ref[...], NT)
        ga = _sigmoid(za_ref[...] + ba_ref[...])
        gb = _sigmoid(zb_ref[...] + bb_ref[...])
        dpa_ref[...] = (dyv * ga).astype(BF16)
        dpb_ref[...] = (dyv * gb).astype(BF16)
        dga = dyv * pa_ref[...].astype(F32) * ga * (1.0 - ga)
        dgb = dyv * pb_ref[...].astype(F32) * gb * (1.0 - gb)
        dza_ref[...] = dga.astype(BF16)
        dzb_ref[...] = dgb.astype(BF16)
        _accumulate(first, dba_ref, _colsum(dga))
        _accumulate(first, dbb_ref, _colsum(dgb))

    tile = pl.BlockSpec((bm, bn), lambda j, i: (i, j))
    vec = pl.BlockSpec((1, bn), lambda j, i: (0, j))
    act, vec_shape = jax.ShapeDtypeStruct((s, d), BF16), jax.ShapeDtypeStruct((1, d), F32)
    return _pcall(
        body, name="out_bwd", grid=(nj, s // bm),
        in_specs=[pl.BlockSpec((bm, d), lambda j, i: (i, 0)), pl.BlockSpec((bn, d), lambda j, i: (j, 0)), tile, tile,
                  pl.BlockSpec((bm, bn), lambda j, i: (i, gate_col * nj + j)),
                  pl.BlockSpec((bm, bn), lambda j, i: (i, (gate_col + 1) * nj + j)),
                  vec, pl.BlockSpec((1, bn), lambda j, i: (0, nj + j))],
        out_specs=[tile, tile, tile, tile, vec, vec], out_shape=[act, act, act, act, vec_shape, vec_shape],
        compiler_params=_params("parallel", "arbitrary"),
    )(dyo, w_out, pa, pb, z, z, b_gate, b_gate)


def _ffn_in(h, w_fi, bm=512, bn=1408):
    s, d = h.shape
    per = w_fi.shape[2]
    ff = 2 * per
    bm, bn = _pick(s, bm), _pick(per, bn)
    per_blocks = per // bn

    def body(h_ref, wa_ref, wu_ref, a_ref, u_ref, hf_ref):
        hv = h_ref[...]
        a = _dot(hv, wa_ref[...])
        up = _dot(hv, wu_ref[...])
        a_ref[...] = a.astype(BF16)
        u_ref[...] = up.astype(BF16)
        hf_ref[...] = (a * _sigmoid(a) * up).astype(BF16)

    out = pl.BlockSpec((bm, bn), lambda j, i: (i, j))
    shape = jax.ShapeDtypeStruct((s, ff), BF16)
    return _pcall(
        body, name="ffn_in", grid=(ff // bn, s // bm),
        in_specs=[pl.BlockSpec((bm, d), lambda j, i: (i, 0)),
                  pl.BlockSpec((None, d, bn), lambda j, i: (j // per_blocks, 0, j % per_blocks)),
                  pl.BlockSpec((None, d, bn), lambda j, i: (2 + j // per_blocks, 0, j % per_blocks))],
        out_specs=[out, out, out], out_shape=[shape, shape, shape],
        compiler_params=_params("parallel", "parallel"),
    )(h, w_fi, w_fi)


def _ffn_out_bwd(dffn, w_fo, a_act, up_act, bm=512, bn=1408):
    s, d = dffn.shape
    ff = w_fo.shape[0]
    bm, bn = _pick(s, bm), _pick(ff, bn)

    def body(d_ref, w_ref, a_ref, u_ref, da_ref, du_ref):
        dhf = _dot(d_ref[...], w_ref[...], NT)
        a = a_ref[...].astype(F32)
        sa = _sigmoid(a)
        da_ref[...] = (dhf * u_ref[...].astype(F32) * sa * (1.0 + a * (1.0 - sa))).astype(BF16)
        du_ref[...] = (dhf * a * sa).astype(BF16)

    tile = pl.BlockSpec((bm, bn), lambda j, i: (i, j))
    shape = jax.ShapeDtypeStruct((s, ff), BF16)
    return _pcall(
        body, name="ffn_out_bwd", grid=(ff // bn, s // bm),
        in_specs=[pl.BlockSpec((bm, d), lambda j, i: (i, 0)), pl.BlockSpec((bn, d), lambda j, i: (j, 0)), tile, tile],
        out_specs=[tile, tile], out_shape=[shape, shape], compiler_params=_params("parallel", "parallel"),
    )(dffn, w_fo, a_act, up_act)


def _tril(n):
    return lax.broadcasted_iota(jnp.int32, (n, n), 0) >= lax.broadcasted_iota(jnp.int32, (n, n), 1)


def _gmlp_norm(v, ln_g, ln_b):
    gv = _gelu(v)
    mu = jnp.mean(gv, axis=-1, keepdims=True)
    cen = gv - mu
    rs = lax.rsqrt(jnp.mean(cen * cen, axis=-1, keepdims=True) + EPS)
    xh = cen * rs
    return xh, rs, xh * ln_g + ln_b


def _gmlp_fwd(z, ln_g, ln_b, ws, bs_t):
    s = z.shape[0]
    gw = ln_g.shape[1]
    groups, chunk, _ = ws.shape

    def body(u_ref, v_ref, lg_ref, lb_ref, ws_ref, bs_ref, ya_ref):
        gu = _gelu(u_ref[...])
        _, _, vn = _gmlp_norm(v_ref[...], lg_ref[...], lb_ref[...])
        mask = _tril(chunk)
        for g in range(groups):
            cols = slice(g * LANES, (g + 1) * LANES)
            wm = jnp.where(mask, ws_ref[g], 0.0).astype(BF16)
            sg = _dot(wm, vn[:, cols].astype(BF16)) + bs_ref[:, g:g + 1]
            ya_ref[:, cols] = (gu[:, cols] * sg).astype(BF16)

    return _pcall(
        body, name="gmlp_fwd", grid=(s // chunk,),
        in_specs=[_row_spec(chunk, gw, 0), _row_spec(chunk, gw, 1), _vec_spec(gw), _vec_spec(gw),
                  pl.BlockSpec((groups, chunk, chunk), lambda i: (0, 0, 0)), pl.BlockSpec((chunk, LANES), lambda i: (0, 0))],
        out_specs=_row_spec(chunk, gw), out_shape=jax.ShapeDtypeStruct((s, gw), BF16),
        compiler_params=_params("parallel"),
    )(z, z, ln_g, ln_b, ws, bs_t)


def _gmlp_bwd(dya, z, ln_g, ln_b, ws, bs_t):
    s = z.shape[0]
    gw = ln_g.shape[1]
    groups, chunk, _ = ws.shape

    def body(dya_ref, u_ref, v_ref, lg_ref, lb_ref, ws_ref, bs_ref, duv_ref, dws_ref, dbs_ref, dlg_ref, dlb_ref, dvn_ref):
        first = pl.program_id(0) == 0
        u, v, lg = u_ref[...], v_ref[...], lg_ref[...]
        gu = _gelu(u)
        xh, rs, vn = _gmlp_norm(v, lg, lb_ref[...])
        dyav = dya_ref[...]
        mask = _tril(chunk)
        lane = lax.broadcasted_iota(jnp.int32, (chunk, LANES), 1)
        dbs = jnp.zeros((chunk, LANES), F32)
        for g in range(groups):
            cols = slice(g * LANES, (g + 1) * LANES)
            wm = jnp.where(mask, ws_ref[g], 0.0).astype(BF16)
            vg = vn[:, cols].astype(BF16)
            sg = _dot(wm, vg) + bs_ref[:, g:g + 1]
            ds = dyav[:, cols] * gu[:, cols]
            duv_ref[:, cols] = (dyav[:, cols] * sg * _gelu_grad(u[:, cols])).astype(BF16)
            dsb = ds.astype(BF16)
            _accumulate(first, dws_ref.at[g], jnp.where(mask, _dot(dsb, vg, NT), 0.0))
            dbs = dbs + jnp.where(lane == g, jnp.sum(ds, axis=-1, keepdims=True), 0.0)
            dvn_ref[:, cols] = _dot(wm, dsb, TN)
        dvn = dvn_ref[...]
        _accumulate(first, dbs_ref, dbs)
        _accumulate(first, dlb_ref, _colsum(dvn))
        _accumulate(first, dlg_ref, _colsum(dvn * xh))
        dxh = dvn * lg
        dgv = rs * (dxh - jnp.mean(dxh, axis=-1, keepdims=True) - xh * jnp.mean(dxh * xh, axis=-1, keepdims=True))
        duv_ref[:, gw:] = (dgv * _gelu_grad(v)).astype(BF16)

    return _pcall(
        body, name="gmlp_bwd", grid=(s // chunk,),
        in_specs=[_row_spec(chunk, gw), _row_spec(chunk, gw, 0), _row_spec(chunk, gw, 1), _vec_spec(gw), _vec_spec(gw),
                  pl.BlockSpec((groups, chunk, chunk), lambda i: (0, 0, 0)), pl.BlockSpec((chunk, LANES), lambda i: (0, 0))],
        out_specs=[_row_spec(chunk, 2 * gw), pl.BlockSpec((groups, chunk, chunk), lambda i: (0, 0, 0)),
                   pl.BlockSpec((chunk, LANES), lambda i: (0, 0)), _vec_spec(gw), _vec_spec(gw)],
        out_shape=[jax.ShapeDtypeStruct((s, 2 * gw), BF16), jax.ShapeDtypeStruct((groups, chunk, chunk), F32),
                   jax.ShapeDtypeStruct((chunk, LANES), F32), jax.ShapeDtypeStruct((1, gw), F32),
                   jax.ShapeDtypeStruct((1, gw), F32)],
        scratch_shapes=[pltpu.VMEM((chunk, gw), F32)],
        compiler_params=_params("arbitrary"),
    )(dya, z, z, ln_g, ln_b, ws, bs_t)


def _lower_bound(lb_ref):
    a0, a1 = lb_ref[0:1, :], lb_ref[1:2, :]
    mx = jnp.maximum(a0, a1)
    e0, e1 = jnp.exp(a0 - mx), jnp.exp(a1 - mx)
    return e0 / (e0 + e1)


def _sum_dot(mask, x):
    hi = x.astype(BF16)
    rest = x - hi.astype(F32)
    mid = rest.astype(BF16)
    low = (rest - mid.astype(F32)).astype(BF16)
    return _dot(mask, hi) + _dot(mask, mid) + _dot(mask, low)


def _ones_where(mask):
    return jnp.where(mask, 1.0, 0.0).astype(BF16)


def _hg_masks(rows, t):
    r = lax.broadcasted_iota(jnp.int32, (rows, rows), 0)
    c = lax.broadcasted_iota(jnp.int32, (rows, rows), 1)
    same = (r // t) == (c // t)
    incl = jnp.logical_and(same, c <= r)
    upto_mid = jnp.logical_and(same, (c % t) <= t // 2)
    rev = jnp.logical_and(same, c >= r)
    return same, incl, upto_mid, rev


def _hg_block(q, fp, lb, masks):
    rows = q.shape[0]
    same, incl, upto_mid, _ = masks
    sig = _sigmoid(fp)
    f = lb + (1.0 - lb) * sig
    k = 1.0 - f
    sq = _sigmoid(q)
    qa = q * sq
    stacked = jnp.concatenate([_ones_where(m) for m in (incl, same, upto_mid)], axis=0)
    sums = _sum_dot(stacked, jnp.log(f))
    b, b_last, b_mid = sums[:rows], sums[rows:2 * rows], sums[2 * rows:]
    e_q = jnp.exp(jnp.minimum(b - b_mid, EXP_CLAMP))
    e_k = jnp.exp(jnp.minimum(b_mid - b, EXP_CLAMP))
    e_in = jnp.exp(b)
    e_out = jnp.exp(b_last - b)
    return dict(sig=sig, f=f, k=k, sq=sq, qa=qa, e_last=jnp.exp(b_last), e_q=e_q, e_k=e_k, e_in=e_in, e_out=e_out,
                q_hat=(qa * e_q).astype(BF16), k_hat=(k * e_k).astype(BF16),
                q_in=(qa * e_in).astype(BF16), k_out=k * e_out)


def _hgrn_fwd(z, hg_lb, norm_g, q_col):
    s = z.shape[0]
    hw = norm_g.shape[1]
    heads = hw // LANES
    t = HG_CHUNK
    rows = min(HG_ROWS, s)
    per_step = rows // t
    per = hw // LANES

    def zspec(which):
        return pl.BlockSpec((rows, LANES), lambda h, r, which=which: (r, q_col + which * per + h))

    def body(q_ref, f_ref, i_ref, g_ref, lb_ref, ng_ref, yb_ref, o_ref, st_out_ref, st_ref, e_last_ref, inter_ref):
        @pl.when(pl.program_id(1) == 0)
        def _():
            st_ref[...] = jnp.zeros_like(st_ref)

        masks = _hg_masks(rows, t)
        blk = _hg_block(q_ref[...], f_ref[...], _lower_bound(lb_ref), masks)
        iv = i_ref[...].astype(BF16)
        q_in, k_out = blk["q_in"], blk["k_out"].astype(BF16)
        e_last_ref[...] = blk["e_last"]
        attn = jnp.where(masks[1], _dot(blk["q_hat"], blk["k_hat"], NT), 0.0).astype(BF16)
        o = _dot(attn, iv)
        grown = [_dot(iv[j * t:(j + 1) * t], k_out[j * t:(j + 1) * t], TN) for j in range(per_step)]
        st = st_ref[...]
        for j in range(per_step):
            st_out_ref[j] = st
            inter_ref[j * t:(j + 1) * t, :] = _dot(q_in[j * t:(j + 1) * t], st.astype(BF16), NT)
            st = st * e_last_ref[j * t:j * t + 1, :] + grown[j]
        st_ref[...] = st
        o = o + inter_ref[...]
        o_ref[...] = o
        og = g_ref[...]
        on, _ = _rms(o)
        yb_ref[...] = (on * ng_ref[...] * (og * _sigmoid(og))).astype(BF16)

    out_row = pl.BlockSpec((rows, LANES), lambda h, r: (r, h))
    return _pcall(
        body, name="hgrn_fwd", grid=(heads, s // rows),
        in_specs=[zspec(0), zspec(1), zspec(2), zspec(3),
                  pl.BlockSpec((2, LANES), lambda h, r: (0, h)), pl.BlockSpec((1, LANES), lambda h, r: (0, h))],
        out_specs=[out_row, out_row, pl.BlockSpec((None, per_step, LANES, LANES), lambda h, r: (h, r, 0, 0))],
        out_shape=[jax.ShapeDtypeStruct((s, hw), BF16), jax.ShapeDtypeStruct((s, hw), F32),
                   jax.ShapeDtypeStruct((heads, s // t, LANES, LANES), F32)],
        scratch_shapes=[pltpu.VMEM((LANES, LANES), F32), pltpu.VMEM((rows, LANES), F32), pltpu.VMEM((rows, LANES), F32)],
        compiler_params=_params("parallel", "arbitrary"),
    )(z, z, z, z, hg_lb, norm_g)


def _hgrn_bwd(dyb, z, o_raw, states, hg_lb, norm_g, q_col):
    s = z.shape[0]
    hw = norm_g.shape[1]
    heads = hw // LANES
    t = HG_CHUNK
    rows = min(HG_ROWS, s)
    per_step = rows // t
    per = hw // LANES
    n_steps = s // rows

    def zspec(which):
        return pl.BlockSpec((rows, LANES), lambda h, r, which=which: (n_steps - 1 - r, q_col + which * per + h))

    def body(dyb_ref, q_ref, f_ref, i_ref, g_ref, o_ref, st_in_ref, lb_ref, ng_ref,
             dq_ref, df_ref, di_ref, dg_ref, dlb_ref, dng_ref, dst_ref, acc_lb_ref, acc_ng_ref,
             e_last_ref, dq_in_ref, dk_out_ref, di_inter_ref, carry_ref):
        step = pl.program_id(1)

        @pl.when(step == 0)
        def _():
            dst_ref[...] = jnp.zeros_like(dst_ref)
            acc_lb_ref[...] = jnp.zeros_like(acc_lb_ref)
            acc_ng_ref[...] = jnp.zeros_like(acc_ng_ref)

        lb = _lower_bound(lb_ref)
        ng = ng_ref[...]
        masks = _hg_masks(rows, t)
        same, incl, _, rev = masks
        q = q_ref[...]
        blk = _hg_block(q, f_ref[...], lb, masks)
        iv = i_ref[...].astype(BF16)
        o, og, dy = o_ref[...], g_ref[...], dyb_ref[...]
        so = _sigmoid(og)
        on, r = _rms(o)
        acc_ng_ref[...] += _colsum(dy * on * (og * so))
        dg_ref[...] = (dy * on * ng * so * (1.0 + og * (1.0 - so))).astype(BF16)
        don = dy * ng * (og * so)
        do = (r * (don - on * jnp.mean(don * on, axis=-1, keepdims=True))).astype(BF16)
        q_hat, k_hat, q_in, k_out = blk["q_hat"], blk["k_hat"], blk["q_in"], blk["k_out"]
        k_out_b = k_out.astype(BF16)
        attn = jnp.where(incl, _dot(q_hat, k_hat, NT), 0.0).astype(BF16)
        d_attn = jnp.where(incl, _dot(do, iv, NT), 0.0).astype(BF16)
        di_intra = _dot(attn, do, TN)
        dq_hat = _dot(d_attn, k_hat)
        dk_hat = _dot(d_attn, q_hat, TN)
        e_last_ref[...] = blk["e_last"]
        grown = [_dot(do[j * t:(j + 1) * t], q_in[j * t:(j + 1) * t], TN) for j in range(per_step)]
        dst = dst_ref[...]
        for j in reversed(range(per_step)):
            rs_ = slice(j * t, (j + 1) * t)
            e_last = e_last_ref[j * t:j * t + 1, :]
            st_prev, dst_b = st_in_ref[j], dst.astype(BF16)
            dq_in_ref[rs_, :] = _dot(do[rs_], st_prev.astype(BF16))
            dk_out_ref[rs_, :] = _dot(iv[rs_], dst_b)
            di_inter_ref[rs_, :] = _dot(k_out_b[rs_], dst_b, NT)
            carry_ref[rs_, :] = jnp.broadcast_to(e_last * _colsum(st_prev * dst), (t, LANES))
            dst = dst * e_last + grown[j]
        dst_ref[...] = dst
        di_ref[...] = (di_intra + di_inter_ref[...]).astype(BF16)
        dk_out = dk_out_ref[...]
        dqa = dq_in_ref[...] * blk["e_in"] + dq_hat * blk["e_q"]
        dk = dk_out * blk["e_out"] + dk_hat * blk["e_k"]
        db = blk["qa"] * dqa - blk["k"] * dk
        sum_mask = jnp.concatenate([_ones_where(rev), _ones_where(same)], axis=1)
        dlf = _sum_dot(sum_mask, jnp.concatenate([db, dk_out * k_out], axis=0)) + carry_ref[...]
        dfv = dlf / blk["f"] - dk
        sig, sq = blk["sig"], blk["sq"]
        df_ref[...] = (dfv * (1.0 - lb) * sig * (1.0 - sig)).astype(BF16)
        acc_lb_ref[...] += _colsum(dfv * (1.0 - sig))
        dq_ref[...] = (dqa * sq * (1.0 + q * (1.0 - sq))).astype(BF16)

        @pl.when(step == n_steps - 1)
        def _():
            d0 = acc_lb_ref[...] * lb * (1.0 - lb)
            dlb_ref[0:1, :] = d0
            dlb_ref[1:2, :] = -d0
            dng_ref[...] = acc_ng_ref[...]

    rev_row = pl.BlockSpec((rows, LANES), lambda h, r: (n_steps - 1 - r, h))
    piece = jax.ShapeDtypeStruct((s, hw), BF16)
    return _pcall(
        body, name="hgrn_bwd", grid=(heads, n_steps),
        in_specs=[rev_row, zspec(0), zspec(1), zspec(2), zspec(3), rev_row,
                  pl.BlockSpec((None, per_step, LANES, LANES), lambda h, r: (h, n_steps - 1 - r, 0, 0)),
                  pl.BlockSpec((2, LANES), lambda h, r: (0, h)), pl.BlockSpec((1, LANES), lambda h, r: (0, h))],
        out_specs=[rev_row, rev_row, rev_row, rev_row,
                   pl.BlockSpec((2, LANES), lambda h, r: (0, h)), pl.BlockSpec((1, LANES), lambda h, r: (0, h))],
        out_shape=[piece, piece, piece, piece, jax.ShapeDtypeStruct((2, hw), F32), jax.ShapeDtypeStruct((1, hw), F32)],
        scratch_shapes=[pltpu.VMEM((LANES, LANES), F32), pltpu.VMEM((1, LANES), F32), pltpu.VMEM((1, LANES), F32)]
        + [pltpu.VMEM((rows, LANES), F32)] * 5,
        compiler_params=_params("parallel", "arbitrary"),
    )(dyb, z, z, z, z, o_raw, states, hg_lb, norm_g)


def _adam_update(w, m, v, g):
    m2 = ADAM_B1 * m + (1.0 - ADAM_B1) * g
    v2 = ADAM_B2 * v + (1.0 - ADAM_B2) * (g * g)
    m_hat = m2 * (1.0 / (1.0 - ADAM_B1 ** ADAM_STEP))
    v_hat = v2 * (1.0 / (1.0 - ADAM_B2 ** ADAM_STEP))
    return -ADAM_LR * (m_hat / (jnp.sqrt(v_hat) + ADAM_EPS) + ADAM_WD * w), m2, v2


def _adamw(w, m, v, parts, name, outer=False):
    rows, cols = w.shape
    bc = cols if cols <= 4096 else _pick(cols, 4096)
    rb = _pick_rows(rows, max(8, (384 * 1024) // bc), mult=8)
    if outer and rb % LANES:
        rb = rows

    def body(w_ref, m_ref, v_ref, *refs):
        g_ref, d_ref, mo_ref, vo_ref = refs[-4:]
        if outer:
            cv = refs[0][...]
            g = _dot(cv * _sigmoid(cv), refs[1][...], TN, lax.Precision.HIGHEST)
        else:
            p_ref = refs[0]
            g = p_ref[0].astype(F32)
            for p in range(1, p_ref.shape[0]):
                g = g + p_ref[p].astype(F32)
        g_ref[...] = g
        d_ref[...], mo_ref[...], vo_ref[...] = _adam_update(w_ref[...], m_ref[...], v_ref[...], g)

    blk = pl.BlockSpec((rb, bc), lambda i, j: (i, j))
    out = jax.ShapeDtypeStruct((rows, cols), F32)
    if outer:
        grad_specs = [pl.BlockSpec((N_DEV, rb), lambda i, j: (0, i)), pl.BlockSpec((N_DEV, bc), lambda i, j: (0, j))]
        grad_ins = tuple(parts)
    else:
        grad_specs = [pl.BlockSpec((parts.shape[0], rb, bc), lambda i, j: (0, i, j))]
        grad_ins = (parts,)
    return _pcall(
        body, name=name, grid=(rows // rb, cols // bc), in_specs=[blk, blk, blk] + grad_specs,
        out_specs=[blk] * 4, out_shape=[out] * 4, compiler_params=_params("parallel", "parallel"),
    )(w, m, v, *grad_ins)


SMALL = ("b_ada", "norm1_g", "b_gate", "gmlp_ln_g", "gmlp_ln_b", "gmlp_ws", "gmlp_bs", "hg_lb", "hg_norm_g",
         "norm2_g", "final_norm_g")
BIG = ("w_in", "w_branch_gmlp", "w_branch_hg", "w_out", "w_ffn_in", "w_ffn_out")
WEIGHTS = ("w_ada", "b_ada", "norm1_g", "w_in", "b_gate", "gmlp_ln_g", "gmlp_ln_b", "gmlp_ws", "gmlp_bs", "hg_lb",
           "hg_norm_g", "w_branch_gmlp", "w_branch_hg", "w_out", "norm2_g", "w_ffn_in", "w_ffn_out", "final_norm_g")


def _pack(parts):
    return jnp.concatenate([p.reshape(-1, LANES) for p in parts], axis=0)


def _step(x, c, loss_target, w, m, v):
    s, d = x.shape[1], x.shape[2]
    gw = w["gmlp_ln_g"].shape[-1]
    hw = w["hg_norm_g"].shape[-1]
    x2d, tgt = x[0], loss_target[0]
    mx, my, mc = lax.axis_index("x"), lax.axis_index("y"), lax.axis_index("c")
    chip = 2 * mx + my
    dev = 2 * chip + mc
    q_col = 2 * gw // LANES
    gate_col = (2 * gw + 4 * hw) // d
    place = jnp.stack([chip, mc]).astype(jnp.int32)
    _Order.last = None

    c_all = _all_gather8(c.reshape(-1, LANES), "gather_c").reshape(N_DEV, d)
    gather_in = _WeightGather("in", [w["w_in"][0]], chip, ring=True)
    n_ada = w["w_ada"].shape[-1]
    b_ada_q = lax.dynamic_slice(w["b_ada"], (0, chip * n_ada), (1, n_ada))
    mod_q = _ada_fwd(c_all, w["w_ada"][0], b_ada_q)
    mod_all = _all_gather8(mod_q, "gather_mod")
    mod = lax.dynamic_index_in_dim(mod_all, dev, axis=1, keepdims=False)[::2].reshape(1, 6 * d)
    sh1, sc1, gt1, sh2, sc2, gt2 = [mod[:, i * d:(i + 1) * d] for i in range(6)]

    gather_in.relay()
    gather_mix = _WeightGather("mix", [w[n][0] for n in ("w_branch_gmlp", "w_branch_hg", "w_out")], chip)
    gather_fi = _WeightGather("fi", [w["w_ffn_in"][0]], chip)
    gather_fo = _WeightGather("fo", [w["w_ffn_out"][0]], chip)

    norm1_g, norm2_g, final_g = w["norm1_g"], w["norm2_g"], w["final_norm_g"].reshape(1, d)
    ln_g, ln_b = w["gmlp_ln_g"], w["gmlp_ln_b"]
    ws = w["gmlp_ws"][0]
    groups = ws.shape[0]
    bs_t = jnp.pad(w["gmlp_bs"][0].T, ((0, 0), (0, LANES - groups)))
    hg_lb, hg_ng, b_gate = w["hg_lb"], w["hg_norm_g"], w["b_gate"]

    h1 = _norm_mod_fwd(x2d, norm1_g, sc1, sh1, "norm1_fwd")
    gather_in.pass_on()
    w_in, = gather_in.done()
    z = _matmul(h1, w_in, mode="nn", name="mm_z", out_dtype=F32, b_slots=True, bn=1280)
    gather_mix.pass_on()
    ya = _gmlp_fwd(z, ln_g, ln_b, ws, bs_t)
    yb, o_raw, states = _hgrn_fwd(z, hg_lb, hg_ng, q_col)
    w_bg, w_bh, w_out = gather_mix.done()
    w_out = w_out.reshape(-1, w_out.shape[-1])
    pa = _matmul(ya, w_bg, mode="nn", name="mm_pa", out_dtype=BF16, b_slots=True)
    pb = _matmul(yb, w_bh, mode="nn", name="mm_pb", out_dtype=BF16, b_slots=True)
    y = _gate_fwd(pa, pb, z, b_gate, gate_col)
    gather_fi.pass_on()
    yo = _matmul(y, w_out, mode="nn", name="mm_yo", out_dtype=F32)
    x1, h2 = _norm_mod_fwd(x2d, norm2_g, sc2, sh2, "norm2_fwd", res=yo, gt=gt1)
    w_fi, = gather_fi.done()
    a_act, up_act, hf = _ffn_in(h2, w_fi)
    gather_fo.pass_on()
    w_fo, = gather_fo.done()
    w_fo = w_fo.reshape(-1, w_fo.shape[-1])
    ffn = _matmul(hf, w_fo, mode="nn", name="mm_ffn", out_dtype=F32)
    dx2, dffn, loss_row, d_final_g, d_gt2 = _final_loss(x1, ffn, gt2, final_g, tgt)

    g_fo = _matmul(hf, dffn, mode="tn", name="mm_g_fo", out_dtype=BF16, bm=1408)
    daup = tuple(_ffn_out_bwd(dffn, w_fo, a_act, up_act))
    g_fi = _matmul(h2, daup, mode="tn", name="mm_g_fi", out_dtype=BF16, out_slots=True, bn=1408)
    red_ffn = _GradReduce("ffn", [g_fo.reshape(N_CHIPS, -1, g_fo.shape[-1]), g_fi])
    dh2 = _matmul(daup, w_fi, mode="nt", name="mm_dh2", out_dtype=F32, b_slots=True)
    red_ffn.step(place)
    dx1, dyo, d_sh2, d_sc2, d_norm2, d_gt1 = _norm_mod_bwd(dh2, x1, dx2, norm2_g, sc2, "norm2_bwd", branch=yo, gt=gt1)
    g_out = _matmul(y, dyo, mode="tn", name="mm_g_out", out_dtype=BF16)
    dpa, dpb, dz_ga, dz_gb, d_b_ga, d_b_gb = _out_bwd(dyo, w_out, pa, pb, z, b_gate, gate_col)
    g_bg = _matmul(ya, dpa, mode="tn", name="mm_g_bg", out_dtype=BF16, out_slots=True)
    g_bh = _matmul(yb, dpb, mode="tn", name="mm_g_bh", out_dtype=BF16, out_slots=True)
    red_mix = _GradReduce("mix", [g_out.reshape(N_CHIPS, -1, g_out.shape[-1]), g_bg, g_bh])
    dya = _matmul(dpa, w_bg, mode="nt", name="mm_dya", out_dtype=F32, b_slots=True)
    dyb = _matmul(dpb, w_bh, mode="nt", name="mm_dyb", out_dtype=F32, b_slots=True)
    red_mix.step(place)
    dz_uv, d_ws, d_bs_t, d_ln_g, d_ln_b = _gmlp_bwd(dya, z, ln_g, ln_b, ws, bs_t)
    dz_q, dz_f, dz_i, dz_g, d_hg_lb, d_hg_ng = _hgrn_bwd(dyb, z, o_raw, states, hg_lb, hg_ng, q_col)
    dz = jnp.concatenate([dz_uv, dz_q, dz_f, dz_i, dz_g, dz_ga, dz_gb], axis=1)
    g_in = _matmul(h1, dz, mode="tn", name="mm_g_in", out_dtype=BF16, out_slots=True, bn=1280)
    red_in = _GradReduce("in", [g_in])
    dh1 = _matmul(dz, w_in, mode="nt", name="mm_dh1", out_dtype=F32, b_slots=True)
    red_in.pair(place)
    grad_x, d_sh1, d_sc1, d_norm1 = _norm_mod_bwd(dh1, x2d, dx1, norm1_g, sc1, "norm1_bwd")

    d_mod = jnp.concatenate([d_sh1, d_sc1, d_gt1, d_sh2, d_sc2, d_gt2], axis=1)
    small_part = {"b_ada": d_mod, "norm1_g": d_norm1, "b_gate": jnp.concatenate([d_b_ga, d_b_gb], axis=1), "gmlp_ln_g": d_ln_g, "gmlp_ln_b": d_ln_b,
                  "gmlp_ws": d_ws, "gmlp_bs": d_bs_t[:, :groups].T, "hg_lb": d_hg_lb, "hg_norm_g": d_hg_ng,
                  "norm2_g": d_norm2, "final_norm_g": d_final_g}
    small_all = _all_gather8(_pack([small_part[n] for n in SMALL]), "gather_small")
    red_in.cross(after_last=True)
    d_mod_all = small_all[:, :6 * d // LANES].reshape(N_DEV, 6 * d)
    d_mod_q = lax.dynamic_slice(d_mod_all, (0, chip * n_ada), (N_DEV, n_ada))

    grad, delta, new_m, new_v = {}, {}, {}, {}

    def update(n, parts, outer=False):
        outs = _adamw(w[n][0], m[n][0], v[n][0], parts, "adamw_" + n, outer=outer)
        grad[n], delta[n], new_m[n], new_v[n] = [o[None] for o in outs]

    update("w_ada", (c_all, d_mod_q), outer=True)
    outs = _adamw(_pack([w[n] for n in SMALL]), _pack([m[n] for n in SMALL]), _pack([v[n] for n in SMALL]),
                  small_all, "adamw_small")
    red_ffn.join(place)
    red_mix.join(place)
    g_fo, g_fi = red_ffn.done()
    update("w_ffn_out", g_fo[None])
    update("w_ffn_in", g_fi[None])
    red_in.join(place)
    g_out, g_bg, g_bh = red_mix.done()
    update("w_out", g_out[None])
    update("w_branch_gmlp", g_bg[None])
    update("w_branch_hg", g_bh[None])
    update("w_in", red_in.done()[0][None])
    row = 0
    for n in SMALL:
        cnt = w[n].size // LANES
        for dst, o in zip((grad, delta, new_m, new_v), outs):
            dst[n] = o[row:row + cnt].reshape(w[n].shape)
        row += cnt

    loss = lax.psum(loss_row[0, 0], ("x", "y", "c"))
    return (loss, grad_x[None], *[grad[n] for n in WEIGHTS], *[delta[n] for n in WEIGHTS],
            *[new_m[n] for n in WEIGHTS], *[new_v[n] for n in WEIGHTS])


def kernel(x, c, w_ada, b_ada, norm1_g, w_in, b_gate, gmlp_ln_g, gmlp_ln_b, gmlp_ws, gmlp_bs, hg_lb, hg_norm_g, w_branch_gmlp, w_branch_hg, w_out, norm2_g, w_ffn_in, w_ffn_out, final_norm_g, loss_target, m_w_ada, m_b_ada, m_norm1_g, m_w_in, m_b_gate, m_gmlp_ln_g, m_gmlp_ln_b, m_gmlp_ws, m_gmlp_bs, m_hg_lb, m_hg_norm_g, m_w_branch_gmlp, m_w_branch_hg, m_w_out, m_norm2_g, m_w_ffn_in, m_w_ffn_out, m_final_norm_g, v_w_ada, v_b_ada, v_norm1_g, v_w_in, v_b_gate, v_gmlp_ln_g, v_gmlp_ln_b, v_gmlp_ws, v_gmlp_bs, v_hg_lb, v_hg_norm_g, v_w_branch_gmlp, v_w_branch_hg, v_w_out, v_norm2_g, v_w_ffn_in, v_w_ffn_out, v_final_norm_g):
    w = dict(w_ada=w_ada, b_ada=b_ada, norm1_g=norm1_g, w_in=w_in, b_gate=b_gate, gmlp_ln_g=gmlp_ln_g,
             gmlp_ln_b=gmlp_ln_b, gmlp_ws=gmlp_ws, gmlp_bs=gmlp_bs, hg_lb=hg_lb, hg_norm_g=hg_norm_g,
             w_branch_gmlp=w_branch_gmlp, w_branch_hg=w_branch_hg, w_out=w_out, norm2_g=norm2_g,
             w_ffn_in=w_ffn_in, w_ffn_out=w_ffn_out, final_norm_g=final_norm_g)
    m = dict(w_ada=m_w_ada, b_ada=m_b_ada, norm1_g=m_norm1_g, w_in=m_w_in, b_gate=m_b_gate, gmlp_ln_g=m_gmlp_ln_g,
             gmlp_ln_b=m_gmlp_ln_b, gmlp_ws=m_gmlp_ws, gmlp_bs=m_gmlp_bs, hg_lb=m_hg_lb, hg_norm_g=m_hg_norm_g,
             w_branch_gmlp=m_w_branch_gmlp, w_branch_hg=m_w_branch_hg, w_out=m_w_out, norm2_g=m_norm2_g,
             w_ffn_in=m_w_ffn_in, w_ffn_out=m_w_ffn_out, final_norm_g=m_final_norm_g)
    v = dict(w_ada=v_w_ada, b_ada=v_b_ada, norm1_g=v_norm1_g, w_in=v_w_in, b_gate=v_b_gate, gmlp_ln_g=v_gmlp_ln_g,
             gmlp_ln_b=v_gmlp_ln_b, gmlp_ws=v_gmlp_ws, gmlp_bs=v_gmlp_bs, hg_lb=v_hg_lb, hg_norm_g=v_hg_norm_g,
             w_branch_gmlp=v_w_branch_gmlp, w_branch_hg=v_w_branch_hg, w_out=v_w_out, norm2_g=v_norm2_g,
             w_ffn_in=v_w_ffn_in, w_ffn_out=v_w_ffn_out, final_norm_g=v_final_norm_g)
    return _step(x, c, loss_target, w, m, v)
```

```python
import functools

import jax
import jax.numpy as jnp
from jax import lax
from jax.experimental import pallas as pl
from jax.experimental.pallas import tpu as pltpu

F32 = jnp.float32
BF16 = jnp.bfloat16
EPS = 1e-6
LANES = 128
N_CHIPS = 4
N_DEV = 8
VMEM_LIMIT_BYTES = 56 * 1024 * 1024
HG_CHUNK = 32
HG_ROWS = 256
HG_HEADS_PER_STEP = 8
EXP_CLAMP = 80.0
ADAM_LR, ADAM_B1, ADAM_B2, ADAM_EPS, ADAM_WD, ADAM_STEP = 0.001, 0.9, 0.999, 1e-08, 0.01, 10
MESH = pl.DeviceIdType.MESH

NN = (((1,), (0,)), ((), ()))
NT = (((1,), (1,)), ((), ()))
TN = (((0,), (0,)), ((), ()))


def _dot(a, b, dims=NN, precision=None):
    return lax.dot_general(a, b, dims, precision=precision, preferred_element_type=F32)


def _params(*semantics):
    return pltpu.CompilerParams(dimension_semantics=semantics, vmem_limit_bytes=VMEM_LIMIT_BYTES)


class _Order:
    last = None


def _pcall(body, *, in_specs, out_specs, grid=(), scratch_shapes=(), num_scalar_prefetch=0, **kw):
    def run(*ins):
        deps = () if _Order.last is None else (_Order.last,)
        n_in, n_dep = len(ins), len(deps)

        def wrapped(*refs):
            body(*refs[:n_in], *refs[n_in + n_dep:])

        specs = list(in_specs) + [pl.BlockSpec(memory_space=pl.ANY)] * n_dep
        if num_scalar_prefetch:
            grid_spec = pltpu.PrefetchScalarGridSpec(
                num_scalar_prefetch=num_scalar_prefetch, grid=grid, in_specs=specs, out_specs=out_specs,
                scratch_shapes=scratch_shapes)
            outs = pl.pallas_call(wrapped, grid_spec=grid_spec, **kw)(*ins, *deps)
        else:
            outs = pl.pallas_call(wrapped, grid=grid, in_specs=specs, out_specs=out_specs,
                                  scratch_shapes=scratch_shapes, **kw)(*ins, *deps)
        _Order.last = jax.tree.leaves(outs)[0]
        return outs

    return run


def _pick_rows(dim, pref, mult=16):
    best = None
    for cand in range(mult, min(dim, pref) + 1, mult):
        if dim % cand == 0:
            best = cand
    assert best is not None, (dim, pref)
    return best


def _pick(dim, pref):
    if dim <= pref:
        return dim
    best = None
    for cand in range(LANES, pref + 1, LANES):
        if dim % cand == 0:
            best = cand
    assert best is not None, (dim, pref)
    return best


def _sigmoid(x):
    return 1.0 / (1.0 + jnp.exp(-x))


def _gelu(x):
    c = 0.7978845608028654
    return 0.5 * x * (1.0 + jnp.tanh(c * (x + 0.044715 * x * x * x)))


def _gelu_grad(x):
    c = 0.7978845608028654
    t = jnp.tanh(c * (x + 0.044715 * x * x * x))
    return 0.5 * (1.0 + t) + 0.5 * x * (1.0 - t * t) * c * (1.0 + 3.0 * 0.044715 * x * x)


def _rms(x):
    r = lax.rsqrt(jnp.mean(x * x, axis=-1, keepdims=True) + EPS)
    return x * r, r


def _colsum(x):
    return jnp.sum(x, axis=0, keepdims=True)


def _accumulate(first, ref, val):
    @pl.when(first)
    def _():
        ref[...] = val

    @pl.when(jnp.logical_not(first))
    def _():
        ref[...] += val


def _matmul(a, b, *, mode, name, out_dtype, b_slots=False, out_slots=False, bm=1024, bn=1024, bk=2816):
    pair = a if isinstance(a, tuple) else b if isinstance(b, tuple) else None
    if mode == "nn":
        m, k = a.shape
        n = b.shape[2] * N_CHIPS if b_slots else b.shape[1]
        per = b.shape[2] if b_slots else n
    elif mode == "nt":
        m, k = (a[0].shape[0], 2 * a[0].shape[1]) if pair else a.shape
        n = b.shape[1] if b_slots else b.shape[0]
        per = b.shape[2] if b_slots else k
    else:
        k, m = a.shape
        n = 2 * b[0].shape[1] if pair else b.shape[1]
        per = n // N_CHIPS if out_slots else n
    bm = _pick(m, bm)
    if mode == "nt":
        bn, bk = _pick(n, bn), _pick(per, bk)
    else:
        bn, bk = _pick(per, bn), _pick(k, bk)
    nk = k // bk
    per_blocks = per // (bk if mode == "nt" else bn)
    dims = {"nn": NN, "nt": NT, "tn": TN}[mode]
    half = (nk if mode == "nt" else n // bn) // 2

    def product(a_ref, b_ref, o_ref, acc):
        part = _dot(a_ref[...], b_ref[...], dims)
        if nk == 1:
            o_ref[...] = part.astype(o_ref.dtype)
            return
        acc_ref, kk = acc[0], pl.program_id(2)

        @pl.when(kk == 0)
        def _():
            acc_ref[...] = part

        @pl.when(jnp.logical_and(kk > 0, kk < nk - 1))
        def _():
            acc_ref[...] += part

        @pl.when(kk == nk - 1)
        def _():
            o_ref[...] = (acc_ref[...] + part).astype(o_ref.dtype)

    def body(*refs):
        if not pair:
            return product(refs[0], refs[1], refs[2], refs[3:])
        first = pl.program_id(2 if mode == "nt" else 1) < half
        x1, x2, y, o_ref, acc = refs[0], refs[1], refs[2], refs[3], refs[4:]

        @pl.when(first)
        def _():
            product(*((x1, y) if mode == "nt" else (y, x1)), o_ref, acc)

        @pl.when(jnp.logical_not(first))
        def _():
            product(*((x2, y) if mode == "nt" else (y, x2)), o_ref, acc)

    if mode == "nn":
        a_spec = pl.BlockSpec((bm, bk), lambda i, j, kk: (i, kk))
        if b_slots:
            b_spec = pl.BlockSpec((None, bk, bn), lambda i, j, kk: (j // per_blocks, kk, j % per_blocks))
        else:
            b_spec = pl.BlockSpec((bk, bn), lambda i, j, kk: (kk, j))
    elif mode == "nt":
        a_spec = pl.BlockSpec((bm, bk), lambda i, j, kk: (i, kk))
        if b_slots:
            b_spec = pl.BlockSpec((None, bn, bk), lambda i, j, kk: (kk // per_blocks, j, kk % per_blocks))
        else:
            b_spec = pl.BlockSpec((bn, bk), lambda i, j, kk: (j, kk))
    else:
        a_spec = pl.BlockSpec((bk, bm), lambda i, j, kk: (kk, i))
        b_spec = pl.BlockSpec((bk, bn), lambda i, j, kk: (kk, j))
    if out_slots:
        o_spec = pl.BlockSpec((None, bm, bn), lambda i, j, kk: (j // per_blocks, i, j % per_blocks))
        out_shape = jax.ShapeDtypeStruct((N_CHIPS, m, per), out_dtype)
    else:
        o_spec = pl.BlockSpec((bm, bn), lambda i, j, kk: (i, j))
        out_shape = jax.ShapeDtypeStruct((m, n), out_dtype)
    if not pair:
        ins, in_specs = (a, b), [a_spec, b_spec]
    elif mode == "nt":
        ins = (*a, b)
        in_specs = [pl.BlockSpec((bm, bk), lambda i, j, kk: (i, jnp.minimum(kk, half - 1))),
                    pl.BlockSpec((bm, bk), lambda i, j, kk: (i, jnp.maximum(kk - half, 0))), b_spec]
    else:
        ins = (*b, a)
        in_specs = [pl.BlockSpec((bk, bn), lambda i, j, kk: (kk, jnp.minimum(j, half - 1))),
                    pl.BlockSpec((bk, bn), lambda i, j, kk: (kk, jnp.maximum(j - half, 0))), a_spec]
    return _pcall(
        body, name=name, grid=(m // bm, n // bn, nk), in_specs=in_specs, out_specs=o_spec, out_shape=out_shape,
        scratch_shapes=[pltpu.VMEM((bm, bn), F32)] if nk > 1 else [],
        compiler_params=_params("parallel", "parallel", "arbitrary"),
    )(*ins)


def _place():
    x, y, c = lax.axis_index("x"), lax.axis_index("y"), lax.axis_index("c")
    chips = [(1 - x, y), (x, 1 - y), (1 - x, 1 - y)]
    return x, y, c, chips


def _all_gather8(block, name):
    def body(x_ref, out_ref, send_sems, recv_sems, local_sem):
        x, y, c, chips = _place()
        me, sibling = (x, y, c), (x, y, 1 - c)

        def slot(px, py, pc):
            return out_ref.at[4 * px + 2 * py + pc]

        def copy(k, blk, to, src=None):
            return pltpu.make_async_remote_copy(
                src_ref=slot(*blk) if src is None else src, dst_ref=slot(*blk),
                send_sem=send_sems.at[k], recv_sem=recv_sems.at[k], device_id=to, device_id_type=MESH)

        mine = pltpu.make_async_copy(x_ref, slot(*me), local_sem)
        mine.start()
        first = [copy(0, me, sibling, src=x_ref)]
        first += [copy(1 + j, me, (*chip, c), src=x_ref) for j, chip in enumerate(chips)]
        for cp in first:
            cp.start()
        passed = [copy(4 + j, (*chip, c), sibling) for j, chip in enumerate(chips)]
        for j, chip in enumerate(chips):
            copy(1 + j, (*chip, c), me).wait_recv()
            passed[j].start()
        copy(0, sibling, me).wait_recv()
        for j, chip in enumerate(chips):
            copy(4 + j, (*chip, 1 - c), me).wait_recv()
        for cp in first + passed:
            cp.wait_send()
        mine.wait()

    return _pcall(
        body, name=name, out_shape=jax.ShapeDtypeStruct((N_DEV,) + block.shape, block.dtype),
        in_specs=[pl.BlockSpec(memory_space=pltpu.VMEM)], out_specs=pl.BlockSpec(memory_space=pltpu.VMEM),
        scratch_shapes=[pltpu.SemaphoreType.DMA((7,)), pltpu.SemaphoreType.DMA((7,)), pltpu.SemaphoreType.DMA],
        compiler_params=pltpu.CompilerParams(vmem_limit_bytes=VMEM_LIMIT_BYTES),
    )(block)


HBM_SPEC = pl.BlockSpec(memory_space=pltpu.HBM)
SEM_SPEC = pl.BlockSpec(memory_space=pltpu.SEMAPHORE)
ANY_SPEC = pl.BlockSpec(memory_space=pl.ANY)
EFFECT = pltpu.SideEffectType.DATAFLOW_SIDE_EFFECTING


def _xfer_start(name, bufs, plan, n_copies, after_last=False):
    nb = len(bufs)
    deps = (_Order.last,) if after_last and _Order.last is not None else ()
    nd = len(deps)

    def body(*refs):
        send_sems, recv_sems = refs[nb + nd], refs[nb + nd + 1]
        token = refs[nb + nd + 2 + nb]
        for k, (src, dst, dev) in enumerate(plan(refs[:nb], *_place())):
            pltpu.make_async_remote_copy(src_ref=src, dst_ref=dst, send_sem=send_sems.at[k], recv_sem=recv_sems.at[k],
                                         device_id=dev, device_id_type=MESH).start()
        token[...] = jnp.zeros_like(token)

    outs = pl.pallas_call(
        body, name=name,
        out_shape=(pltpu.SemaphoreType.DMA((n_copies,)), pltpu.SemaphoreType.DMA((n_copies,)),
                   *[pltpu.HBM(b.shape, b.dtype) for b in bufs], jax.ShapeDtypeStruct((8, LANES), F32)),
        in_specs=[HBM_SPEC] * nb + [ANY_SPEC] * nd,
        out_specs=(SEM_SPEC, SEM_SPEC, *[HBM_SPEC] * nb, pl.BlockSpec(memory_space=pltpu.VMEM)),
        input_output_aliases={i: 2 + i for i in range(nb)},
        compiler_params=pltpu.CompilerParams(has_side_effects=EFFECT),
    )(*[pltpu.with_memory_space_constraint(b, pltpu.HBM) for b in bufs], *deps)
    _Order.last = outs[-1]
    return (outs[0], outs[1]), list(outs[2:2 + nb])


def _xfer_wait(name, sems, bufs, plan):
    nb = len(bufs)

    def body(*refs):
        send_sems, recv_sems = refs[nb], refs[nb + 1]
        for k, (src, dst, dev) in enumerate(plan(refs[:nb], *_place())):
            copy = pltpu.make_async_remote_copy(src_ref=src, dst_ref=dst, send_sem=send_sems.at[k],
                                                recv_sem=recv_sems.at[k], device_id=dev, device_id_type=MESH)
            copy.wait_send()
            copy.wait_recv()

    outs = pl.pallas_call(
        body, name=name, out_shape=tuple(pltpu.HBM(b.shape, b.dtype) for b in bufs),
        in_specs=[HBM_SPEC] * nb + [SEM_SPEC, SEM_SPEC, ANY_SPEC], out_specs=tuple([HBM_SPEC] * nb),
        input_output_aliases={i: i for i in range(nb)},
        compiler_params=pltpu.CompilerParams(has_side_effects=EFFECT),
    )(*bufs, *sems, _Order.last)
    _Order.last = outs[0]
    return list(outs)


def _half(ref, c, axis):
    rows = ref.shape[axis] // 2
    return pl.ds(c * rows, rows)


def _plan_weights_ici(n):
    def plan(refs, x, y, c, chips):
        out = []
        for w in range(n):
            region = refs[w].at[2 * x + y, _half(refs[w], c, 1), :]
            out += [(region, region, (*chip, c)) for chip in chips]
        return out
    return plan


def _plan_weights_ring(n):
    def plan(refs, x, y, c, chips):
        out = []
        for w in range(n):
            region = refs[w].at[2 * x + y, _half(refs[w], c, 1), :]
            out += [(region, region, (*chip, c)) for chip in chips[:2]]
        return out
    return plan


def _plan_weights_relay(n):
    def plan(refs, x, y, c, chips):
        out = []
        for w in range(n):
            quarter_rows = refs[w].shape[1] // 4
            upper = refs[w].at[2 * x + (1 - y), pl.ds(2 * c * quarter_rows, quarter_rows), :]
            lower = refs[w].at[2 * (1 - x) + y, pl.ds((2 * c + 1) * quarter_rows, quarter_rows), :]
            out += [(upper, upper, (1 - x, y, c)), (lower, lower, (x, 1 - y, c))]
        return out
    return plan


def _plan_weights_d2d(n):
    def plan(refs, x, y, c, chips):
        out = []
        for w in range(n):
            rows = _half(refs[w], c, 1)
            for chip in chips:
                region = refs[w].at[2 * chip[0] + chip[1], rows, :]
                out.append((region, region, (x, y, 1 - c)))
        return out
    return plan


def _plan_grads_d2d(n):
    def plan(refs, x, y, c, chips):
        return [(refs[w].at[:, _half(refs[w], 1 - c, 1), :], refs[n + w], (x, y, 1 - c)) for w in range(n)]
    return plan


def _plan_grads_ici(n):
    def plan(refs, x, y, c, chips):
        out = []
        for w in range(n):
            out += [(refs[w].at[2 * chip[0] + chip[1]], refs[n + w].at[2 * x + y], (*chip, c)) for chip in chips]
        return out
    return plan


def _plan_final_d2d(n):
    def plan(refs, x, y, c, chips):
        out = []
        for w in range(n):
            region = refs[w].at[_half(refs[w], c, 0), :]
            out.append((region, region, (x, y, 1 - c)))
        return out
    return plan


def _stream_blocks(hr, cols):
    bc = cols if cols <= 4096 else _pick(cols, 4096)
    return _pick_rows(hr, max(16, (768 * 1024) // bc)), bc


def _pre_reduce(g, landed, place, name):
    _, rows, cols = g.shape
    hr = rows // 2
    rb, bc = _stream_blocks(hr, cols)
    nrb = hr // rb

    def body(place_ref, g_ref, l_ref, o_ref):
        o_ref[...] = (g_ref[...].astype(F32) + l_ref[...].astype(F32)).astype(o_ref.dtype)

    return _pcall(
        body, name=name, num_scalar_prefetch=1, grid=(N_CHIPS, nrb, cols // bc),
        in_specs=[pl.BlockSpec((None, rb, bc), lambda j, i, k, p: (j, p[1] * nrb + i, k)),
                  pl.BlockSpec((None, rb, bc), lambda j, i, k, p: (j, i, k))],
        out_specs=pl.BlockSpec((None, rb, bc), lambda j, i, k, p: (j, i, k)),
        out_shape=jax.ShapeDtypeStruct((N_CHIPS, hr, cols), g.dtype),
        compiler_params=_params("parallel", "parallel", "parallel"),
    )(place, g, landed)


def _sum_slots(mine, landed, place, name):
    _, hr, cols = mine.shape
    rb, bc = _stream_blocks(hr, cols)
    rb = _pick_rows(hr, max(16, rb // 2))
    nrb = hr // rb

    def body(place_ref, m_ref, l_ref, o_ref):
        chip = place_ref[0]
        own = m_ref[...].astype(F32)
        total = jnp.where(chip == 0, own, l_ref[0].astype(F32))
        for j in range(1, N_CHIPS):
            total = total + jnp.where(chip == j, own, l_ref[j].astype(F32))
        o_ref[...] = total

    return _pcall(
        body, name=name, num_scalar_prefetch=1, grid=(nrb, cols // bc),
        in_specs=[pl.BlockSpec((None, rb, bc), lambda i, k, p: (p[0], i, k)),
                  pl.BlockSpec((N_CHIPS, rb, bc), lambda i, k, p: (0, i, k))],
        out_specs=pl.BlockSpec((rb, bc), lambda i, k, p: (p[1] * nrb + i, k)),
        out_shape=jax.ShapeDtypeStruct((2 * hr, cols), F32),
        compiler_params=_params("parallel", "parallel"),
    )(place, mine, landed)


class _WeightGather:
    def __init__(self, tag, quarters, chip, ring=False):
        self.tag, self.n, self.ring = tag, len(quarters), ring
        zones = [lax.dynamic_update_slice(lax.empty((N_CHIPS,) + q.shape, BF16), q.astype(BF16)[None], (chip, 0, 0))
                 for q in quarters]
        self.plan = _plan_weights_ring(self.n) if ring else _plan_weights_ici(self.n)
        self.sems, self.bufs = _xfer_start("wici_start_" + tag, zones, self.plan, (2 if ring else 3) * self.n,
                                           after_last=True)

    def relay(self):
        bufs = _xfer_wait("wici_wait_" + self.tag, self.sems, self.bufs, self.plan)
        self.plan = _plan_weights_relay(self.n)
        self.sems, self.bufs = _xfer_start("wrel_start_" + self.tag, bufs, self.plan, 2 * self.n)

    def pass_on(self):
        bufs = _xfer_wait(("wrel_wait_" if self.ring else "wici_wait_") + self.tag, self.sems, self.bufs, self.plan)
        self.plan = _plan_weights_d2d(self.n)
        self.sems, self.bufs = _xfer_start("wd2d_start_" + self.tag, bufs, self.plan, 3 * self.n)

    def done(self):
        return _xfer_wait("wd2d_wait_" + self.tag, self.sems, self.bufs, self.plan)


class _GradReduce:
    def __init__(self, tag, grads):
        self.tag, self.n = tag, len(grads)
        zones = [lax.empty((N_CHIPS, g.shape[1] // 2, g.shape[2]), g.dtype) for g in grads]
        self.plan = _plan_grads_d2d(self.n)
        self.sems, self.bufs = _xfer_start("gd2d_start_" + tag, list(grads) + zones, self.plan, self.n)

    def pair(self, place):
        n = self.n
        bufs = _xfer_wait("gd2d_wait_" + self.tag, self.sems, self.bufs, self.plan)
        self.halves = [_pre_reduce(bufs[w], bufs[n + w], place, f"pre_reduce_{self.tag}{w}") for w in range(n)]

    def cross(self, after_last=False):
        zones = [lax.empty(h.shape, h.dtype) for h in self.halves]
        self.plan = _plan_grads_ici(self.n)
        self.sems, self.bufs = _xfer_start("gici_start_" + self.tag, self.halves + zones, self.plan, 3 * self.n,
                                           after_last=after_last)

    def step(self, place):
        self.pair(place)
        self.cross()

    def join(self, place):
        n = self.n
        bufs = _xfer_wait("gici_wait_" + self.tag, self.sems, self.bufs, self.plan)
        sums = [_sum_slots(bufs[w], bufs[n + w], place, f"sum_slots_{self.tag}{w}") for w in range(n)]
        self.plan = _plan_final_d2d(n)
        self.sems, self.bufs = _xfer_start("gfin_start_" + self.tag, sums, self.plan, n)

    def done(self):
        return _xfer_wait("gfin_wait_" + self.tag, self.sems, self.bufs, self.plan)


def _ada_fwd(c_all, w_q, b_q):
    d, n = w_q.shape
    bn = _pick(n, 512)

    def body(c_ref, w_ref, b_ref, o_ref):
        cv = c_ref[...]
        act = cv * _sigmoid(cv)
        o_ref[...] = _dot(act, w_ref[...], NN, lax.Precision.HIGHEST) + b_ref[...]

    return _pcall(
        body, name="ada_fwd", grid=(n // bn,),
        in_specs=[pl.BlockSpec((N_DEV, d), lambda j: (0, 0)), pl.BlockSpec((d, bn), lambda j: (0, j)),
                  pl.BlockSpec((1, bn), lambda j: (0, j))],
        out_specs=pl.BlockSpec((N_DEV, bn), lambda j: (0, j)),
        out_shape=jax.ShapeDtypeStruct((N_DEV, n), F32), compiler_params=_params("parallel"),
    )(c_all, w_q, b_q)


def _row_spec(rb, width, col=0):
    return pl.BlockSpec((rb, width), lambda i, col=col: (i, col))


def _vec_spec(width, col=0):
    return pl.BlockSpec((1, width), lambda i, col=col: (0, col))


def _norm_mod_fwd(x, g, sc, sh, name, res=None, gt=None):
    s, d = x.shape
    rb = _pick(s, 256)
    has_res = res is not None

    def body(*refs):
        if has_res:
            x_ref, res_ref, gt_ref, g_ref, sc_ref, sh_ref, x1_ref, h_ref = refs
            xv = x_ref[...] + gt_ref[...] * res_ref[...]
            x1_ref[...] = xv
        else:
            x_ref, g_ref, sc_ref, sh_ref, h_ref = refs
            xv = x_ref[...]
        xh, _ = _rms(xv)
        h_ref[...] = (xh * g_ref[...] * (1.0 + sc_ref[...]) + sh_ref[...]).astype(BF16)

    row, vec = _row_spec(rb, d), _vec_spec(d)
    if has_res:
        ins, in_specs = (x, res, gt, g, sc, sh), [row, row, vec, vec, vec, vec]
        out_shape = [jax.ShapeDtypeStruct((s, d), F32), jax.ShapeDtypeStruct((s, d), BF16)]
        out_specs = [row, row]
    else:
        ins, in_specs = (x, g, sc, sh), [row, vec, vec, vec]
        out_shape, out_specs = jax.ShapeDtypeStruct((s, d), BF16), row
    return _pcall(body, name=name, grid=(s // rb,), in_specs=in_specs, out_specs=out_specs,
                          out_shape=out_shape, compiler_params=_params("parallel"))(*ins)


def _final_loss(x1, f, gt2, g_final, target):
    s, d = x1.shape
    rb = _pick(s, 256)

    def body(x1_ref, f_ref, gt_ref, g_ref, t_ref, dx_ref, df_ref, loss_ref, dg_ref, dgt_ref):
        first = pl.program_id(0) == 0
        fv, gt, gv = f_ref[...], gt_ref[...], g_ref[...]
        x2 = x1_ref[...] + gt * fv
        xh, r = _rms(x2)
        err = xh * gv - t_ref[...]
        blk = 0.5 * jnp.sum(jnp.sum(err * err, axis=1, keepdims=True), axis=0, keepdims=True) / d
        dy = err / d
        dxh = dy * gv
        dx = r * (dxh - xh * jnp.mean(dxh * xh, axis=-1, keepdims=True))
        dx_ref[...] = dx
        df_ref[...] = (dx * gt).astype(BF16)
        _accumulate(first, loss_ref, jnp.broadcast_to(blk, (1, LANES)))
        _accumulate(first, dg_ref, _colsum(dy * xh))
        _accumulate(first, dgt_ref, _colsum(dx * fv))

    row, vec = _row_spec(rb, d), _vec_spec(d)
    return _pcall(
        body, name="final_loss", grid=(s // rb,), in_specs=[row, row, vec, vec, row],
        out_specs=[row, row, _vec_spec(LANES), vec, vec],
        out_shape=[jax.ShapeDtypeStruct((s, d), F32), jax.ShapeDtypeStruct((s, d), BF16),
                   jax.ShapeDtypeStruct((1, LANES), F32), jax.ShapeDtypeStruct((1, d), F32),
                   jax.ShapeDtypeStruct((1, d), F32)],
        compiler_params=_params("arbitrary"),
    )(x1, f, gt2, g_final, target)


def _norm_mod_bwd(dh, xin, dres, g, sc, name, branch=None, gt=None):
    s, d = xin.shape
    rb = _pick(s, 256)
    has_branch = branch is not None

    def body(*refs):
        if has_branch:
            dh_ref, x_ref, dres_ref, g_ref, sc_ref, br_ref, gt_ref, dx_ref, dbr_ref, dsh_ref, dsc_ref, dg_ref, dgt_ref = refs
        else:
            dh_ref, x_ref, dres_ref, g_ref, sc_ref, dx_ref, dsh_ref, dsc_ref, dg_ref = refs
        first = pl.program_id(0) == 0
        gv = g_ref[...]
        xh, r = _rms(x_ref[...])
        dhv = dh_ref[...]
        dn = dhv * (1.0 + sc_ref[...])
        dxh = dn * gv
        dx = dres_ref[...] + r * (dxh - xh * jnp.mean(dxh * xh, axis=-1, keepdims=True))
        dx_ref[...] = dx
        _accumulate(first, dsh_ref, _colsum(dhv))
        _accumulate(first, dsc_ref, _colsum(dhv * xh * gv))
        _accumulate(first, dg_ref, _colsum(dn * xh))
        if has_branch:
            dbr_ref[...] = (dx * gt_ref[...]).astype(BF16)
            _accumulate(first, dgt_ref, _colsum(dx * br_ref[...]))

    row, vec = _row_spec(rb, d), _vec_spec(d)
    vec_shape = jax.ShapeDtypeStruct((1, d), F32)
    if has_branch:
        ins, in_specs = (dh, xin, dres, g, sc, branch, gt), [row, row, row, vec, vec, row, vec]
        out_specs = [row, row, vec, vec, vec, vec]
        out_shape = [jax.ShapeDtypeStruct((s, d), F32), jax.ShapeDtypeStruct((s, d), BF16)] + [vec_shape] * 4
    else:
        ins, in_specs = (dh, xin, dres, g, sc), [row, row, row, vec, vec]
        out_specs = [row, vec, vec, vec]
        out_shape = [jax.ShapeDtypeStruct((s, d), F32)] + [vec_shape] * 3
    return _pcall(body, name=name, grid=(s // rb,), in_specs=in_specs, out_specs=out_specs,
                          out_shape=out_shape, compiler_params=_params("arbitrary"))(*ins)


def _gate_fwd(pa, pb, z, b_gate, gate_col):
    s, d = pa.shape
    rb = _pick(s, 256)

    def body(pa_ref, pb_ref, za_ref, zb_ref, ba_ref, bb_ref, y_ref):
        ga = _sigmoid(za_ref[...] + ba_ref[...])
        gb = _sigmoid(zb_ref[...] + bb_ref[...])
        y_ref[...] = (ga * pa_ref[...].astype(F32) + gb * pb_ref[...].astype(F32)).astype(BF16)

    row = _row_spec(rb, d)
    return _pcall(
        body, name="gate_fwd", grid=(s // rb,),
        in_specs=[row, row, _row_spec(rb, d, gate_col), _row_spec(rb, d, gate_col + 1), _vec_spec(d, 0), _vec_spec(d, 1)],
        out_specs=row, out_shape=jax.ShapeDtypeStruct((s, d), BF16), compiler_params=_params("parallel"),
    )(pa, pb, z, z, b_gate, b_gate)


def _out_bwd(dyo, w_out, pa, pb, z, b_gate, gate_col, bm=512, bn=1024):
    s, d = dyo.shape
    bm, bn = _pick(s, bm), _pick(d, bn)
    nj = d // bn

    def body(dyo_ref, w_ref, pa_ref, pb_ref, za_ref, zb_ref, ba_ref, bb_ref,
             dpa_ref, dpb_ref, dza_ref, dzb_ref, dba_ref, dbb_ref):
        first = pl.program_id(1) == 0
        dyv = _dot(dyo_ref[...], w_ref[...], NT)
        ga = _sigmoid(za_ref[...] + ba_ref[...])
        gb = _sigmoid(zb_ref[...] + bb_ref[...])
        dpa_ref[...] = (dyv * ga).astype(BF16)
        dpb_ref[...] = (dyv * gb).astype(BF16)
        dga = dyv * pa_ref[...].astype(F32) * ga * (1.0 - ga)
        dgb = dyv * pb_ref[...].astype(F32) * gb * (1.0 - gb)
        dza_ref[...] = dga.astype(BF16)
        dzb_ref[...] = dgb.astype(BF16)
        _accumulate(first, dba_ref, _colsum(dga))
        _accumulate(first, dbb_ref, _colsum(dgb))

    tile = pl.BlockSpec((bm, bn), lambda j, i: (i, j))
    vec = pl.BlockSpec((1, bn), lambda j, i: (0, j))
    act, vec_shape = jax.ShapeDtypeStruct((s, d), BF16), jax.ShapeDtypeStruct((1, d), F32)
    return _pcall(
        body, name="out_bwd", grid=(nj, s // bm),
        in_specs=[pl.BlockSpec((bm, d), lambda j, i: (i, 0)), pl.BlockSpec((bn, d), lambda j, i: (j, 0)), tile, tile,
                  pl.BlockSpec((bm, bn), lambda j, i: (i, gate_col * nj + j)),
                  pl.BlockSpec((bm, bn), lambda j, i: (i, (gate_col + 1) * nj + j)),
                  vec, pl.BlockSpec((1, bn), lambda j, i: (0, nj + j))],
        out_specs=[tile, tile, tile, tile, vec, vec], out_shape=[act, act, act, act, vec_shape, vec_shape],
        compiler_params=_params("parallel", "arbitrary"),
    )(dyo, w_out, pa, pb, z, z, b_gate, b_gate)


def _ffn_in(h, w_fi, bm=512, bn=1408):
    s, d = h.shape
    per = w_fi.shape[2]
    ff = 2 * per
    bm, bn = _pick(s, bm), _pick(per, bn)
    per_blocks = per // bn

    def body(h_ref, wa_ref, wu_ref, a_ref, u_ref, hf_ref):
        hv = h_ref[...]
        a = _dot(hv, wa_ref[...])
        up = _dot(hv, wu_ref[...])
        a_ref[...] = a.astype(BF16)
        u_ref[...] = up.astype(BF16)
        hf_ref[...] = (a * _sigmoid(a) * up).astype(BF16)

    out = pl.BlockSpec((bm, bn), lambda j, i: (i, j))
    shape = jax.ShapeDtypeStruct((s, ff), BF16)
    return _pcall(
        body, name="ffn_in", grid=(ff // bn, s // bm),
        in_specs=[pl.BlockSpec((bm, d), lambda j, i: (i, 0)),
                  pl.BlockSpec((None, d, bn), lambda j, i: (j // per_blocks, 0, j % per_blocks)),
                  pl.BlockSpec((None, d, bn), lambda j, i: (2 + j // per_blocks, 0, j % per_blocks))],
        out_specs=[out, out, out], out_shape=[shape, shape, shape],
        compiler_params=_params("parallel", "parallel"),
    )(h, w_fi, w_fi)


def _ffn_out_bwd(dffn, w_fo, a_act, up_act, bm=512, bn=1408):
    s, d = dffn.shape
    ff = w_fo.shape[0]
    bm, bn = _pick(s, bm), _pick(ff, bn)

    def body(d_ref, w_ref, a_ref, u_ref, da_ref, du_ref):
        dhf = _dot(d_ref[...], w_ref[...], NT)
        a = a_ref[...].astype(F32)
        sa = _sigmoid(a)
        da_ref[...] = (dhf * u_ref[...].astype(F32) * sa * (1.0 + a * (1.0 - sa))).astype(BF16)
        du_ref[...] = (dhf * a * sa).astype(BF16)

    tile = pl.BlockSpec((bm, bn), lambda j, i: (i, j))
    shape = jax.ShapeDtypeStruct((s, ff), BF16)
    return _pcall(
        body, name="ffn_out_bwd", grid=(ff // bn, s // bm),
        in_specs=[pl.BlockSpec((bm, d), lambda j, i: (i, 0)), pl.BlockSpec((bn, d), lambda j, i: (j, 0)), tile, tile],
        out_specs=[tile, tile], out_shape=[shape, shape], compiler_params=_params("parallel", "parallel"),
    )(dffn, w_fo, a_act, up_act)


def _tril(n):
    return lax.broadcasted_iota(jnp.int32, (n, n), 0) >= lax.broadcasted_iota(jnp.int32, (n, n), 1)


def _gmlp_norm(v, ln_g, ln_b):
    gv = _gelu(v)
    mu = jnp.mean(gv, axis=-1, keepdims=True)
    cen = gv - mu
    rs = lax.rsqrt(jnp.mean(cen * cen, axis=-1, keepdims=True) + EPS)
    xh = cen * rs
    return xh, rs, xh * ln_g + ln_b


def _gmlp_fwd(z, ln_g, ln_b, ws, bs_t):
    s = z.shape[0]
    gw = ln_g.shape[1]
    groups, chunk, _ = ws.shape

    def body(u_ref, v_ref, lg_ref, lb_ref, ws_ref, bs_ref, ya_ref):
        gu = _gelu(u_ref[...])
        _, _, vn = _gmlp_norm(v_ref[...], lg_ref[...], lb_ref[...])
        mask = _tril(chunk)
        for g in range(groups):
            cols = slice(g * LANES, (g + 1) * LANES)
            wm = jnp.where(mask, ws_ref[g], 0.0).astype(BF16)
            sg = _dot(wm, vn[:, cols].astype(BF16)) + bs_ref[:, g:g + 1]
            ya_ref[:, cols] = (gu[:, cols] * sg).astype(BF16)

    return _pcall(
        body, name="gmlp_fwd", grid=(s // chunk,),
        in_specs=[_row_spec(chunk, gw, 0), _row_spec(chunk, gw, 1), _vec_spec(gw), _vec_spec(gw),
                  pl.BlockSpec((groups, chunk, chunk), lambda i: (0, 0, 0)), pl.BlockSpec((chunk, LANES), lambda i: (0, 0))],
        out_specs=_row_spec(chunk, gw), out_shape=jax.ShapeDtypeStruct((s, gw), BF16),
        compiler_params=_params("parallel"),
    )(z, z, ln_g, ln_b, ws, bs_t)


def _gmlp_bwd(dya, z, ln_g, ln_b, ws, bs_t):
    s = z.shape[0]
    gw = ln_g.shape[1]
    groups, chunk, _ = ws.shape

    def body(dya_ref, u_ref, v_ref, lg_ref, lb_ref, ws_ref, bs_ref, duv_ref, dws_ref, dbs_ref, dlg_ref, dlb_ref, dvn_ref):
        first = pl.program_id(0) == 0
        u, v, lg = u_ref[...], v_ref[...], lg_ref[...]
        gu = _gelu(u)
        xh, rs, vn = _gmlp_norm(v, lg, lb_ref[...])
        dyav = dya_ref[...]
        mask = _tril(chunk)
        lane = lax.broadcasted_iota(jnp.int32, (chunk, LANES), 1)
        dbs = jnp.zeros((chunk, LANES), F32)
        for g in range(groups):
            cols = slice(g * LANES, (g + 1) * LANES)
            wm = jnp.where(mask, ws_ref[g], 0.0).astype(BF16)
            vg = vn[:, cols].astype(BF16)
            sg = _dot(wm, vg) + bs_ref[:, g:g + 1]
            ds = dyav[:, cols] * gu[:, cols]
            duv_ref[:, cols] = (dyav[:, cols] * sg * _gelu_grad(u[:, cols])).astype(BF16)
            dsb = ds.astype(BF16)
            _accumulate(first, dws_ref.at[g], jnp.where(mask, _dot(dsb, vg, NT), 0.0))
            dbs = dbs + jnp.where(lane == g, jnp.sum(ds, axis=-1, keepdims=True), 0.0)
            dvn_ref[:, cols] = _dot(wm, dsb, TN)
        dvn = dvn_ref[...]
        _accumulate(first, dbs_ref, dbs)
        _accumulate(first, dlb_ref, _colsum(dvn))
        _accumulate(first, dlg_ref, _colsum(dvn * xh))
        dxh = dvn * lg
        dgv = rs * (dxh - jnp.mean(dxh, axis=-1, keepdims=True) - xh * jnp.mean(dxh * xh, axis=-1, keepdims=True))
        duv_ref[:, gw:] = (dgv * _gelu_grad(v)).astype(BF16)

    return _pcall(
        body, name="gmlp_bwd", grid=(s // chunk,),
        in_specs=[_row_spec(chunk, gw), _row_spec(chunk, gw, 0), _row_spec(chunk, gw, 1), _vec_spec(gw), _vec_spec(gw),
                  pl.BlockSpec((groups, chunk, chunk), lambda i: (0, 0, 0)), pl.BlockSpec((chunk, LANES), lambda i: (0, 0))],
        out_specs=[_row_spec(chunk, 2 * gw), pl.BlockSpec((groups, chunk, chunk), lambda i: (0, 0, 0)),
                   pl.BlockSpec((chunk, LANES), lambda i: (0, 0)), _vec_spec(gw), _vec_spec(gw)],
        out_shape=[jax.ShapeDtypeStruct((s, 2 * gw), BF16), jax.ShapeDtypeStruct((groups, chunk, chunk), F32),
                   jax.ShapeDtypeStruct((chunk, LANES), F32), jax.ShapeDtypeStruct((1, gw), F32),
                   jax.ShapeDtypeStruct((1, gw), F32)],
        scratch_shapes=[pltpu.VMEM((chunk, gw), F32)],
        compiler_params=_params("arbitrary"),
    )(dya, z, z, ln_g, ln_b, ws, bs_t)


def _lower_bound(lb_ref):
    a0, a1 = lb_ref[0:1, :], lb_ref[1:2, :]
    mx = jnp.maximum(a0, a1)
    e0, e1 = jnp.exp(a0 - mx), jnp.exp(a1 - mx)
    return e0 / (e0 + e1)


def _sum_dot(mask, x):
    hi = x.astype(BF16)
    rest = x - hi.astype(F32)
    mid = rest.astype(BF16)
    low = (rest - mid.astype(F32)).astype(BF16)
    return _dot(mask, hi) + _dot(mask, mid) + _dot(mask, low)


def _ones_where(mask):
    return jnp.where(mask, 1.0, 0.0).astype(BF16)


def _hg_masks(rows, t):
    r = lax.broadcasted_iota(jnp.int32, (rows, rows), 0)
    c = lax.broadcasted_iota(jnp.int32, (rows, rows), 1)
    same = (r // t) == (c // t)
    incl = jnp.logical_and(same, c <= r)
    upto_mid = jnp.logical_and(same, (c % t) <= t // 2)
    rev = jnp.logical_and(same, c >= r)
    return same, incl, upto_mid, rev


def _hg_block(q, fp, lb, masks):
    rows = q.shape[0]
    same, incl, upto_mid, _ = masks
    sig = _sigmoid(fp)
    f = lb + (1.0 - lb) * sig
    k = 1.0 - f
    sq = _sigmoid(q)
    qa = q * sq
    stacked = jnp.concatenate([_ones_where(m) for m in (incl, same, upto_mid)], axis=0)
    sums = _sum_dot(stacked, jnp.log(f))
    b, b_last, b_mid = sums[:rows], sums[rows:2 * rows], sums[2 * rows:]
    e_q = jnp.exp(jnp.minimum(b - b_mid, EXP_CLAMP))
    e_k = jnp.exp(jnp.minimum(b_mid - b, EXP_CLAMP))
    e_in = jnp.exp(b)
    e_out = jnp.exp(b_last - b)
    return dict(sig=sig, f=f, k=k, sq=sq, qa=qa, e_last=jnp.exp(b_last), e_q=e_q, e_k=e_k, e_in=e_in, e_out=e_out,
                q_hat=(qa * e_q).astype(BF16), k_hat=(k * e_k).astype(BF16),
                q_in=(qa * e_in).astype(BF16), k_out=k * e_out)


def _hgrn_fwd(z, hg_lb, norm_g, q_col):
    s = z.shape[0]
    hw = norm_g.shape[1]
    heads = hw // LANES
    t = HG_CHUNK
    rows = min(HG_ROWS, s)
    per_step = rows // t
    hp = min(HG_HEADS_PER_STEP, heads)
    assert heads % hp == 0 and q_col % hp == 0, (heads, q_col)
    wide = hp * LANES

    def zspec(which):
        return pl.BlockSpec((rows, wide), lambda h, r, which=which: (r, (q_col + which * heads) // hp + h))

    def body(q_ref, f_ref, i_ref, g_ref, lb_ref, ng_ref, yb_ref, o_ref, st_out_ref, st_ref, e_last_ref, inter_ref):
        @pl.when(pl.program_id(1) == 0)
        def _():
            st_ref[...] = jnp.zeros_like(st_ref)

        masks = _hg_masks(rows, t)
        for hh in range(hp):
            cols = slice(hh * LANES, (hh + 1) * LANES)
            blk = _hg_block(q_ref[:, cols], f_ref[:, cols], _lower_bound(lb_ref.at[:, cols]), masks)
            iv = i_ref[:, cols].astype(BF16)
            q_in, k_out = blk["q_in"], blk["k_out"].astype(BF16)
            e_last_ref[hh] = blk["e_last"]
            attn = jnp.where(masks[1], _dot(blk["q_hat"], blk["k_hat"], NT), 0.0).astype(BF16)
            o = _dot(attn, iv)
            grown = [_dot(iv[j * t:(j + 1) * t], k_out[j * t:(j + 1) * t], TN) for j in range(per_step)]
            st = st_ref[hh]
            for j in range(per_step):
                st_out_ref[hh, j] = st
                inter_ref[hh, j * t:(j + 1) * t, :] = _dot(q_in[j * t:(j + 1) * t], st.astype(BF16), NT)
                st = st * e_last_ref[hh, j * t:j * t + 1, :] + grown[j]
            st_ref[hh] = st
            o = o + inter_ref[hh]
            o_ref[:, cols] = o
            og = g_ref[:, cols]
            on, _ = _rms(o)
            yb_ref[:, cols] = (on * ng_ref[:, cols] * (og * _sigmoid(og))).astype(BF16)

    out_row = pl.BlockSpec((rows, wide), lambda h, r: (r, h))
    return _pcall(
        body, name="hgrn_fwd", grid=(heads // hp, s // rows),
        in_specs=[zspec(0), zspec(1), zspec(2), zspec(3),
                  pl.BlockSpec((2, wide), lambda h, r: (0, h)), pl.BlockSpec((1, wide), lambda h, r: (0, h))],
        out_specs=[out_row, out_row, pl.BlockSpec((hp, per_step, LANES, LANES), lambda h, r: (h, r, 0, 0))],
        out_shape=[jax.ShapeDtypeStruct((s, hw), BF16), jax.ShapeDtypeStruct((s, hw), F32),
                   jax.ShapeDtypeStruct((heads, s // t, LANES, LANES), F32)],
        scratch_shapes=[pltpu.VMEM((hp, LANES, LANES), F32), pltpu.VMEM((hp, rows, LANES), F32),
                        pltpu.VMEM((hp, rows, LANES), F32)],
        compiler_params=_params("parallel", "arbitrary"),
    )(z, z, z, z, hg_lb, norm_g)


def _hgrn_bwd(dyb, z, o_raw, states, hg_lb, norm_g, q_col):
    s = z.shape[0]
    hw = norm_g.shape[1]
    heads = hw // LANES
    t = HG_CHUNK
    rows = min(HG_ROWS, s)
    per_step = rows // t
    n_steps = s // rows
    hp = min(HG_HEADS_PER_STEP, heads)
    assert heads % hp == 0 and q_col % hp == 0, (heads, q_col)
    wide = hp * LANES

    def zspec(which):
        return pl.BlockSpec((rows, wide), lambda h, r, which=which: (n_steps - 1 - r, (q_col + which * heads) // hp + h))

    def body(dyb_ref, q_ref, f_ref, i_ref, g_ref, o_ref, st_in_ref, lb_ref, ng_ref,
             dq_ref, df_ref, di_ref, dg_ref, dlb_ref, dng_ref, dst_ref, acc_lb_ref, acc_ng_ref,
             e_last_ref, dq_in_ref, dk_out_ref, di_inter_ref, carry_ref):
        step = pl.program_id(1)

        @pl.when(step == 0)
        def _():
            dst_ref[...] = jnp.zeros_like(dst_ref)
            acc_lb_ref[...] = jnp.zeros_like(acc_lb_ref)
            acc_ng_ref[...] = jnp.zeros_like(acc_ng_ref)

        masks = _hg_masks(rows, t)
        same, incl, _, rev = masks
        sum_mask = jnp.concatenate([_ones_where(rev), _ones_where(same)], axis=1)
        for hh in range(hp):
            cols = slice(hh * LANES, (hh + 1) * LANES)
            lb = _lower_bound(lb_ref.at[:, cols])
            ng = ng_ref[:, cols]
            q = q_ref[:, cols]
            blk = _hg_block(q, f_ref[:, cols], lb, masks)
            iv = i_ref[:, cols].astype(BF16)
            o, og, dy = o_ref[:, cols], g_ref[:, cols], dyb_ref[:, cols]
            so = _sigmoid(og)
            on, r = _rms(o)
            acc_ng_ref[:, cols] += _colsum(dy * on * (og * so))
            dg_ref[:, cols] = (dy * on * ng * so * (1.0 + og * (1.0 - so))).astype(BF16)
            don = dy * ng * (og * so)
            do = (r * (don - on * jnp.mean(don * on, axis=-1, keepdims=True))).astype(BF16)
            q_hat, k_hat, q_in, k_out = blk["q_hat"], blk["k_hat"], blk["q_in"], blk["k_out"]
            k_out_b = k_out.astype(BF16)
            attn = jnp.where(incl, _dot(q_hat, k_hat, NT), 0.0).astype(BF16)
            d_attn = jnp.where(incl, _dot(do, iv, NT), 0.0).astype(BF16)
            di_intra = _dot(attn, do, TN)
            dq_hat = _dot(d_attn, k_hat)
            dk_hat = _dot(d_attn, q_hat, TN)
            e_last_ref[hh] = blk["e_last"]
            grown = [_dot(do[j * t:(j + 1) * t], q_in[j * t:(j + 1) * t], TN) for j in range(per_step)]
            dst = dst_ref[hh]
            for j in reversed(range(per_step)):
                rs_ = slice(j * t, (j + 1) * t)
                e_last = e_last_ref[hh, j * t:j * t + 1, :]
                st_prev, dst_b = st_in_ref[hh, j], dst.astype(BF16)
                dq_in_ref[hh, rs_, :] = _dot(do[rs_], st_prev.astype(BF16))
                dk_out_ref[hh, rs_, :] = _dot(iv[rs_], dst_b)
                di_inter_ref[hh, rs_, :] = _dot(k_out_b[rs_], dst_b, NT)
                carry_ref[hh, rs_, :] = jnp.broadcast_to(e_last * _colsum(st_prev * dst), (t, LANES))
                dst = dst * e_last + grown[j]
            dst_ref[hh] = dst
            di_ref[:, cols] = (di_intra + di_inter_ref[hh]).astype(BF16)
            dk_out = dk_out_ref[hh]
            dqa = dq_in_ref[hh] * blk["e_in"] + dq_hat * blk["e_q"]
            dk = dk_out * blk["e_out"] + dk_hat * blk["e_k"]
            db = blk["qa"] * dqa - blk["k"] * dk
            dlf = _sum_dot(sum_mask, jnp.concatenate([db, dk_out * k_out], axis=0)) + carry_ref[hh]
            dfv = dlf / blk["f"] - dk
            sig, sq = blk["sig"], blk["sq"]
            df_ref[:, cols] = (dfv * (1.0 - lb) * sig * (1.0 - sig)).astype(BF16)
            acc_lb_ref[:, cols] += _colsum(dfv * (1.0 - sig))
            dq_ref[:, cols] = (dqa * sq * (1.0 + q * (1.0 - sq))).astype(BF16)

        @pl.when(step == n_steps - 1)
        def _():
            lb = _lower_bound(lb_ref)
            d0 = acc_lb_ref[...] * lb * (1.0 - lb)
            dlb_ref[0:1, :] = d0
            dlb_ref[1:2, :] = -d0
            dng_ref[...] = acc_ng_ref[...]

    rev_row = pl.BlockSpec((rows, wide), lambda h, r: (n_steps - 1 - r, h))
    piece = jax.ShapeDtypeStruct((s, hw), BF16)
    return _pcall(
        body, name="hgrn_bwd", grid=(heads // hp, n_steps),
        in_specs=[rev_row, zspec(0), zspec(1), zspec(2), zspec(3), rev_row,
                  pl.BlockSpec((hp, per_step, LANES, LANES), lambda h, r: (h, n_steps - 1 - r, 0, 0)),
                  pl.BlockSpec((2, wide), lambda h, r: (0, h)), pl.BlockSpec((1, wide), lambda h, r: (0, h))],
        out_specs=[rev_row, rev_row, rev_row, rev_row,
                   pl.BlockSpec((2, wide), lambda h, r: (0, h)), pl.BlockSpec((1, wide), lambda h, r: (0, h))],
        out_shape=[piece, piece, piece, piece, jax.ShapeDtypeStruct((2, hw), F32), jax.ShapeDtypeStruct((1, hw), F32)],
        scratch_shapes=[pltpu.VMEM((hp, LANES, LANES), F32), pltpu.VMEM((1, wide), F32), pltpu.VMEM((1, wide), F32)]
        + [pltpu.VMEM((hp, rows, LANES), F32)] * 5,
        compiler_params=_params("parallel", "arbitrary"),
    )(dyb, z, z, z, z, o_raw, states, hg_lb, norm_g)


def _adam_update(w, m, v, g):
    m2 = ADAM_B1 * m + (1.0 - ADAM_B1) * g
    v2 = ADAM_B2 * v + (1.0 - ADAM_B2) * (g * g)
    m_hat = m2 * (1.0 / (1.0 - ADAM_B1 ** ADAM_STEP))
    v_hat = v2 * (1.0 / (1.0 - ADAM_B2 ** ADAM_STEP))
    return -ADAM_LR * (m_hat / (jnp.sqrt(v_hat) + ADAM_EPS) + ADAM_WD * w), m2, v2


def _adamw(w, m, v, parts, name, outer=False):
    rows, cols = w.shape
    bc = cols if cols <= 4096 else _pick(cols, 4096)
    rb = _pick_rows(rows, max(8, (384 * 1024) // bc), mult=8)
    if outer and rb % LANES:
        rb = rows

    def body(w_ref, m_ref, v_ref, *refs):
        g_ref, d_ref, mo_ref, vo_ref = refs[-4:]
        if outer:
            cv = refs[0][...]
            g = _dot(cv * _sigmoid(cv), refs[1][...], TN, lax.Precision.HIGHEST)
        else:
            p_ref = refs[0]
            g = p_ref[0].astype(F32)
            for p in range(1, p_ref.shape[0]):
                g = g + p_ref[p].astype(F32)
        g_ref[...] = g
        d_ref[...], mo_ref[...], vo_ref[...] = _adam_update(w_ref[...], m_ref[...], v_ref[...], g)

    blk = pl.BlockSpec((rb, bc), lambda i, j: (i, j))
    out = jax.ShapeDtypeStruct((rows, cols), F32)
    if outer:
        grad_specs = [pl.BlockSpec((N_DEV, rb), lambda i, j: (0, i)), pl.BlockSpec((N_DEV, bc), lambda i, j: (0, j))]
        grad_ins = tuple(parts)
    else:
        grad_specs = [pl.BlockSpec((parts.shape[0], rb, bc), lambda i, j: (0, i, j))]
        grad_ins = (parts,)
    return _pcall(
        body, name=name, grid=(rows // rb, cols // bc), in_specs=[blk, blk, blk] + grad_specs,
        out_specs=[blk] * 4, out_shape=[out] * 4, compiler_params=_params("parallel", "parallel"),
    )(w, m, v, *grad_ins)


SMALL = ("b_ada", "norm1_g", "b_gate", "gmlp_ln_g", "gmlp_ln_b", "gmlp_ws", "gmlp_bs", "hg_lb", "hg_norm_g",
         "norm2_g", "final_norm_g")
BIG = ("w_in", "w_branch_gmlp", "w_branch_hg", "w_out", "w_ffn_in", "w_ffn_out")
WEIGHTS = ("w_ada", "b_ada", "norm1_g", "w_in", "b_gate", "gmlp_ln_g", "gmlp_ln_b", "gmlp_ws", "gmlp_bs", "hg_lb",
           "hg_norm_g", "w_branch_gmlp", "w_branch_hg", "w_out", "norm2_g", "w_ffn_in", "w_ffn_out", "final_norm_g")


def _pack(parts):
    return jnp.concatenate([p.reshape(-1, LANES) for p in parts], axis=0)


def _step(x, c, loss_target, w, m, v):
    s, d = x.shape[1], x.shape[2]
    gw = w["gmlp_ln_g"].shape[-1]
    hw = w["hg_norm_g"].shape[-1]
    x2d, tgt = x[0], loss_target[0]
    mx, my, mc = lax.axis_index("x"), lax.axis_index("y"), lax.axis_index("c")
    chip = 2 * mx + my
    dev = 2 * chip + mc
    q_col = 2 * gw // LANES
    gate_col = (2 * gw + 4 * hw) // d
    place = jnp.stack([chip, mc]).astype(jnp.int32)
    _Order.last = None

    c_all = _all_gather8(c.reshape(-1, LANES), "gather_c").reshape(N_DEV, d)
    gather_in = _WeightGather("in", [w["w_in"][0]], chip, ring=True)
    n_ada = w["w_ada"].shape[-1]
    b_ada_q = lax.dynamic_slice(w["b_ada"], (0, chip * n_ada), (1, n_ada))
    mod_q = _ada_fwd(c_all, w["w_ada"][0], b_ada_q)
    mod_all = _all_gather8(mod_q, "gather_mod")
    mod = lax.dynamic_index_in_dim(mod_all, dev, axis=1, keepdims=False)[::2].reshape(1, 6 * d)
    sh1, sc1, gt1, sh2, sc2, gt2 = [mod[:, i * d:(i + 1) * d] for i in range(6)]

    gather_in.relay()
    gather_mix = _WeightGather("mix", [w[n][0] for n in ("w_branch_gmlp", "w_branch_hg", "w_out")], chip)
    gather_fi = _WeightGather("fi", [w["w_ffn_in"][0]], chip)
    gather_fo = _WeightGather("fo", [w["w_ffn_out"][0]], chip)

    norm1_g, norm2_g, final_g = w["norm1_g"], w["norm2_g"], w["final_norm_g"].reshape(1, d)
    ln_g, ln_b = w["gmlp_ln_g"], w["gmlp_ln_b"]
    ws = w["gmlp_ws"][0]
    groups = ws.shape[0]
    bs_t = jnp.pad(w["gmlp_bs"][0].T, ((0, 0), (0, LANES - groups)))
    hg_lb, hg_ng, b_gate = w["hg_lb"], w["hg_norm_g"], w["b_gate"]

    h1 = _norm_mod_fwd(x2d, norm1_g, sc1, sh1, "norm1_fwd")
    gather_in.pass_on()
    w_in, = gather_in.done()
    z = _matmul(h1, w_in, mode="nn", name="mm_z", out_dtype=F32, b_slots=True, bn=1280)
    gather_mix.pass_on()
    ya = _gmlp_fwd(z, ln_g, ln_b, ws, bs_t)
    yb, o_raw, states = _hgrn_fwd(z, hg_lb, hg_ng, q_col)
    w_bg, w_bh, w_out = gather_mix.done()
    w_out = w_out.reshape(-1, w_out.shape[-1])
    pa = _matmul(ya, w_bg, mode="nn", name="mm_pa", out_dtype=BF16, b_slots=True)
    pb = _matmul(yb, w_bh, mode="nn", name="mm_pb", out_dtype=BF16, b_slots=True)
    y = _gate_fwd(pa, pb, z, b_gate, gate_col)
    gather_fi.pass_on()
    yo = _matmul(y, w_out, mode="nn", name="mm_yo", out_dtype=F32)
    x1, h2 = _norm_mod_fwd(x2d, norm2_g, sc2, sh2, "norm2_fwd", res=yo, gt=gt1)
    w_fi, = gather_fi.done()
    a_act, up_act, hf = _ffn_in(h2, w_fi)
    gather_fo.pass_on()
    w_fo, = gather_fo.done()
    w_fo = w_fo.reshape(-1, w_fo.shape[-1])
    ffn = _matmul(hf, w_fo, mode="nn", name="mm_ffn", out_dtype=F32)
    dx2, dffn, loss_row, d_final_g, d_gt2 = _final_loss(x1, ffn, gt2, final_g, tgt)

    g_fo = _matmul(hf, dffn, mode="tn", name="mm_g_fo", out_dtype=BF16, bm=1408)
    daup = tuple(_ffn_out_bwd(dffn, w_fo, a_act, up_act))
    g_fi = _matmul(h2, daup, mode="tn", name="mm_g_fi", out_dtype=BF16, out_slots=True, bn=1408)
    red_ffn = _GradReduce("ffn", [g_fo.reshape(N_CHIPS, -1, g_fo.shape[-1]), g_fi])
    dh2 = _matmul(daup, w_fi, mode="nt", name="mm_dh2", out_dtype=F32, b_slots=True)
    red_ffn.step(place)
    dx1, dyo, d_sh2, d_sc2, d_norm2, d_gt1 = _norm_mod_bwd(dh2, x1, dx2, norm2_g, sc2, "norm2_bwd", branch=yo, gt=gt1)
    g_out = _matmul(y, dyo, mode="tn", name="mm_g_out", out_dtype=BF16)
    dpa, dpb, dz_ga, dz_gb, d_b_ga, d_b_gb = _out_bwd(dyo, w_out, pa, pb, z, b_gate, gate_col)
    g_bg = _matmul(ya, dpa, mode="tn", name="mm_g_bg", out_dtype=BF16, out_slots=True)
    g_bh = _matmul(yb, dpb, mode="tn", name="mm_g_bh", out_dtype=BF16, out_slots=True)
    red_mix = _GradReduce("mix", [g_out.reshape(N_CHIPS, -1, g_out.shape[-1]), g_bg, g_bh])
    dya = _matmul(dpa, w_bg, mode="nt", name="mm_dya", out_dtype=F32, b_slots=True)
    dyb = _matmul(dpb, w_bh, mode="nt", name="mm_dyb", out_dtype=F32, b_slots=True)
    red_mix.step(place)
    dz_uv, d_ws, d_bs_t, d_ln_g, d_ln_b = _gmlp_bwd(dya, z, ln_g, ln_b, ws, bs_t)
    dz_q, dz_f, dz_i, dz_g, d_hg_lb, d_hg_ng = _hgrn_bwd(dyb, z, o_raw, states, hg_lb, hg_ng, q_col)
    dz = jnp.concatenate([dz_uv, dz_q, dz_f, dz_i, dz_g, dz_ga, dz_gb], axis=1)
    g_in = _matmul(h1, dz, mode="tn", name="mm_g_in", out_dtype=BF16, out_slots=True, bn=1280)
    red_in = _GradReduce("in", [g_in])
    dh1 = _matmul(dz, w_in, mode="nt", name="mm_dh1", out_dtype=F32, b_slots=True)
    red_in.pair(place)
    grad_x, d_sh1, d_sc1, d_norm1 = _norm_mod_bwd(dh1, x2d, dx1, norm1_g, sc1, "norm1_bwd")

    d_mod = jnp.concatenate([d_sh1, d_sc1, d_gt1, d_sh2, d_sc2, d_gt2], axis=1)
    small_part = {"b_ada": d_mod, "norm1_g": d_norm1, "b_gate": jnp.concatenate([d_b_ga, d_b_gb], axis=1), "gmlp_ln_g": d_ln_g, "gmlp_ln_b": d_ln_b,
                  "gmlp_ws": d_ws, "gmlp_bs": d_bs_t[:, :groups].T, "hg_lb": d_hg_lb, "hg_norm_g": d_hg_ng,
                  "norm2_g": d_norm2, "final_norm_g": d_final_g}
    small_all = _all_gather8(_pack([small_part[n] for n in SMALL]), "gather_small")
    red_in.cross(after_last=True)
    d_mod_all = small_all[:, :6 * d // LANES].reshape(N_DEV, 6 * d)
    d_mod_q = lax.dynamic_slice(d_mod_all, (0, chip * n_ada), (N_DEV, n_ada))

    grad, delta, new_m, new_v = {}, {}, {}, {}

    def update(n, parts, outer=False):
        outs = _adamw(w[n][0], m[n][0], v[n][0], parts, "adamw_" + n, outer=outer)
        grad[n], delta[n], new_m[n], new_v[n] = [o[None] for o in outs]

    update("w_ada", (c_all, d_mod_q), outer=True)
    outs = _adamw(_pack([w[n] for n in SMALL]), _pack([m[n] for n in SMALL]), _pack([v[n] for n in SMALL]),
                  small_all, "adamw_small")
    red_ffn.join(place)
    red_mix.join(place)
    g_fo, g_fi = red_ffn.done()
    update("w_ffn_out", g_fo[None])
    update("w_ffn_in", g_fi[None])
    red_in.join(place)
    g_out, g_bg, g_bh = red_mix.done()
    update("w_out", g_out[None])
    update("w_branch_gmlp", g_bg[None])
    update("w_branch_hg", g_bh[None])
    update("w_in", red_in.done()[0][None])
    row = 0
    for n in SMALL:
        cnt = w[n].size // LANES
        for dst, o in zip((grad, delta, new_m, new_v), outs):
            dst[n] = o[row:row + cnt].reshape(w[n].shape)
        row += cnt

    loss = lax.psum(loss_row[0, 0], ("x", "y", "c"))
    return (loss, grad_x[None], *[grad[n] for n in WEIGHTS], *[delta[n] for n in WEIGHTS],
            *[new_m[n] for n in WEIGHTS], *[new_v[n] for n in WEIGHTS])


def kernel(x, c, w_ada, b_ada, norm1_g, w_in, b_gate, gmlp_ln_g, gmlp_ln_b, gmlp_ws, gmlp_bs, hg_lb, hg_norm_g, w_branch_gmlp, w_branch_hg, w_out, norm2_g, w_ffn_in, w_ffn_out, final_norm_g, loss_target, m_w_ada, m_b_ada, m_norm1_g, m_w_in, m_b_gate, m_gmlp_ln_g, m_gmlp_ln_b, m_gmlp_ws, m_gmlp_bs, m_hg_lb, m_hg_norm_g, m_w_branch_gmlp, m_w_branch_hg, m_w_out, m_norm2_g, m_w_ffn_in, m_w_ffn_out, m_final_norm_g, v_w_ada, v_b_ada, v_norm1_g, v_w_in, v_b_gate, v_gmlp_ln_g, v_gmlp_ln_b, v_gmlp_ws, v_gmlp_bs, v_hg_lb, v_hg_norm_g, v_w_branch_gmlp, v_w_branch_hg, v_w_out, v_norm2_g, v_w_ffn_in, v_w_ffn_out, v_final_norm_g):
    w = dict(w_ada=w_ada, b_ada=b_ada, norm1_g=norm1_g, w_in=w_in, b_gate=b_gate, gmlp_ln_g=gmlp_ln_g,
             gmlp_ln_b=gmlp_ln_b, gmlp_ws=gmlp_ws, gmlp_bs=gmlp_bs, hg_lb=hg_lb, hg_norm_g=hg_norm_g,
             w_branch_gmlp=w_branch_gmlp, w_branch_hg=w_branch_hg, w_out=w_out, norm2_g=norm2_g,
             w_ffn_in=w_ffn_in, w_ffn_out=w_ffn_out, final_norm_g=final_norm_g)
    m = dict(w_ada=m_w_ada, b_ada=m_b_ada, norm1_g=m_norm1_g, w_in=m_w_in, b_gate=m_b_gate, gmlp_ln_g=m_gmlp_ln_g,
             gmlp_ln_b=m_gmlp_ln_b, gmlp_ws=m_gmlp_ws, gmlp_bs=m_gmlp_bs, hg_lb=m_hg_lb, hg_norm_g=m_hg_norm_g,
             w_branch_gmlp=m_w_branch_gmlp, w_branch_hg=m_w_branch_hg, w_out=m_w_out, norm2_g=m_norm2_g,
             w_ffn_in=m_w_ffn_in, w_ffn_out=m_w_ffn_out, final_norm_g=m_final_norm_g)
    v = dict(w_ada=v_w_ada, b_ada=v_b_ada, norm1_g=v_norm1_g, w_in=v_w_in, b_gate=v_b_gate, gmlp_ln_g=v_gmlp_ln_g,
             gmlp_ln_b=v_gmlp_ln_b, gmlp_ws=v_gmlp_ws, gmlp_bs=v_gmlp_bs, hg_lb=v_hg_lb, hg_norm_g=v_hg_norm_g,
             w_branch_gmlp=v_w_branch_gmlp, w_branch_hg=v_w_branch_hg, w_out=v_w_out, norm2_g=v_norm2_g,
             w_ffn_in=v_w_ffn_in, w_ffn_out=v_w_ffn_out, final_norm_g=v_final_norm_g)
    return _step(x, c, loss_target, w, m, v)
```

```python
import functools

import jax
import jax.numpy as jnp
from jax import lax
from jax.experimental import pallas as pl
from jax.experimental.pallas import tpu as pltpu

F32 = jnp.float32
BF16 = jnp.bfloat16
EPS = 1e-6
LANES = 128
N_CHIPS = 4
N_DEV = 8
VMEM_LIMIT_BYTES = 56 * 1024 * 1024
HG_CHUNK = 32
HG_ROWS = 256
HG_HEADS_PER_STEP = 8
EXP_CLAMP = 80.0
ADAM_LR, ADAM_B1, ADAM_B2, ADAM_EPS, ADAM_WD, ADAM_STEP = 0.001, 0.9, 0.999, 1e-08, 0.01, 10
MESH = pl.DeviceIdType.MESH

NN = (((1,), (0,)), ((), ()))
NT = (((1,), (1,)), ((), ()))
TN = (((0,), (0,)), ((), ()))


def _dot(a, b, dims=NN, precision=None):
    return lax.dot_general(a, b, dims, precision=precision, preferred_element_type=F32)


def _params(*semantics):
    return pltpu.CompilerParams(dimension_semantics=semantics, vmem_limit_bytes=VMEM_LIMIT_BYTES)


class _Order:
    last = None


def _pcall(body, *, in_specs, out_specs, grid=(), scratch_shapes=(), num_scalar_prefetch=0, **kw):
    def run(*ins):
        deps = () if _Order.last is None else (_Order.last,)
        n_in, n_dep = len(ins), len(deps)

        def wrapped(*refs):
            body(*refs[:n_in], *refs[n_in + n_dep:])

        specs = list(in_specs) + [pl.BlockSpec(memory_space=pl.ANY)] * n_dep
        if num_scalar_prefetch:
            grid_spec = pltpu.PrefetchScalarGridSpec(
                num_scalar_prefetch=num_scalar_prefetch, grid=grid, in_specs=specs, out_specs=out_specs,
                scratch_shapes=scratch_shapes)
            outs = pl.pallas_call(wrapped, grid_spec=grid_spec, **kw)(*ins, *deps)
        else:
            outs = pl.pallas_call(wrapped, grid=grid, in_specs=specs, out_specs=out_specs,
                                  scratch_shapes=scratch_shapes, **kw)(*ins, *deps)
        _Order.last = jax.tree.leaves(outs)[0]
        return outs

    return run


def _pick_rows(dim, pref, mult=16):
    best = None
    for cand in range(mult, min(dim, pref) + 1, mult):
        if dim % cand == 0:
            best = cand
    assert best is not None, (dim, pref)
    return best


def _pick(dim, pref):
    if dim <= pref:
        return dim
    best = None
    for cand in range(LANES, pref + 1, LANES):
        if dim % cand == 0:
            best = cand
    assert best is not None, (dim, pref)
    return best


def _sigmoid(x):
    return 1.0 / (1.0 + jnp.exp(-x))


def _gelu(x):
    c = 0.7978845608028654
    return 0.5 * x * (1.0 + jnp.tanh(c * (x + 0.044715 * x * x * x)))


def _gelu_grad(x):
    c = 0.7978845608028654
    t = jnp.tanh(c * (x + 0.044715 * x * x * x))
    return 0.5 * (1.0 + t) + 0.5 * x * (1.0 - t * t) * c * (1.0 + 3.0 * 0.044715 * x * x)


def _rms(x):
    r = lax.rsqrt(jnp.mean(x * x, axis=-1, keepdims=True) + EPS)
    return x * r, r


def _colsum(x):
    return jnp.sum(x, axis=0, keepdims=True)


def _accumulate(first, ref, val):
    @pl.when(first)
    def _():
        ref[...] = val

    @pl.when(jnp.logical_not(first))
    def _():
        ref[...] += val


def _matmul(a, b, *, mode, name, out_dtype, b_slots=False, out_slots=False, bm=1024, bn=1024, bk=2816):
    pair = a if isinstance(a, tuple) else b if isinstance(b, tuple) else None
    if mode == "nn":
        m, k = a.shape
        n = b.shape[2] * N_CHIPS if b_slots else b.shape[1]
        per = b.shape[2] if b_slots else n
    elif mode == "nt":
        m, k = (a[0].shape[0], 2 * a[0].shape[1]) if pair else a.shape
        n = b.shape[1] if b_slots else b.shape[0]
        per = b.shape[2] if b_slots else k
    else:
        k, m = a.shape
        n = 2 * b[0].shape[1] if pair else b.shape[1]
        per = n // N_CHIPS if out_slots else n
    bm = _pick(m, bm)
    if mode == "nt":
        bn, bk = _pick(n, bn), _pick(per, bk)
    else:
        bn, bk = _pick(per, bn), _pick(k, bk)
    nk = k // bk
    per_blocks = per // (bk if mode == "nt" else bn)
    dims = {"nn": NN, "nt": NT, "tn": TN}[mode]
    half = (nk if mode == "nt" else n // bn) // 2

    def product(a_ref, b_ref, o_ref, acc):
        part = _dot(a_ref[...], b_ref[...], dims)
        if nk == 1:
            o_ref[...] = part.astype(o_ref.dtype)
            return
        acc_ref, kk = acc[0], pl.program_id(2)

        @pl.when(kk == 0)
        def _():
            acc_ref[...] = part

        @pl.when(jnp.logical_and(kk > 0, kk < nk - 1))
        def _():
            acc_ref[...] += part

        @pl.when(kk == nk - 1)
        def _():
            o_ref[...] = (acc_ref[...] + part).astype(o_ref.dtype)

    def body(*refs):
        if not pair:
            return product(refs[0], refs[1], refs[2], refs[3:])
        first = pl.program_id(2 if mode == "nt" else 1) < half
        x1, x2, y, o_ref, acc = refs[0], refs[1], refs[2], refs[3], refs[4:]

        @pl.when(first)
        def _():
            product(*((x1, y) if mode == "nt" else (y, x1)), o_ref, acc)

        @pl.when(jnp.logical_not(first))
        def _():
            product(*((x2, y) if mode == "nt" else (y, x2)), o_ref, acc)

    if mode == "nn":
        a_spec = pl.BlockSpec((bm, bk), lambda i, j, kk: (i, kk))
        if b_slots:
            b_spec = pl.BlockSpec((None, bk, bn), lambda i, j, kk: (j // per_blocks, kk, j % per_blocks))
        else:
            b_spec = pl.BlockSpec((bk, bn), lambda i, j, kk: (kk, j))
    elif mode == "nt":
        a_spec = pl.BlockSpec((bm, bk), lambda i, j, kk: (i, kk))
        if b_slots:
            b_spec = pl.BlockSpec((None, bn, bk), lambda i, j, kk: (kk // per_blocks, j, kk % per_blocks))
        else:
            b_spec = pl.BlockSpec((bn, bk), lambda i, j, kk: (j, kk))
    else:
        a_spec = pl.BlockSpec((bk, bm), lambda i, j, kk: (kk, i))
        b_spec = pl.BlockSpec((bk, bn), lambda i, j, kk: (kk, j))
    if out_slots:
        o_spec = pl.BlockSpec((None, bm, bn), lambda i, j, kk: (j // per_blocks, i, j % per_blocks))
        out_shape = jax.ShapeDtypeStruct((N_CHIPS, m, per), out_dtype)
    else:
        o_spec = pl.BlockSpec((bm, bn), lambda i, j, kk: (i, j))
        out_shape = jax.ShapeDtypeStruct((m, n), out_dtype)
    if not pair:
        ins, in_specs = (a, b), [a_spec, b_spec]
    elif mode == "nt":
        ins = (*a, b)
        in_specs = [pl.BlockSpec((bm, bk), lambda i, j, kk: (i, jnp.minimum(kk, half - 1))),
                    pl.BlockSpec((bm, bk), lambda i, j, kk: (i, jnp.maximum(kk - half, 0))), b_spec]
    else:
        ins = (*b, a)
        in_specs = [pl.BlockSpec((bk, bn), lambda i, j, kk: (kk, jnp.minimum(j, half - 1))),
                    pl.BlockSpec((bk, bn), lambda i, j, kk: (kk, jnp.maximum(j - half, 0))), a_spec]
    return _pcall(
        body, name=name, grid=(m // bm, n // bn, nk), in_specs=in_specs, out_specs=o_spec, out_shape=out_shape,
        scratch_shapes=[pltpu.VMEM((bm, bn), F32)] if nk > 1 else [],
        compiler_params=_params("parallel", "parallel", "arbitrary"),
    )(*ins)


def _matmul_quarters(h, w_slots, order, first, count, name, prev=None, bm=1024, bn=1280):
    s, k = h.shape
    n = w_slots.shape[2]
    bm, bn = _pick(s, bm), _pick(n, bn)
    pb = n // bn

    def body(order_ref, h_ref, w_ref, *rest):
        rest[-1][...] = _dot(h_ref[...], w_ref[...])

    ins = (order, h, w_slots) + (() if prev is None else (prev,))
    return _pcall(
        body, name=name, num_scalar_prefetch=1, grid=(count * pb, s // bm),
        in_specs=[pl.BlockSpec((bm, k), lambda j, i, o: (i, 0)),
                  pl.BlockSpec((None, k, bn), lambda j, i, o: (o[first + j // pb], 0, j % pb))]
        + ([] if prev is None else [ANY_SPEC]),
        out_specs=pl.BlockSpec((bm, bn), lambda j, i, o: (i, o[first + j // pb] * pb + j % pb)),
        out_shape=jax.ShapeDtypeStruct((s, N_CHIPS * n), F32),
        input_output_aliases={} if prev is None else {3: 0},
        compiler_params=_params("arbitrary", "arbitrary"),
    )(*ins)


def _place():
    x, y, c = lax.axis_index("x"), lax.axis_index("y"), lax.axis_index("c")
    chips = [(1 - x, y), (x, 1 - y), (1 - x, 1 - y)]
    return x, y, c, chips


def _all_gather8(block, name):
    def body(x_ref, out_ref, send_sems, recv_sems, local_sem):
        x, y, c, chips = _place()
        me, sibling = (x, y, c), (x, y, 1 - c)

        def slot(px, py, pc):
            return out_ref.at[4 * px + 2 * py + pc]

        def copy(k, blk, to, src=None):
            return pltpu.make_async_remote_copy(
                src_ref=slot(*blk) if src is None else src, dst_ref=slot(*blk),
                send_sem=send_sems.at[k], recv_sem=recv_sems.at[k], device_id=to, device_id_type=MESH)

        mine = pltpu.make_async_copy(x_ref, slot(*me), local_sem)
        mine.start()
        first = [copy(0, me, sibling, src=x_ref)]
        first += [copy(1 + j, me, (*chip, c), src=x_ref) for j, chip in enumerate(chips)]
        for cp in first:
            cp.start()
        passed = [copy(4 + j, (*chip, c), sibling) for j, chip in enumerate(chips)]
        for j, chip in enumerate(chips):
            copy(1 + j, (*chip, c), me).wait_recv()
            passed[j].start()
        copy(0, sibling, me).wait_recv()
        for j, chip in enumerate(chips):
            copy(4 + j, (*chip, 1 - c), me).wait_recv()
        for cp in first + passed:
            cp.wait_send()
        mine.wait()

    return _pcall(
        body, name=name, out_shape=jax.ShapeDtypeStruct((N_DEV,) + block.shape, block.dtype),
        in_specs=[pl.BlockSpec(memory_space=pltpu.VMEM)], out_specs=pl.BlockSpec(memory_space=pltpu.VMEM),
        scratch_shapes=[pltpu.SemaphoreType.DMA((7,)), pltpu.SemaphoreType.DMA((7,)), pltpu.SemaphoreType.DMA],
        compiler_params=pltpu.CompilerParams(vmem_limit_bytes=VMEM_LIMIT_BYTES),
    )(block)


HBM_SPEC = pl.BlockSpec(memory_space=pltpu.HBM)
SEM_SPEC = pl.BlockSpec(memory_space=pltpu.SEMAPHORE)
ANY_SPEC = pl.BlockSpec(memory_space=pl.ANY)
EFFECT = pltpu.SideEffectType.DATAFLOW_SIDE_EFFECTING


def _xfer_start(name, bufs, plan, n_copies, after_last=False):
    nb = len(bufs)
    deps = (_Order.last,) if after_last and _Order.last is not None else ()
    nd = len(deps)

    def body(*refs):
        send_sems, recv_sems = refs[nb + nd], refs[nb + nd + 1]
        token = refs[nb + nd + 2 + nb]
        for k, (src, dst, dev) in enumerate(plan(refs[:nb], *_place())):
            pltpu.make_async_remote_copy(src_ref=src, dst_ref=dst, send_sem=send_sems.at[k], recv_sem=recv_sems.at[k],
                                         device_id=dev, device_id_type=MESH).start()
        token[...] = jnp.zeros_like(token)

    outs = pl.pallas_call(
        body, name=name,
        out_shape=(pltpu.SemaphoreType.DMA((n_copies,)), pltpu.SemaphoreType.DMA((n_copies,)),
                   *[pltpu.HBM(b.shape, b.dtype) for b in bufs], jax.ShapeDtypeStruct((8, LANES), F32)),
        in_specs=[HBM_SPEC] * nb + [ANY_SPEC] * nd,
        out_specs=(SEM_SPEC, SEM_SPEC, *[HBM_SPEC] * nb, pl.BlockSpec(memory_space=pltpu.VMEM)),
        input_output_aliases={i: 2 + i for i in range(nb)},
        compiler_params=pltpu.CompilerParams(has_side_effects=EFFECT),
    )(*[pltpu.with_memory_space_constraint(b, pltpu.HBM) for b in bufs], *deps)
    _Order.last = outs[-1]
    return (outs[0], outs[1]), list(outs[2:2 + nb])


def _xfer_wait(name, sems, bufs, plan):
    nb = len(bufs)

    def body(*refs):
        send_sems, recv_sems = refs[nb], refs[nb + 1]
        for k, (src, dst, dev) in enumerate(plan(refs[:nb], *_place())):
            copy = pltpu.make_async_remote_copy(src_ref=src, dst_ref=dst, send_sem=send_sems.at[k],
                                                recv_sem=recv_sems.at[k], device_id=dev, device_id_type=MESH)
            copy.wait_send()
            copy.wait_recv()

    outs = pl.pallas_call(
        body, name=name, out_shape=tuple(pltpu.HBM(b.shape, b.dtype) for b in bufs),
        in_specs=[HBM_SPEC] * nb + [SEM_SPEC, SEM_SPEC, ANY_SPEC], out_specs=tuple([HBM_SPEC] * nb),
        input_output_aliases={i: i for i in range(nb)},
        compiler_params=pltpu.CompilerParams(has_side_effects=EFFECT),
    )(*bufs, *sems, _Order.last)
    _Order.last = outs[0]
    return list(outs)


def _half(ref, c, axis):
    rows = ref.shape[axis] // 2
    return pl.ds(c * rows, rows)


def _plan_weights_ici(n):
    def plan(refs, x, y, c, chips):
        out = []
        for w in range(n):
            region = refs[w].at[2 * x + y, _half(refs[w], c, 1), :]
            out += [(region, region, (*chip, c)) for chip in chips]
        return out
    return plan


def _plan_weights_ring(n):
    def plan(refs, x, y, c, chips):
        out = []
        for w in range(n):
            region = refs[w].at[2 * x + y, _half(refs[w], c, 1), :]
            out += [(region, region, (*chip, c)) for chip in chips[:2]]
        return out
    return plan


def _plan_weights_relay(n):
    def plan(refs, x, y, c, chips):
        out = []
        for w in range(n):
            quarter_rows = refs[w].shape[1] // 4
            upper = refs[w].at[2 * x + (1 - y), pl.ds(2 * c * quarter_rows, quarter_rows), :]
            lower = refs[w].at[2 * (1 - x) + y, pl.ds((2 * c + 1) * quarter_rows, quarter_rows), :]
            out += [(upper, upper, (1 - x, y, c)), (lower, lower, (x, 1 - y, c))]
        return out
    return plan


def _plan_weights_d2d(n, which=slice(0, 3)):
    def plan(refs, x, y, c, chips):
        out = []
        for w in range(n):
            rows = _half(refs[w], c, 1)
            for chip in chips[which]:
                region = refs[w].at[2 * chip[0] + chip[1], rows, :]
                out.append((region, region, (x, y, 1 - c)))
        return out
    return plan


def _plan_grads_d2d(n):
    def plan(refs, x, y, c, chips):
        return [(refs[w].at[:, _half(refs[w], 1 - c, 1), :], refs[n + w], (x, y, 1 - c)) for w in range(n)]
    return plan


def _plan_grads_ici(n):
    def plan(refs, x, y, c, chips):
        out = []
        for w in range(n):
            out += [(refs[w].at[2 * chip[0] + chip[1]], refs[n + w].at[2 * x + y], (*chip, c)) for chip in chips]
        return out
    return plan


def _plan_final_d2d(n):
    def plan(refs, x, y, c, chips):
        out = []
        for w in range(n):
            region = refs[w].at[_half(refs[w], c, 0), :]
            out.append((region, region, (x, y, 1 - c)))
        return out
    return plan


def _stream_blocks(hr, cols):
    bc = cols if cols <= 4096 else _pick(cols, 4096)
    return _pick_rows(hr, max(16, (768 * 1024) // bc)), bc


def _pre_reduce(g, landed, place, name):
    _, rows, cols = g.shape
    hr = rows // 2
    rb, bc = _stream_blocks(hr, cols)
    nrb = hr // rb

    def body(place_ref, g_ref, l_ref, o_ref):
        o_ref[...] = (g_ref[...].astype(F32) + l_ref[...].astype(F32)).astype(o_ref.dtype)

    return _pcall(
        body, name=name, num_scalar_prefetch=1, grid=(N_CHIPS, nrb, cols // bc),
        in_specs=[pl.BlockSpec((None, rb, bc), lambda j, i, k, p: (j, p[1] * nrb + i, k)),
                  pl.BlockSpec((None, rb, bc), lambda j, i, k, p: (j, i, k))],
        out_specs=pl.BlockSpec((None, rb, bc), lambda j, i, k, p: (j, i, k)),
        out_shape=jax.ShapeDtypeStruct((N_CHIPS, hr, cols), g.dtype),
        compiler_params=_params("parallel", "parallel", "parallel"),
    )(place, g, landed)


def _sum_slots(mine, landed, place, name):
    _, hr, cols = mine.shape
    rb, bc = _stream_blocks(hr, cols)
    rb = _pick_rows(hr, max(16, rb // 2))
    nrb = hr // rb

    def body(place_ref, m_ref, l_ref, o_ref):
        chip = place_ref[0]
        own = m_ref[...].astype(F32)
        total = jnp.where(chip == 0, own, l_ref[0].astype(F32))
        for j in range(1, N_CHIPS):
            total = total + jnp.where(chip == j, own, l_ref[j].astype(F32))
        o_ref[...] = total

    return _pcall(
        body, name=name, num_scalar_prefetch=1, grid=(nrb, cols // bc),
        in_specs=[pl.BlockSpec((None, rb, bc), lambda i, k, p: (p[0], i, k)),
                  pl.BlockSpec((N_CHIPS, rb, bc), lambda i, k, p: (0, i, k))],
        out_specs=pl.BlockSpec((rb, bc), lambda i, k, p: (p[1] * nrb + i, k)),
        out_shape=jax.ShapeDtypeStruct((2 * hr, cols), F32),
        compiler_params=_params("parallel", "parallel"),
    )(place, mine, landed)


class _WeightGather:
    def __init__(self, tag, quarters, chip, ring=False):
        self.tag, self.n, self.ring = tag, len(quarters), ring
        zones = [lax.dynamic_update_slice(lax.empty((N_CHIPS,) + q.shape, BF16), q.astype(BF16)[None], (chip, 0, 0))
                 for q in quarters]
        self.plan = _plan_weights_ring(self.n) if ring else _plan_weights_ici(self.n)
        self.sems, self.bufs = _xfer_start("wici_start_" + tag, zones, self.plan, (2 if ring else 3) * self.n,
                                           after_last=True)

    def relay(self):
        bufs = _xfer_wait("wici_wait_" + self.tag, self.sems, self.bufs, self.plan)
        self.plan = _plan_weights_relay(self.n)
        self.sems, self.bufs = _xfer_start("wrel_start_" + self.tag, bufs, self.plan, 2 * self.n)

    def pass_on(self):
        bufs = _xfer_wait(("wrel_wait_" if self.ring else "wici_wait_") + self.tag, self.sems, self.bufs, self.plan)
        self.plan = _plan_weights_d2d(self.n)
        self.sems, self.bufs = _xfer_start("wd2d_start_" + self.tag, bufs, self.plan, 3 * self.n)

    def done(self):
        return _xfer_wait("wd2d_wait_" + self.tag, self.sems, self.bufs, self.plan)

    def relay_and_pass_near(self):
        self.relay()
        self.near_plan = _plan_weights_d2d(self.n, slice(0, 2))
        self.near_sems, self.bufs = _xfer_start("wnear_start_" + self.tag, self.bufs, self.near_plan, 2 * self.n)

    def near_done(self):
        self.bufs = _xfer_wait("wnear_wait_" + self.tag, self.near_sems, self.bufs, self.near_plan)
        return self.bufs

    def pass_far(self):
        bufs = _xfer_wait("wrel_wait_" + self.tag, self.sems, self.bufs, self.plan)
        plan = _plan_weights_d2d(self.n, slice(2, 3))
        sems, bufs = _xfer_start("wfar_start_" + self.tag, bufs, plan, self.n)
        return _xfer_wait("wfar_wait_" + self.tag, sems, bufs, plan)


class _GradReduce:
    def __init__(self, tag, grads):
        self.tag, self.n = tag, len(grads)
        zones = [lax.empty((N_CHIPS, g.shape[1] // 2, g.shape[2]), g.dtype) for g in grads]
        self.plan = _plan_grads_d2d(self.n)
        self.sems, self.bufs = _xfer_start("gd2d_start_" + tag, list(grads) + zones, self.plan, self.n)

    def pair(self, place):
        n = self.n
        bufs = _xfer_wait("gd2d_wait_" + self.tag, self.sems, self.bufs, self.plan)
        self.halves = [_pre_reduce(bufs[w], bufs[n + w], place, f"pre_reduce_{self.tag}{w}") for w in range(n)]

    def cross(self, after_last=False):
        zones = [lax.empty(h.shape, h.dtype) for h in self.halves]
        self.plan = _plan_grads_ici(self.n)
        self.sems, self.bufs = _xfer_start("gici_start_" + self.tag, self.halves + zones, self.plan, 3 * self.n,
                                           after_last=after_last)

    def step(self, place):
        self.pair(place)
        self.cross()

    def join(self, place):
        n = self.n
        bufs = _xfer_wait("gici_wait_" + self.tag, self.sems, self.bufs, self.plan)
        sums = [_sum_slots(bufs[w], bufs[n + w], place, f"sum_slots_{self.tag}{w}") for w in range(n)]
        self.plan = _plan_final_d2d(n)
        self.sems, self.bufs = _xfer_start("gfin_start_" + self.tag, sums, self.plan, n)

    def done(self):
        return _xfer_wait("gfin_wait_" + self.tag, self.sems, self.bufs, self.plan)


def _ada_fwd(c_all, w_q, b_q):
    d, n = w_q.shape
    bn = _pick(n, 512)

    def body(c_ref, w_ref, b_ref, o_ref):
        cv = c_ref[...]
        act = cv * _sigmoid(cv)
        o_ref[...] = _dot(act, w_ref[...], NN, lax.Precision.HIGHEST) + b_ref[...]

    return _pcall(
        body, name="ada_fwd", grid=(n // bn,),
        in_specs=[pl.BlockSpec((N_DEV, d), lambda j: (0, 0)), pl.BlockSpec((d, bn), lambda j: (0, j)),
                  pl.BlockSpec((1, bn), lambda j: (0, j))],
        out_specs=pl.BlockSpec((N_DEV, bn), lambda j: (0, j)),
        out_shape=jax.ShapeDtypeStruct((N_DEV, n), F32), compiler_params=_params("parallel"),
    )(c_all, w_q, b_q)


def _row_spec(rb, width, col=0):
    return pl.BlockSpec((rb, width), lambda i, col=col: (i, col))


def _vec_spec(width, col=0):
    return pl.BlockSpec((1, width), lambda i, col=col: (0, col))


def _norm_mod_fwd(x, g, sc, sh, name, res=None, gt=None):
    s, d = x.shape
    rb = _pick(s, 256)
    has_res = res is not None

    def body(*refs):
        if has_res:
            x_ref, res_ref, gt_ref, g_ref, sc_ref, sh_ref, x1_ref, h_ref = refs
            xv = x_ref[...] + gt_ref[...] * res_ref[...]
            x1_ref[...] = xv
        else:
            x_ref, g_ref, sc_ref, sh_ref, h_ref = refs
            xv = x_ref[...]
        xh, _ = _rms(xv)
        h_ref[...] = (xh * g_ref[...] * (1.0 + sc_ref[...]) + sh_ref[...]).astype(BF16)

    row, vec = _row_spec(rb, d), _vec_spec(d)
    if has_res:
        ins, in_specs = (x, res, gt, g, sc, sh), [row, row, vec, vec, vec, vec]
        out_shape = [jax.ShapeDtypeStruct((s, d), F32), jax.ShapeDtypeStruct((s, d), BF16)]
        out_specs = [row, row]
    else:
        ins, in_specs = (x, g, sc, sh), [row, vec, vec, vec]
        out_shape, out_specs = jax.ShapeDtypeStruct((s, d), BF16), row
    return _pcall(body, name=name, grid=(s // rb,), in_specs=in_specs, out_specs=out_specs,
                          out_shape=out_shape, compiler_params=_params("parallel"))(*ins)


def _final_loss(x1, f, gt2, g_final, target):
    s, d = x1.shape
    rb = _pick(s, 256)

    def body(x1_ref, f_ref, gt_ref, g_ref, t_ref, dx_ref, df_ref, loss_ref, dg_ref, dgt_ref):
        first = pl.program_id(0) == 0
        fv, gt, gv = f_ref[...], gt_ref[...], g_ref[...]
        x2 = x1_ref[...] + gt * fv
        xh, r = _rms(x2)
        err = xh * gv - t_ref[...]
        blk = 0.5 * jnp.sum(jnp.sum(err * err, axis=1, keepdims=True), axis=0, keepdims=True) / d
        dy = err / d
        dxh = dy * gv
        dx = r * (dxh - xh * jnp.mean(dxh * xh, axis=-1, keepdims=True))
        dx_ref[...] = dx
        df_ref[...] = (dx * gt).astype(BF16)
        _accumulate(first, loss_ref, jnp.broadcast_to(blk, (1, LANES)))
        _accumulate(first, dg_ref, _colsum(dy * xh))
        _accumulate(first, dgt_ref, _colsum(dx * fv))

    row, vec = _row_spec(rb, d), _vec_spec(d)
    return _pcall(
        body, name="final_loss", grid=(s // rb,), in_specs=[row, row, vec, vec, row],
        out_specs=[row, row, _vec_spec(LANES), vec, vec],
        out_shape=[jax.ShapeDtypeStruct((s, d), F32), jax.ShapeDtypeStruct((s, d), BF16),
                   jax.ShapeDtypeStruct((1, LANES), F32), jax.ShapeDtypeStruct((1, d), F32),
                   jax.ShapeDtypeStruct((1, d), F32)],
        compiler_params=_params("arbitrary"),
    )(x1, f, gt2, g_final, target)


def _norm_mod_bwd(dh, xin, dres, g, sc, name, branch=None, gt=None):
    s, d = xin.shape
    rb = _pick(s, 256)
    has_branch = branch is not None

    def body(*refs):
        if has_branch:
            dh_ref, x_ref, dres_ref, g_ref, sc_ref, br_ref, gt_ref, dx_ref, dbr_ref, dsh_ref, dsc_ref, dg_ref, dgt_ref = refs
        else:
            dh_ref, x_ref, dres_ref, g_ref, sc_ref, dx_ref, dsh_ref, dsc_ref, dg_ref = refs
        first = pl.program_id(0) == 0
        gv = g_ref[...]
        xh, r = _rms(x_ref[...])
        dhv = dh_ref[...]
        dn = dhv * (1.0 + sc_ref[...])
        dxh = dn * gv
        dx = dres_ref[...] + r * (dxh - xh * jnp.mean(dxh * xh, axis=-1, keepdims=True))
        dx_ref[...] = dx
        _accumulate(first, dsh_ref, _colsum(dhv))
        _accumulate(first, dsc_ref, _colsum(dhv * xh * gv))
        _accumulate(first, dg_ref, _colsum(dn * xh))
        if has_branch:
            dbr_ref[...] = (dx * gt_ref[...]).astype(BF16)
            _accumulate(first, dgt_ref, _colsum(dx * br_ref[...]))

    row, vec = _row_spec(rb, d), _vec_spec(d)
    vec_shape = jax.ShapeDtypeStruct((1, d), F32)
    if has_branch:
        ins, in_specs = (dh, xin, dres, g, sc, branch, gt), [row, row, row, vec, vec, row, vec]
        out_specs = [row, row, vec, vec, vec, vec]
        out_shape = [jax.ShapeDtypeStruct((s, d), F32), jax.ShapeDtypeStruct((s, d), BF16)] + [vec_shape] * 4
    else:
        ins, in_specs = (dh, xin, dres, g, sc), [row, row, row, vec, vec]
        out_specs = [row, vec, vec, vec]
        out_shape = [jax.ShapeDtypeStruct((s, d), F32)] + [vec_shape] * 3
    return _pcall(body, name=name, grid=(s // rb,), in_specs=in_specs, out_specs=out_specs,
                          out_shape=out_shape, compiler_params=_params("arbitrary"))(*ins)


def _gate_fwd(pa, pb, z, b_gate, gate_col):
    s, d = pa.shape
    rb = _pick(s, 256)

    def body(pa_ref, pb_ref, za_ref, zb_ref, ba_ref, bb_ref, y_ref):
        ga = _sigmoid(za_ref[...] + ba_ref[...])
        gb = _sigmoid(zb_ref[...] + bb_ref[...])
        y_ref[...] = (ga * pa_ref[...].astype(F32) + gb * pb_ref[...].astype(F32)).astype(BF16)

    row = _row_spec(rb, d)
    return _pcall(
        body, name="gate_fwd", grid=(s // rb,),
        in_specs=[row, row, _row_spec(rb, d, gate_col), _row_spec(rb, d, gate_col + 1), _vec_spec(d, 0), _vec_spec(d, 1)],
        out_specs=row, out_shape=jax.ShapeDtypeStruct((s, d), BF16), compiler_params=_params("parallel"),
    )(pa, pb, z, z, b_gate, b_gate)


def _out_bwd(dyo, w_out, pa, pb, z, b_gate, gate_col, bm=512, bn=1024):
    s, d = dyo.shape
    bm, bn = _pick(s, bm), _pick(d, bn)
    nj = d // bn

    def body(dyo_ref, w_ref, pa_ref, pb_ref, za_ref, zb_ref, ba_ref, bb_ref,
             dpa_ref, dpb_ref, dza_ref, dzb_ref, dba_ref, dbb_ref):
        first = pl.program_id(1) == 0
        dyv = _dot(dyo_ref[...], w_ref[...], NT)
        ga = _sigmoid(za_ref[...] + ba_ref[...])
        gb = _sigmoid(zb_ref[...] + bb_ref[...])
        dpa_ref[...] = (dyv * ga).astype(BF16)
        dpb_ref[...] = (dyv * gb).astype(BF16)
        dga = dyv * pa_ref[...].astype(F32) * ga * (1.0 - ga)
        dgb = dyv * pb_ref[...].astype(F32) * gb * (1.0 - gb)
        dza_ref[...] = dga.astype(BF16)
        dzb_ref[...] = dgb.astype(BF16)
        _accumulate(first, dba_ref, _colsum(dga))
        _accumulate(first, dbb_ref, _colsum(dgb))

    tile = pl.BlockSpec((bm, bn), lambda j, i: (i, j))
    vec = pl.BlockSpec((1, bn), lambda j, i: (0, j))
    act, vec_shape = jax.ShapeDtypeStruct((s, d), BF16), jax.ShapeDtypeStruct((1, d), F32)
    return _pcall(
        body, name="out_bwd", grid=(nj, s // bm),
        in_specs=[pl.BlockSpec((bm, d), lambda j, i: (i, 0)), pl.BlockSpec((bn, d), lambda j, i: (j, 0)), tile, tile,
                  pl.BlockSpec((bm, bn), lambda j, i: (i, gate_col * nj + j)),
                  pl.BlockSpec((bm, bn), lambda j, i: (i, (gate_col + 1) * nj + j)),
                  vec, pl.BlockSpec((1, bn), lambda j, i: (0, nj + j))],
        out_specs=[tile, tile, tile, tile, vec, vec], out_shape=[act, act, act, act, vec_shape, vec_shape],
        compiler_params=_params("parallel", "arbitrary"),
    )(dyo, w_out, pa, pb, z, z, b_gate, b_gate)


def _ffn_in(h, w_fi, bm=512, bn=1408):
    s, d = h.shape
    per = w_fi.shape[2]
    ff = 2 * per
    bm, bn = _pick(s, bm), _pick(per, bn)
    per_blocks = per // bn

    def body(h_ref, wa_ref, wu_ref, a_ref, u_ref, hf_ref):
        hv = h_ref[...]
        a = _dot(hv, wa_ref[...])
        up = _dot(hv, wu_ref[...])
        a_ref[...] = a.astype(BF16)
        u_ref[...] = up.astype(BF16)
        hf_ref[...] = (a * _sigmoid(a) * up).astype(BF16)

    out = pl.BlockSpec((bm, bn), lambda j, i: (i, j))
    shape = jax.ShapeDtypeStruct((s, ff), BF16)
    return _pcall(
        body, name="ffn_in", grid=(ff // bn, s // bm),
        in_specs=[pl.BlockSpec((bm, d), lambda j, i: (i, 0)),
                  pl.BlockSpec((None, d, bn), lambda j, i: (j // per_blocks, 0, j % per_blocks)),
                  pl.BlockSpec((None, d, bn), lambda j, i: (2 + j // per_blocks, 0, j % per_blocks))],
        out_specs=[out, out, out], out_shape=[shape, shape, shape],
        compiler_params=_params("parallel", "parallel"),
    )(h, w_fi, w_fi)


def _ffn_out_bwd(dffn, w_fo, a_act, up_act, bm=512, bn=1408):
    s, d = dffn.shape
    ff = w_fo.shape[0]
    bm, bn = _pick(s, bm), _pick(ff, bn)

    def body(d_ref, w_ref, a_ref, u_ref, da_ref, du_ref):
        dhf = _dot(d_ref[...], w_ref[...], NT)
        a = a_ref[...].astype(F32)
        sa = _sigmoid(a)
        da_ref[...] = (dhf * u_ref[...].astype(F32) * sa * (1.0 + a * (1.0 - sa))).astype(BF16)
        du_ref[...] = (dhf * a * sa).astype(BF16)

    tile = pl.BlockSpec((bm, bn), lambda j, i: (i, j))
    shape = jax.ShapeDtypeStruct((s, ff), BF16)
    return _pcall(
        body, name="ffn_out_bwd", grid=(ff // bn, s // bm),
        in_specs=[pl.BlockSpec((bm, d), lambda j, i: (i, 0)), pl.BlockSpec((bn, d), lambda j, i: (j, 0)), tile, tile],
        out_specs=[tile, tile], out_shape=[shape, shape], compiler_params=_params("parallel", "parallel"),
    )(dffn, w_fo, a_act, up_act)


def _tril(n):
    return lax.broadcasted_iota(jnp.int32, (n, n), 0) >= lax.broadcasted_iota(jnp.int32, (n, n), 1)


def _gmlp_norm(v, ln_g, ln_b):
    gv = _gelu(v)
    mu = jnp.mean(gv, axis=-1, keepdims=True)
    cen = gv - mu
    rs = lax.rsqrt(jnp.mean(cen * cen, axis=-1, keepdims=True) + EPS)
    xh = cen * rs
    return xh, rs, xh * ln_g + ln_b


def _gmlp_fwd(z, ln_g, ln_b, ws, bs_t):
    s = z.shape[0]
    gw = ln_g.shape[1]
    groups, chunk, _ = ws.shape

    def body(u_ref, v_ref, lg_ref, lb_ref, ws_ref, bs_ref, ya_ref):
        gu = _gelu(u_ref[...])
        _, _, vn = _gmlp_norm(v_ref[...], lg_ref[...], lb_ref[...])
        mask = _tril(chunk)
        for g in range(groups):
            cols = slice(g * LANES, (g + 1) * LANES)
            wm = jnp.where(mask, ws_ref[g], 0.0).astype(BF16)
            sg = _dot(wm, vn[:, cols].astype(BF16)) + bs_ref[:, g:g + 1]
            ya_ref[:, cols] = (gu[:, cols] * sg).astype(BF16)

    return _pcall(
        body, name="gmlp_fwd", grid=(s // chunk,),
        in_specs=[_row_spec(chunk, gw, 0), _row_spec(chunk, gw, 1), _vec_spec(gw), _vec_spec(gw),
                  pl.BlockSpec((groups, chunk, chunk), lambda i: (0, 0, 0)), pl.BlockSpec((chunk, LANES), lambda i: (0, 0))],
        out_specs=_row_spec(chunk, gw), out_shape=jax.ShapeDtypeStruct((s, gw), BF16),
        compiler_params=_params("parallel"),
    )(z, z, ln_g, ln_b, ws, bs_t)


def _gmlp_bwd(dya, z, ln_g, ln_b, ws, bs_t):
    s = z.shape[0]
    gw = ln_g.shape[1]
    groups, chunk, _ = ws.shape

    def body(dya_ref, u_ref, v_ref, lg_ref, lb_ref, ws_ref, bs_ref, duv_ref, dws_ref, dbs_ref, dlg_ref, dlb_ref, dvn_ref):
        first = pl.program_id(0) == 0
        u, v, lg = u_ref[...], v_ref[...], lg_ref[...]
        gu = _gelu(u)
        xh, rs, vn = _gmlp_norm(v, lg, lb_ref[...])
        dyav = dya_ref[...]
        mask = _tril(chunk)
        lane = lax.broadcasted_iota(jnp.int32, (chunk, LANES), 1)
        dbs = jnp.zeros((chunk, LANES), F32)
        for g in range(groups):
            cols = slice(g * LANES, (g + 1) * LANES)
            wm = jnp.where(mask, ws_ref[g], 0.0).astype(BF16)
            vg = vn[:, cols].astype(BF16)
            sg = _dot(wm, vg) + bs_ref[:, g:g + 1]
            ds = dyav[:, cols] * gu[:, cols]
            duv_ref[:, cols] = (dyav[:, cols] * sg * _gelu_grad(u[:, cols])).astype(BF16)
            dsb = ds.astype(BF16)
            _accumulate(first, dws_ref.at[g], jnp.where(mask, _dot(dsb, vg, NT), 0.0))
            dbs = dbs + jnp.where(lane == g, jnp.sum(ds, axis=-1, keepdims=True), 0.0)
            dvn_ref[:, cols] = _dot(wm, dsb, TN)
        dvn = dvn_ref[...]
        _accumulate(first, dbs_ref, dbs)
        _accumulate(first, dlb_ref, _colsum(dvn))
        _accumulate(first, dlg_ref, _colsum(dvn * xh))
        dxh = dvn * lg
        dgv = rs * (dxh - jnp.mean(dxh, axis=-1, keepdims=True) - xh * jnp.mean(dxh * xh, axis=-1, keepdims=True))
        duv_ref[:, gw:] = (dgv * _gelu_grad(v)).astype(BF16)

    return _pcall(
        body, name="gmlp_bwd", grid=(s // chunk,),
        in_specs=[_row_spec(chunk, gw), _row_spec(chunk, gw, 0), _row_spec(chunk, gw, 1), _vec_spec(gw), _vec_spec(gw),
                  pl.BlockSpec((groups, chunk, chunk), lambda i: (0, 0, 0)), pl.BlockSpec((chunk, LANES), lambda i: (0, 0))],
        out_specs=[_row_spec(chunk, 2 * gw), pl.BlockSpec((groups, chunk, chunk), lambda i: (0, 0, 0)),
                   pl.BlockSpec((chunk, LANES), lambda i: (0, 0)), _vec_spec(gw), _vec_spec(gw)],
        out_shape=[jax.ShapeDtypeStruct((s, 2 * gw), BF16), jax.ShapeDtypeStruct((groups, chunk, chunk), F32),
                   jax.ShapeDtypeStruct((chunk, LANES), F32), jax.ShapeDtypeStruct((1, gw), F32),
                   jax.ShapeDtypeStruct((1, gw), F32)],
        scratch_shapes=[pltpu.VMEM((chunk, gw), F32)],
        compiler_params=_params("arbitrary"),
    )(dya, z, z, ln_g, ln_b, ws, bs_t)


def _lower_bound(lb_ref):
    a0, a1 = lb_ref[0:1, :], lb_ref[1:2, :]
    mx = jnp.maximum(a0, a1)
    e0, e1 = jnp.exp(a0 - mx), jnp.exp(a1 - mx)
    return e0 / (e0 + e1)


def _sum_dot(mask, x):
    hi = x.astype(BF16)
    rest = x - hi.astype(F32)
    mid = rest.astype(BF16)
    low = (rest - mid.astype(F32)).astype(BF16)
    return _dot(mask, hi) + _dot(mask, mid) + _dot(mask, low)


def _ones_where(mask):
    return jnp.where(mask, 1.0, 0.0).astype(BF16)


def _hg_masks(rows, t):
    r = lax.broadcasted_iota(jnp.int32, (rows, rows), 0)
    c = lax.broadcasted_iota(jnp.int32, (rows, rows), 1)
    same = (r // t) == (c // t)
    incl = jnp.logical_and(same, c <= r)
    upto_mid = jnp.logical_and(same, (c % t) <= t // 2)
    rev = jnp.logical_and(same, c >= r)
    return same, incl, upto_mid, rev


def _hg_block(q, fp, lb, masks):
    rows = q.shape[0]
    same, incl, upto_mid, _ = masks
    sig = _sigmoid(fp)
    f = lb + (1.0 - lb) * sig
    k = 1.0 - f
    sq = _sigmoid(q)
    qa = q * sq
    stacked = jnp.concatenate([_ones_where(m) for m in (incl, same, upto_mid)], axis=0)
    sums = _sum_dot(stacked, jnp.log(f))
    b, b_last, b_mid = sums[:rows], sums[rows:2 * rows], sums[2 * rows:]
    e_q = jnp.exp(jnp.minimum(b - b_mid, EXP_CLAMP))
    e_k = jnp.exp(jnp.minimum(b_mid - b, EXP_CLAMP))
    e_in = jnp.exp(b)
    e_out = jnp.exp(b_last - b)
    return dict(sig=sig, f=f, k=k, sq=sq, qa=qa, e_last=jnp.exp(b_last), e_q=e_q, e_k=e_k, e_in=e_in, e_out=e_out,
                q_hat=(qa * e_q).astype(BF16), k_hat=(k * e_k).astype(BF16),
                q_in=(qa * e_in).astype(BF16), k_out=k * e_out)


def _hgrn_fwd(z, hg_lb, norm_g, q_col):
    s = z.shape[0]
    hw = norm_g.shape[1]
    heads = hw // LANES
    t = HG_CHUNK
    rows = min(HG_ROWS, s)
    per_step = rows // t
    hp = min(HG_HEADS_PER_STEP, heads)
    assert heads % hp == 0 and q_col % hp == 0, (heads, q_col)
    wide = hp * LANES

    def zspec(which):
        return pl.BlockSpec((rows, wide), lambda h, r, which=which: (r, (q_col + which * heads) // hp + h))

    def body(q_ref, f_ref, i_ref, g_ref, lb_ref, ng_ref, yb_ref, o_ref, st_out_ref, st_ref, e_last_ref, inter_ref):
        @pl.when(pl.program_id(1) == 0)
        def _():
            st_ref[...] = jnp.zeros_like(st_ref)

        masks = _hg_masks(rows, t)
        for hh in range(hp):
            cols = slice(hh * LANES, (hh + 1) * LANES)
            blk = _hg_block(q_ref[:, cols], f_ref[:, cols], _lower_bound(lb_ref.at[:, cols]), masks)
            iv = i_ref[:, cols].astype(BF16)
            q_in, k_out = blk["q_in"], blk["k_out"].astype(BF16)
            e_last_ref[hh] = blk["e_last"]
            attn = jnp.where(masks[1], _dot(blk["q_hat"], blk["k_hat"], NT), 0.0).astype(BF16)
            o = _dot(attn, iv)
            grown = [_dot(iv[j * t:(j + 1) * t], k_out[j * t:(j + 1) * t], TN) for j in range(per_step)]
            st = st_ref[hh]
            for j in range(per_step):
                st_out_ref[hh, j] = st
                inter_ref[hh, j * t:(j + 1) * t, :] = _dot(q_in[j * t:(j + 1) * t], st.astype(BF16), NT)
                st = st * e_last_ref[hh, j * t:j * t + 1, :] + grown[j]
            st_ref[hh] = st
            o = o + inter_ref[hh]
            o_ref[:, cols] = o
            og = g_ref[:, cols]
            on, _ = _rms(o)
            yb_ref[:, cols] = (on * ng_ref[:, cols] * (og * _sigmoid(og))).astype(BF16)

    out_row = pl.BlockSpec((rows, wide), lambda h, r: (r, h))
    return _pcall(
        body, name="hgrn_fwd", grid=(heads // hp, s // rows),
        in_specs=[zspec(0), zspec(1), zspec(2), zspec(3),
                  pl.BlockSpec((2, wide), lambda h, r: (0, h)), pl.BlockSpec((1, wide), lambda h, r: (0, h))],
        out_specs=[out_row, out_row, pl.BlockSpec((hp, per_step, LANES, LANES), lambda h, r: (h, r, 0, 0))],
        out_shape=[jax.ShapeDtypeStruct((s, hw), BF16), jax.ShapeDtypeStruct((s, hw), F32),
                   jax.ShapeDtypeStruct((heads, s // t, LANES, LANES), F32)],
        scratch_shapes=[pltpu.VMEM((hp, LANES, LANES), F32), pltpu.VMEM((hp, rows, LANES), F32),
                        pltpu.VMEM((hp, rows, LANES), F32)],
        compiler_params=_params("parallel", "arbitrary"),
    )(z, z, z, z, hg_lb, norm_g)


def _hgrn_bwd(dyb, z, o_raw, states, hg_lb, norm_g, q_col):
    s = z.shape[0]
    hw = norm_g.shape[1]
    heads = hw // LANES
    t = HG_CHUNK
    rows = min(HG_ROWS, s)
    per_step = rows // t
    n_steps = s // rows
    hp = min(HG_HEADS_PER_STEP, heads)
    assert heads % hp == 0 and q_col % hp == 0, (heads, q_col)
    wide = hp * LANES

    def zspec(which):
        return pl.BlockSpec((rows, wide), lambda h, r, which=which: (n_steps - 1 - r, (q_col + which * heads) // hp + h))

    def body(dyb_ref, q_ref, f_ref, i_ref, g_ref, o_ref, st_in_ref, lb_ref, ng_ref,
             dq_ref, df_ref, di_ref, dg_ref, dlb_ref, dng_ref, dst_ref, acc_lb_ref, acc_ng_ref,
             e_last_ref, dq_in_ref, dk_out_ref, di_inter_ref, carry_ref):
        step = pl.program_id(1)

        @pl.when(step == 0)
        def _():
            dst_ref[...] = jnp.zeros_like(dst_ref)
            acc_lb_ref[...] = jnp.zeros_like(acc_lb_ref)
            acc_ng_ref[...] = jnp.zeros_like(acc_ng_ref)

        masks = _hg_masks(rows, t)
        same, incl, _, rev = masks
        sum_mask = jnp.concatenate([_ones_where(rev), _ones_where(same)], axis=1)
        for hh in range(hp):
            cols = slice(hh * LANES, (hh + 1) * LANES)
            lb = _lower_bound(lb_ref.at[:, cols])
            ng = ng_ref[:, cols]
            q = q_ref[:, cols]
            blk = _hg_block(q, f_ref[:, cols], lb, masks)
            iv = i_ref[:, cols].astype(BF16)
            o, og, dy = o_ref[:, cols], g_ref[:, cols], dyb_ref[:, cols]
            so = _sigmoid(og)
            on, r = _rms(o)
            acc_ng_ref[:, cols] += _colsum(dy * on * (og * so))
            dg_ref[:, cols] = (dy * on * ng * so * (1.0 + og * (1.0 - so))).astype(BF16)
            don = dy * ng * (og * so)
            do = (r * (don - on * jnp.mean(don * on, axis=-1, keepdims=True))).astype(BF16)
            q_hat, k_hat, q_in, k_out = blk["q_hat"], blk["k_hat"], blk["q_in"], blk["k_out"]
            k_out_b = k_out.astype(BF16)
            attn = jnp.where(incl, _dot(q_hat, k_hat, NT), 0.0).astype(BF16)
            d_attn = jnp.where(incl, _dot(do, iv, NT), 0.0).astype(BF16)
            di_intra = _dot(attn, do, TN)
            dq_hat = _dot(d_attn, k_hat)
            dk_hat = _dot(d_attn, q_hat, TN)
            e_last_ref[hh] = blk["e_last"]
            grown = [_dot(do[j * t:(j + 1) * t], q_in[j * t:(j + 1) * t], TN) for j in range(per_step)]
            dst = dst_ref[hh]
            for j in reversed(range(per_step)):
                rs_ = slice(j * t, (j + 1) * t)
                e_last = e_last_ref[hh, j * t:j * t + 1, :]
                st_prev, dst_b = st_in_ref[hh, j], dst.astype(BF16)
                dq_in_ref[hh, rs_, :] = _dot(do[rs_], st_prev.astype(BF16))
                dk_out_ref[hh, rs_, :] = _dot(iv[rs_], dst_b)
                di_inter_ref[hh, rs_, :] = _dot(k_out_b[rs_], dst_b, NT)
                carry_ref[hh, rs_, :] = jnp.broadcast_to(e_last * _colsum(st_prev * dst), (t, LANES))
                dst = dst * e_last + grown[j]
            dst_ref[hh] = dst
            di_ref[:, cols] = (di_intra + di_inter_ref[hh]).astype(BF16)
            dk_out = dk_out_ref[hh]
            dqa = dq_in_ref[hh] * blk["e_in"] + dq_hat * blk["e_q"]
            dk = dk_out * blk["e_out"] + dk_hat * blk["e_k"]
            db = blk["qa"] * dqa - blk["k"] * dk
            dlf = _sum_dot(sum_mask, jnp.concatenate([db, dk_out * k_out], axis=0)) + carry_ref[hh]
            dfv = dlf / blk["f"] - dk
            sig, sq = blk["sig"], blk["sq"]
            df_ref[:, cols] = (dfv * (1.0 - lb) * sig * (1.0 - sig)).astype(BF16)
            acc_lb_ref[:, cols] += _colsum(dfv * (1.0 - sig))
            dq_ref[:, cols] = (dqa * sq * (1.0 + q * (1.0 - sq))).astype(BF16)

        @pl.when(step == n_steps - 1)
        def _():
            lb = _lower_bound(lb_ref)
            d0 = acc_lb_ref[...] * lb * (1.0 - lb)
            dlb_ref[0:1, :] = d0
            dlb_ref[1:2, :] = -d0
            dng_ref[...] = acc_ng_ref[...]

    rev_row = pl.BlockSpec((rows, wide), lambda h, r: (n_steps - 1 - r, h))
    piece = jax.ShapeDtypeStruct((s, hw), BF16)
    return _pcall(
        body, name="hgrn_bwd", grid=(heads // hp, n_steps),
        in_specs=[rev_row, zspec(0), zspec(1), zspec(2), zspec(3), rev_row,
                  pl.BlockSpec((hp, per_step, LANES, LANES), lambda h, r: (h, n_steps - 1 - r, 0, 0)),
                  pl.BlockSpec((2, wide), lambda h, r: (0, h)), pl.BlockSpec((1, wide), lambda h, r: (0, h))],
        out_specs=[rev_row, rev_row, rev_row, rev_row,
                   pl.BlockSpec((2, wide), lambda h, r: (0, h)), pl.BlockSpec((1, wide), lambda h, r: (0, h))],
        out_shape=[piece, piece, piece, piece, jax.ShapeDtypeStruct((2, hw), F32), jax.ShapeDtypeStruct((1, hw), F32)],
        scratch_shapes=[pltpu.VMEM((hp, LANES, LANES), F32), pltpu.VMEM((1, wide), F32), pltpu.VMEM((1, wide), F32)]
        + [pltpu.VMEM((hp, rows, LANES), F32)] * 5,
        compiler_params=_params("parallel", "arbitrary"),
    )(dyb, z, z, z, z, o_raw, states, hg_lb, norm_g)


def _adam_update(w, m, v, g):
    m2 = ADAM_B1 * m + (1.0 - ADAM_B1) * g
    v2 = ADAM_B2 * v + (1.0 - ADAM_B2) * (g * g)
    m_hat = m2 * (1.0 / (1.0 - ADAM_B1 ** ADAM_STEP))
    v_hat = v2 * (1.0 / (1.0 - ADAM_B2 ** ADAM_STEP))
    return -ADAM_LR * (m_hat / (jnp.sqrt(v_hat) + ADAM_EPS) + ADAM_WD * w), m2, v2


def _adamw(w, m, v, parts, name, outer=False):
    rows, cols = w.shape
    bc = cols if cols <= 4096 else _pick(cols, 4096)
    rb = _pick_rows(rows, max(8, (384 * 1024) // bc), mult=8)
    if outer and rb % LANES:
        rb = rows

    def body(w_ref, m_ref, v_ref, *refs):
        g_ref, d_ref, mo_ref, vo_ref = refs[-4:]
        if outer:
            cv = refs[0][...]
            g = _dot(cv * _sigmoid(cv), refs[1][...], TN, lax.Precision.HIGHEST)
        else:
            p_ref = refs[0]
            g = p_ref[0].astype(F32)
            for p in range(1, p_ref.shape[0]):
                g = g + p_ref[p].astype(F32)
        g_ref[...] = g
        d_ref[...], mo_ref[...], vo_ref[...] = _adam_update(w_ref[...], m_ref[...], v_ref[...], g)

    blk = pl.BlockSpec((rb, bc), lambda i, j: (i, j))
    out = jax.ShapeDtypeStruct((rows, cols), F32)
    if outer:
        grad_specs = [pl.BlockSpec((N_DEV, rb), lambda i, j: (0, i)), pl.BlockSpec((N_DEV, bc), lambda i, j: (0, j))]
        grad_ins = tuple(parts)
    else:
        grad_specs = [pl.BlockSpec((parts.shape[0], rb, bc), lambda i, j: (0, i, j))]
        grad_ins = (parts,)
    return _pcall(
        body, name=name, grid=(rows // rb, cols // bc), in_specs=[blk, blk, blk] + grad_specs,
        out_specs=[blk] * 4, out_shape=[out] * 4, compiler_params=_params("parallel", "parallel"),
    )(w, m, v, *grad_ins)


SMALL = ("b_ada", "norm1_g", "b_gate", "gmlp_ln_g", "gmlp_ln_b", "gmlp_ws", "gmlp_bs", "hg_lb", "hg_norm_g",
         "norm2_g", "final_norm_g")
BIG = ("w_in", "w_branch_gmlp", "w_branch_hg", "w_out", "w_ffn_in", "w_ffn_out")
WEIGHTS = ("w_ada", "b_ada", "norm1_g", "w_in", "b_gate", "gmlp_ln_g", "gmlp_ln_b", "gmlp_ws", "gmlp_bs", "hg_lb",
           "hg_norm_g", "w_branch_gmlp", "w_branch_hg", "w_out", "norm2_g", "w_ffn_in", "w_ffn_out", "final_norm_g")


def _pack(parts):
    return jnp.concatenate([p.reshape(-1, LANES) for p in parts], axis=0)


def _step(x, c, loss_target, w, m, v):
    s, d = x.shape[1], x.shape[2]
    gw = w["gmlp_ln_g"].shape[-1]
    hw = w["hg_norm_g"].shape[-1]
    x2d, tgt = x[0], loss_target[0]
    mx, my, mc = lax.axis_index("x"), lax.axis_index("y"), lax.axis_index("c")
    chip = 2 * mx + my
    dev = 2 * chip + mc
    q_col = 2 * gw // LANES
    gate_col = (2 * gw + 4 * hw) // d
    place = jnp.stack([chip, mc]).astype(jnp.int32)
    _Order.last = None

    c_all = _all_gather8(c.reshape(-1, LANES), "gather_c").reshape(N_DEV, d)
    gather_in = _WeightGather("in", [w["w_in"][0]], chip, ring=True)
    n_ada = w["w_ada"].shape[-1]
    b_ada_q = lax.dynamic_slice(w["b_ada"], (0, chip * n_ada), (1, n_ada))
    mod_q = _ada_fwd(c_all, w["w_ada"][0], b_ada_q)
    mod_all = _all_gather8(mod_q, "gather_mod")
    mod = lax.dynamic_index_in_dim(mod_all, dev, axis=1, keepdims=False)[::2].reshape(1, 6 * d)
    sh1, sc1, gt1, sh2, sc2, gt2 = [mod[:, i * d:(i + 1) * d] for i in range(6)]

    gather_in.relay_and_pass_near()
    gather_mix = _WeightGather("mix", [w[n][0] for n in ("w_branch_gmlp", "w_branch_hg", "w_out")], chip)
    gather_fi = _WeightGather("fi", [w["w_ffn_in"][0]], chip, ring=True)

    norm1_g, norm2_g, final_g = w["norm1_g"], w["norm2_g"], w["final_norm_g"].reshape(1, d)
    ln_g, ln_b = w["gmlp_ln_g"], w["gmlp_ln_b"]
    ws = w["gmlp_ws"][0]
    groups = ws.shape[0]
    bs_t = jnp.pad(w["gmlp_bs"][0].T, ((0, 0), (0, LANES - groups)))
    hg_lb, hg_ng, b_gate = w["hg_lb"], w["hg_norm_g"], w["b_gate"]

    h1 = _norm_mod_fwd(x2d, norm1_g, sc1, sh1, "norm1_fwd")
    order = jnp.stack([chip, 2 * (1 - mx) + my, 2 * mx + (1 - my), 2 * (1 - mx) + (1 - my)]).astype(jnp.int32)
    w_in, = gather_in.near_done()
    z = _matmul_quarters(h1, w_in, order, 0, 3, "mm_z_near")
    w_in, = gather_in.pass_far()
    z = _matmul_quarters(h1, w_in, order, 3, 1, "mm_z_far", prev=z)
    gather_mix.pass_on()
    ya = _gmlp_fwd(z, ln_g, ln_b, ws, bs_t)
    yb, o_raw, states = _hgrn_fwd(z, hg_lb, hg_ng, q_col)
    gather_fi.relay()
    gather_fo = _WeightGather("fo", [w["w_ffn_out"][0]], chip)
    w_bg, w_bh, w_out = gather_mix.done()
    w_out = w_out.reshape(-1, w_out.shape[-1])
    pa = _matmul(ya, w_bg, mode="nn", name="mm_pa", out_dtype=BF16, b_slots=True)
    pb = _matmul(yb, w_bh, mode="nn", name="mm_pb", out_dtype=BF16, b_slots=True)
    y = _gate_fwd(pa, pb, z, b_gate, gate_col)
    gather_fi.pass_on()
    yo = _matmul(y, w_out, mode="nn", name="mm_yo", out_dtype=F32)
    x1, h2 = _norm_mod_fwd(x2d, norm2_g, sc2, sh2, "norm2_fwd", res=yo, gt=gt1)
    w_fi, = gather_fi.done()
    a_act, up_act, hf = _ffn_in(h2, w_fi)
    gather_fo.pass_on()
    w_fo, = gather_fo.done()
    w_fo = w_fo.reshape(-1, w_fo.shape[-1])
    ffn = _matmul(hf, w_fo, mode="nn", name="mm_ffn", out_dtype=F32)
    dx2, dffn, loss_row, d_final_g, d_gt2 = _final_loss(x1, ffn, gt2, final_g, tgt)

    g_fo = _matmul(hf, dffn, mode="tn", name="mm_g_fo", out_dtype=BF16, bm=1408)
    daup = tuple(_ffn_out_bwd(dffn, w_fo, a_act, up_act))
    g_fi = _matmul(h2, daup, mode="tn", name="mm_g_fi", out_dtype=BF16, out_slots=True, bn=1408)
    red_ffn = _GradReduce("ffn", [g_fo.reshape(N_CHIPS, -1, g_fo.shape[-1]), g_fi])
    dh2 = _matmul(daup, w_fi, mode="nt", name="mm_dh2", out_dtype=F32, b_slots=True)
    red_ffn.step(place)
    dx1, dyo, d_sh2, d_sc2, d_norm2, d_gt1 = _norm_mod_bwd(dh2, x1, dx2, norm2_g, sc2, "norm2_bwd", branch=yo, gt=gt1)
    g_out = _matmul(y, dyo, mode="tn", name="mm_g_out", out_dtype=BF16)
    dpa, dpb, dz_ga, dz_gb, d_b_ga, d_b_gb = _out_bwd(dyo, w_out, pa, pb, z, b_gate, gate_col)
    g_bg = _matmul(ya, dpa, mode="tn", name="mm_g_bg", out_dtype=BF16, out_slots=True)
    g_bh = _matmul(yb, dpb, mode="tn", name="mm_g_bh", out_dtype=BF16, out_slots=True)
    red_mix = _GradReduce("mix", [g_out.reshape(N_CHIPS, -1, g_out.shape[-1]), g_bg, g_bh])
    dya = _matmul(dpa, w_bg, mode="nt", name="mm_dya", out_dtype=F32, b_slots=True)
    dyb = _matmul(dpb, w_bh, mode="nt", name="mm_dyb", out_dtype=F32, b_slots=True)
    red_mix.step(place)
    dz_uv, d_ws, d_bs_t, d_ln_g, d_ln_b = _gmlp_bwd(dya, z, ln_g, ln_b, ws, bs_t)
    dz_q, dz_f, dz_i, dz_g, d_hg_lb, d_hg_ng = _hgrn_bwd(dyb, z, o_raw, states, hg_lb, hg_ng, q_col)
    dz = jnp.concatenate([dz_uv, dz_q, dz_f, dz_i, dz_g, dz_ga, dz_gb], axis=1)
    g_in = _matmul(h1, dz, mode="tn", name="mm_g_in", out_dtype=BF16, out_slots=True, bn=1280)
    red_in = _GradReduce("in", [g_in])
    dh1 = _matmul(dz, w_in, mode="nt", name="mm_dh1", out_dtype=F32, b_slots=True)
    red_in.pair(place)
    grad_x, d_sh1, d_sc1, d_norm1 = _norm_mod_bwd(dh1, x2d, dx1, norm1_g, sc1, "norm1_bwd")

    d_mod = jnp.concatenate([d_sh1, d_sc1, d_gt1, d_sh2, d_sc2, d_gt2], axis=1)
    small_part = {"b_ada": d_mod, "norm1_g": d_norm1, "b_gate": jnp.concatenate([d_b_ga, d_b_gb], axis=1), "gmlp_ln_g": d_ln_g, "gmlp_ln_b": d_ln_b,
                  "gmlp_ws": d_ws, "gmlp_bs": d_bs_t[:, :groups].T, "hg_lb": d_hg_lb, "hg_norm_g": d_hg_ng,
                  "norm2_g": d_norm2, "final_norm_g": d_final_g}
    small_all = _all_gather8(_pack([small_part[n] for n in SMALL]), "gather_small")
    red_in.cross(after_last=True)
    d_mod_all = small_all[:, :6 * d // LANES].reshape(N_DEV, 6 * d)
    d_mod_q = lax.dynamic_slice(d_mod_all, (0, chip * n_ada), (N_DEV, n_ada))

    grad, delta, new_m, new_v = {}, {}, {}, {}

    def update(n, parts, outer=False):
        outs = _adamw(w[n][0], m[n][0], v[n][0], parts, "adamw_" + n, outer=outer)
        grad[n], delta[n], new_m[n], new_v[n] = [o[None] for o in outs]

    update("w_ada", (c_all, d_mod_q), outer=True)
    outs = _adamw(_pack([w[n] for n in SMALL]), _pack([m[n] for n in SMALL]), _pack([v[n] for n in SMALL]),
                  small_all, "adamw_small")
    red_ffn.join(place)
    red_mix.join(place)
    g_fo, g_fi = red_ffn.done()
    update("w_ffn_out", g_fo[None])
    update("w_ffn_in", g_fi[None])
    red_in.join(place)
    g_out, g_bg, g_bh = red_mix.done()
    update("w_out", g_out[None])
    update("w_branch_gmlp", g_bg[None])
    update("w_branch_hg", g_bh[None])
    update("w_in", red_in.done()[0][None])
    row = 0
    for n in SMALL:
        cnt = w[n].size // LANES
        for dst, o in zip((grad, delta, new_m, new_v), outs):
            dst[n] = o[row:row + cnt].reshape(w[n].shape)
        row += cnt

    loss = lax.psum(loss_row[0, 0], ("x", "y", "c"))
    return (loss, grad_x[None], *[grad[n] for n in WEIGHTS], *[delta[n] for n in WEIGHTS],
            *[new_m[n] for n in WEIGHTS], *[new_v[n] for n in WEIGHTS])


def kernel(x, c, w_ada, b_ada, norm1_g, w_in, b_gate, gmlp_ln_g, gmlp_ln_b, gmlp_ws, gmlp_bs, hg_lb, hg_norm_g, w_branch_gmlp, w_branch_hg, w_out, norm2_g, w_ffn_in, w_ffn_out, final_norm_g, loss_target, m_w_ada, m_b_ada, m_norm1_g, m_w_in, m_b_gate, m_gmlp_ln_g, m_gmlp_ln_b, m_gmlp_ws, m_gmlp_bs, m_hg_lb, m_hg_norm_g, m_w_branch_gmlp, m_w_branch_hg, m_w_out, m_norm2_g, m_w_ffn_in, m_w_ffn_out, m_final_norm_g, v_w_ada, v_b_ada, v_norm1_g, v_w_in, v_b_gate, v_gmlp_ln_g, v_gmlp_ln_b, v_gmlp_ws, v_gmlp_bs, v_hg_lb, v_hg_norm_g, v_w_branch_gmlp, v_w_branch_hg, v_w_out, v_norm2_g, v_w_ffn_in, v_w_ffn_out, v_final_norm_g):
    w = dict(w_ada=w_ada, b_ada=b_ada, norm1_g=norm1_g, w_in=w_in, b_gate=b_gate, gmlp_ln_g=gmlp_ln_g,
             gmlp_ln_b=gmlp_ln_b, gmlp_ws=gmlp_ws, gmlp_bs=gmlp_bs, hg_lb=hg_lb, hg_norm_g=hg_norm_g,
             w_branch_gmlp=w_branch_gmlp, w_branch_hg=w_branch_hg, w_out=w_out, norm2_g=norm2_g,
             w_ffn_in=w_ffn_in, w_ffn_out=w_ffn_out, final_norm_g=final_norm_g)
    m = dict(w_ada=m_w_ada, b_ada=m_b_ada, norm1_g=m_norm1_g, w_in=m_w_in, b_gate=m_b_gate, gmlp_ln_g=m_gmlp_ln_g,
             gmlp_ln_b=m_gmlp_ln_b, gmlp_ws=m_gmlp_ws, gmlp_bs=m_gmlp_bs, hg_lb=m_hg_lb, hg_norm_g=m_hg_norm_g,
             w_branch_gmlp=m_w_branch_gmlp, w_branch_hg=m_w_branch_hg, w_out=m_w_out, norm2_g=m_norm2_g,
             w_ffn_in=m_w_ffn_in, w_ffn_out=m_w_ffn_out, final_norm_g=m_final_norm_g)
    v = dict(w_ada=v_w_ada, b_ada=v_b_ada, norm1_g=v_norm1_g, w_in=v_w_in, b_gate=v_b_gate, gmlp_ln_g=v_gmlp_ln_g,
             gmlp_ln_b=v_gmlp_ln_b, gmlp_ws=v_gmlp_ws, gmlp_bs=v_gmlp_bs, hg_lb=v_hg_lb, hg_norm_g=v_hg_norm_g,
             w_branch_gmlp=v_w_branch_gmlp, w_branch_hg=v_w_branch_hg, w_out=v_w_out, norm2_g=v_norm2_g,
             w_ffn_in=v_w_ffn_in, w_ffn_out=v_w_ffn_out, final_norm_g=v_final_norm_g)
    return _step(x, c, loss_target, w, m, v)
```

```python
import functools

import jax
import jax.numpy as jnp
from jax import lax
from jax.experimental import pallas as pl
from jax.experimental.pallas import tpu as pltpu

F32 = jnp.float32
BF16 = jnp.bfloat16
EPS = 1e-6
LANES = 128
N_CHIPS = 4
N_DEV = 8
VMEM_LIMIT_BYTES = 56 * 1024 * 1024
HG_CHUNK = 32
HG_ROWS = 256
HG_HEADS_PER_STEP = 8
EXP_CLAMP = 80.0
ADAM_LR, ADAM_B1, ADAM_B2, ADAM_EPS, ADAM_WD, ADAM_STEP = 0.001, 0.9, 0.999, 1e-08, 0.01, 10
MESH = pl.DeviceIdType.MESH

NN = (((1,), (0,)), ((), ()))
NT = (((1,), (1,)), ((), ()))
TN = (((0,), (0,)), ((), ()))


def _dot(a, b, dims=NN, precision=None):
    return lax.dot_general(a, b, dims, precision=precision, preferred_element_type=F32)


def _params(*semantics):
    return pltpu.CompilerParams(dimension_semantics=semantics, vmem_limit_bytes=VMEM_LIMIT_BYTES)


class _Order:
    last = None
    also = ()


def _pcall(body, *, in_specs, out_specs, grid=(), scratch_shapes=(), num_scalar_prefetch=0, **kw):
    def run(*ins):
        deps = (() if _Order.last is None else (_Order.last,)) + tuple(_Order.also)
        _Order.also = ()
        n_in, n_dep = len(ins), len(deps)

        def wrapped(*refs):
            body(*refs[:n_in], *refs[n_in + n_dep:])

        specs = list(in_specs) + [pl.BlockSpec(memory_space=pl.ANY)] * n_dep
        if num_scalar_prefetch:
            grid_spec = pltpu.PrefetchScalarGridSpec(
                num_scalar_prefetch=num_scalar_prefetch, grid=grid, in_specs=specs, out_specs=out_specs,
                scratch_shapes=scratch_shapes)
            outs = pl.pallas_call(wrapped, grid_spec=grid_spec, **kw)(*ins, *deps)
        else:
            outs = pl.pallas_call(wrapped, grid=grid, in_specs=specs, out_specs=out_specs,
                                  scratch_shapes=scratch_shapes, **kw)(*ins, *deps)
        _Order.last = jax.tree.leaves(outs)[0]
        return outs

    return run


def _pick_rows(dim, pref, mult=16):
    best = None
    for cand in range(mult, min(dim, pref) + 1, mult):
        if dim % cand == 0:
            best = cand
    assert best is not None, (dim, pref)
    return best


def _pick(dim, pref):
    if dim <= pref:
        return dim
    best = None
    for cand in range(LANES, pref + 1, LANES):
        if dim % cand == 0:
            best = cand
    assert best is not None, (dim, pref)
    return best


def _sigmoid(x):
    return 1.0 / (1.0 + jnp.exp(-x))


def _gelu(x):
    c = 0.7978845608028654
    return 0.5 * x * (1.0 + jnp.tanh(c * (x + 0.044715 * x * x * x)))


def _gelu_grad(x):
    c = 0.7978845608028654
    t = jnp.tanh(c * (x + 0.044715 * x * x * x))
    return 0.5 * (1.0 + t) + 0.5 * x * (1.0 - t * t) * c * (1.0 + 3.0 * 0.044715 * x * x)


def _rms(x):
    r = lax.rsqrt(jnp.mean(x * x, axis=-1, keepdims=True) + EPS)
    return x * r, r


def _colsum(x):
    return jnp.sum(x, axis=0, keepdims=True)


def _accumulate(first, ref, val):
    @pl.when(first)
    def _():
        ref[...] = val

    @pl.when(jnp.logical_not(first))
    def _():
        ref[...] += val


def _matmul(a, b, *, mode, name, out_dtype, b_slots=False, out_slots=False, bm=1024, bn=1024, bk=2816):
    pair = a if isinstance(a, tuple) else b if isinstance(b, tuple) else None
    if mode == "nn":
        m, k = a.shape
        n = b.shape[2] * N_CHIPS if b_slots else b.shape[1]
        per = b.shape[2] if b_slots else n
    elif mode == "nt":
        m, k = (a[0].shape[0], 2 * a[0].shape[1]) if pair else a.shape
        n = b.shape[1] if b_slots else b.shape[0]
        per = b.shape[2] if b_slots else k
    else:
        k, m = a.shape
        n = 2 * b[0].shape[1] if pair else b.shape[1]
        per = n // N_CHIPS if out_slots else n
    bm = _pick(m, bm)
    if mode == "nt":
        bn, bk = _pick(n, bn), _pick(per, bk)
    else:
        bn, bk = _pick(per, bn), _pick(k, bk)
    nk = k // bk
    per_blocks = per // (bk if mode == "nt" else bn)
    dims = {"nn": NN, "nt": NT, "tn": TN}[mode]
    half = (nk if mode == "nt" else n // bn) // 2

    def product(a_ref, b_ref, o_ref, acc):
        part = _dot(a_ref[...], b_ref[...], dims)
        if nk == 1:
            o_ref[...] = part.astype(o_ref.dtype)
            return
        acc_ref, kk = acc[0], pl.program_id(2)

        @pl.when(kk == 0)
        def _():
            acc_ref[...] = part

        @pl.when(jnp.logical_and(kk > 0, kk < nk - 1))
        def _():
            acc_ref[...] += part

        @pl.when(kk == nk - 1)
        def _():
            o_ref[...] = (acc_ref[...] + part).astype(o_ref.dtype)

    def body(*refs):
        if not pair:
            return product(refs[0], refs[1], refs[2], refs[3:])
        first = pl.program_id(2 if mode == "nt" else 1) < half
        x1, x2, y, o_ref, acc = refs[0], refs[1], refs[2], refs[3], refs[4:]

        @pl.when(first)
        def _():
            product(*((x1, y) if mode == "nt" else (y, x1)), o_ref, acc)

        @pl.when(jnp.logical_not(first))
        def _():
            product(*((x2, y) if mode == "nt" else (y, x2)), o_ref, acc)

    if mode == "nn":
        a_spec = pl.BlockSpec((bm, bk), lambda i, j, kk: (i, kk))
        if b_slots:
            b_spec = pl.BlockSpec((None, bk, bn), lambda i, j, kk: (j // per_blocks, kk, j % per_blocks))
        else:
            b_spec = pl.BlockSpec((bk, bn), lambda i, j, kk: (kk, j))
    elif mode == "nt":
        a_spec = pl.BlockSpec((bm, bk), lambda i, j, kk: (i, kk))
        if b_slots:
            b_spec = pl.BlockSpec((None, bn, bk), lambda i, j, kk: (kk // per_blocks, j, kk % per_blocks))
        else:
            b_spec = pl.BlockSpec((bn, bk), lambda i, j, kk: (j, kk))
    else:
        a_spec = pl.BlockSpec((bk, bm), lambda i, j, kk: (kk, i))
        b_spec = pl.BlockSpec((bk, bn), lambda i, j, kk: (kk, j))
    if out_slots:
        o_spec = pl.BlockSpec((None, bm, bn), lambda i, j, kk: (j // per_blocks, i, j % per_blocks))
        out_shape = jax.ShapeDtypeStruct((N_CHIPS, m, per), out_dtype)
    else:
        o_spec = pl.BlockSpec((bm, bn), lambda i, j, kk: (i, j))
        out_shape = jax.ShapeDtypeStruct((m, n), out_dtype)
    if not pair:
        ins, in_specs = (a, b), [a_spec, b_spec]
    elif mode == "nt":
        ins = (*a, b)
        in_specs = [pl.BlockSpec((bm, bk), lambda i, j, kk: (i, jnp.minimum(kk, half - 1))),
                    pl.BlockSpec((bm, bk), lambda i, j, kk: (i, jnp.maximum(kk - half, 0))), b_spec]
    else:
        ins = (*b, a)
        in_specs = [pl.BlockSpec((bk, bn), lambda i, j, kk: (kk, jnp.minimum(j, half - 1))),
                    pl.BlockSpec((bk, bn), lambda i, j, kk: (kk, jnp.maximum(j - half, 0))), a_spec]
    return _pcall(
        body, name=name, grid=(m // bm, n // bn, nk), in_specs=in_specs, out_specs=o_spec, out_shape=out_shape,
        scratch_shapes=[pltpu.VMEM((bm, bn), F32)] if nk > 1 else [],
        compiler_params=_params("parallel", "parallel", "arbitrary"),
    )(*ins)


def _matmul_quarters(h, w_slots, order, first, count, name, prev=None, bm=1024, bn=1280):
    s, k = h.shape
    n = w_slots.shape[2]
    bm, bn = _pick(s, bm), _pick(n, bn)
    pb = n // bn

    def body(order_ref, h_ref, w_ref, *rest):
        rest[-1][...] = _dot(h_ref[...], w_ref[...])

    ins = (order, h, w_slots) + (() if prev is None else (prev,))
    return _pcall(
        body, name=name, num_scalar_prefetch=1, grid=(count * pb, s // bm),
        in_specs=[pl.BlockSpec((bm, k), lambda j, i, o: (i, 0)),
                  pl.BlockSpec((None, k, bn), lambda j, i, o: (o[first + j // pb], 0, j % pb))]
        + ([] if prev is None else [ANY_SPEC]),
        out_specs=pl.BlockSpec((bm, bn), lambda j, i, o: (i, o[first + j // pb] * pb + j % pb)),
        out_shape=jax.ShapeDtypeStruct((s, N_CHIPS * n), F32),
        input_output_aliases={} if prev is None else {3: 0},
        compiler_params=_params("arbitrary", "arbitrary"),
    )(*ins)


def _place():
    x, y, c = lax.axis_index("x"), lax.axis_index("y"), lax.axis_index("c")
    chips = [(1 - x, y), (x, 1 - y), (1 - x, 1 - y)]
    return x, y, c, chips


def _all_gather8(block, name):
    def body(x_ref, out_ref, send_sems, recv_sems, local_sem):
        x, y, c, chips = _place()
        me, sibling = (x, y, c), (x, y, 1 - c)

        def slot(px, py, pc):
            return out_ref.at[4 * px + 2 * py + pc]

        def copy(k, blk, to, src=None):
            return pltpu.make_async_remote_copy(
                src_ref=slot(*blk) if src is None else src, dst_ref=slot(*blk),
                send_sem=send_sems.at[k], recv_sem=recv_sems.at[k], device_id=to, device_id_type=MESH)

        mine = pltpu.make_async_copy(x_ref, slot(*me), local_sem)
        mine.start()
        first = [copy(0, me, sibling, src=x_ref)]
        first += [copy(1 + j, me, (*chip, c), src=x_ref) for j, chip in enumerate(chips)]
        for cp in first:
            cp.start()
        passed = [copy(4 + j, (*chip, c), sibling) for j, chip in enumerate(chips)]
        for j, chip in enumerate(chips):
            copy(1 + j, (*chip, c), me).wait_recv()
            passed[j].start()
        copy(0, sibling, me).wait_recv()
        for j, chip in enumerate(chips):
            copy(4 + j, (*chip, 1 - c), me).wait_recv()
        for cp in first + passed:
            cp.wait_send()
        mine.wait()

    return _pcall(
        body, name=name, out_shape=jax.ShapeDtypeStruct((N_DEV,) + block.shape, block.dtype),
        in_specs=[pl.BlockSpec(memory_space=pltpu.VMEM)], out_specs=pl.BlockSpec(memory_space=pltpu.VMEM),
        scratch_shapes=[pltpu.SemaphoreType.DMA((7,)), pltpu.SemaphoreType.DMA((7,)), pltpu.SemaphoreType.DMA],
        compiler_params=pltpu.CompilerParams(vmem_limit_bytes=VMEM_LIMIT_BYTES),
    )(block)


HBM_SPEC = pl.BlockSpec(memory_space=pltpu.HBM)
SEM_SPEC = pl.BlockSpec(memory_space=pltpu.SEMAPHORE)
ANY_SPEC = pl.BlockSpec(memory_space=pl.ANY)
EFFECT = pltpu.SideEffectType.DATAFLOW_SIDE_EFFECTING


def _xfer_start(name, bufs, plan, n_copies, after_last=False):
    nb = len(bufs)
    deps = (_Order.last,) if after_last and _Order.last is not None else ()
    nd = len(deps)

    def body(*refs):
        send_sems, recv_sems = refs[nb + nd], refs[nb + nd + 1]
        token = refs[nb + nd + 2 + nb]
        for k, (src, dst, dev) in enumerate(plan(refs[:nb], *_place())):
            pltpu.make_async_remote_copy(src_ref=src, dst_ref=dst, send_sem=send_sems.at[k], recv_sem=recv_sems.at[k],
                                         device_id=dev, device_id_type=MESH).start()
        token[...] = jnp.zeros_like(token)

    outs = pl.pallas_call(
        body, name=name,
        out_shape=(pltpu.SemaphoreType.DMA((n_copies,)), pltpu.SemaphoreType.DMA((n_copies,)),
                   *[pltpu.HBM(b.shape, b.dtype) for b in bufs], jax.ShapeDtypeStruct((8, LANES), F32)),
        in_specs=[HBM_SPEC] * nb + [ANY_SPEC] * nd,
        out_specs=(SEM_SPEC, SEM_SPEC, *[HBM_SPEC] * nb, pl.BlockSpec(memory_space=pltpu.VMEM)),
        input_output_aliases={i: 2 + i for i in range(nb)},
        compiler_params=pltpu.CompilerParams(has_side_effects=EFFECT),
    )(*[pltpu.with_memory_space_constraint(b, pltpu.HBM) for b in bufs], *deps)
    _Order.last = outs[-1]
    return (outs[0], outs[1]), list(outs[2:2 + nb])


def _xfer_wait(name, sems, bufs, plan):
    nb = len(bufs)

    def body(*refs):
        send_sems, recv_sems = refs[nb], refs[nb + 1]
        for k, (src, dst, dev) in enumerate(plan(refs[:nb], *_place())):
            copy = pltpu.make_async_remote_copy(src_ref=src, dst_ref=dst, send_sem=send_sems.at[k],
                                                recv_sem=recv_sems.at[k], device_id=dev, device_id_type=MESH)
            copy.wait_send()
            copy.wait_recv()

    outs = pl.pallas_call(
        body, name=name, out_shape=tuple(pltpu.HBM(b.shape, b.dtype) for b in bufs),
        in_specs=[HBM_SPEC] * nb + [SEM_SPEC, SEM_SPEC, ANY_SPEC], out_specs=tuple([HBM_SPEC] * nb),
        input_output_aliases={i: i for i in range(nb)},
        compiler_params=pltpu.CompilerParams(has_side_effects=EFFECT),
    )(*bufs, *sems, _Order.last)
    _Order.last = outs[0]
    return list(outs)


def _half(ref, c, axis):
    rows = ref.shape[axis] // 2
    return pl.ds(c * rows, rows)


def _plan_weights_ici(n):
    def plan(refs, x, y, c, chips):
        out = []
        for w in range(n):
            region = refs[w].at[2 * x + y, _half(refs[w], c, 1), :]
            out += [(region, region, (*chip, c)) for chip in chips]
        return out
    return plan


def _plan_weights_ring(n):
    def plan(refs, x, y, c, chips):
        out = []
        for w in range(n):
            region = refs[w].at[2 * x + y, _half(refs[w], c, 1), :]
            out += [(region, region, (*chip, c)) for chip in chips[:2]]
        return out
    return plan


def _plan_weights_relay(n):
    def plan(refs, x, y, c, chips):
        out = []
        for w in range(n):
            quarter_rows = refs[w].shape[1] // 4
            upper = refs[w].at[2 * x + (1 - y), pl.ds(2 * c * quarter_rows, quarter_rows), :]
            lower = refs[w].at[2 * (1 - x) + y, pl.ds((2 * c + 1) * quarter_rows, quarter_rows), :]
            out += [(upper, upper, (1 - x, y, c)), (lower, lower, (x, 1 - y, c))]
        return out
    return plan


def _plan_weights_d2d(n, which=slice(0, 3)):
    def plan(refs, x, y, c, chips):
        out = []
        for w in range(n):
            rows = _half(refs[w], c, 1)
            for chip in chips[which]:
                region = refs[w].at[2 * chip[0] + chip[1], rows, :]
                out.append((region, region, (x, y, 1 - c)))
        return out
    return plan


def _plan_grads_d2d(n):
    def plan(refs, x, y, c, chips):
        return [(refs[w].at[:, _half(refs[w], 1 - c, 1), :], refs[n + w], (x, y, 1 - c)) for w in range(n)]
    return plan


def _plan_grads_ici(n):
    def plan(refs, x, y, c, chips):
        out = []
        for w in range(n):
            out += [(refs[w].at[2 * chip[0] + chip[1]], refs[n + w].at[2 * x + y], (*chip, c)) for chip in chips]
        return out
    return plan


def _plan_final_d2d(n):
    def plan(refs, x, y, c, chips):
        out = []
        for w in range(n):
            region = refs[w].at[_half(refs[w], c, 0), :]
            out.append((region, region, (x, y, 1 - c)))
        return out
    return plan


def _stream_blocks(hr, cols):
    bc = cols if cols <= 4096 else _pick(cols, 4096)
    return _pick_rows(hr, max(16, (768 * 1024) // bc)), bc


def _pre_reduce(g, landed, place, name):
    _, rows, cols = g.shape
    hr = rows // 2
    rb, bc = _stream_blocks(hr, cols)
    nrb = hr // rb

    def body(place_ref, g_ref, l_ref, o_ref):
        o_ref[...] = (g_ref[...].astype(F32) + l_ref[...].astype(F32)).astype(o_ref.dtype)

    return _pcall(
        body, name=name, num_scalar_prefetch=1, grid=(N_CHIPS, nrb, cols // bc),
        in_specs=[pl.BlockSpec((None, rb, bc), lambda j, i, k, p: (j, p[1] * nrb + i, k)),
                  pl.BlockSpec((None, rb, bc), lambda j, i, k, p: (j, i, k))],
        out_specs=pl.BlockSpec((None, rb, bc), lambda j, i, k, p: (j, i, k)),
        out_shape=jax.ShapeDtypeStruct((N_CHIPS, hr, cols), g.dtype),
        compiler_params=_params("parallel", "parallel", "parallel"),
    )(place, g, landed)


def _sum_slots(mine, landed, place, name):
    _, hr, cols = mine.shape
    rb, bc = _stream_blocks(hr, cols)
    rb = _pick_rows(hr, max(16, rb // 2))
    nrb = hr // rb

    def body(place_ref, m_ref, l_ref, o_ref):
        chip = place_ref[0]
        own = m_ref[...].astype(F32)
        total = jnp.where(chip == 0, own, l_ref[0].astype(F32))
        for j in range(1, N_CHIPS):
            total = total + jnp.where(chip == j, own, l_ref[j].astype(F32))
        o_ref[...] = total

    return _pcall(
        body, name=name, num_scalar_prefetch=1, grid=(nrb, cols // bc),
        in_specs=[pl.BlockSpec((None, rb, bc), lambda i, k, p: (p[0], i, k)),
                  pl.BlockSpec((N_CHIPS, rb, bc), lambda i, k, p: (0, i, k))],
        out_specs=pl.BlockSpec((rb, bc), lambda i, k, p: (p[1] * nrb + i, k)),
        out_shape=jax.ShapeDtypeStruct((2 * hr, cols), F32),
        compiler_params=_params("parallel", "parallel"),
    )(place, mine, landed)


class _WeightGather:
    @staticmethod
    def zones(quarters, chip):
        return [lax.dynamic_update_slice(lax.empty((N_CHIPS,) + q.shape, BF16), q.astype(BF16)[None], (chip, 0, 0))
                for q in quarters]

    def __init__(self, tag, zones, ring=False):
        self.tag, self.n, self.ring = tag, len(zones), ring
        self.plan = _plan_weights_ring(self.n) if ring else _plan_weights_ici(self.n)
        self.sems, self.bufs = _xfer_start("wici_start_" + tag, zones, self.plan, (2 if ring else 3) * self.n,
                                           after_last=True)

    def relay(self):
        bufs = _xfer_wait("wici_wait_" + self.tag, self.sems, self.bufs, self.plan)
        self.plan = _plan_weights_relay(self.n)
        self.sems, self.bufs = _xfer_start("wrel_start_" + self.tag, bufs, self.plan, 2 * self.n)

    def pass_on(self):
        bufs = _xfer_wait(("wrel_wait_" if self.ring else "wici_wait_") + self.tag, self.sems, self.bufs, self.plan)
        self.plan = _plan_weights_d2d(self.n)
        self.sems, self.bufs = _xfer_start("wd2d_start_" + self.tag, bufs, self.plan, 3 * self.n)

    def done(self):
        return _xfer_wait("wd2d_wait_" + self.tag, self.sems, self.bufs, self.plan)

    def relay_and_pass_near(self):
        self.relay()
        self.near_plan = _plan_weights_d2d(self.n, slice(0, 2))
        self.near_sems, self.bufs = _xfer_start("wnear_start_" + self.tag, self.bufs, self.near_plan, 2 * self.n)

    def near_done(self):
        self.bufs = _xfer_wait("wnear_wait_" + self.tag, self.near_sems, self.bufs, self.near_plan)
        return self.bufs

    def pass_far(self):
        bufs = _xfer_wait("wrel_wait_" + self.tag, self.sems, self.bufs, self.plan)
        plan = _plan_weights_d2d(self.n, slice(2, 3))
        sems, bufs = _xfer_start("wfar_start_" + self.tag, bufs, plan, self.n)
        return _xfer_wait("wfar_wait_" + self.tag, sems, bufs, plan)


class _GradReduce:
    def __init__(self, tag, grads):
        self.tag, self.n = tag, len(grads)
        zones = [lax.empty((N_CHIPS, g.shape[1] // 2, g.shape[2]), g.dtype) for g in grads]
        self.plan = _plan_grads_d2d(self.n)
        self.sems, self.bufs = _xfer_start("gd2d_start_" + tag, list(grads) + zones, self.plan, self.n)

    def pair(self, place):
        n = self.n
        bufs = _xfer_wait("gd2d_wait_" + self.tag, self.sems, self.bufs, self.plan)
        self.halves = [_pre_reduce(bufs[w], bufs[n + w], place, f"pre_reduce_{self.tag}{w}") for w in range(n)]

    def cross(self, after_last=False):
        zones = [lax.empty(h.shape, h.dtype) for h in self.halves]
        self.plan = _plan_grads_ici(self.n)
        self.sems, self.bufs = _xfer_start("gici_start_" + self.tag, self.halves + zones, self.plan, 3 * self.n,
                                           after_last=after_last)

    def step(self, place):
        self.pair(place)
        self.cross()

    def join(self, place):
        n = self.n
        bufs = _xfer_wait("gici_wait_" + self.tag, self.sems, self.bufs, self.plan)
        sums = [_sum_slots(bufs[w], bufs[n + w], place, f"sum_slots_{self.tag}{w}") for w in range(n)]
        self.plan = _plan_final_d2d(n)
        self.sems, self.bufs = _xfer_start("gfin_start_" + self.tag, sums, self.plan, n)

    def done(self):
        return _xfer_wait("gfin_wait_" + self.tag, self.sems, self.bufs, self.plan)


def _ada_fwd(c_all, w_q, b_q):
    d, n = w_q.shape
    bn = _pick(n, 512)

    def body(c_ref, w_ref, b_ref, o_ref):
        cv = c_ref[...]
        act = cv * _sigmoid(cv)
        o_ref[...] = _dot(act, w_ref[...], NN, lax.Precision.HIGHEST) + b_ref[...]

    return _pcall(
        body, name="ada_fwd", grid=(n // bn,),
        in_specs=[pl.BlockSpec((N_DEV, d), lambda j: (0, 0)), pl.BlockSpec((d, bn), lambda j: (0, j)),
                  pl.BlockSpec((1, bn), lambda j: (0, j))],
        out_specs=pl.BlockSpec((N_DEV, bn), lambda j: (0, j)),
        out_shape=jax.ShapeDtypeStruct((N_DEV, n), F32), compiler_params=_params("parallel"),
    )(c_all, w_q, b_q)


def _row_spec(rb, width, col=0):
    return pl.BlockSpec((rb, width), lambda i, col=col: (i, col))


def _vec_spec(width, col=0):
    return pl.BlockSpec((1, width), lambda i, col=col: (0, col))


def _norm_mod_fwd(x, g, sc, sh, name, res=None, gt=None):
    s, d = x.shape
    rb = _pick(s, 256)
    has_res = res is not None

    def body(*refs):
        if has_res:
            x_ref, res_ref, gt_ref, g_ref, sc_ref, sh_ref, x1_ref, h_ref = refs
            xv = x_ref[...] + gt_ref[...] * res_ref[...]
            x1_ref[...] = xv
        else:
            x_ref, g_ref, sc_ref, sh_ref, h_ref = refs
            xv = x_ref[...]
        xh, _ = _rms(xv)
        h_ref[...] = (xh * g_ref[...] * (1.0 + sc_ref[...]) + sh_ref[...]).astype(BF16)

    row, vec = _row_spec(rb, d), _vec_spec(d)
    if has_res:
        ins, in_specs = (x, res, gt, g, sc, sh), [row, row, vec, vec, vec, vec]
        out_shape = [jax.ShapeDtypeStruct((s, d), F32), jax.ShapeDtypeStruct((s, d), BF16)]
        out_specs = [row, row]
    else:
        ins, in_specs = (x, g, sc, sh), [row, vec, vec, vec]
        out_shape, out_specs = jax.ShapeDtypeStruct((s, d), BF16), row
    return _pcall(body, name=name, grid=(s // rb,), in_specs=in_specs, out_specs=out_specs,
                          out_shape=out_shape, compiler_params=_params("parallel"))(*ins)


def _final_loss(x1, f, gt2, g_final, target):
    s, d = x1.shape
    rb = _pick(s, 256)

    def body(x1_ref, f_ref, gt_ref, g_ref, t_ref, dx_ref, df_ref, loss_ref, dg_ref, dgt_ref):
        first = pl.program_id(0) == 0
        fv, gt, gv = f_ref[...], gt_ref[...], g_ref[...]
        x2 = x1_ref[...] + gt * fv
        xh, r = _rms(x2)
        err = xh * gv - t_ref[...]
        blk = 0.5 * jnp.sum(jnp.sum(err * err, axis=1, keepdims=True), axis=0, keepdims=True) / d
        dy = err / d
        dxh = dy * gv
        dx = r * (dxh - xh * jnp.mean(dxh * xh, axis=-1, keepdims=True))
        dx_ref[...] = dx
        df_ref[...] = (dx * gt).astype(BF16)
        _accumulate(first, loss_ref, jnp.broadcast_to(blk, (1, LANES)))
        _accumulate(first, dg_ref, _colsum(dy * xh))
        _accumulate(first, dgt_ref, _colsum(dx * fv))

    row, vec = _row_spec(rb, d), _vec_spec(d)
    return _pcall(
        body, name="final_loss", grid=(s // rb,), in_specs=[row, row, vec, vec, row],
        out_specs=[row, row, _vec_spec(LANES), vec, vec],
        out_shape=[jax.ShapeDtypeStruct((s, d), F32), jax.ShapeDtypeStruct((s, d), BF16),
                   jax.ShapeDtypeStruct((1, LANES), F32), jax.ShapeDtypeStruct((1, d), F32),
                   jax.ShapeDtypeStruct((1, d), F32)],
        compiler_params=_params("arbitrary"),
    )(x1, f, gt2, g_final, target)


def _norm_mod_bwd(dh, xin, dres, g, sc, name, branch=None, gt=None):
    s, d = xin.shape
    rb = _pick(s, 256)
    has_branch = branch is not None

    def body(*refs):
        if has_branch:
            dh_ref, x_ref, dres_ref, g_ref, sc_ref, br_ref, gt_ref, dx_ref, dbr_ref, dsh_ref, dsc_ref, dg_ref, dgt_ref = refs
        else:
            dh_ref, x_ref, dres_ref, g_ref, sc_ref, dx_ref, dsh_ref, dsc_ref, dg_ref = refs
        first = pl.program_id(0) == 0
        gv = g_ref[...]
        xh, r = _rms(x_ref[...])
        dhv = dh_ref[...]
        dn = dhv * (1.0 + sc_ref[...])
        dxh = dn * gv
        dx = dres_ref[...] + r * (dxh - xh * jnp.mean(dxh * xh, axis=-1, keepdims=True))
        dx_ref[...] = dx
        _accumulate(first, dsh_ref, _colsum(dhv))
        _accumulate(first, dsc_ref, _colsum(dhv * xh * gv))
        _accumulate(first, dg_ref, _colsum(dn * xh))
        if has_branch:
            dbr_ref[...] = (dx * gt_ref[...]).astype(BF16)
            _accumulate(first, dgt_ref, _colsum(dx * br_ref[...]))

    row, vec = _row_spec(rb, d), _vec_spec(d)
    vec_shape = jax.ShapeDtypeStruct((1, d), F32)
    if has_branch:
        ins, in_specs = (dh, xin, dres, g, sc, branch, gt), [row, row, row, vec, vec, row, vec]
        out_specs = [row, row, vec, vec, vec, vec]
        out_shape = [jax.ShapeDtypeStruct((s, d), F32), jax.ShapeDtypeStruct((s, d), BF16)] + [vec_shape] * 4
    else:
        ins, in_specs = (dh, xin, dres, g, sc), [row, row, row, vec, vec]
        out_specs = [row, vec, vec, vec]
        out_shape = [jax.ShapeDtypeStruct((s, d), F32)] + [vec_shape] * 3
    return _pcall(body, name=name, grid=(s // rb,), in_specs=in_specs, out_specs=out_specs,
                          out_shape=out_shape, compiler_params=_params("arbitrary"))(*ins)


def _gate_fwd(pa, pb, z, b_gate, gate_col):
    s, d = pa.shape
    rb = _pick(s, 256)

    def body(pa_ref, pb_ref, za_ref, zb_ref, ba_ref, bb_ref, y_ref):
        ga = _sigmoid(za_ref[...] + ba_ref[...])
        gb = _sigmoid(zb_ref[...] + bb_ref[...])
        y_ref[...] = (ga * pa_ref[...].astype(F32) + gb * pb_ref[...].astype(F32)).astype(BF16)

    row = _row_spec(rb, d)
    return _pcall(
        body, name="gate_fwd", grid=(s // rb,),
        in_specs=[row, row, _row_spec(rb, d, gate_col), _row_spec(rb, d, gate_col + 1), _vec_spec(d, 0), _vec_spec(d, 1)],
        out_specs=row, out_shape=jax.ShapeDtypeStruct((s, d), BF16), compiler_params=_params("parallel"),
    )(pa, pb, z, z, b_gate, b_gate)


def _out_bwd(dyo, w_out, pa, pb, z, b_gate, gate_col, bm=512, bn=1024):
    s, d = dyo.shape
    bm, bn = _pick(s, bm), _pick(d, bn)
    nj = d // bn

    def body(dyo_ref, w_ref, pa_ref, pb_ref, za_ref, zb_ref, ba_ref, bb_ref,
             dpa_ref, dpb_ref, dza_ref, dzb_ref, dba_ref, dbb_ref):
        first = pl.program_id(1) == 0
        dyv = _dot(dyo_ref[...], w_ref[...], NT)
        ga = _sigmoid(za_ref[...] + ba_ref[...])
        gb = _sigmoid(zb_ref[...] + bb_ref[...])
        dpa_ref[...] = (dyv * ga).astype(BF16)
        dpb_ref[...] = (dyv * gb).astype(BF16)
        dga = dyv * pa_ref[...].astype(F32) * ga * (1.0 - ga)
        dgb = dyv * pb_ref[...].astype(F32) * gb * (1.0 - gb)
        dza_ref[...] = dga.astype(BF16)
        dzb_ref[...] = dgb.astype(BF16)
        _accumulate(first, dba_ref, _colsum(dga))
        _accumulate(first, dbb_ref, _colsum(dgb))

    tile = pl.BlockSpec((bm, bn), lambda j, i: (i, j))
    vec = pl.BlockSpec((1, bn), lambda j, i: (0, j))
    act, vec_shape = jax.ShapeDtypeStruct((s, d), BF16), jax.ShapeDtypeStruct((1, d), F32)
    return _pcall(
        body, name="out_bwd", grid=(nj, s // bm),
        in_specs=[pl.BlockSpec((bm, d), lambda j, i: (i, 0)), pl.BlockSpec((bn, d), lambda j, i: (j, 0)), tile, tile,
                  pl.BlockSpec((bm, bn), lambda j, i: (i, gate_col * nj + j)),
                  pl.BlockSpec((bm, bn), lambda j, i: (i, (gate_col + 1) * nj + j)),
                  vec, pl.BlockSpec((1, bn), lambda j, i: (0, nj + j))],
        out_specs=[tile, tile, tile, tile, vec, vec], out_shape=[act, act, act, act, vec_shape, vec_shape],
        compiler_params=_params("parallel", "arbitrary"),
    )(dyo, w_out, pa, pb, z, z, b_gate, b_gate)


def _ffn_in(h, w_fi, bm=512, bn=1408):
    s, d = h.shape
    per = w_fi.shape[2]
    ff = 2 * per
    bm, bn = _pick(s, bm), _pick(per, bn)
    per_blocks = per // bn

    def body(h_ref, wa_ref, wu_ref, a_ref, u_ref, hf_ref):
        hv = h_ref[...]
        a = _dot(hv, wa_ref[...])
        up = _dot(hv, wu_ref[...])
        a_ref[...] = a.astype(BF16)
        u_ref[...] = up.astype(BF16)
        hf_ref[...] = (a * _sigmoid(a) * up).astype(BF16)

    out = pl.BlockSpec((bm, bn), lambda j, i: (i, j))
    shape = jax.ShapeDtypeStruct((s, ff), BF16)
    return _pcall(
        body, name="ffn_in", grid=(ff // bn, s // bm),
        in_specs=[pl.BlockSpec((bm, d), lambda j, i: (i, 0)),
                  pl.BlockSpec((None, d, bn), lambda j, i: (j // per_blocks, 0, j % per_blocks)),
                  pl.BlockSpec((None, d, bn), lambda j, i: (2 + j // per_blocks, 0, j % per_blocks))],
        out_specs=[out, out, out], out_shape=[shape, shape, shape],
        compiler_params=_params("parallel", "parallel"),
    )(h, w_fi, w_fi)


def _ffn_out_bwd(dffn, w_fo, a_act, up_act, bm=512, bn=1408):
    s, d = dffn.shape
    ff = w_fo.shape[0]
    bm, bn = _pick(s, bm), _pick(ff, bn)

    def body(d_ref, w_ref, a_ref, u_ref, da_ref, du_ref):
        dhf = _dot(d_ref[...], w_ref[...], NT)
        a = a_ref[...].astype(F32)
        sa = _sigmoid(a)
        da_ref[...] = (dhf * u_ref[...].astype(F32) * sa * (1.0 + a * (1.0 - sa))).astype(BF16)
        du_ref[...] = (dhf * a * sa).astype(BF16)

    tile = pl.BlockSpec((bm, bn), lambda j, i: (i, j))
    shape = jax.ShapeDtypeStruct((s, ff), BF16)
    return _pcall(
        body, name="ffn_out_bwd", grid=(ff // bn, s // bm),
        in_specs=[pl.BlockSpec((bm, d), lambda j, i: (i, 0)), pl.BlockSpec((bn, d), lambda j, i: (j, 0)), tile, tile],
        out_specs=[tile, tile], out_shape=[shape, shape], compiler_params=_params("parallel", "parallel"),
    )(dffn, w_fo, a_act, up_act)


def _tril(n):
    return lax.broadcasted_iota(jnp.int32, (n, n), 0) >= lax.broadcasted_iota(jnp.int32, (n, n), 1)


def _gmlp_norm(v, ln_g, ln_b):
    gv = _gelu(v)
    mu = jnp.mean(gv, axis=-1, keepdims=True)
    cen = gv - mu
    rs = lax.rsqrt(jnp.mean(cen * cen, axis=-1, keepdims=True) + EPS)
    xh = cen * rs
    return xh, rs, xh * ln_g + ln_b


def _gmlp_fwd(z, ln_g, ln_b, ws, bs_t):
    s = z.shape[0]
    gw = ln_g.shape[1]
    groups, chunk, _ = ws.shape

    def body(u_ref, v_ref, lg_ref, lb_ref, ws_ref, bs_ref, ya_ref):
        gu = _gelu(u_ref[...])
        _, _, vn = _gmlp_norm(v_ref[...], lg_ref[...], lb_ref[...])
        mask = _tril(chunk)
        for g in range(groups):
            cols = slice(g * LANES, (g + 1) * LANES)
            wm = jnp.where(mask, ws_ref[g], 0.0).astype(BF16)
            sg = _dot(wm, vn[:, cols].astype(BF16)) + bs_ref[:, g:g + 1]
            ya_ref[:, cols] = (gu[:, cols] * sg).astype(BF16)

    return _pcall(
        body, name="gmlp_fwd", grid=(s // chunk,),
        in_specs=[_row_spec(chunk, gw, 0), _row_spec(chunk, gw, 1), _vec_spec(gw), _vec_spec(gw),
                  pl.BlockSpec((groups, chunk, chunk), lambda i: (0, 0, 0)), pl.BlockSpec((chunk, LANES), lambda i: (0, 0))],
        out_specs=_row_spec(chunk, gw), out_shape=jax.ShapeDtypeStruct((s, gw), BF16),
        compiler_params=_params("parallel"),
    )(z, z, ln_g, ln_b, ws, bs_t)


def _gmlp_bwd(dya, z, ln_g, ln_b, ws, bs_t):
    s = z.shape[0]
    gw = ln_g.shape[1]
    groups, chunk, _ = ws.shape

    def body(dya_ref, u_ref, v_ref, lg_ref, lb_ref, ws_ref, bs_ref, duv_ref, dws_ref, dbs_ref, dlg_ref, dlb_ref, dvn_ref):
        first = pl.program_id(0) == 0
        u, v, lg = u_ref[...], v_ref[...], lg_ref[...]
        gu = _gelu(u)
        xh, rs, vn = _gmlp_norm(v, lg, lb_ref[...])
        dyav = dya_ref[...]
        mask = _tril(chunk)
        lane = lax.broadcasted_iota(jnp.int32, (chunk, LANES), 1)
        dbs = jnp.zeros((chunk, LANES), F32)
        for g in range(groups):
            cols = slice(g * LANES, (g + 1) * LANES)
            wm = jnp.where(mask, ws_ref[g], 0.0).astype(BF16)
            vg = vn[:, cols].astype(BF16)
            sg = _dot(wm, vg) + bs_ref[:, g:g + 1]
            ds = dyav[:, cols] * gu[:, cols]
            duv_ref[:, cols] = (dyav[:, cols] * sg * _gelu_grad(u[:, cols])).astype(BF16)
            dsb = ds.astype(BF16)
            _accumulate(first, dws_ref.at[g], jnp.where(mask, _dot(dsb, vg, NT), 0.0))
            dbs = dbs + jnp.where(lane == g, jnp.sum(ds, axis=-1, keepdims=True), 0.0)
            dvn_ref[:, cols] = _dot(wm, dsb, TN)
        dvn = dvn_ref[...]
        _accumulate(first, dbs_ref, dbs)
        _accumulate(first, dlb_ref, _colsum(dvn))
        _accumulate(first, dlg_ref, _colsum(dvn * xh))
        dxh = dvn * lg
        dgv = rs * (dxh - jnp.mean(dxh, axis=-1, keepdims=True) - xh * jnp.mean(dxh * xh, axis=-1, keepdims=True))
        duv_ref[:, gw:] = (dgv * _gelu_grad(v)).astype(BF16)

    return _pcall(
        body, name="gmlp_bwd", grid=(s // chunk,),
        in_specs=[_row_spec(chunk, gw), _row_spec(chunk, gw, 0), _row_spec(chunk, gw, 1), _vec_spec(gw), _vec_spec(gw),
                  pl.BlockSpec((groups, chunk, chunk), lambda i: (0, 0, 0)), pl.BlockSpec((chunk, LANES), lambda i: (0, 0))],
        out_specs=[_row_spec(chunk, 2 * gw), pl.BlockSpec((groups, chunk, chunk), lambda i: (0, 0, 0)),
                   pl.BlockSpec((chunk, LANES), lambda i: (0, 0)), _vec_spec(gw), _vec_spec(gw)],
        out_shape=[jax.ShapeDtypeStruct((s, 2 * gw), BF16), jax.ShapeDtypeStruct((groups, chunk, chunk), F32),
                   jax.ShapeDtypeStruct((chunk, LANES), F32), jax.ShapeDtypeStruct((1, gw), F32),
                   jax.ShapeDtypeStruct((1, gw), F32)],
        scratch_shapes=[pltpu.VMEM((chunk, gw), F32)],
        compiler_params=_params("arbitrary"),
    )(dya, z, z, ln_g, ln_b, ws, bs_t)


def _lower_bound(lb_ref):
    a0, a1 = lb_ref[0:1, :], lb_ref[1:2, :]
    mx = jnp.maximum(a0, a1)
    e0, e1 = jnp.exp(a0 - mx), jnp.exp(a1 - mx)
    return e0 / (e0 + e1)


def _sum_dot(mask, x):
    hi = x.astype(BF16)
    rest = x - hi.astype(F32)
    mid = rest.astype(BF16)
    low = (rest - mid.astype(F32)).astype(BF16)
    return _dot(mask, hi) + _dot(mask, mid) + _dot(mask, low)


def _ones_where(mask):
    return jnp.where(mask, 1.0, 0.0).astype(BF16)


def _hg_masks(rows, t):
    r = lax.broadcasted_iota(jnp.int32, (rows, rows), 0)
    c = lax.broadcasted_iota(jnp.int32, (rows, rows), 1)
    same = (r // t) == (c // t)
    incl = jnp.logical_and(same, c <= r)
    upto_mid = jnp.logical_and(same, (c % t) <= t // 2)
    rev = jnp.logical_and(same, c >= r)
    return same, incl, upto_mid, rev


def _hg_block(q, fp, lb, masks):
    rows = q.shape[0]
    same, incl, upto_mid, _ = masks
    sig = _sigmoid(fp)
    f = lb + (1.0 - lb) * sig
    k = 1.0 - f
    sq = _sigmoid(q)
    qa = q * sq
    stacked = jnp.concatenate([_ones_where(m) for m in (incl, same, upto_mid)], axis=0)
    sums = _sum_dot(stacked, jnp.log(f))
    b, b_last, b_mid = sums[:rows], sums[rows:2 * rows], sums[2 * rows:]
    e_q = jnp.exp(jnp.minimum(b - b_mid, EXP_CLAMP))
    e_k = jnp.exp(jnp.minimum(b_mid - b, EXP_CLAMP))
    e_in = jnp.exp(b)
    e_out = jnp.exp(b_last - b)
    return dict(sig=sig, f=f, k=k, sq=sq, qa=qa, e_last=jnp.exp(b_last), e_q=e_q, e_k=e_k, e_in=e_in, e_out=e_out,
                q_hat=(qa * e_q).astype(BF16), k_hat=(k * e_k).astype(BF16),
                q_in=(qa * e_in).astype(BF16), k_out=k * e_out)


def _hgrn_fwd(z, hg_lb, norm_g, q_col):
    s = z.shape[0]
    hw = norm_g.shape[1]
    heads = hw // LANES
    t = HG_CHUNK
    rows = min(HG_ROWS, s)
    per_step = rows // t
    hp = min(HG_HEADS_PER_STEP, heads)
    assert heads % hp == 0 and q_col % hp == 0, (heads, q_col)
    wide = hp * LANES

    def zspec(which):
        return pl.BlockSpec((rows, wide), lambda h, r, which=which: (r, (q_col + which * heads) // hp + h))

    def body(q_ref, f_ref, i_ref, g_ref, lb_ref, ng_ref, yb_ref, o_ref, st_out_ref, st_ref, e_last_ref, inter_ref):
        @pl.when(pl.program_id(1) == 0)
        def _():
            st_ref[...] = jnp.zeros_like(st_ref)

        masks = _hg_masks(rows, t)
        for hh in range(hp):
            cols = slice(hh * LANES, (hh + 1) * LANES)
            blk = _hg_block(q_ref[:, cols], f_ref[:, cols], _lower_bound(lb_ref.at[:, cols]), masks)
            iv = i_ref[:, cols].astype(BF16)
            q_in, k_out = blk["q_in"], blk["k_out"].astype(BF16)
            e_last_ref[hh] = blk["e_last"]
            attn = jnp.where(masks[1], _dot(blk["q_hat"], blk["k_hat"], NT), 0.0).astype(BF16)
            o = _dot(attn, iv)
            grown = [_dot(iv[j * t:(j + 1) * t], k_out[j * t:(j + 1) * t], TN) for j in range(per_step)]
            st = st_ref[hh]
            for j in range(per_step):
                st_out_ref[hh, j] = st
                inter_ref[hh, j * t:(j + 1) * t, :] = _dot(q_in[j * t:(j + 1) * t], st.astype(BF16), NT)
                st = st * e_last_ref[hh, j * t:j * t + 1, :] + grown[j]
            st_ref[hh] = st
            o = o + inter_ref[hh]
            o_ref[:, cols] = o
            og = g_ref[:, cols]
            on, _ = _rms(o)
            yb_ref[:, cols] = (on * ng_ref[:, cols] * (og * _sigmoid(og))).astype(BF16)

    out_row = pl.BlockSpec((rows, wide), lambda h, r: (r, h))
    return _pcall(
        body, name="hgrn_fwd", grid=(heads // hp, s // rows),
        in_specs=[zspec(0), zspec(1), zspec(2), zspec(3),
                  pl.BlockSpec((2, wide), lambda h, r: (0, h)), pl.BlockSpec((1, wide), lambda h, r: (0, h))],
        out_specs=[out_row, out_row, pl.BlockSpec((hp, per_step, LANES, LANES), lambda h, r: (h, r, 0, 0))],
        out_shape=[jax.ShapeDtypeStruct((s, hw), BF16), jax.ShapeDtypeStruct((s, hw), F32),
                   jax.ShapeDtypeStruct((heads, s // t, LANES, LANES), F32)],
        scratch_shapes=[pltpu.VMEM((hp, LANES, LANES), F32), pltpu.VMEM((hp, rows, LANES), F32),
                        pltpu.VMEM((hp, rows, LANES), F32)],
        compiler_params=_params("parallel", "arbitrary"),
    )(z, z, z, z, hg_lb, norm_g)


def _hgrn_bwd(dyb, z, o_raw, states, hg_lb, norm_g, q_col):
    s = z.shape[0]
    hw = norm_g.shape[1]
    heads = hw // LANES
    t = HG_CHUNK
    rows = min(HG_ROWS, s)
    per_step = rows // t
    n_steps = s // rows
    hp = min(HG_HEADS_PER_STEP, heads)
    assert heads % hp == 0 and q_col % hp == 0, (heads, q_col)
    wide = hp * LANES

    def zspec(which):
        return pl.BlockSpec((rows, wide), lambda h, r, which=which: (n_steps - 1 - r, (q_col + which * heads) // hp + h))

    def body(dyb_ref, q_ref, f_ref, i_ref, g_ref, o_ref, st_in_ref, lb_ref, ng_ref,
             dq_ref, df_ref, di_ref, dg_ref, dlb_ref, dng_ref, dst_ref, acc_lb_ref, acc_ng_ref,
             e_last_ref, dq_in_ref, dk_out_ref, di_inter_ref, carry_ref):
        step = pl.program_id(1)

        @pl.when(step == 0)
        def _():
            dst_ref[...] = jnp.zeros_like(dst_ref)
            acc_lb_ref[...] = jnp.zeros_like(acc_lb_ref)
            acc_ng_ref[...] = jnp.zeros_like(acc_ng_ref)

        masks = _hg_masks(rows, t)
        same, incl, _, rev = masks
        sum_mask = jnp.concatenate([_ones_where(rev), _ones_where(same)], axis=1)
        for hh in range(hp):
            cols = slice(hh * LANES, (hh + 1) * LANES)
            lb = _lower_bound(lb_ref.at[:, cols])
            ng = ng_ref[:, cols]
            q = q_ref[:, cols]
            blk = _hg_block(q, f_ref[:, cols], lb, masks)
            iv = i_ref[:, cols].astype(BF16)
            o, og, dy = o_ref[:, cols], g_ref[:, cols], dyb_ref[:, cols]
            so = _sigmoid(og)
            on, r = _rms(o)
            acc_ng_ref[:, cols] += _colsum(dy * on * (og * so))
            dg_ref[:, cols] = (dy * on * ng * so * (1.0 + og * (1.0 - so))).astype(BF16)
            don = dy * ng * (og * so)
            do = (r * (don - on * jnp.mean(don * on, axis=-1, keepdims=True))).astype(BF16)
            q_hat, k_hat, q_in, k_out = blk["q_hat"], blk["k_hat"], blk["q_in"], blk["k_out"]
            k_out_b = k_out.astype(BF16)
            attn = jnp.where(incl, _dot(q_hat, k_hat, NT), 0.0).astype(BF16)
            d_attn = jnp.where(incl, _dot(do, iv, NT), 0.0).astype(BF16)
            di_intra = _dot(attn, do, TN)
            dq_hat = _dot(d_attn, k_hat)
            dk_hat = _dot(d_attn, q_hat, TN)
            e_last_ref[hh] = blk["e_last"]
            grown = [_dot(do[j * t:(j + 1) * t], q_in[j * t:(j + 1) * t], TN) for j in range(per_step)]
            dst = dst_ref[hh]
            for j in reversed(range(per_step)):
                rs_ = slice(j * t, (j + 1) * t)
                e_last = e_last_ref[hh, j * t:j * t + 1, :]
                st_prev, dst_b = st_in_ref[hh, j], dst.astype(BF16)
                dq_in_ref[hh, rs_, :] = _dot(do[rs_], st_prev.astype(BF16))
                dk_out_ref[hh, rs_, :] = _dot(iv[rs_], dst_b)
                di_inter_ref[hh, rs_, :] = _dot(k_out_b[rs_], dst_b, NT)
                carry_ref[hh, rs_, :] = jnp.broadcast_to(e_last * _colsum(st_prev * dst), (t, LANES))
                dst = dst * e_last + grown[j]
            dst_ref[hh] = dst
            di_ref[:, cols] = (di_intra + di_inter_ref[hh]).astype(BF16)
            dk_out = dk_out_ref[hh]
            dqa = dq_in_ref[hh] * blk["e_in"] + dq_hat * blk["e_q"]
            dk = dk_out * blk["e_out"] + dk_hat * blk["e_k"]
            db = blk["qa"] * dqa - blk["k"] * dk
            dlf = _sum_dot(sum_mask, jnp.concatenate([db, dk_out * k_out], axis=0)) + carry_ref[hh]
            dfv = dlf / blk["f"] - dk
            sig, sq = blk["sig"], blk["sq"]
            df_ref[:, cols] = (dfv * (1.0 - lb) * sig * (1.0 - sig)).astype(BF16)
            acc_lb_ref[:, cols] += _colsum(dfv * (1.0 - sig))
            dq_ref[:, cols] = (dqa * sq * (1.0 + q * (1.0 - sq))).astype(BF16)

        @pl.when(step == n_steps - 1)
        def _():
            lb = _lower_bound(lb_ref)
            d0 = acc_lb_ref[...] * lb * (1.0 - lb)
            dlb_ref[0:1, :] = d0
            dlb_ref[1:2, :] = -d0
            dng_ref[...] = acc_ng_ref[...]

    rev_row = pl.BlockSpec((rows, wide), lambda h, r: (n_steps - 1 - r, h))
    piece = jax.ShapeDtypeStruct((s, hw), BF16)
    return _pcall(
        body, name="hgrn_bwd", grid=(heads // hp, n_steps),
        in_specs=[rev_row, zspec(0), zspec(1), zspec(2), zspec(3), rev_row,
                  pl.BlockSpec((hp, per_step, LANES, LANES), lambda h, r: (h, n_steps - 1 - r, 0, 0)),
                  pl.BlockSpec((2, wide), lambda h, r: (0, h)), pl.BlockSpec((1, wide), lambda h, r: (0, h))],
        out_specs=[rev_row, rev_row, rev_row, rev_row,
                   pl.BlockSpec((2, wide), lambda h, r: (0, h)), pl.BlockSpec((1, wide), lambda h, r: (0, h))],
        out_shape=[piece, piece, piece, piece, jax.ShapeDtypeStruct((2, hw), F32), jax.ShapeDtypeStruct((1, hw), F32)],
        scratch_shapes=[pltpu.VMEM((hp, LANES, LANES), F32), pltpu.VMEM((1, wide), F32), pltpu.VMEM((1, wide), F32)]
        + [pltpu.VMEM((hp, rows, LANES), F32)] * 5,
        compiler_params=_params("parallel", "arbitrary"),
    )(dyb, z, z, z, z, o_raw, states, hg_lb, norm_g)


def _adam_update(w, m, v, g):
    m2 = ADAM_B1 * m + (1.0 - ADAM_B1) * g
    v2 = ADAM_B2 * v + (1.0 - ADAM_B2) * (g * g)
    m_hat = m2 * (1.0 / (1.0 - ADAM_B1 ** ADAM_STEP))
    v_hat = v2 * (1.0 / (1.0 - ADAM_B2 ** ADAM_STEP))
    return -ADAM_LR * (m_hat / (jnp.sqrt(v_hat) + ADAM_EPS) + ADAM_WD * w), m2, v2


def _adamw(w, m, v, parts, name, outer=False):
    rows, cols = w.shape
    bc = cols if cols <= 4096 else _pick(cols, 4096)
    rb = _pick_rows(rows, max(8, (384 * 1024) // bc), mult=8)
    if outer and rb % LANES:
        rb = rows

    def body(w_ref, m_ref, v_ref, *refs):
        g_ref, d_ref, mo_ref, vo_ref = refs[-4:]
        if outer:
            cv = refs[0][...]
            g = _dot(cv * _sigmoid(cv), refs[1][...], TN, lax.Precision.HIGHEST)
        else:
            p_ref = refs[0]
            g = p_ref[0].astype(F32)
            for p in range(1, p_ref.shape[0]):
                g = g + p_ref[p].astype(F32)
        g_ref[...] = g
        d_ref[...], mo_ref[...], vo_ref[...] = _adam_update(w_ref[...], m_ref[...], v_ref[...], g)

    blk = pl.BlockSpec((rb, bc), lambda i, j: (i, j))
    out = jax.ShapeDtypeStruct((rows, cols), F32)
    if outer:
        grad_specs = [pl.BlockSpec((N_DEV, rb), lambda i, j: (0, i)), pl.BlockSpec((N_DEV, bc), lambda i, j: (0, j))]
        grad_ins = tuple(parts)
    else:
        grad_specs = [pl.BlockSpec((parts.shape[0], rb, bc), lambda i, j: (0, i, j))]
        grad_ins = (parts,)
    return _pcall(
        body, name=name, grid=(rows // rb, cols // bc), in_specs=[blk, blk, blk] + grad_specs,
        out_specs=[blk] * 4, out_shape=[out] * 4, compiler_params=_params("parallel", "parallel"),
    )(w, m, v, *grad_ins)


SMALL = ("b_ada", "norm1_g", "b_gate", "gmlp_ln_g", "gmlp_ln_b", "gmlp_ws", "gmlp_bs", "hg_lb", "hg_norm_g",
         "norm2_g", "final_norm_g")
BIG = ("w_in", "w_branch_gmlp", "w_branch_hg", "w_out", "w_ffn_in", "w_ffn_out")
WEIGHTS = ("w_ada", "b_ada", "norm1_g", "w_in", "b_gate", "gmlp_ln_g", "gmlp_ln_b", "gmlp_ws", "gmlp_bs", "hg_lb",
           "hg_norm_g", "w_branch_gmlp", "w_branch_hg", "w_out", "norm2_g", "w_ffn_in", "w_ffn_out", "final_norm_g")


def _pack(parts):
    return jnp.concatenate([p.reshape(-1, LANES) for p in parts], axis=0)


def _step(x, c, loss_target, w, m, v):
    s, d = x.shape[1], x.shape[2]
    gw = w["gmlp_ln_g"].shape[-1]
    hw = w["hg_norm_g"].shape[-1]
    x2d, tgt = x[0], loss_target[0]
    mx, my, mc = lax.axis_index("x"), lax.axis_index("y"), lax.axis_index("c")
    chip = 2 * mx + my
    dev = 2 * chip + mc
    q_col = 2 * gw // LANES
    gate_col = (2 * gw + 4 * hw) // d
    place = jnp.stack([chip, mc]).astype(jnp.int32)
    _Order.last = None

    c_all = _all_gather8(c.reshape(-1, LANES), "gather_c").reshape(N_DEV, d)
    gather_in = _WeightGather("in", _WeightGather.zones([w["w_in"][0]], chip), ring=True)
    zones_mix = _WeightGather.zones([w[n][0] for n in ("w_branch_gmlp", "w_branch_hg", "w_out")], chip)
    zones_fi = _WeightGather.zones([w["w_ffn_in"][0]], chip)
    zones_fo = _WeightGather.zones([w["w_ffn_out"][0]], chip)
    _Order.also = (*zones_mix, *zones_fi, *zones_fo)
    n_ada = w["w_ada"].shape[-1]
    b_ada_q = lax.dynamic_slice(w["b_ada"], (0, chip * n_ada), (1, n_ada))
    mod_q = _ada_fwd(c_all, w["w_ada"][0], b_ada_q)
    mod_all = _all_gather8(mod_q, "gather_mod")
    mod = lax.dynamic_index_in_dim(mod_all, dev, axis=1, keepdims=False)[::2].reshape(1, 6 * d)
    sh1, sc1, gt1, sh2, sc2, gt2 = [mod[:, i * d:(i + 1) * d] for i in range(6)]

    gather_in.relay_and_pass_near()
    gather_mix = _WeightGather("mix", zones_mix)
    gather_fi = _WeightGather("fi", zones_fi, ring=True)

    norm1_g, norm2_g, final_g = w["norm1_g"], w["norm2_g"], w["final_norm_g"].reshape(1, d)
    ln_g, ln_b = w["gmlp_ln_g"], w["gmlp_ln_b"]
    ws = w["gmlp_ws"][0]
    groups = ws.shape[0]
    bs_t = jnp.pad(w["gmlp_bs"][0].T, ((0, 0), (0, LANES - groups)))
    hg_lb, hg_ng, b_gate = w["hg_lb"], w["hg_norm_g"], w["b_gate"]

    h1 = _norm_mod_fwd(x2d, norm1_g, sc1, sh1, "norm1_fwd")
    order = jnp.stack([chip, 2 * (1 - mx) + my, 2 * mx + (1 - my), 2 * (1 - mx) + (1 - my)]).astype(jnp.int32)
    w_in, = gather_in.near_done()
    z = _matmul_quarters(h1, w_in, order, 0, 3, "mm_z_near")
    w_in, = gather_in.pass_far()
    z = _matmul_quarters(h1, w_in, order, 3, 1, "mm_z_far", prev=z)
    gather_mix.pass_on()
    ya = _gmlp_fwd(z, ln_g, ln_b, ws, bs_t)
    yb, o_raw, states = _hgrn_fwd(z, hg_lb, hg_ng, q_col)
    gather_fi.relay()
    gather_fo = _WeightGather("fo", zones_fo)
    w_bg, w_bh, w_out = gather_mix.done()
    w_out = w_out.reshape(-1, w_out.shape[-1])
    pa = _matmul(ya, w_bg, mode="nn", name="mm_pa", out_dtype=BF16, b_slots=True)
    pb = _matmul(yb, w_bh, mode="nn", name="mm_pb", out_dtype=BF16, b_slots=True)
    y = _gate_fwd(pa, pb, z, b_gate, gate_col)
    gather_fi.pass_on()
    yo = _matmul(y, w_out, mode="nn", name="mm_yo", out_dtype=F32)
    x1, h2 = _norm_mod_fwd(x2d, norm2_g, sc2, sh2, "norm2_fwd", res=yo, gt=gt1)
    w_fi, = gather_fi.done()
    a_act, up_act, hf = _ffn_in(h2, w_fi)
    gather_fo.pass_on()
    w_fo, = gather_fo.done()
    w_fo = w_fo.reshape(-1, w_fo.shape[-1])
    ffn = _matmul(hf, w_fo, mode="nn", name="mm_ffn", out_dtype=F32)
    dx2, dffn, loss_row, d_final_g, d_gt2 = _final_loss(x1, ffn, gt2, final_g, tgt)

    g_fo = _matmul(hf, dffn, mode="tn", name="mm_g_fo", out_dtype=BF16, bm=1408)
    daup = tuple(_ffn_out_bwd(dffn, w_fo, a_act, up_act))
    g_fi = _matmul(h2, daup, mode="tn", name="mm_g_fi", out_dtype=BF16, out_slots=True, bn=1408)
    red_ffn = _GradReduce("ffn", [g_fo.reshape(N_CHIPS, -1, g_fo.shape[-1]), g_fi])
    dh2 = _matmul(daup, w_fi, mode="nt", name="mm_dh2", out_dtype=F32, b_slots=True)
    red_ffn.step(place)
    dx1, dyo, d_sh2, d_sc2, d_norm2, d_gt1 = _norm_mod_bwd(dh2, x1, dx2, norm2_g, sc2, "norm2_bwd", branch=yo, gt=gt1)
    g_out = _matmul(y, dyo, mode="tn", name="mm_g_out", out_dtype=BF16)
    dpa, dpb, dz_ga, dz_gb, d_b_ga, d_b_gb = _out_bwd(dyo, w_out, pa, pb, z, b_gate, gate_col)
    g_bg = _matmul(ya, dpa, mode="tn", name="mm_g_bg", out_dtype=BF16, out_slots=True)
    g_bh = _matmul(yb, dpb, mode="tn", name="mm_g_bh", out_dtype=BF16, out_slots=True)
    red_mix = _GradReduce("mix", [g_out.reshape(N_CHIPS, -1, g_out.shape[-1]), g_bg, g_bh])
    dya = _matmul(dpa, w_bg, mode="nt", name="mm_dya", out_dtype=F32, b_slots=True)
    dyb = _matmul(dpb, w_bh, mode="nt", name="mm_dyb", out_dtype=F32, b_slots=True)
    red_mix.step(place)
    dz_uv, d_ws, d_bs_t, d_ln_g, d_ln_b = _gmlp_bwd(dya, z, ln_g, ln_b, ws, bs_t)
    dz_q, dz_f, dz_i, dz_g, d_hg_lb, d_hg_ng = _hgrn_bwd(dyb, z, o_raw, states, hg_lb, hg_ng, q_col)
    dz = jnp.concatenate([dz_uv, dz_q, dz_f, dz_i, dz_g, dz_ga, dz_gb], axis=1)
    g_in = _matmul(h1, dz, mode="tn", name="mm_g_in", out_dtype=BF16, out_slots=True, bn=1280)
    red_in = _GradReduce("in", [g_in])
    dh1 = _matmul(dz, w_in, mode="nt", name="mm_dh1", out_dtype=F32, b_slots=True)
    red_in.pair(place)
    grad_x, d_sh1, d_sc1, d_norm1 = _norm_mod_bwd(dh1, x2d, dx1, norm1_g, sc1, "norm1_bwd")

    d_mod = jnp.concatenate([d_sh1, d_sc1, d_gt1, d_sh2, d_sc2, d_gt2], axis=1)
    small_part = {"b_ada": d_mod, "norm1_g": d_norm1, "b_gate": jnp.concatenate([d_b_ga, d_b_gb], axis=1), "gmlp_ln_g": d_ln_g, "gmlp_ln_b": d_ln_b,
                  "gmlp_ws": d_ws, "gmlp_bs": d_bs_t[:, :groups].T, "hg_lb": d_hg_lb, "hg_norm_g": d_hg_ng,
                  "norm2_g": d_norm2, "final_norm_g": d_final_g}
    small_all = _all_gather8(_pack([small_part[n] for n in SMALL]), "gather_small")
    red_in.cross(after_last=True)
    d_mod_all = small_all[:, :6 * d // LANES].reshape(N_DEV, 6 * d)
    d_mod_q = lax.dynamic_slice(d_mod_all, (0, chip * n_ada), (N_DEV, n_ada))

    grad, delta, new_m, new_v = {}, {}, {}, {}

    def update(n, parts, outer=False):
        outs = _adamw(w[n][0], m[n][0], v[n][0], parts, "adamw_" + n, outer=outer)
        grad[n], delta[n], new_m[n], new_v[n] = [o[None] for o in outs]

    update("w_ada", (c_all, d_mod_q), outer=True)
    outs = _adamw(_pack([w[n] for n in SMALL]), _pack([m[n] for n in SMALL]), _pack([v[n] for n in SMALL]),
                  small_all, "adamw_small")
    red_ffn.join(place)
    red_mix.join(place)
    g_fo, g_fi = red_ffn.done()
    update("w_ffn_out", g_fo[None])
    update("w_ffn_in", g_fi[None])
    red_in.join(place)
    g_out, g_bg, g_bh = red_mix.done()
    update("w_out", g_out[None])
    update("w_branch_gmlp", g_bg[None])
    update("w_branch_hg", g_bh[None])
    update("w_in", red_in.done()[0][None])
    row = 0
    for n in SMALL:
        cnt = w[n].size // LANES
        for dst, o in zip((grad, delta, new_m, new_v), outs):
            dst[n] = o[row:row + cnt].reshape(w[n].shape)
        row += cnt

    loss = lax.psum(loss_row[0, 0], ("x", "y", "c"))
    return (loss, grad_x[None], *[grad[n] for n in WEIGHTS], *[delta[n] for n in WEIGHTS],
            *[new_m[n] for n in WEIGHTS], *[new_v[n] for n in WEIGHTS])


def kernel(x, c, w_ada, b_ada, norm1_g, w_in, b_gate, gmlp_ln_g, gmlp_ln_b, gmlp_ws, gmlp_bs, hg_lb, hg_norm_g, w_branch_gmlp, w_branch_hg, w_out, norm2_g, w_ffn_in, w_ffn_out, final_norm_g, loss_target, m_w_ada, m_b_ada, m_norm1_g, m_w_in, m_b_gate, m_gmlp_ln_g, m_gmlp_ln_b, m_gmlp_ws, m_gmlp_bs, m_hg_lb, m_hg_norm_g, m_w_branch_gmlp, m_w_branch_hg, m_w_out, m_norm2_g, m_w_ffn_in, m_w_ffn_out, m_final_norm_g, v_w_ada, v_b_ada, v_norm1_g, v_w_in, v_b_gate, v_gmlp_ln_g, v_gmlp_ln_b, v_gmlp_ws, v_gmlp_bs, v_hg_lb, v_hg_norm_g, v_w_branch_gmlp, v_w_branch_hg, v_w_out, v_norm2_g, v_w_ffn_in, v_w_ffn_out, v_final_norm_g):
    w = dict(w_ada=w_ada, b_ada=b_ada, norm1_g=norm1_g, w_in=w_in, b_gate=b_gate, gmlp_ln_g=gmlp_ln_g,
             gmlp_ln_b=gmlp_ln_b, gmlp_ws=gmlp_ws, gmlp_bs=gmlp_bs, hg_lb=hg_lb, hg_norm_g=hg_norm_g,
             w_branch_gmlp=w_branch_gmlp, w_branch_hg=w_branch_hg, w_out=w_out, norm2_g=norm2_g,
             w_ffn_in=w_ffn_in, w_ffn_out=w_ffn_out, final_norm_g=final_norm_g)
    m = dict(w_ada=m_w_ada, b_ada=m_b_ada, norm1_g=m_norm1_g, w_in=m_w_in, b_gate=m_b_gate, gmlp_ln_g=m_gmlp_ln_g,
             gmlp_ln_b=m_gmlp_ln_b, gmlp_ws=m_gmlp_ws, gmlp_bs=m_gmlp_bs, hg_lb=m_hg_lb, hg_norm_g=m_hg_norm_g,
             w_branch_gmlp=m_w_branch_gmlp, w_branch_hg=m_w_branch_hg, w_out=m_w_out, norm2_g=m_norm2_g,
             w_ffn_in=m_w_ffn_in, w_ffn_out=m_w_ffn_out, final_norm_g=m_final_norm_g)
    v = dict(w_ada=v_w_ada, b_ada=v_b_ada, norm1_g=v_norm1_g, w_in=v_w_in, b_gate=v_b_gate, gmlp_ln_g=v_gmlp_ln_g,
             gmlp_ln_b=v_gmlp_ln_b, gmlp_ws=v_gmlp_ws, gmlp_bs=v_gmlp_bs, hg_lb=v_hg_lb, hg_norm_g=v_hg_norm_g,
             w_branch_gmlp=v_w_branch_gmlp, w_branch_hg=v_w_branch_hg, w_out=v_w_out, norm2_g=v_norm2_g,
             w_ffn_in=v_w_ffn_in, w_ffn_out=v_w_ffn_out, final_norm_g=v_final_norm_g)
    return _step(x, c, loss_target, w, m, v)
```

```python
import functools

import jax
import jax.numpy as jnp
from jax import lax
from jax.experimental import pallas as pl
from jax.experimental.pallas import tpu as pltpu

F32 = jnp.float32
BF16 = jnp.bfloat16
EPS = 1e-6
LANES = 128
N_CHIPS = 4
N_DEV = 8
VMEM_LIMIT_BYTES = 56 * 1024 * 1024
HG_CHUNK = 32
HG_ROWS = 256
HG_HEADS_PER_STEP = 8
EXP_CLAMP = 80.0
ADAM_LR, ADAM_B1, ADAM_B2, ADAM_EPS, ADAM_WD, ADAM_STEP = 0.001, 0.9, 0.999, 1e-08, 0.01, 10
MESH = pl.DeviceIdType.MESH

NN = (((1,), (0,)), ((), ()))
NT = (((1,), (1,)), ((), ()))
TN = (((0,), (0,)), ((), ()))


def _dot(a, b, dims=NN, precision=None):
    return lax.dot_general(a, b, dims, precision=precision, preferred_element_type=F32)


def _params(*semantics):
    return pltpu.CompilerParams(dimension_semantics=semantics, vmem_limit_bytes=VMEM_LIMIT_BYTES)


class _Order:
    last = None
    also = ()


def _pcall(body, *, in_specs, out_specs, grid=(), scratch_shapes=(), num_scalar_prefetch=0, **kw):
    def run(*ins):
        deps = (() if _Order.last is None else (_Order.last,)) + tuple(_Order.also)
        _Order.also = ()
        n_in, n_dep = len(ins), len(deps)

        def wrapped(*refs):
            body(*refs[:n_in], *refs[n_in + n_dep:])

        specs = list(in_specs) + [pl.BlockSpec(memory_space=pl.ANY)] * n_dep
        if num_scalar_prefetch:
            grid_spec = pltpu.PrefetchScalarGridSpec(
                num_scalar_prefetch=num_scalar_prefetch, grid=grid, in_specs=specs, out_specs=out_specs,
                scratch_shapes=scratch_shapes)
            outs = pl.pallas_call(wrapped, grid_spec=grid_spec, **kw)(*ins, *deps)
        else:
            outs = pl.pallas_call(wrapped, grid=grid, in_specs=specs, out_specs=out_specs,
                                  scratch_shapes=scratch_shapes, **kw)(*ins, *deps)
        _Order.last = jax.tree.leaves(outs)[0]
        return outs

    return run


def _pick_rows(dim, pref, mult=16):
    best = None
    for cand in range(mult, min(dim, pref) + 1, mult):
        if dim % cand == 0:
            best = cand
    assert best is not None, (dim, pref)
    return best


def _pick(dim, pref):
    if dim <= pref:
        return dim
    best = None
    for cand in range(LANES, pref + 1, LANES):
        if dim % cand == 0:
            best = cand
    assert best is not None, (dim, pref)
    return best


def _sigmoid(x):
    return 1.0 / (1.0 + jnp.exp(-x))


def _gelu(x):
    c = 0.7978845608028654
    return 0.5 * x * (1.0 + jnp.tanh(c * (x + 0.044715 * x * x * x)))


def _gelu_grad(x):
    c = 0.7978845608028654
    t = jnp.tanh(c * (x + 0.044715 * x * x * x))
    return 0.5 * (1.0 + t) + 0.5 * x * (1.0 - t * t) * c * (1.0 + 3.0 * 0.044715 * x * x)


def _rms(x):
    r = lax.rsqrt(jnp.mean(x * x, axis=-1, keepdims=True) + EPS)
    return x * r, r


def _colsum(x):
    return jnp.sum(x, axis=0, keepdims=True)


def _accumulate(first, ref, val):
    @pl.when(first)
    def _():
        ref[...] = val

    @pl.when(jnp.logical_not(first))
    def _():
        ref[...] += val


def _matmul(a, b, *, mode, name, out_dtype, b_slots=False, out_slots=False, bm=1024, bn=1024, bk=2816):
    pair = a if isinstance(a, tuple) else b if isinstance(b, tuple) else None
    if mode == "nn":
        m, k = a.shape
        n = b.shape[2] * N_CHIPS if b_slots else b.shape[1]
        per = b.shape[2] if b_slots else n
    elif mode == "nt":
        m, k = (a[0].shape[0], 2 * a[0].shape[1]) if pair else a.shape
        n = b.shape[1] if b_slots else b.shape[0]
        per = b.shape[2] if b_slots else k
    else:
        k, m = a.shape
        n = 2 * b[0].shape[1] if pair else b.shape[1]
        per = n // N_CHIPS if out_slots else n
    bm = _pick(m, bm)
    if mode == "nt":
        bn, bk = _pick(n, bn), _pick(per, bk)
    else:
        bn, bk = _pick(per, bn), _pick(k, bk)
    nk = k // bk
    per_blocks = per // (bk if mode == "nt" else bn)
    dims = {"nn": NN, "nt": NT, "tn": TN}[mode]
    half = (nk if mode == "nt" else n // bn) // 2

    def product(a_ref, b_ref, o_ref, acc):
        part = _dot(a_ref[...], b_ref[...], dims)
        if nk == 1:
            o_ref[...] = part.astype(o_ref.dtype)
            return
        acc_ref, kk = acc[0], pl.program_id(2)

        @pl.when(kk == 0)
        def _():
            acc_ref[...] = part

        @pl.when(jnp.logical_and(kk > 0, kk < nk - 1))
        def _():
            acc_ref[...] += part

        @pl.when(kk == nk - 1)
        def _():
            o_ref[...] = (acc_ref[...] + part).astype(o_ref.dtype)

    def body(*refs):
        if not pair:
            return product(refs[0], refs[1], refs[2], refs[3:])
        first = pl.program_id(2 if mode == "nt" else 1) < half
        x1, x2, y, o_ref, acc = refs[0], refs[1], refs[2], refs[3], refs[4:]

        @pl.when(first)
        def _():
            product(*((x1, y) if mode == "nt" else (y, x1)), o_ref, acc)

        @pl.when(jnp.logical_not(first))
        def _():
            product(*((x2, y) if mode == "nt" else (y, x2)), o_ref, acc)

    if mode == "nn":
        a_spec = pl.BlockSpec((bm, bk), lambda i, j, kk: (i, kk))
        if b_slots:
            b_spec = pl.BlockSpec((None, bk, bn), lambda i, j, kk: (j // per_blocks, kk, j % per_blocks))
        else:
            b_spec = pl.BlockSpec((bk, bn), lambda i, j, kk: (kk, j))
    elif mode == "nt":
        a_spec = pl.BlockSpec((bm, bk), lambda i, j, kk: (i, kk))
        if b_slots:
            b_spec = pl.BlockSpec((None, bn, bk), lambda i, j, kk: (kk // per_blocks, j, kk % per_blocks))
        else:
            b_spec = pl.BlockSpec((bn, bk), lambda i, j, kk: (j, kk))
    else:
        a_spec = pl.BlockSpec((bk, bm), lambda i, j, kk: (kk, i))
        b_spec = pl.BlockSpec((bk, bn), lambda i, j, kk: (kk, j))
    if out_slots:
        o_spec = pl.BlockSpec((None, bm, bn), lambda i, j, kk: (j // per_blocks, i, j % per_blocks))
        out_shape = jax.ShapeDtypeStruct((N_CHIPS, m, per), out_dtype)
    else:
        o_spec = pl.BlockSpec((bm, bn), lambda i, j, kk: (i, j))
        out_shape = jax.ShapeDtypeStruct((m, n), out_dtype)
    if not pair:
        ins, in_specs = (a, b), [a_spec, b_spec]
    elif mode == "nt":
        ins = (*a, b)
        in_specs = [pl.BlockSpec((bm, bk), lambda i, j, kk: (i, jnp.minimum(kk, half - 1))),
                    pl.BlockSpec((bm, bk), lambda i, j, kk: (i, jnp.maximum(kk - half, 0))), b_spec]
    else:
        ins = (*b, a)
        in_specs = [pl.BlockSpec((bk, bn), lambda i, j, kk: (kk, jnp.minimum(j, half - 1))),
                    pl.BlockSpec((bk, bn), lambda i, j, kk: (kk, jnp.maximum(j - half, 0))), a_spec]
    return _pcall(
        body, name=name, grid=(m // bm, n // bn, nk), in_specs=in_specs, out_specs=o_spec, out_shape=out_shape,
        scratch_shapes=[pltpu.VMEM((bm, bn), F32)] if nk > 1 else [],
        compiler_params=_params("parallel", "parallel", "arbitrary"),
    )(*ins)


def _matmul_quarters(h, w_slots, order, first, count, name, prev=None, bm=1024, bn=1280):
    s, k = h.shape
    n = w_slots.shape[2]
    bm, bn = _pick(s, bm), _pick(n, bn)
    pb = n // bn

    def body(order_ref, h_ref, w_ref, *rest):
        rest[-1][...] = _dot(h_ref[...], w_ref[...])

    ins = (order, h, w_slots) + (() if prev is None else (prev,))
    return _pcall(
        body, name=name, num_scalar_prefetch=1, grid=(count * pb, s // bm),
        in_specs=[pl.BlockSpec((bm, k), lambda j, i, o: (i, 0)),
                  pl.BlockSpec((None, k, bn), lambda j, i, o: (o[first + j // pb], 0, j % pb))]
        + ([] if prev is None else [ANY_SPEC]),
        out_specs=pl.BlockSpec((bm, bn), lambda j, i, o: (i, o[first + j // pb] * pb + j % pb)),
        out_shape=jax.ShapeDtypeStruct((s, N_CHIPS * n), F32),
        input_output_aliases={} if prev is None else {3: 0},
        compiler_params=_params("arbitrary", "arbitrary"),
    )(*ins)


def _place():
    x, y, c = lax.axis_index("x"), lax.axis_index("y"), lax.axis_index("c")
    chips = [(1 - x, y), (x, 1 - y), (1 - x, 1 - y)]
    return x, y, c, chips


def _all_gather8(block, name):
    def body(x_ref, out_ref, send_sems, recv_sems, local_sem):
        x, y, c, chips = _place()
        me, sibling = (x, y, c), (x, y, 1 - c)

        def slot(px, py, pc):
            return out_ref.at[4 * px + 2 * py + pc]

        def copy(k, blk, to, src=None):
            return pltpu.make_async_remote_copy(
                src_ref=slot(*blk) if src is None else src, dst_ref=slot(*blk),
                send_sem=send_sems.at[k], recv_sem=recv_sems.at[k], device_id=to, device_id_type=MESH)

        mine = pltpu.make_async_copy(x_ref, slot(*me), local_sem)
        mine.start()
        first = [copy(0, me, sibling, src=x_ref)]
        first += [copy(1 + j, me, (*chip, c), src=x_ref) for j, chip in enumerate(chips)]
        for cp in first:
            cp.start()
        passed = [copy(4 + j, (*chip, c), sibling) for j, chip in enumerate(chips)]
        for j, chip in enumerate(chips):
            copy(1 + j, (*chip, c), me).wait_recv()
            passed[j].start()
        copy(0, sibling, me).wait_recv()
        for j, chip in enumerate(chips):
            copy(4 + j, (*chip, 1 - c), me).wait_recv()
        for cp in first + passed:
            cp.wait_send()
        mine.wait()

    return _pcall(
        body, name=name, out_shape=jax.ShapeDtypeStruct((N_DEV,) + block.shape, block.dtype),
        in_specs=[pl.BlockSpec(memory_space=pltpu.VMEM)], out_specs=pl.BlockSpec(memory_space=pltpu.VMEM),
        scratch_shapes=[pltpu.SemaphoreType.DMA((7,)), pltpu.SemaphoreType.DMA((7,)), pltpu.SemaphoreType.DMA],
        compiler_params=pltpu.CompilerParams(vmem_limit_bytes=VMEM_LIMIT_BYTES),
    )(block)


HBM_SPEC = pl.BlockSpec(memory_space=pltpu.HBM)
SEM_SPEC = pl.BlockSpec(memory_space=pltpu.SEMAPHORE)
ANY_SPEC = pl.BlockSpec(memory_space=pl.ANY)
EFFECT = pltpu.SideEffectType.DATAFLOW_SIDE_EFFECTING


def _xfer_start(name, bufs, plan, n_copies, after_last=False):
    nb = len(bufs)
    deps = (_Order.last,) if after_last and _Order.last is not None else ()
    nd = len(deps)

    def body(*refs):
        send_sems, recv_sems = refs[nb + nd], refs[nb + nd + 1]
        token = refs[nb + nd + 2 + nb]
        for k, (src, dst, dev) in enumerate(plan(refs[:nb], *_place())):
            pltpu.make_async_remote_copy(src_ref=src, dst_ref=dst, send_sem=send_sems.at[k], recv_sem=recv_sems.at[k],
                                         device_id=dev, device_id_type=MESH).start()
        token[...] = jnp.zeros_like(token)

    outs = pl.pallas_call(
        body, name=name,
        out_shape=(pltpu.SemaphoreType.DMA((n_copies,)), pltpu.SemaphoreType.DMA((n_copies,)),
                   *[pltpu.HBM(b.shape, b.dtype) for b in bufs], jax.ShapeDtypeStruct((8, LANES), F32)),
        in_specs=[HBM_SPEC] * nb + [ANY_SPEC] * nd,
        out_specs=(SEM_SPEC, SEM_SPEC, *[HBM_SPEC] * nb, pl.BlockSpec(memory_space=pltpu.VMEM)),
        input_output_aliases={i: 2 + i for i in range(nb)},
        compiler_params=pltpu.CompilerParams(has_side_effects=EFFECT),
    )(*[pltpu.with_memory_space_constraint(b, pltpu.HBM) for b in bufs], *deps)
    _Order.last = outs[-1]
    return (outs[0], outs[1]), list(outs[2:2 + nb])


def _xfer_wait(name, sems, bufs, plan):
    nb = len(bufs)

    def body(*refs):
        send_sems, recv_sems = refs[nb], refs[nb + 1]
        for k, (src, dst, dev) in enumerate(plan(refs[:nb], *_place())):
            copy = pltpu.make_async_remote_copy(src_ref=src, dst_ref=dst, send_sem=send_sems.at[k],
                                                recv_sem=recv_sems.at[k], device_id=dev, device_id_type=MESH)
            copy.wait_send()
            copy.wait_recv()

    outs = pl.pallas_call(
        body, name=name, out_shape=tuple(pltpu.HBM(b.shape, b.dtype) for b in bufs),
        in_specs=[HBM_SPEC] * nb + [SEM_SPEC, SEM_SPEC, ANY_SPEC], out_specs=tuple([HBM_SPEC] * nb),
        input_output_aliases={i: i for i in range(nb)},
        compiler_params=pltpu.CompilerParams(has_side_effects=EFFECT),
    )(*bufs, *sems, _Order.last)
    _Order.last = outs[0]
    return list(outs)


def _xfer_hand_over(name, sems, bufs, plan, next_plans):
    nb, n_next = len(bufs), len(next_plans)

    def body(*refs):
        send_sems, recv_sems = refs[nb], refs[nb + 1]
        outs = refs[nb + 3:]
        place = _place()
        for k, (src, dst, dev) in enumerate(plan(refs[:nb], *place)):
            copy = pltpu.make_async_remote_copy(src_ref=src, dst_ref=dst, send_sem=send_sems.at[k],
                                                recv_sem=recv_sems.at[k], device_id=dev, device_id_type=MESH)
            copy.wait_send()
            copy.wait_recv()
        for p, (next_plan, _) in enumerate(next_plans):
            for k, (src, dst, dev) in enumerate(next_plan(refs[:nb], *place)):
                pltpu.make_async_remote_copy(src_ref=src, dst_ref=dst, send_sem=outs[2 * p].at[k],
                                             recv_sem=outs[2 * p + 1].at[k], device_id=dev, device_id_type=MESH).start()
        outs[-1][...] = jnp.zeros_like(outs[-1])

    sem_shapes = [pltpu.SemaphoreType.DMA((copies,)) for _, copies in next_plans for _ in range(2)]
    outs = pl.pallas_call(
        body, name=name,
        out_shape=(*sem_shapes, *[pltpu.HBM(b.shape, b.dtype) for b in bufs], jax.ShapeDtypeStruct((8, LANES), F32)),
        in_specs=[HBM_SPEC] * nb + [SEM_SPEC, SEM_SPEC, ANY_SPEC],
        out_specs=(*[SEM_SPEC] * (2 * n_next), *[HBM_SPEC] * nb, pl.BlockSpec(memory_space=pltpu.VMEM)),
        input_output_aliases={i: 2 * n_next + i for i in range(nb)},
        compiler_params=pltpu.CompilerParams(has_side_effects=EFFECT),
    )(*bufs, *sems, _Order.last)
    _Order.last = outs[-1]
    return [(outs[2 * p], outs[2 * p + 1]) for p in range(n_next)], list(outs[2 * n_next:2 * n_next + nb])


def _half(ref, c, axis):
    rows = ref.shape[axis] // 2
    return pl.ds(c * rows, rows)


def _plan_weights_ici(n):
    def plan(refs, x, y, c, chips):
        out = []
        for w in range(n):
            region = refs[w].at[2 * x + y, _half(refs[w], c, 1), :]
            out += [(region, region, (*chip, c)) for chip in chips]
        return out
    return plan


def _plan_weights_ring(n):
    def plan(refs, x, y, c, chips):
        out = []
        for w in range(n):
            region = refs[w].at[2 * x + y, _half(refs[w], c, 1), :]
            out += [(region, region, (*chip, c)) for chip in chips[:2]]
        return out
    return plan


def _plan_weights_relay(n):
    def plan(refs, x, y, c, chips):
        out = []
        for w in range(n):
            quarter_rows = refs[w].shape[1] // 4
            upper = refs[w].at[2 * x + (1 - y), pl.ds(2 * c * quarter_rows, quarter_rows), :]
            lower = refs[w].at[2 * (1 - x) + y, pl.ds((2 * c + 1) * quarter_rows, quarter_rows), :]
            out += [(upper, upper, (1 - x, y, c)), (lower, lower, (x, 1 - y, c))]
        return out
    return plan


def _plan_weights_d2d(n, which=slice(0, 3)):
    def plan(refs, x, y, c, chips):
        out = []
        for w in range(n):
            rows = _half(refs[w], c, 1)
            for chip in chips[which]:
                region = refs[w].at[2 * chip[0] + chip[1], rows, :]
                out.append((region, region, (x, y, 1 - c)))
        return out
    return plan


def _plan_grads_d2d(n):
    def plan(refs, x, y, c, chips):
        return [(refs[w].at[:, _half(refs[w], 1 - c, 1), :], refs[n + w], (x, y, 1 - c)) for w in range(n)]
    return plan


def _plan_grads_ici(n):
    def plan(refs, x, y, c, chips):
        out = []
        for w in range(n):
            out += [(refs[w].at[2 * chip[0] + chip[1]], refs[n + w].at[2 * x + y], (*chip, c)) for chip in chips]
        return out
    return plan


def _plan_final_d2d(n):
    def plan(refs, x, y, c, chips):
        out = []
        for w in range(n):
            region = refs[w].at[_half(refs[w], c, 0), :]
            out.append((region, region, (x, y, 1 - c)))
        return out
    return plan


def _stream_blocks(hr, cols):
    bc = cols if cols <= 4096 else _pick(cols, 4096)
    return _pick_rows(hr, max(16, (768 * 1024) // bc)), bc


def _pre_reduce(g, landed, place, name):
    _, rows, cols = g.shape
    hr = rows // 2
    rb, bc = _stream_blocks(hr, cols)
    nrb = hr // rb

    def body(place_ref, g_ref, l_ref, o_ref):
        o_ref[...] = (g_ref[...].astype(F32) + l_ref[...].astype(F32)).astype(o_ref.dtype)

    return _pcall(
        body, name=name, num_scalar_prefetch=1, grid=(N_CHIPS, nrb, cols // bc),
        in_specs=[pl.BlockSpec((None, rb, bc), lambda j, i, k, p: (j, p[1] * nrb + i, k)),
                  pl.BlockSpec((None, rb, bc), lambda j, i, k, p: (j, i, k))],
        out_specs=pl.BlockSpec((None, rb, bc), lambda j, i, k, p: (j, i, k)),
        out_shape=jax.ShapeDtypeStruct((N_CHIPS, hr, cols), g.dtype),
        compiler_params=_params("parallel", "parallel", "parallel"),
    )(place, g, landed)


def _sum_slots(mine, landed, place, name):
    _, hr, cols = mine.shape
    rb, bc = _stream_blocks(hr, cols)
    rb = _pick_rows(hr, max(16, rb // 2))
    nrb = hr // rb

    def body(place_ref, m_ref, l_ref, o_ref):
        chip = place_ref[0]
        own = m_ref[...].astype(F32)
        total = jnp.where(chip == 0, own, l_ref[0].astype(F32))
        for j in range(1, N_CHIPS):
            total = total + jnp.where(chip == j, own, l_ref[j].astype(F32))
        o_ref[...] = total

    return _pcall(
        body, name=name, num_scalar_prefetch=1, grid=(nrb, cols // bc),
        in_specs=[pl.BlockSpec((None, rb, bc), lambda i, k, p: (p[0], i, k)),
                  pl.BlockSpec((N_CHIPS, rb, bc), lambda i, k, p: (0, i, k))],
        out_specs=pl.BlockSpec((rb, bc), lambda i, k, p: (p[1] * nrb + i, k)),
        out_shape=jax.ShapeDtypeStruct((2 * hr, cols), F32),
        compiler_params=_params("parallel", "parallel"),
    )(place, mine, landed)


class _WeightGather:
    @staticmethod
    def zones(quarters, chip):
        return [lax.dynamic_update_slice(lax.empty((N_CHIPS,) + q.shape, BF16), q.astype(BF16)[None], (chip, 0, 0))
                for q in quarters]

    def __init__(self, tag, zones, ring=False):
        self.tag, self.n, self.ring = tag, len(zones), ring
        self.plan = _plan_weights_ring(self.n) if ring else _plan_weights_ici(self.n)
        self.sems, self.bufs = _xfer_start("wici_start_" + tag, zones, self.plan, (2 if ring else 3) * self.n,
                                           after_last=True)

    def relay(self):
        plan = _plan_weights_relay(self.n)
        (self.sems,), self.bufs = _xfer_hand_over("wrel_start_" + self.tag, self.sems, self.bufs, self.plan,
                                                   [(plan, 2 * self.n)])
        self.plan = plan

    def pass_on(self):
        plan = _plan_weights_d2d(self.n)
        (self.sems,), self.bufs = _xfer_hand_over("wd2d_start_" + self.tag, self.sems, self.bufs, self.plan,
                                                   [(plan, 3 * self.n)])
        self.plan = plan

    def done(self):
        return _xfer_wait("wd2d_wait_" + self.tag, self.sems, self.bufs, self.plan)

    def relay_and_pass_near(self):
        plan, self.near_plan = _plan_weights_relay(self.n), _plan_weights_d2d(self.n, slice(0, 2))
        (self.sems, self.near_sems), self.bufs = _xfer_hand_over(
            "wrel_start_" + self.tag, self.sems, self.bufs, self.plan, [(plan, 2 * self.n), (self.near_plan, 2 * self.n)])
        self.plan = plan

    def near_done(self):
        self.bufs = _xfer_wait("wnear_wait_" + self.tag, self.near_sems, self.bufs, self.near_plan)
        return self.bufs

    def pass_far(self):
        plan = _plan_weights_d2d(self.n, slice(2, 3))
        (sems,), bufs = _xfer_hand_over("wfar_start_" + self.tag, self.sems, self.bufs, self.plan, [(plan, self.n)])
        return _xfer_wait("wfar_wait_" + self.tag, sems, bufs, plan)


class _GradReduce:
    def __init__(self, tag, grads):
        self.tag, self.n = tag, len(grads)
        zones = [lax.empty((N_CHIPS, g.shape[1] // 2, g.shape[2]), g.dtype) for g in grads]
        self.plan = _plan_grads_d2d(self.n)
        self.sems, self.bufs = _xfer_start("gd2d_start_" + tag, list(grads) + zones, self.plan, self.n)

    def pair(self, place):
        n = self.n
        bufs = _xfer_wait("gd2d_wait_" + self.tag, self.sems, self.bufs, self.plan)
        self.halves = [_pre_reduce(bufs[w], bufs[n + w], place, f"pre_reduce_{self.tag}{w}") for w in range(n)]

    def cross(self, after_last=False):
        zones = [lax.empty(h.shape, h.dtype) for h in self.halves]
        self.plan = _plan_grads_ici(self.n)
        self.sems, self.bufs = _xfer_start("gici_start_" + self.tag, self.halves + zones, self.plan, 3 * self.n,
                                           after_last=after_last)

    def step(self, place):
        self.pair(place)
        self.cross()

    def join(self, place):
        n = self.n
        bufs = _xfer_wait("gici_wait_" + self.tag, self.sems, self.bufs, self.plan)
        sums = [_sum_slots(bufs[w], bufs[n + w], place, f"sum_slots_{self.tag}{w}") for w in range(n)]
        self.plan = _plan_final_d2d(n)
        self.sems, self.bufs = _xfer_start("gfin_start_" + self.tag, sums, self.plan, n)

    def done(self):
        return _xfer_wait("gfin_wait_" + self.tag, self.sems, self.bufs, self.plan)


def _ada_fwd(c_all, w_q, b_q):
    d, n = w_q.shape
    bn = _pick(n, 512)

    def body(c_ref, w_ref, b_ref, o_ref):
        cv = c_ref[...]
        act = cv * _sigmoid(cv)
        o_ref[...] = _dot(act, w_ref[...], NN, lax.Precision.HIGHEST) + b_ref[...]

    return _pcall(
        body, name="ada_fwd", grid=(n // bn,),
        in_specs=[pl.BlockSpec((N_DEV, d), lambda j: (0, 0)), pl.BlockSpec((d, bn), lambda j: (0, j)),
                  pl.BlockSpec((1, bn), lambda j: (0, j))],
        out_specs=pl.BlockSpec((N_DEV, bn), lambda j: (0, j)),
        out_shape=jax.ShapeDtypeStruct((N_DEV, n), F32), compiler_params=_params("parallel"),
    )(c_all, w_q, b_q)


def _row_spec(rb, width, col=0):
    return pl.BlockSpec((rb, width), lambda i, col=col: (i, col))


def _vec_spec(width, col=0):
    return pl.BlockSpec((1, width), lambda i, col=col: (0, col))


def _norm_mod_fwd(x, g, sc, sh, name, res=None, gt=None):
    s, d = x.shape
    rb = _pick(s, 256)
    has_res = res is not None

    def body(*refs):
        if has_res:
            x_ref, res_ref, gt_ref, g_ref, sc_ref, sh_ref, x1_ref, h_ref = refs
            xv = x_ref[...] + gt_ref[...] * res_ref[...]
            x1_ref[...] = xv
        else:
            x_ref, g_ref, sc_ref, sh_ref, h_ref = refs
            xv = x_ref[...]
        xh, _ = _rms(xv)
        h_ref[...] = (xh * g_ref[...] * (1.0 + sc_ref[...]) + sh_ref[...]).astype(BF16)

    row, vec = _row_spec(rb, d), _vec_spec(d)
    if has_res:
        ins, in_specs = (x, res, gt, g, sc, sh), [row, row, vec, vec, vec, vec]
        out_shape = [jax.ShapeDtypeStruct((s, d), F32), jax.ShapeDtypeStruct((s, d), BF16)]
        out_specs = [row, row]
    else:
        ins, in_specs = (x, g, sc, sh), [row, vec, vec, vec]
        out_shape, out_specs = jax.ShapeDtypeStruct((s, d), BF16), row
    return _pcall(body, name=name, grid=(s // rb,), in_specs=in_specs, out_specs=out_specs,
                          out_shape=out_shape, compiler_params=_params("parallel"))(*ins)


def _final_loss(x1, f, gt2, g_final, target):
    s, d = x1.shape
    rb = _pick(s, 256)

    def body(x1_ref, f_ref, gt_ref, g_ref, t_ref, dx_ref, df_ref, loss_ref, dg_ref, dgt_ref):
        first = pl.program_id(0) == 0
        fv, gt, gv = f_ref[...], gt_ref[...], g_ref[...]
        x2 = x1_ref[...] + gt * fv
        xh, r = _rms(x2)
        err = xh * gv - t_ref[...]
        blk = 0.5 * jnp.sum(jnp.sum(err * err, axis=1, keepdims=True), axis=0, keepdims=True) / d
        dy = err / d
        dxh = dy * gv
        dx = r * (dxh - xh * jnp.mean(dxh * xh, axis=-1, keepdims=True))
        dx_ref[...] = dx
        df_ref[...] = (dx * gt).astype(BF16)
        _accumulate(first, loss_ref, jnp.broadcast_to(blk, (1, LANES)))
        _accumulate(first, dg_ref, _colsum(dy * xh))
        _accumulate(first, dgt_ref, _colsum(dx * fv))

    row, vec = _row_spec(rb, d), _vec_spec(d)
    return _pcall(
        body, name="final_loss", grid=(s // rb,), in_specs=[row, row, vec, vec, row],
        out_specs=[row, row, _vec_spec(LANES), vec, vec],
        out_shape=[jax.ShapeDtypeStruct((s, d), F32), jax.ShapeDtypeStruct((s, d), BF16),
                   jax.ShapeDtypeStruct((1, LANES), F32), jax.ShapeDtypeStruct((1, d), F32),
                   jax.ShapeDtypeStruct((1, d), F32)],
        compiler_params=_params("arbitrary"),
    )(x1, f, gt2, g_final, target)


def _norm_mod_bwd(dh, xin, dres, g, sc, name, branch=None, gt=None):
    s, d = xin.shape
    rb = _pick(s, 256)
    has_branch = branch is not None

    def body(*refs):
        if has_branch:
            dh_ref, x_ref, dres_ref, g_ref, sc_ref, br_ref, gt_ref, dx_ref, dbr_ref, dsh_ref, dsc_ref, dg_ref, dgt_ref = refs
        else:
            dh_ref, x_ref, dres_ref, g_ref, sc_ref, dx_ref, dsh_ref, dsc_ref, dg_ref = refs
        first = pl.program_id(0) == 0
        gv = g_ref[...]
        xh, r = _rms(x_ref[...])
        dhv = dh_ref[...]
        dn = dhv * (1.0 + sc_ref[...])
        dxh = dn * gv
        dx = dres_ref[...] + r * (dxh - xh * jnp.mean(dxh * xh, axis=-1, keepdims=True))
        dx_ref[...] = dx
        _accumulate(first, dsh_ref, _colsum(dhv))
        _accumulate(first, dsc_ref, _colsum(dhv * xh * gv))
        _accumulate(first, dg_ref, _colsum(dn * xh))
        if has_branch:
            dbr_ref[...] = (dx * gt_ref[...]).astype(BF16)
            _accumulate(first, dgt_ref, _colsum(dx * br_ref[...]))

    row, vec = _row_spec(rb, d), _vec_spec(d)
    vec_shape = jax.ShapeDtypeStruct((1, d), F32)
    if has_branch:
        ins, in_specs = (dh, xin, dres, g, sc, branch, gt), [row, row, row, vec, vec, row, vec]
        out_specs = [row, row, vec, vec, vec, vec]
        out_shape = [jax.ShapeDtypeStruct((s, d), F32), jax.ShapeDtypeStruct((s, d), BF16)] + [vec_shape] * 4
    else:
        ins, in_specs = (dh, xin, dres, g, sc), [row, row, row, vec, vec]
        out_specs = [row, vec, vec, vec]
        out_shape = [jax.ShapeDtypeStruct((s, d), F32)] + [vec_shape] * 3
    return _pcall(body, name=name, grid=(s // rb,), in_specs=in_specs, out_specs=out_specs,
                          out_shape=out_shape, compiler_params=_params("arbitrary"))(*ins)


def _gate_fwd(pa, pb, z, b_gate, gate_col):
    s, d = pa.shape
    rb = _pick(s, 256)

    def body(pa_ref, pb_ref, za_ref, zb_ref, ba_ref, bb_ref, y_ref):
        ga = _sigmoid(za_ref[...] + ba_ref[...])
        gb = _sigmoid(zb_ref[...] + bb_ref[...])
        y_ref[...] = (ga * pa_ref[...].astype(F32) + gb * pb_ref[...].astype(F32)).astype(BF16)

    row = _row_spec(rb, d)
    return _pcall(
        body, name="gate_fwd", grid=(s // rb,),
        in_specs=[row, row, _row_spec(rb, d, gate_col), _row_spec(rb, d, gate_col + 1), _vec_spec(d, 0), _vec_spec(d, 1)],
        out_specs=row, out_shape=jax.ShapeDtypeStruct((s, d), BF16), compiler_params=_params("parallel"),
    )(pa, pb, z, z, b_gate, b_gate)


def _out_bwd(dyo, w_out, pa, pb, z, b_gate, gate_col, bm=512, bn=1024):
    s, d = dyo.shape
    bm, bn = _pick(s, bm), _pick(d, bn)
    nj = d // bn

    def body(dyo_ref, w_ref, pa_ref, pb_ref, za_ref, zb_ref, ba_ref, bb_ref,
             dpa_ref, dpb_ref, dza_ref, dzb_ref, dba_ref, dbb_ref):
        first = pl.program_id(1) == 0
        dyv = _dot(dyo_ref[...], w_ref[...], NT)
        ga = _sigmoid(za_ref[...] + ba_ref[...])
        gb = _sigmoid(zb_ref[...] + bb_ref[...])
        dpa_ref[...] = (dyv * ga).astype(BF16)
        dpb_ref[...] = (dyv * gb).astype(BF16)
        dga = dyv * pa_ref[...].astype(F32) * ga * (1.0 - ga)
        dgb = dyv * pb_ref[...].astype(F32) * gb * (1.0 - gb)
        dza_ref[...] = dga.astype(BF16)
        dzb_ref[...] = dgb.astype(BF16)
        _accumulate(first, dba_ref, _colsum(dga))
        _accumulate(first, dbb_ref, _colsum(dgb))

    tile = pl.BlockSpec((bm, bn), lambda j, i: (i, j))
    vec = pl.BlockSpec((1, bn), lambda j, i: (0, j))
    act, vec_shape = jax.ShapeDtypeStruct((s, d), BF16), jax.ShapeDtypeStruct((1, d), F32)
    return _pcall(
        body, name="out_bwd", grid=(nj, s // bm),
        in_specs=[pl.BlockSpec((bm, d), lambda j, i: (i, 0)), pl.BlockSpec((bn, d), lambda j, i: (j, 0)), tile, tile,
                  pl.BlockSpec((bm, bn), lambda j, i: (i, gate_col * nj + j)),
                  pl.BlockSpec((bm, bn), lambda j, i: (i, (gate_col + 1) * nj + j)),
                  vec, pl.BlockSpec((1, bn), lambda j, i: (0, nj + j))],
        out_specs=[tile, tile, tile, tile, vec, vec], out_shape=[act, act, act, act, vec_shape, vec_shape],
        compiler_params=_params("parallel", "arbitrary"),
    )(dyo, w_out, pa, pb, z, z, b_gate, b_gate)


def _ffn_in(h, w_fi, bm=512, bn=1408):
    s, d = h.shape
    per = w_fi.shape[2]
    ff = 2 * per
    bm, bn = _pick(s, bm), _pick(per, bn)
    per_blocks = per // bn

    def body(h_ref, wa_ref, wu_ref, a_ref, u_ref, hf_ref):
        hv = h_ref[...]
        a = _dot(hv, wa_ref[...])
        up = _dot(hv, wu_ref[...])
        a_ref[...] = a.astype(BF16)
        u_ref[...] = up.astype(BF16)
        hf_ref[...] = (a * _sigmoid(a) * up).astype(BF16)

    out = pl.BlockSpec((bm, bn), lambda j, i: (i, j))
    shape = jax.ShapeDtypeStruct((s, ff), BF16)
    return _pcall(
        body, name="ffn_in", grid=(ff // bn, s // bm),
        in_specs=[pl.BlockSpec((bm, d), lambda j, i: (i, 0)),
                  pl.BlockSpec((None, d, bn), lambda j, i: (j // per_blocks, 0, j % per_blocks)),
                  pl.BlockSpec((None, d, bn), lambda j, i: (2 + j // per_blocks, 0, j % per_blocks))],
        out_specs=[out, out, out], out_shape=[shape, shape, shape],
        compiler_params=_params("parallel", "parallel"),
    )(h, w_fi, w_fi)


def _ffn_out_bwd(dffn, w_fo, a_act, up_act, bm=512, bn=1408):
    s, d = dffn.shape
    ff = w_fo.shape[0]
    bm, bn = _pick(s, bm), _pick(ff, bn)

    def body(d_ref, w_ref, a_ref, u_ref, da_ref, du_ref):
        dhf = _dot(d_ref[...], w_ref[...], NT)
        a = a_ref[...].astype(F32)
        sa = _sigmoid(a)
        da_ref[...] = (dhf * u_ref[...].astype(F32) * sa * (1.0 + a * (1.0 - sa))).astype(BF16)
        du_ref[...] = (dhf * a * sa).astype(BF16)

    tile = pl.BlockSpec((bm, bn), lambda j, i: (i, j))
    shape = jax.ShapeDtypeStruct((s, ff), BF16)
    return _pcall(
        body, name="ffn_out_bwd", grid=(ff // bn, s // bm),
        in_specs=[pl.BlockSpec((bm, d), lambda j, i: (i, 0)), pl.BlockSpec((bn, d), lambda j, i: (j, 0)), tile, tile],
        out_specs=[tile, tile], out_shape=[shape, shape], compiler_params=_params("parallel", "parallel"),
    )(dffn, w_fo, a_act, up_act)


def _tril(n):
    return lax.broadcasted_iota(jnp.int32, (n, n), 0) >= lax.broadcasted_iota(jnp.int32, (n, n), 1)


def _gmlp_norm(v, ln_g, ln_b):
    gv = _gelu(v)
    mu = jnp.mean(gv, axis=-1, keepdims=True)
    cen = gv - mu
    rs = lax.rsqrt(jnp.mean(cen * cen, axis=-1, keepdims=True) + EPS)
    xh = cen * rs
    return xh, rs, xh * ln_g + ln_b


def _gmlp_fwd(z, ln_g, ln_b, ws, bs_t):
    s = z.shape[0]
    gw = ln_g.shape[1]
    groups, chunk, _ = ws.shape

    def body(u_ref, v_ref, lg_ref, lb_ref, ws_ref, bs_ref, ya_ref):
        gu = _gelu(u_ref[...])
        _, _, vn = _gmlp_norm(v_ref[...], lg_ref[...], lb_ref[...])
        mask = _tril(chunk)
        for g in range(groups):
            cols = slice(g * LANES, (g + 1) * LANES)
            wm = jnp.where(mask, ws_ref[g], 0.0).astype(BF16)
            sg = _dot(wm, vn[:, cols].astype(BF16)) + bs_ref[:, g:g + 1]
            ya_ref[:, cols] = (gu[:, cols] * sg).astype(BF16)

    return _pcall(
        body, name="gmlp_fwd", grid=(s // chunk,),
        in_specs=[_row_spec(chunk, gw, 0), _row_spec(chunk, gw, 1), _vec_spec(gw), _vec_spec(gw),
                  pl.BlockSpec((groups, chunk, chunk), lambda i: (0, 0, 0)), pl.BlockSpec((chunk, LANES), lambda i: (0, 0))],
        out_specs=_row_spec(chunk, gw), out_shape=jax.ShapeDtypeStruct((s, gw), BF16),
        compiler_params=_params("parallel"),
    )(z, z, ln_g, ln_b, ws, bs_t)


def _gmlp_bwd(dya, z, ln_g, ln_b, ws, bs_t):
    s = z.shape[0]
    gw = ln_g.shape[1]
    groups, chunk, _ = ws.shape

    def body(dya_ref, u_ref, v_ref, lg_ref, lb_ref, ws_ref, bs_ref, duv_ref, dws_ref, dbs_ref, dlg_ref, dlb_ref, dvn_ref):
        first = pl.program_id(0) == 0
        u, v, lg = u_ref[...], v_ref[...], lg_ref[...]
        gu = _gelu(u)
        xh, rs, vn = _gmlp_norm(v, lg, lb_ref[...])
        dyav = dya_ref[...]
        mask = _tril(chunk)
        lane = lax.broadcasted_iota(jnp.int32, (chunk, LANES), 1)
        dbs = jnp.zeros((chunk, LANES), F32)
        for g in range(groups):
            cols = slice(g * LANES, (g + 1) * LANES)
            wm = jnp.where(mask, ws_ref[g], 0.0).astype(BF16)
            vg = vn[:, cols].astype(BF16)
            sg = _dot(wm, vg) + bs_ref[:, g:g + 1]
            ds = dyav[:, cols] * gu[:, cols]
            duv_ref[:, cols] = (dyav[:, cols] * sg * _gelu_grad(u[:, cols])).astype(BF16)
            dsb = ds.astype(BF16)
            _accumulate(first, dws_ref.at[g], jnp.where(mask, _dot(dsb, vg, NT), 0.0))
            dbs = dbs + jnp.where(lane == g, jnp.sum(ds, axis=-1, keepdims=True), 0.0)
            dvn_ref[:, cols] = _dot(wm, dsb, TN)
        dvn = dvn_ref[...]
        _accumulate(first, dbs_ref, dbs)
        _accumulate(first, dlb_ref, _colsum(dvn))
        _accumulate(first, dlg_ref, _colsum(dvn * xh))
        dxh = dvn * lg
        dgv = rs * (dxh - jnp.mean(dxh, axis=-1, keepdims=True) - xh * jnp.mean(dxh * xh, axis=-1, keepdims=True))
        duv_ref[:, gw:] = (dgv * _gelu_grad(v)).astype(BF16)

    return _pcall(
        body, name="gmlp_bwd", grid=(s // chunk,),
        in_specs=[_row_spec(chunk, gw), _row_spec(chunk, gw, 0), _row_spec(chunk, gw, 1), _vec_spec(gw), _vec_spec(gw),
                  pl.BlockSpec((groups, chunk, chunk), lambda i: (0, 0, 0)), pl.BlockSpec((chunk, LANES), lambda i: (0, 0))],
        out_specs=[_row_spec(chunk, 2 * gw), pl.BlockSpec((groups, chunk, chunk), lambda i: (0, 0, 0)),
                   pl.BlockSpec((chunk, LANES), lambda i: (0, 0)), _vec_spec(gw), _vec_spec(gw)],
        out_shape=[jax.ShapeDtypeStruct((s, 2 * gw), BF16), jax.ShapeDtypeStruct((groups, chunk, chunk), F32),
                   jax.ShapeDtypeStruct((chunk, LANES), F32), jax.ShapeDtypeStruct((1, gw), F32),
                   jax.ShapeDtypeStruct((1, gw), F32)],
        scratch_shapes=[pltpu.VMEM((chunk, gw), F32)],
        compiler_params=_params("arbitrary"),
    )(dya, z, z, ln_g, ln_b, ws, bs_t)


def _lower_bound(lb_ref):
    a0, a1 = lb_ref[0:1, :], lb_ref[1:2, :]
    mx = jnp.maximum(a0, a1)
    e0, e1 = jnp.exp(a0 - mx), jnp.exp(a1 - mx)
    return e0 / (e0 + e1)


def _sum_dot(mask, x):
    hi = x.astype(BF16)
    rest = x - hi.astype(F32)
    mid = rest.astype(BF16)
    low = (rest - mid.astype(F32)).astype(BF16)
    return _dot(mask, hi) + _dot(mask, mid) + _dot(mask, low)


def _ones_where(mask):
    return jnp.where(mask, 1.0, 0.0).astype(BF16)


def _hg_masks(rows, t):
    r = lax.broadcasted_iota(jnp.int32, (rows, rows), 0)
    c = lax.broadcasted_iota(jnp.int32, (rows, rows), 1)
    same = (r // t) == (c // t)
    incl = jnp.logical_and(same, c <= r)
    upto_mid = jnp.logical_and(same, (c % t) <= t // 2)
    rev = jnp.logical_and(same, c >= r)
    return same, incl, upto_mid, rev


def _hg_block(q, fp, lb, masks):
    rows = q.shape[0]
    same, incl, upto_mid, _ = masks
    sig = _sigmoid(fp)
    f = lb + (1.0 - lb) * sig
    k = 1.0 - f
    sq = _sigmoid(q)
    qa = q * sq
    stacked = jnp.concatenate([_ones_where(m) for m in (incl, same, upto_mid)], axis=0)
    sums = _sum_dot(stacked, jnp.log(f))
    b, b_last, b_mid = sums[:rows], sums[rows:2 * rows], sums[2 * rows:]
    e_q = jnp.exp(jnp.minimum(b - b_mid, EXP_CLAMP))
    e_k = jnp.exp(jnp.minimum(b_mid - b, EXP_CLAMP))
    e_in = jnp.exp(b)
    e_out = jnp.exp(b_last - b)
    return dict(sig=sig, f=f, k=k, sq=sq, qa=qa, e_last=jnp.exp(b_last), e_q=e_q, e_k=e_k, e_in=e_in, e_out=e_out,
                q_hat=(qa * e_q).astype(BF16), k_hat=(k * e_k).astype(BF16),
                q_in=(qa * e_in).astype(BF16), k_out=k * e_out)


def _hgrn_fwd(z, hg_lb, norm_g, q_col):
    s = z.shape[0]
    hw = norm_g.shape[1]
    heads = hw // LANES
    t = HG_CHUNK
    rows = min(HG_ROWS, s)
    per_step = rows // t
    hp = min(HG_HEADS_PER_STEP, heads)
    assert heads % hp == 0 and q_col % hp == 0, (heads, q_col)
    wide = hp * LANES

    def zspec(which):
        return pl.BlockSpec((rows, wide), lambda h, r, which=which: (r, (q_col + which * heads) // hp + h))

    def body(q_ref, f_ref, i_ref, g_ref, lb_ref, ng_ref, yb_ref, o_ref, st_out_ref, st_ref, e_last_ref, inter_ref):
        @pl.when(pl.program_id(1) == 0)
        def _():
            st_ref[...] = jnp.zeros_like(st_ref)

        masks = _hg_masks(rows, t)
        for hh in range(hp):
            cols = slice(hh * LANES, (hh + 1) * LANES)
            blk = _hg_block(q_ref[:, cols], f_ref[:, cols], _lower_bound(lb_ref.at[:, cols]), masks)
            iv = i_ref[:, cols].astype(BF16)
            q_in, k_out = blk["q_in"], blk["k_out"].astype(BF16)
            e_last_ref[hh] = blk["e_last"]
            attn = jnp.where(masks[1], _dot(blk["q_hat"], blk["k_hat"], NT), 0.0).astype(BF16)
            o = _dot(attn, iv)
            grown = [_dot(iv[j * t:(j + 1) * t], k_out[j * t:(j + 1) * t], TN) for j in range(per_step)]
            st = st_ref[hh]
            for j in range(per_step):
                st_out_ref[hh, j] = st
                inter_ref[hh, j * t:(j + 1) * t, :] = _dot(q_in[j * t:(j + 1) * t], st.astype(BF16), NT)
                st = st * e_last_ref[hh, j * t:j * t + 1, :] + grown[j]
            st_ref[hh] = st
            o = o + inter_ref[hh]
            o_ref[:, cols] = o
            og = g_ref[:, cols]
            on, _ = _rms(o)
            yb_ref[:, cols] = (on * ng_ref[:, cols] * (og * _sigmoid(og))).astype(BF16)

    out_row = pl.BlockSpec((rows, wide), lambda h, r: (r, h))
    return _pcall(
        body, name="hgrn_fwd", grid=(heads // hp, s // rows),
        in_specs=[zspec(0), zspec(1), zspec(2), zspec(3),
                  pl.BlockSpec((2, wide), lambda h, r: (0, h)), pl.BlockSpec((1, wide), lambda h, r: (0, h))],
        out_specs=[out_row, out_row, pl.BlockSpec((hp, per_step, LANES, LANES), lambda h, r: (h, r, 0, 0))],
        out_shape=[jax.ShapeDtypeStruct((s, hw), BF16), jax.ShapeDtypeStruct((s, hw), F32),
                   jax.ShapeDtypeStruct((heads, s // t, LANES, LANES), F32)],
        scratch_shapes=[pltpu.VMEM((hp, LANES, LANES), F32), pltpu.VMEM((hp, rows, LANES), F32),
                        pltpu.VMEM((hp, rows, LANES), F32)],
        compiler_params=_params("parallel", "arbitrary"),
    )(z, z, z, z, hg_lb, norm_g)


def _hgrn_bwd(dyb, z, o_raw, states, hg_lb, norm_g, q_col):
    s = z.shape[0]
    hw = norm_g.shape[1]
    heads = hw // LANES
    t = HG_CHUNK
    rows = min(HG_ROWS, s)
    per_step = rows // t
    n_steps = s // rows
    hp = min(HG_HEADS_PER_STEP, heads)
    assert heads % hp == 0 and q_col % hp == 0, (heads, q_col)
    wide = hp * LANES

    def zspec(which):
        return pl.BlockSpec((rows, wide), lambda h, r, which=which: (n_steps - 1 - r, (q_col + which * heads) // hp + h))

    def body(dyb_ref, q_ref, f_ref, i_ref, g_ref, o_ref, st_in_ref, lb_ref, ng_ref,
             dq_ref, df_ref, di_ref, dg_ref, dlb_ref, dng_ref, dst_ref, acc_lb_ref, acc_ng_ref,
             e_last_ref, dq_in_ref, dk_out_ref, di_inter_ref, carry_ref):
        step = pl.program_id(1)

        @pl.when(step == 0)
        def _():
            dst_ref[...] = jnp.zeros_like(dst_ref)
            acc_lb_ref[...] = jnp.zeros_like(acc_lb_ref)
            acc_ng_ref[...] = jnp.zeros_like(acc_ng_ref)

        masks = _hg_masks(rows, t)
        same, incl, _, rev = masks
        sum_mask = jnp.concatenate([_ones_where(rev), _ones_where(same)], axis=1)
        for hh in range(hp):
            cols = slice(hh * LANES, (hh + 1) * LANES)
            lb = _lower_bound(lb_ref.at[:, cols])
            ng = ng_ref[:, cols]
            q = q_ref[:, cols]
            blk = _hg_block(q, f_ref[:, cols], lb, masks)
            iv = i_ref[:, cols].astype(BF16)
            o, og, dy = o_ref[:, cols], g_ref[:, cols], dyb_ref[:, cols]
            so = _sigmoid(og)
            on, r = _rms(o)
            acc_ng_ref[:, cols] += _colsum(dy * on * (og * so))
            dg_ref[:, cols] = (dy * on * ng * so * (1.0 + og * (1.0 - so))).astype(BF16)
            don = dy * ng * (og * so)
            do = (r * (don - on * jnp.mean(don * on, axis=-1, keepdims=True))).astype(BF16)
            q_hat, k_hat, q_in, k_out = blk["q_hat"], blk["k_hat"], blk["q_in"], blk["k_out"]
            k_out_b = k_out.astype(BF16)
            attn = jnp.where(incl, _dot(q_hat, k_hat, NT), 0.0).astype(BF16)
            d_attn = jnp.where(incl, _dot(do, iv, NT), 0.0).astype(BF16)
            di_intra = _dot(attn, do, TN)
            dq_hat = _dot(d_attn, k_hat)
            dk_hat = _dot(d_attn, q_hat, TN)
            e_last_ref[hh] = blk["e_last"]
            grown = [_dot(do[j * t:(j + 1) * t], q_in[j * t:(j + 1) * t], TN) for j in range(per_step)]
            dst = dst_ref[hh]
            for j in reversed(range(per_step)):
                rs_ = slice(j * t, (j + 1) * t)
                e_last = e_last_ref[hh, j * t:j * t + 1, :]
                st_prev, dst_b = st_in_ref[hh, j], dst.astype(BF16)
                dq_in_ref[hh, rs_, :] = _dot(do[rs_], st_prev.astype(BF16))
                dk_out_ref[hh, rs_, :] = _dot(iv[rs_], dst_b)
                di_inter_ref[hh, rs_, :] = _dot(k_out_b[rs_], dst_b, NT)
                carry_ref[hh, rs_, :] = jnp.broadcast_to(e_last * _colsum(st_prev * dst), (t, LANES))
                dst = dst * e_last + grown[j]
            dst_ref[hh] = dst
            di_ref[:, cols] = (di_intra + di_inter_ref[hh]).astype(BF16)
            dk_out = dk_out_ref[hh]
            dqa = dq_in_ref[hh] * blk["e_in"] + dq_hat * blk["e_q"]
            dk = dk_out * blk["e_out"] + dk_hat * blk["e_k"]
            db = blk["qa"] * dqa - blk["k"] * dk
            dlf = _sum_dot(sum_mask, jnp.concatenate([db, dk_out * k_out], axis=0)) + carry_ref[hh]
            dfv = dlf / blk["f"] - dk
            sig, sq = blk["sig"], blk["sq"]
            df_ref[:, cols] = (dfv * (1.0 - lb) * sig * (1.0 - sig)).astype(BF16)
            acc_lb_ref[:, cols] += _colsum(dfv * (1.0 - sig))
            dq_ref[:, cols] = (dqa * sq * (1.0 + q * (1.0 - sq))).astype(BF16)

        @pl.when(step == n_steps - 1)
        def _():
            lb = _lower_bound(lb_ref)
            d0 = acc_lb_ref[...] * lb * (1.0 - lb)
            dlb_ref[0:1, :] = d0
            dlb_ref[1:2, :] = -d0
            dng_ref[...] = acc_ng_ref[...]

    rev_row = pl.BlockSpec((rows, wide), lambda h, r: (n_steps - 1 - r, h))
    piece = jax.ShapeDtypeStruct((s, hw), BF16)
    return _pcall(
        body, name="hgrn_bwd", grid=(heads // hp, n_steps),
        in_specs=[rev_row, zspec(0), zspec(1), zspec(2), zspec(3), rev_row,
                  pl.BlockSpec((hp, per_step, LANES, LANES), lambda h, r: (h, n_steps - 1 - r, 0, 0)),
                  pl.BlockSpec((2, wide), lambda h, r: (0, h)), pl.BlockSpec((1, wide), lambda h, r: (0, h))],
        out_specs=[rev_row, rev_row, rev_row, rev_row,
                   pl.BlockSpec((2, wide), lambda h, r: (0, h)), pl.BlockSpec((1, wide), lambda h, r: (0, h))],
        out_shape=[piece, piece, piece, piece, jax.ShapeDtypeStruct((2, hw), F32), jax.ShapeDtypeStruct((1, hw), F32)],
        scratch_shapes=[pltpu.VMEM((hp, LANES, LANES), F32), pltpu.VMEM((1, wide), F32), pltpu.VMEM((1, wide), F32)]
        + [pltpu.VMEM((hp, rows, LANES), F32)] * 5,
        compiler_params=_params("parallel", "arbitrary"),
    )(dyb, z, z, z, z, o_raw, states, hg_lb, norm_g)


def _adam_update(w, m, v, g):
    m2 = ADAM_B1 * m + (1.0 - ADAM_B1) * g
    v2 = ADAM_B2 * v + (1.0 - ADAM_B2) * (g * g)
    m_hat = m2 * (1.0 / (1.0 - ADAM_B1 ** ADAM_STEP))
    v_hat = v2 * (1.0 / (1.0 - ADAM_B2 ** ADAM_STEP))
    return -ADAM_LR * (m_hat / (jnp.sqrt(v_hat) + ADAM_EPS) + ADAM_WD * w), m2, v2


def _adamw(w, m, v, parts, name, outer=False):
    rows, cols = w.shape
    bc = cols if cols <= 4096 else _pick(cols, 4096)
    rb = _pick_rows(rows, max(8, (384 * 1024) // bc), mult=8)
    if outer and rb % LANES:
        rb = rows

    def body(w_ref, m_ref, v_ref, *refs):
        g_ref, d_ref, mo_ref, vo_ref = refs[-4:]
        if outer:
            cv = refs[0][...]
            g = _dot(cv * _sigmoid(cv), refs[1][...], TN, lax.Precision.HIGHEST)
        else:
            p_ref = refs[0]
            g = p_ref[0].astype(F32)
            for p in range(1, p_ref.shape[0]):
                g = g + p_ref[p].astype(F32)
        g_ref[...] = g
        d_ref[...], mo_ref[...], vo_ref[...] = _adam_update(w_ref[...], m_ref[...], v_ref[...], g)

    blk = pl.BlockSpec((rb, bc), lambda i, j: (i, j))
    out = jax.ShapeDtypeStruct((rows, cols), F32)
    if outer:
        grad_specs = [pl.BlockSpec((N_DEV, rb), lambda i, j: (0, i)), pl.BlockSpec((N_DEV, bc), lambda i, j: (0, j))]
        grad_ins = tuple(parts)
    else:
        grad_specs = [pl.BlockSpec((parts.shape[0], rb, bc), lambda i, j: (0, i, j))]
        grad_ins = (parts,)
    return _pcall(
        body, name=name, grid=(rows // rb, cols // bc), in_specs=[blk, blk, blk] + grad_specs,
        out_specs=[blk] * 4, out_shape=[out] * 4, compiler_params=_params("parallel", "parallel"),
    )(w, m, v, *grad_ins)


SMALL = ("b_ada", "norm1_g", "b_gate", "gmlp_ln_g", "gmlp_ln_b", "gmlp_ws", "gmlp_bs", "hg_lb", "hg_norm_g",
         "norm2_g", "final_norm_g")
BIG = ("w_in", "w_branch_gmlp", "w_branch_hg", "w_out", "w_ffn_in", "w_ffn_out")
WEIGHTS = ("w_ada", "b_ada", "norm1_g", "w_in", "b_gate", "gmlp_ln_g", "gmlp_ln_b", "gmlp_ws", "gmlp_bs", "hg_lb",
           "hg_norm_g", "w_branch_gmlp", "w_branch_hg", "w_out", "norm2_g", "w_ffn_in", "w_ffn_out", "final_norm_g")


def _pack(parts):
    return jnp.concatenate([p.reshape(-1, LANES) for p in parts], axis=0)


def _step(x, c, loss_target, w, m, v):
    s, d = x.shape[1], x.shape[2]
    gw = w["gmlp_ln_g"].shape[-1]
    hw = w["hg_norm_g"].shape[-1]
    x2d, tgt = x[0], loss_target[0]
    mx, my, mc = lax.axis_index("x"), lax.axis_index("y"), lax.axis_index("c")
    chip = 2 * mx + my
    dev = 2 * chip + mc
    q_col = 2 * gw // LANES
    gate_col = (2 * gw + 4 * hw) // d
    place = jnp.stack([chip, mc]).astype(jnp.int32)
    _Order.last = None

    c_all = _all_gather8(c.reshape(-1, LANES), "gather_c").reshape(N_DEV, d)
    gather_in = _WeightGather("in", _WeightGather.zones([w["w_in"][0]], chip), ring=True)
    zones_mix = _WeightGather.zones([w[n][0] for n in ("w_branch_gmlp", "w_branch_hg", "w_out")], chip)
    zones_fi = _WeightGather.zones([w["w_ffn_in"][0]], chip)
    zones_fo = _WeightGather.zones([w["w_ffn_out"][0]], chip)
    _Order.also = (*zones_mix, *zones_fi, *zones_fo)
    n_ada = w["w_ada"].shape[-1]
    b_ada_q = lax.dynamic_slice(w["b_ada"], (0, chip * n_ada), (1, n_ada))
    mod_q = _ada_fwd(c_all, w["w_ada"][0], b_ada_q)
    mod_all = _all_gather8(mod_q, "gather_mod")
    mod = lax.dynamic_index_in_dim(mod_all, dev, axis=1, keepdims=False)[::2].reshape(1, 6 * d)
    sh1, sc1, gt1, sh2, sc2, gt2 = [mod[:, i * d:(i + 1) * d] for i in range(6)]

    gather_in.relay_and_pass_near()
    gather_mix = _WeightGather("mix", zones_mix)
    gather_fi = _WeightGather("fi", zones_fi, ring=True)

    norm1_g, norm2_g, final_g = w["norm1_g"], w["norm2_g"], w["final_norm_g"].reshape(1, d)
    ln_g, ln_b = w["gmlp_ln_g"], w["gmlp_ln_b"]
    ws = w["gmlp_ws"][0]
    groups = ws.shape[0]
    bs_t = jnp.pad(w["gmlp_bs"][0].T, ((0, 0), (0, LANES - groups)))
    hg_lb, hg_ng, b_gate = w["hg_lb"], w["hg_norm_g"], w["b_gate"]

    h1 = _norm_mod_fwd(x2d, norm1_g, sc1, sh1, "norm1_fwd")
    order = jnp.stack([chip, 2 * (1 - mx) + my, 2 * mx + (1 - my), 2 * (1 - mx) + (1 - my)]).astype(jnp.int32)
    w_in, = gather_in.near_done()
    z = _matmul_quarters(h1, w_in, order, 0, 3, "mm_z_near")
    w_in, = gather_in.pass_far()
    z = _matmul_quarters(h1, w_in, order, 3, 1, "mm_z_far", prev=z)
    gather_mix.pass_on()
    ya = _gmlp_fwd(z, ln_g, ln_b, ws, bs_t)
    yb, o_raw, states = _hgrn_fwd(z, hg_lb, hg_ng, q_col)
    gather_fi.relay()
    gather_fo = _WeightGather("fo", zones_fo)
    w_bg, w_bh, w_out = gather_mix.done()
    w_out = w_out.reshape(-1, w_out.shape[-1])
    pa = _matmul(ya, w_bg, mode="nn", name="mm_pa", out_dtype=BF16, b_slots=True)
    pb = _matmul(yb, w_bh, mode="nn", name="mm_pb", out_dtype=BF16, b_slots=True)
    y = _gate_fwd(pa, pb, z, b_gate, gate_col)
    gather_fi.pass_on()
    yo = _matmul(y, w_out, mode="nn", name="mm_yo", out_dtype=F32)
    x1, h2 = _norm_mod_fwd(x2d, norm2_g, sc2, sh2, "norm2_fwd", res=yo, gt=gt1)
    w_fi, = gather_fi.done()
    a_act, up_act, hf = _ffn_in(h2, w_fi)
    gather_fo.pass_on()
    w_fo, = gather_fo.done()
    w_fo = w_fo.reshape(-1, w_fo.shape[-1])
    ffn = _matmul(hf, w_fo, mode="nn", name="mm_ffn", out_dtype=F32)
    dx2, dffn, loss_row, d_final_g, d_gt2 = _final_loss(x1, ffn, gt2, final_g, tgt)

    g_fo = _matmul(hf, dffn, mode="tn", name="mm_g_fo", out_dtype=BF16, bm=1408)
    daup = tuple(_ffn_out_bwd(dffn, w_fo, a_act, up_act))
    g_fi = _matmul(h2, daup, mode="tn", name="mm_g_fi", out_dtype=BF16, out_slots=True, bn=1408)
    red_ffn = _GradReduce("ffn", [g_fo.reshape(N_CHIPS, -1, g_fo.shape[-1]), g_fi])
    dh2 = _matmul(daup, w_fi, mode="nt", name="mm_dh2", out_dtype=F32, b_slots=True)
    red_ffn.step(place)
    dx1, dyo, d_sh2, d_sc2, d_norm2, d_gt1 = _norm_mod_bwd(dh2, x1, dx2, norm2_g, sc2, "norm2_bwd", branch=yo, gt=gt1)
    g_out = _matmul(y, dyo, mode="tn", name="mm_g_out", out_dtype=BF16)
    dpa, dpb, dz_ga, dz_gb, d_b_ga, d_b_gb = _out_bwd(dyo, w_out, pa, pb, z, b_gate, gate_col)
    g_bg = _matmul(ya, dpa, mode="tn", name="mm_g_bg", out_dtype=BF16, out_slots=True)
    g_bh = _matmul(yb, dpb, mode="tn", name="mm_g_bh", out_dtype=BF16, out_slots=True)
    red_mix = _GradReduce("mix", [g_out.reshape(N_CHIPS, -1, g_out.shape[-1]), g_bg, g_bh])
    dya = _matmul(dpa, w_bg, mode="nt", name="mm_dya", out_dtype=F32, b_slots=True)
    dyb = _matmul(dpb, w_bh, mode="nt", name="mm_dyb", out_dtype=F32, b_slots=True)
    red_mix.step(place)
    dz_uv, d_ws, d_bs_t, d_ln_g, d_ln_b = _gmlp_bwd(dya, z, ln_g, ln_b, ws, bs_t)
    dz_q, dz_f, dz_i, dz_g, d_hg_lb, d_hg_ng = _hgrn_bwd(dyb, z, o_raw, states, hg_lb, hg_ng, q_col)
    dz = jnp.concatenate([dz_uv, dz_q, dz_f, dz_i, dz_g, dz_ga, dz_gb], axis=1)
    g_in = _matmul(h1, dz, mode="tn", name="mm_g_in", out_dtype=BF16, out_slots=True, bn=1280)
    red_in = _GradReduce("in", [g_in])
    dh1 = _matmul(dz, w_in, mode="nt", name="mm_dh1", out_dtype=F32, b_slots=True)
    red_in.pair(place)
    grad_x, d_sh1, d_sc1, d_norm1 = _norm_mod_bwd(dh1, x2d, dx1, norm1_g, sc1, "norm1_bwd")

    d_mod = jnp.concatenate([d_sh1, d_sc1, d_gt1, d_sh2, d_sc2, d_gt2], axis=1)
    small_part = {"b_ada": d_mod, "norm1_g": d_norm1, "b_gate": jnp.concatenate([d_b_ga, d_b_gb], axis=1), "gmlp_ln_g": d_ln_g, "gmlp_ln_b": d_ln_b,
                  "gmlp_ws": d_ws, "gmlp_bs": d_bs_t[:, :groups].T, "hg_lb": d_hg_lb, "hg_norm_g": d_hg_ng,
                  "norm2_g": d_norm2, "final_norm_g": d_final_g}
    small_all = _all_gather8(_pack([small_part[n] for n in SMALL]), "gather_small")
    red_in.cross(after_last=True)
    d_mod_all = small_all[:, :6 * d // LANES].reshape(N_DEV, 6 * d)
    d_mod_q = lax.dynamic_slice(d_mod_all, (0, chip * n_ada), (N_DEV, n_ada))

    grad, delta, new_m, new_v = {}, {}, {}, {}

    def update(n, parts, outer=False):
        outs = _adamw(w[n][0], m[n][0], v[n][0], parts, "adamw_" + n, outer=outer)
        grad[n], delta[n], new_m[n], new_v[n] = [o[None] for o in outs]

    update("w_ada", (c_all, d_mod_q), outer=True)
    outs = _adamw(_pack([w[n] for n in SMALL]), _pack([m[n] for n in SMALL]), _pack([v[n] for n in SMALL]),
                  small_all, "adamw_small")
    red_ffn.join(place)
    red_mix.join(place)
    g_fo, g_fi = red_ffn.done()
    update("w_ffn_out", g_fo[None])
    update("w_ffn_in", g_fi[None])
    red_in.join(place)
    g_out, g_bg, g_bh = red_mix.done()
    update("w_out", g_out[None])
    update("w_branch_gmlp", g_bg[None])
    update("w_branch_hg", g_bh[None])
    update("w_in", red_in.done()[0][None])
    row = 0
    for n in SMALL:
        cnt = w[n].size // LANES
        for dst, o in zip((grad, delta, new_m, new_v), outs):
            dst[n] = o[row:row + cnt].reshape(w[n].shape)
        row += cnt

    loss = lax.psum(loss_row[0, 0], ("x", "y", "c"))
    return (loss, grad_x[None], *[grad[n] for n in WEIGHTS], *[delta[n] for n in WEIGHTS],
            *[new_m[n] for n in WEIGHTS], *[new_v[n] for n in WEIGHTS])


def kernel(x, c, w_ada, b_ada, norm1_g, w_in, b_gate, gmlp_ln_g, gmlp_ln_b, gmlp_ws, gmlp_bs, hg_lb, hg_norm_g, w_branch_gmlp, w_branch_hg, w_out, norm2_g, w_ffn_in, w_ffn_out, final_norm_g, loss_target, m_w_ada, m_b_ada, m_norm1_g, m_w_in, m_b_gate, m_gmlp_ln_g, m_gmlp_ln_b, m_gmlp_ws, m_gmlp_bs, m_hg_lb, m_hg_norm_g, m_w_branch_gmlp, m_w_branch_hg, m_w_out, m_norm2_g, m_w_ffn_in, m_w_ffn_out, m_final_norm_g, v_w_ada, v_b_ada, v_norm1_g, v_w_in, v_b_gate, v_gmlp_ln_g, v_gmlp_ln_b, v_gmlp_ws, v_gmlp_bs, v_hg_lb, v_hg_norm_g, v_w_branch_gmlp, v_w_branch_hg, v_w_out, v_norm2_g, v_w_ffn_in, v_w_ffn_out, v_final_norm_g):
    w = dict(w_ada=w_ada, b_ada=b_ada, norm1_g=norm1_g, w_in=w_in, b_gate=b_gate, gmlp_ln_g=gmlp_ln_g,
             gmlp_ln_b=gmlp_ln_b, gmlp_ws=gmlp_ws, gmlp_bs=gmlp_bs, hg_lb=hg_lb, hg_norm_g=hg_norm_g,
             w_branch_gmlp=w_branch_gmlp, w_branch_hg=w_branch_hg, w_out=w_out, norm2_g=norm2_g,
             w_ffn_in=w_ffn_in, w_ffn_out=w_ffn_out, final_norm_g=final_norm_g)
    m = dict(w_ada=m_w_ada, b_ada=m_b_ada, norm1_g=m_norm1_g, w_in=m_w_in, b_gate=m_b_gate, gmlp_ln_g=m_gmlp_ln_g,
             gmlp_ln_b=m_gmlp_ln_b, gmlp_ws=m_gmlp_ws, gmlp_bs=m_gmlp_bs, hg_lb=m_hg_lb, hg_norm_g=m_hg_norm_g,
             w_branch_gmlp=m_w_branch_gmlp, w_branch_hg=m_w_branch_hg, w_out=m_w_out, norm2_g=m_norm2_g,
             w_ffn_in=m_w_ffn_in, w_ffn_out=m_w_ffn_out, final_norm_g=m_final_norm_g)
    v = dict(w_ada=v_w_ada, b_ada=v_b_ada, norm1_g=v_norm1_g, w_in=v_w_in, b_gate=v_b_gate, gmlp_ln_g=v_gmlp_ln_g,
             gmlp_ln_b=v_gmlp_ln_b, gmlp_ws=v_gmlp_ws, gmlp_bs=v_gmlp_bs, hg_lb=v_hg_lb, hg_norm_g=v_hg_norm_g,
             w_branch_gmlp=v_w_branch_gmlp, w_branch_hg=v_w_branch_hg, w_out=v_w_out, norm2_g=v_norm2_g,
             w_ffn_in=v_w_ffn_in, w_ffn_out=v_w_ffn_out, final_norm_g=v_final_norm_g)
    return _step(x, c, loss_target, w, m, v)
```

```python
import functools

import jax
import jax.numpy as jnp
from jax import lax
from jax.experimental import pallas as pl
from jax.experimental.pallas import tpu as pltpu

F32 = jnp.float32
BF16 = jnp.bfloat16
EPS = 1e-6
LANES = 128
N_CHIPS = 4
N_DEV = 8
VMEM_LIMIT_BYTES = 56 * 1024 * 1024
VMEM_RIDER_LIMIT_BYTES = 60 * 1024 * 1024
HG_CHUNK = 32
HG_ROWS = 256
HG_HEADS_PER_STEP = 8
EXP_CLAMP = 80.0
ADAM_LR, ADAM_B1, ADAM_B2, ADAM_EPS, ADAM_WD, ADAM_STEP = 0.001, 0.9, 0.999, 1e-08, 0.01, 10
MESH = pl.DeviceIdType.MESH

NN = (((1,), (0,)), ((), ()))
NT = (((1,), (1,)), ((), ()))
TN = (((0,), (0,)), ((), ()))


def _dot(a, b, dims=NN, precision=None):
    return lax.dot_general(a, b, dims, precision=precision, preferred_element_type=F32)


def _params(*semantics, vmem_limit_bytes=VMEM_LIMIT_BYTES):
    return pltpu.CompilerParams(dimension_semantics=semantics, vmem_limit_bytes=vmem_limit_bytes)


class _Order:
    last = None
    also = ()


def _pcall(body, *, in_specs, out_specs, grid=(), scratch_shapes=(), num_scalar_prefetch=0, **kw):
    def run(*ins):
        deps = (() if _Order.last is None else (_Order.last,)) + tuple(_Order.also)
        _Order.also = ()
        n_in, n_dep = len(ins), len(deps)

        def wrapped(*refs):
            body(*refs[:n_in], *refs[n_in + n_dep:])

        specs = list(in_specs) + [pl.BlockSpec(memory_space=pl.ANY)] * n_dep
        if num_scalar_prefetch:
            grid_spec = pltpu.PrefetchScalarGridSpec(
                num_scalar_prefetch=num_scalar_prefetch, grid=grid, in_specs=specs, out_specs=out_specs,
                scratch_shapes=scratch_shapes)
            outs = pl.pallas_call(wrapped, grid_spec=grid_spec, **kw)(*ins, *deps)
        else:
            outs = pl.pallas_call(wrapped, grid=grid, in_specs=specs, out_specs=out_specs,
                                  scratch_shapes=scratch_shapes, **kw)(*ins, *deps)
        _Order.last = jax.tree.leaves(outs)[0]
        return outs

    return run


def _pick_rows(dim, pref, mult=16):
    best = None
    for cand in range(mult, min(dim, pref) + 1, mult):
        if dim % cand == 0:
            best = cand
    assert best is not None, (dim, pref)
    return best


def _pick(dim, pref):
    if dim <= pref:
        return dim
    best = None
    for cand in range(LANES, pref + 1, LANES):
        if dim % cand == 0:
            best = cand
    assert best is not None, (dim, pref)
    return best


def _sigmoid(x):
    return 1.0 / (1.0 + jnp.exp(-x))


def _gelu(x):
    c = 0.7978845608028654
    return 0.5 * x * (1.0 + jnp.tanh(c * (x + 0.044715 * x * x * x)))


def _gelu_grad(x):
    c = 0.7978845608028654
    t = jnp.tanh(c * (x + 0.044715 * x * x * x))
    return 0.5 * (1.0 + t) + 0.5 * x * (1.0 - t * t) * c * (1.0 + 3.0 * 0.044715 * x * x)


def _rms(x):
    r = lax.rsqrt(jnp.mean(x * x, axis=-1, keepdims=True) + EPS)
    return x * r, r


def _colsum(x):
    return jnp.sum(x, axis=0, keepdims=True)


def _accumulate(first, ref, val):
    @pl.when(first)
    def _():
        ref[...] = val

    @pl.when(jnp.logical_not(first))
    def _():
        ref[...] += val


def _matmul(a, b, *, mode, name, out_dtype, b_slots=False, out_slots=False, bm=1024, bn=1024, bk=2816, rider=None):
    pair = a if isinstance(a, tuple) else b if isinstance(b, tuple) else None
    if mode == "nn":
        m, k = a.shape
        n = b.shape[2] * N_CHIPS if b_slots else b.shape[1]
        per = b.shape[2] if b_slots else n
    elif mode == "nt":
        m, k = (a[0].shape[0], 2 * a[0].shape[1]) if pair else a.shape
        n = b.shape[1] if b_slots else b.shape[0]
        per = b.shape[2] if b_slots else k
    else:
        k, m = a.shape
        n = 2 * b[0].shape[1] if pair else b.shape[1]
        per = n // N_CHIPS if out_slots else n
    bm = _pick(m, bm)
    if mode == "nt":
        bn, bk = _pick(n, bn), _pick(per, bk)
    else:
        bn, bk = _pick(per, bn), _pick(k, bk)
    nk = k // bk
    per_blocks = per // (bk if mode == "nt" else bn)
    dims = {"nn": NN, "nt": NT, "tn": TN}[mode]
    half = (nk if mode == "nt" else n // bn) // 2

    def product(a_ref, b_ref, o_ref, acc):
        part = _dot(a_ref[...], b_ref[...], dims)
        if nk == 1:
            o_ref[...] = part.astype(o_ref.dtype)
            return
        acc_ref, kk = acc[0], pl.program_id(2)

        @pl.when(kk == 0)
        def _():
            acc_ref[...] = part

        @pl.when(jnp.logical_and(kk > 0, kk < nk - 1))
        def _():
            acc_ref[...] += part

        @pl.when(kk == nk - 1)
        def _():
            o_ref[...] = (acc_ref[...] + part).astype(o_ref.dtype)

    n_ride, n_rode = (len(rider[0]), rider[2]) if rider else (0, 0)

    def body(*refs):
        if rider:
            ride_in, ride_out = refs[2:2 + n_ride], refs[3 + n_ride:3 + n_ride + n_rode]

            def ride(t, carry):
                rows = pl.ds(pl.multiple_of(t * 8, 8), 8)
                for ref, val in zip(ride_out, rider[1](*[r[rows, :] for r in ride_in])):
                    ref[rows, :] = val
                return carry

            lax.fori_loop(0, ride_in[0].shape[0] // 8, ride, 0)
        if not pair:
            return product(refs[0], refs[1], refs[2 + n_ride], refs[3 + n_ride + n_rode:])
        first = pl.program_id(2 if mode == "nt" else 1) < half
        x1, x2, y, o_ref, acc = refs[0], refs[1], refs[2], refs[3], refs[4:]

        @pl.when(first)
        def _():
            product(*((x1, y) if mode == "nt" else (y, x1)), o_ref, acc)

        @pl.when(jnp.logical_not(first))
        def _():
            product(*((x2, y) if mode == "nt" else (y, x2)), o_ref, acc)

    if mode == "nn":
        a_spec = pl.BlockSpec((bm, bk), lambda i, j, kk: (i, kk))
        if b_slots:
            b_spec = pl.BlockSpec((None, bk, bn), lambda i, j, kk: (j // per_blocks, kk, j % per_blocks))
        else:
            b_spec = pl.BlockSpec((bk, bn), lambda i, j, kk: (kk, j))
    elif mode == "nt":
        a_spec = pl.BlockSpec((bm, bk), lambda i, j, kk: (i, kk))
        if b_slots:
            b_spec = pl.BlockSpec((None, bn, bk), lambda i, j, kk: (kk // per_blocks, j, kk % per_blocks))
        else:
            b_spec = pl.BlockSpec((bn, bk), lambda i, j, kk: (j, kk))
    else:
        a_spec = pl.BlockSpec((bk, bm), lambda i, j, kk: (kk, i))
        b_spec = pl.BlockSpec((bk, bn), lambda i, j, kk: (kk, j))
    if out_slots:
        o_spec = pl.BlockSpec((None, bm, bn), lambda i, j, kk: (j // per_blocks, i, j % per_blocks))
        out_shape = jax.ShapeDtypeStruct((N_CHIPS, m, per), out_dtype)
    else:
        o_spec = pl.BlockSpec((bm, bn), lambda i, j, kk: (i, j))
        out_shape = jax.ShapeDtypeStruct((m, n), out_dtype)
    if not pair:
        ins, in_specs = (a, b), [a_spec, b_spec]
    elif mode == "nt":
        ins = (*a, b)
        in_specs = [pl.BlockSpec((bm, bk), lambda i, j, kk: (i, jnp.minimum(kk, half - 1))),
                    pl.BlockSpec((bm, bk), lambda i, j, kk: (i, jnp.maximum(kk - half, 0))), b_spec]
    else:
        ins = (*b, a)
        in_specs = [pl.BlockSpec((bk, bn), lambda i, j, kk: (kk, jnp.minimum(j, half - 1))),
                    pl.BlockSpec((bk, bn), lambda i, j, kk: (kk, jnp.maximum(j - half, 0))), a_spec]
    scratch = [pltpu.VMEM((bm, bn), F32)] if nk > 1 else []
    if not rider:
        return _pcall(
            body, name=name, grid=(m // bm, n // bn, nk), in_specs=in_specs, out_specs=o_spec, out_shape=out_shape,
            scratch_shapes=scratch, compiler_params=_params("parallel", "parallel", "arbitrary"),
        )(*ins)
    assert not pair
    rows, cols = rider[0][0].shape
    nj = n // bn
    rb = rows // ((m // bm) * nj * nk)
    assert rows == rb * (m // bm) * nj * nk and rb % 8 == 0, (rows, rb)
    ride_spec = pl.BlockSpec((rb, cols), lambda i, j, kk: ((i * nj + j) * nk + kk, 0))
    outs = _pcall(
        body, name=name, grid=(m // bm, nj, nk), in_specs=in_specs + [ride_spec] * n_ride,
        out_specs=[o_spec] + [ride_spec] * n_rode,
        out_shape=[out_shape] + [jax.ShapeDtypeStruct((rows, cols), F32)] * n_rode,
        scratch_shapes=scratch,
        compiler_params=_params("arbitrary", "arbitrary", "arbitrary", vmem_limit_bytes=VMEM_RIDER_LIMIT_BYTES),
    )(*ins, *rider[0])
    return outs[0], outs[1:]


def _matmul_quarters(h, w_slots, order, first, count, name, prev=None, bm=1024, bn=1280):
    s, k = h.shape
    n = w_slots.shape[2]
    bm, bn = _pick(s, bm), _pick(n, bn)
    pb = n // bn

    def body(order_ref, h_ref, w_ref, *rest):
        rest[-1][...] = _dot(h_ref[...], w_ref[...])

    ins = (order, h, w_slots) + (() if prev is None else (prev,))
    return _pcall(
        body, name=name, num_scalar_prefetch=1, grid=(count * pb, s // bm),
        in_specs=[pl.BlockSpec((bm, k), lambda j, i, o: (i, 0)),
                  pl.BlockSpec((None, k, bn), lambda j, i, o: (o[first + j // pb], 0, j % pb))]
        + ([] if prev is None else [ANY_SPEC]),
        out_specs=pl.BlockSpec((bm, bn), lambda j, i, o: (i, o[first + j // pb] * pb + j % pb)),
        out_shape=jax.ShapeDtypeStruct((s, N_CHIPS * n), F32),
        input_output_aliases={} if prev is None else {3: 0},
        compiler_params=_params("arbitrary", "arbitrary"),
    )(*ins)


def _place():
    x, y, c = lax.axis_index("x"), lax.axis_index("y"), lax.axis_index("c")
    chips = [(1 - x, y), (x, 1 - y), (1 - x, 1 - y)]
    return x, y, c, chips


def _all_gather8(block, name):
    def body(x_ref, out_ref, send_sems, recv_sems, local_sem):
        x, y, c, chips = _place()
        me, sibling = (x, y, c), (x, y, 1 - c)

        def slot(px, py, pc):
            return out_ref.at[4 * px + 2 * py + pc]

        def copy(k, blk, to, src=None):
            return pltpu.make_async_remote_copy(
                src_ref=slot(*blk) if src is None else src, dst_ref=slot(*blk),
                send_sem=send_sems.at[k], recv_sem=recv_sems.at[k], device_id=to, device_id_type=MESH)

        mine = pltpu.make_async_copy(x_ref, slot(*me), local_sem)
        mine.start()
        first = [copy(0, me, sibling, src=x_ref)]
        first += [copy(1 + j, me, (*chip, c), src=x_ref) for j, chip in enumerate(chips)]
        for cp in first:
            cp.start()
        passed = [copy(4 + j, (*chip, c), sibling) for j, chip in enumerate(chips)]
        for j, chip in enumerate(chips):
            copy(1 + j, (*chip, c), me).wait_recv()
            passed[j].start()
        copy(0, sibling, me).wait_recv()
        for j, chip in enumerate(chips):
            copy(4 + j, (*chip, 1 - c), me).wait_recv()
        for cp in first + passed:
            cp.wait_send()
        mine.wait()

    return _pcall(
        body, name=name, out_shape=jax.ShapeDtypeStruct((N_DEV,) + block.shape, block.dtype),
        in_specs=[pl.BlockSpec(memory_space=pltpu.VMEM)], out_specs=pl.BlockSpec(memory_space=pltpu.VMEM),
        scratch_shapes=[pltpu.SemaphoreType.DMA((7,)), pltpu.SemaphoreType.DMA((7,)), pltpu.SemaphoreType.DMA],
        compiler_params=pltpu.CompilerParams(vmem_limit_bytes=VMEM_LIMIT_BYTES),
    )(block)


HBM_SPEC = pl.BlockSpec(memory_space=pltpu.HBM)
SEM_SPEC = pl.BlockSpec(memory_space=pltpu.SEMAPHORE)
ANY_SPEC = pl.BlockSpec(memory_space=pl.ANY)
EFFECT = pltpu.SideEffectType.DATAFLOW_SIDE_EFFECTING


def _xfer_start(name, bufs, plan, n_copies, after_last=False):
    nb = len(bufs)
    deps = (_Order.last,) if after_last and _Order.last is not None else ()
    nd = len(deps)

    def body(*refs):
        send_sems, recv_sems = refs[nb + nd], refs[nb + nd + 1]
        token = refs[nb + nd + 2 + nb]
        for k, (src, dst, dev) in enumerate(plan(refs[:nb], *_place())):
            pltpu.make_async_remote_copy(src_ref=src, dst_ref=dst, send_sem=send_sems.at[k], recv_sem=recv_sems.at[k],
                                         device_id=dev, device_id_type=MESH).start()
        token[...] = jnp.zeros_like(token)

    outs = pl.pallas_call(
        body, name=name,
        out_shape=(pltpu.SemaphoreType.DMA((n_copies,)), pltpu.SemaphoreType.DMA((n_copies,)),
                   *[pltpu.HBM(b.shape, b.dtype) for b in bufs], jax.ShapeDtypeStruct((8, LANES), F32)),
        in_specs=[HBM_SPEC] * nb + [ANY_SPEC] * nd,
        out_specs=(SEM_SPEC, SEM_SPEC, *[HBM_SPEC] * nb, pl.BlockSpec(memory_space=pltpu.VMEM)),
        input_output_aliases={i: 2 + i for i in range(nb)},
        compiler_params=pltpu.CompilerParams(has_side_effects=EFFECT),
    )(*[pltpu.with_memory_space_constraint(b, pltpu.HBM) for b in bufs], *deps)
    _Order.last = outs[-1]
    return (outs[0], outs[1]), list(outs[2:2 + nb])


def _xfer_wait(name, sems, bufs, plan):
    nb = len(bufs)

    def body(*refs):
        send_sems, recv_sems = refs[nb], refs[nb + 1]
        for k, (src, dst, dev) in enumerate(plan(refs[:nb], *_place())):
            copy = pltpu.make_async_remote_copy(src_ref=src, dst_ref=dst, send_sem=send_sems.at[k],
                                                recv_sem=recv_sems.at[k], device_id=dev, device_id_type=MESH)
            copy.wait_send()
            copy.wait_recv()

    outs = pl.pallas_call(
        body, name=name, out_shape=tuple(pltpu.HBM(b.shape, b.dtype) for b in bufs),
        in_specs=[HBM_SPEC] * nb + [SEM_SPEC, SEM_SPEC, ANY_SPEC], out_specs=tuple([HBM_SPEC] * nb),
        input_output_aliases={i: i for i in range(nb)},
        compiler_params=pltpu.CompilerParams(has_side_effects=EFFECT),
    )(*bufs, *sems, _Order.last)
    _Order.last = outs[0]
    return list(outs)


def _xfer_hand_over(name, sems, bufs, plan, next_plans):
    nb, n_next = len(bufs), len(next_plans)

    def body(*refs):
        send_sems, recv_sems = refs[nb], refs[nb + 1]
        outs = refs[nb + 3:]
        place = _place()
        for k, (src, dst, dev) in enumerate(plan(refs[:nb], *place)):
            copy = pltpu.make_async_remote_copy(src_ref=src, dst_ref=dst, send_sem=send_sems.at[k],
                                                recv_sem=recv_sems.at[k], device_id=dev, device_id_type=MESH)
            copy.wait_send()
            copy.wait_recv()
        for p, (next_plan, _) in enumerate(next_plans):
            for k, (src, dst, dev) in enumerate(next_plan(refs[:nb], *place)):
                pltpu.make_async_remote_copy(src_ref=src, dst_ref=dst, send_sem=outs[2 * p].at[k],
                                             recv_sem=outs[2 * p + 1].at[k], device_id=dev, device_id_type=MESH).start()
        outs[-1][...] = jnp.zeros_like(outs[-1])

    sem_shapes = [pltpu.SemaphoreType.DMA((copies,)) for _, copies in next_plans for _ in range(2)]
    outs = pl.pallas_call(
        body, name=name,
        out_shape=(*sem_shapes, *[pltpu.HBM(b.shape, b.dtype) for b in bufs], jax.ShapeDtypeStruct((8, LANES), F32)),
        in_specs=[HBM_SPEC] * nb + [SEM_SPEC, SEM_SPEC, ANY_SPEC],
        out_specs=(*[SEM_SPEC] * (2 * n_next), *[HBM_SPEC] * nb, pl.BlockSpec(memory_space=pltpu.VMEM)),
        input_output_aliases={i: 2 * n_next + i for i in range(nb)},
        compiler_params=pltpu.CompilerParams(has_side_effects=EFFECT),
    )(*bufs, *sems, _Order.last)
    _Order.last = outs[-1]
    return [(outs[2 * p], outs[2 * p + 1]) for p in range(n_next)], list(outs[2 * n_next:2 * n_next + nb])


def _half(ref, c, axis):
    rows = ref.shape[axis] // 2
    return pl.ds(c * rows, rows)


def _plan_weights_ici(n):
    def plan(refs, x, y, c, chips):
        out = []
        for w in range(n):
            region = refs[w].at[2 * x + y, _half(refs[w], c, 1), :]
            out += [(region, region, (*chip, c)) for chip in chips]
        return out
    return plan


def _plan_weights_ring(n):
    def plan(refs, x, y, c, chips):
        out = []
        for w in range(n):
            region = refs[w].at[2 * x + y, _half(refs[w], c, 1), :]
            out += [(region, region, (*chip, c)) for chip in chips[:2]]
        return out
    return plan


def _plan_weights_relay(n):
    def plan(refs, x, y, c, chips):
        out = []
        for w in range(n):
            quarter_rows = refs[w].shape[1] // 4
            upper = refs[w].at[2 * x + (1 - y), pl.ds(2 * c * quarter_rows, quarter_rows), :]
            lower = refs[w].at[2 * (1 - x) + y, pl.ds((2 * c + 1) * quarter_rows, quarter_rows), :]
            out += [(upper, upper, (1 - x, y, c)), (lower, lower, (x, 1 - y, c))]
        return out
    return plan


def _plan_weights_d2d(n, which=slice(0, 3)):
    def plan(refs, x, y, c, chips):
        out = []
        for w in range(n):
            rows = _half(refs[w], c, 1)
            for chip in chips[which]:
                region = refs[w].at[2 * chip[0] + chip[1], rows, :]
                out.append((region, region, (x, y, 1 - c)))
        return out
    return plan


def _plan_grads_d2d(n):
    def plan(refs, x, y, c, chips):
        return [(refs[w].at[:, _half(refs[w], 1 - c, 1), :], refs[n + w], (x, y, 1 - c)) for w in range(n)]
    return plan


def _plan_grads_ici(n):
    def plan(refs, x, y, c, chips):
        out = []
        for w in range(n):
            out += [(refs[w].at[2 * chip[0] + chip[1]], refs[n + w].at[2 * x + y], (*chip, c)) for chip in chips]
        return out
    return plan


def _plan_final_d2d(n):
    def plan(refs, x, y, c, chips):
        out = []
        for w in range(n):
            region = refs[w].at[_half(refs[w], c, 0), :]
            out.append((region, region, (x, y, 1 - c)))
        return out
    return plan


def _stream_blocks(hr, cols):
    bc = cols if cols <= 4096 else _pick(cols, 4096)
    return _pick_rows(hr, max(16, (768 * 1024) // bc)), bc


def _pre_reduce(g, landed, place, name):
    _, rows, cols = g.shape
    hr = rows // 2
    rb, bc = _stream_blocks(hr, cols)
    nrb = hr // rb

    def body(place_ref, g_ref, l_ref, o_ref):
        o_ref[...] = (g_ref[...].astype(F32) + l_ref[...].astype(F32)).astype(o_ref.dtype)

    return _pcall(
        body, name=name, num_scalar_prefetch=1, grid=(N_CHIPS, nrb, cols // bc),
        in_specs=[pl.BlockSpec((None, rb, bc), lambda j, i, k, p: (j, p[1] * nrb + i, k)),
                  pl.BlockSpec((None, rb, bc), lambda j, i, k, p: (j, i, k))],
        out_specs=pl.BlockSpec((None, rb, bc), lambda j, i, k, p: (j, i, k)),
        out_shape=jax.ShapeDtypeStruct((N_CHIPS, hr, cols), g.dtype),
        compiler_params=_params("parallel", "parallel", "parallel"),
    )(place, g, landed)


def _sum_slots(mine, landed, place, name):
    _, hr, cols = mine.shape
    rb, bc = _stream_blocks(hr, cols)
    rb = _pick_rows(hr, max(16, rb // 2))
    nrb = hr // rb

    def body(place_ref, m_ref, l_ref, o_ref):
        chip = place_ref[0]
        own = m_ref[...].astype(F32)
        total = jnp.where(chip == 0, own, l_ref[0].astype(F32))
        for j in range(1, N_CHIPS):
            total = total + jnp.where(chip == j, own, l_ref[j].astype(F32))
        o_ref[...] = total

    return _pcall(
        body, name=name, num_scalar_prefetch=1, grid=(nrb, cols // bc),
        in_specs=[pl.BlockSpec((None, rb, bc), lambda i, k, p: (p[0], i, k)),
                  pl.BlockSpec((N_CHIPS, rb, bc), lambda i, k, p: (0, i, k))],
        out_specs=pl.BlockSpec((rb, bc), lambda i, k, p: (p[1] * nrb + i, k)),
        out_shape=jax.ShapeDtypeStruct((2 * hr, cols), F32),
        compiler_params=_params("parallel", "parallel"),
    )(place, mine, landed)


class _WeightGather:
    @staticmethod
    def zones(quarters, chip):
        return [lax.dynamic_update_slice(lax.empty((N_CHIPS,) + q.shape, BF16), q.astype(BF16)[None], (chip, 0, 0))
                for q in quarters]

    def __init__(self, tag, zones, ring=False):
        self.tag, self.n, self.ring = tag, len(zones), ring
        self.plan = _plan_weights_ring(self.n) if ring else _plan_weights_ici(self.n)
        self.sems, self.bufs = _xfer_start("wici_start_" + tag, zones, self.plan, (2 if ring else 3) * self.n,
                                           after_last=True)

    def relay(self):
        plan = _plan_weights_relay(self.n)
        (self.sems,), self.bufs = _xfer_hand_over("wrel_start_" + self.tag, self.sems, self.bufs, self.plan,
                                                   [(plan, 2 * self.n)])
        self.plan = plan

    def pass_on(self):
        plan = _plan_weights_d2d(self.n)
        (self.sems,), self.bufs = _xfer_hand_over("wd2d_start_" + self.tag, self.sems, self.bufs, self.plan,
                                                   [(plan, 3 * self.n)])
        self.plan = plan

    def done(self):
        return _xfer_wait("wd2d_wait_" + self.tag, self.sems, self.bufs, self.plan)

    def relay_and_pass_near(self):
        plan, self.near_plan = _plan_weights_relay(self.n), _plan_weights_d2d(self.n, slice(0, 2))
        (self.sems, self.near_sems), self.bufs = _xfer_hand_over(
            "wrel_start_" + self.tag, self.sems, self.bufs, self.plan, [(plan, 2 * self.n), (self.near_plan, 2 * self.n)])
        self.plan = plan

    def near_done(self):
        self.bufs = _xfer_wait("wnear_wait_" + self.tag, self.near_sems, self.bufs, self.near_plan)
        return self.bufs

    def pass_far(self):
        plan = _plan_weights_d2d(self.n, slice(2, 3))
        (sems,), bufs = _xfer_hand_over("wfar_start_" + self.tag, self.sems, self.bufs, self.plan, [(plan, self.n)])
        return _xfer_wait("wfar_wait_" + self.tag, sems, bufs, plan)


class _GradReduce:
    def __init__(self, tag, grads):
        self.tag, self.n = tag, len(grads)
        zones = [lax.empty((N_CHIPS, g.shape[1] // 2, g.shape[2]), g.dtype) for g in grads]
        self.plan = _plan_grads_d2d(self.n)
        self.sems, self.bufs = _xfer_start("gd2d_start_" + tag, list(grads) + zones, self.plan, self.n)

    def pair(self, place):
        n = self.n
        bufs = _xfer_wait("gd2d_wait_" + self.tag, self.sems, self.bufs, self.plan)
        self.halves = [_pre_reduce(bufs[w], bufs[n + w], place, f"pre_reduce_{self.tag}{w}") for w in range(n)]

    def cross(self, after_last=False):
        zones = [lax.empty(h.shape, h.dtype) for h in self.halves]
        self.plan = _plan_grads_ici(self.n)
        self.sems, self.bufs = _xfer_start("gici_start_" + self.tag, self.halves + zones, self.plan, 3 * self.n,
                                           after_last=after_last)

    def step(self, place):
        self.pair(place)
        self.cross()

    def join(self, place):
        n = self.n
        bufs = _xfer_wait("gici_wait_" + self.tag, self.sems, self.bufs, self.plan)
        sums = [_sum_slots(bufs[w], bufs[n + w], place, f"sum_slots_{self.tag}{w}") for w in range(n)]
        self.plan = _plan_final_d2d(n)
        self.sems, self.bufs = _xfer_start("gfin_start_" + self.tag, sums, self.plan, n)

    def done(self):
        return _xfer_wait("gfin_wait_" + self.tag, self.sems, self.bufs, self.plan)


def _ada_fwd(c_all, w_q, b_q):
    d, n = w_q.shape
    bn = _pick(n, 512)

    def body(c_ref, w_ref, b_ref, o_ref):
        cv = c_ref[...]
        act = cv * _sigmoid(cv)
        o_ref[...] = _dot(act, w_ref[...], NN, lax.Precision.HIGHEST) + b_ref[...]

    return _pcall(
        body, name="ada_fwd", grid=(n // bn,),
        in_specs=[pl.BlockSpec((N_DEV, d), lambda j: (0, 0)), pl.BlockSpec((d, bn), lambda j: (0, j)),
                  pl.BlockSpec((1, bn), lambda j: (0, j))],
        out_specs=pl.BlockSpec((N_DEV, bn), lambda j: (0, j)),
        out_shape=jax.ShapeDtypeStruct((N_DEV, n), F32), compiler_params=_params("parallel"),
    )(c_all, w_q, b_q)


def _row_spec(rb, width, col=0):
    return pl.BlockSpec((rb, width), lambda i, col=col: (i, col))


def _vec_spec(width, col=0):
    return pl.BlockSpec((1, width), lambda i, col=col: (0, col))


def _norm_mod_fwd(x, g, sc, sh, name, res=None, gt=None):
    s, d = x.shape
    rb = _pick(s, 256)
    has_res = res is not None

    def body(*refs):
        if has_res:
            x_ref, res_ref, gt_ref, g_ref, sc_ref, sh_ref, x1_ref, h_ref = refs
            xv = x_ref[...] + gt_ref[...] * res_ref[...]
            x1_ref[...] = xv
        else:
            x_ref, g_ref, sc_ref, sh_ref, h_ref = refs
            xv = x_ref[...]
        xh, _ = _rms(xv)
        h_ref[...] = (xh * g_ref[...] * (1.0 + sc_ref[...]) + sh_ref[...]).astype(BF16)

    row, vec = _row_spec(rb, d), _vec_spec(d)
    if has_res:
        ins, in_specs = (x, res, gt, g, sc, sh), [row, row, vec, vec, vec, vec]
        out_shape = [jax.ShapeDtypeStruct((s, d), F32), jax.ShapeDtypeStruct((s, d), BF16)]
        out_specs = [row, row]
    else:
        ins, in_specs = (x, g, sc, sh), [row, vec, vec, vec]
        out_shape, out_specs = jax.ShapeDtypeStruct((s, d), BF16), row
    return _pcall(body, name=name, grid=(s // rb,), in_specs=in_specs, out_specs=out_specs,
                          out_shape=out_shape, compiler_params=_params("parallel"))(*ins)


def _final_loss(x1, f, gt2, g_final, target):
    s, d = x1.shape
    rb = _pick(s, 256)

    def body(x1_ref, f_ref, gt_ref, g_ref, t_ref, dx_ref, df_ref, loss_ref, dg_ref, dgt_ref):
        first = pl.program_id(0) == 0
        fv, gt, gv = f_ref[...], gt_ref[...], g_ref[...]
        x2 = x1_ref[...] + gt * fv
        xh, r = _rms(x2)
        err = xh * gv - t_ref[...]
        blk = 0.5 * jnp.sum(jnp.sum(err * err, axis=1, keepdims=True), axis=0, keepdims=True) / d
        dy = err / d
        dxh = dy * gv
        dx = r * (dxh - xh * jnp.mean(dxh * xh, axis=-1, keepdims=True))
        dx_ref[...] = dx
        df_ref[...] = (dx * gt).astype(BF16)
        _accumulate(first, loss_ref, jnp.broadcast_to(blk, (1, LANES)))
        _accumulate(first, dg_ref, _colsum(dy * xh))
        _accumulate(first, dgt_ref, _colsum(dx * fv))

    row, vec = _row_spec(rb, d), _vec_spec(d)
    return _pcall(
        body, name="final_loss", grid=(s // rb,), in_specs=[row, row, vec, vec, row],
        out_specs=[row, row, _vec_spec(LANES), vec, vec],
        out_shape=[jax.ShapeDtypeStruct((s, d), F32), jax.ShapeDtypeStruct((s, d), BF16),
                   jax.ShapeDtypeStruct((1, LANES), F32), jax.ShapeDtypeStruct((1, d), F32),
                   jax.ShapeDtypeStruct((1, d), F32)],
        compiler_params=_params("arbitrary"),
    )(x1, f, gt2, g_final, target)


def _norm_mod_bwd(dh, xin, dres, g, sc, name, branch=None, gt=None):
    s, d = xin.shape
    rb = _pick(s, 256)
    has_branch = branch is not None

    def body(*refs):
        if has_branch:
            dh_ref, x_ref, dres_ref, g_ref, sc_ref, br_ref, gt_ref, dx_ref, dbr_ref, dsh_ref, dsc_ref, dg_ref, dgt_ref = refs
        else:
            dh_ref, x_ref, dres_ref, g_ref, sc_ref, dx_ref, dsh_ref, dsc_ref, dg_ref = refs
        first = pl.program_id(0) == 0
        gv = g_ref[...]
        xh, r = _rms(x_ref[...])
        dhv = dh_ref[...]
        dn = dhv * (1.0 + sc_ref[...])
        dxh = dn * gv
        dx = dres_ref[...] + r * (dxh - xh * jnp.mean(dxh * xh, axis=-1, keepdims=True))
        dx_ref[...] = dx
        _accumulate(first, dsh_ref, _colsum(dhv))
        _accumulate(first, dsc_ref, _colsum(dhv * xh * gv))
        _accumulate(first, dg_ref, _colsum(dn * xh))
        if has_branch:
            dbr_ref[...] = (dx * gt_ref[...]).astype(BF16)
            _accumulate(first, dgt_ref, _colsum(dx * br_ref[...]))

    row, vec = _row_spec(rb, d), _vec_spec(d)
    vec_shape = jax.ShapeDtypeStruct((1, d), F32)
    if has_branch:
        ins, in_specs = (dh, xin, dres, g, sc, branch, gt), [row, row, row, vec, vec, row, vec]
        out_specs = [row, row, vec, vec, vec, vec]
        out_shape = [jax.ShapeDtypeStruct((s, d), F32), jax.ShapeDtypeStruct((s, d), BF16)] + [vec_shape] * 4
    else:
        ins, in_specs = (dh, xin, dres, g, sc), [row, row, row, vec, vec]
        out_specs = [row, vec, vec, vec]
        out_shape = [jax.ShapeDtypeStruct((s, d), F32)] + [vec_shape] * 3
    return _pcall(body, name=name, grid=(s // rb,), in_specs=in_specs, out_specs=out_specs,
                          out_shape=out_shape, compiler_params=_params("arbitrary"))(*ins)


def _gate_fwd(pa, pb, z, b_gate, gate_col):
    s, d = pa.shape
    rb = _pick(s, 256)

    def body(pa_ref, pb_ref, za_ref, zb_ref, ba_ref, bb_ref, y_ref):
        ga = _sigmoid(za_ref[...] + ba_ref[...])
        gb = _sigmoid(zb_ref[...] + bb_ref[...])
        y_ref[...] = (ga * pa_ref[...].astype(F32) + gb * pb_ref[...].astype(F32)).astype(BF16)

    row = _row_spec(rb, d)
    return _pcall(
        body, name="gate_fwd", grid=(s // rb,),
        in_specs=[row, row, _row_spec(rb, d, gate_col), _row_spec(rb, d, gate_col + 1), _vec_spec(d, 0), _vec_spec(d, 1)],
        out_specs=row, out_shape=jax.ShapeDtypeStruct((s, d), BF16), compiler_params=_params("parallel"),
    )(pa, pb, z, z, b_gate, b_gate)


def _out_bwd(dyo, w_out, pa, pb, z, b_gate, gate_col, bm=512, bn=1024):
    s, d = dyo.shape
    bm, bn = _pick(s, bm), _pick(d, bn)
    nj = d // bn

    def body(dyo_ref, w_ref, pa_ref, pb_ref, za_ref, zb_ref, ba_ref, bb_ref,
             dpa_ref, dpb_ref, dza_ref, dzb_ref, dba_ref, dbb_ref):
        first = pl.program_id(1) == 0
        dyv = _dot(dyo_ref[...], w_ref[...], NT)
        ga = _sigmoid(za_ref[...] + ba_ref[...])
        gb = _sigmoid(zb_ref[...] + bb_ref[...])
        dpa_ref[...] = (dyv * ga).astype(BF16)
        dpb_ref[...] = (dyv * gb).astype(BF16)
        dga = dyv * pa_ref[...].astype(F32) * ga * (1.0 - ga)
        dgb = dyv * pb_ref[...].astype(F32) * gb * (1.0 - gb)
        dza_ref[...] = dga.astype(BF16)
        dzb_ref[...] = dgb.astype(BF16)
        _accumulate(first, dba_ref, _colsum(dga))
        _accumulate(first, dbb_ref, _colsum(dgb))

    tile = pl.BlockSpec((bm, bn), lambda j, i: (i, j))
    vec = pl.BlockSpec((1, bn), lambda j, i: (0, j))
    act, vec_shape = jax.ShapeDtypeStruct((s, d), BF16), jax.ShapeDtypeStruct((1, d), F32)
    return _pcall(
        body, name="out_bwd", grid=(nj, s // bm),
        in_specs=[pl.BlockSpec((bm, d), lambda j, i: (i, 0)), pl.BlockSpec((bn, d), lambda j, i: (j, 0)), tile, tile,
                  pl.BlockSpec((bm, bn), lambda j, i: (i, gate_col * nj + j)),
                  pl.BlockSpec((bm, bn), lambda j, i: (i, (gate_col + 1) * nj + j)),
                  vec, pl.BlockSpec((1, bn), lambda j, i: (0, nj + j))],
        out_specs=[tile, tile, tile, tile, vec, vec], out_shape=[act, act, act, act, vec_shape, vec_shape],
        compiler_params=_params("parallel", "arbitrary"),
    )(dyo, w_out, pa, pb, z, z, b_gate, b_gate)


def _ffn_in(h, w_fi, bm=512, bn=1408):
    s, d = h.shape
    per = w_fi.shape[2]
    ff = 2 * per
    bm, bn = _pick(s, bm), _pick(per, bn)
    per_blocks = per // bn

    def body(h_ref, wa_ref, wu_ref, a_ref, u_ref, hf_ref):
        hv = h_ref[...]
        a = _dot(hv, wa_ref[...])
        up = _dot(hv, wu_ref[...])
        a_ref[...] = a.astype(BF16)
        u_ref[...] = up.astype(BF16)
        hf_ref[...] = (a * _sigmoid(a) * up).astype(BF16)

    out = pl.BlockSpec((bm, bn), lambda j, i: (i, j))
    shape = jax.ShapeDtypeStruct((s, ff), BF16)
    return _pcall(
        body, name="ffn_in", grid=(ff // bn, s // bm),
        in_specs=[pl.BlockSpec((bm, d), lambda j, i: (i, 0)),
                  pl.BlockSpec((None, d, bn), lambda j, i: (j // per_blocks, 0, j % per_blocks)),
                  pl.BlockSpec((None, d, bn), lambda j, i: (2 + j // per_blocks, 0, j % per_blocks))],
        out_specs=[out, out, out], out_shape=[shape, shape, shape],
        compiler_params=_params("parallel", "parallel"),
    )(h, w_fi, w_fi)


def _ffn_out_bwd(dffn, w_fo, a_act, up_act, bm=512, bn=1408):
    s, d = dffn.shape
    ff = w_fo.shape[0]
    bm, bn = _pick(s, bm), _pick(ff, bn)

    def body(d_ref, w_ref, a_ref, u_ref, da_ref, du_ref):
        dhf = _dot(d_ref[...], w_ref[...], NT)
        a = a_ref[...].astype(F32)
        sa = _sigmoid(a)
        da_ref[...] = (dhf * u_ref[...].astype(F32) * sa * (1.0 + a * (1.0 - sa))).astype(BF16)
        du_ref[...] = (dhf * a * sa).astype(BF16)

    tile = pl.BlockSpec((bm, bn), lambda j, i: (i, j))
    shape = jax.ShapeDtypeStruct((s, ff), BF16)
    return _pcall(
        body, name="ffn_out_bwd", grid=(ff // bn, s // bm),
        in_specs=[pl.BlockSpec((bm, d), lambda j, i: (i, 0)), pl.BlockSpec((bn, d), lambda j, i: (j, 0)), tile, tile],
        out_specs=[tile, tile], out_shape=[shape, shape], compiler_params=_params("parallel", "parallel"),
    )(dffn, w_fo, a_act, up_act)


def _tril(n):
    return lax.broadcasted_iota(jnp.int32, (n, n), 0) >= lax.broadcasted_iota(jnp.int32, (n, n), 1)


def _gmlp_norm(v, ln_g, ln_b):
    gv = _gelu(v)
    mu = jnp.mean(gv, axis=-1, keepdims=True)
    cen = gv - mu
    rs = lax.rsqrt(jnp.mean(cen * cen, axis=-1, keepdims=True) + EPS)
    xh = cen * rs
    return xh, rs, xh * ln_g + ln_b


def _gmlp_fwd(z, ln_g, ln_b, ws, bs_t):
    s = z.shape[0]
    gw = ln_g.shape[1]
    groups, chunk, _ = ws.shape

    def body(u_ref, v_ref, lg_ref, lb_ref, ws_ref, bs_ref, ya_ref):
        gu = _gelu(u_ref[...])
        _, _, vn = _gmlp_norm(v_ref[...], lg_ref[...], lb_ref[...])
        mask = _tril(chunk)
        for g in range(groups):
            cols = slice(g * LANES, (g + 1) * LANES)
            wm = jnp.where(mask, ws_ref[g], 0.0).astype(BF16)
            sg = _dot(wm, vn[:, cols].astype(BF16)) + bs_ref[:, g:g + 1]
            ya_ref[:, cols] = (gu[:, cols] * sg).astype(BF16)

    return _pcall(
        body, name="gmlp_fwd", grid=(s // chunk,),
        in_specs=[_row_spec(chunk, gw, 0), _row_spec(chunk, gw, 1), _vec_spec(gw), _vec_spec(gw),
                  pl.BlockSpec((groups, chunk, chunk), lambda i: (0, 0, 0)), pl.BlockSpec((chunk, LANES), lambda i: (0, 0))],
        out_specs=_row_spec(chunk, gw), out_shape=jax.ShapeDtypeStruct((s, gw), BF16),
        compiler_params=_params("parallel"),
    )(z, z, ln_g, ln_b, ws, bs_t)


def _gmlp_bwd(dya, z, ln_g, ln_b, ws, bs_t):
    s = z.shape[0]
    gw = ln_g.shape[1]
    groups, chunk, _ = ws.shape

    def body(dya_ref, u_ref, v_ref, lg_ref, lb_ref, ws_ref, bs_ref, duv_ref, dws_ref, dbs_ref, dlg_ref, dlb_ref, dvn_ref):
        first = pl.program_id(0) == 0
        u, v, lg = u_ref[...], v_ref[...], lg_ref[...]
        gu = _gelu(u)
        xh, rs, vn = _gmlp_norm(v, lg, lb_ref[...])
        dyav = dya_ref[...]
        mask = _tril(chunk)
        lane = lax.broadcasted_iota(jnp.int32, (chunk, LANES), 1)
        dbs = jnp.zeros((chunk, LANES), F32)
        for g in range(groups):
            cols = slice(g * LANES, (g + 1) * LANES)
            wm = jnp.where(mask, ws_ref[g], 0.0).astype(BF16)
            vg = vn[:, cols].astype(BF16)
            sg = _dot(wm, vg) + bs_ref[:, g:g + 1]
            ds = dyav[:, cols] * gu[:, cols]
            duv_ref[:, cols] = (dyav[:, cols] * sg * _gelu_grad(u[:, cols])).astype(BF16)
            dsb = ds.astype(BF16)
            _accumulate(first, dws_ref.at[g], jnp.where(mask, _dot(dsb, vg, NT), 0.0))
            dbs = dbs + jnp.where(lane == g, jnp.sum(ds, axis=-1, keepdims=True), 0.0)
            dvn_ref[:, cols] = _dot(wm, dsb, TN)
        dvn = dvn_ref[...]
        _accumulate(first, dbs_ref, dbs)
        _accumulate(first, dlb_ref, _colsum(dvn))
        _accumulate(first, dlg_ref, _colsum(dvn * xh))
        dxh = dvn * lg
        dgv = rs * (dxh - jnp.mean(dxh, axis=-1, keepdims=True) - xh * jnp.mean(dxh * xh, axis=-1, keepdims=True))
        duv_ref[:, gw:] = (dgv * _gelu_grad(v)).astype(BF16)

    return _pcall(
        body, name="gmlp_bwd", grid=(s // chunk,),
        in_specs=[_row_spec(chunk, gw), _row_spec(chunk, gw, 0), _row_spec(chunk, gw, 1), _vec_spec(gw), _vec_spec(gw),
                  pl.BlockSpec((groups, chunk, chunk), lambda i: (0, 0, 0)), pl.BlockSpec((chunk, LANES), lambda i: (0, 0))],
        out_specs=[_row_spec(chunk, 2 * gw), pl.BlockSpec((groups, chunk, chunk), lambda i: (0, 0, 0)),
                   pl.BlockSpec((chunk, LANES), lambda i: (0, 0)), _vec_spec(gw), _vec_spec(gw)],
        out_shape=[jax.ShapeDtypeStruct((s, 2 * gw), BF16), jax.ShapeDtypeStruct((groups, chunk, chunk), F32),
                   jax.ShapeDtypeStruct((chunk, LANES), F32), jax.ShapeDtypeStruct((1, gw), F32),
                   jax.ShapeDtypeStruct((1, gw), F32)],
        scratch_shapes=[pltpu.VMEM((chunk, gw), F32)],
        compiler_params=_params("arbitrary"),
    )(dya, z, z, ln_g, ln_b, ws, bs_t)


def _lower_bound(lb_ref):
    a0, a1 = lb_ref[0:1, :], lb_ref[1:2, :]
    mx = jnp.maximum(a0, a1)
    e0, e1 = jnp.exp(a0 - mx), jnp.exp(a1 - mx)
    return e0 / (e0 + e1)


def _sum_dot(mask, x):
    hi = x.astype(BF16)
    rest = x - hi.astype(F32)
    mid = rest.astype(BF16)
    low = (rest - mid.astype(F32)).astype(BF16)
    return _dot(mask, hi) + _dot(mask, mid) + _dot(mask, low)


def _ones_where(mask):
    return jnp.where(mask, 1.0, 0.0).astype(BF16)


def _hg_masks(rows, t):
    r = lax.broadcasted_iota(jnp.int32, (rows, rows), 0)
    c = lax.broadcasted_iota(jnp.int32, (rows, rows), 1)
    same = (r // t) == (c // t)
    incl = jnp.logical_and(same, c <= r)
    upto_mid = jnp.logical_and(same, (c % t) <= t // 2)
    rev = jnp.logical_and(same, c >= r)
    return same, incl, upto_mid, rev


def _hg_block(q, fp, lb, masks):
    rows = q.shape[0]
    same, incl, upto_mid, _ = masks
    sig = _sigmoid(fp)
    f = lb + (1.0 - lb) * sig
    k = 1.0 - f
    sq = _sigmoid(q)
    qa = q * sq
    stacked = jnp.concatenate([_ones_where(m) for m in (incl, same, upto_mid)], axis=0)
    sums = _sum_dot(stacked, jnp.log(f))
    b, b_last, b_mid = sums[:rows], sums[rows:2 * rows], sums[2 * rows:]
    e_q = jnp.exp(jnp.minimum(b - b_mid, EXP_CLAMP))
    e_k = jnp.exp(jnp.minimum(b_mid - b, EXP_CLAMP))
    e_in = jnp.exp(b)
    e_out = jnp.exp(b_last - b)
    return dict(sig=sig, f=f, k=k, sq=sq, qa=qa, e_last=jnp.exp(b_last), e_q=e_q, e_k=e_k, e_in=e_in, e_out=e_out,
                q_hat=(qa * e_q).astype(BF16), k_hat=(k * e_k).astype(BF16),
                q_in=(qa * e_in).astype(BF16), k_out=k * e_out)


def _hgrn_fwd(z, hg_lb, norm_g, q_col):
    s = z.shape[0]
    hw = norm_g.shape[1]
    heads = hw // LANES
    t = HG_CHUNK
    rows = min(HG_ROWS, s)
    per_step = rows // t
    hp = min(HG_HEADS_PER_STEP, heads)
    assert heads % hp == 0 and q_col % hp == 0, (heads, q_col)
    wide = hp * LANES

    def zspec(which):
        return pl.BlockSpec((rows, wide), lambda h, r, which=which: (r, (q_col + which * heads) // hp + h))

    def body(q_ref, f_ref, i_ref, g_ref, lb_ref, ng_ref, yb_ref, o_ref, st_out_ref, st_ref, e_last_ref, inter_ref):
        @pl.when(pl.program_id(1) == 0)
        def _():
            st_ref[...] = jnp.zeros_like(st_ref)

        masks = _hg_masks(rows, t)
        for hh in range(hp):
            cols = slice(hh * LANES, (hh + 1) * LANES)
            blk = _hg_block(q_ref[:, cols], f_ref[:, cols], _lower_bound(lb_ref.at[:, cols]), masks)
            iv = i_ref[:, cols].astype(BF16)
            q_in, k_out = blk["q_in"], blk["k_out"].astype(BF16)
            e_last_ref[hh] = blk["e_last"]
            attn = jnp.where(masks[1], _dot(blk["q_hat"], blk["k_hat"], NT), 0.0).astype(BF16)
            o = _dot(attn, iv)
            grown = [_dot(iv[j * t:(j + 1) * t], k_out[j * t:(j + 1) * t], TN) for j in range(per_step)]
            st = st_ref[hh]
            for j in range(per_step):
                st_out_ref[hh, j] = st
                inter_ref[hh, j * t:(j + 1) * t, :] = _dot(q_in[j * t:(j + 1) * t], st.astype(BF16), NT)
                st = st * e_last_ref[hh, j * t:j * t + 1, :] + grown[j]
            st_ref[hh] = st
            o = o + inter_ref[hh]
            o_ref[:, cols] = o
            og = g_ref[:, cols]
            on, _ = _rms(o)
            yb_ref[:, cols] = (on * ng_ref[:, cols] * (og * _sigmoid(og))).astype(BF16)

    out_row = pl.BlockSpec((rows, wide), lambda h, r: (r, h))
    return _pcall(
        body, name="hgrn_fwd", grid=(heads // hp, s // rows),
        in_specs=[zspec(0), zspec(1), zspec(2), zspec(3),
                  pl.BlockSpec((2, wide), lambda h, r: (0, h)), pl.BlockSpec((1, wide), lambda h, r: (0, h))],
        out_specs=[out_row, out_row, pl.BlockSpec((hp, per_step, LANES, LANES), lambda h, r: (h, r, 0, 0))],
        out_shape=[jax.ShapeDtypeStruct((s, hw), BF16), jax.ShapeDtypeStruct((s, hw), F32),
                   jax.ShapeDtypeStruct((heads, s // t, LANES, LANES), F32)],
        scratch_shapes=[pltpu.VMEM((hp, LANES, LANES), F32), pltpu.VMEM((hp, rows, LANES), F32),
                        pltpu.VMEM((hp, rows, LANES), F32)],
        compiler_params=_params("parallel", "arbitrary"),
    )(z, z, z, z, hg_lb, norm_g)


def _hgrn_bwd(dyb, z, o_raw, states, hg_lb, norm_g, q_col):
    s = z.shape[0]
    hw = norm_g.shape[1]
    heads = hw // LANES
    t = HG_CHUNK
    rows = min(HG_ROWS, s)
    per_step = rows // t
    n_steps = s // rows
    hp = min(HG_HEADS_PER_STEP, heads)
    assert heads % hp == 0 and q_col % hp == 0, (heads, q_col)
    wide = hp * LANES

    def zspec(which):
        return pl.BlockSpec((rows, wide), lambda h, r, which=which: (n_steps - 1 - r, (q_col + which * heads) // hp + h))

    def body(dyb_ref, q_ref, f_ref, i_ref, g_ref, o_ref, st_in_ref, lb_ref, ng_ref,
             dq_ref, df_ref, di_ref, dg_ref, dlb_ref, dng_ref, dst_ref, acc_lb_ref, acc_ng_ref,
             e_last_ref, dq_in_ref, dk_out_ref, di_inter_ref, carry_ref):
        step = pl.program_id(1)

        @pl.when(step == 0)
        def _():
            dst_ref[...] = jnp.zeros_like(dst_ref)
            acc_lb_ref[...] = jnp.zeros_like(acc_lb_ref)
            acc_ng_ref[...] = jnp.zeros_like(acc_ng_ref)

        masks = _hg_masks(rows, t)
        same, incl, _, rev = masks
        sum_mask = jnp.concatenate([_ones_where(rev), _ones_where(same)], axis=1)
        for hh in range(hp):
            cols = slice(hh * LANES, (hh + 1) * LANES)
            lb = _lower_bound(lb_ref.at[:, cols])
            ng = ng_ref[:, cols]
            q = q_ref[:, cols]
            blk = _hg_block(q, f_ref[:, cols], lb, masks)
            iv = i_ref[:, cols].astype(BF16)
            o, og, dy = o_ref[:, cols], g_ref[:, cols], dyb_ref[:, cols]
            so = _sigmoid(og)
            on, r = _rms(o)
            acc_ng_ref[:, cols] += _colsum(dy * on * (og * so))
            dg_ref[:, cols] = (dy * on * ng * so * (1.0 + og * (1.0 - so))).astype(BF16)
            don = dy * ng * (og * so)
            do = (r * (don - on * jnp.mean(don * on, axis=-1, keepdims=True))).astype(BF16)
            q_hat, k_hat, q_in, k_out = blk["q_hat"], blk["k_hat"], blk["q_in"], blk["k_out"]
            k_out_b = k_out.astype(BF16)
            attn = jnp.where(incl, _dot(q_hat, k_hat, NT), 0.0).astype(BF16)
            d_attn = jnp.where(incl, _dot(do, iv, NT), 0.0).astype(BF16)
            di_intra = _dot(attn, do, TN)
            dq_hat = _dot(d_attn, k_hat)
            dk_hat = _dot(d_attn, q_hat, TN)
            e_last_ref[hh] = blk["e_last"]
            grown = [_dot(do[j * t:(j + 1) * t], q_in[j * t:(j + 1) * t], TN) for j in range(per_step)]
            dst = dst_ref[hh]
            for j in reversed(range(per_step)):
                rs_ = slice(j * t, (j + 1) * t)
                e_last = e_last_ref[hh, j * t:j * t + 1, :]
                st_prev, dst_b = st_in_ref[hh, j], dst.astype(BF16)
                dq_in_ref[hh, rs_, :] = _dot(do[rs_], st_prev.astype(BF16))
                dk_out_ref[hh, rs_, :] = _dot(iv[rs_], dst_b)
                di_inter_ref[hh, rs_, :] = _dot(k_out_b[rs_], dst_b, NT)
                carry_ref[hh, rs_, :] = jnp.broadcast_to(e_last * _colsum(st_prev * dst), (t, LANES))
                dst = dst * e_last + grown[j]
            dst_ref[hh] = dst
            di_ref[:, cols] = (di_intra + di_inter_ref[hh]).astype(BF16)
            dk_out = dk_out_ref[hh]
            dqa = dq_in_ref[hh] * blk["e_in"] + dq_hat * blk["e_q"]
            dk = dk_out * blk["e_out"] + dk_hat * blk["e_k"]
            db = blk["qa"] * dqa - blk["k"] * dk
            dlf = _sum_dot(sum_mask, jnp.concatenate([db, dk_out * k_out], axis=0)) + carry_ref[hh]
            dfv = dlf / blk["f"] - dk
            sig, sq = blk["sig"], blk["sq"]
            df_ref[:, cols] = (dfv * (1.0 - lb) * sig * (1.0 - sig)).astype(BF16)
            acc_lb_ref[:, cols] += _colsum(dfv * (1.0 - sig))
            dq_ref[:, cols] = (dqa * sq * (1.0 + q * (1.0 - sq))).astype(BF16)

        @pl.when(step == n_steps - 1)
        def _():
            lb = _lower_bound(lb_ref)
            d0 = acc_lb_ref[...] * lb * (1.0 - lb)
            dlb_ref[0:1, :] = d0
            dlb_ref[1:2, :] = -d0
            dng_ref[...] = acc_ng_ref[...]

    rev_row = pl.BlockSpec((rows, wide), lambda h, r: (n_steps - 1 - r, h))
    piece = jax.ShapeDtypeStruct((s, hw), BF16)
    return _pcall(
        body, name="hgrn_bwd", grid=(heads // hp, n_steps),
        in_specs=[rev_row, zspec(0), zspec(1), zspec(2), zspec(3), rev_row,
                  pl.BlockSpec((hp, per_step, LANES, LANES), lambda h, r: (h, n_steps - 1 - r, 0, 0)),
                  pl.BlockSpec((2, wide), lambda h, r: (0, h)), pl.BlockSpec((1, wide), lambda h, r: (0, h))],
        out_specs=[rev_row, rev_row, rev_row, rev_row,
                   pl.BlockSpec((2, wide), lambda h, r: (0, h)), pl.BlockSpec((1, wide), lambda h, r: (0, h))],
        out_shape=[piece, piece, piece, piece, jax.ShapeDtypeStruct((2, hw), F32), jax.ShapeDtypeStruct((1, hw), F32)],
        scratch_shapes=[pltpu.VMEM((hp, LANES, LANES), F32), pltpu.VMEM((1, wide), F32), pltpu.VMEM((1, wide), F32)]
        + [pltpu.VMEM((hp, rows, LANES), F32)] * 5,
        compiler_params=_params("parallel", "arbitrary"),
    )(dyb, z, z, z, z, o_raw, states, hg_lb, norm_g)


def _adam_update(w, m, v, g):
    m2 = ADAM_B1 * m + (1.0 - ADAM_B1) * g
    v2 = ADAM_B2 * v + (1.0 - ADAM_B2) * (g * g)
    m_hat = m2 * (1.0 / (1.0 - ADAM_B1 ** ADAM_STEP))
    v_hat = v2 * (1.0 / (1.0 - ADAM_B2 ** ADAM_STEP))
    return -ADAM_LR * (m_hat / (jnp.sqrt(v_hat) + ADAM_EPS) + ADAM_WD * w), m2, v2


def _adamw(w, m, v, parts, name, outer=False):
    rows, cols = w.shape
    bc = cols if cols <= 4096 else _pick(cols, 4096)
    rb = _pick_rows(rows, max(8, (384 * 1024) // bc), mult=8)
    if outer and rb % LANES:
        rb = rows

    def body(w_ref, m_ref, v_ref, *refs):
        g_ref, d_ref, mo_ref, vo_ref = refs[-4:]
        if outer:
            cv = refs[0][...]
            g = _dot(cv * _sigmoid(cv), refs[1][...], TN, lax.Precision.HIGHEST)
        else:
            p_ref = refs[0]
            g = p_ref[0].astype(F32)
            for p in range(1, p_ref.shape[0]):
                g = g + p_ref[p].astype(F32)
        g_ref[...] = g
        d_ref[...], mo_ref[...], vo_ref[...] = _adam_update(w_ref[...], m_ref[...], v_ref[...], g)

    blk = pl.BlockSpec((rb, bc), lambda i, j: (i, j))
    out = jax.ShapeDtypeStruct((rows, cols), F32)
    if outer:
        grad_specs = [pl.BlockSpec((N_DEV, rb), lambda i, j: (0, i)), pl.BlockSpec((N_DEV, bc), lambda i, j: (0, j))]
        grad_ins = tuple(parts)
    else:
        grad_specs = [pl.BlockSpec((parts.shape[0], rb, bc), lambda i, j: (0, i, j))]
        grad_ins = (parts,)
    return _pcall(
        body, name=name, grid=(rows // rb, cols // bc), in_specs=[blk, blk, blk] + grad_specs,
        out_specs=[blk] * 4, out_shape=[out] * 4, compiler_params=_params("parallel", "parallel"),
    )(w, m, v, *grad_ins)


SMALL = ("b_ada", "norm1_g", "b_gate", "gmlp_ln_g", "gmlp_ln_b", "gmlp_ws", "gmlp_bs", "hg_lb", "hg_norm_g",
         "norm2_g", "final_norm_g")
BIG = ("w_in", "w_branch_gmlp", "w_branch_hg", "w_out", "w_ffn_in", "w_ffn_out")
WEIGHTS = ("w_ada", "b_ada", "norm1_g", "w_in", "b_gate", "gmlp_ln_g", "gmlp_ln_b", "gmlp_ws", "gmlp_bs", "hg_lb",
           "hg_norm_g", "w_branch_gmlp", "w_branch_hg", "w_out", "norm2_g", "w_ffn_in", "w_ffn_out", "final_norm_g")


def _pack(parts):
    return jnp.concatenate([p.reshape(-1, LANES) for p in parts], axis=0)


def _step(x, c, loss_target, w, m, v):
    s, d = x.shape[1], x.shape[2]
    gw = w["gmlp_ln_g"].shape[-1]
    hw = w["hg_norm_g"].shape[-1]
    x2d, tgt = x[0], loss_target[0]
    mx, my, mc = lax.axis_index("x"), lax.axis_index("y"), lax.axis_index("c")
    chip = 2 * mx + my
    dev = 2 * chip + mc
    q_col = 2 * gw // LANES
    gate_col = (2 * gw + 4 * hw) // d
    place = jnp.stack([chip, mc]).astype(jnp.int32)
    _Order.last = None

    c_all = _all_gather8(c.reshape(-1, LANES), "gather_c").reshape(N_DEV, d)
    gather_in = _WeightGather("in", _WeightGather.zones([w["w_in"][0]], chip), ring=True)
    zones_mix = _WeightGather.zones([w[n][0] for n in ("w_branch_gmlp", "w_branch_hg", "w_out")], chip)
    zones_fi = _WeightGather.zones([w["w_ffn_in"][0]], chip)
    zones_fo = _WeightGather.zones([w["w_ffn_out"][0]], chip)
    _Order.also = (*zones_mix, *zones_fi, *zones_fo)
    n_ada = w["w_ada"].shape[-1]
    b_ada_q = lax.dynamic_slice(w["b_ada"], (0, chip * n_ada), (1, n_ada))
    mod_q = _ada_fwd(c_all, w["w_ada"][0], b_ada_q)
    mod_all = _all_gather8(mod_q, "gather_mod")
    mod = lax.dynamic_index_in_dim(mod_all, dev, axis=1, keepdims=False)[::2].reshape(1, 6 * d)
    sh1, sc1, gt1, sh2, sc2, gt2 = [mod[:, i * d:(i + 1) * d] for i in range(6)]

    gather_in.relay_and_pass_near()
    gather_mix = _WeightGather("mix", zones_mix)
    gather_fi = _WeightGather("fi", zones_fi, ring=True)

    norm1_g, norm2_g, final_g = w["norm1_g"], w["norm2_g"], w["final_norm_g"].reshape(1, d)
    ln_g, ln_b = w["gmlp_ln_g"], w["gmlp_ln_b"]
    ws = w["gmlp_ws"][0]
    groups = ws.shape[0]
    bs_t = jnp.pad(w["gmlp_bs"][0].T, ((0, 0), (0, LANES - groups)))
    hg_lb, hg_ng, b_gate = w["hg_lb"], w["hg_norm_g"], w["b_gate"]

    h1 = _norm_mod_fwd(x2d, norm1_g, sc1, sh1, "norm1_fwd")
    order = jnp.stack([chip, 2 * (1 - mx) + my, 2 * mx + (1 - my), 2 * (1 - mx) + (1 - my)]).astype(jnp.int32)
    w_in, = gather_in.near_done()
    z = _matmul_quarters(h1, w_in, order, 0, 3, "mm_z_near")
    w_in, = gather_in.pass_far()
    z = _matmul_quarters(h1, w_in, order, 3, 1, "mm_z_far", prev=z)
    gather_mix.pass_on()
    ya = _gmlp_fwd(z, ln_g, ln_b, ws, bs_t)
    yb, o_raw, states = _hgrn_fwd(z, hg_lb, hg_ng, q_col)
    gather_fi.relay()
    gather_fo = _WeightGather("fo", zones_fo)
    w_bg, w_bh, w_out = gather_mix.done()
    w_out = w_out.reshape(-1, w_out.shape[-1])
    pa = _matmul(ya, w_bg, mode="nn", name="mm_pa", out_dtype=BF16, b_slots=True)
    pb = _matmul(yb, w_bh, mode="nn", name="mm_pb", out_dtype=BF16, b_slots=True)
    y = _gate_fwd(pa, pb, z, b_gate, gate_col)
    gather_fi.pass_on()
    yo = _matmul(y, w_out, mode="nn", name="mm_yo", out_dtype=F32)
    x1, h2 = _norm_mod_fwd(x2d, norm2_g, sc2, sh2, "norm2_fwd", res=yo, gt=gt1)
    w_fi, = gather_fi.done()
    a_act, up_act, hf = _ffn_in(h2, w_fi)
    gather_fo.pass_on()
    w_fo, = gather_fo.done()
    w_fo = w_fo.reshape(-1, w_fo.shape[-1])
    ffn = _matmul(hf, w_fo, mode="nn", name="mm_ffn", out_dtype=F32)
    dx2, dffn, loss_row, d_final_g, d_gt2 = _final_loss(x1, ffn, gt2, final_g, tgt)

    g_fo = _matmul(hf, dffn, mode="tn", name="mm_g_fo", out_dtype=BF16, bm=1408)
    daup = tuple(_ffn_out_bwd(dffn, w_fo, a_act, up_act))
    g_fi = _matmul(h2, daup, mode="tn", name="mm_g_fi", out_dtype=BF16, out_slots=True, bn=1408)
    red_ffn = _GradReduce("ffn", [g_fo.reshape(N_CHIPS, -1, g_fo.shape[-1]), g_fi])
    dh2 = _matmul(daup, w_fi, mode="nt", name="mm_dh2", out_dtype=F32, b_slots=True)
    red_ffn.step(place)
    dx1, dyo, d_sh2, d_sc2, d_norm2, d_gt1 = _norm_mod_bwd(dh2, x1, dx2, norm2_g, sc2, "norm2_bwd", branch=yo, gt=gt1)
    g_out = _matmul(y, dyo, mode="tn", name="mm_g_out", out_dtype=BF16)
    dpa, dpb, dz_ga, dz_gb, d_b_ga, d_b_gb = _out_bwd(dyo, w_out, pa, pb, z, b_gate, gate_col)
    g_bg = _matmul(ya, dpa, mode="tn", name="mm_g_bg", out_dtype=BF16, out_slots=True)
    g_bh = _matmul(yb, dpb, mode="tn", name="mm_g_bh", out_dtype=BF16, out_slots=True)
    red_mix = _GradReduce("mix", [g_out.reshape(N_CHIPS, -1, g_out.shape[-1]), g_bg, g_bh])
    dya = _matmul(dpa, w_bg, mode="nt", name="mm_dya", out_dtype=F32, b_slots=True)
    dyb = _matmul(dpb, w_bh, mode="nt", name="mm_dyb", out_dtype=F32, b_slots=True)
    red_mix.step(place)
    dz_uv, d_ws, d_bs_t, d_ln_g, d_ln_b = _gmlp_bwd(dya, z, ln_g, ln_b, ws, bs_t)
    dz_q, dz_f, dz_i, dz_g, d_hg_lb, d_hg_ng = _hgrn_bwd(dyb, z, o_raw, states, hg_lb, hg_ng, q_col)
    dz = jnp.concatenate([dz_uv, dz_q, dz_f, dz_i, dz_g, dz_ga, dz_gb], axis=1)
    grad, delta, new_m, new_v = {}, {}, {}, {}

    def update(n, parts, outer=False):
        outs = _adamw(w[n][0], m[n][0], v[n][0], parts, "adamw_" + n, outer=outer)
        grad[n], delta[n], new_m[n], new_v[n] = [o[None] for o in outs]

    g_in = _matmul(h1, dz, mode="tn", name="mm_g_in", out_dtype=BF16, out_slots=True, bn=1280)
    red_in = _GradReduce("in", [g_in])
    red_ffn.join(place)
    red_mix.join(place)
    red_in.step(place)
    g_fo, g_fi = red_ffn.done()
    n = "w_ffn_in"
    dh1, rode = _matmul(dz, w_in, mode="nt", name="mm_dh1", out_dtype=F32, b_slots=True,
                        rider=([w[n][0], m[n][0], v[n][0], g_fi], lambda *blk: (blk[3], *_adam_update(*blk)), 4))
    grad[n], delta[n], new_m[n], new_v[n] = [o[None] for o in rode]
    grad_x, d_sh1, d_sc1, d_norm1 = _norm_mod_bwd(dh1, x2d, dx1, norm1_g, sc1, "norm1_bwd")
    update("w_ffn_out", g_fo[None])
    g_out, g_bg, g_bh = red_mix.done()
    update("w_out", g_out[None])
    update("w_branch_gmlp", g_bg[None])
    update("w_branch_hg", g_bh[None])

    d_mod = jnp.concatenate([d_sh1, d_sc1, d_gt1, d_sh2, d_sc2, d_gt2], axis=1)
    small_part = {"b_ada": d_mod, "norm1_g": d_norm1, "b_gate": jnp.concatenate([d_b_ga, d_b_gb], axis=1), "gmlp_ln_g": d_ln_g, "gmlp_ln_b": d_ln_b,
                  "gmlp_ws": d_ws, "gmlp_bs": d_bs_t[:, :groups].T, "hg_lb": d_hg_lb, "hg_norm_g": d_hg_ng,
                  "norm2_g": d_norm2, "final_norm_g": d_final_g}
    small_all = _all_gather8(_pack([small_part[n] for n in SMALL]), "gather_small")
    d_mod_all = small_all[:, :6 * d // LANES].reshape(N_DEV, 6 * d)
    d_mod_q = lax.dynamic_slice(d_mod_all, (0, chip * n_ada), (N_DEV, n_ada))
    red_in.join(place)
    update("w_ada", (c_all, d_mod_q), outer=True)
    outs = _adamw(_pack([w[n] for n in SMALL]), _pack([m[n] for n in SMALL]), _pack([v[n] for n in SMALL]),
                  small_all, "adamw_small")
    update("w_in", red_in.done()[0][None])
    row = 0
    for n in SMALL:
        cnt = w[n].size // LANES
        for dst, o in zip((grad, delta, new_m, new_v), outs):
            dst[n] = o[row:row + cnt].reshape(w[n].shape)
        row += cnt

    loss = lax.psum(loss_row[0, 0], ("x", "y", "c"))
    return (loss, grad_x[None], *[grad[n] for n in WEIGHTS], *[delta[n] for n in WEIGHTS],
            *[new_m[n] for n in WEIGHTS], *[new_v[n] for n in WEIGHTS])


def kernel(x, c, w_ada, b_ada, norm1_g, w_in, b_gate, gmlp_ln_g, gmlp_ln_b, gmlp_ws, gmlp_bs, hg_lb, hg_norm_g, w_branch_gmlp, w_branch_hg, w_out, norm2_g, w_ffn_in, w_ffn_out, final_norm_g, loss_target, m_w_ada, m_b_ada, m_norm1_g, m_w_in, m_b_gate, m_gmlp_ln_g, m_gmlp_ln_b, m_gmlp_ws, m_gmlp_bs, m_hg_lb, m_hg_norm_g, m_w_branch_gmlp, m_w_branch_hg, m_w_out, m_norm2_g, m_w_ffn_in, m_w_ffn_out, m_final_norm_g, v_w_ada, v_b_ada, v_norm1_g, v_w_in, v_b_gate, v_gmlp_ln_g, v_gmlp_ln_b, v_gmlp_ws, v_gmlp_bs, v_hg_lb, v_hg_norm_g, v_w_branch_gmlp, v_w_branch_hg, v_w_out, v_norm2_g, v_w_ffn_in, v_w_ffn_out, v_final_norm_g):
    w = dict(w_ada=w_ada, b_ada=b_ada, norm1_g=norm1_g, w_in=w_in, b_gate=b_gate, gmlp_ln_g=gmlp_ln_g,
             gmlp_ln_b=gmlp_ln_b, gmlp_ws=gmlp_ws, gmlp_bs=gmlp_bs, hg_lb=hg_lb, hg_norm_g=hg_norm_g,
             w_branch_gmlp=w_branch_gmlp, w_branch_hg=w_branch_hg, w_out=w_out, norm2_g=norm2_g,
             w_ffn_in=w_ffn_in, w_ffn_out=w_ffn_out, final_norm_g=final_norm_g)
    m = dict(w_ada=m_w_ada, b_ada=m_b_ada, norm1_g=m_norm1_g, w_in=m_w_in, b_gate=m_b_gate, gmlp_ln_g=m_gmlp_ln_g,
             gmlp_ln_b=m_gmlp_ln_b, gmlp_ws=m_gmlp_ws, gmlp_bs=m_gmlp_bs, hg_lb=m_hg_lb, hg_norm_g=m_hg_norm_g,
             w_branch_gmlp=m_w_branch_gmlp, w_branch_hg=m_w_branch_hg, w_out=m_w_out, norm2_g=m_norm2_g,
             w_ffn_in=m_w_ffn_in, w_ffn_out=m_w_ffn_out, final_norm_g=m_final_norm_g)
    v = dict(w_ada=v_w_ada, b_ada=v_b_ada, norm1_g=v_norm1_g, w_in=v_w_in, b_gate=v_b_gate, gmlp_ln_g=v_gmlp_ln_g,
             gmlp_ln_b=v_gmlp_ln_b, gmlp_ws=v_gmlp_ws, gmlp_bs=v_gmlp_bs, hg_lb=v_hg_lb, hg_norm_g=v_hg_norm_g,
             w_branch_gmlp=v_w_branch_gmlp, w_branch_hg=v_w_branch_hg, w_out=v_w_out, norm2_g=v_norm2_g,
             w_ffn_in=v_w_ffn_in, w_ffn_out=v_w_ffn_out, final_norm_g=v_final_norm_g)
    return _step(x, c, loss_target, w, m, v)
```

```python
import functools

import jax
import jax.numpy as jnp
from jax import lax
from jax.experimental import pallas as pl
from jax.experimental.pallas import tpu as pltpu

F32 = jnp.float32
BF16 = jnp.bfloat16
EPS = 1e-6
LANES = 128
N_CHIPS = 4
N_DEV = 8
VMEM_LIMIT_BYTES = 56 * 1024 * 1024
VMEM_RIDER_LIMIT_BYTES = 60 * 1024 * 1024
HG_CHUNK = 32
HG_ROWS = 256
HG_HEADS_PER_STEP = 8
EXP_CLAMP = 80.0
ADAM_LR, ADAM_B1, ADAM_B2, ADAM_EPS, ADAM_WD, ADAM_STEP = 0.001, 0.9, 0.999, 1e-08, 0.01, 10
MESH = pl.DeviceIdType.MESH

NN = (((1,), (0,)), ((), ()))
NT = (((1,), (1,)), ((), ()))
TN = (((0,), (0,)), ((), ()))


def _dot(a, b, dims=NN, precision=None):
    return lax.dot_general(a, b, dims, precision=precision, preferred_element_type=F32)


def _params(*semantics, vmem_limit_bytes=VMEM_LIMIT_BYTES):
    return pltpu.CompilerParams(dimension_semantics=semantics, vmem_limit_bytes=vmem_limit_bytes)


class _Order:
    last = None
    also = ()


def _pcall(body, *, in_specs, out_specs, grid=(), scratch_shapes=(), num_scalar_prefetch=0, **kw):
    def run(*ins):
        deps = (() if _Order.last is None else (_Order.last,)) + tuple(_Order.also)
        _Order.also = ()
        n_in, n_dep = len(ins), len(deps)

        def wrapped(*refs):
            body(*refs[:n_in], *refs[n_in + n_dep:])

        specs = list(in_specs) + [pl.BlockSpec(memory_space=pl.ANY)] * n_dep
        if num_scalar_prefetch:
            grid_spec = pltpu.PrefetchScalarGridSpec(
                num_scalar_prefetch=num_scalar_prefetch, grid=grid, in_specs=specs, out_specs=out_specs,
                scratch_shapes=scratch_shapes)
            outs = pl.pallas_call(wrapped, grid_spec=grid_spec, **kw)(*ins, *deps)
        else:
            outs = pl.pallas_call(wrapped, grid=grid, in_specs=specs, out_specs=out_specs,
                                  scratch_shapes=scratch_shapes, **kw)(*ins, *deps)
        _Order.last = jax.tree.leaves(outs)[0]
        return outs

    return run


def _pick_rows(dim, pref, mult=16):
    best = None
    for cand in range(mult, min(dim, pref) + 1, mult):
        if dim % cand == 0:
            best = cand
    assert best is not None, (dim, pref)
    return best


def _pick(dim, pref):
    if dim <= pref:
        return dim
    best = None
    for cand in range(LANES, pref + 1, LANES):
        if dim % cand == 0:
            best = cand
    assert best is not None, (dim, pref)
    return best


def _sigmoid(x):
    return 1.0 / (1.0 + jnp.exp(-x))


def _gelu(x):
    c = 0.7978845608028654
    return 0.5 * x * (1.0 + jnp.tanh(c * (x + 0.044715 * x * x * x)))


def _gelu_grad(x):
    c = 0.7978845608028654
    t = jnp.tanh(c * (x + 0.044715 * x * x * x))
    return 0.5 * (1.0 + t) + 0.5 * x * (1.0 - t * t) * c * (1.0 + 3.0 * 0.044715 * x * x)


def _rms(x):
    r = lax.rsqrt(jnp.mean(x * x, axis=-1, keepdims=True) + EPS)
    return x * r, r


def _colsum(x):
    return jnp.sum(x, axis=0, keepdims=True)


def _accumulate(first, ref, val):
    @pl.when(first)
    def _():
        ref[...] = val

    @pl.when(jnp.logical_not(first))
    def _():
        ref[...] += val


def _matmul(a, b, *, mode, name, out_dtype, b_slots=False, out_slots=False, bm=1024, bn=1024, bk=2816, rider=None):
    pair = a if isinstance(a, tuple) else b if isinstance(b, tuple) else None
    if mode == "nn":
        m, k = a.shape
        n = b.shape[2] * N_CHIPS if b_slots else b.shape[1]
        per = b.shape[2] if b_slots else n
    elif mode == "nt":
        m, k = (a[0].shape[0], 2 * a[0].shape[1]) if pair else a.shape
        n = b.shape[1] if b_slots else b.shape[0]
        per = b.shape[2] if b_slots else k
    else:
        k, m = a.shape
        n = 2 * b[0].shape[1] if pair else b.shape[1]
        per = n // N_CHIPS if out_slots else n
    bm = _pick(m, bm)
    if mode == "nt":
        bn, bk = _pick(n, bn), _pick(per, bk)
    else:
        bn, bk = _pick(per, bn), _pick(k, bk)
    nk = k // bk
    per_blocks = per // (bk if mode == "nt" else bn)
    dims = {"nn": NN, "nt": NT, "tn": TN}[mode]
    half = (nk if mode == "nt" else n // bn) // 2

    def product(a_ref, b_ref, o_ref, acc):
        part = _dot(a_ref[...], b_ref[...], dims)
        if nk == 1:
            o_ref[...] = part.astype(o_ref.dtype)
            return
        acc_ref, kk = acc[0], pl.program_id(2)

        @pl.when(kk == 0)
        def _():
            acc_ref[...] = part

        @pl.when(jnp.logical_and(kk > 0, kk < nk - 1))
        def _():
            acc_ref[...] += part

        @pl.when(kk == nk - 1)
        def _():
            o_ref[...] = (acc_ref[...] + part).astype(o_ref.dtype)

    n_ride, n_rode = (len(rider[0]), rider[2]) if rider else (0, 0)

    def body(*refs):
        if rider:
            ride_in, ride_out = refs[2:2 + n_ride], refs[3 + n_ride:3 + n_ride + n_rode]

            def ride(t, carry):
                rows = pl.ds(pl.multiple_of(t * 8, 8), 8)
                for ref, val in zip(ride_out, rider[1](*[r[rows, :] for r in ride_in])):
                    ref[rows, :] = val
                return carry

            lax.fori_loop(0, ride_in[0].shape[0] // 8, ride, 0)
        if not pair:
            return product(refs[0], refs[1], refs[2 + n_ride], refs[3 + n_ride + n_rode:])
        first = pl.program_id(2 if mode == "nt" else 1) < half
        x1, x2, y, o_ref, acc = refs[0], refs[1], refs[2], refs[3], refs[4:]

        @pl.when(first)
        def _():
            product(*((x1, y) if mode == "nt" else (y, x1)), o_ref, acc)

        @pl.when(jnp.logical_not(first))
        def _():
            product(*((x2, y) if mode == "nt" else (y, x2)), o_ref, acc)

    if mode == "nn":
        a_spec = pl.BlockSpec((bm, bk), lambda i, j, kk: (i, kk))
        if b_slots:
            b_spec = pl.BlockSpec((None, bk, bn), lambda i, j, kk: (j // per_blocks, kk, j % per_blocks))
        else:
            b_spec = pl.BlockSpec((bk, bn), lambda i, j, kk: (kk, j))
    elif mode == "nt":
        a_spec = pl.BlockSpec((bm, bk), lambda i, j, kk: (i, kk))
        if b_slots:
            b_spec = pl.BlockSpec((None, bn, bk), lambda i, j, kk: (kk // per_blocks, j, kk % per_blocks))
        else:
            b_spec = pl.BlockSpec((bn, bk), lambda i, j, kk: (j, kk))
    else:
        a_spec = pl.BlockSpec((bk, bm), lambda i, j, kk: (kk, i))
        b_spec = pl.BlockSpec((bk, bn), lambda i, j, kk: (kk, j))
    if out_slots:
        o_spec = pl.BlockSpec((None, bm, bn), lambda i, j, kk: (j // per_blocks, i, j % per_blocks))
        out_shape = jax.ShapeDtypeStruct((N_CHIPS, m, per), out_dtype)
    else:
        o_spec = pl.BlockSpec((bm, bn), lambda i, j, kk: (i, j))
        out_shape = jax.ShapeDtypeStruct((m, n), out_dtype)
    if not pair:
        ins, in_specs = (a, b), [a_spec, b_spec]
    elif mode == "nt":
        ins = (*a, b)
        in_specs = [pl.BlockSpec((bm, bk), lambda i, j, kk: (i, jnp.minimum(kk, half - 1))),
                    pl.BlockSpec((bm, bk), lambda i, j, kk: (i, jnp.maximum(kk - half, 0))), b_spec]
    else:
        ins = (*b, a)
        in_specs = [pl.BlockSpec((bk, bn), lambda i, j, kk: (kk, jnp.minimum(j, half - 1))),
                    pl.BlockSpec((bk, bn), lambda i, j, kk: (kk, jnp.maximum(j - half, 0))), a_spec]
    scratch = [pltpu.VMEM((bm, bn), F32)] if nk > 1 else []
    if not rider:
        return _pcall(
            body, name=name, grid=(m // bm, n // bn, nk), in_specs=in_specs, out_specs=o_spec, out_shape=out_shape,
            scratch_shapes=scratch, compiler_params=_params("parallel", "parallel", "arbitrary"),
        )(*ins)
    assert not pair
    rows, cols = rider[0][0].shape
    nj = n // bn
    rb = rows // ((m // bm) * nj * nk)
    assert rows == rb * (m // bm) * nj * nk and rb % 8 == 0, (rows, rb)
    ride_spec = pl.BlockSpec((rb, cols), lambda i, j, kk: ((i * nj + j) * nk + kk, 0))
    outs = _pcall(
        body, name=name, grid=(m // bm, nj, nk), in_specs=in_specs + [ride_spec] * n_ride,
        out_specs=[o_spec] + [ride_spec] * n_rode,
        out_shape=[out_shape] + [jax.ShapeDtypeStruct((rows, cols), F32)] * n_rode,
        scratch_shapes=scratch,
        compiler_params=_params("arbitrary", "arbitrary", "arbitrary", vmem_limit_bytes=VMEM_RIDER_LIMIT_BYTES),
    )(*ins, *rider[0])
    return outs[0], outs[1:]


def _matmul_quarters(h, w_slots, order, first, count, name, prev=None, bm=1024, bn=1280):
    s, k = h.shape
    n = w_slots.shape[2]
    bm, bn = _pick(s, bm), _pick(n, bn)
    pb = n // bn

    def body(order_ref, h_ref, w_ref, *rest):
        rest[-1][...] = _dot(h_ref[...], w_ref[...])

    ins = (order, h, w_slots) + (() if prev is None else (prev,))
    return _pcall(
        body, name=name, num_scalar_prefetch=1, grid=(count * pb, s // bm),
        in_specs=[pl.BlockSpec((bm, k), lambda j, i, o: (i, 0)),
                  pl.BlockSpec((None, k, bn), lambda j, i, o: (o[first + j // pb], 0, j % pb))]
        + ([] if prev is None else [ANY_SPEC]),
        out_specs=pl.BlockSpec((bm, bn), lambda j, i, o: (i, o[first + j // pb] * pb + j % pb)),
        out_shape=jax.ShapeDtypeStruct((s, N_CHIPS * n), F32),
        input_output_aliases={} if prev is None else {3: 0},
        compiler_params=_params("arbitrary", "arbitrary"),
    )(*ins)


def _place():
    x, y, c = lax.axis_index("x"), lax.axis_index("y"), lax.axis_index("c")
    chips = [(1 - x, y), (x, 1 - y), (1 - x, 1 - y)]
    return x, y, c, chips


def _all_gather8(block, name):
    def body(x_ref, out_ref, send_sems, recv_sems, local_sem):
        x, y, c, chips = _place()
        me, sibling = (x, y, c), (x, y, 1 - c)

        def slot(px, py, pc):
            return out_ref.at[4 * px + 2 * py + pc]

        def copy(k, blk, to, src=None):
            return pltpu.make_async_remote_copy(
                src_ref=slot(*blk) if src is None else src, dst_ref=slot(*blk),
                send_sem=send_sems.at[k], recv_sem=recv_sems.at[k], device_id=to, device_id_type=MESH)

        mine = pltpu.make_async_copy(x_ref, slot(*me), local_sem)
        mine.start()
        first = [copy(0, me, sibling, src=x_ref)]
        first += [copy(1 + j, me, (*chip, c), src=x_ref) for j, chip in enumerate(chips)]
        for cp in first:
            cp.start()
        passed = [copy(4 + j, (*chip, c), sibling) for j, chip in enumerate(chips)]
        for j, chip in enumerate(chips):
            copy(1 + j, (*chip, c), me).wait_recv()
            passed[j].start()
        copy(0, sibling, me).wait_recv()
        for j, chip in enumerate(chips):
            copy(4 + j, (*chip, 1 - c), me).wait_recv()
        for cp in first + passed:
            cp.wait_send()
        mine.wait()

    return _pcall(
        body, name=name, out_shape=jax.ShapeDtypeStruct((N_DEV,) + block.shape, block.dtype),
        in_specs=[pl.BlockSpec(memory_space=pltpu.VMEM)], out_specs=pl.BlockSpec(memory_space=pltpu.VMEM),
        scratch_shapes=[pltpu.SemaphoreType.DMA((7,)), pltpu.SemaphoreType.DMA((7,)), pltpu.SemaphoreType.DMA],
        compiler_params=pltpu.CompilerParams(vmem_limit_bytes=VMEM_LIMIT_BYTES),
    )(block)


HBM_SPEC = pl.BlockSpec(memory_space=pltpu.HBM)
SEM_SPEC = pl.BlockSpec(memory_space=pltpu.SEMAPHORE)
ANY_SPEC = pl.BlockSpec(memory_space=pl.ANY)
EFFECT = pltpu.SideEffectType.DATAFLOW_SIDE_EFFECTING


def _xfer_start(name, bufs, plan, n_copies, after_last=False):
    nb = len(bufs)
    deps = (_Order.last,) if after_last and _Order.last is not None else ()
    nd = len(deps)

    def body(*refs):
        send_sems, recv_sems = refs[nb + nd], refs[nb + nd + 1]
        token = refs[nb + nd + 2 + nb]
        for k, (src, dst, dev) in enumerate(plan(refs[:nb], *_place())):
            pltpu.make_async_remote_copy(src_ref=src, dst_ref=dst, send_sem=send_sems.at[k], recv_sem=recv_sems.at[k],
                                         device_id=dev, device_id_type=MESH).start()
        token[...] = jnp.zeros_like(token)

    outs = pl.pallas_call(
        body, name=name,
        out_shape=(pltpu.SemaphoreType.DMA((n_copies,)), pltpu.SemaphoreType.DMA((n_copies,)),
                   *[pltpu.HBM(b.shape, b.dtype) for b in bufs], jax.ShapeDtypeStruct((8, LANES), F32)),
        in_specs=[HBM_SPEC] * nb + [ANY_SPEC] * nd,
        out_specs=(SEM_SPEC, SEM_SPEC, *[HBM_SPEC] * nb, pl.BlockSpec(memory_space=pltpu.VMEM)),
        input_output_aliases={i: 2 + i for i in range(nb)},
        compiler_params=pltpu.CompilerParams(has_side_effects=EFFECT),
    )(*[pltpu.with_memory_space_constraint(b, pltpu.HBM) for b in bufs], *deps)
    _Order.last = outs[-1]
    return (outs[0], outs[1]), list(outs[2:2 + nb])


def _xfer_wait(name, sems, bufs, plan):
    nb = len(bufs)

    def body(*refs):
        send_sems, recv_sems = refs[nb], refs[nb + 1]
        for k, (src, dst, dev) in enumerate(plan(refs[:nb], *_place())):
            copy = pltpu.make_async_remote_copy(src_ref=src, dst_ref=dst, send_sem=send_sems.at[k],
                                                recv_sem=recv_sems.at[k], device_id=dev, device_id_type=MESH)
            copy.wait_send()
            copy.wait_recv()

    outs = pl.pallas_call(
        body, name=name, out_shape=tuple(pltpu.HBM(b.shape, b.dtype) for b in bufs),
        in_specs=[HBM_SPEC] * nb + [SEM_SPEC, SEM_SPEC, ANY_SPEC], out_specs=tuple([HBM_SPEC] * nb),
        input_output_aliases={i: i for i in range(nb)},
        compiler_params=pltpu.CompilerParams(has_side_effects=EFFECT),
    )(*bufs, *sems, _Order.last)
    _Order.last = outs[0]
    return list(outs)


def _xfer_hand_over(name, sems, bufs, plan, next_plans):
    nb, n_next = len(bufs), len(next_plans)

    def body(*refs):
        send_sems, recv_sems = refs[nb], refs[nb + 1]
        outs = refs[nb + 3:]
        place = _place()
        for k, (src, dst, dev) in enumerate(plan(refs[:nb], *place)):
            copy = pltpu.make_async_remote_copy(src_ref=src, dst_ref=dst, send_sem=send_sems.at[k],
                                                recv_sem=recv_sems.at[k], device_id=dev, device_id_type=MESH)
            copy.wait_send()
            copy.wait_recv()
        for p, (next_plan, _) in enumerate(next_plans):
            for k, (src, dst, dev) in enumerate(next_plan(refs[:nb], *place)):
                pltpu.make_async_remote_copy(src_ref=src, dst_ref=dst, send_sem=outs[2 * p].at[k],
                                             recv_sem=outs[2 * p + 1].at[k], device_id=dev, device_id_type=MESH).start()
        outs[-1][...] = jnp.zeros_like(outs[-1])

    sem_shapes = [pltpu.SemaphoreType.DMA((copies,)) for _, copies in next_plans for _ in range(2)]
    outs = pl.pallas_call(
        body, name=name,
        out_shape=(*sem_shapes, *[pltpu.HBM(b.shape, b.dtype) for b in bufs], jax.ShapeDtypeStruct((8, LANES), F32)),
        in_specs=[HBM_SPEC] * nb + [SEM_SPEC, SEM_SPEC, ANY_SPEC],
        out_specs=(*[SEM_SPEC] * (2 * n_next), *[HBM_SPEC] * nb, pl.BlockSpec(memory_space=pltpu.VMEM)),
        input_output_aliases={i: 2 * n_next + i for i in range(nb)},
        compiler_params=pltpu.CompilerParams(has_side_effects=EFFECT),
    )(*bufs, *sems, _Order.last)
    _Order.last = outs[-1]
    return [(outs[2 * p], outs[2 * p + 1]) for p in range(n_next)], list(outs[2 * n_next:2 * n_next + nb])


def _half(ref, c, axis):
    rows = ref.shape[axis] // 2
    return pl.ds(c * rows, rows)


def _plan_weights_ici(n):
    def plan(refs, x, y, c, chips):
        out = []
        for w in range(n):
            region = refs[w].at[2 * x + y, _half(refs[w], c, 1), :]
            out += [(region, region, (*chip, c)) for chip in chips]
        return out
    return plan


def _plan_weights_ring(n):
    def plan(refs, x, y, c, chips):
        out = []
        for w in range(n):
            region = refs[w].at[2 * x + y, _half(refs[w], c, 1), :]
            out += [(region, region, (*chip, c)) for chip in chips[:2]]
        return out
    return plan


def _plan_weights_relay(n):
    def plan(refs, x, y, c, chips):
        out = []
        for w in range(n):
            quarter_rows = refs[w].shape[1] // 4
            upper = refs[w].at[2 * x + (1 - y), pl.ds(2 * c * quarter_rows, quarter_rows), :]
            lower = refs[w].at[2 * (1 - x) + y, pl.ds((2 * c + 1) * quarter_rows, quarter_rows), :]
            out += [(upper, upper, (1 - x, y, c)), (lower, lower, (x, 1 - y, c))]
        return out
    return plan


def _plan_weights_d2d(n, which=slice(0, 3)):
    def plan(refs, x, y, c, chips):
        out = []
        for w in range(n):
            rows = _half(refs[w], c, 1)
            for chip in chips[which]:
                region = refs[w].at[2 * chip[0] + chip[1], rows, :]
                out.append((region, region, (x, y, 1 - c)))
        return out
    return plan


def _plan_grads_d2d(n):
    def plan(refs, x, y, c, chips):
        return [(refs[w].at[:, _half(refs[w], 1 - c, 1), :], refs[n + w], (x, y, 1 - c)) for w in range(n)]
    return plan


def _plan_grads_ici(n):
    def plan(refs, x, y, c, chips):
        out = []
        for w in range(n):
            out += [(refs[w].at[2 * chip[0] + chip[1]], refs[n + w].at[2 * x + y], (*chip, c)) for chip in chips]
        return out
    return plan


def _plan_final_d2d(n):
    def plan(refs, x, y, c, chips):
        out = []
        for w in range(n):
            region = refs[w].at[_half(refs[w], c, 0), :]
            out.append((region, region, (x, y, 1 - c)))
        return out
    return plan


def _stream_blocks(hr, cols):
    bc = cols if cols <= 4096 else _pick(cols, 4096)
    return _pick_rows(hr, max(16, (768 * 1024) // bc)), bc


def _pre_reduce(g, landed, place, name):
    _, rows, cols = g.shape
    hr = rows // 2
    rb, bc = _stream_blocks(hr, cols)
    nrb = hr // rb

    def body(place_ref, g_ref, l_ref, o_ref):
        o_ref[...] = (g_ref[...].astype(F32) + l_ref[...].astype(F32)).astype(o_ref.dtype)

    return _pcall(
        body, name=name, num_scalar_prefetch=1, grid=(N_CHIPS, nrb, cols // bc),
        in_specs=[pl.BlockSpec((None, rb, bc), lambda j, i, k, p: (j, p[1] * nrb + i, k)),
                  pl.BlockSpec((None, rb, bc), lambda j, i, k, p: (j, i, k))],
        out_specs=pl.BlockSpec((None, rb, bc), lambda j, i, k, p: (j, i, k)),
        out_shape=jax.ShapeDtypeStruct((N_CHIPS, hr, cols), g.dtype),
        compiler_params=_params("parallel", "parallel", "parallel"),
    )(place, g, landed)


def _sum_slots(mine, landed, place, name):
    _, hr, cols = mine.shape
    rb, bc = _stream_blocks(hr, cols)
    rb = _pick_rows(hr, max(16, rb // 2))
    nrb = hr // rb

    def body(place_ref, m_ref, l_ref, o_ref):
        chip = place_ref[0]
        own = m_ref[...].astype(F32)
        total = jnp.where(chip == 0, own, l_ref[0].astype(F32))
        for j in range(1, N_CHIPS):
            total = total + jnp.where(chip == j, own, l_ref[j].astype(F32))
        o_ref[...] = total

    return _pcall(
        body, name=name, num_scalar_prefetch=1, grid=(nrb, cols // bc),
        in_specs=[pl.BlockSpec((None, rb, bc), lambda i, k, p: (p[0], i, k)),
                  pl.BlockSpec((N_CHIPS, rb, bc), lambda i, k, p: (0, i, k))],
        out_specs=pl.BlockSpec((rb, bc), lambda i, k, p: (p[1] * nrb + i, k)),
        out_shape=jax.ShapeDtypeStruct((2 * hr, cols), F32),
        compiler_params=_params("parallel", "parallel"),
    )(place, mine, landed)


class _WeightGather:
    @staticmethod
    def zones(quarters, chip):
        return [lax.dynamic_update_slice(lax.empty((N_CHIPS,) + q.shape, BF16), q.astype(BF16)[None], (chip, 0, 0))
                for q in quarters]

    def __init__(self, tag, zones, ring=False):
        self.tag, self.n, self.ring = tag, len(zones), ring
        self.plan = _plan_weights_ring(self.n) if ring else _plan_weights_ici(self.n)
        self.sems, self.bufs = _xfer_start("wici_start_" + tag, zones, self.plan, (2 if ring else 3) * self.n,
                                           after_last=True)

    def relay(self):
        plan = _plan_weights_relay(self.n)
        (self.sems,), self.bufs = _xfer_hand_over("wrel_start_" + self.tag, self.sems, self.bufs, self.plan,
                                                   [(plan, 2 * self.n)])
        self.plan = plan

    def pass_on(self):
        plan = _plan_weights_d2d(self.n)
        (self.sems,), self.bufs = _xfer_hand_over("wd2d_start_" + self.tag, self.sems, self.bufs, self.plan,
                                                   [(plan, 3 * self.n)])
        self.plan = plan

    def done(self):
        return _xfer_wait("wd2d_wait_" + self.tag, self.sems, self.bufs, self.plan)

    def relay_and_pass_near(self):
        plan, self.near_plan = _plan_weights_relay(self.n), _plan_weights_d2d(self.n, slice(0, 2))
        (self.sems, self.near_sems), self.bufs = _xfer_hand_over(
            "wrel_start_" + self.tag, self.sems, self.bufs, self.plan, [(plan, 2 * self.n), (self.near_plan, 2 * self.n)])
        self.plan = plan

    def near_done(self):
        self.bufs = _xfer_wait("wnear_wait_" + self.tag, self.near_sems, self.bufs, self.near_plan)
        return self.bufs

    def pass_far(self):
        plan = _plan_weights_d2d(self.n, slice(2, 3))
        (sems,), bufs = _xfer_hand_over("wfar_start_" + self.tag, self.sems, self.bufs, self.plan, [(plan, self.n)])
        return _xfer_wait("wfar_wait_" + self.tag, sems, bufs, plan)


class _GradReduce:
    def __init__(self, tag, grads):
        self.tag, self.n = tag, len(grads)
        zones = [lax.empty((N_CHIPS, g.shape[1] // 2, g.shape[2]), g.dtype) for g in grads]
        self.plan = _plan_grads_d2d(self.n)
        self.sems, self.bufs = _xfer_start("gd2d_start_" + tag, list(grads) + zones, self.plan, self.n)

    def pair(self, place):
        n = self.n
        bufs = _xfer_wait("gd2d_wait_" + self.tag, self.sems, self.bufs, self.plan)
        self.halves = [_pre_reduce(bufs[w], bufs[n + w], place, f"pre_reduce_{self.tag}{w}") for w in range(n)]

    def cross(self, after_last=False):
        zones = [lax.empty(h.shape, h.dtype) for h in self.halves]
        self.plan = _plan_grads_ici(self.n)
        self.sems, self.bufs = _xfer_start("gici_start_" + self.tag, self.halves + zones, self.plan, 3 * self.n,
                                           after_last=after_last)

    def step(self, place):
        self.pair(place)
        self.cross()

    def join(self, place):
        n = self.n
        bufs = _xfer_wait("gici_wait_" + self.tag, self.sems, self.bufs, self.plan)
        sums = [_sum_slots(bufs[w], bufs[n + w], place, f"sum_slots_{self.tag}{w}") for w in range(n)]
        self.plan = _plan_final_d2d(n)
        self.sems, self.bufs = _xfer_start("gfin_start_" + self.tag, sums, self.plan, n)

    def done(self):
        return _xfer_wait("gfin_wait_" + self.tag, self.sems, self.bufs, self.plan)


def _ada_fwd(c_all, w_q, b_q):
    d, n = w_q.shape
    bn = _pick(n, 512)

    def body(c_ref, w_ref, b_ref, o_ref):
        cv = c_ref[...]
        act = cv * _sigmoid(cv)
        o_ref[...] = _dot(act, w_ref[...], NN, lax.Precision.HIGHEST) + b_ref[...]

    return _pcall(
        body, name="ada_fwd", grid=(n // bn,),
        in_specs=[pl.BlockSpec((N_DEV, d), lambda j: (0, 0)), pl.BlockSpec((d, bn), lambda j: (0, j)),
                  pl.BlockSpec((1, bn), lambda j: (0, j))],
        out_specs=pl.BlockSpec((N_DEV, bn), lambda j: (0, j)),
        out_shape=jax.ShapeDtypeStruct((N_DEV, n), F32), compiler_params=_params("parallel"),
    )(c_all, w_q, b_q)


def _row_spec(rb, width, col=0):
    return pl.BlockSpec((rb, width), lambda i, col=col: (i, col))


def _vec_spec(width, col=0):
    return pl.BlockSpec((1, width), lambda i, col=col: (0, col))


def _norm_mod_fwd(x, g, sc, sh, name, res=None, gt=None):
    s, d = x.shape
    rb = _pick(s, 256)
    has_res = res is not None

    def body(*refs):
        if has_res:
            x_ref, res_ref, gt_ref, g_ref, sc_ref, sh_ref, x1_ref, h_ref = refs
            xv = x_ref[...] + gt_ref[...] * res_ref[...]
            x1_ref[...] = xv
        else:
            x_ref, g_ref, sc_ref, sh_ref, h_ref = refs
            xv = x_ref[...]
        xh, _ = _rms(xv)
        h_ref[...] = (xh * g_ref[...] * (1.0 + sc_ref[...]) + sh_ref[...]).astype(BF16)

    row, vec = _row_spec(rb, d), _vec_spec(d)
    if has_res:
        ins, in_specs = (x, res, gt, g, sc, sh), [row, row, vec, vec, vec, vec]
        out_shape = [jax.ShapeDtypeStruct((s, d), F32), jax.ShapeDtypeStruct((s, d), BF16)]
        out_specs = [row, row]
    else:
        ins, in_specs = (x, g, sc, sh), [row, vec, vec, vec]
        out_shape, out_specs = jax.ShapeDtypeStruct((s, d), BF16), row
    return _pcall(body, name=name, grid=(s // rb,), in_specs=in_specs, out_specs=out_specs,
                          out_shape=out_shape, compiler_params=_params("parallel"))(*ins)


def _final_loss(x1, f, gt2, g_final, target):
    s, d = x1.shape
    rb = _pick(s, 256)

    def body(x1_ref, f_ref, gt_ref, g_ref, t_ref, dx_ref, df_ref, loss_ref, dg_ref, dgt_ref):
        first = pl.program_id(0) == 0
        fv, gt, gv = f_ref[...], gt_ref[...], g_ref[...]
        x2 = x1_ref[...] + gt * fv
        xh, r = _rms(x2)
        err = xh * gv - t_ref[...]
        blk = 0.5 * jnp.sum(jnp.sum(err * err, axis=1, keepdims=True), axis=0, keepdims=True) / d
        dy = err / d
        dxh = dy * gv
        dx = r * (dxh - xh * jnp.mean(dxh * xh, axis=-1, keepdims=True))
        dx_ref[...] = dx
        df_ref[...] = (dx * gt).astype(BF16)
        _accumulate(first, loss_ref, jnp.broadcast_to(blk, (1, LANES)))
        _accumulate(first, dg_ref, _colsum(dy * xh))
        _accumulate(first, dgt_ref, _colsum(dx * fv))

    row, vec = _row_spec(rb, d), _vec_spec(d)
    return _pcall(
        body, name="final_loss", grid=(s // rb,), in_specs=[row, row, vec, vec, row],
        out_specs=[row, row, _vec_spec(LANES), vec, vec],
        out_shape=[jax.ShapeDtypeStruct((s, d), F32), jax.ShapeDtypeStruct((s, d), BF16),
                   jax.ShapeDtypeStruct((1, LANES), F32), jax.ShapeDtypeStruct((1, d), F32),
                   jax.ShapeDtypeStruct((1, d), F32)],
        compiler_params=_params("arbitrary"),
    )(x1, f, gt2, g_final, target)


def _norm_mod_bwd(dh, xin, dres, g, sc, name, branch=None, gt=None):
    s, d = xin.shape
    rb = _pick(s, 256)
    has_branch = branch is not None

    def body(*refs):
        if has_branch:
            dh_ref, x_ref, dres_ref, g_ref, sc_ref, br_ref, gt_ref, dx_ref, dbr_ref, dsh_ref, dsc_ref, dg_ref, dgt_ref = refs
        else:
            dh_ref, x_ref, dres_ref, g_ref, sc_ref, dx_ref, dsh_ref, dsc_ref, dg_ref = refs
        first = pl.program_id(0) == 0
        gv = g_ref[...]
        xh, r = _rms(x_ref[...])
        dhv = dh_ref[...]
        dn = dhv * (1.0 + sc_ref[...])
        dxh = dn * gv
        dx = dres_ref[...] + r * (dxh - xh * jnp.mean(dxh * xh, axis=-1, keepdims=True))
        dx_ref[...] = dx
        _accumulate(first, dsh_ref, _colsum(dhv))
        _accumulate(first, dsc_ref, _colsum(dhv * xh * gv))
        _accumulate(first, dg_ref, _colsum(dn * xh))
        if has_branch:
            dbr_ref[...] = (dx * gt_ref[...]).astype(BF16)
            _accumulate(first, dgt_ref, _colsum(dx * br_ref[...]))

    row, vec = _row_spec(rb, d), _vec_spec(d)
    vec_shape = jax.ShapeDtypeStruct((1, d), F32)
    if has_branch:
        ins, in_specs = (dh, xin, dres, g, sc, branch, gt), [row, row, row, vec, vec, row, vec]
        out_specs = [row, row, vec, vec, vec, vec]
        out_shape = [jax.ShapeDtypeStruct((s, d), F32), jax.ShapeDtypeStruct((s, d), BF16)] + [vec_shape] * 4
    else:
        ins, in_specs = (dh, xin, dres, g, sc), [row, row, row, vec, vec]
        out_specs = [row, vec, vec, vec]
        out_shape = [jax.ShapeDtypeStruct((s, d), F32)] + [vec_shape] * 3
    return _pcall(body, name=name, grid=(s // rb,), in_specs=in_specs, out_specs=out_specs,
                          out_shape=out_shape, compiler_params=_params("arbitrary"))(*ins)


def _gate_fwd(pa, pb, z, b_gate, gate_col):
    s, d = pa.shape
    rb = _pick(s, 256)

    def body(pa_ref, pb_ref, za_ref, zb_ref, ba_ref, bb_ref, y_ref):
        ga = _sigmoid(za_ref[...] + ba_ref[...])
        gb = _sigmoid(zb_ref[...] + bb_ref[...])
        y_ref[...] = (ga * pa_ref[...].astype(F32) + gb * pb_ref[...].astype(F32)).astype(BF16)

    row = _row_spec(rb, d)
    return _pcall(
        body, name="gate_fwd", grid=(s // rb,),
        in_specs=[row, row, _row_spec(rb, d, gate_col), _row_spec(rb, d, gate_col + 1), _vec_spec(d, 0), _vec_spec(d, 1)],
        out_specs=row, out_shape=jax.ShapeDtypeStruct((s, d), BF16), compiler_params=_params("parallel"),
    )(pa, pb, z, z, b_gate, b_gate)


def _out_bwd(dyo, w_out, pa, pb, z, b_gate, gate_col, bm=512, bn=1024):
    s, d = dyo.shape
    bm, bn = _pick(s, bm), _pick(d, bn)
    nj = d // bn

    def body(dyo_ref, w_ref, pa_ref, pb_ref, za_ref, zb_ref, ba_ref, bb_ref,
             dpa_ref, dpb_ref, dza_ref, dzb_ref, dba_ref, dbb_ref):
        first = pl.program_id(1) == 0
        dyv = _dot(dyo_ref[...], w_ref[...], NT)
        ga = _sigmoid(za_ref[...] + ba_ref[...])
        gb = _sigmoid(zb_ref[...] + bb_ref[...])
        dpa_ref[...] = (dyv * ga).astype(BF16)
        dpb_ref[...] = (dyv * gb).astype(BF16)
        dga = dyv * pa_ref[...].astype(F32) * ga * (1.0 - ga)
        dgb = dyv * pb_ref[...].astype(F32) * gb * (1.0 - gb)
        dza_ref[...] = dga.astype(BF16)
        dzb_ref[...] = dgb.astype(BF16)
        _accumulate(first, dba_ref, _colsum(dga))
        _accumulate(first, dbb_ref, _colsum(dgb))

    tile = pl.BlockSpec((bm, bn), lambda j, i: (i, j))
    vec = pl.BlockSpec((1, bn), lambda j, i: (0, j))
    act, vec_shape = jax.ShapeDtypeStruct((s, d), BF16), jax.ShapeDtypeStruct((1, d), F32)
    return _pcall(
        body, name="out_bwd", grid=(nj, s // bm),
        in_specs=[pl.BlockSpec((bm, d), lambda j, i: (i, 0)), pl.BlockSpec((bn, d), lambda j, i: (j, 0)), tile, tile,
                  pl.BlockSpec((bm, bn), lambda j, i: (i, gate_col * nj + j)),
                  pl.BlockSpec((bm, bn), lambda j, i: (i, (gate_col + 1) * nj + j)),
                  vec, pl.BlockSpec((1, bn), lambda j, i: (0, nj + j))],
        out_specs=[tile, tile, tile, tile, vec, vec], out_shape=[act, act, act, act, vec_shape, vec_shape],
        compiler_params=_params("parallel", "arbitrary"),
    )(dyo, w_out, pa, pb, z, z, b_gate, b_gate)


def _ffn_in(h, w_fi, bm=512, bn=1408):
    s, d = h.shape
    per = w_fi.shape[2]
    ff = 2 * per
    bm, bn = _pick(s, bm), _pick(per, bn)
    per_blocks = per // bn

    def body(h_ref, wa_ref, wu_ref, a_ref, u_ref, hf_ref):
        hv = h_ref[...]
        a = _dot(hv, wa_ref[...])
        up = _dot(hv, wu_ref[...])
        a_ref[...] = a.astype(BF16)
        u_ref[...] = up.astype(BF16)
        hf_ref[...] = (a * _sigmoid(a) * up).astype(BF16)

    out = pl.BlockSpec((bm, bn), lambda j, i: (i, j))
    shape = jax.ShapeDtypeStruct((s, ff), BF16)
    return _pcall(
        body, name="ffn_in", grid=(ff // bn, s // bm),
        in_specs=[pl.BlockSpec((bm, d), lambda j, i: (i, 0)),
                  pl.BlockSpec((None, d, bn), lambda j, i: (j // per_blocks, 0, j % per_blocks)),
                  pl.BlockSpec((None, d, bn), lambda j, i: (2 + j // per_blocks, 0, j % per_blocks))],
        out_specs=[out, out, out], out_shape=[shape, shape, shape],
        compiler_params=_params("parallel", "parallel"),
    )(h, w_fi, w_fi)


def _ffn_out_bwd(dffn, w_fo, a_act, up_act, bm=512, bn=1408):
    s, d = dffn.shape
    ff = w_fo.shape[0]
    bm, bn = _pick(s, bm), _pick(ff, bn)

    def body(d_ref, w_ref, a_ref, u_ref, da_ref, du_ref):
        dhf = _dot(d_ref[...], w_ref[...], NT)
        a = a_ref[...].astype(F32)
        sa = _sigmoid(a)
        da_ref[...] = (dhf * u_ref[...].astype(F32) * sa * (1.0 + a * (1.0 - sa))).astype(BF16)
        du_ref[...] = (dhf * a * sa).astype(BF16)

    tile = pl.BlockSpec((bm, bn), lambda j, i: (i, j))
    shape = jax.ShapeDtypeStruct((s, ff), BF16)
    return _pcall(
        body, name="ffn_out_bwd", grid=(ff // bn, s // bm),
        in_specs=[pl.BlockSpec((bm, d), lambda j, i: (i, 0)), pl.BlockSpec((bn, d), lambda j, i: (j, 0)), tile, tile],
        out_specs=[tile, tile], out_shape=[shape, shape], compiler_params=_params("parallel", "parallel"),
    )(dffn, w_fo, a_act, up_act)


def _tril(n):
    return lax.broadcasted_iota(jnp.int32, (n, n), 0) >= lax.broadcasted_iota(jnp.int32, (n, n), 1)


def _gmlp_norm(v, ln_g, ln_b):
    gv = _gelu(v)
    mu = jnp.mean(gv, axis=-1, keepdims=True)
    cen = gv - mu
    rs = lax.rsqrt(jnp.mean(cen * cen, axis=-1, keepdims=True) + EPS)
    xh = cen * rs
    return xh, rs, xh * ln_g + ln_b


def _gmlp_fwd(z, ln_g, ln_b, ws, bs_t):
    s = z.shape[0]
    gw = ln_g.shape[1]
    groups, chunk, _ = ws.shape

    def body(u_ref, v_ref, lg_ref, lb_ref, ws_ref, bs_ref, ya_ref):
        gu = _gelu(u_ref[...])
        _, _, vn = _gmlp_norm(v_ref[...], lg_ref[...], lb_ref[...])
        mask = _tril(chunk)
        for g in range(groups):
            cols = slice(g * LANES, (g + 1) * LANES)
            wm = jnp.where(mask, ws_ref[g], 0.0).astype(BF16)
            sg = _dot(wm, vn[:, cols].astype(BF16)) + bs_ref[:, g:g + 1]
            ya_ref[:, cols] = (gu[:, cols] * sg).astype(BF16)

    return _pcall(
        body, name="gmlp_fwd", grid=(s // chunk,),
        in_specs=[_row_spec(chunk, gw, 0), _row_spec(chunk, gw, 1), _vec_spec(gw), _vec_spec(gw),
                  pl.BlockSpec((groups, chunk, chunk), lambda i: (0, 0, 0)), pl.BlockSpec((chunk, LANES), lambda i: (0, 0))],
        out_specs=_row_spec(chunk, gw), out_shape=jax.ShapeDtypeStruct((s, gw), BF16),
        compiler_params=_params("parallel"),
    )(z, z, ln_g, ln_b, ws, bs_t)


def _gmlp_bwd(dya, z, ln_g, ln_b, ws, bs_t):
    s = z.shape[0]
    gw = ln_g.shape[1]
    groups, chunk, _ = ws.shape

    def body(dya_ref, u_ref, v_ref, lg_ref, lb_ref, ws_ref, bs_ref, duv_ref, dws_ref, dbs_ref, dlg_ref, dlb_ref, dvn_ref):
        first = pl.program_id(0) == 0
        u, v, lg = u_ref[...], v_ref[...], lg_ref[...]
        gu = _gelu(u)
        xh, rs, vn = _gmlp_norm(v, lg, lb_ref[...])
        dyav = dya_ref[...]
        mask = _tril(chunk)
        lane = lax.broadcasted_iota(jnp.int32, (chunk, LANES), 1)
        dbs = jnp.zeros((chunk, LANES), F32)
        for g in range(groups):
            cols = slice(g * LANES, (g + 1) * LANES)
            wm = jnp.where(mask, ws_ref[g], 0.0).astype(BF16)
            vg = vn[:, cols].astype(BF16)
            sg = _dot(wm, vg) + bs_ref[:, g:g + 1]
            ds = dyav[:, cols] * gu[:, cols]
            duv_ref[:, cols] = (dyav[:, cols] * sg * _gelu_grad(u[:, cols])).astype(BF16)
            dsb = ds.astype(BF16)
            _accumulate(first, dws_ref.at[g], jnp.where(mask, _dot(dsb, vg, NT), 0.0))
            dbs = dbs + jnp.where(lane == g, jnp.sum(ds, axis=-1, keepdims=True), 0.0)
            dvn_ref[:, cols] = _dot(wm, dsb, TN)
        dvn = dvn_ref[...]
        _accumulate(first, dbs_ref, dbs)
        _accumulate(first, dlb_ref, _colsum(dvn))
        _accumulate(first, dlg_ref, _colsum(dvn * xh))
        dxh = dvn * lg
        dgv = rs * (dxh - jnp.mean(dxh, axis=-1, keepdims=True) - xh * jnp.mean(dxh * xh, axis=-1, keepdims=True))
        duv_ref[:, gw:] = (dgv * _gelu_grad(v)).astype(BF16)

    return _pcall(
        body, name="gmlp_bwd", grid=(s // chunk,),
        in_specs=[_row_spec(chunk, gw), _row_spec(chunk, gw, 0), _row_spec(chunk, gw, 1), _vec_spec(gw), _vec_spec(gw),
                  pl.BlockSpec((groups, chunk, chunk), lambda i: (0, 0, 0)), pl.BlockSpec((chunk, LANES), lambda i: (0, 0))],
        out_specs=[_row_spec(chunk, 2 * gw), pl.BlockSpec((groups, chunk, chunk), lambda i: (0, 0, 0)),
                   pl.BlockSpec((chunk, LANES), lambda i: (0, 0)), _vec_spec(gw), _vec_spec(gw)],
        out_shape=[jax.ShapeDtypeStruct((s, 2 * gw), BF16), jax.ShapeDtypeStruct((groups, chunk, chunk), F32),
                   jax.ShapeDtypeStruct((chunk, LANES), F32), jax.ShapeDtypeStruct((1, gw), F32),
                   jax.ShapeDtypeStruct((1, gw), F32)],
        scratch_shapes=[pltpu.VMEM((chunk, gw), F32)],
        compiler_params=_params("arbitrary"),
    )(dya, z, z, ln_g, ln_b, ws, bs_t)


def _lower_bound(lb_ref):
    a0, a1 = lb_ref[0:1, :], lb_ref[1:2, :]
    mx = jnp.maximum(a0, a1)
    e0, e1 = jnp.exp(a0 - mx), jnp.exp(a1 - mx)
    return e0 / (e0 + e1)


def _sum_dot(mask, x):
    hi = x.astype(BF16)
    rest = x - hi.astype(F32)
    mid = rest.astype(BF16)
    low = (rest - mid.astype(F32)).astype(BF16)
    return _dot(mask, hi) + _dot(mask, mid) + _dot(mask, low)


def _ones_where(mask):
    return jnp.where(mask, 1.0, 0.0).astype(BF16)


def _hg_masks(rows, t):
    r = lax.broadcasted_iota(jnp.int32, (rows, rows), 0)
    c = lax.broadcasted_iota(jnp.int32, (rows, rows), 1)
    same = (r // t) == (c // t)
    incl = jnp.logical_and(same, c <= r)
    upto_mid = jnp.logical_and(same, (c % t) <= t // 2)
    rev = jnp.logical_and(same, c >= r)
    return same, incl, upto_mid, rev


def _hg_block(q, fp, lb, masks):
    rows = q.shape[0]
    same, incl, upto_mid, _ = masks
    sig = _sigmoid(fp)
    f = lb + (1.0 - lb) * sig
    k = 1.0 - f
    sq = _sigmoid(q)
    qa = q * sq
    stacked = jnp.concatenate([_ones_where(m) for m in (incl, same, upto_mid)], axis=0)
    sums = _sum_dot(stacked, jnp.log(f))
    b, b_last, b_mid = sums[:rows], sums[rows:2 * rows], sums[2 * rows:]
    e_q = jnp.exp(jnp.minimum(b - b_mid, EXP_CLAMP))
    e_k = jnp.exp(jnp.minimum(b_mid - b, EXP_CLAMP))
    e_in = jnp.exp(b)
    e_out = jnp.exp(b_last - b)
    return dict(sig=sig, f=f, k=k, sq=sq, qa=qa, e_last=jnp.exp(b_last), e_q=e_q, e_k=e_k, e_in=e_in, e_out=e_out,
                q_hat=(qa * e_q).astype(BF16), k_hat=(k * e_k).astype(BF16),
                q_in=(qa * e_in).astype(BF16), k_out=k * e_out)


def _hgrn_fwd(z, hg_lb, norm_g, q_col):
    s = z.shape[0]
    hw = norm_g.shape[1]
    heads = hw // LANES
    t = HG_CHUNK
    rows = min(HG_ROWS, s)
    per_step = rows // t
    hp = min(HG_HEADS_PER_STEP, heads)
    assert heads % hp == 0 and q_col % hp == 0, (heads, q_col)
    wide = hp * LANES

    def zspec(which):
        return pl.BlockSpec((rows, wide), lambda h, r, which=which: (r, (q_col + which * heads) // hp + h))

    def body(q_ref, f_ref, i_ref, g_ref, lb_ref, ng_ref, yb_ref, o_ref, st_out_ref, st_ref, e_last_ref, inter_ref):
        @pl.when(pl.program_id(1) == 0)
        def _():
            st_ref[...] = jnp.zeros_like(st_ref)

        masks = _hg_masks(rows, t)
        for hh in range(hp):
            cols = slice(hh * LANES, (hh + 1) * LANES)
            blk = _hg_block(q_ref[:, cols], f_ref[:, cols], _lower_bound(lb_ref.at[:, cols]), masks)
            iv = i_ref[:, cols].astype(BF16)
            q_in, k_out = blk["q_in"], blk["k_out"].astype(BF16)
            e_last_ref[hh] = blk["e_last"]
            attn = jnp.where(masks[1], _dot(blk["q_hat"], blk["k_hat"], NT), 0.0).astype(BF16)
            o = _dot(attn, iv)
            grown = [_dot(iv[j * t:(j + 1) * t], k_out[j * t:(j + 1) * t], TN) for j in range(per_step)]
            st = st_ref[hh]
            for j in range(per_step):
                st_out_ref[hh, j] = st
                inter_ref[hh, j * t:(j + 1) * t, :] = _dot(q_in[j * t:(j + 1) * t], st.astype(BF16), NT)
                st = st * e_last_ref[hh, j * t:j * t + 1, :] + grown[j]
            st_ref[hh] = st
            o = o + inter_ref[hh]
            o_ref[:, cols] = o
            og = g_ref[:, cols]
            on, _ = _rms(o)
            yb_ref[:, cols] = (on * ng_ref[:, cols] * (og * _sigmoid(og))).astype(BF16)

    out_row = pl.BlockSpec((rows, wide), lambda h, r: (r, h))
    return _pcall(
        body, name="hgrn_fwd", grid=(heads // hp, s // rows),
        in_specs=[zspec(0), zspec(1), zspec(2), zspec(3),
                  pl.BlockSpec((2, wide), lambda h, r: (0, h)), pl.BlockSpec((1, wide), lambda h, r: (0, h))],
        out_specs=[out_row, out_row, pl.BlockSpec((hp, per_step, LANES, LANES), lambda h, r: (h, r, 0, 0))],
        out_shape=[jax.ShapeDtypeStruct((s, hw), BF16), jax.ShapeDtypeStruct((s, hw), F32),
                   jax.ShapeDtypeStruct((heads, s // t, LANES, LANES), F32)],
        scratch_shapes=[pltpu.VMEM((hp, LANES, LANES), F32), pltpu.VMEM((hp, rows, LANES), F32),
                        pltpu.VMEM((hp, rows, LANES), F32)],
        compiler_params=_params("parallel", "arbitrary"),
    )(z, z, z, z, hg_lb, norm_g)


def _hgrn_bwd(dyb, z, o_raw, states, hg_lb, norm_g, q_col):
    s = z.shape[0]
    hw = norm_g.shape[1]
    heads = hw // LANES
    t = HG_CHUNK
    rows = min(HG_ROWS, s)
    per_step = rows // t
    n_steps = s // rows
    hp = min(HG_HEADS_PER_STEP, heads)
    assert heads % hp == 0 and q_col % hp == 0, (heads, q_col)
    wide = hp * LANES

    def zspec(which):
        return pl.BlockSpec((rows, wide), lambda h, r, which=which: (n_steps - 1 - r, (q_col + which * heads) // hp + h))

    def body(dyb_ref, q_ref, f_ref, i_ref, g_ref, o_ref, st_in_ref, lb_ref, ng_ref,
             dq_ref, df_ref, di_ref, dg_ref, dlb_ref, dng_ref, dst_ref, acc_lb_ref, acc_ng_ref,
             e_last_ref, dq_in_ref, dk_out_ref, di_inter_ref, carry_ref):
        step = pl.program_id(1)

        @pl.when(step == 0)
        def _():
            dst_ref[...] = jnp.zeros_like(dst_ref)
            acc_lb_ref[...] = jnp.zeros_like(acc_lb_ref)
            acc_ng_ref[...] = jnp.zeros_like(acc_ng_ref)

        masks = _hg_masks(rows, t)
        same, incl, _, rev = masks
        sum_mask = jnp.concatenate([_ones_where(rev), _ones_where(same)], axis=1)
        for hh in range(hp):
            cols = slice(hh * LANES, (hh + 1) * LANES)
            lb = _lower_bound(lb_ref.at[:, cols])
            ng = ng_ref[:, cols]
            q = q_ref[:, cols]
            blk = _hg_block(q, f_ref[:, cols], lb, masks)
            iv = i_ref[:, cols].astype(BF16)
            o, og, dy = o_ref[:, cols], g_ref[:, cols], dyb_ref[:, cols]
            so = _sigmoid(og)
            on, r = _rms(o)
            acc_ng_ref[:, cols] += _colsum(dy * on * (og * so))
            dg_ref[:, cols] = (dy * on * ng * so * (1.0 + og * (1.0 - so))).astype(BF16)
            don = dy * ng * (og * so)
            do = (r * (don - on * jnp.mean(don * on, axis=-1, keepdims=True))).astype(BF16)
            q_hat, k_hat, q_in, k_out = blk["q_hat"], blk["k_hat"], blk["q_in"], blk["k_out"]
            k_out_b = k_out.astype(BF16)
            attn = jnp.where(incl, _dot(q_hat, k_hat, NT), 0.0).astype(BF16)
            d_attn = jnp.where(incl, _dot(do, iv, NT), 0.0).astype(BF16)
            di_intra = _dot(attn, do, TN)
            dq_hat = _dot(d_attn, k_hat)
            dk_hat = _dot(d_attn, q_hat, TN)
            e_last_ref[hh] = blk["e_last"]
            grown = [_dot(do[j * t:(j + 1) * t], q_in[j * t:(j + 1) * t], TN) for j in range(per_step)]
            dst = dst_ref[hh]
            for j in reversed(range(per_step)):
                rs_ = slice(j * t, (j + 1) * t)
                e_last = e_last_ref[hh, j * t:j * t + 1, :]
                st_prev, dst_b = st_in_ref[hh, j], dst.astype(BF16)
                dq_in_ref[hh, rs_, :] = _dot(do[rs_], st_prev.astype(BF16))
                dk_out_ref[hh, rs_, :] = _dot(iv[rs_], dst_b)
                di_inter_ref[hh, rs_, :] = _dot(k_out_b[rs_], dst_b, NT)
                carry_ref[hh, rs_, :] = jnp.broadcast_to(e_last * _colsum(st_prev * dst), (t, LANES))
                dst = dst * e_last + grown[j]
            dst_ref[hh] = dst
            di_ref[:, cols] = (di_intra + di_inter_ref[hh]).astype(BF16)
            dk_out = dk_out_ref[hh]
            dqa = dq_in_ref[hh] * blk["e_in"] + dq_hat * blk["e_q"]
            dk = dk_out * blk["e_out"] + dk_hat * blk["e_k"]
            db = blk["qa"] * dqa - blk["k"] * dk
            dlf = _sum_dot(sum_mask, jnp.concatenate([db, dk_out * k_out], axis=0)) + carry_ref[hh]
            dfv = dlf / blk["f"] - dk
            sig, sq = blk["sig"], blk["sq"]
            df_ref[:, cols] = (dfv * (1.0 - lb) * sig * (1.0 - sig)).astype(BF16)
            acc_lb_ref[:, cols] += _colsum(dfv * (1.0 - sig))
            dq_ref[:, cols] = (dqa * sq * (1.0 + q * (1.0 - sq))).astype(BF16)

        @pl.when(step == n_steps - 1)
        def _():
            lb = _lower_bound(lb_ref)
            d0 = acc_lb_ref[...] * lb * (1.0 - lb)
            dlb_ref[0:1, :] = d0
            dlb_ref[1:2, :] = -d0
            dng_ref[...] = acc_ng_ref[...]

    rev_row = pl.BlockSpec((rows, wide), lambda h, r: (n_steps - 1 - r, h))
    piece = jax.ShapeDtypeStruct((s, hw), BF16)
    return _pcall(
        body, name="hgrn_bwd", grid=(heads // hp, n_steps),
        in_specs=[rev_row, zspec(0), zspec(1), zspec(2), zspec(3), rev_row,
                  pl.BlockSpec((hp, per_step, LANES, LANES), lambda h, r: (h, n_steps - 1 - r, 0, 0)),
                  pl.BlockSpec((2, wide), lambda h, r: (0, h)), pl.BlockSpec((1, wide), lambda h, r: (0, h))],
        out_specs=[rev_row, rev_row, rev_row, rev_row,
                   pl.BlockSpec((2, wide), lambda h, r: (0, h)), pl.BlockSpec((1, wide), lambda h, r: (0, h))],
        out_shape=[piece, piece, piece, piece, jax.ShapeDtypeStruct((2, hw), F32), jax.ShapeDtypeStruct((1, hw), F32)],
        scratch_shapes=[pltpu.VMEM((hp, LANES, LANES), F32), pltpu.VMEM((1, wide), F32), pltpu.VMEM((1, wide), F32)]
        + [pltpu.VMEM((hp, rows, LANES), F32)] * 5,
        compiler_params=_params("parallel", "arbitrary"),
    )(dyb, z, z, z, z, o_raw, states, hg_lb, norm_g)


def _adam_update(w, m, v, g):
    m2 = ADAM_B1 * m + (1.0 - ADAM_B1) * g
    v2 = ADAM_B2 * v + (1.0 - ADAM_B2) * (g * g)
    m_hat = m2 * (1.0 / (1.0 - ADAM_B1 ** ADAM_STEP))
    v_hat = v2 * (1.0 / (1.0 - ADAM_B2 ** ADAM_STEP))
    return -ADAM_LR * (m_hat / (jnp.sqrt(v_hat) + ADAM_EPS) + ADAM_WD * w), m2, v2


def _adamw(w, m, v, parts, name, outer=False):
    rows, cols = w.shape
    bc = cols if cols <= 4096 else _pick(cols, 4096)
    rb = _pick_rows(rows, max(8, (384 * 1024) // bc), mult=8)
    if outer and rb % LANES:
        rb = rows

    def body(w_ref, m_ref, v_ref, *refs):
        g_ref, d_ref, mo_ref, vo_ref = refs[-4:]
        if outer:
            cv = refs[0][...]
            g = _dot(cv * _sigmoid(cv), refs[1][...], TN, lax.Precision.HIGHEST)
        else:
            p_ref = refs[0]
            g = p_ref[0].astype(F32)
            for p in range(1, p_ref.shape[0]):
                g = g + p_ref[p].astype(F32)
        g_ref[...] = g
        d_ref[...], mo_ref[...], vo_ref[...] = _adam_update(w_ref[...], m_ref[...], v_ref[...], g)

    blk = pl.BlockSpec((rb, bc), lambda i, j: (i, j))
    out = jax.ShapeDtypeStruct((rows, cols), F32)
    if outer:
        grad_specs = [pl.BlockSpec((N_DEV, rb), lambda i, j: (0, i)), pl.BlockSpec((N_DEV, bc), lambda i, j: (0, j))]
        grad_ins = tuple(parts)
    else:
        grad_specs = [pl.BlockSpec((parts.shape[0], rb, bc), lambda i, j: (0, i, j))]
        grad_ins = (parts,)
    return _pcall(
        body, name=name, grid=(rows // rb, cols // bc), in_specs=[blk, blk, blk] + grad_specs,
        out_specs=[blk] * 4, out_shape=[out] * 4, compiler_params=_params("parallel", "parallel"),
    )(w, m, v, *grad_ins)


SMALL = ("b_ada", "norm1_g", "b_gate", "gmlp_ln_g", "gmlp_ln_b", "gmlp_ws", "gmlp_bs", "hg_lb", "hg_norm_g",
         "norm2_g", "final_norm_g")
BIG = ("w_in", "w_branch_gmlp", "w_branch_hg", "w_out", "w_ffn_in", "w_ffn_out")
WEIGHTS = ("w_ada", "b_ada", "norm1_g", "w_in", "b_gate", "gmlp_ln_g", "gmlp_ln_b", "gmlp_ws", "gmlp_bs", "hg_lb",
           "hg_norm_g", "w_branch_gmlp", "w_branch_hg", "w_out", "norm2_g", "w_ffn_in", "w_ffn_out", "final_norm_g")


def _pack(parts):
    return jnp.concatenate([p.reshape(-1, LANES) for p in parts], axis=0)


def _step(x, c, loss_target, w, m, v):
    s, d = x.shape[1], x.shape[2]
    gw = w["gmlp_ln_g"].shape[-1]
    hw = w["hg_norm_g"].shape[-1]
    x2d, tgt = x[0], loss_target[0]
    mx, my, mc = lax.axis_index("x"), lax.axis_index("y"), lax.axis_index("c")
    chip = 2 * mx + my
    dev = 2 * chip + mc
    q_col = 2 * gw // LANES
    gate_col = (2 * gw + 4 * hw) // d
    place = jnp.stack([chip, mc]).astype(jnp.int32)
    _Order.last = None

    c_all = _all_gather8(c.reshape(-1, LANES), "gather_c").reshape(N_DEV, d)
    gather_in = _WeightGather("in", _WeightGather.zones([w["w_in"][0]], chip), ring=True)
    zones_mix = _WeightGather.zones([w[n][0] for n in ("w_branch_gmlp", "w_branch_hg", "w_out")], chip)
    zones_fi = _WeightGather.zones([w["w_ffn_in"][0]], chip)
    zones_fo = _WeightGather.zones([w["w_ffn_out"][0]], chip)
    _Order.also = (*zones_mix, *zones_fi, *zones_fo)
    n_ada = w["w_ada"].shape[-1]
    b_ada_q = lax.dynamic_slice(w["b_ada"], (0, chip * n_ada), (1, n_ada))
    mod_q = _ada_fwd(c_all, w["w_ada"][0], b_ada_q)
    mod_all = _all_gather8(mod_q, "gather_mod")
    mod = lax.dynamic_index_in_dim(mod_all, dev, axis=1, keepdims=False)[::2].reshape(1, 6 * d)
    sh1, sc1, gt1, sh2, sc2, gt2 = [mod[:, i * d:(i + 1) * d] for i in range(6)]

    gather_in.relay_and_pass_near()
    gather_mix = _WeightGather("mix", zones_mix)
    gather_fi = _WeightGather("fi", zones_fi, ring=True)

    norm1_g, norm2_g, final_g = w["norm1_g"], w["norm2_g"], w["final_norm_g"].reshape(1, d)
    ln_g, ln_b = w["gmlp_ln_g"], w["gmlp_ln_b"]
    ws = w["gmlp_ws"][0]
    groups = ws.shape[0]
    bs_t = jnp.pad(w["gmlp_bs"][0].T, ((0, 0), (0, LANES - groups)))
    hg_lb, hg_ng, b_gate = w["hg_lb"], w["hg_norm_g"], w["b_gate"]

    h1 = _norm_mod_fwd(x2d, norm1_g, sc1, sh1, "norm1_fwd")
    order = jnp.stack([chip, 2 * (1 - mx) + my, 2 * mx + (1 - my), 2 * (1 - mx) + (1 - my)]).astype(jnp.int32)
    w_in, = gather_in.near_done()
    z = _matmul_quarters(h1, w_in, order, 0, 3, "mm_z_near")
    w_in, = gather_in.pass_far()
    z = _matmul_quarters(h1, w_in, order, 3, 1, "mm_z_far", prev=z)
    gather_mix.pass_on()
    ya = _gmlp_fwd(z, ln_g, ln_b, ws, bs_t)
    yb, o_raw, states = _hgrn_fwd(z, hg_lb, hg_ng, q_col)
    gather_fi.relay()
    gather_fo = _WeightGather("fo", zones_fo)
    w_bg, w_bh, w_out = gather_mix.done()
    w_bg, w_bh = [wq.transpose(1, 0, 2).reshape(wq.shape[1], -1) for wq in (w_bg, w_bh)]
    w_out = w_out.reshape(-1, w_out.shape[-1])
    pa = _matmul(ya, w_bg, mode="nn", name="mm_pa", out_dtype=BF16)
    pb = _matmul(yb, w_bh, mode="nn", name="mm_pb", out_dtype=BF16)
    y = _gate_fwd(pa, pb, z, b_gate, gate_col)
    gather_fi.pass_on()
    yo = _matmul(y, w_out, mode="nn", name="mm_yo", out_dtype=F32)
    x1, h2 = _norm_mod_fwd(x2d, norm2_g, sc2, sh2, "norm2_fwd", res=yo, gt=gt1)
    w_fi, = gather_fi.done()
    a_act, up_act, hf = _ffn_in(h2, w_fi)
    gather_fo.pass_on()
    w_fo, = gather_fo.done()
    w_fo = w_fo.reshape(-1, w_fo.shape[-1])
    ffn = _matmul(hf, w_fo, mode="nn", name="mm_ffn", out_dtype=F32)
    dx2, dffn, loss_row, d_final_g, d_gt2 = _final_loss(x1, ffn, gt2, final_g, tgt)

    g_fo = _matmul(hf, dffn, mode="tn", name="mm_g_fo", out_dtype=BF16, bm=1408)
    daup = tuple(_ffn_out_bwd(dffn, w_fo, a_act, up_act))
    g_fi = _matmul(h2, daup, mode="tn", name="mm_g_fi", out_dtype=BF16, out_slots=True, bn=1408)
    red_ffn = _GradReduce("ffn", [g_fo.reshape(N_CHIPS, -1, g_fo.shape[-1]), g_fi])
    dh2 = _matmul(daup, w_fi, mode="nt", name="mm_dh2", out_dtype=F32, b_slots=True)
    red_ffn.step(place)
    dx1, dyo, d_sh2, d_sc2, d_norm2, d_gt1 = _norm_mod_bwd(dh2, x1, dx2, norm2_g, sc2, "norm2_bwd", branch=yo, gt=gt1)
    g_out = _matmul(y, dyo, mode="tn", name="mm_g_out", out_dtype=BF16)
    dpa, dpb, dz_ga, dz_gb, d_b_ga, d_b_gb = _out_bwd(dyo, w_out, pa, pb, z, b_gate, gate_col)
    g_bg = _matmul(ya, dpa, mode="tn", name="mm_g_bg", out_dtype=BF16, out_slots=True)
    g_bh = _matmul(yb, dpb, mode="tn", name="mm_g_bh", out_dtype=BF16, out_slots=True)
    red_mix = _GradReduce("mix", [g_out.reshape(N_CHIPS, -1, g_out.shape[-1]), g_bg, g_bh])
    dya = _matmul(dpa, w_bg, mode="nt", name="mm_dya", out_dtype=F32)
    dyb = _matmul(dpb, w_bh, mode="nt", name="mm_dyb", out_dtype=F32)
    red_mix.step(place)
    dz_uv, d_ws, d_bs_t, d_ln_g, d_ln_b = _gmlp_bwd(dya, z, ln_g, ln_b, ws, bs_t)
    dz_q, dz_f, dz_i, dz_g, d_hg_lb, d_hg_ng = _hgrn_bwd(dyb, z, o_raw, states, hg_lb, hg_ng, q_col)
    dz = jnp.concatenate([dz_uv, dz_q, dz_f, dz_i, dz_g, dz_ga, dz_gb], axis=1)
    grad, delta, new_m, new_v = {}, {}, {}, {}

    def update(n, parts, outer=False):
        outs = _adamw(w[n][0], m[n][0], v[n][0], parts, "adamw_" + n, outer=outer)
        grad[n], delta[n], new_m[n], new_v[n] = [o[None] for o in outs]

    g_in = _matmul(h1, dz, mode="tn", name="mm_g_in", out_dtype=BF16, out_slots=True, bn=1280)
    red_in = _GradReduce("in", [g_in])
    red_ffn.join(place)
    red_mix.join(place)
    red_in.step(place)
    g_fo, g_fi = red_ffn.done()
    n = "w_ffn_in"
    dh1, rode = _matmul(dz, w_in, mode="nt", name="mm_dh1", out_dtype=F32, b_slots=True,
                        rider=([w[n][0], m[n][0], v[n][0], g_fi], lambda *blk: (blk[3], *_adam_update(*blk)), 4))
    grad[n], delta[n], new_m[n], new_v[n] = [o[None] for o in rode]
    grad_x, d_sh1, d_sc1, d_norm1 = _norm_mod_bwd(dh1, x2d, dx1, norm1_g, sc1, "norm1_bwd")
    update("w_ffn_out", g_fo[None])
    g_out, g_bg, g_bh = red_mix.done()
    update("w_out", g_out[None])
    update("w_branch_gmlp", g_bg[None])
    update("w_branch_hg", g_bh[None])

    d_mod = jnp.concatenate([d_sh1, d_sc1, d_gt1, d_sh2, d_sc2, d_gt2], axis=1)
    small_part = {"b_ada": d_mod, "norm1_g": d_norm1, "b_gate": jnp.concatenate([d_b_ga, d_b_gb], axis=1), "gmlp_ln_g": d_ln_g, "gmlp_ln_b": d_ln_b,
                  "gmlp_ws": d_ws, "gmlp_bs": d_bs_t[:, :groups].T, "hg_lb": d_hg_lb, "hg_norm_g": d_hg_ng,
                  "norm2_g": d_norm2, "final_norm_g": d_final_g}
    small_all = _all_gather8(_pack([small_part[n] for n in SMALL]), "gather_small")
    d_mod_all = small_all[:, :6 * d // LANES].reshape(N_DEV, 6 * d)
    d_mod_q = lax.dynamic_slice(d_mod_all, (0, chip * n_ada), (N_DEV, n_ada))
    red_in.join(place)
    update("w_ada", (c_all, d_mod_q), outer=True)
    outs = _adamw(_pack([w[n] for n in SMALL]), _pack([m[n] for n in SMALL]), _pack([v[n] for n in SMALL]),
                  small_all, "adamw_small")
    update("w_in", red_in.done()[0][None])
    row = 0
    for n in SMALL:
        cnt = w[n].size // LANES
        for dst, o in zip((grad, delta, new_m, new_v), outs):
            dst[n] = o[row:row + cnt].reshape(w[n].shape)
        row += cnt

    loss = lax.psum(loss_row[0, 0], ("x", "y", "c"))
    return (loss, grad_x[None], *[grad[n] for n in WEIGHTS], *[delta[n] for n in WEIGHTS],
            *[new_m[n] for n in WEIGHTS], *[new_v[n] for n in WEIGHTS])


def kernel(x, c, w_ada, b_ada, norm1_g, w_in, b_gate, gmlp_ln_g, gmlp_ln_b, gmlp_ws, gmlp_bs, hg_lb, hg_norm_g, w_branch_gmlp, w_branch_hg, w_out, norm2_g, w_ffn_in, w_ffn_out, final_norm_g, loss_target, m_w_ada, m_b_ada, m_norm1_g, m_w_in, m_b_gate, m_gmlp_ln_g, m_gmlp_ln_b, m_gmlp_ws, m_gmlp_bs, m_hg_lb, m_hg_norm_g, m_w_branch_gmlp, m_w_branch_hg, m_w_out, m_norm2_g, m_w_ffn_in, m_w_ffn_out, m_final_norm_g, v_w_ada, v_b_ada, v_norm1_g, v_w_in, v_b_gate, v_gmlp_ln_g, v_gmlp_ln_b, v_gmlp_ws, v_gmlp_bs, v_hg_lb, v_hg_norm_g, v_w_branch_gmlp, v_w_branch_hg, v_w_out, v_norm2_g, v_w_ffn_in, v_w_ffn_out, v_final_norm_g):
    w = dict(w_ada=w_ada, b_ada=b_ada, norm1_g=norm1_g, w_in=w_in, b_gate=b_gate, gmlp_ln_g=gmlp_ln_g,
             gmlp_ln_b=gmlp_ln_b, gmlp_ws=gmlp_ws, gmlp_bs=gmlp_bs, hg_lb=hg_lb, hg_norm_g=hg_norm_g,
             w_branch_gmlp=w_branch_gmlp, w_branch_hg=w_branch_hg, w_out=w_out, norm2_g=norm2_g,
             w_ffn_in=w_ffn_in, w_ffn_out=w_ffn_out, final_norm_g=final_norm_g)
    m = dict(w_ada=m_w_ada, b_ada=m_b_ada, norm1_g=m_norm1_g, w_in=m_w_in, b_gate=m_b_gate, gmlp_ln_g=m_gmlp_ln_g,
             gmlp_ln_b=m_gmlp_ln_b, gmlp_ws=m_gmlp_ws, gmlp_bs=m_gmlp_bs, hg_lb=m_hg_lb, hg_norm_g=m_hg_norm_g,
             w_branch_gmlp=m_w_branch_gmlp, w_branch_hg=m_w_branch_hg, w_out=m_w_out, norm2_g=m_norm2_g,
             w_ffn_in=m_w_ffn_in, w_ffn_out=m_w_ffn_out, final_norm_g=m_final_norm_g)
    v = dict(w_ada=v_w_ada, b_ada=v_b_ada, norm1_g=v_norm1_g, w_in=v_w_in, b_gate=v_b_gate, gmlp_ln_g=v_gmlp_ln_g,
             gmlp_ln_b=v_gmlp_ln_b, gmlp_ws=v_gmlp_ws, gmlp_bs=v_gmlp_bs, hg_lb=v_hg_lb, hg_norm_g=v_hg_norm_g,
             w_branch_gmlp=v_w_branch_gmlp, w_branch_hg=v_w_branch_hg, w_out=v_w_out, norm2_g=v_norm2_g,
             w_ffn_in=v_w_ffn_in, w_ffn_out=v_w_ffn_out, final_norm_g=v_final_norm_g)
    return _step(x, c, loss_target, w, m, v)
```

```python
import functools

import jax
import jax.numpy as jnp
from jax import lax
from jax.experimental import pallas as pl
from jax.experimental.pallas import tpu as pltpu

F32 = jnp.float32
BF16 = jnp.bfloat16
EPS = 1e-6
LANES = 128
N_CHIPS = 4
N_DEV = 8
VMEM_LIMIT_BYTES = 56 * 1024 * 1024
VMEM_RIDER_LIMIT_BYTES = 60 * 1024 * 1024
HG_CHUNK = 32
HG_ROWS = 256
HG_HEADS_PER_STEP = 8
EXP_CLAMP = 80.0
ADAM_LR, ADAM_B1, ADAM_B2, ADAM_EPS, ADAM_WD, ADAM_STEP = 0.001, 0.9, 0.999, 1e-08, 0.01, 10
MESH = pl.DeviceIdType.MESH

NN = (((1,), (0,)), ((), ()))
NT = (((1,), (1,)), ((), ()))
TN = (((0,), (0,)), ((), ()))


def _dot(a, b, dims=NN, precision=None):
    return lax.dot_general(a, b, dims, precision=precision, preferred_element_type=F32)


def _params(*semantics, vmem_limit_bytes=VMEM_LIMIT_BYTES):
    return pltpu.CompilerParams(dimension_semantics=semantics, vmem_limit_bytes=vmem_limit_bytes)


class _Order:
    last = None
    also = ()


def _pcall(body, *, in_specs, out_specs, grid=(), scratch_shapes=(), num_scalar_prefetch=0, **kw):
    def run(*ins):
        deps = (() if _Order.last is None else (_Order.last,)) + tuple(_Order.also)
        _Order.also = ()
        n_in, n_dep = len(ins), len(deps)

        def wrapped(*refs):
            body(*refs[:n_in], *refs[n_in + n_dep:])

        specs = list(in_specs) + [pl.BlockSpec(memory_space=pl.ANY)] * n_dep
        if num_scalar_prefetch:
            grid_spec = pltpu.PrefetchScalarGridSpec(
                num_scalar_prefetch=num_scalar_prefetch, grid=grid, in_specs=specs, out_specs=out_specs,
                scratch_shapes=scratch_shapes)
            outs = pl.pallas_call(wrapped, grid_spec=grid_spec, **kw)(*ins, *deps)
        else:
            outs = pl.pallas_call(wrapped, grid=grid, in_specs=specs, out_specs=out_specs,
                                  scratch_shapes=scratch_shapes, **kw)(*ins, *deps)
        _Order.last = jax.tree.leaves(outs)[0]
        return outs

    return run


def _pick_rows(dim, pref, mult=16):
    best = None
    for cand in range(mult, min(dim, pref) + 1, mult):
        if dim % cand == 0:
            best = cand
    assert best is not None, (dim, pref)
    return best


def _pick(dim, pref):
    if dim <= pref:
        return dim
    best = None
    for cand in range(LANES, pref + 1, LANES):
        if dim % cand == 0:
            best = cand
    assert best is not None, (dim, pref)
    return best


def _sigmoid(x):
    return 1.0 / (1.0 + jnp.exp(-x))


def _gelu(x):
    c = 0.7978845608028654
    return 0.5 * x * (1.0 + jnp.tanh(c * (x + 0.044715 * x * x * x)))


def _gelu_grad(x):
    c = 0.7978845608028654
    t = jnp.tanh(c * (x + 0.044715 * x * x * x))
    return 0.5 * (1.0 + t) + 0.5 * x * (1.0 - t * t) * c * (1.0 + 3.0 * 0.044715 * x * x)


def _rms(x):
    r = lax.rsqrt(jnp.mean(x * x, axis=-1, keepdims=True) + EPS)
    return x * r, r


def _colsum(x):
    return jnp.sum(x, axis=0, keepdims=True)


def _accumulate(first, ref, val):
    @pl.when(first)
    def _():
        ref[...] = val

    @pl.when(jnp.logical_not(first))
    def _():
        ref[...] += val


def _matmul(a, b, *, mode, name, out_dtype, b_slots=False, out_slots=False, bm=1024, bn=1024, bk=2816, rider=None):
    pair = a if isinstance(a, tuple) else b if isinstance(b, tuple) else None
    if mode == "nn":
        m, k = a.shape
        n = b.shape[2] * N_CHIPS if b_slots else b.shape[1]
        per = b.shape[2] if b_slots else n
    elif mode == "nt":
        m, k = (a[0].shape[0], 2 * a[0].shape[1]) if pair else a.shape
        n = b.shape[1] if b_slots else b.shape[0]
        per = b.shape[2] if b_slots else k
    else:
        k, m = a.shape
        n = 2 * b[0].shape[1] if pair else b.shape[1]
        per = n // N_CHIPS if out_slots else n
    bm = _pick(m, bm)
    if mode == "nt":
        bn, bk = _pick(n, bn), _pick(per, bk)
    else:
        bn, bk = _pick(per, bn), _pick(k, bk)
    nk = k // bk
    per_blocks = per // (bk if mode == "nt" else bn)
    dims = {"nn": NN, "nt": NT, "tn": TN}[mode]
    half = (nk if mode == "nt" else n // bn) // 2

    def product(a_ref, b_ref, o_ref, acc):
        part = _dot(a_ref[...], b_ref[...], dims)
        if nk == 1:
            o_ref[...] = part.astype(o_ref.dtype)
            return
        acc_ref, kk = acc[0], pl.program_id(2)

        @pl.when(kk == 0)
        def _():
            acc_ref[...] = part

        @pl.when(jnp.logical_and(kk > 0, kk < nk - 1))
        def _():
            acc_ref[...] += part

        @pl.when(kk == nk - 1)
        def _():
            o_ref[...] = (acc_ref[...] + part).astype(o_ref.dtype)

    n_ride, n_rode = (len(rider[0]), rider[2]) if rider else (0, 0)

    def body(*refs):
        if rider:
            ride_in, ride_out = refs[2:2 + n_ride], refs[3 + n_ride:3 + n_ride + n_rode]

            def ride(t, carry):
                rows = pl.ds(pl.multiple_of(t * 8, 8), 8)
                for ref, val in zip(ride_out, rider[1](*[r[rows, :] for r in ride_in])):
                    ref[rows, :] = val
                return carry

            lax.fori_loop(0, ride_in[0].shape[0] // 8, ride, 0)
        if not pair:
            return product(refs[0], refs[1], refs[2 + n_ride], refs[3 + n_ride + n_rode:])
        first = pl.program_id(2 if mode == "nt" else 1) < half
        x1, x2, y, o_ref, acc = refs[0], refs[1], refs[2], refs[3], refs[4:]

        @pl.when(first)
        def _():
            product(*((x1, y) if mode == "nt" else (y, x1)), o_ref, acc)

        @pl.when(jnp.logical_not(first))
        def _():
            product(*((x2, y) if mode == "nt" else (y, x2)), o_ref, acc)

    if mode == "nn":
        a_spec = pl.BlockSpec((bm, bk), lambda i, j, kk: (i, kk))
        if b_slots:
            b_spec = pl.BlockSpec((None, bk, bn), lambda i, j, kk: (j // per_blocks, kk, j % per_blocks))
        else:
            b_spec = pl.BlockSpec((bk, bn), lambda i, j, kk: (kk, j))
    elif mode == "nt":
        a_spec = pl.BlockSpec((bm, bk), lambda i, j, kk: (i, kk))
        if b_slots:
            b_spec = pl.BlockSpec((None, bn, bk), lambda i, j, kk: (kk // per_blocks, j, kk % per_blocks))
        else:
            b_spec = pl.BlockSpec((bn, bk), lambda i, j, kk: (j, kk))
    else:
        a_spec = pl.BlockSpec((bk, bm), lambda i, j, kk: (kk, i))
        b_spec = pl.BlockSpec((bk, bn), lambda i, j, kk: (kk, j))
    if out_slots:
        o_spec = pl.BlockSpec((None, bm, bn), lambda i, j, kk: (j // per_blocks, i, j % per_blocks))
        out_shape = jax.ShapeDtypeStruct((N_CHIPS, m, per), out_dtype)
    else:
        o_spec = pl.BlockSpec((bm, bn), lambda i, j, kk: (i, j))
        out_shape = jax.ShapeDtypeStruct((m, n), out_dtype)
    if not pair:
        ins, in_specs = (a, b), [a_spec, b_spec]
    elif mode == "nt":
        ins = (*a, b)
        in_specs = [pl.BlockSpec((bm, bk), lambda i, j, kk: (i, jnp.minimum(kk, half - 1))),
                    pl.BlockSpec((bm, bk), lambda i, j, kk: (i, jnp.maximum(kk - half, 0))), b_spec]
    else:
        ins = (*b, a)
        in_specs = [pl.BlockSpec((bk, bn), lambda i, j, kk: (kk, jnp.minimum(j, half - 1))),
                    pl.BlockSpec((bk, bn), lambda i, j, kk: (kk, jnp.maximum(j - half, 0))), a_spec]
    scratch = [pltpu.VMEM((bm, bn), F32)] if nk > 1 else []
    if not rider:
        return _pcall(
            body, name=name, grid=(m // bm, n // bn, nk), in_specs=in_specs, out_specs=o_spec, out_shape=out_shape,
            scratch_shapes=scratch, compiler_params=_params("parallel", "parallel", "arbitrary"),
        )(*ins)
    assert not pair
    rows, cols = rider[0][0].shape
    nj = n // bn
    rb = rows // ((m // bm) * nj * nk)
    assert rows == rb * (m // bm) * nj * nk and rb % 8 == 0, (rows, rb)
    ride_spec = pl.BlockSpec((rb, cols), lambda i, j, kk: ((i * nj + j) * nk + kk, 0))
    outs = _pcall(
        body, name=name, grid=(m // bm, nj, nk), in_specs=in_specs + [ride_spec] * n_ride,
        out_specs=[o_spec] + [ride_spec] * n_rode,
        out_shape=[out_shape] + [jax.ShapeDtypeStruct((rows, cols), F32)] * n_rode,
        scratch_shapes=scratch,
        compiler_params=_params("arbitrary", "arbitrary", "arbitrary", vmem_limit_bytes=VMEM_RIDER_LIMIT_BYTES),
    )(*ins, *rider[0])
    return outs[0], outs[1:]


def _matmul_quarters(h, w_slots, order, first, count, name, prev=None, bm=1024, bn=1280):
    s, k = h.shape
    n = w_slots.shape[2]
    bm, bn = _pick(s, bm), _pick(n, bn)
    pb = n // bn

    def body(order_ref, h_ref, w_ref, *rest):
        rest[-1][...] = _dot(h_ref[...], w_ref[...])

    ins = (order, h, w_slots) + (() if prev is None else (prev,))
    return _pcall(
        body, name=name, num_scalar_prefetch=1, grid=(count * pb, s // bm),
        in_specs=[pl.BlockSpec((bm, k), lambda j, i, o: (i, 0)),
                  pl.BlockSpec((None, k, bn), lambda j, i, o: (o[first + j // pb], 0, j % pb))]
        + ([] if prev is None else [ANY_SPEC]),
        out_specs=pl.BlockSpec((bm, bn), lambda j, i, o: (i, o[first + j // pb] * pb + j % pb)),
        out_shape=jax.ShapeDtypeStruct((s, N_CHIPS * n), F32),
        input_output_aliases={} if prev is None else {3: 0},
        compiler_params=_params("arbitrary", "arbitrary"),
    )(*ins)


def _place():
    x, y, c = lax.axis_index("x"), lax.axis_index("y"), lax.axis_index("c")
    chips = [(1 - x, y), (x, 1 - y), (1 - x, 1 - y)]
    return x, y, c, chips


def _all_gather8(block, name):
    def body(x_ref, out_ref, send_sems, recv_sems, local_sem):
        x, y, c, chips = _place()
        me, sibling = (x, y, c), (x, y, 1 - c)

        def slot(px, py, pc):
            return out_ref.at[4 * px + 2 * py + pc]

        def copy(k, blk, to, src=None):
            return pltpu.make_async_remote_copy(
                src_ref=slot(*blk) if src is None else src, dst_ref=slot(*blk),
                send_sem=send_sems.at[k], recv_sem=recv_sems.at[k], device_id=to, device_id_type=MESH)

        mine = pltpu.make_async_copy(x_ref, slot(*me), local_sem)
        mine.start()
        first = [copy(0, me, sibling, src=x_ref)]
        first += [copy(1 + j, me, (*chip, c), src=x_ref) for j, chip in enumerate(chips)]
        for cp in first:
            cp.start()
        passed = [copy(4 + j, (*chip, c), sibling) for j, chip in enumerate(chips)]
        for j, chip in enumerate(chips):
            copy(1 + j, (*chip, c), me).wait_recv()
            passed[j].start()
        copy(0, sibling, me).wait_recv()
        for j, chip in enumerate(chips):
            copy(4 + j, (*chip, 1 - c), me).wait_recv()
        for cp in first + passed:
            cp.wait_send()
        mine.wait()

    return _pcall(
        body, name=name, out_shape=jax.ShapeDtypeStruct((N_DEV,) + block.shape, block.dtype),
        in_specs=[pl.BlockSpec(memory_space=pltpu.VMEM)], out_specs=pl.BlockSpec(memory_space=pltpu.VMEM),
        scratch_shapes=[pltpu.SemaphoreType.DMA((7,)), pltpu.SemaphoreType.DMA((7,)), pltpu.SemaphoreType.DMA],
        compiler_params=pltpu.CompilerParams(vmem_limit_bytes=VMEM_LIMIT_BYTES),
    )(block)


HBM_SPEC = pl.BlockSpec(memory_space=pltpu.HBM)
SEM_SPEC = pl.BlockSpec(memory_space=pltpu.SEMAPHORE)
ANY_SPEC = pl.BlockSpec(memory_space=pl.ANY)
EFFECT = pltpu.SideEffectType.DATAFLOW_SIDE_EFFECTING


def _xfer_start(name, bufs, plan, n_copies, after_last=False):
    nb = len(bufs)
    deps = (_Order.last,) if after_last and _Order.last is not None else ()
    nd = len(deps)

    def body(*refs):
        send_sems, recv_sems = refs[nb + nd], refs[nb + nd + 1]
        token = refs[nb + nd + 2 + nb]
        for k, (src, dst, dev) in enumerate(plan(refs[:nb], *_place())):
            pltpu.make_async_remote_copy(src_ref=src, dst_ref=dst, send_sem=send_sems.at[k], recv_sem=recv_sems.at[k],
                                         device_id=dev, device_id_type=MESH).start()
        token[...] = jnp.zeros_like(token)

    outs = pl.pallas_call(
        body, name=name,
        out_shape=(pltpu.SemaphoreType.DMA((n_copies,)), pltpu.SemaphoreType.DMA((n_copies,)),
                   *[pltpu.HBM(b.shape, b.dtype) for b in bufs], jax.ShapeDtypeStruct((8, LANES), F32)),
        in_specs=[HBM_SPEC] * nb + [ANY_SPEC] * nd,
        out_specs=(SEM_SPEC, SEM_SPEC, *[HBM_SPEC] * nb, pl.BlockSpec(memory_space=pltpu.VMEM)),
        input_output_aliases={i: 2 + i for i in range(nb)},
        compiler_params=pltpu.CompilerParams(has_side_effects=EFFECT),
    )(*[pltpu.with_memory_space_constraint(b, pltpu.HBM) for b in bufs], *deps)
    _Order.last = outs[-1]
    return (outs[0], outs[1]), list(outs[2:2 + nb])


def _xfer_wait(name, sems, bufs, plan):
    nb = len(bufs)

    def body(*refs):
        send_sems, recv_sems = refs[nb], refs[nb + 1]
        for k, (src, dst, dev) in enumerate(plan(refs[:nb], *_place())):
            copy = pltpu.make_async_remote_copy(src_ref=src, dst_ref=dst, send_sem=send_sems.at[k],
                                                recv_sem=recv_sems.at[k], device_id=dev, device_id_type=MESH)
            copy.wait_send()
            copy.wait_recv()

    outs = pl.pallas_call(
        body, name=name, out_shape=tuple(pltpu.HBM(b.shape, b.dtype) for b in bufs),
        in_specs=[HBM_SPEC] * nb + [SEM_SPEC, SEM_SPEC, ANY_SPEC], out_specs=tuple([HBM_SPEC] * nb),
        input_output_aliases={i: i for i in range(nb)},
        compiler_params=pltpu.CompilerParams(has_side_effects=EFFECT),
    )(*bufs, *sems, _Order.last)
    _Order.last = outs[0]
    return list(outs)


def _xfer_hand_over(name, sems, bufs, plan, next_plans):
    nb, n_next = len(bufs), len(next_plans)

    def body(*refs):
        send_sems, recv_sems = refs[nb], refs[nb + 1]
        outs = refs[nb + 3:]
        place = _place()
        for k, (src, dst, dev) in enumerate(plan(refs[:nb], *place)):
            copy = pltpu.make_async_remote_copy(src_ref=src, dst_ref=dst, send_sem=send_sems.at[k],
                                                recv_sem=recv_sems.at[k], device_id=dev, device_id_type=MESH)
            copy.wait_send()
            copy.wait_recv()
        for p, (next_plan, _) in enumerate(next_plans):
            for k, (src, dst, dev) in enumerate(next_plan(refs[:nb], *place)):
                pltpu.make_async_remote_copy(src_ref=src, dst_ref=dst, send_sem=outs[2 * p].at[k],
                                             recv_sem=outs[2 * p + 1].at[k], device_id=dev, device_id_type=MESH).start()
        outs[-1][...] = jnp.zeros_like(outs[-1])

    sem_shapes = [pltpu.SemaphoreType.DMA((copies,)) for _, copies in next_plans for _ in range(2)]
    outs = pl.pallas_call(
        body, name=name,
        out_shape=(*sem_shapes, *[pltpu.HBM(b.shape, b.dtype) for b in bufs], jax.ShapeDtypeStruct((8, LANES), F32)),
        in_specs=[HBM_SPEC] * nb + [SEM_SPEC, SEM_SPEC, ANY_SPEC],
        out_specs=(*[SEM_SPEC] * (2 * n_next), *[HBM_SPEC] * nb, pl.BlockSpec(memory_space=pltpu.VMEM)),
        input_output_aliases={i: 2 * n_next + i for i in range(nb)},
        compiler_params=pltpu.CompilerParams(has_side_effects=EFFECT),
    )(*bufs, *sems, _Order.last)
    _Order.last = outs[-1]
    return [(outs[2 * p], outs[2 * p + 1]) for p in range(n_next)], list(outs[2 * n_next:2 * n_next + nb])


def _half(ref, c, axis):
    rows = ref.shape[axis] // 2
    return pl.ds(c * rows, rows)


def _plan_weights_ici(n):
    def plan(refs, x, y, c, chips):
        out = []
        for w in range(n):
            region = refs[w].at[2 * x + y, _half(refs[w], c, 1), :]
            out += [(region, region, (*chip, c)) for chip in chips]
        return out
    return plan


def _plan_weights_ring(n):
    def plan(refs, x, y, c, chips):
        out = []
        for w in range(n):
            region = refs[w].at[2 * x + y, _half(refs[w], c, 1), :]
            out += [(region, region, (*chip, c)) for chip in chips[:2]]
        return out
    return plan


def _plan_weights_relay(n):
    def plan(refs, x, y, c, chips):
        out = []
        for w in range(n):
            quarter_rows = refs[w].shape[1] // 4
            upper = refs[w].at[2 * x + (1 - y), pl.ds(2 * c * quarter_rows, quarter_rows), :]
            lower = refs[w].at[2 * (1 - x) + y, pl.ds((2 * c + 1) * quarter_rows, quarter_rows), :]
            out += [(upper, upper, (1 - x, y, c)), (lower, lower, (x, 1 - y, c))]
        return out
    return plan


def _plan_weights_d2d(n, which=slice(0, 3)):
    def plan(refs, x, y, c, chips):
        out = []
        for w in range(n):
            rows = _half(refs[w], c, 1)
            for chip in chips[which]:
                region = refs[w].at[2 * chip[0] + chip[1], rows, :]
                out.append((region, region, (x, y, 1 - c)))
        return out
    return plan


def _plan_grads_d2d(n):
    def plan(refs, x, y, c, chips):
        return [(refs[w].at[:, _half(refs[w], 1 - c, 1), :], refs[n + w], (x, y, 1 - c)) for w in range(n)]
    return plan


def _plan_grads_ici(n):
    def plan(refs, x, y, c, chips):
        out = []
        for w in range(n):
            out += [(refs[w].at[2 * chip[0] + chip[1]], refs[n + w].at[2 * x + y], (*chip, c)) for chip in chips]
        return out
    return plan


def _plan_final_d2d(n):
    def plan(refs, x, y, c, chips):
        out = []
        for w in range(n):
            region = refs[w].at[_half(refs[w], c, 0), :]
            out.append((region, region, (x, y, 1 - c)))
        return out
    return plan


def _stream_blocks(hr, cols):
    bc = cols if cols <= 4096 else _pick(cols, 4096)
    return _pick_rows(hr, max(16, (768 * 1024) // bc)), bc


def _pre_reduce(g, landed, place, name):
    _, rows, cols = g.shape
    hr = rows // 2
    rb, bc = _stream_blocks(hr, cols)
    nrb = hr // rb

    def body(place_ref, g_ref, l_ref, o_ref):
        o_ref[...] = (g_ref[...].astype(F32) + l_ref[...].astype(F32)).astype(o_ref.dtype)

    return _pcall(
        body, name=name, num_scalar_prefetch=1, grid=(N_CHIPS, nrb, cols // bc),
        in_specs=[pl.BlockSpec((None, rb, bc), lambda j, i, k, p: (j, p[1] * nrb + i, k)),
                  pl.BlockSpec((None, rb, bc), lambda j, i, k, p: (j, i, k))],
        out_specs=pl.BlockSpec((None, rb, bc), lambda j, i, k, p: (j, i, k)),
        out_shape=jax.ShapeDtypeStruct((N_CHIPS, hr, cols), g.dtype),
        compiler_params=_params("parallel", "parallel", "parallel"),
    )(place, g, landed)


def _sum_slots(mine, landed, place, name):
    _, hr, cols = mine.shape
    rb, bc = _stream_blocks(hr, cols)
    rb = _pick_rows(hr, max(16, rb // 2))
    nrb = hr // rb

    def body(place_ref, m_ref, l_ref, o_ref):
        chip = place_ref[0]
        own = m_ref[...].astype(F32)
        total = jnp.where(chip == 0, own, l_ref[0].astype(F32))
        for j in range(1, N_CHIPS):
            total = total + jnp.where(chip == j, own, l_ref[j].astype(F32))
        o_ref[...] = total

    return _pcall(
        body, name=name, num_scalar_prefetch=1, grid=(nrb, cols // bc),
        in_specs=[pl.BlockSpec((None, rb, bc), lambda i, k, p: (p[0], i, k)),
                  pl.BlockSpec((N_CHIPS, rb, bc), lambda i, k, p: (0, i, k))],
        out_specs=pl.BlockSpec((rb, bc), lambda i, k, p: (p[1] * nrb + i, k)),
        out_shape=jax.ShapeDtypeStruct((2 * hr, cols), F32),
        compiler_params=_params("parallel", "parallel"),
    )(place, mine, landed)


class _WeightGather:
    @staticmethod
    def zones(quarters, chip):
        return [lax.dynamic_update_slice(lax.empty((N_CHIPS,) + q.shape, BF16), q.astype(BF16)[None], (chip, 0, 0))
                for q in quarters]

    def __init__(self, tag, zones, ring=False):
        self.tag, self.n, self.ring = tag, len(zones), ring
        self.plan = _plan_weights_ring(self.n) if ring else _plan_weights_ici(self.n)
        self.sems, self.bufs = _xfer_start("wici_start_" + tag, zones, self.plan, (2 if ring else 3) * self.n,
                                           after_last=True)

    def relay(self):
        plan = _plan_weights_relay(self.n)
        (self.sems,), self.bufs = _xfer_hand_over("wrel_start_" + self.tag, self.sems, self.bufs, self.plan,
                                                   [(plan, 2 * self.n)])
        self.plan = plan

    def pass_on(self):
        plan = _plan_weights_d2d(self.n)
        (self.sems,), self.bufs = _xfer_hand_over("wd2d_start_" + self.tag, self.sems, self.bufs, self.plan,
                                                   [(plan, 3 * self.n)])
        self.plan = plan

    def done(self):
        return _xfer_wait("wd2d_wait_" + self.tag, self.sems, self.bufs, self.plan)

    def relay_and_pass_near(self):
        plan, self.near_plan = _plan_weights_relay(self.n), _plan_weights_d2d(self.n, slice(0, 2))
        (self.sems, self.near_sems), self.bufs = _xfer_hand_over(
            "wrel_start_" + self.tag, self.sems, self.bufs, self.plan, [(plan, 2 * self.n), (self.near_plan, 2 * self.n)])
        self.plan = plan

    def near_done(self):
        self.bufs = _xfer_wait("wnear_wait_" + self.tag, self.near_sems, self.bufs, self.near_plan)
        return self.bufs

    def pass_far(self):
        plan = _plan_weights_d2d(self.n, slice(2, 3))
        (sems,), bufs = _xfer_hand_over("wfar_start_" + self.tag, self.sems, self.bufs, self.plan, [(plan, self.n)])
        return _xfer_wait("wfar_wait_" + self.tag, sems, bufs, plan)


class _GradReduce:
    def __init__(self, tag, grads):
        self.tag, self.n = tag, len(grads)
        zones = [lax.empty((N_CHIPS, g.shape[1] // 2, g.shape[2]), g.dtype) for g in grads]
        self.plan = _plan_grads_d2d(self.n)
        self.sems, self.bufs = _xfer_start("gd2d_start_" + tag, list(grads) + zones, self.plan, self.n)

    def pair(self, place):
        n = self.n
        bufs = _xfer_wait("gd2d_wait_" + self.tag, self.sems, self.bufs, self.plan)
        self.halves = [_pre_reduce(bufs[w], bufs[n + w], place, f"pre_reduce_{self.tag}{w}") for w in range(n)]

    def cross(self, after_last=False):
        zones = [lax.empty(h.shape, h.dtype) for h in self.halves]
        self.plan = _plan_grads_ici(self.n)
        self.sems, self.bufs = _xfer_start("gici_start_" + self.tag, self.halves + zones, self.plan, 3 * self.n,
                                           after_last=after_last)

    def step(self, place):
        self.pair(place)
        self.cross()

    def join(self, place):
        n = self.n
        bufs = _xfer_wait("gici_wait_" + self.tag, self.sems, self.bufs, self.plan)
        sums = [_sum_slots(bufs[w], bufs[n + w], place, f"sum_slots_{self.tag}{w}") for w in range(n)]
        self.plan = _plan_final_d2d(n)
        self.sems, self.bufs = _xfer_start("gfin_start_" + self.tag, sums, self.plan, n)

    def done(self):
        return _xfer_wait("gfin_wait_" + self.tag, self.sems, self.bufs, self.plan)


def _ada_fwd(c_all, w_q, b_q):
    d, n = w_q.shape
    bn = _pick(n, 512)

    def body(c_ref, w_ref, b_ref, o_ref):
        cv = c_ref[...]
        act = cv * _sigmoid(cv)
        o_ref[...] = _dot(act, w_ref[...], NN, lax.Precision.HIGHEST) + b_ref[...]

    return _pcall(
        body, name="ada_fwd", grid=(n // bn,),
        in_specs=[pl.BlockSpec((N_DEV, d), lambda j: (0, 0)), pl.BlockSpec((d, bn), lambda j: (0, j)),
                  pl.BlockSpec((1, bn), lambda j: (0, j))],
        out_specs=pl.BlockSpec((N_DEV, bn), lambda j: (0, j)),
        out_shape=jax.ShapeDtypeStruct((N_DEV, n), F32), compiler_params=_params("parallel"),
    )(c_all, w_q, b_q)


def _row_spec(rb, width, col=0):
    return pl.BlockSpec((rb, width), lambda i, col=col: (i, col))


def _vec_spec(width, col=0):
    return pl.BlockSpec((1, width), lambda i, col=col: (0, col))


def _norm_mod_fwd(x, g, sc, sh, name, res=None, gt=None):
    s, d = x.shape
    rb = _pick(s, 256)
    has_res = res is not None

    def body(*refs):
        if has_res:
            x_ref, res_ref, gt_ref, g_ref, sc_ref, sh_ref, x1_ref, h_ref = refs
            xv = x_ref[...] + gt_ref[...] * res_ref[...]
            x1_ref[...] = xv
        else:
            x_ref, g_ref, sc_ref, sh_ref, h_ref = refs
            xv = x_ref[...]
        xh, _ = _rms(xv)
        h_ref[...] = (xh * g_ref[...] * (1.0 + sc_ref[...]) + sh_ref[...]).astype(BF16)

    row, vec = _row_spec(rb, d), _vec_spec(d)
    if has_res:
        ins, in_specs = (x, res, gt, g, sc, sh), [row, row, vec, vec, vec, vec]
        out_shape = [jax.ShapeDtypeStruct((s, d), F32), jax.ShapeDtypeStruct((s, d), BF16)]
        out_specs = [row, row]
    else:
        ins, in_specs = (x, g, sc, sh), [row, vec, vec, vec]
        out_shape, out_specs = jax.ShapeDtypeStruct((s, d), BF16), row
    return _pcall(body, name=name, grid=(s // rb,), in_specs=in_specs, out_specs=out_specs,
                          out_shape=out_shape, compiler_params=_params("parallel"))(*ins)


def _final_loss(x1, f, gt2, g_final, target):
    s, d = x1.shape
    rb = _pick(s, 256)

    def body(x1_ref, f_ref, gt_ref, g_ref, t_ref, dx_ref, df_ref, loss_ref, dg_ref, dgt_ref):
        first = pl.program_id(0) == 0
        fv, gt, gv = f_ref[...], gt_ref[...], g_ref[...]
        x2 = x1_ref[...] + gt * fv
        xh, r = _rms(x2)
        err = xh * gv - t_ref[...]
        blk = 0.5 * jnp.sum(jnp.sum(err * err, axis=1, keepdims=True), axis=0, keepdims=True) / d
        dy = err / d
        dxh = dy * gv
        dx = r * (dxh - xh * jnp.mean(dxh * xh, axis=-1, keepdims=True))
        dx_ref[...] = dx
        df_ref[...] = (dx * gt).astype(BF16)
        _accumulate(first, loss_ref, jnp.broadcast_to(blk, (1, LANES)))
        _accumulate(first, dg_ref, _colsum(dy * xh))
        _accumulate(first, dgt_ref, _colsum(dx * fv))

    row, vec = _row_spec(rb, d), _vec_spec(d)
    return _pcall(
        body, name="final_loss", grid=(s // rb,), in_specs=[row, row, vec, vec, row],
        out_specs=[row, row, _vec_spec(LANES), vec, vec],
        out_shape=[jax.ShapeDtypeStruct((s, d), F32), jax.ShapeDtypeStruct((s, d), BF16),
                   jax.ShapeDtypeStruct((1, LANES), F32), jax.ShapeDtypeStruct((1, d), F32),
                   jax.ShapeDtypeStruct((1, d), F32)],
        compiler_params=_params("arbitrary"),
    )(x1, f, gt2, g_final, target)


def _norm_mod_bwd(dh, xin, dres, g, sc, name, branch=None, gt=None):
    s, d = xin.shape
    rb = _pick(s, 256)
    has_branch = branch is not None

    def body(*refs):
        if has_branch:
            dh_ref, x_ref, dres_ref, g_ref, sc_ref, br_ref, gt_ref, dx_ref, dbr_ref, dsh_ref, dsc_ref, dg_ref, dgt_ref = refs
        else:
            dh_ref, x_ref, dres_ref, g_ref, sc_ref, dx_ref, dsh_ref, dsc_ref, dg_ref = refs
        first = pl.program_id(0) == 0
        gv = g_ref[...]
        xh, r = _rms(x_ref[...])
        dhv = dh_ref[...]
        dn = dhv * (1.0 + sc_ref[...])
        dxh = dn * gv
        dx = dres_ref[...] + r * (dxh - xh * jnp.mean(dxh * xh, axis=-1, keepdims=True))
        dx_ref[...] = dx
        _accumulate(first, dsh_ref, _colsum(dhv))
        _accumulate(first, dsc_ref, _colsum(dhv * xh * gv))
        _accumulate(first, dg_ref, _colsum(dn * xh))
        if has_branch:
            dbr_ref[...] = (dx * gt_ref[...]).astype(BF16)
            _accumulate(first, dgt_ref, _colsum(dx * br_ref[...]))

    row, vec = _row_spec(rb, d), _vec_spec(d)
    vec_shape = jax.ShapeDtypeStruct((1, d), F32)
    if has_branch:
        ins, in_specs = (dh, xin, dres, g, sc, branch, gt), [row, row, row, vec, vec, row, vec]
        out_specs = [row, row, vec, vec, vec, vec]
        out_shape = [jax.ShapeDtypeStruct((s, d), F32), jax.ShapeDtypeStruct((s, d), BF16)] + [vec_shape] * 4
    else:
        ins, in_specs = (dh, xin, dres, g, sc), [row, row, row, vec, vec]
        out_specs = [row, vec, vec, vec]
        out_shape = [jax.ShapeDtypeStruct((s, d), F32)] + [vec_shape] * 3
    return _pcall(body, name=name, grid=(s // rb,), in_specs=in_specs, out_specs=out_specs,
                          out_shape=out_shape, compiler_params=_params("arbitrary"))(*ins)


def _branch_merge(ya, yb, w_bg, w_bh, z, b_gate, gate_col, bm=512, bn=1024):
    s, gw = ya.shape
    d = w_bg.shape[1]
    bm, bn = _pick(s, bm), _pick(d, bn)
    nj = d // bn

    def body(ya_ref, yb_ref, wa_ref, wb_ref, za_ref, zb_ref, ba_ref, bb_ref, pa_ref, pb_ref, y_ref):
        pa = _dot(ya_ref[...], wa_ref[...])
        pb = _dot(yb_ref[...], wb_ref[...])
        pa_ref[...] = pa.astype(BF16)
        pb_ref[...] = pb.astype(BF16)
        ga = _sigmoid(za_ref[...] + ba_ref[...])
        gb = _sigmoid(zb_ref[...] + bb_ref[...])
        y_ref[...] = (ga * pa + gb * pb).astype(BF16)

    act = pl.BlockSpec((bm, gw), lambda j, i: (i, 0))
    wgt = pl.BlockSpec((gw, bn), lambda j, i: (0, j))
    tile = pl.BlockSpec((bm, bn), lambda j, i: (i, j))
    shape = jax.ShapeDtypeStruct((s, d), BF16)
    return _pcall(
        body, name="branch_merge", grid=(nj, s // bm),
        in_specs=[act, act, wgt, wgt,
                  pl.BlockSpec((bm, bn), lambda j, i: (i, gate_col * nj + j)),
                  pl.BlockSpec((bm, bn), lambda j, i: (i, (gate_col + 1) * nj + j)),
                  pl.BlockSpec((1, bn), lambda j, i: (0, j)), pl.BlockSpec((1, bn), lambda j, i: (0, nj + j))],
        out_specs=[tile, tile, tile], out_shape=[shape, shape, shape], compiler_params=_params("parallel", "parallel"),
    )(ya, yb, w_bg, w_bh, z, z, b_gate, b_gate)


def _branch_bwd(dpa, dpb, w_bg, w_bh, bm=512):
    s, d = dpa.shape
    gw = w_bg.shape[0]
    bm = _pick(s, bm)

    def body(da_ref, db_ref, wa_ref, wb_ref, oa_ref, ob_ref):
        oa_ref[...] = _dot(da_ref[...], wa_ref[...], NT)
        ob_ref[...] = _dot(db_ref[...], wb_ref[...], NT)

    act = pl.BlockSpec((bm, d), lambda i: (i, 0))
    wgt = pl.BlockSpec((gw, d), lambda i: (0, 0))
    out = pl.BlockSpec((bm, gw), lambda i: (i, 0))
    shape = jax.ShapeDtypeStruct((s, gw), F32)
    return _pcall(body, name="branch_bwd", grid=(s // bm,), in_specs=[act, act, wgt, wgt], out_specs=[out, out],
                  out_shape=[shape, shape], compiler_params=_params("parallel"))(dpa, dpb, w_bg, w_bh)


def _out_bwd(dyo, w_out, pa, pb, z, b_gate, gate_col, bm=512, bn=1024):
    s, d = dyo.shape
    bm, bn = _pick(s, bm), _pick(d, bn)
    nj = d // bn

    def body(dyo_ref, w_ref, pa_ref, pb_ref, za_ref, zb_ref, ba_ref, bb_ref,
             dpa_ref, dpb_ref, dza_ref, dzb_ref, dba_ref, dbb_ref):
        first = pl.program_id(1) == 0
        dyv = _dot(dyo_ref[...], w_ref[...], NT)
        ga = _sigmoid(za_ref[...] + ba_ref[...])
        gb = _sigmoid(zb_ref[...] + bb_ref[...])
        dpa_ref[...] = (dyv * ga).astype(BF16)
        dpb_ref[...] = (dyv * gb).astype(BF16)
        dga = dyv * pa_ref[...].astype(F32) * ga * (1.0 - ga)
        dgb = dyv * pb_ref[...].astype(F32) * gb * (1.0 - gb)
        dza_ref[...] = dga.astype(BF16)
        dzb_ref[...] = dgb.astype(BF16)
        _accumulate(first, dba_ref, _colsum(dga))
        _accumulate(first, dbb_ref, _colsum(dgb))

    tile = pl.BlockSpec((bm, bn), lambda j, i: (i, j))
    vec = pl.BlockSpec((1, bn), lambda j, i: (0, j))
    act, vec_shape = jax.ShapeDtypeStruct((s, d), BF16), jax.ShapeDtypeStruct((1, d), F32)
    return _pcall(
        body, name="out_bwd", grid=(nj, s // bm),
        in_specs=[pl.BlockSpec((bm, d), lambda j, i: (i, 0)), pl.BlockSpec((bn, d), lambda j, i: (j, 0)), tile, tile,
                  pl.BlockSpec((bm, bn), lambda j, i: (i, gate_col * nj + j)),
                  pl.BlockSpec((bm, bn), lambda j, i: (i, (gate_col + 1) * nj + j)),
                  vec, pl.BlockSpec((1, bn), lambda j, i: (0, nj + j))],
        out_specs=[tile, tile, tile, tile, vec, vec], out_shape=[act, act, act, act, vec_shape, vec_shape],
        compiler_params=_params("parallel", "arbitrary"),
    )(dyo, w_out, pa, pb, z, z, b_gate, b_gate)


def _ffn_in(h, w_fi, bm=512, bn=1408):
    s, d = h.shape
    per = w_fi.shape[2]
    ff = 2 * per
    bm, bn = _pick(s, bm), _pick(per, bn)
    per_blocks = per // bn

    def body(h_ref, wa_ref, wu_ref, a_ref, u_ref, hf_ref):
        hv = h_ref[...]
        a = _dot(hv, wa_ref[...])
        up = _dot(hv, wu_ref[...])
        a_ref[...] = a.astype(BF16)
        u_ref[...] = up.astype(BF16)
        hf_ref[...] = (a * _sigmoid(a) * up).astype(BF16)

    out = pl.BlockSpec((bm, bn), lambda j, i: (i, j))
    shape = jax.ShapeDtypeStruct((s, ff), BF16)
    return _pcall(
        body, name="ffn_in", grid=(ff // bn, s // bm),
        in_specs=[pl.BlockSpec((bm, d), lambda j, i: (i, 0)),
                  pl.BlockSpec((None, d, bn), lambda j, i: (j // per_blocks, 0, j % per_blocks)),
                  pl.BlockSpec((None, d, bn), lambda j, i: (2 + j // per_blocks, 0, j % per_blocks))],
        out_specs=[out, out, out], out_shape=[shape, shape, shape],
        compiler_params=_params("parallel", "parallel"),
    )(h, w_fi, w_fi)


def _ffn_out_bwd(dffn, w_fo, a_act, up_act, bm=512, bn=1408):
    s, d = dffn.shape
    ff = w_fo.shape[0]
    bm, bn = _pick(s, bm), _pick(ff, bn)

    def body(d_ref, w_ref, a_ref, u_ref, da_ref, du_ref):
        dhf = _dot(d_ref[...], w_ref[...], NT)
        a = a_ref[...].astype(F32)
        sa = _sigmoid(a)
        da_ref[...] = (dhf * u_ref[...].astype(F32) * sa * (1.0 + a * (1.0 - sa))).astype(BF16)
        du_ref[...] = (dhf * a * sa).astype(BF16)

    tile = pl.BlockSpec((bm, bn), lambda j, i: (i, j))
    shape = jax.ShapeDtypeStruct((s, ff), BF16)
    return _pcall(
        body, name="ffn_out_bwd", grid=(ff // bn, s // bm),
        in_specs=[pl.BlockSpec((bm, d), lambda j, i: (i, 0)), pl.BlockSpec((bn, d), lambda j, i: (j, 0)), tile, tile],
        out_specs=[tile, tile], out_shape=[shape, shape], compiler_params=_params("parallel", "parallel"),
    )(dffn, w_fo, a_act, up_act)


def _tril(n):
    return lax.broadcasted_iota(jnp.int32, (n, n), 0) >= lax.broadcasted_iota(jnp.int32, (n, n), 1)


def _gmlp_norm(v, ln_g, ln_b):
    gv = _gelu(v)
    mu = jnp.mean(gv, axis=-1, keepdims=True)
    cen = gv - mu
    rs = lax.rsqrt(jnp.mean(cen * cen, axis=-1, keepdims=True) + EPS)
    xh = cen * rs
    return xh, rs, xh * ln_g + ln_b


def _gmlp_fwd(z, ln_g, ln_b, ws, bs_t):
    s = z.shape[0]
    gw = ln_g.shape[1]
    groups, chunk, _ = ws.shape

    def body(u_ref, v_ref, lg_ref, lb_ref, ws_ref, bs_ref, ya_ref):
        gu = _gelu(u_ref[...])
        _, _, vn = _gmlp_norm(v_ref[...], lg_ref[...], lb_ref[...])
        mask = _tril(chunk)
        for g in range(groups):
            cols = slice(g * LANES, (g + 1) * LANES)
            wm = jnp.where(mask, ws_ref[g], 0.0).astype(BF16)
            sg = _dot(wm, vn[:, cols].astype(BF16)) + bs_ref[:, g:g + 1]
            ya_ref[:, cols] = (gu[:, cols] * sg).astype(BF16)

    return _pcall(
        body, name="gmlp_fwd", grid=(s // chunk,),
        in_specs=[_row_spec(chunk, gw, 0), _row_spec(chunk, gw, 1), _vec_spec(gw), _vec_spec(gw),
                  pl.BlockSpec((groups, chunk, chunk), lambda i: (0, 0, 0)), pl.BlockSpec((chunk, LANES), lambda i: (0, 0))],
        out_specs=_row_spec(chunk, gw), out_shape=jax.ShapeDtypeStruct((s, gw), BF16),
        compiler_params=_params("parallel"),
    )(z, z, ln_g, ln_b, ws, bs_t)


def _gmlp_bwd(dya, z, ln_g, ln_b, ws, bs_t):
    s = z.shape[0]
    gw = ln_g.shape[1]
    groups, chunk, _ = ws.shape

    def body(dya_ref, u_ref, v_ref, lg_ref, lb_ref, ws_ref, bs_ref, duv_ref, dws_ref, dbs_ref, dlg_ref, dlb_ref, dvn_ref):
        first = pl.program_id(0) == 0
        u, v, lg = u_ref[...], v_ref[...], lg_ref[...]
        gu = _gelu(u)
        xh, rs, vn = _gmlp_norm(v, lg, lb_ref[...])
        dyav = dya_ref[...]
        mask = _tril(chunk)
        lane = lax.broadcasted_iota(jnp.int32, (chunk, LANES), 1)
        dbs = jnp.zeros((chunk, LANES), F32)
        for g in range(groups):
            cols = slice(g * LANES, (g + 1) * LANES)
            wm = jnp.where(mask, ws_ref[g], 0.0).astype(BF16)
            vg = vn[:, cols].astype(BF16)
            sg = _dot(wm, vg) + bs_ref[:, g:g + 1]
            ds = dyav[:, cols] * gu[:, cols]
            duv_ref[:, cols] = (dyav[:, cols] * sg * _gelu_grad(u[:, cols])).astype(BF16)
            dsb = ds.astype(BF16)
            _accumulate(first, dws_ref.at[g], jnp.where(mask, _dot(dsb, vg, NT), 0.0))
            dbs = dbs + jnp.where(lane == g, jnp.sum(ds, axis=-1, keepdims=True), 0.0)
            dvn_ref[:, cols] = _dot(wm, dsb, TN)
        dvn = dvn_ref[...]
        _accumulate(first, dbs_ref, dbs)
        _accumulate(first, dlb_ref, _colsum(dvn))
        _accumulate(first, dlg_ref, _colsum(dvn * xh))
        dxh = dvn * lg
        dgv = rs * (dxh - jnp.mean(dxh, axis=-1, keepdims=True) - xh * jnp.mean(dxh * xh, axis=-1, keepdims=True))
        duv_ref[:, gw:] = (dgv * _gelu_grad(v)).astype(BF16)

    return _pcall(
        body, name="gmlp_bwd", grid=(s // chunk,),
        in_specs=[_row_spec(chunk, gw), _row_spec(chunk, gw, 0), _row_spec(chunk, gw, 1), _vec_spec(gw), _vec_spec(gw),
                  pl.BlockSpec((groups, chunk, chunk), lambda i: (0, 0, 0)), pl.BlockSpec((chunk, LANES), lambda i: (0, 0))],
        out_specs=[_row_spec(chunk, 2 * gw), pl.BlockSpec((groups, chunk, chunk), lambda i: (0, 0, 0)),
                   pl.BlockSpec((chunk, LANES), lambda i: (0, 0)), _vec_spec(gw), _vec_spec(gw)],
        out_shape=[jax.ShapeDtypeStruct((s, 2 * gw), BF16), jax.ShapeDtypeStruct((groups, chunk, chunk), F32),
                   jax.ShapeDtypeStruct((chunk, LANES), F32), jax.ShapeDtypeStruct((1, gw), F32),
                   jax.ShapeDtypeStruct((1, gw), F32)],
        scratch_shapes=[pltpu.VMEM((chunk, gw), F32)],
        compiler_params=_params("arbitrary"),
    )(dya, z, z, ln_g, ln_b, ws, bs_t)


def _lower_bound(lb_ref):
    a0, a1 = lb_ref[0:1, :], lb_ref[1:2, :]
    mx = jnp.maximum(a0, a1)
    e0, e1 = jnp.exp(a0 - mx), jnp.exp(a1 - mx)
    return e0 / (e0 + e1)


def _sum_dot(mask, x):
    hi = x.astype(BF16)
    rest = x - hi.astype(F32)
    mid = rest.astype(BF16)
    low = (rest - mid.astype(F32)).astype(BF16)
    return _dot(mask, hi) + _dot(mask, mid) + _dot(mask, low)


def _ones_where(mask):
    return jnp.where(mask, 1.0, 0.0).astype(BF16)


def _hg_masks(rows, t):
    r = lax.broadcasted_iota(jnp.int32, (rows, rows), 0)
    c = lax.broadcasted_iota(jnp.int32, (rows, rows), 1)
    same = (r // t) == (c // t)
    incl = jnp.logical_and(same, c <= r)
    upto_mid = jnp.logical_and(same, (c % t) <= t // 2)
    rev = jnp.logical_and(same, c >= r)
    return same, incl, upto_mid, rev


def _hg_block(q, fp, lb, masks):
    rows = q.shape[0]
    same, incl, upto_mid, _ = masks
    sig = _sigmoid(fp)
    f = lb + (1.0 - lb) * sig
    k = 1.0 - f
    sq = _sigmoid(q)
    qa = q * sq
    stacked = jnp.concatenate([_ones_where(m) for m in (incl, same, upto_mid)], axis=0)
    sums = _sum_dot(stacked, jnp.log(f))
    b, b_last, b_mid = sums[:rows], sums[rows:2 * rows], sums[2 * rows:]
    e_q = jnp.exp(jnp.minimum(b - b_mid, EXP_CLAMP))
    e_k = jnp.exp(jnp.minimum(b_mid - b, EXP_CLAMP))
    e_in = jnp.exp(b)
    e_out = jnp.exp(b_last - b)
    return dict(sig=sig, f=f, k=k, sq=sq, qa=qa, e_last=jnp.exp(b_last), e_q=e_q, e_k=e_k, e_in=e_in, e_out=e_out,
                q_hat=(qa * e_q).astype(BF16), k_hat=(k * e_k).astype(BF16),
                q_in=(qa * e_in).astype(BF16), k_out=k * e_out)


def _hgrn_fwd(z, hg_lb, norm_g, q_col):
    s = z.shape[0]
    hw = norm_g.shape[1]
    heads = hw // LANES
    t = HG_CHUNK
    rows = min(HG_ROWS, s)
    per_step = rows // t
    hp = min(HG_HEADS_PER_STEP, heads)
    assert heads % hp == 0 and q_col % hp == 0, (heads, q_col)
    wide = hp * LANES

    def zspec(which):
        return pl.BlockSpec((rows, wide), lambda h, r, which=which: (r, (q_col + which * heads) // hp + h))

    def body(q_ref, f_ref, i_ref, g_ref, lb_ref, ng_ref, yb_ref, o_ref, st_out_ref, st_ref, e_last_ref, inter_ref):
        @pl.when(pl.program_id(1) == 0)
        def _():
            st_ref[...] = jnp.zeros_like(st_ref)

        masks = _hg_masks(rows, t)
        for hh in range(hp):
            cols = slice(hh * LANES, (hh + 1) * LANES)
            blk = _hg_block(q_ref[:, cols], f_ref[:, cols], _lower_bound(lb_ref.at[:, cols]), masks)
            iv = i_ref[:, cols].astype(BF16)
            q_in, k_out = blk["q_in"], blk["k_out"].astype(BF16)
            e_last_ref[hh] = blk["e_last"]
            attn = jnp.where(masks[1], _dot(blk["q_hat"], blk["k_hat"], NT), 0.0).astype(BF16)
            o = _dot(attn, iv)
            grown = [_dot(iv[j * t:(j + 1) * t], k_out[j * t:(j + 1) * t], TN) for j in range(per_step)]
            st = st_ref[hh]
            for j in range(per_step):
                st_out_ref[hh, j] = st
                inter_ref[hh, j * t:(j + 1) * t, :] = _dot(q_in[j * t:(j + 1) * t], st.astype(BF16), NT)
                st = st * e_last_ref[hh, j * t:j * t + 1, :] + grown[j]
            st_ref[hh] = st
            o = o + inter_ref[hh]
            o_ref[:, cols] = o
            og = g_ref[:, cols]
            on, _ = _rms(o)
            yb_ref[:, cols] = (on * ng_ref[:, cols] * (og * _sigmoid(og))).astype(BF16)

    out_row = pl.BlockSpec((rows, wide), lambda h, r: (r, h))
    return _pcall(
        body, name="hgrn_fwd", grid=(heads // hp, s // rows),
        in_specs=[zspec(0), zspec(1), zspec(2), zspec(3),
                  pl.BlockSpec((2, wide), lambda h, r: (0, h)), pl.BlockSpec((1, wide), lambda h, r: (0, h))],
        out_specs=[out_row, out_row, pl.BlockSpec((hp, per_step, LANES, LANES), lambda h, r: (h, r, 0, 0))],
        out_shape=[jax.ShapeDtypeStruct((s, hw), BF16), jax.ShapeDtypeStruct((s, hw), F32),
                   jax.ShapeDtypeStruct((heads, s // t, LANES, LANES), F32)],
        scratch_shapes=[pltpu.VMEM((hp, LANES, LANES), F32), pltpu.VMEM((hp, rows, LANES), F32),
                        pltpu.VMEM((hp, rows, LANES), F32)],
        compiler_params=_params("parallel", "arbitrary"),
    )(z, z, z, z, hg_lb, norm_g)


def _hgrn_bwd(dyb, z, o_raw, states, hg_lb, norm_g, q_col):
    s = z.shape[0]
    hw = norm_g.shape[1]
    heads = hw // LANES
    t = HG_CHUNK
    rows = min(HG_ROWS, s)
    per_step = rows // t
    n_steps = s // rows
    hp = min(HG_HEADS_PER_STEP, heads)
    assert heads % hp == 0 and q_col % hp == 0, (heads, q_col)
    wide = hp * LANES

    def zspec(which):
        return pl.BlockSpec((rows, wide), lambda h, r, which=which: (n_steps - 1 - r, (q_col + which * heads) // hp + h))

    def body(dyb_ref, q_ref, f_ref, i_ref, g_ref, o_ref, st_in_ref, lb_ref, ng_ref,
             dq_ref, df_ref, di_ref, dg_ref, dlb_ref, dng_ref, dst_ref, acc_lb_ref, acc_ng_ref,
             e_last_ref, dq_in_ref, dk_out_ref, di_inter_ref, carry_ref):
        step = pl.program_id(1)

        @pl.when(step == 0)
        def _():
            dst_ref[...] = jnp.zeros_like(dst_ref)
            acc_lb_ref[...] = jnp.zeros_like(acc_lb_ref)
            acc_ng_ref[...] = jnp.zeros_like(acc_ng_ref)

        masks = _hg_masks(rows, t)
        same, incl, _, rev = masks
        sum_mask = jnp.concatenate([_ones_where(rev), _ones_where(same)], axis=1)
        for hh in range(hp):
            cols = slice(hh * LANES, (hh + 1) * LANES)
            lb = _lower_bound(lb_ref.at[:, cols])
            ng = ng_ref[:, cols]
            q = q_ref[:, cols]
            blk = _hg_block(q, f_ref[:, cols], lb, masks)
            iv = i_ref[:, cols].astype(BF16)
            o, og, dy = o_ref[:, cols], g_ref[:, cols], dyb_ref[:, cols]
            so = _sigmoid(og)
            on, r = _rms(o)
            acc_ng_ref[:, cols] += _colsum(dy * on * (og * so))
            dg_ref[:, cols] = (dy * on * ng * so * (1.0 + og * (1.0 - so))).astype(BF16)
            don = dy * ng * (og * so)
            do = (r * (don - on * jnp.mean(don * on, axis=-1, keepdims=True))).astype(BF16)
            q_hat, k_hat, q_in, k_out = blk["q_hat"], blk["k_hat"], blk["q_in"], blk["k_out"]
            k_out_b = k_out.astype(BF16)
            attn = jnp.where(incl, _dot(q_hat, k_hat, NT), 0.0).astype(BF16)
            d_attn = jnp.where(incl, _dot(do, iv, NT), 0.0).astype(BF16)
            di_intra = _dot(attn, do, TN)
            dq_hat = _dot(d_attn, k_hat)
            dk_hat = _dot(d_attn, q_hat, TN)
            e_last_ref[hh] = blk["e_last"]
            grown = [_dot(do[j * t:(j + 1) * t], q_in[j * t:(j + 1) * t], TN) for j in range(per_step)]
            dst = dst_ref[hh]
            for j in reversed(range(per_step)):
                rs_ = slice(j * t, (j + 1) * t)
                e_last = e_last_ref[hh, j * t:j * t + 1, :]
                st_prev, dst_b = st_in_ref[hh, j], dst.astype(BF16)
                dq_in_ref[hh, rs_, :] = _dot(do[rs_], st_prev.astype(BF16))
                dk_out_ref[hh, rs_, :] = _dot(iv[rs_], dst_b)
                di_inter_ref[hh, rs_, :] = _dot(k_out_b[rs_], dst_b, NT)
                carry_ref[hh, rs_, :] = jnp.broadcast_to(e_last * _colsum(st_prev * dst), (t, LANES))
                dst = dst * e_last + grown[j]
            dst_ref[hh] = dst
            di_ref[:, cols] = (di_intra + di_inter_ref[hh]).astype(BF16)
            dk_out = dk_out_ref[hh]
            dqa = dq_in_ref[hh] * blk["e_in"] + dq_hat * blk["e_q"]
            dk = dk_out * blk["e_out"] + dk_hat * blk["e_k"]
            db = blk["qa"] * dqa - blk["k"] * dk
            dlf = _sum_dot(sum_mask, jnp.concatenate([db, dk_out * k_out], axis=0)) + carry_ref[hh]
            dfv = dlf / blk["f"] - dk
            sig, sq = blk["sig"], blk["sq"]
            df_ref[:, cols] = (dfv * (1.0 - lb) * sig * (1.0 - sig)).astype(BF16)
            acc_lb_ref[:, cols] += _colsum(dfv * (1.0 - sig))
            dq_ref[:, cols] = (dqa * sq * (1.0 + q * (1.0 - sq))).astype(BF16)

        @pl.when(step == n_steps - 1)
        def _():
            lb = _lower_bound(lb_ref)
            d0 = acc_lb_ref[...] * lb * (1.0 - lb)
            dlb_ref[0:1, :] = d0
            dlb_ref[1:2, :] = -d0
            dng_ref[...] = acc_ng_ref[...]

    rev_row = pl.BlockSpec((rows, wide), lambda h, r: (n_steps - 1 - r, h))
    piece = jax.ShapeDtypeStruct((s, hw), BF16)
    return _pcall(
        body, name="hgrn_bwd", grid=(heads // hp, n_steps),
        in_specs=[rev_row, zspec(0), zspec(1), zspec(2), zspec(3), rev_row,
                  pl.BlockSpec((hp, per_step, LANES, LANES), lambda h, r: (h, n_steps - 1 - r, 0, 0)),
                  pl.BlockSpec((2, wide), lambda h, r: (0, h)), pl.BlockSpec((1, wide), lambda h, r: (0, h))],
        out_specs=[rev_row, rev_row, rev_row, rev_row,
                   pl.BlockSpec((2, wide), lambda h, r: (0, h)), pl.BlockSpec((1, wide), lambda h, r: (0, h))],
        out_shape=[piece, piece, piece, piece, jax.ShapeDtypeStruct((2, hw), F32), jax.ShapeDtypeStruct((1, hw), F32)],
        scratch_shapes=[pltpu.VMEM((hp, LANES, LANES), F32), pltpu.VMEM((1, wide), F32), pltpu.VMEM((1, wide), F32)]
        + [pltpu.VMEM((hp, rows, LANES), F32)] * 5,
        compiler_params=_params("parallel", "arbitrary"),
    )(dyb, z, z, z, z, o_raw, states, hg_lb, norm_g)


def _adam_update(w, m, v, g):
    m2 = ADAM_B1 * m + (1.0 - ADAM_B1) * g
    v2 = ADAM_B2 * v + (1.0 - ADAM_B2) * (g * g)
    m_hat = m2 * (1.0 / (1.0 - ADAM_B1 ** ADAM_STEP))
    v_hat = v2 * (1.0 / (1.0 - ADAM_B2 ** ADAM_STEP))
    return -ADAM_LR * (m_hat / (jnp.sqrt(v_hat) + ADAM_EPS) + ADAM_WD * w), m2, v2


def _adamw(w, m, v, parts, name, outer=False):
    rows, cols = w.shape
    bc = cols if cols <= 4096 else _pick(cols, 4096)
    rb = _pick_rows(rows, max(8, (384 * 1024) // bc), mult=8)
    if outer and rb % LANES:
        rb = rows

    def body(w_ref, m_ref, v_ref, *refs):
        g_ref, d_ref, mo_ref, vo_ref = refs[-4:]
        if outer:
            cv = refs[0][...]
            g = _dot(cv * _sigmoid(cv), refs[1][...], TN, lax.Precision.HIGHEST)
        else:
            p_ref = refs[0]
            g = p_ref[0].astype(F32)
            for p in range(1, p_ref.shape[0]):
                g = g + p_ref[p].astype(F32)
        g_ref[...] = g
        d_ref[...], mo_ref[...], vo_ref[...] = _adam_update(w_ref[...], m_ref[...], v_ref[...], g)

    blk = pl.BlockSpec((rb, bc), lambda i, j: (i, j))
    out = jax.ShapeDtypeStruct((rows, cols), F32)
    if outer:
        grad_specs = [pl.BlockSpec((N_DEV, rb), lambda i, j: (0, i)), pl.BlockSpec((N_DEV, bc), lambda i, j: (0, j))]
        grad_ins = tuple(parts)
    else:
        grad_specs = [pl.BlockSpec((parts.shape[0], rb, bc), lambda i, j: (0, i, j))]
        grad_ins = (parts,)
    return _pcall(
        body, name=name, grid=(rows // rb, cols // bc), in_specs=[blk, blk, blk] + grad_specs,
        out_specs=[blk] * 4, out_shape=[out] * 4, compiler_params=_params("parallel", "parallel"),
    )(w, m, v, *grad_ins)


SMALL = ("b_ada", "norm1_g", "b_gate", "gmlp_ln_g", "gmlp_ln_b", "gmlp_ws", "gmlp_bs", "hg_lb", "hg_norm_g",
         "norm2_g", "final_norm_g")
BIG = ("w_in", "w_branch_gmlp", "w_branch_hg", "w_out", "w_ffn_in", "w_ffn_out")
WEIGHTS = ("w_ada", "b_ada", "norm1_g", "w_in", "b_gate", "gmlp_ln_g", "gmlp_ln_b", "gmlp_ws", "gmlp_bs", "hg_lb",
           "hg_norm_g", "w_branch_gmlp", "w_branch_hg", "w_out", "norm2_g", "w_ffn_in", "w_ffn_out", "final_norm_g")


def _pack(parts):
    return jnp.concatenate([p.reshape(-1, LANES) for p in parts], axis=0)


def _step(x, c, loss_target, w, m, v):
    s, d = x.shape[1], x.shape[2]
    gw = w["gmlp_ln_g"].shape[-1]
    hw = w["hg_norm_g"].shape[-1]
    x2d, tgt = x[0], loss_target[0]
    mx, my, mc = lax.axis_index("x"), lax.axis_index("y"), lax.axis_index("c")
    chip = 2 * mx + my
    dev = 2 * chip + mc
    q_col = 2 * gw // LANES
    gate_col = (2 * gw + 4 * hw) // d
    place = jnp.stack([chip, mc]).astype(jnp.int32)
    _Order.last = None

    c_all = _all_gather8(c.reshape(-1, LANES), "gather_c").reshape(N_DEV, d)
    gather_in = _WeightGather("in", _WeightGather.zones([w["w_in"][0]], chip), ring=True)
    zones_mix = _WeightGather.zones([w[n][0] for n in ("w_branch_gmlp", "w_branch_hg", "w_out")], chip)
    zones_fi = _WeightGather.zones([w["w_ffn_in"][0]], chip)
    zones_fo = _WeightGather.zones([w["w_ffn_out"][0]], chip)
    _Order.also = (*zones_mix, *zones_fi, *zones_fo)
    n_ada = w["w_ada"].shape[-1]
    b_ada_q = lax.dynamic_slice(w["b_ada"], (0, chip * n_ada), (1, n_ada))
    mod_q = _ada_fwd(c_all, w["w_ada"][0], b_ada_q)
    mod_all = _all_gather8(mod_q, "gather_mod")
    mod = lax.dynamic_index_in_dim(mod_all, dev, axis=1, keepdims=False)[::2].reshape(1, 6 * d)
    sh1, sc1, gt1, sh2, sc2, gt2 = [mod[:, i * d:(i + 1) * d] for i in range(6)]

    gather_in.relay_and_pass_near()
    gather_mix = _WeightGather("mix", zones_mix)
    gather_fi = _WeightGather("fi", zones_fi, ring=True)

    norm1_g, norm2_g, final_g = w["norm1_g"], w["norm2_g"], w["final_norm_g"].reshape(1, d)
    ln_g, ln_b = w["gmlp_ln_g"], w["gmlp_ln_b"]
    ws = w["gmlp_ws"][0]
    groups = ws.shape[0]
    bs_t = jnp.pad(w["gmlp_bs"][0].T, ((0, 0), (0, LANES - groups)))
    hg_lb, hg_ng, b_gate = w["hg_lb"], w["hg_norm_g"], w["b_gate"]

    h1 = _norm_mod_fwd(x2d, norm1_g, sc1, sh1, "norm1_fwd")
    order = jnp.stack([chip, 2 * (1 - mx) + my, 2 * mx + (1 - my), 2 * (1 - mx) + (1 - my)]).astype(jnp.int32)
    w_in, = gather_in.near_done()
    z = _matmul_quarters(h1, w_in, order, 0, 3, "mm_z_near")
    w_in, = gather_in.pass_far()
    z = _matmul_quarters(h1, w_in, order, 3, 1, "mm_z_far", prev=z)
    gather_mix.pass_on()
    ya = _gmlp_fwd(z, ln_g, ln_b, ws, bs_t)
    yb, o_raw, states = _hgrn_fwd(z, hg_lb, hg_ng, q_col)
    gather_fi.relay()
    gather_fo = _WeightGather("fo", zones_fo)
    w_bg, w_bh, w_out = gather_mix.done()
    w_bg, w_bh = [wq.transpose(1, 0, 2).reshape(wq.shape[1], -1) for wq in (w_bg, w_bh)]
    w_out = w_out.reshape(-1, w_out.shape[-1])
    pa, pb, y = _branch_merge(ya, yb, w_bg, w_bh, z, b_gate, gate_col)
    gather_fi.pass_on()
    yo = _matmul(y, w_out, mode="nn", name="mm_yo", out_dtype=F32)
    x1, h2 = _norm_mod_fwd(x2d, norm2_g, sc2, sh2, "norm2_fwd", res=yo, gt=gt1)
    w_fi, = gather_fi.done()
    a_act, up_act, hf = _ffn_in(h2, w_fi)
    gather_fo.pass_on()
    w_fo, = gather_fo.done()
    w_fo = w_fo.reshape(-1, w_fo.shape[-1])
    ffn = _matmul(hf, w_fo, mode="nn", name="mm_ffn", out_dtype=F32)
    dx2, dffn, loss_row, d_final_g, d_gt2 = _final_loss(x1, ffn, gt2, final_g, tgt)

    g_fo = _matmul(hf, dffn, mode="tn", name="mm_g_fo", out_dtype=BF16, bm=1408)
    daup = tuple(_ffn_out_bwd(dffn, w_fo, a_act, up_act))
    g_fi = _matmul(h2, daup, mode="tn", name="mm_g_fi", out_dtype=BF16, out_slots=True, bn=1408)
    red_ffn = _GradReduce("ffn", [g_fo.reshape(N_CHIPS, -1, g_fo.shape[-1]), g_fi])
    dh2 = _matmul(daup, w_fi, mode="nt", name="mm_dh2", out_dtype=F32, b_slots=True)
    red_ffn.step(place)
    dx1, dyo, d_sh2, d_sc2, d_norm2, d_gt1 = _norm_mod_bwd(dh2, x1, dx2, norm2_g, sc2, "norm2_bwd", branch=yo, gt=gt1)
    g_out = _matmul(y, dyo, mode="tn", name="mm_g_out", out_dtype=BF16)
    dpa, dpb, dz_ga, dz_gb, d_b_ga, d_b_gb = _out_bwd(dyo, w_out, pa, pb, z, b_gate, gate_col)
    g_bg = _matmul(ya, dpa, mode="tn", name="mm_g_bg", out_dtype=BF16, out_slots=True)
    g_bh = _matmul(yb, dpb, mode="tn", name="mm_g_bh", out_dtype=BF16, out_slots=True)
    red_mix = _GradReduce("mix", [g_out.reshape(N_CHIPS, -1, g_out.shape[-1]), g_bg, g_bh])
    dya, dyb = _branch_bwd(dpa, dpb, w_bg, w_bh)
    red_mix.step(place)
    dz_uv, d_ws, d_bs_t, d_ln_g, d_ln_b = _gmlp_bwd(dya, z, ln_g, ln_b, ws, bs_t)
    dz_q, dz_f, dz_i, dz_g, d_hg_lb, d_hg_ng = _hgrn_bwd(dyb, z, o_raw, states, hg_lb, hg_ng, q_col)
    dz = jnp.concatenate([dz_uv, dz_q, dz_f, dz_i, dz_g, dz_ga, dz_gb], axis=1)
    grad, delta, new_m, new_v = {}, {}, {}, {}

    def update(n, parts, outer=False):
        outs = _adamw(w[n][0], m[n][0], v[n][0], parts, "adamw_" + n, outer=outer)
        grad[n], delta[n], new_m[n], new_v[n] = [o[None] for o in outs]

    g_in = _matmul(h1, dz, mode="tn", name="mm_g_in", out_dtype=BF16, out_slots=True, bn=1280)
    red_in = _GradReduce("in", [g_in])
    red_ffn.join(place)
    red_mix.join(place)
    red_in.step(place)
    g_fo, g_fi = red_ffn.done()
    n = "w_ffn_in"
    dh1, rode = _matmul(dz, w_in, mode="nt", name="mm_dh1", out_dtype=F32, b_slots=True,
                        rider=([w[n][0], m[n][0], v[n][0], g_fi], lambda *blk: (blk[3], *_adam_update(*blk)), 4))
    grad[n], delta[n], new_m[n], new_v[n] = [o[None] for o in rode]
    grad_x, d_sh1, d_sc1, d_norm1 = _norm_mod_bwd(dh1, x2d, dx1, norm1_g, sc1, "norm1_bwd")
    update("w_ffn_out", g_fo[None])
    g_out, g_bg, g_bh = red_mix.done()
    update("w_out", g_out[None])
    update("w_branch_gmlp", g_bg[None])
    update("w_branch_hg", g_bh[None])

    d_mod = jnp.concatenate([d_sh1, d_sc1, d_gt1, d_sh2, d_sc2, d_gt2], axis=1)
    small_part = {"b_ada": d_mod, "norm1_g": d_norm1, "b_gate": jnp.concatenate([d_b_ga, d_b_gb], axis=1), "gmlp_ln_g": d_ln_g, "gmlp_ln_b": d_ln_b,
                  "gmlp_ws": d_ws, "gmlp_bs": d_bs_t[:, :groups].T, "hg_lb": d_hg_lb, "hg_norm_g": d_hg_ng,
                  "norm2_g": d_norm2, "final_norm_g": d_final_g}
    small_all = _all_gather8(_pack([small_part[n] for n in SMALL]), "gather_small")
    d_mod_all = small_all[:, :6 * d // LANES].reshape(N_DEV, 6 * d)
    d_mod_q = lax.dynamic_slice(d_mod_all, (0, chip * n_ada), (N_DEV, n_ada))
    red_in.join(place)
    update("w_ada", (c_all, d_mod_q), outer=True)
    outs = _adamw(_pack([w[n] for n in SMALL]), _pack([m[n] for n in SMALL]), _pack([v[n] for n in SMALL]),
                  small_all, "adamw_small")
    update("w_in", red_in.done()[0][None])
    row = 0
    for n in SMALL:
        cnt = w[n].size // LANES
        for dst, o in zip((grad, delta, new_m, new_v), outs):
            dst[n] = o[row:row + cnt].reshape(w[n].shape)
        row += cnt

    loss = lax.psum(loss_row[0, 0], ("x", "y", "c"))
    return (loss, grad_x[None], *[grad[n] for n in WEIGHTS], *[delta[n] for n in WEIGHTS],
            *[new_m[n] for n in WEIGHTS], *[new_v[n] for n in WEIGHTS])


def kernel(x, c, w_ada, b_ada, norm1_g, w_in, b_gate, gmlp_ln_g, gmlp_ln_b, gmlp_ws, gmlp_bs, hg_lb, hg_norm_g, w_branch_gmlp, w_branch_hg, w_out, norm2_g, w_ffn_in, w_ffn_out, final_norm_g, loss_target, m_w_ada, m_b_ada, m_norm1_g, m_w_in, m_b_gate, m_gmlp_ln_g, m_gmlp_ln_b, m_gmlp_ws, m_gmlp_bs, m_hg_lb, m_hg_norm_g, m_w_branch_gmlp, m_w_branch_hg, m_w_out, m_norm2_g, m_w_ffn_in, m_w_ffn_out, m_final_norm_g, v_w_ada, v_b_ada, v_norm1_g, v_w_in, v_b_gate, v_gmlp_ln_g, v_gmlp_ln_b, v_gmlp_ws, v_gmlp_bs, v_hg_lb, v_hg_norm_g, v_w_branch_gmlp, v_w_branch_hg, v_w_out, v_norm2_g, v_w_ffn_in, v_w_ffn_out, v_final_norm_g):
    w = dict(w_ada=w_ada, b_ada=b_ada, norm1_g=norm1_g, w_in=w_in, b_gate=b_gate, gmlp_ln_g=gmlp_ln_g,
             gmlp_ln_b=gmlp_ln_b, gmlp_ws=gmlp_ws, gmlp_bs=gmlp_bs, hg_lb=hg_lb, hg_norm_g=hg_norm_g,
             w_branch_gmlp=w_branch_gmlp, w_branch_hg=w_branch_hg, w_out=w_out, norm2_g=norm2_g,
             w_ffn_in=w_ffn_in, w_ffn_out=w_ffn_out, final_norm_g=final_norm_g)
    m = dict(w_ada=m_w_ada, b_ada=m_b_ada, norm1_g=m_norm1_g, w_in=m_w_in, b_gate=m_b_gate, gmlp_ln_g=m_gmlp_ln_g,
             gmlp_ln_b=m_gmlp_ln_b, gmlp_ws=m_gmlp_ws, gmlp_bs=m_gmlp_bs, hg_lb=m_hg_lb, hg_norm_g=m_hg_norm_g,
             w_branch_gmlp=m_w_branch_gmlp, w_branch_hg=m_w_branch_hg, w_out=m_w_out, norm2_g=m_norm2_g,
             w_ffn_in=m_w_ffn_in, w_ffn_out=m_w_ffn_out, final_norm_g=m_final_norm_g)
    v = dict(w_ada=v_w_ada, b_ada=v_b_ada, norm1_g=v_norm1_g, w_in=v_w_in, b_gate=v_b_gate, gmlp_ln_g=v_gmlp_ln_g,
             gmlp_ln_b=v_gmlp_ln_b, gmlp_ws=v_gmlp_ws, gmlp_bs=v_gmlp_bs, hg_lb=v_hg_lb, hg_norm_g=v_hg_norm_g,
             w_branch_gmlp=v_w_branch_gmlp, w_branch_hg=v_w_branch_hg, w_out=v_w_out, norm2_g=v_norm2_g,
             w_ffn_in=v_w_ffn_in, w_ffn_out=v_w_ffn_out, final_norm_g=v_final_norm_g)
    return _step(x, c, loss_target, w, m, v)
```

```python
import functools

import jax
import jax.numpy as jnp
from jax import lax
from jax.experimental import pallas as pl
from jax.experimental.pallas import tpu as pltpu

F32 = jnp.float32
BF16 = jnp.bfloat16
EPS = 1e-6
LANES = 128
N_CHIPS = 4
N_DEV = 8
VMEM_LIMIT_BYTES = 56 * 1024 * 1024
VMEM_RIDER_LIMIT_BYTES = 60 * 1024 * 1024
HG_CHUNK = 32
HG_ROWS = 256
HG_HEADS_PER_STEP = 8
EXP_CLAMP = 80.0
ADAM_LR, ADAM_B1, ADAM_B2, ADAM_EPS, ADAM_WD, ADAM_STEP = 0.001, 0.9, 0.999, 1e-08, 0.01, 10
MESH = pl.DeviceIdType.MESH

NN = (((1,), (0,)), ((), ()))
NT = (((1,), (1,)), ((), ()))
TN = (((0,), (0,)), ((), ()))


def _dot(a, b, dims=NN, precision=None):
    return lax.dot_general(a, b, dims, precision=precision, preferred_element_type=F32)


def _params(*semantics, vmem_limit_bytes=VMEM_LIMIT_BYTES):
    return pltpu.CompilerParams(dimension_semantics=semantics, vmem_limit_bytes=vmem_limit_bytes)


class _Order:
    last = None
    also = ()


def _pcall(body, *, in_specs, out_specs, grid=(), scratch_shapes=(), num_scalar_prefetch=0, **kw):
    def run(*ins):
        deps = (() if _Order.last is None else (_Order.last,)) + tuple(_Order.also)
        _Order.also = ()
        n_in, n_dep = len(ins), len(deps)

        def wrapped(*refs):
            body(*refs[:n_in], *refs[n_in + n_dep:])

        specs = list(in_specs) + [pl.BlockSpec(memory_space=pl.ANY)] * n_dep
        if num_scalar_prefetch:
            grid_spec = pltpu.PrefetchScalarGridSpec(
                num_scalar_prefetch=num_scalar_prefetch, grid=grid, in_specs=specs, out_specs=out_specs,
                scratch_shapes=scratch_shapes)
            outs = pl.pallas_call(wrapped, grid_spec=grid_spec, **kw)(*ins, *deps)
        else:
            outs = pl.pallas_call(wrapped, grid=grid, in_specs=specs, out_specs=out_specs,
                                  scratch_shapes=scratch_shapes, **kw)(*ins, *deps)
        _Order.last = jax.tree.leaves(outs)[0]
        return outs

    return run


def _pick_rows(dim, pref, mult=16):
    best = None
    for cand in range(mult, min(dim, pref) + 1, mult):
        if dim % cand == 0:
            best = cand
    assert best is not None, (dim, pref)
    return best


def _pick(dim, pref):
    if dim <= pref:
        return dim
    best = None
    for cand in range(LANES, pref + 1, LANES):
        if dim % cand == 0:
            best = cand
    assert best is not None, (dim, pref)
    return best


def _sigmoid(x):
    return 1.0 / (1.0 + jnp.exp(-x))


def _gelu(x):
    c = 0.7978845608028654
    return 0.5 * x * (1.0 + jnp.tanh(c * (x + 0.044715 * x * x * x)))


def _gelu_grad(x):
    c = 0.7978845608028654
    t = jnp.tanh(c * (x + 0.044715 * x * x * x))
    return 0.5 * (1.0 + t) + 0.5 * x * (1.0 - t * t) * c * (1.0 + 3.0 * 0.044715 * x * x)


def _rms(x):
    r = lax.rsqrt(jnp.mean(x * x, axis=-1, keepdims=True) + EPS)
    return x * r, r


def _colsum(x):
    return jnp.sum(x, axis=0, keepdims=True)


def _accumulate(first, ref, val):
    @pl.when(first)
    def _():
        ref[...] = val

    @pl.when(jnp.logical_not(first))
    def _():
        ref[...] += val


def _matmul(a, b, *, mode, name, out_dtype, b_slots=False, out_slots=False, bm=1024, bn=1024, bk=2816, rider=None):
    pair = a if isinstance(a, tuple) else b if isinstance(b, tuple) else None
    if mode == "nn":
        m, k = a.shape
        n = b.shape[2] * N_CHIPS if b_slots else b.shape[1]
        per = b.shape[2] if b_slots else n
    elif mode == "nt":
        m, k = (a[0].shape[0], 2 * a[0].shape[1]) if pair else a.shape
        n = b.shape[1] if b_slots else b.shape[0]
        per = b.shape[2] if b_slots else k
    else:
        k, m = a.shape
        n = 2 * b[0].shape[1] if pair else b.shape[1]
        per = n // N_CHIPS if out_slots else n
    bm = _pick(m, bm)
    if mode == "nt":
        bn, bk = _pick(n, bn), _pick(per, bk)
    else:
        bn, bk = _pick(per, bn), _pick(k, bk)
    nk = k // bk
    per_blocks = per // (bk if mode == "nt" else bn)
    dims = {"nn": NN, "nt": NT, "tn": TN}[mode]
    half = (nk if mode == "nt" else n // bn) // 2

    def product(a_ref, b_ref, o_ref, acc):
        part = _dot(a_ref[...], b_ref[...], dims)
        if nk == 1:
            o_ref[...] = part.astype(o_ref.dtype)
            return
        acc_ref, kk = acc[0], pl.program_id(2)

        @pl.when(kk == 0)
        def _():
            acc_ref[...] = part

        @pl.when(jnp.logical_and(kk > 0, kk < nk - 1))
        def _():
            acc_ref[...] += part

        @pl.when(kk == nk - 1)
        def _():
            o_ref[...] = (acc_ref[...] + part).astype(o_ref.dtype)

    n_ride, n_rode = (len(rider[0]), rider[2]) if rider else (0, 0)

    def body(*refs):
        if rider:
            ride_in, ride_out = refs[2:2 + n_ride], refs[3 + n_ride:3 + n_ride + n_rode]

            def ride(t, carry):
                rows = pl.ds(pl.multiple_of(t * 8, 8), 8)
                for ref, val in zip(ride_out, rider[1](*[r[rows, :] for r in ride_in])):
                    ref[rows, :] = val
                return carry

            lax.fori_loop(0, ride_in[0].shape[0] // 8, ride, 0)
        if not pair:
            return product(refs[0], refs[1], refs[2 + n_ride], refs[3 + n_ride + n_rode:])
        first = pl.program_id(2 if mode == "nt" else 1) < half
        x1, x2, y, o_ref, acc = refs[0], refs[1], refs[2], refs[3], refs[4:]

        @pl.when(first)
        def _():
            product(*((x1, y) if mode == "nt" else (y, x1)), o_ref, acc)

        @pl.when(jnp.logical_not(first))
        def _():
            product(*((x2, y) if mode == "nt" else (y, x2)), o_ref, acc)

    if mode == "nn":
        a_spec = pl.BlockSpec((bm, bk), lambda i, j, kk: (i, kk))
        if b_slots:
            b_spec = pl.BlockSpec((None, bk, bn), lambda i, j, kk: (j // per_blocks, kk, j % per_blocks))
        else:
            b_spec = pl.BlockSpec((bk, bn), lambda i, j, kk: (kk, j))
    elif mode == "nt":
        a_spec = pl.BlockSpec((bm, bk), lambda i, j, kk: (i, kk))
        if b_slots:
            b_spec = pl.BlockSpec((None, bn, bk), lambda i, j, kk: (kk // per_blocks, j, kk % per_blocks))
        else:
            b_spec = pl.BlockSpec((bn, bk), lambda i, j, kk: (j, kk))
    else:
        a_spec = pl.BlockSpec((bk, bm), lambda i, j, kk: (kk, i))
        b_spec = pl.BlockSpec((bk, bn), lambda i, j, kk: (kk, j))
    if out_slots:
        o_spec = pl.BlockSpec((None, bm, bn), lambda i, j, kk: (j // per_blocks, i, j % per_blocks))
        out_shape = jax.ShapeDtypeStruct((N_CHIPS, m, per), out_dtype)
    else:
        o_spec = pl.BlockSpec((bm, bn), lambda i, j, kk: (i, j))
        out_shape = jax.ShapeDtypeStruct((m, n), out_dtype)
    if not pair:
        ins, in_specs = (a, b), [a_spec, b_spec]
    elif mode == "nt":
        ins = (*a, b)
        in_specs = [pl.BlockSpec((bm, bk), lambda i, j, kk: (i, jnp.minimum(kk, half - 1))),
                    pl.BlockSpec((bm, bk), lambda i, j, kk: (i, jnp.maximum(kk - half, 0))), b_spec]
    else:
        ins = (*b, a)
        in_specs = [pl.BlockSpec((bk, bn), lambda i, j, kk: (kk, jnp.minimum(j, half - 1))),
                    pl.BlockSpec((bk, bn), lambda i, j, kk: (kk, jnp.maximum(j - half, 0))), a_spec]
    scratch = [pltpu.VMEM((bm, bn), F32)] if nk > 1 else []
    if not rider:
        return _pcall(
            body, name=name, grid=(m // bm, n // bn, nk), in_specs=in_specs, out_specs=o_spec, out_shape=out_shape,
            scratch_shapes=scratch, compiler_params=_params("parallel", "parallel", "arbitrary"),
        )(*ins)
    assert not pair
    rows, cols = rider[0][0].shape
    nj = n // bn
    rb = rows // ((m // bm) * nj * nk)
    assert rows == rb * (m // bm) * nj * nk and rb % 8 == 0, (rows, rb)
    ride_spec = pl.BlockSpec((rb, cols), lambda i, j, kk: ((i * nj + j) * nk + kk, 0))
    outs = _pcall(
        body, name=name, grid=(m // bm, nj, nk), in_specs=in_specs + [ride_spec] * n_ride,
        out_specs=[o_spec] + [ride_spec] * n_rode,
        out_shape=[out_shape] + [jax.ShapeDtypeStruct((rows, cols), F32)] * n_rode,
        scratch_shapes=scratch,
        compiler_params=_params("arbitrary", "arbitrary", "arbitrary", vmem_limit_bytes=VMEM_RIDER_LIMIT_BYTES),
    )(*ins, *rider[0])
    return outs[0], outs[1:]


def _matmul_quarters(h, w_slots, order, first, count, name, prev=None, bm=1024, bn=1280):
    s, k = h.shape
    n = w_slots.shape[2]
    bm, bn = _pick(s, bm), _pick(n, bn)
    pb = n // bn

    def body(order_ref, h_ref, w_ref, *rest):
        rest[-1][...] = _dot(h_ref[...], w_ref[...])

    ins = (order, h, w_slots) + (() if prev is None else (prev,))
    return _pcall(
        body, name=name, num_scalar_prefetch=1, grid=(count * pb, s // bm),
        in_specs=[pl.BlockSpec((bm, k), lambda j, i, o: (i, 0)),
                  pl.BlockSpec((None, k, bn), lambda j, i, o: (o[first + j // pb], 0, j % pb))]
        + ([] if prev is None else [ANY_SPEC]),
        out_specs=pl.BlockSpec((bm, bn), lambda j, i, o: (i, o[first + j // pb] * pb + j % pb)),
        out_shape=jax.ShapeDtypeStruct((s, N_CHIPS * n), F32),
        input_output_aliases={} if prev is None else {3: 0},
        compiler_params=_params("arbitrary", "arbitrary"),
    )(*ins)


def _place():
    x, y, c = lax.axis_index("x"), lax.axis_index("y"), lax.axis_index("c")
    chips = [(1 - x, y), (x, 1 - y), (1 - x, 1 - y)]
    return x, y, c, chips


def _all_gather8(block, name):
    def body(x_ref, out_ref, send_sems, recv_sems, local_sem):
        x, y, c, chips = _place()
        me, sibling = (x, y, c), (x, y, 1 - c)

        def slot(px, py, pc):
            return out_ref.at[4 * px + 2 * py + pc]

        def copy(k, blk, to, src=None):
            return pltpu.make_async_remote_copy(
                src_ref=slot(*blk) if src is None else src, dst_ref=slot(*blk),
                send_sem=send_sems.at[k], recv_sem=recv_sems.at[k], device_id=to, device_id_type=MESH)

        mine = pltpu.make_async_copy(x_ref, slot(*me), local_sem)
        mine.start()
        first = [copy(0, me, sibling, src=x_ref)]
        first += [copy(1 + j, me, (*chip, c), src=x_ref) for j, chip in enumerate(chips)]
        for cp in first:
            cp.start()
        passed = [copy(4 + j, (*chip, c), sibling) for j, chip in enumerate(chips)]
        for j, chip in enumerate(chips):
            copy(1 + j, (*chip, c), me).wait_recv()
            passed[j].start()
        copy(0, sibling, me).wait_recv()
        for j, chip in enumerate(chips):
            copy(4 + j, (*chip, 1 - c), me).wait_recv()
        for cp in first + passed:
            cp.wait_send()
        mine.wait()

    return _pcall(
        body, name=name, out_shape=jax.ShapeDtypeStruct((N_DEV,) + block.shape, block.dtype),
        in_specs=[pl.BlockSpec(memory_space=pltpu.VMEM)], out_specs=pl.BlockSpec(memory_space=pltpu.VMEM),
        scratch_shapes=[pltpu.SemaphoreType.DMA((7,)), pltpu.SemaphoreType.DMA((7,)), pltpu.SemaphoreType.DMA],
        compiler_params=pltpu.CompilerParams(vmem_limit_bytes=VMEM_LIMIT_BYTES),
    )(block)


HBM_SPEC = pl.BlockSpec(memory_space=pltpu.HBM)
SEM_SPEC = pl.BlockSpec(memory_space=pltpu.SEMAPHORE)
ANY_SPEC = pl.BlockSpec(memory_space=pl.ANY)
EFFECT = pltpu.SideEffectType.DATAFLOW_SIDE_EFFECTING


def _xfer_start(name, bufs, plan, n_copies, after_last=False):
    nb = len(bufs)
    deps = (_Order.last,) if after_last and _Order.last is not None else ()
    nd = len(deps)

    def body(*refs):
        send_sems, recv_sems = refs[nb + nd], refs[nb + nd + 1]
        token = refs[nb + nd + 2 + nb]
        for k, (src, dst, dev) in enumerate(plan(refs[:nb], *_place())):
            pltpu.make_async_remote_copy(src_ref=src, dst_ref=dst, send_sem=send_sems.at[k], recv_sem=recv_sems.at[k],
                                         device_id=dev, device_id_type=MESH).start()
        token[...] = jnp.zeros_like(token)

    outs = pl.pallas_call(
        body, name=name,
        out_shape=(pltpu.SemaphoreType.DMA((n_copies,)), pltpu.SemaphoreType.DMA((n_copies,)),
                   *[pltpu.HBM(b.shape, b.dtype) for b in bufs], jax.ShapeDtypeStruct((8, LANES), F32)),
        in_specs=[HBM_SPEC] * nb + [ANY_SPEC] * nd,
        out_specs=(SEM_SPEC, SEM_SPEC, *[HBM_SPEC] * nb, pl.BlockSpec(memory_space=pltpu.VMEM)),
        input_output_aliases={i: 2 + i for i in range(nb)},
        compiler_params=pltpu.CompilerParams(has_side_effects=EFFECT),
    )(*[pltpu.with_memory_space_constraint(b, pltpu.HBM) for b in bufs], *deps)
    _Order.last = outs[-1]
    return (outs[0], outs[1]), list(outs[2:2 + nb])


def _xfer_wait(name, sems, bufs, plan):
    nb = len(bufs)

    def body(*refs):
        send_sems, recv_sems = refs[nb], refs[nb + 1]
        for k, (src, dst, dev) in enumerate(plan(refs[:nb], *_place())):
            copy = pltpu.make_async_remote_copy(src_ref=src, dst_ref=dst, send_sem=send_sems.at[k],
                                                recv_sem=recv_sems.at[k], device_id=dev, device_id_type=MESH)
            copy.wait_send()
            copy.wait_recv()

    outs = pl.pallas_call(
        body, name=name, out_shape=tuple(pltpu.HBM(b.shape, b.dtype) for b in bufs),
        in_specs=[HBM_SPEC] * nb + [SEM_SPEC, SEM_SPEC, ANY_SPEC], out_specs=tuple([HBM_SPEC] * nb),
        input_output_aliases={i: i for i in range(nb)},
        compiler_params=pltpu.CompilerParams(has_side_effects=EFFECT),
    )(*bufs, *sems, _Order.last)
    _Order.last = outs[0]
    return list(outs)


def _xfer_hand_over(name, sems, bufs, plan, next_plans):
    nb, n_next = len(bufs), len(next_plans)

    def body(*refs):
        send_sems, recv_sems = refs[nb], refs[nb + 1]
        outs = refs[nb + 3:]
        place = _place()
        for k, (src, dst, dev) in enumerate(plan(refs[:nb], *place)):
            copy = pltpu.make_async_remote_copy(src_ref=src, dst_ref=dst, send_sem=send_sems.at[k],
                                                recv_sem=recv_sems.at[k], device_id=dev, device_id_type=MESH)
            copy.wait_send()
            copy.wait_recv()
        for p, (next_plan, _) in enumerate(next_plans):
            for k, (src, dst, dev) in enumerate(next_plan(refs[:nb], *place)):
                pltpu.make_async_remote_copy(src_ref=src, dst_ref=dst, send_sem=outs[2 * p].at[k],
                                             recv_sem=outs[2 * p + 1].at[k], device_id=dev, device_id_type=MESH).start()
        outs[-1][...] = jnp.zeros_like(outs[-1])

    sem_shapes = [pltpu.SemaphoreType.DMA((copies,)) for _, copies in next_plans for _ in range(2)]
    outs = pl.pallas_call(
        body, name=name,
        out_shape=(*sem_shapes, *[pltpu.HBM(b.shape, b.dtype) for b in bufs], jax.ShapeDtypeStruct((8, LANES), F32)),
        in_specs=[HBM_SPEC] * nb + [SEM_SPEC, SEM_SPEC, ANY_SPEC],
        out_specs=(*[SEM_SPEC] * (2 * n_next), *[HBM_SPEC] * nb, pl.BlockSpec(memory_space=pltpu.VMEM)),
        input_output_aliases={i: 2 * n_next + i for i in range(nb)},
        compiler_params=pltpu.CompilerParams(has_side_effects=EFFECT),
    )(*bufs, *sems, _Order.last)
    _Order.last = outs[-1]
    return [(outs[2 * p], outs[2 * p + 1]) for p in range(n_next)], list(outs[2 * n_next:2 * n_next + nb])


def _half(ref, c, axis):
    rows = ref.shape[axis] // 2
    return pl.ds(c * rows, rows)


def _plan_weights_ici(n):
    def plan(refs, x, y, c, chips):
        out = []
        for w in range(n):
            region = refs[w].at[2 * x + y, _half(refs[w], c, 1), :]
            out += [(region, region, (*chip, c)) for chip in chips]
        return out
    return plan


def _plan_weights_ring(n):
    def plan(refs, x, y, c, chips):
        out = []
        for w in range(n):
            region = refs[w].at[2 * x + y, _half(refs[w], c, 1), :]
            out += [(region, region, (*chip, c)) for chip in chips[:2]]
        return out
    return plan


def _plan_weights_relay(n):
    def plan(refs, x, y, c, chips):
        out = []
        for w in range(n):
            quarter_rows = refs[w].shape[1] // 4
            upper = refs[w].at[2 * x + (1 - y), pl.ds(2 * c * quarter_rows, quarter_rows), :]
            lower = refs[w].at[2 * (1 - x) + y, pl.ds((2 * c + 1) * quarter_rows, quarter_rows), :]
            out += [(upper, upper, (1 - x, y, c)), (lower, lower, (x, 1 - y, c))]
        return out
    return plan


def _plan_weights_d2d(n, which=slice(0, 3)):
    def plan(refs, x, y, c, chips):
        out = []
        for w in range(n):
            rows = _half(refs[w], c, 1)
            for chip in chips[which]:
                region = refs[w].at[2 * chip[0] + chip[1], rows, :]
                out.append((region, region, (x, y, 1 - c)))
        return out
    return plan


def _plan_grads_d2d(n):
    def plan(refs, x, y, c, chips):
        return [(refs[w].at[:, _half(refs[w], 1 - c, 1), :], refs[n + w], (x, y, 1 - c)) for w in range(n)]
    return plan


def _plan_grads_ici(n):
    def plan(refs, x, y, c, chips):
        out = []
        for w in range(n):
            out += [(refs[w].at[2 * chip[0] + chip[1]], refs[n + w].at[2 * x + y], (*chip, c)) for chip in chips]
        return out
    return plan


def _plan_final_d2d(n):
    def plan(refs, x, y, c, chips):
        out = []
        for w in range(n):
            region = refs[w].at[_half(refs[w], c, 0), :]
            out.append((region, region, (x, y, 1 - c)))
        return out
    return plan


def _stream_blocks(hr, cols):
    bc = cols if cols <= 4096 else _pick(cols, 4096)
    return _pick_rows(hr, max(16, (768 * 1024) // bc)), bc


def _pre_reduce(g, landed, place, name):
    _, rows, cols = g.shape
    hr = rows // 2
    rb, bc = _stream_blocks(hr, cols)
    nrb = hr // rb

    def body(place_ref, g_ref, l_ref, o_ref):
        o_ref[...] = (g_ref[...].astype(F32) + l_ref[...].astype(F32)).astype(o_ref.dtype)

    return _pcall(
        body, name=name, num_scalar_prefetch=1, grid=(N_CHIPS, nrb, cols // bc),
        in_specs=[pl.BlockSpec((None, rb, bc), lambda j, i, k, p: (j, p[1] * nrb + i, k)),
                  pl.BlockSpec((None, rb, bc), lambda j, i, k, p: (j, i, k))],
        out_specs=pl.BlockSpec((None, rb, bc), lambda j, i, k, p: (j, i, k)),
        out_shape=jax.ShapeDtypeStruct((N_CHIPS, hr, cols), g.dtype),
        compiler_params=_params("parallel", "parallel", "parallel"),
    )(place, g, landed)


def _sum_slots(mine, landed, place, name):
    _, hr, cols = mine.shape
    rb, bc = _stream_blocks(hr, cols)
    rb = _pick_rows(hr, max(16, rb // 2))
    nrb = hr // rb

    def body(place_ref, m_ref, l_ref, o_ref):
        chip = place_ref[0]
        own = m_ref[...].astype(F32)
        total = jnp.where(chip == 0, own, l_ref[0].astype(F32))
        for j in range(1, N_CHIPS):
            total = total + jnp.where(chip == j, own, l_ref[j].astype(F32))
        o_ref[...] = total

    return _pcall(
        body, name=name, num_scalar_prefetch=1, grid=(nrb, cols // bc),
        in_specs=[pl.BlockSpec((None, rb, bc), lambda i, k, p: (p[0], i, k)),
                  pl.BlockSpec((N_CHIPS, rb, bc), lambda i, k, p: (0, i, k))],
        out_specs=pl.BlockSpec((rb, bc), lambda i, k, p: (p[1] * nrb + i, k)),
        out_shape=jax.ShapeDtypeStruct((2 * hr, cols), F32),
        compiler_params=_params("parallel", "parallel"),
    )(place, mine, landed)


class _WeightGather:
    @staticmethod
    def zones(quarters, chip):
        return [lax.dynamic_update_slice(lax.empty((N_CHIPS,) + q.shape, BF16), q.astype(BF16)[None], (chip, 0, 0))
                for q in quarters]

    def __init__(self, tag, zones, ring=False):
        self.tag, self.n, self.ring = tag, len(zones), ring
        self.plan = _plan_weights_ring(self.n) if ring else _plan_weights_ici(self.n)
        self.sems, self.bufs = _xfer_start("wici_start_" + tag, zones, self.plan, (2 if ring else 3) * self.n,
                                           after_last=True)

    def relay(self):
        plan = _plan_weights_relay(self.n)
        (self.sems,), self.bufs = _xfer_hand_over("wrel_start_" + self.tag, self.sems, self.bufs, self.plan,
                                                   [(plan, 2 * self.n)])
        self.plan = plan

    def pass_on(self):
        plan = _plan_weights_d2d(self.n)
        (self.sems,), self.bufs = _xfer_hand_over("wd2d_start_" + self.tag, self.sems, self.bufs, self.plan,
                                                   [(plan, 3 * self.n)])
        self.plan = plan

    def done(self):
        return _xfer_wait("wd2d_wait_" + self.tag, self.sems, self.bufs, self.plan)

    def relay_and_pass_near(self):
        plan, self.near_plan = _plan_weights_relay(self.n), _plan_weights_d2d(self.n, slice(0, 2))
        (self.sems, self.near_sems), self.bufs = _xfer_hand_over(
            "wrel_start_" + self.tag, self.sems, self.bufs, self.plan, [(plan, 2 * self.n), (self.near_plan, 2 * self.n)])
        self.plan = plan

    def near_done(self):
        self.bufs = _xfer_wait("wnear_wait_" + self.tag, self.near_sems, self.bufs, self.near_plan)
        return self.bufs

    def pass_far(self):
        plan = _plan_weights_d2d(self.n, slice(2, 3))
        (sems,), bufs = _xfer_hand_over("wfar_start_" + self.tag, self.sems, self.bufs, self.plan, [(plan, self.n)])
        return _xfer_wait("wfar_wait_" + self.tag, sems, bufs, plan)


class _GradReduce:
    def __init__(self, tag, grads):
        self.tag, self.n = tag, len(grads)
        zones = [lax.empty((N_CHIPS, g.shape[1] // 2, g.shape[2]), g.dtype) for g in grads]
        self.plan = _plan_grads_d2d(self.n)
        self.sems, self.bufs = _xfer_start("gd2d_start_" + tag, list(grads) + zones, self.plan, self.n)

    def pair(self, place):
        n = self.n
        bufs = _xfer_wait("gd2d_wait_" + self.tag, self.sems, self.bufs, self.plan)
        self.halves = [_pre_reduce(bufs[w], bufs[n + w], place, f"pre_reduce_{self.tag}{w}") for w in range(n)]

    def cross(self, after_last=False):
        zones = [lax.empty(h.shape, h.dtype) for h in self.halves]
        self.plan = _plan_grads_ici(self.n)
        self.sems, self.bufs = _xfer_start("gici_start_" + self.tag, self.halves + zones, self.plan, 3 * self.n,
                                           after_last=after_last)

    def step(self, place):
        self.pair(place)
        self.cross()

    def join(self, place):
        n = self.n
        bufs = _xfer_wait("gici_wait_" + self.tag, self.sems, self.bufs, self.plan)
        sums = [_sum_slots(bufs[w], bufs[n + w], place, f"sum_slots_{self.tag}{w}") for w in range(n)]
        self.plan = _plan_final_d2d(n)
        self.sems, self.bufs = _xfer_start("gfin_start_" + self.tag, sums, self.plan, n)

    def done(self):
        return _xfer_wait("gfin_wait_" + self.tag, self.sems, self.bufs, self.plan)


def _ada_fwd(c_all, w_q, b_q):
    d, n = w_q.shape
    bn = _pick(n, 512)

    def body(c_ref, w_ref, b_ref, o_ref):
        cv = c_ref[...]
        act = cv * _sigmoid(cv)
        o_ref[...] = _dot(act, w_ref[...], NN, lax.Precision.HIGHEST) + b_ref[...]

    return _pcall(
        body, name="ada_fwd", grid=(n // bn,),
        in_specs=[pl.BlockSpec((N_DEV, d), lambda j: (0, 0)), pl.BlockSpec((d, bn), lambda j: (0, j)),
                  pl.BlockSpec((1, bn), lambda j: (0, j))],
        out_specs=pl.BlockSpec((N_DEV, bn), lambda j: (0, j)),
        out_shape=jax.ShapeDtypeStruct((N_DEV, n), F32), compiler_params=_params("parallel"),
    )(c_all, w_q, b_q)


def _row_spec(rb, width, col=0):
    return pl.BlockSpec((rb, width), lambda i, col=col: (i, col))


def _vec_spec(width, col=0):
    return pl.BlockSpec((1, width), lambda i, col=col: (0, col))


def _norm_mod_fwd(x, g, sc, sh, name, res=None, gt=None):
    s, d = x.shape
    rb = _pick(s, 256)
    has_res = res is not None

    def body(*refs):
        if has_res:
            x_ref, res_ref, gt_ref, g_ref, sc_ref, sh_ref, x1_ref, h_ref = refs
            xv = x_ref[...] + gt_ref[...] * res_ref[...]
            x1_ref[...] = xv
        else:
            x_ref, g_ref, sc_ref, sh_ref, h_ref = refs
            xv = x_ref[...]
        xh, _ = _rms(xv)
        h_ref[...] = (xh * g_ref[...] * (1.0 + sc_ref[...]) + sh_ref[...]).astype(BF16)

    row, vec = _row_spec(rb, d), _vec_spec(d)
    if has_res:
        ins, in_specs = (x, res, gt, g, sc, sh), [row, row, vec, vec, vec, vec]
        out_shape = [jax.ShapeDtypeStruct((s, d), F32), jax.ShapeDtypeStruct((s, d), BF16)]
        out_specs = [row, row]
    else:
        ins, in_specs = (x, g, sc, sh), [row, vec, vec, vec]
        out_shape, out_specs = jax.ShapeDtypeStruct((s, d), BF16), row
    return _pcall(body, name=name, grid=(s // rb,), in_specs=in_specs, out_specs=out_specs,
                          out_shape=out_shape, compiler_params=_params("parallel"))(*ins)


def _ffn_out_loss(hf, w_fo, x1, gt2, g_final, target, bm=256, bk=2816):
    s, ff = hf.shape
    d = w_fo.shape[1]
    bm, bk = _pick(s, bm), _pick(ff, bk)
    nk = ff // bk

    def body(hf_ref, w_ref, x1_ref, gt_ref, g_ref, t_ref, dx_ref, df_ref, loss_ref, dg_ref, dgt_ref, acc_ref):
        kk = pl.program_id(1)
        part = _dot(hf_ref[...], w_ref[...])

        @pl.when(kk == 0)
        def _():
            acc_ref[...] = part

        @pl.when(jnp.logical_and(kk > 0, kk < nk - 1))
        def _():
            acc_ref[...] += part

        @pl.when(kk == nk - 1)
        def _():
            first = pl.program_id(0) == 0
            fv = part if nk == 1 else acc_ref[...] + part
            gt, gv = gt_ref[...], g_ref[...]
            x2 = x1_ref[...] + gt * fv
            xh, r = _rms(x2)
            err = xh * gv - t_ref[...]
            blk = 0.5 * jnp.sum(jnp.sum(err * err, axis=1, keepdims=True), axis=0, keepdims=True) / d
            dy = err / d
            dxh = dy * gv
            dx = r * (dxh - xh * jnp.mean(dxh * xh, axis=-1, keepdims=True))
            dx_ref[...] = dx
            df_ref[...] = (dx * gt).astype(BF16)
            _accumulate(first, loss_ref, jnp.broadcast_to(blk, (1, LANES)))
            _accumulate(first, dg_ref, _colsum(dy * xh))
            _accumulate(first, dgt_ref, _colsum(dx * fv))

    row = pl.BlockSpec((bm, d), lambda i, kk: (i, 0))
    vec = pl.BlockSpec((1, d), lambda i, kk: (0, 0))
    return _pcall(
        body, name="ffn_out_loss", grid=(s // bm, nk),
        in_specs=[pl.BlockSpec((bm, bk), lambda i, kk: (i, kk)), pl.BlockSpec((bk, d), lambda i, kk: (kk, 0)),
                  row, vec, vec, row],
        out_specs=[row, row, pl.BlockSpec((1, LANES), lambda i, kk: (0, 0)), vec, vec],
        out_shape=[jax.ShapeDtypeStruct((s, d), F32), jax.ShapeDtypeStruct((s, d), BF16),
                   jax.ShapeDtypeStruct((1, LANES), F32), jax.ShapeDtypeStruct((1, d), F32),
                   jax.ShapeDtypeStruct((1, d), F32)],
        scratch_shapes=[pltpu.VMEM((bm, d), F32)],
        compiler_params=_params("arbitrary", "arbitrary"),
    )(hf, w_fo, x1, gt2, g_final, target)


def _norm_mod_bwd(dh, xin, dres, g, sc, name, branch=None, gt=None):
    s, d = xin.shape
    rb = _pick(s, 256)
    has_branch = branch is not None

    def body(*refs):
        if has_branch:
            dh_ref, x_ref, dres_ref, g_ref, sc_ref, br_ref, gt_ref, dx_ref, dbr_ref, dsh_ref, dsc_ref, dg_ref, dgt_ref = refs
        else:
            dh_ref, x_ref, dres_ref, g_ref, sc_ref, dx_ref, dsh_ref, dsc_ref, dg_ref = refs
        first = pl.program_id(0) == 0
        gv = g_ref[...]
        xh, r = _rms(x_ref[...])
        dhv = dh_ref[...]
        dn = dhv * (1.0 + sc_ref[...])
        dxh = dn * gv
        dx = dres_ref[...] + r * (dxh - xh * jnp.mean(dxh * xh, axis=-1, keepdims=True))
        dx_ref[...] = dx
        _accumulate(first, dsh_ref, _colsum(dhv))
        _accumulate(first, dsc_ref, _colsum(dhv * xh * gv))
        _accumulate(first, dg_ref, _colsum(dn * xh))
        if has_branch:
            dbr_ref[...] = (dx * gt_ref[...]).astype(BF16)
            _accumulate(first, dgt_ref, _colsum(dx * br_ref[...]))

    row, vec = _row_spec(rb, d), _vec_spec(d)
    vec_shape = jax.ShapeDtypeStruct((1, d), F32)
    if has_branch:
        ins, in_specs = (dh, xin, dres, g, sc, branch, gt), [row, row, row, vec, vec, row, vec]
        out_specs = [row, row, vec, vec, vec, vec]
        out_shape = [jax.ShapeDtypeStruct((s, d), F32), jax.ShapeDtypeStruct((s, d), BF16)] + [vec_shape] * 4
    else:
        ins, in_specs = (dh, xin, dres, g, sc), [row, row, row, vec, vec]
        out_specs = [row, vec, vec, vec]
        out_shape = [jax.ShapeDtypeStruct((s, d), F32)] + [vec_shape] * 3
    return _pcall(body, name=name, grid=(s // rb,), in_specs=in_specs, out_specs=out_specs,
                          out_shape=out_shape, compiler_params=_params("arbitrary"))(*ins)


def _branch_merge(ya, yb, w_bg, w_bh, z, b_gate, gate_col, bm=512, bn=1024):
    s, gw = ya.shape
    d = w_bg.shape[1]
    bm, bn = _pick(s, bm), _pick(d, bn)
    nj = d // bn

    def body(ya_ref, yb_ref, wa_ref, wb_ref, za_ref, zb_ref, ba_ref, bb_ref, pa_ref, pb_ref, y_ref):
        pa = _dot(ya_ref[...], wa_ref[...])
        pb = _dot(yb_ref[...], wb_ref[...])
        pa_ref[...] = pa.astype(BF16)
        pb_ref[...] = pb.astype(BF16)
        ga = _sigmoid(za_ref[...] + ba_ref[...])
        gb = _sigmoid(zb_ref[...] + bb_ref[...])
        y_ref[...] = (ga * pa + gb * pb).astype(BF16)

    act = pl.BlockSpec((bm, gw), lambda j, i: (i, 0))
    wgt = pl.BlockSpec((gw, bn), lambda j, i: (0, j))
    tile = pl.BlockSpec((bm, bn), lambda j, i: (i, j))
    shape = jax.ShapeDtypeStruct((s, d), BF16)
    return _pcall(
        body, name="branch_merge", grid=(nj, s // bm),
        in_specs=[act, act, wgt, wgt,
                  pl.BlockSpec((bm, bn), lambda j, i: (i, gate_col * nj + j)),
                  pl.BlockSpec((bm, bn), lambda j, i: (i, (gate_col + 1) * nj + j)),
                  pl.BlockSpec((1, bn), lambda j, i: (0, j)), pl.BlockSpec((1, bn), lambda j, i: (0, nj + j))],
        out_specs=[tile, tile, tile], out_shape=[shape, shape, shape], compiler_params=_params("parallel", "parallel"),
    )(ya, yb, w_bg, w_bh, z, z, b_gate, b_gate)


def _branch_bwd(dpa, dpb, w_bg, w_bh, bm=512):
    s, d = dpa.shape
    gw = w_bg.shape[0]
    bm = _pick(s, bm)

    def body(da_ref, db_ref, wa_ref, wb_ref, oa_ref, ob_ref):
        oa_ref[...] = _dot(da_ref[...], wa_ref[...], NT)
        ob_ref[...] = _dot(db_ref[...], wb_ref[...], NT)

    act = pl.BlockSpec((bm, d), lambda i: (i, 0))
    wgt = pl.BlockSpec((gw, d), lambda i: (0, 0))
    out = pl.BlockSpec((bm, gw), lambda i: (i, 0))
    shape = jax.ShapeDtypeStruct((s, gw), F32)
    return _pcall(body, name="branch_bwd", grid=(s // bm,), in_specs=[act, act, wgt, wgt], out_specs=[out, out],
                  out_shape=[shape, shape], compiler_params=_params("parallel"))(dpa, dpb, w_bg, w_bh)


def _out_bwd(dyo, w_out, pa, pb, z, b_gate, gate_col, bm=512, bn=1024):
    s, d = dyo.shape
    bm, bn = _pick(s, bm), _pick(d, bn)
    nj = d // bn

    def body(dyo_ref, w_ref, pa_ref, pb_ref, za_ref, zb_ref, ba_ref, bb_ref,
             dpa_ref, dpb_ref, dza_ref, dzb_ref, dba_ref, dbb_ref):
        first = pl.program_id(1) == 0
        dyv = _dot(dyo_ref[...], w_ref[...], NT)
        ga = _sigmoid(za_ref[...] + ba_ref[...])
        gb = _sigmoid(zb_ref[...] + bb_ref[...])
        dpa_ref[...] = (dyv * ga).astype(BF16)
        dpb_ref[...] = (dyv * gb).astype(BF16)
        dga = dyv * pa_ref[...].astype(F32) * ga * (1.0 - ga)
        dgb = dyv * pb_ref[...].astype(F32) * gb * (1.0 - gb)
        dza_ref[...] = dga.astype(BF16)
        dzb_ref[...] = dgb.astype(BF16)
        _accumulate(first, dba_ref, _colsum(dga))
        _accumulate(first, dbb_ref, _colsum(dgb))

    tile = pl.BlockSpec((bm, bn), lambda j, i: (i, j))
    vec = pl.BlockSpec((1, bn), lambda j, i: (0, j))
    act, vec_shape = jax.ShapeDtypeStruct((s, d), BF16), jax.ShapeDtypeStruct((1, d), F32)
    return _pcall(
        body, name="out_bwd", grid=(nj, s // bm),
        in_specs=[pl.BlockSpec((bm, d), lambda j, i: (i, 0)), pl.BlockSpec((bn, d), lambda j, i: (j, 0)), tile, tile,
                  pl.BlockSpec((bm, bn), lambda j, i: (i, gate_col * nj + j)),
                  pl.BlockSpec((bm, bn), lambda j, i: (i, (gate_col + 1) * nj + j)),
                  vec, pl.BlockSpec((1, bn), lambda j, i: (0, nj + j))],
        out_specs=[tile, tile, tile, tile, vec, vec], out_shape=[act, act, act, act, vec_shape, vec_shape],
        compiler_params=_params("parallel", "arbitrary"),
    )(dyo, w_out, pa, pb, z, z, b_gate, b_gate)


def _ffn_in(h, w_fi, bm=512, bn=1408):
    s, d = h.shape
    per = w_fi.shape[2]
    ff = 2 * per
    bm, bn = _pick(s, bm), _pick(per, bn)
    per_blocks = per // bn

    def body(h_ref, wa_ref, wu_ref, a_ref, u_ref, hf_ref):
        hv = h_ref[...]
        a = _dot(hv, wa_ref[...])
        up = _dot(hv, wu_ref[...])
        a_ref[...] = a.astype(BF16)
        u_ref[...] = up.astype(BF16)
        hf_ref[...] = (a * _sigmoid(a) * up).astype(BF16)

    out = pl.BlockSpec((bm, bn), lambda j, i: (i, j))
    shape = jax.ShapeDtypeStruct((s, ff), BF16)
    return _pcall(
        body, name="ffn_in", grid=(ff // bn, s // bm),
        in_specs=[pl.BlockSpec((bm, d), lambda j, i: (i, 0)),
                  pl.BlockSpec((None, d, bn), lambda j, i: (j // per_blocks, 0, j % per_blocks)),
                  pl.BlockSpec((None, d, bn), lambda j, i: (2 + j // per_blocks, 0, j % per_blocks))],
        out_specs=[out, out, out], out_shape=[shape, shape, shape],
        compiler_params=_params("parallel", "parallel"),
    )(h, w_fi, w_fi)


def _ffn_out_bwd(dffn, w_fo, a_act, up_act, bm=512, bn=1408):
    s, d = dffn.shape
    ff = w_fo.shape[0]
    bm, bn = _pick(s, bm), _pick(ff, bn)

    def body(d_ref, w_ref, a_ref, u_ref, da_ref, du_ref):
        dhf = _dot(d_ref[...], w_ref[...], NT)
        a = a_ref[...].astype(F32)
        sa = _sigmoid(a)
        da_ref[...] = (dhf * u_ref[...].astype(F32) * sa * (1.0 + a * (1.0 - sa))).astype(BF16)
        du_ref[...] = (dhf * a * sa).astype(BF16)

    tile = pl.BlockSpec((bm, bn), lambda j, i: (i, j))
    shape = jax.ShapeDtypeStruct((s, ff), BF16)
    return _pcall(
        body, name="ffn_out_bwd", grid=(ff // bn, s // bm),
        in_specs=[pl.BlockSpec((bm, d), lambda j, i: (i, 0)), pl.BlockSpec((bn, d), lambda j, i: (j, 0)), tile, tile],
        out_specs=[tile, tile], out_shape=[shape, shape], compiler_params=_params("parallel", "parallel"),
    )(dffn, w_fo, a_act, up_act)


def _tril(n):
    return lax.broadcasted_iota(jnp.int32, (n, n), 0) >= lax.broadcasted_iota(jnp.int32, (n, n), 1)


def _gmlp_norm(v, ln_g, ln_b):
    gv = _gelu(v)
    mu = jnp.mean(gv, axis=-1, keepdims=True)
    cen = gv - mu
    rs = lax.rsqrt(jnp.mean(cen * cen, axis=-1, keepdims=True) + EPS)
    xh = cen * rs
    return xh, rs, xh * ln_g + ln_b


def _gmlp_fwd(z, ln_g, ln_b, ws, bs_t):
    s = z.shape[0]
    gw = ln_g.shape[1]
    groups, chunk, _ = ws.shape

    def body(u_ref, v_ref, lg_ref, lb_ref, ws_ref, bs_ref, ya_ref):
        gu = _gelu(u_ref[...])
        _, _, vn = _gmlp_norm(v_ref[...], lg_ref[...], lb_ref[...])
        mask = _tril(chunk)
        for g in range(groups):
            cols = slice(g * LANES, (g + 1) * LANES)
            wm = jnp.where(mask, ws_ref[g], 0.0).astype(BF16)
            sg = _dot(wm, vn[:, cols].astype(BF16)) + bs_ref[:, g:g + 1]
            ya_ref[:, cols] = (gu[:, cols] * sg).astype(BF16)

    return _pcall(
        body, name="gmlp_fwd", grid=(s // chunk,),
        in_specs=[_row_spec(chunk, gw, 0), _row_spec(chunk, gw, 1), _vec_spec(gw), _vec_spec(gw),
                  pl.BlockSpec((groups, chunk, chunk), lambda i: (0, 0, 0)), pl.BlockSpec((chunk, LANES), lambda i: (0, 0))],
        out_specs=_row_spec(chunk, gw), out_shape=jax.ShapeDtypeStruct((s, gw), BF16),
        compiler_params=_params("parallel"),
    )(z, z, ln_g, ln_b, ws, bs_t)


def _gmlp_bwd(dya, z, ln_g, ln_b, ws, bs_t):
    s = z.shape[0]
    gw = ln_g.shape[1]
    groups, chunk, _ = ws.shape

    def body(dya_ref, u_ref, v_ref, lg_ref, lb_ref, ws_ref, bs_ref, duv_ref, dws_ref, dbs_ref, dlg_ref, dlb_ref, dvn_ref):
        first = pl.program_id(0) == 0
        u, v, lg = u_ref[...], v_ref[...], lg_ref[...]
        gu = _gelu(u)
        xh, rs, vn = _gmlp_norm(v, lg, lb_ref[...])
        dyav = dya_ref[...]
        mask = _tril(chunk)
        lane = lax.broadcasted_iota(jnp.int32, (chunk, LANES), 1)
        dbs = jnp.zeros((chunk, LANES), F32)
        for g in range(groups):
            cols = slice(g * LANES, (g + 1) * LANES)
            wm = jnp.where(mask, ws_ref[g], 0.0).astype(BF16)
            vg = vn[:, cols].astype(BF16)
            sg = _dot(wm, vg) + bs_ref[:, g:g + 1]
            ds = dyav[:, cols] * gu[:, cols]
            duv_ref[:, cols] = (dyav[:, cols] * sg * _gelu_grad(u[:, cols])).astype(BF16)
            dsb = ds.astype(BF16)
            _accumulate(first, dws_ref.at[g], jnp.where(mask, _dot(dsb, vg, NT), 0.0))
            dbs = dbs + jnp.where(lane == g, jnp.sum(ds, axis=-1, keepdims=True), 0.0)
            dvn_ref[:, cols] = _dot(wm, dsb, TN)
        dvn = dvn_ref[...]
        _accumulate(first, dbs_ref, dbs)
        _accumulate(first, dlb_ref, _colsum(dvn))
        _accumulate(first, dlg_ref, _colsum(dvn * xh))
        dxh = dvn * lg
        dgv = rs * (dxh - jnp.mean(dxh, axis=-1, keepdims=True) - xh * jnp.mean(dxh * xh, axis=-1, keepdims=True))
        duv_ref[:, gw:] = (dgv * _gelu_grad(v)).astype(BF16)

    return _pcall(
        body, name="gmlp_bwd", grid=(s // chunk,),
        in_specs=[_row_spec(chunk, gw), _row_spec(chunk, gw, 0), _row_spec(chunk, gw, 1), _vec_spec(gw), _vec_spec(gw),
                  pl.BlockSpec((groups, chunk, chunk), lambda i: (0, 0, 0)), pl.BlockSpec((chunk, LANES), lambda i: (0, 0))],
        out_specs=[_row_spec(chunk, 2 * gw), pl.BlockSpec((groups, chunk, chunk), lambda i: (0, 0, 0)),
                   pl.BlockSpec((chunk, LANES), lambda i: (0, 0)), _vec_spec(gw), _vec_spec(gw)],
        out_shape=[jax.ShapeDtypeStruct((s, 2 * gw), BF16), jax.ShapeDtypeStruct((groups, chunk, chunk), F32),
                   jax.ShapeDtypeStruct((chunk, LANES), F32), jax.ShapeDtypeStruct((1, gw), F32),
                   jax.ShapeDtypeStruct((1, gw), F32)],
        scratch_shapes=[pltpu.VMEM((chunk, gw), F32)],
        compiler_params=_params("arbitrary"),
    )(dya, z, z, ln_g, ln_b, ws, bs_t)


def _lower_bound(lb_ref):
    a0, a1 = lb_ref[0:1, :], lb_ref[1:2, :]
    mx = jnp.maximum(a0, a1)
    e0, e1 = jnp.exp(a0 - mx), jnp.exp(a1 - mx)
    return e0 / (e0 + e1)


def _sum_dot(mask, x):
    hi = x.astype(BF16)
    rest = x - hi.astype(F32)
    mid = rest.astype(BF16)
    low = (rest - mid.astype(F32)).astype(BF16)
    return _dot(mask, hi) + _dot(mask, mid) + _dot(mask, low)


def _ones_where(mask):
    return jnp.where(mask, 1.0, 0.0).astype(BF16)


def _hg_masks(rows, t):
    r = lax.broadcasted_iota(jnp.int32, (rows, rows), 0)
    c = lax.broadcasted_iota(jnp.int32, (rows, rows), 1)
    same = (r // t) == (c // t)
    incl = jnp.logical_and(same, c <= r)
    upto_mid = jnp.logical_and(same, (c % t) <= t // 2)
    rev = jnp.logical_and(same, c >= r)
    return same, incl, upto_mid, rev


def _hg_block(q, fp, lb, masks):
    rows = q.shape[0]
    same, incl, upto_mid, _ = masks
    sig = _sigmoid(fp)
    f = lb + (1.0 - lb) * sig
    k = 1.0 - f
    sq = _sigmoid(q)
    qa = q * sq
    stacked = jnp.concatenate([_ones_where(m) for m in (incl, same, upto_mid)], axis=0)
    sums = _sum_dot(stacked, jnp.log(f))
    b, b_last, b_mid = sums[:rows], sums[rows:2 * rows], sums[2 * rows:]
    e_q = jnp.exp(jnp.minimum(b - b_mid, EXP_CLAMP))
    e_k = jnp.exp(jnp.minimum(b_mid - b, EXP_CLAMP))
    e_in = jnp.exp(b)
    e_out = jnp.exp(b_last - b)
    return dict(sig=sig, f=f, k=k, sq=sq, qa=qa, e_last=jnp.exp(b_last), e_q=e_q, e_k=e_k, e_in=e_in, e_out=e_out,
                q_hat=(qa * e_q).astype(BF16), k_hat=(k * e_k).astype(BF16),
                q_in=(qa * e_in).astype(BF16), k_out=k * e_out)


def _hgrn_fwd(z, hg_lb, norm_g, q_col):
    s = z.shape[0]
    hw = norm_g.shape[1]
    heads = hw // LANES
    t = HG_CHUNK
    rows = min(HG_ROWS, s)
    per_step = rows // t
    hp = min(HG_HEADS_PER_STEP, heads)
    assert heads % hp == 0 and q_col % hp == 0, (heads, q_col)
    wide = hp * LANES

    def zspec(which):
        return pl.BlockSpec((rows, wide), lambda h, r, which=which: (r, (q_col + which * heads) // hp + h))

    def body(q_ref, f_ref, i_ref, g_ref, lb_ref, ng_ref, yb_ref, o_ref, st_out_ref, st_ref, e_last_ref, inter_ref):
        @pl.when(pl.program_id(1) == 0)
        def _():
            st_ref[...] = jnp.zeros_like(st_ref)

        masks = _hg_masks(rows, t)
        for hh in range(hp):
            cols = slice(hh * LANES, (hh + 1) * LANES)
            blk = _hg_block(q_ref[:, cols], f_ref[:, cols], _lower_bound(lb_ref.at[:, cols]), masks)
            iv = i_ref[:, cols].astype(BF16)
            q_in, k_out = blk["q_in"], blk["k_out"].astype(BF16)
            e_last_ref[hh] = blk["e_last"]
            attn = jnp.where(masks[1], _dot(blk["q_hat"], blk["k_hat"], NT), 0.0).astype(BF16)
            o = _dot(attn, iv)
            grown = [_dot(iv[j * t:(j + 1) * t], k_out[j * t:(j + 1) * t], TN) for j in range(per_step)]
            st = st_ref[hh]
            for j in range(per_step):
                st_out_ref[hh, j] = st
                inter_ref[hh, j * t:(j + 1) * t, :] = _dot(q_in[j * t:(j + 1) * t], st.astype(BF16), NT)
                st = st * e_last_ref[hh, j * t:j * t + 1, :] + grown[j]
            st_ref[hh] = st
            o = o + inter_ref[hh]
            o_ref[:, cols] = o
            og = g_ref[:, cols]
            on, _ = _rms(o)
            yb_ref[:, cols] = (on * ng_ref[:, cols] * (og * _sigmoid(og))).astype(BF16)

    out_row = pl.BlockSpec((rows, wide), lambda h, r: (r, h))
    return _pcall(
        body, name="hgrn_fwd", grid=(heads // hp, s // rows),
        in_specs=[zspec(0), zspec(1), zspec(2), zspec(3),
                  pl.BlockSpec((2, wide), lambda h, r: (0, h)), pl.BlockSpec((1, wide), lambda h, r: (0, h))],
        out_specs=[out_row, out_row, pl.BlockSpec((hp, per_step, LANES, LANES), lambda h, r: (h, r, 0, 0))],
        out_shape=[jax.ShapeDtypeStruct((s, hw), BF16), jax.ShapeDtypeStruct((s, hw), F32),
                   jax.ShapeDtypeStruct((heads, s // t, LANES, LANES), F32)],
        scratch_shapes=[pltpu.VMEM((hp, LANES, LANES), F32), pltpu.VMEM((hp, rows, LANES), F32),
                        pltpu.VMEM((hp, rows, LANES), F32)],
        compiler_params=_params("parallel", "arbitrary"),
    )(z, z, z, z, hg_lb, norm_g)


def _hgrn_bwd(dyb, z, o_raw, states, hg_lb, norm_g, q_col):
    s = z.shape[0]
    hw = norm_g.shape[1]
    heads = hw // LANES
    t = HG_CHUNK
    rows = min(HG_ROWS, s)
    per_step = rows // t
    n_steps = s // rows
    hp = min(HG_HEADS_PER_STEP, heads)
    assert heads % hp == 0 and q_col % hp == 0, (heads, q_col)
    wide = hp * LANES

    def zspec(which):
        return pl.BlockSpec((rows, wide), lambda h, r, which=which: (n_steps - 1 - r, (q_col + which * heads) // hp + h))

    def body(dyb_ref, q_ref, f_ref, i_ref, g_ref, o_ref, st_in_ref, lb_ref, ng_ref,
             dq_ref, df_ref, di_ref, dg_ref, dlb_ref, dng_ref, dst_ref, acc_lb_ref, acc_ng_ref,
             e_last_ref, dq_in_ref, dk_out_ref, di_inter_ref, carry_ref):
        step = pl.program_id(1)

        @pl.when(step == 0)
        def _():
            dst_ref[...] = jnp.zeros_like(dst_ref)
            acc_lb_ref[...] = jnp.zeros_like(acc_lb_ref)
            acc_ng_ref[...] = jnp.zeros_like(acc_ng_ref)

        masks = _hg_masks(rows, t)
        same, incl, _, rev = masks
        sum_mask = jnp.concatenate([_ones_where(rev), _ones_where(same)], axis=1)
        for hh in range(hp):
            cols = slice(hh * LANES, (hh + 1) * LANES)
            lb = _lower_bound(lb_ref.at[:, cols])
            ng = ng_ref[:, cols]
            q = q_ref[:, cols]
            blk = _hg_block(q, f_ref[:, cols], lb, masks)
            iv = i_ref[:, cols].astype(BF16)
            o, og, dy = o_ref[:, cols], g_ref[:, cols], dyb_ref[:, cols]
            so = _sigmoid(og)
            on, r = _rms(o)
            acc_ng_ref[:, cols] += _colsum(dy * on * (og * so))
            dg_ref[:, cols] = (dy * on * ng * so * (1.0 + og * (1.0 - so))).astype(BF16)
            don = dy * ng * (og * so)
            do = (r * (don - on * jnp.mean(don * on, axis=-1, keepdims=True))).astype(BF16)
            q_hat, k_hat, q_in, k_out = blk["q_hat"], blk["k_hat"], blk["q_in"], blk["k_out"]
            k_out_b = k_out.astype(BF16)
            attn = jnp.where(incl, _dot(q_hat, k_hat, NT), 0.0).astype(BF16)
            d_attn = jnp.where(incl, _dot(do, iv, NT), 0.0).astype(BF16)
            di_intra = _dot(attn, do, TN)
            dq_hat = _dot(d_attn, k_hat)
            dk_hat = _dot(d_attn, q_hat, TN)
            e_last_ref[hh] = blk["e_last"]
            grown = [_dot(do[j * t:(j + 1) * t], q_in[j * t:(j + 1) * t], TN) for j in range(per_step)]
            dst = dst_ref[hh]
            for j in reversed(range(per_step)):
                rs_ = slice(j * t, (j + 1) * t)
                e_last = e_last_ref[hh, j * t:j * t + 1, :]
                st_prev, dst_b = st_in_ref[hh, j], dst.astype(BF16)
                dq_in_ref[hh, rs_, :] = _dot(do[rs_], st_prev.astype(BF16))
                dk_out_ref[hh, rs_, :] = _dot(iv[rs_], dst_b)
                di_inter_ref[hh, rs_, :] = _dot(k_out_b[rs_], dst_b, NT)
                carry_ref[hh, rs_, :] = jnp.broadcast_to(e_last * _colsum(st_prev * dst), (t, LANES))
                dst = dst * e_last + grown[j]
            dst_ref[hh] = dst
            di_ref[:, cols] = (di_intra + di_inter_ref[hh]).astype(BF16)
            dk_out = dk_out_ref[hh]
            dqa = dq_in_ref[hh] * blk["e_in"] + dq_hat * blk["e_q"]
            dk = dk_out * blk["e_out"] + dk_hat * blk["e_k"]
            db = blk["qa"] * dqa - blk["k"] * dk
            dlf = _sum_dot(sum_mask, jnp.concatenate([db, dk_out * k_out], axis=0)) + carry_ref[hh]
            dfv = dlf / blk["f"] - dk
            sig, sq = blk["sig"], blk["sq"]
            df_ref[:, cols] = (dfv * (1.0 - lb) * sig * (1.0 - sig)).astype(BF16)
            acc_lb_ref[:, cols] += _colsum(dfv * (1.0 - sig))
            dq_ref[:, cols] = (dqa * sq * (1.0 + q * (1.0 - sq))).astype(BF16)

        @pl.when(step == n_steps - 1)
        def _():
            lb = _lower_bound(lb_ref)
            d0 = acc_lb_ref[...] * lb * (1.0 - lb)
            dlb_ref[0:1, :] = d0
            dlb_ref[1:2, :] = -d0
            dng_ref[...] = acc_ng_ref[...]

    rev_row = pl.BlockSpec((rows, wide), lambda h, r: (n_steps - 1 - r, h))
    piece = jax.ShapeDtypeStruct((s, hw), BF16)
    return _pcall(
        body, name="hgrn_bwd", grid=(heads // hp, n_steps),
        in_specs=[rev_row, zspec(0), zspec(1), zspec(2), zspec(3), rev_row,
                  pl.BlockSpec((hp, per_step, LANES, LANES), lambda h, r: (h, n_steps - 1 - r, 0, 0)),
                  pl.BlockSpec((2, wide), lambda h, r: (0, h)), pl.BlockSpec((1, wide), lambda h, r: (0, h))],
        out_specs=[rev_row, rev_row, rev_row, rev_row,
                   pl.BlockSpec((2, wide), lambda h, r: (0, h)), pl.BlockSpec((1, wide), lambda h, r: (0, h))],
        out_shape=[piece, piece, piece, piece, jax.ShapeDtypeStruct((2, hw), F32), jax.ShapeDtypeStruct((1, hw), F32)],
        scratch_shapes=[pltpu.VMEM((hp, LANES, LANES), F32), pltpu.VMEM((1, wide), F32), pltpu.VMEM((1, wide), F32)]
        + [pltpu.VMEM((hp, rows, LANES), F32)] * 5,
        compiler_params=_params("parallel", "arbitrary"),
    )(dyb, z, z, z, z, o_raw, states, hg_lb, norm_g)


def _adam_update(w, m, v, g):
    m2 = ADAM_B1 * m + (1.0 - ADAM_B1) * g
    v2 = ADAM_B2 * v + (1.0 - ADAM_B2) * (g * g)
    m_hat = m2 * (1.0 / (1.0 - ADAM_B1 ** ADAM_STEP))
    v_hat = v2 * (1.0 / (1.0 - ADAM_B2 ** ADAM_STEP))
    return -ADAM_LR * (m_hat / (jnp.sqrt(v_hat) + ADAM_EPS) + ADAM_WD * w), m2, v2


def _adamw(w, m, v, parts, name, outer=False):
    rows, cols = w.shape
    bc = cols if cols <= 4096 else _pick(cols, 4096)
    rb = _pick_rows(rows, max(8, (384 * 1024) // bc), mult=8)
    if outer and rb % LANES:
        rb = rows

    def body(w_ref, m_ref, v_ref, *refs):
        g_ref, d_ref, mo_ref, vo_ref = refs[-4:]
        if outer:
            cv = refs[0][...]
            g = _dot(cv * _sigmoid(cv), refs[1][...], TN, lax.Precision.HIGHEST)
        else:
            p_ref = refs[0]
            g = p_ref[0].astype(F32)
            for p in range(1, p_ref.shape[0]):
                g = g + p_ref[p].astype(F32)
        g_ref[...] = g
        d_ref[...], mo_ref[...], vo_ref[...] = _adam_update(w_ref[...], m_ref[...], v_ref[...], g)

    blk = pl.BlockSpec((rb, bc), lambda i, j: (i, j))
    out = jax.ShapeDtypeStruct((rows, cols), F32)
    if outer:
        grad_specs = [pl.BlockSpec((N_DEV, rb), lambda i, j: (0, i)), pl.BlockSpec((N_DEV, bc), lambda i, j: (0, j))]
        grad_ins = tuple(parts)
    else:
        grad_specs = [pl.BlockSpec((parts.shape[0], rb, bc), lambda i, j: (0, i, j))]
        grad_ins = (parts,)
    return _pcall(
        body, name=name, grid=(rows // rb, cols // bc), in_specs=[blk, blk, blk] + grad_specs,
        out_specs=[blk] * 4, out_shape=[out] * 4, compiler_params=_params("parallel", "parallel"),
    )(w, m, v, *grad_ins)


SMALL = ("b_ada", "norm1_g", "b_gate", "gmlp_ln_g", "gmlp_ln_b", "gmlp_ws", "gmlp_bs", "hg_lb", "hg_norm_g",
         "norm2_g", "final_norm_g")
BIG = ("w_in", "w_branch_gmlp", "w_branch_hg", "w_out", "w_ffn_in", "w_ffn_out")
WEIGHTS = ("w_ada", "b_ada", "norm1_g", "w_in", "b_gate", "gmlp_ln_g", "gmlp_ln_b", "gmlp_ws", "gmlp_bs", "hg_lb",
           "hg_norm_g", "w_branch_gmlp", "w_branch_hg", "w_out", "norm2_g", "w_ffn_in", "w_ffn_out", "final_norm_g")


def _pack(parts):
    return jnp.concatenate([p.reshape(-1, LANES) for p in parts], axis=0)


def _step(x, c, loss_target, w, m, v):
    s, d = x.shape[1], x.shape[2]
    gw = w["gmlp_ln_g"].shape[-1]
    hw = w["hg_norm_g"].shape[-1]
    x2d, tgt = x[0], loss_target[0]
    mx, my, mc = lax.axis_index("x"), lax.axis_index("y"), lax.axis_index("c")
    chip = 2 * mx + my
    dev = 2 * chip + mc
    q_col = 2 * gw // LANES
    gate_col = (2 * gw + 4 * hw) // d
    place = jnp.stack([chip, mc]).astype(jnp.int32)
    _Order.last = None

    c_all = _all_gather8(c.reshape(-1, LANES), "gather_c").reshape(N_DEV, d)
    gather_in = _WeightGather("in", _WeightGather.zones([w["w_in"][0]], chip), ring=True)
    zones_mix = _WeightGather.zones([w[n][0] for n in ("w_branch_gmlp", "w_branch_hg", "w_out")], chip)
    zones_fi = _WeightGather.zones([w["w_ffn_in"][0]], chip)
    zones_fo = _WeightGather.zones([w["w_ffn_out"][0]], chip)
    _Order.also = (*zones_mix, *zones_fi, *zones_fo)
    n_ada = w["w_ada"].shape[-1]
    b_ada_q = lax.dynamic_slice(w["b_ada"], (0, chip * n_ada), (1, n_ada))
    mod_q = _ada_fwd(c_all, w["w_ada"][0], b_ada_q)
    mod_all = _all_gather8(mod_q, "gather_mod")
    mod = lax.dynamic_index_in_dim(mod_all, dev, axis=1, keepdims=False)[::2].reshape(1, 6 * d)
    sh1, sc1, gt1, sh2, sc2, gt2 = [mod[:, i * d:(i + 1) * d] for i in range(6)]

    gather_in.relay_and_pass_near()
    gather_mix = _WeightGather("mix", zones_mix)
    gather_fi = _WeightGather("fi", zones_fi, ring=True)

    norm1_g, norm2_g, final_g = w["norm1_g"], w["norm2_g"], w["final_norm_g"].reshape(1, d)
    ln_g, ln_b = w["gmlp_ln_g"], w["gmlp_ln_b"]
    ws = w["gmlp_ws"][0]
    groups = ws.shape[0]
    bs_t = jnp.pad(w["gmlp_bs"][0].T, ((0, 0), (0, LANES - groups)))
    hg_lb, hg_ng, b_gate = w["hg_lb"], w["hg_norm_g"], w["b_gate"]

    h1 = _norm_mod_fwd(x2d, norm1_g, sc1, sh1, "norm1_fwd")
    order = jnp.stack([chip, 2 * (1 - mx) + my, 2 * mx + (1 - my), 2 * (1 - mx) + (1 - my)]).astype(jnp.int32)
    w_in, = gather_in.near_done()
    z = _matmul_quarters(h1, w_in, order, 0, 3, "mm_z_near")
    w_in, = gather_in.pass_far()
    z = _matmul_quarters(h1, w_in, order, 3, 1, "mm_z_far", prev=z)
    gather_mix.pass_on()
    ya = _gmlp_fwd(z, ln_g, ln_b, ws, bs_t)
    yb, o_raw, states = _hgrn_fwd(z, hg_lb, hg_ng, q_col)
    gather_fi.relay()
    gather_fo = _WeightGather("fo", zones_fo)
    w_bg, w_bh, w_out = gather_mix.done()
    w_bg, w_bh = [wq.transpose(1, 0, 2).reshape(wq.shape[1], -1) for wq in (w_bg, w_bh)]
    w_out = w_out.reshape(-1, w_out.shape[-1])
    pa, pb, y = _branch_merge(ya, yb, w_bg, w_bh, z, b_gate, gate_col)
    gather_fi.pass_on()
    yo = _matmul(y, w_out, mode="nn", name="mm_yo", out_dtype=F32)
    x1, h2 = _norm_mod_fwd(x2d, norm2_g, sc2, sh2, "norm2_fwd", res=yo, gt=gt1)
    w_fi, = gather_fi.done()
    a_act, up_act, hf = _ffn_in(h2, w_fi)
    gather_fo.pass_on()
    w_fo, = gather_fo.done()
    w_fo = w_fo.reshape(-1, w_fo.shape[-1])
    dx2, dffn, loss_row, d_final_g, d_gt2 = _ffn_out_loss(hf, w_fo, x1, gt2, final_g, tgt)

    g_fo = _matmul(hf, dffn, mode="tn", name="mm_g_fo", out_dtype=BF16, bm=1408)
    daup = tuple(_ffn_out_bwd(dffn, w_fo, a_act, up_act))
    g_fi = _matmul(h2, daup, mode="tn", name="mm_g_fi", out_dtype=BF16, out_slots=True, bn=1408)
    red_ffn = _GradReduce("ffn", [g_fo.reshape(N_CHIPS, -1, g_fo.shape[-1]), g_fi])
    dh2 = _matmul(daup, w_fi, mode="nt", name="mm_dh2", out_dtype=F32, b_slots=True)
    red_ffn.step(place)
    dx1, dyo, d_sh2, d_sc2, d_norm2, d_gt1 = _norm_mod_bwd(dh2, x1, dx2, norm2_g, sc2, "norm2_bwd", branch=yo, gt=gt1)
    g_out = _matmul(y, dyo, mode="tn", name="mm_g_out", out_dtype=BF16)
    dpa, dpb, dz_ga, dz_gb, d_b_ga, d_b_gb = _out_bwd(dyo, w_out, pa, pb, z, b_gate, gate_col)
    g_bg = _matmul(ya, dpa, mode="tn", name="mm_g_bg", out_dtype=BF16, out_slots=True)
    g_bh = _matmul(yb, dpb, mode="tn", name="mm_g_bh", out_dtype=BF16, out_slots=True)
    red_mix = _GradReduce("mix", [g_out.reshape(N_CHIPS, -1, g_out.shape[-1]), g_bg, g_bh])
    dya, dyb = _branch_bwd(dpa, dpb, w_bg, w_bh)
    red_mix.step(place)
    dz_uv, d_ws, d_bs_t, d_ln_g, d_ln_b = _gmlp_bwd(dya, z, ln_g, ln_b, ws, bs_t)
    dz_q, dz_f, dz_i, dz_g, d_hg_lb, d_hg_ng = _hgrn_bwd(dyb, z, o_raw, states, hg_lb, hg_ng, q_col)
    dz = jnp.concatenate([dz_uv, dz_q, dz_f, dz_i, dz_g, dz_ga, dz_gb], axis=1)
    grad, delta, new_m, new_v = {}, {}, {}, {}

    def update(n, parts, outer=False):
        outs = _adamw(w[n][0], m[n][0], v[n][0], parts, "adamw_" + n, outer=outer)
        grad[n], delta[n], new_m[n], new_v[n] = [o[None] for o in outs]

    g_in = _matmul(h1, dz, mode="tn", name="mm_g_in", out_dtype=BF16, out_slots=True, bn=1280)
    red_in = _GradReduce("in", [g_in])
    red_ffn.join(place)
    red_mix.join(place)
    red_in.step(place)
    g_fo, g_fi = red_ffn.done()
    n = "w_ffn_in"
    dh1, rode = _matmul(dz, w_in, mode="nt", name="mm_dh1", out_dtype=F32, b_slots=True,
                        rider=([w[n][0], m[n][0], v[n][0], g_fi], lambda *blk: (blk[3], *_adam_update(*blk)), 4))
    grad[n], delta[n], new_m[n], new_v[n] = [o[None] for o in rode]
    grad_x, d_sh1, d_sc1, d_norm1 = _norm_mod_bwd(dh1, x2d, dx1, norm1_g, sc1, "norm1_bwd")
    update("w_ffn_out", g_fo[None])
    g_out, g_bg, g_bh = red_mix.done()
    update("w_out", g_out[None])
    update("w_branch_gmlp", g_bg[None])
    update("w_branch_hg", g_bh[None])

    d_mod = jnp.concatenate([d_sh1, d_sc1, d_gt1, d_sh2, d_sc2, d_gt2], axis=1)
    small_part = {"b_ada": d_mod, "norm1_g": d_norm1, "b_gate": jnp.concatenate([d_b_ga, d_b_gb], axis=1), "gmlp_ln_g": d_ln_g, "gmlp_ln_b": d_ln_b,
                  "gmlp_ws": d_ws, "gmlp_bs": d_bs_t[:, :groups].T, "hg_lb": d_hg_lb, "hg_norm_g": d_hg_ng,
                  "norm2_g": d_norm2, "final_norm_g": d_final_g}
    small_all = _all_gather8(_pack([small_part[n] for n in SMALL]), "gather_small")
    d_mod_all = small_all[:, :6 * d // LANES].reshape(N_DEV, 6 * d)
    d_mod_q = lax.dynamic_slice(d_mod_all, (0, chip * n_ada), (N_DEV, n_ada))
    red_in.join(place)
    update("w_ada", (c_all, d_mod_q), outer=True)
    outs = _adamw(_pack([w[n] for n in SMALL]), _pack([m[n] for n in SMALL]), _pack([v[n] for n in SMALL]),
                  small_all, "adamw_small")
    update("w_in", red_in.done()[0][None])
    row = 0
    for n in SMALL:
        cnt = w[n].size // LANES
        for dst, o in zip((grad, delta, new_m, new_v), outs):
            dst[n] = o[row:row + cnt].reshape(w[n].shape)
        row += cnt

    loss = lax.psum(loss_row[0, 0], ("x", "y", "c"))
    return (loss, grad_x[None], *[grad[n] for n in WEIGHTS], *[delta[n] for n in WEIGHTS],
            *[new_m[n] for n in WEIGHTS], *[new_v[n] for n in WEIGHTS])


def kernel(x, c, w_ada, b_ada, norm1_g, w_in, b_gate, gmlp_ln_g, gmlp_ln_b, gmlp_ws, gmlp_bs, hg_lb, hg_norm_g, w_branch_gmlp, w_branch_hg, w_out, norm2_g, w_ffn_in, w_ffn_out, final_norm_g, loss_target, m_w_ada, m_b_ada, m_norm1_g, m_w_in, m_b_gate, m_gmlp_ln_g, m_gmlp_ln_b, m_gmlp_ws, m_gmlp_bs, m_hg_lb, m_hg_norm_g, m_w_branch_gmlp, m_w_branch_hg, m_w_out, m_norm2_g, m_w_ffn_in, m_w_ffn_out, m_final_norm_g, v_w_ada, v_b_ada, v_norm1_g, v_w_in, v_b_gate, v_gmlp_ln_g, v_gmlp_ln_b, v_gmlp_ws, v_gmlp_bs, v_hg_lb, v_hg_norm_g, v_w_branch_gmlp, v_w_branch_hg, v_w_out, v_norm2_g, v_w_ffn_in, v_w_ffn_out, v_final_norm_g):
    w = dict(w_ada=w_ada, b_ada=b_ada, norm1_g=norm1_g, w_in=w_in, b_gate=b_gate, gmlp_ln_g=gmlp_ln_g,
             gmlp_ln_b=gmlp_ln_b, gmlp_ws=gmlp_ws, gmlp_bs=gmlp_bs, hg_lb=hg_lb, hg_norm_g=hg_norm_g,
             w_branch_gmlp=w_branch_gmlp, w_branch_hg=w_branch_hg, w_out=w_out, norm2_g=norm2_g,
             w_ffn_in=w_ffn_in, w_ffn_out=w_ffn_out, final_norm_g=final_norm_g)
    m = dict(w_ada=m_w_ada, b_ada=m_b_ada, norm1_g=m_norm1_g, w_in=m_w_in, b_gate=m_b_gate, gmlp_ln_g=m_gmlp_ln_g,
             gmlp_ln_b=m_gmlp_ln_b, gmlp_ws=m_gmlp_ws, gmlp_bs=m_gmlp_bs, hg_lb=m_hg_lb, hg_norm_g=m_hg_norm_g,
             w_branch_gmlp=m_w_branch_gmlp, w_branch_hg=m_w_branch_hg, w_out=m_w_out, norm2_g=m_norm2_g,
             w_ffn_in=m_w_ffn_in, w_ffn_out=m_w_ffn_out, final_norm_g=m_final_norm_g)
    v = dict(w_ada=v_w_ada, b_ada=v_b_ada, norm1_g=v_norm1_g, w_in=v_w_in, b_gate=v_b_gate, gmlp_ln_g=v_gmlp_ln_g,
             gmlp_ln_b=v_gmlp_ln_b, gmlp_ws=v_gmlp_ws, gmlp_bs=v_gmlp_bs, hg_lb=v_hg_lb, hg_norm_g=v_hg_norm_g,
             w_branch_gmlp=v_w_branch_gmlp, w_branch_hg=v_w_branch_hg, w_out=v_w_out, norm2_g=v_norm2_g,
             w_ffn_in=v_w_ffn_in, w_ffn_out=v_w_ffn_out, final_norm_g=v_final_norm_g)
    return _step(x, c, loss_target, w, m, v)
```

```python
import functools

import jax
import jax.numpy as jnp
from jax import lax
from jax.experimental import pallas as pl
from jax.experimental.pallas import tpu as pltpu

F32 = jnp.float32
BF16 = jnp.bfloat16
EPS = 1e-6
LANES = 128
N_CHIPS = 4
N_DEV = 8
VMEM_LIMIT_BYTES = 56 * 1024 * 1024
VMEM_RIDER_LIMIT_BYTES = 60 * 1024 * 1024
HG_CHUNK = 32
HG_ROWS = 256
HG_HEADS_PER_STEP = 8
EXP_CLAMP = 80.0
ADAM_LR, ADAM_B1, ADAM_B2, ADAM_EPS, ADAM_WD, ADAM_STEP = 0.001, 0.9, 0.999, 1e-08, 0.01, 10
MESH = pl.DeviceIdType.MESH

NN = (((1,), (0,)), ((), ()))
NT = (((1,), (1,)), ((), ()))
TN = (((0,), (0,)), ((), ()))


def _dot(a, b, dims=NN, precision=None):
    return lax.dot_general(a, b, dims, precision=precision, preferred_element_type=F32)


def _params(*semantics, vmem_limit_bytes=VMEM_LIMIT_BYTES):
    return pltpu.CompilerParams(dimension_semantics=semantics, vmem_limit_bytes=vmem_limit_bytes)


class _Order:
    last = None
    also = ()


def _pcall(body, *, in_specs, out_specs, grid=(), scratch_shapes=(), num_scalar_prefetch=0, **kw):
    def run(*ins):
        deps = (() if _Order.last is None else (_Order.last,)) + tuple(_Order.also)
        _Order.also = ()
        n_in, n_dep = len(ins), len(deps)

        def wrapped(*refs):
            body(*refs[:n_in], *refs[n_in + n_dep:])

        specs = list(in_specs) + [pl.BlockSpec(memory_space=pl.ANY)] * n_dep
        if num_scalar_prefetch:
            grid_spec = pltpu.PrefetchScalarGridSpec(
                num_scalar_prefetch=num_scalar_prefetch, grid=grid, in_specs=specs, out_specs=out_specs,
                scratch_shapes=scratch_shapes)
            outs = pl.pallas_call(wrapped, grid_spec=grid_spec, **kw)(*ins, *deps)
        else:
            outs = pl.pallas_call(wrapped, grid=grid, in_specs=specs, out_specs=out_specs,
                                  scratch_shapes=scratch_shapes, **kw)(*ins, *deps)
        _Order.last = jax.tree.leaves(outs)[0]
        return outs

    return run


def _pick_rows(dim, pref, mult=16):
    best = None
    for cand in range(mult, min(dim, pref) + 1, mult):
        if dim % cand == 0:
            best = cand
    assert best is not None, (dim, pref)
    return best


def _pick(dim, pref):
    if dim <= pref:
        return dim
    best = None
    for cand in range(LANES, pref + 1, LANES):
        if dim % cand == 0:
            best = cand
    assert best is not None, (dim, pref)
    return best


def _sigmoid(x):
    return 1.0 / (1.0 + jnp.exp(-x))


def _gelu(x):
    c = 0.7978845608028654
    return 0.5 * x * (1.0 + jnp.tanh(c * (x + 0.044715 * x * x * x)))


def _gelu_grad(x):
    c = 0.7978845608028654
    t = jnp.tanh(c * (x + 0.044715 * x * x * x))
    return 0.5 * (1.0 + t) + 0.5 * x * (1.0 - t * t) * c * (1.0 + 3.0 * 0.044715 * x * x)


def _rms(x):
    r = lax.rsqrt(jnp.mean(x * x, axis=-1, keepdims=True) + EPS)
    return x * r, r


def _colsum(x):
    return jnp.sum(x, axis=0, keepdims=True)


def _accumulate(first, ref, val):
    @pl.when(first)
    def _():
        ref[...] = val

    @pl.when(jnp.logical_not(first))
    def _():
        ref[...] += val


def _matmul(a, b, *, mode, name, out_dtype, b_slots=False, out_slots=False, bm=1024, bn=1024, bk=2816, rider=None):
    pair = a if isinstance(a, tuple) else b if isinstance(b, tuple) else None
    if mode == "nn":
        m, k = a.shape
        n = b.shape[2] * N_CHIPS if b_slots else b.shape[1]
        per = b.shape[2] if b_slots else n
    elif mode == "nt":
        m, k = (a[0].shape[0], 2 * a[0].shape[1]) if pair else a.shape
        n = b.shape[1] if b_slots else b.shape[0]
        per = b.shape[2] if b_slots else k
    else:
        k, m = a.shape
        n = 2 * b[0].shape[1] if pair else b.shape[1]
        per = n // N_CHIPS if out_slots else n
    bm = _pick(m, bm)
    if mode == "nt":
        bn, bk = _pick(n, bn), _pick(per, bk)
    else:
        bn, bk = _pick(per, bn), _pick(k, bk)
    nk = k // bk
    per_blocks = per // (bk if mode == "nt" else bn)
    dims = {"nn": NN, "nt": NT, "tn": TN}[mode]
    half = (nk if mode == "nt" else n // bn) // 2

    def product(a_ref, b_ref, o_ref, acc):
        part = _dot(a_ref[...], b_ref[...], dims)
        if nk == 1:
            o_ref[...] = part.astype(o_ref.dtype)
            return
        acc_ref, kk = acc[0], pl.program_id(2)

        @pl.when(kk == 0)
        def _():
            acc_ref[...] = part

        @pl.when(jnp.logical_and(kk > 0, kk < nk - 1))
        def _():
            acc_ref[...] += part

        @pl.when(kk == nk - 1)
        def _():
            o_ref[...] = (acc_ref[...] + part).astype(o_ref.dtype)

    n_ride, n_rode = (len(rider[0]), rider[2]) if rider else (0, 0)

    def body(*refs):
        if rider:
            ride_in, ride_out = refs[2:2 + n_ride], refs[3 + n_ride:3 + n_ride + n_rode]

            def ride(t, carry):
                rows = pl.ds(pl.multiple_of(t * 8, 8), 8)
                for ref, val in zip(ride_out, rider[1](*[r[rows, :] for r in ride_in])):
                    ref[rows, :] = val
                return carry

            lax.fori_loop(0, ride_in[0].shape[0] // 8, ride, 0)
        if not pair:
            return product(refs[0], refs[1], refs[2 + n_ride], refs[3 + n_ride + n_rode:])
        first = pl.program_id(2 if mode == "nt" else 1) < half
        x1, x2, y, o_ref, acc = refs[0], refs[1], refs[2], refs[3], refs[4:]

        @pl.when(first)
        def _():
            product(*((x1, y) if mode == "nt" else (y, x1)), o_ref, acc)

        @pl.when(jnp.logical_not(first))
        def _():
            product(*((x2, y) if mode == "nt" else (y, x2)), o_ref, acc)

    if mode == "nn":
        a_spec = pl.BlockSpec((bm, bk), lambda i, j, kk: (i, kk))
        if b_slots:
            b_spec = pl.BlockSpec((None, bk, bn), lambda i, j, kk: (j // per_blocks, kk, j % per_blocks))
        else:
            b_spec = pl.BlockSpec((bk, bn), lambda i, j, kk: (kk, j))
    elif mode == "nt":
        a_spec = pl.BlockSpec((bm, bk), lambda i, j, kk: (i, kk))
        if b_slots:
            b_spec = pl.BlockSpec((None, bn, bk), lambda i, j, kk: (kk // per_blocks, j, kk % per_blocks))
        else:
            b_spec = pl.BlockSpec((bn, bk), lambda i, j, kk: (j, kk))
    else:
        a_spec = pl.BlockSpec((bk, bm), lambda i, j, kk: (kk, i))
        b_spec = pl.BlockSpec((bk, bn), lambda i, j, kk: (kk, j))
    if out_slots:
        o_spec = pl.BlockSpec((None, bm, bn), lambda i, j, kk: (j // per_blocks, i, j % per_blocks))
        out_shape = jax.ShapeDtypeStruct((N_CHIPS, m, per), out_dtype)
    else:
        o_spec = pl.BlockSpec((bm, bn), lambda i, j, kk: (i, j))
        out_shape = jax.ShapeDtypeStruct((m, n), out_dtype)
    if not pair:
        ins, in_specs = (a, b), [a_spec, b_spec]
    elif mode == "nt":
        ins = (*a, b)
        in_specs = [pl.BlockSpec((bm, bk), lambda i, j, kk: (i, jnp.minimum(kk, half - 1))),
                    pl.BlockSpec((bm, bk), lambda i, j, kk: (i, jnp.maximum(kk - half, 0))), b_spec]
    else:
        ins = (*b, a)
        in_specs = [pl.BlockSpec((bk, bn), lambda i, j, kk: (kk, jnp.minimum(j, half - 1))),
                    pl.BlockSpec((bk, bn), lambda i, j, kk: (kk, jnp.maximum(j - half, 0))), a_spec]
    scratch = [pltpu.VMEM((bm, bn), F32)] if nk > 1 else []
    if not rider:
        return _pcall(
            body, name=name, grid=(m // bm, n // bn, nk), in_specs=in_specs, out_specs=o_spec, out_shape=out_shape,
            scratch_shapes=scratch, compiler_params=_params("parallel", "parallel", "arbitrary"),
        )(*ins)
    assert not pair
    rows, cols = rider[0][0].shape
    nj = n // bn
    rb = rows // ((m // bm) * nj * nk)
    assert rows == rb * (m // bm) * nj * nk and rb % 8 == 0, (rows, rb)
    ride_spec = pl.BlockSpec((rb, cols), lambda i, j, kk: ((i * nj + j) * nk + kk, 0))
    outs = _pcall(
        body, name=name, grid=(m // bm, nj, nk), in_specs=in_specs + [ride_spec] * n_ride,
        out_specs=[o_spec] + [ride_spec] * n_rode,
        out_shape=[out_shape] + [jax.ShapeDtypeStruct((rows, cols), F32)] * n_rode,
        scratch_shapes=scratch,
        compiler_params=_params("arbitrary", "arbitrary", "arbitrary", vmem_limit_bytes=VMEM_RIDER_LIMIT_BYTES),
    )(*ins, *rider[0])
    return outs[0], outs[1:]


def _matmul_quarters(h, w_slots, order, first, count, name, prev=None, bm=1024, bn=1280):
    s, k = h.shape
    n = w_slots.shape[2]
    bm, bn = _pick(s, bm), _pick(n, bn)
    pb = n // bn

    def body(order_ref, h_ref, w_ref, *rest):
        rest[-1][...] = _dot(h_ref[...], w_ref[...])

    ins = (order, h, w_slots) + (() if prev is None else (prev,))
    return _pcall(
        body, name=name, num_scalar_prefetch=1, grid=(count * pb, s // bm),
        in_specs=[pl.BlockSpec((bm, k), lambda j, i, o: (i, 0)),
                  pl.BlockSpec((None, k, bn), lambda j, i, o: (o[first + j // pb], 0, j % pb))]
        + ([] if prev is None else [ANY_SPEC]),
        out_specs=pl.BlockSpec((bm, bn), lambda j, i, o: (i, o[first + j // pb] * pb + j % pb)),
        out_shape=jax.ShapeDtypeStruct((s, N_CHIPS * n), F32),
        input_output_aliases={} if prev is None else {3: 0},
        compiler_params=_params("arbitrary", "arbitrary"),
    )(*ins)


def _place():
    x, y, c = lax.axis_index("x"), lax.axis_index("y"), lax.axis_index("c")
    chips = [(1 - x, y), (x, 1 - y), (1 - x, 1 - y)]
    return x, y, c, chips


def _all_gather8(block, name):
    def body(x_ref, out_ref, send_sems, recv_sems, local_sem):
        x, y, c, chips = _place()
        me, sibling = (x, y, c), (x, y, 1 - c)

        def slot(px, py, pc):
            return out_ref.at[4 * px + 2 * py + pc]

        def copy(k, blk, to, src=None):
            return pltpu.make_async_remote_copy(
                src_ref=slot(*blk) if src is None else src, dst_ref=slot(*blk),
                send_sem=send_sems.at[k], recv_sem=recv_sems.at[k], device_id=to, device_id_type=MESH)

        mine = pltpu.make_async_copy(x_ref, slot(*me), local_sem)
        mine.start()
        first = [copy(0, me, sibling, src=x_ref)]
        first += [copy(1 + j, me, (*chip, c), src=x_ref) for j, chip in enumerate(chips)]
        for cp in first:
            cp.start()
        passed = [copy(4 + j, (*chip, c), sibling) for j, chip in enumerate(chips)]
        for j, chip in enumerate(chips):
            copy(1 + j, (*chip, c), me).wait_recv()
            passed[j].start()
        copy(0, sibling, me).wait_recv()
        for j, chip in enumerate(chips):
            copy(4 + j, (*chip, 1 - c), me).wait_recv()
        for cp in first + passed:
            cp.wait_send()
        mine.wait()

    return _pcall(
        body, name=name, out_shape=jax.ShapeDtypeStruct((N_DEV,) + block.shape, block.dtype),
        in_specs=[pl.BlockSpec(memory_space=pltpu.VMEM)], out_specs=pl.BlockSpec(memory_space=pltpu.VMEM),
        scratch_shapes=[pltpu.SemaphoreType.DMA((7,)), pltpu.SemaphoreType.DMA((7,)), pltpu.SemaphoreType.DMA],
        compiler_params=pltpu.CompilerParams(vmem_limit_bytes=VMEM_LIMIT_BYTES),
    )(block)


HBM_SPEC = pl.BlockSpec(memory_space=pltpu.HBM)
SEM_SPEC = pl.BlockSpec(memory_space=pltpu.SEMAPHORE)
ANY_SPEC = pl.BlockSpec(memory_space=pl.ANY)
EFFECT = pltpu.SideEffectType.DATAFLOW_SIDE_EFFECTING


def _xfer_start(name, bufs, plan, n_copies, after_last=False):
    nb = len(bufs)
    deps = (_Order.last,) if after_last and _Order.last is not None else ()
    nd = len(deps)

    def body(*refs):
        send_sems, recv_sems = refs[nb + nd], refs[nb + nd + 1]
        token = refs[nb + nd + 2 + nb]
        for k, (src, dst, dev) in enumerate(plan(refs[:nb], *_place())):
            pltpu.make_async_remote_copy(src_ref=src, dst_ref=dst, send_sem=send_sems.at[k], recv_sem=recv_sems.at[k],
                                         device_id=dev, device_id_type=MESH).start()
        token[...] = jnp.zeros_like(token)

    outs = pl.pallas_call(
        body, name=name,
        out_shape=(pltpu.SemaphoreType.DMA((n_copies,)), pltpu.SemaphoreType.DMA((n_copies,)),
                   *[pltpu.HBM(b.shape, b.dtype) for b in bufs], jax.ShapeDtypeStruct((8, LANES), F32)),
        in_specs=[HBM_SPEC] * nb + [ANY_SPEC] * nd,
        out_specs=(SEM_SPEC, SEM_SPEC, *[HBM_SPEC] * nb, pl.BlockSpec(memory_space=pltpu.VMEM)),
        input_output_aliases={i: 2 + i for i in range(nb)},
        compiler_params=pltpu.CompilerParams(has_side_effects=EFFECT),
    )(*[pltpu.with_memory_space_constraint(b, pltpu.HBM) for b in bufs], *deps)
    _Order.last = outs[-1]
    return (outs[0], outs[1]), list(outs[2:2 + nb])


def _xfer_wait(name, sems, bufs, plan):
    nb = len(bufs)

    def body(*refs):
        send_sems, recv_sems = refs[nb], refs[nb + 1]
        for k, (src, dst, dev) in enumerate(plan(refs[:nb], *_place())):
            copy = pltpu.make_async_remote_copy(src_ref=src, dst_ref=dst, send_sem=send_sems.at[k],
                                                recv_sem=recv_sems.at[k], device_id=dev, device_id_type=MESH)
            copy.wait_send()
            copy.wait_recv()

    outs = pl.pallas_call(
        body, name=name, out_shape=tuple(pltpu.HBM(b.shape, b.dtype) for b in bufs),
        in_specs=[HBM_SPEC] * nb + [SEM_SPEC, SEM_SPEC, ANY_SPEC], out_specs=tuple([HBM_SPEC] * nb),
        input_output_aliases={i: i for i in range(nb)},
        compiler_params=pltpu.CompilerParams(has_side_effects=EFFECT),
    )(*bufs, *sems, _Order.last)
    _Order.last = outs[0]
    return list(outs)


def _xfer_hand_over(name, sems, bufs, plan, next_plans):
    nb, n_next = len(bufs), len(next_plans)

    def body(*refs):
        send_sems, recv_sems = refs[nb], refs[nb + 1]
        outs = refs[nb + 3:]
        place = _place()
        for k, (src, dst, dev) in enumerate(plan(refs[:nb], *place)):
            copy = pltpu.make_async_remote_copy(src_ref=src, dst_ref=dst, send_sem=send_sems.at[k],
                                                recv_sem=recv_sems.at[k], device_id=dev, device_id_type=MESH)
            copy.wait_send()
            copy.wait_recv()
        for p, (next_plan, _) in enumerate(next_plans):
            for k, (src, dst, dev) in enumerate(next_plan(refs[:nb], *place)):
                pltpu.make_async_remote_copy(src_ref=src, dst_ref=dst, send_sem=outs[2 * p].at[k],
                                             recv_sem=outs[2 * p + 1].at[k], device_id=dev, device_id_type=MESH).start()
        outs[-1][...] = jnp.zeros_like(outs[-1])

    sem_shapes = [pltpu.SemaphoreType.DMA((copies,)) for _, copies in next_plans for _ in range(2)]
    outs = pl.pallas_call(
        body, name=name,
        out_shape=(*sem_shapes, *[pltpu.HBM(b.shape, b.dtype) for b in bufs], jax.ShapeDtypeStruct((8, LANES), F32)),
        in_specs=[HBM_SPEC] * nb + [SEM_SPEC, SEM_SPEC, ANY_SPEC],
        out_specs=(*[SEM_SPEC] * (2 * n_next), *[HBM_SPEC] * nb, pl.BlockSpec(memory_space=pltpu.VMEM)),
        input_output_aliases={i: 2 * n_next + i for i in range(nb)},
        compiler_params=pltpu.CompilerParams(has_side_effects=EFFECT),
    )(*bufs, *sems, _Order.last)
    _Order.last = outs[-1]
    return [(outs[2 * p], outs[2 * p + 1]) for p in range(n_next)], list(outs[2 * n_next:2 * n_next + nb])


def _half(ref, c, axis):
    rows = ref.shape[axis] // 2
    return pl.ds(c * rows, rows)


def _plan_weights_ici(n):
    def plan(refs, x, y, c, chips):
        out = []
        for w in range(n):
            region = refs[w].at[2 * x + y, _half(refs[w], c, 1), :]
            out += [(region, region, (*chip, c)) for chip in chips]
        return out
    return plan


def _plan_weights_ring(n):
    def plan(refs, x, y, c, chips):
        out = []
        for w in range(n):
            region = refs[w].at[2 * x + y, _half(refs[w], c, 1), :]
            out += [(region, region, (*chip, c)) for chip in chips[:2]]
        return out
    return plan


def _plan_weights_relay(n):
    def plan(refs, x, y, c, chips):
        out = []
        for w in range(n):
            quarter_rows = refs[w].shape[1] // 4
            upper = refs[w].at[2 * x + (1 - y), pl.ds(2 * c * quarter_rows, quarter_rows), :]
            lower = refs[w].at[2 * (1 - x) + y, pl.ds((2 * c + 1) * quarter_rows, quarter_rows), :]
            out += [(upper, upper, (1 - x, y, c)), (lower, lower, (x, 1 - y, c))]
        return out
    return plan


def _plan_weights_d2d(n, which=slice(0, 3)):
    def plan(refs, x, y, c, chips):
        out = []
        for w in range(n):
            rows = _half(refs[w], c, 1)
            for chip in chips[which]:
                region = refs[w].at[2 * chip[0] + chip[1], rows, :]
                out.append((region, region, (x, y, 1 - c)))
        return out
    return plan


def _plan_grads_d2d(n):
    def plan(refs, x, y, c, chips):
        return [(refs[w].at[:, _half(refs[w], 1 - c, 1), :], refs[n + w], (x, y, 1 - c)) for w in range(n)]
    return plan


def _plan_grads_ici(n):
    def plan(refs, x, y, c, chips):
        out = []
        for w in range(n):
            out += [(refs[w].at[2 * chip[0] + chip[1]], refs[n + w].at[2 * x + y], (*chip, c)) for chip in chips]
        return out
    return plan


def _plan_final_d2d(n):
    def plan(refs, x, y, c, chips):
        out = []
        for w in range(n):
            region = refs[w].at[_half(refs[w], c, 0), :]
            out.append((region, region, (x, y, 1 - c)))
        return out
    return plan


def _stream_blocks(hr, cols):
    bc = cols if cols <= 4096 else _pick(cols, 4096)
    return _pick_rows(hr, max(16, (768 * 1024) // bc)), bc


def _pre_reduce(g, landed, place, name):
    _, rows, cols = g.shape
    hr = rows // 2
    rb, bc = _stream_blocks(hr, cols)
    nrb = hr // rb

    def body(place_ref, g_ref, l_ref, o_ref):
        o_ref[...] = (g_ref[...].astype(F32) + l_ref[...].astype(F32)).astype(o_ref.dtype)

    return _pcall(
        body, name=name, num_scalar_prefetch=1, grid=(N_CHIPS, nrb, cols // bc),
        in_specs=[pl.BlockSpec((None, rb, bc), lambda j, i, k, p: (j, p[1] * nrb + i, k)),
                  pl.BlockSpec((None, rb, bc), lambda j, i, k, p: (j, i, k))],
        out_specs=pl.BlockSpec((None, rb, bc), lambda j, i, k, p: (j, i, k)),
        out_shape=jax.ShapeDtypeStruct((N_CHIPS, hr, cols), g.dtype),
        compiler_params=_params("parallel", "parallel", "parallel"),
    )(place, g, landed)


def _sum_slots(mine, landed, place, name):
    _, hr, cols = mine.shape
    rb, bc = _stream_blocks(hr, cols)
    rb = _pick_rows(hr, max(16, rb // 2))
    nrb = hr // rb

    def body(place_ref, m_ref, l_ref, o_ref):
        chip = place_ref[0]
        own = m_ref[...].astype(F32)
        total = jnp.where(chip == 0, own, l_ref[0].astype(F32))
        for j in range(1, N_CHIPS):
            total = total + jnp.where(chip == j, own, l_ref[j].astype(F32))
        o_ref[...] = total

    return _pcall(
        body, name=name, num_scalar_prefetch=1, grid=(nrb, cols // bc),
        in_specs=[pl.BlockSpec((None, rb, bc), lambda i, k, p: (p[0], i, k)),
                  pl.BlockSpec((N_CHIPS, rb, bc), lambda i, k, p: (0, i, k))],
        out_specs=pl.BlockSpec((rb, bc), lambda i, k, p: (p[1] * nrb + i, k)),
        out_shape=jax.ShapeDtypeStruct((2 * hr, cols), F32),
        compiler_params=_params("parallel", "parallel"),
    )(place, mine, landed)


class _WeightGather:
    @staticmethod
    def zones(quarters, chip):
        return [lax.dynamic_update_slice(lax.empty((N_CHIPS,) + q.shape, BF16), q.astype(BF16)[None], (chip, 0, 0))
                for q in quarters]

    def __init__(self, tag, zones, ring=False):
        self.tag, self.n, self.ring = tag, len(zones), ring
        self.plan = _plan_weights_ring(self.n) if ring else _plan_weights_ici(self.n)
        self.sems, self.bufs = _xfer_start("wici_start_" + tag, zones, self.plan, (2 if ring else 3) * self.n,
                                           after_last=True)

    def relay(self):
        plan = _plan_weights_relay(self.n)
        (self.sems,), self.bufs = _xfer_hand_over("wrel_start_" + self.tag, self.sems, self.bufs, self.plan,
                                                   [(plan, 2 * self.n)])
        self.plan = plan

    def pass_on(self):
        plan = _plan_weights_d2d(self.n)
        (self.sems,), self.bufs = _xfer_hand_over("wd2d_start_" + self.tag, self.sems, self.bufs, self.plan,
                                                   [(plan, 3 * self.n)])
        self.plan = plan

    def done(self):
        return _xfer_wait("wd2d_wait_" + self.tag, self.sems, self.bufs, self.plan)

    def relay_and_pass_near(self):
        plan, self.near_plan = _plan_weights_relay(self.n), _plan_weights_d2d(self.n, slice(0, 2))
        (self.sems, self.near_sems), self.bufs = _xfer_hand_over(
            "wrel_start_" + self.tag, self.sems, self.bufs, self.plan, [(plan, 2 * self.n), (self.near_plan, 2 * self.n)])
        self.plan = plan

    def near_done(self):
        self.bufs = _xfer_wait("wnear_wait_" + self.tag, self.near_sems, self.bufs, self.near_plan)
        return self.bufs

    def pass_far(self):
        plan = _plan_weights_d2d(self.n, slice(2, 3))
        (sems,), bufs = _xfer_hand_over("wfar_start_" + self.tag, self.sems, self.bufs, self.plan, [(plan, self.n)])
        return _xfer_wait("wfar_wait_" + self.tag, sems, bufs, plan)


class _GradReduce:
    def __init__(self, tag, grads):
        self.tag, self.n = tag, len(grads)
        zones = [lax.empty((N_CHIPS, g.shape[1] // 2, g.shape[2]), g.dtype) for g in grads]
        self.plan = _plan_grads_d2d(self.n)
        self.sems, self.bufs = _xfer_start("gd2d_start_" + tag, list(grads) + zones, self.plan, self.n)

    def pair(self, place):
        n = self.n
        bufs = _xfer_wait("gd2d_wait_" + self.tag, self.sems, self.bufs, self.plan)
        self.halves = [_pre_reduce(bufs[w], bufs[n + w], place, f"pre_reduce_{self.tag}{w}") for w in range(n)]

    def cross(self, after_last=False):
        zones = [lax.empty(h.shape, h.dtype) for h in self.halves]
        self.plan = _plan_grads_ici(self.n)
        self.sems, self.bufs = _xfer_start("gici_start_" + self.tag, self.halves + zones, self.plan, 3 * self.n,
                                           after_last=after_last)

    def step(self, place):
        self.pair(place)
        self.cross()

    def join(self, place):
        n = self.n
        bufs = _xfer_wait("gici_wait_" + self.tag, self.sems, self.bufs, self.plan)
        sums = [_sum_slots(bufs[w], bufs[n + w], place, f"sum_slots_{self.tag}{w}") for w in range(n)]
        self.plan = _plan_final_d2d(n)
        self.sems, self.bufs = _xfer_start("gfin_start_" + self.tag, sums, self.plan, n)

    def done(self):
        return _xfer_wait("gfin_wait_" + self.tag, self.sems, self.bufs, self.plan)


def _ada_fwd(c_all, w_q, b_q):
    d, n = w_q.shape
    bn = _pick(n, 512)

    def body(c_ref, w_ref, b_ref, o_ref):
        cv = c_ref[...]
        act = cv * _sigmoid(cv)
        o_ref[...] = _dot(act, w_ref[...], NN, lax.Precision.HIGHEST) + b_ref[...]

    return _pcall(
        body, name="ada_fwd", grid=(n // bn,),
        in_specs=[pl.BlockSpec((N_DEV, d), lambda j: (0, 0)), pl.BlockSpec((d, bn), lambda j: (0, j)),
                  pl.BlockSpec((1, bn), lambda j: (0, j))],
        out_specs=pl.BlockSpec((N_DEV, bn), lambda j: (0, j)),
        out_shape=jax.ShapeDtypeStruct((N_DEV, n), F32), compiler_params=_params("parallel"),
    )(c_all, w_q, b_q)


def _row_spec(rb, width, col=0):
    return pl.BlockSpec((rb, width), lambda i, col=col: (i, col))


def _vec_spec(width, col=0):
    return pl.BlockSpec((1, width), lambda i, col=col: (0, col))


def _norm_mod_fwd(x, g, sc, sh, name, res=None, gt=None):
    s, d = x.shape
    rb = _pick(s, 256)
    has_res = res is not None

    def body(*refs):
        if has_res:
            x_ref, res_ref, gt_ref, g_ref, sc_ref, sh_ref, x1_ref, h_ref = refs
            xv = x_ref[...] + gt_ref[...] * res_ref[...]
            x1_ref[...] = xv
        else:
            x_ref, g_ref, sc_ref, sh_ref, h_ref = refs
            xv = x_ref[...]
        xh, _ = _rms(xv)
        h_ref[...] = (xh * g_ref[...] * (1.0 + sc_ref[...]) + sh_ref[...]).astype(BF16)

    row, vec = _row_spec(rb, d), _vec_spec(d)
    if has_res:
        ins, in_specs = (x, res, gt, g, sc, sh), [row, row, vec, vec, vec, vec]
        out_shape = [jax.ShapeDtypeStruct((s, d), F32), jax.ShapeDtypeStruct((s, d), BF16)]
        out_specs = [row, row]
    else:
        ins, in_specs = (x, g, sc, sh), [row, vec, vec, vec]
        out_shape, out_specs = jax.ShapeDtypeStruct((s, d), BF16), row
    return _pcall(body, name=name, grid=(s // rb,), in_specs=in_specs, out_specs=out_specs,
                          out_shape=out_shape, compiler_params=_params("parallel"))(*ins)


def _final_loss(x1, f, gt2, g_final, target):
    s, d = x1.shape
    rb = _pick(s, 256)

    def body(x1_ref, f_ref, gt_ref, g_ref, t_ref, dx_ref, df_ref, loss_ref, dg_ref, dgt_ref):
        first = pl.program_id(0) == 0
        fv, gt, gv = f_ref[...], gt_ref[...], g_ref[...]
        x2 = x1_ref[...] + gt * fv
        xh, r = _rms(x2)
        err = xh * gv - t_ref[...]
        blk = 0.5 * jnp.sum(jnp.sum(err * err, axis=1, keepdims=True), axis=0, keepdims=True) / d
        dy = err / d
        dxh = dy * gv
        dx = r * (dxh - xh * jnp.mean(dxh * xh, axis=-1, keepdims=True))
        dx_ref[...] = dx
        df_ref[...] = (dx * gt).astype(BF16)
        _accumulate(first, loss_ref, jnp.broadcast_to(blk, (1, LANES)))
        _accumulate(first, dg_ref, _colsum(dy * xh))
        _accumulate(first, dgt_ref, _colsum(dx * fv))

    row, vec = _row_spec(rb, d), _vec_spec(d)
    return _pcall(
        body, name="final_loss", grid=(s // rb,), in_specs=[row, row, vec, vec, row],
        out_specs=[row, row, _vec_spec(LANES), vec, vec],
        out_shape=[jax.ShapeDtypeStruct((s, d), F32), jax.ShapeDtypeStruct((s, d), BF16),
                   jax.ShapeDtypeStruct((1, LANES), F32), jax.ShapeDtypeStruct((1, d), F32),
                   jax.ShapeDtypeStruct((1, d), F32)],
        compiler_params=_params("arbitrary"),
    )(x1, f, gt2, g_final, target)


def _norm_mod_bwd(dh, xin, dres, g, sc, name, branch=None, gt=None):
    s, d = xin.shape
    rb = _pick(s, 256)
    has_branch = branch is not None

    def body(*refs):
        if has_branch:
            dh_ref, x_ref, dres_ref, g_ref, sc_ref, br_ref, gt_ref, dx_ref, dbr_ref, dsh_ref, dsc_ref, dg_ref, dgt_ref = refs
        else:
            dh_ref, x_ref, dres_ref, g_ref, sc_ref, dx_ref, dsh_ref, dsc_ref, dg_ref = refs
        first = pl.program_id(0) == 0
        gv = g_ref[...]
        xh, r = _rms(x_ref[...])
        dhv = dh_ref[...]
        dn = dhv * (1.0 + sc_ref[...])
        dxh = dn * gv
        dx = dres_ref[...] + r * (dxh - xh * jnp.mean(dxh * xh, axis=-1, keepdims=True))
        dx_ref[...] = dx
        _accumulate(first, dsh_ref, _colsum(dhv))
        _accumulate(first, dsc_ref, _colsum(dhv * xh * gv))
        _accumulate(first, dg_ref, _colsum(dn * xh))
        if has_branch:
            dbr_ref[...] = (dx * gt_ref[...]).astype(BF16)
            _accumulate(first, dgt_ref, _colsum(dx * br_ref[...]))

    row, vec = _row_spec(rb, d), _vec_spec(d)
    vec_shape = jax.ShapeDtypeStruct((1, d), F32)
    if has_branch:
        ins, in_specs = (dh, xin, dres, g, sc, branch, gt), [row, row, row, vec, vec, row, vec]
        out_specs = [row, row, vec, vec, vec, vec]
        out_shape = [jax.ShapeDtypeStruct((s, d), F32), jax.ShapeDtypeStruct((s, d), BF16)] + [vec_shape] * 4
    else:
        ins, in_specs = (dh, xin, dres, g, sc), [row, row, row, vec, vec]
        out_specs = [row, vec, vec, vec]
        out_shape = [jax.ShapeDtypeStruct((s, d), F32)] + [vec_shape] * 3
    return _pcall(body, name=name, grid=(s // rb,), in_specs=in_specs, out_specs=out_specs,
                          out_shape=out_shape, compiler_params=_params("arbitrary"))(*ins)


def _branch_merge(ya, yb, w_bg, w_bh, z, b_gate, gate_col, bm=512, bn=1024):
    s, gw = ya.shape
    d = w_bg.shape[1]
    bm, bn = _pick(s, bm), _pick(d, bn)
    nj = d // bn

    def body(ya_ref, yb_ref, wa_ref, wb_ref, za_ref, zb_ref, ba_ref, bb_ref, pa_ref, pb_ref, y_ref):
        pa = _dot(ya_ref[...], wa_ref[...])
        pb = _dot(yb_ref[...], wb_ref[...])
        pa_ref[...] = pa.astype(BF16)
        pb_ref[...] = pb.astype(BF16)
        ga = _sigmoid(za_ref[...] + ba_ref[...])
        gb = _sigmoid(zb_ref[...] + bb_ref[...])
        y_ref[...] = (ga * pa + gb * pb).astype(BF16)

    act = pl.BlockSpec((bm, gw), lambda j, i: (i, 0))
    wgt = pl.BlockSpec((gw, bn), lambda j, i: (0, j))
    tile = pl.BlockSpec((bm, bn), lambda j, i: (i, j))
    shape = jax.ShapeDtypeStruct((s, d), BF16)
    return _pcall(
        body, name="branch_merge", grid=(nj, s // bm),
        in_specs=[act, act, wgt, wgt,
                  pl.BlockSpec((bm, bn), lambda j, i: (i, gate_col * nj + j)),
                  pl.BlockSpec((bm, bn), lambda j, i: (i, (gate_col + 1) * nj + j)),
                  pl.BlockSpec((1, bn), lambda j, i: (0, j)), pl.BlockSpec((1, bn), lambda j, i: (0, nj + j))],
        out_specs=[tile, tile, tile], out_shape=[shape, shape, shape], compiler_params=_params("parallel", "parallel"),
    )(ya, yb, w_bg, w_bh, z, z, b_gate, b_gate)


def _branch_bwd(dpa, dpb, w_bg, w_bh, bm=512):
    s, d = dpa.shape
    gw = w_bg.shape[0]
    bm = _pick(s, bm)

    def body(da_ref, db_ref, wa_ref, wb_ref, oa_ref, ob_ref):
        oa_ref[...] = _dot(da_ref[...], wa_ref[...], NT)
        ob_ref[...] = _dot(db_ref[...], wb_ref[...], NT)

    act = pl.BlockSpec((bm, d), lambda i: (i, 0))
    wgt = pl.BlockSpec((gw, d), lambda i: (0, 0))
    out = pl.BlockSpec((bm, gw), lambda i: (i, 0))
    shape = jax.ShapeDtypeStruct((s, gw), F32)
    return _pcall(body, name="branch_bwd", grid=(s // bm,), in_specs=[act, act, wgt, wgt], out_specs=[out, out],
                  out_shape=[shape, shape], compiler_params=_params("parallel"))(dpa, dpb, w_bg, w_bh)


def _out_bwd(dyo, w_out, pa, pb, z, b_gate, gate_col, bm=512, bn=1024):
    s, d = dyo.shape
    bm, bn = _pick(s, bm), _pick(d, bn)
    nj = d // bn

    def body(dyo_ref, w_ref, pa_ref, pb_ref, za_ref, zb_ref, ba_ref, bb_ref,
             dpa_ref, dpb_ref, dza_ref, dzb_ref, dba_ref, dbb_ref):
        first = pl.program_id(1) == 0
        dyv = _dot(dyo_ref[...], w_ref[...], NT)
        ga = _sigmoid(za_ref[...] + ba_ref[...])
        gb = _sigmoid(zb_ref[...] + bb_ref[...])
        dpa_ref[...] = (dyv * ga).astype(BF16)
        dpb_ref[...] = (dyv * gb).astype(BF16)
        dga = dyv * pa_ref[...].astype(F32) * ga * (1.0 - ga)
        dgb = dyv * pb_ref[...].astype(F32) * gb * (1.0 - gb)
        dza_ref[...] = dga.astype(BF16)
        dzb_ref[...] = dgb.astype(BF16)
        _accumulate(first, dba_ref, _colsum(dga))
        _accumulate(first, dbb_ref, _colsum(dgb))

    tile = pl.BlockSpec((bm, bn), lambda j, i: (i, j))
    vec = pl.BlockSpec((1, bn), lambda j, i: (0, j))
    act, vec_shape = jax.ShapeDtypeStruct((s, d), BF16), jax.ShapeDtypeStruct((1, d), F32)
    return _pcall(
        body, name="out_bwd", grid=(nj, s // bm),
        in_specs=[pl.BlockSpec((bm, d), lambda j, i: (i, 0)), pl.BlockSpec((bn, d), lambda j, i: (j, 0)), tile, tile,
                  pl.BlockSpec((bm, bn), lambda j, i: (i, gate_col * nj + j)),
                  pl.BlockSpec((bm, bn), lambda j, i: (i, (gate_col + 1) * nj + j)),
                  vec, pl.BlockSpec((1, bn), lambda j, i: (0, nj + j))],
        out_specs=[tile, tile, tile, tile, vec, vec], out_shape=[act, act, act, act, vec_shape, vec_shape],
        compiler_params=_params("parallel", "arbitrary"),
    )(dyo, w_out, pa, pb, z, z, b_gate, b_gate)


def _ffn_in(h, w_fi, bm=512, bn=1408):
    s, d = h.shape
    per = w_fi.shape[2]
    ff = 2 * per
    bm, bn = _pick(s, bm), _pick(per, bn)
    per_blocks = per // bn

    def body(h_ref, wa_ref, wu_ref, a_ref, u_ref, hf_ref):
        hv = h_ref[...]
        a = _dot(hv, wa_ref[...])
        up = _dot(hv, wu_ref[...])
        a_ref[...] = a.astype(BF16)
        u_ref[...] = up.astype(BF16)
        hf_ref[...] = (a * _sigmoid(a) * up).astype(BF16)

    out = pl.BlockSpec((bm, bn), lambda j, i: (i, j))
    shape = jax.ShapeDtypeStruct((s, ff), BF16)
    return _pcall(
        body, name="ffn_in", grid=(ff // bn, s // bm),
        in_specs=[pl.BlockSpec((bm, d), lambda j, i: (i, 0)),
                  pl.BlockSpec((None, d, bn), lambda j, i: (j // per_blocks, 0, j % per_blocks)),
                  pl.BlockSpec((None, d, bn), lambda j, i: (2 + j // per_blocks, 0, j % per_blocks))],
        out_specs=[out, out, out], out_shape=[shape, shape, shape],
        compiler_params=_params("parallel", "parallel"),
    )(h, w_fi, w_fi)


def _ffn_out_bwd(dffn, w_fo, a_act, up_act, bm=512, bn=1408):
    s, d = dffn.shape
    ff = w_fo.shape[0]
    bm, bn = _pick(s, bm), _pick(ff, bn)

    def body(d_ref, w_ref, a_ref, u_ref, da_ref, du_ref):
        dhf = _dot(d_ref[...], w_ref[...], NT)
        a = a_ref[...].astype(F32)
        sa = _sigmoid(a)
        da_ref[...] = (dhf * u_ref[...].astype(F32) * sa * (1.0 + a * (1.0 - sa))).astype(BF16)
        du_ref[...] = (dhf * a * sa).astype(BF16)

    tile = pl.BlockSpec((bm, bn), lambda j, i: (i, j))
    shape = jax.ShapeDtypeStruct((s, ff), BF16)
    return _pcall(
        body, name="ffn_out_bwd", grid=(ff // bn, s // bm),
        in_specs=[pl.BlockSpec((bm, d), lambda j, i: (i, 0)), pl.BlockSpec((bn, d), lambda j, i: (j, 0)), tile, tile],
        out_specs=[tile, tile], out_shape=[shape, shape], compiler_params=_params("parallel", "parallel"),
    )(dffn, w_fo, a_act, up_act)


def _tril(n):
    return lax.broadcasted_iota(jnp.int32, (n, n), 0) >= lax.broadcasted_iota(jnp.int32, (n, n), 1)


def _gmlp_norm(v, ln_g, ln_b):
    gv = _gelu(v)
    mu = jnp.mean(gv, axis=-1, keepdims=True)
    cen = gv - mu
    rs = lax.rsqrt(jnp.mean(cen * cen, axis=-1, keepdims=True) + EPS)
    xh = cen * rs
    return xh, rs, xh * ln_g + ln_b


def _gmlp_fwd(z, ln_g, ln_b, ws, bs_t):
    s = z.shape[0]
    gw = ln_g.shape[1]
    groups, chunk, _ = ws.shape

    def body(u_ref, v_ref, lg_ref, lb_ref, ws_ref, bs_ref, ya_ref):
        gu = _gelu(u_ref[...])
        _, _, vn = _gmlp_norm(v_ref[...], lg_ref[...], lb_ref[...])
        mask = _tril(chunk)
        for g in range(groups):
            cols = slice(g * LANES, (g + 1) * LANES)
            wm = jnp.where(mask, ws_ref[g], 0.0).astype(BF16)
            sg = _dot(wm, vn[:, cols].astype(BF16)) + bs_ref[:, g:g + 1]
            ya_ref[:, cols] = (gu[:, cols] * sg).astype(BF16)

    return _pcall(
        body, name="gmlp_fwd", grid=(s // chunk,),
        in_specs=[_row_spec(chunk, gw, 0), _row_spec(chunk, gw, 1), _vec_spec(gw), _vec_spec(gw),
                  pl.BlockSpec((groups, chunk, chunk), lambda i: (0, 0, 0)), pl.BlockSpec((chunk, LANES), lambda i: (0, 0))],
        out_specs=_row_spec(chunk, gw), out_shape=jax.ShapeDtypeStruct((s, gw), BF16),
        compiler_params=_params("parallel"),
    )(z, z, ln_g, ln_b, ws, bs_t)


def _gmlp_bwd(dya, z, ln_g, ln_b, ws, bs_t):
    s = z.shape[0]
    gw = ln_g.shape[1]
    groups, chunk, _ = ws.shape

    def body(dya_ref, u_ref, v_ref, lg_ref, lb_ref, ws_ref, bs_ref, duv_ref, dws_ref, dbs_ref, dlg_ref, dlb_ref, dvn_ref):
        first = pl.program_id(0) == 0
        u, v, lg = u_ref[...], v_ref[...], lg_ref[...]
        gu = _gelu(u)
        xh, rs, vn = _gmlp_norm(v, lg, lb_ref[...])
        dyav = dya_ref[...]
        mask = _tril(chunk)
        lane = lax.broadcasted_iota(jnp.int32, (chunk, LANES), 1)
        dbs = jnp.zeros((chunk, LANES), F32)
        for g in range(groups):
            cols = slice(g * LANES, (g + 1) * LANES)
            wm = jnp.where(mask, ws_ref[g], 0.0).astype(BF16)
            vg = vn[:, cols].astype(BF16)
            sg = _dot(wm, vg) + bs_ref[:, g:g + 1]
            ds = dyav[:, cols] * gu[:, cols]
            duv_ref[:, cols] = (dyav[:, cols] * sg * _gelu_grad(u[:, cols])).astype(BF16)
            dsb = ds.astype(BF16)
            _accumulate(first, dws_ref.at[g], jnp.where(mask, _dot(dsb, vg, NT), 0.0))
            dbs = dbs + jnp.where(lane == g, jnp.sum(ds, axis=-1, keepdims=True), 0.0)
            dvn_ref[:, cols] = _dot(wm, dsb, TN)
        dvn = dvn_ref[...]
        _accumulate(first, dbs_ref, dbs)
        _accumulate(first, dlb_ref, _colsum(dvn))
        _accumulate(first, dlg_ref, _colsum(dvn * xh))
        dxh = dvn * lg
        dgv = rs * (dxh - jnp.mean(dxh, axis=-1, keepdims=True) - xh * jnp.mean(dxh * xh, axis=-1, keepdims=True))
        duv_ref[:, gw:] = (dgv * _gelu_grad(v)).astype(BF16)

    return _pcall(
        body, name="gmlp_bwd", grid=(s // chunk,),
        in_specs=[_row_spec(chunk, gw), _row_spec(chunk, gw, 0), _row_spec(chunk, gw, 1), _vec_spec(gw), _vec_spec(gw),
                  pl.BlockSpec((groups, chunk, chunk), lambda i: (0, 0, 0)), pl.BlockSpec((chunk, LANES), lambda i: (0, 0))],
        out_specs=[_row_spec(chunk, 2 * gw), pl.BlockSpec((groups, chunk, chunk), lambda i: (0, 0, 0)),
                   pl.BlockSpec((chunk, LANES), lambda i: (0, 0)), _vec_spec(gw), _vec_spec(gw)],
        out_shape=[jax.ShapeDtypeStruct((s, 2 * gw), BF16), jax.ShapeDtypeStruct((groups, chunk, chunk), F32),
                   jax.ShapeDtypeStruct((chunk, LANES), F32), jax.ShapeDtypeStruct((1, gw), F32),
                   jax.ShapeDtypeStruct((1, gw), F32)],
        scratch_shapes=[pltpu.VMEM((chunk, gw), F32)],
        compiler_params=_params("arbitrary"),
    )(dya, z, z, ln_g, ln_b, ws, bs_t)


def _lower_bound(lb_ref):
    a0, a1 = lb_ref[0:1, :], lb_ref[1:2, :]
    mx = jnp.maximum(a0, a1)
    e0, e1 = jnp.exp(a0 - mx), jnp.exp(a1 - mx)
    return e0 / (e0 + e1)


def _sum_dot(mask, x):
    hi = x.astype(BF16)
    rest = x - hi.astype(F32)
    mid = rest.astype(BF16)
    low = (rest - mid.astype(F32)).astype(BF16)
    return _dot(mask, hi) + _dot(mask, mid) + _dot(mask, low)


def _ones_where(mask):
    return jnp.where(mask, 1.0, 0.0).astype(BF16)


def _hg_masks(rows, t):
    r = lax.broadcasted_iota(jnp.int32, (rows, rows), 0)
    c = lax.broadcasted_iota(jnp.int32, (rows, rows), 1)
    same = (r // t) == (c // t)
    incl = jnp.logical_and(same, c <= r)
    upto_mid = jnp.logical_and(same, (c % t) <= t // 2)
    rev = jnp.logical_and(same, c >= r)
    return same, incl, upto_mid, rev


def _hg_block(q, fp, lb, masks):
    rows = q.shape[0]
    same, incl, upto_mid, _ = masks
    sig = _sigmoid(fp)
    f = lb + (1.0 - lb) * sig
    k = 1.0 - f
    sq = _sigmoid(q)
    qa = q * sq
    stacked = jnp.concatenate([_ones_where(m) for m in (incl, same, upto_mid)], axis=0)
    sums = _sum_dot(stacked, jnp.log(f))
    b, b_last, b_mid = sums[:rows], sums[rows:2 * rows], sums[2 * rows:]
    e_q = jnp.exp(jnp.minimum(b - b_mid, EXP_CLAMP))
    e_k = jnp.exp(jnp.minimum(b_mid - b, EXP_CLAMP))
    e_in = jnp.exp(b)
    e_out = jnp.exp(b_last - b)
    return dict(sig=sig, f=f, k=k, sq=sq, qa=qa, e_last=jnp.exp(b_last), e_q=e_q, e_k=e_k, e_in=e_in, e_out=e_out,
                q_hat=(qa * e_q).astype(BF16), k_hat=(k * e_k).astype(BF16),
                q_in=(qa * e_in).astype(BF16), k_out=k * e_out)


def _hgrn_fwd(z, hg_lb, norm_g, q_col):
    s = z.shape[0]
    hw = norm_g.shape[1]
    heads = hw // LANES
    t = HG_CHUNK
    rows = min(HG_ROWS, s)
    per_step = rows // t
    hp = min(HG_HEADS_PER_STEP, heads)
    assert heads % hp == 0 and q_col % hp == 0, (heads, q_col)
    wide = hp * LANES

    def zspec(which):
        return pl.BlockSpec((rows, wide), lambda h, r, which=which: (r, (q_col + which * heads) // hp + h))

    def body(q_ref, f_ref, i_ref, g_ref, lb_ref, ng_ref, yb_ref, o_ref, st_out_ref, st_ref, e_last_ref, inter_ref):
        @pl.when(pl.program_id(1) == 0)
        def _():
            st_ref[...] = jnp.zeros_like(st_ref)

        masks = _hg_masks(rows, t)
        for hh in range(hp):
            cols = slice(hh * LANES, (hh + 1) * LANES)
            blk = _hg_block(q_ref[:, cols], f_ref[:, cols], _lower_bound(lb_ref.at[:, cols]), masks)
            iv = i_ref[:, cols].astype(BF16)
            q_in, k_out = blk["q_in"], blk["k_out"].astype(BF16)
            e_last_ref[hh] = blk["e_last"]
            attn = jnp.where(masks[1], _dot(blk["q_hat"], blk["k_hat"], NT), 0.0).astype(BF16)
            o = _dot(attn, iv)
            grown = [_dot(iv[j * t:(j + 1) * t], k_out[j * t:(j + 1) * t], TN) for j in range(per_step)]
            st = st_ref[hh]
            for j in range(per_step):
                st_out_ref[hh, j] = st
                inter_ref[hh, j * t:(j + 1) * t, :] = _dot(q_in[j * t:(j + 1) * t], st.astype(BF16), NT)
                st = st * e_last_ref[hh, j * t:j * t + 1, :] + grown[j]
            st_ref[hh] = st
            o = o + inter_ref[hh]
            o_ref[:, cols] = o
            og = g_ref[:, cols]
            on, _ = _rms(o)
            yb_ref[:, cols] = (on * ng_ref[:, cols] * (og * _sigmoid(og))).astype(BF16)

    out_row = pl.BlockSpec((rows, wide), lambda h, r: (r, h))
    return _pcall(
        body, name="hgrn_fwd", grid=(heads // hp, s // rows),
        in_specs=[zspec(0), zspec(1), zspec(2), zspec(3),
                  pl.BlockSpec((2, wide), lambda h, r: (0, h)), pl.BlockSpec((1, wide), lambda h, r: (0, h))],
        out_specs=[out_row, out_row, pl.BlockSpec((hp, per_step, LANES, LANES), lambda h, r: (h, r, 0, 0))],
        out_shape=[jax.ShapeDtypeStruct((s, hw), BF16), jax.ShapeDtypeStruct((s, hw), F32),
                   jax.ShapeDtypeStruct((heads, s // t, LANES, LANES), F32)],
        scratch_shapes=[pltpu.VMEM((hp, LANES, LANES), F32), pltpu.VMEM((hp, rows, LANES), F32),
                        pltpu.VMEM((hp, rows, LANES), F32)],
        compiler_params=_params("parallel", "arbitrary"),
    )(z, z, z, z, hg_lb, norm_g)


def _hgrn_bwd(dyb, z, o_raw, states, hg_lb, norm_g, q_col):
    s = z.shape[0]
    hw = norm_g.shape[1]
    heads = hw // LANES
    t = HG_CHUNK
    rows = min(HG_ROWS, s)
    per_step = rows // t
    n_steps = s // rows
    hp = min(HG_HEADS_PER_STEP, heads)
    assert heads % hp == 0 and q_col % hp == 0, (heads, q_col)
    wide = hp * LANES

    def zspec(which):
        return pl.BlockSpec((rows, wide), lambda h, r, which=which: (n_steps - 1 - r, (q_col + which * heads) // hp + h))

    def body(dyb_ref, q_ref, f_ref, i_ref, g_ref, o_ref, st_in_ref, lb_ref, ng_ref,
             dq_ref, df_ref, di_ref, dg_ref, dlb_ref, dng_ref, dst_ref, acc_lb_ref, acc_ng_ref,
             e_last_ref, dq_in_ref, dk_out_ref, di_inter_ref, carry_ref):
        step = pl.program_id(1)

        @pl.when(step == 0)
        def _():
            dst_ref[...] = jnp.zeros_like(dst_ref)
            acc_lb_ref[...] = jnp.zeros_like(acc_lb_ref)
            acc_ng_ref[...] = jnp.zeros_like(acc_ng_ref)

        masks = _hg_masks(rows, t)
        same, incl, _, rev = masks
        sum_mask = jnp.concatenate([_ones_where(rev), _ones_where(same)], axis=1)
        for hh in range(hp):
            cols = slice(hh * LANES, (hh + 1) * LANES)
            lb = _lower_bound(lb_ref.at[:, cols])
            ng = ng_ref[:, cols]
            q = q_ref[:, cols]
            blk = _hg_block(q, f_ref[:, cols], lb, masks)
            iv = i_ref[:, cols].astype(BF16)
            o, og, dy = o_ref[:, cols], g_ref[:, cols], dyb_ref[:, cols]
            so = _sigmoid(og)
            on, r = _rms(o)
            acc_ng_ref[:, cols] += _colsum(dy * on * (og * so))
            dg_ref[:, cols] = (dy * on * ng * so * (1.0 + og * (1.0 - so))).astype(BF16)
            don = dy * ng * (og * so)
            do = (r * (don - on * jnp.mean(don * on, axis=-1, keepdims=True))).astype(BF16)
            q_hat, k_hat, q_in, k_out = blk["q_hat"], blk["k_hat"], blk["q_in"], blk["k_out"]
            k_out_b = k_out.astype(BF16)
            attn = jnp.where(incl, _dot(q_hat, k_hat, NT), 0.0).astype(BF16)
            d_attn = jnp.where(incl, _dot(do, iv, NT), 0.0).astype(BF16)
            di_intra = _dot(attn, do, TN)
            dq_hat = _dot(d_attn, k_hat)
            dk_hat = _dot(d_attn, q_hat, TN)
            e_last_ref[hh] = blk["e_last"]
            grown = [_dot(do[j * t:(j + 1) * t], q_in[j * t:(j + 1) * t], TN) for j in range(per_step)]
            dst = dst_ref[hh]
            for j in reversed(range(per_step)):
                rs_ = slice(j * t, (j + 1) * t)
                e_last = e_last_ref[hh, j * t:j * t + 1, :]
                st_prev, dst_b = st_in_ref[hh, j], dst.astype(BF16)
                dq_in_ref[hh, rs_, :] = _dot(do[rs_], st_prev.astype(BF16))
                dk_out_ref[hh, rs_, :] = _dot(iv[rs_], dst_b)
                di_inter_ref[hh, rs_, :] = _dot(k_out_b[rs_], dst_b, NT)
                carry_ref[hh, rs_, :] = jnp.broadcast_to(e_last * _colsum(st_prev * dst), (t, LANES))
                dst = dst * e_last + grown[j]
            dst_ref[hh] = dst
            di_ref[:, cols] = (di_intra + di_inter_ref[hh]).astype(BF16)
            dk_out = dk_out_ref[hh]
            dqa = dq_in_ref[hh] * blk["e_in"] + dq_hat * blk["e_q"]
            dk = dk_out * blk["e_out"] + dk_hat * blk["e_k"]
            db = blk["qa"] * dqa - blk["k"] * dk
            dlf = _sum_dot(sum_mask, jnp.concatenate([db, dk_out * k_out], axis=0)) + carry_ref[hh]
            dfv = dlf / blk["f"] - dk
            sig, sq = blk["sig"], blk["sq"]
            df_ref[:, cols] = (dfv * (1.0 - lb) * sig * (1.0 - sig)).astype(BF16)
            acc_lb_ref[:, cols] += _colsum(dfv * (1.0 - sig))
            dq_ref[:, cols] = (dqa * sq * (1.0 + q * (1.0 - sq))).astype(BF16)

        @pl.when(step == n_steps - 1)
        def _():
            lb = _lower_bound(lb_ref)
            d0 = acc_lb_ref[...] * lb * (1.0 - lb)
            dlb_ref[0:1, :] = d0
            dlb_ref[1:2, :] = -d0
            dng_ref[...] = acc_ng_ref[...]

    rev_row = pl.BlockSpec((rows, wide), lambda h, r: (n_steps - 1 - r, h))
    piece = jax.ShapeDtypeStruct((s, hw), BF16)
    return _pcall(
        body, name="hgrn_bwd", grid=(heads // hp, n_steps),
        in_specs=[rev_row, zspec(0), zspec(1), zspec(2), zspec(3), rev_row,
                  pl.BlockSpec((hp, per_step, LANES, LANES), lambda h, r: (h, n_steps - 1 - r, 0, 0)),
                  pl.BlockSpec((2, wide), lambda h, r: (0, h)), pl.BlockSpec((1, wide), lambda h, r: (0, h))],
        out_specs=[rev_row, rev_row, rev_row, rev_row,
                   pl.BlockSpec((2, wide), lambda h, r: (0, h)), pl.BlockSpec((1, wide), lambda h, r: (0, h))],
        out_shape=[piece, piece, piece, piece, jax.ShapeDtypeStruct((2, hw), F32), jax.ShapeDtypeStruct((1, hw), F32)],
        scratch_shapes=[pltpu.VMEM((hp, LANES, LANES), F32), pltpu.VMEM((1, wide), F32), pltpu.VMEM((1, wide), F32)]
        + [pltpu.VMEM((hp, rows, LANES), F32)] * 5,
        compiler_params=_params("parallel", "arbitrary"),
    )(dyb, z, z, z, z, o_raw, states, hg_lb, norm_g)


def _adam_update(w, m, v, g):
    m2 = ADAM_B1 * m + (1.0 - ADAM_B1) * g
    v2 = ADAM_B2 * v + (1.0 - ADAM_B2) * (g * g)
    m_hat = m2 * (1.0 / (1.0 - ADAM_B1 ** ADAM_STEP))
    v_hat = v2 * (1.0 / (1.0 - ADAM_B2 ** ADAM_STEP))
    return -ADAM_LR * (m_hat / (jnp.sqrt(v_hat) + ADAM_EPS) + ADAM_WD * w), m2, v2


def _adamw(w, m, v, parts, name, outer=False):
    rows, cols = w.shape
    bc = cols if cols <= 4096 else _pick(cols, 4096)
    rb = _pick_rows(rows, max(8, (384 * 1024) // bc), mult=8)
    if outer and rb % LANES:
        rb = rows

    def body(w_ref, m_ref, v_ref, *refs):
        g_ref, d_ref, mo_ref, vo_ref = refs[-4:]
        if outer:
            cv = refs[0][...]
            g = _dot(cv * _sigmoid(cv), refs[1][...], TN, lax.Precision.HIGHEST)
        else:
            p_ref = refs[0]
            g = p_ref[0].astype(F32)
            for p in range(1, p_ref.shape[0]):
                g = g + p_ref[p].astype(F32)
        g_ref[...] = g
        d_ref[...], mo_ref[...], vo_ref[...] = _adam_update(w_ref[...], m_ref[...], v_ref[...], g)

    blk = pl.BlockSpec((rb, bc), lambda i, j: (i, j))
    out = jax.ShapeDtypeStruct((rows, cols), F32)
    if outer:
        grad_specs = [pl.BlockSpec((N_DEV, rb), lambda i, j: (0, i)), pl.BlockSpec((N_DEV, bc), lambda i, j: (0, j))]
        grad_ins = tuple(parts)
    else:
        grad_specs = [pl.BlockSpec((parts.shape[0], rb, bc), lambda i, j: (0, i, j))]
        grad_ins = (parts,)
    return _pcall(
        body, name=name, grid=(rows // rb, cols // bc), in_specs=[blk, blk, blk] + grad_specs,
        out_specs=[blk] * 4, out_shape=[out] * 4, compiler_params=_params("parallel", "parallel"),
    )(w, m, v, *grad_ins)


SMALL = ("b_ada", "norm1_g", "b_gate", "gmlp_ln_g", "gmlp_ln_b", "gmlp_ws", "gmlp_bs", "hg_lb", "hg_norm_g",
         "norm2_g", "final_norm_g")
BIG = ("w_in", "w_branch_gmlp", "w_branch_hg", "w_out", "w_ffn_in", "w_ffn_out")
WEIGHTS = ("w_ada", "b_ada", "norm1_g", "w_in", "b_gate", "gmlp_ln_g", "gmlp_ln_b", "gmlp_ws", "gmlp_bs", "hg_lb",
           "hg_norm_g", "w_branch_gmlp", "w_branch_hg", "w_out", "norm2_g", "w_ffn_in", "w_ffn_out", "final_norm_g")


def _pack(parts):
    return jnp.concatenate([p.reshape(-1, LANES) for p in parts], axis=0)


def _step(x, c, loss_target, w, m, v):
    s, d = x.shape[1], x.shape[2]
    gw = w["gmlp_ln_g"].shape[-1]
    hw = w["hg_norm_g"].shape[-1]
    x2d, tgt = x[0], loss_target[0]
    mx, my, mc = lax.axis_index("x"), lax.axis_index("y"), lax.axis_index("c")
    chip = 2 * mx + my
    dev = 2 * chip + mc
    q_col = 2 * gw // LANES
    gate_col = (2 * gw + 4 * hw) // d
    place = jnp.stack([chip, mc]).astype(jnp.int32)
    _Order.last = None

    c_all = _all_gather8(c.reshape(-1, LANES), "gather_c").reshape(N_DEV, d)
    gather_in = _WeightGather("in", _WeightGather.zones([w["w_in"][0]], chip), ring=True)
    zones_mix = _WeightGather.zones([w[n][0] for n in ("w_branch_gmlp", "w_branch_hg", "w_out")], chip)
    zones_fi = _WeightGather.zones([w["w_ffn_in"][0]], chip)
    zones_fo = _WeightGather.zones([w["w_ffn_out"][0]], chip)
    _Order.also = (*zones_mix, *zones_fi, *zones_fo)
    n_ada = w["w_ada"].shape[-1]
    b_ada_q = lax.dynamic_slice(w["b_ada"], (0, chip * n_ada), (1, n_ada))
    mod_q = _ada_fwd(c_all, w["w_ada"][0], b_ada_q)
    mod_all = _all_gather8(mod_q, "gather_mod")
    mod = lax.dynamic_index_in_dim(mod_all, dev, axis=1, keepdims=False)[::2].reshape(1, 6 * d)
    sh1, sc1, gt1, sh2, sc2, gt2 = [mod[:, i * d:(i + 1) * d] for i in range(6)]

    gather_in.relay_and_pass_near()
    gather_mix = _WeightGather("mix", zones_mix)
    gather_fi = _WeightGather("fi", zones_fi, ring=True)

    norm1_g, norm2_g, final_g = w["norm1_g"], w["norm2_g"], w["final_norm_g"].reshape(1, d)
    ln_g, ln_b = w["gmlp_ln_g"], w["gmlp_ln_b"]
    ws = w["gmlp_ws"][0]
    groups = ws.shape[0]
    bs_t = jnp.pad(w["gmlp_bs"][0].T, ((0, 0), (0, LANES - groups)))
    hg_lb, hg_ng, b_gate = w["hg_lb"], w["hg_norm_g"], w["b_gate"]

    h1 = _norm_mod_fwd(x2d, norm1_g, sc1, sh1, "norm1_fwd")
    order = jnp.stack([chip, 2 * (1 - mx) + my, 2 * mx + (1 - my), 2 * (1 - mx) + (1 - my)]).astype(jnp.int32)
    w_in, = gather_in.near_done()
    z = _matmul_quarters(h1, w_in, order, 0, 3, "mm_z_near")
    w_in, = gather_in.pass_far()
    z = _matmul_quarters(h1, w_in, order, 3, 1, "mm_z_far", prev=z)
    gather_mix.pass_on()
    ya = _gmlp_fwd(z, ln_g, ln_b, ws, bs_t)
    yb, o_raw, states = _hgrn_fwd(z, hg_lb, hg_ng, q_col)
    gather_fi.relay()
    gather_fo = _WeightGather("fo", zones_fo)
    w_bg, w_bh, w_out = gather_mix.done()
    w_bg, w_bh = [wq.transpose(1, 0, 2).reshape(wq.shape[1], -1) for wq in (w_bg, w_bh)]
    w_out = w_out.reshape(-1, w_out.shape[-1])
    pa, pb, y = _branch_merge(ya, yb, w_bg, w_bh, z, b_gate, gate_col)
    gather_fi.pass_on()
    yo = _matmul(y, w_out, mode="nn", name="mm_yo", out_dtype=F32)
    x1, h2 = _norm_mod_fwd(x2d, norm2_g, sc2, sh2, "norm2_fwd", res=yo, gt=gt1)
    w_fi, = gather_fi.done()
    a_act, up_act, hf = _ffn_in(h2, w_fi)
    gather_fo.pass_on()
    w_fo, = gather_fo.done()
    w_fo = w_fo.reshape(-1, w_fo.shape[-1])
    ffn = _matmul(hf, w_fo, mode="nn", name="mm_ffn", out_dtype=F32)
    dx2, dffn, loss_row, d_final_g, d_gt2 = _final_loss(x1, ffn, gt2, final_g, tgt)

    g_fo = _matmul(hf, dffn, mode="tn", name="mm_g_fo", out_dtype=BF16, bm=1408)
    daup = tuple(_ffn_out_bwd(dffn, w_fo, a_act, up_act))
    g_fi = _matmul(h2, daup, mode="tn", name="mm_g_fi", out_dtype=BF16, out_slots=True, bn=1408)
    red_ffn = _GradReduce("ffn", [g_fo.reshape(N_CHIPS, -1, g_fo.shape[-1]), g_fi])
    dh2 = _matmul(daup, w_fi, mode="nt", name="mm_dh2", out_dtype=F32, b_slots=True)
    red_ffn.step(place)
    dx1, dyo, d_sh2, d_sc2, d_norm2, d_gt1 = _norm_mod_bwd(dh2, x1, dx2, norm2_g, sc2, "norm2_bwd", branch=yo, gt=gt1)
    g_out = _matmul(y, dyo, mode="tn", name="mm_g_out", out_dtype=BF16)
    dpa, dpb, dz_ga, dz_gb, d_b_ga, d_b_gb = _out_bwd(dyo, w_out, pa, pb, z, b_gate, gate_col)
    g_bg = _matmul(ya, dpa, mode="tn", name="mm_g_bg", out_dtype=BF16, out_slots=True)
    g_bh = _matmul(yb, dpb, mode="tn", name="mm_g_bh", out_dtype=BF16, out_slots=True)
    red_mix = _GradReduce("mix", [g_out.reshape(N_CHIPS, -1, g_out.shape[-1]), g_bg, g_bh])
    dya, dyb = _branch_bwd(dpa, dpb, w_bg, w_bh)
    red_mix.step(place)
    dz_uv, d_ws, d_bs_t, d_ln_g, d_ln_b = _gmlp_bwd(dya, z, ln_g, ln_b, ws, bs_t)
    dz_q, dz_f, dz_i, dz_g, d_hg_lb, d_hg_ng = _hgrn_bwd(dyb, z, o_raw, states, hg_lb, hg_ng, q_col)
    dz = jnp.concatenate([dz_uv, dz_q, dz_f, dz_i, dz_g, dz_ga, dz_gb], axis=1)
    grad, delta, new_m, new_v = {}, {}, {}, {}

    def update(n, parts, outer=False):
        outs = _adamw(w[n][0], m[n][0], v[n][0], parts, "adamw_" + n, outer=outer)
        grad[n], delta[n], new_m[n], new_v[n] = [o[None] for o in outs]

    g_in = _matmul(h1, dz, mode="tn", name="mm_g_in", out_dtype=BF16, out_slots=True, bn=1280)
    red_in = _GradReduce("in", [g_in])
    red_ffn.join(place)
    red_mix.join(place)
    red_in.step(place)
    g_fo, g_fi = red_ffn.done()
    n = "w_ffn_in"
    dh1, rode = _matmul(dz, w_in, mode="nt", name="mm_dh1", out_dtype=F32, b_slots=True,
                        rider=([w[n][0], m[n][0], v[n][0], g_fi], lambda *blk: (blk[3], *_adam_update(*blk)), 4))
    grad[n], delta[n], new_m[n], new_v[n] = [o[None] for o in rode]
    grad_x, d_sh1, d_sc1, d_norm1 = _norm_mod_bwd(dh1, x2d, dx1, norm1_g, sc1, "norm1_bwd")
    update("w_ffn_out", g_fo[None])
    g_out, g_bg, g_bh = red_mix.done()
    update("w_out", g_out[None])
    update("w_branch_gmlp", g_bg[None])
    update("w_branch_hg", g_bh[None])

    d_mod = jnp.concatenate([d_sh1, d_sc1, d_gt1, d_sh2, d_sc2, d_gt2], axis=1)
    small_part = {"b_ada": d_mod, "norm1_g": d_norm1, "b_gate": jnp.concatenate([d_b_ga, d_b_gb], axis=1), "gmlp_ln_g": d_ln_g, "gmlp_ln_b": d_ln_b,
                  "gmlp_ws": d_ws, "gmlp_bs": d_bs_t[:, :groups].T, "hg_lb": d_hg_lb, "hg_norm_g": d_hg_ng,
                  "norm2_g": d_norm2, "final_norm_g": d_final_g}
    loss_rows = jnp.broadcast_to(loss_row, (8, LANES))
    small_all = _all_gather8(_pack([small_part[n] for n in SMALL] + [loss_rows]), "gather_small")
    d_mod_all = small_all[:, :6 * d // LANES].reshape(N_DEV, 6 * d)
    d_mod_q = lax.dynamic_slice(d_mod_all, (0, chip * n_ada), (N_DEV, n_ada))
    red_in.join(place)
    update("w_ada", (c_all, d_mod_q), outer=True)
    pad = [jnp.zeros_like(loss_rows)]
    outs = _adamw(_pack([w[n] for n in SMALL] + pad), _pack([m[n] for n in SMALL] + pad),
                  _pack([v[n] for n in SMALL] + pad), small_all, "adamw_small")
    update("w_in", red_in.done()[0][None])
    row = 0
    for n in SMALL:
        cnt = w[n].size // LANES
        for dst, o in zip((grad, delta, new_m, new_v), outs):
            dst[n] = o[row:row + cnt].reshape(w[n].shape)
        row += cnt

    loss = outs[0][row, 0]
    return (loss, grad_x[None], *[grad[n] for n in WEIGHTS], *[delta[n] for n in WEIGHTS],
            *[new_m[n] for n in WEIGHTS], *[new_v[n] for n in WEIGHTS])


def kernel(x, c, w_ada, b_ada, norm1_g, w_in, b_gate, gmlp_ln_g, gmlp_ln_b, gmlp_ws, gmlp_bs, hg_lb, hg_norm_g, w_branch_gmlp, w_branch_hg, w_out, norm2_g, w_ffn_in, w_ffn_out, final_norm_g, loss_target, m_w_ada, m_b_ada, m_norm1_g, m_w_in, m_b_gate, m_gmlp_ln_g, m_gmlp_ln_b, m_gmlp_ws, m_gmlp_bs, m_hg_lb, m_hg_norm_g, m_w_branch_gmlp, m_w_branch_hg, m_w_out, m_norm2_g, m_w_ffn_in, m_w_ffn_out, m_final_norm_g, v_w_ada, v_b_ada, v_norm1_g, v_w_in, v_b_gate, v_gmlp_ln_g, v_gmlp_ln_b, v_gmlp_ws, v_gmlp_bs, v_hg_lb, v_hg_norm_g, v_w_branch_gmlp, v_w_branch_hg, v_w_out, v_norm2_g, v_w_ffn_in, v_w_ffn_out, v_final_norm_g):
    w = dict(w_ada=w_ada, b_ada=b_ada, norm1_g=norm1_g, w_in=w_in, b_gate=b_gate, gmlp_ln_g=gmlp_ln_g,
             gmlp_ln_b=gmlp_ln_b, gmlp_ws=gmlp_ws, gmlp_bs=gmlp_bs, hg_lb=hg_lb, hg_norm_g=hg_norm_g,
             w_branch_gmlp=w_branch_gmlp, w_branch_hg=w_branch_hg, w_out=w_out, norm2_g=norm2_g,
             w_ffn_in=w_ffn_in, w_ffn_out=w_ffn_out, final_norm_g=final_norm_g)
    m = dict(w_ada=m_w_ada, b_ada=m_b_ada, norm1_g=m_norm1_g, w_in=m_w_in, b_gate=m_b_gate, gmlp_ln_g=m_gmlp_ln_g,
             gmlp_ln_b=m_gmlp_ln_b, gmlp_ws=m_gmlp_ws, gmlp_bs=m_gmlp_bs, hg_lb=m_hg_lb, hg_norm_g=m_hg_norm_g,
             w_branch_gmlp=m_w_branch_gmlp, w_branch_hg=m_w_branch_hg, w_out=m_w_out, norm2_g=m_norm2_g,
             w_ffn_in=m_w_ffn_in, w_ffn_out=m_w_ffn_out, final_norm_g=m_final_norm_g)
    v = dict(w_ada=v_w_ada, b_ada=v_b_ada, norm1_g=v_norm1_g, w_in=v_w_in, b_gate=v_b_gate, gmlp_ln_g=v_gmlp_ln_g,
             gmlp_ln_b=v_gmlp_ln_b, gmlp_ws=v_gmlp_ws, gmlp_bs=v_gmlp_bs, hg_lb=v_hg_lb, hg_norm_g=v_hg_norm_g,
             w_branch_gmlp=v_w_branch_gmlp, w_branch_hg=v_w_branch_hg, w_out=v_w_out, norm2_g=v_norm2_g,
             w_ffn_in=v_w_ffn_in, w_ffn_out=v_w_ffn_out, final_norm_g=v_final_norm_g)
    return _step(x, c, loss_target, w, m, v)
```

```python
import functools

import jax
import jax.numpy as jnp
from jax import lax
from jax.experimental import pallas as pl
from jax.experimental.pallas import tpu as pltpu

F32 = jnp.float32
BF16 = jnp.bfloat16
EPS = 1e-6
LANES = 128
N_CHIPS = 4
N_DEV = 8
VMEM_LIMIT_BYTES = 56 * 1024 * 1024
VMEM_RIDER_LIMIT_BYTES = 60 * 1024 * 1024
HG_CHUNK = 32
HG_ROWS = 256
HG_HEADS_PER_STEP = 8
EXP_CLAMP = 80.0
ADAM_LR, ADAM_B1, ADAM_B2, ADAM_EPS, ADAM_WD, ADAM_STEP = 0.001, 0.9, 0.999, 1e-08, 0.01, 10
MESH = pl.DeviceIdType.MESH

NN = (((1,), (0,)), ((), ()))
NT = (((1,), (1,)), ((), ()))
TN = (((0,), (0,)), ((), ()))


def _dot(a, b, dims=NN, precision=None):
    return lax.dot_general(a, b, dims, precision=precision, preferred_element_type=F32)


def _params(*semantics, vmem_limit_bytes=VMEM_LIMIT_BYTES):
    return pltpu.CompilerParams(dimension_semantics=semantics, vmem_limit_bytes=vmem_limit_bytes)


class _Order:
    last = None
    also = ()


def _pcall(body, *, in_specs, out_specs, grid=(), scratch_shapes=(), num_scalar_prefetch=0, **kw):
    def run(*ins):
        deps = (() if _Order.last is None else (_Order.last,)) + tuple(_Order.also)
        _Order.also = ()
        n_in, n_dep = len(ins), len(deps)

        def wrapped(*refs):
            body(*refs[:n_in], *refs[n_in + n_dep:])

        specs = list(in_specs) + [pl.BlockSpec(memory_space=pl.ANY)] * n_dep
        if num_scalar_prefetch:
            grid_spec = pltpu.PrefetchScalarGridSpec(
                num_scalar_prefetch=num_scalar_prefetch, grid=grid, in_specs=specs, out_specs=out_specs,
                scratch_shapes=scratch_shapes)
            outs = pl.pallas_call(wrapped, grid_spec=grid_spec, **kw)(*ins, *deps)
        else:
            outs = pl.pallas_call(wrapped, grid=grid, in_specs=specs, out_specs=out_specs,
                                  scratch_shapes=scratch_shapes, **kw)(*ins, *deps)
        _Order.last = jax.tree.leaves(outs)[0]
        return outs

    return run


def _pick_rows(dim, pref, mult=16):
    best = None
    for cand in range(mult, min(dim, pref) + 1, mult):
        if dim % cand == 0:
            best = cand
    assert best is not None, (dim, pref)
    return best


def _pick(dim, pref):
    if dim <= pref:
        return dim
    best = None
    for cand in range(LANES, pref + 1, LANES):
        if dim % cand == 0:
            best = cand
    assert best is not None, (dim, pref)
    return best


def _sigmoid(x):
    return 1.0 / (1.0 + jnp.exp(-x))


def _gelu(x):
    c = 0.7978845608028654
    return 0.5 * x * (1.0 + jnp.tanh(c * (x + 0.044715 * x * x * x)))


def _gelu_grad(x):
    c = 0.7978845608028654
    t = jnp.tanh(c * (x + 0.044715 * x * x * x))
    return 0.5 * (1.0 + t) + 0.5 * x * (1.0 - t * t) * c * (1.0 + 3.0 * 0.044715 * x * x)


def _rms(x):
    r = lax.rsqrt(jnp.mean(x * x, axis=-1, keepdims=True) + EPS)
    return x * r, r


def _colsum(x):
    return jnp.sum(x, axis=0, keepdims=True)


def _accumulate(first, ref, val):
    @pl.when(first)
    def _():
        ref[...] = val

    @pl.when(jnp.logical_not(first))
    def _():
        ref[...] += val


def _matmul(a, b, *, mode, name, out_dtype, b_slots=False, out_slots=False, bm=1024, bn=1024, bk=2816, rider=None):
    pair = a if isinstance(a, tuple) else b if isinstance(b, tuple) else None
    if mode == "nn":
        m, k = a.shape
        n = b.shape[2] * N_CHIPS if b_slots else b.shape[1]
        per = b.shape[2] if b_slots else n
    elif mode == "nt":
        m, k = (a[0].shape[0], 2 * a[0].shape[1]) if pair else a.shape
        n = b.shape[1] if b_slots else b.shape[0]
        per = b.shape[2] if b_slots else k
    else:
        k, m = a.shape
        n = 2 * b[0].shape[1] if pair else b.shape[1]
        per = n // N_CHIPS if out_slots else n
    bm = _pick(m, bm)
    if mode == "nt":
        bn, bk = _pick(n, bn), _pick(per, bk)
    else:
        bn, bk = _pick(per, bn), _pick(k, bk)
    nk = k // bk
    per_blocks = per // (bk if mode == "nt" else bn)
    dims = {"nn": NN, "nt": NT, "tn": TN}[mode]
    half = (nk if mode == "nt" else n // bn) // 2

    def product(a_ref, b_ref, o_ref, acc):
        part = _dot(a_ref[...], b_ref[...], dims)
        if nk == 1:
            o_ref[...] = part.astype(o_ref.dtype)
            return
        acc_ref, kk = acc[0], pl.program_id(2)

        @pl.when(kk == 0)
        def _():
            acc_ref[...] = part

        @pl.when(jnp.logical_and(kk > 0, kk < nk - 1))
        def _():
            acc_ref[...] += part

        @pl.when(kk == nk - 1)
        def _():
            o_ref[...] = (acc_ref[...] + part).astype(o_ref.dtype)

    n_ride, n_rode = (len(rider[0]), rider[2]) if rider else (0, 0)

    def body(*refs):
        if rider:
            ride_in, ride_out = refs[2:2 + n_ride], refs[3 + n_ride:3 + n_ride + n_rode]

            def ride(t, carry):
                rows = pl.ds(pl.multiple_of(t * 8, 8), 8)
                for ref, val in zip(ride_out, rider[1](*[r[rows, :] for r in ride_in])):
                    ref[rows, :] = val
                return carry

            lax.fori_loop(0, ride_in[0].shape[0] // 8, ride, 0)
        if not pair:
            return product(refs[0], refs[1], refs[2 + n_ride], refs[3 + n_ride + n_rode:])
        first = pl.program_id(2 if mode == "nt" else 1) < half
        x1, x2, y, o_ref, acc = refs[0], refs[1], refs[2], refs[3], refs[4:]

        @pl.when(first)
        def _():
            product(*((x1, y) if mode == "nt" else (y, x1)), o_ref, acc)

        @pl.when(jnp.logical_not(first))
        def _():
            product(*((x2, y) if mode == "nt" else (y, x2)), o_ref, acc)

    if mode == "nn":
        a_spec = pl.BlockSpec((bm, bk), lambda i, j, kk: (i, kk))
        if b_slots:
            b_spec = pl.BlockSpec((None, bk, bn), lambda i, j, kk: (j // per_blocks, kk, j % per_blocks))
        else:
            b_spec = pl.BlockSpec((bk, bn), lambda i, j, kk: (kk, j))
    elif mode == "nt":
        a_spec = pl.BlockSpec((bm, bk), lambda i, j, kk: (i, kk))
        if b_slots:
            b_spec = pl.BlockSpec((None, bn, bk), lambda i, j, kk: (kk // per_blocks, j, kk % per_blocks))
        else:
            b_spec = pl.BlockSpec((bn, bk), lambda i, j, kk: (j, kk))
    else:
        a_spec = pl.BlockSpec((bk, bm), lambda i, j, kk: (kk, i))
        b_spec = pl.BlockSpec((bk, bn), lambda i, j, kk: (kk, j))
    if out_slots:
        o_spec = pl.BlockSpec((None, bm, bn), lambda i, j, kk: (j // per_blocks, i, j % per_blocks))
        out_shape = jax.ShapeDtypeStruct((N_CHIPS, m, per), out_dtype)
    else:
        o_spec = pl.BlockSpec((bm, bn), lambda i, j, kk: (i, j))
        out_shape = jax.ShapeDtypeStruct((m, n), out_dtype)
    if not pair:
        ins, in_specs = (a, b), [a_spec, b_spec]
    elif mode == "nt":
        ins = (*a, b)
        in_specs = [pl.BlockSpec((bm, bk), lambda i, j, kk: (i, jnp.minimum(kk, half - 1))),
                    pl.BlockSpec((bm, bk), lambda i, j, kk: (i, jnp.maximum(kk - half, 0))), b_spec]
    else:
        ins = (*b, a)
        in_specs = [pl.BlockSpec((bk, bn), lambda i, j, kk: (kk, jnp.minimum(j, half - 1))),
                    pl.BlockSpec((bk, bn), lambda i, j, kk: (kk, jnp.maximum(j - half, 0))), a_spec]
    scratch = [pltpu.VMEM((bm, bn), F32)] if nk > 1 else []
    if not rider:
        return _pcall(
            body, name=name, grid=(m // bm, n // bn, nk), in_specs=in_specs, out_specs=o_spec, out_shape=out_shape,
            scratch_shapes=scratch, compiler_params=_params("parallel", "parallel", "arbitrary"),
        )(*ins)
    assert not pair
    rows, cols = rider[0][0].shape
    nj = n // bn
    rb = rows // ((m // bm) * nj * nk)
    assert rows == rb * (m // bm) * nj * nk and rb % 8 == 0, (rows, rb)
    ride_spec = pl.BlockSpec((rb, cols), lambda i, j, kk: ((i * nj + j) * nk + kk, 0))
    outs = _pcall(
        body, name=name, grid=(m // bm, nj, nk), in_specs=in_specs + [ride_spec] * n_ride,
        out_specs=[o_spec] + [ride_spec] * n_rode,
        out_shape=[out_shape] + [jax.ShapeDtypeStruct((rows, cols), F32)] * n_rode,
        scratch_shapes=scratch,
        compiler_params=_params("arbitrary", "arbitrary", "arbitrary", vmem_limit_bytes=VMEM_RIDER_LIMIT_BYTES),
    )(*ins, *rider[0])
    return outs[0], outs[1:]


def _matmul_quarters(h, w_slots, order, first, count, name, prev=None, bm=1024, bn=1280):
    s, k = h.shape
    n = w_slots.shape[2]
    bm, bn = _pick(s, bm), _pick(n, bn)
    pb = n // bn

    def body(order_ref, h_ref, w_ref, *rest):
        rest[-1][...] = _dot(h_ref[...], w_ref[...])

    ins = (order, h, w_slots) + (() if prev is None else (prev,))
    return _pcall(
        body, name=name, num_scalar_prefetch=1, grid=(count * pb, s // bm),
        in_specs=[pl.BlockSpec((bm, k), lambda j, i, o: (i, 0)),
                  pl.BlockSpec((None, k, bn), lambda j, i, o: (o[first + j // pb], 0, j % pb))]
        + ([] if prev is None else [ANY_SPEC]),
        out_specs=pl.BlockSpec((bm, bn), lambda j, i, o: (i, o[first + j // pb] * pb + j % pb)),
        out_shape=jax.ShapeDtypeStruct((s, N_CHIPS * n), F32),
        input_output_aliases={} if prev is None else {3: 0},
        compiler_params=_params("arbitrary", "arbitrary"),
    )(*ins)


def _place():
    x, y, c = lax.axis_index("x"), lax.axis_index("y"), lax.axis_index("c")
    chips = [(1 - x, y), (x, 1 - y), (1 - x, 1 - y)]
    return x, y, c, chips


def _all_gather8(block, name):
    def body(x_ref, out_ref, send_sems, recv_sems, local_sem):
        x, y, c, chips = _place()
        me, sibling = (x, y, c), (x, y, 1 - c)

        def slot(px, py, pc):
            return out_ref.at[4 * px + 2 * py + pc]

        def copy(k, blk, to, src=None):
            return pltpu.make_async_remote_copy(
                src_ref=slot(*blk) if src is None else src, dst_ref=slot(*blk),
                send_sem=send_sems.at[k], recv_sem=recv_sems.at[k], device_id=to, device_id_type=MESH)

        mine = pltpu.make_async_copy(x_ref, slot(*me), local_sem)
        mine.start()
        first = [copy(0, me, sibling, src=x_ref)]
        first += [copy(1 + j, me, (*chip, c), src=x_ref) for j, chip in enumerate(chips)]
        for cp in first:
            cp.start()
        passed = [copy(4 + j, (*chip, c), sibling) for j, chip in enumerate(chips)]
        for j, chip in enumerate(chips):
            copy(1 + j, (*chip, c), me).wait_recv()
            passed[j].start()
        copy(0, sibling, me).wait_recv()
        for j, chip in enumerate(chips):
            copy(4 + j, (*chip, 1 - c), me).wait_recv()
        for cp in first + passed:
            cp.wait_send()
        mine.wait()

    return _pcall(
        body, name=name, out_shape=jax.ShapeDtypeStruct((N_DEV,) + block.shape, block.dtype),
        in_specs=[pl.BlockSpec(memory_space=pltpu.VMEM)], out_specs=pl.BlockSpec(memory_space=pltpu.VMEM),
        scratch_shapes=[pltpu.SemaphoreType.DMA((7,)), pltpu.SemaphoreType.DMA((7,)), pltpu.SemaphoreType.DMA],
        compiler_params=pltpu.CompilerParams(vmem_limit_bytes=VMEM_LIMIT_BYTES),
    )(block)


HBM_SPEC = pl.BlockSpec(memory_space=pltpu.HBM)
SEM_SPEC = pl.BlockSpec(memory_space=pltpu.SEMAPHORE)
ANY_SPEC = pl.BlockSpec(memory_space=pl.ANY)
EFFECT = pltpu.SideEffectType.DATAFLOW_SIDE_EFFECTING


def _xfer_start(name, bufs, plan, n_copies, after_last=False):
    nb = len(bufs)
    deps = (_Order.last,) if after_last and _Order.last is not None else ()
    nd = len(deps)

    def body(*refs):
        send_sems, recv_sems = refs[nb + nd], refs[nb + nd + 1]
        token = refs[nb + nd + 2 + nb]
        for k, (src, dst, dev) in enumerate(plan(refs[:nb], *_place())):
            pltpu.make_async_remote_copy(src_ref=src, dst_ref=dst, send_sem=send_sems.at[k], recv_sem=recv_sems.at[k],
                                         device_id=dev, device_id_type=MESH).start()
        token[...] = jnp.zeros_like(token)

    outs = pl.pallas_call(
        body, name=name,
        out_shape=(pltpu.SemaphoreType.DMA((n_copies,)), pltpu.SemaphoreType.DMA((n_copies,)),
                   *[pltpu.HBM(b.shape, b.dtype) for b in bufs], jax.ShapeDtypeStruct((8, LANES), F32)),
        in_specs=[HBM_SPEC] * nb + [ANY_SPEC] * nd,
        out_specs=(SEM_SPEC, SEM_SPEC, *[HBM_SPEC] * nb, pl.BlockSpec(memory_space=pltpu.VMEM)),
        input_output_aliases={i: 2 + i for i in range(nb)},
        compiler_params=pltpu.CompilerParams(has_side_effects=EFFECT),
    )(*[pltpu.with_memory_space_constraint(b, pltpu.HBM) for b in bufs], *deps)
    _Order.last = outs[-1]
    return (outs[0], outs[1]), list(outs[2:2 + nb])


def _xfer_wait(name, sems, bufs, plan):
    nb = len(bufs)

    def body(*refs):
        send_sems, recv_sems = refs[nb], refs[nb + 1]
        for k, (src, dst, dev) in enumerate(plan(refs[:nb], *_place())):
            copy = pltpu.make_async_remote_copy(src_ref=src, dst_ref=dst, send_sem=send_sems.at[k],
                                                recv_sem=recv_sems.at[k], device_id=dev, device_id_type=MESH)
            copy.wait_send()
            copy.wait_recv()

    outs = pl.pallas_call(
        body, name=name, out_shape=tuple(pltpu.HBM(b.shape, b.dtype) for b in bufs),
        in_specs=[HBM_SPEC] * nb + [SEM_SPEC, SEM_SPEC, ANY_SPEC], out_specs=tuple([HBM_SPEC] * nb),
        input_output_aliases={i: i for i in range(nb)},
        compiler_params=pltpu.CompilerParams(has_side_effects=EFFECT),
    )(*bufs, *sems, _Order.last)
    _Order.last = outs[0]
    return list(outs)


def _xfer_hand_over(name, sems, bufs, plan, next_plans):
    nb, n_next = len(bufs), len(next_plans)

    def body(*refs):
        send_sems, recv_sems = refs[nb], refs[nb + 1]
        outs = refs[nb + 3:]
        place = _place()
        for k, (src, dst, dev) in enumerate(plan(refs[:nb], *place)):
            copy = pltpu.make_async_remote_copy(src_ref=src, dst_ref=dst, send_sem=send_sems.at[k],
                                                recv_sem=recv_sems.at[k], device_id=dev, device_id_type=MESH)
            copy.wait_send()
            copy.wait_recv()
        for p, (next_plan, _) in enumerate(next_plans):
            for k, (src, dst, dev) in enumerate(next_plan(refs[:nb], *place)):
                pltpu.make_async_remote_copy(src_ref=src, dst_ref=dst, send_sem=outs[2 * p].at[k],
                                             recv_sem=outs[2 * p + 1].at[k], device_id=dev, device_id_type=MESH).start()
        outs[-1][...] = jnp.zeros_like(outs[-1])

    sem_shapes = [pltpu.SemaphoreType.DMA((copies,)) for _, copies in next_plans for _ in range(2)]
    outs = pl.pallas_call(
        body, name=name,
        out_shape=(*sem_shapes, *[pltpu.HBM(b.shape, b.dtype) for b in bufs], jax.ShapeDtypeStruct((8, LANES), F32)),
        in_specs=[HBM_SPEC] * nb + [SEM_SPEC, SEM_SPEC, ANY_SPEC],
        out_specs=(*[SEM_SPEC] * (2 * n_next), *[HBM_SPEC] * nb, pl.BlockSpec(memory_space=pltpu.VMEM)),
        input_output_aliases={i: 2 * n_next + i for i in range(nb)},
        compiler_params=pltpu.CompilerParams(has_side_effects=EFFECT),
    )(*bufs, *sems, _Order.last)
    _Order.last = outs[-1]
    return [(outs[2 * p], outs[2 * p + 1]) for p in range(n_next)], list(outs[2 * n_next:2 * n_next + nb])


def _half(ref, c, axis):
    rows = ref.shape[axis] // 2
    return pl.ds(c * rows, rows)


def _plan_weights_ici(n):
    def plan(refs, x, y, c, chips):
        out = []
        for w in range(n):
            region = refs[w].at[2 * x + y, _half(refs[w], c, 1), :]
            out += [(region, region, (*chip, c)) for chip in chips]
        return out
    return plan


def _plan_weights_ring(n):
    def plan(refs, x, y, c, chips):
        out = []
        for w in range(n):
            region = refs[w].at[2 * x + y, _half(refs[w], c, 1), :]
            out += [(region, region, (*chip, c)) for chip in chips[:2]]
        return out
    return plan


def _plan_weights_relay(n):
    def plan(refs, x, y, c, chips):
        out = []
        for w in range(n):
            quarter_rows = refs[w].shape[1] // 4
            upper = refs[w].at[2 * x + (1 - y), pl.ds(2 * c * quarter_rows, quarter_rows), :]
            lower = refs[w].at[2 * (1 - x) + y, pl.ds((2 * c + 1) * quarter_rows, quarter_rows), :]
            out += [(upper, upper, (1 - x, y, c)), (lower, lower, (x, 1 - y, c))]
        return out
    return plan


def _plan_weights_d2d(n, which=slice(0, 3)):
    def plan(refs, x, y, c, chips):
        out = []
        for w in range(n):
            rows = _half(refs[w], c, 1)
            for chip in chips[which]:
                region = refs[w].at[2 * chip[0] + chip[1], rows, :]
                out.append((region, region, (x, y, 1 - c)))
        return out
    return plan


def _plan_grads_d2d(n):
    def plan(refs, x, y, c, chips):
        return [(refs[w].at[:, _half(refs[w], 1 - c, 1), :], refs[n + w], (x, y, 1 - c)) for w in range(n)]
    return plan


def _plan_grads_ici(n):
    def plan(refs, x, y, c, chips):
        out = []
        for w in range(n):
            out += [(refs[w].at[2 * chip[0] + chip[1]], refs[n + w].at[2 * x + y], (*chip, c)) for chip in chips]
        return out
    return plan


def _plan_final_d2d(n):
    def plan(refs, x, y, c, chips):
        out = []
        for w in range(n):
            region = refs[w].at[_half(refs[w], c, 0), :]
            out.append((region, region, (x, y, 1 - c)))
        return out
    return plan


def _stream_blocks(hr, cols):
    bc = cols if cols <= 4096 else _pick(cols, 4096)
    return _pick_rows(hr, max(16, (768 * 1024) // bc)), bc


def _pre_reduce(g, landed, place, name):
    _, rows, cols = g.shape
    hr = rows // 2
    rb, bc = _stream_blocks(hr, cols)
    nrb = hr // rb

    def body(place_ref, g_ref, l_ref, o_ref):
        o_ref[...] = (g_ref[...].astype(F32) + l_ref[...].astype(F32)).astype(o_ref.dtype)

    return _pcall(
        body, name=name, num_scalar_prefetch=1, grid=(N_CHIPS, nrb, cols // bc),
        in_specs=[pl.BlockSpec((None, rb, bc), lambda j, i, k, p: (j, p[1] * nrb + i, k)),
                  pl.BlockSpec((None, rb, bc), lambda j, i, k, p: (j, i, k))],
        out_specs=pl.BlockSpec((None, rb, bc), lambda j, i, k, p: (j, i, k)),
        out_shape=jax.ShapeDtypeStruct((N_CHIPS, hr, cols), g.dtype),
        compiler_params=_params("parallel", "parallel", "parallel"),
    )(place, g, landed)


def _sum_slots(mine, landed, place, name):
    _, hr, cols = mine.shape
    rb, bc = _stream_blocks(hr, cols)
    rb = _pick_rows(hr, max(16, rb // 2))
    nrb = hr // rb

    def body(place_ref, m_ref, l_ref, o_ref):
        chip = place_ref[0]
        own = m_ref[...].astype(F32)
        total = jnp.where(chip == 0, own, l_ref[0].astype(F32))
        for j in range(1, N_CHIPS):
            total = total + jnp.where(chip == j, own, l_ref[j].astype(F32))
        o_ref[...] = total

    return _pcall(
        body, name=name, num_scalar_prefetch=1, grid=(nrb, cols // bc),
        in_specs=[pl.BlockSpec((None, rb, bc), lambda i, k, p: (p[0], i, k)),
                  pl.BlockSpec((N_CHIPS, rb, bc), lambda i, k, p: (0, i, k))],
        out_specs=pl.BlockSpec((rb, bc), lambda i, k, p: (p[1] * nrb + i, k)),
        out_shape=jax.ShapeDtypeStruct((2 * hr, cols), F32),
        compiler_params=_params("parallel", "parallel"),
    )(place, mine, landed)


class _WeightGather:
    @staticmethod
    def zones(quarters, chip):
        return [lax.dynamic_update_slice(lax.empty((N_CHIPS,) + q.shape, BF16), q.astype(BF16)[None], (chip, 0, 0))
                for q in quarters]

    def __init__(self, tag, zones, ring=False):
        self.tag, self.n, self.ring = tag, len(zones), ring
        self.plan = _plan_weights_ring(self.n) if ring else _plan_weights_ici(self.n)
        self.sems, self.bufs = _xfer_start("wici_start_" + tag, zones, self.plan, (2 if ring else 3) * self.n,
                                           after_last=True)

    def relay(self):
        plan = _plan_weights_relay(self.n)
        (self.sems,), self.bufs = _xfer_hand_over("wrel_start_" + self.tag, self.sems, self.bufs, self.plan,
                                                   [(plan, 2 * self.n)])
        self.plan = plan

    def pass_on(self):
        plan = _plan_weights_d2d(self.n)
        (self.sems,), self.bufs = _xfer_hand_over("wd2d_start_" + self.tag, self.sems, self.bufs, self.plan,
                                                   [(plan, 3 * self.n)])
        self.plan = plan

    def done(self):
        return _xfer_wait("wd2d_wait_" + self.tag, self.sems, self.bufs, self.plan)

    def relay_and_pass_near(self):
        plan, self.near_plan = _plan_weights_relay(self.n), _plan_weights_d2d(self.n, slice(0, 2))
        (self.sems, self.near_sems), self.bufs = _xfer_hand_over(
            "wrel_start_" + self.tag, self.sems, self.bufs, self.plan, [(plan, 2 * self.n), (self.near_plan, 2 * self.n)])
        self.plan = plan

    def near_done(self):
        self.bufs = _xfer_wait("wnear_wait_" + self.tag, self.near_sems, self.bufs, self.near_plan)
        return self.bufs

    def pass_far(self):
        plan = _plan_weights_d2d(self.n, slice(2, 3))
        (sems,), bufs = _xfer_hand_over("wfar_start_" + self.tag, self.sems, self.bufs, self.plan, [(plan, self.n)])
        return _xfer_wait("wfar_wait_" + self.tag, sems, bufs, plan)


class _GradReduce:
    def __init__(self, tag, grads):
        self.tag, self.n = tag, len(grads)
        zones = [lax.empty((N_CHIPS, g.shape[1] // 2, g.shape[2]), g.dtype) for g in grads]
        self.plan = _plan_grads_d2d(self.n)
        self.sems, self.bufs = _xfer_start("gd2d_start_" + tag, list(grads) + zones, self.plan, self.n)

    def pair(self, place):
        n = self.n
        bufs = _xfer_wait("gd2d_wait_" + self.tag, self.sems, self.bufs, self.plan)
        self.halves = [_pre_reduce(bufs[w], bufs[n + w], place, f"pre_reduce_{self.tag}{w}") for w in range(n)]

    def cross(self, after_last=False):
        zones = [lax.empty(h.shape, h.dtype) for h in self.halves]
        self.plan = _plan_grads_ici(self.n)
        self.sems, self.bufs = _xfer_start("gici_start_" + self.tag, self.halves + zones, self.plan, 3 * self.n,
                                           after_last=after_last)

    def step(self, place):
        self.pair(place)
        self.cross()

    def join(self, place):
        n = self.n
        bufs = _xfer_wait("gici_wait_" + self.tag, self.sems, self.bufs, self.plan)
        sums = [_sum_slots(bufs[w], bufs[n + w], place, f"sum_slots_{self.tag}{w}") for w in range(n)]
        self.plan = _plan_final_d2d(n)
        self.sems, self.bufs = _xfer_start("gfin_start_" + self.tag, sums, self.plan, n)

    def done(self):
        return _xfer_wait("gfin_wait_" + self.tag, self.sems, self.bufs, self.plan)


def _ada_fwd(c_all, w_q, b_q):
    d, n = w_q.shape
    bn = _pick(n, 512)

    def body(c_ref, w_ref, b_ref, o_ref):
        cv = c_ref[...]
        act = cv * _sigmoid(cv)
        o_ref[...] = _dot(act, w_ref[...], NN, lax.Precision.HIGHEST) + b_ref[...]

    return _pcall(
        body, name="ada_fwd", grid=(n // bn,),
        in_specs=[pl.BlockSpec((N_DEV, d), lambda j: (0, 0)), pl.BlockSpec((d, bn), lambda j: (0, j)),
                  pl.BlockSpec((1, bn), lambda j: (0, j))],
        out_specs=pl.BlockSpec((N_DEV, bn), lambda j: (0, j)),
        out_shape=jax.ShapeDtypeStruct((N_DEV, n), F32), compiler_params=_params("parallel"),
    )(c_all, w_q, b_q)


def _row_spec(rb, width, col=0):
    return pl.BlockSpec((rb, width), lambda i, col=col: (i, col))


def _vec_spec(width, col=0):
    return pl.BlockSpec((1, width), lambda i, col=col: (0, col))


def _norm_mod_fwd(x, g, sc, sh, name, res=None, gt=None):
    s, d = x.shape
    rb = _pick(s, 256)
    has_res = res is not None

    def body(*refs):
        if has_res:
            x_ref, res_ref, gt_ref, g_ref, sc_ref, sh_ref, x1_ref, h_ref = refs
            xv = x_ref[...] + gt_ref[...] * res_ref[...]
            x1_ref[...] = xv
        else:
            x_ref, g_ref, sc_ref, sh_ref, h_ref = refs
            xv = x_ref[...]
        xh, _ = _rms(xv)
        h_ref[...] = (xh * g_ref[...] * (1.0 + sc_ref[...]) + sh_ref[...]).astype(BF16)

    row, vec = _row_spec(rb, d), _vec_spec(d)
    if has_res:
        ins, in_specs = (x, res, gt, g, sc, sh), [row, row, vec, vec, vec, vec]
        out_shape = [jax.ShapeDtypeStruct((s, d), F32), jax.ShapeDtypeStruct((s, d), BF16)]
        out_specs = [row, row]
    else:
        ins, in_specs = (x, g, sc, sh), [row, vec, vec, vec]
        out_shape, out_specs = jax.ShapeDtypeStruct((s, d), BF16), row
    return _pcall(body, name=name, grid=(s // rb,), in_specs=in_specs, out_specs=out_specs,
                          out_shape=out_shape, compiler_params=_params("parallel"))(*ins)


def _final_loss(x1, f, gt2, g_final, target):
    s, d = x1.shape
    rb = _pick(s, 256)

    def body(x1_ref, f_ref, gt_ref, g_ref, t_ref, dx_ref, df_ref, loss_ref, dg_ref, dgt_ref):
        first = pl.program_id(0) == 0
        fv, gt, gv = f_ref[...], gt_ref[...], g_ref[...]
        x2 = x1_ref[...] + gt * fv
        xh, r = _rms(x2)
        err = xh * gv - t_ref[...]
        blk = 0.5 * jnp.sum(jnp.sum(err * err, axis=1, keepdims=True), axis=0, keepdims=True) / d
        dy = err / d
        dxh = dy * gv
        dx = r * (dxh - xh * jnp.mean(dxh * xh, axis=-1, keepdims=True))
        dx_ref[...] = dx
        df_ref[...] = (dx * gt).astype(BF16)
        _accumulate(first, loss_ref, jnp.broadcast_to(blk, (1, LANES)))
        _accumulate(first, dg_ref, _colsum(dy * xh))
        _accumulate(first, dgt_ref, _colsum(dx * fv))

    row, vec = _row_spec(rb, d), _vec_spec(d)
    return _pcall(
        body, name="final_loss", grid=(s // rb,), in_specs=[row, row, vec, vec, row],
        out_specs=[row, row, _vec_spec(LANES), vec, vec],
        out_shape=[jax.ShapeDtypeStruct((s, d), F32), jax.ShapeDtypeStruct((s, d), BF16),
                   jax.ShapeDtypeStruct((1, LANES), F32), jax.ShapeDtypeStruct((1, d), F32),
                   jax.ShapeDtypeStruct((1, d), F32)],
        compiler_params=_params("arbitrary"),
    )(x1, f, gt2, g_final, target)


def _norm_mod_bwd(dh, xin, dres, g, sc, name, branch=None, gt=None):
    s, d = xin.shape
    rb = _pick(s, 256)
    has_branch = branch is not None

    def body(*refs):
        if has_branch:
            dh_ref, x_ref, dres_ref, g_ref, sc_ref, br_ref, gt_ref, dx_ref, dbr_ref, dsh_ref, dsc_ref, dg_ref, dgt_ref = refs
        else:
            dh_ref, x_ref, dres_ref, g_ref, sc_ref, dx_ref, dsh_ref, dsc_ref, dg_ref = refs
        first = pl.program_id(0) == 0
        gv = g_ref[...]
        xh, r = _rms(x_ref[...])
        dhv = dh_ref[...]
        dn = dhv * (1.0 + sc_ref[...])
        dxh = dn * gv
        dx = dres_ref[...] + r * (dxh - xh * jnp.mean(dxh * xh, axis=-1, keepdims=True))
        dx_ref[...] = dx
        _accumulate(first, dsh_ref, _colsum(dhv))
        _accumulate(first, dsc_ref, _colsum(dhv * xh * gv))
        _accumulate(first, dg_ref, _colsum(dn * xh))
        if has_branch:
            dbr_ref[...] = (dx * gt_ref[...]).astype(BF16)
            _accumulate(first, dgt_ref, _colsum(dx * br_ref[...]))

    row, vec = _row_spec(rb, d), _vec_spec(d)
    vec_shape = jax.ShapeDtypeStruct((1, d), F32)
    if has_branch:
        ins, in_specs = (dh, xin, dres, g, sc, branch, gt), [row, row, row, vec, vec, row, vec]
        out_specs = [row, row, vec, vec, vec, vec]
        out_shape = [jax.ShapeDtypeStruct((s, d), F32), jax.ShapeDtypeStruct((s, d), BF16)] + [vec_shape] * 4
    else:
        ins, in_specs = (dh, xin, dres, g, sc), [row, row, row, vec, vec]
        out_specs = [row, vec, vec, vec]
        out_shape = [jax.ShapeDtypeStruct((s, d), F32)] + [vec_shape] * 3
    return _pcall(body, name=name, grid=(s // rb,), in_specs=in_specs, out_specs=out_specs,
                          out_shape=out_shape, compiler_params=_params("arbitrary"))(*ins)


def _branch_merge(ya, yb, w_bg, w_bh, z, b_gate, gate_col, bm=512, bn=1024):
    s, gw = ya.shape
    d = w_bg.shape[1]
    bm, bn = _pick(s, bm), _pick(d, bn)
    nj = d // bn

    def body(ya_ref, yb_ref, wa_ref, wb_ref, za_ref, zb_ref, ba_ref, bb_ref, pa_ref, pb_ref, y_ref):
        pa = _dot(ya_ref[...], wa_ref[...])
        pb = _dot(yb_ref[...], wb_ref[...])
        pa_ref[...] = pa.astype(BF16)
        pb_ref[...] = pb.astype(BF16)
        ga = _sigmoid(za_ref[...] + ba_ref[...])
        gb = _sigmoid(zb_ref[...] + bb_ref[...])
        y_ref[...] = (ga * pa + gb * pb).astype(BF16)

    act = pl.BlockSpec((bm, gw), lambda j, i: (i, 0))
    wgt = pl.BlockSpec((gw, bn), lambda j, i: (0, j))
    tile = pl.BlockSpec((bm, bn), lambda j, i: (i, j))
    shape = jax.ShapeDtypeStruct((s, d), BF16)
    return _pcall(
        body, name="branch_merge", grid=(nj, s // bm),
        in_specs=[act, act, wgt, wgt,
                  pl.BlockSpec((bm, bn), lambda j, i: (i, gate_col * nj + j)),
                  pl.BlockSpec((bm, bn), lambda j, i: (i, (gate_col + 1) * nj + j)),
                  pl.BlockSpec((1, bn), lambda j, i: (0, j)), pl.BlockSpec((1, bn), lambda j, i: (0, nj + j))],
        out_specs=[tile, tile, tile], out_shape=[shape, shape, shape], compiler_params=_params("parallel", "parallel"),
    )(ya, yb, w_bg, w_bh, z, z, b_gate, b_gate)


def _branch_wgrad(ya, yb, dpa, dpb):
    s, gw = ya.shape
    per = dpa.shape[1] // N_CHIPS

    def body(ya_ref, yb_ref, da_ref, db_ref, ga_ref, gb_ref):
        ga_ref[...] = _dot(ya_ref[...], da_ref[...], TN).astype(BF16)
        gb_ref[...] = _dot(yb_ref[...], db_ref[...], TN).astype(BF16)

    act = pl.BlockSpec((s, gw), lambda j: (0, 0))
    grd = pl.BlockSpec((s, per), lambda j: (0, j))
    out = pl.BlockSpec((None, gw, per), lambda j: (j, 0, 0))
    shape = jax.ShapeDtypeStruct((N_CHIPS, gw, per), BF16)
    return _pcall(body, name="branch_wgrad", grid=(N_CHIPS,), in_specs=[act, act, grd, grd], out_specs=[out, out],
                  out_shape=[shape, shape], compiler_params=_params("parallel"))(ya, yb, dpa, dpb)


def _branch_bwd(dpa, dpb, w_bg, w_bh, bm=512):
    s, d = dpa.shape
    gw = w_bg.shape[0]
    bm = _pick(s, bm)

    def body(da_ref, db_ref, wa_ref, wb_ref, oa_ref, ob_ref):
        oa_ref[...] = _dot(da_ref[...], wa_ref[...], NT)
        ob_ref[...] = _dot(db_ref[...], wb_ref[...], NT)

    act = pl.BlockSpec((bm, d), lambda i: (i, 0))
    wgt = pl.BlockSpec((gw, d), lambda i: (0, 0))
    out = pl.BlockSpec((bm, gw), lambda i: (i, 0))
    shape = jax.ShapeDtypeStruct((s, gw), F32)
    return _pcall(body, name="branch_bwd", grid=(s // bm,), in_specs=[act, act, wgt, wgt], out_specs=[out, out],
                  out_shape=[shape, shape], compiler_params=_params("parallel"))(dpa, dpb, w_bg, w_bh)


def _out_bwd(dyo, w_out, pa, pb, z, b_gate, gate_col, bm=512, bn=1024):
    s, d = dyo.shape
    bm, bn = _pick(s, bm), _pick(d, bn)
    nj = d // bn

    def body(dyo_ref, w_ref, pa_ref, pb_ref, za_ref, zb_ref, ba_ref, bb_ref,
             dpa_ref, dpb_ref, dza_ref, dzb_ref, dba_ref, dbb_ref):
        first = pl.program_id(1) == 0
        dyv = _dot(dyo_ref[...], w_ref[...], NT)
        ga = _sigmoid(za_ref[...] + ba_ref[...])
        gb = _sigmoid(zb_ref[...] + bb_ref[...])
        dpa_ref[...] = (dyv * ga).astype(BF16)
        dpb_ref[...] = (dyv * gb).astype(BF16)
        dga = dyv * pa_ref[...].astype(F32) * ga * (1.0 - ga)
        dgb = dyv * pb_ref[...].astype(F32) * gb * (1.0 - gb)
        dza_ref[...] = dga.astype(BF16)
        dzb_ref[...] = dgb.astype(BF16)
        _accumulate(first, dba_ref, _colsum(dga))
        _accumulate(first, dbb_ref, _colsum(dgb))

    tile = pl.BlockSpec((bm, bn), lambda j, i: (i, j))
    vec = pl.BlockSpec((1, bn), lambda j, i: (0, j))
    act, vec_shape = jax.ShapeDtypeStruct((s, d), BF16), jax.ShapeDtypeStruct((1, d), F32)
    return _pcall(
        body, name="out_bwd", grid=(nj, s // bm),
        in_specs=[pl.BlockSpec((bm, d), lambda j, i: (i, 0)), pl.BlockSpec((bn, d), lambda j, i: (j, 0)), tile, tile,
                  pl.BlockSpec((bm, bn), lambda j, i: (i, gate_col * nj + j)),
                  pl.BlockSpec((bm, bn), lambda j, i: (i, (gate_col + 1) * nj + j)),
                  vec, pl.BlockSpec((1, bn), lambda j, i: (0, nj + j))],
        out_specs=[tile, tile, tile, tile, vec, vec], out_shape=[act, act, act, act, vec_shape, vec_shape],
        compiler_params=_params("parallel", "arbitrary"),
    )(dyo, w_out, pa, pb, z, z, b_gate, b_gate)


def _ffn_in(h, w_fi, bm=512, bn=1408):
    s, d = h.shape
    per = w_fi.shape[2]
    ff = 2 * per
    bm, bn = _pick(s, bm), _pick(per, bn)
    per_blocks = per // bn

    def body(h_ref, wa_ref, wu_ref, a_ref, u_ref, hf_ref):
        hv = h_ref[...]
        a = _dot(hv, wa_ref[...])
        up = _dot(hv, wu_ref[...])
        a_ref[...] = a.astype(BF16)
        u_ref[...] = up.astype(BF16)
        hf_ref[...] = (a * _sigmoid(a) * up).astype(BF16)

    out = pl.BlockSpec((bm, bn), lambda j, i: (i, j))
    shape = jax.ShapeDtypeStruct((s, ff), BF16)
    return _pcall(
        body, name="ffn_in", grid=(ff // bn, s // bm),
        in_specs=[pl.BlockSpec((bm, d), lambda j, i: (i, 0)),
                  pl.BlockSpec((None, d, bn), lambda j, i: (j // per_blocks, 0, j % per_blocks)),
                  pl.BlockSpec((None, d, bn), lambda j, i: (2 + j // per_blocks, 0, j % per_blocks))],
        out_specs=[out, out, out], out_shape=[shape, shape, shape],
        compiler_params=_params("parallel", "parallel"),
    )(h, w_fi, w_fi)


def _ffn_out_bwd(dffn, w_fo, a_act, up_act, bm=512, bn=1408):
    s, d = dffn.shape
    ff = w_fo.shape[0]
    bm, bn = _pick(s, bm), _pick(ff, bn)

    def body(d_ref, w_ref, a_ref, u_ref, da_ref, du_ref):
        dhf = _dot(d_ref[...], w_ref[...], NT)
        a = a_ref[...].astype(F32)
        sa = _sigmoid(a)
        da_ref[...] = (dhf * u_ref[...].astype(F32) * sa * (1.0 + a * (1.0 - sa))).astype(BF16)
        du_ref[...] = (dhf * a * sa).astype(BF16)

    tile = pl.BlockSpec((bm, bn), lambda j, i: (i, j))
    shape = jax.ShapeDtypeStruct((s, ff), BF16)
    return _pcall(
        body, name="ffn_out_bwd", grid=(ff // bn, s // bm),
        in_specs=[pl.BlockSpec((bm, d), lambda j, i: (i, 0)), pl.BlockSpec((bn, d), lambda j, i: (j, 0)), tile, tile],
        out_specs=[tile, tile], out_shape=[shape, shape], compiler_params=_params("parallel", "parallel"),
    )(dffn, w_fo, a_act, up_act)


def _tril(n):
    return lax.broadcasted_iota(jnp.int32, (n, n), 0) >= lax.broadcasted_iota(jnp.int32, (n, n), 1)


def _gmlp_norm(v, ln_g, ln_b):
    gv = _gelu(v)
    mu = jnp.mean(gv, axis=-1, keepdims=True)
    cen = gv - mu
    rs = lax.rsqrt(jnp.mean(cen * cen, axis=-1, keepdims=True) + EPS)
    xh = cen * rs
    return xh, rs, xh * ln_g + ln_b


def _gmlp_fwd(z, ln_g, ln_b, ws, bs_t):
    s = z.shape[0]
    gw = ln_g.shape[1]
    groups, chunk, _ = ws.shape

    def body(u_ref, v_ref, lg_ref, lb_ref, ws_ref, bs_ref, ya_ref):
        gu = _gelu(u_ref[...])
        _, _, vn = _gmlp_norm(v_ref[...], lg_ref[...], lb_ref[...])
        mask = _tril(chunk)
        for g in range(groups):
            cols = slice(g * LANES, (g + 1) * LANES)
            wm = jnp.where(mask, ws_ref[g], 0.0).astype(BF16)
            sg = _dot(wm, vn[:, cols].astype(BF16)) + bs_ref[:, g:g + 1]
            ya_ref[:, cols] = (gu[:, cols] * sg).astype(BF16)

    return _pcall(
        body, name="gmlp_fwd", grid=(s // chunk,),
        in_specs=[_row_spec(chunk, gw, 0), _row_spec(chunk, gw, 1), _vec_spec(gw), _vec_spec(gw),
                  pl.BlockSpec((groups, chunk, chunk), lambda i: (0, 0, 0)), pl.BlockSpec((chunk, LANES), lambda i: (0, 0))],
        out_specs=_row_spec(chunk, gw), out_shape=jax.ShapeDtypeStruct((s, gw), BF16),
        compiler_params=_params("parallel"),
    )(z, z, ln_g, ln_b, ws, bs_t)


def _gmlp_bwd(dya, z, ln_g, ln_b, ws, bs_t):
    s = z.shape[0]
    gw = ln_g.shape[1]
    groups, chunk, _ = ws.shape

    def body(dya_ref, u_ref, v_ref, lg_ref, lb_ref, ws_ref, bs_ref, duv_ref, dws_ref, dbs_ref, dlg_ref, dlb_ref, dvn_ref):
        first = pl.program_id(0) == 0
        u, v, lg = u_ref[...], v_ref[...], lg_ref[...]
        gu = _gelu(u)
        xh, rs, vn = _gmlp_norm(v, lg, lb_ref[...])
        dyav = dya_ref[...]
        mask = _tril(chunk)
        lane = lax.broadcasted_iota(jnp.int32, (chunk, LANES), 1)
        dbs = jnp.zeros((chunk, LANES), F32)
        for g in range(groups):
            cols = slice(g * LANES, (g + 1) * LANES)
            wm = jnp.where(mask, ws_ref[g], 0.0).astype(BF16)
            vg = vn[:, cols].astype(BF16)
            sg = _dot(wm, vg) + bs_ref[:, g:g + 1]
            ds = dyav[:, cols] * gu[:, cols]
            duv_ref[:, cols] = (dyav[:, cols] * sg * _gelu_grad(u[:, cols])).astype(BF16)
            dsb = ds.astype(BF16)
            _accumulate(first, dws_ref.at[g], jnp.where(mask, _dot(dsb, vg, NT), 0.0))
            dbs = dbs + jnp.where(lane == g, jnp.sum(ds, axis=-1, keepdims=True), 0.0)
            dvn_ref[:, cols] = _dot(wm, dsb, TN)
        dvn = dvn_ref[...]
        _accumulate(first, dbs_ref, dbs)
        _accumulate(first, dlb_ref, _colsum(dvn))
        _accumulate(first, dlg_ref, _colsum(dvn * xh))
        dxh = dvn * lg
        dgv = rs * (dxh - jnp.mean(dxh, axis=-1, keepdims=True) - xh * jnp.mean(dxh * xh, axis=-1, keepdims=True))
        duv_ref[:, gw:] = (dgv * _gelu_grad(v)).astype(BF16)

    return _pcall(
        body, name="gmlp_bwd", grid=(s // chunk,),
        in_specs=[_row_spec(chunk, gw), _row_spec(chunk, gw, 0), _row_spec(chunk, gw, 1), _vec_spec(gw), _vec_spec(gw),
                  pl.BlockSpec((groups, chunk, chunk), lambda i: (0, 0, 0)), pl.BlockSpec((chunk, LANES), lambda i: (0, 0))],
        out_specs=[_row_spec(chunk, 2 * gw), pl.BlockSpec((groups, chunk, chunk), lambda i: (0, 0, 0)),
                   pl.BlockSpec((chunk, LANES), lambda i: (0, 0)), _vec_spec(gw), _vec_spec(gw)],
        out_shape=[jax.ShapeDtypeStruct((s, 2 * gw), BF16), jax.ShapeDtypeStruct((groups, chunk, chunk), F32),
                   jax.ShapeDtypeStruct((chunk, LANES), F32), jax.ShapeDtypeStruct((1, gw), F32),
                   jax.ShapeDtypeStruct((1, gw), F32)],
        scratch_shapes=[pltpu.VMEM((chunk, gw), F32)],
        compiler_params=_params("arbitrary"),
    )(dya, z, z, ln_g, ln_b, ws, bs_t)


def _lower_bound(lb_ref):
    a0, a1 = lb_ref[0:1, :], lb_ref[1:2, :]
    mx = jnp.maximum(a0, a1)
    e0, e1 = jnp.exp(a0 - mx), jnp.exp(a1 - mx)
    return e0 / (e0 + e1)


def _sum_dot(mask, x):
    hi = x.astype(BF16)
    rest = x - hi.astype(F32)
    mid = rest.astype(BF16)
    low = (rest - mid.astype(F32)).astype(BF16)
    return _dot(mask, hi) + _dot(mask, mid) + _dot(mask, low)


def _ones_where(mask):
    return jnp.where(mask, 1.0, 0.0).astype(BF16)


def _hg_masks(rows, t):
    r = lax.broadcasted_iota(jnp.int32, (rows, rows), 0)
    c = lax.broadcasted_iota(jnp.int32, (rows, rows), 1)
    same = (r // t) == (c // t)
    incl = jnp.logical_and(same, c <= r)
    upto_mid = jnp.logical_and(same, (c % t) <= t // 2)
    rev = jnp.logical_and(same, c >= r)
    return same, incl, upto_mid, rev


def _hg_block(q, fp, lb, masks):
    rows = q.shape[0]
    same, incl, upto_mid, _ = masks
    sig = _sigmoid(fp)
    f = lb + (1.0 - lb) * sig
    k = 1.0 - f
    sq = _sigmoid(q)
    qa = q * sq
    stacked = jnp.concatenate([_ones_where(m) for m in (incl, same, upto_mid)], axis=0)
    sums = _sum_dot(stacked, jnp.log(f))
    b, b_last, b_mid = sums[:rows], sums[rows:2 * rows], sums[2 * rows:]
    e_q = jnp.exp(jnp.minimum(b - b_mid, EXP_CLAMP))
    e_k = jnp.exp(jnp.minimum(b_mid - b, EXP_CLAMP))
    e_in = jnp.exp(b)
    e_out = jnp.exp(b_last - b)
    return dict(sig=sig, f=f, k=k, sq=sq, qa=qa, e_last=jnp.exp(b_last), e_q=e_q, e_k=e_k, e_in=e_in, e_out=e_out,
                q_hat=(qa * e_q).astype(BF16), k_hat=(k * e_k).astype(BF16),
                q_in=(qa * e_in).astype(BF16), k_out=k * e_out)


def _hgrn_fwd(z, hg_lb, norm_g, q_col):
    s = z.shape[0]
    hw = norm_g.shape[1]
    heads = hw // LANES
    t = HG_CHUNK
    rows = min(HG_ROWS, s)
    per_step = rows // t
    hp = min(HG_HEADS_PER_STEP, heads)
    assert heads % hp == 0 and q_col % hp == 0, (heads, q_col)
    wide = hp * LANES

    def zspec(which):
        return pl.BlockSpec((rows, wide), lambda h, r, which=which: (r, (q_col + which * heads) // hp + h))

    def body(q_ref, f_ref, i_ref, g_ref, lb_ref, ng_ref, yb_ref, o_ref, st_out_ref, st_ref, e_last_ref, inter_ref):
        @pl.when(pl.program_id(1) == 0)
        def _():
            st_ref[...] = jnp.zeros_like(st_ref)

        masks = _hg_masks(rows, t)
        for hh in range(hp):
            cols = slice(hh * LANES, (hh + 1) * LANES)
            blk = _hg_block(q_ref[:, cols], f_ref[:, cols], _lower_bound(lb_ref.at[:, cols]), masks)
            iv = i_ref[:, cols].astype(BF16)
            q_in, k_out = blk["q_in"], blk["k_out"].astype(BF16)
            e_last_ref[hh] = blk["e_last"]
            attn = jnp.where(masks[1], _dot(blk["q_hat"], blk["k_hat"], NT), 0.0).astype(BF16)
            o = _dot(attn, iv)
            grown = [_dot(iv[j * t:(j + 1) * t], k_out[j * t:(j + 1) * t], TN) for j in range(per_step)]
            st = st_ref[hh]
            for j in range(per_step):
                st_out_ref[hh, j] = st
                inter_ref[hh, j * t:(j + 1) * t, :] = _dot(q_in[j * t:(j + 1) * t], st.astype(BF16), NT)
                st = st * e_last_ref[hh, j * t:j * t + 1, :] + grown[j]
            st_ref[hh] = st
            o = o + inter_ref[hh]
            o_ref[:, cols] = o
            og = g_ref[:, cols]
            on, _ = _rms(o)
            yb_ref[:, cols] = (on * ng_ref[:, cols] * (og * _sigmoid(og))).astype(BF16)

    out_row = pl.BlockSpec((rows, wide), lambda h, r: (r, h))
    return _pcall(
        body, name="hgrn_fwd", grid=(heads // hp, s // rows),
        in_specs=[zspec(0), zspec(1), zspec(2), zspec(3),
                  pl.BlockSpec((2, wide), lambda h, r: (0, h)), pl.BlockSpec((1, wide), lambda h, r: (0, h))],
        out_specs=[out_row, out_row, pl.BlockSpec((hp, per_step, LANES, LANES), lambda h, r: (h, r, 0, 0))],
        out_shape=[jax.ShapeDtypeStruct((s, hw), BF16), jax.ShapeDtypeStruct((s, hw), F32),
                   jax.ShapeDtypeStruct((heads, s // t, LANES, LANES), F32)],
        scratch_shapes=[pltpu.VMEM((hp, LANES, LANES), F32), pltpu.VMEM((hp, rows, LANES), F32),
                        pltpu.VMEM((hp, rows, LANES), F32)],
        compiler_params=_params("parallel", "arbitrary"),
    )(z, z, z, z, hg_lb, norm_g)


def _hgrn_bwd(dyb, z, o_raw, states, hg_lb, norm_g, q_col):
    s = z.shape[0]
    hw = norm_g.shape[1]
    heads = hw // LANES
    t = HG_CHUNK
    rows = min(HG_ROWS, s)
    per_step = rows // t
    n_steps = s // rows
    hp = min(HG_HEADS_PER_STEP, heads)
    assert heads % hp == 0 and q_col % hp == 0, (heads, q_col)
    wide = hp * LANES

    def zspec(which):
        return pl.BlockSpec((rows, wide), lambda h, r, which=which: (n_steps - 1 - r, (q_col + which * heads) // hp + h))

    def body(dyb_ref, q_ref, f_ref, i_ref, g_ref, o_ref, st_in_ref, lb_ref, ng_ref,
             dq_ref, df_ref, di_ref, dg_ref, dlb_ref, dng_ref, dst_ref, acc_lb_ref, acc_ng_ref,
             e_last_ref, dq_in_ref, dk_out_ref, di_inter_ref, carry_ref):
        step = pl.program_id(1)

        @pl.when(step == 0)
        def _():
            dst_ref[...] = jnp.zeros_like(dst_ref)
            acc_lb_ref[...] = jnp.zeros_like(acc_lb_ref)
            acc_ng_ref[...] = jnp.zeros_like(acc_ng_ref)

        masks = _hg_masks(rows, t)
        same, incl, _, rev = masks
        sum_mask = jnp.concatenate([_ones_where(rev), _ones_where(same)], axis=1)
        for hh in range(hp):
            cols = slice(hh * LANES, (hh + 1) * LANES)
            lb = _lower_bound(lb_ref.at[:, cols])
            ng = ng_ref[:, cols]
            q = q_ref[:, cols]
            blk = _hg_block(q, f_ref[:, cols], lb, masks)
            iv = i_ref[:, cols].astype(BF16)
            o, og, dy = o_ref[:, cols], g_ref[:, cols], dyb_ref[:, cols]
            so = _sigmoid(og)
            on, r = _rms(o)
            acc_ng_ref[:, cols] += _colsum(dy * on * (og * so))
            dg_ref[:, cols] = (dy * on * ng * so * (1.0 + og * (1.0 - so))).astype(BF16)
            don = dy * ng * (og * so)
            do = (r * (don - on * jnp.mean(don * on, axis=-1, keepdims=True))).astype(BF16)
            q_hat, k_hat, q_in, k_out = blk["q_hat"], blk["k_hat"], blk["q_in"], blk["k_out"]
            k_out_b = k_out.astype(BF16)
            attn = jnp.where(incl, _dot(q_hat, k_hat, NT), 0.0).astype(BF16)
            d_attn = jnp.where(incl, _dot(do, iv, NT), 0.0).astype(BF16)
            di_intra = _dot(attn, do, TN)
            dq_hat = _dot(d_attn, k_hat)
            dk_hat = _dot(d_attn, q_hat, TN)
            e_last_ref[hh] = blk["e_last"]
            grown = [_dot(do[j * t:(j + 1) * t], q_in[j * t:(j + 1) * t], TN) for j in range(per_step)]
            dst = dst_ref[hh]
            for j in reversed(range(per_step)):
                rs_ = slice(j * t, (j + 1) * t)
                e_last = e_last_ref[hh, j * t:j * t + 1, :]
                st_prev, dst_b = st_in_ref[hh, j], dst.astype(BF16)
                dq_in_ref[hh, rs_, :] = _dot(do[rs_], st_prev.astype(BF16))
                dk_out_ref[hh, rs_, :] = _dot(iv[rs_], dst_b)
                di_inter_ref[hh, rs_, :] = _dot(k_out_b[rs_], dst_b, NT)
                carry_ref[hh, rs_, :] = jnp.broadcast_to(e_last * _colsum(st_prev * dst), (t, LANES))
                dst = dst * e_last + grown[j]
            dst_ref[hh] = dst
            di_ref[:, cols] = (di_intra + di_inter_ref[hh]).astype(BF16)
            dk_out = dk_out_ref[hh]
            dqa = dq_in_ref[hh] * blk["e_in"] + dq_hat * blk["e_q"]
            dk = dk_out * blk["e_out"] + dk_hat * blk["e_k"]
            db = blk["qa"] * dqa - blk["k"] * dk
            dlf = _sum_dot(sum_mask, jnp.concatenate([db, dk_out * k_out], axis=0)) + carry_ref[hh]
            dfv = dlf / blk["f"] - dk
            sig, sq = blk["sig"], blk["sq"]
            df_ref[:, cols] = (dfv * (1.0 - lb) * sig * (1.0 - sig)).astype(BF16)
            acc_lb_ref[:, cols] += _colsum(dfv * (1.0 - sig))
            dq_ref[:, cols] = (dqa * sq * (1.0 + q * (1.0 - sq))).astype(BF16)

        @pl.when(step == n_steps - 1)
        def _():
            lb = _lower_bound(lb_ref)
            d0 = acc_lb_ref[...] * lb * (1.0 - lb)
            dlb_ref[0:1, :] = d0
            dlb_ref[1:2, :] = -d0
            dng_ref[...] = acc_ng_ref[...]

    rev_row = pl.BlockSpec((rows, wide), lambda h, r: (n_steps - 1 - r, h))
    piece = jax.ShapeDtypeStruct((s, hw), BF16)
    return _pcall(
        body, name="hgrn_bwd", grid=(heads // hp, n_steps),
        in_specs=[rev_row, zspec(0), zspec(1), zspec(2), zspec(3), rev_row,
                  pl.BlockSpec((hp, per_step, LANES, LANES), lambda h, r: (h, n_steps - 1 - r, 0, 0)),
                  pl.BlockSpec((2, wide), lambda h, r: (0, h)), pl.BlockSpec((1, wide), lambda h, r: (0, h))],
        out_specs=[rev_row, rev_row, rev_row, rev_row,
                   pl.BlockSpec((2, wide), lambda h, r: (0, h)), pl.BlockSpec((1, wide), lambda h, r: (0, h))],
        out_shape=[piece, piece, piece, piece, jax.ShapeDtypeStruct((2, hw), F32), jax.ShapeDtypeStruct((1, hw), F32)],
        scratch_shapes=[pltpu.VMEM((hp, LANES, LANES), F32), pltpu.VMEM((1, wide), F32), pltpu.VMEM((1, wide), F32)]
        + [pltpu.VMEM((hp, rows, LANES), F32)] * 5,
        compiler_params=_params("parallel", "arbitrary"),
    )(dyb, z, z, z, z, o_raw, states, hg_lb, norm_g)


def _adam_update(w, m, v, g):
    m2 = ADAM_B1 * m + (1.0 - ADAM_B1) * g
    v2 = ADAM_B2 * v + (1.0 - ADAM_B2) * (g * g)
    m_hat = m2 * (1.0 / (1.0 - ADAM_B1 ** ADAM_STEP))
    v_hat = v2 * (1.0 / (1.0 - ADAM_B2 ** ADAM_STEP))
    return -ADAM_LR * (m_hat / (jnp.sqrt(v_hat) + ADAM_EPS) + ADAM_WD * w), m2, v2


def _adamw(w, m, v, parts, name, outer=False):
    rows, cols = w.shape
    bc = cols if cols <= 4096 else _pick(cols, 4096)
    rb = _pick_rows(rows, max(8, (384 * 1024) // bc), mult=8)
    if outer and rb % LANES:
        rb = rows

    def body(w_ref, m_ref, v_ref, *refs):
        g_ref, d_ref, mo_ref, vo_ref = refs[-4:]
        if outer:
            cv = refs[0][...]
            g = _dot(cv * _sigmoid(cv), refs[1][...], TN, lax.Precision.HIGHEST)
        else:
            p_ref = refs[0]
            g = p_ref[0].astype(F32)
            for p in range(1, p_ref.shape[0]):
                g = g + p_ref[p].astype(F32)
        g_ref[...] = g
        d_ref[...], mo_ref[...], vo_ref[...] = _adam_update(w_ref[...], m_ref[...], v_ref[...], g)

    blk = pl.BlockSpec((rb, bc), lambda i, j: (i, j))
    out = jax.ShapeDtypeStruct((rows, cols), F32)
    if outer:
        grad_specs = [pl.BlockSpec((N_DEV, rb), lambda i, j: (0, i)), pl.BlockSpec((N_DEV, bc), lambda i, j: (0, j))]
        grad_ins = tuple(parts)
    else:
        grad_specs = [pl.BlockSpec((parts.shape[0], rb, bc), lambda i, j: (0, i, j))]
        grad_ins = (parts,)
    return _pcall(
        body, name=name, grid=(rows // rb, cols // bc), in_specs=[blk, blk, blk] + grad_specs,
        out_specs=[blk] * 4, out_shape=[out] * 4, compiler_params=_params("parallel", "parallel"),
    )(w, m, v, *grad_ins)


SMALL = ("b_ada", "norm1_g", "b_gate", "gmlp_ln_g", "gmlp_ln_b", "gmlp_ws", "gmlp_bs", "hg_lb", "hg_norm_g",
         "norm2_g", "final_norm_g")
BIG = ("w_in", "w_branch_gmlp", "w_branch_hg", "w_out", "w_ffn_in", "w_ffn_out")
WEIGHTS = ("w_ada", "b_ada", "norm1_g", "w_in", "b_gate", "gmlp_ln_g", "gmlp_ln_b", "gmlp_ws", "gmlp_bs", "hg_lb",
           "hg_norm_g", "w_branch_gmlp", "w_branch_hg", "w_out", "norm2_g", "w_ffn_in", "w_ffn_out", "final_norm_g")


def _pack(parts):
    return jnp.concatenate([p.reshape(-1, LANES) for p in parts], axis=0)


def _step(x, c, loss_target, w, m, v):
    s, d = x.shape[1], x.shape[2]
    gw = w["gmlp_ln_g"].shape[-1]
    hw = w["hg_norm_g"].shape[-1]
    x2d, tgt = x[0], loss_target[0]
    mx, my, mc = lax.axis_index("x"), lax.axis_index("y"), lax.axis_index("c")
    chip = 2 * mx + my
    dev = 2 * chip + mc
    q_col = 2 * gw // LANES
    gate_col = (2 * gw + 4 * hw) // d
    place = jnp.stack([chip, mc]).astype(jnp.int32)
    _Order.last = None

    c_all = _all_gather8(c.reshape(-1, LANES), "gather_c").reshape(N_DEV, d)
    gather_in = _WeightGather("in", _WeightGather.zones([w["w_in"][0]], chip), ring=True)
    zones_mix = _WeightGather.zones([w[n][0] for n in ("w_branch_gmlp", "w_branch_hg", "w_out")], chip)
    zones_fi = _WeightGather.zones([w["w_ffn_in"][0]], chip)
    zones_fo = _WeightGather.zones([w["w_ffn_out"][0]], chip)
    _Order.also = (*zones_mix, *zones_fi, *zones_fo)
    n_ada = w["w_ada"].shape[-1]
    b_ada_q = lax.dynamic_slice(w["b_ada"], (0, chip * n_ada), (1, n_ada))
    mod_q = _ada_fwd(c_all, w["w_ada"][0], b_ada_q)
    mod_all = _all_gather8(mod_q, "gather_mod")
    mod = lax.dynamic_index_in_dim(mod_all, dev, axis=1, keepdims=False)[::2].reshape(1, 6 * d)
    sh1, sc1, gt1, sh2, sc2, gt2 = [mod[:, i * d:(i + 1) * d] for i in range(6)]

    gather_in.relay_and_pass_near()
    gather_mix = _WeightGather("mix", zones_mix)
    gather_fi = _WeightGather("fi", zones_fi, ring=True)

    norm1_g, norm2_g, final_g = w["norm1_g"], w["norm2_g"], w["final_norm_g"].reshape(1, d)
    ln_g, ln_b = w["gmlp_ln_g"], w["gmlp_ln_b"]
    ws = w["gmlp_ws"][0]
    groups = ws.shape[0]
    bs_t = jnp.pad(w["gmlp_bs"][0].T, ((0, 0), (0, LANES - groups)))
    hg_lb, hg_ng, b_gate = w["hg_lb"], w["hg_norm_g"], w["b_gate"]

    h1 = _norm_mod_fwd(x2d, norm1_g, sc1, sh1, "norm1_fwd")
    order = jnp.stack([chip, 2 * (1 - mx) + my, 2 * mx + (1 - my), 2 * (1 - mx) + (1 - my)]).astype(jnp.int32)
    w_in, = gather_in.near_done()
    z = _matmul_quarters(h1, w_in, order, 0, 3, "mm_z_near")
    w_in, = gather_in.pass_far()
    z = _matmul_quarters(h1, w_in, order, 3, 1, "mm_z_far", prev=z)
    gather_mix.pass_on()
    ya = _gmlp_fwd(z, ln_g, ln_b, ws, bs_t)
    yb, o_raw, states = _hgrn_fwd(z, hg_lb, hg_ng, q_col)
    gather_fi.relay()
    gather_fo = _WeightGather("fo", zones_fo)
    w_bg, w_bh, w_out = gather_mix.done()
    w_bg, w_bh = [wq.transpose(1, 0, 2).reshape(wq.shape[1], -1) for wq in (w_bg, w_bh)]
    w_out = w_out.reshape(-1, w_out.shape[-1])
    pa, pb, y = _branch_merge(ya, yb, w_bg, w_bh, z, b_gate, gate_col)
    yo = _matmul(y, w_out, mode="nn", name="mm_yo", out_dtype=F32)
    gather_fi.pass_on()
    x1, h2 = _norm_mod_fwd(x2d, norm2_g, sc2, sh2, "norm2_fwd", res=yo, gt=gt1)
    w_fi, = gather_fi.done()
    a_act, up_act, hf = _ffn_in(h2, w_fi)
    gather_fo.pass_on()
    w_fo, = gather_fo.done()
    w_fo = w_fo.reshape(-1, w_fo.shape[-1])
    ffn = _matmul(hf, w_fo, mode="nn", name="mm_ffn", out_dtype=F32)
    dx2, dffn, loss_row, d_final_g, d_gt2 = _final_loss(x1, ffn, gt2, final_g, tgt)

    g_fo = _matmul(hf, dffn, mode="tn", name="mm_g_fo", out_dtype=BF16, bm=1408)
    daup = tuple(_ffn_out_bwd(dffn, w_fo, a_act, up_act))
    g_fi = _matmul(h2, daup, mode="tn", name="mm_g_fi", out_dtype=BF16, out_slots=True, bn=1408)
    red_ffn = _GradReduce("ffn", [g_fo.reshape(N_CHIPS, -1, g_fo.shape[-1]), g_fi])
    dh2 = _matmul(daup, w_fi, mode="nt", name="mm_dh2", out_dtype=F32, b_slots=True)
    red_ffn.step(place)
    dx1, dyo, d_sh2, d_sc2, d_norm2, d_gt1 = _norm_mod_bwd(dh2, x1, dx2, norm2_g, sc2, "norm2_bwd", branch=yo, gt=gt1)
    g_out = _matmul(y, dyo, mode="tn", name="mm_g_out", out_dtype=BF16)
    dpa, dpb, dz_ga, dz_gb, d_b_ga, d_b_gb = _out_bwd(dyo, w_out, pa, pb, z, b_gate, gate_col)
    g_bg, g_bh = _branch_wgrad(ya, yb, dpa, dpb)
    red_mix = _GradReduce("mix", [g_out.reshape(N_CHIPS, -1, g_out.shape[-1]), g_bg, g_bh])
    dya, dyb = _branch_bwd(dpa, dpb, w_bg, w_bh)
    red_mix.step(place)
    dz_uv, d_ws, d_bs_t, d_ln_g, d_ln_b = _gmlp_bwd(dya, z, ln_g, ln_b, ws, bs_t)
    dz_q, dz_f, dz_i, dz_g, d_hg_lb, d_hg_ng = _hgrn_bwd(dyb, z, o_raw, states, hg_lb, hg_ng, q_col)
    dz = jnp.concatenate([dz_uv, dz_q, dz_f, dz_i, dz_g, dz_ga, dz_gb], axis=1)
    grad, delta, new_m, new_v = {}, {}, {}, {}

    def update(n, parts, outer=False):
        outs = _adamw(w[n][0], m[n][0], v[n][0], parts, "adamw_" + n, outer=outer)
        grad[n], delta[n], new_m[n], new_v[n] = [o[None] for o in outs]

    g_in = _matmul(h1, dz, mode="tn", name="mm_g_in", out_dtype=BF16, out_slots=True, bn=1280)
    red_in = _GradReduce("in", [g_in])
    red_ffn.join(place)
    red_mix.join(place)
    red_in.step(place)
    g_fo, g_fi = red_ffn.done()
    n = "w_ffn_in"
    dh1, rode = _matmul(dz, w_in, mode="nt", name="mm_dh1", out_dtype=F32, b_slots=True,
                        rider=([w[n][0], m[n][0], v[n][0], g_fi], lambda *blk: (blk[3], *_adam_update(*blk)), 4))
    grad[n], delta[n], new_m[n], new_v[n] = [o[None] for o in rode]
    grad_x, d_sh1, d_sc1, d_norm1 = _norm_mod_bwd(dh1, x2d, dx1, norm1_g, sc1, "norm1_bwd")
    update("w_ffn_out", g_fo[None])
    g_out, g_bg, g_bh = red_mix.done()
    update("w_out", g_out[None])
    update("w_branch_gmlp", g_bg[None])
    update("w_branch_hg", g_bh[None])

    d_mod = jnp.concatenate([d_sh1, d_sc1, d_gt1, d_sh2, d_sc2, d_gt2], axis=1)
    small_part = {"b_ada": d_mod, "norm1_g": d_norm1, "b_gate": jnp.concatenate([d_b_ga, d_b_gb], axis=1), "gmlp_ln_g": d_ln_g, "gmlp_ln_b": d_ln_b,
                  "gmlp_ws": d_ws, "gmlp_bs": d_bs_t[:, :groups].T, "hg_lb": d_hg_lb, "hg_norm_g": d_hg_ng,
                  "norm2_g": d_norm2, "final_norm_g": d_final_g}
    loss_rows = jnp.broadcast_to(loss_row, (8, LANES))
    small_all = _all_gather8(_pack([small_part[n] for n in SMALL] + [loss_rows]), "gather_small")
    d_mod_all = small_all[:, :6 * d // LANES].reshape(N_DEV, 6 * d)
    d_mod_q = lax.dynamic_slice(d_mod_all, (0, chip * n_ada), (N_DEV, n_ada))
    red_in.join(place)
    update("w_ada", (c_all, d_mod_q), outer=True)
    pad = [jnp.zeros_like(loss_rows)]
    outs = _adamw(_pack([w[n] for n in SMALL] + pad), _pack([m[n] for n in SMALL] + pad),
                  _pack([v[n] for n in SMALL] + pad), small_all, "adamw_small")
    update("w_in", red_in.done()[0][None])
    row = 0
    for n in SMALL:
        cnt = w[n].size // LANES
        for dst, o in zip((grad, delta, new_m, new_v), outs):
            dst[n] = o[row:row + cnt].reshape(w[n].shape)
        row += cnt

    loss = outs[0][row, 0]
    return (loss, grad_x[None], *[grad[n] for n in WEIGHTS], *[delta[n] for n in WEIGHTS],
            *[new_m[n] for n in WEIGHTS], *[new_v[n] for n in WEIGHTS])


def kernel(x, c, w_ada, b_ada, norm1_g, w_in, b_gate, gmlp_ln_g, gmlp_ln_b, gmlp_ws, gmlp_bs, hg_lb, hg_norm_g, w_branch_gmlp, w_branch_hg, w_out, norm2_g, w_ffn_in, w_ffn_out, final_norm_g, loss_target, m_w_ada, m_b_ada, m_norm1_g, m_w_in, m_b_gate, m_gmlp_ln_g, m_gmlp_ln_b, m_gmlp_ws, m_gmlp_bs, m_hg_lb, m_hg_norm_g, m_w_branch_gmlp, m_w_branch_hg, m_w_out, m_norm2_g, m_w_ffn_in, m_w_ffn_out, m_final_norm_g, v_w_ada, v_b_ada, v_norm1_g, v_w_in, v_b_gate, v_gmlp_ln_g, v_gmlp_ln_b, v_gmlp_ws, v_gmlp_bs, v_hg_lb, v_hg_norm_g, v_w_branch_gmlp, v_w_branch_hg, v_w_out, v_norm2_g, v_w_ffn_in, v_w_ffn_out, v_final_norm_g):
    w = dict(w_ada=w_ada, b_ada=b_ada, norm1_g=norm1_g, w_in=w_in, b_gate=b_gate, gmlp_ln_g=gmlp_ln_g,
             gmlp_ln_b=gmlp_ln_b, gmlp_ws=gmlp_ws, gmlp_bs=gmlp_bs, hg_lb=hg_lb, hg_norm_g=hg_norm_g,
             w_branch_gmlp=w_branch_gmlp, w_branch_hg=w_branch_hg, w_out=w_out, norm2_g=norm2_g,
             w_ffn_in=w_ffn_in, w_ffn_out=w_ffn_out, final_norm_g=final_norm_g)
    m = dict(w_ada=m_w_ada, b_ada=m_b_ada, norm1_g=m_norm1_g, w_in=m_w_in, b_gate=m_b_gate, gmlp_ln_g=m_gmlp_ln_g,
             gmlp_ln_b=m_gmlp_ln_b, gmlp_ws=m_gmlp_ws, gmlp_bs=m_gmlp_bs, hg_lb=m_hg_lb, hg_norm_g=m_hg_norm_g,
             w_branch_gmlp=m_w_branch_gmlp, w_branch_hg=m_w_branch_hg, w_out=m_w_out, norm2_g=m_norm2_g,
             w_ffn_in=m_w_ffn_in, w_ffn_out=m_w_ffn_out, final_norm_g=m_final_norm_g)
    v = dict(w_ada=v_w_ada, b_ada=v_b_ada, norm1_g=v_norm1_g, w_in=v_w_in, b_gate=v_b_gate, gmlp_ln_g=v_gmlp_ln_g,
             gmlp_ln_b=v_gmlp_ln_b, gmlp_ws=v_gmlp_ws, gmlp_bs=v_gmlp_bs, hg_lb=v_hg_lb, hg_norm_g=v_hg_norm_g,
             w_branch_gmlp=v_w_branch_gmlp, w_branch_hg=v_w_branch_hg, w_out=v_w_out, norm2_g=v_norm2_g,
             w_ffn_in=v_w_ffn_in, w_ffn_out=v_w_ffn_out, final_norm_g=v_final_norm_g)
    return _step(x, c, loss_target, w, m, v)
```

```python
import functools

import jax
import jax.numpy as jnp
from jax import lax
from jax.experimental import pallas as pl
from jax.experimental.pallas import tpu as pltpu

F32 = jnp.float32
BF16 = jnp.bfloat16
EPS = 1e-6
LANES = 128
N_CHIPS = 4
N_DEV = 8
VMEM_LIMIT_BYTES = 56 * 1024 * 1024
VMEM_RIDER_LIMIT_BYTES = 60 * 1024 * 1024
HG_CHUNK = 32
HG_ROWS = 256
HG_HEADS_PER_STEP = 8
EXP_CLAMP = 80.0
ADAM_LR, ADAM_B1, ADAM_B2, ADAM_EPS, ADAM_WD, ADAM_STEP = 0.001, 0.9, 0.999, 1e-08, 0.01, 10
MESH = pl.DeviceIdType.MESH

NN = (((1,), (0,)), ((), ()))
NT = (((1,), (1,)), ((), ()))
TN = (((0,), (0,)), ((), ()))


def _dot(a, b, dims=NN, precision=None):
    return lax.dot_general(a, b, dims, precision=precision, preferred_element_type=F32)


def _params(*semantics, vmem_limit_bytes=VMEM_LIMIT_BYTES):
    return pltpu.CompilerParams(dimension_semantics=semantics, vmem_limit_bytes=vmem_limit_bytes)


class _Order:
    last = None
    also = ()


def _pcall(body, *, in_specs, out_specs, grid=(), scratch_shapes=(), num_scalar_prefetch=0, **kw):
    def run(*ins):
        deps = (() if _Order.last is None else (_Order.last,)) + tuple(_Order.also)
        _Order.also = ()
        n_in, n_dep = len(ins), len(deps)

        def wrapped(*refs):
            body(*refs[:n_in], *refs[n_in + n_dep:])

        specs = list(in_specs) + [pl.BlockSpec(memory_space=pl.ANY)] * n_dep
        if num_scalar_prefetch:
            grid_spec = pltpu.PrefetchScalarGridSpec(
                num_scalar_prefetch=num_scalar_prefetch, grid=grid, in_specs=specs, out_specs=out_specs,
                scratch_shapes=scratch_shapes)
            outs = pl.pallas_call(wrapped, grid_spec=grid_spec, **kw)(*ins, *deps)
        else:
            outs = pl.pallas_call(wrapped, grid=grid, in_specs=specs, out_specs=out_specs,
                                  scratch_shapes=scratch_shapes, **kw)(*ins, *deps)
        _Order.last = jax.tree.leaves(outs)[0]
        return outs

    return run


def _pick_rows(dim, pref, mult=16):
    best = None
    for cand in range(mult, min(dim, pref) + 1, mult):
        if dim % cand == 0:
            best = cand
    assert best is not None, (dim, pref)
    return best


def _half_width(n):
    return n // 2 if n % (2 * LANES) == 0 else n


def _pick(dim, pref):
    if dim <= pref:
        return dim
    best = None
    for cand in range(LANES, pref + 1, LANES):
        if dim % cand == 0:
            best = cand
    assert best is not None, (dim, pref)
    return best


def _sigmoid(x):
    return 1.0 / (1.0 + jnp.exp(-x))


def _gelu(x):
    c = 0.7978845608028654
    return 0.5 * x * (1.0 + jnp.tanh(c * (x + 0.044715 * x * x * x)))


def _gelu_grad(x):
    c = 0.7978845608028654
    t = jnp.tanh(c * (x + 0.044715 * x * x * x))
    return 0.5 * (1.0 + t) + 0.5 * x * (1.0 - t * t) * c * (1.0 + 3.0 * 0.044715 * x * x)


def _rms(x):
    r = lax.rsqrt(jnp.mean(x * x, axis=-1, keepdims=True) + EPS)
    return x * r, r


def _colsum(x):
    return jnp.sum(x, axis=0, keepdims=True)


def _accumulate(first, ref, val):
    @pl.when(first)
    def _():
        ref[...] = val

    @pl.when(jnp.logical_not(first))
    def _():
        ref[...] += val


def _matmul(a, b, *, mode, name, out_dtype, b_slots=False, out_slots=False, bm=1024, bn=1024, bk=2816, rider=None):
    pair = a if isinstance(a, tuple) else b if isinstance(b, tuple) else None
    if mode == "nn":
        m, k = a.shape
        n = b.shape[2] * N_CHIPS if b_slots else b.shape[1]
        per = b.shape[2] if b_slots else n
    elif mode == "nt":
        m, k = (a[0].shape[0], 2 * a[0].shape[1]) if pair else a.shape
        n = b.shape[1] if b_slots else b.shape[0]
        per = b.shape[2] if b_slots else k
    else:
        k, m = a.shape
        n = 2 * b[0].shape[1] if pair else b.shape[1]
        per = n // N_CHIPS if out_slots else n
    bm = _pick(m, bm)
    if mode == "nt":
        bn, bk = _pick(n, bn), _pick(per, bk)
    else:
        bn, bk = _pick(per, bn), _pick(k, bk)
    nk = k // bk
    per_blocks = per // (bk if mode == "nt" else bn)
    dims = {"nn": NN, "nt": NT, "tn": TN}[mode]
    half = (nk if mode == "nt" else n // bn) // 2

    def product(a_ref, b_ref, o_ref, acc):
        part = _dot(a_ref[...], b_ref[...], dims)
        if nk == 1:
            o_ref[...] = part.astype(o_ref.dtype)
            return
        acc_ref, kk = acc[0], pl.program_id(2)

        @pl.when(kk == 0)
        def _():
            acc_ref[...] = part

        @pl.when(jnp.logical_and(kk > 0, kk < nk - 1))
        def _():
            acc_ref[...] += part

        @pl.when(kk == nk - 1)
        def _():
            o_ref[...] = (acc_ref[...] + part).astype(o_ref.dtype)

    n_ride, n_rode = (len(rider[0]), rider[2]) if rider else (0, 0)

    def body(*refs):
        if rider:
            ride_in, ride_out = refs[2:2 + n_ride], refs[3 + n_ride:3 + n_ride + n_rode]

            def ride(t, carry):
                rows = pl.ds(pl.multiple_of(t * 8, 8), 8)
                for ref, val in zip(ride_out, rider[1](*[r[rows, :] for r in ride_in])):
                    ref[rows, :] = val
                return carry

            lax.fori_loop(0, ride_in[0].shape[0] // 8, ride, 0)
        if not pair:
            return product(refs[0], refs[1], refs[2 + n_ride], refs[3 + n_ride + n_rode:])
        first = pl.program_id(2 if mode == "nt" else 1) < half
        x1, x2, y, o_ref, acc = refs[0], refs[1], refs[2], refs[3], refs[4:]

        @pl.when(first)
        def _():
            product(*((x1, y) if mode == "nt" else (y, x1)), o_ref, acc)

        @pl.when(jnp.logical_not(first))
        def _():
            product(*((x2, y) if mode == "nt" else (y, x2)), o_ref, acc)

    if mode == "nn":
        a_spec = pl.BlockSpec((bm, bk), lambda i, j, kk: (i, kk))
        if b_slots:
            b_spec = pl.BlockSpec((None, bk, bn), lambda i, j, kk: (j // per_blocks, kk, j % per_blocks))
        else:
            b_spec = pl.BlockSpec((bk, bn), lambda i, j, kk: (kk, j))
    elif mode == "nt":
        a_spec = pl.BlockSpec((bm, bk), lambda i, j, kk: (i, kk))
        if b_slots:
            b_spec = pl.BlockSpec((None, bn, bk), lambda i, j, kk: (kk // per_blocks, j, kk % per_blocks))
        else:
            b_spec = pl.BlockSpec((bn, bk), lambda i, j, kk: (j, kk))
    else:
        a_spec = pl.BlockSpec((bk, bm), lambda i, j, kk: (kk, i))
        b_spec = pl.BlockSpec((bk, bn), lambda i, j, kk: (kk, j))
    if out_slots:
        o_spec = pl.BlockSpec((None, bm, bn), lambda i, j, kk: (j // per_blocks, i, j % per_blocks))
        out_shape = jax.ShapeDtypeStruct((N_CHIPS, m, per), out_dtype)
    else:
        o_spec = pl.BlockSpec((bm, bn), lambda i, j, kk: (i, j))
        out_shape = jax.ShapeDtypeStruct((m, n), out_dtype)
    if not pair:
        ins, in_specs = (a, b), [a_spec, b_spec]
    elif mode == "nt":
        ins = (*a, b)
        in_specs = [pl.BlockSpec((bm, bk), lambda i, j, kk: (i, jnp.minimum(kk, half - 1))),
                    pl.BlockSpec((bm, bk), lambda i, j, kk: (i, jnp.maximum(kk - half, 0))), b_spec]
    else:
        ins = (*b, a)
        in_specs = [pl.BlockSpec((bk, bn), lambda i, j, kk: (kk, jnp.minimum(j, half - 1))),
                    pl.BlockSpec((bk, bn), lambda i, j, kk: (kk, jnp.maximum(j - half, 0))), a_spec]
    scratch = [pltpu.VMEM((bm, bn), F32)] if nk > 1 else []
    if not rider:
        return _pcall(
            body, name=name, grid=(m // bm, n // bn, nk), in_specs=in_specs, out_specs=o_spec, out_shape=out_shape,
            scratch_shapes=scratch, compiler_params=_params("parallel", "parallel", "arbitrary"),
        )(*ins)
    assert not pair
    rows, cols = rider[0][0].shape
    nj = n // bn
    rb = rows // ((m // bm) * nj * nk)
    assert rows == rb * (m // bm) * nj * nk and rb % 8 == 0, (rows, rb)
    ride_spec = pl.BlockSpec((rb, cols), lambda i, j, kk: ((i * nj + j) * nk + kk, 0))
    outs = _pcall(
        body, name=name, grid=(m // bm, nj, nk), in_specs=in_specs + [ride_spec] * n_ride,
        out_specs=[o_spec] + [ride_spec] * n_rode,
        out_shape=[out_shape] + [jax.ShapeDtypeStruct((rows, cols), F32)] * n_rode,
        scratch_shapes=scratch,
        compiler_params=_params("arbitrary", "arbitrary", "arbitrary", vmem_limit_bytes=VMEM_RIDER_LIMIT_BYTES),
    )(*ins, *rider[0])
    return outs[0], outs[1:]


def _matmul_quarters(h, w_slots, order, first, count, name, prev=None, bm=1024):
    s, k = h.shape
    n = w_slots.shape[2]
    bm, bn = _pick(s, bm), _half_width(n)
    pb = n // bn

    def body(order_ref, h_ref, w_ref, *rest):
        rest[-1][...] = _dot(h_ref[...], w_ref[...])

    ins = (order, h, w_slots) + (() if prev is None else (prev,))
    return _pcall(
        body, name=name, num_scalar_prefetch=1, grid=(count * pb, s // bm),
        in_specs=[pl.BlockSpec((bm, k), lambda j, i, o: (i, 0)),
                  pl.BlockSpec((None, k, bn), lambda j, i, o: (o[first + j // pb], 0, j % pb))]
        + ([] if prev is None else [ANY_SPEC]),
        out_specs=pl.BlockSpec((bm, bn), lambda j, i, o: (i, o[first + j // pb] * pb + j % pb)),
        out_shape=jax.ShapeDtypeStruct((s, N_CHIPS * n), F32),
        input_output_aliases={} if prev is None else {3: 0},
        compiler_params=_params("arbitrary", "arbitrary"),
    )(*ins)


def _place():
    x, y, c = lax.axis_index("x"), lax.axis_index("y"), lax.axis_index("c")
    chips = [(1 - x, y), (x, 1 - y), (1 - x, 1 - y)]
    return x, y, c, chips


def _all_gather8(block, name):
    def body(x_ref, out_ref, send_sems, recv_sems, local_sem):
        x, y, c, chips = _place()
        me, sibling = (x, y, c), (x, y, 1 - c)

        def slot(px, py, pc):
            return out_ref.at[4 * px + 2 * py + pc]

        def copy(k, blk, to, src=None):
            return pltpu.make_async_remote_copy(
                src_ref=slot(*blk) if src is None else src, dst_ref=slot(*blk),
                send_sem=send_sems.at[k], recv_sem=recv_sems.at[k], device_id=to, device_id_type=MESH)

        mine = pltpu.make_async_copy(x_ref, slot(*me), local_sem)
        mine.start()
        first = [copy(0, me, sibling, src=x_ref)]
        first += [copy(1 + j, me, (*chip, c), src=x_ref) for j, chip in enumerate(chips)]
        for cp in first:
            cp.start()
        passed = [copy(4 + j, (*chip, c), sibling) for j, chip in enumerate(chips)]
        for j, chip in enumerate(chips):
            copy(1 + j, (*chip, c), me).wait_recv()
            passed[j].start()
        copy(0, sibling, me).wait_recv()
        for j, chip in enumerate(chips):
            copy(4 + j, (*chip, 1 - c), me).wait_recv()
        for cp in first + passed:
            cp.wait_send()
        mine.wait()

    return _pcall(
        body, name=name, out_shape=jax.ShapeDtypeStruct((N_DEV,) + block.shape, block.dtype),
        in_specs=[pl.BlockSpec(memory_space=pltpu.VMEM)], out_specs=pl.BlockSpec(memory_space=pltpu.VMEM),
        scratch_shapes=[pltpu.SemaphoreType.DMA((7,)), pltpu.SemaphoreType.DMA((7,)), pltpu.SemaphoreType.DMA],
        compiler_params=pltpu.CompilerParams(vmem_limit_bytes=VMEM_LIMIT_BYTES),
    )(block)


HBM_SPEC = pl.BlockSpec(memory_space=pltpu.HBM)
SEM_SPEC = pl.BlockSpec(memory_space=pltpu.SEMAPHORE)
ANY_SPEC = pl.BlockSpec(memory_space=pl.ANY)
EFFECT = pltpu.SideEffectType.DATAFLOW_SIDE_EFFECTING


def _xfer_start(name, bufs, plan, n_copies, after_last=False):
    nb = len(bufs)
    deps = (_Order.last,) if after_last and _Order.last is not None else ()
    nd = len(deps)

    def body(*refs):
        send_sems, recv_sems = refs[nb + nd], refs[nb + nd + 1]
        token = refs[nb + nd + 2 + nb]
        for k, (src, dst, dev) in enumerate(plan(refs[:nb], *_place())):
            pltpu.make_async_remote_copy(src_ref=src, dst_ref=dst, send_sem=send_sems.at[k], recv_sem=recv_sems.at[k],
                                         device_id=dev, device_id_type=MESH).start()
        token[...] = jnp.zeros_like(token)

    outs = pl.pallas_call(
        body, name=name,
        out_shape=(pltpu.SemaphoreType.DMA((n_copies,)), pltpu.SemaphoreType.DMA((n_copies,)),
                   *[pltpu.HBM(b.shape, b.dtype) for b in bufs], jax.ShapeDtypeStruct((8, LANES), F32)),
        in_specs=[HBM_SPEC] * nb + [ANY_SPEC] * nd,
        out_specs=(SEM_SPEC, SEM_SPEC, *[HBM_SPEC] * nb, pl.BlockSpec(memory_space=pltpu.VMEM)),
        input_output_aliases={i: 2 + i for i in range(nb)},
        compiler_params=pltpu.CompilerParams(has_side_effects=EFFECT),
    )(*[pltpu.with_memory_space_constraint(b, pltpu.HBM) for b in bufs], *deps)
    _Order.last = outs[-1]
    return (outs[0], outs[1]), list(outs[2:2 + nb])


def _xfer_wait(name, sems, bufs, plan):
    nb = len(bufs)

    def body(*refs):
        send_sems, recv_sems = refs[nb], refs[nb + 1]
        for k, (src, dst, dev) in enumerate(plan(refs[:nb], *_place())):
            copy = pltpu.make_async_remote_copy(src_ref=src, dst_ref=dst, send_sem=send_sems.at[k],
                                                recv_sem=recv_sems.at[k], device_id=dev, device_id_type=MESH)
            copy.wait_send()
            copy.wait_recv()

    outs = pl.pallas_call(
        body, name=name, out_shape=tuple(pltpu.HBM(b.shape, b.dtype) for b in bufs),
        in_specs=[HBM_SPEC] * nb + [SEM_SPEC, SEM_SPEC, ANY_SPEC], out_specs=tuple([HBM_SPEC] * nb),
        input_output_aliases={i: i for i in range(nb)},
        compiler_params=pltpu.CompilerParams(has_side_effects=EFFECT),
    )(*bufs, *sems, _Order.last)
    _Order.last = outs[0]
    return list(outs)


def _xfer_hand_over(name, sems, bufs, plan, next_plans):
    nb, n_next = len(bufs), len(next_plans)

    def body(*refs):
        send_sems, recv_sems = refs[nb], refs[nb + 1]
        outs = refs[nb + 3:]
        place = _place()
        for k, (src, dst, dev) in enumerate(plan(refs[:nb], *place)):
            copy = pltpu.make_async_remote_copy(src_ref=src, dst_ref=dst, send_sem=send_sems.at[k],
                                                recv_sem=recv_sems.at[k], device_id=dev, device_id_type=MESH)
            copy.wait_send()
            copy.wait_recv()
        for p, (next_plan, _) in enumerate(next_plans):
            for k, (src, dst, dev) in enumerate(next_plan(refs[:nb], *place)):
                pltpu.make_async_remote_copy(src_ref=src, dst_ref=dst, send_sem=outs[2 * p].at[k],
                                             recv_sem=outs[2 * p + 1].at[k], device_id=dev, device_id_type=MESH).start()
        outs[-1][...] = jnp.zeros_like(outs[-1])

    sem_shapes = [pltpu.SemaphoreType.DMA((copies,)) for _, copies in next_plans for _ in range(2)]
    outs = pl.pallas_call(
        body, name=name,
        out_shape=(*sem_shapes, *[pltpu.HBM(b.shape, b.dtype) for b in bufs], jax.ShapeDtypeStruct((8, LANES), F32)),
        in_specs=[HBM_SPEC] * nb + [SEM_SPEC, SEM_SPEC, ANY_SPEC],
        out_specs=(*[SEM_SPEC] * (2 * n_next), *[HBM_SPEC] * nb, pl.BlockSpec(memory_space=pltpu.VMEM)),
        input_output_aliases={i: 2 * n_next + i for i in range(nb)},
        compiler_params=pltpu.CompilerParams(has_side_effects=EFFECT),
    )(*bufs, *sems, _Order.last)
    _Order.last = outs[-1]
    return [(outs[2 * p], outs[2 * p + 1]) for p in range(n_next)], list(outs[2 * n_next:2 * n_next + nb])


def _half(ref, c, axis):
    rows = ref.shape[axis] // 2
    return pl.ds(c * rows, rows)


def _plan_weights_ici(n):
    def plan(refs, x, y, c, chips):
        out = []
        for w in range(n):
            region = refs[w].at[2 * x + y, _half(refs[w], c, 1), :]
            out += [(region, region, (*chip, c)) for chip in chips]
        return out
    return plan


def _plan_weights_ring(n):
    def plan(refs, x, y, c, chips):
        out = []
        for w in range(n):
            region = refs[w].at[2 * x + y, _half(refs[w], c, 1), :]
            out += [(region, region, (*chip, c)) for chip in chips[:2]]
        return out
    return plan


def _plan_weights_relay(n):
    def plan(refs, x, y, c, chips):
        out = []
        for w in range(n):
            quarter_rows = refs[w].shape[1] // 4
            upper = refs[w].at[2 * x + (1 - y), pl.ds(2 * c * quarter_rows, quarter_rows), :]
            lower = refs[w].at[2 * (1 - x) + y, pl.ds((2 * c + 1) * quarter_rows, quarter_rows), :]
            out += [(upper, upper, (1 - x, y, c)), (lower, lower, (x, 1 - y, c))]
        return out
    return plan


def _plan_weights_d2d(n, which=slice(0, 3)):
    def plan(refs, x, y, c, chips):
        out = []
        for w in range(n):
            rows = _half(refs[w], c, 1)
            for chip in chips[which]:
                region = refs[w].at[2 * chip[0] + chip[1], rows, :]
                out.append((region, region, (x, y, 1 - c)))
        return out
    return plan


def _plan_grads_d2d(n):
    def plan(refs, x, y, c, chips):
        return [(refs[w].at[:, _half(refs[w], 1 - c, 1), :], refs[n + w], (x, y, 1 - c)) for w in range(n)]
    return plan


def _plan_grads_ici(n):
    def plan(refs, x, y, c, chips):
        out = []
        for w in range(n):
            out += [(refs[w].at[2 * chip[0] + chip[1]], refs[n + w].at[2 * x + y], (*chip, c)) for chip in chips]
        return out
    return plan


def _plan_final_d2d(n):
    def plan(refs, x, y, c, chips):
        out = []
        for w in range(n):
            region = refs[w].at[_half(refs[w], c, 0), :]
            out.append((region, region, (x, y, 1 - c)))
        return out
    return plan


def _stream_blocks(hr, cols):
    bc = cols if cols <= 4096 else _pick(cols, 4096)
    return _pick_rows(hr, max(16, (768 * 1024) // bc)), bc


def _pre_reduce(g, landed, place, name):
    _, rows, cols = g.shape
    hr = rows // 2
    rb, bc = _stream_blocks(hr, cols)
    nrb = hr // rb

    def body(place_ref, g_ref, l_ref, o_ref):
        o_ref[...] = (g_ref[...].astype(F32) + l_ref[...].astype(F32)).astype(o_ref.dtype)

    return _pcall(
        body, name=name, num_scalar_prefetch=1, grid=(N_CHIPS, nrb, cols // bc),
        in_specs=[pl.BlockSpec((None, rb, bc), lambda j, i, k, p: (j, p[1] * nrb + i, k)),
                  pl.BlockSpec((None, rb, bc), lambda j, i, k, p: (j, i, k))],
        out_specs=pl.BlockSpec((None, rb, bc), lambda j, i, k, p: (j, i, k)),
        out_shape=jax.ShapeDtypeStruct((N_CHIPS, hr, cols), g.dtype),
        compiler_params=_params("parallel", "parallel", "parallel"),
    )(place, g, landed)


def _sum_slots(mine, landed, place, name):
    _, hr, cols = mine.shape
    rb, bc = _stream_blocks(hr, cols)
    rb = _pick_rows(hr, max(16, rb // 2))
    nrb = hr // rb

    def body(place_ref, m_ref, l_ref, o_ref):
        chip = place_ref[0]
        own = m_ref[...].astype(F32)
        total = jnp.where(chip == 0, own, l_ref[0].astype(F32))
        for j in range(1, N_CHIPS):
            total = total + jnp.where(chip == j, own, l_ref[j].astype(F32))
        o_ref[...] = total

    return _pcall(
        body, name=name, num_scalar_prefetch=1, grid=(nrb, cols // bc),
        in_specs=[pl.BlockSpec((None, rb, bc), lambda i, k, p: (p[0], i, k)),
                  pl.BlockSpec((N_CHIPS, rb, bc), lambda i, k, p: (0, i, k))],
        out_specs=pl.BlockSpec((rb, bc), lambda i, k, p: (p[1] * nrb + i, k)),
        out_shape=jax.ShapeDtypeStruct((2 * hr, cols), F32),
        compiler_params=_params("parallel", "parallel"),
    )(place, mine, landed)


class _WeightGather:
    @staticmethod
    def zones(quarters, chip):
        return [lax.dynamic_update_slice(lax.empty((N_CHIPS,) + q.shape, BF16), q.astype(BF16)[None], (chip, 0, 0))
                for q in quarters]

    def __init__(self, tag, zones, ring=False):
        self.tag, self.n, self.ring = tag, len(zones), ring
        self.plan = _plan_weights_ring(self.n) if ring else _plan_weights_ici(self.n)
        self.sems, self.bufs = _xfer_start("wici_start_" + tag, zones, self.plan, (2 if ring else 3) * self.n,
                                           after_last=True)

    def relay(self):
        plan = _plan_weights_relay(self.n)
        (self.sems,), self.bufs = _xfer_hand_over("wrel_start_" + self.tag, self.sems, self.bufs, self.plan,
                                                   [(plan, 2 * self.n)])
        self.plan = plan

    def pass_on(self):
        plan = _plan_weights_d2d(self.n)
        (self.sems,), self.bufs = _xfer_hand_over("wd2d_start_" + self.tag, self.sems, self.bufs, self.plan,
                                                   [(plan, 3 * self.n)])
        self.plan = plan

    def done(self):
        return _xfer_wait("wd2d_wait_" + self.tag, self.sems, self.bufs, self.plan)

    def relay_and_pass_near(self):
        plan, self.near_plan = _plan_weights_relay(self.n), _plan_weights_d2d(self.n, slice(0, 2))
        (self.sems, self.near_sems), self.bufs = _xfer_hand_over(
            "wrel_start_" + self.tag, self.sems, self.bufs, self.plan, [(plan, 2 * self.n), (self.near_plan, 2 * self.n)])
        self.plan = plan

    def near_done(self):
        self.bufs = _xfer_wait("wnear_wait_" + self.tag, self.near_sems, self.bufs, self.near_plan)
        return self.bufs

    def pass_far(self):
        plan = _plan_weights_d2d(self.n, slice(2, 3))
        (sems,), bufs = _xfer_hand_over("wfar_start_" + self.tag, self.sems, self.bufs, self.plan, [(plan, self.n)])
        return _xfer_wait("wfar_wait_" + self.tag, sems, bufs, plan)


class _GradReduce:
    def __init__(self, tag, grads):
        self.tag, self.n = tag, len(grads)
        zones = [lax.empty((N_CHIPS, g.shape[1] // 2, g.shape[2]), g.dtype) for g in grads]
        self.plan = _plan_grads_d2d(self.n)
        self.sems, self.bufs = _xfer_start("gd2d_start_" + tag, list(grads) + zones, self.plan, self.n)

    def pair(self, place):
        n = self.n
        bufs = _xfer_wait("gd2d_wait_" + self.tag, self.sems, self.bufs, self.plan)
        self.halves = [_pre_reduce(bufs[w], bufs[n + w], place, f"pre_reduce_{self.tag}{w}") for w in range(n)]

    def cross(self, after_last=False):
        zones = [lax.empty(h.shape, h.dtype) for h in self.halves]
        self.plan = _plan_grads_ici(self.n)
        self.sems, self.bufs = _xfer_start("gici_start_" + self.tag, self.halves + zones, self.plan, 3 * self.n,
                                           after_last=after_last)

    def step(self, place):
        self.pair(place)
        self.cross()

    def join(self, place):
        n = self.n
        bufs = _xfer_wait("gici_wait_" + self.tag, self.sems, self.bufs, self.plan)
        sums = [_sum_slots(bufs[w], bufs[n + w], place, f"sum_slots_{self.tag}{w}") for w in range(n)]
        self.plan = _plan_final_d2d(n)
        self.sems, self.bufs = _xfer_start("gfin_start_" + self.tag, sums, self.plan, n)

    def done(self):
        return _xfer_wait("gfin_wait_" + self.tag, self.sems, self.bufs, self.plan)


def _ada_fwd(c_all, w_q, b_q):
    d, n = w_q.shape
    bn = _pick(n, 512)

    def body(c_ref, w_ref, b_ref, o_ref):
        cv = c_ref[...]
        act = cv * _sigmoid(cv)
        o_ref[...] = _dot(act, w_ref[...], NN, lax.Precision.HIGHEST) + b_ref[...]

    return _pcall(
        body, name="ada_fwd", grid=(n // bn,),
        in_specs=[pl.BlockSpec((N_DEV, d), lambda j: (0, 0)), pl.BlockSpec((d, bn), lambda j: (0, j)),
                  pl.BlockSpec((1, bn), lambda j: (0, j))],
        out_specs=pl.BlockSpec((N_DEV, bn), lambda j: (0, j)),
        out_shape=jax.ShapeDtypeStruct((N_DEV, n), F32), compiler_params=_params("parallel"),
    )(c_all, w_q, b_q)


def _row_spec(rb, width, col=0):
    return pl.BlockSpec((rb, width), lambda i, col=col: (i, col))


def _vec_spec(width, col=0):
    return pl.BlockSpec((1, width), lambda i, col=col: (0, col))


def _norm_mod_fwd(x, g, sc, sh, name, res=None, gt=None):
    s, d = x.shape
    rb = _pick(s, 256)
    has_res = res is not None

    def body(*refs):
        if has_res:
            x_ref, res_ref, gt_ref, g_ref, sc_ref, sh_ref, x1_ref, h_ref = refs
            xv = x_ref[...] + gt_ref[...] * res_ref[...]
            x1_ref[...] = xv
        else:
            x_ref, g_ref, sc_ref, sh_ref, h_ref = refs
            xv = x_ref[...]
        xh, _ = _rms(xv)
        h_ref[...] = (xh * g_ref[...] * (1.0 + sc_ref[...]) + sh_ref[...]).astype(BF16)

    row, vec = _row_spec(rb, d), _vec_spec(d)
    if has_res:
        ins, in_specs = (x, res, gt, g, sc, sh), [row, row, vec, vec, vec, vec]
        out_shape = [jax.ShapeDtypeStruct((s, d), F32), jax.ShapeDtypeStruct((s, d), BF16)]
        out_specs = [row, row]
    else:
        ins, in_specs = (x, g, sc, sh), [row, vec, vec, vec]
        out_shape, out_specs = jax.ShapeDtypeStruct((s, d), BF16), row
    return _pcall(body, name=name, grid=(s // rb,), in_specs=in_specs, out_specs=out_specs,
                          out_shape=out_shape, compiler_params=_params("parallel"))(*ins)


def _final_loss(x1, f, gt2, g_final, target):
    s, d = x1.shape
    rb = _pick(s, 256)

    def body(x1_ref, f_ref, gt_ref, g_ref, t_ref, dx_ref, df_ref, loss_ref, dg_ref, dgt_ref):
        first = pl.program_id(0) == 0
        fv, gt, gv = f_ref[...], gt_ref[...], g_ref[...]
        x2 = x1_ref[...] + gt * fv
        xh, r = _rms(x2)
        err = xh * gv - t_ref[...]
        blk = 0.5 * jnp.sum(jnp.sum(err * err, axis=1, keepdims=True), axis=0, keepdims=True) / d
        dy = err / d
        dxh = dy * gv
        dx = r * (dxh - xh * jnp.mean(dxh * xh, axis=-1, keepdims=True))
        dx_ref[...] = dx
        df_ref[...] = (dx * gt).astype(BF16)
        _accumulate(first, loss_ref, jnp.broadcast_to(blk, (1, LANES)))
        _accumulate(first, dg_ref, _colsum(dy * xh))
        _accumulate(first, dgt_ref, _colsum(dx * fv))

    row, vec = _row_spec(rb, d), _vec_spec(d)
    return _pcall(
        body, name="final_loss", grid=(s // rb,), in_specs=[row, row, vec, vec, row],
        out_specs=[row, row, _vec_spec(LANES), vec, vec],
        out_shape=[jax.ShapeDtypeStruct((s, d), F32), jax.ShapeDtypeStruct((s, d), BF16),
                   jax.ShapeDtypeStruct((1, LANES), F32), jax.ShapeDtypeStruct((1, d), F32),
                   jax.ShapeDtypeStruct((1, d), F32)],
        compiler_params=_params("arbitrary"),
    )(x1, f, gt2, g_final, target)


def _norm_mod_bwd(dh, xin, dres, g, sc, name, branch=None, gt=None):
    s, d = xin.shape
    rb = _pick(s, 256)
    has_branch = branch is not None

    def body(*refs):
        if has_branch:
            dh_ref, x_ref, dres_ref, g_ref, sc_ref, br_ref, gt_ref, dx_ref, dbr_ref, dsh_ref, dsc_ref, dg_ref, dgt_ref = refs
        else:
            dh_ref, x_ref, dres_ref, g_ref, sc_ref, dx_ref, dsh_ref, dsc_ref, dg_ref = refs
        first = pl.program_id(0) == 0
        gv = g_ref[...]
        xh, r = _rms(x_ref[...])
        dhv = dh_ref[...]
        dn = dhv * (1.0 + sc_ref[...])
        dxh = dn * gv
        dx = dres_ref[...] + r * (dxh - xh * jnp.mean(dxh * xh, axis=-1, keepdims=True))
        dx_ref[...] = dx
        _accumulate(first, dsh_ref, _colsum(dhv))
        _accumulate(first, dsc_ref, _colsum(dhv * xh * gv))
        _accumulate(first, dg_ref, _colsum(dn * xh))
        if has_branch:
            dbr_ref[...] = (dx * gt_ref[...]).astype(BF16)
            _accumulate(first, dgt_ref, _colsum(dx * br_ref[...]))

    row, vec = _row_spec(rb, d), _vec_spec(d)
    vec_shape = jax.ShapeDtypeStruct((1, d), F32)
    if has_branch:
        ins, in_specs = (dh, xin, dres, g, sc, branch, gt), [row, row, row, vec, vec, row, vec]
        out_specs = [row, row, vec, vec, vec, vec]
        out_shape = [jax.ShapeDtypeStruct((s, d), F32), jax.ShapeDtypeStruct((s, d), BF16)] + [vec_shape] * 4
    else:
        ins, in_specs = (dh, xin, dres, g, sc), [row, row, row, vec, vec]
        out_specs = [row, vec, vec, vec]
        out_shape = [jax.ShapeDtypeStruct((s, d), F32)] + [vec_shape] * 3
    return _pcall(body, name=name, grid=(s // rb,), in_specs=in_specs, out_specs=out_specs,
                          out_shape=out_shape, compiler_params=_params("arbitrary"))(*ins)


def _branch_merge(ya, yb, w_bg, w_bh, z, b_gate, gate_col, bm=512, bn=1024):
    s, gw = ya.shape
    d = w_bg.shape[1]
    bm, bn = _pick(s, bm), _pick(d, bn)
    nj = d // bn

    def body(ya_ref, yb_ref, wa_ref, wb_ref, za_ref, zb_ref, ba_ref, bb_ref, pa_ref, pb_ref, y_ref):
        pa = _dot(ya_ref[...], wa_ref[...])
        pb = _dot(yb_ref[...], wb_ref[...])
        pa_ref[...] = pa.astype(BF16)
        pb_ref[...] = pb.astype(BF16)
        ga = _sigmoid(za_ref[...] + ba_ref[...])
        gb = _sigmoid(zb_ref[...] + bb_ref[...])
        y_ref[...] = (ga * pa + gb * pb).astype(BF16)

    act = pl.BlockSpec((bm, gw), lambda j, i: (i, 0))
    wgt = pl.BlockSpec((gw, bn), lambda j, i: (0, j))
    tile = pl.BlockSpec((bm, bn), lambda j, i: (i, j))
    shape = jax.ShapeDtypeStruct((s, d), BF16)
    return _pcall(
        body, name="branch_merge", grid=(nj, s // bm),
        in_specs=[act, act, wgt, wgt,
                  pl.BlockSpec((bm, bn), lambda j, i: (i, gate_col * nj + j)),
                  pl.BlockSpec((bm, bn), lambda j, i: (i, (gate_col + 1) * nj + j)),
                  pl.BlockSpec((1, bn), lambda j, i: (0, j)), pl.BlockSpec((1, bn), lambda j, i: (0, nj + j))],
        out_specs=[tile, tile, tile], out_shape=[shape, shape, shape], compiler_params=_params("parallel", "parallel"),
    )(ya, yb, w_bg, w_bh, z, z, b_gate, b_gate)


def _branch_wgrad(ya, yb, dpa, dpb):
    s, gw = ya.shape
    per = dpa.shape[1] // N_CHIPS

    def body(ya_ref, yb_ref, da_ref, db_ref, ga_ref, gb_ref):
        ga_ref[...] = _dot(ya_ref[...], da_ref[...], TN).astype(BF16)
        gb_ref[...] = _dot(yb_ref[...], db_ref[...], TN).astype(BF16)

    act = pl.BlockSpec((s, gw), lambda j: (0, 0))
    grd = pl.BlockSpec((s, per), lambda j: (0, j))
    out = pl.BlockSpec((None, gw, per), lambda j: (j, 0, 0))
    shape = jax.ShapeDtypeStruct((N_CHIPS, gw, per), BF16)
    return _pcall(body, name="branch_wgrad", grid=(N_CHIPS,), in_specs=[act, act, grd, grd], out_specs=[out, out],
                  out_shape=[shape, shape], compiler_params=_params("parallel"))(ya, yb, dpa, dpb)


def _branch_bwd(dpa, dpb, w_bg, w_bh, bm=512):
    s, d = dpa.shape
    gw = w_bg.shape[0]
    bm = _pick(s, bm)

    def body(da_ref, db_ref, wa_ref, wb_ref, oa_ref, ob_ref):
        oa_ref[...] = _dot(da_ref[...], wa_ref[...], NT)
        ob_ref[...] = _dot(db_ref[...], wb_ref[...], NT)

    act = pl.BlockSpec((bm, d), lambda i: (i, 0))
    wgt = pl.BlockSpec((gw, d), lambda i: (0, 0))
    out = pl.BlockSpec((bm, gw), lambda i: (i, 0))
    shape = jax.ShapeDtypeStruct((s, gw), F32)
    return _pcall(body, name="branch_bwd", grid=(s // bm,), in_specs=[act, act, wgt, wgt], out_specs=[out, out],
                  out_shape=[shape, shape], compiler_params=_params("parallel"))(dpa, dpb, w_bg, w_bh)


def _out_bwd(dyo, w_out, pa, pb, z, b_gate, gate_col, bm=512, bn=1024):
    s, d = dyo.shape
    bm, bn = _pick(s, bm), _pick(d, bn)
    nj = d // bn

    def body(dyo_ref, w_ref, pa_ref, pb_ref, za_ref, zb_ref, ba_ref, bb_ref,
             dpa_ref, dpb_ref, dza_ref, dzb_ref, dba_ref, dbb_ref):
        first = pl.program_id(1) == 0
        dyv = _dot(dyo_ref[...], w_ref[...], NT)
        ga = _sigmoid(za_ref[...] + ba_ref[...])
        gb = _sigmoid(zb_ref[...] + bb_ref[...])
        dpa_ref[...] = (dyv * ga).astype(BF16)
        dpb_ref[...] = (dyv * gb).astype(BF16)
        dga = dyv * pa_ref[...].astype(F32) * ga * (1.0 - ga)
        dgb = dyv * pb_ref[...].astype(F32) * gb * (1.0 - gb)
        dza_ref[...] = dga.astype(BF16)
        dzb_ref[...] = dgb.astype(BF16)
        _accumulate(first, dba_ref, _colsum(dga))
        _accumulate(first, dbb_ref, _colsum(dgb))

    tile = pl.BlockSpec((bm, bn), lambda j, i: (i, j))
    vec = pl.BlockSpec((1, bn), lambda j, i: (0, j))
    act, vec_shape = jax.ShapeDtypeStruct((s, d), BF16), jax.ShapeDtypeStruct((1, d), F32)
    return _pcall(
        body, name="out_bwd", grid=(nj, s // bm),
        in_specs=[pl.BlockSpec((bm, d), lambda j, i: (i, 0)), pl.BlockSpec((bn, d), lambda j, i: (j, 0)), tile, tile,
                  pl.BlockSpec((bm, bn), lambda j, i: (i, gate_col * nj + j)),
                  pl.BlockSpec((bm, bn), lambda j, i: (i, (gate_col + 1) * nj + j)),
                  vec, pl.BlockSpec((1, bn), lambda j, i: (0, nj + j))],
        out_specs=[tile, tile, tile, tile, vec, vec], out_shape=[act, act, act, act, vec_shape, vec_shape],
        compiler_params=_params("parallel", "arbitrary"),
    )(dyo, w_out, pa, pb, z, z, b_gate, b_gate)


def _ffn_in(h, w_fi, bm=512):
    s, d = h.shape
    per = w_fi.shape[2]
    ff = 2 * per
    bm, bn = _pick(s, bm), _half_width(per)
    per_blocks = per // bn

    def body(h_ref, wa_ref, wu_ref, a_ref, u_ref, hf_ref):
        hv = h_ref[...]
        a = _dot(hv, wa_ref[...])
        up = _dot(hv, wu_ref[...])
        a_ref[...] = a.astype(BF16)
        u_ref[...] = up.astype(BF16)
        hf_ref[...] = (a * _sigmoid(a) * up).astype(BF16)

    out = pl.BlockSpec((bm, bn), lambda j, i: (i, j))
    shape = jax.ShapeDtypeStruct((s, ff), BF16)
    return _pcall(
        body, name="ffn_in", grid=(ff // bn, s // bm),
        in_specs=[pl.BlockSpec((bm, d), lambda j, i: (i, 0)),
                  pl.BlockSpec((None, d, bn), lambda j, i: (j // per_blocks, 0, j % per_blocks)),
                  pl.BlockSpec((None, d, bn), lambda j, i: (2 + j // per_blocks, 0, j % per_blocks))],
        out_specs=[out, out, out], out_shape=[shape, shape, shape],
        compiler_params=_params("parallel", "parallel"),
    )(h, w_fi, w_fi)


def _ffn_out_bwd(dffn, w_fo, a_act, up_act, bm=512):
    s, d = dffn.shape
    ff = w_fo.shape[0]
    bm, bn = _pick(s, bm), _half_width(ff // 2)

    def body(d_ref, w_ref, a_ref, u_ref, da_ref, du_ref):
        dhf = _dot(d_ref[...], w_ref[...], NT)
        a = a_ref[...].astype(F32)
        sa = _sigmoid(a)
        da_ref[...] = (dhf * u_ref[...].astype(F32) * sa * (1.0 + a * (1.0 - sa))).astype(BF16)
        du_ref[...] = (dhf * a * sa).astype(BF16)

    tile = pl.BlockSpec((bm, bn), lambda j, i: (i, j))
    shape = jax.ShapeDtypeStruct((s, ff), BF16)
    return _pcall(
        body, name="ffn_out_bwd", grid=(ff // bn, s // bm),
        in_specs=[pl.BlockSpec((bm, d), lambda j, i: (i, 0)), pl.BlockSpec((bn, d), lambda j, i: (j, 0)), tile, tile],
        out_specs=[tile, tile], out_shape=[shape, shape], compiler_params=_params("parallel", "parallel"),
    )(dffn, w_fo, a_act, up_act)


def _tril(n):
    return lax.broadcasted_iota(jnp.int32, (n, n), 0) >= lax.broadcasted_iota(jnp.int32, (n, n), 1)


def _gmlp_norm(v, ln_g, ln_b):
    gv = _gelu(v)
    mu = jnp.mean(gv, axis=-1, keepdims=True)
    cen = gv - mu
    rs = lax.rsqrt(jnp.mean(cen * cen, axis=-1, keepdims=True) + EPS)
    xh = cen * rs
    return xh, rs, xh * ln_g + ln_b


def _gmlp_fwd(z, ln_g, ln_b, ws, bs_t):
    s = z.shape[0]
    gw = ln_g.shape[1]
    groups, chunk, _ = ws.shape

    def body(u_ref, v_ref, lg_ref, lb_ref, ws_ref, bs_ref, ya_ref):
        gu = _gelu(u_ref[...])
        _, _, vn = _gmlp_norm(v_ref[...], lg_ref[...], lb_ref[...])
        mask = _tril(chunk)
        for g in range(groups):
            cols = slice(g * LANES, (g + 1) * LANES)
            wm = jnp.where(mask, ws_ref[g], 0.0).astype(BF16)
            sg = _dot(wm, vn[:, cols].astype(BF16)) + bs_ref[:, g:g + 1]
            ya_ref[:, cols] = (gu[:, cols] * sg).astype(BF16)

    return _pcall(
        body, name="gmlp_fwd", grid=(s // chunk,),
        in_specs=[_row_spec(chunk, gw, 0), _row_spec(chunk, gw, 1), _vec_spec(gw), _vec_spec(gw),
                  pl.BlockSpec((groups, chunk, chunk), lambda i: (0, 0, 0)), pl.BlockSpec((chunk, LANES), lambda i: (0, 0))],
        out_specs=_row_spec(chunk, gw), out_shape=jax.ShapeDtypeStruct((s, gw), BF16),
        compiler_params=_params("parallel"),
    )(z, z, ln_g, ln_b, ws, bs_t)


def _gmlp_bwd(dya, z, ln_g, ln_b, ws, bs_t):
    s = z.shape[0]
    gw = ln_g.shape[1]
    groups, chunk, _ = ws.shape

    def body(dya_ref, u_ref, v_ref, lg_ref, lb_ref, ws_ref, bs_ref, duv_ref, dws_ref, dbs_ref, dlg_ref, dlb_ref, dvn_ref):
        first = pl.program_id(0) == 0
        u, v, lg = u_ref[...], v_ref[...], lg_ref[...]
        gu = _gelu(u)
        xh, rs, vn = _gmlp_norm(v, lg, lb_ref[...])
        dyav = dya_ref[...]
        mask = _tril(chunk)
        lane = lax.broadcasted_iota(jnp.int32, (chunk, LANES), 1)
        dbs = jnp.zeros((chunk, LANES), F32)
        for g in range(groups):
            cols = slice(g * LANES, (g + 1) * LANES)
            wm = jnp.where(mask, ws_ref[g], 0.0).astype(BF16)
            vg = vn[:, cols].astype(BF16)
            sg = _dot(wm, vg) + bs_ref[:, g:g + 1]
            ds = dyav[:, cols] * gu[:, cols]
            duv_ref[:, cols] = (dyav[:, cols] * sg * _gelu_grad(u[:, cols])).astype(BF16)
            dsb = ds.astype(BF16)
            _accumulate(first, dws_ref.at[g], jnp.where(mask, _dot(dsb, vg, NT), 0.0))
            dbs = dbs + jnp.where(lane == g, jnp.sum(ds, axis=-1, keepdims=True), 0.0)
            dvn_ref[:, cols] = _dot(wm, dsb, TN)
        dvn = dvn_ref[...]
        _accumulate(first, dbs_ref, dbs)
        _accumulate(first, dlb_ref, _colsum(dvn))
        _accumulate(first, dlg_ref, _colsum(dvn * xh))
        dxh = dvn * lg
        dgv = rs * (dxh - jnp.mean(dxh, axis=-1, keepdims=True) - xh * jnp.mean(dxh * xh, axis=-1, keepdims=True))
        duv_ref[:, gw:] = (dgv * _gelu_grad(v)).astype(BF16)

    return _pcall(
        body, name="gmlp_bwd", grid=(s // chunk,),
        in_specs=[_row_spec(chunk, gw), _row_spec(chunk, gw, 0), _row_spec(chunk, gw, 1), _vec_spec(gw), _vec_spec(gw),
                  pl.BlockSpec((groups, chunk, chunk), lambda i: (0, 0, 0)), pl.BlockSpec((chunk, LANES), lambda i: (0, 0))],
        out_specs=[_row_spec(chunk, 2 * gw), pl.BlockSpec((groups, chunk, chunk), lambda i: (0, 0, 0)),
                   pl.BlockSpec((chunk, LANES), lambda i: (0, 0)), _vec_spec(gw), _vec_spec(gw)],
        out_shape=[jax.ShapeDtypeStruct((s, 2 * gw), BF16), jax.ShapeDtypeStruct((groups, chunk, chunk), F32),
                   jax.ShapeDtypeStruct((chunk, LANES), F32), jax.ShapeDtypeStruct((1, gw), F32),
                   jax.ShapeDtypeStruct((1, gw), F32)],
        scratch_shapes=[pltpu.VMEM((chunk, gw), F32)],
        compiler_params=_params("arbitrary"),
    )(dya, z, z, ln_g, ln_b, ws, bs_t)


def _lower_bound(lb_ref):
    a0, a1 = lb_ref[0:1, :], lb_ref[1:2, :]
    mx = jnp.maximum(a0, a1)
    e0, e1 = jnp.exp(a0 - mx), jnp.exp(a1 - mx)
    return e0 / (e0 + e1)


def _sum_dot(mask, x):
    hi = x.astype(BF16)
    rest = x - hi.astype(F32)
    mid = rest.astype(BF16)
    low = (rest - mid.astype(F32)).astype(BF16)
    return _dot(mask, hi) + _dot(mask, mid) + _dot(mask, low)


def _ones_where(mask):
    return jnp.where(mask, 1.0, 0.0).astype(BF16)


def _hg_masks(rows, t):
    r = lax.broadcasted_iota(jnp.int32, (rows, rows), 0)
    c = lax.broadcasted_iota(jnp.int32, (rows, rows), 1)
    same = (r // t) == (c // t)
    incl = jnp.logical_and(same, c <= r)
    upto_mid = jnp.logical_and(same, (c % t) <= t // 2)
    rev = jnp.logical_and(same, c >= r)
    return same, incl, upto_mid, rev


def _hg_block(q, fp, lb, masks):
    rows = q.shape[0]
    same, incl, upto_mid, _ = masks
    sig = _sigmoid(fp)
    f = lb + (1.0 - lb) * sig
    k = 1.0 - f
    sq = _sigmoid(q)
    qa = q * sq
    stacked = jnp.concatenate([_ones_where(m) for m in (incl, same, upto_mid)], axis=0)
    sums = _sum_dot(stacked, jnp.log(f))
    b, b_last, b_mid = sums[:rows], sums[rows:2 * rows], sums[2 * rows:]
    e_q = jnp.exp(jnp.minimum(b - b_mid, EXP_CLAMP))
    e_k = jnp.exp(jnp.minimum(b_mid - b, EXP_CLAMP))
    e_in = jnp.exp(b)
    e_out = jnp.exp(b_last - b)
    return dict(sig=sig, f=f, k=k, sq=sq, qa=qa, e_last=jnp.exp(b_last), e_q=e_q, e_k=e_k, e_in=e_in, e_out=e_out,
                q_hat=(qa * e_q).astype(BF16), k_hat=(k * e_k).astype(BF16),
                q_in=(qa * e_in).astype(BF16), k_out=k * e_out)


def _hgrn_fwd(z, hg_lb, norm_g, q_col):
    s = z.shape[0]
    hw = norm_g.shape[1]
    heads = hw // LANES
    t = HG_CHUNK
    rows = min(HG_ROWS, s)
    per_step = rows // t
    hp = min(HG_HEADS_PER_STEP, heads)
    assert heads % hp == 0 and q_col % hp == 0, (heads, q_col)
    wide = hp * LANES

    def zspec(which):
        return pl.BlockSpec((rows, wide), lambda h, r, which=which: (r, (q_col + which * heads) // hp + h))

    def body(q_ref, f_ref, i_ref, g_ref, lb_ref, ng_ref, yb_ref, o_ref, st_out_ref, st_ref, e_last_ref, inter_ref):
        @pl.when(pl.program_id(1) == 0)
        def _():
            st_ref[...] = jnp.zeros_like(st_ref)

        masks = _hg_masks(rows, t)
        for hh in range(hp):
            cols = slice(hh * LANES, (hh + 1) * LANES)
            blk = _hg_block(q_ref[:, cols], f_ref[:, cols], _lower_bound(lb_ref.at[:, cols]), masks)
            iv = i_ref[:, cols].astype(BF16)
            q_in, k_out = blk["q_in"], blk["k_out"].astype(BF16)
            e_last_ref[hh] = blk["e_last"]
            attn = jnp.where(masks[1], _dot(blk["q_hat"], blk["k_hat"], NT), 0.0).astype(BF16)
            o = _dot(attn, iv)
            grown = [_dot(iv[j * t:(j + 1) * t], k_out[j * t:(j + 1) * t], TN) for j in range(per_step)]
            st = st_ref[hh]
            for j in range(per_step):
                st_out_ref[hh, j] = st
                inter_ref[hh, j * t:(j + 1) * t, :] = _dot(q_in[j * t:(j + 1) * t], st.astype(BF16), NT)
                st = st * e_last_ref[hh, j * t:j * t + 1, :] + grown[j]
            st_ref[hh] = st
            o = o + inter_ref[hh]
            o_ref[:, cols] = o
            og = g_ref[:, cols]
            on, _ = _rms(o)
            yb_ref[:, cols] = (on * ng_ref[:, cols] * (og * _sigmoid(og))).astype(BF16)

    out_row = pl.BlockSpec((rows, wide), lambda h, r: (r, h))
    return _pcall(
        body, name="hgrn_fwd", grid=(heads // hp, s // rows),
        in_specs=[zspec(0), zspec(1), zspec(2), zspec(3),
                  pl.BlockSpec((2, wide), lambda h, r: (0, h)), pl.BlockSpec((1, wide), lambda h, r: (0, h))],
        out_specs=[out_row, out_row, pl.BlockSpec((hp, per_step, LANES, LANES), lambda h, r: (h, r, 0, 0))],
        out_shape=[jax.ShapeDtypeStruct((s, hw), BF16), jax.ShapeDtypeStruct((s, hw), F32),
                   jax.ShapeDtypeStruct((heads, s // t, LANES, LANES), F32)],
        scratch_shapes=[pltpu.VMEM((hp, LANES, LANES), F32), pltpu.VMEM((hp, rows, LANES), F32),
                        pltpu.VMEM((hp, rows, LANES), F32)],
        compiler_params=_params("parallel", "arbitrary"),
    )(z, z, z, z, hg_lb, norm_g)


def _hgrn_bwd(dyb, z, o_raw, states, hg_lb, norm_g, q_col):
    s = z.shape[0]
    hw = norm_g.shape[1]
    heads = hw // LANES
    t = HG_CHUNK
    rows = min(HG_ROWS, s)
    per_step = rows // t
    n_steps = s // rows
    hp = min(HG_HEADS_PER_STEP, heads)
    assert heads % hp == 0 and q_col % hp == 0, (heads, q_col)
    wide = hp * LANES

    def zspec(which):
        return pl.BlockSpec((rows, wide), lambda h, r, which=which: (n_steps - 1 - r, (q_col + which * heads) // hp + h))

    def body(dyb_ref, q_ref, f_ref, i_ref, g_ref, o_ref, st_in_ref, lb_ref, ng_ref,
             dq_ref, df_ref, di_ref, dg_ref, dlb_ref, dng_ref, dst_ref, acc_lb_ref, acc_ng_ref,
             e_last_ref, dq_in_ref, dk_out_ref, di_inter_ref, carry_ref):
        step = pl.program_id(1)

        @pl.when(step == 0)
        def _():
            dst_ref[...] = jnp.zeros_like(dst_ref)
            acc_lb_ref[...] = jnp.zeros_like(acc_lb_ref)
            acc_ng_ref[...] = jnp.zeros_like(acc_ng_ref)

        masks = _hg_masks(rows, t)
        same, incl, _, rev = masks
        sum_mask = jnp.concatenate([_ones_where(rev), _ones_where(same)], axis=1)
        for hh in range(hp):
            cols = slice(hh * LANES, (hh + 1) * LANES)
            lb = _lower_bound(lb_ref.at[:, cols])
            ng = ng_ref[:, cols]
            q = q_ref[:, cols]
            blk = _hg_block(q, f_ref[:, cols], lb, masks)
            iv = i_ref[:, cols].astype(BF16)
            o, og, dy = o_ref[:, cols], g_ref[:, cols], dyb_ref[:, cols]
            so = _sigmoid(og)
            on, r = _rms(o)
            acc_ng_ref[:, cols] += _colsum(dy * on * (og * so))
            dg_ref[:, cols] = (dy * on * ng * so * (1.0 + og * (1.0 - so))).astype(BF16)
            don = dy * ng * (og * so)
            do = (r * (don - on * jnp.mean(don * on, axis=-1, keepdims=True))).astype(BF16)
            q_hat, k_hat, q_in, k_out = blk["q_hat"], blk["k_hat"], blk["q_in"], blk["k_out"]
            k_out_b = k_out.astype(BF16)
            attn = jnp.where(incl, _dot(q_hat, k_hat, NT), 0.0).astype(BF16)
            d_attn = jnp.where(incl, _dot(do, iv, NT), 0.0).astype(BF16)
            di_intra = _dot(attn, do, TN)
            dq_hat = _dot(d_attn, k_hat)
            dk_hat = _dot(d_attn, q_hat, TN)
            e_last_ref[hh] = blk["e_last"]
            grown = [_dot(do[j * t:(j + 1) * t], q_in[j * t:(j + 1) * t], TN) for j in range(per_step)]
            dst = dst_ref[hh]
            for j in reversed(range(per_step)):
                rs_ = slice(j * t, (j + 1) * t)
                e_last = e_last_ref[hh, j * t:j * t + 1, :]
                st_prev, dst_b = st_in_ref[hh, j], dst.astype(BF16)
                dq_in_ref[hh, rs_, :] = _dot(do[rs_], st_prev.astype(BF16))
                dk_out_ref[hh, rs_, :] = _dot(iv[rs_], dst_b)
                di_inter_ref[hh, rs_, :] = _dot(k_out_b[rs_], dst_b, NT)
                carry_ref[hh, rs_, :] = jnp.broadcast_to(e_last * _colsum(st_prev * dst), (t, LANES))
                dst = dst * e_last + grown[j]
            dst_ref[hh] = dst
            di_ref[:, cols] = (di_intra + di_inter_ref[hh]).astype(BF16)
            dk_out = dk_out_ref[hh]
            dqa = dq_in_ref[hh] * blk["e_in"] + dq_hat * blk["e_q"]
            dk = dk_out * blk["e_out"] + dk_hat * blk["e_k"]
            db = blk["qa"] * dqa - blk["k"] * dk
            dlf = _sum_dot(sum_mask, jnp.concatenate([db, dk_out * k_out], axis=0)) + carry_ref[hh]
            dfv = dlf / blk["f"] - dk
            sig, sq = blk["sig"], blk["sq"]
            df_ref[:, cols] = (dfv * (1.0 - lb) * sig * (1.0 - sig)).astype(BF16)
            acc_lb_ref[:, cols] += _colsum(dfv * (1.0 - sig))
            dq_ref[:, cols] = (dqa * sq * (1.0 + q * (1.0 - sq))).astype(BF16)

        @pl.when(step == n_steps - 1)
        def _():
            lb = _lower_bound(lb_ref)
            d0 = acc_lb_ref[...] * lb * (1.0 - lb)
            dlb_ref[0:1, :] = d0
            dlb_ref[1:2, :] = -d0
            dng_ref[...] = acc_ng_ref[...]

    rev_row = pl.BlockSpec((rows, wide), lambda h, r: (n_steps - 1 - r, h))
    piece = jax.ShapeDtypeStruct((s, hw), BF16)
    return _pcall(
        body, name="hgrn_bwd", grid=(heads // hp, n_steps),
        in_specs=[rev_row, zspec(0), zspec(1), zspec(2), zspec(3), rev_row,
                  pl.BlockSpec((hp, per_step, LANES, LANES), lambda h, r: (h, n_steps - 1 - r, 0, 0)),
                  pl.BlockSpec((2, wide), lambda h, r: (0, h)), pl.BlockSpec((1, wide), lambda h, r: (0, h))],
        out_specs=[rev_row, rev_row, rev_row, rev_row,
                   pl.BlockSpec((2, wide), lambda h, r: (0, h)), pl.BlockSpec((1, wide), lambda h, r: (0, h))],
        out_shape=[piece, piece, piece, piece, jax.ShapeDtypeStruct((2, hw), F32), jax.ShapeDtypeStruct((1, hw), F32)],
        scratch_shapes=[pltpu.VMEM((hp, LANES, LANES), F32), pltpu.VMEM((1, wide), F32), pltpu.VMEM((1, wide), F32)]
        + [pltpu.VMEM((hp, rows, LANES), F32)] * 5,
        compiler_params=_params("parallel", "arbitrary"),
    )(dyb, z, z, z, z, o_raw, states, hg_lb, norm_g)


def _adam_update(w, m, v, g):
    m2 = ADAM_B1 * m + (1.0 - ADAM_B1) * g
    v2 = ADAM_B2 * v + (1.0 - ADAM_B2) * (g * g)
    m_hat = m2 * (1.0 / (1.0 - ADAM_B1 ** ADAM_STEP))
    v_hat = v2 * (1.0 / (1.0 - ADAM_B2 ** ADAM_STEP))
    return -ADAM_LR * (m_hat / (jnp.sqrt(v_hat) + ADAM_EPS) + ADAM_WD * w), m2, v2


def _adamw(w, m, v, parts, name, outer=False):
    rows, cols = w.shape
    bc = cols if cols <= 4096 else _pick(cols, 4096)
    rb = _pick_rows(rows, max(8, (384 * 1024) // bc), mult=8)
    if outer and rb % LANES:
        rb = rows

    def body(w_ref, m_ref, v_ref, *refs):
        g_ref, d_ref, mo_ref, vo_ref = refs[-4:]
        if outer:
            cv = refs[0][...]
            g = _dot(cv * _sigmoid(cv), refs[1][...], TN, lax.Precision.HIGHEST)
        else:
            p_ref = refs[0]
            g = p_ref[0].astype(F32)
            for p in range(1, p_ref.shape[0]):
                g = g + p_ref[p].astype(F32)
        g_ref[...] = g
        d_ref[...], mo_ref[...], vo_ref[...] = _adam_update(w_ref[...], m_ref[...], v_ref[...], g)

    blk = pl.BlockSpec((rb, bc), lambda i, j: (i, j))
    out = jax.ShapeDtypeStruct((rows, cols), F32)
    if outer:
        grad_specs = [pl.BlockSpec((N_DEV, rb), lambda i, j: (0, i)), pl.BlockSpec((N_DEV, bc), lambda i, j: (0, j))]
        grad_ins = tuple(parts)
    else:
        grad_specs = [pl.BlockSpec((parts.shape[0], rb, bc), lambda i, j: (0, i, j))]
        grad_ins = (parts,)
    return _pcall(
        body, name=name, grid=(rows // rb, cols // bc), in_specs=[blk, blk, blk] + grad_specs,
        out_specs=[blk] * 4, out_shape=[out] * 4, compiler_params=_params("parallel", "parallel"),
    )(w, m, v, *grad_ins)


SMALL = ("b_ada", "norm1_g", "b_gate", "gmlp_ln_g", "gmlp_ln_b", "gmlp_ws", "gmlp_bs", "hg_lb", "hg_norm_g",
         "norm2_g", "final_norm_g")
BIG = ("w_in", "w_branch_gmlp", "w_branch_hg", "w_out", "w_ffn_in", "w_ffn_out")
WEIGHTS = ("w_ada", "b_ada", "norm1_g", "w_in", "b_gate", "gmlp_ln_g", "gmlp_ln_b", "gmlp_ws", "gmlp_bs", "hg_lb",
           "hg_norm_g", "w_branch_gmlp", "w_branch_hg", "w_out", "norm2_g", "w_ffn_in", "w_ffn_out", "final_norm_g")


def _pack(parts):
    return jnp.concatenate([p.reshape(-1, LANES) for p in parts], axis=0)


def _step(x, c, loss_target, w, m, v):
    s, d = x.shape[1], x.shape[2]
    gw = w["gmlp_ln_g"].shape[-1]
    hw = w["hg_norm_g"].shape[-1]
    x2d, tgt = x[0], loss_target[0]
    mx, my, mc = lax.axis_index("x"), lax.axis_index("y"), lax.axis_index("c")
    chip = 2 * mx + my
    dev = 2 * chip + mc
    q_col = 2 * gw // LANES
    gate_col = (2 * gw + 4 * hw) // d
    place = jnp.stack([chip, mc]).astype(jnp.int32)
    _Order.last = None

    c_all = _all_gather8(c.reshape(-1, LANES), "gather_c").reshape(N_DEV, d)
    gather_in = _WeightGather("in", _WeightGather.zones([w["w_in"][0]], chip), ring=True)
    zones_mix = _WeightGather.zones([w[n][0] for n in ("w_branch_gmlp", "w_branch_hg", "w_out")], chip)
    zones_fi = _WeightGather.zones([w["w_ffn_in"][0]], chip)
    zones_fo = _WeightGather.zones([w["w_ffn_out"][0]], chip)
    _Order.also = (*zones_mix, *zones_fi, *zones_fo)
    n_ada = w["w_ada"].shape[-1]
    b_ada_q = lax.dynamic_slice(w["b_ada"], (0, chip * n_ada), (1, n_ada))
    mod_q = _ada_fwd(c_all, w["w_ada"][0], b_ada_q)
    mod_all = _all_gather8(mod_q, "gather_mod")
    mod = lax.dynamic_index_in_dim(mod_all, dev, axis=1, keepdims=False)[::2].reshape(1, 6 * d)
    sh1, sc1, gt1, sh2, sc2, gt2 = [mod[:, i * d:(i + 1) * d] for i in range(6)]

    gather_in.relay_and_pass_near()
    gather_mix = _WeightGather("mix", zones_mix)
    gather_fi = _WeightGather("fi", zones_fi, ring=True)

    norm1_g, norm2_g, final_g = w["norm1_g"], w["norm2_g"], w["final_norm_g"].reshape(1, d)
    ln_g, ln_b = w["gmlp_ln_g"], w["gmlp_ln_b"]
    ws = w["gmlp_ws"][0]
    groups = ws.shape[0]
    bs_t = jnp.pad(w["gmlp_bs"][0].T, ((0, 0), (0, LANES - groups)))
    hg_lb, hg_ng, b_gate = w["hg_lb"], w["hg_norm_g"], w["b_gate"]

    h1 = _norm_mod_fwd(x2d, norm1_g, sc1, sh1, "norm1_fwd")
    order = jnp.stack([chip, 2 * (1 - mx) + my, 2 * mx + (1 - my), 2 * (1 - mx) + (1 - my)]).astype(jnp.int32)
    w_in, = gather_in.near_done()
    z = _matmul_quarters(h1, w_in, order, 0, 3, "mm_z_near")
    w_in, = gather_in.pass_far()
    z = _matmul_quarters(h1, w_in, order, 3, 1, "mm_z_far", prev=z)
    gather_mix.pass_on()
    ya = _gmlp_fwd(z, ln_g, ln_b, ws, bs_t)
    yb, o_raw, states = _hgrn_fwd(z, hg_lb, hg_ng, q_col)
    gather_fi.relay()
    gather_fo = _WeightGather("fo", zones_fo)
    w_bg, w_bh, w_out = gather_mix.done()
    w_bg, w_bh = [wq.transpose(1, 0, 2).reshape(wq.shape[1], -1) for wq in (w_bg, w_bh)]
    w_out = w_out.reshape(-1, w_out.shape[-1])
    pa, pb, y = _branch_merge(ya, yb, w_bg, w_bh, z, b_gate, gate_col)
    yo = _matmul(y, w_out, mode="nn", name="mm_yo", out_dtype=F32)
    gather_fi.pass_on()
    x1, h2 = _norm_mod_fwd(x2d, norm2_g, sc2, sh2, "norm2_fwd", res=yo, gt=gt1)
    w_fi, = gather_fi.done()
    a_act, up_act, hf = _ffn_in(h2, w_fi)
    gather_fo.pass_on()
    w_fo, = gather_fo.done()
    w_fo = w_fo.reshape(-1, w_fo.shape[-1])
    ffn = _matmul(hf, w_fo, mode="nn", name="mm_ffn", out_dtype=F32)
    dx2, dffn, loss_row, d_final_g, d_gt2 = _final_loss(x1, ffn, gt2, final_g, tgt)

    g_fo = _matmul(hf, dffn, mode="tn", name="mm_g_fo", out_dtype=BF16, bm=_half_width(hf.shape[1] // 2))
    daup = tuple(_ffn_out_bwd(dffn, w_fo, a_act, up_act))
    g_fi = _matmul(h2, daup, mode="tn", name="mm_g_fi", out_dtype=BF16, out_slots=True, bn=_half_width(w_fi.shape[2]))
    red_ffn = _GradReduce("ffn", [g_fo.reshape(N_CHIPS, -1, g_fo.shape[-1]), g_fi])
    dh2 = _matmul(daup, w_fi, mode="nt", name="mm_dh2", out_dtype=F32, b_slots=True)
    red_ffn.step(place)
    dx1, dyo, d_sh2, d_sc2, d_norm2, d_gt1 = _norm_mod_bwd(dh2, x1, dx2, norm2_g, sc2, "norm2_bwd", branch=yo, gt=gt1)
    g_out = _matmul(y, dyo, mode="tn", name="mm_g_out", out_dtype=BF16)
    dpa, dpb, dz_ga, dz_gb, d_b_ga, d_b_gb = _out_bwd(dyo, w_out, pa, pb, z, b_gate, gate_col)
    g_bg, g_bh = _branch_wgrad(ya, yb, dpa, dpb)
    red_mix = _GradReduce("mix", [g_out.reshape(N_CHIPS, -1, g_out.shape[-1]), g_bg, g_bh])
    dya, dyb = _branch_bwd(dpa, dpb, w_bg, w_bh)
    red_mix.step(place)
    dz_uv, d_ws, d_bs_t, d_ln_g, d_ln_b = _gmlp_bwd(dya, z, ln_g, ln_b, ws, bs_t)
    dz_q, dz_f, dz_i, dz_g, d_hg_lb, d_hg_ng = _hgrn_bwd(dyb, z, o_raw, states, hg_lb, hg_ng, q_col)
    dz = jnp.concatenate([dz_uv, dz_q, dz_f, dz_i, dz_g, dz_ga, dz_gb], axis=1)
    grad, delta, new_m, new_v = {}, {}, {}, {}

    def update(n, parts, outer=False):
        outs = _adamw(w[n][0], m[n][0], v[n][0], parts, "adamw_" + n, outer=outer)
        grad[n], delta[n], new_m[n], new_v[n] = [o[None] for o in outs]

    red_ffn.join(place)
    g_fo, g_fi = red_ffn.done()
    n = "w_ffn_in"
    g_in, rode = _matmul(h1, dz, mode="tn", name="mm_g_in", out_dtype=BF16, out_slots=True, bn=_half_width(w_in.shape[2]),
                         rider=([w[n][0], m[n][0], v[n][0], g_fi], lambda *blk: (blk[3], *_adam_update(*blk)), 4))
    grad[n], delta[n], new_m[n], new_v[n] = [o[None] for o in rode]
    red_in = _GradReduce("in", [g_in])
    red_mix.join(place)
    red_in.step(place)
    dh1 = _matmul(dz, w_in, mode="nt", name="mm_dh1", out_dtype=F32, b_slots=True)
    grad_x, d_sh1, d_sc1, d_norm1 = _norm_mod_bwd(dh1, x2d, dx1, norm1_g, sc1, "norm1_bwd")
    update("w_ffn_out", g_fo[None])
    g_out, g_bg, g_bh = red_mix.done()
    update("w_out", g_out[None])
    update("w_branch_gmlp", g_bg[None])
    update("w_branch_hg", g_bh[None])

    d_mod = jnp.concatenate([d_sh1, d_sc1, d_gt1, d_sh2, d_sc2, d_gt2], axis=1)
    small_part = {"b_ada": d_mod, "norm1_g": d_norm1, "b_gate": jnp.concatenate([d_b_ga, d_b_gb], axis=1), "gmlp_ln_g": d_ln_g, "gmlp_ln_b": d_ln_b,
                  "gmlp_ws": d_ws, "gmlp_bs": d_bs_t[:, :groups].T, "hg_lb": d_hg_lb, "hg_norm_g": d_hg_ng,
                  "norm2_g": d_norm2, "final_norm_g": d_final_g}
    loss_rows = jnp.broadcast_to(loss_row, (8, LANES))
    small_all = _all_gather8(_pack([small_part[n] for n in SMALL] + [loss_rows]), "gather_small")
    d_mod_all = small_all[:, :6 * d // LANES].reshape(N_DEV, 6 * d)
    d_mod_q = lax.dynamic_slice(d_mod_all, (0, chip * n_ada), (N_DEV, n_ada))
    red_in.join(place)
    update("w_ada", (c_all, d_mod_q), outer=True)
    pad = [jnp.zeros_like(loss_rows)]
    outs = _adamw(_pack([w[n] for n in SMALL] + pad), _pack([m[n] for n in SMALL] + pad),
                  _pack([v[n] for n in SMALL] + pad), small_all, "adamw_small")
    update("w_in", red_in.done()[0][None])
    row = 0
    for n in SMALL:
        cnt = w[n].size // LANES
        for dst, o in zip((grad, delta, new_m, new_v), outs):
            dst[n] = o[row:row + cnt].reshape(w[n].shape)
        row += cnt

    loss = outs[0][row, 0]
    return (loss, grad_x[None], *[grad[n] for n in WEIGHTS], *[delta[n] for n in WEIGHTS],
            *[new_m[n] for n in WEIGHTS], *[new_v[n] for n in WEIGHTS])


def kernel(x, c, w_ada, b_ada, norm1_g, w_in, b_gate, gmlp_ln_g, gmlp_ln_b, gmlp_ws, gmlp_bs, hg_lb, hg_norm_g, w_branch_gmlp, w_branch_hg, w_out, norm2_g, w_ffn_in, w_ffn_out, final_norm_g, loss_target, m_w_ada, m_b_ada, m_norm1_g, m_w_in, m_b_gate, m_gmlp_ln_g, m_gmlp_ln_b, m_gmlp_ws, m_gmlp_bs, m_hg_lb, m_hg_norm_g, m_w_branch_gmlp, m_w_branch_hg, m_w_out, m_norm2_g, m_w_ffn_in, m_w_ffn_out, m_final_norm_g, v_w_ada, v_b_ada, v_norm1_g, v_w_in, v_b_gate, v_gmlp_ln_g, v_gmlp_ln_b, v_gmlp_ws, v_gmlp_bs, v_hg_lb, v_hg_norm_g, v_w_branch_gmlp, v_w_branch_hg, v_w_out, v_norm2_g, v_w_ffn_in, v_w_ffn_out, v_final_norm_g):
    w = dict(w_ada=w_ada, b_ada=b_ada, norm1_g=norm1_g, w_in=w_in, b_gate=b_gate, gmlp_ln_g=gmlp_ln_g,
             gmlp_ln_b=gmlp_ln_b, gmlp_ws=gmlp_ws, gmlp_bs=gmlp_bs, hg_lb=hg_lb, hg_norm_g=hg_norm_g,
             w_branch_gmlp=w_branch_gmlp, w_branch_hg=w_branch_hg, w_out=w_out, norm2_g=norm2_g,
             w_ffn_in=w_ffn_in, w_ffn_out=w_ffn_out, final_norm_g=final_norm_g)
    m = dict(w_ada=m_w_ada, b_ada=m_b_ada, norm1_g=m_norm1_g, w_in=m_w_in, b_gate=m_b_gate, gmlp_ln_g=m_gmlp_ln_g,
             gmlp_ln_b=m_gmlp_ln_b, gmlp_ws=m_gmlp_ws, gmlp_bs=m_gmlp_bs, hg_lb=m_hg_lb, hg_norm_g=m_hg_norm_g,
             w_branch_gmlp=m_w_branch_gmlp, w_branch_hg=m_w_branch_hg, w_out=m_w_out, norm2_g=m_norm2_g,
             w_ffn_in=m_w_ffn_in, w_ffn_out=m_w_ffn_out, final_norm_g=m_final_norm_g)
    v = dict(w_ada=v_w_ada, b_ada=v_b_ada, norm1_g=v_norm1_g, w_in=v_w_in, b_gate=v_b_gate, gmlp_ln_g=v_gmlp_ln_g,
             gmlp_ln_b=v_gmlp_ln_b, gmlp_ws=v_gmlp_ws, gmlp_bs=v_gmlp_bs, hg_lb=v_hg_lb, hg_norm_g=v_hg_norm_g,
             w_branch_gmlp=v_w_branch_gmlp, w_branch_hg=v_w_branch_hg, w_out=v_w_out, norm2_g=v_norm2_g,
             w_ffn_in=v_w_ffn_in, w_ffn_out=v_w_ffn_out, final_norm_g=v_final_norm_g)
    return _step(x, c, loss_target, w, m, v)
```

```python
import jax
import jax.numpy as jnp
from jax import lax
from jax.experimental import pallas as pl
from jax.experimental.pallas import tpu as pltpu

F32 = jnp.float32
BF16 = jnp.bfloat16
EPS = 1e-6
LANES = 128
N_CHIPS = 4
N_DEV = 8
VMEM_LIMIT_BYTES = 56 * 1024 * 1024
VMEM_RIDER_LIMIT_BYTES = 60 * 1024 * 1024
HG_CHUNK = 32
HG_ROWS = 256
HG_HEADS_PER_STEP = 8
EXP_CLAMP = 80.0
ADAM_LR, ADAM_B1, ADAM_B2, ADAM_EPS, ADAM_WD, ADAM_STEP = 0.001, 0.9, 0.999, 1e-08, 0.01, 10
MESH = pl.DeviceIdType.MESH

NN = (((1,), (0,)), ((), ()))
NT = (((1,), (1,)), ((), ()))
TN = (((0,), (0,)), ((), ()))


def _dot(a, b, dims=NN, precision=None):
    return lax.dot_general(a, b, dims, precision=precision, preferred_element_type=F32)


def _params(*semantics, vmem_limit_bytes=VMEM_LIMIT_BYTES):
    return pltpu.CompilerParams(dimension_semantics=semantics, vmem_limit_bytes=vmem_limit_bytes)


class _Order:
    last = None
    also = ()


def _pcall(body, *, in_specs, out_specs, grid=(), scratch_shapes=(), num_scalar_prefetch=0, **kw):
    def run(*ins):
        deps = (() if _Order.last is None else (_Order.last,)) + tuple(_Order.also)
        _Order.also = ()
        n_in, n_dep = len(ins), len(deps)

        def wrapped(*refs):
            body(*refs[:n_in], *refs[n_in + n_dep:])

        specs = list(in_specs) + [pl.BlockSpec(memory_space=pl.ANY)] * n_dep
        if num_scalar_prefetch:
            grid_spec = pltpu.PrefetchScalarGridSpec(
                num_scalar_prefetch=num_scalar_prefetch, grid=grid, in_specs=specs, out_specs=out_specs,
                scratch_shapes=scratch_shapes)
            outs = pl.pallas_call(wrapped, grid_spec=grid_spec, **kw)(*ins, *deps)
        else:
            outs = pl.pallas_call(wrapped, grid=grid, in_specs=specs, out_specs=out_specs,
                                  scratch_shapes=scratch_shapes, **kw)(*ins, *deps)
        _Order.last = jax.tree.leaves(outs)[0]
        return outs

    return run


def _pick_rows(dim, pref, mult=16):
    best = None
    for cand in range(mult, min(dim, pref) + 1, mult):
        if dim % cand == 0:
            best = cand
    assert best is not None, (dim, pref)
    return best


def _half_width(n):
    return n // 2 if n % (2 * LANES) == 0 else n


def _pick(dim, pref):
    if dim <= pref:
        return dim
    best = None
    for cand in range(LANES, pref + 1, LANES):
        if dim % cand == 0:
            best = cand
    assert best is not None, (dim, pref)
    return best


def _sigmoid(x):
    return 1.0 / (1.0 + jnp.exp(-x))


def _gelu(x):
    c = 0.7978845608028654
    return 0.5 * x * (1.0 + jnp.tanh(c * (x + 0.044715 * x * x * x)))


def _gelu_grad(x):
    c = 0.7978845608028654
    t = jnp.tanh(c * (x + 0.044715 * x * x * x))
    return 0.5 * (1.0 + t) + 0.5 * x * (1.0 - t * t) * c * (1.0 + 3.0 * 0.044715 * x * x)


def _rms(x):
    r = lax.rsqrt(jnp.mean(x * x, axis=-1, keepdims=True) + EPS)
    return x * r, r


def _colsum(x):
    return jnp.sum(x, axis=0, keepdims=True)


def _accumulate(first, ref, val):
    @pl.when(first)
    def _():
        ref[...] = val

    @pl.when(jnp.logical_not(first))
    def _():
        ref[...] += val


def _matmul(a, b, *, mode, name, out_dtype, b_slots=False, out_slots=False, bm=1024, bn=1024, bk=2816, rider=None):
    pair = a if isinstance(a, tuple) else b if isinstance(b, tuple) else None
    if mode == "nn":
        m, k = a.shape
        n = b.shape[2] * N_CHIPS if b_slots else b.shape[1]
        per = b.shape[2] if b_slots else n
    elif mode == "nt":
        m, k = (a[0].shape[0], 2 * a[0].shape[1]) if pair else a.shape
        n = b.shape[1] if b_slots else b.shape[0]
        per = b.shape[2] if b_slots else k
    else:
        k, m = a.shape
        n = 2 * b[0].shape[1] if pair else b.shape[1]
        per = n // N_CHIPS if out_slots else n
    bm = _pick(m, bm)
    if mode == "nt":
        bn, bk = _pick(n, bn), _pick(per, bk)
    else:
        bn, bk = _pick(per, bn), _pick(k, bk)
    nk = k // bk
    per_blocks = per // (bk if mode == "nt" else bn)
    dims = {"nn": NN, "nt": NT, "tn": TN}[mode]
    half = (nk if mode == "nt" else n // bn) // 2

    def product(a_ref, b_ref, o_ref, acc):
        part = _dot(a_ref[...], b_ref[...], dims)
        if nk == 1:
            o_ref[...] = part.astype(o_ref.dtype)
            return
        acc_ref, kk = acc[0], pl.program_id(2)

        @pl.when(kk == 0)
        def _():
            acc_ref[...] = part

        @pl.when(jnp.logical_and(kk > 0, kk < nk - 1))
        def _():
            acc_ref[...] += part

        @pl.when(kk == nk - 1)
        def _():
            o_ref[...] = (acc_ref[...] + part).astype(o_ref.dtype)

    n_ride, n_rode = (len(rider[0]), rider[2]) if rider else (0, 0)

    def body(*refs):
        if rider:
            ride_in, ride_out = refs[2:2 + n_ride], refs[3 + n_ride:3 + n_ride + n_rode]

            def ride(t, carry):
                rows = pl.ds(pl.multiple_of(t * 8, 8), 8)
                for ref, val in zip(ride_out, rider[1](*[r[rows, :] for r in ride_in])):
                    ref[rows, :] = val
                return carry

            lax.fori_loop(0, ride_in[0].shape[0] // 8, ride, 0)
        if not pair:
            return product(refs[0], refs[1], refs[2 + n_ride], refs[3 + n_ride + n_rode:])
        first = pl.program_id(2 if mode == "nt" else 1) < half
        x1, x2, y, o_ref, acc = refs[0], refs[1], refs[2], refs[3], refs[4:]

        @pl.when(first)
        def _():
            product(*((x1, y) if mode == "nt" else (y, x1)), o_ref, acc)

        @pl.when(jnp.logical_not(first))
        def _():
            product(*((x2, y) if mode == "nt" else (y, x2)), o_ref, acc)

    if mode == "nn":
        a_spec = pl.BlockSpec((bm, bk), lambda i, j, kk: (i, kk))
        if b_slots:
            b_spec = pl.BlockSpec((None, bk, bn), lambda i, j, kk: (j // per_blocks, kk, j % per_blocks))
        else:
            b_spec = pl.BlockSpec((bk, bn), lambda i, j, kk: (kk, j))
    elif mode == "nt":
        a_spec = pl.BlockSpec((bm, bk), lambda i, j, kk: (i, kk))
        if b_slots:
            b_spec = pl.BlockSpec((None, bn, bk), lambda i, j, kk: (kk // per_blocks, j, kk % per_blocks))
        else:
            b_spec = pl.BlockSpec((bn, bk), lambda i, j, kk: (j, kk))
    else:
        a_spec = pl.BlockSpec((bk, bm), lambda i, j, kk: (kk, i))
        b_spec = pl.BlockSpec((bk, bn), lambda i, j, kk: (kk, j))
    if out_slots:
        o_spec = pl.BlockSpec((None, bm, bn), lambda i, j, kk: (j // per_blocks, i, j % per_blocks))
        out_shape = jax.ShapeDtypeStruct((N_CHIPS, m, per), out_dtype)
    else:
        o_spec = pl.BlockSpec((bm, bn), lambda i, j, kk: (i, j))
        out_shape = jax.ShapeDtypeStruct((m, n), out_dtype)
    if not pair:
        ins, in_specs = (a, b), [a_spec, b_spec]
    elif mode == "nt":
        ins = (*a, b)
        in_specs = [pl.BlockSpec((bm, bk), lambda i, j, kk: (i, jnp.minimum(kk, half - 1))),
                    pl.BlockSpec((bm, bk), lambda i, j, kk: (i, jnp.maximum(kk - half, 0))), b_spec]
    else:
        ins = (*b, a)
        in_specs = [pl.BlockSpec((bk, bn), lambda i, j, kk: (kk, jnp.minimum(j, half - 1))),
                    pl.BlockSpec((bk, bn), lambda i, j, kk: (kk, jnp.maximum(j - half, 0))), a_spec]
    scratch = [pltpu.VMEM((bm, bn), F32)] if nk > 1 else []
    if not rider:
        return _pcall(
            body, name=name, grid=(m // bm, n // bn, nk), in_specs=in_specs, out_specs=o_spec, out_shape=out_shape,
            scratch_shapes=scratch, compiler_params=_params("parallel", "parallel", "arbitrary"),
        )(*ins)
    assert not pair
    rows, cols = rider[0][0].shape
    nj = n // bn
    rb = rows // ((m // bm) * nj * nk)
    assert rows == rb * (m // bm) * nj * nk and rb % 8 == 0, (rows, rb)
    ride_spec = pl.BlockSpec((rb, cols), lambda i, j, kk: ((i * nj + j) * nk + kk, 0))
    outs = _pcall(
        body, name=name, grid=(m // bm, nj, nk), in_specs=in_specs + [ride_spec] * n_ride,
        out_specs=[o_spec] + [ride_spec] * n_rode,
        out_shape=[out_shape] + [jax.ShapeDtypeStruct((rows, cols), F32)] * n_rode,
        scratch_shapes=scratch,
        compiler_params=_params("arbitrary", "arbitrary", "arbitrary", vmem_limit_bytes=VMEM_RIDER_LIMIT_BYTES),
    )(*ins, *rider[0])
    return outs[0], outs[1:]


def _matmul_quarters(h, w_slots, order, first, count, name, prev=None, bm=1024):
    s, k = h.shape
    n = w_slots.shape[2]
    bm, bn = _pick(s, bm), _half_width(n)
    pb = n // bn

    def body(order_ref, h_ref, w_ref, *rest):
        rest[-1][...] = _dot(h_ref[...], w_ref[...])

    ins = (order, h, w_slots) + (() if prev is None else (prev,))
    return _pcall(
        body, name=name, num_scalar_prefetch=1, grid=(count * pb, s // bm),
        in_specs=[pl.BlockSpec((bm, k), lambda j, i, o: (i, 0)),
                  pl.BlockSpec((None, k, bn), lambda j, i, o: (o[first + j // pb], 0, j % pb))]
        + ([] if prev is None else [ANY_SPEC]),
        out_specs=pl.BlockSpec((bm, bn), lambda j, i, o: (i, o[first + j // pb] * pb + j % pb)),
        out_shape=jax.ShapeDtypeStruct((s, N_CHIPS * n), F32),
        input_output_aliases={} if prev is None else {3: 0},
        compiler_params=_params("arbitrary", "arbitrary"),
    )(*ins)


def _place():
    x, y, c = lax.axis_index("x"), lax.axis_index("y"), lax.axis_index("c")
    chips = [(1 - x, y), (x, 1 - y), (1 - x, 1 - y)]
    return x, y, c, chips


def _all_gather8(block, name):
    def body(x_ref, out_ref, send_sems, recv_sems, local_sem):
        x, y, c, chips = _place()
        me, sibling = (x, y, c), (x, y, 1 - c)

        def slot(px, py, pc):
            return out_ref.at[4 * px + 2 * py + pc]

        def copy(k, blk, to, src=None):
            return pltpu.make_async_remote_copy(
                src_ref=slot(*blk) if src is None else src, dst_ref=slot(*blk),
                send_sem=send_sems.at[k], recv_sem=recv_sems.at[k], device_id=to, device_id_type=MESH)

        mine = pltpu.make_async_copy(x_ref, slot(*me), local_sem)
        mine.start()
        first = [copy(0, me, sibling, src=x_ref)]
        first += [copy(1 + j, me, (*chip, c), src=x_ref) for j, chip in enumerate(chips)]
        for cp in first:
            cp.start()
        passed = [copy(4 + j, (*chip, c), sibling) for j, chip in enumerate(chips)]
        for j, chip in enumerate(chips):
            copy(1 + j, (*chip, c), me).wait_recv()
            passed[j].start()
        copy(0, sibling, me).wait_recv()
        for j, chip in enumerate(chips):
            copy(4 + j, (*chip, 1 - c), me).wait_recv()
        for cp in first + passed:
            cp.wait_send()
        mine.wait()

    return _pcall(
        body, name=name, out_shape=jax.ShapeDtypeStruct((N_DEV,) + block.shape, block.dtype),
        in_specs=[pl.BlockSpec(memory_space=pltpu.VMEM)], out_specs=pl.BlockSpec(memory_space=pltpu.VMEM),
        scratch_shapes=[pltpu.SemaphoreType.DMA((7,)), pltpu.SemaphoreType.DMA((7,)), pltpu.SemaphoreType.DMA],
        compiler_params=pltpu.CompilerParams(vmem_limit_bytes=VMEM_LIMIT_BYTES),
    )(block)


HBM_SPEC = pl.BlockSpec(memory_space=pltpu.HBM)
SEM_SPEC = pl.BlockSpec(memory_space=pltpu.SEMAPHORE)
ANY_SPEC = pl.BlockSpec(memory_space=pl.ANY)
EFFECT = pltpu.SideEffectType.DATAFLOW_SIDE_EFFECTING


def _xfer_start(name, bufs, plan, n_copies, after_last=False):
    nb = len(bufs)
    deps = (_Order.last,) if after_last and _Order.last is not None else ()
    nd = len(deps)

    def body(*refs):
        send_sems, recv_sems = refs[nb + nd], refs[nb + nd + 1]
        token = refs[nb + nd + 2 + nb]
        for k, (src, dst, dev) in enumerate(plan(refs[:nb], *_place())):
            pltpu.make_async_remote_copy(src_ref=src, dst_ref=dst, send_sem=send_sems.at[k], recv_sem=recv_sems.at[k],
                                         device_id=dev, device_id_type=MESH).start()
        token[...] = jnp.zeros_like(token)

    outs = pl.pallas_call(
        body, name=name,
        out_shape=(pltpu.SemaphoreType.DMA((n_copies,)), pltpu.SemaphoreType.DMA((n_copies,)),
                   *[pltpu.HBM(b.shape, b.dtype) for b in bufs], jax.ShapeDtypeStruct((8, LANES), F32)),
        in_specs=[HBM_SPEC] * nb + [ANY_SPEC] * nd,
        out_specs=(SEM_SPEC, SEM_SPEC, *[HBM_SPEC] * nb, pl.BlockSpec(memory_space=pltpu.VMEM)),
        input_output_aliases={i: 2 + i for i in range(nb)},
        compiler_params=pltpu.CompilerParams(has_side_effects=EFFECT),
    )(*[pltpu.with_memory_space_constraint(b, pltpu.HBM) for b in bufs], *deps)
    _Order.last = outs[-1]
    return (outs[0], outs[1]), list(outs[2:2 + nb])


def _xfer_wait(name, sems, bufs, plan):
    nb = len(bufs)

    def body(*refs):
        send_sems, recv_sems = refs[nb], refs[nb + 1]
        for k, (src, dst, dev) in enumerate(plan(refs[:nb], *_place())):
            copy = pltpu.make_async_remote_copy(src_ref=src, dst_ref=dst, send_sem=send_sems.at[k],
                                                recv_sem=recv_sems.at[k], device_id=dev, device_id_type=MESH)
            copy.wait_send()
            copy.wait_recv()

    outs = pl.pallas_call(
        body, name=name, out_shape=tuple(pltpu.HBM(b.shape, b.dtype) for b in bufs),
        in_specs=[HBM_SPEC] * nb + [SEM_SPEC, SEM_SPEC, ANY_SPEC], out_specs=tuple([HBM_SPEC] * nb),
        input_output_aliases={i: i for i in range(nb)},
        compiler_params=pltpu.CompilerParams(has_side_effects=EFFECT),
    )(*bufs, *sems, _Order.last)
    _Order.last = outs[0]
    return list(outs)


def _xfer_hand_over(name, sems, bufs, plan, next_plans):
    nb, n_next = len(bufs), len(next_plans)

    def body(*refs):
        send_sems, recv_sems = refs[nb], refs[nb + 1]
        outs = refs[nb + 3:]
        place = _place()
        for k, (src, dst, dev) in enumerate(plan(refs[:nb], *place)):
            copy = pltpu.make_async_remote_copy(src_ref=src, dst_ref=dst, send_sem=send_sems.at[k],
                                                recv_sem=recv_sems.at[k], device_id=dev, device_id_type=MESH)
            copy.wait_send()
            copy.wait_recv()
        for p, (next_plan, _) in enumerate(next_plans):
            for k, (src, dst, dev) in enumerate(next_plan(refs[:nb], *place)):
                pltpu.make_async_remote_copy(src_ref=src, dst_ref=dst, send_sem=outs[2 * p].at[k],
                                             recv_sem=outs[2 * p + 1].at[k], device_id=dev, device_id_type=MESH).start()
        outs[-1][...] = jnp.zeros_like(outs[-1])

    sem_shapes = [pltpu.SemaphoreType.DMA((copies,)) for _, copies in next_plans for _ in range(2)]
    outs = pl.pallas_call(
        body, name=name,
        out_shape=(*sem_shapes, *[pltpu.HBM(b.shape, b.dtype) for b in bufs], jax.ShapeDtypeStruct((8, LANES), F32)),
        in_specs=[HBM_SPEC] * nb + [SEM_SPEC, SEM_SPEC, ANY_SPEC],
        out_specs=(*[SEM_SPEC] * (2 * n_next), *[HBM_SPEC] * nb, pl.BlockSpec(memory_space=pltpu.VMEM)),
        input_output_aliases={i: 2 * n_next + i for i in range(nb)},
        compiler_params=pltpu.CompilerParams(has_side_effects=EFFECT),
    )(*bufs, *sems, _Order.last)
    _Order.last = outs[-1]
    return [(outs[2 * p], outs[2 * p + 1]) for p in range(n_next)], list(outs[2 * n_next:2 * n_next + nb])


def _half(ref, c, axis):
    rows = ref.shape[axis] // 2
    return pl.ds(c * rows, rows)


def _plan_weights_ici(n):
    def plan(refs, x, y, c, chips):
        out = []
        for w in range(n):
            region = refs[w].at[2 * x + y, _half(refs[w], c, 1), :]
            out += [(region, region, (*chip, c)) for chip in chips]
        return out
    return plan


def _plan_weights_ring(n):
    def plan(refs, x, y, c, chips):
        out = []
        for w in range(n):
            region = refs[w].at[2 * x + y, _half(refs[w], c, 1), :]
            out += [(region, region, (*chip, c)) for chip in chips[:2]]
        return out
    return plan


def _plan_weights_relay(n):
    def plan(refs, x, y, c, chips):
        out = []
        for w in range(n):
            quarter_rows = refs[w].shape[1] // 4
            upper = refs[w].at[2 * x + (1 - y), pl.ds(2 * c * quarter_rows, quarter_rows), :]
            lower = refs[w].at[2 * (1 - x) + y, pl.ds((2 * c + 1) * quarter_rows, quarter_rows), :]
            out += [(upper, upper, (1 - x, y, c)), (lower, lower, (x, 1 - y, c))]
        return out
    return plan


def _plan_weights_d2d(n, which=slice(0, 3)):
    def plan(refs, x, y, c, chips):
        out = []
        for w in range(n):
            rows = _half(refs[w], c, 1)
            for chip in chips[which]:
                region = refs[w].at[2 * chip[0] + chip[1], rows, :]
                out.append((region, region, (x, y, 1 - c)))
        return out
    return plan


def _plan_gather_out(refs, x, y, c, chips):
    mine = refs[0].at[4 * x + 2 * y + c]
    return [(mine, mine, (x, y, 1 - c))] + [(mine, mine, (*chip, c)) for chip in chips]


def _plan_gather_on(refs, x, y, c, chips):
    out = []
    for chip in chips:
        block = refs[0].at[4 * chip[0] + 2 * chip[1] + c]
        out.append((block, block, (x, y, 1 - c)))
    return out


def _plan_grads_d2d(n):
    def plan(refs, x, y, c, chips):
        return [(refs[w].at[:, _half(refs[w], 1 - c, 1), :], refs[n + w], (x, y, 1 - c)) for w in range(n)]
    return plan


def _plan_grads_ici(n):
    def plan(refs, x, y, c, chips):
        out = []
        for w in range(n):
            out += [(refs[w].at[2 * chip[0] + chip[1]], refs[n + w].at[2 * x + y], (*chip, c)) for chip in chips]
        return out
    return plan


def _plan_final_d2d(n):
    def plan(refs, x, y, c, chips):
        out = []
        for w in range(n):
            region = refs[w].at[_half(refs[w], c, 0), :]
            out.append((region, region, (x, y, 1 - c)))
        return out
    return plan


def _stream_blocks(hr, cols):
    bc = cols if cols <= 4096 else _pick(cols, 4096)
    return _pick_rows(hr, max(16, (768 * 1024) // bc)), bc


def _pre_reduce(g, landed, place, name):
    _, rows, cols = g.shape
    hr = rows // 2
    rb, bc = _stream_blocks(hr, cols)
    nrb = hr // rb

    def body(place_ref, g_ref, l_ref, o_ref):
        o_ref[...] = (g_ref[...].astype(F32) + l_ref[...].astype(F32)).astype(o_ref.dtype)

    return _pcall(
        body, name=name, num_scalar_prefetch=1, grid=(N_CHIPS, nrb, cols // bc),
        in_specs=[pl.BlockSpec((None, rb, bc), lambda j, i, k, p: (j, p[1] * nrb + i, k)),
                  pl.BlockSpec((None, rb, bc), lambda j, i, k, p: (j, i, k))],
        out_specs=pl.BlockSpec((None, rb, bc), lambda j, i, k, p: (j, i, k)),
        out_shape=jax.ShapeDtypeStruct((N_CHIPS, hr, cols), g.dtype),
        compiler_params=_params("parallel", "parallel", "parallel"),
    )(place, g, landed)


def _sum_slots(mine, landed, place, name):
    _, hr, cols = mine.shape
    rb, bc = _stream_blocks(hr, cols)
    rb = _pick_rows(hr, max(16, rb // 2))
    nrb = hr // rb

    def body(place_ref, m_ref, l_ref, o_ref):
        chip = place_ref[0]
        own = m_ref[...].astype(F32)
        total = jnp.where(chip == 0, own, l_ref[0].astype(F32))
        for j in range(1, N_CHIPS):
            total = total + jnp.where(chip == j, own, l_ref[j].astype(F32))
        o_ref[...] = total

    return _pcall(
        body, name=name, num_scalar_prefetch=1, grid=(nrb, cols // bc),
        in_specs=[pl.BlockSpec((None, rb, bc), lambda i, k, p: (p[0], i, k)),
                  pl.BlockSpec((N_CHIPS, rb, bc), lambda i, k, p: (0, i, k))],
        out_specs=pl.BlockSpec((rb, bc), lambda i, k, p: (p[1] * nrb + i, k)),
        out_shape=jax.ShapeDtypeStruct((2 * hr, cols), F32),
        compiler_params=_params("parallel", "parallel"),
    )(place, mine, landed)


class _WeightGather:
    @staticmethod
    def zones(quarters, chip):
        return [lax.dynamic_update_slice(lax.empty((N_CHIPS,) + q.shape, BF16), q.astype(BF16)[None], (chip, 0, 0))
                for q in quarters]

    def __init__(self, tag, zones, ring=False):
        self.tag, self.n, self.ring = tag, len(zones), ring
        self.plan = _plan_weights_ring(self.n) if ring else _plan_weights_ici(self.n)
        self.sems, self.bufs = _xfer_start("wici_start_" + tag, zones, self.plan, (2 if ring else 3) * self.n,
                                           after_last=True)

    def relay(self):
        plan = _plan_weights_relay(self.n)
        (self.sems,), self.bufs = _xfer_hand_over("wrel_start_" + self.tag, self.sems, self.bufs, self.plan,
                                                   [(plan, 2 * self.n)])
        self.plan = plan

    def pass_on(self):
        plan = _plan_weights_d2d(self.n)
        (self.sems,), self.bufs = _xfer_hand_over("wd2d_start_" + self.tag, self.sems, self.bufs, self.plan,
                                                   [(plan, 3 * self.n)])
        self.plan = plan

    def done(self):
        return _xfer_wait("wd2d_wait_" + self.tag, self.sems, self.bufs, self.plan)

    def relay_and_pass_near(self):
        plan, self.near_plan = _plan_weights_relay(self.n), _plan_weights_d2d(self.n, slice(0, 2))
        (self.sems, self.near_sems), self.bufs = _xfer_hand_over(
            "wrel_start_" + self.tag, self.sems, self.bufs, self.plan, [(plan, 2 * self.n), (self.near_plan, 2 * self.n)])
        self.plan = plan

    def near_done(self):
        self.bufs = _xfer_wait("wnear_wait_" + self.tag, self.near_sems, self.bufs, self.near_plan)
        return self.bufs

    def pass_far(self):
        plan = _plan_weights_d2d(self.n, slice(2, 3))
        (sems,), bufs = _xfer_hand_over("wfar_start_" + self.tag, self.sems, self.bufs, self.plan, [(plan, self.n)])
        return _xfer_wait("wfar_wait_" + self.tag, sems, bufs, plan)


class _GradReduce:
    def __init__(self, tag, grads):
        self.tag, self.n = tag, len(grads)
        zones = [lax.empty((N_CHIPS, g.shape[1] // 2, g.shape[2]), g.dtype) for g in grads]
        self.plan = _plan_grads_d2d(self.n)
        self.sems, self.bufs = _xfer_start("gd2d_start_" + tag, list(grads) + zones, self.plan, self.n)

    def pair(self, place):
        n = self.n
        bufs = _xfer_wait("gd2d_wait_" + self.tag, self.sems, self.bufs, self.plan)
        self.halves = [_pre_reduce(bufs[w], bufs[n + w], place, f"pre_reduce_{self.tag}{w}") for w in range(n)]

    def cross(self, after_last=False):
        zones = [lax.empty(h.shape, h.dtype) for h in self.halves]
        self.plan = _plan_grads_ici(self.n)
        self.sems, self.bufs = _xfer_start("gici_start_" + self.tag, self.halves + zones, self.plan, 3 * self.n,
                                           after_last=after_last)

    def step(self, place):
        self.pair(place)
        self.cross()

    def join(self, place):
        n = self.n
        bufs = _xfer_wait("gici_wait_" + self.tag, self.sems, self.bufs, self.plan)
        sums = [_sum_slots(bufs[w], bufs[n + w], place, f"sum_slots_{self.tag}{w}") for w in range(n)]
        self.plan = _plan_final_d2d(n)
        self.sems, self.bufs = _xfer_start("gfin_start_" + self.tag, sums, self.plan, n)

    def done(self):
        return _xfer_wait("gfin_wait_" + self.tag, self.sems, self.bufs, self.plan)


def _ada_fwd(c_all, w_q, b_q):
    d, n = w_q.shape
    bn = _pick(n, 512)

    def body(c_ref, w_ref, b_ref, o_ref):
        cv = c_ref[...]
        act = cv * _sigmoid(cv)
        o_ref[...] = _dot(act, w_ref[...], NN, lax.Precision.HIGHEST) + b_ref[...]

    return _pcall(
        body, name="ada_fwd", grid=(n // bn,),
        in_specs=[pl.BlockSpec((N_DEV, d), lambda j: (0, 0)), pl.BlockSpec((d, bn), lambda j: (0, j)),
                  pl.BlockSpec((1, bn), lambda j: (0, j))],
        out_specs=pl.BlockSpec((N_DEV, bn), lambda j: (0, j)),
        out_shape=jax.ShapeDtypeStruct((N_DEV, n), F32), compiler_params=_params("parallel"),
    )(c_all, w_q, b_q)


def _row_spec(rb, width, col=0):
    return pl.BlockSpec((rb, width), lambda i, col=col: (i, col))


def _vec_spec(width, col=0):
    return pl.BlockSpec((1, width), lambda i, col=col: (0, col))


def _norm_mod_fwd(x, g, sc, sh, name, res=None, gt=None):
    s, d = x.shape
    rb = _pick(s, 256)
    has_res = res is not None

    def body(*refs):
        if has_res:
            x_ref, res_ref, gt_ref, g_ref, sc_ref, sh_ref, x1_ref, h_ref = refs
            xv = x_ref[...] + gt_ref[...] * res_ref[...]
            x1_ref[...] = xv
        else:
            x_ref, g_ref, sc_ref, sh_ref, h_ref = refs
            xv = x_ref[...]
        xh, _ = _rms(xv)
        h_ref[...] = (xh * g_ref[...] * (1.0 + sc_ref[...]) + sh_ref[...]).astype(BF16)

    row, vec = _row_spec(rb, d), _vec_spec(d)
    if has_res:
        ins, in_specs = (x, res, gt, g, sc, sh), [row, row, vec, vec, vec, vec]
        out_shape = [jax.ShapeDtypeStruct((s, d), F32), jax.ShapeDtypeStruct((s, d), BF16)]
        out_specs = [row, row]
    else:
        ins, in_specs = (x, g, sc, sh), [row, vec, vec, vec]
        out_shape, out_specs = jax.ShapeDtypeStruct((s, d), BF16), row
    return _pcall(body, name=name, grid=(s // rb,), in_specs=in_specs, out_specs=out_specs,
                          out_shape=out_shape, compiler_params=_params("parallel"))(*ins)


def _final_loss(x1, f, gt2, g_final, target):
    s, d = x1.shape
    rb = _pick(s, 256)

    def body(x1_ref, f_ref, gt_ref, g_ref, t_ref, dx_ref, df_ref, loss_ref, dg_ref, dgt_ref):
        first = pl.program_id(0) == 0
        fv, gt, gv = f_ref[...], gt_ref[...], g_ref[...]
        x2 = x1_ref[...] + gt * fv
        xh, r = _rms(x2)
        err = xh * gv - t_ref[...]
        blk = 0.5 * jnp.sum(jnp.sum(err * err, axis=1, keepdims=True), axis=0, keepdims=True) / d
        dy = err / d
        dxh = dy * gv
        dx = r * (dxh - xh * jnp.mean(dxh * xh, axis=-1, keepdims=True))
        dx_ref[...] = dx
        df_ref[...] = (dx * gt).astype(BF16)
        _accumulate(first, loss_ref, jnp.broadcast_to(blk, (1, LANES)))
        _accumulate(first, dg_ref, _colsum(dy * xh))
        _accumulate(first, dgt_ref, _colsum(dx * fv))

    row, vec = _row_spec(rb, d), _vec_spec(d)
    return _pcall(
        body, name="final_loss", grid=(s // rb,), in_specs=[row, row, vec, vec, row],
        out_specs=[row, row, _vec_spec(LANES), vec, vec],
        out_shape=[jax.ShapeDtypeStruct((s, d), F32), jax.ShapeDtypeStruct((s, d), BF16),
                   jax.ShapeDtypeStruct((1, LANES), F32), jax.ShapeDtypeStruct((1, d), F32),
                   jax.ShapeDtypeStruct((1, d), F32)],
        compiler_params=_params("arbitrary"),
    )(x1, f, gt2, g_final, target)


def _norm_mod_bwd(dh, xin, dres, g, sc, name, branch=None, gt=None):
    s, d = xin.shape
    rb = _pick(s, 256)
    has_branch = branch is not None

    def body(*refs):
        if has_branch:
            dh_ref, x_ref, dres_ref, g_ref, sc_ref, br_ref, gt_ref, dx_ref, dbr_ref, dsh_ref, dsc_ref, dg_ref, dgt_ref = refs
        else:
            dh_ref, x_ref, dres_ref, g_ref, sc_ref, dx_ref, dsh_ref, dsc_ref, dg_ref = refs
        first = pl.program_id(0) == 0
        gv = g_ref[...]
        xh, r = _rms(x_ref[...])
        dhv = dh_ref[...]
        dn = dhv * (1.0 + sc_ref[...])
        dxh = dn * gv
        dx = dres_ref[...] + r * (dxh - xh * jnp.mean(dxh * xh, axis=-1, keepdims=True))
        dx_ref[...] = dx
        _accumulate(first, dsh_ref, _colsum(dhv))
        _accumulate(first, dsc_ref, _colsum(dhv * xh * gv))
        _accumulate(first, dg_ref, _colsum(dn * xh))
        if has_branch:
            dbr_ref[...] = (dx * gt_ref[...]).astype(BF16)
            _accumulate(first, dgt_ref, _colsum(dx * br_ref[...]))

    row, vec = _row_spec(rb, d), _vec_spec(d)
    vec_shape = jax.ShapeDtypeStruct((1, d), F32)
    if has_branch:
        ins, in_specs = (dh, xin, dres, g, sc, branch, gt), [row, row, row, vec, vec, row, vec]
        out_specs = [row, row, vec, vec, vec, vec]
        out_shape = [jax.ShapeDtypeStruct((s, d), F32), jax.ShapeDtypeStruct((s, d), BF16)] + [vec_shape] * 4
    else:
        ins, in_specs = (dh, xin, dres, g, sc), [row, row, row, vec, vec]
        out_specs = [row, vec, vec, vec]
        out_shape = [jax.ShapeDtypeStruct((s, d), F32)] + [vec_shape] * 3
    return _pcall(body, name=name, grid=(s // rb,), in_specs=in_specs, out_specs=out_specs,
                          out_shape=out_shape, compiler_params=_params("arbitrary"))(*ins)


def _branch_merge(ya, yb, w_bg, w_bh, z, b_gate, gate_col, bm=512, bn=1024):
    s, gw = ya.shape
    d = w_bg.shape[1]
    bm, bn = _pick(s, bm), _pick(d, bn)
    nj = d // bn

    def body(ya_ref, yb_ref, wa_ref, wb_ref, za_ref, zb_ref, ba_ref, bb_ref, pa_ref, pb_ref, y_ref):
        pa = _dot(ya_ref[...], wa_ref[...])
        pb = _dot(yb_ref[...], wb_ref[...])
        pa_ref[...] = pa.astype(BF16)
        pb_ref[...] = pb.astype(BF16)
        ga = _sigmoid(za_ref[...] + ba_ref[...])
        gb = _sigmoid(zb_ref[...] + bb_ref[...])
        y_ref[...] = (ga * pa + gb * pb).astype(BF16)

    act = pl.BlockSpec((bm, gw), lambda j, i: (i, 0))
    wgt = pl.BlockSpec((gw, bn), lambda j, i: (0, j))
    tile = pl.BlockSpec((bm, bn), lambda j, i: (i, j))
    shape = jax.ShapeDtypeStruct((s, d), BF16)
    return _pcall(
        body, name="branch_merge", grid=(nj, s // bm),
        in_specs=[act, act, wgt, wgt,
                  pl.BlockSpec((bm, bn), lambda j, i: (i, gate_col * nj + j)),
                  pl.BlockSpec((bm, bn), lambda j, i: (i, (gate_col + 1) * nj + j)),
                  pl.BlockSpec((1, bn), lambda j, i: (0, j)), pl.BlockSpec((1, bn), lambda j, i: (0, nj + j))],
        out_specs=[tile, tile, tile], out_shape=[shape, shape, shape], compiler_params=_params("parallel", "parallel"),
    )(ya, yb, w_bg, w_bh, z, z, b_gate, b_gate)


def _branch_wgrad(ya, yb, dpa, dpb):
    s, gw = ya.shape
    per = dpa.shape[1] // N_CHIPS

    def body(ya_ref, yb_ref, da_ref, db_ref, ga_ref, gb_ref):
        ga_ref[...] = _dot(ya_ref[...], da_ref[...], TN).astype(BF16)
        gb_ref[...] = _dot(yb_ref[...], db_ref[...], TN).astype(BF16)

    act = pl.BlockSpec((s, gw), lambda j: (0, 0))
    grd = pl.BlockSpec((s, per), lambda j: (0, j))
    out = pl.BlockSpec((None, gw, per), lambda j: (j, 0, 0))
    shape = jax.ShapeDtypeStruct((N_CHIPS, gw, per), BF16)
    return _pcall(body, name="branch_wgrad", grid=(N_CHIPS,), in_specs=[act, act, grd, grd], out_specs=[out, out],
                  out_shape=[shape, shape], compiler_params=_params("parallel"))(ya, yb, dpa, dpb)


def _branch_bwd(dpa, dpb, w_bg, w_bh, bm=512):
    s, d = dpa.shape
    gw = w_bg.shape[0]
    bm = _pick(s, bm)

    def body(da_ref, db_ref, wa_ref, wb_ref, oa_ref, ob_ref):
        oa_ref[...] = _dot(da_ref[...], wa_ref[...], NT)
        ob_ref[...] = _dot(db_ref[...], wb_ref[...], NT)

    act = pl.BlockSpec((bm, d), lambda i: (i, 0))
    wgt = pl.BlockSpec((gw, d), lambda i: (0, 0))
    out = pl.BlockSpec((bm, gw), lambda i: (i, 0))
    shape = jax.ShapeDtypeStruct((s, gw), F32)
    return _pcall(body, name="branch_bwd", grid=(s // bm,), in_specs=[act, act, wgt, wgt], out_specs=[out, out],
                  out_shape=[shape, shape], compiler_params=_params("parallel"))(dpa, dpb, w_bg, w_bh)


def _out_bwd(dyo, w_out, pa, pb, z, b_gate, gate_col, bm=512, bn=1024):
    s, d = dyo.shape
    bm, bn = _pick(s, bm), _pick(d, bn)
    nj = d // bn

    def body(dyo_ref, w_ref, pa_ref, pb_ref, za_ref, zb_ref, ba_ref, bb_ref,
             dpa_ref, dpb_ref, dza_ref, dzb_ref, dba_ref, dbb_ref):
        first = pl.program_id(1) == 0
        dyv = _dot(dyo_ref[...], w_ref[...], NT)
        ga = _sigmoid(za_ref[...] + ba_ref[...])
        gb = _sigmoid(zb_ref[...] + bb_ref[...])
        dpa_ref[...] = (dyv * ga).astype(BF16)
        dpb_ref[...] = (dyv * gb).astype(BF16)
        dga = dyv * pa_ref[...].astype(F32) * ga * (1.0 - ga)
        dgb = dyv * pb_ref[...].astype(F32) * gb * (1.0 - gb)
        dza_ref[...] = dga.astype(BF16)
        dzb_ref[...] = dgb.astype(BF16)
        _accumulate(first, dba_ref, _colsum(dga))
        _accumulate(first, dbb_ref, _colsum(dgb))

    tile = pl.BlockSpec((bm, bn), lambda j, i: (i, j))
    vec = pl.BlockSpec((1, bn), lambda j, i: (0, j))
    act, vec_shape = jax.ShapeDtypeStruct((s, d), BF16), jax.ShapeDtypeStruct((1, d), F32)
    return _pcall(
        body, name="out_bwd", grid=(nj, s // bm),
        in_specs=[pl.BlockSpec((bm, d), lambda j, i: (i, 0)), pl.BlockSpec((bn, d), lambda j, i: (j, 0)), tile, tile,
                  pl.BlockSpec((bm, bn), lambda j, i: (i, gate_col * nj + j)),
                  pl.BlockSpec((bm, bn), lambda j, i: (i, (gate_col + 1) * nj + j)),
                  vec, pl.BlockSpec((1, bn), lambda j, i: (0, nj + j))],
        out_specs=[tile, tile, tile, tile, vec, vec], out_shape=[act, act, act, act, vec_shape, vec_shape],
        compiler_params=_params("parallel", "arbitrary"),
    )(dyo, w_out, pa, pb, z, z, b_gate, b_gate)


def _ffn_in(h, w_fi, bm=512):
    s, d = h.shape
    per = w_fi.shape[2]
    ff = 2 * per
    bm, bn = _pick(s, bm), _half_width(per)
    per_blocks = per // bn

    def body(h_ref, wa_ref, wu_ref, a_ref, u_ref, hf_ref):
        hv = h_ref[...]
        a = _dot(hv, wa_ref[...])
        up = _dot(hv, wu_ref[...])
        a_ref[...] = a.astype(BF16)
        u_ref[...] = up.astype(BF16)
        hf_ref[...] = (a * _sigmoid(a) * up).astype(BF16)

    out = pl.BlockSpec((bm, bn), lambda j, i: (i, j))
    shape = jax.ShapeDtypeStruct((s, ff), BF16)
    return _pcall(
        body, name="ffn_in", grid=(ff // bn, s // bm),
        in_specs=[pl.BlockSpec((bm, d), lambda j, i: (i, 0)),
                  pl.BlockSpec((None, d, bn), lambda j, i: (j // per_blocks, 0, j % per_blocks)),
                  pl.BlockSpec((None, d, bn), lambda j, i: (2 + j // per_blocks, 0, j % per_blocks))],
        out_specs=[out, out, out], out_shape=[shape, shape, shape],
        compiler_params=_params("parallel", "parallel"),
    )(h, w_fi, w_fi)


def _ffn_out_bwd(dffn, w_fo, a_act, up_act, bm=512):
    s, d = dffn.shape
    ff = w_fo.shape[0]
    bm, bn = _pick(s, bm), _half_width(ff // 2)

    def body(d_ref, w_ref, a_ref, u_ref, da_ref, du_ref):
        dhf = _dot(d_ref[...], w_ref[...], NT)
        a = a_ref[...].astype(F32)
        sa = _sigmoid(a)
        da_ref[...] = (dhf * u_ref[...].astype(F32) * sa * (1.0 + a * (1.0 - sa))).astype(BF16)
        du_ref[...] = (dhf * a * sa).astype(BF16)

    tile = pl.BlockSpec((bm, bn), lambda j, i: (i, j))
    shape = jax.ShapeDtypeStruct((s, ff), BF16)
    return _pcall(
        body, name="ffn_out_bwd", grid=(ff // bn, s // bm),
        in_specs=[pl.BlockSpec((bm, d), lambda j, i: (i, 0)), pl.BlockSpec((bn, d), lambda j, i: (j, 0)), tile, tile],
        out_specs=[tile, tile], out_shape=[shape, shape], compiler_params=_params("parallel", "parallel"),
    )(dffn, w_fo, a_act, up_act)


def _tril(n):
    return lax.broadcasted_iota(jnp.int32, (n, n), 0) >= lax.broadcasted_iota(jnp.int32, (n, n), 1)


def _gmlp_norm(v, ln_g, ln_b):
    gv = _gelu(v)
    mu = jnp.mean(gv, axis=-1, keepdims=True)
    cen = gv - mu
    rs = lax.rsqrt(jnp.mean(cen * cen, axis=-1, keepdims=True) + EPS)
    xh = cen * rs
    return xh, rs, xh * ln_g + ln_b


def _gmlp_fwd(z, ln_g, ln_b, ws, bs_t):
    s = z.shape[0]
    gw = ln_g.shape[1]
    groups, chunk, _ = ws.shape

    def body(u_ref, v_ref, lg_ref, lb_ref, ws_ref, bs_ref, ya_ref):
        gu = _gelu(u_ref[...])
        _, _, vn = _gmlp_norm(v_ref[...], lg_ref[...], lb_ref[...])
        mask = _tril(chunk)
        for g in range(groups):
            cols = slice(g * LANES, (g + 1) * LANES)
            wm = jnp.where(mask, ws_ref[g], 0.0).astype(BF16)
            sg = _dot(wm, vn[:, cols].astype(BF16)) + bs_ref[:, g:g + 1]
            ya_ref[:, cols] = (gu[:, cols] * sg).astype(BF16)

    return _pcall(
        body, name="gmlp_fwd", grid=(s // chunk,),
        in_specs=[_row_spec(chunk, gw, 0), _row_spec(chunk, gw, 1), _vec_spec(gw), _vec_spec(gw),
                  pl.BlockSpec((groups, chunk, chunk), lambda i: (0, 0, 0)), pl.BlockSpec((chunk, LANES), lambda i: (0, 0))],
        out_specs=_row_spec(chunk, gw), out_shape=jax.ShapeDtypeStruct((s, gw), BF16),
        compiler_params=_params("parallel"),
    )(z, z, ln_g, ln_b, ws, bs_t)


def _gmlp_bwd(dya, z, ln_g, ln_b, ws, bs_t):
    s = z.shape[0]
    gw = ln_g.shape[1]
    groups, chunk, _ = ws.shape

    def body(dya_ref, u_ref, v_ref, lg_ref, lb_ref, ws_ref, bs_ref, duv_ref, dws_ref, dbs_ref, dlg_ref, dlb_ref, dvn_ref):
        first = pl.program_id(0) == 0
        u, v, lg = u_ref[...], v_ref[...], lg_ref[...]
        gu = _gelu(u)
        xh, rs, vn = _gmlp_norm(v, lg, lb_ref[...])
        dyav = dya_ref[...]
        mask = _tril(chunk)
        lane = lax.broadcasted_iota(jnp.int32, (chunk, LANES), 1)
        dbs = jnp.zeros((chunk, LANES), F32)
        for g in range(groups):
            cols = slice(g * LANES, (g + 1) * LANES)
            wm = jnp.where(mask, ws_ref[g], 0.0).astype(BF16)
            vg = vn[:, cols].astype(BF16)
            sg = _dot(wm, vg) + bs_ref[:, g:g + 1]
            ds = dyav[:, cols] * gu[:, cols]
            duv_ref[:, cols] = (dyav[:, cols] * sg * _gelu_grad(u[:, cols])).astype(BF16)
            dsb = ds.astype(BF16)
            _accumulate(first, dws_ref.at[g], jnp.where(mask, _dot(dsb, vg, NT), 0.0))
            dbs = dbs + jnp.where(lane == g, jnp.sum(ds, axis=-1, keepdims=True), 0.0)
            dvn_ref[:, cols] = _dot(wm, dsb, TN)
        dvn = dvn_ref[...]
        _accumulate(first, dbs_ref, dbs)
        _accumulate(first, dlb_ref, _colsum(dvn))
        _accumulate(first, dlg_ref, _colsum(dvn * xh))
        dxh = dvn * lg
        dgv = rs * (dxh - jnp.mean(dxh, axis=-1, keepdims=True) - xh * jnp.mean(dxh * xh, axis=-1, keepdims=True))
        duv_ref[:, gw:] = (dgv * _gelu_grad(v)).astype(BF16)

    return _pcall(
        body, name="gmlp_bwd", grid=(s // chunk,),
        in_specs=[_row_spec(chunk, gw), _row_spec(chunk, gw, 0), _row_spec(chunk, gw, 1), _vec_spec(gw), _vec_spec(gw),
                  pl.BlockSpec((groups, chunk, chunk), lambda i: (0, 0, 0)), pl.BlockSpec((chunk, LANES), lambda i: (0, 0))],
        out_specs=[_row_spec(chunk, 2 * gw), pl.BlockSpec((groups, chunk, chunk), lambda i: (0, 0, 0)),
                   pl.BlockSpec((chunk, LANES), lambda i: (0, 0)), _vec_spec(gw), _vec_spec(gw)],
        out_shape=[jax.ShapeDtypeStruct((s, 2 * gw), BF16), jax.ShapeDtypeStruct((groups, chunk, chunk), F32),
                   jax.ShapeDtypeStruct((chunk, LANES), F32), jax.ShapeDtypeStruct((1, gw), F32),
                   jax.ShapeDtypeStruct((1, gw), F32)],
        scratch_shapes=[pltpu.VMEM((chunk, gw), F32)],
        compiler_params=_params("arbitrary"),
    )(dya, z, z, ln_g, ln_b, ws, bs_t)


def _lower_bound(lb_ref):
    a0, a1 = lb_ref[0:1, :], lb_ref[1:2, :]
    mx = jnp.maximum(a0, a1)
    e0, e1 = jnp.exp(a0 - mx), jnp.exp(a1 - mx)
    return e0 / (e0 + e1)


def _sum_dot(mask, x):
    hi = x.astype(BF16)
    rest = x - hi.astype(F32)
    mid = rest.astype(BF16)
    low = (rest - mid.astype(F32)).astype(BF16)
    return _dot(mask, hi) + _dot(mask, mid) + _dot(mask, low)


def _ones_where(mask):
    return jnp.where(mask, 1.0, 0.0).astype(BF16)


def _hg_masks(rows, t):
    r = lax.broadcasted_iota(jnp.int32, (rows, rows), 0)
    c = lax.broadcasted_iota(jnp.int32, (rows, rows), 1)
    same = (r // t) == (c // t)
    incl = jnp.logical_and(same, c <= r)
    upto_mid = jnp.logical_and(same, (c % t) <= t // 2)
    rev = jnp.logical_and(same, c >= r)
    return same, incl, upto_mid, rev


def _hg_block(q, fp, lb, masks):
    rows = q.shape[0]
    same, incl, upto_mid, _ = masks
    sig = _sigmoid(fp)
    f = lb + (1.0 - lb) * sig
    k = 1.0 - f
    sq = _sigmoid(q)
    qa = q * sq
    stacked = jnp.concatenate([_ones_where(m) for m in (incl, same, upto_mid)], axis=0)
    sums = _sum_dot(stacked, jnp.log(f))
    b, b_last, b_mid = sums[:rows], sums[rows:2 * rows], sums[2 * rows:]
    e_q = jnp.exp(jnp.minimum(b - b_mid, EXP_CLAMP))
    e_k = jnp.exp(jnp.minimum(b_mid - b, EXP_CLAMP))
    e_in = jnp.exp(b)
    e_out = jnp.exp(b_last - b)
    return dict(sig=sig, f=f, k=k, sq=sq, qa=qa, e_last=jnp.exp(b_last), e_q=e_q, e_k=e_k, e_in=e_in, e_out=e_out,
                q_hat=(qa * e_q).astype(BF16), k_hat=(k * e_k).astype(BF16),
                q_in=(qa * e_in).astype(BF16), k_out=k * e_out)


def _hgrn_fwd(z, hg_lb, norm_g, q_col):
    s = z.shape[0]
    hw = norm_g.shape[1]
    heads = hw // LANES
    t = HG_CHUNK
    rows = min(HG_ROWS, s)
    per_step = rows // t
    hp = min(HG_HEADS_PER_STEP, heads)
    assert heads % hp == 0 and q_col % hp == 0, (heads, q_col)
    wide = hp * LANES

    def zspec(which):
        return pl.BlockSpec((rows, wide), lambda h, r, which=which: (r, (q_col + which * heads) // hp + h))

    def body(q_ref, f_ref, i_ref, g_ref, lb_ref, ng_ref, yb_ref, o_ref, st_out_ref, st_ref, e_last_ref, inter_ref):
        @pl.when(pl.program_id(1) == 0)
        def _():
            st_ref[...] = jnp.zeros_like(st_ref)

        masks = _hg_masks(rows, t)
        for hh in range(hp):
            cols = slice(hh * LANES, (hh + 1) * LANES)
            blk = _hg_block(q_ref[:, cols], f_ref[:, cols], _lower_bound(lb_ref.at[:, cols]), masks)
            iv = i_ref[:, cols].astype(BF16)
            q_in, k_out = blk["q_in"], blk["k_out"].astype(BF16)
            e_last_ref[hh] = blk["e_last"]
            attn = jnp.where(masks[1], _dot(blk["q_hat"], blk["k_hat"], NT), 0.0).astype(BF16)
            o = _dot(attn, iv)
            grown = [_dot(iv[j * t:(j + 1) * t], k_out[j * t:(j + 1) * t], TN) for j in range(per_step)]
            st = st_ref[hh]
            for j in range(per_step):
                st_out_ref[hh, j] = st
                inter_ref[hh, j * t:(j + 1) * t, :] = _dot(q_in[j * t:(j + 1) * t], st.astype(BF16), NT)
                st = st * e_last_ref[hh, j * t:j * t + 1, :] + grown[j]
            st_ref[hh] = st
            o = o + inter_ref[hh]
            o_ref[:, cols] = o
            og = g_ref[:, cols]
            on, _ = _rms(o)
            yb_ref[:, cols] = (on * ng_ref[:, cols] * (og * _sigmoid(og))).astype(BF16)

    out_row = pl.BlockSpec((rows, wide), lambda h, r: (r, h))
    return _pcall(
        body, name="hgrn_fwd", grid=(heads // hp, s // rows),
        in_specs=[zspec(0), zspec(1), zspec(2), zspec(3),
                  pl.BlockSpec((2, wide), lambda h, r: (0, h)), pl.BlockSpec((1, wide), lambda h, r: (0, h))],
        out_specs=[out_row, out_row, pl.BlockSpec((hp, per_step, LANES, LANES), lambda h, r: (h, r, 0, 0))],
        out_shape=[jax.ShapeDtypeStruct((s, hw), BF16), jax.ShapeDtypeStruct((s, hw), F32),
                   jax.ShapeDtypeStruct((heads, s // t, LANES, LANES), F32)],
        scratch_shapes=[pltpu.VMEM((hp, LANES, LANES), F32), pltpu.VMEM((hp, rows, LANES), F32),
                        pltpu.VMEM((hp, rows, LANES), F32)],
        compiler_params=_params("parallel", "arbitrary"),
    )(z, z, z, z, hg_lb, norm_g)


def _hgrn_bwd(dyb, z, o_raw, states, hg_lb, norm_g, q_col):
    s = z.shape[0]
    hw = norm_g.shape[1]
    heads = hw // LANES
    t = HG_CHUNK
    rows = min(HG_ROWS, s)
    per_step = rows // t
    n_steps = s // rows
    hp = min(HG_HEADS_PER_STEP, heads)
    assert heads % hp == 0 and q_col % hp == 0, (heads, q_col)
    wide = hp * LANES

    def zspec(which):
        return pl.BlockSpec((rows, wide), lambda h, r, which=which: (n_steps - 1 - r, (q_col + which * heads) // hp + h))

    def body(dyb_ref, q_ref, f_ref, i_ref, g_ref, o_ref, st_in_ref, lb_ref, ng_ref,
             dq_ref, df_ref, di_ref, dg_ref, dlb_ref, dng_ref, dst_ref, acc_lb_ref, acc_ng_ref,
             e_last_ref, dq_in_ref, dk_out_ref, di_inter_ref, carry_ref):
        step = pl.program_id(1)

        @pl.when(step == 0)
        def _():
            dst_ref[...] = jnp.zeros_like(dst_ref)
            acc_lb_ref[...] = jnp.zeros_like(acc_lb_ref)
            acc_ng_ref[...] = jnp.zeros_like(acc_ng_ref)

        masks = _hg_masks(rows, t)
        same, incl, _, rev = masks
        sum_mask = jnp.concatenate([_ones_where(rev), _ones_where(same)], axis=1)
        for hh in range(hp):
            cols = slice(hh * LANES, (hh + 1) * LANES)
            lb = _lower_bound(lb_ref.at[:, cols])
            ng = ng_ref[:, cols]
            q = q_ref[:, cols]
            blk = _hg_block(q, f_ref[:, cols], lb, masks)
            iv = i_ref[:, cols].astype(BF16)
            o, og, dy = o_ref[:, cols], g_ref[:, cols], dyb_ref[:, cols]
            so = _sigmoid(og)
            on, r = _rms(o)
            acc_ng_ref[:, cols] += _colsum(dy * on * (og * so))
            dg_ref[:, cols] = (dy * on * ng * so * (1.0 + og * (1.0 - so))).astype(BF16)
            don = dy * ng * (og * so)
            do = (r * (don - on * jnp.mean(don * on, axis=-1, keepdims=True))).astype(BF16)
            q_hat, k_hat, q_in, k_out = blk["q_hat"], blk["k_hat"], blk["q_in"], blk["k_out"]
            k_out_b = k_out.astype(BF16)
            attn = jnp.where(incl, _dot(q_hat, k_hat, NT), 0.0).astype(BF16)
            d_attn = jnp.where(incl, _dot(do, iv, NT), 0.0).astype(BF16)
            di_intra = _dot(attn, do, TN)
            dq_hat = _dot(d_attn, k_hat)
            dk_hat = _dot(d_attn, q_hat, TN)
            e_last_ref[hh] = blk["e_last"]
            grown = [_dot(do[j * t:(j + 1) * t], q_in[j * t:(j + 1) * t], TN) for j in range(per_step)]
            dst = dst_ref[hh]
            for j in reversed(range(per_step)):
                rs_ = slice(j * t, (j + 1) * t)
                e_last = e_last_ref[hh, j * t:j * t + 1, :]
                st_prev, dst_b = st_in_ref[hh, j], dst.astype(BF16)
                dq_in_ref[hh, rs_, :] = _dot(do[rs_], st_prev.astype(BF16))
                dk_out_ref[hh, rs_, :] = _dot(iv[rs_], dst_b)
                di_inter_ref[hh, rs_, :] = _dot(k_out_b[rs_], dst_b, NT)
                carry_ref[hh, rs_, :] = jnp.broadcast_to(e_last * _colsum(st_prev * dst), (t, LANES))
                dst = dst * e_last + grown[j]
            dst_ref[hh] = dst
            di_ref[:, cols] = (di_intra + di_inter_ref[hh]).astype(BF16)
            dk_out = dk_out_ref[hh]
            dqa = dq_in_ref[hh] * blk["e_in"] + dq_hat * blk["e_q"]
            dk = dk_out * blk["e_out"] + dk_hat * blk["e_k"]
            db = blk["qa"] * dqa - blk["k"] * dk
            dlf = _sum_dot(sum_mask, jnp.concatenate([db, dk_out * k_out], axis=0)) + carry_ref[hh]
            dfv = dlf / blk["f"] - dk
            sig, sq = blk["sig"], blk["sq"]
            df_ref[:, cols] = (dfv * (1.0 - lb) * sig * (1.0 - sig)).astype(BF16)
            acc_lb_ref[:, cols] += _colsum(dfv * (1.0 - sig))
            dq_ref[:, cols] = (dqa * sq * (1.0 + q * (1.0 - sq))).astype(BF16)

        @pl.when(step == n_steps - 1)
        def _():
            lb = _lower_bound(lb_ref)
            d0 = acc_lb_ref[...] * lb * (1.0 - lb)
            dlb_ref[0:1, :] = d0
            dlb_ref[1:2, :] = -d0
            dng_ref[...] = acc_ng_ref[...]

    rev_row = pl.BlockSpec((rows, wide), lambda h, r: (n_steps - 1 - r, h))
    piece = jax.ShapeDtypeStruct((s, hw), BF16)
    return _pcall(
        body, name="hgrn_bwd", grid=(heads // hp, n_steps),
        in_specs=[rev_row, zspec(0), zspec(1), zspec(2), zspec(3), rev_row,
                  pl.BlockSpec((hp, per_step, LANES, LANES), lambda h, r: (h, n_steps - 1 - r, 0, 0)),
                  pl.BlockSpec((2, wide), lambda h, r: (0, h)), pl.BlockSpec((1, wide), lambda h, r: (0, h))],
        out_specs=[rev_row, rev_row, rev_row, rev_row,
                   pl.BlockSpec((2, wide), lambda h, r: (0, h)), pl.BlockSpec((1, wide), lambda h, r: (0, h))],
        out_shape=[piece, piece, piece, piece, jax.ShapeDtypeStruct((2, hw), F32), jax.ShapeDtypeStruct((1, hw), F32)],
        scratch_shapes=[pltpu.VMEM((hp, LANES, LANES), F32), pltpu.VMEM((1, wide), F32), pltpu.VMEM((1, wide), F32)]
        + [pltpu.VMEM((hp, rows, LANES), F32)] * 5,
        compiler_params=_params("parallel", "arbitrary"),
    )(dyb, z, z, z, z, o_raw, states, hg_lb, norm_g)


def _adam_update(w, m, v, g):
    m2 = ADAM_B1 * m + (1.0 - ADAM_B1) * g
    v2 = ADAM_B2 * v + (1.0 - ADAM_B2) * (g * g)
    m_hat = m2 * (1.0 / (1.0 - ADAM_B1 ** ADAM_STEP))
    v_hat = v2 * (1.0 / (1.0 - ADAM_B2 ** ADAM_STEP))
    return -ADAM_LR * (m_hat / (jnp.sqrt(v_hat) + ADAM_EPS) + ADAM_WD * w), m2, v2


def _adamw(w, m, v, parts, name, outer=False):
    rows, cols = w.shape
    bc = cols if cols <= 4096 else _pick(cols, 4096)
    rb = _pick_rows(rows, max(8, (384 * 1024) // bc), mult=8)
    if outer and rb % LANES:
        rb = rows

    def body(w_ref, m_ref, v_ref, *refs):
        g_ref, d_ref, mo_ref, vo_ref = refs[-4:]
        if outer:
            cv = refs[0][...]
            g = _dot(cv * _sigmoid(cv), refs[1][...], TN, lax.Precision.HIGHEST)
        else:
            p_ref = refs[0]
            g = p_ref[0].astype(F32)
            for p in range(1, p_ref.shape[0]):
                g = g + p_ref[p].astype(F32)
        g_ref[...] = g
        d_ref[...], mo_ref[...], vo_ref[...] = _adam_update(w_ref[...], m_ref[...], v_ref[...], g)

    blk = pl.BlockSpec((rb, bc), lambda i, j: (i, j))
    out = jax.ShapeDtypeStruct((rows, cols), F32)
    if outer:
        grad_specs = [pl.BlockSpec((N_DEV, rb), lambda i, j: (0, i)), pl.BlockSpec((N_DEV, bc), lambda i, j: (0, j))]
        grad_ins = tuple(parts)
    else:
        grad_specs = [pl.BlockSpec((parts.shape[0], rb, bc), lambda i, j: (0, i, j))]
        grad_ins = (parts,)
    return _pcall(
        body, name=name, grid=(rows // rb, cols // bc), in_specs=[blk, blk, blk] + grad_specs,
        out_specs=[blk] * 4, out_shape=[out] * 4, compiler_params=_params("parallel", "parallel"),
    )(w, m, v, *grad_ins)


SMALL = ("b_ada", "norm1_g", "b_gate", "gmlp_ln_g", "gmlp_ln_b", "gmlp_ws", "gmlp_bs", "hg_lb", "hg_norm_g",
         "norm2_g", "final_norm_g")
BIG = ("w_in", "w_branch_gmlp", "w_branch_hg", "w_out", "w_ffn_in", "w_ffn_out")
WEIGHTS = ("w_ada", "b_ada", "norm1_g", "w_in", "b_gate", "gmlp_ln_g", "gmlp_ln_b", "gmlp_ws", "gmlp_bs", "hg_lb",
           "hg_norm_g", "w_branch_gmlp", "w_branch_hg", "w_out", "norm2_g", "w_ffn_in", "w_ffn_out", "final_norm_g")


def _pack(parts):
    return jnp.concatenate([p.reshape(-1, LANES) for p in parts], axis=0)


def _step(x, c, loss_target, w, m, v):
    s, d = x.shape[1], x.shape[2]
    gw = w["gmlp_ln_g"].shape[-1]
    hw = w["hg_norm_g"].shape[-1]
    x2d, tgt = x[0], loss_target[0]
    mx, my, mc = lax.axis_index("x"), lax.axis_index("y"), lax.axis_index("c")
    chip = 2 * mx + my
    dev = 2 * chip + mc
    q_col = 2 * gw // LANES
    gate_col = (2 * gw + 4 * hw) // d
    place = jnp.stack([chip, mc]).astype(jnp.int32)
    _Order.last = None

    c_all = _all_gather8(c.reshape(-1, LANES), "gather_c").reshape(N_DEV, d)
    gather_in = _WeightGather("in", _WeightGather.zones([w["w_in"][0]], chip), ring=True)
    zones_mix = _WeightGather.zones([w[n][0] for n in ("w_branch_gmlp", "w_branch_hg", "w_out")], chip)
    zones_fi = _WeightGather.zones([w["w_ffn_in"][0]], chip)
    zones_fo = _WeightGather.zones([w["w_ffn_out"][0]], chip)
    _Order.also = (*zones_mix, *zones_fi, *zones_fo)
    n_ada = w["w_ada"].shape[-1]
    b_ada_q = lax.dynamic_slice(w["b_ada"], (0, chip * n_ada), (1, n_ada))
    mod_q = _ada_fwd(c_all, w["w_ada"][0], b_ada_q)
    mod_all = _all_gather8(mod_q, "gather_mod")
    mod = lax.dynamic_index_in_dim(mod_all, dev, axis=1, keepdims=False)[::2].reshape(1, 6 * d)
    sh1, sc1, gt1, sh2, sc2, gt2 = [mod[:, i * d:(i + 1) * d] for i in range(6)]

    gather_in.relay_and_pass_near()
    gather_mix = _WeightGather("mix", zones_mix)
    gather_fi = _WeightGather("fi", zones_fi, ring=True)

    norm1_g, norm2_g, final_g = w["norm1_g"], w["norm2_g"], w["final_norm_g"].reshape(1, d)
    ln_g, ln_b = w["gmlp_ln_g"], w["gmlp_ln_b"]
    ws = w["gmlp_ws"][0]
    groups = ws.shape[0]
    bs_t = jnp.pad(w["gmlp_bs"][0].T, ((0, 0), (0, LANES - groups)))
    hg_lb, hg_ng, b_gate = w["hg_lb"], w["hg_norm_g"], w["b_gate"]

    h1 = _norm_mod_fwd(x2d, norm1_g, sc1, sh1, "norm1_fwd")
    order = jnp.stack([chip, 2 * (1 - mx) + my, 2 * mx + (1 - my), 2 * (1 - mx) + (1 - my)]).astype(jnp.int32)
    w_in, = gather_in.near_done()
    z = _matmul_quarters(h1, w_in, order, 0, 3, "mm_z_near")
    w_in, = gather_in.pass_far()
    z = _matmul_quarters(h1, w_in, order, 3, 1, "mm_z_far", prev=z)
    gather_mix.pass_on()
    ya = _gmlp_fwd(z, ln_g, ln_b, ws, bs_t)
    yb, o_raw, states = _hgrn_fwd(z, hg_lb, hg_ng, q_col)
    gather_fi.relay()
    gather_fo = _WeightGather("fo", zones_fo)
    w_bg, w_bh, w_out = gather_mix.done()
    w_bg, w_bh = [wq.transpose(1, 0, 2).reshape(wq.shape[1], -1) for wq in (w_bg, w_bh)]
    w_out = w_out.reshape(-1, w_out.shape[-1])
    pa, pb, y = _branch_merge(ya, yb, w_bg, w_bh, z, b_gate, gate_col)
    yo = _matmul(y, w_out, mode="nn", name="mm_yo", out_dtype=F32)
    gather_fi.pass_on()
    x1, h2 = _norm_mod_fwd(x2d, norm2_g, sc2, sh2, "norm2_fwd", res=yo, gt=gt1)
    w_fi, = gather_fi.done()
    a_act, up_act, hf = _ffn_in(h2, w_fi)
    gather_fo.pass_on()
    w_fo, = gather_fo.done()
    w_fo = w_fo.reshape(-1, w_fo.shape[-1])
    ffn = _matmul(hf, w_fo, mode="nn", name="mm_ffn", out_dtype=F32)
    dx2, dffn, loss_row, d_final_g, d_gt2 = _final_loss(x1, ffn, gt2, final_g, tgt)

    g_fo = _matmul(hf, dffn, mode="tn", name="mm_g_fo", out_dtype=BF16, bm=_half_width(hf.shape[1] // 2))
    daup = tuple(_ffn_out_bwd(dffn, w_fo, a_act, up_act))
    g_fi = _matmul(h2, daup, mode="tn", name="mm_g_fi", out_dtype=BF16, out_slots=True, bn=_half_width(w_fi.shape[2]))
    red_ffn = _GradReduce("ffn", [g_fo.reshape(N_CHIPS, -1, g_fo.shape[-1]), g_fi])
    dh2 = _matmul(daup, w_fi, mode="nt", name="mm_dh2", out_dtype=F32, b_slots=True)
    red_ffn.step(place)
    dx1, dyo, d_sh2, d_sc2, d_norm2, d_gt1 = _norm_mod_bwd(dh2, x1, dx2, norm2_g, sc2, "norm2_bwd", branch=yo, gt=gt1)
    g_out = _matmul(y, dyo, mode="tn", name="mm_g_out", out_dtype=BF16)
    dpa, dpb, dz_ga, dz_gb, d_b_ga, d_b_gb = _out_bwd(dyo, w_out, pa, pb, z, b_gate, gate_col)
    g_bg, g_bh = _branch_wgrad(ya, yb, dpa, dpb)
    red_mix = _GradReduce("mix", [g_out.reshape(N_CHIPS, -1, g_out.shape[-1]), g_bg, g_bh])
    dya, dyb = _branch_bwd(dpa, dpb, w_bg, w_bh)
    red_mix.step(place)
    dz_uv, d_ws, d_bs_t, d_ln_g, d_ln_b = _gmlp_bwd(dya, z, ln_g, ln_b, ws, bs_t)
    dz_q, dz_f, dz_i, dz_g, d_hg_lb, d_hg_ng = _hgrn_bwd(dyb, z, o_raw, states, hg_lb, hg_ng, q_col)
    dz = jnp.concatenate([dz_uv, dz_q, dz_f, dz_i, dz_g, dz_ga, dz_gb], axis=1)
    grad, delta, new_m, new_v = {}, {}, {}, {}

    def update(n, parts, outer=False):
        outs = _adamw(w[n][0], m[n][0], v[n][0], parts, "adamw_" + n, outer=outer)
        grad[n], delta[n], new_m[n], new_v[n] = [o[None] for o in outs]

    g_in = _matmul(h1, dz, mode="tn", name="mm_g_in", out_dtype=BF16, out_slots=True, bn=_half_width(w_in.shape[2]))
    red_in = _GradReduce("in", [g_in])
    red_ffn.join(place)
    red_mix.join(place)
    red_in.step(place)
    g_fo, g_fi = red_ffn.done()
    n = "w_ffn_in"
    dh1, rode = _matmul(dz, w_in, mode="nt", name="mm_dh1", out_dtype=F32, b_slots=True,
                        rider=([w[n][0], m[n][0], v[n][0], g_fi], lambda *blk: (blk[3], *_adam_update(*blk)), 4))
    grad[n], delta[n], new_m[n], new_v[n] = [o[None] for o in rode]
    grad_x, d_sh1, d_sc1, d_norm1 = _norm_mod_bwd(dh1, x2d, dx1, norm1_g, sc1, "norm1_bwd")

    d_mod = jnp.concatenate([d_sh1, d_sc1, d_gt1, d_sh2, d_sc2, d_gt2], axis=1)
    small_part = {"b_ada": d_mod, "norm1_g": d_norm1, "b_gate": jnp.concatenate([d_b_ga, d_b_gb], axis=1), "gmlp_ln_g": d_ln_g, "gmlp_ln_b": d_ln_b,
                  "gmlp_ws": d_ws, "gmlp_bs": d_bs_t[:, :groups].T, "hg_lb": d_hg_lb, "hg_norm_g": d_hg_ng,
                  "norm2_g": d_norm2, "final_norm_g": d_final_g}
    loss_rows = jnp.broadcast_to(loss_row, (8, LANES))
    packed = _pack([small_part[n] for n in SMALL] + [loss_rows])
    zone = lax.dynamic_update_slice(lax.empty((N_DEV,) + packed.shape, F32), packed[None], (dev, 0, 0))
    small_sems, small_bufs = _xfer_start("gsmall_start", [zone], _plan_gather_out, 4, after_last=True)
    update("w_ffn_out", g_fo[None])
    g_out, g_bg, g_bh = red_mix.done()
    update("w_out", g_out[None])
    update("w_branch_gmlp", g_bg[None])
    update("w_branch_hg", g_bh[None])
    red_in.join(place)
    (small_sems,), small_bufs = _xfer_hand_over("gsmall_pass", small_sems, small_bufs, _plan_gather_out,
                                                [(_plan_gather_on, 3)])
    update("w_in", red_in.done()[0][None])
    small_all, = _xfer_wait("gsmall_wait", small_sems, small_bufs, _plan_gather_on)
    d_mod_all = small_all[:, :6 * d // LANES].reshape(N_DEV, 6 * d)
    d_mod_q = lax.dynamic_slice(d_mod_all, (0, chip * n_ada), (N_DEV, n_ada))
    update("w_ada", (c_all, d_mod_q), outer=True)
    pad = [jnp.zeros_like(loss_rows)]
    outs = _adamw(_pack([w[n] for n in SMALL] + pad), _pack([m[n] for n in SMALL] + pad),
                  _pack([v[n] for n in SMALL] + pad), small_all, "adamw_small")
    row = 0
    for n in SMALL:
        cnt = w[n].size // LANES
        for dst, o in zip((grad, delta, new_m, new_v), outs):
            dst[n] = o[row:row + cnt].reshape(w[n].shape)
        row += cnt

    loss = outs[0][row, 0]
    return (loss, grad_x[None], *[grad[n] for n in WEIGHTS], *[delta[n] for n in WEIGHTS],
            *[new_m[n] for n in WEIGHTS], *[new_v[n] for n in WEIGHTS])


def kernel(x, c, w_ada, b_ada, norm1_g, w_in, b_gate, gmlp_ln_g, gmlp_ln_b, gmlp_ws, gmlp_bs, hg_lb, hg_norm_g, w_branch_gmlp, w_branch_hg, w_out, norm2_g, w_ffn_in, w_ffn_out, final_norm_g, loss_target, m_w_ada, m_b_ada, m_norm1_g, m_w_in, m_b_gate, m_gmlp_ln_g, m_gmlp_ln_b, m_gmlp_ws, m_gmlp_bs, m_hg_lb, m_hg_norm_g, m_w_branch_gmlp, m_w_branch_hg, m_w_out, m_norm2_g, m_w_ffn_in, m_w_ffn_out, m_final_norm_g, v_w_ada, v_b_ada, v_norm1_g, v_w_in, v_b_gate, v_gmlp_ln_g, v_gmlp_ln_b, v_gmlp_ws, v_gmlp_bs, v_hg_lb, v_hg_norm_g, v_w_branch_gmlp, v_w_branch_hg, v_w_out, v_norm2_g, v_w_ffn_in, v_w_ffn_out, v_final_norm_g):
    w = dict(w_ada=w_ada, b_ada=b_ada, norm1_g=norm1_g, w_in=w_in, b_gate=b_gate, gmlp_ln_g=gmlp_ln_g,
             gmlp_ln_b=gmlp_ln_b, gmlp_ws=gmlp_ws, gmlp_bs=gmlp_bs, hg_lb=hg_lb, hg_norm_g=hg_norm_g,
             w_branch_gmlp=w_branch_gmlp, w_branch_hg=w_branch_hg, w_out=w_out, norm2_g=norm2_g,
             w_ffn_in=w_ffn_in, w_ffn_out=w_ffn_out, final_norm_g=final_norm_g)
    m = dict(w_ada=m_w_ada, b_ada=m_b_ada, norm1_g=m_norm1_g, w_in=m_w_in, b_gate=m_b_gate, gmlp_ln_g=m_gmlp_ln_g,
             gmlp_ln_b=m_gmlp_ln_b, gmlp_ws=m_gmlp_ws, gmlp_bs=m_gmlp_bs, hg_lb=m_hg_lb, hg_norm_g=m_hg_norm_g,
             w_branch_gmlp=m_w_branch_gmlp, w_branch_hg=m_w_branch_hg, w_out=m_w_out, norm2_g=m_norm2_g,
             w_ffn_in=m_w_ffn_in, w_ffn_out=m_w_ffn_out, final_norm_g=m_final_norm_g)
    v = dict(w_ada=v_w_ada, b_ada=v_b_ada, norm1_g=v_norm1_g, w_in=v_w_in, b_gate=v_b_gate, gmlp_ln_g=v_gmlp_ln_g,
             gmlp_ln_b=v_gmlp_ln_b, gmlp_ws=v_gmlp_ws, gmlp_bs=v_gmlp_bs, hg_lb=v_hg_lb, hg_norm_g=v_hg_norm_g,
             w_branch_gmlp=v_w_branch_gmlp, w_branch_hg=v_w_branch_hg, w_out=v_w_out, norm2_g=v_norm2_g,
             w_ffn_in=v_w_ffn_in, w_ffn_out=v_w_ffn_out, final_norm_g=v_final_norm_g)
    return _step(x, c, loss_target, w, m, v)
```

```python
import jax
import jax.numpy as jnp
from jax import lax
from jax.experimental import pallas as pl
from jax.experimental.pallas import tpu as pltpu

F32 = jnp.float32
BF16 = jnp.bfloat16
EPS = 1e-6
LANES = 128
N_CHIPS = 4
N_DEV = 8
VMEM_LIMIT_BYTES = 56 * 1024 * 1024
VMEM_RIDER_LIMIT_BYTES = 60 * 1024 * 1024
HG_CHUNK = 32
HG_ROWS = 256
HG_HEADS_PER_STEP = 8
EXP_CLAMP = 80.0
ADAM_LR, ADAM_B1, ADAM_B2, ADAM_EPS, ADAM_WD, ADAM_STEP = 0.001, 0.9, 0.999, 1e-08, 0.01, 10
MESH = pl.DeviceIdType.MESH

NN = (((1,), (0,)), ((), ()))
NT = (((1,), (1,)), ((), ()))
TN = (((0,), (0,)), ((), ()))


def _dot(a, b, dims=NN, precision=None):
    return lax.dot_general(a, b, dims, precision=precision, preferred_element_type=F32)


def _params(*semantics, vmem_limit_bytes=VMEM_LIMIT_BYTES):
    return pltpu.CompilerParams(dimension_semantics=semantics, vmem_limit_bytes=vmem_limit_bytes)


class _Order:
    last = None
    also = ()


def _pcall(body, *, in_specs, out_specs, grid=(), scratch_shapes=(), num_scalar_prefetch=0, **kw):
    def run(*ins):
        deps = (() if _Order.last is None else (_Order.last,)) + tuple(_Order.also)
        _Order.also = ()
        n_in, n_dep = len(ins), len(deps)

        def wrapped(*refs):
            body(*refs[:n_in], *refs[n_in + n_dep:])

        specs = list(in_specs) + [pl.BlockSpec(memory_space=pl.ANY)] * n_dep
        if num_scalar_prefetch:
            grid_spec = pltpu.PrefetchScalarGridSpec(
                num_scalar_prefetch=num_scalar_prefetch, grid=grid, in_specs=specs, out_specs=out_specs,
                scratch_shapes=scratch_shapes)
            outs = pl.pallas_call(wrapped, grid_spec=grid_spec, **kw)(*ins, *deps)
        else:
            outs = pl.pallas_call(wrapped, grid=grid, in_specs=specs, out_specs=out_specs,
                                  scratch_shapes=scratch_shapes, **kw)(*ins, *deps)
        _Order.last = jax.tree.leaves(outs)[0]
        return outs

    return run


def _pick_rows(dim, pref, mult=16):
    best = None
    for cand in range(mult, min(dim, pref) + 1, mult):
        if dim % cand == 0:
            best = cand
    assert best is not None, (dim, pref)
    return best


def _half_width(n):
    return n // 2 if n % (2 * LANES) == 0 else n


def _pick(dim, pref):
    if dim <= pref:
        return dim
    best = None
    for cand in range(LANES, pref + 1, LANES):
        if dim % cand == 0:
            best = cand
    assert best is not None, (dim, pref)
    return best


def _sigmoid(x):
    return 1.0 / (1.0 + jnp.exp(-x))


def _gelu(x):
    c = 0.7978845608028654
    return 0.5 * x * (1.0 + jnp.tanh(c * (x + 0.044715 * x * x * x)))


def _gelu_grad(x):
    c = 0.7978845608028654
    t = jnp.tanh(c * (x + 0.044715 * x * x * x))
    return 0.5 * (1.0 + t) + 0.5 * x * (1.0 - t * t) * c * (1.0 + 3.0 * 0.044715 * x * x)


def _rms(x):
    r = lax.rsqrt(jnp.mean(x * x, axis=-1, keepdims=True) + EPS)
    return x * r, r


def _colsum(x):
    return jnp.sum(x, axis=0, keepdims=True)


def _accumulate(first, ref, val):
    @pl.when(first)
    def _():
        ref[...] = val

    @pl.when(jnp.logical_not(first))
    def _():
        ref[...] += val


def _matmul(a, b, *, mode, name, out_dtype, b_slots=False, out_slots=False, bm=1024, bn=1024, bk=2816, rider=None):
    pair = a if isinstance(a, tuple) else b if isinstance(b, tuple) else None
    if mode == "nn":
        m, k = a.shape
        n = b.shape[2] * N_CHIPS if b_slots else b.shape[1]
        per = b.shape[2] if b_slots else n
    elif mode == "nt":
        m, k = (a[0].shape[0], 2 * a[0].shape[1]) if pair else a.shape
        n = b.shape[1] if b_slots else b.shape[0]
        per = b.shape[2] if b_slots else k
    else:
        k, m = a.shape
        n = 2 * b[0].shape[1] if pair else b.shape[1]
        per = n // N_CHIPS if out_slots else n
    bm = _pick(m, bm)
    if mode == "nt":
        bn, bk = _pick(n, bn), _pick(per, bk)
    else:
        bn, bk = _pick(per, bn), _pick(k, bk)
    nk = k // bk
    per_blocks = per // (bk if mode == "nt" else bn)
    dims = {"nn": NN, "nt": NT, "tn": TN}[mode]
    half = (nk if mode == "nt" else n // bn) // 2

    def product(a_ref, b_ref, o_ref, acc):
        part = _dot(a_ref[...], b_ref[...], dims)
        if nk == 1:
            o_ref[...] = part.astype(o_ref.dtype)
            return
        acc_ref, kk = acc[0], pl.program_id(2)

        @pl.when(kk == 0)
        def _():
            acc_ref[...] = part

        @pl.when(jnp.logical_and(kk > 0, kk < nk - 1))
        def _():
            acc_ref[...] += part

        @pl.when(kk == nk - 1)
        def _():
            o_ref[...] = (acc_ref[...] + part).astype(o_ref.dtype)

    n_ride, n_rode = (len(rider[0]), rider[2]) if rider else (0, 0)

    def body(*refs):
        if rider:
            ride_in, ride_out = refs[2:2 + n_ride], refs[3 + n_ride:3 + n_ride + n_rode]

            def ride(t, carry):
                rows = pl.ds(pl.multiple_of(t * 8, 8), 8)
                for ref, val in zip(ride_out, rider[1](*[r[rows, :] for r in ride_in])):
                    ref[rows, :] = val
                return carry

            lax.fori_loop(0, ride_in[0].shape[0] // 8, ride, 0)
        if not pair:
            return product(refs[0], refs[1], refs[2 + n_ride], refs[3 + n_ride + n_rode:])
        first = pl.program_id(2 if mode == "nt" else 1) < half
        x1, x2, y, o_ref, acc = refs[0], refs[1], refs[2], refs[3], refs[4:]

        @pl.when(first)
        def _():
            product(*((x1, y) if mode == "nt" else (y, x1)), o_ref, acc)

        @pl.when(jnp.logical_not(first))
        def _():
            product(*((x2, y) if mode == "nt" else (y, x2)), o_ref, acc)

    if mode == "nn":
        a_spec = pl.BlockSpec((bm, bk), lambda i, j, kk: (i, kk))
        if b_slots:
            b_spec = pl.BlockSpec((None, bk, bn), lambda i, j, kk: (j // per_blocks, kk, j % per_blocks))
        else:
            b_spec = pl.BlockSpec((bk, bn), lambda i, j, kk: (kk, j))
    elif mode == "nt":
        a_spec = pl.BlockSpec((bm, bk), lambda i, j, kk: (i, kk))
        if b_slots:
            b_spec = pl.BlockSpec((None, bn, bk), lambda i, j, kk: (kk // per_blocks, j, kk % per_blocks))
        else:
            b_spec = pl.BlockSpec((bn, bk), lambda i, j, kk: (j, kk))
    else:
        a_spec = pl.BlockSpec((bk, bm), lambda i, j, kk: (kk, i))
        b_spec = pl.BlockSpec((bk, bn), lambda i, j, kk: (kk, j))
    if out_slots:
        o_spec = pl.BlockSpec((None, bm, bn), lambda i, j, kk: (j // per_blocks, i, j % per_blocks))
        out_shape = jax.ShapeDtypeStruct((N_CHIPS, m, per), out_dtype)
    else:
        o_spec = pl.BlockSpec((bm, bn), lambda i, j, kk: (i, j))
        out_shape = jax.ShapeDtypeStruct((m, n), out_dtype)
    if not pair:
        ins, in_specs = (a, b), [a_spec, b_spec]
    elif mode == "nt":
        ins = (*a, b)
        in_specs = [pl.BlockSpec((bm, bk), lambda i, j, kk: (i, jnp.minimum(kk, half - 1))),
                    pl.BlockSpec((bm, bk), lambda i, j, kk: (i, jnp.maximum(kk - half, 0))), b_spec]
    else:
        ins = (*b, a)
        in_specs = [pl.BlockSpec((bk, bn), lambda i, j, kk: (kk, jnp.minimum(j, half - 1))),
                    pl.BlockSpec((bk, bn), lambda i, j, kk: (kk, jnp.maximum(j - half, 0))), a_spec]
    scratch = [pltpu.VMEM((bm, bn), F32)] if nk > 1 else []
    if not rider:
        return _pcall(
            body, name=name, grid=(m // bm, n // bn, nk), in_specs=in_specs, out_specs=o_spec, out_shape=out_shape,
            scratch_shapes=scratch, compiler_params=_params("parallel", "parallel", "arbitrary"),
        )(*ins)
    assert not pair
    rows, cols = rider[0][0].shape
    nj = n // bn
    rb = rows // ((m // bm) * nj * nk)
    assert rows == rb * (m // bm) * nj * nk and rb % 8 == 0, (rows, rb)
    ride_spec = pl.BlockSpec((rb, cols), lambda i, j, kk: ((i * nj + j) * nk + kk, 0))
    outs = _pcall(
        body, name=name, grid=(m // bm, nj, nk), in_specs=in_specs + [ride_spec] * n_ride,
        out_specs=[o_spec] + [ride_spec] * n_rode,
        out_shape=[out_shape] + [jax.ShapeDtypeStruct((rows, cols), F32)] * n_rode,
        scratch_shapes=scratch,
        compiler_params=_params("arbitrary", "arbitrary", "arbitrary", vmem_limit_bytes=VMEM_RIDER_LIMIT_BYTES),
    )(*ins, *rider[0])
    return outs[0], outs[1:]


def _matmul_quarters(h, w_slots, order, first, count, name, prev=None, bm=1024):
    s, k = h.shape
    n = w_slots.shape[2]
    bm, bn = _pick(s, bm), _half_width(n)
    pb = n // bn

    def body(order_ref, h_ref, w_ref, *rest):
        rest[-1][...] = _dot(h_ref[...], w_ref[...])

    ins = (order, h, w_slots) + (() if prev is None else (prev,))
    return _pcall(
        body, name=name, num_scalar_prefetch=1, grid=(count * pb, s // bm),
        in_specs=[pl.BlockSpec((bm, k), lambda j, i, o: (i, 0)),
                  pl.BlockSpec((None, k, bn), lambda j, i, o: (o[first + j // pb], 0, j % pb))]
        + ([] if prev is None else [ANY_SPEC]),
        out_specs=pl.BlockSpec((bm, bn), lambda j, i, o: (i, o[first + j // pb] * pb + j % pb)),
        out_shape=jax.ShapeDtypeStruct((s, N_CHIPS * n), F32),
        input_output_aliases={} if prev is None else {3: 0},
        compiler_params=_params("arbitrary", "arbitrary"),
    )(*ins)


def _place():
    x, y, c = lax.axis_index("x"), lax.axis_index("y"), lax.axis_index("c")
    chips = [(1 - x, y), (x, 1 - y), (1 - x, 1 - y)]
    return x, y, c, chips


def _all_gather8(block, name):
    def body(x_ref, out_ref, send_sems, recv_sems, local_sem):
        x, y, c, chips = _place()
        me, sibling = (x, y, c), (x, y, 1 - c)

        def slot(px, py, pc):
            return out_ref.at[4 * px + 2 * py + pc]

        def copy(k, blk, to, src=None):
            return pltpu.make_async_remote_copy(
                src_ref=slot(*blk) if src is None else src, dst_ref=slot(*blk),
                send_sem=send_sems.at[k], recv_sem=recv_sems.at[k], device_id=to, device_id_type=MESH)

        mine = pltpu.make_async_copy(x_ref, slot(*me), local_sem)
        mine.start()
        first = [copy(0, me, sibling, src=x_ref)]
        first += [copy(1 + j, me, (*chip, c), src=x_ref) for j, chip in enumerate(chips)]
        for cp in first:
            cp.start()
        passed = [copy(4 + j, (*chip, c), sibling) for j, chip in enumerate(chips)]
        for j, chip in enumerate(chips):
            copy(1 + j, (*chip, c), me).wait_recv()
            passed[j].start()
        copy(0, sibling, me).wait_recv()
        for j, chip in enumerate(chips):
            copy(4 + j, (*chip, 1 - c), me).wait_recv()
        for cp in first + passed:
            cp.wait_send()
        mine.wait()

    return _pcall(
        body, name=name, out_shape=jax.ShapeDtypeStruct((N_DEV,) + block.shape, block.dtype),
        in_specs=[pl.BlockSpec(memory_space=pltpu.VMEM)], out_specs=pl.BlockSpec(memory_space=pltpu.VMEM),
        scratch_shapes=[pltpu.SemaphoreType.DMA((7,)), pltpu.SemaphoreType.DMA((7,)), pltpu.SemaphoreType.DMA],
        compiler_params=pltpu.CompilerParams(vmem_limit_bytes=VMEM_LIMIT_BYTES),
    )(block)


HBM_SPEC = pl.BlockSpec(memory_space=pltpu.HBM)
SEM_SPEC = pl.BlockSpec(memory_space=pltpu.SEMAPHORE)
ANY_SPEC = pl.BlockSpec(memory_space=pl.ANY)
EFFECT = pltpu.SideEffectType.DATAFLOW_SIDE_EFFECTING


def _xfer_start(name, bufs, plan, n_copies, after_last=False):
    nb = len(bufs)
    deps = (_Order.last,) if after_last and _Order.last is not None else ()
    nd = len(deps)

    def body(*refs):
        send_sems, recv_sems = refs[nb + nd], refs[nb + nd + 1]
        token = refs[nb + nd + 2 + nb]
        for k, (src, dst, dev) in enumerate(plan(refs[:nb], *_place())):
            pltpu.make_async_remote_copy(src_ref=src, dst_ref=dst, send_sem=send_sems.at[k], recv_sem=recv_sems.at[k],
                                         device_id=dev, device_id_type=MESH).start()
        token[...] = jnp.zeros_like(token)

    outs = pl.pallas_call(
        body, name=name,
        out_shape=(pltpu.SemaphoreType.DMA((n_copies,)), pltpu.SemaphoreType.DMA((n_copies,)),
                   *[pltpu.HBM(b.shape, b.dtype) for b in bufs], jax.ShapeDtypeStruct((8, LANES), F32)),
        in_specs=[HBM_SPEC] * nb + [ANY_SPEC] * nd,
        out_specs=(SEM_SPEC, SEM_SPEC, *[HBM_SPEC] * nb, pl.BlockSpec(memory_space=pltpu.VMEM)),
        input_output_aliases={i: 2 + i for i in range(nb)},
        compiler_params=pltpu.CompilerParams(has_side_effects=EFFECT),
    )(*[pltpu.with_memory_space_constraint(b, pltpu.HBM) for b in bufs], *deps)
    _Order.last = outs[-1]
    return (outs[0], outs[1]), list(outs[2:2 + nb])


def _xfer_wait(name, sems, bufs, plan):
    nb = len(bufs)

    def body(*refs):
        send_sems, recv_sems = refs[nb], refs[nb + 1]
        for k, (src, dst, dev) in enumerate(plan(refs[:nb], *_place())):
            copy = pltpu.make_async_remote_copy(src_ref=src, dst_ref=dst, send_sem=send_sems.at[k],
                                                recv_sem=recv_sems.at[k], device_id=dev, device_id_type=MESH)
            copy.wait_send()
            copy.wait_recv()

    outs = pl.pallas_call(
        body, name=name, out_shape=tuple(pltpu.HBM(b.shape, b.dtype) for b in bufs),
        in_specs=[HBM_SPEC] * nb + [SEM_SPEC, SEM_SPEC, ANY_SPEC], out_specs=tuple([HBM_SPEC] * nb),
        input_output_aliases={i: i for i in range(nb)},
        compiler_params=pltpu.CompilerParams(has_side_effects=EFFECT),
    )(*bufs, *sems, _Order.last)
    _Order.last = outs[0]
    return list(outs)


def _xfer_hand_over(name, sems, bufs, plan, next_plans):
    nb, n_next = len(bufs), len(next_plans)

    def body(*refs):
        send_sems, recv_sems = refs[nb], refs[nb + 1]
        outs = refs[nb + 3:]
        place = _place()
        for k, (src, dst, dev) in enumerate(plan(refs[:nb], *place)):
            copy = pltpu.make_async_remote_copy(src_ref=src, dst_ref=dst, send_sem=send_sems.at[k],
                                                recv_sem=recv_sems.at[k], device_id=dev, device_id_type=MESH)
            copy.wait_send()
            copy.wait_recv()
        for p, (next_plan, _) in enumerate(next_plans):
            for k, (src, dst, dev) in enumerate(next_plan(refs[:nb], *place)):
                pltpu.make_async_remote_copy(src_ref=src, dst_ref=dst, send_sem=outs[2 * p].at[k],
                                             recv_sem=outs[2 * p + 1].at[k], device_id=dev, device_id_type=MESH).start()
        outs[-1][...] = jnp.zeros_like(outs[-1])

    sem_shapes = [pltpu.SemaphoreType.DMA((copies,)) for _, copies in next_plans for _ in range(2)]
    outs = pl.pallas_call(
        body, name=name,
        out_shape=(*sem_shapes, *[pltpu.HBM(b.shape, b.dtype) for b in bufs], jax.ShapeDtypeStruct((8, LANES), F32)),
        in_specs=[HBM_SPEC] * nb + [SEM_SPEC, SEM_SPEC, ANY_SPEC],
        out_specs=(*[SEM_SPEC] * (2 * n_next), *[HBM_SPEC] * nb, pl.BlockSpec(memory_space=pltpu.VMEM)),
        input_output_aliases={i: 2 * n_next + i for i in range(nb)},
        compiler_params=pltpu.CompilerParams(has_side_effects=EFFECT),
    )(*bufs, *sems, _Order.last)
    _Order.last = outs[-1]
    return [(outs[2 * p], outs[2 * p + 1]) for p in range(n_next)], list(outs[2 * n_next:2 * n_next + nb])


def _half(ref, c, axis):
    rows = ref.shape[axis] // 2
    return pl.ds(c * rows, rows)


def _plan_weights_ici(n):
    def plan(refs, x, y, c, chips):
        out = []
        for w in range(n):
            region = refs[w].at[2 * x + y, _half(refs[w], c, 1), :]
            out += [(region, region, (*chip, c)) for chip in chips]
        return out
    return plan


def _plan_weights_ring(n):
    def plan(refs, x, y, c, chips):
        out = []
        for w in range(n):
            region = refs[w].at[2 * x + y, _half(refs[w], c, 1), :]
            out += [(region, region, (*chip, c)) for chip in chips[:2]]
        return out
    return plan


def _plan_weights_relay(n):
    def plan(refs, x, y, c, chips):
        out = []
        for w in range(n):
            quarter_rows = refs[w].shape[1] // 4
            upper = refs[w].at[2 * x + (1 - y), pl.ds(2 * c * quarter_rows, quarter_rows), :]
            lower = refs[w].at[2 * (1 - x) + y, pl.ds((2 * c + 1) * quarter_rows, quarter_rows), :]
            out += [(upper, upper, (1 - x, y, c)), (lower, lower, (x, 1 - y, c))]
        return out
    return plan


def _plan_weights_d2d(n, which=slice(0, 3)):
    def plan(refs, x, y, c, chips):
        out = []
        for w in range(n):
            rows = _half(refs[w], c, 1)
            for chip in chips[which]:
                region = refs[w].at[2 * chip[0] + chip[1], rows, :]
                out.append((region, region, (x, y, 1 - c)))
        return out
    return plan


def _plan_gather_out(refs, x, y, c, chips):
    mine = refs[0].at[4 * x + 2 * y + c]
    return [(mine, mine, (x, y, 1 - c))] + [(mine, mine, (*chip, c)) for chip in chips]


def _plan_gather_on(refs, x, y, c, chips):
    out = []
    for chip in chips:
        block = refs[0].at[4 * chip[0] + 2 * chip[1] + c]
        out.append((block, block, (x, y, 1 - c)))
    return out


def _plan_grads_d2d(n):
    def plan(refs, x, y, c, chips):
        return [(refs[w].at[:, _half(refs[w], 1 - c, 1), :], refs[n + w], (x, y, 1 - c)) for w in range(n)]
    return plan


def _plan_grads_ici(n):
    def plan(refs, x, y, c, chips):
        out = []
        for w in range(n):
            out += [(refs[w].at[2 * chip[0] + chip[1]], refs[n + w].at[2 * x + y], (*chip, c)) for chip in chips]
        return out
    return plan


def _plan_final_d2d(n):
    def plan(refs, x, y, c, chips):
        out = []
        for w in range(n):
            region = refs[w].at[_half(refs[w], c, 0), :]
            out.append((region, region, (x, y, 1 - c)))
        return out
    return plan


def _stream_blocks(hr, cols):
    bc = cols if cols <= 4096 else _pick(cols, 4096)
    return _pick_rows(hr, max(16, (768 * 1024) // bc)), bc


def _pre_reduce(g, landed, place, name):
    _, rows, cols = g.shape
    hr = rows // 2
    rb, bc = _stream_blocks(hr, cols)
    nrb = hr // rb

    def body(place_ref, g_ref, l_ref, o_ref):
        o_ref[...] = (g_ref[...].astype(F32) + l_ref[...].astype(F32)).astype(o_ref.dtype)

    return _pcall(
        body, name=name, num_scalar_prefetch=1, grid=(N_CHIPS, nrb, cols // bc),
        in_specs=[pl.BlockSpec((None, rb, bc), lambda j, i, k, p: (j, p[1] * nrb + i, k)),
                  pl.BlockSpec((None, rb, bc), lambda j, i, k, p: (j, i, k))],
        out_specs=pl.BlockSpec((None, rb, bc), lambda j, i, k, p: (j, i, k)),
        out_shape=jax.ShapeDtypeStruct((N_CHIPS, hr, cols), g.dtype),
        compiler_params=_params("parallel", "parallel", "parallel"),
    )(place, g, landed)


def _sum_slots(mine, landed, place, name):
    _, hr, cols = mine.shape
    rb, bc = _stream_blocks(hr, cols)
    rb = _pick_rows(hr, max(16, rb // 2))
    nrb = hr // rb

    def body(place_ref, m_ref, l_ref, o_ref):
        chip = place_ref[0]
        own = m_ref[...].astype(F32)
        total = jnp.where(chip == 0, own, l_ref[0].astype(F32))
        for j in range(1, N_CHIPS):
            total = total + jnp.where(chip == j, own, l_ref[j].astype(F32))
        o_ref[...] = total

    return _pcall(
        body, name=name, num_scalar_prefetch=1, grid=(nrb, cols // bc),
        in_specs=[pl.BlockSpec((None, rb, bc), lambda i, k, p: (p[0], i, k)),
                  pl.BlockSpec((N_CHIPS, rb, bc), lambda i, k, p: (0, i, k))],
        out_specs=pl.BlockSpec((rb, bc), lambda i, k, p: (p[1] * nrb + i, k)),
        out_shape=jax.ShapeDtypeStruct((2 * hr, cols), F32),
        compiler_params=_params("parallel", "parallel"),
    )(place, mine, landed)


class _WeightGather:
    @staticmethod
    def zones(quarters, chip):
        return [lax.dynamic_update_slice(lax.empty((N_CHIPS,) + q.shape, BF16), q.astype(BF16)[None], (chip, 0, 0))
                for q in quarters]

    def __init__(self, tag, zones, ring=False):
        self.tag, self.n, self.ring = tag, len(zones), ring
        self.plan = _plan_weights_ring(self.n) if ring else _plan_weights_ici(self.n)
        self.sems, self.bufs = _xfer_start("wici_start_" + tag, zones, self.plan, (2 if ring else 3) * self.n,
                                           after_last=True)

    def relay(self):
        plan = _plan_weights_relay(self.n)
        (self.sems,), self.bufs = _xfer_hand_over("wrel_start_" + self.tag, self.sems, self.bufs, self.plan,
                                                   [(plan, 2 * self.n)])
        self.plan = plan

    def pass_on(self):
        plan = _plan_weights_d2d(self.n)
        (self.sems,), self.bufs = _xfer_hand_over("wd2d_start_" + self.tag, self.sems, self.bufs, self.plan,
                                                   [(plan, 3 * self.n)])
        self.plan = plan

    def done(self):
        return _xfer_wait("wd2d_wait_" + self.tag, self.sems, self.bufs, self.plan)

    def relay_and_pass_near(self):
        plan, self.near_plan = _plan_weights_relay(self.n), _plan_weights_d2d(self.n, slice(0, 2))
        (self.sems, self.near_sems), self.bufs = _xfer_hand_over(
            "wrel_start_" + self.tag, self.sems, self.bufs, self.plan, [(plan, 2 * self.n), (self.near_plan, 2 * self.n)])
        self.plan = plan

    def near_done(self):
        self.bufs = _xfer_wait("wnear_wait_" + self.tag, self.near_sems, self.bufs, self.near_plan)
        return self.bufs

    def pass_far(self):
        plan = _plan_weights_d2d(self.n, slice(2, 3))
        (sems,), bufs = _xfer_hand_over("wfar_start_" + self.tag, self.sems, self.bufs, self.plan, [(plan, self.n)])
        return _xfer_wait("wfar_wait_" + self.tag, sems, bufs, plan)


class _GradReduce:
    def __init__(self, tag, grads):
        self.tag, self.n = tag, len(grads)
        zones = [lax.empty((N_CHIPS, g.shape[1] // 2, g.shape[2]), g.dtype) for g in grads]
        self.plan = _plan_grads_d2d(self.n)
        self.sems, self.bufs = _xfer_start("gd2d_start_" + tag, list(grads) + zones, self.plan, self.n)

    def pair(self, place):
        n = self.n
        bufs = _xfer_wait("gd2d_wait_" + self.tag, self.sems, self.bufs, self.plan)
        self.halves = [_pre_reduce(bufs[w], bufs[n + w], place, f"pre_reduce_{self.tag}{w}") for w in range(n)]

    def cross(self, after_last=False):
        zones = [lax.empty(h.shape, h.dtype) for h in self.halves]
        self.plan = _plan_grads_ici(self.n)
        self.sems, self.bufs = _xfer_start("gici_start_" + self.tag, self.halves + zones, self.plan, 3 * self.n,
                                           after_last=after_last)

    def step(self, place):
        self.pair(place)
        self.cross()

    def join(self, place):
        n = self.n
        bufs = _xfer_wait("gici_wait_" + self.tag, self.sems, self.bufs, self.plan)
        sums = [_sum_slots(bufs[w], bufs[n + w], place, f"sum_slots_{self.tag}{w}") for w in range(n)]
        self.plan = _plan_final_d2d(n)
        self.sems, self.bufs = _xfer_start("gfin_start_" + self.tag, sums, self.plan, n)

    def done(self):
        return _xfer_wait("gfin_wait_" + self.tag, self.sems, self.bufs, self.plan)


def _ada_fwd(c_all, w_q, b_q):
    d, n = w_q.shape
    bn = _pick(n, 512)

    def body(c_ref, w_ref, b_ref, o_ref):
        cv = c_ref[...]
        act = cv * _sigmoid(cv)
        o_ref[...] = _dot(act, w_ref[...], NN, lax.Precision.HIGHEST) + b_ref[...]

    return _pcall(
        body, name="ada_fwd", grid=(n // bn,),
        in_specs=[pl.BlockSpec((N_DEV, d), lambda j: (0, 0)), pl.BlockSpec((d, bn), lambda j: (0, j)),
                  pl.BlockSpec((1, bn), lambda j: (0, j))],
        out_specs=pl.BlockSpec((N_DEV, bn), lambda j: (0, j)),
        out_shape=jax.ShapeDtypeStruct((N_DEV, n), F32), compiler_params=_params("parallel"),
    )(c_all, w_q, b_q)


def _row_spec(rb, width, col=0):
    return pl.BlockSpec((rb, width), lambda i, col=col: (i, col))


def _vec_spec(width, col=0):
    return pl.BlockSpec((1, width), lambda i, col=col: (0, col))


def _norm_mod_fwd(x, g, sc, sh, name, res=None, gt=None):
    s, d = x.shape
    rb = _pick(s, 256)
    has_res = res is not None

    def body(*refs):
        if has_res:
            x_ref, res_ref, gt_ref, g_ref, sc_ref, sh_ref, x1_ref, h_ref = refs
            xv = x_ref[...] + gt_ref[...] * res_ref[...]
            x1_ref[...] = xv
        else:
            x_ref, g_ref, sc_ref, sh_ref, h_ref = refs
            xv = x_ref[...]
        xh, _ = _rms(xv)
        h_ref[...] = (xh * g_ref[...] * (1.0 + sc_ref[...]) + sh_ref[...]).astype(BF16)

    row, vec = _row_spec(rb, d), _vec_spec(d)
    if has_res:
        ins, in_specs = (x, res, gt, g, sc, sh), [row, row, vec, vec, vec, vec]
        out_shape = [jax.ShapeDtypeStruct((s, d), F32), jax.ShapeDtypeStruct((s, d), BF16)]
        out_specs = [row, row]
    else:
        ins, in_specs = (x, g, sc, sh), [row, vec, vec, vec]
        out_shape, out_specs = jax.ShapeDtypeStruct((s, d), BF16), row
    return _pcall(body, name=name, grid=(s // rb,), in_specs=in_specs, out_specs=out_specs,
                          out_shape=out_shape, compiler_params=_params("parallel"))(*ins)


def _final_loss(x1, f, gt2, g_final, target):
    s, d = x1.shape
    rb = _pick(s, 256)

    def body(x1_ref, f_ref, gt_ref, g_ref, t_ref, dx_ref, df_ref, loss_ref, dg_ref, dgt_ref):
        first = pl.program_id(0) == 0
        fv, gt, gv = f_ref[...], gt_ref[...], g_ref[...]
        x2 = x1_ref[...] + gt * fv
        xh, r = _rms(x2)
        err = xh * gv - t_ref[...]
        blk = 0.5 * jnp.sum(jnp.sum(err * err, axis=1, keepdims=True), axis=0, keepdims=True) / d
        dy = err / d
        dxh = dy * gv
        dx = r * (dxh - xh * jnp.mean(dxh * xh, axis=-1, keepdims=True))
        dx_ref[...] = dx
        df_ref[...] = (dx * gt).astype(BF16)
        _accumulate(first, loss_ref, jnp.broadcast_to(blk, (1, LANES)))
        _accumulate(first, dg_ref, _colsum(dy * xh))
        _accumulate(first, dgt_ref, _colsum(dx * fv))

    row, vec = _row_spec(rb, d), _vec_spec(d)
    return _pcall(
        body, name="final_loss", grid=(s // rb,), in_specs=[row, row, vec, vec, row],
        out_specs=[row, row, _vec_spec(LANES), vec, vec],
        out_shape=[jax.ShapeDtypeStruct((s, d), F32), jax.ShapeDtypeStruct((s, d), BF16),
                   jax.ShapeDtypeStruct((1, LANES), F32), jax.ShapeDtypeStruct((1, d), F32),
                   jax.ShapeDtypeStruct((1, d), F32)],
        compiler_params=_params("arbitrary"),
    )(x1, f, gt2, g_final, target)


def _norm_mod_bwd(dh, xin, dres, g, sc, name, branch=None, gt=None):
    s, d = xin.shape
    rb = _pick(s, 256)
    has_branch = branch is not None

    def body(*refs):
        if has_branch:
            dh_ref, x_ref, dres_ref, g_ref, sc_ref, br_ref, gt_ref, dx_ref, dbr_ref, dsh_ref, dsc_ref, dg_ref, dgt_ref = refs
        else:
            dh_ref, x_ref, dres_ref, g_ref, sc_ref, dx_ref, dsh_ref, dsc_ref, dg_ref = refs
        first = pl.program_id(0) == 0
        gv = g_ref[...]
        xh, r = _rms(x_ref[...])
        dhv = dh_ref[...]
        dn = dhv * (1.0 + sc_ref[...])
        dxh = dn * gv
        dx = dres_ref[...] + r * (dxh - xh * jnp.mean(dxh * xh, axis=-1, keepdims=True))
        dx_ref[...] = dx
        _accumulate(first, dsh_ref, _colsum(dhv))
        _accumulate(first, dsc_ref, _colsum(dhv * xh * gv))
        _accumulate(first, dg_ref, _colsum(dn * xh))
        if has_branch:
            dbr_ref[...] = (dx * gt_ref[...]).astype(BF16)
            _accumulate(first, dgt_ref, _colsum(dx * br_ref[...]))

    row, vec = _row_spec(rb, d), _vec_spec(d)
    vec_shape = jax.ShapeDtypeStruct((1, d), F32)
    if has_branch:
        ins, in_specs = (dh, xin, dres, g, sc, branch, gt), [row, row, row, vec, vec, row, vec]
        out_specs = [row, row, vec, vec, vec, vec]
        out_shape = [jax.ShapeDtypeStruct((s, d), F32), jax.ShapeDtypeStruct((s, d), BF16)] + [vec_shape] * 4
    else:
        ins, in_specs = (dh, xin, dres, g, sc), [row, row, row, vec, vec]
        out_specs = [row, vec, vec, vec]
        out_shape = [jax.ShapeDtypeStruct((s, d), F32)] + [vec_shape] * 3
    return _pcall(body, name=name, grid=(s // rb,), in_specs=in_specs, out_specs=out_specs,
                          out_shape=out_shape, compiler_params=_params("arbitrary"))(*ins)


def _branch_merge(ya, yb, w_bg, w_bh, z, b_gate, gate_col, bm=512, bn=1024):
    s, gw = ya.shape
    d = w_bg.shape[1]
    bm, bn = _pick(s, bm), _pick(d, bn)
    nj = d // bn

    def body(ya_ref, yb_ref, wa_ref, wb_ref, za_ref, zb_ref, ba_ref, bb_ref, pa_ref, pb_ref, y_ref):
        pa = _dot(ya_ref[...], wa_ref[...])
        pb = _dot(yb_ref[...], wb_ref[...])
        pa_ref[...] = pa.astype(BF16)
        pb_ref[...] = pb.astype(BF16)
        ga = _sigmoid(za_ref[...] + ba_ref[...])
        gb = _sigmoid(zb_ref[...] + bb_ref[...])
        y_ref[...] = (ga * pa + gb * pb).astype(BF16)

    act = pl.BlockSpec((bm, gw), lambda j, i: (i, 0))
    wgt = pl.BlockSpec((gw, bn), lambda j, i: (0, j))
    tile = pl.BlockSpec((bm, bn), lambda j, i: (i, j))
    shape = jax.ShapeDtypeStruct((s, d), BF16)
    return _pcall(
        body, name="branch_merge", grid=(nj, s // bm),
        in_specs=[act, act, wgt, wgt,
                  pl.BlockSpec((bm, bn), lambda j, i: (i, gate_col * nj + j)),
                  pl.BlockSpec((bm, bn), lambda j, i: (i, (gate_col + 1) * nj + j)),
                  pl.BlockSpec((1, bn), lambda j, i: (0, j)), pl.BlockSpec((1, bn), lambda j, i: (0, nj + j))],
        out_specs=[tile, tile, tile], out_shape=[shape, shape, shape], compiler_params=_params("parallel", "parallel"),
    )(ya, yb, w_bg, w_bh, z, z, b_gate, b_gate)


def _branch_wgrad(ya, yb, dpa, dpb):
    s, gw = ya.shape
    per = dpa.shape[1] // N_CHIPS

    def body(ya_ref, yb_ref, da_ref, db_ref, ga_ref, gb_ref):
        ga_ref[...] = _dot(ya_ref[...], da_ref[...], TN).astype(BF16)
        gb_ref[...] = _dot(yb_ref[...], db_ref[...], TN).astype(BF16)

    act = pl.BlockSpec((s, gw), lambda j: (0, 0))
    grd = pl.BlockSpec((s, per), lambda j: (0, j))
    out = pl.BlockSpec((None, gw, per), lambda j: (j, 0, 0))
    shape = jax.ShapeDtypeStruct((N_CHIPS, gw, per), BF16)
    return _pcall(body, name="branch_wgrad", grid=(N_CHIPS,), in_specs=[act, act, grd, grd], out_specs=[out, out],
                  out_shape=[shape, shape], compiler_params=_params("parallel"))(ya, yb, dpa, dpb)


def _branch_bwd(dpa, dpb, w_bg, w_bh, bm=512):
    s, d = dpa.shape
    gw = w_bg.shape[0]
    bm = _pick(s, bm)

    def body(da_ref, db_ref, wa_ref, wb_ref, oa_ref, ob_ref):
        oa_ref[...] = _dot(da_ref[...], wa_ref[...], NT)
        ob_ref[...] = _dot(db_ref[...], wb_ref[...], NT)

    act = pl.BlockSpec((bm, d), lambda i: (i, 0))
    wgt = pl.BlockSpec((gw, d), lambda i: (0, 0))
    out = pl.BlockSpec((bm, gw), lambda i: (i, 0))
    shape = jax.ShapeDtypeStruct((s, gw), F32)
    return _pcall(body, name="branch_bwd", grid=(s // bm,), in_specs=[act, act, wgt, wgt], out_specs=[out, out],
                  out_shape=[shape, shape], compiler_params=_params("parallel"))(dpa, dpb, w_bg, w_bh)


def _out_bwd(dyo, w_out, pa, pb, z, b_gate, gate_col, bm=512, bn=1024):
    s, d = dyo.shape
    bm, bn = _pick(s, bm), _pick(d, bn)
    nj = d // bn

    def body(dyo_ref, w_ref, pa_ref, pb_ref, za_ref, zb_ref, ba_ref, bb_ref,
             dpa_ref, dpb_ref, dza_ref, dzb_ref, dba_ref, dbb_ref):
        first = pl.program_id(1) == 0
        dyv = _dot(dyo_ref[...], w_ref[...], NT)
        ga = _sigmoid(za_ref[...] + ba_ref[...])
        gb = _sigmoid(zb_ref[...] + bb_ref[...])
        dpa_ref[...] = (dyv * ga).astype(BF16)
        dpb_ref[...] = (dyv * gb).astype(BF16)
        dga = dyv * pa_ref[...].astype(F32) * ga * (1.0 - ga)
        dgb = dyv * pb_ref[...].astype(F32) * gb * (1.0 - gb)
        dza_ref[...] = dga.astype(BF16)
        dzb_ref[...] = dgb.astype(BF16)
        _accumulate(first, dba_ref, _colsum(dga))
        _accumulate(first, dbb_ref, _colsum(dgb))

    tile = pl.BlockSpec((bm, bn), lambda j, i: (i, j))
    vec = pl.BlockSpec((1, bn), lambda j, i: (0, j))
    act, vec_shape = jax.ShapeDtypeStruct((s, d), BF16), jax.ShapeDtypeStruct((1, d), F32)
    return _pcall(
        body, name="out_bwd", grid=(nj, s // bm),
        in_specs=[pl.BlockSpec((bm, d), lambda j, i: (i, 0)), pl.BlockSpec((bn, d), lambda j, i: (j, 0)), tile, tile,
                  pl.BlockSpec((bm, bn), lambda j, i: (i, gate_col * nj + j)),
                  pl.BlockSpec((bm, bn), lambda j, i: (i, (gate_col + 1) * nj + j)),
                  vec, pl.BlockSpec((1, bn), lambda j, i: (0, nj + j))],
        out_specs=[tile, tile, tile, tile, vec, vec], out_shape=[act, act, act, act, vec_shape, vec_shape],
        compiler_params=_params("parallel", "arbitrary"),
    )(dyo, w_out, pa, pb, z, z, b_gate, b_gate)


def _ffn_in(h, w_fi, bm=512):
    s, d = h.shape
    per = w_fi.shape[2]
    ff = 2 * per
    bm, bn = _pick(s, bm), _half_width(per)
    per_blocks = per // bn

    def body(h_ref, wa_ref, wu_ref, a_ref, u_ref, hf_ref):
        hv = h_ref[...]
        a = _dot(hv, wa_ref[...])
        up = _dot(hv, wu_ref[...])
        a_ref[...] = a.astype(BF16)
        u_ref[...] = up.astype(BF16)
        hf_ref[...] = (a * _sigmoid(a) * up).astype(BF16)

    out = pl.BlockSpec((bm, bn), lambda j, i: (i, j))
    shape = jax.ShapeDtypeStruct((s, ff), BF16)
    return _pcall(
        body, name="ffn_in", grid=(ff // bn, s // bm),
        in_specs=[pl.BlockSpec((bm, d), lambda j, i: (i, 0)),
                  pl.BlockSpec((None, d, bn), lambda j, i: (j // per_blocks, 0, j % per_blocks)),
                  pl.BlockSpec((None, d, bn), lambda j, i: (2 + j // per_blocks, 0, j % per_blocks))],
        out_specs=[out, out, out], out_shape=[shape, shape, shape],
        compiler_params=_params("parallel", "parallel"),
    )(h, w_fi, w_fi)


def _ffn_out_bwd(dffn, w_fo, a_act, up_act, bm=512):
    s, d = dffn.shape
    ff = w_fo.shape[0]
    bm, bn = _pick(s, bm), _half_width(ff // 2)

    def body(d_ref, w_ref, a_ref, u_ref, da_ref, du_ref):
        dhf = _dot(d_ref[...], w_ref[...], NT)
        a = a_ref[...].astype(F32)
        sa = _sigmoid(a)
        da_ref[...] = (dhf * u_ref[...].astype(F32) * sa * (1.0 + a * (1.0 - sa))).astype(BF16)
        du_ref[...] = (dhf * a * sa).astype(BF16)

    tile = pl.BlockSpec((bm, bn), lambda j, i: (i, j))
    shape = jax.ShapeDtypeStruct((s, ff), BF16)
    return _pcall(
        body, name="ffn_out_bwd", grid=(ff // bn, s // bm),
        in_specs=[pl.BlockSpec((bm, d), lambda j, i: (i, 0)), pl.BlockSpec((bn, d), lambda j, i: (j, 0)), tile, tile],
        out_specs=[tile, tile], out_shape=[shape, shape], compiler_params=_params("parallel", "parallel"),
    )(dffn, w_fo, a_act, up_act)


def _tril(n):
    return lax.broadcasted_iota(jnp.int32, (n, n), 0) >= lax.broadcasted_iota(jnp.int32, (n, n), 1)


def _gmlp_norm(v, ln_g, ln_b):
    gv = _gelu(v)
    mu = jnp.mean(gv, axis=-1, keepdims=True)
    cen = gv - mu
    rs = lax.rsqrt(jnp.mean(cen * cen, axis=-1, keepdims=True) + EPS)
    xh = cen * rs
    return xh, rs, xh * ln_g + ln_b


def _gmlp_fwd(z, ln_g, ln_b, ws, bs_t):
    s = z.shape[0]
    gw = ln_g.shape[1]
    groups, chunk, _ = ws.shape

    def body(u_ref, v_ref, lg_ref, lb_ref, ws_ref, bs_ref, ya_ref):
        gu = _gelu(u_ref[...])
        _, _, vn = _gmlp_norm(v_ref[...], lg_ref[...], lb_ref[...])
        mask = _tril(chunk)
        for g in range(groups):
            cols = slice(g * LANES, (g + 1) * LANES)
            wm = jnp.where(mask, ws_ref[g], 0.0).astype(BF16)
            sg = _dot(wm, vn[:, cols].astype(BF16)) + bs_ref[:, g:g + 1]
            ya_ref[:, cols] = (gu[:, cols] * sg).astype(BF16)

    return _pcall(
        body, name="gmlp_fwd", grid=(s // chunk,),
        in_specs=[_row_spec(chunk, gw, 0), _row_spec(chunk, gw, 1), _vec_spec(gw), _vec_spec(gw),
                  pl.BlockSpec((groups, chunk, chunk), lambda i: (0, 0, 0)), pl.BlockSpec((chunk, LANES), lambda i: (0, 0))],
        out_specs=_row_spec(chunk, gw), out_shape=jax.ShapeDtypeStruct((s, gw), BF16),
        compiler_params=_params("parallel"),
    )(z, z, ln_g, ln_b, ws, bs_t)


def _gmlp_bwd(dya, z, ln_g, ln_b, ws, bs_t):
    s = z.shape[0]
    gw = ln_g.shape[1]
    groups, chunk, _ = ws.shape

    def body(dya_ref, u_ref, v_ref, lg_ref, lb_ref, ws_ref, bs_ref, duv_ref, dws_ref, dbs_ref, dlg_ref, dlb_ref, dvn_ref):
        first = pl.program_id(0) == 0
        u, v, lg = u_ref[...], v_ref[...], lg_ref[...]
        gu = _gelu(u)
        xh, rs, vn = _gmlp_norm(v, lg, lb_ref[...])
        dyav = dya_ref[...]
        mask = _tril(chunk)
        lane = lax.broadcasted_iota(jnp.int32, (chunk, LANES), 1)
        dbs = jnp.zeros((chunk, LANES), F32)
        for g in range(groups):
            cols = slice(g * LANES, (g + 1) * LANES)
            wm = jnp.where(mask, ws_ref[g], 0.0).astype(BF16)
            vg = vn[:, cols].astype(BF16)
            sg = _dot(wm, vg) + bs_ref[:, g:g + 1]
            ds = dyav[:, cols] * gu[:, cols]
            duv_ref[:, cols] = (dyav[:, cols] * sg * _gelu_grad(u[:, cols])).astype(BF16)
            dsb = ds.astype(BF16)
            _accumulate(first, dws_ref.at[g], jnp.where(mask, _dot(dsb, vg, NT), 0.0))
            dbs = dbs + jnp.where(lane == g, jnp.sum(ds, axis=-1, keepdims=True), 0.0)
            dvn_ref[:, cols] = _dot(wm, dsb, TN)
        dvn = dvn_ref[...]
        _accumulate(first, dbs_ref, dbs)
        _accumulate(first, dlb_ref, _colsum(dvn))
        _accumulate(first, dlg_ref, _colsum(dvn * xh))
        dxh = dvn * lg
        dgv = rs * (dxh - jnp.mean(dxh, axis=-1, keepdims=True) - xh * jnp.mean(dxh * xh, axis=-1, keepdims=True))
        duv_ref[:, gw:] = (dgv * _gelu_grad(v)).astype(BF16)

    return _pcall(
        body, name="gmlp_bwd", grid=(s // chunk,),
        in_specs=[_row_spec(chunk, gw), _row_spec(chunk, gw, 0), _row_spec(chunk, gw, 1), _vec_spec(gw), _vec_spec(gw),
                  pl.BlockSpec((groups, chunk, chunk), lambda i: (0, 0, 0)), pl.BlockSpec((chunk, LANES), lambda i: (0, 0))],
        out_specs=[_row_spec(chunk, 2 * gw), pl.BlockSpec((groups, chunk, chunk), lambda i: (0, 0, 0)),
                   pl.BlockSpec((chunk, LANES), lambda i: (0, 0)), _vec_spec(gw), _vec_spec(gw)],
        out_shape=[jax.ShapeDtypeStruct((s, 2 * gw), BF16), jax.ShapeDtypeStruct((groups, chunk, chunk), F32),
                   jax.ShapeDtypeStruct((chunk, LANES), F32), jax.ShapeDtypeStruct((1, gw), F32),
                   jax.ShapeDtypeStruct((1, gw), F32)],
        scratch_shapes=[pltpu.VMEM((chunk, gw), F32)],
        compiler_params=_params("arbitrary"),
    )(dya, z, z, ln_g, ln_b, ws, bs_t)


def _lower_bound(lb_ref):
    a0, a1 = lb_ref[0:1, :], lb_ref[1:2, :]
    mx = jnp.maximum(a0, a1)
    e0, e1 = jnp.exp(a0 - mx), jnp.exp(a1 - mx)
    return e0 / (e0 + e1)


def _sum_dot(mask, x):
    hi = x.astype(BF16)
    rest = x - hi.astype(F32)
    mid = rest.astype(BF16)
    low = (rest - mid.astype(F32)).astype(BF16)
    return _dot(mask, hi) + _dot(mask, mid) + _dot(mask, low)


def _ones_where(mask):
    return jnp.where(mask, 1.0, 0.0).astype(BF16)


def _hg_masks(rows, t):
    r = lax.broadcasted_iota(jnp.int32, (rows, rows), 0)
    c = lax.broadcasted_iota(jnp.int32, (rows, rows), 1)
    same = (r // t) == (c // t)
    incl = jnp.logical_and(same, c <= r)
    upto_mid = jnp.logical_and(same, (c % t) <= t // 2)
    rev = jnp.logical_and(same, c >= r)
    return same, incl, upto_mid, rev


def _hg_block(q, fp, lb, masks):
    rows = q.shape[0]
    same, incl, upto_mid, _ = masks
    sig = _sigmoid(fp)
    f = lb + (1.0 - lb) * sig
    k = 1.0 - f
    sq = _sigmoid(q)
    qa = q * sq
    stacked = jnp.concatenate([_ones_where(m) for m in (incl, same, upto_mid)], axis=0)
    sums = _sum_dot(stacked, jnp.log(f))
    b, b_last, b_mid = sums[:rows], sums[rows:2 * rows], sums[2 * rows:]
    e_q = jnp.exp(jnp.minimum(b - b_mid, EXP_CLAMP))
    e_k = jnp.exp(jnp.minimum(b_mid - b, EXP_CLAMP))
    e_in = jnp.exp(b)
    e_out = jnp.exp(b_last - b)
    return dict(sig=sig, f=f, k=k, sq=sq, qa=qa, e_last=jnp.exp(b_last), e_q=e_q, e_k=e_k, e_in=e_in, e_out=e_out,
                q_hat=(qa * e_q).astype(BF16), k_hat=(k * e_k).astype(BF16),
                q_in=(qa * e_in).astype(BF16), k_out=k * e_out)


def _hgrn_fwd(z, hg_lb, norm_g, q_col):
    s = z.shape[0]
    hw = norm_g.shape[1]
    heads = hw // LANES
    t = HG_CHUNK
    rows = min(HG_ROWS, s)
    per_step = rows // t
    hp = min(HG_HEADS_PER_STEP, heads)
    assert heads % hp == 0 and q_col % hp == 0, (heads, q_col)
    wide = hp * LANES

    def zspec(which):
        return pl.BlockSpec((rows, wide), lambda h, r, which=which: (r, (q_col + which * heads) // hp + h))

    def body(q_ref, f_ref, i_ref, g_ref, lb_ref, ng_ref, yb_ref, o_ref, st_out_ref, st_ref, e_last_ref, inter_ref):
        @pl.when(pl.program_id(1) == 0)
        def _():
            st_ref[...] = jnp.zeros_like(st_ref)

        masks = _hg_masks(rows, t)
        for hh in range(hp):
            cols = slice(hh * LANES, (hh + 1) * LANES)
            blk = _hg_block(q_ref[:, cols], f_ref[:, cols], _lower_bound(lb_ref.at[:, cols]), masks)
            iv = i_ref[:, cols].astype(BF16)
            q_in, k_out = blk["q_in"], blk["k_out"].astype(BF16)
            e_last_ref[hh] = blk["e_last"]
            attn = jnp.where(masks[1], _dot(blk["q_hat"], blk["k_hat"], NT), 0.0).astype(BF16)
            o = _dot(attn, iv)
            grown = [_dot(iv[j * t:(j + 1) * t], k_out[j * t:(j + 1) * t], TN) for j in range(per_step)]
            st = st_ref[hh]
            for j in range(per_step):
                st_out_ref[hh, j] = st
                inter_ref[hh, j * t:(j + 1) * t, :] = _dot(q_in[j * t:(j + 1) * t], st.astype(BF16), NT)
                st = st * e_last_ref[hh, j * t:j * t + 1, :] + grown[j]
            st_ref[hh] = st
            o = o + inter_ref[hh]
            o_ref[:, cols] = o
            og = g_ref[:, cols]
            on, _ = _rms(o)
            yb_ref[:, cols] = (on * ng_ref[:, cols] * (og * _sigmoid(og))).astype(BF16)

    out_row = pl.BlockSpec((rows, wide), lambda h, r: (r, h))
    return _pcall(
        body, name="hgrn_fwd", grid=(heads // hp, s // rows),
        in_specs=[zspec(0), zspec(1), zspec(2), zspec(3),
                  pl.BlockSpec((2, wide), lambda h, r: (0, h)), pl.BlockSpec((1, wide), lambda h, r: (0, h))],
        out_specs=[out_row, out_row, pl.BlockSpec((hp, per_step, LANES, LANES), lambda h, r: (h, r, 0, 0))],
        out_shape=[jax.ShapeDtypeStruct((s, hw), BF16), jax.ShapeDtypeStruct((s, hw), F32),
                   jax.ShapeDtypeStruct((heads, s // t, LANES, LANES), F32)],
        scratch_shapes=[pltpu.VMEM((hp, LANES, LANES), F32), pltpu.VMEM((hp, rows, LANES), F32),
                        pltpu.VMEM((hp, rows, LANES), F32)],
        compiler_params=_params("parallel", "arbitrary"),
    )(z, z, z, z, hg_lb, norm_g)


def _hgrn_bwd(dyb, z, o_raw, states, hg_lb, norm_g, q_col):
    s = z.shape[0]
    hw = norm_g.shape[1]
    heads = hw // LANES
    t = HG_CHUNK
    rows = min(HG_ROWS, s)
    per_step = rows // t
    n_steps = s // rows
    hp = min(HG_HEADS_PER_STEP, heads)
    assert heads % hp == 0 and q_col % hp == 0, (heads, q_col)
    wide = hp * LANES

    def zspec(which):
        return pl.BlockSpec((rows, wide), lambda h, r, which=which: (n_steps - 1 - r, (q_col + which * heads) // hp + h))

    def body(dyb_ref, q_ref, f_ref, i_ref, g_ref, o_ref, st_in_ref, lb_ref, ng_ref,
             dq_ref, df_ref, di_ref, dg_ref, dlb_ref, dng_ref, dst_ref, acc_lb_ref, acc_ng_ref,
             e_last_ref, dq_in_ref, dk_out_ref, di_inter_ref, carry_ref):
        step = pl.program_id(1)

        @pl.when(step == 0)
        def _():
            dst_ref[...] = jnp.zeros_like(dst_ref)
            acc_lb_ref[...] = jnp.zeros_like(acc_lb_ref)
            acc_ng_ref[...] = jnp.zeros_like(acc_ng_ref)

        masks = _hg_masks(rows, t)
        same, incl, _, rev = masks
        sum_mask = jnp.concatenate([_ones_where(rev), _ones_where(same)], axis=1)
        for hh in range(hp):
            cols = slice(hh * LANES, (hh + 1) * LANES)
            lb = _lower_bound(lb_ref.at[:, cols])
            ng = ng_ref[:, cols]
            q = q_ref[:, cols]
            blk = _hg_block(q, f_ref[:, cols], lb, masks)
            iv = i_ref[:, cols].astype(BF16)
            o, og, dy = o_ref[:, cols], g_ref[:, cols], dyb_ref[:, cols]
            so = _sigmoid(og)
            on, r = _rms(o)
            acc_ng_ref[:, cols] += _colsum(dy * on * (og * so))
            dg_ref[:, cols] = (dy * on * ng * so * (1.0 + og * (1.0 - so))).astype(BF16)
            don = dy * ng * (og * so)
            do = (r * (don - on * jnp.mean(don * on, axis=-1, keepdims=True))).astype(BF16)
            q_hat, k_hat, q_in, k_out = blk["q_hat"], blk["k_hat"], blk["q_in"], blk["k_out"]
            k_out_b = k_out.astype(BF16)
            attn = jnp.where(incl, _dot(q_hat, k_hat, NT), 0.0).astype(BF16)
            d_attn = jnp.where(incl, _dot(do, iv, NT), 0.0).astype(BF16)
            di_intra = _dot(attn, do, TN)
            dq_hat = _dot(d_attn, k_hat)
            dk_hat = _dot(d_attn, q_hat, TN)
            e_last_ref[hh] = blk["e_last"]
            grown = [_dot(do[j * t:(j + 1) * t], q_in[j * t:(j + 1) * t], TN) for j in range(per_step)]
            dst = dst_ref[hh]
            for j in reversed(range(per_step)):
                rs_ = slice(j * t, (j + 1) * t)
                e_last = e_last_ref[hh, j * t:j * t + 1, :]
                st_prev, dst_b = st_in_ref[hh, j], dst.astype(BF16)
                dq_in_ref[hh, rs_, :] = _dot(do[rs_], st_prev.astype(BF16))
                dk_out_ref[hh, rs_, :] = _dot(iv[rs_], dst_b)
                di_inter_ref[hh, rs_, :] = _dot(k_out_b[rs_], dst_b, NT)
                carry_ref[hh, rs_, :] = jnp.broadcast_to(e_last * _colsum(st_prev * dst), (t, LANES))
                dst = dst * e_last + grown[j]
            dst_ref[hh] = dst
            di_ref[:, cols] = (di_intra + di_inter_ref[hh]).astype(BF16)
            dk_out = dk_out_ref[hh]
            dqa = dq_in_ref[hh] * blk["e_in"] + dq_hat * blk["e_q"]
            dk = dk_out * blk["e_out"] + dk_hat * blk["e_k"]
            db = blk["qa"] * dqa - blk["k"] * dk
            dlf = _sum_dot(sum_mask, jnp.concatenate([db, dk_out * k_out], axis=0)) + carry_ref[hh]
            dfv = dlf / blk["f"] - dk
            sig, sq = blk["sig"], blk["sq"]
            df_ref[:, cols] = (dfv * (1.0 - lb) * sig * (1.0 - sig)).astype(BF16)
            acc_lb_ref[:, cols] += _colsum(dfv * (1.0 - sig))
            dq_ref[:, cols] = (dqa * sq * (1.0 + q * (1.0 - sq))).astype(BF16)

        @pl.when(step == n_steps - 1)
        def _():
            lb = _lower_bound(lb_ref)
            d0 = acc_lb_ref[...] * lb * (1.0 - lb)
            dlb_ref[0:1, :] = d0
            dlb_ref[1:2, :] = -d0
            dng_ref[...] = acc_ng_ref[...]

    rev_row = pl.BlockSpec((rows, wide), lambda h, r: (n_steps - 1 - r, h))
    piece = jax.ShapeDtypeStruct((s, hw), BF16)
    return _pcall(
        body, name="hgrn_bwd", grid=(heads // hp, n_steps),
        in_specs=[rev_row, zspec(0), zspec(1), zspec(2), zspec(3), rev_row,
                  pl.BlockSpec((hp, per_step, LANES, LANES), lambda h, r: (h, n_steps - 1 - r, 0, 0)),
                  pl.BlockSpec((2, wide), lambda h, r: (0, h)), pl.BlockSpec((1, wide), lambda h, r: (0, h))],
        out_specs=[rev_row, rev_row, rev_row, rev_row,
                   pl.BlockSpec((2, wide), lambda h, r: (0, h)), pl.BlockSpec((1, wide), lambda h, r: (0, h))],
        out_shape=[piece, piece, piece, piece, jax.ShapeDtypeStruct((2, hw), F32), jax.ShapeDtypeStruct((1, hw), F32)],
        scratch_shapes=[pltpu.VMEM((hp, LANES, LANES), F32), pltpu.VMEM((1, wide), F32), pltpu.VMEM((1, wide), F32)]
        + [pltpu.VMEM((hp, rows, LANES), F32)] * 5,
        compiler_params=_params("parallel", "arbitrary"),
    )(dyb, z, z, z, z, o_raw, states, hg_lb, norm_g)


def _adam_update(w, m, v, g):
    m2 = ADAM_B1 * m + (1.0 - ADAM_B1) * g
    v2 = ADAM_B2 * v + (1.0 - ADAM_B2) * (g * g)
    m_hat = m2 * (1.0 / (1.0 - ADAM_B1 ** ADAM_STEP))
    v_hat = v2 * (1.0 / (1.0 - ADAM_B2 ** ADAM_STEP))
    return -ADAM_LR * (m_hat / (jnp.sqrt(v_hat) + ADAM_EPS) + ADAM_WD * w), m2, v2


def _adamw(w, m, v, parts, name, outer=False):
    rows, cols = w.shape
    bc = cols if cols <= 4096 else _pick(cols, 4096)
    rb = _pick_rows(rows, max(8, (384 * 1024) // bc), mult=8)
    if outer and rb % LANES:
        rb = rows

    def body(w_ref, m_ref, v_ref, *refs):
        g_ref, d_ref, mo_ref, vo_ref = refs[-4:]
        if outer:
            cv = refs[0][...]
            g = _dot(cv * _sigmoid(cv), refs[1][...], TN, lax.Precision.HIGHEST)
        else:
            p_ref = refs[0]
            g = p_ref[0].astype(F32)
            for p in range(1, p_ref.shape[0]):
                g = g + p_ref[p].astype(F32)
        g_ref[...] = g
        d_ref[...], mo_ref[...], vo_ref[...] = _adam_update(w_ref[...], m_ref[...], v_ref[...], g)

    blk = pl.BlockSpec((rb, bc), lambda i, j: (i, j))
    out = jax.ShapeDtypeStruct((rows, cols), F32)
    if outer:
        grad_specs = [pl.BlockSpec((N_DEV, rb), lambda i, j: (0, i)), pl.BlockSpec((N_DEV, bc), lambda i, j: (0, j))]
        grad_ins = tuple(parts)
    else:
        grad_specs = [pl.BlockSpec((parts.shape[0], rb, bc), lambda i, j: (0, i, j))]
        grad_ins = (parts,)
    return _pcall(
        body, name=name, grid=(rows // rb, cols // bc), in_specs=[blk, blk, blk] + grad_specs,
        out_specs=[blk] * 4, out_shape=[out] * 4, compiler_params=_params("parallel", "parallel"),
    )(w, m, v, *grad_ins)


SMALL = ("b_ada", "norm1_g", "b_gate", "gmlp_ln_g", "gmlp_ln_b", "gmlp_ws", "gmlp_bs", "hg_lb", "hg_norm_g",
         "norm2_g", "final_norm_g")
BIG = ("w_in", "w_branch_gmlp", "w_branch_hg", "w_out", "w_ffn_in", "w_ffn_out")
WEIGHTS = ("w_ada", "b_ada", "norm1_g", "w_in", "b_gate", "gmlp_ln_g", "gmlp_ln_b", "gmlp_ws", "gmlp_bs", "hg_lb",
           "hg_norm_g", "w_branch_gmlp", "w_branch_hg", "w_out", "norm2_g", "w_ffn_in", "w_ffn_out", "final_norm_g")


def _pack(parts):
    return jnp.concatenate([p.reshape(-1, LANES) for p in parts], axis=0)


def _step(x, c, loss_target, w, m, v):
    s, d = x.shape[1], x.shape[2]
    gw = w["gmlp_ln_g"].shape[-1]
    hw = w["hg_norm_g"].shape[-1]
    x2d, tgt = x[0], loss_target[0]
    mx, my, mc = lax.axis_index("x"), lax.axis_index("y"), lax.axis_index("c")
    chip = 2 * mx + my
    dev = 2 * chip + mc
    q_col = 2 * gw // LANES
    gate_col = (2 * gw + 4 * hw) // d
    place = jnp.stack([chip, mc]).astype(jnp.int32)
    _Order.last = None

    c_all = _all_gather8(c.reshape(-1, LANES), "gather_c").reshape(N_DEV, d)
    gather_in = _WeightGather("in", _WeightGather.zones([w["w_in"][0]], chip), ring=True)
    zones_mix = _WeightGather.zones([w[n][0] for n in ("w_branch_gmlp", "w_branch_hg", "w_out")], chip)
    zones_fi = _WeightGather.zones([w["w_ffn_in"][0]], chip)
    zones_fo = _WeightGather.zones([w["w_ffn_out"][0]], chip)
    _Order.also = (*zones_mix, *zones_fi, *zones_fo)
    n_ada = w["w_ada"].shape[-1]
    b_ada_q = lax.dynamic_slice(w["b_ada"], (0, chip * n_ada), (1, n_ada))
    mod_q = _ada_fwd(c_all, w["w_ada"][0], b_ada_q)
    mod_all = _all_gather8(mod_q, "gather_mod")
    mod = lax.dynamic_index_in_dim(mod_all, dev, axis=1, keepdims=False)[::2].reshape(1, 6 * d)
    sh1, sc1, gt1, sh2, sc2, gt2 = [mod[:, i * d:(i + 1) * d] for i in range(6)]

    gather_in.relay_and_pass_near()
    gather_mix = _WeightGather("mix", zones_mix)
    gather_fi = _WeightGather("fi", zones_fi, ring=True)

    norm1_g, norm2_g, final_g = w["norm1_g"], w["norm2_g"], w["final_norm_g"].reshape(1, d)
    ln_g, ln_b = w["gmlp_ln_g"], w["gmlp_ln_b"]
    ws = w["gmlp_ws"][0]
    groups = ws.shape[0]
    bs_t = jnp.pad(w["gmlp_bs"][0].T, ((0, 0), (0, LANES - groups)))
    hg_lb, hg_ng, b_gate = w["hg_lb"], w["hg_norm_g"], w["b_gate"]

    h1 = _norm_mod_fwd(x2d, norm1_g, sc1, sh1, "norm1_fwd")
    order = jnp.stack([chip, 2 * (1 - mx) + my, 2 * mx + (1 - my), 2 * (1 - mx) + (1 - my)]).astype(jnp.int32)
    w_in, = gather_in.near_done()
    z = _matmul_quarters(h1, w_in, order, 0, 3, "mm_z_near")
    w_in, = gather_in.pass_far()
    z = _matmul_quarters(h1, w_in, order, 3, 1, "mm_z_far", prev=z)
    gather_mix.pass_on()
    ya = _gmlp_fwd(z, ln_g, ln_b, ws, bs_t)
    yb, o_raw, states = _hgrn_fwd(z, hg_lb, hg_ng, q_col)
    gather_fi.relay()
    gather_fo = _WeightGather("fo", zones_fo)
    w_bg, w_bh, w_out = gather_mix.done()
    w_bg, w_bh = [wq.transpose(1, 0, 2).reshape(wq.shape[1], -1) for wq in (w_bg, w_bh)]
    w_out = w_out.reshape(-1, w_out.shape[-1])
    pa, pb, y = _branch_merge(ya, yb, w_bg, w_bh, z, b_gate, gate_col)
    yo = _matmul(y, w_out, mode="nn", name="mm_yo", out_dtype=F32)
    gather_fi.pass_on()
    x1, h2 = _norm_mod_fwd(x2d, norm2_g, sc2, sh2, "norm2_fwd", res=yo, gt=gt1)
    w_fi, = gather_fi.done()
    a_act, up_act, hf = _ffn_in(h2, w_fi)
    gather_fo.pass_on()
    w_fo, = gather_fo.done()
    w_fo = w_fo.reshape(-1, w_fo.shape[-1])
    ffn = _matmul(hf, w_fo, mode="nn", name="mm_ffn", out_dtype=F32)
    dx2, dffn, loss_row, d_final_g, d_gt2 = _final_loss(x1, ffn, gt2, final_g, tgt)

    g_fo = _matmul(hf, dffn, mode="tn", name="mm_g_fo", out_dtype=BF16, bm=_half_width(hf.shape[1] // 2))
    daup = tuple(_ffn_out_bwd(dffn, w_fo, a_act, up_act))
    g_fi = _matmul(h2, daup, mode="tn", name="mm_g_fi", out_dtype=BF16, out_slots=True, bn=_half_width(w_fi.shape[2]))
    red_ffn = _GradReduce("ffn", [g_fo.reshape(N_CHIPS, -1, g_fo.shape[-1]), g_fi])
    dh2 = _matmul(daup, w_fi, mode="nt", name="mm_dh2", out_dtype=F32, b_slots=True)
    red_ffn.step(place)
    dx1, dyo, d_sh2, d_sc2, d_norm2, d_gt1 = _norm_mod_bwd(dh2, x1, dx2, norm2_g, sc2, "norm2_bwd", branch=yo, gt=gt1)
    g_out = _matmul(y, dyo, mode="tn", name="mm_g_out", out_dtype=BF16)
    dpa, dpb, dz_ga, dz_gb, d_b_ga, d_b_gb = _out_bwd(dyo, w_out, pa, pb, z, b_gate, gate_col)
    g_bg, g_bh = _branch_wgrad(ya, yb, dpa, dpb)
    red_mix = _GradReduce("mix", [g_out.reshape(N_CHIPS, -1, g_out.shape[-1]), g_bg, g_bh])
    dya, dyb = _branch_bwd(dpa, dpb, w_bg, w_bh)
    red_mix.step(place)
    dz_uv, d_ws, d_bs_t, d_ln_g, d_ln_b = _gmlp_bwd(dya, z, ln_g, ln_b, ws, bs_t)
    dz_q, dz_f, dz_i, dz_g, d_hg_lb, d_hg_ng = _hgrn_bwd(dyb, z, o_raw, states, hg_lb, hg_ng, q_col)
    dz = jnp.concatenate([dz_uv, dz_q, dz_f, dz_i, dz_g, dz_ga, dz_gb], axis=1)
    grad, delta, new_m, new_v = {}, {}, {}, {}

    def update(n, parts, outer=False):
        outs = _adamw(w[n][0], m[n][0], v[n][0], parts, "adamw_" + n, outer=outer)
        grad[n], delta[n], new_m[n], new_v[n] = [o[None] for o in outs]

    g_in = _matmul(h1, dz, mode="tn", name="mm_g_in", out_dtype=BF16, out_slots=True, bn=_half_width(w_in.shape[2]))
    red_in = _GradReduce("in", [g_in])
    red_ffn.join(place)
    red_mix.join(place)
    red_in.step(place)
    g_fo, g_fi = red_ffn.done()
    n = "w_ffn_in"
    dh1, rode = _matmul(dz, w_in, mode="nt", name="mm_dh1", out_dtype=F32, b_slots=True,
                        rider=([w[n][0], m[n][0], v[n][0], g_fi], lambda *blk: (blk[3], *_adam_update(*blk)), 4))
    grad[n], delta[n], new_m[n], new_v[n] = [o[None] for o in rode]
    grad_x, d_sh1, d_sc1, d_norm1 = _norm_mod_bwd(dh1, x2d, dx1, norm1_g, sc1, "norm1_bwd")

    d_mod = jnp.concatenate([d_sh1, d_sc1, d_gt1, d_sh2, d_sc2, d_gt2], axis=1)
    small_part = {"b_ada": d_mod, "norm1_g": d_norm1, "b_gate": jnp.concatenate([d_b_ga, d_b_gb], axis=1), "gmlp_ln_g": d_ln_g, "gmlp_ln_b": d_ln_b,
                  "gmlp_ws": d_ws, "gmlp_bs": d_bs_t[:, :groups].T, "hg_lb": d_hg_lb, "hg_norm_g": d_hg_ng,
                  "norm2_g": d_norm2, "final_norm_g": d_final_g}
    loss_rows = jnp.broadcast_to(loss_row, (8, LANES))
    packed = _pack([small_part[n] for n in SMALL] + [loss_rows])
    zone = lax.dynamic_update_slice(lax.empty((N_DEV,) + packed.shape, F32), packed[None], (dev, 0, 0))
    small_sems, small_bufs = _xfer_start("gsmall_start", [zone], _plan_gather_out, 4, after_last=True)
    update("w_ffn_out", g_fo[None])
    g_out, g_bg, g_bh = red_mix.done()
    update("w_out", g_out[None])
    red_in.join(place)
    (small_sems,), small_bufs = _xfer_hand_over("gsmall_pass", small_sems, small_bufs, _plan_gather_out,
                                                [(_plan_gather_on, 3)])
    update("w_branch_gmlp", g_bg[None])
    update("w_branch_hg", g_bh[None])
    small_all, = _xfer_wait("gsmall_wait", small_sems, small_bufs, _plan_gather_on)
    d_mod_all = small_all[:, :6 * d // LANES].reshape(N_DEV, 6 * d)
    d_mod_q = lax.dynamic_slice(d_mod_all, (0, chip * n_ada), (N_DEV, n_ada))
    pad = [jnp.zeros_like(loss_rows)]
    outs = _adamw(_pack([w[n] for n in SMALL] + pad), _pack([m[n] for n in SMALL] + pad),
                  _pack([v[n] for n in SMALL] + pad), small_all, "adamw_small")
    update("w_ada", (c_all, d_mod_q), outer=True)
    update("w_in", red_in.done()[0][None])
    row = 0
    for n in SMALL:
        cnt = w[n].size // LANES
        for dst, o in zip((grad, delta, new_m, new_v), outs):
            dst[n] = o[row:row + cnt].reshape(w[n].shape)
        row += cnt

    loss = outs[0][row, 0]
    return (loss, grad_x[None], *[grad[n] for n in WEIGHTS], *[delta[n] for n in WEIGHTS],
            *[new_m[n] for n in WEIGHTS], *[new_v[n] for n in WEIGHTS])


def kernel(x, c, w_ada, b_ada, norm1_g, w_in, b_gate, gmlp_ln_g, gmlp_ln_b, gmlp_ws, gmlp_bs, hg_lb, hg_norm_g, w_branch_gmlp, w_branch_hg, w_out, norm2_g, w_ffn_in, w_ffn_out, final_norm_g, loss_target, m_w_ada, m_b_ada, m_norm1_g, m_w_in, m_b_gate, m_gmlp_ln_g, m_gmlp_ln_b, m_gmlp_ws, m_gmlp_bs, m_hg_lb, m_hg_norm_g, m_w_branch_gmlp, m_w_branch_hg, m_w_out, m_norm2_g, m_w_ffn_in, m_w_ffn_out, m_final_norm_g, v_w_ada, v_b_ada, v_norm1_g, v_w_in, v_b_gate, v_gmlp_ln_g, v_gmlp_ln_b, v_gmlp_ws, v_gmlp_bs, v_hg_lb, v_hg_norm_g, v_w_branch_gmlp, v_w_branch_hg, v_w_out, v_norm2_g, v_w_ffn_in, v_w_ffn_out, v_final_norm_g):
    w = dict(w_ada=w_ada, b_ada=b_ada, norm1_g=norm1_g, w_in=w_in, b_gate=b_gate, gmlp_ln_g=gmlp_ln_g,
             gmlp_ln_b=gmlp_ln_b, gmlp_ws=gmlp_ws, gmlp_bs=gmlp_bs, hg_lb=hg_lb, hg_norm_g=hg_norm_g,
             w_branch_gmlp=w_branch_gmlp, w_branch_hg=w_branch_hg, w_out=w_out, norm2_g=norm2_g,
             w_ffn_in=w_ffn_in, w_ffn_out=w_ffn_out, final_norm_g=final_norm_g)
    m = dict(w_ada=m_w_ada, b_ada=m_b_ada, norm1_g=m_norm1_g, w_in=m_w_in, b_gate=m_b_gate, gmlp_ln_g=m_gmlp_ln_g,
             gmlp_ln_b=m_gmlp_ln_b, gmlp_ws=m_gmlp_ws, gmlp_bs=m_gmlp_bs, hg_lb=m_hg_lb, hg_norm_g=m_hg_norm_g,
             w_branch_gmlp=m_w_branch_gmlp, w_branch_hg=m_w_branch_hg, w_out=m_w_out, norm2_g=m_norm2_g,
             w_ffn_in=m_w_ffn_in, w_ffn_out=m_w_ffn_out, final_norm_g=m_final_norm_g)
    v = dict(w_ada=v_w_ada, b_ada=v_b_ada, norm1_g=v_norm1_g, w_in=v_w_in, b_gate=v_b_gate, gmlp_ln_g=v_gmlp_ln_g,
             gmlp_ln_b=v_gmlp_ln_b, gmlp_ws=v_gmlp_ws, gmlp_bs=v_gmlp_bs, hg_lb=v_hg_lb, hg_norm_g=v_hg_norm_g,
             w_branch_gmlp=v_w_branch_gmlp, w_branch_hg=v_w_branch_hg, w_out=v_w_out, norm2_g=v_norm2_g,
             w_ffn_in=v_w_ffn_in, w_ffn_out=v_w_ffn_out, final_norm_g=v_final_norm_g)
    return _step(x, c, loss_target, w, m, v)
```

```python
import jax
import jax.numpy as jnp
from jax import lax
from jax.experimental import pallas as pl
from jax.experimental.pallas import tpu as pltpu

F32 = jnp.float32
BF16 = jnp.bfloat16
EPS = 1e-6
LANES = 128
N_CHIPS = 4
N_DEV = 8
VMEM_LIMIT_BYTES = 56 * 1024 * 1024
VMEM_RIDER_LIMIT_BYTES = 60 * 1024 * 1024
HG_CHUNK = 32
HG_ROWS = 256
HG_HEADS_PER_STEP = 8
GMLP_CHUNKS_PER_STEP = 2
EXP_CLAMP = 80.0
ADAM_LR, ADAM_B1, ADAM_B2, ADAM_EPS, ADAM_WD, ADAM_STEP = 0.001, 0.9, 0.999, 1e-08, 0.01, 10
MESH = pl.DeviceIdType.MESH

NN = (((1,), (0,)), ((), ()))
NT = (((1,), (1,)), ((), ()))
TN = (((0,), (0,)), ((), ()))


def _dot(a, b, dims=NN, precision=None):
    return lax.dot_general(a, b, dims, precision=precision, preferred_element_type=F32)


def _params(*semantics, vmem_limit_bytes=VMEM_LIMIT_BYTES):
    return pltpu.CompilerParams(dimension_semantics=semantics, vmem_limit_bytes=vmem_limit_bytes)


class _Order:
    last = None
    also = ()


def _pcall(body, *, in_specs, out_specs, grid=(), scratch_shapes=(), num_scalar_prefetch=0, **kw):
    def run(*ins):
        deps = (() if _Order.last is None else (_Order.last,)) + tuple(_Order.also)
        _Order.also = ()
        n_in, n_dep = len(ins), len(deps)

        def wrapped(*refs):
            body(*refs[:n_in], *refs[n_in + n_dep:])

        specs = list(in_specs) + [pl.BlockSpec(memory_space=pl.ANY)] * n_dep
        if num_scalar_prefetch:
            grid_spec = pltpu.PrefetchScalarGridSpec(
                num_scalar_prefetch=num_scalar_prefetch, grid=grid, in_specs=specs, out_specs=out_specs,
                scratch_shapes=scratch_shapes)
            outs = pl.pallas_call(wrapped, grid_spec=grid_spec, **kw)(*ins, *deps)
        else:
            outs = pl.pallas_call(wrapped, grid=grid, in_specs=specs, out_specs=out_specs,
                                  scratch_shapes=scratch_shapes, **kw)(*ins, *deps)
        _Order.last = jax.tree.leaves(outs)[0]
        return outs

    return run


def _pick_rows(dim, pref, mult=16):
    best = None
    for cand in range(mult, min(dim, pref) + 1, mult):
        if dim % cand == 0:
            best = cand
    assert best is not None, (dim, pref)
    return best


def _half_width(n):
    return n // 2 if n % (2 * LANES) == 0 else n


def _pick(dim, pref):
    if dim <= pref:
        return dim
    best = None
    for cand in range(LANES, pref + 1, LANES):
        if dim % cand == 0:
            best = cand
    assert best is not None, (dim, pref)
    return best


def _sigmoid(x):
    return 1.0 / (1.0 + jnp.exp(-x))


def _gelu(x):
    c = 0.7978845608028654
    return 0.5 * x * (1.0 + jnp.tanh(c * (x + 0.044715 * x * x * x)))


def _gelu_grad(x):
    c = 0.7978845608028654
    t = jnp.tanh(c * (x + 0.044715 * x * x * x))
    return 0.5 * (1.0 + t) + 0.5 * x * (1.0 - t * t) * c * (1.0 + 3.0 * 0.044715 * x * x)


def _rms(x):
    r = lax.rsqrt(jnp.mean(x * x, axis=-1, keepdims=True) + EPS)
    return x * r, r


def _colsum(x):
    return jnp.sum(x, axis=0, keepdims=True)


def _accumulate(first, ref, val):
    @pl.when(first)
    def _():
        ref[...] = val

    @pl.when(jnp.logical_not(first))
    def _():
        ref[...] += val


def _matmul(a, b, *, mode, name, out_dtype, b_slots=False, out_slots=False, bm=1024, bn=1024, bk=2816, rider=None):
    pair = a if isinstance(a, tuple) else b if isinstance(b, tuple) else None
    if mode == "nn":
        m, k = a.shape
        n = b.shape[2] * N_CHIPS if b_slots else b.shape[1]
        per = b.shape[2] if b_slots else n
    elif mode == "nt":
        m, k = (a[0].shape[0], 2 * a[0].shape[1]) if pair else a.shape
        n = b.shape[1] if b_slots else b.shape[0]
        per = b.shape[2] if b_slots else k
    else:
        k, m = a.shape
        n = 2 * b[0].shape[1] if pair else b.shape[1]
        per = n // N_CHIPS if out_slots else n
    bm = _pick(m, bm)
    if mode == "nt":
        bn, bk = _pick(n, bn), _pick(per, bk)
    else:
        bn, bk = _pick(per, bn), _pick(k, bk)
    nk = k // bk
    per_blocks = per // (bk if mode == "nt" else bn)
    dims = {"nn": NN, "nt": NT, "tn": TN}[mode]
    half = (nk if mode == "nt" else n // bn) // 2

    def product(a_ref, b_ref, o_ref, acc):
        part = _dot(a_ref[...], b_ref[...], dims)
        if nk == 1:
            o_ref[...] = part.astype(o_ref.dtype)
            return
        acc_ref, kk = acc[0], pl.program_id(2)

        @pl.when(kk == 0)
        def _():
            acc_ref[...] = part

        @pl.when(jnp.logical_and(kk > 0, kk < nk - 1))
        def _():
            acc_ref[...] += part

        @pl.when(kk == nk - 1)
        def _():
            o_ref[...] = (acc_ref[...] + part).astype(o_ref.dtype)

    n_ride, n_rode = (len(rider[0]), rider[2]) if rider else (0, 0)

    def body(*refs):
        if rider:
            ride_in, ride_out = refs[2:2 + n_ride], refs[3 + n_ride:3 + n_ride + n_rode]

            def ride(t, carry):
                rows = pl.ds(pl.multiple_of(t * 8, 8), 8)
                for ref, val in zip(ride_out, rider[1](*[r[rows, :] for r in ride_in])):
                    ref[rows, :] = val
                return carry

            lax.fori_loop(0, ride_in[0].shape[0] // 8, ride, 0)
        if not pair:
            return product(refs[0], refs[1], refs[2 + n_ride], refs[3 + n_ride + n_rode:])
        first = pl.program_id(2 if mode == "nt" else 1) < half
        x1, x2, y, o_ref, acc = refs[0], refs[1], refs[2], refs[3], refs[4:]

        @pl.when(first)
        def _():
            product(*((x1, y) if mode == "nt" else (y, x1)), o_ref, acc)

        @pl.when(jnp.logical_not(first))
        def _():
            product(*((x2, y) if mode == "nt" else (y, x2)), o_ref, acc)

    if mode == "nn":
        a_spec = pl.BlockSpec((bm, bk), lambda i, j, kk: (i, kk))
        if b_slots:
            b_spec = pl.BlockSpec((None, bk, bn), lambda i, j, kk: (j // per_blocks, kk, j % per_blocks))
        else:
            b_spec = pl.BlockSpec((bk, bn), lambda i, j, kk: (kk, j))
    elif mode == "nt":
        a_spec = pl.BlockSpec((bm, bk), lambda i, j, kk: (i, kk))
        if b_slots:
            b_spec = pl.BlockSpec((None, bn, bk), lambda i, j, kk: (kk // per_blocks, j, kk % per_blocks))
        else:
            b_spec = pl.BlockSpec((bn, bk), lambda i, j, kk: (j, kk))
    else:
        a_spec = pl.BlockSpec((bk, bm), lambda i, j, kk: (kk, i))
        b_spec = pl.BlockSpec((bk, bn), lambda i, j, kk: (kk, j))
    if out_slots:
        o_spec = pl.BlockSpec((None, bm, bn), lambda i, j, kk: (j // per_blocks, i, j % per_blocks))
        out_shape = jax.ShapeDtypeStruct((N_CHIPS, m, per), out_dtype)
    else:
        o_spec = pl.BlockSpec((bm, bn), lambda i, j, kk: (i, j))
        out_shape = jax.ShapeDtypeStruct((m, n), out_dtype)
    if not pair:
        ins, in_specs = (a, b), [a_spec, b_spec]
    elif mode == "nt":
        ins = (*a, b)
        in_specs = [pl.BlockSpec((bm, bk), lambda i, j, kk: (i, jnp.minimum(kk, half - 1))),
                    pl.BlockSpec((bm, bk), lambda i, j, kk: (i, jnp.maximum(kk - half, 0))), b_spec]
    else:
        ins = (*b, a)
        in_specs = [pl.BlockSpec((bk, bn), lambda i, j, kk: (kk, jnp.minimum(j, half - 1))),
                    pl.BlockSpec((bk, bn), lambda i, j, kk: (kk, jnp.maximum(j - half, 0))), a_spec]
    scratch = [pltpu.VMEM((bm, bn), F32)] if nk > 1 else []
    if not rider:
        return _pcall(
            body, name=name, grid=(m // bm, n // bn, nk), in_specs=in_specs, out_specs=o_spec, out_shape=out_shape,
            scratch_shapes=scratch, compiler_params=_params("parallel", "parallel", "arbitrary"),
        )(*ins)
    assert not pair
    rows, cols = rider[0][0].shape
    nj = n // bn
    rb = rows // ((m // bm) * nj * nk)
    assert rows == rb * (m // bm) * nj * nk and rb % 8 == 0, (rows, rb)
    ride_spec = pl.BlockSpec((rb, cols), lambda i, j, kk: ((i * nj + j) * nk + kk, 0))
    outs = _pcall(
        body, name=name, grid=(m // bm, nj, nk), in_specs=in_specs + [ride_spec] * n_ride,
        out_specs=[o_spec] + [ride_spec] * n_rode,
        out_shape=[out_shape] + [jax.ShapeDtypeStruct((rows, cols), F32)] * n_rode,
        scratch_shapes=scratch,
        compiler_params=_params("arbitrary", "arbitrary", "arbitrary", vmem_limit_bytes=VMEM_RIDER_LIMIT_BYTES),
    )(*ins, *rider[0])
    return outs[0], outs[1:]


def _matmul_quarters(h, w_slots, order, first, count, name, prev=None, bm=1024):
    s, k = h.shape
    n = w_slots.shape[2]
    bm, bn = _pick(s, bm), _half_width(n)
    pb = n // bn

    def body(order_ref, h_ref, w_ref, *rest):
        rest[-1][...] = _dot(h_ref[...], w_ref[...])

    ins = (order, h, w_slots) + (() if prev is None else (prev,))
    return _pcall(
        body, name=name, num_scalar_prefetch=1, grid=(count * pb, s // bm),
        in_specs=[pl.BlockSpec((bm, k), lambda j, i, o: (i, 0)),
                  pl.BlockSpec((None, k, bn), lambda j, i, o: (o[first + j // pb], 0, j % pb))]
        + ([] if prev is None else [ANY_SPEC]),
        out_specs=pl.BlockSpec((bm, bn), lambda j, i, o: (i, o[first + j // pb] * pb + j % pb)),
        out_shape=jax.ShapeDtypeStruct((s, N_CHIPS * n), F32),
        input_output_aliases={} if prev is None else {3: 0},
        compiler_params=_params("arbitrary", "arbitrary"),
    )(*ins)


def _place():
    x, y, c = lax.axis_index("x"), lax.axis_index("y"), lax.axis_index("c")
    chips = [(1 - x, y), (x, 1 - y), (1 - x, 1 - y)]
    return x, y, c, chips


def _all_gather8(block, name):
    def body(x_ref, out_ref, send_sems, recv_sems, local_sem):
        x, y, c, chips = _place()
        me, sibling = (x, y, c), (x, y, 1 - c)

        def slot(px, py, pc):
            return out_ref.at[4 * px + 2 * py + pc]

        def copy(k, blk, to, src=None):
            return pltpu.make_async_remote_copy(
                src_ref=slot(*blk) if src is None else src, dst_ref=slot(*blk),
                send_sem=send_sems.at[k], recv_sem=recv_sems.at[k], device_id=to, device_id_type=MESH)

        mine = pltpu.make_async_copy(x_ref, slot(*me), local_sem)
        mine.start()
        first = [copy(0, me, sibling, src=x_ref)]
        first += [copy(1 + j, me, (*chip, c), src=x_ref) for j, chip in enumerate(chips)]
        for cp in first:
            cp.start()
        passed = [copy(4 + j, (*chip, c), sibling) for j, chip in enumerate(chips)]
        for j, chip in enumerate(chips):
            copy(1 + j, (*chip, c), me).wait_recv()
            passed[j].start()
        copy(0, sibling, me).wait_recv()
        for j, chip in enumerate(chips):
            copy(4 + j, (*chip, 1 - c), me).wait_recv()
        for cp in first + passed:
            cp.wait_send()
        mine.wait()

    return _pcall(
        body, name=name, out_shape=jax.ShapeDtypeStruct((N_DEV,) + block.shape, block.dtype),
        in_specs=[pl.BlockSpec(memory_space=pltpu.VMEM)], out_specs=pl.BlockSpec(memory_space=pltpu.VMEM),
        scratch_shapes=[pltpu.SemaphoreType.DMA((7,)), pltpu.SemaphoreType.DMA((7,)), pltpu.SemaphoreType.DMA],
        compiler_params=pltpu.CompilerParams(vmem_limit_bytes=VMEM_LIMIT_BYTES),
    )(block)


HBM_SPEC = pl.BlockSpec(memory_space=pltpu.HBM)
SEM_SPEC = pl.BlockSpec(memory_space=pltpu.SEMAPHORE)
ANY_SPEC = pl.BlockSpec(memory_space=pl.ANY)
EFFECT = pltpu.SideEffectType.DATAFLOW_SIDE_EFFECTING


def _xfer_start(name, bufs, plan, n_copies, after_last=False):
    nb = len(bufs)
    deps = (_Order.last,) if after_last and _Order.last is not None else ()
    nd = len(deps)

    def body(*refs):
        send_sems, recv_sems = refs[nb + nd], refs[nb + nd + 1]
        token = refs[nb + nd + 2 + nb]
        for k, (src, dst, dev) in enumerate(plan(refs[:nb], *_place())):
            pltpu.make_async_remote_copy(src_ref=src, dst_ref=dst, send_sem=send_sems.at[k], recv_sem=recv_sems.at[k],
                                         device_id=dev, device_id_type=MESH).start()
        token[...] = jnp.zeros_like(token)

    outs = pl.pallas_call(
        body, name=name,
        out_shape=(pltpu.SemaphoreType.DMA((n_copies,)), pltpu.SemaphoreType.DMA((n_copies,)),
                   *[pltpu.HBM(b.shape, b.dtype) for b in bufs], jax.ShapeDtypeStruct((8, LANES), F32)),
        in_specs=[HBM_SPEC] * nb + [ANY_SPEC] * nd,
        out_specs=(SEM_SPEC, SEM_SPEC, *[HBM_SPEC] * nb, pl.BlockSpec(memory_space=pltpu.VMEM)),
        input_output_aliases={i: 2 + i for i in range(nb)},
        compiler_params=pltpu.CompilerParams(has_side_effects=EFFECT),
    )(*[pltpu.with_memory_space_constraint(b, pltpu.HBM) for b in bufs], *deps)
    _Order.last = outs[-1]
    return (outs[0], outs[1]), list(outs[2:2 + nb])


def _xfer_wait(name, sems, bufs, plan):
    nb = len(bufs)

    def body(*refs):
        send_sems, recv_sems = refs[nb], refs[nb + 1]
        for k, (src, dst, dev) in enumerate(plan(refs[:nb], *_place())):
            copy = pltpu.make_async_remote_copy(src_ref=src, dst_ref=dst, send_sem=send_sems.at[k],
                                                recv_sem=recv_sems.at[k], device_id=dev, device_id_type=MESH)
            copy.wait_send()
            copy.wait_recv()

    outs = pl.pallas_call(
        body, name=name, out_shape=tuple(pltpu.HBM(b.shape, b.dtype) for b in bufs),
        in_specs=[HBM_SPEC] * nb + [SEM_SPEC, SEM_SPEC, ANY_SPEC], out_specs=tuple([HBM_SPEC] * nb),
        input_output_aliases={i: i for i in range(nb)},
        compiler_params=pltpu.CompilerParams(has_side_effects=EFFECT),
    )(*bufs, *sems, _Order.last)
    _Order.last = outs[0]
    return list(outs)


def _xfer_hand_over(name, sems, bufs, plan, next_plans):
    nb, n_next = len(bufs), len(next_plans)

    def body(*refs):
        send_sems, recv_sems = refs[nb], refs[nb + 1]
        outs = refs[nb + 3:]
        place = _place()
        for k, (src, dst, dev) in enumerate(plan(refs[:nb], *place)):
            copy = pltpu.make_async_remote_copy(src_ref=src, dst_ref=dst, send_sem=send_sems.at[k],
                                                recv_sem=recv_sems.at[k], device_id=dev, device_id_type=MESH)
            copy.wait_send()
            copy.wait_recv()
        for p, (next_plan, _) in enumerate(next_plans):
            for k, (src, dst, dev) in enumerate(next_plan(refs[:nb], *place)):
                pltpu.make_async_remote_copy(src_ref=src, dst_ref=dst, send_sem=outs[2 * p].at[k],
                                             recv_sem=outs[2 * p + 1].at[k], device_id=dev, device_id_type=MESH).start()
        outs[-1][...] = jnp.zeros_like(outs[-1])

    sem_shapes = [pltpu.SemaphoreType.DMA((copies,)) for _, copies in next_plans for _ in range(2)]
    outs = pl.pallas_call(
        body, name=name,
        out_shape=(*sem_shapes, *[pltpu.HBM(b.shape, b.dtype) for b in bufs], jax.ShapeDtypeStruct((8, LANES), F32)),
        in_specs=[HBM_SPEC] * nb + [SEM_SPEC, SEM_SPEC, ANY_SPEC],
        out_specs=(*[SEM_SPEC] * (2 * n_next), *[HBM_SPEC] * nb, pl.BlockSpec(memory_space=pltpu.VMEM)),
        input_output_aliases={i: 2 * n_next + i for i in range(nb)},
        compiler_params=pltpu.CompilerParams(has_side_effects=EFFECT),
    )(*bufs, *sems, _Order.last)
    _Order.last = outs[-1]
    return [(outs[2 * p], outs[2 * p + 1]) for p in range(n_next)], list(outs[2 * n_next:2 * n_next + nb])


def _half(ref, c, axis):
    rows = ref.shape[axis] // 2
    return pl.ds(c * rows, rows)


def _plan_weights_ici(n):
    def plan(refs, x, y, c, chips):
        out = []
        for w in range(n):
            region = refs[w].at[2 * x + y, _half(refs[w], c, 1), :]
            out += [(region, region, (*chip, c)) for chip in chips]
        return out
    return plan


def _plan_weights_ring(n):
    def plan(refs, x, y, c, chips):
        out = []
        for w in range(n):
            region = refs[w].at[2 * x + y, _half(refs[w], c, 1), :]
            out += [(region, region, (*chip, c)) for chip in chips[:2]]
        return out
    return plan


def _plan_weights_relay(n):
    def plan(refs, x, y, c, chips):
        out = []
        for w in range(n):
            quarter_rows = refs[w].shape[1] // 4
            upper = refs[w].at[2 * x + (1 - y), pl.ds(2 * c * quarter_rows, quarter_rows), :]
            lower = refs[w].at[2 * (1 - x) + y, pl.ds((2 * c + 1) * quarter_rows, quarter_rows), :]
            out += [(upper, upper, (1 - x, y, c)), (lower, lower, (x, 1 - y, c))]
        return out
    return plan


def _plan_weights_d2d(n, which=slice(0, 3)):
    def plan(refs, x, y, c, chips):
        out = []
        for w in range(n):
            rows = _half(refs[w], c, 1)
            for chip in chips[which]:
                region = refs[w].at[2 * chip[0] + chip[1], rows, :]
                out.append((region, region, (x, y, 1 - c)))
        return out
    return plan


def _plan_gather_out(refs, x, y, c, chips):
    mine = refs[0].at[4 * x + 2 * y + c]
    return [(mine, mine, (x, y, 1 - c))] + [(mine, mine, (*chip, c)) for chip in chips]


def _plan_gather_on(refs, x, y, c, chips):
    out = []
    for chip in chips:
        block = refs[0].at[4 * chip[0] + 2 * chip[1] + c]
        out.append((block, block, (x, y, 1 - c)))
    return out


def _plan_grads_d2d(n):
    def plan(refs, x, y, c, chips):
        return [(refs[w].at[:, _half(refs[w], 1 - c, 1), :], refs[n + w], (x, y, 1 - c)) for w in range(n)]
    return plan


def _plan_grads_ici(n):
    def plan(refs, x, y, c, chips):
        out = []
        for w in range(n):
            out += [(refs[w].at[2 * chip[0] + chip[1]], refs[n + w].at[2 * x + y], (*chip, c)) for chip in chips]
        return out
    return plan


def _plan_final_d2d(n):
    def plan(refs, x, y, c, chips):
        out = []
        for w in range(n):
            region = refs[w].at[_half(refs[w], c, 0), :]
            out.append((region, region, (x, y, 1 - c)))
        return out
    return plan


def _stream_blocks(hr, cols):
    bc = cols if cols <= 4096 else _pick(cols, 4096)
    return _pick_rows(hr, max(16, (768 * 1024) // bc)), bc


def _pre_reduce(g, landed, place, name):
    _, rows, cols = g.shape
    hr = rows // 2
    rb, bc = _stream_blocks(hr, cols)
    nrb = hr // rb

    def body(place_ref, g_ref, l_ref, o_ref):
        o_ref[...] = (g_ref[...].astype(F32) + l_ref[...].astype(F32)).astype(o_ref.dtype)

    return _pcall(
        body, name=name, num_scalar_prefetch=1, grid=(N_CHIPS, nrb, cols // bc),
        in_specs=[pl.BlockSpec((None, rb, bc), lambda j, i, k, p: (j, p[1] * nrb + i, k)),
                  pl.BlockSpec((None, rb, bc), lambda j, i, k, p: (j, i, k))],
        out_specs=pl.BlockSpec((None, rb, bc), lambda j, i, k, p: (j, i, k)),
        out_shape=jax.ShapeDtypeStruct((N_CHIPS, hr, cols), g.dtype),
        compiler_params=_params("parallel", "parallel", "parallel"),
    )(place, g, landed)


def _sum_slots(mine, landed, place, name):
    _, hr, cols = mine.shape
    rb, bc = _stream_blocks(hr, cols)
    rb = _pick_rows(hr, max(16, rb // 2))
    nrb = hr // rb

    def body(place_ref, m_ref, l_ref, o_ref):
        chip = place_ref[0]
        own = m_ref[...].astype(F32)
        total = jnp.where(chip == 0, own, l_ref[0].astype(F32))
        for j in range(1, N_CHIPS):
            total = total + jnp.where(chip == j, own, l_ref[j].astype(F32))
        o_ref[...] = total

    return _pcall(
        body, name=name, num_scalar_prefetch=1, grid=(nrb, cols // bc),
        in_specs=[pl.BlockSpec((None, rb, bc), lambda i, k, p: (p[0], i, k)),
                  pl.BlockSpec((N_CHIPS, rb, bc), lambda i, k, p: (0, i, k))],
        out_specs=pl.BlockSpec((rb, bc), lambda i, k, p: (p[1] * nrb + i, k)),
        out_shape=jax.ShapeDtypeStruct((2 * hr, cols), F32),
        compiler_params=_params("parallel", "parallel"),
    )(place, mine, landed)


class _WeightGather:
    @staticmethod
    def zones(quarters, chip):
        return [lax.dynamic_update_slice(lax.empty((N_CHIPS,) + q.shape, BF16), q.astype(BF16)[None], (chip, 0, 0))
                for q in quarters]

    def __init__(self, tag, zones, ring=False):
        self.tag, self.n, self.ring = tag, len(zones), ring
        self.plan = _plan_weights_ring(self.n) if ring else _plan_weights_ici(self.n)
        self.sems, self.bufs = _xfer_start("wici_start_" + tag, zones, self.plan, (2 if ring else 3) * self.n,
                                           after_last=True)

    def relay(self):
        plan = _plan_weights_relay(self.n)
        (self.sems,), self.bufs = _xfer_hand_over("wrel_start_" + self.tag, self.sems, self.bufs, self.plan,
                                                   [(plan, 2 * self.n)])
        self.plan = plan

    def pass_on(self):
        plan = _plan_weights_d2d(self.n)
        (self.sems,), self.bufs = _xfer_hand_over("wd2d_start_" + self.tag, self.sems, self.bufs, self.plan,
                                                   [(plan, 3 * self.n)])
        self.plan = plan

    def done(self):
        return _xfer_wait("wd2d_wait_" + self.tag, self.sems, self.bufs, self.plan)

    def relay_and_pass_near(self):
        plan, self.near_plan = _plan_weights_relay(self.n), _plan_weights_d2d(self.n, slice(0, 2))
        (self.sems, self.near_sems), self.bufs = _xfer_hand_over(
            "wrel_start_" + self.tag, self.sems, self.bufs, self.plan, [(plan, 2 * self.n), (self.near_plan, 2 * self.n)])
        self.plan = plan

    def near_done(self):
        self.bufs = _xfer_wait("wnear_wait_" + self.tag, self.near_sems, self.bufs, self.near_plan)
        return self.bufs

    def pass_far(self):
        plan = _plan_weights_d2d(self.n, slice(2, 3))
        (sems,), bufs = _xfer_hand_over("wfar_start_" + self.tag, self.sems, self.bufs, self.plan, [(plan, self.n)])
        return _xfer_wait("wfar_wait_" + self.tag, sems, bufs, plan)


class _GradReduce:
    def __init__(self, tag, grads):
        self.tag, self.n = tag, len(grads)
        zones = [lax.empty((N_CHIPS, g.shape[1] // 2, g.shape[2]), g.dtype) for g in grads]
        self.plan = _plan_grads_d2d(self.n)
        self.sems, self.bufs = _xfer_start("gd2d_start_" + tag, list(grads) + zones, self.plan, self.n)

    def pair(self, place):
        n = self.n
        bufs = _xfer_wait("gd2d_wait_" + self.tag, self.sems, self.bufs, self.plan)
        self.halves = [_pre_reduce(bufs[w], bufs[n + w], place, f"pre_reduce_{self.tag}{w}") for w in range(n)]

    def cross(self, after_last=False):
        zones = [lax.empty(h.shape, h.dtype) for h in self.halves]
        self.plan = _plan_grads_ici(self.n)
        self.sems, self.bufs = _xfer_start("gici_start_" + self.tag, self.halves + zones, self.plan, 3 * self.n,
                                           after_last=after_last)

    def step(self, place):
        self.pair(place)
        self.cross()

    def join(self, place):
        n = self.n
        bufs = _xfer_wait("gici_wait_" + self.tag, self.sems, self.bufs, self.plan)
        sums = [_sum_slots(bufs[w], bufs[n + w], place, f"sum_slots_{self.tag}{w}") for w in range(n)]
        self.plan = _plan_final_d2d(n)
        self.sems, self.bufs = _xfer_start("gfin_start_" + self.tag, sums, self.plan, n)

    def done(self):
        return _xfer_wait("gfin_wait_" + self.tag, self.sems, self.bufs, self.plan)


def _ada_fwd(c_all, w_q, b_q):
    d, n = w_q.shape
    bn = _pick(n, 512)

    def body(c_ref, w_ref, b_ref, o_ref):
        cv = c_ref[...]
        act = cv * _sigmoid(cv)
        o_ref[...] = _dot(act, w_ref[...], NN, lax.Precision.HIGHEST) + b_ref[...]

    return _pcall(
        body, name="ada_fwd", grid=(n // bn,),
        in_specs=[pl.BlockSpec((N_DEV, d), lambda j: (0, 0)), pl.BlockSpec((d, bn), lambda j: (0, j)),
                  pl.BlockSpec((1, bn), lambda j: (0, j))],
        out_specs=pl.BlockSpec((N_DEV, bn), lambda j: (0, j)),
        out_shape=jax.ShapeDtypeStruct((N_DEV, n), F32), compiler_params=_params("parallel"),
    )(c_all, w_q, b_q)


def _row_spec(rb, width, col=0):
    return pl.BlockSpec((rb, width), lambda i, col=col: (i, col))


def _vec_spec(width, col=0):
    return pl.BlockSpec((1, width), lambda i, col=col: (0, col))


def _norm_mod_fwd(x, g, sc, sh, name, res=None, gt=None):
    s, d = x.shape
    rb = _pick(s, 256)
    has_res = res is not None

    def body(*refs):
        if has_res:
            x_ref, res_ref, gt_ref, g_ref, sc_ref, sh_ref, x1_ref, h_ref = refs
            xv = x_ref[...] + gt_ref[...] * res_ref[...]
            x1_ref[...] = xv
        else:
            x_ref, g_ref, sc_ref, sh_ref, h_ref = refs
            xv = x_ref[...]
        xh, _ = _rms(xv)
        h_ref[...] = (xh * g_ref[...] * (1.0 + sc_ref[...]) + sh_ref[...]).astype(BF16)

    row, vec = _row_spec(rb, d), _vec_spec(d)
    if has_res:
        ins, in_specs = (x, res, gt, g, sc, sh), [row, row, vec, vec, vec, vec]
        out_shape = [jax.ShapeDtypeStruct((s, d), F32), jax.ShapeDtypeStruct((s, d), BF16)]
        out_specs = [row, row]
    else:
        ins, in_specs = (x, g, sc, sh), [row, vec, vec, vec]
        out_shape, out_specs = jax.ShapeDtypeStruct((s, d), BF16), row
    return _pcall(body, name=name, grid=(s // rb,), in_specs=in_specs, out_specs=out_specs,
                          out_shape=out_shape, compiler_params=_params("parallel"))(*ins)


def _final_loss(x1, f, gt2, g_final, target):
    s, d = x1.shape
    rb = _pick(s, 256)

    def body(x1_ref, f_ref, gt_ref, g_ref, t_ref, dx_ref, df_ref, loss_ref, dg_ref, dgt_ref):
        first = pl.program_id(0) == 0
        fv, gt, gv = f_ref[...], gt_ref[...], g_ref[...]
        x2 = x1_ref[...] + gt * fv
        xh, r = _rms(x2)
        err = xh * gv - t_ref[...]
        blk = 0.5 * jnp.sum(jnp.sum(err * err, axis=1, keepdims=True), axis=0, keepdims=True) / d
        dy = err / d
        dxh = dy * gv
        dx = r * (dxh - xh * jnp.mean(dxh * xh, axis=-1, keepdims=True))
        dx_ref[...] = dx
        df_ref[...] = (dx * gt).astype(BF16)
        _accumulate(first, loss_ref, jnp.broadcast_to(blk, (1, LANES)))
        _accumulate(first, dg_ref, _colsum(dy * xh))
        _accumulate(first, dgt_ref, _colsum(dx * fv))

    row, vec = _row_spec(rb, d), _vec_spec(d)
    return _pcall(
        body, name="final_loss", grid=(s // rb,), in_specs=[row, row, vec, vec, row],
        out_specs=[row, row, _vec_spec(LANES), vec, vec],
        out_shape=[jax.ShapeDtypeStruct((s, d), F32), jax.ShapeDtypeStruct((s, d), BF16),
                   jax.ShapeDtypeStruct((1, LANES), F32), jax.ShapeDtypeStruct((1, d), F32),
                   jax.ShapeDtypeStruct((1, d), F32)],
        compiler_params=_params("arbitrary"),
    )(x1, f, gt2, g_final, target)


def _norm_mod_bwd(dh, xin, dres, g, sc, name, branch=None, gt=None):
    s, d = xin.shape
    rb = _pick(s, 256)
    has_branch = branch is not None

    def body(*refs):
        if has_branch:
            dh_ref, x_ref, dres_ref, g_ref, sc_ref, br_ref, gt_ref, dx_ref, dbr_ref, dsh_ref, dsc_ref, dg_ref, dgt_ref = refs
        else:
            dh_ref, x_ref, dres_ref, g_ref, sc_ref, dx_ref, dsh_ref, dsc_ref, dg_ref = refs
        first = pl.program_id(0) == 0
        gv = g_ref[...]
        xh, r = _rms(x_ref[...])
        dhv = dh_ref[...]
        dn = dhv * (1.0 + sc_ref[...])
        dxh = dn * gv
        dx = dres_ref[...] + r * (dxh - xh * jnp.mean(dxh * xh, axis=-1, keepdims=True))
        dx_ref[...] = dx
        _accumulate(first, dsh_ref, _colsum(dhv))
        _accumulate(first, dsc_ref, _colsum(dhv * xh * gv))
        _accumulate(first, dg_ref, _colsum(dn * xh))
        if has_branch:
            dbr_ref[...] = (dx * gt_ref[...]).astype(BF16)
            _accumulate(first, dgt_ref, _colsum(dx * br_ref[...]))

    row, vec = _row_spec(rb, d), _vec_spec(d)
    vec_shape = jax.ShapeDtypeStruct((1, d), F32)
    if has_branch:
        ins, in_specs = (dh, xin, dres, g, sc, branch, gt), [row, row, row, vec, vec, row, vec]
        out_specs = [row, row, vec, vec, vec, vec]
        out_shape = [jax.ShapeDtypeStruct((s, d), F32), jax.ShapeDtypeStruct((s, d), BF16)] + [vec_shape] * 4
    else:
        ins, in_specs = (dh, xin, dres, g, sc), [row, row, row, vec, vec]
        out_specs = [row, vec, vec, vec]
        out_shape = [jax.ShapeDtypeStruct((s, d), F32)] + [vec_shape] * 3
    return _pcall(body, name=name, grid=(s // rb,), in_specs=in_specs, out_specs=out_specs,
                          out_shape=out_shape, compiler_params=_params("arbitrary"))(*ins)


def _branch_merge(ya, yb, w_bg, w_bh, z, b_gate, gate_col, bm=512, bn=1024):
    s, gw = ya.shape
    d = w_bg.shape[1]
    bm, bn = _pick(s, bm), _pick(d, bn)
    nj = d // bn

    def body(ya_ref, yb_ref, wa_ref, wb_ref, za_ref, zb_ref, ba_ref, bb_ref, pa_ref, pb_ref, y_ref):
        pa = _dot(ya_ref[...], wa_ref[...])
        pb = _dot(yb_ref[...], wb_ref[...])
        pa_ref[...] = pa.astype(BF16)
        pb_ref[...] = pb.astype(BF16)
        ga = _sigmoid(za_ref[...] + ba_ref[...])
        gb = _sigmoid(zb_ref[...] + bb_ref[...])
        y_ref[...] = (ga * pa + gb * pb).astype(BF16)

    act = pl.BlockSpec((bm, gw), lambda j, i: (i, 0))
    wgt = pl.BlockSpec((gw, bn), lambda j, i: (0, j))
    tile = pl.BlockSpec((bm, bn), lambda j, i: (i, j))
    shape = jax.ShapeDtypeStruct((s, d), BF16)
    return _pcall(
        body, name="branch_merge", grid=(nj, s // bm),
        in_specs=[act, act, wgt, wgt,
                  pl.BlockSpec((bm, bn), lambda j, i: (i, gate_col * nj + j)),
                  pl.BlockSpec((bm, bn), lambda j, i: (i, (gate_col + 1) * nj + j)),
                  pl.BlockSpec((1, bn), lambda j, i: (0, j)), pl.BlockSpec((1, bn), lambda j, i: (0, nj + j))],
        out_specs=[tile, tile, tile], out_shape=[shape, shape, shape], compiler_params=_params("parallel", "parallel"),
    )(ya, yb, w_bg, w_bh, z, z, b_gate, b_gate)


def _branch_wgrad(ya, yb, dpa, dpb):
    s, gw = ya.shape
    per = dpa.shape[1] // N_CHIPS

    def body(ya_ref, yb_ref, da_ref, db_ref, ga_ref, gb_ref):
        ga_ref[...] = _dot(ya_ref[...], da_ref[...], TN).astype(BF16)
        gb_ref[...] = _dot(yb_ref[...], db_ref[...], TN).astype(BF16)

    act = pl.BlockSpec((s, gw), lambda j: (0, 0))
    grd = pl.BlockSpec((s, per), lambda j: (0, j))
    out = pl.BlockSpec((None, gw, per), lambda j: (j, 0, 0))
    shape = jax.ShapeDtypeStruct((N_CHIPS, gw, per), BF16)
    return _pcall(body, name="branch_wgrad", grid=(N_CHIPS,), in_specs=[act, act, grd, grd], out_specs=[out, out],
                  out_shape=[shape, shape], compiler_params=_params("parallel"))(ya, yb, dpa, dpb)


def _branch_bwd(dpa, dpb, w_bg, w_bh, bm=512):
    s, d = dpa.shape
    gw = w_bg.shape[0]
    bm = _pick(s, bm)

    def body(da_ref, db_ref, wa_ref, wb_ref, oa_ref, ob_ref):
        oa_ref[...] = _dot(da_ref[...], wa_ref[...], NT)
        ob_ref[...] = _dot(db_ref[...], wb_ref[...], NT)

    act = pl.BlockSpec((bm, d), lambda i: (i, 0))
    wgt = pl.BlockSpec((gw, d), lambda i: (0, 0))
    out = pl.BlockSpec((bm, gw), lambda i: (i, 0))
    shape = jax.ShapeDtypeStruct((s, gw), F32)
    return _pcall(body, name="branch_bwd", grid=(s // bm,), in_specs=[act, act, wgt, wgt], out_specs=[out, out],
                  out_shape=[shape, shape], compiler_params=_params("parallel"))(dpa, dpb, w_bg, w_bh)


def _out_bwd(dyo, w_out, pa, pb, z, b_gate, gate_col, bm=512, bn=1024):
    s, d = dyo.shape
    bm, bn = _pick(s, bm), _pick(d, bn)
    nj = d // bn

    def body(dyo_ref, w_ref, pa_ref, pb_ref, za_ref, zb_ref, ba_ref, bb_ref,
             dpa_ref, dpb_ref, dza_ref, dzb_ref, dba_ref, dbb_ref):
        first = pl.program_id(1) == 0
        dyv = _dot(dyo_ref[...], w_ref[...], NT)
        ga = _sigmoid(za_ref[...] + ba_ref[...])
        gb = _sigmoid(zb_ref[...] + bb_ref[...])
        dpa_ref[...] = (dyv * ga).astype(BF16)
        dpb_ref[...] = (dyv * gb).astype(BF16)
        dga = dyv * pa_ref[...].astype(F32) * ga * (1.0 - ga)
        dgb = dyv * pb_ref[...].astype(F32) * gb * (1.0 - gb)
        dza_ref[...] = dga.astype(BF16)
        dzb_ref[...] = dgb.astype(BF16)
        _accumulate(first, dba_ref, _colsum(dga))
        _accumulate(first, dbb_ref, _colsum(dgb))

    tile = pl.BlockSpec((bm, bn), lambda j, i: (i, j))
    vec = pl.BlockSpec((1, bn), lambda j, i: (0, j))
    act, vec_shape = jax.ShapeDtypeStruct((s, d), BF16), jax.ShapeDtypeStruct((1, d), F32)
    return _pcall(
        body, name="out_bwd", grid=(nj, s // bm),
        in_specs=[pl.BlockSpec((bm, d), lambda j, i: (i, 0)), pl.BlockSpec((bn, d), lambda j, i: (j, 0)), tile, tile,
                  pl.BlockSpec((bm, bn), lambda j, i: (i, gate_col * nj + j)),
                  pl.BlockSpec((bm, bn), lambda j, i: (i, (gate_col + 1) * nj + j)),
                  vec, pl.BlockSpec((1, bn), lambda j, i: (0, nj + j))],
        out_specs=[tile, tile, tile, tile, vec, vec], out_shape=[act, act, act, act, vec_shape, vec_shape],
        compiler_params=_params("parallel", "arbitrary"),
    )(dyo, w_out, pa, pb, z, z, b_gate, b_gate)


def _ffn_in(h, w_fi, bm=512):
    s, d = h.shape
    per = w_fi.shape[2]
    ff = 2 * per
    bm, bn = _pick(s, bm), _half_width(per)
    per_blocks = per // bn

    def body(h_ref, wa_ref, wu_ref, a_ref, u_ref, hf_ref):
        hv = h_ref[...]
        a = _dot(hv, wa_ref[...])
        up = _dot(hv, wu_ref[...])
        a_ref[...] = a.astype(BF16)
        u_ref[...] = up.astype(BF16)
        hf_ref[...] = (a * _sigmoid(a) * up).astype(BF16)

    out = pl.BlockSpec((bm, bn), lambda j, i: (i, j))
    shape = jax.ShapeDtypeStruct((s, ff), BF16)
    return _pcall(
        body, name="ffn_in", grid=(ff // bn, s // bm),
        in_specs=[pl.BlockSpec((bm, d), lambda j, i: (i, 0)),
                  pl.BlockSpec((None, d, bn), lambda j, i: (j // per_blocks, 0, j % per_blocks)),
                  pl.BlockSpec((None, d, bn), lambda j, i: (2 + j // per_blocks, 0, j % per_blocks))],
        out_specs=[out, out, out], out_shape=[shape, shape, shape],
        compiler_params=_params("parallel", "parallel"),
    )(h, w_fi, w_fi)


def _ffn_out_bwd(dffn, w_fo, a_act, up_act, bm=512):
    s, d = dffn.shape
    ff = w_fo.shape[0]
    bm, bn = _pick(s, bm), _half_width(ff // 2)

    def body(d_ref, w_ref, a_ref, u_ref, da_ref, du_ref):
        dhf = _dot(d_ref[...], w_ref[...], NT)
        a = a_ref[...].astype(F32)
        sa = _sigmoid(a)
        da_ref[...] = (dhf * u_ref[...].astype(F32) * sa * (1.0 + a * (1.0 - sa))).astype(BF16)
        du_ref[...] = (dhf * a * sa).astype(BF16)

    tile = pl.BlockSpec((bm, bn), lambda j, i: (i, j))
    shape = jax.ShapeDtypeStruct((s, ff), BF16)
    return _pcall(
        body, name="ffn_out_bwd", grid=(ff // bn, s // bm),
        in_specs=[pl.BlockSpec((bm, d), lambda j, i: (i, 0)), pl.BlockSpec((bn, d), lambda j, i: (j, 0)), tile, tile],
        out_specs=[tile, tile], out_shape=[shape, shape], compiler_params=_params("parallel", "parallel"),
    )(dffn, w_fo, a_act, up_act)


def _tril(n):
    return lax.broadcasted_iota(jnp.int32, (n, n), 0) >= lax.broadcasted_iota(jnp.int32, (n, n), 1)


def _gmlp_norm(v, ln_g, ln_b):
    gv = _gelu(v)
    mu = jnp.mean(gv, axis=-1, keepdims=True)
    cen = gv - mu
    rs = lax.rsqrt(jnp.mean(cen * cen, axis=-1, keepdims=True) + EPS)
    xh = cen * rs
    return xh, rs, xh * ln_g + ln_b


def _gmlp_fwd(z, ln_g, ln_b, ws, bs_t):
    s = z.shape[0]
    gw = ln_g.shape[1]
    groups, chunk, _ = ws.shape
    per = GMLP_CHUNKS_PER_STEP if s % (GMLP_CHUNKS_PER_STEP * chunk) == 0 else 1
    rows = per * chunk

    def body(u_ref, v_ref, lg_ref, lb_ref, ws_ref, bs_ref, ya_ref):
        mask = _tril(chunk)
        for cc in range(per):
            rs_ = slice(cc * chunk, (cc + 1) * chunk)
            gu = _gelu(u_ref[rs_, :])
            _, _, vn = _gmlp_norm(v_ref[rs_, :], lg_ref[...], lb_ref[...])
            for g in range(groups):
                cols = slice(g * LANES, (g + 1) * LANES)
                wm = jnp.where(mask, ws_ref[g], 0.0).astype(BF16)
                sg = _dot(wm, vn[:, cols].astype(BF16)) + bs_ref[:, g:g + 1]
                ya_ref[rs_, cols] = (gu[:, cols] * sg).astype(BF16)

    return _pcall(
        body, name="gmlp_fwd", grid=(s // rows,),
        in_specs=[_row_spec(rows, gw, 0), _row_spec(rows, gw, 1), _vec_spec(gw), _vec_spec(gw),
                  pl.BlockSpec((groups, chunk, chunk), lambda i: (0, 0, 0)), pl.BlockSpec((chunk, LANES), lambda i: (0, 0))],
        out_specs=_row_spec(rows, gw), out_shape=jax.ShapeDtypeStruct((s, gw), BF16),
        compiler_params=_params("parallel"),
    )(z, z, ln_g, ln_b, ws, bs_t)


def _gmlp_bwd(dya, z, ln_g, ln_b, ws, bs_t):
    s = z.shape[0]
    gw = ln_g.shape[1]
    groups, chunk, _ = ws.shape
    per = GMLP_CHUNKS_PER_STEP if s % (GMLP_CHUNKS_PER_STEP * chunk) == 0 else 1
    rows = per * chunk

    def body(dya_ref, u_ref, v_ref, lg_ref, lb_ref, ws_ref, bs_ref, duv_ref, dws_ref, dbs_ref, dlg_ref, dlb_ref, dvn_ref):
        first = pl.program_id(0) == 0
        lg = lg_ref[...]
        mask = _tril(chunk)
        lane = lax.broadcasted_iota(jnp.int32, (chunk, LANES), 1)
        dbs = jnp.zeros((chunk, LANES), F32)
        dws = [jnp.zeros((chunk, chunk), F32) for _ in range(groups)]
        dlb = jnp.zeros((1, gw), F32)
        dlg = jnp.zeros((1, gw), F32)
        for cc in range(per):
            rs_ = slice(cc * chunk, (cc + 1) * chunk)
            u, v = u_ref[rs_, :], v_ref[rs_, :]
            gu = _gelu(u)
            xh, rs, vn = _gmlp_norm(v, lg, lb_ref[...])
            dyav = dya_ref[rs_, :]
            for g in range(groups):
                cols = slice(g * LANES, (g + 1) * LANES)
                wm = jnp.where(mask, ws_ref[g], 0.0).astype(BF16)
                vg = vn[:, cols].astype(BF16)
                sg = _dot(wm, vg) + bs_ref[:, g:g + 1]
                ds = dyav[:, cols] * gu[:, cols]
                duv_ref[rs_, cols] = (dyav[:, cols] * sg * _gelu_grad(u[:, cols])).astype(BF16)
                dsb = ds.astype(BF16)
                dws[g] = dws[g] + jnp.where(mask, _dot(dsb, vg, NT), 0.0)
                dbs = dbs + jnp.where(lane == g, jnp.sum(ds, axis=-1, keepdims=True), 0.0)
                dvn_ref[rs_, cols] = _dot(wm, dsb, TN)
            dvn = dvn_ref[rs_, :]
            dlb = dlb + _colsum(dvn)
            dlg = dlg + _colsum(dvn * xh)
            dxh = dvn * lg
            dgv = rs * (dxh - jnp.mean(dxh, axis=-1, keepdims=True) - xh * jnp.mean(dxh * xh, axis=-1, keepdims=True))
            duv_ref[rs_, gw:] = (dgv * _gelu_grad(v)).astype(BF16)
        for g in range(groups):
            _accumulate(first, dws_ref.at[g], dws[g])
        _accumulate(first, dbs_ref, dbs)
        _accumulate(first, dlb_ref, dlb)
        _accumulate(first, dlg_ref, dlg)

    return _pcall(
        body, name="gmlp_bwd", grid=(s // rows,),
        in_specs=[_row_spec(rows, gw), _row_spec(rows, gw, 0), _row_spec(rows, gw, 1), _vec_spec(gw), _vec_spec(gw),
                  pl.BlockSpec((groups, chunk, chunk), lambda i: (0, 0, 0)), pl.BlockSpec((chunk, LANES), lambda i: (0, 0))],
        out_specs=[_row_spec(rows, 2 * gw), pl.BlockSpec((groups, chunk, chunk), lambda i: (0, 0, 0)),
                   pl.BlockSpec((chunk, LANES), lambda i: (0, 0)), _vec_spec(gw), _vec_spec(gw)],
        out_shape=[jax.ShapeDtypeStruct((s, 2 * gw), BF16), jax.ShapeDtypeStruct((groups, chunk, chunk), F32),
                   jax.ShapeDtypeStruct((chunk, LANES), F32), jax.ShapeDtypeStruct((1, gw), F32),
                   jax.ShapeDtypeStruct((1, gw), F32)],
        scratch_shapes=[pltpu.VMEM((rows, gw), F32)],
        compiler_params=_params("arbitrary"),
    )(dya, z, z, ln_g, ln_b, ws, bs_t)


def _lower_bound(lb_ref):
    a0, a1 = lb_ref[0:1, :], lb_ref[1:2, :]
    mx = jnp.maximum(a0, a1)
    e0, e1 = jnp.exp(a0 - mx), jnp.exp(a1 - mx)
    return e0 / (e0 + e1)


def _sum_dot(mask, x):
    hi = x.astype(BF16)
    rest = x - hi.astype(F32)
    mid = rest.astype(BF16)
    low = (rest - mid.astype(F32)).astype(BF16)
    return _dot(mask, hi) + _dot(mask, mid) + _dot(mask, low)


def _ones_where(mask):
    return jnp.where(mask, 1.0, 0.0).astype(BF16)


def _hg_masks(rows, t):
    r = lax.broadcasted_iota(jnp.int32, (rows, rows), 0)
    c = lax.broadcasted_iota(jnp.int32, (rows, rows), 1)
    same = (r // t) == (c // t)
    incl = jnp.logical_and(same, c <= r)
    upto_mid = jnp.logical_and(same, (c % t) <= t // 2)
    rev = jnp.logical_and(same, c >= r)
    return same, incl, upto_mid, rev


def _hg_block(q, fp, lb, masks):
    rows = q.shape[0]
    same, incl, upto_mid, _ = masks
    sig = _sigmoid(fp)
    f = lb + (1.0 - lb) * sig
    k = 1.0 - f
    sq = _sigmoid(q)
    qa = q * sq
    stacked = jnp.concatenate([_ones_where(m) for m in (incl, same, upto_mid)], axis=0)
    sums = _sum_dot(stacked, jnp.log(f))
    b, b_last, b_mid = sums[:rows], sums[rows:2 * rows], sums[2 * rows:]
    e_q = jnp.exp(jnp.minimum(b - b_mid, EXP_CLAMP))
    e_k = jnp.exp(jnp.minimum(b_mid - b, EXP_CLAMP))
    e_in = jnp.exp(b)
    e_out = jnp.exp(b_last - b)
    return dict(sig=sig, f=f, k=k, sq=sq, qa=qa, e_last=jnp.exp(b_last), e_q=e_q, e_k=e_k, e_in=e_in, e_out=e_out,
                q_hat=(qa * e_q).astype(BF16), k_hat=(k * e_k).astype(BF16),
                q_in=(qa * e_in).astype(BF16), k_out=k * e_out)


def _hgrn_fwd(z, hg_lb, norm_g, q_col):
    s = z.shape[0]
    hw = norm_g.shape[1]
    heads = hw // LANES
    t = HG_CHUNK
    rows = min(HG_ROWS, s)
    per_step = rows // t
    hp = min(HG_HEADS_PER_STEP, heads)
    assert heads % hp == 0 and q_col % hp == 0, (heads, q_col)
    wide = hp * LANES

    def zspec(which):
        return pl.BlockSpec((rows, wide), lambda h, r, which=which: (r, (q_col + which * heads) // hp + h))

    def body(q_ref, f_ref, i_ref, g_ref, lb_ref, ng_ref, yb_ref, o_ref, st_out_ref, st_ref, e_last_ref, inter_ref):
        @pl.when(pl.program_id(1) == 0)
        def _():
            st_ref[...] = jnp.zeros_like(st_ref)

        masks = _hg_masks(rows, t)
        for hh in range(hp):
            cols = slice(hh * LANES, (hh + 1) * LANES)
            blk = _hg_block(q_ref[:, cols], f_ref[:, cols], _lower_bound(lb_ref.at[:, cols]), masks)
            iv = i_ref[:, cols].astype(BF16)
            q_in, k_out = blk["q_in"], blk["k_out"].astype(BF16)
            e_last_ref[hh] = blk["e_last"]
            attn = jnp.where(masks[1], _dot(blk["q_hat"], blk["k_hat"], NT), 0.0).astype(BF16)
            o = _dot(attn, iv)
            grown = [_dot(iv[j * t:(j + 1) * t], k_out[j * t:(j + 1) * t], TN) for j in range(per_step)]
            st = st_ref[hh]
            for j in range(per_step):
                st_out_ref[hh, j] = st
                inter_ref[hh, j * t:(j + 1) * t, :] = _dot(q_in[j * t:(j + 1) * t], st.astype(BF16), NT)
                st = st * e_last_ref[hh, j * t:j * t + 1, :] + grown[j]
            st_ref[hh] = st
            o = o + inter_ref[hh]
            o_ref[:, cols] = o
            og = g_ref[:, cols]
            on, _ = _rms(o)
            yb_ref[:, cols] = (on * ng_ref[:, cols] * (og * _sigmoid(og))).astype(BF16)

    out_row = pl.BlockSpec((rows, wide), lambda h, r: (r, h))
    return _pcall(
        body, name="hgrn_fwd", grid=(heads // hp, s // rows),
        in_specs=[zspec(0), zspec(1), zspec(2), zspec(3),
                  pl.BlockSpec((2, wide), lambda h, r: (0, h)), pl.BlockSpec((1, wide), lambda h, r: (0, h))],
        out_specs=[out_row, out_row, pl.BlockSpec((hp, per_step, LANES, LANES), lambda h, r: (h, r, 0, 0))],
        out_shape=[jax.ShapeDtypeStruct((s, hw), BF16), jax.ShapeDtypeStruct((s, hw), F32),
                   jax.ShapeDtypeStruct((heads, s // t, LANES, LANES), F32)],
        scratch_shapes=[pltpu.VMEM((hp, LANES, LANES), F32), pltpu.VMEM((hp, rows, LANES), F32),
                        pltpu.VMEM((hp, rows, LANES), F32)],
        compiler_params=_params("parallel", "arbitrary"),
    )(z, z, z, z, hg_lb, norm_g)


def _hgrn_bwd(dyb, z, o_raw, states, hg_lb, norm_g, q_col):
    s = z.shape[0]
    hw = norm_g.shape[1]
    heads = hw // LANES
    t = HG_CHUNK
    rows = min(HG_ROWS, s)
    per_step = rows // t
    n_steps = s // rows
    hp = min(HG_HEADS_PER_STEP, heads)
    assert heads % hp == 0 and q_col % hp == 0, (heads, q_col)
    wide = hp * LANES

    def zspec(which):
        return pl.BlockSpec((rows, wide), lambda h, r, which=which: (n_steps - 1 - r, (q_col + which * heads) // hp + h))

    def body(dyb_ref, q_ref, f_ref, i_ref, g_ref, o_ref, st_in_ref, lb_ref, ng_ref,
             dq_ref, df_ref, di_ref, dg_ref, dlb_ref, dng_ref, dst_ref, acc_lb_ref, acc_ng_ref,
             e_last_ref, dq_in_ref, dk_out_ref, di_inter_ref, carry_ref):
        step = pl.program_id(1)

        @pl.when(step == 0)
        def _():
            dst_ref[...] = jnp.zeros_like(dst_ref)
            acc_lb_ref[...] = jnp.zeros_like(acc_lb_ref)
            acc_ng_ref[...] = jnp.zeros_like(acc_ng_ref)

        masks = _hg_masks(rows, t)
        same, incl, _, rev = masks
        sum_mask = jnp.concatenate([_ones_where(rev), _ones_where(same)], axis=1)
        for hh in range(hp):
            cols = slice(hh * LANES, (hh + 1) * LANES)
            lb = _lower_bound(lb_ref.at[:, cols])
            ng = ng_ref[:, cols]
            q = q_ref[:, cols]
            blk = _hg_block(q, f_ref[:, cols], lb, masks)
            iv = i_ref[:, cols].astype(BF16)
            o, og, dy = o_ref[:, cols], g_ref[:, cols], dyb_ref[:, cols]
            so = _sigmoid(og)
            on, r = _rms(o)
            acc_ng_ref[:, cols] += _colsum(dy * on * (og * so))
            dg_ref[:, cols] = (dy * on * ng * so * (1.0 + og * (1.0 - so))).astype(BF16)
            don = dy * ng * (og * so)
            do = (r * (don - on * jnp.mean(don * on, axis=-1, keepdims=True))).astype(BF16)
            q_hat, k_hat, q_in, k_out = blk["q_hat"], blk["k_hat"], blk["q_in"], blk["k_out"]
            k_out_b = k_out.astype(BF16)
            attn = jnp.where(incl, _dot(q_hat, k_hat, NT), 0.0).astype(BF16)
            d_attn = jnp.where(incl, _dot(do, iv, NT), 0.0).astype(BF16)
            di_intra = _dot(attn, do, TN)
            dq_hat = _dot(d_attn, k_hat)
            dk_hat = _dot(d_attn, q_hat, TN)
            e_last_ref[hh] = blk["e_last"]
            grown = [_dot(do[j * t:(j + 1) * t], q_in[j * t:(j + 1) * t], TN) for j in range(per_step)]
            dst = dst_ref[hh]
            for j in reversed(range(per_step)):
                rs_ = slice(j * t, (j + 1) * t)
                e_last = e_last_ref[hh, j * t:j * t + 1, :]
                st_prev, dst_b = st_in_ref[hh, j], dst.astype(BF16)
                dq_in_ref[hh, rs_, :] = _dot(do[rs_], st_prev.astype(BF16))
                dk_out_ref[hh, rs_, :] = _dot(iv[rs_], dst_b)
                di_inter_ref[hh, rs_, :] = _dot(k_out_b[rs_], dst_b, NT)
                carry_ref[hh, rs_, :] = jnp.broadcast_to(e_last * _colsum(st_prev * dst), (t, LANES))
                dst = dst * e_last + grown[j]
            dst_ref[hh] = dst
            di_ref[:, cols] = (di_intra + di_inter_ref[hh]).astype(BF16)
            dk_out = dk_out_ref[hh]
            dqa = dq_in_ref[hh] * blk["e_in"] + dq_hat * blk["e_q"]
            dk = dk_out * blk["e_out"] + dk_hat * blk["e_k"]
            db = blk["qa"] * dqa - blk["k"] * dk
            dlf = _sum_dot(sum_mask, jnp.concatenate([db, dk_out * k_out], axis=0)) + carry_ref[hh]
            dfv = dlf / blk["f"] - dk
            sig, sq = blk["sig"], blk["sq"]
            df_ref[:, cols] = (dfv * (1.0 - lb) * sig * (1.0 - sig)).astype(BF16)
            acc_lb_ref[:, cols] += _colsum(dfv * (1.0 - sig))
            dq_ref[:, cols] = (dqa * sq * (1.0 + q * (1.0 - sq))).astype(BF16)

        @pl.when(step == n_steps - 1)
        def _():
            lb = _lower_bound(lb_ref)
            d0 = acc_lb_ref[...] * lb * (1.0 - lb)
            dlb_ref[0:1, :] = d0
            dlb_ref[1:2, :] = -d0
            dng_ref[...] = acc_ng_ref[...]

    rev_row = pl.BlockSpec((rows, wide), lambda h, r: (n_steps - 1 - r, h))
    piece = jax.ShapeDtypeStruct((s, hw), BF16)
    return _pcall(
        body, name="hgrn_bwd", grid=(heads // hp, n_steps),
        in_specs=[rev_row, zspec(0), zspec(1), zspec(2), zspec(3), rev_row,
                  pl.BlockSpec((hp, per_step, LANES, LANES), lambda h, r: (h, n_steps - 1 - r, 0, 0)),
                  pl.BlockSpec((2, wide), lambda h, r: (0, h)), pl.BlockSpec((1, wide), lambda h, r: (0, h))],
        out_specs=[rev_row, rev_row, rev_row, rev_row,
                   pl.BlockSpec((2, wide), lambda h, r: (0, h)), pl.BlockSpec((1, wide), lambda h, r: (0, h))],
        out_shape=[piece, piece, piece, piece, jax.ShapeDtypeStruct((2, hw), F32), jax.ShapeDtypeStruct((1, hw), F32)],
        scratch_shapes=[pltpu.VMEM((hp, LANES, LANES), F32), pltpu.VMEM((1, wide), F32), pltpu.VMEM((1, wide), F32)]
        + [pltpu.VMEM((hp, rows, LANES), F32)] * 5,
        compiler_params=_params("parallel", "arbitrary"),
    )(dyb, z, z, z, z, o_raw, states, hg_lb, norm_g)


def _adam_update(w, m, v, g):
    m2 = ADAM_B1 * m + (1.0 - ADAM_B1) * g
    v2 = ADAM_B2 * v + (1.0 - ADAM_B2) * (g * g)
    m_hat = m2 * (1.0 / (1.0 - ADAM_B1 ** ADAM_STEP))
    v_hat = v2 * (1.0 / (1.0 - ADAM_B2 ** ADAM_STEP))
    return -ADAM_LR * (m_hat / (jnp.sqrt(v_hat) + ADAM_EPS) + ADAM_WD * w), m2, v2


def _adamw(w, m, v, parts, name, outer=False):
    rows, cols = w.shape
    bc = cols if cols <= 4096 else _pick(cols, 4096)
    rb = _pick_rows(rows, max(8, (384 * 1024) // bc), mult=8)
    if outer and rb % LANES:
        rb = rows

    def body(w_ref, m_ref, v_ref, *refs):
        g_ref, d_ref, mo_ref, vo_ref = refs[-4:]
        if outer:
            cv = refs[0][...]
            g = _dot(cv * _sigmoid(cv), refs[1][...], TN, lax.Precision.HIGHEST)
        else:
            p_ref = refs[0]
            g = p_ref[0].astype(F32)
            for p in range(1, p_ref.shape[0]):
                g = g + p_ref[p].astype(F32)
        g_ref[...] = g
        d_ref[...], mo_ref[...], vo_ref[...] = _adam_update(w_ref[...], m_ref[...], v_ref[...], g)

    blk = pl.BlockSpec((rb, bc), lambda i, j: (i, j))
    out = jax.ShapeDtypeStruct((rows, cols), F32)
    if outer:
        grad_specs = [pl.BlockSpec((N_DEV, rb), lambda i, j: (0, i)), pl.BlockSpec((N_DEV, bc), lambda i, j: (0, j))]
        grad_ins = tuple(parts)
    else:
        grad_specs = [pl.BlockSpec((parts.shape[0], rb, bc), lambda i, j: (0, i, j))]
        grad_ins = (parts,)
    return _pcall(
        body, name=name, grid=(rows // rb, cols // bc), in_specs=[blk, blk, blk] + grad_specs,
        out_specs=[blk] * 4, out_shape=[out] * 4, compiler_params=_params("parallel", "parallel"),
    )(w, m, v, *grad_ins)


SMALL = ("b_ada", "norm1_g", "b_gate", "gmlp_ln_g", "gmlp_ln_b", "gmlp_ws", "gmlp_bs", "hg_lb", "hg_norm_g",
         "norm2_g", "final_norm_g")
BIG = ("w_in", "w_branch_gmlp", "w_branch_hg", "w_out", "w_ffn_in", "w_ffn_out")
WEIGHTS = ("w_ada", "b_ada", "norm1_g", "w_in", "b_gate", "gmlp_ln_g", "gmlp_ln_b", "gmlp_ws", "gmlp_bs", "hg_lb",
           "hg_norm_g", "w_branch_gmlp", "w_branch_hg", "w_out", "norm2_g", "w_ffn_in", "w_ffn_out", "final_norm_g")


def _pack(parts):
    return jnp.concatenate([p.reshape(-1, LANES) for p in parts], axis=0)


def _step(x, c, loss_target, w, m, v):
    s, d = x.shape[1], x.shape[2]
    gw = w["gmlp_ln_g"].shape[-1]
    hw = w["hg_norm_g"].shape[-1]
    x2d, tgt = x[0], loss_target[0]
    mx, my, mc = lax.axis_index("x"), lax.axis_index("y"), lax.axis_index("c")
    chip = 2 * mx + my
    dev = 2 * chip + mc
    q_col = 2 * gw // LANES
    gate_col = (2 * gw + 4 * hw) // d
    place = jnp.stack([chip, mc]).astype(jnp.int32)
    _Order.last = None

    c_all = _all_gather8(c.reshape(-1, LANES), "gather_c").reshape(N_DEV, d)
    gather_in = _WeightGather("in", _WeightGather.zones([w["w_in"][0]], chip), ring=True)
    zones_mix = _WeightGather.zones([w[n][0] for n in ("w_branch_gmlp", "w_branch_hg", "w_out")], chip)
    zones_fi = _WeightGather.zones([w["w_ffn_in"][0]], chip)
    zones_fo = _WeightGather.zones([w["w_ffn_out"][0]], chip)
    _Order.also = (*zones_mix, *zones_fi, *zones_fo)
    n_ada = w["w_ada"].shape[-1]
    b_ada_q = lax.dynamic_slice(w["b_ada"], (0, chip * n_ada), (1, n_ada))
    mod_q = _ada_fwd(c_all, w["w_ada"][0], b_ada_q)
    mod_all = _all_gather8(mod_q, "gather_mod")
    mod = lax.dynamic_index_in_dim(mod_all, dev, axis=1, keepdims=False)[::2].reshape(1, 6 * d)
    sh1, sc1, gt1, sh2, sc2, gt2 = [mod[:, i * d:(i + 1) * d] for i in range(6)]

    gather_in.relay_and_pass_near()
    gather_mix = _WeightGather("mix", zones_mix)
    gather_fi = _WeightGather("fi", zones_fi, ring=True)

    norm1_g, norm2_g, final_g = w["norm1_g"], w["norm2_g"], w["final_norm_g"].reshape(1, d)
    ln_g, ln_b = w["gmlp_ln_g"], w["gmlp_ln_b"]
    ws = w["gmlp_ws"][0]
    groups = ws.shape[0]
    bs_t = jnp.pad(w["gmlp_bs"][0].T, ((0, 0), (0, LANES - groups)))
    hg_lb, hg_ng, b_gate = w["hg_lb"], w["hg_norm_g"], w["b_gate"]

    h1 = _norm_mod_fwd(x2d, norm1_g, sc1, sh1, "norm1_fwd")
    order = jnp.stack([chip, 2 * (1 - mx) + my, 2 * mx + (1 - my), 2 * (1 - mx) + (1 - my)]).astype(jnp.int32)
    w_in, = gather_in.near_done()
    z = _matmul_quarters(h1, w_in, order, 0, 3, "mm_z_near")
    w_in, = gather_in.pass_far()
    z = _matmul_quarters(h1, w_in, order, 3, 1, "mm_z_far", prev=z)
    gather_mix.pass_on()
    ya = _gmlp_fwd(z, ln_g, ln_b, ws, bs_t)
    yb, o_raw, states = _hgrn_fwd(z, hg_lb, hg_ng, q_col)
    gather_fi.relay()
    gather_fo = _WeightGather("fo", zones_fo)
    w_bg, w_bh, w_out = gather_mix.done()
    w_bg, w_bh = [wq.transpose(1, 0, 2).reshape(wq.shape[1], -1) for wq in (w_bg, w_bh)]
    w_out = w_out.reshape(-1, w_out.shape[-1])
    pa, pb, y = _branch_merge(ya, yb, w_bg, w_bh, z, b_gate, gate_col)
    yo = _matmul(y, w_out, mode="nn", name="mm_yo", out_dtype=F32)
    gather_fi.pass_on()
    x1, h2 = _norm_mod_fwd(x2d, norm2_g, sc2, sh2, "norm2_fwd", res=yo, gt=gt1)
    w_fi, = gather_fi.done()
    a_act, up_act, hf = _ffn_in(h2, w_fi)
    gather_fo.pass_on()
    w_fo, = gather_fo.done()
    w_fo = w_fo.reshape(-1, w_fo.shape[-1])
    ffn = _matmul(hf, w_fo, mode="nn", name="mm_ffn", out_dtype=F32)
    dx2, dffn, loss_row, d_final_g, d_gt2 = _final_loss(x1, ffn, gt2, final_g, tgt)

    g_fo = _matmul(hf, dffn, mode="tn", name="mm_g_fo", out_dtype=BF16, bm=_half_width(hf.shape[1] // 2))
    daup = tuple(_ffn_out_bwd(dffn, w_fo, a_act, up_act))
    g_fi = _matmul(h2, daup, mode="tn", name="mm_g_fi", out_dtype=BF16, out_slots=True, bn=_half_width(w_fi.shape[2]))
    red_ffn = _GradReduce("ffn", [g_fo.reshape(N_CHIPS, -1, g_fo.shape[-1]), g_fi])
    dh2 = _matmul(daup, w_fi, mode="nt", name="mm_dh2", out_dtype=F32, b_slots=True)
    red_ffn.step(place)
    dx1, dyo, d_sh2, d_sc2, d_norm2, d_gt1 = _norm_mod_bwd(dh2, x1, dx2, norm2_g, sc2, "norm2_bwd", branch=yo, gt=gt1)
    g_out = _matmul(y, dyo, mode="tn", name="mm_g_out", out_dtype=BF16)
    dpa, dpb, dz_ga, dz_gb, d_b_ga, d_b_gb = _out_bwd(dyo, w_out, pa, pb, z, b_gate, gate_col)
    g_bg, g_bh = _branch_wgrad(ya, yb, dpa, dpb)
    red_mix = _GradReduce("mix", [g_out.reshape(N_CHIPS, -1, g_out.shape[-1]), g_bg, g_bh])
    dya, dyb = _branch_bwd(dpa, dpb, w_bg, w_bh)
    red_mix.step(place)
    dz_uv, d_ws, d_bs_t, d_ln_g, d_ln_b = _gmlp_bwd(dya, z, ln_g, ln_b, ws, bs_t)
    dz_q, dz_f, dz_i, dz_g, d_hg_lb, d_hg_ng = _hgrn_bwd(dyb, z, o_raw, states, hg_lb, hg_ng, q_col)
    dz = jnp.concatenate([dz_uv, dz_q, dz_f, dz_i, dz_g, dz_ga, dz_gb], axis=1)
    grad, delta, new_m, new_v = {}, {}, {}, {}

    def update(n, parts, outer=False):
        outs = _adamw(w[n][0], m[n][0], v[n][0], parts, "adamw_" + n, outer=outer)
        grad[n], delta[n], new_m[n], new_v[n] = [o[None] for o in outs]

    g_in = _matmul(h1, dz, mode="tn", name="mm_g_in", out_dtype=BF16, out_slots=True, bn=_half_width(w_in.shape[2]))
    red_in = _GradReduce("in", [g_in])
    red_ffn.join(place)
    red_mix.join(place)
    red_in.step(place)
    g_fo, g_fi = red_ffn.done()
    n = "w_ffn_in"
    dh1, rode = _matmul(dz, w_in, mode="nt", name="mm_dh1", out_dtype=F32, b_slots=True,
                        rider=([w[n][0], m[n][0], v[n][0], g_fi], lambda *blk: (blk[3], *_adam_update(*blk)), 4))
    grad[n], delta[n], new_m[n], new_v[n] = [o[None] for o in rode]
    grad_x, d_sh1, d_sc1, d_norm1 = _norm_mod_bwd(dh1, x2d, dx1, norm1_g, sc1, "norm1_bwd")

    d_mod = jnp.concatenate([d_sh1, d_sc1, d_gt1, d_sh2, d_sc2, d_gt2], axis=1)
    small_part = {"b_ada": d_mod, "norm1_g": d_norm1, "b_gate": jnp.concatenate([d_b_ga, d_b_gb], axis=1), "gmlp_ln_g": d_ln_g, "gmlp_ln_b": d_ln_b,
                  "gmlp_ws": d_ws, "gmlp_bs": d_bs_t[:, :groups].T, "hg_lb": d_hg_lb, "hg_norm_g": d_hg_ng,
                  "norm2_g": d_norm2, "final_norm_g": d_final_g}
    loss_rows = jnp.broadcast_to(loss_row, (8, LANES))
    packed = _pack([small_part[n] for n in SMALL] + [loss_rows])
    zone = lax.dynamic_update_slice(lax.empty((N_DEV,) + packed.shape, F32), packed[None], (dev, 0, 0))
    small_sems, small_bufs = _xfer_start("gsmall_start", [zone], _plan_gather_out, 4, after_last=True)
    update("w_ffn_out", g_fo[None])
    g_out, g_bg, g_bh = red_mix.done()
    update("w_out", g_out[None])
    red_in.join(place)
    (small_sems,), small_bufs = _xfer_hand_over("gsmall_pass", small_sems, small_bufs, _plan_gather_out,
                                                [(_plan_gather_on, 3)])
    update("w_branch_gmlp", g_bg[None])
    update("w_branch_hg", g_bh[None])
    small_all, = _xfer_wait("gsmall_wait", small_sems, small_bufs, _plan_gather_on)
    d_mod_all = small_all[:, :6 * d // LANES].reshape(N_DEV, 6 * d)
    d_mod_q = lax.dynamic_slice(d_mod_all, (0, chip * n_ada), (N_DEV, n_ada))
    pad = [jnp.zeros_like(loss_rows)]
    outs = _adamw(_pack([w[n] for n in SMALL] + pad), _pack([m[n] for n in SMALL] + pad),
                  _pack([v[n] for n in SMALL] + pad), small_all, "adamw_small")
    update("w_ada", (c_all, d_mod_q), outer=True)
    update("w_in", red_in.done()[0][None])
    row = 0
    for n in SMALL:
        cnt = w[n].size // LANES
        for dst, o in zip((grad, delta, new_m, new_v), outs):
            dst[n] = o[row:row + cnt].reshape(w[n].shape)
        row += cnt

    loss = outs[0][row, 0]
    return (loss, grad_x[None], *[grad[n] for n in WEIGHTS], *[delta[n] for n in WEIGHTS],
            *[new_m[n] for n in WEIGHTS], *[new_v[n] for n in WEIGHTS])


def kernel(x, c, w_ada, b_ada, norm1_g, w_in, b_gate, gmlp_ln_g, gmlp_ln_b, gmlp_ws, gmlp_bs, hg_lb, hg_norm_g, w_branch_gmlp, w_branch_hg, w_out, norm2_g, w_ffn_in, w_ffn_out, final_norm_g, loss_target, m_w_ada, m_b_ada, m_norm1_g, m_w_in, m_b_gate, m_gmlp_ln_g, m_gmlp_ln_b, m_gmlp_ws, m_gmlp_bs, m_hg_lb, m_hg_norm_g, m_w_branch_gmlp, m_w_branch_hg, m_w_out, m_norm2_g, m_w_ffn_in, m_w_ffn_out, m_final_norm_g, v_w_ada, v_b_ada, v_norm1_g, v_w_in, v_b_gate, v_gmlp_ln_g, v_gmlp_ln_b, v_gmlp_ws, v_gmlp_bs, v_hg_lb, v_hg_norm_g, v_w_branch_gmlp, v_w_branch_hg, v_w_out, v_norm2_g, v_w_ffn_in, v_w_ffn_out, v_final_norm_g):
    w = dict(w_ada=w_ada, b_ada=b_ada, norm1_g=norm1_g, w_in=w_in, b_gate=b_gate, gmlp_ln_g=gmlp_ln_g,
             gmlp_ln_b=gmlp_ln_b, gmlp_ws=gmlp_ws, gmlp_bs=gmlp_bs, hg_lb=hg_lb, hg_norm_g=hg_norm_g,
             w_branch_gmlp=w_branch_gmlp, w_branch_hg=w_branch_hg, w_out=w_out, norm2_g=norm2_g,
             w_ffn_in=w_ffn_in, w_ffn_out=w_ffn_out, final_norm_g=final_norm_g)
    m = dict(w_ada=m_w_ada, b_ada=m_b_ada, norm1_g=m_norm1_g, w_in=m_w_in, b_gate=m_b_gate, gmlp_ln_g=m_gmlp_ln_g,
             gmlp_ln_b=m_gmlp_ln_b, gmlp_ws=m_gmlp_ws, gmlp_bs=m_gmlp_bs, hg_lb=m_hg_lb, hg_norm_g=m_hg_norm_g,
             w_branch_gmlp=m_w_branch_gmlp, w_branch_hg=m_w_branch_hg, w_out=m_w_out, norm2_g=m_norm2_g,
             w_ffn_in=m_w_ffn_in, w_ffn_out=m_w_ffn_out, final_norm_g=m_final_norm_g)
    v = dict(w_ada=v_w_ada, b_ada=v_b_ada, norm1_g=v_norm1_g, w_in=v_w_in, b_gate=v_b_gate, gmlp_ln_g=v_gmlp_ln_g,
             gmlp_ln_b=v_gmlp_ln_b, gmlp_ws=v_gmlp_ws, gmlp_bs=v_gmlp_bs, hg_lb=v_hg_lb, hg_norm_g=v_hg_norm_g,
             w_branch_gmlp=v_w_branch_gmlp, w_branch_hg=v_w_branch_hg, w_out=v_w_out, norm2_g=v_norm2_g,
             w_ffn_in=v_w_ffn_in, w_ffn_out=v_w_ffn_out, final_norm_g=v_final_norm_g)
    return _step(x, c, loss_target, w, m, v)
```

```python
import jax
import jax.numpy as jnp
from jax import lax
from jax.experimental import pallas as pl
from jax.experimental.pallas import tpu as pltpu

F32 = jnp.float32
BF16 = jnp.bfloat16
EPS = 1e-6
LANES = 128
N_CHIPS = 4
N_DEV = 8
VMEM_LIMIT_BYTES = 56 * 1024 * 1024
VMEM_RIDER_LIMIT_BYTES = 60 * 1024 * 1024
HG_CHUNK = 32
HG_ROWS = 256
HG_HEADS_PER_STEP = 8
GMLP_CHUNKS_PER_STEP = 2
EXP_CLAMP = 80.0
ADAM_LR, ADAM_B1, ADAM_B2, ADAM_EPS, ADAM_WD, ADAM_STEP = 0.001, 0.9, 0.999, 1e-08, 0.01, 10
MESH = pl.DeviceIdType.MESH

NN = (((1,), (0,)), ((), ()))
NT = (((1,), (1,)), ((), ()))
TN = (((0,), (0,)), ((), ()))


def _dot(a, b, dims=NN, precision=None):
    return lax.dot_general(a, b, dims, precision=precision, preferred_element_type=F32)


def _params(*semantics, vmem_limit_bytes=VMEM_LIMIT_BYTES):
    return pltpu.CompilerParams(dimension_semantics=semantics, vmem_limit_bytes=vmem_limit_bytes)


class _Order:
    last = None
    also = ()


def _pcall(body, *, in_specs, out_specs, grid=(), scratch_shapes=(), num_scalar_prefetch=0, **kw):
    def run(*ins):
        deps = (() if _Order.last is None else (_Order.last,)) + tuple(_Order.also)
        _Order.also = ()
        n_in, n_dep = len(ins), len(deps)

        def wrapped(*refs):
            body(*refs[:n_in], *refs[n_in + n_dep:])

        specs = list(in_specs) + [pl.BlockSpec(memory_space=pl.ANY)] * n_dep
        if num_scalar_prefetch:
            grid_spec = pltpu.PrefetchScalarGridSpec(
                num_scalar_prefetch=num_scalar_prefetch, grid=grid, in_specs=specs, out_specs=out_specs,
                scratch_shapes=scratch_shapes)
            outs = pl.pallas_call(wrapped, grid_spec=grid_spec, **kw)(*ins, *deps)
        else:
            outs = pl.pallas_call(wrapped, grid=grid, in_specs=specs, out_specs=out_specs,
                                  scratch_shapes=scratch_shapes, **kw)(*ins, *deps)
        _Order.last = jax.tree.leaves(outs)[0]
        return outs

    return run


def _pick_rows(dim, pref, mult=16):
    best = None
    for cand in range(mult, min(dim, pref) + 1, mult):
        if dim % cand == 0:
            best = cand
    assert best is not None, (dim, pref)
    return best


def _half_width(n):
    return n // 2 if n % (2 * LANES) == 0 else n


def _pick(dim, pref):
    if dim <= pref:
        return dim
    best = None
    for cand in range(LANES, pref + 1, LANES):
        if dim % cand == 0:
            best = cand
    assert best is not None, (dim, pref)
    return best


def _sigmoid(x):
    return 1.0 / (1.0 + jnp.exp(-x))


def _gelu(x):
    c = 0.7978845608028654
    return 0.5 * x * (1.0 + jnp.tanh(c * (x + 0.044715 * x * x * x)))


def _gelu_grad(x):
    c = 0.7978845608028654
    t = jnp.tanh(c * (x + 0.044715 * x * x * x))
    return 0.5 * (1.0 + t) + 0.5 * x * (1.0 - t * t) * c * (1.0 + 3.0 * 0.044715 * x * x)


def _rms(x):
    r = lax.rsqrt(jnp.mean(x * x, axis=-1, keepdims=True) + EPS)
    return x * r, r


def _colsum(x):
    return jnp.sum(x, axis=0, keepdims=True)


def _accumulate(first, ref, val):
    @pl.when(first)
    def _():
        ref[...] = val

    @pl.when(jnp.logical_not(first))
    def _():
        ref[...] += val


def _matmul(a, b, *, mode, name, out_dtype, b_slots=False, out_slots=False, bm=1024, bn=1024, bk=2816, rider=None):
    pair = a if isinstance(a, tuple) else b if isinstance(b, tuple) else None
    if mode == "nn":
        m, k = a.shape
        n = b.shape[2] * N_CHIPS if b_slots else b.shape[1]
        per = b.shape[2] if b_slots else n
    elif mode == "nt":
        m, k = (a[0].shape[0], 2 * a[0].shape[1]) if pair else a.shape
        n = b.shape[1] if b_slots else b.shape[0]
        per = b.shape[2] if b_slots else k
    else:
        k, m = a.shape
        n = 2 * b[0].shape[1] if pair else b.shape[1]
        per = n // N_CHIPS if out_slots else n
    bm = _pick(m, bm)
    if mode == "nt":
        bn, bk = _pick(n, bn), _pick(per, bk)
    else:
        bn, bk = _pick(per, bn), _pick(k, bk)
    nk = k // bk
    per_blocks = per // (bk if mode == "nt" else bn)
    dims = {"nn": NN, "nt": NT, "tn": TN}[mode]
    half = (nk if mode == "nt" else n // bn) // 2

    def product(a_ref, b_ref, o_ref, acc):
        part = _dot(a_ref[...], b_ref[...], dims)
        if nk == 1:
            o_ref[...] = part.astype(o_ref.dtype)
            return
        acc_ref, kk = acc[0], pl.program_id(2)

        @pl.when(kk == 0)
        def _():
            acc_ref[...] = part

        @pl.when(jnp.logical_and(kk > 0, kk < nk - 1))
        def _():
            acc_ref[...] += part

        @pl.when(kk == nk - 1)
        def _():
            o_ref[...] = (acc_ref[...] + part).astype(o_ref.dtype)

    n_ride, n_rode = (len(rider[0]), rider[2]) if rider else (0, 0)

    def body(*refs):
        if rider:
            ride_in, ride_out = refs[2:2 + n_ride], refs[3 + n_ride:3 + n_ride + n_rode]

            def ride(t, carry):
                rows = pl.ds(pl.multiple_of(t * 8, 8), 8)
                for ref, val in zip(ride_out, rider[1](*[r[rows, :] for r in ride_in])):
                    ref[rows, :] = val
                return carry

            lax.fori_loop(0, ride_in[0].shape[0] // 8, ride, 0)
        if not pair:
            return product(refs[0], refs[1], refs[2 + n_ride], refs[3 + n_ride + n_rode:])
        first = pl.program_id(2 if mode == "nt" else 1) < half
        x1, x2, y, o_ref, acc = refs[0], refs[1], refs[2], refs[3], refs[4:]

        @pl.when(first)
        def _():
            product(*((x1, y) if mode == "nt" else (y, x1)), o_ref, acc)

        @pl.when(jnp.logical_not(first))
        def _():
            product(*((x2, y) if mode == "nt" else (y, x2)), o_ref, acc)

    if mode == "nn":
        a_spec = pl.BlockSpec((bm, bk), lambda i, j, kk: (i, kk))
        if b_slots:
            b_spec = pl.BlockSpec((None, bk, bn), lambda i, j, kk: (j // per_blocks, kk, j % per_blocks))
        else:
            b_spec = pl.BlockSpec((bk, bn), lambda i, j, kk: (kk, j))
    elif mode == "nt":
        a_spec = pl.BlockSpec((bm, bk), lambda i, j, kk: (i, kk))
        if b_slots:
            b_spec = pl.BlockSpec((None, bn, bk), lambda i, j, kk: (kk // per_blocks, j, kk % per_blocks))
        else:
            b_spec = pl.BlockSpec((bn, bk), lambda i, j, kk: (j, kk))
    else:
        a_spec = pl.BlockSpec((bk, bm), lambda i, j, kk: (kk, i))
        b_spec = pl.BlockSpec((bk, bn), lambda i, j, kk: (kk, j))
    if out_slots:
        o_spec = pl.BlockSpec((None, bm, bn), lambda i, j, kk: (j // per_blocks, i, j % per_blocks))
        out_shape = jax.ShapeDtypeStruct((N_CHIPS, m, per), out_dtype)
    else:
        o_spec = pl.BlockSpec((bm, bn), lambda i, j, kk: (i, j))
        out_shape = jax.ShapeDtypeStruct((m, n), out_dtype)
    if not pair:
        ins, in_specs = (a, b), [a_spec, b_spec]
    elif mode == "nt":
        ins = (*a, b)
        in_specs = [pl.BlockSpec((bm, bk), lambda i, j, kk: (i, jnp.minimum(kk, half - 1))),
                    pl.BlockSpec((bm, bk), lambda i, j, kk: (i, jnp.maximum(kk - half, 0))), b_spec]
    else:
        ins = (*b, a)
        in_specs = [pl.BlockSpec((bk, bn), lambda i, j, kk: (kk, jnp.minimum(j, half - 1))),
                    pl.BlockSpec((bk, bn), lambda i, j, kk: (kk, jnp.maximum(j - half, 0))), a_spec]
    scratch = [pltpu.VMEM((bm, bn), F32)] if nk > 1 else []
    if not rider:
        return _pcall(
            body, name=name, grid=(m // bm, n // bn, nk), in_specs=in_specs, out_specs=o_spec, out_shape=out_shape,
            scratch_shapes=scratch, compiler_params=_params("parallel", "parallel", "arbitrary"),
        )(*ins)
    assert not pair
    rows, cols = rider[0][0].shape
    nj = n // bn
    rb = rows // ((m // bm) * nj * nk)
    assert rows == rb * (m // bm) * nj * nk and rb % 8 == 0, (rows, rb)
    ride_spec = pl.BlockSpec((rb, cols), lambda i, j, kk: ((i * nj + j) * nk + kk, 0))
    outs = _pcall(
        body, name=name, grid=(m // bm, nj, nk), in_specs=in_specs + [ride_spec] * n_ride,
        out_specs=[o_spec] + [ride_spec] * n_rode,
        out_shape=[out_shape] + [jax.ShapeDtypeStruct((rows, cols), F32)] * n_rode,
        scratch_shapes=scratch,
        compiler_params=_params("arbitrary", "arbitrary", "arbitrary", vmem_limit_bytes=VMEM_RIDER_LIMIT_BYTES),
    )(*ins, *rider[0])
    return outs[0], outs[1:]


def _matmul_quarters(h, w_slots, order, first, count, name, prev=None, bm=1024):
    s, k = h.shape
    n = w_slots.shape[2]
    bm, bn = _pick(s, bm), _half_width(n)
    pb = n // bn

    def body(order_ref, h_ref, w_ref, *rest):
        rest[-1][...] = _dot(h_ref[...], w_ref[...])

    ins = (order, h, w_slots) + (() if prev is None else (prev,))
    return _pcall(
        body, name=name, num_scalar_prefetch=1, grid=(count * pb, s // bm),
        in_specs=[pl.BlockSpec((bm, k), lambda j, i, o: (i, 0)),
                  pl.BlockSpec((None, k, bn), lambda j, i, o: (o[first + j // pb], 0, j % pb))]
        + ([] if prev is None else [ANY_SPEC]),
        out_specs=pl.BlockSpec((bm, bn), lambda j, i, o: (i, o[first + j // pb] * pb + j % pb)),
        out_shape=jax.ShapeDtypeStruct((s, N_CHIPS * n), F32),
        input_output_aliases={} if prev is None else {3: 0},
        compiler_params=_params("arbitrary", "arbitrary"),
    )(*ins)


def _place():
    x, y, c = lax.axis_index("x"), lax.axis_index("y"), lax.axis_index("c")
    chips = [(1 - x, y), (x, 1 - y), (1 - x, 1 - y)]
    return x, y, c, chips


def _all_gather8(block, name):
    def body(x_ref, out_ref, send_sems, recv_sems, local_sem):
        x, y, c, chips = _place()
        me, sibling = (x, y, c), (x, y, 1 - c)

        def slot(px, py, pc):
            return out_ref.at[4 * px + 2 * py + pc]

        def copy(k, blk, to, src=None):
            return pltpu.make_async_remote_copy(
                src_ref=slot(*blk) if src is None else src, dst_ref=slot(*blk),
                send_sem=send_sems.at[k], recv_sem=recv_sems.at[k], device_id=to, device_id_type=MESH)

        mine = pltpu.make_async_copy(x_ref, slot(*me), local_sem)
        mine.start()
        first = [copy(0, me, sibling, src=x_ref)]
        first += [copy(1 + j, me, (*chip, c), src=x_ref) for j, chip in enumerate(chips)]
        for cp in first:
            cp.start()
        passed = [copy(4 + j, (*chip, c), sibling) for j, chip in enumerate(chips)]
        for j, chip in enumerate(chips):
            copy(1 + j, (*chip, c), me).wait_recv()
            passed[j].start()
        copy(0, sibling, me).wait_recv()
        for j, chip in enumerate(chips):
            copy(4 + j, (*chip, 1 - c), me).wait_recv()
        for cp in first + passed:
            cp.wait_send()
        mine.wait()

    return _pcall(
        body, name=name, out_shape=jax.ShapeDtypeStruct((N_DEV,) + block.shape, block.dtype),
        in_specs=[pl.BlockSpec(memory_space=pltpu.VMEM)], out_specs=pl.BlockSpec(memory_space=pltpu.VMEM),
        scratch_shapes=[pltpu.SemaphoreType.DMA((7,)), pltpu.SemaphoreType.DMA((7,)), pltpu.SemaphoreType.DMA],
        compiler_params=pltpu.CompilerParams(vmem_limit_bytes=VMEM_LIMIT_BYTES),
    )(block)


HBM_SPEC = pl.BlockSpec(memory_space=pltpu.HBM)
SEM_SPEC = pl.BlockSpec(memory_space=pltpu.SEMAPHORE)
ANY_SPEC = pl.BlockSpec(memory_space=pl.ANY)
EFFECT = pltpu.SideEffectType.DATAFLOW_SIDE_EFFECTING


def _xfer_start(name, bufs, plan, n_copies, after_last=False):
    nb = len(bufs)
    deps = (_Order.last,) if after_last and _Order.last is not None else ()
    nd = len(deps)

    def body(*refs):
        send_sems, recv_sems = refs[nb + nd], refs[nb + nd + 1]
        token = refs[nb + nd + 2 + nb]
        for k, (src, dst, dev) in enumerate(plan(refs[:nb], *_place())):
            pltpu.make_async_remote_copy(src_ref=src, dst_ref=dst, send_sem=send_sems.at[k], recv_sem=recv_sems.at[k],
                                         device_id=dev, device_id_type=MESH).start()
        token[...] = jnp.zeros_like(token)

    outs = pl.pallas_call(
        body, name=name,
        out_shape=(pltpu.SemaphoreType.DMA((n_copies,)), pltpu.SemaphoreType.DMA((n_copies,)),
                   *[pltpu.HBM(b.shape, b.dtype) for b in bufs], jax.ShapeDtypeStruct((8, LANES), F32)),
        in_specs=[HBM_SPEC] * nb + [ANY_SPEC] * nd,
        out_specs=(SEM_SPEC, SEM_SPEC, *[HBM_SPEC] * nb, pl.BlockSpec(memory_space=pltpu.VMEM)),
        input_output_aliases={i: 2 + i for i in range(nb)},
        compiler_params=pltpu.CompilerParams(has_side_effects=EFFECT),
    )(*[pltpu.with_memory_space_constraint(b, pltpu.HBM) for b in bufs], *deps)
    _Order.last = outs[-1]
    return (outs[0], outs[1]), list(outs[2:2 + nb])


def _xfer_wait(name, sems, bufs, plan):
    nb = len(bufs)

    def body(*refs):
        send_sems, recv_sems = refs[nb], refs[nb + 1]
        for k, (src, dst, dev) in enumerate(plan(refs[:nb], *_place())):
            copy = pltpu.make_async_remote_copy(src_ref=src, dst_ref=dst, send_sem=send_sems.at[k],
                                                recv_sem=recv_sems.at[k], device_id=dev, device_id_type=MESH)
            copy.wait_send()
            copy.wait_recv()

    outs = pl.pallas_call(
        body, name=name, out_shape=tuple(pltpu.HBM(b.shape, b.dtype) for b in bufs),
        in_specs=[HBM_SPEC] * nb + [SEM_SPEC, SEM_SPEC, ANY_SPEC], out_specs=tuple([HBM_SPEC] * nb),
        input_output_aliases={i: i for i in range(nb)},
        compiler_params=pltpu.CompilerParams(has_side_effects=EFFECT),
    )(*bufs, *sems, _Order.last)
    _Order.last = outs[0]
    return list(outs)


def _xfer_hand_over(name, sems, bufs, plan, next_plans):
    nb, n_next = len(bufs), len(next_plans)

    def body(*refs):
        send_sems, recv_sems = refs[nb], refs[nb + 1]
        outs = refs[nb + 3:]
        place = _place()
        for k, (src, dst, dev) in enumerate(plan(refs[:nb], *place)):
            copy = pltpu.make_async_remote_copy(src_ref=src, dst_ref=dst, send_sem=send_sems.at[k],
                                                recv_sem=recv_sems.at[k], device_id=dev, device_id_type=MESH)
            copy.wait_send()
            copy.wait_recv()
        for p, (next_plan, _) in enumerate(next_plans):
            for k, (src, dst, dev) in enumerate(next_plan(refs[:nb], *place)):
                pltpu.make_async_remote_copy(src_ref=src, dst_ref=dst, send_sem=outs[2 * p].at[k],
                                             recv_sem=outs[2 * p + 1].at[k], device_id=dev, device_id_type=MESH).start()
        outs[-1][...] = jnp.zeros_like(outs[-1])

    sem_shapes = [pltpu.SemaphoreType.DMA((copies,)) for _, copies in next_plans for _ in range(2)]
    outs = pl.pallas_call(
        body, name=name,
        out_shape=(*sem_shapes, *[pltpu.HBM(b.shape, b.dtype) for b in bufs], jax.ShapeDtypeStruct((8, LANES), F32)),
        in_specs=[HBM_SPEC] * nb + [SEM_SPEC, SEM_SPEC, ANY_SPEC],
        out_specs=(*[SEM_SPEC] * (2 * n_next), *[HBM_SPEC] * nb, pl.BlockSpec(memory_space=pltpu.VMEM)),
        input_output_aliases={i: 2 * n_next + i for i in range(nb)},
        compiler_params=pltpu.CompilerParams(has_side_effects=EFFECT),
    )(*bufs, *sems, _Order.last)
    _Order.last = outs[-1]
    return [(outs[2 * p], outs[2 * p + 1]) for p in range(n_next)], list(outs[2 * n_next:2 * n_next + nb])


def _half(ref, c, axis):
    rows = ref.shape[axis] // 2
    return pl.ds(c * rows, rows)


def _plan_weights_ici(n):
    def plan(refs, x, y, c, chips):
        out = []
        for w in range(n):
            region = refs[w].at[2 * x + y, _half(refs[w], c, 1), :]
            out += [(region, region, (*chip, c)) for chip in chips]
        return out
    return plan


def _plan_weights_ring(n):
    def plan(refs, x, y, c, chips):
        out = []
        for w in range(n):
            region = refs[w].at[2 * x + y, _half(refs[w], c, 1), :]
            out += [(region, region, (*chip, c)) for chip in chips[:2]]
        return out
    return plan


def _plan_weights_relay(n):
    def plan(refs, x, y, c, chips):
        out = []
        for w in range(n):
            quarter_rows = refs[w].shape[1] // 4
            upper = refs[w].at[2 * x + (1 - y), pl.ds(2 * c * quarter_rows, quarter_rows), :]
            lower = refs[w].at[2 * (1 - x) + y, pl.ds((2 * c + 1) * quarter_rows, quarter_rows), :]
            out += [(upper, upper, (1 - x, y, c)), (lower, lower, (x, 1 - y, c))]
        return out
    return plan


def _plan_weights_d2d(n, which=slice(0, 3)):
    def plan(refs, x, y, c, chips):
        out = []
        for w in range(n):
            rows = _half(refs[w], c, 1)
            for chip in chips[which]:
                region = refs[w].at[2 * chip[0] + chip[1], rows, :]
                out.append((region, region, (x, y, 1 - c)))
        return out
    return plan


def _plan_gather_out(refs, x, y, c, chips):
    mine = refs[0].at[4 * x + 2 * y + c]
    return [(mine, mine, (x, y, 1 - c))] + [(mine, mine, (*chip, c)) for chip in chips]


def _plan_gather_on(refs, x, y, c, chips):
    out = []
    for chip in chips:
        block = refs[0].at[4 * chip[0] + 2 * chip[1] + c]
        out.append((block, block, (x, y, 1 - c)))
    return out


def _plan_grads_d2d(n):
    def plan(refs, x, y, c, chips):
        return [(refs[w].at[:, _half(refs[w], 1 - c, 1), :], refs[n + w], (x, y, 1 - c)) for w in range(n)]
    return plan


def _plan_grads_ici(n):
    def plan(refs, x, y, c, chips):
        out = []
        for w in range(n):
            out += [(refs[w].at[2 * chip[0] + chip[1]], refs[n + w].at[2 * x + y], (*chip, c)) for chip in chips]
        return out
    return plan


def _plan_final_d2d(n):
    def plan(refs, x, y, c, chips):
        out = []
        for w in range(n):
            region = refs[w].at[_half(refs[w], c, 0), :]
            out.append((region, region, (x, y, 1 - c)))
        return out
    return plan


def _stream_blocks(hr, cols):
    bc = cols if cols <= 4096 else _pick(cols, 4096)
    return _pick_rows(hr, max(16, (768 * 1024) // bc)), bc


def _pre_reduce(g, landed, place, name):
    _, rows, cols = g.shape
    hr = rows // 2
    rb, bc = _stream_blocks(hr, cols)
    nrb = hr // rb

    def body(place_ref, g_ref, l_ref, o_ref):
        o_ref[...] = (g_ref[...].astype(F32) + l_ref[...].astype(F32)).astype(o_ref.dtype)

    return _pcall(
        body, name=name, num_scalar_prefetch=1, grid=(N_CHIPS, nrb, cols // bc),
        in_specs=[pl.BlockSpec((None, rb, bc), lambda j, i, k, p: (j, p[1] * nrb + i, k)),
                  pl.BlockSpec((None, rb, bc), lambda j, i, k, p: (j, i, k))],
        out_specs=pl.BlockSpec((None, rb, bc), lambda j, i, k, p: (j, i, k)),
        out_shape=jax.ShapeDtypeStruct((N_CHIPS, hr, cols), g.dtype),
        compiler_params=_params("parallel", "parallel", "parallel"),
    )(place, g, landed)


def _small_group(arrays):
    return all(a.shape[1] * a.shape[2] * a.dtype.itemsize <= 2 * 1024 * 1024 for a in arrays)


def _pre_reduce_group(gs, landeds, place, name):
    n = len(gs)

    def body(place_ref, *refs):
        for w in range(n):
            refs[2 * n + w][...] = (refs[w][...].astype(F32) + refs[n + w][...].astype(F32)).astype(BF16)

    halves = [(g.shape[1] // 2, g.shape[2]) for g in gs]
    return _pcall(
        body, name=name, num_scalar_prefetch=1, grid=(N_CHIPS,),
        in_specs=[pl.BlockSpec((None, hr, cols), lambda j, p: (j, p[1], 0)) for hr, cols in halves]
        + [pl.BlockSpec((None, hr, cols), lambda j, p: (j, 0, 0)) for hr, cols in halves],
        out_specs=[pl.BlockSpec((None, hr, cols), lambda j, p: (j, 0, 0)) for hr, cols in halves],
        out_shape=[jax.ShapeDtypeStruct((N_CHIPS, hr, cols), BF16) for hr, cols in halves],
        compiler_params=_params("parallel"),
    )(place, *gs, *landeds)


def _sum_slots_group(mines, landeds, place, name):
    n = len(mines)

    def body(place_ref, *refs):
        chip = place_ref[0]
        for w in range(n):
            own, l_ref = refs[w][...].astype(F32), refs[n + w]
            total = jnp.where(chip == 0, own, l_ref[0].astype(F32))
            for j in range(1, N_CHIPS):
                total = total + jnp.where(chip == j, own, l_ref[j].astype(F32))
            refs[2 * n + w][...] = total

    shapes = [(a.shape[1], a.shape[2]) for a in mines]
    return _pcall(
        body, name=name, num_scalar_prefetch=1, grid=(1,),
        in_specs=[pl.BlockSpec((None, hr, cols), lambda i, p: (p[0], 0, 0)) for hr, cols in shapes]
        + [pl.BlockSpec((N_CHIPS, hr, cols), lambda i, p: (0, 0, 0)) for hr, cols in shapes],
        out_specs=[pl.BlockSpec((hr, cols), lambda i, p: (p[1], 0)) for hr, cols in shapes],
        out_shape=[jax.ShapeDtypeStruct((2 * hr, cols), F32) for hr, cols in shapes],
        compiler_params=_params("arbitrary"),
    )(place, *mines, *landeds)


def _sum_slots(mine, landed, place, name):
    _, hr, cols = mine.shape
    rb, bc = _stream_blocks(hr, cols)
    rb = _pick_rows(hr, max(16, rb // 2))
    nrb = hr // rb

    def body(place_ref, m_ref, l_ref, o_ref):
        chip = place_ref[0]
        own = m_ref[...].astype(F32)
        total = jnp.where(chip == 0, own, l_ref[0].astype(F32))
        for j in range(1, N_CHIPS):
            total = total + jnp.where(chip == j, own, l_ref[j].astype(F32))
        o_ref[...] = total

    return _pcall(
        body, name=name, num_scalar_prefetch=1, grid=(nrb, cols // bc),
        in_specs=[pl.BlockSpec((None, rb, bc), lambda i, k, p: (p[0], i, k)),
                  pl.BlockSpec((N_CHIPS, rb, bc), lambda i, k, p: (0, i, k))],
        out_specs=pl.BlockSpec((rb, bc), lambda i, k, p: (p[1] * nrb + i, k)),
        out_shape=jax.ShapeDtypeStruct((2 * hr, cols), F32),
        compiler_params=_params("parallel", "parallel"),
    )(place, mine, landed)


class _WeightGather:
    @staticmethod
    def zones(quarters, chip):
        return [lax.dynamic_update_slice(lax.empty((N_CHIPS,) + q.shape, BF16), q.astype(BF16)[None], (chip, 0, 0))
                for q in quarters]

    def __init__(self, tag, zones, ring=False):
        self.tag, self.n, self.ring = tag, len(zones), ring
        self.plan = _plan_weights_ring(self.n) if ring else _plan_weights_ici(self.n)
        self.sems, self.bufs = _xfer_start("wici_start_" + tag, zones, self.plan, (2 if ring else 3) * self.n,
                                           after_last=True)

    def relay(self):
        plan = _plan_weights_relay(self.n)
        (self.sems,), self.bufs = _xfer_hand_over("wrel_start_" + self.tag, self.sems, self.bufs, self.plan,
                                                   [(plan, 2 * self.n)])
        self.plan = plan

    def pass_on(self):
        plan = _plan_weights_d2d(self.n)
        (self.sems,), self.bufs = _xfer_hand_over("wd2d_start_" + self.tag, self.sems, self.bufs, self.plan,
                                                   [(plan, 3 * self.n)])
        self.plan = plan

    def done(self):
        return _xfer_wait("wd2d_wait_" + self.tag, self.sems, self.bufs, self.plan)

    def relay_and_pass_near(self):
        plan, self.near_plan = _plan_weights_relay(self.n), _plan_weights_d2d(self.n, slice(0, 2))
        (self.sems, self.near_sems), self.bufs = _xfer_hand_over(
            "wrel_start_" + self.tag, self.sems, self.bufs, self.plan, [(plan, 2 * self.n), (self.near_plan, 2 * self.n)])
        self.plan = plan

    def near_done(self):
        self.bufs = _xfer_wait("wnear_wait_" + self.tag, self.near_sems, self.bufs, self.near_plan)
        return self.bufs

    def pass_far(self):
        plan = _plan_weights_d2d(self.n, slice(2, 3))
        (sems,), bufs = _xfer_hand_over("wfar_start_" + self.tag, self.sems, self.bufs, self.plan, [(plan, self.n)])
        return _xfer_wait("wfar_wait_" + self.tag, sems, bufs, plan)


class _GradReduce:
    def __init__(self, tag, grads):
        self.tag, self.n = tag, len(grads)
        zones = [lax.empty((N_CHIPS, g.shape[1] // 2, g.shape[2]), g.dtype) for g in grads]
        self.plan = _plan_grads_d2d(self.n)
        self.sems, self.bufs = _xfer_start("gd2d_start_" + tag, list(grads) + zones, self.plan, self.n)

    def pair(self, place):
        n = self.n
        bufs = _xfer_wait("gd2d_wait_" + self.tag, self.sems, self.bufs, self.plan)
        if n > 1 and _small_group(bufs[n:]):
            self.halves = list(_pre_reduce_group(bufs[:n], bufs[n:], place, "pre_reduce_" + self.tag))
        else:
            self.halves = [_pre_reduce(bufs[w], bufs[n + w], place, f"pre_reduce_{self.tag}{w}") for w in range(n)]

    def cross(self, after_last=False):
        zones = [lax.empty(h.shape, h.dtype) for h in self.halves]
        self.plan = _plan_grads_ici(self.n)
        self.sems, self.bufs = _xfer_start("gici_start_" + self.tag, self.halves + zones, self.plan, 3 * self.n,
                                           after_last=after_last)

    def step(self, place):
        self.pair(place)
        self.cross()

    def join(self, place):
        n = self.n
        bufs = _xfer_wait("gici_wait_" + self.tag, self.sems, self.bufs, self.plan)
        if n > 1 and _small_group(bufs[:n]):
            sums = list(_sum_slots_group(bufs[:n], bufs[n:], place, "sum_slots_" + self.tag))
        else:
            sums = [_sum_slots(bufs[w], bufs[n + w], place, f"sum_slots_{self.tag}{w}") for w in range(n)]
        self.plan = _plan_final_d2d(n)
        self.sems, self.bufs = _xfer_start("gfin_start_" + self.tag, sums, self.plan, n)

    def done(self):
        return _xfer_wait("gfin_wait_" + self.tag, self.sems, self.bufs, self.plan)


def _ada_fwd(c_all, w_q, b_q):
    d, n = w_q.shape
    bn = _pick(n, 512)

    def body(c_ref, w_ref, b_ref, o_ref):
        cv = c_ref[...]
        act = cv * _sigmoid(cv)
        o_ref[...] = _dot(act, w_ref[...], NN, lax.Precision.HIGHEST) + b_ref[...]

    return _pcall(
        body, name="ada_fwd", grid=(n // bn,),
        in_specs=[pl.BlockSpec((N_DEV, d), lambda j: (0, 0)), pl.BlockSpec((d, bn), lambda j: (0, j)),
                  pl.BlockSpec((1, bn), lambda j: (0, j))],
        out_specs=pl.BlockSpec((N_DEV, bn), lambda j: (0, j)),
        out_shape=jax.ShapeDtypeStruct((N_DEV, n), F32), compiler_params=_params("parallel"),
    )(c_all, w_q, b_q)


def _row_spec(rb, width, col=0):
    return pl.BlockSpec((rb, width), lambda i, col=col: (i, col))


def _vec_spec(width, col=0):
    return pl.BlockSpec((1, width), lambda i, col=col: (0, col))


def _norm_mod_fwd(x, g, sc, sh, name, res=None, gt=None):
    s, d = x.shape
    rb = _pick(s, 256)
    has_res = res is not None

    def body(*refs):
        if has_res:
            x_ref, res_ref, gt_ref, g_ref, sc_ref, sh_ref, x1_ref, h_ref = refs
            xv = x_ref[...] + gt_ref[...] * res_ref[...]
            x1_ref[...] = xv
        else:
            x_ref, g_ref, sc_ref, sh_ref, h_ref = refs
            xv = x_ref[...]
        xh, _ = _rms(xv)
        h_ref[...] = (xh * g_ref[...] * (1.0 + sc_ref[...]) + sh_ref[...]).astype(BF16)

    row, vec = _row_spec(rb, d), _vec_spec(d)
    if has_res:
        ins, in_specs = (x, res, gt, g, sc, sh), [row, row, vec, vec, vec, vec]
        out_shape = [jax.ShapeDtypeStruct((s, d), F32), jax.ShapeDtypeStruct((s, d), BF16)]
        out_specs = [row, row]
    else:
        ins, in_specs = (x, g, sc, sh), [row, vec, vec, vec]
        out_shape, out_specs = jax.ShapeDtypeStruct((s, d), BF16), row
    return _pcall(body, name=name, grid=(s // rb,), in_specs=in_specs, out_specs=out_specs,
                          out_shape=out_shape, compiler_params=_params("parallel"))(*ins)


def _final_loss(x1, f, gt2, g_final, target):
    s, d = x1.shape
    rb = _pick(s, 256)

    def body(x1_ref, f_ref, gt_ref, g_ref, t_ref, dx_ref, df_ref, loss_ref, dg_ref, dgt_ref):
        first = pl.program_id(0) == 0
        fv, gt, gv = f_ref[...], gt_ref[...], g_ref[...]
        x2 = x1_ref[...] + gt * fv
        xh, r = _rms(x2)
        err = xh * gv - t_ref[...]
        blk = 0.5 * jnp.sum(jnp.sum(err * err, axis=1, keepdims=True), axis=0, keepdims=True) / d
        dy = err / d
        dxh = dy * gv
        dx = r * (dxh - xh * jnp.mean(dxh * xh, axis=-1, keepdims=True))
        dx_ref[...] = dx
        df_ref[...] = (dx * gt).astype(BF16)
        _accumulate(first, loss_ref, jnp.broadcast_to(blk, (1, LANES)))
        _accumulate(first, dg_ref, _colsum(dy * xh))
        _accumulate(first, dgt_ref, _colsum(dx * fv))

    row, vec = _row_spec(rb, d), _vec_spec(d)
    return _pcall(
        body, name="final_loss", grid=(s // rb,), in_specs=[row, row, vec, vec, row],
        out_specs=[row, row, _vec_spec(LANES), vec, vec],
        out_shape=[jax.ShapeDtypeStruct((s, d), F32), jax.ShapeDtypeStruct((s, d), BF16),
                   jax.ShapeDtypeStruct((1, LANES), F32), jax.ShapeDtypeStruct((1, d), F32),
                   jax.ShapeDtypeStruct((1, d), F32)],
        compiler_params=_params("arbitrary"),
    )(x1, f, gt2, g_final, target)


def _norm_mod_bwd(dh, xin, dres, g, sc, name, branch=None, gt=None):
    s, d = xin.shape
    rb = _pick(s, 256)
    has_branch = branch is not None

    def body(*refs):
        if has_branch:
            dh_ref, x_ref, dres_ref, g_ref, sc_ref, br_ref, gt_ref, dx_ref, dbr_ref, dsh_ref, dsc_ref, dg_ref, dgt_ref = refs
        else:
            dh_ref, x_ref, dres_ref, g_ref, sc_ref, dx_ref, dsh_ref, dsc_ref, dg_ref = refs
        first = pl.program_id(0) == 0
        gv = g_ref[...]
        xh, r = _rms(x_ref[...])
        dhv = dh_ref[...]
        dn = dhv * (1.0 + sc_ref[...])
        dxh = dn * gv
        dx = dres_ref[...] + r * (dxh - xh * jnp.mean(dxh * xh, axis=-1, keepdims=True))
        dx_ref[...] = dx
        _accumulate(first, dsh_ref, _colsum(dhv))
        _accumulate(first, dsc_ref, _colsum(dhv * xh * gv))
        _accumulate(first, dg_ref, _colsum(dn * xh))
        if has_branch:
            dbr_ref[...] = (dx * gt_ref[...]).astype(BF16)
            _accumulate(first, dgt_ref, _colsum(dx * br_ref[...]))

    row, vec = _row_spec(rb, d), _vec_spec(d)
    vec_shape = jax.ShapeDtypeStruct((1, d), F32)
    if has_branch:
        ins, in_specs = (dh, xin, dres, g, sc, branch, gt), [row, row, row, vec, vec, row, vec]
        out_specs = [row, row, vec, vec, vec, vec]
        out_shape = [jax.ShapeDtypeStruct((s, d), F32), jax.ShapeDtypeStruct((s, d), BF16)] + [vec_shape] * 4
    else:
        ins, in_specs = (dh, xin, dres, g, sc), [row, row, row, vec, vec]
        out_specs = [row, vec, vec, vec]
        out_shape = [jax.ShapeDtypeStruct((s, d), F32)] + [vec_shape] * 3
    return _pcall(body, name=name, grid=(s // rb,), in_specs=in_specs, out_specs=out_specs,
                          out_shape=out_shape, compiler_params=_params("arbitrary"))(*ins)


def _branch_merge(ya, yb, w_bg, w_bh, z, b_gate, gate_col, bm=512, bn=1024):
    s, gw = ya.shape
    d = w_bg.shape[1]
    bm, bn = _pick(s, bm), _pick(d, bn)
    nj = d // bn

    def body(ya_ref, yb_ref, wa_ref, wb_ref, za_ref, zb_ref, ba_ref, bb_ref, pa_ref, pb_ref, y_ref):
        pa = _dot(ya_ref[...], wa_ref[...])
        pb = _dot(yb_ref[...], wb_ref[...])
        pa_ref[...] = pa.astype(BF16)
        pb_ref[...] = pb.astype(BF16)
        ga = _sigmoid(za_ref[...] + ba_ref[...])
        gb = _sigmoid(zb_ref[...] + bb_ref[...])
        y_ref[...] = (ga * pa + gb * pb).astype(BF16)

    act = pl.BlockSpec((bm, gw), lambda j, i: (i, 0))
    wgt = pl.BlockSpec((gw, bn), lambda j, i: (0, j))
    tile = pl.BlockSpec((bm, bn), lambda j, i: (i, j))
    shape = jax.ShapeDtypeStruct((s, d), BF16)
    return _pcall(
        body, name="branch_merge", grid=(nj, s // bm),
        in_specs=[act, act, wgt, wgt,
                  pl.BlockSpec((bm, bn), lambda j, i: (i, gate_col * nj + j)),
                  pl.BlockSpec((bm, bn), lambda j, i: (i, (gate_col + 1) * nj + j)),
                  pl.BlockSpec((1, bn), lambda j, i: (0, j)), pl.BlockSpec((1, bn), lambda j, i: (0, nj + j))],
        out_specs=[tile, tile, tile], out_shape=[shape, shape, shape], compiler_params=_params("parallel", "parallel"),
    )(ya, yb, w_bg, w_bh, z, z, b_gate, b_gate)


def _branch_wgrad(ya, yb, dpa, dpb):
    s, gw = ya.shape
    per = dpa.shape[1] // N_CHIPS

    def body(ya_ref, yb_ref, da_ref, db_ref, ga_ref, gb_ref):
        ga_ref[...] = _dot(ya_ref[...], da_ref[...], TN).astype(BF16)
        gb_ref[...] = _dot(yb_ref[...], db_ref[...], TN).astype(BF16)

    act = pl.BlockSpec((s, gw), lambda j: (0, 0))
    grd = pl.BlockSpec((s, per), lambda j: (0, j))
    out = pl.BlockSpec((None, gw, per), lambda j: (j, 0, 0))
    shape = jax.ShapeDtypeStruct((N_CHIPS, gw, per), BF16)
    return _pcall(body, name="branch_wgrad", grid=(N_CHIPS,), in_specs=[act, act, grd, grd], out_specs=[out, out],
                  out_shape=[shape, shape], compiler_params=_params("parallel"))(ya, yb, dpa, dpb)


def _branch_bwd(dpa, dpb, w_bg, w_bh, bm=512):
    s, d = dpa.shape
    gw = w_bg.shape[0]
    bm = _pick(s, bm)

    def body(da_ref, db_ref, wa_ref, wb_ref, oa_ref, ob_ref):
        oa_ref[...] = _dot(da_ref[...], wa_ref[...], NT)
        ob_ref[...] = _dot(db_ref[...], wb_ref[...], NT)

    act = pl.BlockSpec((bm, d), lambda i: (i, 0))
    wgt = pl.BlockSpec((gw, d), lambda i: (0, 0))
    out = pl.BlockSpec((bm, gw), lambda i: (i, 0))
    shape = jax.ShapeDtypeStruct((s, gw), F32)
    return _pcall(body, name="branch_bwd", grid=(s // bm,), in_specs=[act, act, wgt, wgt], out_specs=[out, out],
                  out_shape=[shape, shape], compiler_params=_params("parallel"))(dpa, dpb, w_bg, w_bh)


def _out_bwd(dyo, w_out, pa, pb, z, b_gate, gate_col, bm=512, bn=1024):
    s, d = dyo.shape
    bm, bn = _pick(s, bm), _pick(d, bn)
    nj = d // bn

    def body(dyo_ref, w_ref, pa_ref, pb_ref, za_ref, zb_ref, ba_ref, bb_ref,
             dpa_ref, dpb_ref, dza_ref, dzb_ref, dba_ref, dbb_ref):
        first = pl.program_id(1) == 0
        dyv = _dot(dyo_ref[...], w_ref[...], NT)
        ga = _sigmoid(za_ref[...] + ba_ref[...])
        gb = _sigmoid(zb_ref[...] + bb_ref[...])
        dpa_ref[...] = (dyv * ga).astype(BF16)
        dpb_ref[...] = (dyv * gb).astype(BF16)
        dga = dyv * pa_ref[...].astype(F32) * ga * (1.0 - ga)
        dgb = dyv * pb_ref[...].astype(F32) * gb * (1.0 - gb)
        dza_ref[...] = dga.astype(BF16)
        dzb_ref[...] = dgb.astype(BF16)
        _accumulate(first, dba_ref, _colsum(dga))
        _accumulate(first, dbb_ref, _colsum(dgb))

    tile = pl.BlockSpec((bm, bn), lambda j, i: (i, j))
    vec = pl.BlockSpec((1, bn), lambda j, i: (0, j))
    act, vec_shape = jax.ShapeDtypeStruct((s, d), BF16), jax.ShapeDtypeStruct((1, d), F32)
    return _pcall(
        body, name="out_bwd", grid=(nj, s // bm),
        in_specs=[pl.BlockSpec((bm, d), lambda j, i: (i, 0)), pl.BlockSpec((bn, d), lambda j, i: (j, 0)), tile, tile,
                  pl.BlockSpec((bm, bn), lambda j, i: (i, gate_col * nj + j)),
                  pl.BlockSpec((bm, bn), lambda j, i: (i, (gate_col + 1) * nj + j)),
                  vec, pl.BlockSpec((1, bn), lambda j, i: (0, nj + j))],
        out_specs=[tile, tile, tile, tile, vec, vec], out_shape=[act, act, act, act, vec_shape, vec_shape],
        compiler_params=_params("parallel", "arbitrary"),
    )(dyo, w_out, pa, pb, z, z, b_gate, b_gate)


def _ffn_in(h, w_fi, bm=512):
    s, d = h.shape
    per = w_fi.shape[2]
    ff = 2 * per
    bm, bn = _pick(s, bm), _half_width(per)
    per_blocks = per // bn

    def body(h_ref, wa_ref, wu_ref, a_ref, u_ref, hf_ref):
        hv = h_ref[...]
        a = _dot(hv, wa_ref[...])
        up = _dot(hv, wu_ref[...])
        a_ref[...] = a.astype(BF16)
        u_ref[...] = up.astype(BF16)
        hf_ref[...] = (a * _sigmoid(a) * up).astype(BF16)

    out = pl.BlockSpec((bm, bn), lambda j, i: (i, j))
    shape = jax.ShapeDtypeStruct((s, ff), BF16)
    return _pcall(
        body, name="ffn_in", grid=(ff // bn, s // bm),
        in_specs=[pl.BlockSpec((bm, d), lambda j, i: (i, 0)),
                  pl.BlockSpec((None, d, bn), lambda j, i: (j // per_blocks, 0, j % per_blocks)),
                  pl.BlockSpec((None, d, bn), lambda j, i: (2 + j // per_blocks, 0, j % per_blocks))],
        out_specs=[out, out, out], out_shape=[shape, shape, shape],
        compiler_params=_params("parallel", "parallel"),
    )(h, w_fi, w_fi)


def _ffn_out_bwd(dffn, w_fo, a_act, up_act, bm=512):
    s, d = dffn.shape
    ff = w_fo.shape[0]
    bm, bn = _pick(s, bm), _half_width(ff // 2)

    def body(d_ref, w_ref, a_ref, u_ref, da_ref, du_ref):
        dhf = _dot(d_ref[...], w_ref[...], NT)
        a = a_ref[...].astype(F32)
        sa = _sigmoid(a)
        da_ref[...] = (dhf * u_ref[...].astype(F32) * sa * (1.0 + a * (1.0 - sa))).astype(BF16)
        du_ref[...] = (dhf * a * sa).astype(BF16)

    tile = pl.BlockSpec((bm, bn), lambda j, i: (i, j))
    shape = jax.ShapeDtypeStruct((s, ff), BF16)
    return _pcall(
        body, name="ffn_out_bwd", grid=(ff // bn, s // bm),
        in_specs=[pl.BlockSpec((bm, d), lambda j, i: (i, 0)), pl.BlockSpec((bn, d), lambda j, i: (j, 0)), tile, tile],
        out_specs=[tile, tile], out_shape=[shape, shape], compiler_params=_params("parallel", "parallel"),
    )(dffn, w_fo, a_act, up_act)


def _tril(n):
    return lax.broadcasted_iota(jnp.int32, (n, n), 0) >= lax.broadcasted_iota(jnp.int32, (n, n), 1)


def _gmlp_norm(v, ln_g, ln_b):
    gv = _gelu(v)
    mu = jnp.mean(gv, axis=-1, keepdims=True)
    cen = gv - mu
    rs = lax.rsqrt(jnp.mean(cen * cen, axis=-1, keepdims=True) + EPS)
    xh = cen * rs
    return xh, rs, xh * ln_g + ln_b


def _gmlp_fwd(z, ln_g, ln_b, ws, bs_t):
    s = z.shape[0]
    gw = ln_g.shape[1]
    groups, chunk, _ = ws.shape
    per = GMLP_CHUNKS_PER_STEP if s % (GMLP_CHUNKS_PER_STEP * chunk) == 0 else 1
    rows = per * chunk

    def body(u_ref, v_ref, lg_ref, lb_ref, ws_ref, bs_ref, ya_ref):
        mask = _tril(chunk)
        for cc in range(per):
            rs_ = slice(cc * chunk, (cc + 1) * chunk)
            gu = _gelu(u_ref[rs_, :])
            _, _, vn = _gmlp_norm(v_ref[rs_, :], lg_ref[...], lb_ref[...])
            for g in range(groups):
                cols = slice(g * LANES, (g + 1) * LANES)
                wm = jnp.where(mask, ws_ref[g], 0.0).astype(BF16)
                sg = _dot(wm, vn[:, cols].astype(BF16)) + bs_ref[:, g:g + 1]
                ya_ref[rs_, cols] = (gu[:, cols] * sg).astype(BF16)

    return _pcall(
        body, name="gmlp_fwd", grid=(s // rows,),
        in_specs=[_row_spec(rows, gw, 0), _row_spec(rows, gw, 1), _vec_spec(gw), _vec_spec(gw),
                  pl.BlockSpec((groups, chunk, chunk), lambda i: (0, 0, 0)), pl.BlockSpec((chunk, LANES), lambda i: (0, 0))],
        out_specs=_row_spec(rows, gw), out_shape=jax.ShapeDtypeStruct((s, gw), BF16),
        compiler_params=_params("parallel"),
    )(z, z, ln_g, ln_b, ws, bs_t)


def _gmlp_bwd(dya, z, ln_g, ln_b, ws, bs_t):
    s = z.shape[0]
    gw = ln_g.shape[1]
    groups, chunk, _ = ws.shape
    per = GMLP_CHUNKS_PER_STEP if s % (GMLP_CHUNKS_PER_STEP * chunk) == 0 else 1
    rows = per * chunk

    def body(dya_ref, u_ref, v_ref, lg_ref, lb_ref, ws_ref, bs_ref, duv_ref, dws_ref, dbs_ref, dlg_ref, dlb_ref, dvn_ref):
        first = pl.program_id(0) == 0
        lg = lg_ref[...]
        mask = _tril(chunk)
        lane = lax.broadcasted_iota(jnp.int32, (chunk, LANES), 1)
        dbs = jnp.zeros((chunk, LANES), F32)
        dws = [jnp.zeros((chunk, chunk), F32) for _ in range(groups)]
        dlb = jnp.zeros((1, gw), F32)
        dlg = jnp.zeros((1, gw), F32)
        for cc in range(per):
            rs_ = slice(cc * chunk, (cc + 1) * chunk)
            u, v = u_ref[rs_, :], v_ref[rs_, :]
            gu = _gelu(u)
            xh, rs, vn = _gmlp_norm(v, lg, lb_ref[...])
            dyav = dya_ref[rs_, :]
            for g in range(groups):
                cols = slice(g * LANES, (g + 1) * LANES)
                wm = jnp.where(mask, ws_ref[g], 0.0).astype(BF16)
                vg = vn[:, cols].astype(BF16)
                sg = _dot(wm, vg) + bs_ref[:, g:g + 1]
                ds = dyav[:, cols] * gu[:, cols]
                duv_ref[rs_, cols] = (dyav[:, cols] * sg * _gelu_grad(u[:, cols])).astype(BF16)
                dsb = ds.astype(BF16)
                dws[g] = dws[g] + jnp.where(mask, _dot(dsb, vg, NT), 0.0)
                dbs = dbs + jnp.where(lane == g, jnp.sum(ds, axis=-1, keepdims=True), 0.0)
                dvn_ref[rs_, cols] = _dot(wm, dsb, TN)
            dvn = dvn_ref[rs_, :]
            dlb = dlb + _colsum(dvn)
            dlg = dlg + _colsum(dvn * xh)
            dxh = dvn * lg
            dgv = rs * (dxh - jnp.mean(dxh, axis=-1, keepdims=True) - xh * jnp.mean(dxh * xh, axis=-1, keepdims=True))
            duv_ref[rs_, gw:] = (dgv * _gelu_grad(v)).astype(BF16)
        for g in range(groups):
            _accumulate(first, dws_ref.at[g], dws[g])
        _accumulate(first, dbs_ref, dbs)
        _accumulate(first, dlb_ref, dlb)
        _accumulate(first, dlg_ref, dlg)

    return _pcall(
        body, name="gmlp_bwd", grid=(s // rows,),
        in_specs=[_row_spec(rows, gw), _row_spec(rows, gw, 0), _row_spec(rows, gw, 1), _vec_spec(gw), _vec_spec(gw),
                  pl.BlockSpec((groups, chunk, chunk), lambda i: (0, 0, 0)), pl.BlockSpec((chunk, LANES), lambda i: (0, 0))],
        out_specs=[_row_spec(rows, 2 * gw), pl.BlockSpec((groups, chunk, chunk), lambda i: (0, 0, 0)),
                   pl.BlockSpec((chunk, LANES), lambda i: (0, 0)), _vec_spec(gw), _vec_spec(gw)],
        out_shape=[jax.ShapeDtypeStruct((s, 2 * gw), BF16), jax.ShapeDtypeStruct((groups, chunk, chunk), F32),
                   jax.ShapeDtypeStruct((chunk, LANES), F32), jax.ShapeDtypeStruct((1, gw), F32),
                   jax.ShapeDtypeStruct((1, gw), F32)],
        scratch_shapes=[pltpu.VMEM((rows, gw), F32)],
        compiler_params=_params("arbitrary"),
    )(dya, z, z, ln_g, ln_b, ws, bs_t)


def _lower_bound(lb_ref):
    a0, a1 = lb_ref[0:1, :], lb_ref[1:2, :]
    mx = jnp.maximum(a0, a1)
    e0, e1 = jnp.exp(a0 - mx), jnp.exp(a1 - mx)
    return e0 / (e0 + e1)


def _sum_dot(mask, x):
    hi = x.astype(BF16)
    rest = x - hi.astype(F32)
    mid = rest.astype(BF16)
    low = (rest - mid.astype(F32)).astype(BF16)
    return _dot(mask, hi) + _dot(mask, mid) + _dot(mask, low)


def _ones_where(mask):
    return jnp.where(mask, 1.0, 0.0).astype(BF16)


def _hg_masks(rows, t):
    r = lax.broadcasted_iota(jnp.int32, (rows, rows), 0)
    c = lax.broadcasted_iota(jnp.int32, (rows, rows), 1)
    same = (r // t) == (c // t)
    incl = jnp.logical_and(same, c <= r)
    upto_mid = jnp.logical_and(same, (c % t) <= t // 2)
    rev = jnp.logical_and(same, c >= r)
    return same, incl, upto_mid, rev


def _hg_block(q, fp, lb, masks):
    rows = q.shape[0]
    same, incl, upto_mid, _ = masks
    sig = _sigmoid(fp)
    f = lb + (1.0 - lb) * sig
    k = 1.0 - f
    sq = _sigmoid(q)
    qa = q * sq
    stacked = jnp.concatenate([_ones_where(m) for m in (incl, same, upto_mid)], axis=0)
    sums = _sum_dot(stacked, jnp.log(f))
    b, b_last, b_mid = sums[:rows], sums[rows:2 * rows], sums[2 * rows:]
    e_q = jnp.exp(jnp.minimum(b - b_mid, EXP_CLAMP))
    e_k = jnp.exp(jnp.minimum(b_mid - b, EXP_CLAMP))
    e_in = jnp.exp(b)
    e_out = jnp.exp(b_last - b)
    return dict(sig=sig, f=f, k=k, sq=sq, qa=qa, e_last=jnp.exp(b_last), e_q=e_q, e_k=e_k, e_in=e_in, e_out=e_out,
                q_hat=(qa * e_q).astype(BF16), k_hat=(k * e_k).astype(BF16),
                q_in=(qa * e_in).astype(BF16), k_out=k * e_out)


def _hgrn_fwd(z, hg_lb, norm_g, q_col):
    s = z.shape[0]
    hw = norm_g.shape[1]
    heads = hw // LANES
    t = HG_CHUNK
    rows = min(HG_ROWS, s)
    per_step = rows // t
    hp = min(HG_HEADS_PER_STEP, heads)
    assert heads % hp == 0 and q_col % hp == 0, (heads, q_col)
    wide = hp * LANES

    def zspec(which):
        return pl.BlockSpec((rows, wide), lambda h, r, which=which: (r, (q_col + which * heads) // hp + h))

    def body(q_ref, f_ref, i_ref, g_ref, lb_ref, ng_ref, yb_ref, o_ref, st_out_ref, st_ref, e_last_ref, inter_ref):
        @pl.when(pl.program_id(1) == 0)
        def _():
            st_ref[...] = jnp.zeros_like(st_ref)

        masks = _hg_masks(rows, t)
        for hh in range(hp):
            cols = slice(hh * LANES, (hh + 1) * LANES)
            blk = _hg_block(q_ref[:, cols], f_ref[:, cols], _lower_bound(lb_ref.at[:, cols]), masks)
            iv = i_ref[:, cols].astype(BF16)
            q_in, k_out = blk["q_in"], blk["k_out"].astype(BF16)
            e_last_ref[hh] = blk["e_last"]
            attn = jnp.where(masks[1], _dot(blk["q_hat"], blk["k_hat"], NT), 0.0).astype(BF16)
            o = _dot(attn, iv)
            grown = [_dot(iv[j * t:(j + 1) * t], k_out[j * t:(j + 1) * t], TN) for j in range(per_step)]
            st = st_ref[hh]
            for j in range(per_step):
                st_out_ref[hh, j] = st
                inter_ref[hh, j * t:(j + 1) * t, :] = _dot(q_in[j * t:(j + 1) * t], st.astype(BF16), NT)
                st = st * e_last_ref[hh, j * t:j * t + 1, :] + grown[j]
            st_ref[hh] = st
            o = o + inter_ref[hh]
            o_ref[:, cols] = o
            og = g_ref[:, cols]
            on, _ = _rms(o)
            yb_ref[:, cols] = (on * ng_ref[:, cols] * (og * _sigmoid(og))).astype(BF16)

    out_row = pl.BlockSpec((rows, wide), lambda h, r: (r, h))
    return _pcall(
        body, name="hgrn_fwd", grid=(heads // hp, s // rows),
        in_specs=[zspec(0), zspec(1), zspec(2), zspec(3),
                  pl.BlockSpec((2, wide), lambda h, r: (0, h)), pl.BlockSpec((1, wide), lambda h, r: (0, h))],
        out_specs=[out_row, out_row, pl.BlockSpec((hp, per_step, LANES, LANES), lambda h, r: (h, r, 0, 0))],
        out_shape=[jax.ShapeDtypeStruct((s, hw), BF16), jax.ShapeDtypeStruct((s, hw), F32),
                   jax.ShapeDtypeStruct((heads, s // t, LANES, LANES), F32)],
        scratch_shapes=[pltpu.VMEM((hp, LANES, LANES), F32), pltpu.VMEM((hp, rows, LANES), F32),
                        pltpu.VMEM((hp, rows, LANES), F32)],
        compiler_params=_params("parallel", "arbitrary"),
    )(z, z, z, z, hg_lb, norm_g)


def _hgrn_bwd(dyb, z, o_raw, states, hg_lb, norm_g, q_col):
    s = z.shape[0]
    hw = norm_g.shape[1]
    heads = hw // LANES
    t = HG_CHUNK
    rows = min(HG_ROWS, s)
    per_step = rows // t
    n_steps = s // rows
    hp = min(HG_HEADS_PER_STEP, heads)
    assert heads % hp == 0 and q_col % hp == 0, (heads, q_col)
    wide = hp * LANES

    def zspec(which):
        return pl.BlockSpec((rows, wide), lambda h, r, which=which: (n_steps - 1 - r, (q_col + which * heads) // hp + h))

    def body(dyb_ref, q_ref, f_ref, i_ref, g_ref, o_ref, st_in_ref, lb_ref, ng_ref,
             dq_ref, df_ref, di_ref, dg_ref, dlb_ref, dng_ref, dst_ref, acc_lb_ref, acc_ng_ref,
             e_last_ref, dq_in_ref, dk_out_ref, di_inter_ref, carry_ref):
        step = pl.program_id(1)

        @pl.when(step == 0)
        def _():
            dst_ref[...] = jnp.zeros_like(dst_ref)
            acc_lb_ref[...] = jnp.zeros_like(acc_lb_ref)
            acc_ng_ref[...] = jnp.zeros_like(acc_ng_ref)

        masks = _hg_masks(rows, t)
        same, incl, _, rev = masks
        sum_mask = jnp.concatenate([_ones_where(rev), _ones_where(same)], axis=1)
        for hh in range(hp):
            cols = slice(hh * LANES, (hh + 1) * LANES)
            lb = _lower_bound(lb_ref.at[:, cols])
            ng = ng_ref[:, cols]
            q = q_ref[:, cols]
            blk = _hg_block(q, f_ref[:, cols], lb, masks)
            iv = i_ref[:, cols].astype(BF16)
            o, og, dy = o_ref[:, cols], g_ref[:, cols], dyb_ref[:, cols]
            so = _sigmoid(og)
            on, r = _rms(o)
            acc_ng_ref[:, cols] += _colsum(dy * on * (og * so))
            dg_ref[:, cols] = (dy * on * ng * so * (1.0 + og * (1.0 - so))).astype(BF16)
            don = dy * ng * (og * so)
            do = (r * (don - on * jnp.mean(don * on, axis=-1, keepdims=True))).astype(BF16)
            q_hat, k_hat, q_in, k_out = blk["q_hat"], blk["k_hat"], blk["q_in"], blk["k_out"]
            k_out_b = k_out.astype(BF16)
            attn = jnp.where(incl, _dot(q_hat, k_hat, NT), 0.0).astype(BF16)
            d_attn = jnp.where(incl, _dot(do, iv, NT), 0.0).astype(BF16)
            di_intra = _dot(attn, do, TN)
            dq_hat = _dot(d_attn, k_hat)
            dk_hat = _dot(d_attn, q_hat, TN)
            e_last_ref[hh] = blk["e_last"]
            grown = [_dot(do[j * t:(j + 1) * t], q_in[j * t:(j + 1) * t], TN) for j in range(per_step)]
            dst = dst_ref[hh]
            for j in reversed(range(per_step)):
                rs_ = slice(j * t, (j + 1) * t)
                e_last = e_last_ref[hh, j * t:j * t + 1, :]
                st_prev, dst_b = st_in_ref[hh, j], dst.astype(BF16)
                dq_in_ref[hh, rs_, :] = _dot(do[rs_], st_prev.astype(BF16))
                dk_out_ref[hh, rs_, :] = _dot(iv[rs_], dst_b)
                di_inter_ref[hh, rs_, :] = _dot(k_out_b[rs_], dst_b, NT)
                carry_ref[hh, rs_, :] = jnp.broadcast_to(e_last * _colsum(st_prev * dst), (t, LANES))
                dst = dst * e_last + grown[j]
            dst_ref[hh] = dst
            di_ref[:, cols] = (di_intra + di_inter_ref[hh]).astype(BF16)
            dk_out = dk_out_ref[hh]
            dqa = dq_in_ref[hh] * blk["e_in"] + dq_hat * blk["e_q"]
            dk = dk_out * blk["e_out"] + dk_hat * blk["e_k"]
            db = blk["qa"] * dqa - blk["k"] * dk
            dlf = _sum_dot(sum_mask, jnp.concatenate([db, dk_out * k_out], axis=0)) + carry_ref[hh]
            dfv = dlf / blk["f"] - dk
            sig, sq = blk["sig"], blk["sq"]
            df_ref[:, cols] = (dfv * (1.0 - lb) * sig * (1.0 - sig)).astype(BF16)
            acc_lb_ref[:, cols] += _colsum(dfv * (1.0 - sig))
            dq_ref[:, cols] = (dqa * sq * (1.0 + q * (1.0 - sq))).astype(BF16)

        @pl.when(step == n_steps - 1)
        def _():
            lb = _lower_bound(lb_ref)
            d0 = acc_lb_ref[...] * lb * (1.0 - lb)
            dlb_ref[0:1, :] = d0
            dlb_ref[1:2, :] = -d0
            dng_ref[...] = acc_ng_ref[...]

    rev_row = pl.BlockSpec((rows, wide), lambda h, r: (n_steps - 1 - r, h))
    piece = jax.ShapeDtypeStruct((s, hw), BF16)
    return _pcall(
        body, name="hgrn_bwd", grid=(heads // hp, n_steps),
        in_specs=[rev_row, zspec(0), zspec(1), zspec(2), zspec(3), rev_row,
                  pl.BlockSpec((hp, per_step, LANES, LANES), lambda h, r: (h, n_steps - 1 - r, 0, 0)),
                  pl.BlockSpec((2, wide), lambda h, r: (0, h)), pl.BlockSpec((1, wide), lambda h, r: (0, h))],
        out_specs=[rev_row, rev_row, rev_row, rev_row,
                   pl.BlockSpec((2, wide), lambda h, r: (0, h)), pl.BlockSpec((1, wide), lambda h, r: (0, h))],
        out_shape=[piece, piece, piece, piece, jax.ShapeDtypeStruct((2, hw), F32), jax.ShapeDtypeStruct((1, hw), F32)],
        scratch_shapes=[pltpu.VMEM((hp, LANES, LANES), F32), pltpu.VMEM((1, wide), F32), pltpu.VMEM((1, wide), F32)]
        + [pltpu.VMEM((hp, rows, LANES), F32)] * 5,
        compiler_params=_params("parallel", "arbitrary"),
    )(dyb, z, z, z, z, o_raw, states, hg_lb, norm_g)


def _adam_update(w, m, v, g):
    m2 = ADAM_B1 * m + (1.0 - ADAM_B1) * g
    v2 = ADAM_B2 * v + (1.0 - ADAM_B2) * (g * g)
    m_hat = m2 * (1.0 / (1.0 - ADAM_B1 ** ADAM_STEP))
    v_hat = v2 * (1.0 / (1.0 - ADAM_B2 ** ADAM_STEP))
    return -ADAM_LR * (m_hat / (jnp.sqrt(v_hat) + ADAM_EPS) + ADAM_WD * w), m2, v2


def _adamw(w, m, v, parts, name, outer=False):
    rows, cols = w.shape
    bc = cols if cols <= 4096 else _pick(cols, 4096)
    rb = _pick_rows(rows, max(8, (384 * 1024) // bc), mult=8)
    if outer and rb % LANES:
        rb = rows

    def body(w_ref, m_ref, v_ref, *refs):
        g_ref, d_ref, mo_ref, vo_ref = refs[-4:]
        if outer:
            cv = refs[0][...]
            g = _dot(cv * _sigmoid(cv), refs[1][...], TN, lax.Precision.HIGHEST)
        else:
            p_ref = refs[0]
            g = p_ref[0].astype(F32)
            for p in range(1, p_ref.shape[0]):
                g = g + p_ref[p].astype(F32)
        g_ref[...] = g
        d_ref[...], mo_ref[...], vo_ref[...] = _adam_update(w_ref[...], m_ref[...], v_ref[...], g)

    blk = pl.BlockSpec((rb, bc), lambda i, j: (i, j))
    out = jax.ShapeDtypeStruct((rows, cols), F32)
    if outer:
        grad_specs = [pl.BlockSpec((N_DEV, rb), lambda i, j: (0, i)), pl.BlockSpec((N_DEV, bc), lambda i, j: (0, j))]
        grad_ins = tuple(parts)
    else:
        grad_specs = [pl.BlockSpec((parts.shape[0], rb, bc), lambda i, j: (0, i, j))]
        grad_ins = (parts,)
    return _pcall(
        body, name=name, grid=(rows // rb, cols // bc), in_specs=[blk, blk, blk] + grad_specs,
        out_specs=[blk] * 4, out_shape=[out] * 4, compiler_params=_params("parallel", "parallel"),
    )(w, m, v, *grad_ins)


SMALL = ("b_ada", "norm1_g", "b_gate", "gmlp_ln_g", "gmlp_ln_b", "gmlp_ws", "gmlp_bs", "hg_lb", "hg_norm_g",
         "norm2_g", "final_norm_g")
BIG = ("w_in", "w_branch_gmlp", "w_branch_hg", "w_out", "w_ffn_in", "w_ffn_out")
WEIGHTS = ("w_ada", "b_ada", "norm1_g", "w_in", "b_gate", "gmlp_ln_g", "gmlp_ln_b", "gmlp_ws", "gmlp_bs", "hg_lb",
           "hg_norm_g", "w_branch_gmlp", "w_branch_hg", "w_out", "norm2_g", "w_ffn_in", "w_ffn_out", "final_norm_g")


def _pack(parts):
    return jnp.concatenate([p.reshape(-1, LANES) for p in parts], axis=0)


def _step(x, c, loss_target, w, m, v):
    s, d = x.shape[1], x.shape[2]
    gw = w["gmlp_ln_g"].shape[-1]
    hw = w["hg_norm_g"].shape[-1]
    x2d, tgt = x[0], loss_target[0]
    mx, my, mc = lax.axis_index("x"), lax.axis_index("y"), lax.axis_index("c")
    chip = 2 * mx + my
    dev = 2 * chip + mc
    q_col = 2 * gw // LANES
    gate_col = (2 * gw + 4 * hw) // d
    place = jnp.stack([chip, mc]).astype(jnp.int32)
    _Order.last = None

    c_all = _all_gather8(c.reshape(-1, LANES), "gather_c").reshape(N_DEV, d)
    gather_in = _WeightGather("in", _WeightGather.zones([w["w_in"][0]], chip), ring=True)
    zones_mix = _WeightGather.zones([w[n][0] for n in ("w_branch_gmlp", "w_branch_hg", "w_out")], chip)
    zones_fi = _WeightGather.zones([w["w_ffn_in"][0]], chip)
    zones_fo = _WeightGather.zones([w["w_ffn_out"][0]], chip)
    _Order.also = (*zones_mix, *zones_fi, *zones_fo)
    n_ada = w["w_ada"].shape[-1]
    b_ada_q = lax.dynamic_slice(w["b_ada"], (0, chip * n_ada), (1, n_ada))
    mod_q = _ada_fwd(c_all, w["w_ada"][0], b_ada_q)
    mod_all = _all_gather8(mod_q, "gather_mod")
    mod = lax.dynamic_index_in_dim(mod_all, dev, axis=1, keepdims=False)[::2].reshape(1, 6 * d)
    sh1, sc1, gt1, sh2, sc2, gt2 = [mod[:, i * d:(i + 1) * d] for i in range(6)]

    gather_in.relay_and_pass_near()
    gather_mix = _WeightGather("mix", zones_mix)
    gather_fi = _WeightGather("fi", zones_fi, ring=True)

    norm1_g, norm2_g, final_g = w["norm1_g"], w["norm2_g"], w["final_norm_g"].reshape(1, d)
    ln_g, ln_b = w["gmlp_ln_g"], w["gmlp_ln_b"]
    ws = w["gmlp_ws"][0]
    groups = ws.shape[0]
    bs_t = jnp.pad(w["gmlp_bs"][0].T, ((0, 0), (0, LANES - groups)))
    hg_lb, hg_ng, b_gate = w["hg_lb"], w["hg_norm_g"], w["b_gate"]

    h1 = _norm_mod_fwd(x2d, norm1_g, sc1, sh1, "norm1_fwd")
    order = jnp.stack([chip, 2 * (1 - mx) + my, 2 * mx + (1 - my), 2 * (1 - mx) + (1 - my)]).astype(jnp.int32)
    w_in, = gather_in.near_done()
    z = _matmul_quarters(h1, w_in, order, 0, 3, "mm_z_near")
    w_in, = gather_in.pass_far()
    z = _matmul_quarters(h1, w_in, order, 3, 1, "mm_z_far", prev=z)
    gather_mix.pass_on()
    ya = _gmlp_fwd(z, ln_g, ln_b, ws, bs_t)
    yb, o_raw, states = _hgrn_fwd(z, hg_lb, hg_ng, q_col)
    gather_fi.relay()
    gather_fo = _WeightGather("fo", zones_fo)
    w_bg, w_bh, w_out = gather_mix.done()
    w_bg, w_bh = [wq.transpose(1, 0, 2).reshape(wq.shape[1], -1) for wq in (w_bg, w_bh)]
    w_out = w_out.reshape(-1, w_out.shape[-1])
    pa, pb, y = _branch_merge(ya, yb, w_bg, w_bh, z, b_gate, gate_col)
    yo = _matmul(y, w_out, mode="nn", name="mm_yo", out_dtype=F32)
    gather_fi.pass_on()
    x1, h2 = _norm_mod_fwd(x2d, norm2_g, sc2, sh2, "norm2_fwd", res=yo, gt=gt1)
    w_fi, = gather_fi.done()
    a_act, up_act, hf = _ffn_in(h2, w_fi)
    gather_fo.pass_on()
    w_fo, = gather_fo.done()
    w_fo = w_fo.reshape(-1, w_fo.shape[-1])
    ffn = _matmul(hf, w_fo, mode="nn", name="mm_ffn", out_dtype=F32)
    dx2, dffn, loss_row, d_final_g, d_gt2 = _final_loss(x1, ffn, gt2, final_g, tgt)

    g_fo = _matmul(hf, dffn, mode="tn", name="mm_g_fo", out_dtype=BF16, bm=_half_width(hf.shape[1] // 2))
    daup = tuple(_ffn_out_bwd(dffn, w_fo, a_act, up_act))
    g_fi = _matmul(h2, daup, mode="tn", name="mm_g_fi", out_dtype=BF16, out_slots=True, bn=_half_width(w_fi.shape[2]))
    red_ffn = _GradReduce("ffn", [g_fo.reshape(N_CHIPS, -1, g_fo.shape[-1]), g_fi])
    dh2 = _matmul(daup, w_fi, mode="nt", name="mm_dh2", out_dtype=F32, b_slots=True)
    red_ffn.step(place)
    dx1, dyo, d_sh2, d_sc2, d_norm2, d_gt1 = _norm_mod_bwd(dh2, x1, dx2, norm2_g, sc2, "norm2_bwd", branch=yo, gt=gt1)
    g_out = _matmul(y, dyo, mode="tn", name="mm_g_out", out_dtype=BF16)
    dpa, dpb, dz_ga, dz_gb, d_b_ga, d_b_gb = _out_bwd(dyo, w_out, pa, pb, z, b_gate, gate_col)
    g_bg, g_bh = _branch_wgrad(ya, yb, dpa, dpb)
    red_mix = _GradReduce("mix", [g_out.reshape(N_CHIPS, -1, g_out.shape[-1]), g_bg, g_bh])
    dya, dyb = _branch_bwd(dpa, dpb, w_bg, w_bh)
    red_mix.step(place)
    dz_uv, d_ws, d_bs_t, d_ln_g, d_ln_b = _gmlp_bwd(dya, z, ln_g, ln_b, ws, bs_t)
    dz_q, dz_f, dz_i, dz_g, d_hg_lb, d_hg_ng = _hgrn_bwd(dyb, z, o_raw, states, hg_lb, hg_ng, q_col)
    dz = jnp.concatenate([dz_uv, dz_q, dz_f, dz_i, dz_g, dz_ga, dz_gb], axis=1)
    grad, delta, new_m, new_v = {}, {}, {}, {}

    def update(n, parts, outer=False):
        outs = _adamw(w[n][0], m[n][0], v[n][0], parts, "adamw_" + n, outer=outer)
        grad[n], delta[n], new_m[n], new_v[n] = [o[None] for o in outs]

    g_in = _matmul(h1, dz, mode="tn", name="mm_g_in", out_dtype=BF16, out_slots=True, bn=_half_width(w_in.shape[2]))
    red_in = _GradReduce("in", [g_in])
    red_ffn.join(place)
    red_mix.join(place)
    red_in.step(place)
    g_fo, g_fi = red_ffn.done()
    n = "w_ffn_in"
    dh1, rode = _matmul(dz, w_in, mode="nt", name="mm_dh1", out_dtype=F32, b_slots=True,
                        rider=([w[n][0], m[n][0], v[n][0], g_fi], lambda *blk: (blk[3], *_adam_update(*blk)), 4))
    grad[n], delta[n], new_m[n], new_v[n] = [o[None] for o in rode]
    grad_x, d_sh1, d_sc1, d_norm1 = _norm_mod_bwd(dh1, x2d, dx1, norm1_g, sc1, "norm1_bwd")

    d_mod = jnp.concatenate([d_sh1, d_sc1, d_gt1, d_sh2, d_sc2, d_gt2], axis=1)
    small_part = {"b_ada": d_mod, "norm1_g": d_norm1, "b_gate": jnp.concatenate([d_b_ga, d_b_gb], axis=1), "gmlp_ln_g": d_ln_g, "gmlp_ln_b": d_ln_b,
                  "gmlp_ws": d_ws, "gmlp_bs": d_bs_t[:, :groups].T, "hg_lb": d_hg_lb, "hg_norm_g": d_hg_ng,
                  "norm2_g": d_norm2, "final_norm_g": d_final_g}
    loss_rows = jnp.broadcast_to(loss_row, (8, LANES))
    packed = _pack([small_part[n] for n in SMALL] + [loss_rows])
    zone = lax.dynamic_update_slice(lax.empty((N_DEV,) + packed.shape, F32), packed[None], (dev, 0, 0))
    small_sems, small_bufs = _xfer_start("gsmall_start", [zone], _plan_gather_out, 4, after_last=True)
    update("w_ffn_out", g_fo[None])
    g_out, g_bg, g_bh = red_mix.done()
    update("w_out", g_out[None])
    red_in.join(place)
    (small_sems,), small_bufs = _xfer_hand_over("gsmall_pass", small_sems, small_bufs, _plan_gather_out,
                                                [(_plan_gather_on, 3)])
    update("w_branch_gmlp", g_bg[None])
    update("w_branch_hg", g_bh[None])
    small_all, = _xfer_wait("gsmall_wait", small_sems, small_bufs, _plan_gather_on)
    d_mod_all = small_all[:, :6 * d // LANES].reshape(N_DEV, 6 * d)
    d_mod_q = lax.dynamic_slice(d_mod_all, (0, chip * n_ada), (N_DEV, n_ada))
    pad = [jnp.zeros_like(loss_rows)]
    outs = _adamw(_pack([w[n] for n in SMALL] + pad), _pack([m[n] for n in SMALL] + pad),
                  _pack([v[n] for n in SMALL] + pad), small_all, "adamw_small")
    update("w_ada", (c_all, d_mod_q), outer=True)
    update("w_in", red_in.done()[0][None])
    row = 0
    for n in SMALL:
        cnt = w[n].size // LANES
        for dst, o in zip((grad, delta, new_m, new_v), outs):
            dst[n] = o[row:row + cnt].reshape(w[n].shape)
        row += cnt

    loss = outs[0][row, 0]
    return (loss, grad_x[None], *[grad[n] for n in WEIGHTS], *[delta[n] for n in WEIGHTS],
            *[new_m[n] for n in WEIGHTS], *[new_v[n] for n in WEIGHTS])


def kernel(x, c, w_ada, b_ada, norm1_g, w_in, b_gate, gmlp_ln_g, gmlp_ln_b, gmlp_ws, gmlp_bs, hg_lb, hg_norm_g, w_branch_gmlp, w_branch_hg, w_out, norm2_g, w_ffn_in, w_ffn_out, final_norm_g, loss_target, m_w_ada, m_b_ada, m_norm1_g, m_w_in, m_b_gate, m_gmlp_ln_g, m_gmlp_ln_b, m_gmlp_ws, m_gmlp_bs, m_hg_lb, m_hg_norm_g, m_w_branch_gmlp, m_w_branch_hg, m_w_out, m_norm2_g, m_w_ffn_in, m_w_ffn_out, m_final_norm_g, v_w_ada, v_b_ada, v_norm1_g, v_w_in, v_b_gate, v_gmlp_ln_g, v_gmlp_ln_b, v_gmlp_ws, v_gmlp_bs, v_hg_lb, v_hg_norm_g, v_w_branch_gmlp, v_w_branch_hg, v_w_out, v_norm2_g, v_w_ffn_in, v_w_ffn_out, v_final_norm_g):
    w = dict(w_ada=w_ada, b_ada=b_ada, norm1_g=norm1_g, w_in=w_in, b_gate=b_gate, gmlp_ln_g=gmlp_ln_g,
             gmlp_ln_b=gmlp_ln_b, gmlp_ws=gmlp_ws, gmlp_bs=gmlp_bs, hg_lb=hg_lb, hg_norm_g=hg_norm_g,
             w_branch_gmlp=w_branch_gmlp, w_branch_hg=w_branch_hg, w_out=w_out, norm2_g=norm2_g,
             w_ffn_in=w_ffn_in, w_ffn_out=w_ffn_out, final_norm_g=final_norm_g)
    m = dict(w_ada=m_w_ada, b_ada=m_b_ada, norm1_g=m_norm1_g, w_in=m_w_in, b_gate=m_b_gate, gmlp_ln_g=m_gmlp_ln_g,
             gmlp_ln_b=m_gmlp_ln_b, gmlp_ws=m_gmlp_ws, gmlp_bs=m_gmlp_bs, hg_lb=m_hg_lb, hg_norm_g=m_hg_norm_g,
             w_branch_gmlp=m_w_branch_gmlp, w_branch_hg=m_w_branch_hg, w_out=m_w_out, norm2_g=m_norm2_g,
             w_ffn_in=m_w_ffn_in, w_ffn_out=m_w_ffn_out, final_norm_g=m_final_norm_g)
    v = dict(w_ada=v_w_ada, b_ada=v_b_ada, norm1_g=v_norm1_g, w_in=v_w_in, b_gate=v_b_gate, gmlp_ln_g=v_gmlp_ln_g,
             gmlp_ln_b=v_gmlp_ln_b, gmlp_ws=v_gmlp_ws, gmlp_bs=v_gmlp_bs, hg_lb=v_hg_lb, hg_norm_g=v_hg_norm_g,
             w_branch_gmlp=v_w_branch_gmlp, w_branch_hg=v_w_branch_hg, w_out=v_w_out, norm2_g=v_norm2_g,
             w_ffn_in=v_w_ffn_in, w_ffn_out=v_w_ffn_out, final_norm_g=v_final_norm_g)
    return _step(x, c, loss_target, w, m, v)
```

```python
import jax
import jax.numpy as jnp
from jax import lax
from jax.experimental import pallas as pl
from jax.experimental.pallas import tpu as pltpu

F32 = jnp.float32
BF16 = jnp.bfloat16
EPS = 1e-6
LANES = 128
N_CHIPS = 4
N_DEV = 8
VMEM_LIMIT_BYTES = 56 * 1024 * 1024
VMEM_RIDER_LIMIT_BYTES = 60 * 1024 * 1024
HG_CHUNK = 32
HG_ROWS = 256
HG_HEADS_PER_STEP = 8
GMLP_CHUNKS_PER_STEP = 4
EXP_CLAMP = 80.0
ADAM_LR, ADAM_B1, ADAM_B2, ADAM_EPS, ADAM_WD, ADAM_STEP = 0.001, 0.9, 0.999, 1e-08, 0.01, 10
MESH = pl.DeviceIdType.MESH

NN = (((1,), (0,)), ((), ()))
NT = (((1,), (1,)), ((), ()))
TN = (((0,), (0,)), ((), ()))


def _dot(a, b, dims=NN, precision=None):
    return lax.dot_general(a, b, dims, precision=precision, preferred_element_type=F32)


def _params(*semantics, vmem_limit_bytes=VMEM_LIMIT_BYTES):
    return pltpu.CompilerParams(dimension_semantics=semantics, vmem_limit_bytes=vmem_limit_bytes)


class _Order:
    last = None
    also = ()


def _pcall(body, *, in_specs, out_specs, grid=(), scratch_shapes=(), num_scalar_prefetch=0, **kw):
    def run(*ins):
        deps = (() if _Order.last is None else (_Order.last,)) + tuple(_Order.also)
        _Order.also = ()
        n_in, n_dep = len(ins), len(deps)

        def wrapped(*refs):
            body(*refs[:n_in], *refs[n_in + n_dep:])

        specs = list(in_specs) + [pl.BlockSpec(memory_space=pl.ANY)] * n_dep
        if num_scalar_prefetch:
            grid_spec = pltpu.PrefetchScalarGridSpec(
                num_scalar_prefetch=num_scalar_prefetch, grid=grid, in_specs=specs, out_specs=out_specs,
                scratch_shapes=scratch_shapes)
            outs = pl.pallas_call(wrapped, grid_spec=grid_spec, **kw)(*ins, *deps)
        else:
            outs = pl.pallas_call(wrapped, grid=grid, in_specs=specs, out_specs=out_specs,
                                  scratch_shapes=scratch_shapes, **kw)(*ins, *deps)
        _Order.last = jax.tree.leaves(outs)[0]
        return outs

    return run


def _pick_rows(dim, pref, mult=16):
    best = None
    for cand in range(mult, min(dim, pref) + 1, mult):
        if dim % cand == 0:
            best = cand
    assert best is not None, (dim, pref)
    return best


def _half_width(n):
    return n // 2 if n % (2 * LANES) == 0 else n


def _pick(dim, pref):
    if dim <= pref:
        return dim
    best = None
    for cand in range(LANES, pref + 1, LANES):
        if dim % cand == 0:
            best = cand
    assert best is not None, (dim, pref)
    return best


def _sigmoid(x):
    return 1.0 / (1.0 + jnp.exp(-x))


def _gelu(x):
    c = 0.7978845608028654
    return 0.5 * x * (1.0 + jnp.tanh(c * (x + 0.044715 * x * x * x)))


def _gelu_grad(x):
    c = 0.7978845608028654
    t = jnp.tanh(c * (x + 0.044715 * x * x * x))
    return 0.5 * (1.0 + t) + 0.5 * x * (1.0 - t * t) * c * (1.0 + 3.0 * 0.044715 * x * x)


def _rms(x):
    r = lax.rsqrt(jnp.mean(x * x, axis=-1, keepdims=True) + EPS)
    return x * r, r


def _colsum(x):
    return jnp.sum(x, axis=0, keepdims=True)


def _accumulate(first, ref, val):
    @pl.when(first)
    def _():
        ref[...] = val

    @pl.when(jnp.logical_not(first))
    def _():
        ref[...] += val


def _matmul(a, b, *, mode, name, out_dtype, b_slots=False, out_slots=False, bm=1024, bn=1024, bk=2816, rider=None):
    pair = a if isinstance(a, tuple) else b if isinstance(b, tuple) else None
    if mode == "nn":
        m, k = a.shape
        n = b.shape[2] * N_CHIPS if b_slots else b.shape[1]
        per = b.shape[2] if b_slots else n
    elif mode == "nt":
        m, k = (a[0].shape[0], 2 * a[0].shape[1]) if pair else a.shape
        n = b.shape[1] if b_slots else b.shape[0]
        per = b.shape[2] if b_slots else k
    else:
        k, m = a.shape
        n = 2 * b[0].shape[1] if pair else b.shape[1]
        per = n // N_CHIPS if out_slots else n
    bm = _pick(m, bm)
    if mode == "nt":
        bn, bk = _pick(n, bn), _pick(per, bk)
    else:
        bn, bk = _pick(per, bn), _pick(k, bk)
    nk = k // bk
    per_blocks = per // (bk if mode == "nt" else bn)
    dims = {"nn": NN, "nt": NT, "tn": TN}[mode]
    half = (nk if mode == "nt" else n // bn) // 2

    def product(a_ref, b_ref, o_ref, acc):
        part = _dot(a_ref[...], b_ref[...], dims)
        if nk == 1:
            o_ref[...] = part.astype(o_ref.dtype)
            return
        acc_ref, kk = acc[0], pl.program_id(2)

        @pl.when(kk == 0)
        def _():
            acc_ref[...] = part

        @pl.when(jnp.logical_and(kk > 0, kk < nk - 1))
        def _():
            acc_ref[...] += part

        @pl.when(kk == nk - 1)
        def _():
            o_ref[...] = (acc_ref[...] + part).astype(o_ref.dtype)

    n_ride, n_rode = (len(rider[0]), rider[2]) if rider else (0, 0)

    def body(*refs):
        if rider:
            ride_in, ride_out = refs[2:2 + n_ride], refs[3 + n_ride:3 + n_ride + n_rode]

            def ride(t, carry):
                rows = pl.ds(pl.multiple_of(t * 8, 8), 8)
                for ref, val in zip(ride_out, rider[1](*[r[rows, :] for r in ride_in])):
                    ref[rows, :] = val
                return carry

            lax.fori_loop(0, ride_in[0].shape[0] // 8, ride, 0)
        if not pair:
            return product(refs[0], refs[1], refs[2 + n_ride], refs[3 + n_ride + n_rode:])
        first = pl.program_id(2 if mode == "nt" else 1) < half
        x1, x2, y, o_ref, acc = refs[0], refs[1], refs[2], refs[3], refs[4:]

        @pl.when(first)
        def _():
            product(*((x1, y) if mode == "nt" else (y, x1)), o_ref, acc)

        @pl.when(jnp.logical_not(first))
        def _():
            product(*((x2, y) if mode == "nt" else (y, x2)), o_ref, acc)

    if mode == "nn":
        a_spec = pl.BlockSpec((bm, bk), lambda i, j, kk: (i, kk))
        if b_slots:
            b_spec = pl.BlockSpec((None, bk, bn), lambda i, j, kk: (j // per_blocks, kk, j % per_blocks))
        else:
            b_spec = pl.BlockSpec((bk, bn), lambda i, j, kk: (kk, j))
    elif mode == "nt":
        a_spec = pl.BlockSpec((bm, bk), lambda i, j, kk: (i, kk))
        if b_slots:
            b_spec = pl.BlockSpec((None, bn, bk), lambda i, j, kk: (kk // per_blocks, j, kk % per_blocks))
        else:
            b_spec = pl.BlockSpec((bn, bk), lambda i, j, kk: (j, kk))
    else:
        a_spec = pl.BlockSpec((bk, bm), lambda i, j, kk: (kk, i))
        b_spec = pl.BlockSpec((bk, bn), lambda i, j, kk: (kk, j))
    if out_slots:
        o_spec = pl.BlockSpec((None, bm, bn), lambda i, j, kk: (j // per_blocks, i, j % per_blocks))
        out_shape = jax.ShapeDtypeStruct((N_CHIPS, m, per), out_dtype)
    else:
        o_spec = pl.BlockSpec((bm, bn), lambda i, j, kk: (i, j))
        out_shape = jax.ShapeDtypeStruct((m, n), out_dtype)
    if not pair:
        ins, in_specs = (a, b), [a_spec, b_spec]
    elif mode == "nt":
        ins = (*a, b)
        in_specs = [pl.BlockSpec((bm, bk), lambda i, j, kk: (i, jnp.minimum(kk, half - 1))),
                    pl.BlockSpec((bm, bk), lambda i, j, kk: (i, jnp.maximum(kk - half, 0))), b_spec]
    else:
        ins = (*b, a)
        in_specs = [pl.BlockSpec((bk, bn), lambda i, j, kk: (kk, jnp.minimum(j, half - 1))),
                    pl.BlockSpec((bk, bn), lambda i, j, kk: (kk, jnp.maximum(j - half, 0))), a_spec]
    scratch = [pltpu.VMEM((bm, bn), F32)] if nk > 1 else []
    if not rider:
        return _pcall(
            body, name=name, grid=(m // bm, n // bn, nk), in_specs=in_specs, out_specs=o_spec, out_shape=out_shape,
            scratch_shapes=scratch, compiler_params=_params("parallel", "parallel", "arbitrary"),
        )(*ins)
    assert not pair
    rows, cols = rider[0][0].shape
    nj = n // bn
    rb = rows // ((m // bm) * nj * nk)
    assert rows == rb * (m // bm) * nj * nk and rb % 8 == 0, (rows, rb)
    ride_spec = pl.BlockSpec((rb, cols), lambda i, j, kk: ((i * nj + j) * nk + kk, 0))
    outs = _pcall(
        body, name=name, grid=(m // bm, nj, nk), in_specs=in_specs + [ride_spec] * n_ride,
        out_specs=[o_spec] + [ride_spec] * n_rode,
        out_shape=[out_shape] + [jax.ShapeDtypeStruct((rows, cols), F32)] * n_rode,
        scratch_shapes=scratch,
        compiler_params=_params("arbitrary", "arbitrary", "arbitrary", vmem_limit_bytes=VMEM_RIDER_LIMIT_BYTES),
    )(*ins, *rider[0])
    return outs[0], outs[1:]


def _matmul_quarters(h, w_slots, order, first, count, name, prev=None, bm=1024):
    s, k = h.shape
    n = w_slots.shape[2]
    bm, bn = _pick(s, bm), _half_width(n)
    pb = n // bn

    def body(order_ref, h_ref, w_ref, *rest):
        rest[-1][...] = _dot(h_ref[...], w_ref[...])

    ins = (order, h, w_slots) + (() if prev is None else (prev,))
    return _pcall(
        body, name=name, num_scalar_prefetch=1, grid=(count * pb, s // bm),
        in_specs=[pl.BlockSpec((bm, k), lambda j, i, o: (i, 0)),
                  pl.BlockSpec((None, k, bn), lambda j, i, o: (o[first + j // pb], 0, j % pb))]
        + ([] if prev is None else [ANY_SPEC]),
        out_specs=pl.BlockSpec((bm, bn), lambda j, i, o: (i, o[first + j // pb] * pb + j % pb)),
        out_shape=jax.ShapeDtypeStruct((s, N_CHIPS * n), F32),
        input_output_aliases={} if prev is None else {3: 0},
        compiler_params=_params("arbitrary", "arbitrary"),
    )(*ins)


def _place():
    x, y, c = lax.axis_index("x"), lax.axis_index("y"), lax.axis_index("c")
    chips = [(1 - x, y), (x, 1 - y), (1 - x, 1 - y)]
    return x, y, c, chips


def _all_gather8(block, name):
    def body(x_ref, out_ref, send_sems, recv_sems, local_sem):
        x, y, c, chips = _place()
        me, sibling = (x, y, c), (x, y, 1 - c)

        def slot(px, py, pc):
            return out_ref.at[4 * px + 2 * py + pc]

        def copy(k, blk, to, src=None):
            return pltpu.make_async_remote_copy(
                src_ref=slot(*blk) if src is None else src, dst_ref=slot(*blk),
                send_sem=send_sems.at[k], recv_sem=recv_sems.at[k], device_id=to, device_id_type=MESH)

        mine = pltpu.make_async_copy(x_ref, slot(*me), local_sem)
        mine.start()
        first = [copy(0, me, sibling, src=x_ref)]
        first += [copy(1 + j, me, (*chip, c), src=x_ref) for j, chip in enumerate(chips)]
        for cp in first:
            cp.start()
        passed = [copy(4 + j, (*chip, c), sibling) for j, chip in enumerate(chips)]
        for j, chip in enumerate(chips):
            copy(1 + j, (*chip, c), me).wait_recv()
            passed[j].start()
        copy(0, sibling, me).wait_recv()
        for j, chip in enumerate(chips):
            copy(4 + j, (*chip, 1 - c), me).wait_recv()
        for cp in first + passed:
            cp.wait_send()
        mine.wait()

    return _pcall(
        body, name=name, out_shape=jax.ShapeDtypeStruct((N_DEV,) + block.shape, block.dtype),
        in_specs=[pl.BlockSpec(memory_space=pltpu.VMEM)], out_specs=pl.BlockSpec(memory_space=pltpu.VMEM),
        scratch_shapes=[pltpu.SemaphoreType.DMA((7,)), pltpu.SemaphoreType.DMA((7,)), pltpu.SemaphoreType.DMA],
        compiler_params=pltpu.CompilerParams(vmem_limit_bytes=VMEM_LIMIT_BYTES),
    )(block)


HBM_SPEC = pl.BlockSpec(memory_space=pltpu.HBM)
SEM_SPEC = pl.BlockSpec(memory_space=pltpu.SEMAPHORE)
ANY_SPEC = pl.BlockSpec(memory_space=pl.ANY)
EFFECT = pltpu.SideEffectType.DATAFLOW_SIDE_EFFECTING


def _xfer_start(name, bufs, plan, n_copies, after_last=False):
    nb = len(bufs)
    deps = (_Order.last,) if after_last and _Order.last is not None else ()
    nd = len(deps)

    def body(*refs):
        send_sems, recv_sems = refs[nb + nd], refs[nb + nd + 1]
        token = refs[nb + nd + 2 + nb]
        for k, (src, dst, dev) in enumerate(plan(refs[:nb], *_place())):
            pltpu.make_async_remote_copy(src_ref=src, dst_ref=dst, send_sem=send_sems.at[k], recv_sem=recv_sems.at[k],
                                         device_id=dev, device_id_type=MESH).start()
        token[...] = jnp.zeros_like(token)

    outs = pl.pallas_call(
        body, name=name,
        out_shape=(pltpu.SemaphoreType.DMA((n_copies,)), pltpu.SemaphoreType.DMA((n_copies,)),
                   *[pltpu.HBM(b.shape, b.dtype) for b in bufs], jax.ShapeDtypeStruct((8, LANES), F32)),
        in_specs=[HBM_SPEC] * nb + [ANY_SPEC] * nd,
        out_specs=(SEM_SPEC, SEM_SPEC, *[HBM_SPEC] * nb, pl.BlockSpec(memory_space=pltpu.VMEM)),
        input_output_aliases={i: 2 + i for i in range(nb)},
        compiler_params=pltpu.CompilerParams(has_side_effects=EFFECT),
    )(*[pltpu.with_memory_space_constraint(b, pltpu.HBM) for b in bufs], *deps)
    _Order.last = outs[-1]
    return (outs[0], outs[1]), list(outs[2:2 + nb])


def _xfer_wait(name, sems, bufs, plan):
    nb = len(bufs)

    def body(*refs):
        send_sems, recv_sems = refs[nb], refs[nb + 1]
        for k, (src, dst, dev) in enumerate(plan(refs[:nb], *_place())):
            copy = pltpu.make_async_remote_copy(src_ref=src, dst_ref=dst, send_sem=send_sems.at[k],
                                                recv_sem=recv_sems.at[k], device_id=dev, device_id_type=MESH)
            copy.wait_send()
            copy.wait_recv()

    outs = pl.pallas_call(
        body, name=name, out_shape=tuple(pltpu.HBM(b.shape, b.dtype) for b in bufs),
        in_specs=[HBM_SPEC] * nb + [SEM_SPEC, SEM_SPEC, ANY_SPEC], out_specs=tuple([HBM_SPEC] * nb),
        input_output_aliases={i: i for i in range(nb)},
        compiler_params=pltpu.CompilerParams(has_side_effects=EFFECT),
    )(*bufs, *sems, _Order.last)
    _Order.last = outs[0]
    return list(outs)


def _xfer_hand_over(name, sems, bufs, plan, next_plans):
    nb, n_next = len(bufs), len(next_plans)

    def body(*refs):
        send_sems, recv_sems = refs[nb], refs[nb + 1]
        outs = refs[nb + 3:]
        place = _place()
        for k, (src, dst, dev) in enumerate(plan(refs[:nb], *place)):
            copy = pltpu.make_async_remote_copy(src_ref=src, dst_ref=dst, send_sem=send_sems.at[k],
                                                recv_sem=recv_sems.at[k], device_id=dev, device_id_type=MESH)
            copy.wait_send()
            copy.wait_recv()
        for p, (next_plan, _) in enumerate(next_plans):
            for k, (src, dst, dev) in enumerate(next_plan(refs[:nb], *place)):
                pltpu.make_async_remote_copy(src_ref=src, dst_ref=dst, send_sem=outs[2 * p].at[k],
                                             recv_sem=outs[2 * p + 1].at[k], device_id=dev, device_id_type=MESH).start()
        outs[-1][...] = jnp.zeros_like(outs[-1])

    sem_shapes = [pltpu.SemaphoreType.DMA((copies,)) for _, copies in next_plans for _ in range(2)]
    outs = pl.pallas_call(
        body, name=name,
        out_shape=(*sem_shapes, *[pltpu.HBM(b.shape, b.dtype) for b in bufs], jax.ShapeDtypeStruct((8, LANES), F32)),
        in_specs=[HBM_SPEC] * nb + [SEM_SPEC, SEM_SPEC, ANY_SPEC],
        out_specs=(*[SEM_SPEC] * (2 * n_next), *[HBM_SPEC] * nb, pl.BlockSpec(memory_space=pltpu.VMEM)),
        input_output_aliases={i: 2 * n_next + i for i in range(nb)},
        compiler_params=pltpu.CompilerParams(has_side_effects=EFFECT),
    )(*bufs, *sems, _Order.last)
    _Order.last = outs[-1]
    return [(outs[2 * p], outs[2 * p + 1]) for p in range(n_next)], list(outs[2 * n_next:2 * n_next + nb])


def _half(ref, c, axis):
    rows = ref.shape[axis] // 2
    return pl.ds(c * rows, rows)


def _plan_weights_ici(n):
    def plan(refs, x, y, c, chips):
        out = []
        for w in range(n):
            region = refs[w].at[2 * x + y, _half(refs[w], c, 1), :]
            out += [(region, region, (*chip, c)) for chip in chips]
        return out
    return plan


def _plan_weights_ring(n):
    def plan(refs, x, y, c, chips):
        out = []
        for w in range(n):
            region = refs[w].at[2 * x + y, _half(refs[w], c, 1), :]
            out += [(region, region, (*chip, c)) for chip in chips[:2]]
        return out
    return plan


def _plan_weights_relay(n):
    def plan(refs, x, y, c, chips):
        out = []
        for w in range(n):
            quarter_rows = refs[w].shape[1] // 4
            upper = refs[w].at[2 * x + (1 - y), pl.ds(2 * c * quarter_rows, quarter_rows), :]
            lower = refs[w].at[2 * (1 - x) + y, pl.ds((2 * c + 1) * quarter_rows, quarter_rows), :]
            out += [(upper, upper, (1 - x, y, c)), (lower, lower, (x, 1 - y, c))]
        return out
    return plan


def _plan_weights_d2d(n, which=slice(0, 3)):
    def plan(refs, x, y, c, chips):
        out = []
        for w in range(n):
            rows = _half(refs[w], c, 1)
            for chip in chips[which]:
                region = refs[w].at[2 * chip[0] + chip[1], rows, :]
                out.append((region, region, (x, y, 1 - c)))
        return out
    return plan


def _plan_gather_out(refs, x, y, c, chips):
    mine = refs[0].at[4 * x + 2 * y + c]
    return [(mine, mine, (x, y, 1 - c))] + [(mine, mine, (*chip, c)) for chip in chips]


def _plan_gather_on(refs, x, y, c, chips):
    out = []
    for chip in chips:
        block = refs[0].at[4 * chip[0] + 2 * chip[1] + c]
        out.append((block, block, (x, y, 1 - c)))
    return out


def _plan_grads_d2d(n):
    def plan(refs, x, y, c, chips):
        return [(refs[w].at[:, _half(refs[w], 1 - c, 1), :], refs[n + w], (x, y, 1 - c)) for w in range(n)]
    return plan


def _plan_grads_ici(n):
    def plan(refs, x, y, c, chips):
        out = []
        for w in range(n):
            out += [(refs[w].at[2 * chip[0] + chip[1]], refs[n + w].at[2 * x + y], (*chip, c)) for chip in chips]
        return out
    return plan


def _plan_final_d2d(n):
    def plan(refs, x, y, c, chips):
        out = []
        for w in range(n):
            region = refs[w].at[_half(refs[w], c, 0), :]
            out.append((region, region, (x, y, 1 - c)))
        return out
    return plan


def _stream_blocks(hr, cols):
    bc = cols if cols <= 4096 else _pick(cols, 4096)
    return _pick_rows(hr, max(16, (768 * 1024) // bc)), bc


def _pre_reduce(g, landed, place, name):
    _, rows, cols = g.shape
    hr = rows // 2
    rb, bc = _stream_blocks(hr, cols)
    nrb = hr // rb

    def body(place_ref, g_ref, l_ref, o_ref):
        o_ref[...] = (g_ref[...].astype(F32) + l_ref[...].astype(F32)).astype(o_ref.dtype)

    return _pcall(
        body, name=name, num_scalar_prefetch=1, grid=(N_CHIPS, nrb, cols // bc),
        in_specs=[pl.BlockSpec((None, rb, bc), lambda j, i, k, p: (j, p[1] * nrb + i, k)),
                  pl.BlockSpec((None, rb, bc), lambda j, i, k, p: (j, i, k))],
        out_specs=pl.BlockSpec((None, rb, bc), lambda j, i, k, p: (j, i, k)),
        out_shape=jax.ShapeDtypeStruct((N_CHIPS, hr, cols), g.dtype),
        compiler_params=_params("parallel", "parallel", "parallel"),
    )(place, g, landed)


def _sum_slots(mine, landed, place, name):
    _, hr, cols = mine.shape
    rb, bc = _stream_blocks(hr, cols)
    rb = _pick_rows(hr, max(16, rb // 2))
    nrb = hr // rb

    def body(place_ref, m_ref, l_ref, o_ref):
        chip = place_ref[0]
        own = m_ref[...].astype(F32)
        total = jnp.where(chip == 0, own, l_ref[0].astype(F32))
        for j in range(1, N_CHIPS):
            total = total + jnp.where(chip == j, own, l_ref[j].astype(F32))
        o_ref[...] = total

    return _pcall(
        body, name=name, num_scalar_prefetch=1, grid=(nrb, cols // bc),
        in_specs=[pl.BlockSpec((None, rb, bc), lambda i, k, p: (p[0], i, k)),
                  pl.BlockSpec((N_CHIPS, rb, bc), lambda i, k, p: (0, i, k))],
        out_specs=pl.BlockSpec((rb, bc), lambda i, k, p: (p[1] * nrb + i, k)),
        out_shape=jax.ShapeDtypeStruct((2 * hr, cols), F32),
        compiler_params=_params("parallel", "parallel"),
    )(place, mine, landed)


class _WeightGather:
    @staticmethod
    def zones(quarters, chip):
        return [lax.dynamic_update_slice(lax.empty((N_CHIPS,) + q.shape, BF16), q.astype(BF16)[None], (chip, 0, 0))
                for q in quarters]

    def __init__(self, tag, zones, ring=False):
        self.tag, self.n, self.ring = tag, len(zones), ring
        self.plan = _plan_weights_ring(self.n) if ring else _plan_weights_ici(self.n)
        self.sems, self.bufs = _xfer_start("wici_start_" + tag, zones, self.plan, (2 if ring else 3) * self.n,
                                           after_last=True)

    def relay(self):
        plan = _plan_weights_relay(self.n)
        (self.sems,), self.bufs = _xfer_hand_over("wrel_start_" + self.tag, self.sems, self.bufs, self.plan,
                                                   [(plan, 2 * self.n)])
        self.plan = plan

    def pass_on(self):
        plan = _plan_weights_d2d(self.n)
        (self.sems,), self.bufs = _xfer_hand_over("wd2d_start_" + self.tag, self.sems, self.bufs, self.plan,
                                                   [(plan, 3 * self.n)])
        self.plan = plan

    def done(self):
        return _xfer_wait("wd2d_wait_" + self.tag, self.sems, self.bufs, self.plan)

    def relay_and_pass_near(self):
        plan, self.near_plan = _plan_weights_relay(self.n), _plan_weights_d2d(self.n, slice(0, 2))
        (self.sems, self.near_sems), self.bufs = _xfer_hand_over(
            "wrel_start_" + self.tag, self.sems, self.bufs, self.plan, [(plan, 2 * self.n), (self.near_plan, 2 * self.n)])
        self.plan = plan

    def near_done(self):
        self.bufs = _xfer_wait("wnear_wait_" + self.tag, self.near_sems, self.bufs, self.near_plan)
        return self.bufs

    def pass_far(self):
        plan = _plan_weights_d2d(self.n, slice(2, 3))
        (sems,), bufs = _xfer_hand_over("wfar_start_" + self.tag, self.sems, self.bufs, self.plan, [(plan, self.n)])
        return _xfer_wait("wfar_wait_" + self.tag, sems, bufs, plan)


class _GradReduce:
    def __init__(self, tag, grads):
        self.tag, self.n = tag, len(grads)
        zones = [lax.empty((N_CHIPS, g.shape[1] // 2, g.shape[2]), g.dtype) for g in grads]
        self.plan = _plan_grads_d2d(self.n)
        self.sems, self.bufs = _xfer_start("gd2d_start_" + tag, list(grads) + zones, self.plan, self.n)

    def pair(self, place):
        n = self.n
        bufs = _xfer_wait("gd2d_wait_" + self.tag, self.sems, self.bufs, self.plan)
        self.halves = [_pre_reduce(bufs[w], bufs[n + w], place, f"pre_reduce_{self.tag}{w}") for w in range(n)]

    def cross(self, after_last=False):
        zones = [lax.empty(h.shape, h.dtype) for h in self.halves]
        self.plan = _plan_grads_ici(self.n)
        self.sems, self.bufs = _xfer_start("gici_start_" + self.tag, self.halves + zones, self.plan, 3 * self.n,
                                           after_last=after_last)

    def step(self, place):
        self.pair(place)
        self.cross()

    def join(self, place):
        n = self.n
        bufs = _xfer_wait("gici_wait_" + self.tag, self.sems, self.bufs, self.plan)
        sums = [_sum_slots(bufs[w], bufs[n + w], place, f"sum_slots_{self.tag}{w}") for w in range(n)]
        self.plan = _plan_final_d2d(n)
        self.sems, self.bufs = _xfer_start("gfin_start_" + self.tag, sums, self.plan, n)

    def done(self):
        return _xfer_wait("gfin_wait_" + self.tag, self.sems, self.bufs, self.plan)


def _ada_fwd(c_all, w_q, b_q):
    d, n = w_q.shape
    bn = _pick(n, 512)

    def body(c_ref, w_ref, b_ref, o_ref):
        cv = c_ref[...]
        act = cv * _sigmoid(cv)
        o_ref[...] = _dot(act, w_ref[...], NN, lax.Precision.HIGHEST) + b_ref[...]

    return _pcall(
        body, name="ada_fwd", grid=(n // bn,),
        in_specs=[pl.BlockSpec((N_DEV, d), lambda j: (0, 0)), pl.BlockSpec((d, bn), lambda j: (0, j)),
                  pl.BlockSpec((1, bn), lambda j: (0, j))],
        out_specs=pl.BlockSpec((N_DEV, bn), lambda j: (0, j)),
        out_shape=jax.ShapeDtypeStruct((N_DEV, n), F32), compiler_params=_params("parallel"),
    )(c_all, w_q, b_q)


def _row_spec(rb, width, col=0):
    return pl.BlockSpec((rb, width), lambda i, col=col: (i, col))


def _vec_spec(width, col=0):
    return pl.BlockSpec((1, width), lambda i, col=col: (0, col))


def _norm_mod_fwd(x, g, sc, sh, name, res=None, gt=None):
    s, d = x.shape
    rb = _pick(s, 256)
    has_res = res is not None

    def body(*refs):
        if has_res:
            x_ref, res_ref, gt_ref, g_ref, sc_ref, sh_ref, x1_ref, h_ref = refs
            xv = x_ref[...] + gt_ref[...] * res_ref[...]
            x1_ref[...] = xv
        else:
            x_ref, g_ref, sc_ref, sh_ref, h_ref = refs
            xv = x_ref[...]
        xh, _ = _rms(xv)
        h_ref[...] = (xh * g_ref[...] * (1.0 + sc_ref[...]) + sh_ref[...]).astype(BF16)

    row, vec = _row_spec(rb, d), _vec_spec(d)
    if has_res:
        ins, in_specs = (x, res, gt, g, sc, sh), [row, row, vec, vec, vec, vec]
        out_shape = [jax.ShapeDtypeStruct((s, d), F32), jax.ShapeDtypeStruct((s, d), BF16)]
        out_specs = [row, row]
    else:
        ins, in_specs = (x, g, sc, sh), [row, vec, vec, vec]
        out_shape, out_specs = jax.ShapeDtypeStruct((s, d), BF16), row
    return _pcall(body, name=name, grid=(s // rb,), in_specs=in_specs, out_specs=out_specs,
                          out_shape=out_shape, compiler_params=_params("parallel"))(*ins)


def _final_loss(x1, f, gt2, g_final, target):
    s, d = x1.shape
    rb = _pick(s, 256)

    def body(x1_ref, f_ref, gt_ref, g_ref, t_ref, dx_ref, df_ref, loss_ref, dg_ref, dgt_ref):
        first = pl.program_id(0) == 0
        fv, gt, gv = f_ref[...], gt_ref[...], g_ref[...]
        x2 = x1_ref[...] + gt * fv
        xh, r = _rms(x2)
        err = xh * gv - t_ref[...]
        blk = 0.5 * jnp.sum(jnp.sum(err * err, axis=1, keepdims=True), axis=0, keepdims=True) / d
        dy = err / d
        dxh = dy * gv
        dx = r * (dxh - xh * jnp.mean(dxh * xh, axis=-1, keepdims=True))
        dx_ref[...] = dx
        df_ref[...] = (dx * gt).astype(BF16)
        _accumulate(first, loss_ref, jnp.broadcast_to(blk, (1, LANES)))
        _accumulate(first, dg_ref, _colsum(dy * xh))
        _accumulate(first, dgt_ref, _colsum(dx * fv))

    row, vec = _row_spec(rb, d), _vec_spec(d)
    return _pcall(
        body, name="final_loss", grid=(s // rb,), in_specs=[row, row, vec, vec, row],
        out_specs=[row, row, _vec_spec(LANES), vec, vec],
        out_shape=[jax.ShapeDtypeStruct((s, d), F32), jax.ShapeDtypeStruct((s, d), BF16),
                   jax.ShapeDtypeStruct((1, LANES), F32), jax.ShapeDtypeStruct((1, d), F32),
                   jax.ShapeDtypeStruct((1, d), F32)],
        compiler_params=_params("arbitrary"),
    )(x1, f, gt2, g_final, target)


def _norm_mod_bwd(dh, xin, dres, g, sc, name, branch=None, gt=None):
    s, d = xin.shape
    rb = _pick(s, 256)
    has_branch = branch is not None

    def body(*refs):
        if has_branch:
            dh_ref, x_ref, dres_ref, g_ref, sc_ref, br_ref, gt_ref, dx_ref, dbr_ref, dsh_ref, dsc_ref, dg_ref, dgt_ref = refs
        else:
            dh_ref, x_ref, dres_ref, g_ref, sc_ref, dx_ref, dsh_ref, dsc_ref, dg_ref = refs
        first = pl.program_id(0) == 0
        gv = g_ref[...]
        xh, r = _rms(x_ref[...])
        dhv = dh_ref[...]
        dn = dhv * (1.0 + sc_ref[...])
        dxh = dn * gv
        dx = dres_ref[...] + r * (dxh - xh * jnp.mean(dxh * xh, axis=-1, keepdims=True))
        dx_ref[...] = dx
        _accumulate(first, dsh_ref, _colsum(dhv))
        _accumulate(first, dsc_ref, _colsum(dhv * xh * gv))
        _accumulate(first, dg_ref, _colsum(dn * xh))
        if has_branch:
            dbr_ref[...] = (dx * gt_ref[...]).astype(BF16)
            _accumulate(first, dgt_ref, _colsum(dx * br_ref[...]))

    row, vec = _row_spec(rb, d), _vec_spec(d)
    vec_shape = jax.ShapeDtypeStruct((1, d), F32)
    if has_branch:
        ins, in_specs = (dh, xin, dres, g, sc, branch, gt), [row, row, row, vec, vec, row, vec]
        out_specs = [row, row, vec, vec, vec, vec]
        out_shape = [jax.ShapeDtypeStruct((s, d), F32), jax.ShapeDtypeStruct((s, d), BF16)] + [vec_shape] * 4
    else:
        ins, in_specs = (dh, xin, dres, g, sc), [row, row, row, vec, vec]
        out_specs = [row, vec, vec, vec]
        out_shape = [jax.ShapeDtypeStruct((s, d), F32)] + [vec_shape] * 3
    return _pcall(body, name=name, grid=(s // rb,), in_specs=in_specs, out_specs=out_specs,
                          out_shape=out_shape, compiler_params=_params("arbitrary"))(*ins)


def _branch_merge(ya, yb, w_bg, w_bh, z, b_gate, gate_col, bm=512, bn=1024):
    s, gw = ya.shape
    d = w_bg.shape[1]
    bm, bn = _pick(s, bm), _pick(d, bn)
    nj = d // bn

    def body(ya_ref, yb_ref, wa_ref, wb_ref, za_ref, zb_ref, ba_ref, bb_ref, pa_ref, pb_ref, y_ref):
        pa = _dot(ya_ref[...], wa_ref[...])
        pb = _dot(yb_ref[...], wb_ref[...])
        pa_ref[...] = pa.astype(BF16)
        pb_ref[...] = pb.astype(BF16)
        ga = _sigmoid(za_ref[...] + ba_ref[...])
        gb = _sigmoid(zb_ref[...] + bb_ref[...])
        y_ref[...] = (ga * pa + gb * pb).astype(BF16)

    act = pl.BlockSpec((bm, gw), lambda j, i: (i, 0))
    wgt = pl.BlockSpec((gw, bn), lambda j, i: (0, j))
    tile = pl.BlockSpec((bm, bn), lambda j, i: (i, j))
    shape = jax.ShapeDtypeStruct((s, d), BF16)
    return _pcall(
        body, name="branch_merge", grid=(nj, s // bm),
        in_specs=[act, act, wgt, wgt,
                  pl.BlockSpec((bm, bn), lambda j, i: (i, gate_col * nj + j)),
                  pl.BlockSpec((bm, bn), lambda j, i: (i, (gate_col + 1) * nj + j)),
                  pl.BlockSpec((1, bn), lambda j, i: (0, j)), pl.BlockSpec((1, bn), lambda j, i: (0, nj + j))],
        out_specs=[tile, tile, tile], out_shape=[shape, shape, shape], compiler_params=_params("parallel", "parallel"),
    )(ya, yb, w_bg, w_bh, z, z, b_gate, b_gate)


def _branch_wgrad(ya, yb, dpa, dpb):
    s, gw = ya.shape
    per = dpa.shape[1] // N_CHIPS

    def body(ya_ref, yb_ref, da_ref, db_ref, ga_ref, gb_ref):
        ga_ref[...] = _dot(ya_ref[...], da_ref[...], TN).astype(BF16)
        gb_ref[...] = _dot(yb_ref[...], db_ref[...], TN).astype(BF16)

    act = pl.BlockSpec((s, gw), lambda j: (0, 0))
    grd = pl.BlockSpec((s, per), lambda j: (0, j))
    out = pl.BlockSpec((None, gw, per), lambda j: (j, 0, 0))
    shape = jax.ShapeDtypeStruct((N_CHIPS, gw, per), BF16)
    return _pcall(body, name="branch_wgrad", grid=(N_CHIPS,), in_specs=[act, act, grd, grd], out_specs=[out, out],
                  out_shape=[shape, shape], compiler_params=_params("parallel"))(ya, yb, dpa, dpb)


def _branch_bwd(dpa, dpb, w_bg, w_bh, bm=512):
    s, d = dpa.shape
    gw = w_bg.shape[0]
    bm = _pick(s, bm)

    def body(da_ref, db_ref, wa_ref, wb_ref, oa_ref, ob_ref):
        oa_ref[...] = _dot(da_ref[...], wa_ref[...], NT)
        ob_ref[...] = _dot(db_ref[...], wb_ref[...], NT)

    act = pl.BlockSpec((bm, d), lambda i: (i, 0))
    wgt = pl.BlockSpec((gw, d), lambda i: (0, 0))
    out = pl.BlockSpec((bm, gw), lambda i: (i, 0))
    shape = jax.ShapeDtypeStruct((s, gw), F32)
    return _pcall(body, name="branch_bwd", grid=(s // bm,), in_specs=[act, act, wgt, wgt], out_specs=[out, out],
                  out_shape=[shape, shape], compiler_params=_params("parallel"))(dpa, dpb, w_bg, w_bh)


def _out_bwd(dyo, w_out, pa, pb, z, b_gate, gate_col, bm=512, bn=1024):
    s, d = dyo.shape
    bm, bn = _pick(s, bm), _pick(d, bn)
    nj = d // bn

    def body(dyo_ref, w_ref, pa_ref, pb_ref, za_ref, zb_ref, ba_ref, bb_ref,
             dpa_ref, dpb_ref, dza_ref, dzb_ref, dba_ref, dbb_ref):
        first = pl.program_id(1) == 0
        dyv = _dot(dyo_ref[...], w_ref[...], NT)
        ga = _sigmoid(za_ref[...] + ba_ref[...])
        gb = _sigmoid(zb_ref[...] + bb_ref[...])
        dpa_ref[...] = (dyv * ga).astype(BF16)
        dpb_ref[...] = (dyv * gb).astype(BF16)
        dga = dyv * pa_ref[...].astype(F32) * ga * (1.0 - ga)
        dgb = dyv * pb_ref[...].astype(F32) * gb * (1.0 - gb)
        dza_ref[...] = dga.astype(BF16)
        dzb_ref[...] = dgb.astype(BF16)
        _accumulate(first, dba_ref, _colsum(dga))
        _accumulate(first, dbb_ref, _colsum(dgb))

    tile = pl.BlockSpec((bm, bn), lambda j, i: (i, j))
    vec = pl.BlockSpec((1, bn), lambda j, i: (0, j))
    act, vec_shape = jax.ShapeDtypeStruct((s, d), BF16), jax.ShapeDtypeStruct((1, d), F32)
    return _pcall(
        body, name="out_bwd", grid=(nj, s // bm),
        in_specs=[pl.BlockSpec((bm, d), lambda j, i: (i, 0)), pl.BlockSpec((bn, d), lambda j, i: (j, 0)), tile, tile,
                  pl.BlockSpec((bm, bn), lambda j, i: (i, gate_col * nj + j)),
                  pl.BlockSpec((bm, bn), lambda j, i: (i, (gate_col + 1) * nj + j)),
                  vec, pl.BlockSpec((1, bn), lambda j, i: (0, nj + j))],
        out_specs=[tile, tile, tile, tile, vec, vec], out_shape=[act, act, act, act, vec_shape, vec_shape],
        compiler_params=_params("parallel", "arbitrary"),
    )(dyo, w_out, pa, pb, z, z, b_gate, b_gate)


def _ffn_in(h, w_fi, bm=512):
    s, d = h.shape
    per = w_fi.shape[2]
    ff = 2 * per
    bm, bn = _pick(s, bm), _half_width(per)
    per_blocks = per // bn

    def body(h_ref, wa_ref, wu_ref, a_ref, u_ref, hf_ref):
        hv = h_ref[...]
        a = _dot(hv, wa_ref[...])
        up = _dot(hv, wu_ref[...])
        a_ref[...] = a.astype(BF16)
        u_ref[...] = up.astype(BF16)
        hf_ref[...] = (a * _sigmoid(a) * up).astype(BF16)

    out = pl.BlockSpec((bm, bn), lambda j, i: (i, j))
    shape = jax.ShapeDtypeStruct((s, ff), BF16)
    return _pcall(
        body, name="ffn_in", grid=(ff // bn, s // bm),
        in_specs=[pl.BlockSpec((bm, d), lambda j, i: (i, 0)),
                  pl.BlockSpec((None, d, bn), lambda j, i: (j // per_blocks, 0, j % per_blocks)),
                  pl.BlockSpec((None, d, bn), lambda j, i: (2 + j // per_blocks, 0, j % per_blocks))],
        out_specs=[out, out, out], out_shape=[shape, shape, shape],
        compiler_params=_params("parallel", "parallel"),
    )(h, w_fi, w_fi)


def _ffn_out_bwd(dffn, w_fo, a_act, up_act, bm=512):
    s, d = dffn.shape
    ff = w_fo.shape[0]
    bm, bn = _pick(s, bm), _half_width(ff // 2)

    def body(d_ref, w_ref, a_ref, u_ref, da_ref, du_ref):
        dhf = _dot(d_ref[...], w_ref[...], NT)
        a = a_ref[...].astype(F32)
        sa = _sigmoid(a)
        da_ref[...] = (dhf * u_ref[...].astype(F32) * sa * (1.0 + a * (1.0 - sa))).astype(BF16)
        du_ref[...] = (dhf * a * sa).astype(BF16)

    tile = pl.BlockSpec((bm, bn), lambda j, i: (i, j))
    shape = jax.ShapeDtypeStruct((s, ff), BF16)
    return _pcall(
        body, name="ffn_out_bwd", grid=(ff // bn, s // bm),
        in_specs=[pl.BlockSpec((bm, d), lambda j, i: (i, 0)), pl.BlockSpec((bn, d), lambda j, i: (j, 0)), tile, tile],
        out_specs=[tile, tile], out_shape=[shape, shape], compiler_params=_params("parallel", "parallel"),
    )(dffn, w_fo, a_act, up_act)


def _tril(n):
    return lax.broadcasted_iota(jnp.int32, (n, n), 0) >= lax.broadcasted_iota(jnp.int32, (n, n), 1)


def _gmlp_norm(v, ln_g, ln_b):
    gv = _gelu(v)
    mu = jnp.mean(gv, axis=-1, keepdims=True)
    cen = gv - mu
    rs = lax.rsqrt(jnp.mean(cen * cen, axis=-1, keepdims=True) + EPS)
    xh = cen * rs
    return xh, rs, xh * ln_g + ln_b


def _gmlp_fwd(z, ln_g, ln_b, ws, bs_t):
    s = z.shape[0]
    gw = ln_g.shape[1]
    groups, chunk, _ = ws.shape
    per = GMLP_CHUNKS_PER_STEP if s % (GMLP_CHUNKS_PER_STEP * chunk) == 0 else 1
    rows = per * chunk

    def body(u_ref, v_ref, lg_ref, lb_ref, ws_ref, bs_ref, ya_ref):
        mask = _tril(chunk)
        for cc in range(per):
            rs_ = slice(cc * chunk, (cc + 1) * chunk)
            gu = _gelu(u_ref[rs_, :])
            _, _, vn = _gmlp_norm(v_ref[rs_, :], lg_ref[...], lb_ref[...])
            for g in range(groups):
                cols = slice(g * LANES, (g + 1) * LANES)
                wm = jnp.where(mask, ws_ref[g], 0.0).astype(BF16)
                sg = _dot(wm, vn[:, cols].astype(BF16)) + bs_ref[:, g:g + 1]
                ya_ref[rs_, cols] = (gu[:, cols] * sg).astype(BF16)

    return _pcall(
        body, name="gmlp_fwd", grid=(s // rows,),
        in_specs=[_row_spec(rows, gw, 0), _row_spec(rows, gw, 1), _vec_spec(gw), _vec_spec(gw),
                  pl.BlockSpec((groups, chunk, chunk), lambda i: (0, 0, 0)), pl.BlockSpec((chunk, LANES), lambda i: (0, 0))],
        out_specs=_row_spec(rows, gw), out_shape=jax.ShapeDtypeStruct((s, gw), BF16),
        compiler_params=_params("parallel"),
    )(z, z, ln_g, ln_b, ws, bs_t)


def _gmlp_bwd(dya, z, ln_g, ln_b, ws, bs_t):
    s = z.shape[0]
    gw = ln_g.shape[1]
    groups, chunk, _ = ws.shape
    per = GMLP_CHUNKS_PER_STEP if s % (GMLP_CHUNKS_PER_STEP * chunk) == 0 else 1
    rows = per * chunk

    def body(dya_ref, u_ref, v_ref, lg_ref, lb_ref, ws_ref, bs_ref, duv_ref, dws_ref, dbs_ref, dlg_ref, dlb_ref, dvn_ref):
        first = pl.program_id(0) == 0
        lg = lg_ref[...]
        mask = _tril(chunk)
        lane = lax.broadcasted_iota(jnp.int32, (chunk, LANES), 1)
        dbs = jnp.zeros((chunk, LANES), F32)
        dws = [jnp.zeros((chunk, chunk), F32) for _ in range(groups)]
        dlb = jnp.zeros((1, gw), F32)
        dlg = jnp.zeros((1, gw), F32)
        for cc in range(per):
            rs_ = slice(cc * chunk, (cc + 1) * chunk)
            u, v = u_ref[rs_, :], v_ref[rs_, :]
            gu = _gelu(u)
            xh, rs, vn = _gmlp_norm(v, lg, lb_ref[...])
            dyav = dya_ref[rs_, :]
            for g in range(groups):
                cols = slice(g * LANES, (g + 1) * LANES)
                wm = jnp.where(mask, ws_ref[g], 0.0).astype(BF16)
                vg = vn[:, cols].astype(BF16)
                sg = _dot(wm, vg) + bs_ref[:, g:g + 1]
                ds = dyav[:, cols] * gu[:, cols]
                duv_ref[rs_, cols] = (dyav[:, cols] * sg * _gelu_grad(u[:, cols])).astype(BF16)
                dsb = ds.astype(BF16)
                dws[g] = dws[g] + jnp.where(mask, _dot(dsb, vg, NT), 0.0)
                dbs = dbs + jnp.where(lane == g, jnp.sum(ds, axis=-1, keepdims=True), 0.0)
                dvn_ref[rs_, cols] = _dot(wm, dsb, TN)
            dvn = dvn_ref[rs_, :]
            dlb = dlb + _colsum(dvn)
            dlg = dlg + _colsum(dvn * xh)
            dxh = dvn * lg
            dgv = rs * (dxh - jnp.mean(dxh, axis=-1, keepdims=True) - xh * jnp.mean(dxh * xh, axis=-1, keepdims=True))
            duv_ref[rs_, gw:] = (dgv * _gelu_grad(v)).astype(BF16)
        for g in range(groups):
            _accumulate(first, dws_ref.at[g], dws[g])
        _accumulate(first, dbs_ref, dbs)
        _accumulate(first, dlb_ref, dlb)
        _accumulate(first, dlg_ref, dlg)

    return _pcall(
        body, name="gmlp_bwd", grid=(s // rows,),
        in_specs=[_row_spec(rows, gw), _row_spec(rows, gw, 0), _row_spec(rows, gw, 1), _vec_spec(gw), _vec_spec(gw),
                  pl.BlockSpec((groups, chunk, chunk), lambda i: (0, 0, 0)), pl.BlockSpec((chunk, LANES), lambda i: (0, 0))],
        out_specs=[_row_spec(rows, 2 * gw), pl.BlockSpec((groups, chunk, chunk), lambda i: (0, 0, 0)),
                   pl.BlockSpec((chunk, LANES), lambda i: (0, 0)), _vec_spec(gw), _vec_spec(gw)],
        out_shape=[jax.ShapeDtypeStruct((s, 2 * gw), BF16), jax.ShapeDtypeStruct((groups, chunk, chunk), F32),
                   jax.ShapeDtypeStruct((chunk, LANES), F32), jax.ShapeDtypeStruct((1, gw), F32),
                   jax.ShapeDtypeStruct((1, gw), F32)],
        scratch_shapes=[pltpu.VMEM((rows, gw), F32)],
        compiler_params=_params("arbitrary"),
    )(dya, z, z, ln_g, ln_b, ws, bs_t)


def _lower_bound(lb_ref):
    a0, a1 = lb_ref[0:1, :], lb_ref[1:2, :]
    mx = jnp.maximum(a0, a1)
    e0, e1 = jnp.exp(a0 - mx), jnp.exp(a1 - mx)
    return e0 / (e0 + e1)


def _sum_dot(mask, x):
    hi = x.astype(BF16)
    rest = x - hi.astype(F32)
    mid = rest.astype(BF16)
    low = (rest - mid.astype(F32)).astype(BF16)
    return _dot(mask, hi) + _dot(mask, mid) + _dot(mask, low)


def _ones_where(mask):
    return jnp.where(mask, 1.0, 0.0).astype(BF16)


def _hg_masks(rows, t):
    r = lax.broadcasted_iota(jnp.int32, (rows, rows), 0)
    c = lax.broadcasted_iota(jnp.int32, (rows, rows), 1)
    same = (r // t) == (c // t)
    incl = jnp.logical_and(same, c <= r)
    upto_mid = jnp.logical_and(same, (c % t) <= t // 2)
    rev = jnp.logical_and(same, c >= r)
    return same, incl, upto_mid, rev


def _hg_block(q, fp, lb, masks):
    rows = q.shape[0]
    same, incl, upto_mid, _ = masks
    sig = _sigmoid(fp)
    f = lb + (1.0 - lb) * sig
    k = 1.0 - f
    sq = _sigmoid(q)
    qa = q * sq
    stacked = jnp.concatenate([_ones_where(m) for m in (incl, same, upto_mid)], axis=0)
    sums = _sum_dot(stacked, jnp.log(f))
    b, b_last, b_mid = sums[:rows], sums[rows:2 * rows], sums[2 * rows:]
    e_q = jnp.exp(jnp.minimum(b - b_mid, EXP_CLAMP))
    e_k = jnp.exp(jnp.minimum(b_mid - b, EXP_CLAMP))
    e_in = jnp.exp(b)
    e_out = jnp.exp(b_last - b)
    return dict(sig=sig, f=f, k=k, sq=sq, qa=qa, e_last=jnp.exp(b_last), e_q=e_q, e_k=e_k, e_in=e_in, e_out=e_out,
                q_hat=(qa * e_q).astype(BF16), k_hat=(k * e_k).astype(BF16),
                q_in=(qa * e_in).astype(BF16), k_out=k * e_out)


def _hgrn_fwd(z, hg_lb, norm_g, q_col):
    s = z.shape[0]
    hw = norm_g.shape[1]
    heads = hw // LANES
    t = HG_CHUNK
    rows = min(HG_ROWS, s)
    per_step = rows // t
    hp = min(HG_HEADS_PER_STEP, heads)
    assert heads % hp == 0 and q_col % hp == 0, (heads, q_col)
    wide = hp * LANES

    def zspec(which):
        return pl.BlockSpec((rows, wide), lambda h, r, which=which: (r, (q_col + which * heads) // hp + h))

    def body(q_ref, f_ref, i_ref, g_ref, lb_ref, ng_ref, yb_ref, o_ref, st_out_ref, st_ref, e_last_ref, inter_ref):
        @pl.when(pl.program_id(1) == 0)
        def _():
            st_ref[...] = jnp.zeros_like(st_ref)

        masks = _hg_masks(rows, t)
        for hh in range(hp):
            cols = slice(hh * LANES, (hh + 1) * LANES)
            blk = _hg_block(q_ref[:, cols], f_ref[:, cols], _lower_bound(lb_ref.at[:, cols]), masks)
            iv = i_ref[:, cols].astype(BF16)
            q_in, k_out = blk["q_in"], blk["k_out"].astype(BF16)
            e_last_ref[hh] = blk["e_last"]
            attn = jnp.where(masks[1], _dot(blk["q_hat"], blk["k_hat"], NT), 0.0).astype(BF16)
            o = _dot(attn, iv)
            grown = [_dot(iv[j * t:(j + 1) * t], k_out[j * t:(j + 1) * t], TN) for j in range(per_step)]
            st = st_ref[hh]
            for j in range(per_step):
                st_out_ref[hh, j] = st
                inter_ref[hh, j * t:(j + 1) * t, :] = _dot(q_in[j * t:(j + 1) * t], st.astype(BF16), NT)
                st = st * e_last_ref[hh, j * t:j * t + 1, :] + grown[j]
            st_ref[hh] = st
            o = o + inter_ref[hh]
            o_ref[:, cols] = o
            og = g_ref[:, cols]
            on, _ = _rms(o)
            yb_ref[:, cols] = (on * ng_ref[:, cols] * (og * _sigmoid(og))).astype(BF16)

    out_row = pl.BlockSpec((rows, wide), lambda h, r: (r, h))
    return _pcall(
        body, name="hgrn_fwd", grid=(heads // hp, s // rows),
        in_specs=[zspec(0), zspec(1), zspec(2), zspec(3),
                  pl.BlockSpec((2, wide), lambda h, r: (0, h)), pl.BlockSpec((1, wide), lambda h, r: (0, h))],
        out_specs=[out_row, out_row, pl.BlockSpec((hp, per_step, LANES, LANES), lambda h, r: (h, r, 0, 0))],
        out_shape=[jax.ShapeDtypeStruct((s, hw), BF16), jax.ShapeDtypeStruct((s, hw), F32),
                   jax.ShapeDtypeStruct((heads, s // t, LANES, LANES), F32)],
        scratch_shapes=[pltpu.VMEM((hp, LANES, LANES), F32), pltpu.VMEM((hp, rows, LANES), F32),
                        pltpu.VMEM((hp, rows, LANES), F32)],
        compiler_params=_params("parallel", "arbitrary"),
    )(z, z, z, z, hg_lb, norm_g)


def _hgrn_bwd(dyb, z, o_raw, states, hg_lb, norm_g, q_col):
    s = z.shape[0]
    hw = norm_g.shape[1]
    heads = hw // LANES
    t = HG_CHUNK
    rows = min(HG_ROWS, s)
    per_step = rows // t
    n_steps = s // rows
    hp = min(HG_HEADS_PER_STEP, heads)
    assert heads % hp == 0 and q_col % hp == 0, (heads, q_col)
    wide = hp * LANES

    def zspec(which):
        return pl.BlockSpec((rows, wide), lambda h, r, which=which: (n_steps - 1 - r, (q_col + which * heads) // hp + h))

    def body(dyb_ref, q_ref, f_ref, i_ref, g_ref, o_ref, st_in_ref, lb_ref, ng_ref,
             dq_ref, df_ref, di_ref, dg_ref, dlb_ref, dng_ref, dst_ref, acc_lb_ref, acc_ng_ref,
             e_last_ref, dq_in_ref, dk_out_ref, di_inter_ref, carry_ref):
        step = pl.program_id(1)

        @pl.when(step == 0)
        def _():
            dst_ref[...] = jnp.zeros_like(dst_ref)
            acc_lb_ref[...] = jnp.zeros_like(acc_lb_ref)
            acc_ng_ref[...] = jnp.zeros_like(acc_ng_ref)

        masks = _hg_masks(rows, t)
        same, incl, _, rev = masks
        sum_mask = jnp.concatenate([_ones_where(rev), _ones_where(same)], axis=1)
        for hh in range(hp):
            cols = slice(hh * LANES, (hh + 1) * LANES)
            lb = _lower_bound(lb_ref.at[:, cols])
            ng = ng_ref[:, cols]
            q = q_ref[:, cols]
            blk = _hg_block(q, f_ref[:, cols], lb, masks)
            iv = i_ref[:, cols].astype(BF16)
            o, og, dy = o_ref[:, cols], g_ref[:, cols], dyb_ref[:, cols]
            so = _sigmoid(og)
            on, r = _rms(o)
            acc_ng_ref[:, cols] += _colsum(dy * on * (og * so))
            dg_ref[:, cols] = (dy * on * ng * so * (1.0 + og * (1.0 - so))).astype(BF16)
            don = dy * ng * (og * so)
            do = (r * (don - on * jnp.mean(don * on, axis=-1, keepdims=True))).astype(BF16)
            q_hat, k_hat, q_in, k_out = blk["q_hat"], blk["k_hat"], blk["q_in"], blk["k_out"]
            k_out_b = k_out.astype(BF16)
            attn = jnp.where(incl, _dot(q_hat, k_hat, NT), 0.0).astype(BF16)
            d_attn = jnp.where(incl, _dot(do, iv, NT), 0.0).astype(BF16)
            di_intra = _dot(attn, do, TN)
            dq_hat = _dot(d_attn, k_hat)
            dk_hat = _dot(d_attn, q_hat, TN)
            e_last_ref[hh] = blk["e_last"]
            grown = [_dot(do[j * t:(j + 1) * t], q_in[j * t:(j + 1) * t], TN) for j in range(per_step)]
            dst = dst_ref[hh]
            for j in reversed(range(per_step)):
                rs_ = slice(j * t, (j + 1) * t)
                e_last = e_last_ref[hh, j * t:j * t + 1, :]
                st_prev, dst_b = st_in_ref[hh, j], dst.astype(BF16)
                dq_in_ref[hh, rs_, :] = _dot(do[rs_], st_prev.astype(BF16))
                dk_out_ref[hh, rs_, :] = _dot(iv[rs_], dst_b)
                di_inter_ref[hh, rs_, :] = _dot(k_out_b[rs_], dst_b, NT)
                carry_ref[hh, rs_, :] = jnp.broadcast_to(e_last * _colsum(st_prev * dst), (t, LANES))
                dst = dst * e_last + grown[j]
            dst_ref[hh] = dst
            di_ref[:, cols] = (di_intra + di_inter_ref[hh]).astype(BF16)
            dk_out = dk_out_ref[hh]
            dqa = dq_in_ref[hh] * blk["e_in"] + dq_hat * blk["e_q"]
            dk = dk_out * blk["e_out"] + dk_hat * blk["e_k"]
            db = blk["qa"] * dqa - blk["k"] * dk
            dlf = _sum_dot(sum_mask, jnp.concatenate([db, dk_out * k_out], axis=0)) + carry_ref[hh]
            dfv = dlf / blk["f"] - dk
            sig, sq = blk["sig"], blk["sq"]
            df_ref[:, cols] = (dfv * (1.0 - lb) * sig * (1.0 - sig)).astype(BF16)
            acc_lb_ref[:, cols] += _colsum(dfv * (1.0 - sig))
            dq_ref[:, cols] = (dqa * sq * (1.0 + q * (1.0 - sq))).astype(BF16)

        @pl.when(step == n_steps - 1)
        def _():
            lb = _lower_bound(lb_ref)
            d0 = acc_lb_ref[...] * lb * (1.0 - lb)
            dlb_ref[0:1, :] = d0
            dlb_ref[1:2, :] = -d0
            dng_ref[...] = acc_ng_ref[...]

    rev_row = pl.BlockSpec((rows, wide), lambda h, r: (n_steps - 1 - r, h))
    piece = jax.ShapeDtypeStruct((s, hw), BF16)
    return _pcall(
        body, name="hgrn_bwd", grid=(heads // hp, n_steps),
        in_specs=[rev_row, zspec(0), zspec(1), zspec(2), zspec(3), rev_row,
                  pl.BlockSpec((hp, per_step, LANES, LANES), lambda h, r: (h, n_steps - 1 - r, 0, 0)),
                  pl.BlockSpec((2, wide), lambda h, r: (0, h)), pl.BlockSpec((1, wide), lambda h, r: (0, h))],
        out_specs=[rev_row, rev_row, rev_row, rev_row,
                   pl.BlockSpec((2, wide), lambda h, r: (0, h)), pl.BlockSpec((1, wide), lambda h, r: (0, h))],
        out_shape=[piece, piece, piece, piece, jax.ShapeDtypeStruct((2, hw), F32), jax.ShapeDtypeStruct((1, hw), F32)],
        scratch_shapes=[pltpu.VMEM((hp, LANES, LANES), F32), pltpu.VMEM((1, wide), F32), pltpu.VMEM((1, wide), F32)]
        + [pltpu.VMEM((hp, rows, LANES), F32)] * 5,
        compiler_params=_params("parallel", "arbitrary"),
    )(dyb, z, z, z, z, o_raw, states, hg_lb, norm_g)


def _adam_update(w, m, v, g):
    m2 = ADAM_B1 * m + (1.0 - ADAM_B1) * g
    v2 = ADAM_B2 * v + (1.0 - ADAM_B2) * (g * g)
    m_hat = m2 * (1.0 / (1.0 - ADAM_B1 ** ADAM_STEP))
    v_hat = v2 * (1.0 / (1.0 - ADAM_B2 ** ADAM_STEP))
    return -ADAM_LR * (m_hat / (jnp.sqrt(v_hat) + ADAM_EPS) + ADAM_WD * w), m2, v2


def _adamw(w, m, v, parts, name, outer=False):
    rows, cols = w.shape
    bc = cols if cols <= 4096 else _pick(cols, 4096)
    rb = _pick_rows(rows, max(8, (384 * 1024) // bc), mult=8)
    if outer and rb % LANES:
        rb = rows

    def body(w_ref, m_ref, v_ref, *refs):
        g_ref, d_ref, mo_ref, vo_ref = refs[-4:]
        if outer:
            cv = refs[0][...]
            g = _dot(cv * _sigmoid(cv), refs[1][...], TN, lax.Precision.HIGHEST)
        else:
            p_ref = refs[0]
            g = p_ref[0].astype(F32)
            for p in range(1, p_ref.shape[0]):
                g = g + p_ref[p].astype(F32)
        g_ref[...] = g
        d_ref[...], mo_ref[...], vo_ref[...] = _adam_update(w_ref[...], m_ref[...], v_ref[...], g)

    blk = pl.BlockSpec((rb, bc), lambda i, j: (i, j))
    out = jax.ShapeDtypeStruct((rows, cols), F32)
    if outer:
        grad_specs = [pl.BlockSpec((N_DEV, rb), lambda i, j: (0, i)), pl.BlockSpec((N_DEV, bc), lambda i, j: (0, j))]
        grad_ins = tuple(parts)
    else:
        grad_specs = [pl.BlockSpec((parts.shape[0], rb, bc), lambda i, j: (0, i, j))]
        grad_ins = (parts,)
    return _pcall(
        body, name=name, grid=(rows // rb, cols // bc), in_specs=[blk, blk, blk] + grad_specs,
        out_specs=[blk] * 4, out_shape=[out] * 4, compiler_params=_params("parallel", "parallel"),
    )(w, m, v, *grad_ins)


SMALL = ("b_ada", "norm1_g", "b_gate", "gmlp_ln_g", "gmlp_ln_b", "gmlp_ws", "gmlp_bs", "hg_lb", "hg_norm_g",
         "norm2_g", "final_norm_g")
BIG = ("w_in", "w_branch_gmlp", "w_branch_hg", "w_out", "w_ffn_in", "w_ffn_out")
WEIGHTS = ("w_ada", "b_ada", "norm1_g", "w_in", "b_gate", "gmlp_ln_g", "gmlp_ln_b", "gmlp_ws", "gmlp_bs", "hg_lb",
           "hg_norm_g", "w_branch_gmlp", "w_branch_hg", "w_out", "norm2_g", "w_ffn_in", "w_ffn_out", "final_norm_g")


def _pack(parts):
    return jnp.concatenate([p.reshape(-1, LANES) for p in parts], axis=0)


def _step(x, c, loss_target, w, m, v):
    s, d = x.shape[1], x.shape[2]
    gw = w["gmlp_ln_g"].shape[-1]
    hw = w["hg_norm_g"].shape[-1]
    x2d, tgt = x[0], loss_target[0]
    mx, my, mc = lax.axis_index("x"), lax.axis_index("y"), lax.axis_index("c")
    chip = 2 * mx + my
    dev = 2 * chip + mc
    q_col = 2 * gw // LANES
    gate_col = (2 * gw + 4 * hw) // d
    place = jnp.stack([chip, mc]).astype(jnp.int32)
    _Order.last = None

    c_all = _all_gather8(c.reshape(-1, LANES), "gather_c").reshape(N_DEV, d)
    gather_in = _WeightGather("in", _WeightGather.zones([w["w_in"][0]], chip), ring=True)
    zones_mix = _WeightGather.zones([w[n][0] for n in ("w_branch_gmlp", "w_branch_hg", "w_out")], chip)
    zones_fi = _WeightGather.zones([w["w_ffn_in"][0]], chip)
    zones_fo = _WeightGather.zones([w["w_ffn_out"][0]], chip)
    _Order.also = (*zones_mix, *zones_fi, *zones_fo)
    n_ada = w["w_ada"].shape[-1]
    b_ada_q = lax.dynamic_slice(w["b_ada"], (0, chip * n_ada), (1, n_ada))
    mod_q = _ada_fwd(c_all, w["w_ada"][0], b_ada_q)
    mod_all = _all_gather8(mod_q, "gather_mod")
    mod = lax.dynamic_index_in_dim(mod_all, dev, axis=1, keepdims=False)[::2].reshape(1, 6 * d)
    sh1, sc1, gt1, sh2, sc2, gt2 = [mod[:, i * d:(i + 1) * d] for i in range(6)]

    gather_in.relay_and_pass_near()
    gather_mix = _WeightGather("mix", zones_mix, ring=True)
    gather_fi = _WeightGather("fi", zones_fi, ring=True)

    norm1_g, norm2_g, final_g = w["norm1_g"], w["norm2_g"], w["final_norm_g"].reshape(1, d)
    ln_g, ln_b = w["gmlp_ln_g"], w["gmlp_ln_b"]
    ws = w["gmlp_ws"][0]
    groups = ws.shape[0]
    bs_t = jnp.pad(w["gmlp_bs"][0].T, ((0, 0), (0, LANES - groups)))
    hg_lb, hg_ng, b_gate = w["hg_lb"], w["hg_norm_g"], w["b_gate"]

    h1 = _norm_mod_fwd(x2d, norm1_g, sc1, sh1, "norm1_fwd")
    order = jnp.stack([chip, 2 * (1 - mx) + my, 2 * mx + (1 - my), 2 * (1 - mx) + (1 - my)]).astype(jnp.int32)
    w_in, = gather_in.near_done()
    z = _matmul_quarters(h1, w_in, order, 0, 3, "mm_z_near")
    w_in, = gather_in.pass_far()
    z = _matmul_quarters(h1, w_in, order, 3, 1, "mm_z_far", prev=z)
    gather_mix.relay()
    ya = _gmlp_fwd(z, ln_g, ln_b, ws, bs_t)
    yb, o_raw, states = _hgrn_fwd(z, hg_lb, hg_ng, q_col)
    gather_mix.pass_on()
    gather_fi.relay()
    gather_fo = _WeightGather("fo", zones_fo)
    w_bg, w_bh, w_out = gather_mix.done()
    w_bg, w_bh = [wq.transpose(1, 0, 2).reshape(wq.shape[1], -1) for wq in (w_bg, w_bh)]
    w_out = w_out.reshape(-1, w_out.shape[-1])
    pa, pb, y = _branch_merge(ya, yb, w_bg, w_bh, z, b_gate, gate_col)
    yo = _matmul(y, w_out, mode="nn", name="mm_yo", out_dtype=F32)
    gather_fi.pass_on()
    x1, h2 = _norm_mod_fwd(x2d, norm2_g, sc2, sh2, "norm2_fwd", res=yo, gt=gt1)
    w_fi, = gather_fi.done()
    a_act, up_act, hf = _ffn_in(h2, w_fi)
    gather_fo.pass_on()
    w_fo, = gather_fo.done()
    w_fo = w_fo.reshape(-1, w_fo.shape[-1])
    ffn = _matmul(hf, w_fo, mode="nn", name="mm_ffn", out_dtype=F32)
    dx2, dffn, loss_row, d_final_g, d_gt2 = _final_loss(x1, ffn, gt2, final_g, tgt)

    g_fo = _matmul(hf, dffn, mode="tn", name="mm_g_fo", out_dtype=BF16, bm=_half_width(hf.shape[1] // 2))
    daup = tuple(_ffn_out_bwd(dffn, w_fo, a_act, up_act))
    g_fi = _matmul(h2, daup, mode="tn", name="mm_g_fi", out_dtype=BF16, out_slots=True, bn=_half_width(w_fi.shape[2]))
    red_ffn = _GradReduce("ffn", [g_fo.reshape(N_CHIPS, -1, g_fo.shape[-1]), g_fi])
    dh2 = _matmul(daup, w_fi, mode="nt", name="mm_dh2", out_dtype=F32, b_slots=True)
    red_ffn.step(place)
    dx1, dyo, d_sh2, d_sc2, d_norm2, d_gt1 = _norm_mod_bwd(dh2, x1, dx2, norm2_g, sc2, "norm2_bwd", branch=yo, gt=gt1)
    g_out = _matmul(y, dyo, mode="tn", name="mm_g_out", out_dtype=BF16)
    dpa, dpb, dz_ga, dz_gb, d_b_ga, d_b_gb = _out_bwd(dyo, w_out, pa, pb, z, b_gate, gate_col)
    g_bg, g_bh = _branch_wgrad(ya, yb, dpa, dpb)
    red_mix = _GradReduce("mix", [g_out.reshape(N_CHIPS, -1, g_out.shape[-1]), g_bg, g_bh])
    dya, dyb = _branch_bwd(dpa, dpb, w_bg, w_bh)
    red_mix.step(place)
    dz_uv, d_ws, d_bs_t, d_ln_g, d_ln_b = _gmlp_bwd(dya, z, ln_g, ln_b, ws, bs_t)
    dz_q, dz_f, dz_i, dz_g, d_hg_lb, d_hg_ng = _hgrn_bwd(dyb, z, o_raw, states, hg_lb, hg_ng, q_col)
    dz = jnp.concatenate([dz_uv, dz_q, dz_f, dz_i, dz_g, dz_ga, dz_gb], axis=1)
    grad, delta, new_m, new_v = {}, {}, {}, {}

    def update(n, parts, outer=False):
        outs = _adamw(w[n][0], m[n][0], v[n][0], parts, "adamw_" + n, outer=outer)
        grad[n], delta[n], new_m[n], new_v[n] = [o[None] for o in outs]

    g_in = _matmul(h1, dz, mode="tn", name="mm_g_in", out_dtype=BF16, out_slots=True, bn=_half_width(w_in.shape[2]))
    red_in = _GradReduce("in", [g_in])
    red_ffn.join(place)
    red_mix.join(place)
    red_in.step(place)
    g_fo, g_fi = red_ffn.done()
    n = "w_ffn_in"
    dh1, rode = _matmul(dz, w_in, mode="nt", name="mm_dh1", out_dtype=F32, b_slots=True,
                        rider=([w[n][0], m[n][0], v[n][0], g_fi], lambda *blk: (blk[3], *_adam_update(*blk)), 4))
    grad[n], delta[n], new_m[n], new_v[n] = [o[None] for o in rode]
    grad_x, d_sh1, d_sc1, d_norm1 = _norm_mod_bwd(dh1, x2d, dx1, norm1_g, sc1, "norm1_bwd")

    d_mod = jnp.concatenate([d_sh1, d_sc1, d_gt1, d_sh2, d_sc2, d_gt2], axis=1)
    small_part = {"b_ada": d_mod, "norm1_g": d_norm1, "b_gate": jnp.concatenate([d_b_ga, d_b_gb], axis=1), "gmlp_ln_g": d_ln_g, "gmlp_ln_b": d_ln_b,
                  "gmlp_ws": d_ws, "gmlp_bs": d_bs_t[:, :groups].T, "hg_lb": d_hg_lb, "hg_norm_g": d_hg_ng,
                  "norm2_g": d_norm2, "final_norm_g": d_final_g}
    loss_rows = jnp.broadcast_to(loss_row, (8, LANES))
    packed = _pack([small_part[n] for n in SMALL] + [loss_rows])
    zone = lax.dynamic_update_slice(lax.empty((N_DEV,) + packed.shape, F32), packed[None], (dev, 0, 0))
    small_sems, small_bufs = _xfer_start("gsmall_start", [zone], _plan_gather_out, 4, after_last=True)
    update("w_ffn_out", g_fo[None])
    g_out, g_bg, g_bh = red_mix.done()
    update("w_out", g_out[None])
    red_in.join(place)
    (small_sems,), small_bufs = _xfer_hand_over("gsmall_pass", small_sems, small_bufs, _plan_gather_out,
                                                [(_plan_gather_on, 3)])
    update("w_branch_gmlp", g_bg[None])
    update("w_branch_hg", g_bh[None])
    small_all, = _xfer_wait("gsmall_wait", small_sems, small_bufs, _plan_gather_on)
    d_mod_all = small_all[:, :6 * d // LANES].reshape(N_DEV, 6 * d)
    d_mod_q = lax.dynamic_slice(d_mod_all, (0, chip * n_ada), (N_DEV, n_ada))
    pad = [jnp.zeros_like(loss_rows)]
    outs = _adamw(_pack([w[n] for n in SMALL] + pad), _pack([m[n] for n in SMALL] + pad),
                  _pack([v[n] for n in SMALL] + pad), small_all, "adamw_small")
    update("w_ada", (c_all, d_mod_q), outer=True)
    update("w_in", red_in.done()[0][None])
    row = 0
    for n in SMALL:
        cnt = w[n].size // LANES
        for dst, o in zip((grad, delta, new_m, new_v), outs):
            dst[n] = o[row:row + cnt].reshape(w[n].shape)
        row += cnt

    loss = outs[0][row, 0]
    return (loss, grad_x[None], *[grad[n] for n in WEIGHTS], *[delta[n] for n in WEIGHTS],
            *[new_m[n] for n in WEIGHTS], *[new_v[n] for n in WEIGHTS])


def kernel(x, c, w_ada, b_ada, norm1_g, w_in, b_gate, gmlp_ln_g, gmlp_ln_b, gmlp_ws, gmlp_bs, hg_lb, hg_norm_g, w_branch_gmlp, w_branch_hg, w_out, norm2_g, w_ffn_in, w_ffn_out, final_norm_g, loss_target, m_w_ada, m_b_ada, m_norm1_g, m_w_in, m_b_gate, m_gmlp_ln_g, m_gmlp_ln_b, m_gmlp_ws, m_gmlp_bs, m_hg_lb, m_hg_norm_g, m_w_branch_gmlp, m_w_branch_hg, m_w_out, m_norm2_g, m_w_ffn_in, m_w_ffn_out, m_final_norm_g, v_w_ada, v_b_ada, v_norm1_g, v_w_in, v_b_gate, v_gmlp_ln_g, v_gmlp_ln_b, v_gmlp_ws, v_gmlp_bs, v_hg_lb, v_hg_norm_g, v_w_branch_gmlp, v_w_branch_hg, v_w_out, v_norm2_g, v_w_ffn_in, v_w_ffn_out, v_final_norm_g):
    w = dict(w_ada=w_ada, b_ada=b_ada, norm1_g=norm1_g, w_in=w_in, b_gate=b_gate, gmlp_ln_g=gmlp_ln_g,
             gmlp_ln_b=gmlp_ln_b, gmlp_ws=gmlp_ws, gmlp_bs=gmlp_bs, hg_lb=hg_lb, hg_norm_g=hg_norm_g,
             w_branch_gmlp=w_branch_gmlp, w_branch_hg=w_branch_hg, w_out=w_out, norm2_g=norm2_g,
             w_ffn_in=w_ffn_in, w_ffn_out=w_ffn_out, final_norm_g=final_norm_g)
    m = dict(w_ada=m_w_ada, b_ada=m_b_ada, norm1_g=m_norm1_g, w_in=m_w_in, b_gate=m_b_gate, gmlp_ln_g=m_gmlp_ln_g,
             gmlp_ln_b=m_gmlp_ln_b, gmlp_ws=m_gmlp_ws, gmlp_bs=m_gmlp_bs, hg_lb=m_hg_lb, hg_norm_g=m_hg_norm_g,
             w_branch_gmlp=m_w_branch_gmlp, w_branch_hg=m_w_branch_hg, w_out=m_w_out, norm2_g=m_norm2_g,
             w_ffn_in=m_w_ffn_in, w_ffn_out=m_w_ffn_out, final_norm_g=m_final_norm_g)
    v = dict(w_ada=v_w_ada, b_ada=v_b_ada, norm1_g=v_norm1_g, w_in=v_w_in, b_gate=v_b_gate, gmlp_ln_g=v_gmlp_ln_g,
             gmlp_ln_b=v_gmlp_ln_b, gmlp_ws=v_gmlp_ws, gmlp_bs=v_gmlp_bs, hg_lb=v_hg_lb, hg_norm_g=v_hg_norm_g,
             w_branch_gmlp=v_w_branch_gmlp, w_branch_hg=v_w_branch_hg, w_out=v_w_out, norm2_g=v_norm2_g,
             w_ffn_in=v_w_ffn_in, w_ffn_out=v_w_ffn_out, final_norm_g=v_final_norm_g)
    return _step(x, c, loss_target, w, m, v)
```

```python
import jax
import jax.numpy as jnp
from jax import lax
from jax.experimental import pallas as pl
from jax.experimental.pallas import tpu as pltpu

F32 = jnp.float32
BF16 = jnp.bfloat16
EPS = 1e-6
LANES = 128
N_CHIPS = 4
N_DEV = 8
VMEM_LIMIT_BYTES = 56 * 1024 * 1024
VMEM_RIDER_LIMIT_BYTES = 60 * 1024 * 1024
HG_CHUNK = 32
HG_ROWS = 256
HG_HEADS_PER_STEP = 8
GMLP_CHUNKS_PER_STEP = 4
EXP_CLAMP = 80.0
ADAM_LR, ADAM_B1, ADAM_B2, ADAM_EPS, ADAM_WD, ADAM_STEP = 0.001, 0.9, 0.999, 1e-08, 0.01, 10
MESH = pl.DeviceIdType.MESH

NN = (((1,), (0,)), ((), ()))
NT = (((1,), (1,)), ((), ()))
TN = (((0,), (0,)), ((), ()))


def _dot(a, b, dims=NN, precision=None):
    return lax.dot_general(a, b, dims, precision=precision, preferred_element_type=F32)


def _params(*semantics, vmem_limit_bytes=VMEM_LIMIT_BYTES):
    return pltpu.CompilerParams(dimension_semantics=semantics, vmem_limit_bytes=vmem_limit_bytes)


class _Order:
    last = None
    also = ()


def _pcall(body, *, in_specs, out_specs, grid=(), scratch_shapes=(), num_scalar_prefetch=0, **kw):
    def run(*ins):
        deps = (() if _Order.last is None else (_Order.last,)) + tuple(_Order.also)
        _Order.also = ()
        n_in, n_dep = len(ins), len(deps)

        def wrapped(*refs):
            body(*refs[:n_in], *refs[n_in + n_dep:])

        specs = list(in_specs) + [pl.BlockSpec(memory_space=pl.ANY)] * n_dep
        if num_scalar_prefetch:
            grid_spec = pltpu.PrefetchScalarGridSpec(
                num_scalar_prefetch=num_scalar_prefetch, grid=grid, in_specs=specs, out_specs=out_specs,
                scratch_shapes=scratch_shapes)
            outs = pl.pallas_call(wrapped, grid_spec=grid_spec, **kw)(*ins, *deps)
        else:
            outs = pl.pallas_call(wrapped, grid=grid, in_specs=specs, out_specs=out_specs,
                                  scratch_shapes=scratch_shapes, **kw)(*ins, *deps)
        _Order.last = jax.tree.leaves(outs)[0]
        return outs

    return run


def _pick_rows(dim, pref, mult=16):
    best = None
    for cand in range(mult, min(dim, pref) + 1, mult):
        if dim % cand == 0:
            best = cand
    assert best is not None, (dim, pref)
    return best


def _half_width(n):
    return n // 2 if n % (2 * LANES) == 0 else n


def _pick(dim, pref):
    if dim <= pref:
        return dim
    best = None
    for cand in range(LANES, pref + 1, LANES):
        if dim % cand == 0:
            best = cand
    assert best is not None, (dim, pref)
    return best


def _sigmoid(x):
    return 1.0 / (1.0 + jnp.exp(-x))


def _gelu(x):
    c = 0.7978845608028654
    return 0.5 * x * (1.0 + jnp.tanh(c * (x + 0.044715 * x * x * x)))


def _gelu_grad(x):
    c = 0.7978845608028654
    t = jnp.tanh(c * (x + 0.044715 * x * x * x))
    return 0.5 * (1.0 + t) + 0.5 * x * (1.0 - t * t) * c * (1.0 + 3.0 * 0.044715 * x * x)


def _rms(x):
    r = lax.rsqrt(jnp.mean(x * x, axis=-1, keepdims=True) + EPS)
    return x * r, r


def _colsum(x):
    return jnp.sum(x, axis=0, keepdims=True)


def _accumulate(first, ref, val):
    @pl.when(first)
    def _():
        ref[...] = val

    @pl.when(jnp.logical_not(first))
    def _():
        ref[...] += val


def _matmul(a, b, *, mode, name, out_dtype, b_slots=False, out_slots=False, bm=1024, bn=1024, bk=2816, rider=None):
    pair = a if isinstance(a, tuple) else b if isinstance(b, tuple) else None
    if mode == "nn":
        m, k = a.shape
        n = b.shape[2] * N_CHIPS if b_slots else b.shape[1]
        per = b.shape[2] if b_slots else n
    elif mode == "nt":
        m, k = (a[0].shape[0], 2 * a[0].shape[1]) if pair else a.shape
        n = b.shape[1] if b_slots else b.shape[0]
        per = b.shape[2] if b_slots else k
    else:
        k, m = a.shape
        n = 2 * b[0].shape[1] if pair else b.shape[1]
        per = n // N_CHIPS if out_slots else n
    bm = _pick(m, bm)
    if mode == "nt":
        bn, bk = _pick(n, bn), _pick(per, bk)
    else:
        bn, bk = _pick(per, bn), _pick(k, bk)
    nk = k // bk
    per_blocks = per // (bk if mode == "nt" else bn)
    dims = {"nn": NN, "nt": NT, "tn": TN}[mode]
    half = (nk if mode == "nt" else n // bn) // 2

    def product(a_ref, b_ref, o_ref, acc):
        part = _dot(a_ref[...], b_ref[...], dims)
        if nk == 1:
            o_ref[...] = part.astype(o_ref.dtype)
            return
        acc_ref, kk = acc[0], pl.program_id(2)

        @pl.when(kk == 0)
        def _():
            acc_ref[...] = part

        @pl.when(jnp.logical_and(kk > 0, kk < nk - 1))
        def _():
            acc_ref[...] += part

        @pl.when(kk == nk - 1)
        def _():
            o_ref[...] = (acc_ref[...] + part).astype(o_ref.dtype)

    n_ride, n_rode = (len(rider[0]), rider[2]) if rider else (0, 0)

    def body(*refs):
        if rider:
            ride_in, ride_out = refs[2:2 + n_ride], refs[3 + n_ride:3 + n_ride + n_rode]

            def ride(t, carry):
                rows = pl.ds(pl.multiple_of(t * 8, 8), 8)
                for ref, val in zip(ride_out, rider[1](*[r[rows, :] for r in ride_in])):
                    ref[rows, :] = val
                return carry

            lax.fori_loop(0, ride_in[0].shape[0] // 8, ride, 0)
        if not pair:
            return product(refs[0], refs[1], refs[2 + n_ride], refs[3 + n_ride + n_rode:])
        first = pl.program_id(2 if mode == "nt" else 1) < half
        x1, x2, y, o_ref, acc = refs[0], refs[1], refs[2], refs[3], refs[4:]

        @pl.when(first)
        def _():
            product(*((x1, y) if mode == "nt" else (y, x1)), o_ref, acc)

        @pl.when(jnp.logical_not(first))
        def _():
            product(*((x2, y) if mode == "nt" else (y, x2)), o_ref, acc)

    if mode == "nn":
        a_spec = pl.BlockSpec((bm, bk), lambda i, j, kk: (i, kk))
        if b_slots:
            b_spec = pl.BlockSpec((None, bk, bn), lambda i, j, kk: (j // per_blocks, kk, j % per_blocks))
        else:
            b_spec = pl.BlockSpec((bk, bn), lambda i, j, kk: (kk, j))
    elif mode == "nt":
        a_spec = pl.BlockSpec((bm, bk), lambda i, j, kk: (i, kk))
        if b_slots:
            b_spec = pl.BlockSpec((None, bn, bk), lambda i, j, kk: (kk // per_blocks, j, kk % per_blocks))
        else:
            b_spec = pl.BlockSpec((bn, bk), lambda i, j, kk: (j, kk))
    else:
        a_spec = pl.BlockSpec((bk, bm), lambda i, j, kk: (kk, i))
        b_spec = pl.BlockSpec((bk, bn), lambda i, j, kk: (kk, j))
    if out_slots:
        o_spec = pl.BlockSpec((None, bm, bn), lambda i, j, kk: (j // per_blocks, i, j % per_blocks))
        out_shape = jax.ShapeDtypeStruct((N_CHIPS, m, per), out_dtype)
    else:
        o_spec = pl.BlockSpec((bm, bn), lambda i, j, kk: (i, j))
        out_shape = jax.ShapeDtypeStruct((m, n), out_dtype)
    if not pair:
        ins, in_specs = (a, b), [a_spec, b_spec]
    elif mode == "nt":
        ins = (*a, b)
        in_specs = [pl.BlockSpec((bm, bk), lambda i, j, kk: (i, jnp.minimum(kk, half - 1))),
                    pl.BlockSpec((bm, bk), lambda i, j, kk: (i, jnp.maximum(kk - half, 0))), b_spec]
    else:
        ins = (*b, a)
        in_specs = [pl.BlockSpec((bk, bn), lambda i, j, kk: (kk, jnp.minimum(j, half - 1))),
                    pl.BlockSpec((bk, bn), lambda i, j, kk: (kk, jnp.maximum(j - half, 0))), a_spec]
    scratch = [pltpu.VMEM((bm, bn), F32)] if nk > 1 else []
    if not rider:
        return _pcall(
            body, name=name, grid=(m // bm, n // bn, nk), in_specs=in_specs, out_specs=o_spec, out_shape=out_shape,
            scratch_shapes=scratch, compiler_params=_params("parallel", "parallel", "arbitrary"),
        )(*ins)
    assert not pair
    rows, cols = rider[0][0].shape
    nj = n // bn
    rb = rows // ((m // bm) * nj * nk)
    assert rows == rb * (m // bm) * nj * nk and rb % 8 == 0, (rows, rb)
    ride_spec = pl.BlockSpec((rb, cols), lambda i, j, kk: ((i * nj + j) * nk + kk, 0))
    outs = _pcall(
        body, name=name, grid=(m // bm, nj, nk), in_specs=in_specs + [ride_spec] * n_ride,
        out_specs=[o_spec] + [ride_spec] * n_rode,
        out_shape=[out_shape] + [jax.ShapeDtypeStruct((rows, cols), F32)] * n_rode,
        scratch_shapes=scratch,
        compiler_params=_params("arbitrary", "arbitrary", "arbitrary", vmem_limit_bytes=VMEM_RIDER_LIMIT_BYTES),
    )(*ins, *rider[0])
    return outs[0], outs[1:]


def _matmul_quarters(h, w_slots, order, first, count, name, prev=None, bm=1024):
    s, k = h.shape
    n = w_slots.shape[2]
    bm, bn = _pick(s, bm), _half_width(n)
    pb = n // bn

    def body(order_ref, h_ref, w_ref, *rest):
        rest[-1][...] = _dot(h_ref[...], w_ref[...])

    ins = (order, h, w_slots) + (() if prev is None else (prev,))
    return _pcall(
        body, name=name, num_scalar_prefetch=1, grid=(count * pb, s // bm),
        in_specs=[pl.BlockSpec((bm, k), lambda j, i, o: (i, 0)),
                  pl.BlockSpec((None, k, bn), lambda j, i, o: (o[first + j // pb], 0, j % pb))]
        + ([] if prev is None else [ANY_SPEC]),
        out_specs=pl.BlockSpec((bm, bn), lambda j, i, o: (i, o[first + j // pb] * pb + j % pb)),
        out_shape=jax.ShapeDtypeStruct((s, N_CHIPS * n), F32),
        input_output_aliases={} if prev is None else {3: 0},
        compiler_params=_params("arbitrary", "arbitrary"),
    )(*ins)


def _place():
    x, y, c = lax.axis_index("x"), lax.axis_index("y"), lax.axis_index("c")
    chips = [(1 - x, y), (x, 1 - y), (1 - x, 1 - y)]
    return x, y, c, chips


def _all_gather8(block, name):
    def body(x_ref, out_ref, send_sems, recv_sems, local_sem):
        x, y, c, chips = _place()
        me, sibling = (x, y, c), (x, y, 1 - c)

        def slot(px, py, pc):
            return out_ref.at[4 * px + 2 * py + pc]

        def copy(k, blk, to, src=None):
            return pltpu.make_async_remote_copy(
                src_ref=slot(*blk) if src is None else src, dst_ref=slot(*blk),
                send_sem=send_sems.at[k], recv_sem=recv_sems.at[k], device_id=to, device_id_type=MESH)

        mine = pltpu.make_async_copy(x_ref, slot(*me), local_sem)
        mine.start()
        first = [copy(0, me, sibling, src=x_ref)]
        first += [copy(1 + j, me, (*chip, c), src=x_ref) for j, chip in enumerate(chips)]
        for cp in first:
            cp.start()
        passed = [copy(4 + j, (*chip, c), sibling) for j, chip in enumerate(chips)]
        for j, chip in enumerate(chips):
            copy(1 + j, (*chip, c), me).wait_recv()
            passed[j].start()
        copy(0, sibling, me).wait_recv()
        for j, chip in enumerate(chips):
            copy(4 + j, (*chip, 1 - c), me).wait_recv()
        for cp in first + passed:
            cp.wait_send()
        mine.wait()

    return _pcall(
        body, name=name, out_shape=jax.ShapeDtypeStruct((N_DEV,) + block.shape, block.dtype),
        in_specs=[pl.BlockSpec(memory_space=pltpu.VMEM)], out_specs=pl.BlockSpec(memory_space=pltpu.VMEM),
        scratch_shapes=[pltpu.SemaphoreType.DMA((7,)), pltpu.SemaphoreType.DMA((7,)), pltpu.SemaphoreType.DMA],
        compiler_params=pltpu.CompilerParams(vmem_limit_bytes=VMEM_LIMIT_BYTES),
    )(block)


HBM_SPEC = pl.BlockSpec(memory_space=pltpu.HBM)
SEM_SPEC = pl.BlockSpec(memory_space=pltpu.SEMAPHORE)
ANY_SPEC = pl.BlockSpec(memory_space=pl.ANY)
EFFECT = pltpu.SideEffectType.DATAFLOW_SIDE_EFFECTING


def _xfer_start(name, bufs, plan, n_copies, after_last=False):
    nb = len(bufs)
    deps = (_Order.last,) if after_last and _Order.last is not None else ()
    nd = len(deps)

    def body(*refs):
        send_sems, recv_sems = refs[nb + nd], refs[nb + nd + 1]
        token = refs[nb + nd + 2 + nb]
        for k, (src, dst, dev) in enumerate(plan(refs[:nb], *_place())):
            pltpu.make_async_remote_copy(src_ref=src, dst_ref=dst, send_sem=send_sems.at[k], recv_sem=recv_sems.at[k],
                                         device_id=dev, device_id_type=MESH).start()
        token[...] = jnp.zeros_like(token)

    outs = pl.pallas_call(
        body, name=name,
        out_shape=(pltpu.SemaphoreType.DMA((n_copies,)), pltpu.SemaphoreType.DMA((n_copies,)),
                   *[pltpu.HBM(b.shape, b.dtype) for b in bufs], jax.ShapeDtypeStruct((8, LANES), F32)),
        in_specs=[HBM_SPEC] * nb + [ANY_SPEC] * nd,
        out_specs=(SEM_SPEC, SEM_SPEC, *[HBM_SPEC] * nb, pl.BlockSpec(memory_space=pltpu.VMEM)),
        input_output_aliases={i: 2 + i for i in range(nb)},
        compiler_params=pltpu.CompilerParams(has_side_effects=EFFECT),
    )(*[pltpu.with_memory_space_constraint(b, pltpu.HBM) for b in bufs], *deps)
    _Order.last = outs[-1]
    return (outs[0], outs[1]), list(outs[2:2 + nb])


def _xfer_wait(name, sems, bufs, plan):
    nb = len(bufs)

    def body(*refs):
        send_sems, recv_sems = refs[nb], refs[nb + 1]
        for k, (src, dst, dev) in enumerate(plan(refs[:nb], *_place())):
            copy = pltpu.make_async_remote_copy(src_ref=src, dst_ref=dst, send_sem=send_sems.at[k],
                                                recv_sem=recv_sems.at[k], device_id=dev, device_id_type=MESH)
            copy.wait_send()
            copy.wait_recv()

    outs = pl.pallas_call(
        body, name=name, out_shape=tuple(pltpu.HBM(b.shape, b.dtype) for b in bufs),
        in_specs=[HBM_SPEC] * nb + [SEM_SPEC, SEM_SPEC, ANY_SPEC], out_specs=tuple([HBM_SPEC] * nb),
        input_output_aliases={i: i for i in range(nb)},
        compiler_params=pltpu.CompilerParams(has_side_effects=EFFECT),
    )(*bufs, *sems, _Order.last)
    _Order.last = outs[0]
    return list(outs)


def _xfer_hand_over(name, sems, bufs, plan, next_plans):
    nb, n_next = len(bufs), len(next_plans)

    def body(*refs):
        send_sems, recv_sems = refs[nb], refs[nb + 1]
        outs = refs[nb + 3:]
        place = _place()
        for k, (src, dst, dev) in enumerate(plan(refs[:nb], *place)):
            copy = pltpu.make_async_remote_copy(src_ref=src, dst_ref=dst, send_sem=send_sems.at[k],
                                                recv_sem=recv_sems.at[k], device_id=dev, device_id_type=MESH)
            copy.wait_send()
            copy.wait_recv()
        for p, (next_plan, _) in enumerate(next_plans):
            for k, (src, dst, dev) in enumerate(next_plan(refs[:nb], *place)):
                pltpu.make_async_remote_copy(src_ref=src, dst_ref=dst, send_sem=outs[2 * p].at[k],
                                             recv_sem=outs[2 * p + 1].at[k], device_id=dev, device_id_type=MESH).start()
        outs[-1][...] = jnp.zeros_like(outs[-1])

    sem_shapes = [pltpu.SemaphoreType.DMA((copies,)) for _, copies in next_plans for _ in range(2)]
    outs = pl.pallas_call(
        body, name=name,
        out_shape=(*sem_shapes, *[pltpu.HBM(b.shape, b.dtype) for b in bufs], jax.ShapeDtypeStruct((8, LANES), F32)),
        in_specs=[HBM_SPEC] * nb + [SEM_SPEC, SEM_SPEC, ANY_SPEC],
        out_specs=(*[SEM_SPEC] * (2 * n_next), *[HBM_SPEC] * nb, pl.BlockSpec(memory_space=pltpu.VMEM)),
        input_output_aliases={i: 2 * n_next + i for i in range(nb)},
        compiler_params=pltpu.CompilerParams(has_side_effects=EFFECT),
    )(*bufs, *sems, _Order.last)
    _Order.last = outs[-1]
    return [(outs[2 * p], outs[2 * p + 1]) for p in range(n_next)], list(outs[2 * n_next:2 * n_next + nb])


def _half(ref, c, axis):
    rows = ref.shape[axis] // 2
    return pl.ds(c * rows, rows)


def _plan_weights_ici(n):
    def plan(refs, x, y, c, chips):
        out = []
        for w in range(n):
            region = refs[w].at[2 * x + y, _half(refs[w], c, 1), :]
            out += [(region, region, (*chip, c)) for chip in chips]
        return out
    return plan


def _plan_weights_ring(n):
    def plan(refs, x, y, c, chips):
        out = []
        for w in range(n):
            region = refs[w].at[2 * x + y, _half(refs[w], c, 1), :]
            out += [(region, region, (*chip, c)) for chip in chips[:2]]
        return out
    return plan


def _plan_weights_relay(n):
    def plan(refs, x, y, c, chips):
        out = []
        for w in range(n):
            quarter_rows = refs[w].shape[1] // 4
            upper = refs[w].at[2 * x + (1 - y), pl.ds(2 * c * quarter_rows, quarter_rows), :]
            lower = refs[w].at[2 * (1 - x) + y, pl.ds((2 * c + 1) * quarter_rows, quarter_rows), :]
            out += [(upper, upper, (1 - x, y, c)), (lower, lower, (x, 1 - y, c))]
        return out
    return plan


def _plan_weights_d2d(n, which=slice(0, 3)):
    def plan(refs, x, y, c, chips):
        out = []
        for w in range(n):
            rows = _half(refs[w], c, 1)
            for chip in chips[which]:
                region = refs[w].at[2 * chip[0] + chip[1], rows, :]
                out.append((region, region, (x, y, 1 - c)))
        return out
    return plan


def _plan_gather_out(refs, x, y, c, chips):
    mine = refs[0].at[4 * x + 2 * y + c]
    return [(mine, mine, (x, y, 1 - c))] + [(mine, mine, (*chip, c)) for chip in chips]


def _plan_gather_on(refs, x, y, c, chips):
    out = []
    for chip in chips:
        block = refs[0].at[4 * chip[0] + 2 * chip[1] + c]
        out.append((block, block, (x, y, 1 - c)))
    return out


def _plan_grads_d2d(n):
    def plan(refs, x, y, c, chips):
        return [(refs[w].at[:, _half(refs[w], 1 - c, 1), :], refs[n + w], (x, y, 1 - c)) for w in range(n)]
    return plan


def _plan_grads_ici(n):
    def plan(refs, x, y, c, chips):
        out = []
        for w in range(n):
            out += [(refs[w].at[2 * chip[0] + chip[1]], refs[n + w].at[2 * x + y], (*chip, c)) for chip in chips]
        return out
    return plan


def _plan_final_d2d(n):
    def plan(refs, x, y, c, chips):
        out = []
        for w in range(n):
            region = refs[w].at[_half(refs[w], c, 0), :]
            out.append((region, region, (x, y, 1 - c)))
        return out
    return plan


def _stream_blocks(hr, cols):
    bc = cols if cols <= 4096 else _pick(cols, 4096)
    return _pick_rows(hr, max(16, (768 * 1024) // bc)), bc


def _pre_reduce(g, landed, place, name):
    _, rows, cols = g.shape
    hr = rows // 2
    rb, bc = _stream_blocks(hr, cols)
    nrb = hr // rb

    def body(place_ref, g_ref, l_ref, o_ref):
        o_ref[...] = (g_ref[...].astype(F32) + l_ref[...].astype(F32)).astype(o_ref.dtype)

    return _pcall(
        body, name=name, num_scalar_prefetch=1, grid=(N_CHIPS, nrb, cols // bc),
        in_specs=[pl.BlockSpec((None, rb, bc), lambda j, i, k, p: (j, p[1] * nrb + i, k)),
                  pl.BlockSpec((None, rb, bc), lambda j, i, k, p: (j, i, k))],
        out_specs=pl.BlockSpec((None, rb, bc), lambda j, i, k, p: (j, i, k)),
        out_shape=jax.ShapeDtypeStruct((N_CHIPS, hr, cols), g.dtype),
        compiler_params=_params("parallel", "parallel", "parallel"),
    )(place, g, landed)


def _small_group(arrays):
    return all(a.shape[1] * a.shape[2] * a.dtype.itemsize <= 2 * 1024 * 1024 for a in arrays)


def _pre_reduce_group(gs, landeds, place, name):
    n = len(gs)

    def body(place_ref, *refs):
        for w in range(n):
            refs[2 * n + w][...] = (refs[w][...].astype(F32) + refs[n + w][...].astype(F32)).astype(BF16)

    halves = [(g.shape[1] // 2, g.shape[2]) for g in gs]
    return _pcall(
        body, name=name, num_scalar_prefetch=1, grid=(N_CHIPS,),
        in_specs=[pl.BlockSpec((None, hr, cols), lambda j, p: (j, p[1], 0)) for hr, cols in halves]
        + [pl.BlockSpec((None, hr, cols), lambda j, p: (j, 0, 0)) for hr, cols in halves],
        out_specs=[pl.BlockSpec((None, hr, cols), lambda j, p: (j, 0, 0)) for hr, cols in halves],
        out_shape=[jax.ShapeDtypeStruct((N_CHIPS, hr, cols), BF16) for hr, cols in halves],
        compiler_params=_params("parallel"),
    )(place, *gs, *landeds)


def _sum_slots_group(mines, landeds, place, name):
    n = len(mines)

    def body(place_ref, *refs):
        chip = place_ref[0]
        for w in range(n):
            own, l_ref = refs[w][...].astype(F32), refs[n + w]
            total = jnp.where(chip == 0, own, l_ref[0].astype(F32))
            for j in range(1, N_CHIPS):
                total = total + jnp.where(chip == j, own, l_ref[j].astype(F32))
            refs[2 * n + w][...] = total

    shapes = [(a.shape[1], a.shape[2]) for a in mines]
    return _pcall(
        body, name=name, num_scalar_prefetch=1, grid=(1,),
        in_specs=[pl.BlockSpec((None, hr, cols), lambda i, p: (p[0], 0, 0)) for hr, cols in shapes]
        + [pl.BlockSpec((N_CHIPS, hr, cols), lambda i, p: (0, 0, 0)) for hr, cols in shapes],
        out_specs=[pl.BlockSpec((hr, cols), lambda i, p: (p[1], 0)) for hr, cols in shapes],
        out_shape=[jax.ShapeDtypeStruct((2 * hr, cols), F32) for hr, cols in shapes],
        compiler_params=_params("arbitrary"),
    )(place, *mines, *landeds)


def _sum_slots(mine, landed, place, name):
    _, hr, cols = mine.shape
    rb, bc = _stream_blocks(hr, cols)
    rb = _pick_rows(hr, max(16, rb // 2))
    nrb = hr // rb

    def body(place_ref, m_ref, l_ref, o_ref):
        chip = place_ref[0]
        own = m_ref[...].astype(F32)
        total = jnp.where(chip == 0, own, l_ref[0].astype(F32))
        for j in range(1, N_CHIPS):
            total = total + jnp.where(chip == j, own, l_ref[j].astype(F32))
        o_ref[...] = total

    return _pcall(
        body, name=name, num_scalar_prefetch=1, grid=(nrb, cols // bc),
        in_specs=[pl.BlockSpec((None, rb, bc), lambda i, k, p: (p[0], i, k)),
                  pl.BlockSpec((N_CHIPS, rb, bc), lambda i, k, p: (0, i, k))],
        out_specs=pl.BlockSpec((rb, bc), lambda i, k, p: (p[1] * nrb + i, k)),
        out_shape=jax.ShapeDtypeStruct((2 * hr, cols), F32),
        compiler_params=_params("parallel", "parallel"),
    )(place, mine, landed)


class _WeightGather:
    @staticmethod
    def zones(quarters, chip):
        return [lax.dynamic_update_slice(lax.empty((N_CHIPS,) + q.shape, BF16), q.astype(BF16)[None], (chip, 0, 0))
                for q in quarters]

    def __init__(self, tag, zones, ring=False):
        self.tag, self.n, self.ring = tag, len(zones), ring
        self.plan = _plan_weights_ring(self.n) if ring else _plan_weights_ici(self.n)
        self.sems, self.bufs = _xfer_start("wici_start_" + tag, zones, self.plan, (2 if ring else 3) * self.n,
                                           after_last=True)

    def relay(self):
        plan = _plan_weights_relay(self.n)
        (self.sems,), self.bufs = _xfer_hand_over("wrel_start_" + self.tag, self.sems, self.bufs, self.plan,
                                                   [(plan, 2 * self.n)])
        self.plan = plan

    def pass_on(self):
        plan = _plan_weights_d2d(self.n)
        (self.sems,), self.bufs = _xfer_hand_over("wd2d_start_" + self.tag, self.sems, self.bufs, self.plan,
                                                   [(plan, 3 * self.n)])
        self.plan = plan

    def done(self):
        return _xfer_wait("wd2d_wait_" + self.tag, self.sems, self.bufs, self.plan)

    def relay_and_pass_near(self):
        plan, self.near_plan = _plan_weights_relay(self.n), _plan_weights_d2d(self.n, slice(0, 2))
        (self.sems, self.near_sems), self.bufs = _xfer_hand_over(
            "wrel_start_" + self.tag, self.sems, self.bufs, self.plan, [(plan, 2 * self.n), (self.near_plan, 2 * self.n)])
        self.plan = plan

    def near_done(self):
        self.bufs = _xfer_wait("wnear_wait_" + self.tag, self.near_sems, self.bufs, self.near_plan)
        return self.bufs

    def pass_far(self):
        plan = _plan_weights_d2d(self.n, slice(2, 3))
        (sems,), bufs = _xfer_hand_over("wfar_start_" + self.tag, self.sems, self.bufs, self.plan, [(plan, self.n)])
        return _xfer_wait("wfar_wait_" + self.tag, sems, bufs, plan)


class _GradReduce:
    def __init__(self, tag, grads):
        self.tag, self.n = tag, len(grads)
        zones = [lax.empty((N_CHIPS, g.shape[1] // 2, g.shape[2]), g.dtype) for g in grads]
        self.plan = _plan_grads_d2d(self.n)
        self.sems, self.bufs = _xfer_start("gd2d_start_" + tag, list(grads) + zones, self.plan, self.n)

    def pair(self, place):
        n = self.n
        bufs = _xfer_wait("gd2d_wait_" + self.tag, self.sems, self.bufs, self.plan)
        if n > 1 and _small_group(bufs[n:]):
            self.halves = list(_pre_reduce_group(bufs[:n], bufs[n:], place, "pre_reduce_" + self.tag))
        else:
            self.halves = [_pre_reduce(bufs[w], bufs[n + w], place, f"pre_reduce_{self.tag}{w}") for w in range(n)]

    def cross(self, after_last=False):
        zones = [lax.empty(h.shape, h.dtype) for h in self.halves]
        self.plan = _plan_grads_ici(self.n)
        self.sems, self.bufs = _xfer_start("gici_start_" + self.tag, self.halves + zones, self.plan, 3 * self.n,
                                           after_last=after_last)

    def step(self, place):
        self.pair(place)
        self.cross()

    def join(self, place):
        n = self.n
        bufs = _xfer_wait("gici_wait_" + self.tag, self.sems, self.bufs, self.plan)
        if n > 1 and _small_group(bufs[:n]):
            sums = list(_sum_slots_group(bufs[:n], bufs[n:], place, "sum_slots_" + self.tag))
        else:
            sums = [_sum_slots(bufs[w], bufs[n + w], place, f"sum_slots_{self.tag}{w}") for w in range(n)]
        self.plan = _plan_final_d2d(n)
        self.sems, self.bufs = _xfer_start("gfin_start_" + self.tag, sums, self.plan, n)

    def done(self):
        return _xfer_wait("gfin_wait_" + self.tag, self.sems, self.bufs, self.plan)


def _ada_fwd(c_all, w_q, b_q):
    d, n = w_q.shape
    bn = _pick(n, 512)

    def body(c_ref, w_ref, b_ref, o_ref):
        cv = c_ref[...]
        act = cv * _sigmoid(cv)
        o_ref[...] = _dot(act, w_ref[...], NN, lax.Precision.HIGHEST) + b_ref[...]

    return _pcall(
        body, name="ada_fwd", grid=(n // bn,),
        in_specs=[pl.BlockSpec((N_DEV, d), lambda j: (0, 0)), pl.BlockSpec((d, bn), lambda j: (0, j)),
                  pl.BlockSpec((1, bn), lambda j: (0, j))],
        out_specs=pl.BlockSpec((N_DEV, bn), lambda j: (0, j)),
        out_shape=jax.ShapeDtypeStruct((N_DEV, n), F32), compiler_params=_params("parallel"),
    )(c_all, w_q, b_q)


def _row_spec(rb, width, col=0):
    return pl.BlockSpec((rb, width), lambda i, col=col: (i, col))


def _vec_spec(width, col=0):
    return pl.BlockSpec((1, width), lambda i, col=col: (0, col))


def _norm_mod_fwd(x, g, sc, sh, name, res=None, gt=None):
    s, d = x.shape
    rb = _pick(s, 256)
    has_res = res is not None

    def body(*refs):
        if has_res:
            x_ref, res_ref, gt_ref, g_ref, sc_ref, sh_ref, x1_ref, h_ref = refs
            xv = x_ref[...] + gt_ref[...] * res_ref[...]
            x1_ref[...] = xv
        else:
            x_ref, g_ref, sc_ref, sh_ref, h_ref = refs
            xv = x_ref[...]
        xh, _ = _rms(xv)
        h_ref[...] = (xh * g_ref[...] * (1.0 + sc_ref[...]) + sh_ref[...]).astype(BF16)

    row, vec = _row_spec(rb, d), _vec_spec(d)
    if has_res:
        ins, in_specs = (x, res, gt, g, sc, sh), [row, row, vec, vec, vec, vec]
        out_shape = [jax.ShapeDtypeStruct((s, d), F32), jax.ShapeDtypeStruct((s, d), BF16)]
        out_specs = [row, row]
    else:
        ins, in_specs = (x, g, sc, sh), [row, vec, vec, vec]
        out_shape, out_specs = jax.ShapeDtypeStruct((s, d), BF16), row
    return _pcall(body, name=name, grid=(s // rb,), in_specs=in_specs, out_specs=out_specs,
                          out_shape=out_shape, compiler_params=_params("parallel"))(*ins)


def _final_loss(x1, f, gt2, g_final, target):
    s, d = x1.shape
    rb = _pick(s, 256)

    def body(x1_ref, f_ref, gt_ref, g_ref, t_ref, dx_ref, df_ref, loss_ref, dg_ref, dgt_ref):
        first = pl.program_id(0) == 0
        fv, gt, gv = f_ref[...], gt_ref[...], g_ref[...]
        x2 = x1_ref[...] + gt * fv
        xh, r = _rms(x2)
        err = xh * gv - t_ref[...]
        blk = 0.5 * jnp.sum(jnp.sum(err * err, axis=1, keepdims=True), axis=0, keepdims=True) / d
        dy = err / d
        dxh = dy * gv
        dx = r * (dxh - xh * jnp.mean(dxh * xh, axis=-1, keepdims=True))
        dx_ref[...] = dx
        df_ref[...] = (dx * gt).astype(BF16)
        _accumulate(first, loss_ref, jnp.broadcast_to(blk, (1, LANES)))
        _accumulate(first, dg_ref, _colsum(dy * xh))
        _accumulate(first, dgt_ref, _colsum(dx * fv))

    row, vec = _row_spec(rb, d), _vec_spec(d)
    return _pcall(
        body, name="final_loss", grid=(s // rb,), in_specs=[row, row, vec, vec, row],
        out_specs=[row, row, _vec_spec(LANES), vec, vec],
        out_shape=[jax.ShapeDtypeStruct((s, d), F32), jax.ShapeDtypeStruct((s, d), BF16),
                   jax.ShapeDtypeStruct((1, LANES), F32), jax.ShapeDtypeStruct((1, d), F32),
                   jax.ShapeDtypeStruct((1, d), F32)],
        compiler_params=_params("arbitrary"),
    )(x1, f, gt2, g_final, target)


def _norm_mod_bwd(dh, xin, dres, g, sc, name, branch=None, gt=None):
    s, d = xin.shape
    rb = _pick(s, 256)
    has_branch = branch is not None

    def body(*refs):
        if has_branch:
            dh_ref, x_ref, dres_ref, g_ref, sc_ref, br_ref, gt_ref, dx_ref, dbr_ref, dsh_ref, dsc_ref, dg_ref, dgt_ref = refs
        else:
            dh_ref, x_ref, dres_ref, g_ref, sc_ref, dx_ref, dsh_ref, dsc_ref, dg_ref = refs
        first = pl.program_id(0) == 0
        gv = g_ref[...]
        xh, r = _rms(x_ref[...])
        dhv = dh_ref[...]
        dn = dhv * (1.0 + sc_ref[...])
        dxh = dn * gv
        dx = dres_ref[...] + r * (dxh - xh * jnp.mean(dxh * xh, axis=-1, keepdims=True))
        dx_ref[...] = dx
        _accumulate(first, dsh_ref, _colsum(dhv))
        _accumulate(first, dsc_ref, _colsum(dhv * xh * gv))
        _accumulate(first, dg_ref, _colsum(dn * xh))
        if has_branch:
            dbr_ref[...] = (dx * gt_ref[...]).astype(BF16)
            _accumulate(first, dgt_ref, _colsum(dx * br_ref[...]))

    row, vec = _row_spec(rb, d), _vec_spec(d)
    vec_shape = jax.ShapeDtypeStruct((1, d), F32)
    if has_branch:
        ins, in_specs = (dh, xin, dres, g, sc, branch, gt), [row, row, row, vec, vec, row, vec]
        out_specs = [row, row, vec, vec, vec, vec]
        out_shape = [jax.ShapeDtypeStruct((s, d), F32), jax.ShapeDtypeStruct((s, d), BF16)] + [vec_shape] * 4
    else:
        ins, in_specs = (dh, xin, dres, g, sc), [row, row, row, vec, vec]
        out_specs = [row, vec, vec, vec]
        out_shape = [jax.ShapeDtypeStruct((s, d), F32)] + [vec_shape] * 3
    return _pcall(body, name=name, grid=(s // rb,), in_specs=in_specs, out_specs=out_specs,
                          out_shape=out_shape, compiler_params=_params("arbitrary"))(*ins)


def _branch_merge(ya, yb, w_bg, w_bh, z, b_gate, gate_col, bm=512, bn=1024):
    s, gw = ya.shape
    d = w_bg.shape[1]
    bm, bn = _pick(s, bm), _pick(d, bn)
    nj = d // bn

    def body(ya_ref, yb_ref, wa_ref, wb_ref, za_ref, zb_ref, ba_ref, bb_ref, pa_ref, pb_ref, y_ref):
        pa = _dot(ya_ref[...], wa_ref[...])
        pb = _dot(yb_ref[...], wb_ref[...])
        pa_ref[...] = pa.astype(BF16)
        pb_ref[...] = pb.astype(BF16)
        ga = _sigmoid(za_ref[...] + ba_ref[...])
        gb = _sigmoid(zb_ref[...] + bb_ref[...])
        y_ref[...] = (ga * pa + gb * pb).astype(BF16)

    act = pl.BlockSpec((bm, gw), lambda j, i: (i, 0))
    wgt = pl.BlockSpec((gw, bn), lambda j, i: (0, j))
    tile = pl.BlockSpec((bm, bn), lambda j, i: (i, j))
    shape = jax.ShapeDtypeStruct((s, d), BF16)
    return _pcall(
        body, name="branch_merge", grid=(nj, s // bm),
        in_specs=[act, act, wgt, wgt,
                  pl.BlockSpec((bm, bn), lambda j, i: (i, gate_col * nj + j)),
                  pl.BlockSpec((bm, bn), lambda j, i: (i, (gate_col + 1) * nj + j)),
                  pl.BlockSpec((1, bn), lambda j, i: (0, j)), pl.BlockSpec((1, bn), lambda j, i: (0, nj + j))],
        out_specs=[tile, tile, tile], out_shape=[shape, shape, shape], compiler_params=_params("parallel", "parallel"),
    )(ya, yb, w_bg, w_bh, z, z, b_gate, b_gate)


def _branch_wgrad(ya, yb, dpa, dpb):
    s, gw = ya.shape
    per = dpa.shape[1] // N_CHIPS

    def body(ya_ref, yb_ref, da_ref, db_ref, ga_ref, gb_ref):
        ga_ref[...] = _dot(ya_ref[...], da_ref[...], TN).astype(BF16)
        gb_ref[...] = _dot(yb_ref[...], db_ref[...], TN).astype(BF16)

    act = pl.BlockSpec((s, gw), lambda j: (0, 0))
    grd = pl.BlockSpec((s, per), lambda j: (0, j))
    out = pl.BlockSpec((None, gw, per), lambda j: (j, 0, 0))
    shape = jax.ShapeDtypeStruct((N_CHIPS, gw, per), BF16)
    return _pcall(body, name="branch_wgrad", grid=(N_CHIPS,), in_specs=[act, act, grd, grd], out_specs=[out, out],
                  out_shape=[shape, shape], compiler_params=_params("parallel"))(ya, yb, dpa, dpb)


def _branch_bwd(dpa, dpb, w_bg, w_bh, bm=512):
    s, d = dpa.shape
    gw = w_bg.shape[0]
    bm = _pick(s, bm)

    def body(da_ref, db_ref, wa_ref, wb_ref, oa_ref, ob_ref):
        oa_ref[...] = _dot(da_ref[...], wa_ref[...], NT)
        ob_ref[...] = _dot(db_ref[...], wb_ref[...], NT)

    act = pl.BlockSpec((bm, d), lambda i: (i, 0))
    wgt = pl.BlockSpec((gw, d), lambda i: (0, 0))
    out = pl.BlockSpec((bm, gw), lambda i: (i, 0))
    shape = jax.ShapeDtypeStruct((s, gw), F32)
    return _pcall(body, name="branch_bwd", grid=(s // bm,), in_specs=[act, act, wgt, wgt], out_specs=[out, out],
                  out_shape=[shape, shape], compiler_params=_params("parallel"))(dpa, dpb, w_bg, w_bh)


def _out_bwd(dyo, w_out, pa, pb, z, b_gate, gate_col, bm=512, bn=1024):
    s, d = dyo.shape
    bm, bn = _pick(s, bm), _pick(d, bn)
    nj = d // bn

    def body(dyo_ref, w_ref, pa_ref, pb_ref, za_ref, zb_ref, ba_ref, bb_ref,
             dpa_ref, dpb_ref, dza_ref, dzb_ref, dba_ref, dbb_ref):
        first = pl.program_id(1) == 0
        dyv = _dot(dyo_ref[...], w_ref[...], NT)
        ga = _sigmoid(za_ref[...] + ba_ref[...])
        gb = _sigmoid(zb_ref[...] + bb_ref[...])
        dpa_ref[...] = (dyv * ga).astype(BF16)
        dpb_ref[...] = (dyv * gb).astype(BF16)
        dga = dyv * pa_ref[...].astype(F32) * ga * (1.0 - ga)
        dgb = dyv * pb_ref[...].astype(F32) * gb * (1.0 - gb)
        dza_ref[...] = dga.astype(BF16)
        dzb_ref[...] = dgb.astype(BF16)
        _accumulate(first, dba_ref, _colsum(dga))
        _accumulate(first, dbb_ref, _colsum(dgb))

    tile = pl.BlockSpec((bm, bn), lambda j, i: (i, j))
    vec = pl.BlockSpec((1, bn), lambda j, i: (0, j))
    act, vec_shape = jax.ShapeDtypeStruct((s, d), BF16), jax.ShapeDtypeStruct((1, d), F32)
    return _pcall(
        body, name="out_bwd", grid=(nj, s // bm),
        in_specs=[pl.BlockSpec((bm, d), lambda j, i: (i, 0)), pl.BlockSpec((bn, d), lambda j, i: (j, 0)), tile, tile,
                  pl.BlockSpec((bm, bn), lambda j, i: (i, gate_col * nj + j)),
                  pl.BlockSpec((bm, bn), lambda j, i: (i, (gate_col + 1) * nj + j)),
                  vec, pl.BlockSpec((1, bn), lambda j, i: (0, nj + j))],
        out_specs=[tile, tile, tile, tile, vec, vec], out_shape=[act, act, act, act, vec_shape, vec_shape],
        compiler_params=_params("parallel", "arbitrary"),
    )(dyo, w_out, pa, pb, z, z, b_gate, b_gate)


def _ffn_in(h, w_fi, bm=512):
    s, d = h.shape
    per = w_fi.shape[2]
    ff = 2 * per
    bm, bn = _pick(s, bm), _half_width(per)
    per_blocks = per // bn

    def body(h_ref, wa_ref, wu_ref, a_ref, u_ref, hf_ref):
        hv = h_ref[...]
        a = _dot(hv, wa_ref[...])
        up = _dot(hv, wu_ref[...])
        a_ref[...] = a.astype(BF16)
        u_ref[...] = up.astype(BF16)
        hf_ref[...] = (a * _sigmoid(a) * up).astype(BF16)

    out = pl.BlockSpec((bm, bn), lambda j, i: (i, j))
    shape = jax.ShapeDtypeStruct((s, ff), BF16)
    return _pcall(
        body, name="ffn_in", grid=(ff // bn, s // bm),
        in_specs=[pl.BlockSpec((bm, d), lambda j, i: (i, 0)),
                  pl.BlockSpec((None, d, bn), lambda j, i: (j // per_blocks, 0, j % per_blocks)),
                  pl.BlockSpec((None, d, bn), lambda j, i: (2 + j // per_blocks, 0, j % per_blocks))],
        out_specs=[out, out, out], out_shape=[shape, shape, shape],
        compiler_params=_params("parallel", "parallel"),
    )(h, w_fi, w_fi)


def _ffn_out_bwd(dffn, w_fo, a_act, up_act, bm=512):
    s, d = dffn.shape
    ff = w_fo.shape[0]
    bm, bn = _pick(s, bm), _half_width(ff // 2)

    def body(d_ref, w_ref, a_ref, u_ref, da_ref, du_ref):
        dhf = _dot(d_ref[...], w_ref[...], NT)
        a = a_ref[...].astype(F32)
        sa = _sigmoid(a)
        da_ref[...] = (dhf * u_ref[...].astype(F32) * sa * (1.0 + a * (1.0 - sa))).astype(BF16)
        du_ref[...] = (dhf * a * sa).astype(BF16)

    tile = pl.BlockSpec((bm, bn), lambda j, i: (i, j))
    shape = jax.ShapeDtypeStruct((s, ff), BF16)
    return _pcall(
        body, name="ffn_out_bwd", grid=(ff // bn, s // bm),
        in_specs=[pl.BlockSpec((bm, d), lambda j, i: (i, 0)), pl.BlockSpec((bn, d), lambda j, i: (j, 0)), tile, tile],
        out_specs=[tile, tile], out_shape=[shape, shape], compiler_params=_params("parallel", "parallel"),
    )(dffn, w_fo, a_act, up_act)


def _tril(n):
    return lax.broadcasted_iota(jnp.int32, (n, n), 0) >= lax.broadcasted_iota(jnp.int32, (n, n), 1)


def _gmlp_norm(v, ln_g, ln_b):
    gv = _gelu(v)
    mu = jnp.mean(gv, axis=-1, keepdims=True)
    cen = gv - mu
    rs = lax.rsqrt(jnp.mean(cen * cen, axis=-1, keepdims=True) + EPS)
    xh = cen * rs
    return xh, rs, xh * ln_g + ln_b


def _gmlp_fwd(z, ln_g, ln_b, ws, bs_t):
    s = z.shape[0]
    gw = ln_g.shape[1]
    groups, chunk, _ = ws.shape
    per = GMLP_CHUNKS_PER_STEP if s % (GMLP_CHUNKS_PER_STEP * chunk) == 0 else 1
    rows = per * chunk

    def body(u_ref, v_ref, lg_ref, lb_ref, ws_ref, bs_ref, ya_ref):
        mask = _tril(chunk)
        for cc in range(per):
            rs_ = slice(cc * chunk, (cc + 1) * chunk)
            gu = _gelu(u_ref[rs_, :])
            _, _, vn = _gmlp_norm(v_ref[rs_, :], lg_ref[...], lb_ref[...])
            for g in range(groups):
                cols = slice(g * LANES, (g + 1) * LANES)
                wm = jnp.where(mask, ws_ref[g], 0.0).astype(BF16)
                sg = _dot(wm, vn[:, cols].astype(BF16)) + bs_ref[:, g:g + 1]
                ya_ref[rs_, cols] = (gu[:, cols] * sg).astype(BF16)

    return _pcall(
        body, name="gmlp_fwd", grid=(s // rows,),
        in_specs=[_row_spec(rows, gw, 0), _row_spec(rows, gw, 1), _vec_spec(gw), _vec_spec(gw),
                  pl.BlockSpec((groups, chunk, chunk), lambda i: (0, 0, 0)), pl.BlockSpec((chunk, LANES), lambda i: (0, 0))],
        out_specs=_row_spec(rows, gw), out_shape=jax.ShapeDtypeStruct((s, gw), BF16),
        compiler_params=_params("parallel"),
    )(z, z, ln_g, ln_b, ws, bs_t)


def _gmlp_bwd(dya, z, ln_g, ln_b, ws, bs_t):
    s = z.shape[0]
    gw = ln_g.shape[1]
    groups, chunk, _ = ws.shape
    per = GMLP_CHUNKS_PER_STEP if s % (GMLP_CHUNKS_PER_STEP * chunk) == 0 else 1
    rows = per * chunk

    def body(dya_ref, u_ref, v_ref, lg_ref, lb_ref, ws_ref, bs_ref, duv_ref, dws_ref, dbs_ref, dlg_ref, dlb_ref, dvn_ref):
        first = pl.program_id(0) == 0
        lg = lg_ref[...]
        mask = _tril(chunk)
        lane = lax.broadcasted_iota(jnp.int32, (chunk, LANES), 1)
        dbs = jnp.zeros((chunk, LANES), F32)
        dws = [jnp.zeros((chunk, chunk), F32) for _ in range(groups)]
        dlb = jnp.zeros((1, gw), F32)
        dlg = jnp.zeros((1, gw), F32)
        for cc in range(per):
            rs_ = slice(cc * chunk, (cc + 1) * chunk)
            u, v = u_ref[rs_, :], v_ref[rs_, :]
            gu = _gelu(u)
            xh, rs, vn = _gmlp_norm(v, lg, lb_ref[...])
            dyav = dya_ref[rs_, :]
            for g in range(groups):
                cols = slice(g * LANES, (g + 1) * LANES)
                wm = jnp.where(mask, ws_ref[g], 0.0).astype(BF16)
                vg = vn[:, cols].astype(BF16)
                sg = _dot(wm, vg) + bs_ref[:, g:g + 1]
                ds = dyav[:, cols] * gu[:, cols]
                duv_ref[rs_, cols] = (dyav[:, cols] * sg * _gelu_grad(u[:, cols])).astype(BF16)
                dsb = ds.astype(BF16)
                dws[g] = dws[g] + jnp.where(mask, _dot(dsb, vg, NT), 0.0)
                dbs = dbs + jnp.where(lane == g, jnp.sum(ds, axis=-1, keepdims=True), 0.0)
                dvn_ref[rs_, cols] = _dot(wm, dsb, TN)
            dvn = dvn_ref[rs_, :]
            dlb = dlb + _colsum(dvn)
            dlg = dlg + _colsum(dvn * xh)
            dxh = dvn * lg
            dgv = rs * (dxh - jnp.mean(dxh, axis=-1, keepdims=True) - xh * jnp.mean(dxh * xh, axis=-1, keepdims=True))
            duv_ref[rs_, gw:] = (dgv * _gelu_grad(v)).astype(BF16)
        for g in range(groups):
            _accumulate(first, dws_ref.at[g], dws[g])
        _accumulate(first, dbs_ref, dbs)
        _accumulate(first, dlb_ref, dlb)
        _accumulate(first, dlg_ref, dlg)

    return _pcall(
        body, name="gmlp_bwd", grid=(s // rows,),
        in_specs=[_row_spec(rows, gw), _row_spec(rows, gw, 0), _row_spec(rows, gw, 1), _vec_spec(gw), _vec_spec(gw),
                  pl.BlockSpec((groups, chunk, chunk), lambda i: (0, 0, 0)), pl.BlockSpec((chunk, LANES), lambda i: (0, 0))],
        out_specs=[_row_spec(rows, 2 * gw), pl.BlockSpec((groups, chunk, chunk), lambda i: (0, 0, 0)),
                   pl.BlockSpec((chunk, LANES), lambda i: (0, 0)), _vec_spec(gw), _vec_spec(gw)],
        out_shape=[jax.ShapeDtypeStruct((s, 2 * gw), BF16), jax.ShapeDtypeStruct((groups, chunk, chunk), F32),
                   jax.ShapeDtypeStruct((chunk, LANES), F32), jax.ShapeDtypeStruct((1, gw), F32),
                   jax.ShapeDtypeStruct((1, gw), F32)],
        scratch_shapes=[pltpu.VMEM((rows, gw), F32)],
        compiler_params=_params("arbitrary"),
    )(dya, z, z, ln_g, ln_b, ws, bs_t)


def _lower_bound(lb_ref):
    a0, a1 = lb_ref[0:1, :], lb_ref[1:2, :]
    mx = jnp.maximum(a0, a1)
    e0, e1 = jnp.exp(a0 - mx), jnp.exp(a1 - mx)
    return e0 / (e0 + e1)


def _sum_dot(mask, x):
    hi = x.astype(BF16)
    rest = x - hi.astype(F32)
    mid = rest.astype(BF16)
    low = (rest - mid.astype(F32)).astype(BF16)
    return _dot(mask, hi) + _dot(mask, mid) + _dot(mask, low)


def _ones_where(mask):
    return jnp.where(mask, 1.0, 0.0).astype(BF16)


def _hg_masks(rows, t):
    r = lax.broadcasted_iota(jnp.int32, (rows, rows), 0)
    c = lax.broadcasted_iota(jnp.int32, (rows, rows), 1)
    same = (r // t) == (c // t)
    incl = jnp.logical_and(same, c <= r)
    upto_mid = jnp.logical_and(same, (c % t) <= t // 2)
    rev = jnp.logical_and(same, c >= r)
    return same, incl, upto_mid, rev


def _hg_block(q, fp, lb, masks):
    rows = q.shape[0]
    same, incl, upto_mid, _ = masks
    sig = _sigmoid(fp)
    f = lb + (1.0 - lb) * sig
    k = 1.0 - f
    sq = _sigmoid(q)
    qa = q * sq
    stacked = jnp.concatenate([_ones_where(m) for m in (incl, same, upto_mid)], axis=0)
    sums = _sum_dot(stacked, jnp.log(f))
    b, b_last, b_mid = sums[:rows], sums[rows:2 * rows], sums[2 * rows:]
    e_q = jnp.exp(jnp.minimum(b - b_mid, EXP_CLAMP))
    e_k = jnp.exp(jnp.minimum(b_mid - b, EXP_CLAMP))
    e_in = jnp.exp(b)
    e_out = jnp.exp(b_last - b)
    return dict(sig=sig, f=f, k=k, sq=sq, qa=qa, e_last=jnp.exp(b_last), e_q=e_q, e_k=e_k, e_in=e_in, e_out=e_out,
                q_hat=(qa * e_q).astype(BF16), k_hat=(k * e_k).astype(BF16),
                q_in=(qa * e_in).astype(BF16), k_out=k * e_out)


def _hgrn_fwd(z, hg_lb, norm_g, q_col):
    s = z.shape[0]
    hw = norm_g.shape[1]
    heads = hw // LANES
    t = HG_CHUNK
    rows = min(HG_ROWS, s)
    per_step = rows // t
    hp = min(HG_HEADS_PER_STEP, heads)
    assert heads % hp == 0 and q_col % hp == 0, (heads, q_col)
    wide = hp * LANES

    def zspec(which):
        return pl.BlockSpec((rows, wide), lambda h, r, which=which: (r, (q_col + which * heads) // hp + h))

    def body(q_ref, f_ref, i_ref, g_ref, lb_ref, ng_ref, yb_ref, o_ref, st_out_ref, st_ref, e_last_ref, inter_ref):
        @pl.when(pl.program_id(1) == 0)
        def _():
            st_ref[...] = jnp.zeros_like(st_ref)

        masks = _hg_masks(rows, t)
        for hh in range(hp):
            cols = slice(hh * LANES, (hh + 1) * LANES)
            blk = _hg_block(q_ref[:, cols], f_ref[:, cols], _lower_bound(lb_ref.at[:, cols]), masks)
            iv = i_ref[:, cols].astype(BF16)
            q_in, k_out = blk["q_in"], blk["k_out"].astype(BF16)
            e_last_ref[hh] = blk["e_last"]
            attn = jnp.where(masks[1], _dot(blk["q_hat"], blk["k_hat"], NT), 0.0).astype(BF16)
            o = _dot(attn, iv)
            grown = [_dot(iv[j * t:(j + 1) * t], k_out[j * t:(j + 1) * t], TN) for j in range(per_step)]
            st = st_ref[hh]
            for j in range(per_step):
                st_out_ref[hh, j] = st
                inter_ref[hh, j * t:(j + 1) * t, :] = _dot(q_in[j * t:(j + 1) * t], st.astype(BF16), NT)
                st = st * e_last_ref[hh, j * t:j * t + 1, :] + grown[j]
            st_ref[hh] = st
            o = o + inter_ref[hh]
            o_ref[:, cols] = o
            og = g_ref[:, cols]
            on, _ = _rms(o)
            yb_ref[:, cols] = (on * ng_ref[:, cols] * (og * _sigmoid(og))).astype(BF16)

    out_row = pl.BlockSpec((rows, wide), lambda h, r: (r, h))
    return _pcall(
        body, name="hgrn_fwd", grid=(heads // hp, s // rows),
        in_specs=[zspec(0), zspec(1), zspec(2), zspec(3),
                  pl.BlockSpec((2, wide), lambda h, r: (0, h)), pl.BlockSpec((1, wide), lambda h, r: (0, h))],
        out_specs=[out_row, out_row, pl.BlockSpec((hp, per_step, LANES, LANES), lambda h, r: (h, r, 0, 0))],
        out_shape=[jax.ShapeDtypeStruct((s, hw), BF16), jax.ShapeDtypeStruct((s, hw), F32),
                   jax.ShapeDtypeStruct((heads, s // t, LANES, LANES), F32)],
        scratch_shapes=[pltpu.VMEM((hp, LANES, LANES), F32), pltpu.VMEM((hp, rows, LANES), F32),
                        pltpu.VMEM((hp, rows, LANES), F32)],
        compiler_params=_params("parallel", "arbitrary"),
    )(z, z, z, z, hg_lb, norm_g)


def _hgrn_bwd(dyb, z, o_raw, states, hg_lb, norm_g, q_col):
    s = z.shape[0]
    hw = norm_g.shape[1]
    heads = hw // LANES
    t = HG_CHUNK
    rows = min(HG_ROWS, s)
    per_step = rows // t
    n_steps = s // rows
    hp = min(HG_HEADS_PER_STEP, heads)
    assert heads % hp == 0 and q_col % hp == 0, (heads, q_col)
    wide = hp * LANES

    def zspec(which):
        return pl.BlockSpec((rows, wide), lambda h, r, which=which: (n_steps - 1 - r, (q_col + which * heads) // hp + h))

    def body(dyb_ref, q_ref, f_ref, i_ref, g_ref, o_ref, st_in_ref, lb_ref, ng_ref,
             dq_ref, df_ref, di_ref, dg_ref, dlb_ref, dng_ref, dst_ref, acc_lb_ref, acc_ng_ref,
             e_last_ref, dq_in_ref, dk_out_ref, di_inter_ref, carry_ref):
        step = pl.program_id(1)

        @pl.when(step == 0)
        def _():
            dst_ref[...] = jnp.zeros_like(dst_ref)
            acc_lb_ref[...] = jnp.zeros_like(acc_lb_ref)
            acc_ng_ref[...] = jnp.zeros_like(acc_ng_ref)

        masks = _hg_masks(rows, t)
        same, incl, _, rev = masks
        sum_mask = jnp.concatenate([_ones_where(rev), _ones_where(same)], axis=1)
        for hh in range(hp):
            cols = slice(hh * LANES, (hh + 1) * LANES)
            lb = _lower_bound(lb_ref.at[:, cols])
            ng = ng_ref[:, cols]
            q = q_ref[:, cols]
            blk = _hg_block(q, f_ref[:, cols], lb, masks)
            iv = i_ref[:, cols].astype(BF16)
            o, og, dy = o_ref[:, cols], g_ref[:, cols], dyb_ref[:, cols]
            so = _sigmoid(og)
            on, r = _rms(o)
            acc_ng_ref[:, cols] += _colsum(dy * on * (og * so))
            dg_ref[:, cols] = (dy * on * ng * so * (1.0 + og * (1.0 - so))).astype(BF16)
            don = dy * ng * (og * so)
            do = (r * (don - on * jnp.mean(don * on, axis=-1, keepdims=True))).astype(BF16)
            q_hat, k_hat, q_in, k_out = blk["q_hat"], blk["k_hat"], blk["q_in"], blk["k_out"]
            k_out_b = k_out.astype(BF16)
            attn = jnp.where(incl, _dot(q_hat, k_hat, NT), 0.0).astype(BF16)
            d_attn = jnp.where(incl, _dot(do, iv, NT), 0.0).astype(BF16)
            di_intra = _dot(attn, do, TN)
            dq_hat = _dot(d_attn, k_hat)
            dk_hat = _dot(d_attn, q_hat, TN)
            e_last_ref[hh] = blk["e_last"]
            grown = [_dot(do[j * t:(j + 1) * t], q_in[j * t:(j + 1) * t], TN) for j in range(per_step)]
            dst = dst_ref[hh]
            for j in reversed(range(per_step)):
                rs_ = slice(j * t, (j + 1) * t)
                e_last = e_last_ref[hh, j * t:j * t + 1, :]
                st_prev, dst_b = st_in_ref[hh, j], dst.astype(BF16)
                dq_in_ref[hh, rs_, :] = _dot(do[rs_], st_prev.astype(BF16))
                dk_out_ref[hh, rs_, :] = _dot(iv[rs_], dst_b)
                di_inter_ref[hh, rs_, :] = _dot(k_out_b[rs_], dst_b, NT)
                carry_ref[hh, rs_, :] = jnp.broadcast_to(e_last * _colsum(st_prev * dst), (t, LANES))
                dst = dst * e_last + grown[j]
            dst_ref[hh] = dst
            di_ref[:, cols] = (di_intra + di_inter_ref[hh]).astype(BF16)
            dk_out = dk_out_ref[hh]
            dqa = dq_in_ref[hh] * blk["e_in"] + dq_hat * blk["e_q"]
            dk = dk_out * blk["e_out"] + dk_hat * blk["e_k"]
            db = blk["qa"] * dqa - blk["k"] * dk
            dlf = _sum_dot(sum_mask, jnp.concatenate([db, dk_out * k_out], axis=0)) + carry_ref[hh]
            dfv = dlf / blk["f"] - dk
            sig, sq = blk["sig"], blk["sq"]
            df_ref[:, cols] = (dfv * (1.0 - lb) * sig * (1.0 - sig)).astype(BF16)
            acc_lb_ref[:, cols] += _colsum(dfv * (1.0 - sig))
            dq_ref[:, cols] = (dqa * sq * (1.0 + q * (1.0 - sq))).astype(BF16)

        @pl.when(step == n_steps - 1)
        def _():
            lb = _lower_bound(lb_ref)
            d0 = acc_lb_ref[...] * lb * (1.0 - lb)
            dlb_ref[0:1, :] = d0
            dlb_ref[1:2, :] = -d0
            dng_ref[...] = acc_ng_ref[...]

    rev_row = pl.BlockSpec((rows, wide), lambda h, r: (n_steps - 1 - r, h))
    piece = jax.ShapeDtypeStruct((s, hw), BF16)
    return _pcall(
        body, name="hgrn_bwd", grid=(heads // hp, n_steps),
        in_specs=[rev_row, zspec(0), zspec(1), zspec(2), zspec(3), rev_row,
                  pl.BlockSpec((hp, per_step, LANES, LANES), lambda h, r: (h, n_steps - 1 - r, 0, 0)),
                  pl.BlockSpec((2, wide), lambda h, r: (0, h)), pl.BlockSpec((1, wide), lambda h, r: (0, h))],
        out_specs=[rev_row, rev_row, rev_row, rev_row,
                   pl.BlockSpec((2, wide), lambda h, r: (0, h)), pl.BlockSpec((1, wide), lambda h, r: (0, h))],
        out_shape=[piece, piece, piece, piece, jax.ShapeDtypeStruct((2, hw), F32), jax.ShapeDtypeStruct((1, hw), F32)],
        scratch_shapes=[pltpu.VMEM((hp, LANES, LANES), F32), pltpu.VMEM((1, wide), F32), pltpu.VMEM((1, wide), F32)]
        + [pltpu.VMEM((hp, rows, LANES), F32)] * 5,
        compiler_params=_params("parallel", "arbitrary"),
    )(dyb, z, z, z, z, o_raw, states, hg_lb, norm_g)


def _adam_update(w, m, v, g):
    m2 = ADAM_B1 * m + (1.0 - ADAM_B1) * g
    v2 = ADAM_B2 * v + (1.0 - ADAM_B2) * (g * g)
    m_hat = m2 * (1.0 / (1.0 - ADAM_B1 ** ADAM_STEP))
    v_hat = v2 * (1.0 / (1.0 - ADAM_B2 ** ADAM_STEP))
    return -ADAM_LR * (m_hat / (jnp.sqrt(v_hat) + ADAM_EPS) + ADAM_WD * w), m2, v2


def _adamw(w, m, v, parts, name, outer=False):
    rows, cols = w.shape
    bc = cols if cols <= 4096 else _pick(cols, 4096)
    rb = _pick_rows(rows, max(8, (384 * 1024) // bc), mult=8)
    if outer and rb % LANES:
        rb = rows

    def body(w_ref, m_ref, v_ref, *refs):
        g_ref, d_ref, mo_ref, vo_ref = refs[-4:]
        if outer:
            cv = refs[0][...]
            g = _dot(cv * _sigmoid(cv), refs[1][...], TN, lax.Precision.HIGHEST)
        else:
            p_ref = refs[0]
            g = p_ref[0].astype(F32)
            for p in range(1, p_ref.shape[0]):
                g = g + p_ref[p].astype(F32)
        g_ref[...] = g
        d_ref[...], mo_ref[...], vo_ref[...] = _adam_update(w_ref[...], m_ref[...], v_ref[...], g)

    blk = pl.BlockSpec((rb, bc), lambda i, j: (i, j))
    out = jax.ShapeDtypeStruct((rows, cols), F32)
    if outer:
        grad_specs = [pl.BlockSpec((N_DEV, rb), lambda i, j: (0, i)), pl.BlockSpec((N_DEV, bc), lambda i, j: (0, j))]
        grad_ins = tuple(parts)
    else:
        grad_specs = [pl.BlockSpec((parts.shape[0], rb, bc), lambda i, j: (0, i, j))]
        grad_ins = (parts,)
    return _pcall(
        body, name=name, grid=(rows // rb, cols // bc), in_specs=[blk, blk, blk] + grad_specs,
        out_specs=[blk] * 4, out_shape=[out] * 4, compiler_params=_params("parallel", "parallel"),
    )(w, m, v, *grad_ins)


SMALL = ("b_ada", "norm1_g", "b_gate", "gmlp_ln_g", "gmlp_ln_b", "gmlp_ws", "gmlp_bs", "hg_lb", "hg_norm_g",
         "norm2_g", "final_norm_g")
BIG = ("w_in", "w_branch_gmlp", "w_branch_hg", "w_out", "w_ffn_in", "w_ffn_out")
WEIGHTS = ("w_ada", "b_ada", "norm1_g", "w_in", "b_gate", "gmlp_ln_g", "gmlp_ln_b", "gmlp_ws", "gmlp_bs", "hg_lb",
           "hg_norm_g", "w_branch_gmlp", "w_branch_hg", "w_out", "norm2_g", "w_ffn_in", "w_ffn_out", "final_norm_g")


def _pack(parts):
    return jnp.concatenate([p.reshape(-1, LANES) for p in parts], axis=0)


def _step(x, c, loss_target, w, m, v):
    s, d = x.shape[1], x.shape[2]
    gw = w["gmlp_ln_g"].shape[-1]
    hw = w["hg_norm_g"].shape[-1]
    x2d, tgt = x[0], loss_target[0]
    mx, my, mc = lax.axis_index("x"), lax.axis_index("y"), lax.axis_index("c")
    chip = 2 * mx + my
    dev = 2 * chip + mc
    q_col = 2 * gw // LANES
    gate_col = (2 * gw + 4 * hw) // d
    place = jnp.stack([chip, mc]).astype(jnp.int32)
    _Order.last = None

    c_all = _all_gather8(c.reshape(-1, LANES), "gather_c").reshape(N_DEV, d)
    gather_in = _WeightGather("in", _WeightGather.zones([w["w_in"][0]], chip), ring=True)
    zones_mix = _WeightGather.zones([w[n][0] for n in ("w_branch_gmlp", "w_branch_hg", "w_out")], chip)
    zones_fi = _WeightGather.zones([w["w_ffn_in"][0]], chip)
    zones_fo = _WeightGather.zones([w["w_ffn_out"][0]], chip)
    _Order.also = (*zones_mix, *zones_fi, *zones_fo)
    n_ada = w["w_ada"].shape[-1]
    b_ada_q = lax.dynamic_slice(w["b_ada"], (0, chip * n_ada), (1, n_ada))
    mod_q = _ada_fwd(c_all, w["w_ada"][0], b_ada_q)
    mod_all = _all_gather8(mod_q, "gather_mod")
    mod = lax.dynamic_index_in_dim(mod_all, dev, axis=1, keepdims=False)[::2].reshape(1, 6 * d)
    sh1, sc1, gt1, sh2, sc2, gt2 = [mod[:, i * d:(i + 1) * d] for i in range(6)]

    gather_in.relay_and_pass_near()
    gather_mix = _WeightGather("mix", zones_mix, ring=True)
    gather_fi = _WeightGather("fi", zones_fi, ring=True)

    norm1_g, norm2_g, final_g = w["norm1_g"], w["norm2_g"], w["final_norm_g"].reshape(1, d)
    ln_g, ln_b = w["gmlp_ln_g"], w["gmlp_ln_b"]
    ws = w["gmlp_ws"][0]
    groups = ws.shape[0]
    bs_t = jnp.pad(w["gmlp_bs"][0].T, ((0, 0), (0, LANES - groups)))
    hg_lb, hg_ng, b_gate = w["hg_lb"], w["hg_norm_g"], w["b_gate"]

    h1 = _norm_mod_fwd(x2d, norm1_g, sc1, sh1, "norm1_fwd")
    order = jnp.stack([chip, 2 * (1 - mx) + my, 2 * mx + (1 - my), 2 * (1 - mx) + (1 - my)]).astype(jnp.int32)
    w_in, = gather_in.near_done()
    z = _matmul_quarters(h1, w_in, order, 0, 3, "mm_z_near")
    w_in, = gather_in.pass_far()
    z = _matmul_quarters(h1, w_in, order, 3, 1, "mm_z_far", prev=z)
    gather_mix.relay()
    ya = _gmlp_fwd(z, ln_g, ln_b, ws, bs_t)
    yb, o_raw, states = _hgrn_fwd(z, hg_lb, hg_ng, q_col)
    gather_mix.pass_on()
    gather_fi.relay()
    gather_fo = _WeightGather("fo", zones_fo)
    w_bg, w_bh, w_out = gather_mix.done()
    w_bg, w_bh = [wq.transpose(1, 0, 2).reshape(wq.shape[1], -1) for wq in (w_bg, w_bh)]
    w_out = w_out.reshape(-1, w_out.shape[-1])
    pa, pb, y = _branch_merge(ya, yb, w_bg, w_bh, z, b_gate, gate_col)
    yo = _matmul(y, w_out, mode="nn", name="mm_yo", out_dtype=F32)
    gather_fi.pass_on()
    x1, h2 = _norm_mod_fwd(x2d, norm2_g, sc2, sh2, "norm2_fwd", res=yo, gt=gt1)
    w_fi, = gather_fi.done()
    a_act, up_act, hf = _ffn_in(h2, w_fi)
    gather_fo.pass_on()
    w_fo, = gather_fo.done()
    w_fo = w_fo.reshape(-1, w_fo.shape[-1])
    ffn = _matmul(hf, w_fo, mode="nn", name="mm_ffn", out_dtype=F32)
    dx2, dffn, loss_row, d_final_g, d_gt2 = _final_loss(x1, ffn, gt2, final_g, tgt)

    g_fo = _matmul(hf, dffn, mode="tn", name="mm_g_fo", out_dtype=BF16, bm=_half_width(hf.shape[1] // 2))
    daup = tuple(_ffn_out_bwd(dffn, w_fo, a_act, up_act))
    g_fi = _matmul(h2, daup, mode="tn", name="mm_g_fi", out_dtype=BF16, out_slots=True, bn=_half_width(w_fi.shape[2]))
    red_ffn = _GradReduce("ffn", [g_fo.reshape(N_CHIPS, -1, g_fo.shape[-1]), g_fi])
    dh2 = _matmul(daup, w_fi, mode="nt", name="mm_dh2", out_dtype=F32, b_slots=True)
    red_ffn.step(place)
    dx1, dyo, d_sh2, d_sc2, d_norm2, d_gt1 = _norm_mod_bwd(dh2, x1, dx2, norm2_g, sc2, "norm2_bwd", branch=yo, gt=gt1)
    g_out = _matmul(y, dyo, mode="tn", name="mm_g_out", out_dtype=BF16)
    dpa, dpb, dz_ga, dz_gb, d_b_ga, d_b_gb = _out_bwd(dyo, w_out, pa, pb, z, b_gate, gate_col)
    g_bg, g_bh = _branch_wgrad(ya, yb, dpa, dpb)
    red_mix = _GradReduce("mix", [g_out.reshape(N_CHIPS, -1, g_out.shape[-1]), g_bg, g_bh])
    dya, dyb = _branch_bwd(dpa, dpb, w_bg, w_bh)
    red_mix.step(place)
    dz_uv, d_ws, d_bs_t, d_ln_g, d_ln_b = _gmlp_bwd(dya, z, ln_g, ln_b, ws, bs_t)
    dz_q, dz_f, dz_i, dz_g, d_hg_lb, d_hg_ng = _hgrn_bwd(dyb, z, o_raw, states, hg_lb, hg_ng, q_col)
    dz = jnp.concatenate([dz_uv, dz_q, dz_f, dz_i, dz_g, dz_ga, dz_gb], axis=1)
    grad, delta, new_m, new_v = {}, {}, {}, {}

    def update(n, parts, outer=False):
        outs = _adamw(w[n][0], m[n][0], v[n][0], parts, "adamw_" + n, outer=outer)
        grad[n], delta[n], new_m[n], new_v[n] = [o[None] for o in outs]

    g_in = _matmul(h1, dz, mode="tn", name="mm_g_in", out_dtype=BF16, out_slots=True, bn=_half_width(w_in.shape[2]))
    red_in = _GradReduce("in", [g_in])
    red_ffn.join(place)
    red_mix.join(place)
    red_in.step(place)
    g_fo, g_fi = red_ffn.done()
    n = "w_ffn_in"
    dh1, rode = _matmul(dz, w_in, mode="nt", name="mm_dh1", out_dtype=F32, b_slots=True,
                        rider=([w[n][0], m[n][0], v[n][0], g_fi], lambda *blk: (blk[3], *_adam_update(*blk)), 4))
    grad[n], delta[n], new_m[n], new_v[n] = [o[None] for o in rode]
    grad_x, d_sh1, d_sc1, d_norm1 = _norm_mod_bwd(dh1, x2d, dx1, norm1_g, sc1, "norm1_bwd")

    d_mod = jnp.concatenate([d_sh1, d_sc1, d_gt1, d_sh2, d_sc2, d_gt2], axis=1)
    small_part = {"b_ada": d_mod, "norm1_g": d_norm1, "b_gate": jnp.concatenate([d_b_ga, d_b_gb], axis=1), "gmlp_ln_g": d_ln_g, "gmlp_ln_b": d_ln_b,
                  "gmlp_ws": d_ws, "gmlp_bs": d_bs_t[:, :groups].T, "hg_lb": d_hg_lb, "hg_norm_g": d_hg_ng,
                  "norm2_g": d_norm2, "final_norm_g": d_final_g}
    loss_rows = jnp.broadcast_to(loss_row, (8, LANES))
    packed = _pack([small_part[n] for n in SMALL] + [loss_rows])
    zone = lax.dynamic_update_slice(lax.empty((N_DEV,) + packed.shape, F32), packed[None], (dev, 0, 0))
    small_sems, small_bufs = _xfer_start("gsmall_start", [zone], _plan_gather_out, 4, after_last=True)
    update("w_ffn_out", g_fo[None])
    g_out, g_bg, g_bh = red_mix.done()
    update("w_out", g_out[None])
    red_in.join(place)
    (small_sems,), small_bufs = _xfer_hand_over("gsmall_pass", small_sems, small_bufs, _plan_gather_out,
                                                [(_plan_gather_on, 3)])
    update("w_branch_gmlp", g_bg[None])
    update("w_branch_hg", g_bh[None])
    small_all, = _xfer_wait("gsmall_wait", small_sems, small_bufs, _plan_gather_on)
    d_mod_all = small_all[:, :6 * d // LANES].reshape(N_DEV, 6 * d)
    d_mod_q = lax.dynamic_slice(d_mod_all, (0, chip * n_ada), (N_DEV, n_ada))
    pad = [jnp.zeros_like(loss_rows)]
    outs = _adamw(_pack([w[n] for n in SMALL] + pad), _pack([m[n] for n in SMALL] + pad),
                  _pack([v[n] for n in SMALL] + pad), small_all, "adamw_small")
    update("w_ada", (c_all, d_mod_q), outer=True)
    update("w_in", red_in.done()[0][None])
    row = 0
    for n in SMALL:
        cnt = w[n].size // LANES
        for dst, o in zip((grad, delta, new_m, new_v), outs):
            dst[n] = o[row:row + cnt].reshape(w[n].shape)
        row += cnt

    loss = outs[0][row, 0]
    return (loss, grad_x[None], *[grad[n] for n in WEIGHTS], *[delta[n] for n in WEIGHTS],
            *[new_m[n] for n in WEIGHTS], *[new_v[n] for n in WEIGHTS])


def kernel(x, c, w_ada, b_ada, norm1_g, w_in, b_gate, gmlp_ln_g, gmlp_ln_b, gmlp_ws, gmlp_bs, hg_lb, hg_norm_g, w_branch_gmlp, w_branch_hg, w_out, norm2_g, w_ffn_in, w_ffn_out, final_norm_g, loss_target, m_w_ada, m_b_ada, m_norm1_g, m_w_in, m_b_gate, m_gmlp_ln_g, m_gmlp_ln_b, m_gmlp_ws, m_gmlp_bs, m_hg_lb, m_hg_norm_g, m_w_branch_gmlp, m_w_branch_hg, m_w_out, m_norm2_g, m_w_ffn_in, m_w_ffn_out, m_final_norm_g, v_w_ada, v_b_ada, v_norm1_g, v_w_in, v_b_gate, v_gmlp_ln_g, v_gmlp_ln_b, v_gmlp_ws, v_gmlp_bs, v_hg_lb, v_hg_norm_g, v_w_branch_gmlp, v_w_branch_hg, v_w_out, v_norm2_g, v_w_ffn_in, v_w_ffn_out, v_final_norm_g):
    w = dict(w_ada=w_ada, b_ada=b_ada, norm1_g=norm1_g, w_in=w_in, b_gate=b_gate, gmlp_ln_g=gmlp_ln_g,
             gmlp_ln_b=gmlp_ln_b, gmlp_ws=gmlp_ws, gmlp_bs=gmlp_bs, hg_lb=hg_lb, hg_norm_g=hg_norm_g,
             w_branch_gmlp=w_branch_gmlp, w_branch_hg=w_branch_hg, w_out=w_out, norm2_g=norm2_g,
             w_ffn_in=w_ffn_in, w_ffn_out=w_ffn_out, final_norm_g=final_norm_g)
    m = dict(w_ada=m_w_ada, b_ada=m_b_ada, norm1_g=m_norm1_g, w_in=m_w_in, b_gate=m_b_gate, gmlp_ln_g=m_gmlp_ln_g,
             gmlp_ln_b=m_gmlp_ln_b, gmlp_ws=m_gmlp_ws, gmlp_bs=m_gmlp_bs, hg_lb=m_hg_lb, hg_norm_g=m_hg_norm_g,
             w_branch_gmlp=m_w_branch_gmlp, w_branch_hg=m_w_branch_hg, w_out=m_w_out, norm2_g=m_norm2_g,
             w_ffn_in=m_w_ffn_in, w_ffn_out=m_w_ffn_out, final_norm_g=m_final_norm_g)
    v = dict(w_ada=v_w_ada, b_ada=v_b_ada, norm1_g=v_norm1_g, w_in=v_w_in, b_gate=v_b_gate, gmlp_ln_g=v_gmlp_ln_g,
             gmlp_ln_b=v_gmlp_ln_b, gmlp_ws=v_gmlp_ws, gmlp_bs=v_gmlp_bs, hg_lb=v_hg_lb, hg_norm_g=v_hg_norm_g,
             w_branch_gmlp=v_w_branch_gmlp, w_branch_hg=v_w_branch_hg, w_out=v_w_out, norm2_g=v_norm2_g,
             w_ffn_in=v_w_ffn_in, w_ffn_out=v_w_ffn_out, final_norm_g=v_final_norm_g)
    return _step(x, c, loss_target, w, m, v)
```
